```python
import jax, jax.numpy as jnp
from jax import lax
import numpy as np

D_MODEL = 1024
BATCH = 8
SEQ = 4096
DEPTH = 2

D_MIX = D_MODEL
GLA_HEADS = 4
GLA_WIDTH = D_MIX // 2
GLA_DV = GLA_WIDTH // GLA_HEADS
GLA_DK = GLA_DV // 2
GLA_LOWRANK = 16
GLA_TAU = 16.0
GLA_CHUNK = 64
DIL_HEADS = 4
DIL_WIDTH = D_MIX - GLA_WIDTH
DIL_HD = DIL_WIDTH // DIL_HEADS
DIL_PATTERNS = ((128, 1), (512, 4), (2048, 16))
ROPE_THETA = 10000.0
EPS = 1e-6
IN_SIZES = (GLA_HEADS * GLA_DK, GLA_HEADS * GLA_DK, GLA_WIDTH, GLA_WIDTH, GLA_LOWRANK,
            DIL_WIDTH, DIL_WIDTH, DIL_WIDTH, DIL_WIDTH)
IN_COLS = sum(IN_SIZES)

kernel_name = "hybrid_gla_dilated_parallel_heads"


def rmsnorm(x, g):
    x32 = x.astype(jnp.float32)
    r = x32 * lax.rsqrt(jnp.mean(x32 * x32, axis=-1, keepdims=True) + EPS)
    return (r * g.astype(jnp.float32)).astype(x.dtype)


def head_rmsnorm(o, g):
    r = o * lax.rsqrt(jnp.mean(o * o, axis=-1, keepdims=True) + EPS)
    B, S, H, dh = o.shape
    return r.reshape(B, S, H * dh) * g.astype(jnp.float32)


def rope(x):
    S, hd = x.shape[1], x.shape[3]
    inv_freq = ROPE_THETA ** (-jnp.arange(0, hd, 2, dtype=jnp.float32) / hd)
    ang = jnp.arange(S, dtype=jnp.float32)[:, None] * inv_freq[None, :]
    cos = jnp.cos(ang)[None, :, None, :]
    sin = jnp.sin(ang)[None, :, None, :]
    x32 = x.astype(jnp.float32)
    x1, x2 = x32[..., : hd // 2], x32[..., hd // 2:]
    return jnp.concatenate([x1 * cos - x2 * sin, x2 * cos + x1 * sin], axis=-1)


def gla_mixer(q, k, v, lr, w_gate_up, b_gate_up):
    B, S, H, DK = q.shape
    DV = v.shape[-1]
    C = GLA_CHUNK
    n = S // C
    z = jnp.einsum('bsr,rk->bsk', lr.astype(jnp.float32), w_gate_up.astype(jnp.float32)) + b_gate_up.astype(jnp.float32)
    log_a = jax.nn.log_sigmoid(z).reshape(B, S, H, DK) / GLA_TAU

    def chunks(t):
        return t.astype(jnp.float32).reshape(B, n, C, H, t.shape[-1]).transpose(0, 3, 1, 2, 4)

    qc, kc, vc, la = chunks(q), chunks(k), chunks(v), chunks(log_a)
    b = jnp.cumsum(la, axis=3)
    q_e = qc * jnp.exp(b)
    k_e = kc * jnp.exp(-b)
    causal = jnp.tril(jnp.ones((C, C), dtype=bool))
    A = jnp.where(causal, jnp.einsum('bhnik,bhnjk->bhnij', q_e, k_e), 0.0)
    o_intra = jnp.einsum('bhnij,bhnjv->bhniv', A, vc)
    b_last = b[:, :, :, -1, :]
    k_end = kc * jnp.exp(b_last[:, :, :, None, :] - b)
    chunk_state = jnp.einsum('bhnjk,bhnjv->bhnkv', k_end, vc)
    decay = jnp.exp(b_last)

    def step(state, inp):
        dec, cs = inp
        return dec[..., None] * state + cs, state

    init = jnp.zeros((B, H, DK, DV), jnp.float32)
    _, before = lax.scan(step, init, (jnp.moveaxis(decay, 2, 0), jnp.moveaxis(chunk_state, 2, 0)))
    before = jnp.moveaxis(before, 0, 2)
    o = o_intra + jnp.einsum('bhnik,bhnkv->bhniv', q_e, before)
    return o.transpose(0, 2, 3, 1, 4).reshape(B, S, H, DV)


def dilated_pattern(q, k, v, window, dilation):
    B, S, H, hd = q.shape
    lb = window // dilation
    span = lb * dilation
    s_pad = -(-S // span) * span
    pad = s_pad - S
    L = s_pad // dilation
    nb = L // lb

    def to_blocks(t):
        t = jnp.pad(t, ((0, 0), (0, pad), (0, 0), (0, 0)))
        t = t.reshape(B, L, dilation, H, hd).transpose(0, 3, 2, 1, 4)
        return t.reshape(B, H, dilation, nb, lb, hd)

    def with_prev(t):
        prev = jnp.concatenate([jnp.zeros_like(t[:, :, :, :1]), t[:, :, :, :-1]], axis=3)
        return jnp.concatenate([prev, t], axis=4)

    qb = to_blocks(q)
    kk = with_prev(to_blocks(k))
    vv = with_prev(to_blocks(v))
    s = jnp.einsum('bhrnqd,bhrnkd->bhrnqk', qb, kk) * (hd ** -0.5)
    qi = jnp.arange(lb)[:, None]
    ki = jnp.arange(2 * lb)[None, :]
    dist = qi + lb - ki
    blk = jnp.arange(nb)[:, None, None]
    valid = (dist >= 0) & (dist <= lb) & ((blk > 0) | (ki >= lb)[None])
    s = jnp.where(valid, s, -jnp.inf)
    m = jnp.max(s, axis=-1, keepdims=True)
    p = jnp.exp(s - m)
    den = jnp.sum(p, axis=-1)
    o = jnp.einsum('bhrnqk,bhrnkd->bhrnqd', p, vv) / den[..., None]
    lse = m[..., 0] + jnp.log(den)
    o = o.reshape(B, H, dilation, L, hd).transpose(0, 3, 2, 1, 4).reshape(B, s_pad, H, hd)[:, :S]
    lse = lse.reshape(B, H, dilation, L).transpose(0, 3, 2, 1).reshape(B, s_pad, H)[:, :S]
    return o, lse


def dilated_mixer(q, k, v):
    v = v.astype(jnp.float32)
    outs, lses = [], []
    for window, dilation in DIL_PATTERNS:
        o, lse = dilated_pattern(q, k, v, window, dilation)
        outs.append(o)
        lses.append(lse)
    w = jax.nn.softmax(jnp.stack(lses, axis=0), axis=0)
    return jnp.sum(w[..., None] * jnp.stack(outs, axis=0), axis=0)


def hybrid_layer(x, c, w_ada, b_ada, g_pre, w_in, w_gate_up, b_gate_up, g_gla, g_dil, w_out, g_post):
    B, S, D = x.shape
    mod = jnp.einsum('bd,de->be', jax.nn.silu(c), w_ada) + b_ada
    shift, scale, gate = jnp.split(mod, 3, axis=-1)
    h = rmsnorm(x, g_pre) * (1 + scale[:, None, :]) + shift[:, None, :]
    proj = jnp.einsum('bsd,de->bse', h, w_in)
    offs = [int(o) for o in np.cumsum(IN_SIZES)[:-1]]
    q_a, k_a, v_a, z_a, lr, q_b, k_b, v_b, z_b = jnp.split(proj, offs, axis=-1)
    q_a = q_a.reshape(B, S, GLA_HEADS, GLA_DK) * (GLA_DK ** -0.5)
    k_a = k_a.reshape(B, S, GLA_HEADS, GLA_DK)
    v_a = v_a.reshape(B, S, GLA_HEADS, GLA_DV)
    o_a = gla_mixer(q_a, k_a, v_a, lr, w_gate_up, b_gate_up)
    y_a = head_rmsnorm(o_a, g_gla) * jax.nn.silu(z_a.astype(jnp.float32))
    q_b = rope(q_b.reshape(B, S, DIL_HEADS, DIL_HD))
    k_b = rope(k_b.reshape(B, S, DIL_HEADS, DIL_HD))
    v_b = v_b.reshape(B, S, DIL_HEADS, DIL_HD)
    o_b = dilated_mixer(q_b, k_b, v_b)
    y_b = head_rmsnorm(o_b, g_dil) * jax.nn.silu(z_b.astype(jnp.float32))
    y = jnp.concatenate([y_a, y_b], axis=-1).astype(x.dtype)
    y = jnp.einsum('bse,ed->bsd', y, w_out)
    return x + gate[:, None, :] * rmsnorm(y, g_post)


def _fwd_setup_inputs(seed: int = 0) -> dict:
    key = jax.random.key(seed)
    ks = jax.random.split(key, 14)
    f = jnp.float32
    nrm = lambda k, shape, s: jax.random.normal(k, shape, f) * s
    return {
        "x": nrm(ks[0], (BATCH, SEQ, D_MODEL), 1.0),
        "c": nrm(ks[1], (BATCH, D_MODEL), 1.0),
        "w_ada": nrm(ks[2], (DEPTH, D_MODEL, 3 * D_MODEL), D_MODEL ** -0.5),
        "b_ada": nrm(ks[3], (DEPTH, 3 * D_MODEL), 0.02),
        "g_pre": 1.0 + nrm(ks[4], (DEPTH, D_MODEL), 0.02),
        "w_in": nrm(ks[5], (DEPTH, D_MODEL, IN_COLS), D_MODEL ** -0.5),
        "w_gate_up": nrm(ks[6], (DEPTH, GLA_LOWRANK, GLA_HEADS * GLA_DK), GLA_LOWRANK ** -0.5),
        "b_gate_up": nrm(ks[7], (DEPTH, GLA_HEADS * GLA_DK), 0.1),
        "g_gla": 1.0 + nrm(ks[8], (DEPTH, GLA_WIDTH), 0.02),
        "g_dil": 1.0 + nrm(ks[9], (DEPTH, DIL_WIDTH), 0.02),
        "w_out": nrm(ks[10], (DEPTH, D_MIX, D_MODEL), D_MIX ** -0.5),
        "g_post": 1.0 + nrm(ks[11], (DEPTH, D_MODEL), 0.02),
    }


def _fwd_reference(x, c, w_ada, b_ada, g_pre, w_in, w_gate_up, b_gate_up, g_gla, g_dil, w_out, g_post):
    for l in range(DEPTH):
        x = hybrid_layer(x, c, w_ada[l], b_ada[l], g_pre[l], w_in[l], w_gate_up[l], b_gate_up[l],
                         g_gla[l], g_dil[l], w_out[l], g_post[l])
    return x


import jax as _jax
import jax.numpy as _jnp

TWIN_FORMAT = 'train_step'
FWD_PARAMS = ['x', 'c', 'w_ada', 'b_ada', 'g_pre', 'w_in', 'w_gate_up', 'b_gate_up', 'g_gla', 'g_dil', 'w_out', 'g_post']
TWIN_WEIGHTS = ['w_ada', 'b_ada', 'g_pre', 'w_in', 'w_gate_up', 'b_gate_up', 'g_gla', 'g_dil', 'w_out', 'g_post']
TWIN_DIFF_INPUT = 'x'
TWIN_INPUTS = ['x', 'c', 'w_ada', 'b_ada', 'g_pre', 'w_in', 'w_gate_up', 'b_gate_up', 'g_gla', 'g_dil', 'w_out', 'g_post', 'loss_target', 'm_w_ada', 'm_b_ada', 'm_g_pre', 'm_w_in', 'm_w_gate_up', 'm_b_gate_up', 'm_g_gla', 'm_g_dil', 'm_w_out', 'm_g_post', 'v_w_ada', 'v_b_ada', 'v_g_pre', 'v_w_in', 'v_w_gate_up', 'v_b_gate_up', 'v_g_gla', 'v_g_dil', 'v_w_out', 'v_g_post']
TWIN_OUTPUTS = ['loss', 'grad_x', 'grad_w_ada', 'grad_b_ada', 'grad_g_pre', 'grad_w_in', 'grad_w_gate_up', 'grad_b_gate_up', 'grad_g_gla', 'grad_g_dil', 'grad_w_out', 'grad_g_post', 'delta_w_ada', 'delta_b_ada', 'delta_g_pre', 'delta_w_in', 'delta_w_gate_up', 'delta_b_gate_up', 'delta_g_gla', 'delta_g_dil', 'delta_w_out', 'delta_g_post', 'new_m_w_ada', 'new_m_b_ada', 'new_m_g_pre', 'new_m_w_in', 'new_m_w_gate_up', 'new_m_b_gate_up', 'new_m_g_gla', 'new_m_g_dil', 'new_m_w_out', 'new_m_g_post', 'new_v_w_ada', 'new_v_b_ada', 'new_v_g_pre', 'new_v_w_in', 'new_v_w_gate_up', 'new_v_b_gate_up', 'new_v_g_gla', 'new_v_g_dil', 'new_v_w_out', 'new_v_g_post']
TWIN_LEAF_KINDS = {'loss': 'loss', 'grad_x': 'grad_x', 'grad_w_ada': 'grad_w', 'grad_b_ada': 'grad_w', 'grad_g_pre': 'grad_w', 'grad_w_in': 'grad_w', 'grad_w_gate_up': 'grad_w', 'grad_b_gate_up': 'grad_w', 'grad_g_gla': 'grad_w', 'grad_g_dil': 'grad_w', 'grad_w_out': 'grad_w', 'grad_g_post': 'grad_w', 'delta_w_ada': 'delta_w', 'delta_b_ada': 'delta_w', 'delta_g_pre': 'delta_w', 'delta_w_in': 'delta_w', 'delta_w_gate_up': 'delta_w', 'delta_b_gate_up': 'delta_w', 'delta_g_gla': 'delta_w', 'delta_g_dil': 'delta_w', 'delta_w_out': 'delta_w', 'delta_g_post': 'delta_w', 'new_m_w_ada': 'new_m', 'new_m_b_ada': 'new_m', 'new_m_g_pre': 'new_m', 'new_m_w_in': 'new_m', 'new_m_w_gate_up': 'new_m', 'new_m_b_gate_up': 'new_m', 'new_m_g_gla': 'new_m', 'new_m_g_dil': 'new_m', 'new_m_w_out': 'new_m', 'new_m_g_post': 'new_m', 'new_v_w_ada': 'new_v', 'new_v_b_ada': 'new_v', 'new_v_g_pre': 'new_v', 'new_v_w_in': 'new_v', 'new_v_w_gate_up': 'new_v', 'new_v_b_gate_up': 'new_v', 'new_v_g_gla': 'new_v', 'new_v_g_dil': 'new_v', 'new_v_w_out': 'new_v', 'new_v_g_post': 'new_v'}


def _forward(args):
    return _fwd_reference(*[args[k] for k in FWD_PARAMS])


def _output_shape():
    out = _jax.eval_shape(lambda: _forward(_fwd_setup_inputs(0)))
    return out.shape, out.dtype

N_MICROBATCH = 1
ADAM_LR = 0.001
ADAM_B1 = 0.9
ADAM_B2 = 0.999
ADAM_EPS = 1e-08
ADAM_WD = 0.01
ADAM_STEP = 10
PER_EXAMPLE_BATCH_AXIS = {'x': 0, 'c': 0, 'loss_target': 0}
SHARED_INPUTS = []
_WEIGHT_DTYPES = {'w_ada': _jnp.float32, 'b_ada': _jnp.float32, 'g_pre': _jnp.float32, 'w_in': _jnp.float32, 'w_gate_up': _jnp.float32, 'b_gate_up': _jnp.float32, 'g_gla': _jnp.float32, 'g_dil': _jnp.float32, 'w_out': _jnp.float32, 'g_post': _jnp.float32}
MOMENT_SCALE = {'w_ada': 3.943309e+00, 'b_ada': 7.107238e+00, 'g_pre': 6.012066e-01, 'w_in': 1.126716e+00, 'w_gate_up': 4.272108e-01, 'b_gate_up': 6.016837e-01, 'g_gla': 1.402095e+00, 'g_dil': 2.467666e+00, 'w_out': 1.744465e+00, 'g_post': 1.542798e+01}


def _to_microbatches(a, axis):
    t = _jnp.moveaxis(a, axis, 0)
    t = t.reshape((N_MICROBATCH, t.shape[0] // N_MICROBATCH) + t.shape[1:])
    return _jnp.moveaxis(t, 1, axis + 1)


def setup_inputs(seed: int = 0) -> dict:
    inp = _fwd_setup_inputs(seed)
    key = _jax.random.fold_in(_jax.random.key(seed), 7919)
    shape, _ = _output_shape()
    out = dict(inp)
    out["loss_target"] = _jax.random.normal(_jax.random.fold_in(key, 0), shape, _jnp.float32)
    for i, name in enumerate(TWIN_WEIGHTS):
        w = inp[name].astype(_jnp.float32)
        if MOMENT_SCALE is None:
            s = _jnp.sqrt(_jnp.mean(_jnp.square(w)) + 1e-30)
        else:
            s = MOMENT_SCALE[name]
        km, kv = _jax.random.split(_jax.random.fold_in(key, i + 1))
        out[name] = w
        out["m_" + name] = s * _jax.random.normal(km, w.shape, _jnp.float32)
        out["v_" + name] = (s * s) * _jax.random.uniform(kv, w.shape, _jnp.float32, 0.5, 1.5)
    if N_MICROBATCH > 1:
        for name, axis in PER_EXAMPLE_BATCH_AXIS.items():
            out[name] = _to_microbatches(out[name], axis)
    return {'x': out['x'], 'c': out['c'], 'w_ada': out['w_ada'], 'b_ada': out['b_ada'], 'g_pre': out['g_pre'], 'w_in': out['w_in'], 'w_gate_up': out['w_gate_up'], 'b_gate_up': out['b_gate_up'], 'g_gla': out['g_gla'], 'g_dil': out['g_dil'], 'w_out': out['w_out'], 'g_post': out['g_post'], 'loss_target': out['loss_target'], 'm_w_ada': out['m_w_ada'], 'm_b_ada': out['m_b_ada'], 'm_g_pre': out['m_g_pre'], 'm_w_in': out['m_w_in'], 'm_w_gate_up': out['m_w_gate_up'], 'm_b_gate_up': out['m_b_gate_up'], 'm_g_gla': out['m_g_gla'], 'm_g_dil': out['m_g_dil'], 'm_w_out': out['m_w_out'], 'm_g_post': out['m_g_post'], 'v_w_ada': out['v_w_ada'], 'v_b_ada': out['v_b_ada'], 'v_g_pre': out['v_g_pre'], 'v_w_in': out['v_w_in'], 'v_w_gate_up': out['v_w_gate_up'], 'v_b_gate_up': out['v_b_gate_up'], 'v_g_gla': out['v_g_gla'], 'v_g_dil': out['v_g_dil'], 'v_w_out': out['v_w_out'], 'v_g_post': out['v_g_post']}


def _loss(weights, diff, rest, loss_target):
    with _jax.named_scope("forward"):
        args = {**rest, TWIN_DIFF_INPUT: diff, **{k: w.astype(_WEIGHT_DTYPES[k]) for k, w in weights.items()}}
        y = _forward(args)
    with _jax.named_scope("loss_head"):
        err = _jnp.square(y.astype(_jnp.float32) - loss_target)
        return 0.5 * _jnp.sum(_jnp.mean(err, axis=-1)) if err.ndim else 0.5 * err


def _adamw(w, g, m, v):
    m = ADAM_B1 * m + (1.0 - ADAM_B1) * g
    v = ADAM_B2 * v + (1.0 - ADAM_B2) * _jnp.square(g)
    m_hat = m / (1.0 - ADAM_B1 ** ADAM_STEP)
    v_hat = v / (1.0 - ADAM_B2 ** ADAM_STEP)
    delta = -ADAM_LR * (m_hat / (_jnp.sqrt(v_hat) + ADAM_EPS) + ADAM_WD * w)
    return delta, m, v


def reference(x, c, w_ada, b_ada, g_pre, w_in, w_gate_up, b_gate_up, g_gla, g_dil, w_out, g_post, loss_target, m_w_ada, m_b_ada, m_g_pre, m_w_in, m_w_gate_up, m_b_gate_up, m_g_gla, m_g_dil, m_w_out, m_g_post, v_w_ada, v_b_ada, v_g_pre, v_w_in, v_w_gate_up, v_b_gate_up, v_g_gla, v_g_dil, v_w_out, v_g_post):
    given = dict(x=x, c=c, w_ada=w_ada, b_ada=b_ada, g_pre=g_pre, w_in=w_in, w_gate_up=w_gate_up, b_gate_up=b_gate_up, g_gla=g_gla, g_dil=g_dil, w_out=w_out, g_post=g_post, loss_target=loss_target, m_w_ada=m_w_ada, m_b_ada=m_b_ada, m_g_pre=m_g_pre, m_w_in=m_w_in, m_w_gate_up=m_w_gate_up, m_b_gate_up=m_b_gate_up, m_g_gla=m_g_gla, m_g_dil=m_g_dil, m_w_out=m_w_out, m_g_post=m_g_post, v_w_ada=v_w_ada, v_b_ada=v_b_ada, v_g_pre=v_g_pre, v_w_in=v_w_in, v_w_gate_up=v_w_gate_up, v_b_gate_up=v_b_gate_up, v_g_gla=v_g_gla, v_g_dil=v_g_dil, v_w_out=v_w_out, v_g_post=v_g_post)
    weights = {n: given[n] for n in TWIN_WEIGHTS}
    shared = {n: given[n] for n in SHARED_INPUTS}
    per_example = {n: given[n] for n in ['x', 'c']}
    grad_fn = _jax.value_and_grad(_loss, argnums=(0, 1))

    def one_microbatch(ex, loss_target):
        ex = dict(ex)
        diff = ex.pop(TWIN_DIFF_INPUT)
        return grad_fn(weights, diff, {**shared, **ex}, loss_target)

    if N_MICROBATCH == 1:
        loss, (grad_w, grad_x) = one_microbatch(per_example, given["loss_target"])
    else:
        def body(carry, xs):
            loss_sum, grad_sum = carry
            l_k, (gw_k, gx_k) = one_microbatch(xs[0], xs[1])
            with _jax.named_scope("update"):
                return (loss_sum + l_k, _jax.tree.map(_jnp.add, grad_sum, gw_k)), gx_k

        init = (_jnp.zeros((), _jnp.float32), _jax.tree.map(_jnp.zeros_like, weights))
        (loss, grad_w), grad_x = _jax.lax.scan(body, init, (per_example, given["loss_target"]))
    with _jax.named_scope("update"):
        delta_w, new_m, new_v = {}, {}, {}
        for n in TWIN_WEIGHTS:
            delta_w[n], new_m[n], new_v[n] = _adamw(weights[n], grad_w[n], given["m_" + n], given["v_" + n])
    return (loss, grad_x, *[grad_w[n] for n in TWIN_WEIGHTS], *[delta_w[n] for n in TWIN_WEIGHTS],
            *[new_m[n] for n in TWIN_WEIGHTS], *[new_v[n] for n in TWIN_WEIGHTS])
```

```python
import functools
import math

import jax
import jax.numpy as jnp
from jax import lax
from jax.experimental import pallas as pl
from jax.experimental.pallas import tpu as pltpu

F32 = jnp.float32
BF16 = jnp.bfloat16

N_DEV = 8
D_MODEL = 1024
DEPTH = 2
GLA_HEADS = 4
GLA_DK = 64
GLA_DV = 128
GLA_CHUNK = 64
GLA_TAU = 16.0
GLA_LOWRANK = 16
DIL_HEADS = 4
DIL_HD = 128
DIL_BLOCK = 128
DIL_DILATIONS = (1, 4, 16)
ROPE_THETA = 10000.0
EPS = 1e-6
IN_COLS = 3600
W_IN_SHARD = IN_COLS // N_DEV
ADA_SHARD = 3 * D_MODEL // N_DEV
OUT_SHARD = D_MODEL // N_DEV
GU_COLS = GLA_HEADS * GLA_DK
GU_SHARD = GU_COLS // N_DEV

ADAM_LR = 0.001
ADAM_B1 = 0.9
ADAM_B2 = 0.999
ADAM_EPS = 1e-08
ADAM_WD = 0.01
ADAM_STEP = 10

NP = 3712
COL_Z, COL_QA, COL_KA, COL_VA, COL_QB, COL_KB, COL_VB, COL_LR = 0, 1024, 1280, 1536, 2048, 2560, 3072, 3584
LANE = 128
MASK_VALUE = -1e30

MESH = pl.DeviceIdType.MESH
ANY = pl.BlockSpec(memory_space=pl.ANY)


def _params(sem=None, vmem_mb=None):
    kw = {}
    if sem is not None:
        kw["dimension_semantics"] = sem
    if vmem_mb is not None:
        kw["vmem_limit_bytes"] = vmem_mb * 1024 * 1024
    return pltpu.CompilerParams(**kw)


def _dot(a, b):
    return jnp.dot(a, b, preferred_element_type=F32)


def _dot_nt(a, b):
    return lax.dot_general(a, b, (((1,), (1,)), ((), ())), preferred_element_type=F32)


def _dot_tn(a, b):
    return lax.dot_general(a, b, (((0,), (0,)), ((), ())), preferred_element_type=F32)


def _sigmoid(z):
    return 1.0 / (1.0 + jnp.exp(-z))


def _log_sigmoid(z):
    return jnp.minimum(z, 0.0) - jnp.log(1.0 + jnp.exp(-jnp.abs(z)))


def _my_position():
    return lax.axis_index("x"), lax.axis_index("y"), lax.axis_index("c")


def _linear(px, py, pc):
    return 4 * px + 2 * py + pc


def _all_gather(xs, name):
    m, n = xs.shape

    def body(x_ref, out_ref, send_sems, recv_sems, local_sem):
        x, y, c = _my_position()
        me, sibling = (x, y, c), (x, y, 1 - c)
        chips = [(1 - x, y), (x, 1 - y), (1 - x, 1 - y)]

        def rows(px, py, pc):
            return out_ref.at[pl.ds(_linear(px, py, pc) * m, m), :]

        def copy(k, block, to, src=None):
            return pltpu.make_async_remote_copy(
                src_ref=rows(*block) if src is None else src, dst_ref=rows(*block),
                send_sem=send_sems.at[k], recv_sem=recv_sems.at[k], device_id=to, device_id_type=MESH)

        mine = pltpu.make_async_copy(x_ref, rows(*me), local_sem)
        mine.start()
        first = [copy(0, me, sibling, src=x_ref)]
        first += [copy(1 + j, me, (*chip, c), src=x_ref) for j, chip in enumerate(chips)]
        for cp in first:
            cp.start()
        passed = [copy(4 + j, (*chip, c), sibling) for j, chip in enumerate(chips)]
        for j, chip in enumerate(chips):
            copy(1 + j, (*chip, c), me).wait_recv()
            passed[j].start()
        copy(0, sibling, me).wait_recv()
        for j, chip in enumerate(chips):
            copy(4 + j, (*chip, 1 - c), me).wait_recv()
        for cp in first + passed:
            cp.wait_send()
        mine.wait()

    return pl.pallas_call(
        body, name=name, out_shape=jax.ShapeDtypeStruct((N_DEV * m, n), xs.dtype),
        in_specs=[ANY], out_specs=ANY,
        scratch_shapes=[pltpu.SemaphoreType.DMA((7,)), pltpu.SemaphoreType.DMA((7,)), pltpu.SemaphoreType.DMA(())],
    )(xs)


def _all_to_all(xs, name):
    m8, n = xs.shape
    m = m8 // N_DEV

    def body(x_ref, out_ref, send_sems, recv_sems, local_sem):
        x, y, c = _my_position()
        me = _linear(x, y, c)

        def rows(ref, idx):
            return ref.at[pl.ds(idx * m, m), :]

        local = pltpu.make_async_copy(rows(x_ref, me), rows(out_ref, me), local_sem)
        local.start()
        peers = []
        for j in range(1, N_DEV):
            px = 1 - x if j & 4 else x
            py = 1 - y if j & 2 else y
            pc = 1 - c if j & 1 else c
            peers.append((px, py, pc))
        sends = []
        for j, peer in enumerate(peers):
            cp = pltpu.make_async_remote_copy(
                src_ref=rows(x_ref, _linear(*peer)), dst_ref=rows(out_ref, me),
                send_sem=send_sems.at[j], recv_sem=recv_sems.at[j], device_id=peer, device_id_type=MESH)
            cp.start()
            sends.append(cp)
        for j, peer in enumerate(peers):
            pltpu.make_async_remote_copy(
                src_ref=rows(x_ref, _linear(*peer)), dst_ref=rows(out_ref, _linear(*peer)),
                send_sem=send_sems.at[j], recv_sem=recv_sems.at[j], device_id=peer, device_id_type=MESH).wait_recv()
        for cp in sends:
            cp.wait_send()
        local.wait()

    return pl.pallas_call(
        body, name=name, out_shape=jax.ShapeDtypeStruct((m8, n), xs.dtype),
        in_specs=[ANY], out_specs=ANY,
        scratch_shapes=[pltpu.SemaphoreType.DMA((7,)), pltpu.SemaphoreType.DMA((7,)), pltpu.SemaphoreType.DMA(())],
    )(xs)


def _mod_fwd(c_all, w_ada):
    def body(c_ref, w_ref, o_ref):
        cv = c_ref[...]
        sc = cv * _sigmoid(cv)
        o_ref[0] = jnp.dot(sc, w_ref[0], precision=lax.Precision.HIGHEST, preferred_element_type=F32)

    return pl.pallas_call(
        body, name="mod_fwd", grid=(DEPTH,),
        out_shape=jax.ShapeDtypeStruct((DEPTH, N_DEV, ADA_SHARD), F32),
        in_specs=[pl.BlockSpec((N_DEV, D_MODEL), lambda l: (0, 0)),
                  pl.BlockSpec((1, D_MODEL, ADA_SHARD), lambda l: (l, 0, 0))],
        out_specs=pl.BlockSpec((1, N_DEV, ADA_SHARD), lambda l: (l, 0, 0)),
        compiler_params=_params(("arbitrary",)),
    )(c_all, w_ada)


def _w_ada_grad(c_all, dmod_cols):
    def body(c_ref, d_ref, o_ref):
        cv = c_ref[...]
        sc = cv * _sigmoid(cv)
        o_ref[0] = lax.dot_general(sc, d_ref[0], (((0,), (0,)), ((), ())), precision=lax.Precision.HIGHEST,
                                   preferred_element_type=F32)

    return pl.pallas_call(
        body, name="w_ada_grad", grid=(DEPTH,),
        out_shape=jax.ShapeDtypeStruct((DEPTH, D_MODEL, ADA_SHARD), F32),
        in_specs=[pl.BlockSpec((N_DEV, D_MODEL), lambda l: (0, 0)),
                  pl.BlockSpec((1, N_DEV, ADA_SHARD), lambda l: (l, 0, 0))],
        out_specs=pl.BlockSpec((1, D_MODEL, ADA_SHARD), lambda l: (l, 0, 0)),
        compiler_params=_params(("arbitrary",)),
    )(c_all, dmod_cols)


def _prenorm_proj(x, g_pre, scale, shift, w_new, ts=256):
    s_len = x.shape[0]

    def body(x_ref, g_ref, sc_ref, sh_ref, w_ref, proj_ref, h_ref):
        xv = x_ref[...]
        rstd = lax.rsqrt(jnp.mean(xv * xv, axis=-1, keepdims=True) + EPS)
        h = (xv * rstd * g_ref[...]) * (1.0 + sc_ref[...]) + sh_ref[...]
        hb = h.astype(BF16)
        h_ref[...] = hb
        for j in range(0, NP, 512):
            w = min(512, NP - j)
            proj_ref[:, j:j + w] = _dot(hb, w_ref[:, j:j + w]).astype(BF16)

    vec = pl.BlockSpec((1, D_MODEL), lambda i: (0, 0))
    return pl.pallas_call(
        body, name="prenorm_proj", grid=(s_len // ts,),
        out_shape=(jax.ShapeDtypeStruct((s_len, NP), BF16), jax.ShapeDtypeStruct((s_len, D_MODEL), BF16)),
        in_specs=[pl.BlockSpec((ts, D_MODEL), lambda i: (i, 0)), vec, vec, vec,
                  pl.BlockSpec((D_MODEL, NP), lambda i: (0, 0))],
        out_specs=(pl.BlockSpec((ts, NP), lambda i: (i, 0)), pl.BlockSpec((ts, D_MODEL), lambda i: (i, 0))),
        compiler_params=_params(("arbitrary",), 48),
    )(x, g_pre, scale, shift, w_new)


def _gla_chunk_common(q_ref, k_ref, lr_ref, wgu_ref, bgu_ref, rows):
    c = GLA_CHUNK
    q = q_ref[rows, :].astype(F32) * (GLA_DK ** -0.5)
    k = k_ref[rows, :].astype(F32)
    z = _dot(lr_ref[rows, :], wgu_ref[...]) + bgu_ref[...]
    la = _log_sigmoid(z) * (1.0 / GLA_TAU)
    ri = lax.broadcasted_iota(jnp.int32, (c, c), 0)
    ci = lax.broadcasted_iota(jnp.int32, (c, c), 1)
    tril = (ri >= ci).astype(F32)
    b = jnp.dot(tril, la, precision=lax.Precision.HIGHEST, preferred_element_type=F32)
    bl = b[c - 1:c, :]
    qe = q * jnp.exp(b)
    ke = k * jnp.exp(-b)
    kend = k * jnp.exp(bl - b)
    dec = jnp.exp(bl)
    return q, k, z, b, bl, qe, ke, kend, dec, ri, ci


def _head_lane_mask(hh):
    return (lax.broadcasted_iota(jnp.int32, (1, LANE), 1) // GLA_DK) == hh


def _state_block_mask():
    r = lax.broadcasted_iota(jnp.int32, (2 * GLA_DV, LANE), 0) // GLA_DV
    cc = lax.broadcasted_iota(jnp.int32, (2 * GLA_DV, LANE), 1) // GLA_DK
    return r == cc


def _gla_fwd(proj, wgu, bgu):
    s_len = proj.shape[0]
    nc = s_len // GLA_CHUNK

    def body(q_ref, k_ref, v_ref, lr_ref, wgu_ref, bgu_ref, o_ref, st_ref, state):
        state[...] = jnp.zeros_like(state)
        bd = _state_block_mask()

        def chunk(n, carry):
            rows = pl.ds(pl.multiple_of(n * GLA_CHUNK, GLA_CHUNK), GLA_CHUNK)
            q, k, z, b, bl, qe, ke, kend, dec, ri, ci = _gla_chunk_common(q_ref, k_ref, lr_ref, wgu_ref, bgu_ref, rows)
            v = v_ref[rows, :]
            st = state[...]
            stb = st.astype(BF16)
            st_ref[0, n] = stb
            keb = ke.astype(BF16)
            o = _dot_nt(qe.astype(BF16), stb)
            parts = []
            for hh in range(2):
                qeh = jnp.where(_head_lane_mask(hh), qe, 0.0).astype(BF16)
                a = jnp.where(ri >= ci, _dot_nt(qeh, keb), 0.0)
                parts.append(_dot(a.astype(BF16), v[:, hh * GLA_DV:(hh + 1) * GLA_DV]))
            o_ref[rows, :] = o + jnp.concatenate(parts, axis=1)
            cs_t = jnp.where(bd, _dot_tn(v, kend.astype(BF16)), 0.0)
            state[...] = dec * st + cs_t
            return carry

        lax.fori_loop(0, nc, chunk, 0)

    return pl.pallas_call(
        body, name="gla_fwd", grid=(2,),
        out_shape=(jax.ShapeDtypeStruct((s_len, GLA_HEADS * GLA_DV), F32),
                   jax.ShapeDtypeStruct((2, nc, 2 * GLA_DV, LANE), BF16)),
        in_specs=[pl.BlockSpec((s_len, LANE), lambda g: (0, COL_QA // LANE + g)),
                  pl.BlockSpec((s_len, LANE), lambda g: (0, COL_KA // LANE + g)),
                  pl.BlockSpec((s_len, 2 * GLA_DV), lambda g: (0, COL_VA // (2 * GLA_DV) + g)),
                  pl.BlockSpec((s_len, LANE), lambda g: (0, COL_LR // LANE)),
                  pl.BlockSpec((LANE, LANE), lambda g: (0, g)),
                  pl.BlockSpec((1, LANE), lambda g: (0, g))],
        out_specs=(pl.BlockSpec((s_len, 2 * GLA_DV), lambda g: (0, g)),
                   pl.BlockSpec((1, nc, 2 * GLA_DV, LANE), lambda g: (g, 0, 0, 0))),
        scratch_shapes=[pltpu.VMEM((2 * GLA_DV, LANE), F32)],
        compiler_params=_params(("arbitrary",), 48),
    )(proj, proj, proj, proj, wgu, bgu)


def _rope_tables(s_len):
    inv_freq = ROPE_THETA ** (-jnp.arange(0, DIL_HD, 2, dtype=F32) / DIL_HD)
    ang = jnp.arange(s_len, dtype=F32)[:, None] * inv_freq[None, :]
    cos, sin = jnp.cos(ang), jnp.sin(ang)
    return jnp.concatenate([cos, cos], axis=1), jnp.concatenate([-sin, sin], axis=1)


def _rope(xv, cos, sin_signed):
    return xv * cos + pltpu.roll(xv, DIL_HD // 2, 1) * sin_signed


def _dil_block_indices(i, d):
    r = i % d
    n = i // d
    kb = jnp.maximum(n - 1, 0)
    qs = r + d * DIL_BLOCK * n
    ks = r + d * DIL_BLOCK * kb
    qi = lax.broadcasted_iota(jnp.int32, (DIL_BLOCK, 2 * DIL_BLOCK), 0) + DIL_BLOCK * n
    kj = lax.broadcasted_iota(jnp.int32, (DIL_BLOCK, 2 * DIL_BLOCK), 1) + DIL_BLOCK * kb
    dist = qi - kj
    valid = (dist >= 0) & (dist <= DIL_BLOCK)
    return qs, ks, valid


def _strided(start, size, d):
    return pl.ds(start, size) if d == 1 else pl.ds(start, size, stride=d)


def _dil_fwd(proj, cos, sin_signed):
    s_len = proj.shape[0]
    nblk = s_len // DIL_BLOCK
    prep_rows = 256
    scale = DIL_HD ** -0.5

    def body(q_ref, k_ref, v_ref, cos_ref, sin_ref, o_ref, lse_ref, qf, kf, vf, o0, o1, o2, l0, l1, l2):
        def prep(t, carry):
            rows = pl.ds(pl.multiple_of(t * prep_rows, prep_rows), prep_rows)
            cs, sn = cos_ref[rows, :], sin_ref[rows, :]
            qf[rows, :] = _rope(q_ref[rows, :].astype(F32), cs, sn)
            kf[rows, :] = _rope(k_ref[rows, :].astype(F32), cs, sn)
            vf[rows, :] = v_ref[rows, :].astype(F32)
            return carry

        lax.fori_loop(0, s_len // prep_rows, prep, 0)

        for d, o_p, l_p in zip(DIL_DILATIONS, (o0, o1, o2), (l0, l1, l2)):
            def blk(i, carry, d=d, o_p=o_p, l_p=l_p):
                qs, ks, valid = _dil_block_indices(i, d)
                qb = qf[_strided(qs, DIL_BLOCK, d), :].astype(BF16)
                kk = kf[_strided(ks, 2 * DIL_BLOCK, d), :].astype(BF16)
                vv = vf[_strided(ks, 2 * DIL_BLOCK, d), :].astype(BF16)
                s = jnp.where(valid, _dot_nt(qb, kk) * scale, MASK_VALUE)
                m = jnp.max(s, axis=-1, keepdims=True)
                p = jnp.exp(s - m)
                den = jnp.sum(p, axis=-1, keepdims=True)
                o_p[_strided(qs, DIL_BLOCK, d), :] = _dot(p.astype(BF16), vv) / den
                l_p[_strided(qs, DIL_BLOCK, d), :] = jnp.broadcast_to(m + jnp.log(den), (DIL_BLOCK, DIL_HD))
                return carry

            lax.fori_loop(0, nblk, blk, 0)

        def comb(t, carry):
            rows = pl.ds(pl.multiple_of(t * prep_rows, prep_rows), prep_rows)
            a0, a1, a2 = l0[rows, :], l1[rows, :], l2[rows, :]
            m = jnp.maximum(jnp.maximum(a0, a1), a2)
            e0, e1, e2 = jnp.exp(a0 - m), jnp.exp(a1 - m), jnp.exp(a2 - m)
            tot = e0 + e1 + e2
            o_ref[rows, :] = (e0 * o0[rows, :] + e1 * o1[rows, :] + e2 * o2[rows, :]) / tot
            lse_ref[rows, :] = m + jnp.log(tot)
            return carry

        lax.fori_loop(0, s_len // prep_rows, comb, 0)

    head = lambda base: pl.BlockSpec((s_len, DIL_HD), lambda h: (0, base // DIL_HD + h))
    table = pl.BlockSpec((s_len, DIL_HD), lambda h: (0, 0))
    out = pl.BlockSpec((s_len, DIL_HD), lambda h: (0, h))
    return pl.pallas_call(
        body, name="dil_fwd", grid=(DIL_HEADS,),
        out_shape=(jax.ShapeDtypeStruct((s_len, DIL_HEADS * DIL_HD), F32),
                   jax.ShapeDtypeStruct((s_len, DIL_HEADS * DIL_HD), F32)),
        in_specs=[head(COL_QB), head(COL_KB), head(COL_VB), table, table],
        out_specs=(out, out),
        scratch_shapes=[pltpu.VMEM((s_len, DIL_HD), F32) for _ in range(9)],
        compiler_params=_params(("arbitrary",), 56),
    )(proj, proj, proj, cos, sin_signed)


def _silu_and_grad(z):
    sg = _sigmoid(z)
    return z * sg, sg * (1.0 + z * (1.0 - sg))


def _post_fwd(o_a, o_b, proj, g_heads, w_out, x, gate, g_post, ts=256):
    s_len = x.shape[0]
    half = GLA_HEADS * GLA_DV

    def body(oa_ref, ob_ref, z_ref, gh_ref, w_ref, x_ref, gate_ref, gp_ref, xo_ref, y_ref, u_ref):
        for src, base in ((oa_ref, 0), (ob_ref, half)):
            for hh in range(4):
                lo = hh * LANE
                og = src[:, lo:lo + LANE]
                on = og * lax.rsqrt(jnp.mean(og * og, axis=-1, keepdims=True) + EPS)
                zg = z_ref[:, base + lo:base + lo + LANE].astype(F32)
                y_ref[:, base + lo:base + lo + LANE] = (on * gh_ref[:, base + lo:base + lo + LANE]
                                                        * (zg * _sigmoid(zg))).astype(BF16)
        u = _dot(y_ref[...], w_ref[...])
        u_ref[...] = u.astype(BF16)
        rstd = lax.rsqrt(jnp.mean(u * u, axis=-1, keepdims=True) + EPS)
        xo_ref[...] = x_ref[...] + gate_ref[...] * (u * rstd * gp_ref[...])

    vec = pl.BlockSpec((1, D_MODEL), lambda i: (0, 0))
    tile = pl.BlockSpec((ts, D_MODEL), lambda i: (i, 0))
    halft = pl.BlockSpec((ts, half), lambda i: (i, 0))
    return pl.pallas_call(
        body, name="post_fwd", grid=(s_len // ts,),
        out_shape=(jax.ShapeDtypeStruct((s_len, D_MODEL), F32), jax.ShapeDtypeStruct((s_len, D_MODEL), BF16),
                   jax.ShapeDtypeStruct((s_len, D_MODEL), BF16)),
        in_specs=[halft, halft, tile, vec, pl.BlockSpec((D_MODEL, D_MODEL), lambda i: (0, 0)), tile, vec, vec],
        out_specs=(tile, tile, tile),
        compiler_params=_params(("arbitrary",), 40),
    )(o_a, o_b, proj, g_heads, w_out, x, gate, g_post)


def _loss_grad(y, target, ts=512):
    s_len = y.shape[0]

    def body(y_ref, t_ref, dy_ref, loss_ref):
        @pl.when(pl.program_id(0) == 0)
        def _():
            loss_ref[...] = jnp.zeros_like(loss_ref)

        e = y_ref[...] - t_ref[...]
        dy_ref[...] = e * (1.0 / D_MODEL)
        loss_ref[...] += 0.5 * jnp.sum(jnp.mean(e * e, axis=-1, keepdims=True))

    tile = pl.BlockSpec((ts, D_MODEL), lambda i: (i, 0))
    return pl.pallas_call(
        body, name="loss_grad", grid=(s_len // ts,),
        out_shape=(jax.ShapeDtypeStruct((s_len, D_MODEL), F32), jax.ShapeDtypeStruct((8, LANE), F32)),
        in_specs=[tile, tile], out_specs=(tile, pl.BlockSpec((8, LANE), lambda i: (0, 0))),
        compiler_params=_params(("arbitrary",)),
    )(y, target)


def _post_bwd(dxo, u, gate, g_post, w_out, o_a, o_b, proj, g_heads, ts=256):
    s_len = dxo.shape[0]
    half = GLA_HEADS * GLA_DV

    def body(dx_ref, u_ref, gate_ref, gp_ref, w_ref, oa_ref, ob_ref, z_ref, gh_ref, du_ref, do_ref, dz_ref, sums_ref):
        @pl.when(pl.program_id(0) == 0)
        def _():
            sums_ref[...] = jnp.zeros_like(sums_ref)

        dx = dx_ref[...]
        u = u_ref[...].astype(F32)
        rstd = lax.rsqrt(jnp.mean(u * u, axis=-1, keepdims=True) + EPS)
        un = u * rstd
        sums_ref[0:1, :] += jnp.sum(dx * (un * gp_ref[...]), axis=0, keepdims=True)
        drn = dx * gate_ref[...]
        sums_ref[1:2, :] += jnp.sum(drn * un, axis=0, keepdims=True)
        dun = drn * gp_ref[...]
        du = rstd * (dun - un * jnp.mean(dun * un, axis=-1, keepdims=True))
        dub = du.astype(BF16)
        du_ref[...] = dub
        dy = _dot_nt(dub, w_ref[...])
        for src, base in ((oa_ref, 0), (ob_ref, half)):
            for hh in range(4):
                lo = base + hh * LANE
                og = src[:, hh * LANE:(hh + 1) * LANE]
                rs = lax.rsqrt(jnp.mean(og * og, axis=-1, keepdims=True) + EPS)
                on = og * rs
                zg = z_ref[:, lo:lo + LANE].astype(F32)
                sz, dsz = _silu_and_grad(zg)
                gg = gh_ref[:, lo:lo + LANE]
                dyg = dy[:, lo:lo + LANE]
                sums_ref[2:3, lo:lo + LANE] += jnp.sum(dyg * sz * on, axis=0, keepdims=True)
                dz_ref[:, lo:lo + LANE] = (dyg * on * gg * dsz).astype(BF16)
                don = dyg * gg * sz
                do_ref[:, lo:lo + LANE] = rs * (don - on * jnp.mean(don * on, axis=-1, keepdims=True))

    vec = pl.BlockSpec((1, D_MODEL), lambda i: (0, 0))
    tile = pl.BlockSpec((ts, D_MODEL), lambda i: (i, 0))
    halft = pl.BlockSpec((ts, half), lambda i: (i, 0))
    return pl.pallas_call(
        body, name="post_bwd", grid=(s_len // ts,),
        out_shape=(jax.ShapeDtypeStruct((s_len, D_MODEL), BF16), jax.ShapeDtypeStruct((s_len, D_MODEL), F32),
                   jax.ShapeDtypeStruct((s_len, D_MODEL), BF16), jax.ShapeDtypeStruct((8, D_MODEL), F32)),
        in_specs=[tile, tile, vec, vec, pl.BlockSpec((D_MODEL, D_MODEL), lambda i: (0, 0)), halft, halft, tile, vec],
        out_specs=(tile, tile, tile, pl.BlockSpec((8, D_MODEL), lambda i: (0, 0))),
        compiler_params=_params(("arbitrary",), 40),
    )(dxo, u, gate, g_post, w_out, o_a, o_b, proj, g_heads)


def _gla_bwd(proj, wgu, bgu, states, do):
    s_len = proj.shape[0]
    nc = s_len // GLA_CHUNK
    c = GLA_CHUNK

    def body(q_ref, k_ref, v_ref, lr_ref, wgu_ref, bgu_ref, st_ref, do_ref,
             dq_ref, dk_ref, dv_ref, dlr_ref, dwgu_ref, dbgu_ref, dstate, dw_acc, db_acc):
        dstate[...] = jnp.zeros_like(dstate)
        dw_acc[...] = jnp.zeros_like(dw_acc)
        db_acc[...] = jnp.zeros_like(db_acc)
        bd = _state_block_mask()

        def chunk(t, carry):
            n = nc - 1 - t
            rows = pl.ds(pl.multiple_of(n * c, c), c)
            q, k, z, b, bl, qe, ke, kend, dec, ri, ci = _gla_chunk_common(q_ref, k_ref, lr_ref, wgu_ref, bgu_ref, rows)
            v = v_ref[rows, :]
            dov = do_ref[rows, :]
            dob = dov.astype(BF16)
            stb = st_ref[0, n]
            dst = dstate[...]
            dstb = dst.astype(BF16)
            qeb, keb, kendb = qe.astype(BF16), ke.astype(BF16), kend.astype(BF16)

            dqe = _dot(dob, stb)
            dkend = _dot(v, dstb)
            dv = _dot_nt(kendb, dstb)
            ddec = jnp.sum(dst * stb.astype(F32), axis=0, keepdims=True)
            dke = jnp.zeros((c, LANE), F32)
            dv_parts = []
            for hh in range(2):
                hm = _head_lane_mask(hh)
                qeh = jnp.where(hm, qe, 0.0).astype(BF16)
                keh = jnp.where(hm, ke, 0.0).astype(BF16)
                vh = v[:, hh * GLA_DV:(hh + 1) * GLA_DV]
                doh = dob[:, hh * GLA_DV:(hh + 1) * GLA_DV]
                a_t = jnp.where(ci >= ri, _dot_nt(keh, qeb), 0.0)
                da = jnp.where(ri >= ci, _dot_nt(doh, vh), 0.0)
                da_t = jnp.where(ci >= ri, _dot_nt(vh, doh), 0.0)
                dv_parts.append(_dot(a_t.astype(BF16), doh))
                dqe += jnp.where(hm, _dot(da.astype(BF16), keb), 0.0)
                dke += jnp.where(hm, _dot(da_t.astype(BF16), qeh), 0.0)
            dv_ref[rows, :] = (dv + jnp.concatenate(dv_parts, axis=1)).astype(BF16)

            eb = jnp.exp(b)
            dq_ref[rows, :] = (dqe * eb * (GLA_DK ** -0.5)).astype(BF16)
            dk_ref[rows, :] = (dke * jnp.exp(-b) + dkend * jnp.exp(bl - b)).astype(BF16)
            db = dqe * qe - dke * ke - dkend * kend
            dbl = jnp.sum(dkend * kend, axis=0, keepdims=True) + ddec * dec
            db = db + jnp.where(lax.broadcasted_iota(jnp.int32, (c, LANE), 0) == c - 1, dbl, 0.0)
            triu = (ci >= ri).astype(F32)
            dla = jnp.dot(triu, db, precision=lax.Precision.HIGHEST, preferred_element_type=F32)
            dz = dla * (1.0 / GLA_TAU) * _sigmoid(-z)
            dzb = dz.astype(BF16)
            dlr_ref[0, rows, :] = _dot_nt(dzb, wgu_ref[...])
            dw_acc[...] += _dot_tn(lr_ref[rows, :], dzb)
            db_acc[0:1, :] += jnp.sum(dz, axis=0, keepdims=True)

            dstate[...] = dec * dst + jnp.where(bd, _dot_tn(dob, qeb), 0.0)
            return carry

        lax.fori_loop(0, nc, chunk, 0)
        dwgu_ref[...] = dw_acc[...]
        dbgu_ref[...] = db_acc[...]

    pair = pl.BlockSpec((s_len, LANE), lambda g: (0, g))
    return pl.pallas_call(
        body, name="gla_bwd", grid=(2,),
        out_shape=(jax.ShapeDtypeStruct((s_len, GU_COLS), BF16), jax.ShapeDtypeStruct((s_len, GU_COLS), BF16),
                   jax.ShapeDtypeStruct((s_len, GLA_HEADS * GLA_DV), BF16),
                   jax.ShapeDtypeStruct((2, s_len, LANE), F32),
                   jax.ShapeDtypeStruct((LANE, GU_COLS), F32), jax.ShapeDtypeStruct((8, GU_COLS), F32)),
        in_specs=[pl.BlockSpec((s_len, LANE), lambda g: (0, COL_QA // LANE + g)),
                  pl.BlockSpec((s_len, LANE), lambda g: (0, COL_KA // LANE + g)),
                  pl.BlockSpec((s_len, 2 * GLA_DV), lambda g: (0, COL_VA // (2 * GLA_DV) + g)),
                  pl.BlockSpec((s_len, LANE), lambda g: (0, COL_LR // LANE)),
                  pl.BlockSpec((LANE, LANE), lambda g: (0, g)),
                  pl.BlockSpec((1, LANE), lambda g: (0, g)),
                  pl.BlockSpec((1, nc, 2 * GLA_DV, LANE), lambda g: (g, 0, 0, 0)),
                  pl.BlockSpec((s_len, 2 * GLA_DV), lambda g: (0, g))],
        out_specs=(pair, pair, pl.BlockSpec((s_len, 2 * GLA_DV), lambda g: (0, g)),
                   pl.BlockSpec((1, s_len, LANE), lambda g: (g, 0, 0)),
                   pl.BlockSpec((LANE, LANE), lambda g: (0, g)), pl.BlockSpec((8, LANE), lambda g: (0, g))),
        scratch_shapes=[pltpu.VMEM((2 * GLA_DV, LANE), F32), pltpu.VMEM((LANE, LANE), F32), pltpu.VMEM((8, LANE), F32)],
        compiler_params=_params(("arbitrary",), 56),
    )(proj, proj, proj, proj, wgu, bgu, states, do)


def _dil_bwd(proj, cos, sin_signed, do, o_b, lse):
    s_len = proj.shape[0]
    nblk = s_len // DIL_BLOCK
    prep_rows = 256
    scale = DIL_HD ** -0.5

    def body(q_ref, k_ref, v_ref, cos_ref, sin_ref, do_ref, o_ref, lse_ref, dq_ref, dk_ref, dv_ref,
             qf, kf, vf, dl, dqa, dka, dva):
        def prep(t, carry):
            rows = pl.ds(pl.multiple_of(t * prep_rows, prep_rows), prep_rows)
            cs, sn = cos_ref[rows, :], sin_ref[rows, :]
            qf[rows, :] = _rope(q_ref[rows, :].astype(F32), cs, sn)
            kf[rows, :] = _rope(k_ref[rows, :].astype(F32), cs, sn)
            vf[rows, :] = v_ref[rows, :].astype(F32)
            dl[rows, :] = jnp.broadcast_to(jnp.sum(do_ref[rows, :] * o_ref[rows, :], axis=-1, keepdims=True),
                                           (prep_rows, DIL_HD))
            zero = jnp.zeros((prep_rows, DIL_HD), F32)
            dqa[rows, :] = zero
            dka[rows, :] = zero
            dva[rows, :] = zero
            return carry

        lax.fori_loop(0, s_len // prep_rows, prep, 0)

        for d in DIL_DILATIONS:
            def blk(i, carry, d=d):
                qs, ks, valid = _dil_block_indices(i, d)
                qrows = _strided(qs, DIL_BLOCK, d)
                krows = _strided(ks, 2 * DIL_BLOCK, d)
                qb = qf[qrows, :].astype(BF16)
                kk = kf[krows, :].astype(BF16)
                vv = vf[krows, :].astype(BF16)
                dob = do_ref[qrows, :].astype(BF16)
                lse_b = lse_ref[qrows, :][:, 0:1]
                dl_b = dl[qrows, :][:, 0:1]
                s = _dot_nt(qb, kk) * scale
                p = jnp.where(valid, jnp.exp(s - lse_b), 0.0)
                dp = _dot_nt(dob, vv)
                ds = (p * (dp - dl_b) * scale).astype(BF16)
                dqa[qrows, :] += _dot(ds, kk)
                dka[krows, :] += _dot_tn(ds, qb)
                dva[krows, :] += _dot_tn(p.astype(BF16), dob)
                return carry

            lax.fori_loop(0, nblk, blk, 0)

        def fin(t, carry):
            rows = pl.ds(pl.multiple_of(t * prep_rows, prep_rows), prep_rows)
            cs, sn = cos_ref[rows, :], sin_ref[rows, :]
            gq, gk = dqa[rows, :], dka[rows, :]
            dq_ref[rows, :] = (gq * cs - pltpu.roll(gq, DIL_HD // 2, 1) * sn).astype(BF16)
            dk_ref[rows, :] = (gk * cs - pltpu.roll(gk, DIL_HD // 2, 1) * sn).astype(BF16)
            dv_ref[rows, :] = dva[rows, :].astype(BF16)
            return carry

        lax.fori_loop(0, s_len // prep_rows, fin, 0)

    head = lambda base: pl.BlockSpec((s_len, DIL_HD), lambda h: (0, base // DIL_HD + h))
    table = pl.BlockSpec((s_len, DIL_HD), lambda h: (0, 0))
    out = pl.BlockSpec((s_len, DIL_HD), lambda h: (0, h))
    shp = jax.ShapeDtypeStruct((s_len, DIL_HEADS * DIL_HD), BF16)
    return pl.pallas_call(
        body, name="dil_bwd", grid=(DIL_HEADS,),
        out_shape=(shp, shp, shp),
        in_specs=[head(COL_QB), head(COL_KB), head(COL_VB), table, table,
                  pl.BlockSpec((s_len, DIL_HD), lambda h: (0, DIL_HEADS + h)), out, out],
        out_specs=(out, out, out),
        scratch_shapes=[pltpu.VMEM((s_len, DIL_HD), F32) for _ in range(7)],
        compiler_params=_params(("arbitrary",), 56),
    )(proj, proj, proj, cos, sin_signed, do, o_b, lse)


_PIECES = ((COL_Z, 1024), (COL_QA, 256), (COL_KA, 256), (COL_VA, 512), (COL_QB, 512), (COL_KB, 512), (COL_VB, 512),
           (COL_LR, 128))


def _in_bwd(pieces, w_new, x, dxo, g_pre, scale, ts=256):
    s_len = x.shape[0]

    def body(*refs):
        p_refs = refs[:len(_PIECES)]
        w_ref, x_ref, dxo_ref, g_ref, sc_ref, dx_ref, sums_ref = refs[len(_PIECES):]

        @pl.when(pl.program_id(0) == 0)
        def _():
            sums_ref[...] = jnp.zeros_like(sums_ref)

        dh = jnp.zeros((ts, D_MODEL), F32)
        for p_ref, (col, width) in zip(p_refs, _PIECES):
            dh += _dot_nt(p_ref[...], w_ref[:, col:col + width])
        xv = x_ref[...]
        rstd = lax.rsqrt(jnp.mean(xv * xv, axis=-1, keepdims=True) + EPS)
        xn = xv * rstd
        sums_ref[0:1, :] += jnp.sum(dh, axis=0, keepdims=True)
        sums_ref[1:2, :] += jnp.sum(dh * (xn * g_ref[...]), axis=0, keepdims=True)
        dr = dh * (1.0 + sc_ref[...])
        sums_ref[2:3, :] += jnp.sum(dr * xn, axis=0, keepdims=True)
        dxn = dr * g_ref[...]
        dx_ref[...] = dxo_ref[...] + rstd * (dxn - xn * jnp.mean(dxn * xn, axis=-1, keepdims=True))

    vec = pl.BlockSpec((1, D_MODEL), lambda i: (0, 0))
    tile = pl.BlockSpec((ts, D_MODEL), lambda i: (i, 0))
    return pl.pallas_call(
        body, name="in_bwd", grid=(s_len // ts,),
        out_shape=(jax.ShapeDtypeStruct((s_len, D_MODEL), F32), jax.ShapeDtypeStruct((8, D_MODEL), F32)),
        in_specs=[pl.BlockSpec((ts, width), lambda i: (i, 0)) for _, width in _PIECES]
        + [pl.BlockSpec((D_MODEL, NP), lambda i: (0, 0)), tile, tile, vec, vec],
        out_specs=(tile, pl.BlockSpec((8, D_MODEL), lambda i: (0, 0))),
        compiler_params=_params(("arbitrary",), 48),
    )(*pieces, w_new, x, dxo, g_pre, scale)


def _matmul_tn(a, b, name, bn, ts=512):
    s_len, m = a.shape
    n = b.shape[1]

    def body(a_ref, b_ref, o_ref):
        @pl.when(pl.program_id(1) == 0)
        def _():
            o_ref[...] = jnp.zeros_like(o_ref)

        o_ref[...] += _dot_tn(a_ref[...], b_ref[...])

    return pl.pallas_call(
        body, name=name, grid=(n // bn, s_len // ts),
        out_shape=jax.ShapeDtypeStruct((m, n), F32),
        in_specs=[pl.BlockSpec((ts, m), lambda j, i: (i, 0)), pl.BlockSpec((ts, bn), lambda j, i: (i, j))],
        out_specs=pl.BlockSpec((m, bn), lambda j, i: (0, j)),
        compiler_params=_params(("arbitrary", "arbitrary"), 40),
    )(a, b)


def _adam_math(w, g, m, v):
    m = ADAM_B1 * m + (1.0 - ADAM_B1) * g
    v = ADAM_B2 * v + (1.0 - ADAM_B2) * (g * g)
    m_hat = m / (1.0 - ADAM_B1 ** ADAM_STEP)
    v_hat = v / (1.0 - ADAM_B2 ** ADAM_STEP)
    delta = -ADAM_LR * (m_hat / (jnp.sqrt(v_hat) + ADAM_EPS) + ADAM_WD * w)
    return delta, m, v


def _adamw(w, parts, m, v, name, tr):
    r, cdim = w.shape
    n_parts = parts.shape[0]

    def body(w_ref, p_ref, m_ref, v_ref, g_ref, d_ref, nm_ref, nv_ref):
        g = p_ref[0].astype(F32)
        for k in range(1, n_parts):
            g = g + p_ref[k].astype(F32)
        g_ref[...] = g
        d_ref[...], nm_ref[...], nv_ref[...] = _adam_math(w_ref[...], g, m_ref[...], v_ref[...])

    tile = pl.BlockSpec((tr, cdim), lambda i: (i, 0))
    shp = jax.ShapeDtypeStruct((r, cdim), F32)
    return pl.pallas_call(
        body, name=name, grid=(r // tr,), out_shape=(shp, shp, shp, shp),
        in_specs=[tile, pl.BlockSpec((n_parts, tr, cdim), lambda i: (0, i, 0)), tile, tile],
        out_specs=(tile, tile, tile, tile),
        compiler_params=_params(("arbitrary",), 40),
    )(w, parts, m, v)


def _to_kernel_columns(w):
    pad = jnp.zeros((w.shape[0], LANE - GLA_LOWRANK), w.dtype)
    return jnp.concatenate([w[:, 1024:1536], w[:, 3088:3600], w[:, 0:1024], w[:, 1552:3088], w[:, 1536:1552], pad], axis=1)


def _from_kernel_columns(g):
    return jnp.concatenate([g[:, 1024:2048], g[:, 0:512], g[:, COL_LR:COL_LR + GLA_LOWRANK], g[:, 2048:3584],
                            g[:, 512:1024]], axis=1)


def _row(vec, width):
    vec = vec.reshape(1, -1)
    return jnp.pad(vec, ((0, 0), (0, width - vec.shape[1])))


def kernel(x, c, w_ada, b_ada, g_pre, w_in, w_gate_up, b_gate_up, g_gla, g_dil, w_out, g_post, loss_target, m_w_ada, m_b_ada, m_g_pre, m_w_in, m_w_gate_up, m_b_gate_up, m_g_gla, m_g_dil, m_w_out, m_g_post, v_w_ada, v_b_ada, v_g_pre, v_w_in, v_w_gate_up, v_b_gate_up, v_g_gla, v_g_dil, v_w_out, v_g_post):
    px, py, pc = _my_position()
    me = _linear(px, py, pc)
    xs = x[0]
    target = loss_target[0]
    s_len = xs.shape[0]
    assert s_len % (DIL_BLOCK * max(DIL_DILATIONS) * 2) == 0 and xs.shape[1] == D_MODEL

    c_all = _all_gather(jnp.pad(c, ((0, 7), (0, 0))), "gather_c").reshape(N_DEV, 8, D_MODEL)[:, 0]
    mod_part = _mod_fwd(c_all, w_ada)
    mod_all = _all_gather(mod_part.reshape(DEPTH * N_DEV, ADA_SHARD), "gather_mod")
    mod_all = mod_all.reshape(N_DEV, DEPTH, N_DEV, ADA_SHARD)
    mod_mine = lax.dynamic_index_in_dim(mod_all, me, axis=2, keepdims=False)
    mod = jnp.transpose(mod_mine, (1, 0, 2)).reshape(DEPTH, 3 * D_MODEL) + b_ada

    w_in_all = _all_gather(w_in.astype(BF16).reshape(DEPTH * D_MODEL, W_IN_SHARD), "gather_w_in")
    w_in_all = w_in_all.reshape(N_DEV, DEPTH, D_MODEL, W_IN_SHARD)
    w_in_full = jnp.transpose(w_in_all, (1, 2, 0, 3)).reshape(DEPTH, D_MODEL, IN_COLS)
    w_out_all = _all_gather(w_out.astype(BF16).reshape(DEPTH * OUT_SHARD, D_MODEL), "gather_w_out")
    w_out_full = jnp.transpose(w_out_all.reshape(N_DEV, DEPTH, OUT_SHARD, D_MODEL), (1, 0, 2, 3)).reshape(
        DEPTH, D_MODEL, D_MODEL)
    wgu_all = _all_gather(w_gate_up.reshape(DEPTH * GLA_LOWRANK, GU_SHARD), "gather_w_gu")
    wgu_full = jnp.transpose(wgu_all.reshape(N_DEV, DEPTH, GLA_LOWRANK, GU_SHARD), (1, 2, 0, 3)).reshape(
        DEPTH, GLA_LOWRANK, GU_COLS)
    wgu_pad = jnp.pad(wgu_full, ((0, 0), (0, LANE - GLA_LOWRANK), (0, 0))).astype(BF16)

    cos, sin_signed = _rope_tables(s_len)
    g_heads = jnp.concatenate([g_gla, g_dil], axis=1)

    saved = []
    xl = xs
    for l in range(DEPTH):
        shift, scale, gate = (mod[l, k * D_MODEL:(k + 1) * D_MODEL].reshape(1, D_MODEL) for k in range(3))
        w_new = _to_kernel_columns(w_in_full[l])
        proj, h = _prenorm_proj(xl, g_pre[l:l + 1], scale, shift, w_new)
        o_a, states = _gla_fwd(proj, wgu_pad[l], b_gate_up[l:l + 1])
        o_b, lse = _dil_fwd(proj, cos, sin_signed)
        x_next, y, u = _post_fwd(o_a, o_b, proj, g_heads[l:l + 1], w_out_full[l], xl, gate, g_post[l:l + 1])
        saved.append((xl, scale, gate, w_new, proj, h, o_a, states, o_b, lse, y, u))
        xl = x_next

    dx, loss_part = _loss_grad(xl, target)

    small_rows = []
    gw_in, gw_out = [None] * DEPTH, [None] * DEPTH
    for l in reversed(range(DEPTH)):
        x_in, scale, gate, w_new, proj, h, o_a, states, o_b, lse, y, u = saved[l]
        du, do, dz, sums_post = _post_bwd(dx, u, gate, g_post[l:l + 1], w_out_full[l], o_a, o_b, proj, g_heads[l:l + 1])
        gw_out[l] = _matmul_tn(y, du, "grad_w_out", 512)
        dq_a, dk_a, dv_a, dlr2, dwgu, dbgu = _gla_bwd(proj, wgu_pad[l], b_gate_up[l:l + 1], states, do)
        dq_b, dk_b, dv_b = _dil_bwd(proj, cos, sin_signed, do, o_b, lse)
        dlr = (dlr2[0] + dlr2[1]).astype(BF16)
        pieces = (dz, dq_a, dk_a, dv_a, dq_b, dk_b, dv_b, dlr)
        dx, sums_in = _in_bwd(pieces, w_new, x_in, dx, g_pre[l:l + 1], scale)
        blocks = {1024: 512, 256: 256, 512: 512, 128: 128}
        g_new = jnp.concatenate([_matmul_tn(h, p, "grad_w_in_%d" % p.shape[1], blocks[p.shape[1]]) for p in pieces],
                                axis=1)
        gw_in[l] = _from_kernel_columns(g_new)
        dmod = jnp.concatenate([sums_in[0], sums_in[1], sums_post[0]])
        vecs = jnp.concatenate([sums_in[2], sums_post[1], sums_post[2], dbgu[0]])
        small_rows[0:0] = [_row(dmod, 4096), _row(vecs, 4096), _row(dwgu[:GLA_LOWRANK], 4096)]
    grad_x = dx[None]

    gin = jnp.stack(gw_in).astype(BF16).reshape(DEPTH, D_MODEL, N_DEV, W_IN_SHARD)
    gin = jnp.transpose(gin, (2, 0, 1, 3)).reshape(N_DEV * DEPTH * D_MODEL, W_IN_SHARD)
    gin_parts = _all_to_all(gin, "exchange_grad_w_in").reshape(N_DEV, DEPTH * D_MODEL, W_IN_SHARD)
    gout = jnp.stack(gw_out).astype(BF16).reshape(DEPTH, N_DEV, OUT_SHARD, D_MODEL)
    gout = jnp.transpose(gout, (1, 0, 2, 3)).reshape(N_DEV * DEPTH * OUT_SHARD, D_MODEL)
    gout_parts = _all_to_all(gout, "exchange_grad_w_out").reshape(N_DEV, DEPTH * OUT_SHARD, D_MODEL)

    flat = lambda a, rows: a.reshape(rows, a.shape[-1])
    r_in, r_out, r_ada = DEPTH * D_MODEL, DEPTH * OUT_SHARD, DEPTH * D_MODEL
    g_w_in, d_w_in, nm_w_in, nv_w_in = (
        t.reshape(w_in.shape) for t in _adamw(flat(w_in, r_in), gin_parts, flat(m_w_in, r_in), flat(v_w_in, r_in),
                                              "adamw_w_in", 256))
    g_w_out, d_w_out, nm_w_out, nv_w_out = (
        t.reshape(w_out.shape) for t in _adamw(flat(w_out, r_out), gout_parts, flat(m_w_out, r_out),
                                               flat(v_w_out, r_out), "adamw_w_out", 128))

    small_rows += [_row(loss_part[0, 0:1], 4096), jnp.zeros((1, 4096), F32)]
    small = _all_gather(jnp.concatenate(small_rows, axis=0), "gather_small").reshape(N_DEV, 8, 4096)
    dmod_all = jnp.stack([small[:, 0, :3 * D_MODEL], small[:, 3, :3 * D_MODEL]])
    dmod_cols = lax.dynamic_slice_in_dim(dmod_all, me * ADA_SHARD, ADA_SHARD, axis=2)
    gwa = _w_ada_grad(c_all, dmod_cols).reshape(1, r_ada, ADA_SHARD)
    g_w_ada, d_w_ada, nm_w_ada, nv_w_ada = (
        t.reshape(w_ada.shape) for t in _adamw(flat(w_ada, r_ada), gwa, flat(m_w_ada, r_ada), flat(v_w_ada, r_ada),
                                               "adamw_w_ada", 256))

    def small_param(w, m, v, cols, row, name):
        n = w.shape[1]
        parts = jnp.stack([small[:, row, cols:cols + n], small[:, row + 3, cols:cols + n]], axis=1)
        return _adamw(w, parts, m, v, name, DEPTH)

    g_b_ada, d_b_ada, nm_b_ada, nv_b_ada = small_param(b_ada, m_b_ada, v_b_ada, 0, 0, "adamw_b_ada")
    g_g_pre, d_g_pre, nm_g_pre, nv_g_pre = small_param(g_pre, m_g_pre, v_g_pre, 0, 1, "adamw_g_pre")
    g_g_post, d_g_post, nm_g_post, nv_g_post = small_param(g_post, m_g_post, v_g_post, 1024, 1, "adamw_g_post")
    g_g_gla, d_g_gla, nm_g_gla, nv_g_gla = small_param(g_gla, m_g_gla, v_g_gla, 2048, 1, "adamw_g_gla")
    g_g_dil, d_g_dil, nm_g_dil, nv_g_dil = small_param(g_dil, m_g_dil, v_g_dil, 2560, 1, "adamw_g_dil")
    g_b_gu, d_b_gu, nm_b_gu, nv_b_gu = small_param(b_gate_up, m_b_gate_up, v_b_gate_up, 3072, 1, "adamw_b_gate_up")
    gu_parts = jnp.stack([small[:, 2], small[:, 5]], axis=1).reshape(N_DEV, DEPTH, GLA_LOWRANK, GU_COLS)
    gu_parts = lax.dynamic_slice_in_dim(gu_parts, me * GU_SHARD, GU_SHARD, axis=3).reshape(
        N_DEV, DEPTH * GLA_LOWRANK, GU_SHARD)
    r_gu = DEPTH * GLA_LOWRANK
    g_w_gu, d_w_gu, nm_w_gu, nv_w_gu = (
        t.reshape(w_gate_up.shape) for t in _adamw(flat(w_gate_up, r_gu), gu_parts, flat(m_w_gate_up, r_gu),
                                                   flat(v_w_gate_up, r_gu), "adamw_w_gate_up", r_gu))
    loss_parts = jnp.broadcast_to(small[:, 6, 0:1].reshape(N_DEV, 1, 1), (N_DEV, 8, LANE))
    loss = _sum_parts(loss_parts)[0, 0]

    return (loss, grad_x,
            g_w_ada, g_b_ada, g_g_pre, g_w_in, g_w_gu, g_b_gu, g_g_gla, g_g_dil, g_w_out, g_g_post,
            d_w_ada, d_b_ada, d_g_pre, d_w_in, d_w_gu, d_b_gu, d_g_gla, d_g_dil, d_w_out, d_g_post,
            nm_w_ada, nm_b_ada, nm_g_pre, nm_w_in, nm_w_gu, nm_b_gu, nm_g_gla, nm_g_dil, nm_w_out, nm_g_post,
            nv_w_ada, nv_b_ada, nv_g_pre, nv_w_in, nv_w_gu, nv_b_gu, nv_g_gla, nv_g_dil, nv_w_out, nv_g_post)


def _sum_parts(parts):
    n_parts = parts.shape[0]

    def body(p_ref, o_ref):
        acc = p_ref[0]
        for k in range(1, n_parts):
            acc = acc + p_ref[k]
        o_ref[...] = acc

    return pl.pallas_call(body, name="sum_loss", out_shape=jax.ShapeDtypeStruct(parts.shape[1:], F32))(parts)
```

```python
import functools
import math

import jax
import jax.numpy as jnp
from jax import lax
from jax.experimental import pallas as pl
from jax.experimental.pallas import tpu as pltpu

F32 = jnp.float32
BF16 = jnp.bfloat16

N_DEV = 8
D_MODEL = 1024
DEPTH = 2
GLA_HEADS = 4
GLA_DK = 64
GLA_DV = 128
GLA_CHUNK = 64
GLA_TAU = 16.0
GLA_LOWRANK = 16
DIL_HEADS = 4
DIL_HD = 128
DIL_BLOCK = 128
DIL_DILATIONS = (1, 4, 16)
ROPE_THETA = 10000.0
EPS = 1e-6
IN_COLS = 3600
W_IN_SHARD = IN_COLS // N_DEV
ADA_SHARD = 3 * D_MODEL // N_DEV
OUT_SHARD = D_MODEL // N_DEV
GU_COLS = GLA_HEADS * GLA_DK
GU_SHARD = GU_COLS // N_DEV

ADAM_LR = 0.001
ADAM_B1 = 0.9
ADAM_B2 = 0.999
ADAM_EPS = 1e-08
ADAM_WD = 0.01
ADAM_STEP = 10

NP = 3712
COL_Z, COL_QA, COL_KA, COL_QB, COL_KB, COL_VA, COL_VB, COL_LR = 0, 1024, 1280, 1536, 2048, 2560, 3072, 3584
NP_F32 = COL_VA
NP_BF16 = NP - NP_F32
LANE = 128
MASK_VALUE = -1e30

MESH = pl.DeviceIdType.MESH
ANY = pl.BlockSpec(memory_space=pl.ANY)


def _params(sem=None, vmem_mb=None):
    kw = {}
    if sem is not None:
        kw["dimension_semantics"] = sem
    if vmem_mb is not None:
        kw["vmem_limit_bytes"] = vmem_mb * 1024 * 1024
    return pltpu.CompilerParams(**kw)


def _dot(a, b):
    return jnp.dot(a, b, preferred_element_type=F32)


def _dot_nt(a, b):
    return lax.dot_general(a, b, (((1,), (1,)), ((), ())), preferred_element_type=F32)


def _dot_tn(a, b):
    return lax.dot_general(a, b, (((0,), (0,)), ((), ())), preferred_element_type=F32)


def _sigmoid(z):
    return 1.0 / (1.0 + jnp.exp(-z))


def _log_sigmoid(z):
    return jnp.minimum(z, 0.0) - jnp.log(1.0 + jnp.exp(-jnp.abs(z)))


def _my_position():
    return lax.axis_index("x"), lax.axis_index("y"), lax.axis_index("c")


def _linear(px, py, pc):
    return 4 * px + 2 * py + pc


def _all_gather(xs, name):
    m, n = xs.shape

    def body(x_ref, out_ref, send_sems, recv_sems, local_sem):
        x, y, c = _my_position()
        me, sibling = (x, y, c), (x, y, 1 - c)
        chips = [(1 - x, y), (x, 1 - y), (1 - x, 1 - y)]

        def rows(px, py, pc):
            return out_ref.at[pl.ds(_linear(px, py, pc) * m, m), :]

        def copy(k, block, to, src=None):
            return pltpu.make_async_remote_copy(
                src_ref=rows(*block) if src is None else src, dst_ref=rows(*block),
                send_sem=send_sems.at[k], recv_sem=recv_sems.at[k], device_id=to, device_id_type=MESH)

        mine = pltpu.make_async_copy(x_ref, rows(*me), local_sem)
        mine.start()
        first = [copy(0, me, sibling, src=x_ref)]
        first += [copy(1 + j, me, (*chip, c), src=x_ref) for j, chip in enumerate(chips)]
        for cp in first:
            cp.start()
        passed = [copy(4 + j, (*chip, c), sibling) for j, chip in enumerate(chips)]
        for j, chip in enumerate(chips):
            copy(1 + j, (*chip, c), me).wait_recv()
            passed[j].start()
        copy(0, sibling, me).wait_recv()
        for j, chip in enumerate(chips):
            copy(4 + j, (*chip, 1 - c), me).wait_recv()
        for cp in first + passed:
            cp.wait_send()
        mine.wait()

    return pl.pallas_call(
        body, name=name, out_shape=jax.ShapeDtypeStruct((N_DEV * m, n), xs.dtype),
        in_specs=[ANY], out_specs=ANY,
        scratch_shapes=[pltpu.SemaphoreType.DMA((7,)), pltpu.SemaphoreType.DMA((7,)), pltpu.SemaphoreType.DMA(())],
    )(xs)


def _all_to_all(xs, name):
    m8, n = xs.shape
    m = m8 // N_DEV

    def body(x_ref, out_ref, send_sems, recv_sems, local_sem):
        x, y, c = _my_position()
        me = _linear(x, y, c)

        def rows(ref, idx):
            return ref.at[pl.ds(idx * m, m), :]

        local = pltpu.make_async_copy(rows(x_ref, me), rows(out_ref, me), local_sem)
        local.start()
        peers = []
        for j in range(1, N_DEV):
            px = 1 - x if j & 4 else x
            py = 1 - y if j & 2 else y
            pc = 1 - c if j & 1 else c
            peers.append((px, py, pc))
        sends = []
        for j, peer in enumerate(peers):
            cp = pltpu.make_async_remote_copy(
                src_ref=rows(x_ref, _linear(*peer)), dst_ref=rows(out_ref, me),
                send_sem=send_sems.at[j], recv_sem=recv_sems.at[j], device_id=peer, device_id_type=MESH)
            cp.start()
            sends.append(cp)
        for j, peer in enumerate(peers):
            pltpu.make_async_remote_copy(
                src_ref=rows(x_ref, _linear(*peer)), dst_ref=rows(out_ref, _linear(*peer)),
                send_sem=send_sems.at[j], recv_sem=recv_sems.at[j], device_id=peer, device_id_type=MESH).wait_recv()
        for cp in sends:
            cp.wait_send()
        local.wait()

    return pl.pallas_call(
        body, name=name, out_shape=jax.ShapeDtypeStruct((m8, n), xs.dtype),
        in_specs=[ANY], out_specs=ANY,
        scratch_shapes=[pltpu.SemaphoreType.DMA((7,)), pltpu.SemaphoreType.DMA((7,)), pltpu.SemaphoreType.DMA(())],
    )(xs)


def _mod_fwd(c_all, w_ada):
    def body(c_ref, w_ref, o_ref):
        cv = c_ref[...]
        sc = cv * _sigmoid(cv)
        o_ref[0] = _dot(sc.astype(BF16), w_ref[0].astype(BF16))

    return pl.pallas_call(
        body, name="mod_fwd", grid=(DEPTH,),
        out_shape=jax.ShapeDtypeStruct((DEPTH, N_DEV, ADA_SHARD), F32),
        in_specs=[pl.BlockSpec((N_DEV, D_MODEL), lambda l: (0, 0)),
                  pl.BlockSpec((1, D_MODEL, ADA_SHARD), lambda l: (l, 0, 0))],
        out_specs=pl.BlockSpec((1, N_DEV, ADA_SHARD), lambda l: (l, 0, 0)),
        compiler_params=_params(("arbitrary",)),
    )(c_all, w_ada)


def _w_ada_grad(c_all, dmod_cols):
    def body(c_ref, d_ref, o_ref):
        cv = c_ref[...]
        sc = cv * _sigmoid(cv)
        o_ref[0] = lax.dot_general(sc, d_ref[0], (((0,), (0,)), ((), ())), precision=lax.Precision.HIGHEST,
                                   preferred_element_type=F32)

    return pl.pallas_call(
        body, name="w_ada_grad", grid=(DEPTH,),
        out_shape=jax.ShapeDtypeStruct((DEPTH, D_MODEL, ADA_SHARD), F32),
        in_specs=[pl.BlockSpec((N_DEV, D_MODEL), lambda l: (0, 0)),
                  pl.BlockSpec((1, N_DEV, ADA_SHARD), lambda l: (l, 0, 0))],
        out_specs=pl.BlockSpec((1, D_MODEL, ADA_SHARD), lambda l: (l, 0, 0)),
        compiler_params=_params(("arbitrary",)),
    )(c_all, dmod_cols)


def _prenorm_proj(x, g_pre, scale, shift, w_new, ts=256):
    s_len = x.shape[0]

    def body(x_ref, g_ref, sc_ref, sh_ref, w_ref, pf_ref, pb_ref, h_ref):
        xv = x_ref[...]
        rstd = lax.rsqrt(jnp.mean(xv * xv, axis=-1, keepdims=True) + EPS)
        h = (xv * rstd * g_ref[...]) * (1.0 + sc_ref[...]) + sh_ref[...]
        hb = h.astype(BF16)
        h_ref[...] = hb
        for j in range(0, NP, 512):
            w = min(512, NP - j)
            acc = _dot(hb, w_ref[:, j:j + w])
            if j < NP_F32:
                pf_ref[:, j:j + w] = acc
            else:
                pb_ref[:, j - NP_F32:j - NP_F32 + w] = acc.astype(BF16)

    vec = pl.BlockSpec((1, D_MODEL), lambda i: (0, 0))
    return pl.pallas_call(
        body, name="prenorm_proj", grid=(s_len // ts,),
        out_shape=(jax.ShapeDtypeStruct((s_len, NP_F32), F32), jax.ShapeDtypeStruct((s_len, NP_BF16), BF16),
                   jax.ShapeDtypeStruct((s_len, D_MODEL), BF16)),
        in_specs=[pl.BlockSpec((ts, D_MODEL), lambda i: (i, 0)), vec, vec, vec,
                  pl.BlockSpec((D_MODEL, NP), lambda i: (0, 0))],
        out_specs=(pl.BlockSpec((ts, NP_F32), lambda i: (i, 0)), pl.BlockSpec((ts, NP_BF16), lambda i: (i, 0)),
                   pl.BlockSpec((ts, D_MODEL), lambda i: (i, 0))),
        compiler_params=_params(("arbitrary",), 48),
    )(x, g_pre, scale, shift, w_new)


def _gla_chunk_common(q_ref, k_ref, lr_ref, wgu_ref, bgu_ref, rows):
    c = GLA_CHUNK
    q = q_ref[rows, :].astype(F32) * (GLA_DK ** -0.5)
    k = k_ref[rows, :].astype(F32)
    z = _dot(lr_ref[rows, :], wgu_ref[...]) + bgu_ref[...]
    la = _log_sigmoid(z) * (1.0 / GLA_TAU)
    ri = lax.broadcasted_iota(jnp.int32, (c, c), 0)
    ci = lax.broadcasted_iota(jnp.int32, (c, c), 1)
    tril = (ri >= ci).astype(F32)
    b = jnp.dot(tril, la, precision=lax.Precision.HIGHEST, preferred_element_type=F32)
    bl = b[c - 1:c, :]
    qe = q * jnp.exp(b)
    ke = k * jnp.exp(-b)
    kend = k * jnp.exp(bl - b)
    dec = jnp.exp(bl)
    return q, k, z, b, bl, qe, ke, kend, dec, ri, ci


def _head_lane_mask(hh):
    return (lax.broadcasted_iota(jnp.int32, (1, LANE), 1) // GLA_DK) == hh


def _state_block_mask():
    r = lax.broadcasted_iota(jnp.int32, (2 * GLA_DV, LANE), 0) // GLA_DV
    cc = lax.broadcasted_iota(jnp.int32, (2 * GLA_DV, LANE), 1) // GLA_DK
    return r == cc


def _gla_fwd(pf, pb, wgu, bgu):
    s_len = pf.shape[0]
    nc = s_len // GLA_CHUNK

    def body(q_ref, k_ref, v_ref, lr_ref, wgu_ref, bgu_ref, o_ref, st_ref, state):
        state[...] = jnp.zeros_like(state)
        bd = _state_block_mask()

        def chunk(n, carry):
            rows = pl.ds(pl.multiple_of(n * GLA_CHUNK, GLA_CHUNK), GLA_CHUNK)
            q, k, z, b, bl, qe, ke, kend, dec, ri, ci = _gla_chunk_common(q_ref, k_ref, lr_ref, wgu_ref, bgu_ref, rows)
            v = v_ref[rows, :]
            st = state[...]
            stb = st.astype(BF16)
            st_ref[0, n] = stb
            keb = ke.astype(BF16)
            o = _dot_nt(qe.astype(BF16), stb)
            parts = []
            for hh in range(2):
                qeh = jnp.where(_head_lane_mask(hh), qe, 0.0).astype(BF16)
                a = jnp.where(ri >= ci, _dot_nt(qeh, keb), 0.0)
                parts.append(_dot(a.astype(BF16), v[:, hh * GLA_DV:(hh + 1) * GLA_DV]))
            o_ref[rows, :] = o + jnp.concatenate(parts, axis=1)
            cs_t = jnp.where(bd, _dot_tn(v, kend.astype(BF16)), 0.0)
            state[...] = dec * st + cs_t
            return carry

        lax.fori_loop(0, nc, chunk, 0, unroll=2)

    return pl.pallas_call(
        body, name="gla_fwd", grid=(2,),
        out_shape=(jax.ShapeDtypeStruct((s_len, GLA_HEADS * GLA_DV), F32),
                   jax.ShapeDtypeStruct((2, nc, 2 * GLA_DV, LANE), BF16)),
        in_specs=[pl.BlockSpec((s_len, LANE), lambda g: (0, COL_QA // LANE + g)),
                  pl.BlockSpec((s_len, LANE), lambda g: (0, COL_KA // LANE + g)),
                  pl.BlockSpec((s_len, 2 * GLA_DV), lambda g: (0, (COL_VA - NP_F32) // (2 * GLA_DV) + g)),
                  pl.BlockSpec((s_len, LANE), lambda g: (0, (COL_LR - NP_F32) // LANE)),
                  pl.BlockSpec((LANE, LANE), lambda g: (0, g)),
                  pl.BlockSpec((1, LANE), lambda g: (0, g))],
        out_specs=(pl.BlockSpec((s_len, 2 * GLA_DV), lambda g: (0, g)),
                   pl.BlockSpec((1, nc, 2 * GLA_DV, LANE), lambda g: (g, 0, 0, 0))),
        scratch_shapes=[pltpu.VMEM((2 * GLA_DV, LANE), F32)],
        compiler_params=_params(("arbitrary",), 48),
    )(pf, pf, pb, pb, wgu, bgu)


def _rope_tables(s_len):
    inv_freq = ROPE_THETA ** (-jnp.arange(0, DIL_HD, 2, dtype=F32) / DIL_HD)
    ang = jnp.arange(s_len, dtype=F32)[:, None] * inv_freq[None, :]
    cos, sin = jnp.cos(ang), jnp.sin(ang)
    return jnp.concatenate([cos, cos], axis=1), jnp.concatenate([-sin, sin], axis=1)


def _rope(xv, cos, sin_signed):
    return xv * cos + pltpu.roll(xv, DIL_HD // 2, 1) * sin_signed


DIL_GROUP = 4


def _dil_pair_block(i, half, d, nblk):
    nb = nblk // d
    j = i + half * (nblk // DIL_GROUP)
    if nb >= 2 * DIL_GROUP:
        r, n = j % d, j // d
    else:
        r, n = j // nb, j % nb
    kb = jnp.maximum(n - 1, 0)
    qs = r + d * DIL_BLOCK * n
    ks = r + d * DIL_BLOCK * kb
    return qs, ks, jnp.minimum(n, 1)


def _dil_fill_bias(bias):
    qi = lax.broadcasted_iota(jnp.int32, (DIL_BLOCK, 2 * DIL_BLOCK), 0)
    kj = lax.broadcasted_iota(jnp.int32, (DIL_BLOCK, 2 * DIL_BLOCK), 1)
    for sel in range(2):
        dist = qi - kj + DIL_BLOCK * sel
        bias[sel] = jnp.where((dist >= 0) & (dist <= DIL_BLOCK), 0.0, MASK_VALUE)


def _strided(start, size, d):
    return pl.ds(start, size) if d == 1 else pl.ds(start, size, stride=d)


def _dil_fwd(pf, pb, cos, sin_signed):
    s_len = pf.shape[0]
    nblk = s_len // DIL_BLOCK
    prep_rows = 256
    scale = DIL_HD ** -0.5

    def body(q_ref, k_ref, v_ref, cos_ref, sin_ref, o_ref, lse_ref, qf, kf, vf, o0, o1, o2, l0, l1, l2, bias):
        _dil_fill_bias(bias)

        def prep(t, carry):
            rows = pl.ds(pl.multiple_of(t * prep_rows, prep_rows), prep_rows)
            cs, sn = cos_ref[rows, :], sin_ref[rows, :]
            qf[rows, :] = _rope(q_ref[rows, :], cs, sn)
            kf[rows, :] = _rope(k_ref[rows, :], cs, sn)
            vf[rows, :] = v_ref[rows, :].astype(F32)
            return carry

        lax.fori_loop(0, s_len // prep_rows, prep, 0)
        ones = jnp.ones((2 * DIL_BLOCK, DIL_HD), BF16)

        for d, o_p, l_p in zip(DIL_DILATIONS, (o0, o1, o2), (l0, l1, l2)):
            def pair(i, carry, d=d, o_p=o_p, l_p=l_p):
                idx = [_dil_pair_block(i, half, d, nblk) for half in range(DIL_GROUP)]
                ld = [(qf[_strided(qs, DIL_BLOCK, d), :].astype(BF16),
                       kf[_strided(ks, 2 * DIL_BLOCK, d), :].astype(BF16),
                       vf[_strided(ks, 2 * DIL_BLOCK, d), :].astype(BF16)) for qs, ks, _ in idx]
                s = [_dot_nt(qb, kk) * scale + bias[sel] for (qb, kk, _), (_, _, sel) in zip(ld, idx)]
                m = [jnp.max(sv, axis=-1, keepdims=True) for sv in s]
                p = [jnp.exp(sv - mv) for sv, mv in zip(s, m)]
                hi = [pv.astype(BF16) for pv in p]
                lo = [(pv - hv.astype(F32)).astype(BF16) for pv, hv in zip(p, hi)]
                r = [_dot(hv, jnp.concatenate([vv, ones], axis=1)) for hv, (_, _, vv) in zip(hi, ld)]
                r2 = [_dot(lv, ones) for lv in lo]
                for rv, r2v, mv, (qs, _, _) in zip(r, r2, m, idx):
                    den = rv[:, DIL_HD:] + r2v
                    o_p[_strided(qs, DIL_BLOCK, d), :] = rv[:, :DIL_HD] / den
                    l_p[_strided(qs, DIL_BLOCK, d), :] = mv + jnp.log(den)
                return carry

            lax.fori_loop(0, nblk // DIL_GROUP, pair, 0)

        def comb(t, carry):
            rows = pl.ds(pl.multiple_of(t * prep_rows, prep_rows), prep_rows)
            a0, a1, a2 = l0[rows, :], l1[rows, :], l2[rows, :]
            m = jnp.maximum(jnp.maximum(a0, a1), a2)
            e0, e1, e2 = jnp.exp(a0 - m), jnp.exp(a1 - m), jnp.exp(a2 - m)
            tot = e0 + e1 + e2
            o_ref[rows, :] = (e0 * o0[rows, :] + e1 * o1[rows, :] + e2 * o2[rows, :]) / tot
            lse_ref[rows, :] = m + jnp.log(tot)
            return carry

        lax.fori_loop(0, s_len // prep_rows, comb, 0)

    head = lambda base: pl.BlockSpec((s_len, DIL_HD), lambda h: (0, base // DIL_HD + h))
    table = pl.BlockSpec((s_len, DIL_HD), lambda h: (0, 0))
    out = pl.BlockSpec((s_len, DIL_HD), lambda h: (0, h))
    return pl.pallas_call(
        body, name="dil_fwd", grid=(DIL_HEADS,),
        out_shape=(jax.ShapeDtypeStruct((s_len, DIL_HEADS * DIL_HD), F32),
                   jax.ShapeDtypeStruct((s_len, DIL_HEADS * DIL_HD), F32)),
        in_specs=[head(COL_QB), head(COL_KB), head(COL_VB - NP_F32), table, table],
        out_specs=(out, out),
        scratch_shapes=[pltpu.VMEM((s_len, DIL_HD), F32) for _ in range(9)]
        + [pltpu.VMEM((2, DIL_BLOCK, 2 * DIL_BLOCK), F32)],
        compiler_params=_params(("arbitrary",), 56),
    )(pf, pf, pb, cos, sin_signed)


def _silu_and_grad(z):
    sg = _sigmoid(z)
    return z * sg, sg * (1.0 + z * (1.0 - sg))


def _post_fwd(o_a, o_b, pf, g_heads, w_out, x, gate, g_post, ts=256):
    s_len = x.shape[0]
    half = GLA_HEADS * GLA_DV

    def body(oa_ref, ob_ref, z_ref, gh_ref, w_ref, x_ref, gate_ref, gp_ref, xo_ref, y_ref, u_ref):
        for src, base in ((oa_ref, 0), (ob_ref, half)):
            for hh in range(4):
                lo = hh * LANE
                og = src[:, lo:lo + LANE]
                on = og * lax.rsqrt(jnp.mean(og * og, axis=-1, keepdims=True) + EPS)
                zg = z_ref[:, base + lo:base + lo + LANE].astype(F32)
                y_ref[:, base + lo:base + lo + LANE] = (on * gh_ref[:, base + lo:base + lo + LANE]
                                                        * (zg * _sigmoid(zg))).astype(BF16)
        u = _dot(y_ref[...], w_ref[...])
        u_ref[...] = u.astype(BF16)
        rstd = lax.rsqrt(jnp.mean(u * u, axis=-1, keepdims=True) + EPS)
        xo_ref[...] = x_ref[...] + gate_ref[...] * (u * rstd * gp_ref[...])

    vec = pl.BlockSpec((1, D_MODEL), lambda i: (0, 0))
    tile = pl.BlockSpec((ts, D_MODEL), lambda i: (i, 0))
    halft = pl.BlockSpec((ts, half), lambda i: (i, 0))
    return pl.pallas_call(
        body, name="post_fwd", grid=(s_len // ts,),
        out_shape=(jax.ShapeDtypeStruct((s_len, D_MODEL), F32), jax.ShapeDtypeStruct((s_len, D_MODEL), BF16),
                   jax.ShapeDtypeStruct((s_len, D_MODEL), BF16)),
        in_specs=[halft, halft, tile, vec, pl.BlockSpec((D_MODEL, D_MODEL), lambda i: (0, 0)), tile, vec, vec],
        out_specs=(tile, tile, tile),
        compiler_params=_params(("arbitrary",), 40),
    )(o_a, o_b, pf, g_heads, w_out, x, gate, g_post)


def _loss_grad(y, target, ts=512):
    s_len = y.shape[0]

    def body(y_ref, t_ref, dy_ref, loss_ref):
        @pl.when(pl.program_id(0) == 0)
        def _():
            loss_ref[...] = jnp.zeros_like(loss_ref)

        e = y_ref[...] - t_ref[...]
        dy_ref[...] = e * (1.0 / D_MODEL)
        loss_ref[...] += 0.5 * jnp.sum(jnp.mean(e * e, axis=-1, keepdims=True))

    tile = pl.BlockSpec((ts, D_MODEL), lambda i: (i, 0))
    return pl.pallas_call(
        body, name="loss_grad", grid=(s_len // ts,),
        out_shape=(jax.ShapeDtypeStruct((s_len, D_MODEL), F32), jax.ShapeDtypeStruct((8, LANE), F32)),
        in_specs=[tile, tile], out_specs=(tile, pl.BlockSpec((8, LANE), lambda i: (0, 0))),
        compiler_params=_params(("arbitrary",)),
    )(y, target)


def _post_bwd(dxo, u, gate, g_post, w_out, o_a, o_b, pf, g_heads, ts=256):
    s_len = dxo.shape[0]
    half = GLA_HEADS * GLA_DV

    def body(dx_ref, u_ref, gate_ref, gp_ref, w_ref, oa_ref, ob_ref, z_ref, gh_ref, du_ref, do_ref, dz_ref, sums_ref):
        @pl.when(pl.program_id(0) == 0)
        def _():
            sums_ref[...] = jnp.zeros_like(sums_ref)

        dx = dx_ref[...]
        u = u_ref[...].astype(F32)
        rstd = lax.rsqrt(jnp.mean(u * u, axis=-1, keepdims=True) + EPS)
        un = u * rstd
        sums_ref[0:1, :] += jnp.sum(dx * (un * gp_ref[...]), axis=0, keepdims=True)
        drn = dx * gate_ref[...]
        sums_ref[1:2, :] += jnp.sum(drn * un, axis=0, keepdims=True)
        dun = drn * gp_ref[...]
        du = rstd * (dun - un * jnp.mean(dun * un, axis=-1, keepdims=True))
        dub = du.astype(BF16)
        du_ref[...] = dub
        dy = _dot_nt(dub, w_ref[...])
        for src, base in ((oa_ref, 0), (ob_ref, half)):
            for hh in range(4):
                lo = base + hh * LANE
                og = src[:, hh * LANE:(hh + 1) * LANE]
                rs = lax.rsqrt(jnp.mean(og * og, axis=-1, keepdims=True) + EPS)
                on = og * rs
                zg = z_ref[:, lo:lo + LANE].astype(F32)
                sz, dsz = _silu_and_grad(zg)
                gg = gh_ref[:, lo:lo + LANE]
                dyg = dy[:, lo:lo + LANE]
                sums_ref[2:3, lo:lo + LANE] += jnp.sum(dyg * sz * on, axis=0, keepdims=True)
                dz_ref[:, lo:lo + LANE] = (dyg * on * gg * dsz).astype(BF16)
                don = dyg * gg * sz
                do_ref[:, lo:lo + LANE] = rs * (don - on * jnp.mean(don * on, axis=-1, keepdims=True))

    vec = pl.BlockSpec((1, D_MODEL), lambda i: (0, 0))
    tile = pl.BlockSpec((ts, D_MODEL), lambda i: (i, 0))
    halft = pl.BlockSpec((ts, half), lambda i: (i, 0))
    return pl.pallas_call(
        body, name="post_bwd", grid=(s_len // ts,),
        out_shape=(jax.ShapeDtypeStruct((s_len, D_MODEL), BF16), jax.ShapeDtypeStruct((s_len, D_MODEL), F32),
                   jax.ShapeDtypeStruct((s_len, D_MODEL), BF16), jax.ShapeDtypeStruct((8, D_MODEL), F32)),
        in_specs=[tile, tile, vec, vec, pl.BlockSpec((D_MODEL, D_MODEL), lambda i: (0, 0)), halft, halft, tile, vec],
        out_specs=(tile, tile, tile, pl.BlockSpec((8, D_MODEL), lambda i: (0, 0))),
        compiler_params=_params(("arbitrary",), 40),
    )(dxo, u, gate, g_post, w_out, o_a, o_b, pf, g_heads)


def _gla_bwd(pf, pb, wgu, bgu, states, do):
    s_len = pf.shape[0]
    nc = s_len // GLA_CHUNK
    c = GLA_CHUNK

    def body(q_ref, k_ref, v_ref, lr_ref, wgu_ref, bgu_ref, st_ref, do_ref,
             dq_ref, dk_ref, dv_ref, dlr_ref, dwgu_ref, dbgu_ref, dstate, dw_acc, db_acc):
        dstate[...] = jnp.zeros_like(dstate)
        dw_acc[...] = jnp.zeros_like(dw_acc)
        db_acc[...] = jnp.zeros_like(db_acc)
        bd = _state_block_mask()

        def chunk(t, carry):
            n = nc - 1 - t
            rows = pl.ds(pl.multiple_of(n * c, c), c)
            q, k, z, b, bl, qe, ke, kend, dec, ri, ci = _gla_chunk_common(q_ref, k_ref, lr_ref, wgu_ref, bgu_ref, rows)
            v = v_ref[rows, :]
            dov = do_ref[rows, :]
            dob = dov.astype(BF16)
            stb = st_ref[0, n]
            dst = dstate[...]
            dstb = dst.astype(BF16)
            qeb, keb, kendb = qe.astype(BF16), ke.astype(BF16), kend.astype(BF16)

            dqe = _dot(dob, stb)
            dkend = _dot(v, dstb)
            dv = _dot_nt(kendb, dstb)
            ddec = jnp.sum(dst * stb.astype(F32), axis=0, keepdims=True)
            dke = jnp.zeros((c, LANE), F32)
            dv_parts = []
            for hh in range(2):
                hm = _head_lane_mask(hh)
                qeh = jnp.where(hm, qe, 0.0).astype(BF16)
                keh = jnp.where(hm, ke, 0.0).astype(BF16)
                vh = v[:, hh * GLA_DV:(hh + 1) * GLA_DV]
                doh = dob[:, hh * GLA_DV:(hh + 1) * GLA_DV]
                a_t = jnp.where(ci >= ri, _dot_nt(keh, qeb), 0.0)
                da = jnp.where(ri >= ci, _dot_nt(doh, vh), 0.0)
                da_t = jnp.where(ci >= ri, _dot_nt(vh, doh), 0.0)
                dv_parts.append(_dot(a_t.astype(BF16), doh))
                dqe += jnp.where(hm, _dot(da.astype(BF16), keb), 0.0)
                dke += jnp.where(hm, _dot(da_t.astype(BF16), qeh), 0.0)
            dv_ref[rows, :] = (dv + jnp.concatenate(dv_parts, axis=1)).astype(BF16)

            eb = jnp.exp(b)
            dq_ref[rows, :] = (dqe * eb * (GLA_DK ** -0.5)).astype(BF16)
            dk_ref[rows, :] = (dke * jnp.exp(-b) + dkend * jnp.exp(bl - b)).astype(BF16)
            db = dqe * qe - dke * ke - dkend * kend
            dbl = jnp.sum(dkend * kend, axis=0, keepdims=True) + ddec * dec
            db = db + jnp.where(lax.broadcasted_iota(jnp.int32, (c, LANE), 0) == c - 1, dbl, 0.0)
            triu = (ci >= ri).astype(F32)
            dla = jnp.dot(triu, db, precision=lax.Precision.HIGHEST, preferred_element_type=F32)
            dz = dla * (1.0 / GLA_TAU) * _sigmoid(-z)
            dzb = dz.astype(BF16)
            dlr_ref[0, rows, :] = _dot_nt(dzb, wgu_ref[...])
            dw_acc[...] += _dot_tn(lr_ref[rows, :], dzb)
            db_acc[0:1, :] += jnp.sum(dz, axis=0, keepdims=True)

            dstate[...] = dec * dst + jnp.where(bd, _dot_tn(dob, qeb), 0.0)
            return carry

        lax.fori_loop(0, nc, chunk, 0, unroll=2)
        dwgu_ref[...] = dw_acc[...]
        dbgu_ref[...] = db_acc[...]

    pair = pl.BlockSpec((s_len, LANE), lambda g: (0, g))
    return pl.pallas_call(
        body, name="gla_bwd", grid=(2,),
        out_shape=(jax.ShapeDtypeStruct((s_len, GU_COLS), BF16), jax.ShapeDtypeStruct((s_len, GU_COLS), BF16),
                   jax.ShapeDtypeStruct((s_len, GLA_HEADS * GLA_DV), BF16),
                   jax.ShapeDtypeStruct((2, s_len, LANE), F32),
                   jax.ShapeDtypeStruct((LANE, GU_COLS), F32), jax.ShapeDtypeStruct((8, GU_COLS), F32)),
        in_specs=[pl.BlockSpec((s_len, LANE), lambda g: (0, COL_QA // LANE + g)),
                  pl.BlockSpec((s_len, LANE), lambda g: (0, COL_KA // LANE + g)),
                  pl.BlockSpec((s_len, 2 * GLA_DV), lambda g: (0, (COL_VA - NP_F32) // (2 * GLA_DV) + g)),
                  pl.BlockSpec((s_len, LANE), lambda g: (0, (COL_LR - NP_F32) // LANE)),
                  pl.BlockSpec((LANE, LANE), lambda g: (0, g)),
                  pl.BlockSpec((1, LANE), lambda g: (0, g)),
                  pl.BlockSpec((1, nc, 2 * GLA_DV, LANE), lambda g: (g, 0, 0, 0)),
                  pl.BlockSpec((s_len, 2 * GLA_DV), lambda g: (0, g))],
        out_specs=(pair, pair, pl.BlockSpec((s_len, 2 * GLA_DV), lambda g: (0, g)),
                   pl.BlockSpec((1, s_len, LANE), lambda g: (g, 0, 0)),
                   pl.BlockSpec((LANE, LANE), lambda g: (0, g)), pl.BlockSpec((8, LANE), lambda g: (0, g))),
        scratch_shapes=[pltpu.VMEM((2 * GLA_DV, LANE), F32), pltpu.VMEM((LANE, LANE), F32), pltpu.VMEM((8, LANE), F32)],
        compiler_params=_params(("arbitrary",), 56),
    )(pf, pf, pb, pb, wgu, bgu, states, do)


def _dil_bwd(pf, pb, cos, sin_signed, do, o_b, lse):
    s_len = pf.shape[0]
    nblk = s_len // DIL_BLOCK
    prep_rows = 256
    scale = DIL_HD ** -0.5

    def body(q_ref, k_ref, v_ref, cos_ref, sin_ref, do_ref, o_ref, lse_ref, dq_ref, dk_ref, dv_ref,
             qf, kf, vf, dl, dqa, dka, dva, bias):
        _dil_fill_bias(bias)

        def prep(t, carry):
            rows = pl.ds(pl.multiple_of(t * prep_rows, prep_rows), prep_rows)
            cs, sn = cos_ref[rows, :], sin_ref[rows, :]
            qf[rows, :] = _rope(q_ref[rows, :].astype(F32), cs, sn) * scale
            kf[rows, :] = _rope(k_ref[rows, :].astype(F32), cs, sn)
            vf[rows, :] = v_ref[rows, :].astype(F32)
            dl[rows, :] = jnp.broadcast_to(jnp.sum(do_ref[rows, :] * o_ref[rows, :], axis=-1, keepdims=True),
                                           (prep_rows, DIL_HD))
            zero = jnp.zeros((prep_rows, DIL_HD), F32)
            dqa[rows, :] = zero
            dka[rows, :] = zero
            dva[rows, :] = zero
            return carry

        lax.fori_loop(0, s_len // prep_rows, prep, 0)

        for d in DIL_DILATIONS:
            def pair(i, carry, d=d):
                idx = [_dil_pair_block(i, half, d, nblk) for half in range(DIL_GROUP)]
                rows = [(_strided(qs, DIL_BLOCK, d), _strided(ks, 2 * DIL_BLOCK, d)) for qs, ks, _ in idx]
                ld = [(qf[qr, :].astype(BF16), kf[kr, :].astype(BF16), vf[kr, :].astype(BF16),
                       do_ref[qr, :].astype(BF16)) for qr, kr in rows]
                s = [_dot_nt(qb, kk) + bias[sel] for (qb, kk, _, _), (_, _, sel) in zip(ld, idx)]
                dp = [_dot_nt(dob, vv) for _, _, vv, dob in ld]
                p = [jnp.exp(sv - lse_ref[qr, :][:, 0:1]) for sv, (qr, _) in zip(s, rows)]
                ds = [(pv * (dpv - dl[qr, :][:, 0:1])).astype(BF16) for pv, dpv, (qr, _) in zip(p, dp, rows)]
                pb = [pv.astype(BF16) for pv in p]
                gq = [_dot(dsv, kk) for dsv, (_, kk, _, _) in zip(ds, ld)]
                gk = [_dot_tn(dsv, qb) for dsv, (qb, _, _, _) in zip(ds, ld)]
                gv = [_dot_tn(pv, dob) for pv, (_, _, _, dob) in zip(pb, ld)]
                for (qr, kr), a, b, c in zip(rows, gq, gk, gv):
                    dqa[qr, :] += a
                    dka[kr, :] += b
                    dva[kr, :] += c
                return carry

            lax.fori_loop(0, nblk // DIL_GROUP, pair, 0)

        def fin(t, carry):
            rows = pl.ds(pl.multiple_of(t * prep_rows, prep_rows), prep_rows)
            cs, sn = cos_ref[rows, :], sin_ref[rows, :]
            gq, gk = dqa[rows, :] * scale, dka[rows, :]
            dq_ref[rows, :] = (gq * cs - pltpu.roll(gq, DIL_HD // 2, 1) * sn).astype(BF16)
            dk_ref[rows, :] = (gk * cs - pltpu.roll(gk, DIL_HD // 2, 1) * sn).astype(BF16)
            dv_ref[rows, :] = dva[rows, :].astype(BF16)
            return carry

        lax.fori_loop(0, s_len // prep_rows, fin, 0)

    head = lambda base: pl.BlockSpec((s_len, DIL_HD), lambda h: (0, base // DIL_HD + h))
    table = pl.BlockSpec((s_len, DIL_HD), lambda h: (0, 0))
    out = pl.BlockSpec((s_len, DIL_HD), lambda h: (0, h))
    shp = jax.ShapeDtypeStruct((s_len, DIL_HEADS * DIL_HD), BF16)
    return pl.pallas_call(
        body, name="dil_bwd", grid=(DIL_HEADS,),
        out_shape=(shp, shp, shp),
        in_specs=[head(COL_QB), head(COL_KB), head(COL_VB - NP_F32), table, table,
                  pl.BlockSpec((s_len, DIL_HD), lambda h: (0, DIL_HEADS + h)), out, out],
        out_specs=(out, out, out),
        scratch_shapes=[pltpu.VMEM((s_len, DIL_HD), F32) for _ in range(7)]
        + [pltpu.VMEM((2, DIL_BLOCK, 2 * DIL_BLOCK), F32)],
        compiler_params=_params(("arbitrary",), 56),
    )(pf, pf, pb, cos, sin_signed, do, o_b, lse)


_PIECES = ((COL_Z, 1024), (COL_QA, 256), (COL_KA, 256), (COL_QB, 512), (COL_KB, 512), (COL_VA, 512), (COL_VB, 512),
           (COL_LR, 128))


def _in_bwd(pieces, w_new, x, dxo, g_pre, scale, ts=256):
    s_len = x.shape[0]

    def body(*refs):
        p_refs = refs[:len(_PIECES)]
        w_ref, x_ref, dxo_ref, g_ref, sc_ref, dx_ref, sums_ref = refs[len(_PIECES):]

        @pl.when(pl.program_id(0) == 0)
        def _():
            sums_ref[...] = jnp.zeros_like(sums_ref)

        dh = jnp.zeros((ts, D_MODEL), F32)
        for p_ref, (col, width) in zip(p_refs, _PIECES):
            dh += _dot_nt(p_ref[...], w_ref[:, col:col + width])
        xv = x_ref[...]
        rstd = lax.rsqrt(jnp.mean(xv * xv, axis=-1, keepdims=True) + EPS)
        xn = xv * rstd
        sums_ref[0:1, :] += jnp.sum(dh, axis=0, keepdims=True)
        sums_ref[1:2, :] += jnp.sum(dh * (xn * g_ref[...]), axis=0, keepdims=True)
        dr = dh * (1.0 + sc_ref[...])
        sums_ref[2:3, :] += jnp.sum(dr * xn, axis=0, keepdims=True)
        dxn = dr * g_ref[...]
        dx_ref[...] = dxo_ref[...] + rstd * (dxn - xn * jnp.mean(dxn * xn, axis=-1, keepdims=True))

    vec = pl.BlockSpec((1, D_MODEL), lambda i: (0, 0))
    tile = pl.BlockSpec((ts, D_MODEL), lambda i: (i, 0))
    return pl.pallas_call(
        body, name="in_bwd", grid=(s_len // ts,),
        out_shape=(jax.ShapeDtypeStruct((s_len, D_MODEL), F32), jax.ShapeDtypeStruct((8, D_MODEL), F32)),
        in_specs=[pl.BlockSpec((ts, width), lambda i: (i, 0)) for _, width in _PIECES]
        + [pl.BlockSpec((D_MODEL, NP), lambda i: (0, 0)), tile, tile, vec, vec],
        out_specs=(tile, pl.BlockSpec((8, D_MODEL), lambda i: (0, 0))),
        compiler_params=_params(("arbitrary",), 48),
    )(*pieces, w_new, x, dxo, g_pre, scale)


def _matmul_tn(a, b, name, bn, ts=512):
    s_len, m = a.shape
    n = b.shape[1]

    def body(a_ref, b_ref, o_ref):
        @pl.when(pl.program_id(1) == 0)
        def _():
            o_ref[...] = jnp.zeros_like(o_ref)

        o_ref[...] += _dot_tn(a_ref[...], b_ref[...])

    return pl.pallas_call(
        body, name=name, grid=(n // bn, s_len // ts),
        out_shape=jax.ShapeDtypeStruct((m, n), F32),
        in_specs=[pl.BlockSpec((ts, m), lambda j, i: (i, 0)), pl.BlockSpec((ts, bn), lambda j, i: (i, j))],
        out_specs=pl.BlockSpec((m, bn), lambda j, i: (0, j)),
        compiler_params=_params(("arbitrary", "arbitrary"), 40),
    )(a, b)


def _adam_math(w, g, m, v):
    m = ADAM_B1 * m + (1.0 - ADAM_B1) * g
    v = ADAM_B2 * v + (1.0 - ADAM_B2) * (g * g)
    m_hat = m / (1.0 - ADAM_B1 ** ADAM_STEP)
    v_hat = v / (1.0 - ADAM_B2 ** ADAM_STEP)
    delta = -ADAM_LR * (m_hat / (jnp.sqrt(v_hat) + ADAM_EPS) + ADAM_WD * w)
    return delta, m, v


def _adamw(w, parts, m, v, name, tr):
    r, cdim = w.shape
    n_parts = parts.shape[0]

    def body(w_ref, p_ref, m_ref, v_ref, g_ref, d_ref, nm_ref, nv_ref):
        g = p_ref[0].astype(F32)
        for k in range(1, n_parts):
            g = g + p_ref[k].astype(F32)
        g_ref[...] = g
        d_ref[...], nm_ref[...], nv_ref[...] = _adam_math(w_ref[...], g, m_ref[...], v_ref[...])

    tile = pl.BlockSpec((tr, cdim), lambda i: (i, 0))
    shp = jax.ShapeDtypeStruct((r, cdim), F32)
    return pl.pallas_call(
        body, name=name, grid=(r // tr,), out_shape=(shp, shp, shp, shp),
        in_specs=[tile, pl.BlockSpec((n_parts, tr, cdim), lambda i: (0, i, 0)), tile, tile],
        out_specs=(tile, tile, tile, tile),
        compiler_params=_params(("arbitrary",), 40),
    )(w, parts, m, v)


def _to_kernel_columns(w):
    pad = jnp.zeros((w.shape[0], LANE - GLA_LOWRANK), w.dtype)
    return jnp.concatenate([w[:, 1024:1536], w[:, 3088:3600], w[:, 0:512], w[:, 1552:2576], w[:, 512:1024],
                            w[:, 2576:3088], w[:, 1536:1552], pad], axis=1)


def _from_kernel_columns(g):
    return jnp.concatenate([g[:, COL_QA:COL_QB], g[:, COL_VA:COL_VB], g[:, 0:512], g[:, COL_LR:COL_LR + GLA_LOWRANK],
                            g[:, COL_QB:COL_VA], g[:, COL_VB:COL_LR], g[:, 512:1024]], axis=1)


def _row(vec, width):
    vec = vec.reshape(1, -1)
    return jnp.pad(vec, ((0, 0), (0, width - vec.shape[1])))


def kernel(x, c, w_ada, b_ada, g_pre, w_in, w_gate_up, b_gate_up, g_gla, g_dil, w_out, g_post, loss_target, m_w_ada, m_b_ada, m_g_pre, m_w_in, m_w_gate_up, m_b_gate_up, m_g_gla, m_g_dil, m_w_out, m_g_post, v_w_ada, v_b_ada, v_g_pre, v_w_in, v_w_gate_up, v_b_gate_up, v_g_gla, v_g_dil, v_w_out, v_g_post):
    px, py, pc = _my_position()
    me = _linear(px, py, pc)
    xs = x[0]
    target = loss_target[0]
    s_len = xs.shape[0]
    assert s_len % (DIL_BLOCK * max(DIL_DILATIONS) * 2) == 0 and xs.shape[1] == D_MODEL

    c_all = _all_gather(jnp.pad(c, ((0, 7), (0, 0))), "gather_c").reshape(N_DEV, 8, D_MODEL)[:, 0]
    mod_part = _mod_fwd(c_all, w_ada)
    mod_all = _all_gather(mod_part.reshape(DEPTH * N_DEV, ADA_SHARD), "gather_mod")
    mod_all = mod_all.reshape(N_DEV, DEPTH, N_DEV, ADA_SHARD)
    mod_mine = lax.dynamic_index_in_dim(mod_all, me, axis=2, keepdims=False)
    mod = jnp.transpose(mod_mine, (1, 0, 2)).reshape(DEPTH, 3 * D_MODEL) + b_ada

    w_in_all = _all_gather(w_in.astype(BF16).reshape(DEPTH * D_MODEL, W_IN_SHARD), "gather_w_in")
    w_in_all = w_in_all.reshape(N_DEV, DEPTH, D_MODEL, W_IN_SHARD)
    w_in_full = jnp.transpose(w_in_all, (1, 2, 0, 3)).reshape(DEPTH, D_MODEL, IN_COLS)
    w_out_all = _all_gather(w_out.astype(BF16).reshape(DEPTH * OUT_SHARD, D_MODEL), "gather_w_out")
    w_out_full = jnp.transpose(w_out_all.reshape(N_DEV, DEPTH, OUT_SHARD, D_MODEL), (1, 0, 2, 3)).reshape(
        DEPTH, D_MODEL, D_MODEL)
    wgu_all = _all_gather(w_gate_up.reshape(DEPTH * GLA_LOWRANK, GU_SHARD), "gather_w_gu")
    wgu_full = jnp.transpose(wgu_all.reshape(N_DEV, DEPTH, GLA_LOWRANK, GU_SHARD), (1, 2, 0, 3)).reshape(
        DEPTH, GLA_LOWRANK, GU_COLS)
    wgu_pad = jnp.pad(wgu_full, ((0, 0), (0, LANE - GLA_LOWRANK), (0, 0))).astype(BF16)

    cos, sin_signed = _rope_tables(s_len)
    g_heads = jnp.concatenate([g_gla, g_dil], axis=1)

    saved = []
    xl = xs
    for l in range(DEPTH):
        shift, scale, gate = (mod[l, k * D_MODEL:(k + 1) * D_MODEL].reshape(1, D_MODEL) for k in range(3))
        w_new = _to_kernel_columns(w_in_full[l])
        pf, pb, h = _prenorm_proj(xl, g_pre[l:l + 1], scale, shift, w_new)
        o_a, states = _gla_fwd(pf, pb, wgu_pad[l], b_gate_up[l:l + 1])
        o_b, lse = _dil_fwd(pf, pb, cos, sin_signed)
        x_next, y, u = _post_fwd(o_a, o_b, pf, g_heads[l:l + 1], w_out_full[l], xl, gate, g_post[l:l + 1])
        saved.append((xl, scale, gate, w_new, pf, pb, h, o_a, states, o_b, lse, y, u))
        xl = x_next

    dx, loss_part = _loss_grad(xl, target)

    small_rows = []
    gw_in, gw_out = [None] * DEPTH, [None] * DEPTH
    for l in reversed(range(DEPTH)):
        x_in, scale, gate, w_new, pf, pb, h, o_a, states, o_b, lse, y, u = saved[l]
        du, do, dz, sums_post = _post_bwd(dx, u, gate, g_post[l:l + 1], w_out_full[l], o_a, o_b, pf, g_heads[l:l + 1])
        gw_out[l] = _matmul_tn(y, du, "grad_w_out", 512)
        dq_a, dk_a, dv_a, dlr2, dwgu, dbgu = _gla_bwd(pf, pb, wgu_pad[l], b_gate_up[l:l + 1], states, do)
        dq_b, dk_b, dv_b = _dil_bwd(pf, pb, cos, sin_signed, do, o_b, lse)
        dlr = (dlr2[0] + dlr2[1]).astype(BF16)
        pieces = (dz, dq_a, dk_a, dq_b, dk_b, dv_a, dv_b, dlr)
        dx, sums_in = _in_bwd(pieces, w_new, x_in, dx, g_pre[l:l + 1], scale)
        blocks = {1024: 512, 256: 256, 512: 512, 128: 128}
        g_new = jnp.concatenate([_matmul_tn(h, p, "grad_w_in_%d" % p.shape[1], blocks[p.shape[1]]) for p in pieces],
                                axis=1)
        gw_in[l] = _from_kernel_columns(g_new)
        dmod = jnp.concatenate([sums_in[0], sums_in[1], sums_post[0]])
        vecs = jnp.concatenate([sums_in[2], sums_post[1], sums_post[2], dbgu[0]])
        small_rows[0:0] = [_row(dmod, 4096), _row(vecs, 4096), _row(dwgu[:GLA_LOWRANK], 4096)]
    grad_x = dx[None]

    gin = jnp.stack(gw_in).astype(BF16).reshape(DEPTH, D_MODEL, N_DEV, W_IN_SHARD)
    gin = jnp.transpose(gin, (2, 0, 1, 3)).reshape(N_DEV * DEPTH * D_MODEL, W_IN_SHARD)
    gin_parts = _all_to_all(gin, "exchange_grad_w_in").reshape(N_DEV, DEPTH * D_MODEL, W_IN_SHARD)
    gout = jnp.stack(gw_out).astype(BF16).reshape(DEPTH, N_DEV, OUT_SHARD, D_MODEL)
    gout = jnp.transpose(gout, (1, 0, 2, 3)).reshape(N_DEV * DEPTH * OUT_SHARD, D_MODEL)
    gout_parts = _all_to_all(gout, "exchange_grad_w_out").reshape(N_DEV, DEPTH * OUT_SHARD, D_MODEL)

    flat = lambda a, rows: a.reshape(rows, a.shape[-1])
    r_in, r_out, r_ada = DEPTH * D_MODEL, DEPTH * OUT_SHARD, DEPTH * D_MODEL
    g_w_in, d_w_in, nm_w_in, nv_w_in = (
        t.reshape(w_in.shape) for t in _adamw(flat(w_in, r_in), gin_parts, flat(m_w_in, r_in), flat(v_w_in, r_in),
                                              "adamw_w_in", 256))
    g_w_out, d_w_out, nm_w_out, nv_w_out = (
        t.reshape(w_out.shape) for t in _adamw(flat(w_out, r_out), gout_parts, flat(m_w_out, r_out),
                                               flat(v_w_out, r_out), "adamw_w_out", 128))

    small_rows += [_row(loss_part[0, 0:1], 4096), jnp.zeros((1, 4096), F32)]
    small = _all_gather(jnp.concatenate(small_rows, axis=0), "gather_small").reshape(N_DEV, 8, 4096)
    dmod_all = jnp.stack([small[:, 0, :3 * D_MODEL], small[:, 3, :3 * D_MODEL]])
    dmod_cols = lax.dynamic_slice_in_dim(dmod_all, me * ADA_SHARD, ADA_SHARD, axis=2)
    gwa = _w_ada_grad(c_all, dmod_cols).reshape(1, r_ada, ADA_SHARD)
    g_w_ada, d_w_ada, nm_w_ada, nv_w_ada = (
        t.reshape(w_ada.shape) for t in _adamw(flat(w_ada, r_ada), gwa, flat(m_w_ada, r_ada), flat(v_w_ada, r_ada),
                                               "adamw_w_ada", 256))

    def small_param(w, m, v, cols, row, name):
        n = w.shape[1]
        parts = jnp.stack([small[:, row, cols:cols + n], small[:, row + 3, cols:cols + n]], axis=1)
        return _adamw(w, parts, m, v, name, DEPTH)

    g_b_ada, d_b_ada, nm_b_ada, nv_b_ada = small_param(b_ada, m_b_ada, v_b_ada, 0, 0, "adamw_b_ada")
    g_g_pre, d_g_pre, nm_g_pre, nv_g_pre = small_param(g_pre, m_g_pre, v_g_pre, 0, 1, "adamw_g_pre")
    g_g_post, d_g_post, nm_g_post, nv_g_post = small_param(g_post, m_g_post, v_g_post, 1024, 1, "adamw_g_post")
    g_g_gla, d_g_gla, nm_g_gla, nv_g_gla = small_param(g_gla, m_g_gla, v_g_gla, 2048, 1, "adamw_g_gla")
    g_g_dil, d_g_dil, nm_g_dil, nv_g_dil = small_param(g_dil, m_g_dil, v_g_dil, 2560, 1, "adamw_g_dil")
    g_b_gu, d_b_gu, nm_b_gu, nv_b_gu = small_param(b_gate_up, m_b_gate_up, v_b_gate_up, 3072, 1, "adamw_b_gate_up")
    gu_parts = jnp.stack([small[:, 2], small[:, 5]], axis=1).reshape(N_DEV, DEPTH, GLA_LOWRANK, GU_COLS)
    gu_parts = lax.dynamic_slice_in_dim(gu_parts, me * GU_SHARD, GU_SHARD, axis=3).reshape(
        N_DEV, DEPTH * GLA_LOWRANK, GU_SHARD)
    r_gu = DEPTH * GLA_LOWRANK
    g_w_gu, d_w_gu, nm_w_gu, nv_w_gu = (
        t.reshape(w_gate_up.shape) for t in _adamw(flat(w_gate_up, r_gu), gu_parts, flat(m_w_gate_up, r_gu),
                                                   flat(v_w_gate_up, r_gu), "adamw_w_gate_up", r_gu))
    loss_parts = jnp.broadcast_to(small[:, 6, 0:1].reshape(N_DEV, 1, 1), (N_DEV, 8, LANE))
    loss = _sum_parts(loss_parts)[0, 0]

    return (loss, grad_x,
            g_w_ada, g_b_ada, g_g_pre, g_w_in, g_w_gu, g_b_gu, g_g_gla, g_g_dil, g_w_out, g_g_post,
            d_w_ada, d_b_ada, d_g_pre, d_w_in, d_w_gu, d_b_gu, d_g_gla, d_g_dil, d_w_out, d_g_post,
            nm_w_ada, nm_b_ada, nm_g_pre, nm_w_in, nm_w_gu, nm_b_gu, nm_g_gla, nm_g_dil, nm_w_out, nm_g_post,
            nv_w_ada, nv_b_ada, nv_g_pre, nv_w_in, nv_w_gu, nv_b_gu, nv_g_gla, nv_g_dil, nv_w_out, nv_g_post)


def _sum_parts(parts):
    n_parts = parts.shape[0]

    def body(p_ref, o_ref):
        acc = p_ref[0]
        for k in range(1, n_parts):
            acc = acc + p_ref[k]
        o_ref[...] = acc

    return pl.pallas_call(body, name="sum_loss", out_shape=jax.ShapeDtypeStruct(parts.shape[1:], F32))(parts)
```

```python
import functools
import math

import jax
import jax.numpy as jnp
from jax import lax
from jax.experimental import pallas as pl
from jax.experimental.pallas import tpu as pltpu

F32 = jnp.float32
BF16 = jnp.bfloat16

N_DEV = 8
D_MODEL = 1024
DEPTH = 2
GLA_HEADS = 4
GLA_DK = 64
GLA_DV = 128
GLA_CHUNK = 64
GLA_TAU = 16.0
GLA_LOWRANK = 16
DIL_HEADS = 4
DIL_HD = 128
DIL_BLOCK = 128
DIL_DILATIONS = (1, 4, 16)
ROPE_THETA = 10000.0
EPS = 1e-6
IN_COLS = 3600
W_IN_SHARD = IN_COLS // N_DEV
ADA_SHARD = 3 * D_MODEL // N_DEV
OUT_SHARD = D_MODEL // N_DEV
GU_COLS = GLA_HEADS * GLA_DK
GU_SHARD = GU_COLS // N_DEV

ADAM_LR = 0.001
ADAM_B1 = 0.9
ADAM_B2 = 0.999
ADAM_EPS = 1e-08
ADAM_WD = 0.01
ADAM_STEP = 10

NP = 3712
COL_Z, COL_QA, COL_KA, COL_QB, COL_KB, COL_VA, COL_VB, COL_LR = 0, 1024, 1280, 1536, 2048, 2560, 3072, 3584
NP_F32 = COL_VA
NP_BF16 = NP - NP_F32
LANE = 128
MASK_VALUE = -1e30

MESH = pl.DeviceIdType.MESH
ANY = pl.BlockSpec(memory_space=pl.ANY)


def _params(sem=None, vmem_mb=None):
    kw = {}
    if sem is not None:
        kw["dimension_semantics"] = sem
    if vmem_mb is not None:
        kw["vmem_limit_bytes"] = vmem_mb * 1024 * 1024
    return pltpu.CompilerParams(**kw)


def _dot(a, b):
    return jnp.dot(a, b, preferred_element_type=F32)


def _dot_nt(a, b):
    return lax.dot_general(a, b, (((1,), (1,)), ((), ())), preferred_element_type=F32)


def _dot_tn(a, b):
    return lax.dot_general(a, b, (((0,), (0,)), ((), ())), preferred_element_type=F32)


def _sigmoid(z):
    return 1.0 / (1.0 + jnp.exp(-z))


def _log_sigmoid(z):
    return jnp.minimum(z, 0.0) - jnp.log(1.0 + jnp.exp(-jnp.abs(z)))


def _my_position():
    return lax.axis_index("x"), lax.axis_index("y"), lax.axis_index("c")


def _linear(px, py, pc):
    return 4 * px + 2 * py + pc


def _all_gather(xs, name):
    m, n = xs.shape

    def body(x_ref, out_ref, send_sems, recv_sems, local_sem):
        x, y, c = _my_position()
        me, sibling = (x, y, c), (x, y, 1 - c)
        chips = [(1 - x, y), (x, 1 - y), (1 - x, 1 - y)]

        def rows(px, py, pc):
            return out_ref.at[pl.ds(_linear(px, py, pc) * m, m), :]

        def copy(k, block, to, src=None):
            return pltpu.make_async_remote_copy(
                src_ref=rows(*block) if src is None else src, dst_ref=rows(*block),
                send_sem=send_sems.at[k], recv_sem=recv_sems.at[k], device_id=to, device_id_type=MESH)

        mine = pltpu.make_async_copy(x_ref, rows(*me), local_sem)
        mine.start()
        first = [copy(0, me, sibling, src=x_ref)]
        first += [copy(1 + j, me, (*chip, c), src=x_ref) for j, chip in enumerate(chips)]
        for cp in first:
            cp.start()
        passed = [copy(4 + j, (*chip, c), sibling) for j, chip in enumerate(chips)]
        for j, chip in enumerate(chips):
            copy(1 + j, (*chip, c), me).wait_recv()
            passed[j].start()
        copy(0, sibling, me).wait_recv()
        for j, chip in enumerate(chips):
            copy(4 + j, (*chip, 1 - c), me).wait_recv()
        for cp in first + passed:
            cp.wait_send()
        mine.wait()

    return pl.pallas_call(
        body, name=name, out_shape=jax.ShapeDtypeStruct((N_DEV * m, n), xs.dtype),
        in_specs=[ANY], out_specs=ANY,
        scratch_shapes=[pltpu.SemaphoreType.DMA((7,)), pltpu.SemaphoreType.DMA((7,)), pltpu.SemaphoreType.DMA(())],
    )(xs)


def _all_to_all(xs, name):
    m8, n = xs.shape
    m = m8 // N_DEV

    def body(x_ref, out_ref, send_sems, recv_sems, local_sem):
        x, y, c = _my_position()
        me = _linear(x, y, c)

        def rows(ref, idx):
            return ref.at[pl.ds(idx * m, m), :]

        local = pltpu.make_async_copy(rows(x_ref, me), rows(out_ref, me), local_sem)
        local.start()
        peers = []
        for j in range(1, N_DEV):
            px = 1 - x if j & 4 else x
            py = 1 - y if j & 2 else y
            pc = 1 - c if j & 1 else c
            peers.append((px, py, pc))
        sends = []
        for j, peer in enumerate(peers):
            cp = pltpu.make_async_remote_copy(
                src_ref=rows(x_ref, _linear(*peer)), dst_ref=rows(out_ref, me),
                send_sem=send_sems.at[j], recv_sem=recv_sems.at[j], device_id=peer, device_id_type=MESH)
            cp.start()
            sends.append(cp)
        for j, peer in enumerate(peers):
            pltpu.make_async_remote_copy(
                src_ref=rows(x_ref, _linear(*peer)), dst_ref=rows(out_ref, _linear(*peer)),
                send_sem=send_sems.at[j], recv_sem=recv_sems.at[j], device_id=peer, device_id_type=MESH).wait_recv()
        for cp in sends:
            cp.wait_send()
        local.wait()

    return pl.pallas_call(
        body, name=name, out_shape=jax.ShapeDtypeStruct((m8, n), xs.dtype),
        in_specs=[ANY], out_specs=ANY,
        scratch_shapes=[pltpu.SemaphoreType.DMA((7,)), pltpu.SemaphoreType.DMA((7,)), pltpu.SemaphoreType.DMA(())],
    )(xs)


def _mod_fwd(c_all, w_ada):
    def body(c_ref, w_ref, o_ref):
        cv = c_ref[...]
        sc = cv * _sigmoid(cv)
        o_ref[0] = _dot(sc.astype(BF16), w_ref[0].astype(BF16))

    return pl.pallas_call(
        body, name="mod_fwd", grid=(DEPTH,),
        out_shape=jax.ShapeDtypeStruct((DEPTH, N_DEV, ADA_SHARD), F32),
        in_specs=[pl.BlockSpec((N_DEV, D_MODEL), lambda l: (0, 0)),
                  pl.BlockSpec((1, D_MODEL, ADA_SHARD), lambda l: (l, 0, 0))],
        out_specs=pl.BlockSpec((1, N_DEV, ADA_SHARD), lambda l: (l, 0, 0)),
        compiler_params=_params(("arbitrary",)),
    )(c_all, w_ada)


def _w_ada_grad(c_all, dmod_cols):
    def body(c_ref, d_ref, o_ref):
        cv = c_ref[...]
        sc = cv * _sigmoid(cv)
        o_ref[0] = lax.dot_general(sc, d_ref[0], (((0,), (0,)), ((), ())), precision=lax.Precision.HIGHEST,
                                   preferred_element_type=F32)

    return pl.pallas_call(
        body, name="w_ada_grad", grid=(DEPTH,),
        out_shape=jax.ShapeDtypeStruct((DEPTH, D_MODEL, ADA_SHARD), F32),
        in_specs=[pl.BlockSpec((N_DEV, D_MODEL), lambda l: (0, 0)),
                  pl.BlockSpec((1, N_DEV, ADA_SHARD), lambda l: (l, 0, 0))],
        out_specs=pl.BlockSpec((1, D_MODEL, ADA_SHARD), lambda l: (l, 0, 0)),
        compiler_params=_params(("arbitrary",)),
    )(c_all, dmod_cols)


def _prenorm_proj(x, g_pre, scale, shift, w_new, ts=256):
    s_len = x.shape[0]

    def body(x_ref, g_ref, sc_ref, sh_ref, w_ref, pf_ref, pb_ref, h_ref):
        xv = x_ref[...]
        rstd = lax.rsqrt(jnp.mean(xv * xv, axis=-1, keepdims=True) + EPS)
        h = (xv * rstd * g_ref[...]) * (1.0 + sc_ref[...]) + sh_ref[...]
        hb = h.astype(BF16)
        h_ref[...] = hb
        for j in range(0, NP, 512):
            w = min(512, NP - j)
            acc = _dot(hb, w_ref[:, j:j + w])
            if j < NP_F32:
                pf_ref[:, j:j + w] = acc
            else:
                pb_ref[:, j - NP_F32:j - NP_F32 + w] = acc.astype(BF16)

    vec = pl.BlockSpec((1, D_MODEL), lambda i: (0, 0))
    return pl.pallas_call(
        body, name="prenorm_proj", grid=(s_len // ts,),
        out_shape=(jax.ShapeDtypeStruct((s_len, NP_F32), F32), jax.ShapeDtypeStruct((s_len, NP_BF16), BF16),
                   jax.ShapeDtypeStruct((s_len, D_MODEL), BF16)),
        in_specs=[pl.BlockSpec((ts, D_MODEL), lambda i: (i, 0)), vec, vec, vec,
                  pl.BlockSpec((D_MODEL, NP), lambda i: (0, 0))],
        out_specs=(pl.BlockSpec((ts, NP_F32), lambda i: (i, 0)), pl.BlockSpec((ts, NP_BF16), lambda i: (i, 0)),
                   pl.BlockSpec((ts, D_MODEL), lambda i: (i, 0))),
        compiler_params=_params(("arbitrary",), 48),
    )(x, g_pre, scale, shift, w_new)


GLA_GROUP = 4


def _gla_group_rows(t):
    return [pl.ds(pl.multiple_of((t * GLA_GROUP + j) * GLA_CHUNK, GLA_CHUNK), GLA_CHUNK) for j in range(GLA_GROUP)]


def _gla_chunks_common(q_ref, k_ref, lr_ref, wgu_ref, bgu_ref, rows_list):
    c = GLA_CHUNK
    ri = lax.broadcasted_iota(jnp.int32, (c, c), 0)
    ci = lax.broadcasted_iota(jnp.int32, (c, c), 1)
    tril = (ri >= ci).astype(F32)
    zs = [_dot(lr_ref[rows, :], wgu_ref[...]) + bgu_ref[...] for rows in rows_list]
    las = [_log_sigmoid(z) * (1.0 / GLA_TAU) for z in zs]
    bs = [jnp.dot(tril, la, precision=lax.Precision.HIGHEST, preferred_element_type=F32) for la in las]
    out = []
    for rows, z, b in zip(rows_list, zs, bs):
        q = q_ref[rows, :] * (GLA_DK ** -0.5)
        k = k_ref[rows, :]
        bl = b[c - 1:c, :]
        out.append(dict(z=z, b=b, bl=bl, qe=q * jnp.exp(b), ke=k * jnp.exp(-b), kend=k * jnp.exp(bl - b),
                        dec=jnp.exp(bl)))
    return out, ri, ci


def _head_lane_mask(hh):
    return (lax.broadcasted_iota(jnp.int32, (1, LANE), 1) // GLA_DK) == hh


def _state_block_mask():
    r = lax.broadcasted_iota(jnp.int32, (2 * GLA_DV, LANE), 0) // GLA_DV
    cc = lax.broadcasted_iota(jnp.int32, (2 * GLA_DV, LANE), 1) // GLA_DK
    return r == cc


def _gla_fwd(pf, pb, wgu, bgu):
    s_len = pf.shape[0]
    nc = s_len // GLA_CHUNK

    def body(q_ref, k_ref, v_ref, lr_ref, wgu_ref, bgu_ref, o_ref, st_ref, qe_s, cs_s, dec_s):
        bd = _state_block_mask()

        def local(t, carry):
            rows_list = _gla_group_rows(t)
            cm, ri, ci = _gla_chunks_common(q_ref, k_ref, lr_ref, wgu_ref, bgu_ref, rows_list)
            vs = [v_ref[rows, :] for rows in rows_list]
            kebs = [c["ke"].astype(BF16) for c in cm]
            a = [[jnp.where(ri >= ci, _dot_nt(jnp.where(_head_lane_mask(hh), c["qe"], 0.0).astype(BF16), keb), 0.0)
                  .astype(BF16) for hh in range(2)] for c, keb in zip(cm, kebs)]
            oi = [[_dot(ah[hh], v[:, hh * GLA_DV:(hh + 1) * GLA_DV]) for hh in range(2)] for ah, v in zip(a, vs)]
            cs = [jnp.where(bd, _dot_tn(v, c["kend"].astype(BF16)), 0.0) for c, v in zip(cm, vs)]
            for j, (rows, c) in enumerate(zip(rows_list, cm)):
                n = t * GLA_GROUP + j
                o_ref[rows, :] = jnp.concatenate(oi[j], axis=1)
                qe_s[rows, :] = c["qe"].astype(BF16)
                cs_s[n] = cs[j]
                dec_s[n] = jnp.broadcast_to(c["dec"], (8, LANE))
            return carry

        lax.fori_loop(0, nc // GLA_GROUP, local, 0)

        def scan(n, st):
            st_ref[0, n] = st.astype(BF16)
            return dec_s[n][0:1, :] * st + cs_s[n]

        lax.fori_loop(0, nc, scan, jnp.zeros((2 * GLA_DV, LANE), F32))

        def inter(t, carry):
            rows_list = _gla_group_rows(t)
            add = [_dot_nt(qe_s[rows, :], st_ref[0, t * GLA_GROUP + j]) for j, rows in enumerate(rows_list)]
            for rows, av in zip(rows_list, add):
                o_ref[rows, :] = o_ref[rows, :] + av
            return carry

        lax.fori_loop(0, nc // GLA_GROUP, inter, 0)

    return pl.pallas_call(
        body, name="gla_fwd", grid=(2,),
        out_shape=(jax.ShapeDtypeStruct((s_len, GLA_HEADS * GLA_DV), F32),
                   jax.ShapeDtypeStruct((2, nc, 2 * GLA_DV, LANE), BF16)),
        in_specs=[pl.BlockSpec((s_len, LANE), lambda g: (0, COL_QA // LANE + g)),
                  pl.BlockSpec((s_len, LANE), lambda g: (0, COL_KA // LANE + g)),
                  pl.BlockSpec((s_len, 2 * GLA_DV), lambda g: (0, (COL_VA - NP_F32) // (2 * GLA_DV) + g)),
                  pl.BlockSpec((s_len, LANE), lambda g: (0, (COL_LR - NP_F32) // LANE)),
                  pl.BlockSpec((LANE, LANE), lambda g: (0, g)),
                  pl.BlockSpec((1, LANE), lambda g: (0, g))],
        out_specs=(pl.BlockSpec((s_len, 2 * GLA_DV), lambda g: (0, g)),
                   pl.BlockSpec((1, nc, 2 * GLA_DV, LANE), lambda g: (g, 0, 0, 0))),
        scratch_shapes=[pltpu.VMEM((s_len, LANE), BF16), pltpu.VMEM((nc, 2 * GLA_DV, LANE), F32),
                        pltpu.VMEM((nc, 8, LANE), F32)],
        compiler_params=_params(("arbitrary",), 56),
    )(pf, pf, pb, pb, wgu, bgu)


def _rope_tables(s_len):
    inv_freq = ROPE_THETA ** (-jnp.arange(0, DIL_HD, 2, dtype=F32) / DIL_HD)
    ang = jnp.arange(s_len, dtype=F32)[:, None] * inv_freq[None, :]
    cos, sin = jnp.cos(ang), jnp.sin(ang)
    return jnp.concatenate([cos, cos], axis=1), jnp.concatenate([-sin, sin], axis=1)


def _rope(xv, cos, sin_signed):
    return xv * cos + pltpu.roll(xv, DIL_HD // 2, 1) * sin_signed


DIL_GROUP = 4


def _dil_pair_block(i, half, d, nblk):
    nb = nblk // d
    j = i + half * (nblk // DIL_GROUP)
    if nb >= 2 * DIL_GROUP:
        r, n = j % d, j // d
    else:
        r, n = j // nb, j % nb
    kb = jnp.maximum(n - 1, 0)
    qs = r + d * DIL_BLOCK * n
    ks = r + d * DIL_BLOCK * kb
    return qs, ks, jnp.minimum(n, 1)


def _dil_fill_bias(bias):
    qi = lax.broadcasted_iota(jnp.int32, (DIL_BLOCK, 2 * DIL_BLOCK), 0)
    kj = lax.broadcasted_iota(jnp.int32, (DIL_BLOCK, 2 * DIL_BLOCK), 1)
    for sel in range(2):
        dist = qi - kj + DIL_BLOCK * sel
        bias[sel] = jnp.where((dist >= 0) & (dist <= DIL_BLOCK), 0.0, MASK_VALUE)


def _strided(start, size, d):
    return pl.ds(start, size) if d == 1 else pl.ds(start, size, stride=d)


def _dil_fwd(pf, pb, cos, sin_signed):
    s_len = pf.shape[0]
    nblk = s_len // DIL_BLOCK
    prep_rows = 256
    scale = DIL_HD ** -0.5

    def body(q_ref, k_ref, v_ref, cos_ref, sin_ref, o_ref, lse_ref, qf, kf, vf, o0, o1, o2, l0, l1, l2, bias):
        _dil_fill_bias(bias)

        def prep(t, carry):
            rows = pl.ds(pl.multiple_of(t * prep_rows, prep_rows), prep_rows)
            cs, sn = cos_ref[rows, :], sin_ref[rows, :]
            qf[rows, :] = _rope(q_ref[rows, :], cs, sn)
            kf[rows, :] = _rope(k_ref[rows, :], cs, sn)
            vf[rows, :] = v_ref[rows, :].astype(F32)
            return carry

        lax.fori_loop(0, s_len // prep_rows, prep, 0)
        ones = jnp.ones((2 * DIL_BLOCK, DIL_HD), BF16)

        for d, o_p, l_p in zip(DIL_DILATIONS, (o0, o1, o2), (l0, l1, l2)):
            def pair(i, carry, d=d, o_p=o_p, l_p=l_p):
                idx = [_dil_pair_block(i, half, d, nblk) for half in range(DIL_GROUP)]
                ld = [(qf[_strided(qs, DIL_BLOCK, d), :].astype(BF16),
                       kf[_strided(ks, 2 * DIL_BLOCK, d), :].astype(BF16),
                       vf[_strided(ks, 2 * DIL_BLOCK, d), :].astype(BF16)) for qs, ks, _ in idx]
                s = [_dot_nt(qb, kk) * scale + bias[sel] for (qb, kk, _), (_, _, sel) in zip(ld, idx)]
                m = [jnp.max(sv, axis=-1, keepdims=True) for sv in s]
                p = [jnp.exp(sv - mv) for sv, mv in zip(s, m)]
                hi = [pv.astype(BF16) for pv in p]
                lo = [(pv - hv.astype(F32)).astype(BF16) for pv, hv in zip(p, hi)]
                r = [_dot(hv, jnp.concatenate([vv, ones], axis=1)) for hv, (_, _, vv) in zip(hi, ld)]
                r2 = [_dot(lv, ones) for lv in lo]
                for rv, r2v, mv, (qs, _, _) in zip(r, r2, m, idx):
                    den = rv[:, DIL_HD:] + r2v
                    o_p[_strided(qs, DIL_BLOCK, d), :] = rv[:, :DIL_HD] / den
                    l_p[_strided(qs, DIL_BLOCK, d), :] = mv + jnp.log(den)
                return carry

            lax.fori_loop(0, nblk // DIL_GROUP, pair, 0)

        def comb(t, carry):
            rows = pl.ds(pl.multiple_of(t * prep_rows, prep_rows), prep_rows)
            a0, a1, a2 = l0[rows, :], l1[rows, :], l2[rows, :]
            m = jnp.maximum(jnp.maximum(a0, a1), a2)
            e0, e1, e2 = jnp.exp(a0 - m), jnp.exp(a1 - m), jnp.exp(a2 - m)
            tot = e0 + e1 + e2
            o_ref[rows, :] = (e0 * o0[rows, :] + e1 * o1[rows, :] + e2 * o2[rows, :]) / tot
            lse_ref[rows, :] = m + jnp.log(tot)
            return carry

        lax.fori_loop(0, s_len // prep_rows, comb, 0)

    head = lambda base: pl.BlockSpec((s_len, DIL_HD), lambda h: (0, base // DIL_HD + h))
    table = pl.BlockSpec((s_len, DIL_HD), lambda h: (0, 0))
    out = pl.BlockSpec((s_len, DIL_HD), lambda h: (0, h))
    return pl.pallas_call(
        body, name="dil_fwd", grid=(DIL_HEADS,),
        out_shape=(jax.ShapeDtypeStruct((s_len, DIL_HEADS * DIL_HD), F32),
                   jax.ShapeDtypeStruct((s_len, DIL_HEADS * DIL_HD), F32)),
        in_specs=[head(COL_QB), head(COL_KB), head(COL_VB - NP_F32), table, table],
        out_specs=(out, out),
        scratch_shapes=[pltpu.VMEM((s_len, DIL_HD), F32) for _ in range(9)]
        + [pltpu.VMEM((2, DIL_BLOCK, 2 * DIL_BLOCK), F32)],
        compiler_params=_params(("arbitrary",), 56),
    )(pf, pf, pb, cos, sin_signed)


def _silu_and_grad(z):
    sg = _sigmoid(z)
    return z * sg, sg * (1.0 + z * (1.0 - sg))


def _post_fwd(o_a, o_b, pf, g_heads, w_out, x, gate, g_post, ts=256):
    s_len = x.shape[0]
    half = GLA_HEADS * GLA_DV

    def body(oa_ref, ob_ref, z_ref, gh_ref, w_ref, x_ref, gate_ref, gp_ref, xo_ref, y_ref, u_ref):
        for src, base in ((oa_ref, 0), (ob_ref, half)):
            for hh in range(4):
                lo = hh * LANE
                og = src[:, lo:lo + LANE]
                on = og * lax.rsqrt(jnp.mean(og * og, axis=-1, keepdims=True) + EPS)
                zg = z_ref[:, base + lo:base + lo + LANE].astype(F32)
                y_ref[:, base + lo:base + lo + LANE] = (on * gh_ref[:, base + lo:base + lo + LANE]
                                                        * (zg * _sigmoid(zg))).astype(BF16)
        u = _dot(y_ref[...], w_ref[...])
        u_ref[...] = u.astype(BF16)
        rstd = lax.rsqrt(jnp.mean(u * u, axis=-1, keepdims=True) + EPS)
        xo_ref[...] = x_ref[...] + gate_ref[...] * (u * rstd * gp_ref[...])

    vec = pl.BlockSpec((1, D_MODEL), lambda i: (0, 0))
    tile = pl.BlockSpec((ts, D_MODEL), lambda i: (i, 0))
    halft = pl.BlockSpec((ts, half), lambda i: (i, 0))
    return pl.pallas_call(
        body, name="post_fwd", grid=(s_len // ts,),
        out_shape=(jax.ShapeDtypeStruct((s_len, D_MODEL), F32), jax.ShapeDtypeStruct((s_len, D_MODEL), BF16),
                   jax.ShapeDtypeStruct((s_len, D_MODEL), BF16)),
        in_specs=[halft, halft, tile, vec, pl.BlockSpec((D_MODEL, D_MODEL), lambda i: (0, 0)), tile, vec, vec],
        out_specs=(tile, tile, tile),
        compiler_params=_params(("arbitrary",), 40),
    )(o_a, o_b, pf, g_heads, w_out, x, gate, g_post)


def _loss_grad(y, target, ts=512):
    s_len = y.shape[0]

    def body(y_ref, t_ref, dy_ref, loss_ref):
        @pl.when(pl.program_id(0) == 0)
        def _():
            loss_ref[...] = jnp.zeros_like(loss_ref)

        e = y_ref[...] - t_ref[...]
        dy_ref[...] = e * (1.0 / D_MODEL)
        loss_ref[...] += 0.5 * jnp.sum(jnp.mean(e * e, axis=-1, keepdims=True))

    tile = pl.BlockSpec((ts, D_MODEL), lambda i: (i, 0))
    return pl.pallas_call(
        body, name="loss_grad", grid=(s_len // ts,),
        out_shape=(jax.ShapeDtypeStruct((s_len, D_MODEL), F32), jax.ShapeDtypeStruct((8, LANE), F32)),
        in_specs=[tile, tile], out_specs=(tile, pl.BlockSpec((8, LANE), lambda i: (0, 0))),
        compiler_params=_params(("arbitrary",)),
    )(y, target)


def _post_bwd(dxo, u, gate, g_post, w_out, o_a, o_b, pf, g_heads, ts=256):
    s_len = dxo.shape[0]
    half = GLA_HEADS * GLA_DV

    def body(dx_ref, u_ref, gate_ref, gp_ref, w_ref, oa_ref, ob_ref, z_ref, gh_ref, du_ref, do_ref, dz_ref, sums_ref):
        @pl.when(pl.program_id(0) == 0)
        def _():
            sums_ref[...] = jnp.zeros_like(sums_ref)

        dx = dx_ref[...]
        u = u_ref[...].astype(F32)
        rstd = lax.rsqrt(jnp.mean(u * u, axis=-1, keepdims=True) + EPS)
        un = u * rstd
        sums_ref[0:1, :] += jnp.sum(dx * (un * gp_ref[...]), axis=0, keepdims=True)
        drn = dx * gate_ref[...]
        sums_ref[1:2, :] += jnp.sum(drn * un, axis=0, keepdims=True)
        dun = drn * gp_ref[...]
        du = rstd * (dun - un * jnp.mean(dun * un, axis=-1, keepdims=True))
        dub = du.astype(BF16)
        du_ref[...] = dub
        dy = _dot_nt(dub, w_ref[...])
        for src, base in ((oa_ref, 0), (ob_ref, half)):
            for hh in range(4):
                lo = base + hh * LANE
                og = src[:, hh * LANE:(hh + 1) * LANE]
                rs = lax.rsqrt(jnp.mean(og * og, axis=-1, keepdims=True) + EPS)
                on = og * rs
                zg = z_ref[:, lo:lo + LANE].astype(F32)
                sz, dsz = _silu_and_grad(zg)
                gg = gh_ref[:, lo:lo + LANE]
                dyg = dy[:, lo:lo + LANE]
                sums_ref[2:3, lo:lo + LANE] += jnp.sum(dyg * sz * on, axis=0, keepdims=True)
                dz_ref[:, lo:lo + LANE] = (dyg * on * gg * dsz).astype(BF16)
                don = dyg * gg * sz
                do_ref[:, lo:lo + LANE] = (rs * (don - on * jnp.mean(don * on, axis=-1, keepdims=True))).astype(BF16)

    vec = pl.BlockSpec((1, D_MODEL), lambda i: (0, 0))
    tile = pl.BlockSpec((ts, D_MODEL), lambda i: (i, 0))
    halft = pl.BlockSpec((ts, half), lambda i: (i, 0))
    return pl.pallas_call(
        body, name="post_bwd", grid=(s_len // ts,),
        out_shape=(jax.ShapeDtypeStruct((s_len, D_MODEL), BF16), jax.ShapeDtypeStruct((s_len, D_MODEL), BF16),
                   jax.ShapeDtypeStruct((s_len, D_MODEL), BF16), jax.ShapeDtypeStruct((8, D_MODEL), F32)),
        in_specs=[tile, tile, vec, vec, pl.BlockSpec((D_MODEL, D_MODEL), lambda i: (0, 0)), halft, halft, tile, vec],
        out_specs=(tile, tile, tile, pl.BlockSpec((8, D_MODEL), lambda i: (0, 0))),
        compiler_params=_params(("arbitrary",), 40),
    )(dxo, u, gate, g_post, w_out, o_a, o_b, pf, g_heads)


def _gla_bwd(pf, pb, wgu, bgu, states, do):
    s_len = pf.shape[0]
    nc = s_len // GLA_CHUNK
    c = GLA_CHUNK

    def body(q_ref, k_ref, v_ref, lr_ref, wgu_ref, bgu_ref, st_ref, do_ref,
             dq_ref, dk_ref, dv_ref, dlr_ref, dwgu_ref, dbgu_ref, ds_s, dec_s, dw_acc, db_acc):
        dw_acc[...] = jnp.zeros_like(dw_acc)
        db_acc[...] = jnp.zeros_like(db_acc)
        bd = _state_block_mask()
        last_row = lax.broadcasted_iota(jnp.int32, (c, LANE), 0) == c - 1

        def local(t, carry):
            rows_list = _gla_group_rows(t)
            cm, _, _ = _gla_chunks_common(q_ref, k_ref, lr_ref, wgu_ref, bgu_ref, rows_list)
            loc = [jnp.where(bd, _dot_tn(do_ref[rows, :], cc["qe"].astype(BF16)), 0.0)
                   for rows, cc in zip(rows_list, cm)]
            for j, cc in enumerate(cm):
                ds_s[t * GLA_GROUP + j] = loc[j]
                dec_s[t * GLA_GROUP + j] = jnp.broadcast_to(cc["dec"], (8, LANE))
            return carry

        lax.fori_loop(0, nc // GLA_GROUP, local, 0)

        def scan(t, dst):
            n = nc - 1 - t
            loc = ds_s[n]
            ds_s[n] = dst
            return dec_s[n][0:1, :] * dst + loc

        lax.fori_loop(0, nc, scan, jnp.zeros((2 * GLA_DV, LANE), F32))

        def rest(t, carry):
            rows_list = _gla_group_rows(t)
            cm, ri, ci = _gla_chunks_common(q_ref, k_ref, lr_ref, wgu_ref, bgu_ref, rows_list)
            ns = [t * GLA_GROUP + j for j in range(GLA_GROUP)]
            vs = [v_ref[rows, :] for rows in rows_list]
            dobs = [do_ref[rows, :] for rows in rows_list]
            stbs = [st_ref[0, n] for n in ns]
            dsts = [ds_s[n] for n in ns]
            dstbs = [d.astype(BF16) for d in dsts]
            qebs = [cc["qe"].astype(BF16) for cc in cm]
            kebs = [cc["ke"].astype(BF16) for cc in cm]
            kendbs = [cc["kend"].astype(BF16) for cc in cm]
            hms = [_head_lane_mask(hh) for hh in range(2)]
            qehs = [[jnp.where(hm, cc["qe"], 0.0).astype(BF16) for hm in hms] for cc in cm]
            kehs = [[jnp.where(hm, cc["ke"], 0.0).astype(BF16) for hm in hms] for cc in cm]
            heads = lambda x: [x[:, hh * GLA_DV:(hh + 1) * GLA_DV] for hh in range(2)]
            vhs, dohs = [heads(v) for v in vs], [heads(d) for d in dobs]

            dqe0 = [_dot(dob, stb) for dob, stb in zip(dobs, stbs)]
            dkend = [_dot(v, dstb) for v, dstb in zip(vs, dstbs)]
            dv0 = [_dot_nt(kb, dstb) for kb, dstb in zip(kendbs, dstbs)]
            a_t = [[jnp.where(ci >= ri, _dot_nt(kehs[j][hh], qebs[j]), 0.0).astype(BF16) for hh in range(2)]
                   for j in range(GLA_GROUP)]
            da = [[jnp.where(ri >= ci, _dot_nt(dohs[j][hh], vhs[j][hh]), 0.0).astype(BF16) for hh in range(2)]
                  for j in range(GLA_GROUP)]
            da_t = [[jnp.where(ci >= ri, _dot_nt(vhs[j][hh], dohs[j][hh]), 0.0).astype(BF16) for hh in range(2)]
                    for j in range(GLA_GROUP)]
            dv1 = [[_dot(a_t[j][hh], dohs[j][hh]) for hh in range(2)] for j in range(GLA_GROUP)]
            dqe1 = [[_dot(da[j][hh], kebs[j]) for hh in range(2)] for j in range(GLA_GROUP)]
            dke1 = [[_dot(da_t[j][hh], qehs[j][hh]) for hh in range(2)] for j in range(GLA_GROUP)]

            dbs, dzs = [], []
            for j, (rows, cc) in enumerate(zip(rows_list, cm)):
                qe, ke, kend, b, bl = cc["qe"], cc["ke"], cc["kend"], cc["b"], cc["bl"]
                dqe = dqe0[j] + jnp.where(hms[0], dqe1[j][0], 0.0) + jnp.where(hms[1], dqe1[j][1], 0.0)
                dke = jnp.where(hms[0], dke1[j][0], 0.0) + jnp.where(hms[1], dke1[j][1], 0.0)
                dv_ref[rows, :] = (dv0[j] + jnp.concatenate(dv1[j], axis=1)).astype(BF16)
                dq_ref[rows, :] = (dqe * jnp.exp(b) * (GLA_DK ** -0.5)).astype(BF16)
                dk_ref[rows, :] = (dke * jnp.exp(-b) + dkend[j] * jnp.exp(bl - b)).astype(BF16)
                ddec = jnp.sum(dsts[j] * stbs[j].astype(F32), axis=0, keepdims=True)
                dbl = jnp.sum(dkend[j] * kend, axis=0, keepdims=True) + ddec * cc["dec"]
                dbs.append(dqe * qe - dke * ke - dkend[j] * kend + jnp.where(last_row, dbl, 0.0))
            triu = (ci >= ri).astype(F32)
            dlas = [jnp.dot(triu, db, precision=lax.Precision.HIGHEST, preferred_element_type=F32) for db in dbs]
            dzs = [dla * (1.0 / GLA_TAU) * _sigmoid(-cc["z"]) for dla, cc in zip(dlas, cm)]
            dzbs = [dz.astype(BF16) for dz in dzs]
            dlrs = [_dot_nt(dzb, wgu_ref[...]) for dzb in dzbs]
            dws = [_dot_tn(lr_ref[rows, :], dzb) for rows, dzb in zip(rows_list, dzbs)]
            for rows, dlr in zip(rows_list, dlrs):
                dlr_ref[0, rows, :] = dlr
            dw_acc[...] += functools.reduce(lambda x, y: x + y, dws)
            db_acc[0:1, :] += jnp.sum(functools.reduce(lambda x, y: x + y, dzs), axis=0, keepdims=True)
            return carry

        lax.fori_loop(0, nc // GLA_GROUP, rest, 0)
        dwgu_ref[...] = dw_acc[...]
        dbgu_ref[...] = db_acc[...]

    pair = pl.BlockSpec((s_len, LANE), lambda g: (0, g))
    return pl.pallas_call(
        body, name="gla_bwd", grid=(2,),
        out_shape=(jax.ShapeDtypeStruct((s_len, GU_COLS), BF16), jax.ShapeDtypeStruct((s_len, GU_COLS), BF16),
                   jax.ShapeDtypeStruct((s_len, GLA_HEADS * GLA_DV), BF16),
                   jax.ShapeDtypeStruct((2, s_len, LANE), F32),
                   jax.ShapeDtypeStruct((LANE, GU_COLS), F32), jax.ShapeDtypeStruct((8, GU_COLS), F32)),
        in_specs=[pl.BlockSpec((s_len, LANE), lambda g: (0, COL_QA // LANE + g)),
                  pl.BlockSpec((s_len, LANE), lambda g: (0, COL_KA // LANE + g)),
                  pl.BlockSpec((s_len, 2 * GLA_DV), lambda g: (0, (COL_VA - NP_F32) // (2 * GLA_DV) + g)),
                  pl.BlockSpec((s_len, LANE), lambda g: (0, (COL_LR - NP_F32) // LANE)),
                  pl.BlockSpec((LANE, LANE), lambda g: (0, g)),
                  pl.BlockSpec((1, LANE), lambda g: (0, g)),
                  pl.BlockSpec((1, nc, 2 * GLA_DV, LANE), lambda g: (g, 0, 0, 0)),
                  pl.BlockSpec((s_len, 2 * GLA_DV), lambda g: (0, g))],
        out_specs=(pair, pair, pl.BlockSpec((s_len, 2 * GLA_DV), lambda g: (0, g)),
                   pl.BlockSpec((1, s_len, LANE), lambda g: (g, 0, 0)),
                   pl.BlockSpec((LANE, LANE), lambda g: (0, g)), pl.BlockSpec((8, LANE), lambda g: (0, g))),
        scratch_shapes=[pltpu.VMEM((nc, 2 * GLA_DV, LANE), F32), pltpu.VMEM((nc, 8, LANE), F32),
                        pltpu.VMEM((LANE, LANE), F32), pltpu.VMEM((8, LANE), F32)],
        compiler_params=_params(("arbitrary",), 56),
    )(pf, pf, pb, pb, wgu, bgu, states, do)


def _dil_bwd(pf, pb, cos, sin_signed, do, o_b, lse):
    s_len = pf.shape[0]
    nblk = s_len // DIL_BLOCK
    prep_rows = 256
    scale = DIL_HD ** -0.5

    def body(q_ref, k_ref, v_ref, cos_ref, sin_ref, do_ref, o_ref, lse_ref, dq_ref, dk_ref, dv_ref,
             qf, kf, vf, dof, dl, dqa, dka, dva, bias):
        _dil_fill_bias(bias)

        def prep(t, carry):
            rows = pl.ds(pl.multiple_of(t * prep_rows, prep_rows), prep_rows)
            cs, sn = cos_ref[rows, :], sin_ref[rows, :]
            qf[rows, :] = _rope(q_ref[rows, :], cs, sn) * scale
            kf[rows, :] = _rope(k_ref[rows, :], cs, sn)
            vf[rows, :] = v_ref[rows, :].astype(F32)
            dov = do_ref[rows, :].astype(F32)
            dof[rows, :] = dov
            dl[rows, :] = jnp.broadcast_to(jnp.sum(dov * o_ref[rows, :], axis=-1, keepdims=True), (prep_rows, DIL_HD))
            zero = jnp.zeros((prep_rows, DIL_HD), F32)
            dqa[rows, :] = zero
            dka[rows, :] = zero
            dva[rows, :] = zero
            return carry

        lax.fori_loop(0, s_len // prep_rows, prep, 0)

        for d in DIL_DILATIONS:
            def pair(i, carry, d=d):
                idx = [_dil_pair_block(i, half, d, nblk) for half in range(DIL_GROUP)]
                rows = [(_strided(qs, DIL_BLOCK, d), _strided(ks, 2 * DIL_BLOCK, d)) for qs, ks, _ in idx]
                ld = [(qf[qr, :].astype(BF16), kf[kr, :].astype(BF16), vf[kr, :].astype(BF16),
                       dof[qr, :].astype(BF16)) for qr, kr in rows]
                s = [_dot_nt(qb, kk) + bias[sel] for (qb, kk, _, _), (_, _, sel) in zip(ld, idx)]
                dp = [_dot_nt(dob, vv) for _, _, vv, dob in ld]
                p = [jnp.exp(sv - lse_ref[qr, :][:, 0:1]) for sv, (qr, _) in zip(s, rows)]
                ds = [(pv * (dpv - dl[qr, :][:, 0:1])).astype(BF16) for pv, dpv, (qr, _) in zip(p, dp, rows)]
                pb = [pv.astype(BF16) for pv in p]
                gq = [_dot(dsv, kk) for dsv, (_, kk, _, _) in zip(ds, ld)]
                gk = [_dot_tn(dsv, qb) for dsv, (qb, _, _, _) in zip(ds, ld)]
                gv = [_dot_tn(pv, dob) for pv, (_, _, _, dob) in zip(pb, ld)]
                for (qr, kr), a, b, c in zip(rows, gq, gk, gv):
                    dqa[qr, :] += a
                    dka[kr, :] += b
                    dva[kr, :] += c
                return carry

            lax.fori_loop(0, nblk // DIL_GROUP, pair, 0)

        def fin(t, carry):
            rows = pl.ds(pl.multiple_of(t * prep_rows, prep_rows), prep_rows)
            cs, sn = cos_ref[rows, :], sin_ref[rows, :]
            gq, gk = dqa[rows, :] * scale, dka[rows, :]
            dq_ref[rows, :] = (gq * cs - pltpu.roll(gq, DIL_HD // 2, 1) * sn).astype(BF16)
            dk_ref[rows, :] = (gk * cs - pltpu.roll(gk, DIL_HD // 2, 1) * sn).astype(BF16)
            dv_ref[rows, :] = dva[rows, :].astype(BF16)
            return carry

        lax.fori_loop(0, s_len // prep_rows, fin, 0)

    head = lambda base: pl.BlockSpec((s_len, DIL_HD), lambda h: (0, base // DIL_HD + h))
    table = pl.BlockSpec((s_len, DIL_HD), lambda h: (0, 0))
    out = pl.BlockSpec((s_len, DIL_HD), lambda h: (0, h))
    shp = jax.ShapeDtypeStruct((s_len, DIL_HEADS * DIL_HD), BF16)
    return pl.pallas_call(
        body, name="dil_bwd", grid=(DIL_HEADS,),
        out_shape=(shp, shp, shp),
        in_specs=[head(COL_QB), head(COL_KB), head(COL_VB - NP_F32), table, table,
                  pl.BlockSpec((s_len, DIL_HD), lambda h: (0, DIL_HEADS + h)), out, out],
        out_specs=(out, out, out),
        scratch_shapes=[pltpu.VMEM((s_len, DIL_HD), F32) for _ in range(8)]
        + [pltpu.VMEM((2, DIL_BLOCK, 2 * DIL_BLOCK), F32)],
        compiler_params=_params(("arbitrary",), 56),
    )(pf, pf, pb, cos, sin_signed, do, o_b, lse)


_PIECES = ((COL_Z, 1024), (COL_QA, 256), (COL_KA, 256), (COL_QB, 512), (COL_KB, 512), (COL_VA, 512), (COL_VB, 512),
           (COL_LR, 128))


def _in_bwd(pieces, w_new, x, dxo, g_pre, scale, ts=256):
    s_len = x.shape[0]

    def body(*refs):
        p_refs = refs[:len(_PIECES)]
        w_ref, x_ref, dxo_ref, g_ref, sc_ref, dx_ref, sums_ref = refs[len(_PIECES):]

        @pl.when(pl.program_id(0) == 0)
        def _():
            sums_ref[...] = jnp.zeros_like(sums_ref)

        dh = jnp.zeros((ts, D_MODEL), F32)
        for p_ref, (col, width) in zip(p_refs, _PIECES):
            dh += _dot_nt(p_ref[...], w_ref[:, col:col + width])
        xv = x_ref[...]
        rstd = lax.rsqrt(jnp.mean(xv * xv, axis=-1, keepdims=True) + EPS)
        xn = xv * rstd
        sums_ref[0:1, :] += jnp.sum(dh, axis=0, keepdims=True)
        sums_ref[1:2, :] += jnp.sum(dh * (xn * g_ref[...]), axis=0, keepdims=True)
        dr = dh * (1.0 + sc_ref[...])
        sums_ref[2:3, :] += jnp.sum(dr * xn, axis=0, keepdims=True)
        dxn = dr * g_ref[...]
        dx_ref[...] = dxo_ref[...] + rstd * (dxn - xn * jnp.mean(dxn * xn, axis=-1, keepdims=True))

    vec = pl.BlockSpec((1, D_MODEL), lambda i: (0, 0))
    tile = pl.BlockSpec((ts, D_MODEL), lambda i: (i, 0))
    return pl.pallas_call(
        body, name="in_bwd", grid=(s_len // ts,),
        out_shape=(jax.ShapeDtypeStruct((s_len, D_MODEL), F32), jax.ShapeDtypeStruct((8, D_MODEL), F32)),
        in_specs=[pl.BlockSpec((ts, width), lambda i: (i, 0)) for _, width in _PIECES]
        + [pl.BlockSpec((D_MODEL, NP), lambda i: (0, 0)), tile, tile, vec, vec],
        out_specs=(tile, pl.BlockSpec((8, D_MODEL), lambda i: (0, 0))),
        compiler_params=_params(("arbitrary",), 48),
    )(*pieces, w_new, x, dxo, g_pre, scale)


def _matmul_tn(a, b, name, bn, ts=512):
    s_len, m = a.shape
    n = b.shape[1]

    def body(a_ref, b_ref, o_ref):
        @pl.when(pl.program_id(1) == 0)
        def _():
            o_ref[...] = jnp.zeros_like(o_ref)

        o_ref[...] += _dot_tn(a_ref[...], b_ref[...])

    return pl.pallas_call(
        body, name=name, grid=(n // bn, s_len // ts),
        out_shape=jax.ShapeDtypeStruct((m, n), F32),
        in_specs=[pl.BlockSpec((ts, m), lambda j, i: (i, 0)), pl.BlockSpec((ts, bn), lambda j, i: (i, j))],
        out_specs=pl.BlockSpec((m, bn), lambda j, i: (0, j)),
        compiler_params=_params(("arbitrary", "arbitrary"), 40),
    )(a, b)


def _adam_math(w, g, m, v):
    m = ADAM_B1 * m + (1.0 - ADAM_B1) * g
    v = ADAM_B2 * v + (1.0 - ADAM_B2) * (g * g)
    m_hat = m / (1.0 - ADAM_B1 ** ADAM_STEP)
    v_hat = v / (1.0 - ADAM_B2 ** ADAM_STEP)
    delta = -ADAM_LR * (m_hat / (jnp.sqrt(v_hat) + ADAM_EPS) + ADAM_WD * w)
    return delta, m, v


def _adamw(w, parts, m, v, name, tr):
    r, cdim = w.shape
    n_parts = parts.shape[0]

    def body(w_ref, p_ref, m_ref, v_ref, g_ref, d_ref, nm_ref, nv_ref):
        g = p_ref[0].astype(F32)
        for k in range(1, n_parts):
            g = g + p_ref[k].astype(F32)
        g_ref[...] = g
        d_ref[...], nm_ref[...], nv_ref[...] = _adam_math(w_ref[...], g, m_ref[...], v_ref[...])

    tile = pl.BlockSpec((tr, cdim), lambda i: (i, 0))
    shp = jax.ShapeDtypeStruct((r, cdim), F32)
    return pl.pallas_call(
        body, name=name, grid=(r // tr,), out_shape=(shp, shp, shp, shp),
        in_specs=[tile, pl.BlockSpec((n_parts, tr, cdim), lambda i: (0, i, 0)), tile, tile],
        out_specs=(tile, tile, tile, tile),
        compiler_params=_params(("arbitrary",), 40),
    )(w, parts, m, v)


def _to_kernel_columns(w):
    pad = jnp.zeros((w.shape[0], LANE - GLA_LOWRANK), w.dtype)
    return jnp.concatenate([w[:, 1024:1536], w[:, 3088:3600], w[:, 0:512], w[:, 1552:2576], w[:, 512:1024],
                            w[:, 2576:3088], w[:, 1536:1552], pad], axis=1)


def _from_kernel_columns(g):
    return jnp.concatenate([g[:, COL_QA:COL_QB], g[:, COL_VA:COL_VB], g[:, 0:512], g[:, COL_LR:COL_LR + GLA_LOWRANK],
                            g[:, COL_QB:COL_VA], g[:, COL_VB:COL_LR], g[:, 512:1024]], axis=1)


def _row(vec, width):
    vec = vec.reshape(1, -1)
    return jnp.pad(vec, ((0, 0), (0, width - vec.shape[1])))


def kernel(x, c, w_ada, b_ada, g_pre, w_in, w_gate_up, b_gate_up, g_gla, g_dil, w_out, g_post, loss_target, m_w_ada, m_b_ada, m_g_pre, m_w_in, m_w_gate_up, m_b_gate_up, m_g_gla, m_g_dil, m_w_out, m_g_post, v_w_ada, v_b_ada, v_g_pre, v_w_in, v_w_gate_up, v_b_gate_up, v_g_gla, v_g_dil, v_w_out, v_g_post):
    px, py, pc = _my_position()
    me = _linear(px, py, pc)
    xs = x[0]
    target = loss_target[0]
    s_len = xs.shape[0]
    assert s_len % (DIL_BLOCK * max(DIL_DILATIONS) * 2) == 0 and xs.shape[1] == D_MODEL

    c_all = _all_gather(jnp.pad(c, ((0, 7), (0, 0))), "gather_c").reshape(N_DEV, 8, D_MODEL)[:, 0]
    mod_part = _mod_fwd(c_all, w_ada)
    mod_all = _all_gather(mod_part.reshape(DEPTH * N_DEV, ADA_SHARD), "gather_mod")
    mod_all = mod_all.reshape(N_DEV, DEPTH, N_DEV, ADA_SHARD)
    mod_mine = lax.dynamic_index_in_dim(mod_all, me, axis=2, keepdims=False)
    mod = jnp.transpose(mod_mine, (1, 0, 2)).reshape(DEPTH, 3 * D_MODEL) + b_ada

    w_in_all = _all_gather(w_in.astype(BF16).reshape(DEPTH * D_MODEL, W_IN_SHARD), "gather_w_in")
    w_in_all = w_in_all.reshape(N_DEV, DEPTH, D_MODEL, W_IN_SHARD)
    w_in_full = jnp.transpose(w_in_all, (1, 2, 0, 3)).reshape(DEPTH, D_MODEL, IN_COLS)
    w_out_all = _all_gather(w_out.astype(BF16).reshape(DEPTH * OUT_SHARD, D_MODEL), "gather_w_out")
    w_out_full = jnp.transpose(w_out_all.reshape(N_DEV, DEPTH, OUT_SHARD, D_MODEL), (1, 0, 2, 3)).reshape(
        DEPTH, D_MODEL, D_MODEL)
    wgu_all = _all_gather(w_gate_up.reshape(DEPTH * GLA_LOWRANK, GU_SHARD), "gather_w_gu")
    wgu_full = jnp.transpose(wgu_all.reshape(N_DEV, DEPTH, GLA_LOWRANK, GU_SHARD), (1, 2, 0, 3)).reshape(
        DEPTH, GLA_LOWRANK, GU_COLS)
    wgu_pad = jnp.pad(wgu_full, ((0, 0), (0, LANE - GLA_LOWRANK), (0, 0))).astype(BF16)

    cos, sin_signed = _rope_tables(s_len)
    g_heads = jnp.concatenate([g_gla, g_dil], axis=1)

    saved = []
    xl = xs
    for l in range(DEPTH):
        shift, scale, gate = (mod[l, k * D_MODEL:(k + 1) * D_MODEL].reshape(1, D_MODEL) for k in range(3))
        w_new = _to_kernel_columns(w_in_full[l])
        pf, pb, h = _prenorm_proj(xl, g_pre[l:l + 1], scale, shift, w_new)
        o_a, states = _gla_fwd(pf, pb, wgu_pad[l], b_gate_up[l:l + 1])
        o_b, lse = _dil_fwd(pf, pb, cos, sin_signed)
        x_next, y, u = _post_fwd(o_a, o_b, pf, g_heads[l:l + 1], w_out_full[l], xl, gate, g_post[l:l + 1])
        saved.append((xl, scale, gate, w_new, pf, pb, h, o_a, states, o_b, lse, y, u))
        xl = x_next

    dx, loss_part = _loss_grad(xl, target)

    small_rows = []
    gw_in, gw_out = [None] * DEPTH, [None] * DEPTH
    for l in reversed(range(DEPTH)):
        x_in, scale, gate, w_new, pf, pb, h, o_a, states, o_b, lse, y, u = saved[l]
        du, do, dz, sums_post = _post_bwd(dx, u, gate, g_post[l:l + 1], w_out_full[l], o_a, o_b, pf, g_heads[l:l + 1])
        gw_out[l] = _matmul_tn(y, du, "grad_w_out", 512)
        dq_a, dk_a, dv_a, dlr2, dwgu, dbgu = _gla_bwd(pf, pb, wgu_pad[l], b_gate_up[l:l + 1], states, do)
        dq_b, dk_b, dv_b = _dil_bwd(pf, pb, cos, sin_signed, do, o_b, lse)
        dlr = (dlr2[0] + dlr2[1]).astype(BF16)
        pieces = (dz, dq_a, dk_a, dq_b, dk_b, dv_a, dv_b, dlr)
        dx, sums_in = _in_bwd(pieces, w_new, x_in, dx, g_pre[l:l + 1], scale)
        blocks = {1024: 512, 256: 256, 512: 512, 128: 128}
        g_new = jnp.concatenate([_matmul_tn(h, p, "grad_w_in_%d" % p.shape[1], blocks[p.shape[1]]) for p in pieces],
                                axis=1)
        gw_in[l] = _from_kernel_columns(g_new)
        dmod = jnp.concatenate([sums_in[0], sums_in[1], sums_post[0]])
        vecs = jnp.concatenate([sums_in[2], sums_post[1], sums_post[2], dbgu[0]])
        small_rows[0:0] = [_row(dmod, 4096), _row(vecs, 4096), _row(dwgu[:GLA_LOWRANK], 4096)]
    grad_x = dx[None]

    gin = jnp.stack(gw_in).astype(BF16).reshape(DEPTH, D_MODEL, N_DEV, W_IN_SHARD)
    gin = jnp.transpose(gin, (2, 0, 1, 3)).reshape(N_DEV * DEPTH * D_MODEL, W_IN_SHARD)
    gin_parts = _all_to_all(gin, "exchange_grad_w_in").reshape(N_DEV, DEPTH * D_MODEL, W_IN_SHARD)
    gout = jnp.stack(gw_out).astype(BF16).reshape(DEPTH, N_DEV, OUT_SHARD, D_MODEL)
    gout = jnp.transpose(gout, (1, 0, 2, 3)).reshape(N_DEV * DEPTH * OUT_SHARD, D_MODEL)
    gout_parts = _all_to_all(gout, "exchange_grad_w_out").reshape(N_DEV, DEPTH * OUT_SHARD, D_MODEL)

    flat = lambda a, rows: a.reshape(rows, a.shape[-1])
    r_in, r_out, r_ada = DEPTH * D_MODEL, DEPTH * OUT_SHARD, DEPTH * D_MODEL
    g_w_in, d_w_in, nm_w_in, nv_w_in = (
        t.reshape(w_in.shape) for t in _adamw(flat(w_in, r_in), gin_parts, flat(m_w_in, r_in), flat(v_w_in, r_in),
                                              "adamw_w_in", 256))
    g_w_out, d_w_out, nm_w_out, nv_w_out = (
        t.reshape(w_out.shape) for t in _adamw(flat(w_out, r_out), gout_parts, flat(m_w_out, r_out),
                                               flat(v_w_out, r_out), "adamw_w_out", 128))

    small_rows += [_row(loss_part[0, 0:1], 4096), jnp.zeros((1, 4096), F32)]
    small = _all_gather(jnp.concatenate(small_rows, axis=0), "gather_small").reshape(N_DEV, 8, 4096)
    dmod_all = jnp.stack([small[:, 0, :3 * D_MODEL], small[:, 3, :3 * D_MODEL]])
    dmod_cols = lax.dynamic_slice_in_dim(dmod_all, me * ADA_SHARD, ADA_SHARD, axis=2)
    gwa = _w_ada_grad(c_all, dmod_cols).reshape(1, r_ada, ADA_SHARD)
    g_w_ada, d_w_ada, nm_w_ada, nv_w_ada = (
        t.reshape(w_ada.shape) for t in _adamw(flat(w_ada, r_ada), gwa, flat(m_w_ada, r_ada), flat(v_w_ada, r_ada),
                                               "adamw_w_ada", 256))

    def small_param(w, m, v, cols, row, name):
        n = w.shape[1]
        parts = jnp.stack([small[:, row, cols:cols + n], small[:, row + 3, cols:cols + n]], axis=1)
        return _adamw(w, parts, m, v, name, DEPTH)

    g_b_ada, d_b_ada, nm_b_ada, nv_b_ada = small_param(b_ada, m_b_ada, v_b_ada, 0, 0, "adamw_b_ada")
    g_g_pre, d_g_pre, nm_g_pre, nv_g_pre = small_param(g_pre, m_g_pre, v_g_pre, 0, 1, "adamw_g_pre")
    g_g_post, d_g_post, nm_g_post, nv_g_post = small_param(g_post, m_g_post, v_g_post, 1024, 1, "adamw_g_post")
    g_g_gla, d_g_gla, nm_g_gla, nv_g_gla = small_param(g_gla, m_g_gla, v_g_gla, 2048, 1, "adamw_g_gla")
    g_g_dil, d_g_dil, nm_g_dil, nv_g_dil = small_param(g_dil, m_g_dil, v_g_dil, 2560, 1, "adamw_g_dil")
    g_b_gu, d_b_gu, nm_b_gu, nv_b_gu = small_param(b_gate_up, m_b_gate_up, v_b_gate_up, 3072, 1, "adamw_b_gate_up")
    gu_parts = jnp.stack([small[:, 2], small[:, 5]], axis=1).reshape(N_DEV, DEPTH, GLA_LOWRANK, GU_COLS)
    gu_parts = lax.dynamic_slice_in_dim(gu_parts, me * GU_SHARD, GU_SHARD, axis=3).reshape(
        N_DEV, DEPTH * GLA_LOWRANK, GU_SHARD)
    r_gu = DEPTH * GLA_LOWRANK
    g_w_gu, d_w_gu, nm_w_gu, nv_w_gu = (
        t.reshape(w_gate_up.shape) for t in _adamw(flat(w_gate_up, r_gu), gu_parts, flat(m_w_gate_up, r_gu),
                                                   flat(v_w_gate_up, r_gu), "adamw_w_gate_up", r_gu))
    loss_parts = jnp.broadcast_to(small[:, 6, 0:1].reshape(N_DEV, 1, 1), (N_DEV, 8, LANE))
    loss = _sum_parts(loss_parts)[0, 0]

    return (loss, grad_x,
            g_w_ada, g_b_ada, g_g_pre, g_w_in, g_w_gu, g_b_gu, g_g_gla, g_g_dil, g_w_out, g_g_post,
            d_w_ada, d_b_ada, d_g_pre, d_w_in, d_w_gu, d_b_gu, d_g_gla, d_g_dil, d_w_out, d_g_post,
            nm_w_ada, nm_b_ada, nm_g_pre, nm_w_in, nm_w_gu, nm_b_gu, nm_g_gla, nm_g_dil, nm_w_out, nm_g_post,
            nv_w_ada, nv_b_ada, nv_g_pre, nv_w_in, nv_w_gu, nv_b_gu, nv_g_gla, nv_g_dil, nv_w_out, nv_g_post)


def _sum_parts(parts):
    n_parts = parts.shape[0]

    def body(p_ref, o_ref):
        acc = p_ref[0]
        for k in range(1, n_parts):
            acc = acc + p_ref[k]
        o_ref[...] = acc

    return pl.pallas_call(body, name="sum_loss", out_shape=jax.ShapeDtypeStruct(parts.shape[1:], F32))(parts)
```

```python
import functools
import math

import jax
import jax.numpy as jnp
from jax import lax
from jax.experimental import pallas as pl
from jax.experimental.pallas import tpu as pltpu

F32 = jnp.float32
BF16 = jnp.bfloat16

N_DEV = 8
D_MODEL = 1024
DEPTH = 2
GLA_HEADS = 4
GLA_DK = 64
GLA_DV = 128
GLA_CHUNK = 64
GLA_TAU = 16.0
GLA_LOWRANK = 16
DIL_HEADS = 4
DIL_HD = 128
DIL_BLOCK = 128
DIL_DILATIONS = (1, 4, 16)
ROPE_THETA = 10000.0
EPS = 1e-6
IN_COLS = 3600
W_IN_SHARD = IN_COLS // N_DEV
ADA_SHARD = 3 * D_MODEL // N_DEV
OUT_SHARD = D_MODEL // N_DEV
GU_COLS = GLA_HEADS * GLA_DK
GU_SHARD = GU_COLS // N_DEV

ADAM_LR = 0.001
ADAM_B1 = 0.9
ADAM_B2 = 0.999
ADAM_EPS = 1e-08
ADAM_WD = 0.01
ADAM_STEP = 10

NP = 3712
COL_Z, COL_QA, COL_KA, COL_QB, COL_KB, COL_VA, COL_VB, COL_LR = 0, 1024, 1280, 1536, 2048, 2560, 3072, 3584
NP_F32 = COL_VA
NP_BF16 = NP - NP_F32
LANE = 128
MASK_VALUE = -1e30

MESH = pl.DeviceIdType.MESH
ANY = pl.BlockSpec(memory_space=pl.ANY)


def _params(sem=None, vmem_mb=None):
    kw = {}
    if sem is not None:
        kw["dimension_semantics"] = sem
    if vmem_mb is not None:
        kw["vmem_limit_bytes"] = vmem_mb * 1024 * 1024
    return pltpu.CompilerParams(**kw)


def _dot(a, b):
    return jnp.dot(a, b, preferred_element_type=F32)


def _dot_nt(a, b):
    return lax.dot_general(a, b, (((1,), (1,)), ((), ())), preferred_element_type=F32)


def _dot_tn(a, b):
    return lax.dot_general(a, b, (((0,), (0,)), ((), ())), preferred_element_type=F32)


def _sigmoid(z):
    return 1.0 / (1.0 + jnp.exp(-z))


def _log_sigmoid(z):
    return jnp.minimum(z, 0.0) - jnp.log(1.0 + jnp.exp(-jnp.abs(z)))


def _my_position():
    return lax.axis_index("x"), lax.axis_index("y"), lax.axis_index("c")


def _linear(px, py, pc):
    return 4 * px + 2 * py + pc


def _gather_phase(phase, x_ref, out_ref, send_sem, recv_sem, local_sem):
    m = x_ref.shape[0]
    x, y, c = _my_position()
    me, sibling = (x, y, c), (x, y, 1 - c)
    chips = [(1 - x, y), (x, 1 - y), (1 - x, 1 - y)]

    def rows(px, py, pc):
        return out_ref.at[pl.ds(_linear(px, py, pc) * m, m), :]

    def copy(k, block, to, src=None):
        return pltpu.make_async_remote_copy(
            src_ref=rows(*block) if src is None else src, dst_ref=rows(*block),
            send_sem=send_sem(k), recv_sem=recv_sem(k), device_id=to, device_id_type=MESH)

    mine = pltpu.make_async_copy(x_ref, rows(*me), local_sem)
    first = [copy(0, me, sibling, src=x_ref)] + [copy(1 + j, me, (*chip, c), src=x_ref) for j, chip in enumerate(chips)]
    passed = [copy(4 + j, (*chip, c), sibling) for j, chip in enumerate(chips)]
    if phase == "start":
        mine.start()
        for cp in first:
            cp.start()
    elif phase == "forward":
        for j, chip in enumerate(chips):
            copy(1 + j, (*chip, c), me).wait_recv()
            passed[j].start()
    else:
        copy(0, sibling, me).wait_recv()
        for j, chip in enumerate(chips):
            copy(4 + j, (*chip, 1 - c), me).wait_recv()
        for cp in first + passed:
            cp.wait_send()
        mine.wait()


def _exchange_phase(phase, x_ref, out_ref, send_sem, recv_sem, local_sem):
    m = x_ref.shape[0] // N_DEV
    x, y, c = _my_position()
    me = _linear(x, y, c)

    def rows(ref, idx):
        return ref.at[pl.ds(idx * m, m), :]

    peers = [(1 - x if j & 4 else x, 1 - y if j & 2 else y, 1 - c if j & 1 else c) for j in range(1, N_DEV)]
    local = pltpu.make_async_copy(rows(x_ref, me), rows(out_ref, me), local_sem)
    sends = [pltpu.make_async_remote_copy(
        src_ref=rows(x_ref, _linear(*peer)), dst_ref=rows(out_ref, me),
        send_sem=send_sem(j), recv_sem=recv_sem(j), device_id=peer, device_id_type=MESH) for j, peer in enumerate(peers)]
    if phase == "start":
        local.start()
        for cp in sends:
            cp.start()
    else:
        for j, peer in enumerate(peers):
            pltpu.make_async_remote_copy(
                src_ref=rows(x_ref, _linear(*peer)), dst_ref=rows(out_ref, _linear(*peer)),
                send_sem=send_sem(j), recv_sem=recv_sem(j), device_id=peer, device_id_type=MESH).wait_recv()
        for cp in sends:
            cp.wait_send()
        local.wait()


_COMM_PHASES = {"gather": (_gather_phase, ("start", "forward", "finish")),
                "exchange": (_exchange_phase, ("start", "finish"))}


def _comm_scratch(n_arrays):
    return [pltpu.SemaphoreType.DMA((n_arrays, 7)), pltpu.SemaphoreType.DMA((n_arrays, 7)),
            pltpu.SemaphoreType.DMA((n_arrays,))]


def _comm_run(kind, phases, x_refs, out_refs, send_sems, recv_sems, local_sems):
    fn = _COMM_PHASES[kind][0]
    for phase in phases:
        for a, (x_ref, out_ref) in enumerate(zip(x_refs, out_refs)):
            fn(phase, x_ref, out_ref, lambda k, a=a: send_sems.at[a, k], lambda k, a=a: recv_sems.at[a, k],
               local_sems.at[a])


def _comm_out_shapes(kind, arrays):
    return [jax.ShapeDtypeStruct((N_DEV * a.shape[0], a.shape[1]) if kind == "gather" else a.shape, a.dtype)
            for a in arrays]


def _comm_call(kind, arrays, name):
    n = len(arrays)

    def body(*refs):
        _comm_run(kind, _COMM_PHASES[kind][1], refs[:n], refs[n:2 * n], *refs[2 * n:])

    return pl.pallas_call(body, name=name, out_shape=_comm_out_shapes(kind, arrays), in_specs=[ANY] * n,
                          out_specs=[ANY] * n, scratch_shapes=_comm_scratch(n))(*arrays)


def _all_gather(xs, name):
    return _comm_call("gather", [xs], name)[0]


def _all_to_all(xs, name):
    return _comm_call("exchange", [xs], name)[0]


def _mod_fwd(c_all, w_ada):
    def body(c_ref, w_ref, o_ref):
        cv = c_ref[...]
        sc = cv * _sigmoid(cv)
        o_ref[0] = _dot(sc.astype(BF16), w_ref[0].astype(BF16))

    return pl.pallas_call(
        body, name="mod_fwd", grid=(DEPTH,),
        out_shape=jax.ShapeDtypeStruct((DEPTH, N_DEV, ADA_SHARD), F32),
        in_specs=[pl.BlockSpec((N_DEV, D_MODEL), lambda l: (0, 0)),
                  pl.BlockSpec((1, D_MODEL, ADA_SHARD), lambda l: (l, 0, 0))],
        out_specs=pl.BlockSpec((1, N_DEV, ADA_SHARD), lambda l: (l, 0, 0)),
        compiler_params=_params(("arbitrary",)),
    )(c_all, w_ada)


def _w_ada_grad(c_all, dmod_cols):
    def body(c_ref, d_ref, o_ref):
        cv = c_ref[...]
        sc = cv * _sigmoid(cv)
        o_ref[0] = lax.dot_general(sc, d_ref[0], (((0,), (0,)), ((), ())), precision=lax.Precision.HIGHEST,
                                   preferred_element_type=F32)

    return pl.pallas_call(
        body, name="w_ada_grad", grid=(DEPTH,),
        out_shape=jax.ShapeDtypeStruct((DEPTH, D_MODEL, ADA_SHARD), F32),
        in_specs=[pl.BlockSpec((N_DEV, D_MODEL), lambda l: (0, 0)),
                  pl.BlockSpec((1, N_DEV, ADA_SHARD), lambda l: (l, 0, 0))],
        out_specs=pl.BlockSpec((1, D_MODEL, ADA_SHARD), lambda l: (l, 0, 0)),
        compiler_params=_params(("arbitrary",)),
    )(c_all, dmod_cols)


def _prenorm_proj(x, g_pre, scale, shift, w_new, ts=256):
    s_len = x.shape[0]

    def body(x_ref, g_ref, sc_ref, sh_ref, w_ref, pf_ref, pb_ref, h_ref):
        xv = x_ref[...]
        rstd = lax.rsqrt(jnp.mean(xv * xv, axis=-1, keepdims=True) + EPS)
        h = (xv * rstd * g_ref[...]) * (1.0 + sc_ref[...]) + sh_ref[...]
        hb = h.astype(BF16)
        h_ref[...] = hb
        for j in range(0, NP, 512):
            w = min(512, NP - j)
            acc = _dot(hb, w_ref[:, j:j + w])
            if j < NP_F32:
                pf_ref[:, j:j + w] = acc
            else:
                pb_ref[:, j - NP_F32:j - NP_F32 + w] = acc.astype(BF16)

    vec = pl.BlockSpec((1, D_MODEL), lambda i: (0, 0))
    return pl.pallas_call(
        body, name="prenorm_proj", grid=(s_len // ts,),
        out_shape=(jax.ShapeDtypeStruct((s_len, NP_F32), F32), jax.ShapeDtypeStruct((s_len, NP_BF16), BF16),
                   jax.ShapeDtypeStruct((s_len, D_MODEL), BF16)),
        in_specs=[pl.BlockSpec((ts, D_MODEL), lambda i: (i, 0)), vec, vec, vec,
                  pl.BlockSpec((D_MODEL, NP), lambda i: (0, 0))],
        out_specs=(pl.BlockSpec((ts, NP_F32), lambda i: (i, 0)), pl.BlockSpec((ts, NP_BF16), lambda i: (i, 0)),
                   pl.BlockSpec((ts, D_MODEL), lambda i: (i, 0))),
        compiler_params=_params(("arbitrary",), 48),
    )(x, g_pre, scale, shift, w_new)


GLA_GROUP = 4


def _gla_group_rows(t):
    return [pl.ds(pl.multiple_of((t * GLA_GROUP + j) * GLA_CHUNK, GLA_CHUNK), GLA_CHUNK) for j in range(GLA_GROUP)]


def _gla_chunks_common(q_ref, k_ref, lr_ref, wgu_ref, bgu_ref, rows_list):
    c = GLA_CHUNK
    ri = lax.broadcasted_iota(jnp.int32, (c, c), 0)
    ci = lax.broadcasted_iota(jnp.int32, (c, c), 1)
    tril = (ri >= ci).astype(F32)
    zs = [_dot(lr_ref[rows, :], wgu_ref[...]) + bgu_ref[...] for rows in rows_list]
    las = [_log_sigmoid(z) * (1.0 / GLA_TAU) for z in zs]
    bs = [jnp.dot(tril, la, precision=lax.Precision.HIGHEST, preferred_element_type=F32) for la in las]
    out = []
    for rows, z, b in zip(rows_list, zs, bs):
        q = q_ref[rows, :] * (GLA_DK ** -0.5)
        k = k_ref[rows, :]
        bl = b[c - 1:c, :]
        out.append(dict(z=z, b=b, bl=bl, qe=q * jnp.exp(b), ke=k * jnp.exp(-b), kend=k * jnp.exp(bl - b),
                        dec=jnp.exp(bl)))
    return out, ri, ci


def _head_lane_mask(hh):
    return (lax.broadcasted_iota(jnp.int32, (1, LANE), 1) // GLA_DK) == hh


def _state_block_mask():
    r = lax.broadcasted_iota(jnp.int32, (2 * GLA_DV, LANE), 0) // GLA_DV
    cc = lax.broadcasted_iota(jnp.int32, (2 * GLA_DV, LANE), 1) // GLA_DK
    return r == cc


def _gla_fwd(pf, pb, wgu, bgu):
    s_len = pf.shape[0]
    nc = s_len // GLA_CHUNK

    def body(q_ref, k_ref, v_ref, lr_ref, wgu_ref, bgu_ref, o_ref, st_ref, qe_s, cs_s, dec_s):
        bd = _state_block_mask()

        def local(t, carry):
            rows_list = _gla_group_rows(t)
            cm, ri, ci = _gla_chunks_common(q_ref, k_ref, lr_ref, wgu_ref, bgu_ref, rows_list)
            vs = [v_ref[rows, :] for rows in rows_list]
            kebs = [c["ke"].astype(BF16) for c in cm]
            a = [[jnp.where(ri >= ci, _dot_nt(jnp.where(_head_lane_mask(hh), c["qe"], 0.0).astype(BF16), keb), 0.0)
                  .astype(BF16) for hh in range(2)] for c, keb in zip(cm, kebs)]
            oi = [[_dot(ah[hh], v[:, hh * GLA_DV:(hh + 1) * GLA_DV]) for hh in range(2)] for ah, v in zip(a, vs)]
            cs = [jnp.where(bd, _dot_tn(v, c["kend"].astype(BF16)), 0.0) for c, v in zip(cm, vs)]
            for j, (rows, c) in enumerate(zip(rows_list, cm)):
                n = t * GLA_GROUP + j
                o_ref[rows, :] = jnp.concatenate(oi[j], axis=1)
                qe_s[rows, :] = c["qe"].astype(BF16)
                cs_s[n] = cs[j]
                dec_s[n] = jnp.broadcast_to(c["dec"], (8, LANE))
            return carry

        lax.fori_loop(0, nc // GLA_GROUP, local, 0)

        def scan(n, st):
            st_ref[0, n] = st.astype(BF16)
            return dec_s[n][0:1, :] * st + cs_s[n]

        lax.fori_loop(0, nc, scan, jnp.zeros((2 * GLA_DV, LANE), F32))

        def inter(t, carry):
            rows_list = _gla_group_rows(t)
            add = [_dot_nt(qe_s[rows, :], st_ref[0, t * GLA_GROUP + j]) for j, rows in enumerate(rows_list)]
            for rows, av in zip(rows_list, add):
                o_ref[rows, :] = o_ref[rows, :] + av
            return carry

        lax.fori_loop(0, nc // GLA_GROUP, inter, 0)

    return pl.pallas_call(
        body, name="gla_fwd", grid=(2,),
        out_shape=(jax.ShapeDtypeStruct((s_len, GLA_HEADS * GLA_DV), F32),
                   jax.ShapeDtypeStruct((2, nc, 2 * GLA_DV, LANE), BF16)),
        in_specs=[pl.BlockSpec((s_len, LANE), lambda g: (0, COL_QA // LANE + g)),
                  pl.BlockSpec((s_len, LANE), lambda g: (0, COL_KA // LANE + g)),
                  pl.BlockSpec((s_len, 2 * GLA_DV), lambda g: (0, (COL_VA - NP_F32) // (2 * GLA_DV) + g)),
                  pl.BlockSpec((s_len, LANE), lambda g: (0, (COL_LR - NP_F32) // LANE)),
                  pl.BlockSpec((LANE, LANE), lambda g: (0, g)),
                  pl.BlockSpec((1, LANE), lambda g: (0, g))],
        out_specs=(pl.BlockSpec((s_len, 2 * GLA_DV), lambda g: (0, g)),
                   pl.BlockSpec((1, nc, 2 * GLA_DV, LANE), lambda g: (g, 0, 0, 0))),
        scratch_shapes=[pltpu.VMEM((s_len, LANE), BF16), pltpu.VMEM((nc, 2 * GLA_DV, LANE), F32),
                        pltpu.VMEM((nc, 8, LANE), F32)],
        compiler_params=_params(("arbitrary",), 56),
    )(pf, pf, pb, pb, wgu, bgu)


def _rope_tables(s_len):
    inv_freq = ROPE_THETA ** (-jnp.arange(0, DIL_HD, 2, dtype=F32) / DIL_HD)
    ang = jnp.arange(s_len, dtype=F32)[:, None] * inv_freq[None, :]
    cos, sin = jnp.cos(ang), jnp.sin(ang)
    return jnp.concatenate([cos, cos], axis=1), jnp.concatenate([-sin, sin], axis=1)


def _rope(xv, cos, sin_signed):
    return xv * cos + pltpu.roll(xv, DIL_HD // 2, 1) * sin_signed


DIL_GROUP = 4


def _dil_pair_block(i, half, d, nblk):
    nb = nblk // d
    j = i + half * (nblk // DIL_GROUP)
    if nb >= 2 * DIL_GROUP:
        r, n = j % d, j // d
    else:
        r, n = j // nb, j % nb
    kb = jnp.maximum(n - 1, 0)
    qs = r + d * DIL_BLOCK * n
    ks = r + d * DIL_BLOCK * kb
    return qs, ks, jnp.minimum(n, 1)


def _dil_fill_bias(bias):
    qi = lax.broadcasted_iota(jnp.int32, (DIL_BLOCK, 2 * DIL_BLOCK), 0)
    kj = lax.broadcasted_iota(jnp.int32, (DIL_BLOCK, 2 * DIL_BLOCK), 1)
    for sel in range(2):
        dist = qi - kj + DIL_BLOCK * sel
        bias[sel] = jnp.where((dist >= 0) & (dist <= DIL_BLOCK), 0.0, MASK_VALUE)


def _strided(start, size, d):
    return pl.ds(start, size) if d == 1 else pl.ds(start, size, stride=d)


def _comm_hooks(comm, cin, cout, csem):
    def before():
        if comm:
            @pl.when(pl.program_id(0) == 0)
            def _():
                _comm_run(comm[0], ("start",), cin, cout, *csem)

            if comm[0] == "gather":
                @pl.when(pl.program_id(0) == DIL_HEADS - 1)
                def _():
                    _comm_run(comm[0], ("forward",), cin, cout, *csem)

    def after():
        if comm:
            @pl.when(pl.program_id(0) == DIL_HEADS - 1)
            def _():
                _comm_run(comm[0], ("finish",), cin, cout, *csem)

    return before, after


def _dil_fwd(pf, pb, cos, sin_signed, comm=None):
    s_len = pf.shape[0]
    nblk = s_len // DIL_BLOCK
    prep_rows = 256
    scale = DIL_HD ** -0.5
    nc = len(comm[1]) if comm else 0

    def body(*refs):
        q_ref, k_ref, v_ref, cos_ref, sin_ref = refs[:5]
        cin, (o_ref, lse_ref), cout = refs[5:5 + nc], refs[5 + nc:7 + nc], refs[7 + nc:7 + 2 * nc]
        qf, kf, vf, o0, o1, o2, l0, l1, l2, bias = refs[7 + 2 * nc:17 + 2 * nc]
        comm_before, comm_after = _comm_hooks(comm, cin, cout, refs[17 + 2 * nc:])
        comm_before()
        _dil_fill_bias(bias)

        def prep(t, carry):
            rows = pl.ds(pl.multiple_of(t * prep_rows, prep_rows), prep_rows)
            cs, sn = cos_ref[rows, :], sin_ref[rows, :]
            qf[rows, :] = _rope(q_ref[rows, :], cs, sn)
            kf[rows, :] = _rope(k_ref[rows, :], cs, sn)
            vf[rows, :] = v_ref[rows, :].astype(F32)
            return carry

        lax.fori_loop(0, s_len // prep_rows, prep, 0)
        ones = jnp.ones((2 * DIL_BLOCK, DIL_HD), BF16)

        for d, o_p, l_p in zip(DIL_DILATIONS, (o0, o1, o2), (l0, l1, l2)):
            def pair(i, carry, d=d, o_p=o_p, l_p=l_p):
                idx = [_dil_pair_block(i, half, d, nblk) for half in range(DIL_GROUP)]
                ld = [(qf[_strided(qs, DIL_BLOCK, d), :].astype(BF16),
                       kf[_strided(ks, 2 * DIL_BLOCK, d), :].astype(BF16),
                       vf[_strided(ks, 2 * DIL_BLOCK, d), :].astype(BF16)) for qs, ks, _ in idx]
                s = [_dot_nt(qb, kk) * scale + bias[sel] for (qb, kk, _), (_, _, sel) in zip(ld, idx)]
                m = [jnp.max(sv, axis=-1, keepdims=True) for sv in s]
                p = [jnp.exp(sv - mv) for sv, mv in zip(s, m)]
                hi = [pv.astype(BF16) for pv in p]
                lo = [(pv - hv.astype(F32)).astype(BF16) for pv, hv in zip(p, hi)]
                r = [_dot(hv, jnp.concatenate([vv, ones], axis=1)) for hv, (_, _, vv) in zip(hi, ld)]
                r2 = [_dot(lv, ones) for lv in lo]
                for rv, r2v, mv, (qs, _, _) in zip(r, r2, m, idx):
                    den = rv[:, DIL_HD:] + r2v
                    o_p[_strided(qs, DIL_BLOCK, d), :] = rv[:, :DIL_HD] / den
                    l_p[_strided(qs, DIL_BLOCK, d), :] = mv + jnp.log(den)
                return carry

            lax.fori_loop(0, nblk // DIL_GROUP, pair, 0)

        def comb(t, carry):
            rows = pl.ds(pl.multiple_of(t * prep_rows, prep_rows), prep_rows)
            a0, a1, a2 = l0[rows, :], l1[rows, :], l2[rows, :]
            m = jnp.maximum(jnp.maximum(a0, a1), a2)
            e0, e1, e2 = jnp.exp(a0 - m), jnp.exp(a1 - m), jnp.exp(a2 - m)
            tot = e0 + e1 + e2
            o_ref[rows, :] = (e0 * o0[rows, :] + e1 * o1[rows, :] + e2 * o2[rows, :]) / tot
            lse_ref[rows, :] = m + jnp.log(tot)
            return carry

        lax.fori_loop(0, s_len // prep_rows, comb, 0)
        comm_after()

    head = lambda base: pl.BlockSpec((s_len, DIL_HD), lambda h: (0, base // DIL_HD + h))
    table = pl.BlockSpec((s_len, DIL_HD), lambda h: (0, 0))
    out = pl.BlockSpec((s_len, DIL_HD), lambda h: (0, h))
    shp = jax.ShapeDtypeStruct((s_len, DIL_HEADS * DIL_HD), F32)
    return pl.pallas_call(
        body, name="dil_fwd_comm" if comm else "dil_fwd", grid=(DIL_HEADS,),
        out_shape=[shp, shp] + (_comm_out_shapes(*comm) if comm else []),
        in_specs=[head(COL_QB), head(COL_KB), head(COL_VB - NP_F32), table, table] + [ANY] * nc,
        out_specs=[out, out] + [ANY] * nc,
        scratch_shapes=[pltpu.VMEM((s_len, DIL_HD), F32) for _ in range(9)]
        + [pltpu.VMEM((2, DIL_BLOCK, 2 * DIL_BLOCK), F32)] + (_comm_scratch(nc) if comm else []),
        compiler_params=_params(("arbitrary",), 56),
    )(pf, pf, pb, cos, sin_signed, *(comm[1] if comm else []))


def _silu_and_grad(z):
    sg = _sigmoid(z)
    return z * sg, sg * (1.0 + z * (1.0 - sg))


def _post_fwd(o_a, o_b, pf, g_heads, w_out, x, gate, g_post, ts=256):
    s_len = x.shape[0]
    half = GLA_HEADS * GLA_DV

    def body(oa_ref, ob_ref, z_ref, gh_ref, w_ref, x_ref, gate_ref, gp_ref, xo_ref, y_ref, u_ref):
        for src, base in ((oa_ref, 0), (ob_ref, half)):
            for hh in range(4):
                lo = hh * LANE
                og = src[:, lo:lo + LANE]
                on = og * lax.rsqrt(jnp.mean(og * og, axis=-1, keepdims=True) + EPS)
                zg = z_ref[:, base + lo:base + lo + LANE].astype(F32)
                y_ref[:, base + lo:base + lo + LANE] = (on * gh_ref[:, base + lo:base + lo + LANE]
                                                        * (zg * _sigmoid(zg))).astype(BF16)
        u = _dot(y_ref[...], w_ref[...])
        u_ref[...] = u.astype(BF16)
        rstd = lax.rsqrt(jnp.mean(u * u, axis=-1, keepdims=True) + EPS)
        xo_ref[...] = x_ref[...] + gate_ref[...] * (u * rstd * gp_ref[...])

    vec = pl.BlockSpec((1, D_MODEL), lambda i: (0, 0))
    tile = pl.BlockSpec((ts, D_MODEL), lambda i: (i, 0))
    halft = pl.BlockSpec((ts, half), lambda i: (i, 0))
    return pl.pallas_call(
        body, name="post_fwd", grid=(s_len // ts,),
        out_shape=(jax.ShapeDtypeStruct((s_len, D_MODEL), F32), jax.ShapeDtypeStruct((s_len, D_MODEL), BF16),
                   jax.ShapeDtypeStruct((s_len, D_MODEL), BF16)),
        in_specs=[halft, halft, tile, vec, pl.BlockSpec((D_MODEL, D_MODEL), lambda i: (0, 0)), tile, vec, vec],
        out_specs=(tile, tile, tile),
        compiler_params=_params(("arbitrary",), 40),
    )(o_a, o_b, pf, g_heads, w_out, x, gate, g_post)


def _loss_grad(y, target, ts=512):
    s_len = y.shape[0]

    def body(y_ref, t_ref, dy_ref, loss_ref):
        @pl.when(pl.program_id(0) == 0)
        def _():
            loss_ref[...] = jnp.zeros_like(loss_ref)

        e = y_ref[...] - t_ref[...]
        dy_ref[...] = e * (1.0 / D_MODEL)
        loss_ref[...] += 0.5 * jnp.sum(jnp.mean(e * e, axis=-1, keepdims=True))

    tile = pl.BlockSpec((ts, D_MODEL), lambda i: (i, 0))
    return pl.pallas_call(
        body, name="loss_grad", grid=(s_len // ts,),
        out_shape=(jax.ShapeDtypeStruct((s_len, D_MODEL), F32), jax.ShapeDtypeStruct((8, LANE), F32)),
        in_specs=[tile, tile], out_specs=(tile, pl.BlockSpec((8, LANE), lambda i: (0, 0))),
        compiler_params=_params(("arbitrary",)),
    )(y, target)


def _post_bwd(dxo, u, gate, g_post, w_out, o_a, o_b, pf, g_heads, ts=256):
    s_len = dxo.shape[0]
    half = GLA_HEADS * GLA_DV

    def body(dx_ref, u_ref, gate_ref, gp_ref, w_ref, oa_ref, ob_ref, z_ref, gh_ref, du_ref, do_ref, dz_ref, sums_ref):
        @pl.when(pl.program_id(0) == 0)
        def _():
            sums_ref[...] = jnp.zeros_like(sums_ref)

        dx = dx_ref[...]
        u = u_ref[...].astype(F32)
        rstd = lax.rsqrt(jnp.mean(u * u, axis=-1, keepdims=True) + EPS)
        un = u * rstd
        sums_ref[0:1, :] += jnp.sum(dx * (un * gp_ref[...]), axis=0, keepdims=True)
        drn = dx * gate_ref[...]
        sums_ref[1:2, :] += jnp.sum(drn * un, axis=0, keepdims=True)
        dun = drn * gp_ref[...]
        du = rstd * (dun - un * jnp.mean(dun * un, axis=-1, keepdims=True))
        dub = du.astype(BF16)
        du_ref[...] = dub
        dy = _dot_nt(dub, w_ref[...])
        for src, base in ((oa_ref, 0), (ob_ref, half)):
            for hh in range(4):
                lo = base + hh * LANE
                og = src[:, hh * LANE:(hh + 1) * LANE]
                rs = lax.rsqrt(jnp.mean(og * og, axis=-1, keepdims=True) + EPS)
                on = og * rs
                zg = z_ref[:, lo:lo + LANE].astype(F32)
                sz, dsz = _silu_and_grad(zg)
                gg = gh_ref[:, lo:lo + LANE]
                dyg = dy[:, lo:lo + LANE]
                sums_ref[2:3, lo:lo + LANE] += jnp.sum(dyg * sz * on, axis=0, keepdims=True)
                dz_ref[:, lo:lo + LANE] = (dyg * on * gg * dsz).astype(BF16)
                don = dyg * gg * sz
                do_ref[:, lo:lo + LANE] = (rs * (don - on * jnp.mean(don * on, axis=-1, keepdims=True))).astype(BF16)

    vec = pl.BlockSpec((1, D_MODEL), lambda i: (0, 0))
    tile = pl.BlockSpec((ts, D_MODEL), lambda i: (i, 0))
    halft = pl.BlockSpec((ts, half), lambda i: (i, 0))
    return pl.pallas_call(
        body, name="post_bwd", grid=(s_len // ts,),
        out_shape=(jax.ShapeDtypeStruct((s_len, D_MODEL), BF16), jax.ShapeDtypeStruct((s_len, D_MODEL), BF16),
                   jax.ShapeDtypeStruct((s_len, D_MODEL), BF16), jax.ShapeDtypeStruct((8, D_MODEL), F32)),
        in_specs=[tile, tile, vec, vec, pl.BlockSpec((D_MODEL, D_MODEL), lambda i: (0, 0)), halft, halft, tile, vec],
        out_specs=(tile, tile, tile, pl.BlockSpec((8, D_MODEL), lambda i: (0, 0))),
        compiler_params=_params(("arbitrary",), 40),
    )(dxo, u, gate, g_post, w_out, o_a, o_b, pf, g_heads)


def _gla_bwd(pf, pb, wgu, bgu, states, do):
    s_len = pf.shape[0]
    nc = s_len // GLA_CHUNK
    c = GLA_CHUNK

    def body(q_ref, k_ref, v_ref, lr_ref, wgu_ref, bgu_ref, st_ref, do_ref,
             dq_ref, dk_ref, dv_ref, dlr_ref, dwgu_ref, dbgu_ref, ds_s, dec_s, dw_acc, db_acc):
        dw_acc[...] = jnp.zeros_like(dw_acc)
        db_acc[...] = jnp.zeros_like(db_acc)
        bd = _state_block_mask()
        last_row = lax.broadcasted_iota(jnp.int32, (c, LANE), 0) == c - 1

        def local(t, carry):
            rows_list = _gla_group_rows(t)
            cm, _, _ = _gla_chunks_common(q_ref, k_ref, lr_ref, wgu_ref, bgu_ref, rows_list)
            loc = [jnp.where(bd, _dot_tn(do_ref[rows, :], cc["qe"].astype(BF16)), 0.0)
                   for rows, cc in zip(rows_list, cm)]
            for j, cc in enumerate(cm):
                ds_s[t * GLA_GROUP + j] = loc[j]
                dec_s[t * GLA_GROUP + j] = jnp.broadcast_to(cc["dec"], (8, LANE))
            return carry

        lax.fori_loop(0, nc // GLA_GROUP, local, 0)

        def scan(t, dst):
            n = nc - 1 - t
            loc = ds_s[n]
            ds_s[n] = dst
            return dec_s[n][0:1, :] * dst + loc

        lax.fori_loop(0, nc, scan, jnp.zeros((2 * GLA_DV, LANE), F32))

        def rest(t, carry):
            rows_list = _gla_group_rows(t)
            cm, ri, ci = _gla_chunks_common(q_ref, k_ref, lr_ref, wgu_ref, bgu_ref, rows_list)
            ns = [t * GLA_GROUP + j for j in range(GLA_GROUP)]
            vs = [v_ref[rows, :] for rows in rows_list]
            dobs = [do_ref[rows, :] for rows in rows_list]
            stbs = [st_ref[0, n] for n in ns]
            dsts = [ds_s[n] for n in ns]
            dstbs = [d.astype(BF16) for d in dsts]
            qebs = [cc["qe"].astype(BF16) for cc in cm]
            kebs = [cc["ke"].astype(BF16) for cc in cm]
            kendbs = [cc["kend"].astype(BF16) for cc in cm]
            hms = [_head_lane_mask(hh) for hh in range(2)]
            qehs = [[jnp.where(hm, cc["qe"], 0.0).astype(BF16) for hm in hms] for cc in cm]
            kehs = [[jnp.where(hm, cc["ke"], 0.0).astype(BF16) for hm in hms] for cc in cm]
            heads = lambda x: [x[:, hh * GLA_DV:(hh + 1) * GLA_DV] for hh in range(2)]
            vhs, dohs = [heads(v) for v in vs], [heads(d) for d in dobs]

            dqe0 = [_dot(dob, stb) for dob, stb in zip(dobs, stbs)]
            dkend = [_dot(v, dstb) for v, dstb in zip(vs, dstbs)]
            dv0 = [_dot_nt(kb, dstb) for kb, dstb in zip(kendbs, dstbs)]
            a_t = [[jnp.where(ci >= ri, _dot_nt(kehs[j][hh], qebs[j]), 0.0).astype(BF16) for hh in range(2)]
                   for j in range(GLA_GROUP)]
            da = [[jnp.where(ri >= ci, _dot_nt(dohs[j][hh], vhs[j][hh]), 0.0).astype(BF16) for hh in range(2)]
                  for j in range(GLA_GROUP)]
            da_t = [[jnp.where(ci >= ri, _dot_nt(vhs[j][hh], dohs[j][hh]), 0.0).astype(BF16) for hh in range(2)]
                    for j in range(GLA_GROUP)]
            dv1 = [[_dot(a_t[j][hh], dohs[j][hh]) for hh in range(2)] for j in range(GLA_GROUP)]
            dqe1 = [[_dot(da[j][hh], kebs[j]) for hh in range(2)] for j in range(GLA_GROUP)]
            dke1 = [[_dot(da_t[j][hh], qehs[j][hh]) for hh in range(2)] for j in range(GLA_GROUP)]

            dbs, dzs = [], []
            for j, (rows, cc) in enumerate(zip(rows_list, cm)):
                qe, ke, kend, b, bl = cc["qe"], cc["ke"], cc["kend"], cc["b"], cc["bl"]
                dqe = dqe0[j] + jnp.where(hms[0], dqe1[j][0], 0.0) + jnp.where(hms[1], dqe1[j][1], 0.0)
                dke = jnp.where(hms[0], dke1[j][0], 0.0) + jnp.where(hms[1], dke1[j][1], 0.0)
                dv_ref[rows, :] = (dv0[j] + jnp.concatenate(dv1[j], axis=1)).astype(BF16)
                dq_ref[rows, :] = (dqe * jnp.exp(b) * (GLA_DK ** -0.5)).astype(BF16)
                dk_ref[rows, :] = (dke * jnp.exp(-b) + dkend[j] * jnp.exp(bl - b)).astype(BF16)
                ddec = jnp.sum(dsts[j] * stbs[j].astype(F32), axis=0, keepdims=True)
                dbl = jnp.sum(dkend[j] * kend, axis=0, keepdims=True) + ddec * cc["dec"]
                dbs.append(dqe * qe - dke * ke - dkend[j] * kend + jnp.where(last_row, dbl, 0.0))
            triu = (ci >= ri).astype(F32)
            dlas = [jnp.dot(triu, db, precision=lax.Precision.HIGHEST, preferred_element_type=F32) for db in dbs]
            dzs = [dla * (1.0 / GLA_TAU) * _sigmoid(-cc["z"]) for dla, cc in zip(dlas, cm)]
            dzbs = [dz.astype(BF16) for dz in dzs]
            dlrs = [_dot_nt(dzb, wgu_ref[...]) for dzb in dzbs]
            dws = [_dot_tn(lr_ref[rows, :], dzb) for rows, dzb in zip(rows_list, dzbs)]
            for rows, dlr in zip(rows_list, dlrs):
                dlr_ref[0, rows, :] = dlr
            dw_acc[...] += functools.reduce(lambda x, y: x + y, dws)
            db_acc[0:1, :] += jnp.sum(functools.reduce(lambda x, y: x + y, dzs), axis=0, keepdims=True)
            return carry

        lax.fori_loop(0, nc // GLA_GROUP, rest, 0)
        dwgu_ref[...] = dw_acc[...]
        dbgu_ref[...] = db_acc[...]

    pair = pl.BlockSpec((s_len, LANE), lambda g: (0, g))
    return pl.pallas_call(
        body, name="gla_bwd", grid=(2,),
        out_shape=(jax.ShapeDtypeStruct((s_len, GU_COLS), BF16), jax.ShapeDtypeStruct((s_len, GU_COLS), BF16),
                   jax.ShapeDtypeStruct((s_len, GLA_HEADS * GLA_DV), BF16),
                   jax.ShapeDtypeStruct((2, s_len, LANE), F32),
                   jax.ShapeDtypeStruct((LANE, GU_COLS), F32), jax.ShapeDtypeStruct((8, GU_COLS), F32)),
        in_specs=[pl.BlockSpec((s_len, LANE), lambda g: (0, COL_QA // LANE + g)),
                  pl.BlockSpec((s_len, LANE), lambda g: (0, COL_KA // LANE + g)),
                  pl.BlockSpec((s_len, 2 * GLA_DV), lambda g: (0, (COL_VA - NP_F32) // (2 * GLA_DV) + g)),
                  pl.BlockSpec((s_len, LANE), lambda g: (0, (COL_LR - NP_F32) // LANE)),
                  pl.BlockSpec((LANE, LANE), lambda g: (0, g)),
                  pl.BlockSpec((1, LANE), lambda g: (0, g)),
                  pl.BlockSpec((1, nc, 2 * GLA_DV, LANE), lambda g: (g, 0, 0, 0)),
                  pl.BlockSpec((s_len, 2 * GLA_DV), lambda g: (0, g))],
        out_specs=(pair, pair, pl.BlockSpec((s_len, 2 * GLA_DV), lambda g: (0, g)),
                   pl.BlockSpec((1, s_len, LANE), lambda g: (g, 0, 0)),
                   pl.BlockSpec((LANE, LANE), lambda g: (0, g)), pl.BlockSpec((8, LANE), lambda g: (0, g))),
        scratch_shapes=[pltpu.VMEM((nc, 2 * GLA_DV, LANE), F32), pltpu.VMEM((nc, 8, LANE), F32),
                        pltpu.VMEM((LANE, LANE), F32), pltpu.VMEM((8, LANE), F32)],
        compiler_params=_params(("arbitrary",), 56),
    )(pf, pf, pb, pb, wgu, bgu, states, do)


def _dil_bwd(pf, pb, cos, sin_signed, do, o_b, lse, comm=None):
    s_len = pf.shape[0]
    nblk = s_len // DIL_BLOCK
    prep_rows = 256
    scale = DIL_HD ** -0.5
    nc = len(comm[1]) if comm else 0

    def body(*refs):
        q_ref, k_ref, v_ref, cos_ref, sin_ref, do_ref, o_ref, lse_ref = refs[:8]
        cin, (dq_ref, dk_ref, dv_ref), cout = refs[8:8 + nc], refs[8 + nc:11 + nc], refs[11 + nc:11 + 2 * nc]
        qf, kf, vf, dof, dl, dqa, dka, dva, bias = refs[11 + 2 * nc:20 + 2 * nc]
        comm_before, comm_after = _comm_hooks(comm, cin, cout, refs[20 + 2 * nc:])
        comm_before()
        _dil_fill_bias(bias)

        def prep(t, carry):
            rows = pl.ds(pl.multiple_of(t * prep_rows, prep_rows), prep_rows)
            cs, sn = cos_ref[rows, :], sin_ref[rows, :]
            qf[rows, :] = _rope(q_ref[rows, :], cs, sn) * scale
            kf[rows, :] = _rope(k_ref[rows, :], cs, sn)
            vf[rows, :] = v_ref[rows, :].astype(F32)
            dov = do_ref[rows, :].astype(F32)
            dof[rows, :] = dov
            dl[rows, :] = jnp.broadcast_to(jnp.sum(dov * o_ref[rows, :], axis=-1, keepdims=True), (prep_rows, DIL_HD))
            zero = jnp.zeros((prep_rows, DIL_HD), F32)
            dqa[rows, :] = zero
            dka[rows, :] = zero
            dva[rows, :] = zero
            return carry

        lax.fori_loop(0, s_len // prep_rows, prep, 0)

        for d in DIL_DILATIONS:
            def pair(i, carry, d=d):
                idx = [_dil_pair_block(i, half, d, nblk) for half in range(DIL_GROUP)]
                rows = [(_strided(qs, DIL_BLOCK, d), _strided(ks, 2 * DIL_BLOCK, d)) for qs, ks, _ in idx]
                ld = [(qf[qr, :].astype(BF16), kf[kr, :].astype(BF16), vf[kr, :].astype(BF16),
                       dof[qr, :].astype(BF16)) for qr, kr in rows]
                s = [_dot_nt(qb, kk) + bias[sel] for (qb, kk, _, _), (_, _, sel) in zip(ld, idx)]
                dp = [_dot_nt(dob, vv) for _, _, vv, dob in ld]
                p = [jnp.exp(sv - lse_ref[qr, :][:, 0:1]) for sv, (qr, _) in zip(s, rows)]
                ds = [(pv * (dpv - dl[qr, :][:, 0:1])).astype(BF16) for pv, dpv, (qr, _) in zip(p, dp, rows)]
                pb = [pv.astype(BF16) for pv in p]
                gq = [_dot(dsv, kk) for dsv, (_, kk, _, _) in zip(ds, ld)]
                gk = [_dot_tn(dsv, qb) for dsv, (qb, _, _, _) in zip(ds, ld)]
                gv = [_dot_tn(pv, dob) for pv, (_, _, _, dob) in zip(pb, ld)]
                for (qr, kr), a, b, c in zip(rows, gq, gk, gv):
                    dqa[qr, :] += a
                    dka[kr, :] += b
                    dva[kr, :] += c
                return carry

            lax.fori_loop(0, nblk // DIL_GROUP, pair, 0)

        def fin(t, carry):
            rows = pl.ds(pl.multiple_of(t * prep_rows, prep_rows), prep_rows)
            cs, sn = cos_ref[rows, :], sin_ref[rows, :]
            gq, gk = dqa[rows, :] * scale, dka[rows, :]
            dq_ref[rows, :] = (gq * cs - pltpu.roll(gq, DIL_HD // 2, 1) * sn).astype(BF16)
            dk_ref[rows, :] = (gk * cs - pltpu.roll(gk, DIL_HD // 2, 1) * sn).astype(BF16)
            dv_ref[rows, :] = dva[rows, :].astype(BF16)
            return carry

        lax.fori_loop(0, s_len // prep_rows, fin, 0)
        comm_after()

    head = lambda base: pl.BlockSpec((s_len, DIL_HD), lambda h: (0, base // DIL_HD + h))
    table = pl.BlockSpec((s_len, DIL_HD), lambda h: (0, 0))
    out = pl.BlockSpec((s_len, DIL_HD), lambda h: (0, h))
    shp = jax.ShapeDtypeStruct((s_len, DIL_HEADS * DIL_HD), BF16)
    return pl.pallas_call(
        body, name="dil_bwd_comm" if comm else "dil_bwd", grid=(DIL_HEADS,),
        out_shape=[shp, shp, shp] + (_comm_out_shapes(*comm) if comm else []),
        in_specs=[head(COL_QB), head(COL_KB), head(COL_VB - NP_F32), table, table,
                  pl.BlockSpec((s_len, DIL_HD), lambda h: (0, DIL_HEADS + h)), out, out] + [ANY] * nc,
        out_specs=[out, out, out] + [ANY] * nc,
        scratch_shapes=[pltpu.VMEM((s_len, DIL_HD), F32) for _ in range(8)]
        + [pltpu.VMEM((2, DIL_BLOCK, 2 * DIL_BLOCK), F32)] + (_comm_scratch(nc) if comm else []),
        compiler_params=_params(("arbitrary",), 56),
    )(pf, pf, pb, cos, sin_signed, do, o_b, lse, *(comm[1] if comm else []))


_PIECES = ((COL_Z, 1024), (COL_QA, 256), (COL_KA, 256), (COL_QB, 512), (COL_KB, 512), (COL_VA, 512), (COL_VB, 512),
           (COL_LR, 128))


def _in_bwd(pieces, w_new, x, dxo, g_pre, scale, ts=256):
    s_len = x.shape[0]

    def body(*refs):
        p_refs = refs[:len(_PIECES)]
        w_ref, x_ref, dxo_ref, g_ref, sc_ref, dx_ref, sums_ref = refs[len(_PIECES):]

        @pl.when(pl.program_id(0) == 0)
        def _():
            sums_ref[...] = jnp.zeros_like(sums_ref)

        dh = jnp.zeros((ts, D_MODEL), F32)
        for p_ref, (col, width) in zip(p_refs, _PIECES):
            dh += _dot_nt(p_ref[...], w_ref[:, col:col + width])
        xv = x_ref[...]
        rstd = lax.rsqrt(jnp.mean(xv * xv, axis=-1, keepdims=True) + EPS)
        xn = xv * rstd
        sums_ref[0:1, :] += jnp.sum(dh, axis=0, keepdims=True)
        sums_ref[1:2, :] += jnp.sum(dh * (xn * g_ref[...]), axis=0, keepdims=True)
        dr = dh * (1.0 + sc_ref[...])
        sums_ref[2:3, :] += jnp.sum(dr * xn, axis=0, keepdims=True)
        dxn = dr * g_ref[...]
        dx_ref[...] = dxo_ref[...] + rstd * (dxn - xn * jnp.mean(dxn * xn, axis=-1, keepdims=True))

    vec = pl.BlockSpec((1, D_MODEL), lambda i: (0, 0))
    tile = pl.BlockSpec((ts, D_MODEL), lambda i: (i, 0))
    return pl.pallas_call(
        body, name="in_bwd", grid=(s_len // ts,),
        out_shape=(jax.ShapeDtypeStruct((s_len, D_MODEL), F32), jax.ShapeDtypeStruct((8, D_MODEL), F32)),
        in_specs=[pl.BlockSpec((ts, width), lambda i: (i, 0)) for _, width in _PIECES]
        + [pl.BlockSpec((D_MODEL, NP), lambda i: (0, 0)), tile, tile, vec, vec],
        out_specs=(tile, pl.BlockSpec((8, D_MODEL), lambda i: (0, 0))),
        compiler_params=_params(("arbitrary",), 48),
    )(*pieces, w_new, x, dxo, g_pre, scale)


def _matmul_tn(a, b, name, bn, ts=512):
    s_len, m = a.shape
    n = b.shape[1]

    def body(a_ref, b_ref, o_ref):
        @pl.when(pl.program_id(1) == 0)
        def _():
            o_ref[...] = jnp.zeros_like(o_ref)

        o_ref[...] += _dot_tn(a_ref[...], b_ref[...])

    return pl.pallas_call(
        body, name=name, grid=(n // bn, s_len // ts),
        out_shape=jax.ShapeDtypeStruct((m, n), F32),
        in_specs=[pl.BlockSpec((ts, m), lambda j, i: (i, 0)), pl.BlockSpec((ts, bn), lambda j, i: (i, j))],
        out_specs=pl.BlockSpec((m, bn), lambda j, i: (0, j)),
        compiler_params=_params(("arbitrary", "arbitrary"), 40),
    )(a, b)


def _adam_math(w, g, m, v):
    m = ADAM_B1 * m + (1.0 - ADAM_B1) * g
    v = ADAM_B2 * v + (1.0 - ADAM_B2) * (g * g)
    m_hat = m / (1.0 - ADAM_B1 ** ADAM_STEP)
    v_hat = v / (1.0 - ADAM_B2 ** ADAM_STEP)
    delta = -ADAM_LR * (m_hat / (jnp.sqrt(v_hat) + ADAM_EPS) + ADAM_WD * w)
    return delta, m, v


def _adamw(w, parts, m, v, name, tr):
    r, cdim = w.shape
    n_parts = parts.shape[0]

    def body(w_ref, p_ref, m_ref, v_ref, g_ref, d_ref, nm_ref, nv_ref):
        g = p_ref[0].astype(F32)
        for k in range(1, n_parts):
            g = g + p_ref[k].astype(F32)
        g_ref[...] = g
        d_ref[...], nm_ref[...], nv_ref[...] = _adam_math(w_ref[...], g, m_ref[...], v_ref[...])

    tile = pl.BlockSpec((tr, cdim), lambda i: (i, 0))
    shp = jax.ShapeDtypeStruct((r, cdim), F32)
    return pl.pallas_call(
        body, name=name, grid=(r // tr,), out_shape=(shp, shp, shp, shp),
        in_specs=[tile, pl.BlockSpec((n_parts, tr, cdim), lambda i: (0, i, 0)), tile, tile],
        out_specs=(tile, tile, tile, tile),
        compiler_params=_params(("arbitrary",), 40),
    )(w, parts, m, v)


def _adamw_layers(w, parts, m, v, name, tr):
    n_layers, r, cdim = w.shape

    def body(*refs):
        w_ref, p_refs, (m_ref, v_ref) = refs[0], refs[1:1 + n_layers], refs[1 + n_layers:3 + n_layers]
        g_ref, d_ref, nm_ref, nv_ref = refs[3 + n_layers:]
        for l, p_ref in enumerate(p_refs):
            @pl.when(pl.program_id(0) == l)
            def _(p_ref=p_ref):
                g = p_ref[0].astype(F32)
                for k in range(1, p_ref.shape[0]):
                    g = g + p_ref[k].astype(F32)
                g_ref[0] = g
                d_ref[0], nm_ref[0], nv_ref[0] = _adam_math(w_ref[0], g, m_ref[0], v_ref[0])

    tile = pl.BlockSpec((1, tr, cdim), lambda l, i: (l, i, 0))
    part = lambda own: pl.BlockSpec((parts[own].shape[0], tr, cdim), lambda l, i: (0, jnp.where(l == own, i, 0), 0))
    shp = jax.ShapeDtypeStruct(w.shape, F32)
    return pl.pallas_call(
        body, name=name, grid=(n_layers, r // tr), out_shape=(shp, shp, shp, shp),
        in_specs=[tile] + [part(l) for l in range(n_layers)] + [tile, tile],
        out_specs=(tile, tile, tile, tile),
        compiler_params=_params(("arbitrary", "arbitrary"), 40),
    )(w, *parts, m, v)


def _to_kernel_columns(w):
    pad = jnp.zeros((w.shape[0], LANE - GLA_LOWRANK), w.dtype)
    return jnp.concatenate([w[:, 1024:1536], w[:, 3088:3600], w[:, 0:512], w[:, 1552:2576], w[:, 512:1024],
                            w[:, 2576:3088], w[:, 1536:1552], pad], axis=1)


def _from_kernel_columns(g):
    return jnp.concatenate([g[:, COL_QA:COL_QB], g[:, COL_VA:COL_VB], g[:, 0:512], g[:, COL_LR:COL_LR + GLA_LOWRANK],
                            g[:, COL_QB:COL_VA], g[:, COL_VB:COL_LR], g[:, 512:1024]], axis=1)


def _row(vec, width):
    vec = vec.reshape(1, -1)
    return jnp.pad(vec, ((0, 0), (0, width - vec.shape[1])))


def kernel(x, c, w_ada, b_ada, g_pre, w_in, w_gate_up, b_gate_up, g_gla, g_dil, w_out, g_post, loss_target, m_w_ada, m_b_ada, m_g_pre, m_w_in, m_w_gate_up, m_b_gate_up, m_g_gla, m_g_dil, m_w_out, m_g_post, v_w_ada, v_b_ada, v_g_pre, v_w_in, v_w_gate_up, v_b_gate_up, v_g_gla, v_g_dil, v_w_out, v_g_post):
    px, py, pc = _my_position()
    me = _linear(px, py, pc)
    xs = x[0]
    target = loss_target[0]
    s_len = xs.shape[0]
    assert s_len % (DIL_BLOCK * max(DIL_DILATIONS) * 2) == 0 and xs.shape[1] == D_MODEL

    c_all = _all_gather(jnp.pad(c, ((0, 7), (0, 0))), "gather_c").reshape(N_DEV, 8, D_MODEL)[:, 0]
    mod_part = _mod_fwd(c_all, w_ada)
    w_in_b, w_out_b = w_in.astype(BF16), w_out.astype(BF16)
    mod_all, wgu_all, w_in_all, w_out_all = _comm_call(
        "gather", [mod_part.reshape(DEPTH * N_DEV, ADA_SHARD), w_gate_up.reshape(DEPTH * GLA_LOWRANK, GU_SHARD),
                   w_in_b[0], w_out_b[0]], "gather_first")
    mod_all = mod_all.reshape(N_DEV, DEPTH, N_DEV, ADA_SHARD)
    mod_mine = lax.dynamic_index_in_dim(mod_all, me, axis=2, keepdims=False)
    mod = jnp.transpose(mod_mine, (1, 0, 2)).reshape(DEPTH, 3 * D_MODEL) + b_ada
    wgu_full = jnp.transpose(wgu_all.reshape(N_DEV, DEPTH, GLA_LOWRANK, GU_SHARD), (1, 2, 0, 3)).reshape(
        DEPTH, GLA_LOWRANK, GU_COLS)
    wgu_pad = jnp.pad(wgu_full, ((0, 0), (0, LANE - GLA_LOWRANK), (0, 0))).astype(BF16)

    def kernel_w_in(gathered):
        cols = jnp.transpose(gathered.reshape(N_DEV, D_MODEL, W_IN_SHARD), (1, 0, 2)).reshape(D_MODEL, IN_COLS)
        return _to_kernel_columns(cols)

    cos, sin_signed = _rope_tables(s_len)
    g_heads = jnp.concatenate([g_gla, g_dil], axis=1)

    saved = []
    xl = xs
    for l in range(DEPTH):
        shift, scale, gate = (mod[l, k * D_MODEL:(k + 1) * D_MODEL].reshape(1, D_MODEL) for k in range(3))
        w_new, w_out_l = kernel_w_in(w_in_all), w_out_all
        pf, pb, h = _prenorm_proj(xl, g_pre[l:l + 1], scale, shift, w_new)
        o_a, states = _gla_fwd(pf, pb, wgu_pad[l], b_gate_up[l:l + 1])
        if l + 1 < DEPTH:
            o_b, lse, w_in_all, w_out_all = _dil_fwd(pf, pb, cos, sin_signed,
                                                     comm=("gather", [w_in_b[l + 1], w_out_b[l + 1]]))
        else:
            o_b, lse = _dil_fwd(pf, pb, cos, sin_signed)
        x_next, y, u = _post_fwd(o_a, o_b, pf, g_heads[l:l + 1], w_out_l, xl, gate, g_post[l:l + 1])
        saved.append((xl, scale, gate, w_new, w_out_l, pf, pb, h, o_a, states, o_b, lse, y, u))
        xl = x_next

    dx, loss_part = _loss_grad(xl, target)

    small_rows = []
    gin_slots, gin_parts, gout_parts = None, [None] * DEPTH, [None] * DEPTH
    for l in reversed(range(DEPTH)):
        x_in, scale, gate, w_new, w_out_l, pf, pb, h, o_a, states, o_b, lse, y, u = saved[l]
        du, do, dz, sums_post = _post_bwd(dx, u, gate, g_post[l:l + 1], w_out_l, o_a, o_b, pf, g_heads[l:l + 1])
        gout_slots = _matmul_tn(y, du, "grad_w_out", 512).astype(BF16)
        dq_a, dk_a, dv_a, dlr2, dwgu, dbgu = _gla_bwd(pf, pb, wgu_pad[l], b_gate_up[l:l + 1], states, do)
        travelling = [gout_slots] + ([gin_slots] if gin_slots is not None else [])
        dq_b, dk_b, dv_b, *arrived = _dil_bwd(pf, pb, cos, sin_signed, do, o_b, lse, comm=("exchange", travelling))
        gout_parts[l] = arrived[0].reshape(N_DEV, OUT_SHARD, D_MODEL)
        if gin_slots is not None:
            gin_parts[l + 1] = arrived[1].reshape(N_DEV, D_MODEL, W_IN_SHARD)
        dlr = (dlr2[0] + dlr2[1]).astype(BF16)
        pieces = (dz, dq_a, dk_a, dq_b, dk_b, dv_a, dv_b, dlr)
        dx, sums_in = _in_bwd(pieces, w_new, x_in, dx, g_pre[l:l + 1], scale)
        blocks = {1024: 512, 256: 256, 512: 512, 128: 128}
        g_new = jnp.concatenate([_matmul_tn(h, p, "grad_w_in_%d" % p.shape[1], blocks[p.shape[1]]) for p in pieces],
                                axis=1)
        g_cols = _from_kernel_columns(g_new).astype(BF16).reshape(D_MODEL, N_DEV, W_IN_SHARD)
        gin_slots = jnp.transpose(g_cols, (1, 0, 2)).reshape(N_DEV * D_MODEL, W_IN_SHARD)
        dmod = jnp.concatenate([sums_in[0], sums_in[1], sums_post[0]])
        vecs = jnp.concatenate([sums_in[2], sums_post[1], sums_post[2], dbgu[0]])
        small_rows[0:0] = [_row(dmod, 4096), _row(vecs, 4096), _row(dwgu[:GLA_LOWRANK], 4096)]
    grad_x = dx[None]

    gin_parts[0] = _all_to_all(gin_slots, "exchange_grad_w_in").reshape(N_DEV, D_MODEL, W_IN_SHARD)
    flat = lambda a, rows: a.reshape(rows, a.shape[-1])
    r_ada = DEPTH * D_MODEL
    g_w_in, d_w_in, nm_w_in, nv_w_in = _adamw_layers(w_in, gin_parts, m_w_in, v_w_in, "adamw_w_in", 256)
    g_w_out, d_w_out, nm_w_out, nv_w_out = _adamw_layers(w_out, gout_parts, m_w_out, v_w_out, "adamw_w_out", 128)

    small_rows += [_row(loss_part[0, 0:1], 4096), jnp.zeros((1, 4096), F32)]
    small = _all_gather(jnp.concatenate(small_rows, axis=0), "gather_small").reshape(N_DEV, 8, 4096)
    dmod_all = jnp.stack([small[:, 0, :3 * D_MODEL], small[:, 3, :3 * D_MODEL]])
    dmod_cols = lax.dynamic_slice_in_dim(dmod_all, me * ADA_SHARD, ADA_SHARD, axis=2)
    gwa = _w_ada_grad(c_all, dmod_cols).reshape(1, r_ada, ADA_SHARD)
    g_w_ada, d_w_ada, nm_w_ada, nv_w_ada = (
        t.reshape(w_ada.shape) for t in _adamw(flat(w_ada, r_ada), gwa, flat(m_w_ada, r_ada), flat(v_w_ada, r_ada),
                                               "adamw_w_ada", 256))

    def small_param(w, m, v, cols, row, name):
        n = w.shape[1]
        parts = jnp.stack([small[:, row, cols:cols + n], small[:, row + 3, cols:cols + n]], axis=1)
        return _adamw(w, parts, m, v, name, DEPTH)

    g_b_ada, d_b_ada, nm_b_ada, nv_b_ada = small_param(b_ada, m_b_ada, v_b_ada, 0, 0, "adamw_b_ada")
    g_g_pre, d_g_pre, nm_g_pre, nv_g_pre = small_param(g_pre, m_g_pre, v_g_pre, 0, 1, "adamw_g_pre")
    g_g_post, d_g_post, nm_g_post, nv_g_post = small_param(g_post, m_g_post, v_g_post, 1024, 1, "adamw_g_post")
    g_g_gla, d_g_gla, nm_g_gla, nv_g_gla = small_param(g_gla, m_g_gla, v_g_gla, 2048, 1, "adamw_g_gla")
    g_g_dil, d_g_dil, nm_g_dil, nv_g_dil = small_param(g_dil, m_g_dil, v_g_dil, 2560, 1, "adamw_g_dil")
    g_b_gu, d_b_gu, nm_b_gu, nv_b_gu = small_param(b_gate_up, m_b_gate_up, v_b_gate_up, 3072, 1, "adamw_b_gate_up")
    gu_parts = jnp.stack([small[:, 2], small[:, 5]], axis=1).reshape(N_DEV, DEPTH, GLA_LOWRANK, GU_COLS)
    gu_parts = lax.dynamic_slice_in_dim(gu_parts, me * GU_SHARD, GU_SHARD, axis=3).reshape(
        N_DEV, DEPTH * GLA_LOWRANK, GU_SHARD)
    r_gu = DEPTH * GLA_LOWRANK
    g_w_gu, d_w_gu, nm_w_gu, nv_w_gu = (
        t.reshape(w_gate_up.shape) for t in _adamw(flat(w_gate_up, r_gu), gu_parts, flat(m_w_gate_up, r_gu),
                                                   flat(v_w_gate_up, r_gu), "adamw_w_gate_up", r_gu))
    loss_parts = jnp.broadcast_to(small[:, 6, 0:1].reshape(N_DEV, 1, 1), (N_DEV, 8, LANE))
    loss = _sum_parts(loss_parts)[0, 0]

    return (loss, grad_x,
            g_w_ada, g_b_ada, g_g_pre, g_w_in, g_w_gu, g_b_gu, g_g_gla, g_g_dil, g_w_out, g_g_post,
            d_w_ada, d_b_ada, d_g_pre, d_w_in, d_w_gu, d_b_gu, d_g_gla, d_g_dil, d_w_out, d_g_post,
            nm_w_ada, nm_b_ada, nm_g_pre, nm_w_in, nm_w_gu, nm_b_gu, nm_g_gla, nm_g_dil, nm_w_out, nm_g_post,
            nv_w_ada, nv_b_ada, nv_g_pre, nv_w_in, nv_w_gu, nv_b_gu, nv_g_gla, nv_g_dil, nv_w_out, nv_g_post)


def _sum_parts(parts):
    n_parts = parts.shape[0]

    def body(p_ref, o_ref):
        acc = p_ref[0]
        for k in range(1, n_parts):
            acc = acc + p_ref[k]
        o_ref[...] = acc

    return pl.pallas_call(body, name="sum_loss", out_shape=jax.ShapeDtypeStruct(parts.shape[1:], F32))(parts)
```

```python
import functools
import math

import jax
import jax.numpy as jnp
from jax import lax
from jax.experimental import pallas as pl
from jax.experimental.pallas import tpu as pltpu

F32 = jnp.float32
BF16 = jnp.bfloat16

N_DEV = 8
D_MODEL = 1024
DEPTH = 2
GLA_HEADS = 4
GLA_DK = 64
GLA_DV = 128
GLA_CHUNK = 64
GLA_TAU = 16.0
GLA_LOWRANK = 16
DIL_HEADS = 4
DIL_HD = 128
DIL_BLOCK = 128
DIL_DILATIONS = (1, 4, 16)
ROPE_THETA = 10000.0
EPS = 1e-6
IN_COLS = 3600
W_IN_SHARD = IN_COLS // N_DEV
ADA_SHARD = 3 * D_MODEL // N_DEV
OUT_SHARD = D_MODEL // N_DEV
GU_COLS = GLA_HEADS * GLA_DK
GU_SHARD = GU_COLS // N_DEV

ADAM_LR = 0.001
ADAM_B1 = 0.9
ADAM_B2 = 0.999
ADAM_EPS = 1e-08
ADAM_WD = 0.01
ADAM_STEP = 10

NP = 3712
COL_Z, COL_QA, COL_KA, COL_QB, COL_KB, COL_VA, COL_VB, COL_LR = 0, 1024, 1280, 1536, 2048, 2560, 3072, 3584
NP_F32 = COL_VA
NP_BF16 = NP - NP_F32
LANE = 128
MASK_VALUE = -1e30

MESH = pl.DeviceIdType.MESH
ANY = pl.BlockSpec(memory_space=pl.ANY)


def _params(sem=None, vmem_mb=None):
    kw = {}
    if sem is not None:
        kw["dimension_semantics"] = sem
    if vmem_mb is not None:
        kw["vmem_limit_bytes"] = vmem_mb * 1024 * 1024
    return pltpu.CompilerParams(**kw)


def _dot(a, b):
    return jnp.dot(a, b, preferred_element_type=F32)


def _dot_nt(a, b):
    return lax.dot_general(a, b, (((1,), (1,)), ((), ())), preferred_element_type=F32)


def _dot_tn(a, b):
    return lax.dot_general(a, b, (((0,), (0,)), ((), ())), preferred_element_type=F32)


def _sigmoid(z):
    return 1.0 / (1.0 + jnp.exp(-z))


def _log_sigmoid(z):
    return jnp.minimum(z, 0.0) - jnp.log(1.0 + jnp.exp(-jnp.abs(z)))


def _rowvec(v, width=D_MODEL):
    arr, row, cb = v
    return arr.reshape(arr.shape[0], 1, arr.shape[1]), pl.BlockSpec((None, 1, width), lambda *_: (row, 0, cb))


def _my_position():
    return lax.axis_index("x"), lax.axis_index("y"), lax.axis_index("c")


def _linear(px, py, pc):
    return 4 * px + 2 * py + pc


def _gather_phase(phase, x_ref, out_ref, send_sem, recv_sem, local_sem):
    m = x_ref.shape[0]
    x, y, c = _my_position()
    me, sibling = (x, y, c), (x, y, 1 - c)
    chips = [(1 - x, y), (x, 1 - y), (1 - x, 1 - y)]

    def rows(px, py, pc):
        return out_ref.at[pl.ds(_linear(px, py, pc) * m, m), :]

    def copy(k, block, to, src=None):
        return pltpu.make_async_remote_copy(
            src_ref=rows(*block) if src is None else src, dst_ref=rows(*block),
            send_sem=send_sem(k), recv_sem=recv_sem(k), device_id=to, device_id_type=MESH)

    mine = pltpu.make_async_copy(x_ref, rows(*me), local_sem)
    first = [copy(0, me, sibling, src=x_ref)] + [copy(1 + j, me, (*chip, c), src=x_ref) for j, chip in enumerate(chips)]
    passed = [copy(4 + j, (*chip, c), sibling) for j, chip in enumerate(chips)]
    if phase == "start":
        mine.start()
        for cp in first:
            cp.start()
    elif phase == "forward":
        for j, chip in enumerate(chips):
            copy(1 + j, (*chip, c), me).wait_recv()
            passed[j].start()
    else:
        copy(0, sibling, me).wait_recv()
        for j, chip in enumerate(chips):
            copy(4 + j, (*chip, 1 - c), me).wait_recv()
        for cp in first + passed:
            cp.wait_send()
        mine.wait()


def _exchange_phase(phase, x_ref, out_ref, send_sem, recv_sem, local_sem):
    m = x_ref.shape[0] // N_DEV
    x, y, c = _my_position()
    me = _linear(x, y, c)

    def rows(ref, idx):
        return ref.at[pl.ds(idx * m, m), :]

    peers = [(1 - x if j & 4 else x, 1 - y if j & 2 else y, 1 - c if j & 1 else c) for j in range(1, N_DEV)]
    local = pltpu.make_async_copy(rows(x_ref, me), rows(out_ref, me), local_sem)
    sends = [pltpu.make_async_remote_copy(
        src_ref=rows(x_ref, _linear(*peer)), dst_ref=rows(out_ref, me),
        send_sem=send_sem(j), recv_sem=recv_sem(j), device_id=peer, device_id_type=MESH) for j, peer in enumerate(peers)]
    if phase == "start":
        local.start()
        for cp in sends:
            cp.start()
    else:
        for j, peer in enumerate(peers):
            pltpu.make_async_remote_copy(
                src_ref=rows(x_ref, _linear(*peer)), dst_ref=rows(out_ref, _linear(*peer)),
                send_sem=send_sem(j), recv_sem=recv_sem(j), device_id=peer, device_id_type=MESH).wait_recv()
        for cp in sends:
            cp.wait_send()
        local.wait()


_COMM_PHASES = {"gather": (_gather_phase, ("start", "forward", "finish")),
                "exchange": (_exchange_phase, ("start", "finish"))}


def _comm_scratch(n_arrays):
    return [pltpu.SemaphoreType.DMA((n_arrays, 7)), pltpu.SemaphoreType.DMA((n_arrays, 7)),
            pltpu.SemaphoreType.DMA((n_arrays,))]


def _comm_run(kind, phases, x_refs, out_refs, send_sems, recv_sems, local_sems):
    fn = _COMM_PHASES[kind][0]
    for phase in phases:
        for a, (x_ref, out_ref) in enumerate(zip(x_refs, out_refs)):
            fn(phase, x_ref, out_ref, lambda k, a=a: send_sems.at[a, k], lambda k, a=a: recv_sems.at[a, k],
               local_sems.at[a])


def _comm_out_shapes(kind, arrays):
    return [jax.ShapeDtypeStruct((N_DEV * a.shape[0], a.shape[1]) if kind == "gather" else a.shape, a.dtype)
            for a in arrays]


def _comm_call(kind, arrays, name):
    n = len(arrays)

    def body(*refs):
        _comm_run(kind, _COMM_PHASES[kind][1], refs[:n], refs[n:2 * n], *refs[2 * n:])

    return pl.pallas_call(body, name=name, out_shape=_comm_out_shapes(kind, arrays), in_specs=[ANY] * n,
                          out_specs=[ANY] * n, scratch_shapes=_comm_scratch(n))(*arrays)


def _all_gather(xs, name):
    return _comm_call("gather", [xs], name)[0]


def _all_to_all(xs, name):
    return _comm_call("exchange", [xs], name)[0]


def _mod_fwd(c_all, w_ada):
    def body(c_ref, w_ref, o_ref):
        cv = c_ref[...]
        sc = cv * _sigmoid(cv)
        o_ref[0] = _dot(sc.astype(BF16), w_ref[0].astype(BF16))

    return pl.pallas_call(
        body, name="mod_fwd", grid=(DEPTH,),
        out_shape=jax.ShapeDtypeStruct((DEPTH, N_DEV, ADA_SHARD), F32),
        in_specs=[pl.BlockSpec((N_DEV, D_MODEL), lambda l: (0, 0)),
                  pl.BlockSpec((1, D_MODEL, ADA_SHARD), lambda l: (l, 0, 0))],
        out_specs=pl.BlockSpec((1, N_DEV, ADA_SHARD), lambda l: (l, 0, 0)),
        compiler_params=_params(("arbitrary",)),
    )(c_all, w_ada)


def _w_ada_grad(c_all, dmod_cols):
    def body(c_ref, d_ref, o_ref):
        cv = c_ref[...]
        sc = cv * _sigmoid(cv)
        o_ref[0] = lax.dot_general(sc, d_ref[0], (((0,), (0,)), ((), ())), precision=lax.Precision.HIGHEST,
                                   preferred_element_type=F32)

    return pl.pallas_call(
        body, name="w_ada_grad", grid=(DEPTH,),
        out_shape=jax.ShapeDtypeStruct((DEPTH, D_MODEL, ADA_SHARD), F32),
        in_specs=[pl.BlockSpec((N_DEV, D_MODEL), lambda l: (0, 0)),
                  pl.BlockSpec((1, N_DEV, ADA_SHARD), lambda l: (l, 0, 0))],
        out_specs=pl.BlockSpec((1, D_MODEL, ADA_SHARD), lambda l: (l, 0, 0)),
        compiler_params=_params(("arbitrary",)),
    )(c_all, dmod_cols)


def _prenorm_proj(x, g_pre, scale, shift, w_new, ts=256):
    s_len = x.shape[0]

    def body(x_ref, g_ref, sc_ref, sh_ref, w_ref, pf_ref, pb_ref, h_ref):
        xv = x_ref[...]
        rstd = lax.rsqrt(jnp.mean(xv * xv, axis=-1, keepdims=True) + EPS)
        h = (xv * rstd * g_ref[...]) * (1.0 + sc_ref[...]) + sh_ref[...]
        hb = h.astype(BF16)
        h_ref[...] = hb
        for j in range(0, NP, 512):
            w = min(512, NP - j)
            acc = _dot(hb, w_ref[:, j:j + w])
            if j < NP_F32:
                pf_ref[:, j:j + w] = acc
            else:
                pb_ref[:, j - NP_F32:j - NP_F32 + w] = acc.astype(BF16)

    (g_pre, g_spec), (scale, sc_spec), (shift, sh_spec) = _rowvec(g_pre), _rowvec(scale), _rowvec(shift)
    return pl.pallas_call(
        body, name="prenorm_proj", grid=(s_len // ts,),
        out_shape=(jax.ShapeDtypeStruct((s_len, NP_F32), F32), jax.ShapeDtypeStruct((s_len, NP_BF16), BF16),
                   jax.ShapeDtypeStruct((s_len, D_MODEL), BF16)),
        in_specs=[pl.BlockSpec((ts, D_MODEL), lambda i: (i, 0)), g_spec, sc_spec, sh_spec,
                  pl.BlockSpec((D_MODEL, NP), lambda i: (0, 0))],
        out_specs=(pl.BlockSpec((ts, NP_F32), lambda i: (i, 0)), pl.BlockSpec((ts, NP_BF16), lambda i: (i, 0)),
                   pl.BlockSpec((ts, D_MODEL), lambda i: (i, 0))),
        compiler_params=_params(("arbitrary",), 48),
    )(x, g_pre, scale, shift, w_new)


GLA_GROUP = 4


def _gla_group_rows(t):
    return [pl.ds(pl.multiple_of((t * GLA_GROUP + j) * GLA_CHUNK, GLA_CHUNK), GLA_CHUNK) for j in range(GLA_GROUP)]


def _gla_chunks_common(q_ref, k_ref, lr_ref, wgu_ref, bgu_ref, rows_list):
    c = GLA_CHUNK
    ri = lax.broadcasted_iota(jnp.int32, (c, c), 0)
    ci = lax.broadcasted_iota(jnp.int32, (c, c), 1)
    tril = (ri >= ci).astype(F32)
    zs = [_dot(lr_ref[rows, :], wgu_ref[...]) + bgu_ref[...] for rows in rows_list]
    las = [_log_sigmoid(z) * (1.0 / GLA_TAU) for z in zs]
    bs = [jnp.dot(tril, la, precision=lax.Precision.HIGHEST, preferred_element_type=F32) for la in las]
    out = []
    for rows, z, b in zip(rows_list, zs, bs):
        q = q_ref[rows, :] * (GLA_DK ** -0.5)
        k = k_ref[rows, :]
        bl = b[c - 1:c, :]
        out.append(dict(z=z, b=b, bl=bl, qe=q * jnp.exp(b), ke=k * jnp.exp(-b), kend=k * jnp.exp(bl - b),
                        dec=jnp.exp(bl)))
    return out, ri, ci


def _head_lane_mask(hh):
    return (lax.broadcasted_iota(jnp.int32, (1, LANE), 1) // GLA_DK) == hh


def _state_block_mask():
    r = lax.broadcasted_iota(jnp.int32, (2 * GLA_DV, LANE), 0) // GLA_DV
    cc = lax.broadcasted_iota(jnp.int32, (2 * GLA_DV, LANE), 1) // GLA_DK
    return r == cc


def _gla_fwd(pf, pb, wgu, bgu, layer):
    s_len = pf.shape[0]
    nc = s_len // GLA_CHUNK

    def body(q_ref, k_ref, v_ref, lr_ref, wgu_ref, bgu_ref, o_ref, st_ref, qe_s, cs_s, dec_s):
        bd = _state_block_mask()

        def local(t, carry):
            rows_list = _gla_group_rows(t)
            cm, ri, ci = _gla_chunks_common(q_ref, k_ref, lr_ref, wgu_ref, bgu_ref, rows_list)
            vs = [v_ref[rows, :] for rows in rows_list]
            kebs = [c["ke"].astype(BF16) for c in cm]
            a = [[jnp.where(ri >= ci, _dot_nt(jnp.where(_head_lane_mask(hh), c["qe"], 0.0).astype(BF16), keb), 0.0)
                  .astype(BF16) for hh in range(2)] for c, keb in zip(cm, kebs)]
            oi = [[_dot(ah[hh], v[:, hh * GLA_DV:(hh + 1) * GLA_DV]) for hh in range(2)] for ah, v in zip(a, vs)]
            cs = [jnp.where(bd, _dot_tn(v, c["kend"].astype(BF16)), 0.0) for c, v in zip(cm, vs)]
            for j, (rows, c) in enumerate(zip(rows_list, cm)):
                n = t * GLA_GROUP + j
                o_ref[rows, :] = jnp.concatenate(oi[j], axis=1)
                qe_s[rows, :] = c["qe"].astype(BF16)
                cs_s[n] = cs[j]
                dec_s[n] = jnp.broadcast_to(c["dec"], (8, LANE))
            return carry

        lax.fori_loop(0, nc // GLA_GROUP, local, 0)

        def scan(n, st):
            st_ref[0, n] = st.astype(BF16)
            return dec_s[n][0:1, :] * st + cs_s[n]

        lax.fori_loop(0, nc, scan, jnp.zeros((2 * GLA_DV, LANE), F32))

        def inter(t, carry):
            rows_list = _gla_group_rows(t)
            add = [_dot_nt(qe_s[rows, :], st_ref[0, t * GLA_GROUP + j]) for j, rows in enumerate(rows_list)]
            for rows, av in zip(rows_list, add):
                o_ref[rows, :] = o_ref[rows, :] + av
            return carry

        lax.fori_loop(0, nc // GLA_GROUP, inter, 0)

    return pl.pallas_call(
        body, name="gla_fwd", grid=(2,),
        out_shape=(jax.ShapeDtypeStruct((s_len, GLA_HEADS * GLA_DV), F32),
                   jax.ShapeDtypeStruct((2, nc, 2 * GLA_DV, LANE), BF16)),
        in_specs=[pl.BlockSpec((s_len, LANE), lambda g: (0, COL_QA // LANE + g)),
                  pl.BlockSpec((s_len, LANE), lambda g: (0, COL_KA // LANE + g)),
                  pl.BlockSpec((s_len, 2 * GLA_DV), lambda g: (0, (COL_VA - NP_F32) // (2 * GLA_DV) + g)),
                  pl.BlockSpec((s_len, LANE), lambda g: (0, (COL_LR - NP_F32) // LANE)),
                  pl.BlockSpec((None, LANE, LANE), lambda g: (layer, 0, g)),
                  pl.BlockSpec((None, 1, LANE), lambda g: (layer, 0, g))],
        out_specs=(pl.BlockSpec((s_len, 2 * GLA_DV), lambda g: (0, g)),
                   pl.BlockSpec((1, nc, 2 * GLA_DV, LANE), lambda g: (g, 0, 0, 0))),
        scratch_shapes=[pltpu.VMEM((s_len, LANE), BF16), pltpu.VMEM((nc, 2 * GLA_DV, LANE), F32),
                        pltpu.VMEM((nc, 8, LANE), F32)],
        compiler_params=_params(("arbitrary",), 56),
    )(pf, pf, pb, pb, wgu, bgu.reshape(bgu.shape[0], 1, GU_COLS))


def _rope_tables(s_len):
    inv_freq = ROPE_THETA ** (-jnp.arange(0, DIL_HD, 2, dtype=F32) / DIL_HD)
    ang = jnp.arange(s_len, dtype=F32)[:, None] * inv_freq[None, :]
    cos, sin = jnp.cos(ang), jnp.sin(ang)
    return jnp.concatenate([cos, cos], axis=1), jnp.concatenate([-sin, sin], axis=1)


def _rope(xv, cos, sin_signed):
    return xv * cos + pltpu.roll(xv, DIL_HD // 2, 1) * sin_signed


DIL_GROUP = 4


def _dil_pair_block(i, half, d, nblk):
    nb = nblk // d
    j = i + half * (nblk // DIL_GROUP)
    if nb >= 2 * DIL_GROUP:
        r, n = j % d, j // d
    else:
        r, n = j // nb, j % nb
    kb = jnp.maximum(n - 1, 0)
    qs = r + d * DIL_BLOCK * n
    ks = r + d * DIL_BLOCK * kb
    return qs, ks, jnp.minimum(n, 1)


def _dil_fill_bias(bias):
    qi = lax.broadcasted_iota(jnp.int32, (DIL_BLOCK, 2 * DIL_BLOCK), 0)
    kj = lax.broadcasted_iota(jnp.int32, (DIL_BLOCK, 2 * DIL_BLOCK), 1)
    for sel in range(2):
        dist = qi - kj + DIL_BLOCK * sel
        bias[sel] = jnp.where((dist >= 0) & (dist <= DIL_BLOCK), 0.0, MASK_VALUE)


def _strided(start, size, d):
    return pl.ds(start, size) if d == 1 else pl.ds(start, size, stride=d)


def _comm_hooks(comm, cin, cout, csem, steps=DIL_HEADS):
    def before():
        if comm:
            @pl.when(pl.program_id(0) == 0)
            def _():
                _comm_run(comm[0], ("start",), cin, cout, *csem)

            if comm[0] == "gather":
                @pl.when(pl.program_id(0) == steps - 1)
                def _():
                    _comm_run(comm[0], ("forward",), cin, cout, *csem)

    def after():
        if comm:
            @pl.when(pl.program_id(0) == steps - 1)
            def _():
                _comm_run(comm[0], ("finish",), cin, cout, *csem)

    return before, after


def _dil_fwd(pf, pb, cos, sin_signed, comm=None):
    s_len = pf.shape[0]
    nblk = s_len // DIL_BLOCK
    prep_rows = 256
    scale = DIL_HD ** -0.5
    nc = len(comm[1]) if comm else 0

    def body(*refs):
        q_ref, k_ref, v_ref, cos_ref, sin_ref = refs[:5]
        cin, (o_ref, lse_ref), cout = refs[5:5 + nc], refs[5 + nc:7 + nc], refs[7 + nc:7 + 2 * nc]
        qf, kf, vf, o0, o1, o2, l0, l1, l2, bias = refs[7 + 2 * nc:17 + 2 * nc]
        comm_before, comm_after = _comm_hooks(comm, cin, cout, refs[17 + 2 * nc:])
        comm_before()
        _dil_fill_bias(bias)

        def prep(t, carry):
            rows = pl.ds(pl.multiple_of(t * prep_rows, prep_rows), prep_rows)
            cs, sn = cos_ref[rows, :], sin_ref[rows, :]
            qf[rows, :] = _rope(q_ref[rows, :], cs, sn)
            kf[rows, :] = _rope(k_ref[rows, :], cs, sn)
            vf[rows, :] = v_ref[rows, :].astype(F32)
            return carry

        lax.fori_loop(0, s_len // prep_rows, prep, 0)
        ones = jnp.ones((2 * DIL_BLOCK, DIL_HD), BF16)

        for d, o_p, l_p in zip(DIL_DILATIONS, (o0, o1, o2), (l0, l1, l2)):
            def pair(i, carry, d=d, o_p=o_p, l_p=l_p):
                idx = [_dil_pair_block(i, half, d, nblk) for half in range(DIL_GROUP)]
                ld = [(qf[_strided(qs, DIL_BLOCK, d), :].astype(BF16),
                       kf[_strided(ks, 2 * DIL_BLOCK, d), :].astype(BF16),
                       vf[_strided(ks, 2 * DIL_BLOCK, d), :].astype(BF16)) for qs, ks, _ in idx]
                s = [_dot_nt(qb, kk) * scale + bias[sel] for (qb, kk, _), (_, _, sel) in zip(ld, idx)]
                m = [jnp.max(sv, axis=-1, keepdims=True) for sv in s]
                p = [jnp.exp(sv - mv) for sv, mv in zip(s, m)]
                hi = [pv.astype(BF16) for pv in p]
                lo = [(pv - hv.astype(F32)).astype(BF16) for pv, hv in zip(p, hi)]
                r = [_dot(hv, jnp.concatenate([vv, ones], axis=1)) for hv, (_, _, vv) in zip(hi, ld)]
                r2 = [_dot(lv, ones) for lv in lo]
                for rv, r2v, mv, (qs, _, _) in zip(r, r2, m, idx):
                    den = rv[:, DIL_HD:] + r2v
                    o_p[_strided(qs, DIL_BLOCK, d), :] = rv[:, :DIL_HD] / den
                    l_p[_strided(qs, DIL_BLOCK, d), :] = mv + jnp.log(den)
                return carry

            lax.fori_loop(0, nblk // DIL_GROUP, pair, 0)

        def comb(t, carry):
            rows = pl.ds(pl.multiple_of(t * prep_rows, prep_rows), prep_rows)
            a0, a1, a2 = l0[rows, :], l1[rows, :], l2[rows, :]
            m = jnp.maximum(jnp.maximum(a0, a1), a2)
            e0, e1, e2 = jnp.exp(a0 - m), jnp.exp(a1 - m), jnp.exp(a2 - m)
            tot = e0 + e1 + e2
            o_ref[rows, :] = (e0 * o0[rows, :] + e1 * o1[rows, :] + e2 * o2[rows, :]) / tot
            lse_ref[rows, :] = m + jnp.log(tot)
            return carry

        lax.fori_loop(0, s_len // prep_rows, comb, 0)
        comm_after()

    head = lambda base: pl.BlockSpec((s_len, DIL_HD), lambda h: (0, base // DIL_HD + h))
    table = pl.BlockSpec((s_len, DIL_HD), lambda h: (0, 0))
    out = pl.BlockSpec((s_len, DIL_HD), lambda h: (0, h))
    shp = jax.ShapeDtypeStruct((s_len, DIL_HEADS * DIL_HD), F32)
    return pl.pallas_call(
        body, name="dil_fwd_comm" if comm else "dil_fwd", grid=(DIL_HEADS,),
        out_shape=[shp, shp] + (_comm_out_shapes(*comm) if comm else []),
        in_specs=[head(COL_QB), head(COL_KB), head(COL_VB - NP_F32), table, table] + [ANY] * nc,
        out_specs=[out, out] + [ANY] * nc,
        scratch_shapes=[pltpu.VMEM((s_len, DIL_HD), F32) for _ in range(9)]
        + [pltpu.VMEM((2, DIL_BLOCK, 2 * DIL_BLOCK), F32)] + (_comm_scratch(nc) if comm else []),
        compiler_params=_params(("arbitrary",), 56),
    )(pf, pf, pb, cos, sin_signed, *(comm[1] if comm else []))


def _silu_and_grad(z):
    sg = _sigmoid(z)
    return z * sg, sg * (1.0 + z * (1.0 - sg))


def _post_fwd(o_a, o_b, pf, g_heads, w_out, x, gate, g_post, ts=256):
    s_len = x.shape[0]
    half = GLA_HEADS * GLA_DV

    def body(oa_ref, ob_ref, z_ref, gh_ref, w_ref, x_ref, gate_ref, gp_ref, xo_ref, y_ref, u_ref):
        for src, base in ((oa_ref, 0), (ob_ref, half)):
            for hh in range(4):
                lo = hh * LANE
                og = src[:, lo:lo + LANE]
                on = og * lax.rsqrt(jnp.mean(og * og, axis=-1, keepdims=True) + EPS)
                zg = z_ref[:, base + lo:base + lo + LANE].astype(F32)
                y_ref[:, base + lo:base + lo + LANE] = (on * gh_ref[:, base + lo:base + lo + LANE]
                                                        * (zg * _sigmoid(zg))).astype(BF16)
        u = _dot(y_ref[...], w_ref[...])
        u_ref[...] = u.astype(BF16)
        rstd = lax.rsqrt(jnp.mean(u * u, axis=-1, keepdims=True) + EPS)
        xo_ref[...] = x_ref[...] + gate_ref[...] * (u * rstd * gp_ref[...])

    (g_heads, gh_spec), (gate, gate_spec), (g_post, gp_spec) = _rowvec(g_heads), _rowvec(gate), _rowvec(g_post)
    tile = pl.BlockSpec((ts, D_MODEL), lambda i: (i, 0))
    halft = pl.BlockSpec((ts, half), lambda i: (i, 0))
    return pl.pallas_call(
        body, name="post_fwd", grid=(s_len // ts,),
        out_shape=(jax.ShapeDtypeStruct((s_len, D_MODEL), F32), jax.ShapeDtypeStruct((s_len, D_MODEL), BF16),
                   jax.ShapeDtypeStruct((s_len, D_MODEL), BF16)),
        in_specs=[halft, halft, tile, gh_spec, pl.BlockSpec((D_MODEL, D_MODEL), lambda i: (0, 0)), tile, gate_spec,
                  gp_spec],
        out_specs=(tile, tile, tile),
        compiler_params=_params(("arbitrary",), 40),
    )(o_a, o_b, pf, g_heads, w_out, x, gate, g_post)


def _loss_grad(y, target, ts=512):
    s_len = y.shape[0]

    def body(y_ref, t_ref, dy_ref, loss_ref):
        @pl.when(pl.program_id(0) == 0)
        def _():
            loss_ref[...] = jnp.zeros_like(loss_ref)

        e = y_ref[...] - t_ref[...]
        dy_ref[...] = e * (1.0 / D_MODEL)
        loss_ref[...] += 0.5 * jnp.sum(jnp.mean(e * e, axis=-1, keepdims=True))

    tile = pl.BlockSpec((ts, D_MODEL), lambda i: (i, 0))
    return pl.pallas_call(
        body, name="loss_grad", grid=(s_len // ts,),
        out_shape=(jax.ShapeDtypeStruct((s_len, D_MODEL), F32), jax.ShapeDtypeStruct((8, LANE), F32)),
        in_specs=[tile, tile], out_specs=(tile, pl.BlockSpec((8, LANE), lambda i: (0, 0))),
        compiler_params=_params(("arbitrary",)),
    )(y, target)


def _post_bwd(dxo, u, gate, g_post, w_out, o_a, o_b, pf, g_heads, ts=256):
    s_len = dxo.shape[0]
    half = GLA_HEADS * GLA_DV

    def body(dx_ref, u_ref, gate_ref, gp_ref, w_ref, oa_ref, ob_ref, z_ref, gh_ref, du_ref, do_ref, dz_ref, sums_ref):
        @pl.when(pl.program_id(0) == 0)
        def _():
            sums_ref[...] = jnp.zeros_like(sums_ref)

        dx = dx_ref[...]
        u = u_ref[...].astype(F32)
        rstd = lax.rsqrt(jnp.mean(u * u, axis=-1, keepdims=True) + EPS)
        un = u * rstd
        sums_ref[0:1, :] += jnp.sum(dx * (un * gp_ref[...]), axis=0, keepdims=True)
        drn = dx * gate_ref[...]
        sums_ref[1:2, :] += jnp.sum(drn * un, axis=0, keepdims=True)
        dun = drn * gp_ref[...]
        du = rstd * (dun - un * jnp.mean(dun * un, axis=-1, keepdims=True))
        dub = du.astype(BF16)
        du_ref[...] = dub
        dy = _dot_nt(dub, w_ref[...])
        for src, base in ((oa_ref, 0), (ob_ref, half)):
            for hh in range(4):
                lo = base + hh * LANE
                og = src[:, hh * LANE:(hh + 1) * LANE]
                rs = lax.rsqrt(jnp.mean(og * og, axis=-1, keepdims=True) + EPS)
                on = og * rs
                zg = z_ref[:, lo:lo + LANE].astype(F32)
                sz, dsz = _silu_and_grad(zg)
                gg = gh_ref[:, lo:lo + LANE]
                dyg = dy[:, lo:lo + LANE]
                sums_ref[2:3, lo:lo + LANE] += jnp.sum(dyg * sz * on, axis=0, keepdims=True)
                dz_ref[:, lo:lo + LANE] = (dyg * on * gg * dsz).astype(BF16)
                don = dyg * gg * sz
                do_ref[:, lo:lo + LANE] = (rs * (don - on * jnp.mean(don * on, axis=-1, keepdims=True))).astype(BF16)

    (g_heads, gh_spec), (gate, gate_spec), (g_post, gp_spec) = _rowvec(g_heads), _rowvec(gate), _rowvec(g_post)
    tile = pl.BlockSpec((ts, D_MODEL), lambda i: (i, 0))
    halft = pl.BlockSpec((ts, half), lambda i: (i, 0))
    return pl.pallas_call(
        body, name="post_bwd", grid=(s_len // ts,),
        out_shape=(jax.ShapeDtypeStruct((s_len, D_MODEL), BF16), jax.ShapeDtypeStruct((s_len, D_MODEL), BF16),
                   jax.ShapeDtypeStruct((s_len, D_MODEL), BF16), jax.ShapeDtypeStruct((8, D_MODEL), F32)),
        in_specs=[tile, tile, gate_spec, gp_spec, pl.BlockSpec((D_MODEL, D_MODEL), lambda i: (0, 0)), halft, halft,
                  tile, gh_spec],
        out_specs=(tile, tile, tile, pl.BlockSpec((8, D_MODEL), lambda i: (0, 0))),
        compiler_params=_params(("arbitrary",), 40),
    )(dxo, u, gate, g_post, w_out, o_a, o_b, pf, g_heads)


def _gla_bwd(pf, pb, wgu, bgu, layer, states, do):
    s_len = pf.shape[0]
    nc = s_len // GLA_CHUNK
    c = GLA_CHUNK

    def body(q_ref, k_ref, v_ref, lr_ref, wgu_ref, bgu_ref, st_ref, do_ref,
             dq_ref, dk_ref, dv_ref, dlr_ref, dwgu_ref, dbgu_ref, ds_s, dec_s, dw_acc, db_acc):
        dw_acc[...] = jnp.zeros_like(dw_acc)
        db_acc[...] = jnp.zeros_like(db_acc)
        bd = _state_block_mask()
        last_row = lax.broadcasted_iota(jnp.int32, (c, LANE), 0) == c - 1

        def local(t, carry):
            rows_list = _gla_group_rows(t)
            cm, _, _ = _gla_chunks_common(q_ref, k_ref, lr_ref, wgu_ref, bgu_ref, rows_list)
            loc = [jnp.where(bd, _dot_tn(do_ref[rows, :], cc["qe"].astype(BF16)), 0.0)
                   for rows, cc in zip(rows_list, cm)]
            for j, cc in enumerate(cm):
                ds_s[t * GLA_GROUP + j] = loc[j]
                dec_s[t * GLA_GROUP + j] = jnp.broadcast_to(cc["dec"], (8, LANE))
            return carry

        lax.fori_loop(0, nc // GLA_GROUP, local, 0)

        def scan(t, dst):
            n = nc - 1 - t
            loc = ds_s[n]
            ds_s[n] = dst
            return dec_s[n][0:1, :] * dst + loc

        lax.fori_loop(0, nc, scan, jnp.zeros((2 * GLA_DV, LANE), F32))

        def rest(t, carry):
            rows_list = _gla_group_rows(t)
            cm, ri, ci = _gla_chunks_common(q_ref, k_ref, lr_ref, wgu_ref, bgu_ref, rows_list)
            ns = [t * GLA_GROUP + j for j in range(GLA_GROUP)]
            vs = [v_ref[rows, :] for rows in rows_list]
            dobs = [do_ref[rows, :] for rows in rows_list]
            stbs = [st_ref[0, n] for n in ns]
            dsts = [ds_s[n] for n in ns]
            dstbs = [d.astype(BF16) for d in dsts]
            qebs = [cc["qe"].astype(BF16) for cc in cm]
            kebs = [cc["ke"].astype(BF16) for cc in cm]
            kendbs = [cc["kend"].astype(BF16) for cc in cm]
            hms = [_head_lane_mask(hh) for hh in range(2)]
            qehs = [[jnp.where(hm, cc["qe"], 0.0).astype(BF16) for hm in hms] for cc in cm]
            kehs = [[jnp.where(hm, cc["ke"], 0.0).astype(BF16) for hm in hms] for cc in cm]
            heads = lambda x: [x[:, hh * GLA_DV:(hh + 1) * GLA_DV] for hh in range(2)]
            vhs, dohs = [heads(v) for v in vs], [heads(d) for d in dobs]

            dqe0 = [_dot(dob, stb) for dob, stb in zip(dobs, stbs)]
            dkend = [_dot(v, dstb) for v, dstb in zip(vs, dstbs)]
            dv0 = [_dot_nt(kb, dstb) for kb, dstb in zip(kendbs, dstbs)]
            a_t = [[jnp.where(ci >= ri, _dot_nt(kehs[j][hh], qebs[j]), 0.0).astype(BF16) for hh in range(2)]
                   for j in range(GLA_GROUP)]
            da = [[jnp.where(ri >= ci, _dot_nt(dohs[j][hh], vhs[j][hh]), 0.0).astype(BF16) for hh in range(2)]
                  for j in range(GLA_GROUP)]
            da_t = [[jnp.where(ci >= ri, _dot_nt(vhs[j][hh], dohs[j][hh]), 0.0).astype(BF16) for hh in range(2)]
                    for j in range(GLA_GROUP)]
            dv1 = [[_dot(a_t[j][hh], dohs[j][hh]) for hh in range(2)] for j in range(GLA_GROUP)]
            dqe1 = [[_dot(da[j][hh], kebs[j]) for hh in range(2)] for j in range(GLA_GROUP)]
            dke1 = [[_dot(da_t[j][hh], qehs[j][hh]) for hh in range(2)] for j in range(GLA_GROUP)]

            dbs, dzs = [], []
            for j, (rows, cc) in enumerate(zip(rows_list, cm)):
                qe, ke, kend, b, bl = cc["qe"], cc["ke"], cc["kend"], cc["b"], cc["bl"]
                dqe = dqe0[j] + jnp.where(hms[0], dqe1[j][0], 0.0) + jnp.where(hms[1], dqe1[j][1], 0.0)
                dke = jnp.where(hms[0], dke1[j][0], 0.0) + jnp.where(hms[1], dke1[j][1], 0.0)
                dv_ref[rows, :] = (dv0[j] + jnp.concatenate(dv1[j], axis=1)).astype(BF16)
                dq_ref[rows, :] = (dqe * jnp.exp(b) * (GLA_DK ** -0.5)).astype(BF16)
                dk_ref[rows, :] = (dke * jnp.exp(-b) + dkend[j] * jnp.exp(bl - b)).astype(BF16)
                ddec = jnp.sum(dsts[j] * stbs[j].astype(F32), axis=0, keepdims=True)
                dbl = jnp.sum(dkend[j] * kend, axis=0, keepdims=True) + ddec * cc["dec"]
                dbs.append(dqe * qe - dke * ke - dkend[j] * kend + jnp.where(last_row, dbl, 0.0))
            triu = (ci >= ri).astype(F32)
            dlas = [jnp.dot(triu, db, precision=lax.Precision.HIGHEST, preferred_element_type=F32) for db in dbs]
            dzs = [dla * (1.0 / GLA_TAU) * _sigmoid(-cc["z"]) for dla, cc in zip(dlas, cm)]
            dzbs = [dz.astype(BF16) for dz in dzs]
            dlrs = [_dot_nt(dzb, wgu_ref[...]) for dzb in dzbs]
            dws = [_dot_tn(lr_ref[rows, :], dzb) for rows, dzb in zip(rows_list, dzbs)]
            for rows, dlr in zip(rows_list, dlrs):
                dlr_ref[0, rows, :] = dlr
            dw_acc[...] += functools.reduce(lambda x, y: x + y, dws)
            db_acc[0:1, :] += jnp.sum(functools.reduce(lambda x, y: x + y, dzs), axis=0, keepdims=True)
            return carry

        lax.fori_loop(0, nc // GLA_GROUP, rest, 0)
        dwgu_ref[...] = dw_acc[...]
        dbgu_ref[...] = db_acc[...]

    pair = pl.BlockSpec((s_len, LANE), lambda g: (0, g))
    return pl.pallas_call(
        body, name="gla_bwd", grid=(2,),
        out_shape=(jax.ShapeDtypeStruct((s_len, GU_COLS), BF16), jax.ShapeDtypeStruct((s_len, GU_COLS), BF16),
                   jax.ShapeDtypeStruct((s_len, GLA_HEADS * GLA_DV), BF16),
                   jax.ShapeDtypeStruct((2, s_len, LANE), F32),
                   jax.ShapeDtypeStruct((LANE, GU_COLS), F32), jax.ShapeDtypeStruct((8, GU_COLS), F32)),
        in_specs=[pl.BlockSpec((s_len, LANE), lambda g: (0, COL_QA // LANE + g)),
                  pl.BlockSpec((s_len, LANE), lambda g: (0, COL_KA // LANE + g)),
                  pl.BlockSpec((s_len, 2 * GLA_DV), lambda g: (0, (COL_VA - NP_F32) // (2 * GLA_DV) + g)),
                  pl.BlockSpec((s_len, LANE), lambda g: (0, (COL_LR - NP_F32) // LANE)),
                  pl.BlockSpec((None, LANE, LANE), lambda g: (layer, 0, g)),
                  pl.BlockSpec((None, 1, LANE), lambda g: (layer, 0, g)),
                  pl.BlockSpec((1, nc, 2 * GLA_DV, LANE), lambda g: (g, 0, 0, 0)),
                  pl.BlockSpec((s_len, 2 * GLA_DV), lambda g: (0, g))],
        out_specs=(pair, pair, pl.BlockSpec((s_len, 2 * GLA_DV), lambda g: (0, g)),
                   pl.BlockSpec((1, s_len, LANE), lambda g: (g, 0, 0)),
                   pl.BlockSpec((LANE, LANE), lambda g: (0, g)), pl.BlockSpec((8, LANE), lambda g: (0, g))),
        scratch_shapes=[pltpu.VMEM((nc, 2 * GLA_DV, LANE), F32), pltpu.VMEM((nc, 8, LANE), F32),
                        pltpu.VMEM((LANE, LANE), F32), pltpu.VMEM((8, LANE), F32)],
        compiler_params=_params(("arbitrary",), 56),
    )(pf, pf, pb, pb, wgu, bgu.reshape(bgu.shape[0], 1, GU_COLS), states, do)


def _dil_bwd(pf, pb, cos, sin_signed, do, o_b, lse, comm=None):
    s_len = pf.shape[0]
    nblk = s_len // DIL_BLOCK
    prep_rows = 256
    scale = DIL_HD ** -0.5
    nc = len(comm[1]) if comm else 0

    def body(*refs):
        q_ref, k_ref, v_ref, cos_ref, sin_ref, do_ref, o_ref, lse_ref = refs[:8]
        cin, (dq_ref, dk_ref, dv_ref), cout = refs[8:8 + nc], refs[8 + nc:11 + nc], refs[11 + nc:11 + 2 * nc]
        qf, kf, vf, dof, dl, dqa, dka, dva, bias = refs[11 + 2 * nc:20 + 2 * nc]
        comm_before, comm_after = _comm_hooks(comm, cin, cout, refs[20 + 2 * nc:])
        comm_before()
        _dil_fill_bias(bias)

        def prep(t, carry):
            rows = pl.ds(pl.multiple_of(t * prep_rows, prep_rows), prep_rows)
            cs, sn = cos_ref[rows, :], sin_ref[rows, :]
            qf[rows, :] = _rope(q_ref[rows, :], cs, sn) * scale
            kf[rows, :] = _rope(k_ref[rows, :], cs, sn)
            vf[rows, :] = v_ref[rows, :].astype(F32)
            dov = do_ref[rows, :].astype(F32)
            dof[rows, :] = dov
            dl[rows, :] = jnp.broadcast_to(jnp.sum(dov * o_ref[rows, :], axis=-1, keepdims=True), (prep_rows, DIL_HD))
            zero = jnp.zeros((prep_rows, DIL_HD), F32)
            dqa[rows, :] = zero
            dka[rows, :] = zero
            dva[rows, :] = zero
            return carry

        lax.fori_loop(0, s_len // prep_rows, prep, 0)

        for d in DIL_DILATIONS:
            def pair(i, carry, d=d):
                idx = [_dil_pair_block(i, half, d, nblk) for half in range(DIL_GROUP)]
                rows = [(_strided(qs, DIL_BLOCK, d), _strided(ks, 2 * DIL_BLOCK, d)) for qs, ks, _ in idx]
                ld = [(qf[qr, :].astype(BF16), kf[kr, :].astype(BF16), vf[kr, :].astype(BF16),
                       dof[qr, :].astype(BF16)) for qr, kr in rows]
                s = [_dot_nt(qb, kk) + bias[sel] for (qb, kk, _, _), (_, _, sel) in zip(ld, idx)]
                dp = [_dot_nt(dob, vv) for _, _, vv, dob in ld]
                p = [jnp.exp(sv - lse_ref[qr, :][:, 0:1]) for sv, (qr, _) in zip(s, rows)]
                ds = [(pv * (dpv - dl[qr, :][:, 0:1])).astype(BF16) for pv, dpv, (qr, _) in zip(p, dp, rows)]
                pb = [pv.astype(BF16) for pv in p]
                gq = [_dot(dsv, kk) for dsv, (_, kk, _, _) in zip(ds, ld)]
                gk = [_dot_tn(dsv, qb) for dsv, (qb, _, _, _) in zip(ds, ld)]
                gv = [_dot_tn(pv, dob) for pv, (_, _, _, dob) in zip(pb, ld)]
                for (qr, kr), a, b, c in zip(rows, gq, gk, gv):
                    dqa[qr, :] += a
                    dka[kr, :] += b
                    dva[kr, :] += c
                return carry

            lax.fori_loop(0, nblk // DIL_GROUP, pair, 0)

        def fin(t, carry):
            rows = pl.ds(pl.multiple_of(t * prep_rows, prep_rows), prep_rows)
            cs, sn = cos_ref[rows, :], sin_ref[rows, :]
            gq, gk = dqa[rows, :] * scale, dka[rows, :]
            dq_ref[rows, :] = (gq * cs - pltpu.roll(gq, DIL_HD // 2, 1) * sn).astype(BF16)
            dk_ref[rows, :] = (gk * cs - pltpu.roll(gk, DIL_HD // 2, 1) * sn).astype(BF16)
            dv_ref[rows, :] = dva[rows, :].astype(BF16)
            return carry

        lax.fori_loop(0, s_len // prep_rows, fin, 0)
        comm_after()

    head = lambda base: pl.BlockSpec((s_len, DIL_HD), lambda h: (0, base // DIL_HD + h))
    table = pl.BlockSpec((s_len, DIL_HD), lambda h: (0, 0))
    out = pl.BlockSpec((s_len, DIL_HD), lambda h: (0, h))
    shp = jax.ShapeDtypeStruct((s_len, DIL_HEADS * DIL_HD), BF16)
    return pl.pallas_call(
        body, name="dil_bwd_comm" if comm else "dil_bwd", grid=(DIL_HEADS,),
        out_shape=[shp, shp, shp] + (_comm_out_shapes(*comm) if comm else []),
        in_specs=[head(COL_QB), head(COL_KB), head(COL_VB - NP_F32), table, table,
                  pl.BlockSpec((s_len, DIL_HD), lambda h: (0, DIL_HEADS + h)), out, out] + [ANY] * nc,
        out_specs=[out, out, out] + [ANY] * nc,
        scratch_shapes=[pltpu.VMEM((s_len, DIL_HD), F32) for _ in range(8)]
        + [pltpu.VMEM((2, DIL_BLOCK, 2 * DIL_BLOCK), F32)] + (_comm_scratch(nc) if comm else []),
        compiler_params=_params(("arbitrary",), 56),
    )(pf, pf, pb, cos, sin_signed, do, o_b, lse, *(comm[1] if comm else []))


_PIECES = ((COL_Z, 1024), (COL_QA, 256), (COL_KA, 256), (COL_QB, 512), (COL_KB, 512), (COL_VA, 512), (COL_VB, 512),
           (COL_LR, 128))


def _in_bwd(pieces, w_new, x, dxo, g_pre, scale, comm=None, ts=256):
    s_len = x.shape[0]
    nc = len(comm[1]) if comm else 0
    npc = len(_PIECES)

    def body(*refs):
        p_refs = refs[:npc]
        w_ref, x_ref, dxo_ref, g_ref, sc_ref = refs[npc:npc + 5]
        cin, (dx_ref, sums_ref), cout = (refs[npc + 5:npc + 5 + nc], refs[npc + 5 + nc:npc + 7 + nc],
                                         refs[npc + 7 + nc:npc + 7 + 2 * nc])
        comm_before, comm_after = _comm_hooks(comm, cin, cout, refs[npc + 7 + 2 * nc:], steps=s_len // ts)
        comm_before()

        @pl.when(pl.program_id(0) == 0)
        def _():
            sums_ref[...] = jnp.zeros_like(sums_ref)

        dh = jnp.zeros((ts, D_MODEL), F32)
        for p_ref, (col, width) in zip(p_refs, _PIECES):
            dh += _dot_nt(p_ref[...], w_ref[:, col:col + width])
        xv = x_ref[...]
        rstd = lax.rsqrt(jnp.mean(xv * xv, axis=-1, keepdims=True) + EPS)
        xn = xv * rstd
        sums_ref[0:1, :] += jnp.sum(dh, axis=0, keepdims=True)
        sums_ref[1:2, :] += jnp.sum(dh * (xn * g_ref[...]), axis=0, keepdims=True)
        dr = dh * (1.0 + sc_ref[...])
        sums_ref[2:3, :] += jnp.sum(dr * xn, axis=0, keepdims=True)
        dxn = dr * g_ref[...]
        dx_ref[...] = dxo_ref[...] + rstd * (dxn - xn * jnp.mean(dxn * xn, axis=-1, keepdims=True))
        comm_after()

    (g_pre, g_spec), (scale, sc_spec) = _rowvec(g_pre), _rowvec(scale)
    tile = pl.BlockSpec((ts, D_MODEL), lambda i: (i, 0))
    return pl.pallas_call(
        body, name="in_bwd_comm" if comm else "in_bwd", grid=(s_len // ts,),
        out_shape=[jax.ShapeDtypeStruct((s_len, D_MODEL), F32), jax.ShapeDtypeStruct((8, D_MODEL), F32)]
        + (_comm_out_shapes(*comm) if comm else []),
        in_specs=[pl.BlockSpec((ts, width), lambda i: (i, 0)) for _, width in _PIECES]
        + [pl.BlockSpec((D_MODEL, NP), lambda i: (0, 0)), tile, tile, g_spec, sc_spec] + [ANY] * nc,
        out_specs=[tile, pl.BlockSpec((8, D_MODEL), lambda i: (0, 0))] + [ANY] * nc,
        scratch_shapes=_comm_scratch(nc) if comm else [],
        compiler_params=_params(("arbitrary",), 48),
    )(*pieces, w_new, x, dxo, g_pre, scale, *(comm[1] if comm else []))


def _grad_w_in(h, pieces, ts=512):
    s_len = h.shape[0]

    def body(*refs):
        h_ref, p_refs, o_ref = refs[0], refs[1:1 + len(_PIECES)], refs[1 + len(_PIECES)]

        @pl.when(pl.program_id(0) == 0)
        def _():
            o_ref[...] = jnp.zeros_like(o_ref)

        hv = h_ref[...]
        for p_ref, (col, width) in zip(p_refs, _PIECES):
            o_ref[:, col:col + width] += _dot_tn(hv, p_ref[...])

    return pl.pallas_call(
        body, name="grad_w_in", grid=(s_len // ts,),
        out_shape=jax.ShapeDtypeStruct((D_MODEL, NP), F32),
        in_specs=[pl.BlockSpec((ts, D_MODEL), lambda i: (i, 0))]
        + [pl.BlockSpec((ts, width), lambda i: (i, 0)) for _, width in _PIECES],
        out_specs=pl.BlockSpec((D_MODEL, NP), lambda i: (0, 0)),
        compiler_params=_params(("arbitrary",), 56),
    )(h, *pieces)


def _matmul_tn(a, b, name, bn, ts=512):
    s_len, m = a.shape
    n = b.shape[1]

    def body(a_ref, b_ref, o_ref):
        @pl.when(pl.program_id(1) == 0)
        def _():
            o_ref[...] = jnp.zeros_like(o_ref)

        o_ref[...] += _dot_tn(a_ref[...], b_ref[...])

    return pl.pallas_call(
        body, name=name, grid=(n // bn, s_len // ts),
        out_shape=jax.ShapeDtypeStruct((m, n), F32),
        in_specs=[pl.BlockSpec((ts, m), lambda j, i: (i, 0)), pl.BlockSpec((ts, bn), lambda j, i: (i, j))],
        out_specs=pl.BlockSpec((m, bn), lambda j, i: (0, j)),
        compiler_params=_params(("arbitrary", "arbitrary"), 40),
    )(a, b)


def _adam_math(w, g, m, v):
    m = ADAM_B1 * m + (1.0 - ADAM_B1) * g
    v = ADAM_B2 * v + (1.0 - ADAM_B2) * (g * g)
    m_hat = m / (1.0 - ADAM_B1 ** ADAM_STEP)
    v_hat = v / (1.0 - ADAM_B2 ** ADAM_STEP)
    delta = -ADAM_LR * (m_hat / (jnp.sqrt(v_hat) + ADAM_EPS) + ADAM_WD * w)
    return delta, m, v


def _adamw(w, parts, m, v, name, tr):
    r, cdim = w.shape
    n_parts = parts.shape[0]

    def body(w_ref, p_ref, m_ref, v_ref, g_ref, d_ref, nm_ref, nv_ref):
        g = p_ref[0].astype(F32)
        for k in range(1, n_parts):
            g = g + p_ref[k].astype(F32)
        g_ref[...] = g
        d_ref[...], nm_ref[...], nv_ref[...] = _adam_math(w_ref[...], g, m_ref[...], v_ref[...])

    tile = pl.BlockSpec((tr, cdim), lambda i: (i, 0))
    shp = jax.ShapeDtypeStruct((r, cdim), F32)
    return pl.pallas_call(
        body, name=name, grid=(r // tr,), out_shape=(shp, shp, shp, shp),
        in_specs=[tile, pl.BlockSpec((n_parts, tr, cdim), lambda i: (0, i, 0)), tile, tile],
        out_specs=(tile, tile, tile, tile),
        compiler_params=_params(("arbitrary",), 40),
    )(w, parts, m, v)


def _adamw_layers(w, parts, m, v, name, tr):
    n_layers, r, cdim = w.shape

    def body(*refs):
        w_ref, p_refs, (m_ref, v_ref) = refs[0], refs[1:1 + n_layers], refs[1 + n_layers:3 + n_layers]
        g_ref, d_ref, nm_ref, nv_ref = refs[3 + n_layers:]
        for l, p_ref in enumerate(p_refs):
            @pl.when(pl.program_id(0) == l)
            def _(p_ref=p_ref):
                g = p_ref[0].astype(F32)
                for k in range(1, p_ref.shape[0]):
                    g = g + p_ref[k].astype(F32)
                g_ref[0] = g
                d_ref[0], nm_ref[0], nv_ref[0] = _adam_math(w_ref[0], g, m_ref[0], v_ref[0])

    tile = pl.BlockSpec((1, tr, cdim), lambda l, i: (l, i, 0))
    part = lambda own: pl.BlockSpec((parts[own].shape[0], tr, cdim), lambda l, i: (0, jnp.where(l == own, i, 0), 0))
    shp = jax.ShapeDtypeStruct(w.shape, F32)
    return pl.pallas_call(
        body, name=name, grid=(n_layers, r // tr), out_shape=(shp, shp, shp, shp),
        in_specs=[tile] + [part(l) for l in range(n_layers)] + [tile, tile],
        out_specs=(tile, tile, tile, tile),
        compiler_params=_params(("arbitrary", "arbitrary"), 40),
    )(w, *parts, m, v)


def _to_kernel_columns(w):
    pad = jnp.zeros((w.shape[0], LANE - GLA_LOWRANK), w.dtype)
    return jnp.concatenate([w[:, 1024:1536], w[:, 3088:3600], w[:, 0:512], w[:, 1552:2576], w[:, 512:1024],
                            w[:, 2576:3088], w[:, 1536:1552], pad], axis=1)


def _from_kernel_columns(g):
    return jnp.concatenate([g[:, COL_QA:COL_QB], g[:, COL_VA:COL_VB], g[:, 0:512], g[:, COL_LR:COL_LR + GLA_LOWRANK],
                            g[:, COL_QB:COL_VA], g[:, COL_VB:COL_LR], g[:, 512:1024]], axis=1)


def _row(vec, width):
    vec = vec.reshape(1, -1)
    return jnp.pad(vec, ((0, 0), (0, width - vec.shape[1])))


def kernel(x, c, w_ada, b_ada, g_pre, w_in, w_gate_up, b_gate_up, g_gla, g_dil, w_out, g_post, loss_target, m_w_ada, m_b_ada, m_g_pre, m_w_in, m_w_gate_up, m_b_gate_up, m_g_gla, m_g_dil, m_w_out, m_g_post, v_w_ada, v_b_ada, v_g_pre, v_w_in, v_w_gate_up, v_b_gate_up, v_g_gla, v_g_dil, v_w_out, v_g_post):
    px, py, pc = _my_position()
    me = _linear(px, py, pc)
    xs = x[0]
    target = loss_target[0]
    s_len = xs.shape[0]
    assert s_len % (DIL_BLOCK * max(DIL_DILATIONS) * 2) == 0 and xs.shape[1] == D_MODEL

    c_all = _all_gather(jnp.pad(c, ((0, 7), (0, 0))), "gather_c").reshape(N_DEV, 8, D_MODEL)[:, 0]
    mod_part = _mod_fwd(c_all, w_ada)
    w_in_b, w_out_b = w_in.astype(BF16), w_out.astype(BF16)
    mod_all, wgu_all, w_in_all, w_out_all = _comm_call(
        "gather", [mod_part.reshape(DEPTH * N_DEV, ADA_SHARD), w_gate_up.reshape(DEPTH * GLA_LOWRANK, GU_SHARD),
                   w_in_b[0], w_out_b[0]], "gather_first")
    mod_all = mod_all.reshape(N_DEV, DEPTH, N_DEV, ADA_SHARD)
    mod_mine = lax.dynamic_index_in_dim(mod_all, me, axis=2, keepdims=False)
    mod = jnp.transpose(mod_mine, (1, 0, 2)).reshape(DEPTH, 3 * D_MODEL) + b_ada
    wgu_full = jnp.transpose(wgu_all.reshape(N_DEV, DEPTH, GLA_LOWRANK, GU_SHARD), (1, 2, 0, 3)).reshape(
        DEPTH, GLA_LOWRANK, GU_COLS)
    wgu_pad = jnp.pad(wgu_full, ((0, 0), (0, LANE - GLA_LOWRANK), (0, 0))).astype(BF16)

    def kernel_w_in(gathered):
        cols = jnp.transpose(gathered.reshape(N_DEV, D_MODEL, W_IN_SHARD), (1, 0, 2)).reshape(D_MODEL, IN_COLS)
        return _to_kernel_columns(cols)

    cos, sin_signed = _rope_tables(s_len)
    g_heads = jnp.concatenate([g_gla, g_dil], axis=1)

    saved = []
    xl = xs
    for l in range(DEPTH):
        shift, scale, gate = ((mod, l, k) for k in range(3))
        w_new, w_out_l = kernel_w_in(w_in_all), w_out_all
        pf, pb, h = _prenorm_proj(xl, (g_pre, l, 0), scale, shift, w_new)
        o_a, states = _gla_fwd(pf, pb, wgu_pad, b_gate_up, l)
        if l + 1 < DEPTH:
            o_b, lse, w_in_all, w_out_all = _dil_fwd(pf, pb, cos, sin_signed,
                                                     comm=("gather", [w_in_b[l + 1], w_out_b[l + 1]]))
        else:
            o_b, lse = _dil_fwd(pf, pb, cos, sin_signed)
        x_next, y, u = _post_fwd(o_a, o_b, pf, (g_heads, l, 0), w_out_l, xl, gate, (g_post, l, 0))
        saved.append((xl, scale, gate, w_new, w_out_l, pf, pb, h, o_a, states, o_b, lse, y, u))
        xl = x_next

    dx, loss_part = _loss_grad(xl, target)

    small_rows = []
    gin_slots, gin_parts, gout_parts = None, [None] * DEPTH, [None] * DEPTH
    for l in reversed(range(DEPTH)):
        x_in, scale, gate, w_new, w_out_l, pf, pb, h, o_a, states, o_b, lse, y, u = saved[l]
        du, do, dz, sums_post = _post_bwd(dx, u, gate, (g_post, l, 0), w_out_l, o_a, o_b, pf, (g_heads, l, 0))
        gout_slots = _matmul_tn(y, du, "grad_w_out", 512).astype(BF16)
        dq_a, dk_a, dv_a, dlr2, dwgu, dbgu = _gla_bwd(pf, pb, wgu_pad, b_gate_up, l, states, do)
        travelling = [gout_slots] + ([gin_slots] if gin_slots is not None else [])
        dq_b, dk_b, dv_b, *arrived = _dil_bwd(pf, pb, cos, sin_signed, do, o_b, lse, comm=("exchange", travelling))
        gout_parts[l] = arrived[0].reshape(N_DEV, OUT_SHARD, D_MODEL)
        if gin_slots is not None:
            gin_parts[l + 1] = arrived[1].reshape(N_DEV, D_MODEL, W_IN_SHARD)
        dlr = (dlr2[0] + dlr2[1]).astype(BF16)
        pieces = (dz, dq_a, dk_a, dq_b, dk_b, dv_a, dv_b, dlr)
        g_cols = _from_kernel_columns(_grad_w_in(h, pieces)).astype(BF16).reshape(D_MODEL, N_DEV, W_IN_SHARD)
        gin_slots = jnp.transpose(g_cols, (1, 0, 2)).reshape(N_DEV * D_MODEL, W_IN_SHARD)
        if l == 0:
            dx, sums_in, arrived = _in_bwd(pieces, w_new, x_in, dx, (g_pre, l, 0), scale,
                                           comm=("exchange", [gin_slots]))
            gin_parts[0] = arrived.reshape(N_DEV, D_MODEL, W_IN_SHARD)
        else:
            dx, sums_in = _in_bwd(pieces, w_new, x_in, dx, (g_pre, l, 0), scale)
        dmod = jnp.concatenate([sums_in[0], sums_in[1], sums_post[0]])
        vecs = jnp.concatenate([sums_in[2], sums_post[1], sums_post[2], dbgu[0]])
        small_rows[0:0] = [_row(dmod, 4096), _row(vecs, 4096), _row(dwgu[:GLA_LOWRANK], 4096)]
    grad_x = dx[None]

    flat = lambda a, rows: a.reshape(rows, a.shape[-1])
    r_ada = DEPTH * D_MODEL
    g_w_in, d_w_in, nm_w_in, nv_w_in = _adamw_layers(w_in, gin_parts, m_w_in, v_w_in, "adamw_w_in", 256)
    g_w_out, d_w_out, nm_w_out, nv_w_out = _adamw_layers(w_out, gout_parts, m_w_out, v_w_out, "adamw_w_out", 128)

    small_rows += [_row(loss_part[0, 0:1], 4096), jnp.zeros((1, 4096), F32)]
    small = _all_gather(jnp.concatenate(small_rows, axis=0), "gather_small").reshape(N_DEV, 8, 4096)
    dmod_all = jnp.stack([small[:, 0, :3 * D_MODEL], small[:, 3, :3 * D_MODEL]])
    dmod_cols = lax.dynamic_slice_in_dim(dmod_all, me * ADA_SHARD, ADA_SHARD, axis=2)
    gwa = _w_ada_grad(c_all, dmod_cols).reshape(1, r_ada, ADA_SHARD)
    g_w_ada, d_w_ada, nm_w_ada, nv_w_ada = (
        t.reshape(w_ada.shape) for t in _adamw(flat(w_ada, r_ada), gwa, flat(m_w_ada, r_ada), flat(v_w_ada, r_ada),
                                               "adamw_w_ada", 256))

    def small_param(w, m, v, cols, row, name):
        n = w.shape[1]
        parts = jnp.stack([small[:, row, cols:cols + n], small[:, row + 3, cols:cols + n]], axis=1)
        return _adamw(w, parts, m, v, name, DEPTH)

    g_b_ada, d_b_ada, nm_b_ada, nv_b_ada = small_param(b_ada, m_b_ada, v_b_ada, 0, 0, "adamw_b_ada")
    g_g_pre, d_g_pre, nm_g_pre, nv_g_pre = small_param(g_pre, m_g_pre, v_g_pre, 0, 1, "adamw_g_pre")
    g_g_post, d_g_post, nm_g_post, nv_g_post = small_param(g_post, m_g_post, v_g_post, 1024, 1, "adamw_g_post")
    g_g_gla, d_g_gla, nm_g_gla, nv_g_gla = small_param(g_gla, m_g_gla, v_g_gla, 2048, 1, "adamw_g_gla")
    g_g_dil, d_g_dil, nm_g_dil, nv_g_dil = small_param(g_dil, m_g_dil, v_g_dil, 2560, 1, "adamw_g_dil")
    g_b_gu, d_b_gu, nm_b_gu, nv_b_gu = small_param(b_gate_up, m_b_gate_up, v_b_gate_up, 3072, 1, "adamw_b_gate_up")
    gu_parts = jnp.stack([small[:, 2], small[:, 5]], axis=1).reshape(N_DEV, DEPTH, GLA_LOWRANK, GU_COLS)
    gu_parts = lax.dynamic_slice_in_dim(gu_parts, me * GU_SHARD, GU_SHARD, axis=3).reshape(
        N_DEV, DEPTH * GLA_LOWRANK, GU_SHARD)
    r_gu = DEPTH * GLA_LOWRANK
    g_w_gu, d_w_gu, nm_w_gu, nv_w_gu = (
        t.reshape(w_gate_up.shape) for t in _adamw(flat(w_gate_up, r_gu), gu_parts, flat(m_w_gate_up, r_gu),
                                                   flat(v_w_gate_up, r_gu), "adamw_w_gate_up", r_gu))
    loss_parts = jnp.broadcast_to(small[:, 6, 0:1].reshape(N_DEV, 1, 1), (N_DEV, 8, LANE))
    loss = _sum_parts(loss_parts)[0, 0]

    return (loss, grad_x,
            g_w_ada, g_b_ada, g_g_pre, g_w_in, g_w_gu, g_b_gu, g_g_gla, g_g_dil, g_w_out, g_g_post,
            d_w_ada, d_b_ada, d_g_pre, d_w_in, d_w_gu, d_b_gu, d_g_gla, d_g_dil, d_w_out, d_g_post,
            nm_w_ada, nm_b_ada, nm_g_pre, nm_w_in, nm_w_gu, nm_b_gu, nm_g_gla, nm_g_dil, nm_w_out, nm_g_post,
            nv_w_ada, nv_b_ada, nv_g_pre, nv_w_in, nv_w_gu, nv_b_gu, nv_g_gla, nv_g_dil, nv_w_out, nv_g_post)


def _sum_parts(parts):
    n_parts = parts.shape[0]

    def body(p_ref, o_ref):
        acc = p_ref[0]
        for k in range(1, n_parts):
            acc = acc + p_ref[k]
        o_ref[...] = acc

    return pl.pallas_call(body, name="sum_loss", out_shape=jax.ShapeDtypeStruct(parts.shape[1:], F32))(parts)
```

```python
import functools
import math

import jax
import jax.numpy as jnp
from jax import lax
from jax.experimental import pallas as pl
from jax.experimental.pallas import tpu as pltpu

F32 = jnp.float32
BF16 = jnp.bfloat16

N_DEV = 8
D_MODEL = 1024
DEPTH = 2
GLA_HEADS = 4
GLA_DK = 64
GLA_DV = 128
GLA_CHUNK = 64
GLA_TAU = 16.0
GLA_LOWRANK = 16
DIL_HEADS = 4
DIL_HD = 128
DIL_BLOCK = 128
DIL_DILATIONS = (1, 4, 16)
ROPE_THETA = 10000.0
EPS = 1e-6
IN_COLS = 3600
W_IN_SHARD = IN_COLS // N_DEV
ADA_SHARD = 3 * D_MODEL // N_DEV
OUT_SHARD = D_MODEL // N_DEV
GU_COLS = GLA_HEADS * GLA_DK
GU_SHARD = GU_COLS // N_DEV

ADAM_LR = 0.001
ADAM_B1 = 0.9
ADAM_B2 = 0.999
ADAM_EPS = 1e-08
ADAM_WD = 0.01
ADAM_STEP = 10

NP = 3712
COL_Z, COL_QA, COL_KA, COL_QB, COL_KB, COL_VA, COL_VB, COL_LR = 0, 1024, 1280, 1536, 2048, 2560, 3072, 3584
NP_F32 = COL_VA
NP_BF16 = NP - NP_F32
LANE = 128
MASK_VALUE = -1e30

MESH = pl.DeviceIdType.MESH
ANY = pl.BlockSpec(memory_space=pl.ANY)


def _params(sem=None, vmem_mb=None):
    kw = {}
    if sem is not None:
        kw["dimension_semantics"] = sem
    if vmem_mb is not None:
        kw["vmem_limit_bytes"] = vmem_mb * 1024 * 1024
    return pltpu.CompilerParams(**kw)


def _dot(a, b):
    return jnp.dot(a, b, preferred_element_type=F32)


def _dot_nt(a, b):
    return lax.dot_general(a, b, (((1,), (1,)), ((), ())), preferred_element_type=F32)


def _dot_tn(a, b):
    return lax.dot_general(a, b, (((0,), (0,)), ((), ())), preferred_element_type=F32)


def _sigmoid(z):
    return 1.0 / (1.0 + jnp.exp(-z))


def _log_sigmoid(z):
    return jnp.minimum(z, 0.0) - jnp.log(1.0 + jnp.exp(-jnp.abs(z)))


def _rowvec(v, width=D_MODEL):
    arr, row, cb = v
    return arr.reshape(arr.shape[0], 1, arr.shape[1]), pl.BlockSpec((None, 1, width), lambda *_: (row, 0, cb))


def _my_position():
    return lax.axis_index("x"), lax.axis_index("y"), lax.axis_index("c")


def _linear(px, py, pc):
    return 4 * px + 2 * py + pc


def _gather_phase(phase, x_ref, out_ref, send_sem, recv_sem, local_sem):
    m = x_ref.shape[0]
    x, y, c = _my_position()
    me, sibling = (x, y, c), (x, y, 1 - c)
    chips = [(1 - x, y), (x, 1 - y), (1 - x, 1 - y)]

    def rows(px, py, pc):
        return out_ref.at[pl.ds(_linear(px, py, pc) * m, m), :]

    def copy(k, block, to, src=None):
        return pltpu.make_async_remote_copy(
            src_ref=rows(*block) if src is None else src, dst_ref=rows(*block),
            send_sem=send_sem(k), recv_sem=recv_sem(k), device_id=to, device_id_type=MESH)

    mine = pltpu.make_async_copy(x_ref, rows(*me), local_sem)
    first = [copy(0, me, sibling, src=x_ref)] + [copy(1 + j, me, (*chip, c), src=x_ref) for j, chip in enumerate(chips)]
    passed = [copy(4 + j, (*chip, c), sibling) for j, chip in enumerate(chips)]
    if phase == "start":
        mine.start()
        for cp in first:
            cp.start()
    elif phase == "forward":
        for j, chip in enumerate(chips):
            copy(1 + j, (*chip, c), me).wait_recv()
            passed[j].start()
    else:
        copy(0, sibling, me).wait_recv()
        for j, chip in enumerate(chips):
            copy(4 + j, (*chip, 1 - c), me).wait_recv()
        for cp in first + passed:
            cp.wait_send()
        mine.wait()


def _exchange_phase(phase, x_ref, out_ref, send_sem, recv_sem, local_sem):
    m = x_ref.shape[0] // N_DEV
    x, y, c = _my_position()
    me = _linear(x, y, c)

    def rows(ref, idx):
        return ref.at[pl.ds(idx * m, m), :]

    peers = [(1 - x if j & 4 else x, 1 - y if j & 2 else y, 1 - c if j & 1 else c) for j in range(1, N_DEV)]
    local = pltpu.make_async_copy(rows(x_ref, me), rows(out_ref, me), local_sem)
    sends = [pltpu.make_async_remote_copy(
        src_ref=rows(x_ref, _linear(*peer)), dst_ref=rows(out_ref, me),
        send_sem=send_sem(j), recv_sem=recv_sem(j), device_id=peer, device_id_type=MESH) for j, peer in enumerate(peers)]
    if phase == "start":
        local.start()
        for cp in sends:
            cp.start()
    else:
        for j, peer in enumerate(peers):
            pltpu.make_async_remote_copy(
                src_ref=rows(x_ref, _linear(*peer)), dst_ref=rows(out_ref, _linear(*peer)),
                send_sem=send_sem(j), recv_sem=recv_sem(j), device_id=peer, device_id_type=MESH).wait_recv()
        for cp in sends:
            cp.wait_send()
        local.wait()


_COMM_PHASES = {"gather": (_gather_phase, ("start", "forward", "finish")),
                "exchange": (_exchange_phase, ("start", "finish"))}


def _comm_scratch(n_arrays):
    return [pltpu.SemaphoreType.DMA((n_arrays, 7)), pltpu.SemaphoreType.DMA((n_arrays, 7)),
            pltpu.SemaphoreType.DMA((n_arrays,))]


def _comm_run(kind, phases, x_refs, out_refs, send_sems, recv_sems, local_sems):
    fn = _COMM_PHASES[kind][0]
    for phase in phases:
        for a, (x_ref, out_ref) in enumerate(zip(x_refs, out_refs)):
            fn(phase, x_ref, out_ref, lambda k, a=a: send_sems.at[a, k], lambda k, a=a: recv_sems.at[a, k],
               local_sems.at[a])


def _comm_out_shapes(kind, arrays):
    return [jax.ShapeDtypeStruct((N_DEV * a.shape[0], a.shape[1]) if kind == "gather" else a.shape, a.dtype)
            for a in arrays]


def _comm_call(kind, arrays, name):
    n = len(arrays)

    def body(*refs):
        _comm_run(kind, _COMM_PHASES[kind][1], refs[:n], refs[n:2 * n], *refs[2 * n:])

    return pl.pallas_call(body, name=name, out_shape=_comm_out_shapes(kind, arrays), in_specs=[ANY] * n,
                          out_specs=[ANY] * n, scratch_shapes=_comm_scratch(n))(*arrays)


def _all_gather(xs, name):
    return _comm_call("gather", [xs], name)[0]


def _all_to_all(xs, name):
    return _comm_call("exchange", [xs], name)[0]


def _mod_fwd(c_all, w_ada):
    def body(c_ref, w_ref, o_ref):
        cv = c_ref[...]
        sc = cv * _sigmoid(cv)
        o_ref[0] = _dot(sc.astype(BF16), w_ref[0].astype(BF16))

    return pl.pallas_call(
        body, name="mod_fwd", grid=(DEPTH,),
        out_shape=jax.ShapeDtypeStruct((DEPTH, N_DEV, ADA_SHARD), F32),
        in_specs=[pl.BlockSpec((N_DEV, D_MODEL), lambda l: (0, 0)),
                  pl.BlockSpec((1, D_MODEL, ADA_SHARD), lambda l: (l, 0, 0))],
        out_specs=pl.BlockSpec((1, N_DEV, ADA_SHARD), lambda l: (l, 0, 0)),
        compiler_params=_params(("arbitrary",)),
    )(c_all, w_ada)


def _w_ada_grad(c_all, dmod_cols):
    def body(c_ref, d_ref, o_ref):
        cv = c_ref[...]
        sc = cv * _sigmoid(cv)
        o_ref[0] = lax.dot_general(sc, d_ref[0], (((0,), (0,)), ((), ())), precision=lax.Precision.HIGHEST,
                                   preferred_element_type=F32)

    return pl.pallas_call(
        body, name="w_ada_grad", grid=(DEPTH,),
        out_shape=jax.ShapeDtypeStruct((DEPTH, D_MODEL, ADA_SHARD), F32),
        in_specs=[pl.BlockSpec((N_DEV, D_MODEL), lambda l: (0, 0)),
                  pl.BlockSpec((1, N_DEV, ADA_SHARD), lambda l: (l, 0, 0))],
        out_specs=pl.BlockSpec((1, D_MODEL, ADA_SHARD), lambda l: (l, 0, 0)),
        compiler_params=_params(("arbitrary",)),
    )(c_all, dmod_cols)


def _prenorm_proj(x, g_pre, scale, shift, w_new, ts=256):
    s_len = x.shape[0]

    def body(x_ref, g_ref, sc_ref, sh_ref, w_ref, pf_ref, pb_ref, h_ref):
        xv = x_ref[...]
        rstd = lax.rsqrt(jnp.mean(xv * xv, axis=-1, keepdims=True) + EPS)
        h = (xv * rstd * g_ref[...]) * (1.0 + sc_ref[...]) + sh_ref[...]
        hb = h.astype(BF16)
        h_ref[...] = hb
        for j in range(0, NP, 512):
            w = min(512, NP - j)
            acc = _dot(hb, w_ref[:, j:j + w])
            if j < NP_F32:
                pf_ref[:, j:j + w] = acc
            else:
                pb_ref[:, j - NP_F32:j - NP_F32 + w] = acc.astype(BF16)

    (g_pre, g_spec), (scale, sc_spec), (shift, sh_spec) = _rowvec(g_pre), _rowvec(scale), _rowvec(shift)
    return pl.pallas_call(
        body, name="prenorm_proj", grid=(s_len // ts,),
        out_shape=(jax.ShapeDtypeStruct((s_len, NP_F32), F32), jax.ShapeDtypeStruct((s_len, NP_BF16), BF16),
                   jax.ShapeDtypeStruct((s_len, D_MODEL), BF16)),
        in_specs=[pl.BlockSpec((ts, D_MODEL), lambda i: (i, 0)), g_spec, sc_spec, sh_spec,
                  pl.BlockSpec((D_MODEL, NP), lambda i: (0, 0))],
        out_specs=(pl.BlockSpec((ts, NP_F32), lambda i: (i, 0)), pl.BlockSpec((ts, NP_BF16), lambda i: (i, 0)),
                   pl.BlockSpec((ts, D_MODEL), lambda i: (i, 0))),
        compiler_params=_params(("arbitrary",), 48),
    )(x, g_pre, scale, shift, w_new)


GLA_GROUP = 4


def _gla_group_rows(t):
    return [pl.ds(pl.multiple_of((t * GLA_GROUP + j) * GLA_CHUNK, GLA_CHUNK), GLA_CHUNK) for j in range(GLA_GROUP)]


def _gla_chunks_common(q_ref, k_ref, lr_ref, wgu_ref, bgu_ref, rows_list):
    c = GLA_CHUNK
    ri = lax.broadcasted_iota(jnp.int32, (c, c), 0)
    ci = lax.broadcasted_iota(jnp.int32, (c, c), 1)
    tril = (ri >= ci).astype(F32)
    zs = [_dot(lr_ref[rows, :], wgu_ref[...]) + bgu_ref[...] for rows in rows_list]
    las = [_log_sigmoid(z) * (1.0 / GLA_TAU) for z in zs]
    bs = [jnp.dot(tril, la, precision=lax.Precision.HIGHEST, preferred_element_type=F32) for la in las]
    out = []
    for rows, z, b in zip(rows_list, zs, bs):
        q = q_ref[rows, :] * (GLA_DK ** -0.5)
        k = k_ref[rows, :]
        bl = b[c - 1:c, :]
        out.append(dict(z=z, b=b, bl=bl, qe=q * jnp.exp(b), ke=k * jnp.exp(-b), kend=k * jnp.exp(bl - b),
                        dec=jnp.exp(bl)))
    return out, ri, ci


def _head_lane_mask(hh):
    return (lax.broadcasted_iota(jnp.int32, (1, LANE), 1) // GLA_DK) == hh


def _state_block_mask():
    r = lax.broadcasted_iota(jnp.int32, (2 * GLA_DV, LANE), 0) // GLA_DV
    cc = lax.broadcasted_iota(jnp.int32, (2 * GLA_DV, LANE), 1) // GLA_DK
    return r == cc


def _gla_fwd(pf, pb, wgu, bgu, layer):
    s_len = pf.shape[0]
    nc = s_len // GLA_CHUNK

    def body(q_ref, k_ref, v_ref, lr_ref, wgu_ref, bgu_ref, o_ref, st_ref, qe_s, cs_s, dec_s):
        bd = _state_block_mask()

        def local(t, carry):
            rows_list = _gla_group_rows(t)
            cm, ri, ci = _gla_chunks_common(q_ref, k_ref, lr_ref, wgu_ref, bgu_ref, rows_list)
            vs = [v_ref[rows, :] for rows in rows_list]
            kebs = [c["ke"].astype(BF16) for c in cm]
            a = [[jnp.where(ri >= ci, _dot_nt(jnp.where(_head_lane_mask(hh), c["qe"], 0.0).astype(BF16), keb), 0.0)
                  .astype(BF16) for hh in range(2)] for c, keb in zip(cm, kebs)]
            oi = [[_dot(ah[hh], v[:, hh * GLA_DV:(hh + 1) * GLA_DV]) for hh in range(2)] for ah, v in zip(a, vs)]
            cs = [jnp.where(bd, _dot_tn(v, c["kend"].astype(BF16)), 0.0) for c, v in zip(cm, vs)]
            for j, (rows, c) in enumerate(zip(rows_list, cm)):
                n = t * GLA_GROUP + j
                o_ref[rows, :] = jnp.concatenate(oi[j], axis=1)
                qe_s[rows, :] = c["qe"].astype(BF16)
                cs_s[n] = cs[j]
                dec_s[n] = jnp.broadcast_to(c["dec"], (8, LANE))
            return carry

        lax.fori_loop(0, nc // GLA_GROUP, local, 0)

        def scan(n, st):
            st_ref[0, n] = st.astype(BF16)
            return dec_s[n][0:1, :] * st + cs_s[n]

        lax.fori_loop(0, nc, scan, jnp.zeros((2 * GLA_DV, LANE), F32))

        def inter(t, carry):
            rows_list = _gla_group_rows(t)
            add = [_dot_nt(qe_s[rows, :], st_ref[0, t * GLA_GROUP + j]) for j, rows in enumerate(rows_list)]
            for rows, av in zip(rows_list, add):
                o_ref[rows, :] = o_ref[rows, :] + av
            return carry

        lax.fori_loop(0, nc // GLA_GROUP, inter, 0)

    return pl.pallas_call(
        body, name="gla_fwd", grid=(2,),
        out_shape=(jax.ShapeDtypeStruct((s_len, GLA_HEADS * GLA_DV), F32),
                   jax.ShapeDtypeStruct((2, nc, 2 * GLA_DV, LANE), BF16)),
        in_specs=[pl.BlockSpec((s_len, LANE), lambda g: (0, COL_QA // LANE + g)),
                  pl.BlockSpec((s_len, LANE), lambda g: (0, COL_KA // LANE + g)),
                  pl.BlockSpec((s_len, 2 * GLA_DV), lambda g: (0, (COL_VA - NP_F32) // (2 * GLA_DV) + g)),
                  pl.BlockSpec((s_len, LANE), lambda g: (0, (COL_LR - NP_F32) // LANE)),
                  pl.BlockSpec((None, LANE, LANE), lambda g: (layer, 0, g)),
                  pl.BlockSpec((None, 1, LANE), lambda g: (layer, 0, g))],
        out_specs=(pl.BlockSpec((s_len, 2 * GLA_DV), lambda g: (0, g)),
                   pl.BlockSpec((1, nc, 2 * GLA_DV, LANE), lambda g: (g, 0, 0, 0))),
        scratch_shapes=[pltpu.VMEM((s_len, LANE), BF16), pltpu.VMEM((nc, 2 * GLA_DV, LANE), F32),
                        pltpu.VMEM((nc, 8, LANE), F32)],
        compiler_params=_params(("arbitrary",), 56),
    )(pf, pf, pb, pb, wgu, bgu.reshape(bgu.shape[0], 1, GU_COLS))


def _rope_tables(s_len):
    inv_freq = ROPE_THETA ** (-jnp.arange(0, DIL_HD, 2, dtype=F32) / DIL_HD)
    ang = jnp.arange(s_len, dtype=F32)[:, None] * inv_freq[None, :]
    cos, sin = jnp.cos(ang), jnp.sin(ang)
    return jnp.concatenate([cos, cos], axis=1), jnp.concatenate([-sin, sin], axis=1)


def _rope(xv, cos, sin_signed):
    return xv * cos + pltpu.roll(xv, DIL_HD // 2, 1) * sin_signed


DIL_GROUP = 4


def _dil_pair_block(i, half, d, nblk):
    nb = nblk // d
    j = i + half * (nblk // DIL_GROUP)
    if nb >= 2 * DIL_GROUP:
        r, n = j % d, j // d
    else:
        r, n = j // nb, j % nb
    kb = jnp.maximum(n - 1, 0)
    qs = r + d * DIL_BLOCK * n
    ks = r + d * DIL_BLOCK * kb
    return qs, ks, jnp.minimum(n, 1)


def _dil_fill_bias(bias):
    qi = lax.broadcasted_iota(jnp.int32, (DIL_BLOCK, 2 * DIL_BLOCK), 0)
    kj = lax.broadcasted_iota(jnp.int32, (DIL_BLOCK, 2 * DIL_BLOCK), 1)
    for sel in range(2):
        dist = qi - kj + DIL_BLOCK * sel
        bias[sel] = jnp.where((dist >= 0) & (dist <= DIL_BLOCK), 0.0, MASK_VALUE)


def _strided(start, size, d):
    return pl.ds(start, size) if d == 1 else pl.ds(start, size, stride=d)


def _comm_hooks(comm, cin, cout, csem, steps=DIL_HEADS):
    def before():
        if comm:
            @pl.when(pl.program_id(0) == 0)
            def _():
                _comm_run(comm[0], ("start",), cin, cout, *csem)

            if comm[0] == "gather":
                @pl.when(pl.program_id(0) == steps - 1)
                def _():
                    _comm_run(comm[0], ("forward",), cin, cout, *csem)

    def after():
        if comm:
            @pl.when(pl.program_id(0) == steps - 1)
            def _():
                _comm_run(comm[0], ("finish",), cin, cout, *csem)

    return before, after


def _dil_fwd(pf, pb, cos, sin_signed, comm=None):
    s_len = pf.shape[0]
    nblk = s_len // DIL_BLOCK
    prep_rows = 256
    scale = DIL_HD ** -0.5
    nc = len(comm[1]) if comm else 0

    def body(*refs):
        q_ref, k_ref, v_ref, cos_ref, sin_ref = refs[:5]
        cin, (o_ref, lse_ref), cout = refs[5:5 + nc], refs[5 + nc:7 + nc], refs[7 + nc:7 + 2 * nc]
        qf, kf, vf, o0, o1, o2, l0, l1, l2, bias = refs[7 + 2 * nc:17 + 2 * nc]
        comm_before, comm_after = _comm_hooks(comm, cin, cout, refs[17 + 2 * nc:])
        comm_before()
        _dil_fill_bias(bias)

        def prep(t, carry):
            rows = pl.ds(pl.multiple_of(t * prep_rows, prep_rows), prep_rows)
            cs, sn = cos_ref[rows, :], sin_ref[rows, :]
            qf[rows, :] = _rope(q_ref[rows, :], cs, sn)
            kf[rows, :] = _rope(k_ref[rows, :], cs, sn)
            vf[rows, :] = v_ref[rows, :].astype(F32)
            return carry

        lax.fori_loop(0, s_len // prep_rows, prep, 0)
        ones = jnp.ones((2 * DIL_BLOCK, DIL_HD), BF16)

        for d, o_p, l_p in zip(DIL_DILATIONS, (o0, o1, o2), (l0, l1, l2)):
            def pair(i, carry, d=d, o_p=o_p, l_p=l_p):
                idx = [_dil_pair_block(i, half, d, nblk) for half in range(DIL_GROUP)]
                ld = [(qf[_strided(qs, DIL_BLOCK, d), :].astype(BF16),
                       kf[_strided(ks, 2 * DIL_BLOCK, d), :].astype(BF16),
                       vf[_strided(ks, 2 * DIL_BLOCK, d), :].astype(BF16)) for qs, ks, _ in idx]
                s = [_dot_nt(qb, kk) * scale + bias[sel] for (qb, kk, _), (_, _, sel) in zip(ld, idx)]
                m = [jnp.max(sv, axis=-1, keepdims=True) for sv in s]
                p = [jnp.exp(sv - mv) for sv, mv in zip(s, m)]
                hi = [pv.astype(BF16) for pv in p]
                lo = [(pv - hv.astype(F32)).astype(BF16) for pv, hv in zip(p, hi)]
                r = [_dot(hv, jnp.concatenate([vv, ones], axis=1)) for hv, (_, _, vv) in zip(hi, ld)]
                r2 = [_dot(lv, ones) for lv in lo]
                for rv, r2v, mv, (qs, _, _) in zip(r, r2, m, idx):
                    den = rv[:, DIL_HD:] + r2v
                    o_p[_strided(qs, DIL_BLOCK, d), :] = rv[:, :DIL_HD] / den
                    l_p[_strided(qs, DIL_BLOCK, d), :] = mv + jnp.log(den)
                return carry

            lax.fori_loop(0, nblk // DIL_GROUP, pair, 0)

        def comb(t, carry):
            rows = pl.ds(pl.multiple_of(t * prep_rows, prep_rows), prep_rows)
            a0, a1, a2 = l0[rows, :], l1[rows, :], l2[rows, :]
            m = jnp.maximum(jnp.maximum(a0, a1), a2)
            e0, e1, e2 = jnp.exp(a0 - m), jnp.exp(a1 - m), jnp.exp(a2 - m)
            tot = e0 + e1 + e2
            o_ref[rows, :] = (e0 * o0[rows, :] + e1 * o1[rows, :] + e2 * o2[rows, :]) / tot
            lse_ref[rows, :] = m + jnp.log(tot)
            return carry

        lax.fori_loop(0, s_len // prep_rows, comb, 0)
        comm_after()

    head = lambda base: pl.BlockSpec((s_len, DIL_HD), lambda h: (0, base // DIL_HD + h))
    table = pl.BlockSpec((s_len, DIL_HD), lambda h: (0, 0))
    out = pl.BlockSpec((s_len, DIL_HD), lambda h: (0, h))
    shp = jax.ShapeDtypeStruct((s_len, DIL_HEADS * DIL_HD), F32)
    return pl.pallas_call(
        body, name="dil_fwd_comm" if comm else "dil_fwd", grid=(DIL_HEADS,),
        out_shape=[shp, shp] + (_comm_out_shapes(*comm) if comm else []),
        in_specs=[head(COL_QB), head(COL_KB), head(COL_VB - NP_F32), table, table] + [ANY] * nc,
        out_specs=[out, out] + [ANY] * nc,
        scratch_shapes=[pltpu.VMEM((s_len, DIL_HD), F32) for _ in range(9)]
        + [pltpu.VMEM((2, DIL_BLOCK, 2 * DIL_BLOCK), F32)] + (_comm_scratch(nc) if comm else []),
        compiler_params=_params(("arbitrary",), 56),
    )(pf, pf, pb, cos, sin_signed, *(comm[1] if comm else []))


def _silu_and_grad(z):
    sg = _sigmoid(z)
    return z * sg, sg * (1.0 + z * (1.0 - sg))


def _post_fwd(o_a, o_b, pf, g_heads, w_out, x, gate, g_post, ts=256):
    s_len = x.shape[0]
    half = GLA_HEADS * GLA_DV

    def body(oa_ref, ob_ref, z_ref, gh_ref, w_ref, x_ref, gate_ref, gp_ref, xo_ref, y_ref, u_ref):
        for src, base in ((oa_ref, 0), (ob_ref, half)):
            for hh in range(4):
                lo = hh * LANE
                og = src[:, lo:lo + LANE]
                on = og * lax.rsqrt(jnp.mean(og * og, axis=-1, keepdims=True) + EPS)
                zg = z_ref[:, base + lo:base + lo + LANE].astype(F32)
                y_ref[:, base + lo:base + lo + LANE] = (on * gh_ref[:, base + lo:base + lo + LANE]
                                                        * (zg * _sigmoid(zg))).astype(BF16)
        u = _dot(y_ref[...], w_ref[...])
        u_ref[...] = u.astype(BF16)
        rstd = lax.rsqrt(jnp.mean(u * u, axis=-1, keepdims=True) + EPS)
        xo_ref[...] = x_ref[...] + gate_ref[...] * (u * rstd * gp_ref[...])

    (g_heads, gh_spec), (gate, gate_spec), (g_post, gp_spec) = _rowvec(g_heads), _rowvec(gate), _rowvec(g_post)
    tile = pl.BlockSpec((ts, D_MODEL), lambda i: (i, 0))
    halft = pl.BlockSpec((ts, half), lambda i: (i, 0))
    return pl.pallas_call(
        body, name="post_fwd", grid=(s_len // ts,),
        out_shape=(jax.ShapeDtypeStruct((s_len, D_MODEL), F32), jax.ShapeDtypeStruct((s_len, D_MODEL), BF16),
                   jax.ShapeDtypeStruct((s_len, D_MODEL), BF16)),
        in_specs=[halft, halft, tile, gh_spec, pl.BlockSpec((D_MODEL, D_MODEL), lambda i: (0, 0)), tile, gate_spec,
                  gp_spec],
        out_specs=(tile, tile, tile),
        compiler_params=_params(("arbitrary",), 40),
    )(o_a, o_b, pf, g_heads, w_out, x, gate, g_post)


def _loss_grad(y, target, ts=512):
    s_len = y.shape[0]

    def body(y_ref, t_ref, dy_ref, loss_ref):
        @pl.when(pl.program_id(0) == 0)
        def _():
            loss_ref[...] = jnp.zeros_like(loss_ref)

        e = y_ref[...] - t_ref[...]
        dy_ref[...] = e * (1.0 / D_MODEL)
        loss_ref[...] += 0.5 * jnp.sum(jnp.mean(e * e, axis=-1, keepdims=True))

    tile = pl.BlockSpec((ts, D_MODEL), lambda i: (i, 0))
    return pl.pallas_call(
        body, name="loss_grad", grid=(s_len // ts,),
        out_shape=(jax.ShapeDtypeStruct((s_len, D_MODEL), F32), jax.ShapeDtypeStruct((8, LANE), F32)),
        in_specs=[tile, tile], out_specs=(tile, pl.BlockSpec((8, LANE), lambda i: (0, 0))),
        compiler_params=_params(("arbitrary",)),
    )(y, target)


def _post_bwd(dxo, u, gate, g_post, w_out, o_a, o_b, pf, g_heads, ts=256):
    s_len = dxo.shape[0]
    half = GLA_HEADS * GLA_DV

    def body(dx_ref, u_ref, gate_ref, gp_ref, w_ref, oa_ref, ob_ref, z_ref, gh_ref, du_ref, do_ref, dz_ref, sums_ref):
        @pl.when(pl.program_id(0) == 0)
        def _():
            sums_ref[...] = jnp.zeros_like(sums_ref)

        dx = dx_ref[...]
        u = u_ref[...].astype(F32)
        rstd = lax.rsqrt(jnp.mean(u * u, axis=-1, keepdims=True) + EPS)
        un = u * rstd
        sums_ref[0:1, :] += jnp.sum(dx * (un * gp_ref[...]), axis=0, keepdims=True)
        drn = dx * gate_ref[...]
        sums_ref[1:2, :] += jnp.sum(drn * un, axis=0, keepdims=True)
        dun = drn * gp_ref[...]
        du = rstd * (dun - un * jnp.mean(dun * un, axis=-1, keepdims=True))
        dub = du.astype(BF16)
        du_ref[...] = dub
        dy = _dot_nt(dub, w_ref[...])
        for src, base in ((oa_ref, 0), (ob_ref, half)):
            for hh in range(4):
                lo = base + hh * LANE
                og = src[:, hh * LANE:(hh + 1) * LANE]
                rs = lax.rsqrt(jnp.mean(og * og, axis=-1, keepdims=True) + EPS)
                on = og * rs
                zg = z_ref[:, lo:lo + LANE].astype(F32)
                sz, dsz = _silu_and_grad(zg)
                gg = gh_ref[:, lo:lo + LANE]
                dyg = dy[:, lo:lo + LANE]
                sums_ref[2:3, lo:lo + LANE] += jnp.sum(dyg * sz * on, axis=0, keepdims=True)
                dz_ref[:, lo:lo + LANE] = (dyg * on * gg * dsz).astype(BF16)
                don = dyg * gg * sz
                do_ref[:, lo:lo + LANE] = (rs * (don - on * jnp.mean(don * on, axis=-1, keepdims=True))).astype(BF16)

    (g_heads, gh_spec), (gate, gate_spec), (g_post, gp_spec) = _rowvec(g_heads), _rowvec(gate), _rowvec(g_post)
    tile = pl.BlockSpec((ts, D_MODEL), lambda i: (i, 0))
    halft = pl.BlockSpec((ts, half), lambda i: (i, 0))
    return pl.pallas_call(
        body, name="post_bwd", grid=(s_len // ts,),
        out_shape=(jax.ShapeDtypeStruct((s_len, D_MODEL), BF16), jax.ShapeDtypeStruct((s_len, D_MODEL), BF16),
                   jax.ShapeDtypeStruct((s_len, D_MODEL), BF16), jax.ShapeDtypeStruct((8, D_MODEL), F32)),
        in_specs=[tile, tile, gate_spec, gp_spec, pl.BlockSpec((D_MODEL, D_MODEL), lambda i: (0, 0)), halft, halft,
                  tile, gh_spec],
        out_specs=(tile, tile, tile, pl.BlockSpec((8, D_MODEL), lambda i: (0, 0))),
        compiler_params=_params(("arbitrary",), 40),
    )(dxo, u, gate, g_post, w_out, o_a, o_b, pf, g_heads)


def _gla_bwd(pf, pb, wgu, bgu, layer, states, do):
    s_len = pf.shape[0]
    nc = s_len // GLA_CHUNK
    c = GLA_CHUNK

    def body(q_ref, k_ref, v_ref, lr_ref, wgu_ref, bgu_ref, st_ref, do_ref,
             dq_ref, dk_ref, dv_ref, dlr_ref, dwgu_ref, dbgu_ref, ds_s, dec_s, dw_acc, db_acc):
        dw_acc[...] = jnp.zeros_like(dw_acc)
        db_acc[...] = jnp.zeros_like(db_acc)
        bd = _state_block_mask()
        last_row = lax.broadcasted_iota(jnp.int32, (c, LANE), 0) == c - 1

        def local(t, carry):
            rows_list = _gla_group_rows(t)
            cm, _, _ = _gla_chunks_common(q_ref, k_ref, lr_ref, wgu_ref, bgu_ref, rows_list)
            loc = [jnp.where(bd, _dot_tn(do_ref[rows, :], cc["qe"].astype(BF16)), 0.0)
                   for rows, cc in zip(rows_list, cm)]
            for j, cc in enumerate(cm):
                ds_s[t * GLA_GROUP + j] = loc[j]
                dec_s[t * GLA_GROUP + j] = jnp.broadcast_to(cc["dec"], (8, LANE))
            return carry

        lax.fori_loop(0, nc // GLA_GROUP, local, 0)

        def scan(t, dst):
            n = nc - 1 - t
            loc = ds_s[n]
            ds_s[n] = dst
            return dec_s[n][0:1, :] * dst + loc

        lax.fori_loop(0, nc, scan, jnp.zeros((2 * GLA_DV, LANE), F32))

        def rest(t, carry):
            rows_list = _gla_group_rows(t)
            cm, ri, ci = _gla_chunks_common(q_ref, k_ref, lr_ref, wgu_ref, bgu_ref, rows_list)
            ns = [t * GLA_GROUP + j for j in range(GLA_GROUP)]
            vs = [v_ref[rows, :] for rows in rows_list]
            dobs = [do_ref[rows, :] for rows in rows_list]
            stbs = [st_ref[0, n] for n in ns]
            dsts = [ds_s[n] for n in ns]
            dstbs = [d.astype(BF16) for d in dsts]
            qebs = [cc["qe"].astype(BF16) for cc in cm]
            kebs = [cc["ke"].astype(BF16) for cc in cm]
            kendbs = [cc["kend"].astype(BF16) for cc in cm]
            hms = [_head_lane_mask(hh) for hh in range(2)]
            qehs = [[jnp.where(hm, cc["qe"], 0.0).astype(BF16) for hm in hms] for cc in cm]
            kehs = [[jnp.where(hm, cc["ke"], 0.0).astype(BF16) for hm in hms] for cc in cm]
            heads = lambda x: [x[:, hh * GLA_DV:(hh + 1) * GLA_DV] for hh in range(2)]
            vhs, dohs = [heads(v) for v in vs], [heads(d) for d in dobs]

            dqe0 = [_dot(dob, stb) for dob, stb in zip(dobs, stbs)]
            dkend = [_dot(v, dstb) for v, dstb in zip(vs, dstbs)]
            dv0 = [_dot_nt(kb, dstb) for kb, dstb in zip(kendbs, dstbs)]
            a_t = [[jnp.where(ci >= ri, _dot_nt(kehs[j][hh], qebs[j]), 0.0).astype(BF16) for hh in range(2)]
                   for j in range(GLA_GROUP)]
            da = [[jnp.where(ri >= ci, _dot_nt(dohs[j][hh], vhs[j][hh]), 0.0).astype(BF16) for hh in range(2)]
                  for j in range(GLA_GROUP)]
            da_t = [[jnp.where(ci >= ri, _dot_nt(vhs[j][hh], dohs[j][hh]), 0.0).astype(BF16) for hh in range(2)]
                    for j in range(GLA_GROUP)]
            dv1 = [[_dot(a_t[j][hh], dohs[j][hh]) for hh in range(2)] for j in range(GLA_GROUP)]
            dqe1 = [[_dot(da[j][hh], kebs[j]) for hh in range(2)] for j in range(GLA_GROUP)]
            dke1 = [[_dot(da_t[j][hh], qehs[j][hh]) for hh in range(2)] for j in range(GLA_GROUP)]

            dbs, dzs = [], []
            for j, (rows, cc) in enumerate(zip(rows_list, cm)):
                qe, ke, kend, b, bl = cc["qe"], cc["ke"], cc["kend"], cc["b"], cc["bl"]
                dqe = dqe0[j] + jnp.where(hms[0], dqe1[j][0], 0.0) + jnp.where(hms[1], dqe1[j][1], 0.0)
                dke = jnp.where(hms[0], dke1[j][0], 0.0) + jnp.where(hms[1], dke1[j][1], 0.0)
                dv_ref[rows, :] = (dv0[j] + jnp.concatenate(dv1[j], axis=1)).astype(BF16)
                dq_ref[rows, :] = (dqe * jnp.exp(b) * (GLA_DK ** -0.5)).astype(BF16)
                dk_ref[rows, :] = (dke * jnp.exp(-b) + dkend[j] * jnp.exp(bl - b)).astype(BF16)
                ddec = jnp.sum(dsts[j] * stbs[j].astype(F32), axis=0, keepdims=True)
                dbl = jnp.sum(dkend[j] * kend, axis=0, keepdims=True) + ddec * cc["dec"]
                dbs.append(dqe * qe - dke * ke - dkend[j] * kend + jnp.where(last_row, dbl, 0.0))
            triu = (ci >= ri).astype(F32)
            dlas = [jnp.dot(triu, db, precision=lax.Precision.HIGHEST, preferred_element_type=F32) for db in dbs]
            dzs = [dla * (1.0 / GLA_TAU) * _sigmoid(-cc["z"]) for dla, cc in zip(dlas, cm)]
            dzbs = [dz.astype(BF16) for dz in dzs]
            dlrs = [_dot_nt(dzb, wgu_ref[...]) for dzb in dzbs]
            dws = [_dot_tn(lr_ref[rows, :], dzb) for rows, dzb in zip(rows_list, dzbs)]
            for rows, dlr in zip(rows_list, dlrs):
                dlr_ref[0, rows, :] = dlr
            dw_acc[...] += functools.reduce(lambda x, y: x + y, dws)
            db_acc[0:1, :] += jnp.sum(functools.reduce(lambda x, y: x + y, dzs), axis=0, keepdims=True)
            return carry

        lax.fori_loop(0, nc // GLA_GROUP, rest, 0)
        dwgu_ref[...] = dw_acc[...]
        dbgu_ref[...] = db_acc[...]

    pair = pl.BlockSpec((s_len, LANE), lambda g: (0, g))
    return pl.pallas_call(
        body, name="gla_bwd", grid=(2,),
        out_shape=(jax.ShapeDtypeStruct((s_len, GU_COLS), BF16), jax.ShapeDtypeStruct((s_len, GU_COLS), BF16),
                   jax.ShapeDtypeStruct((s_len, GLA_HEADS * GLA_DV), BF16),
                   jax.ShapeDtypeStruct((2, s_len, LANE), F32),
                   jax.ShapeDtypeStruct((LANE, GU_COLS), F32), jax.ShapeDtypeStruct((8, GU_COLS), F32)),
        in_specs=[pl.BlockSpec((s_len, LANE), lambda g: (0, COL_QA // LANE + g)),
                  pl.BlockSpec((s_len, LANE), lambda g: (0, COL_KA // LANE + g)),
                  pl.BlockSpec((s_len, 2 * GLA_DV), lambda g: (0, (COL_VA - NP_F32) // (2 * GLA_DV) + g)),
                  pl.BlockSpec((s_len, LANE), lambda g: (0, (COL_LR - NP_F32) // LANE)),
                  pl.BlockSpec((None, LANE, LANE), lambda g: (layer, 0, g)),
                  pl.BlockSpec((None, 1, LANE), lambda g: (layer, 0, g)),
                  pl.BlockSpec((1, nc, 2 * GLA_DV, LANE), lambda g: (g, 0, 0, 0)),
                  pl.BlockSpec((s_len, 2 * GLA_DV), lambda g: (0, g))],
        out_specs=(pair, pair, pl.BlockSpec((s_len, 2 * GLA_DV), lambda g: (0, g)),
                   pl.BlockSpec((1, s_len, LANE), lambda g: (g, 0, 0)),
                   pl.BlockSpec((LANE, LANE), lambda g: (0, g)), pl.BlockSpec((8, LANE), lambda g: (0, g))),
        scratch_shapes=[pltpu.VMEM((nc, 2 * GLA_DV, LANE), F32), pltpu.VMEM((nc, 8, LANE), F32),
                        pltpu.VMEM((LANE, LANE), F32), pltpu.VMEM((8, LANE), F32)],
        compiler_params=_params(("arbitrary",), 56),
    )(pf, pf, pb, pb, wgu, bgu.reshape(bgu.shape[0], 1, GU_COLS), states, do)


def _dil_bwd(pf, pb, cos, sin_signed, do, o_b, lse, comm=None):
    s_len = pf.shape[0]
    nblk = s_len // DIL_BLOCK
    prep_rows = 256
    scale = DIL_HD ** -0.5
    nc = len(comm[1]) if comm else 0

    def body(*refs):
        q_ref, k_ref, v_ref, cos_ref, sin_ref, do_ref, o_ref, lse_ref = refs[:8]
        cin, (dq_ref, dk_ref, dv_ref), cout = refs[8:8 + nc], refs[8 + nc:11 + nc], refs[11 + nc:11 + 2 * nc]
        qf, kf, vf, dof, dl, dqa, dka, dva, bias = refs[11 + 2 * nc:20 + 2 * nc]
        comm_before, comm_after = _comm_hooks(comm, cin, cout, refs[20 + 2 * nc:])
        comm_before()
        _dil_fill_bias(bias)

        def prep(t, carry):
            rows = pl.ds(pl.multiple_of(t * prep_rows, prep_rows), prep_rows)
            cs, sn = cos_ref[rows, :], sin_ref[rows, :]
            qf[rows, :] = _rope(q_ref[rows, :], cs, sn) * scale
            kf[rows, :] = _rope(k_ref[rows, :], cs, sn)
            vf[rows, :] = v_ref[rows, :].astype(F32)
            dov = do_ref[rows, :].astype(F32)
            dof[rows, :] = dov
            dl[rows, :] = jnp.broadcast_to(jnp.sum(dov * o_ref[rows, :], axis=-1, keepdims=True), (prep_rows, DIL_HD))
            zero = jnp.zeros((prep_rows, DIL_HD), F32)
            dqa[rows, :] = zero
            dka[rows, :] = zero
            dva[rows, :] = zero
            return carry

        lax.fori_loop(0, s_len // prep_rows, prep, 0)

        for d in DIL_DILATIONS:
            def pair(i, carry, d=d):
                idx = [_dil_pair_block(i, half, d, nblk) for half in range(DIL_GROUP)]
                rows = [(_strided(qs, DIL_BLOCK, d), _strided(ks, 2 * DIL_BLOCK, d)) for qs, ks, _ in idx]
                ld = [(qf[qr, :].astype(BF16), kf[kr, :].astype(BF16), vf[kr, :].astype(BF16),
                       dof[qr, :].astype(BF16)) for qr, kr in rows]
                s = [_dot_nt(qb, kk) + bias[sel] for (qb, kk, _, _), (_, _, sel) in zip(ld, idx)]
                dp = [_dot_nt(dob, vv) for _, _, vv, dob in ld]
                p = [jnp.exp(sv - lse_ref[qr, :][:, 0:1]) for sv, (qr, _) in zip(s, rows)]
                ds = [(pv * (dpv - dl[qr, :][:, 0:1])).astype(BF16) for pv, dpv, (qr, _) in zip(p, dp, rows)]
                pb = [pv.astype(BF16) for pv in p]
                gq = [_dot(dsv, kk) for dsv, (_, kk, _, _) in zip(ds, ld)]
                gk = [_dot_tn(dsv, qb) for dsv, (qb, _, _, _) in zip(ds, ld)]
                gv = [_dot_tn(pv, dob) for pv, (_, _, _, dob) in zip(pb, ld)]
                for (qr, kr), a, b, c in zip(rows, gq, gk, gv):
                    dqa[qr, :] += a
                    dka[kr, :] += b
                    dva[kr, :] += c
                return carry

            lax.fori_loop(0, nblk // DIL_GROUP, pair, 0)

        def fin(t, carry):
            rows = pl.ds(pl.multiple_of(t * prep_rows, prep_rows), prep_rows)
            cs, sn = cos_ref[rows, :], sin_ref[rows, :]
            gq, gk = dqa[rows, :] * scale, dka[rows, :]
            dq_ref[rows, :] = (gq * cs - pltpu.roll(gq, DIL_HD // 2, 1) * sn).astype(BF16)
            dk_ref[rows, :] = (gk * cs - pltpu.roll(gk, DIL_HD // 2, 1) * sn).astype(BF16)
            dv_ref[rows, :] = dva[rows, :].astype(BF16)
            return carry

        lax.fori_loop(0, s_len // prep_rows, fin, 0)
        comm_after()

    head = lambda base: pl.BlockSpec((s_len, DIL_HD), lambda h: (0, base // DIL_HD + h))
    table = pl.BlockSpec((s_len, DIL_HD), lambda h: (0, 0))
    out = pl.BlockSpec((s_len, DIL_HD), lambda h: (0, h))
    shp = jax.ShapeDtypeStruct((s_len, DIL_HEADS * DIL_HD), BF16)
    return pl.pallas_call(
        body, name="dil_bwd_comm" if comm else "dil_bwd", grid=(DIL_HEADS,),
        out_shape=[shp, shp, shp] + (_comm_out_shapes(*comm) if comm else []),
        in_specs=[head(COL_QB), head(COL_KB), head(COL_VB - NP_F32), table, table,
                  pl.BlockSpec((s_len, DIL_HD), lambda h: (0, DIL_HEADS + h)), out, out] + [ANY] * nc,
        out_specs=[out, out, out] + [ANY] * nc,
        scratch_shapes=[pltpu.VMEM((s_len, DIL_HD), F32) for _ in range(8)]
        + [pltpu.VMEM((2, DIL_BLOCK, 2 * DIL_BLOCK), F32)] + (_comm_scratch(nc) if comm else []),
        compiler_params=_params(("arbitrary",), 56),
    )(pf, pf, pb, cos, sin_signed, do, o_b, lse, *(comm[1] if comm else []))


_PIECES = ((COL_Z, 1024), (COL_QA, 256), (COL_KA, 256), (COL_QB, 512), (COL_KB, 512), (COL_VA, 512), (COL_VB, 512),
           (COL_LR, 128))


def _in_bwd(pieces, w_new, x, dxo, g_pre, scale, comm=None, ts=256):
    s_len = x.shape[0]
    nc = len(comm[1]) if comm else 0
    npc = len(_PIECES)

    def body(*refs):
        p_refs = refs[:npc]
        w_ref, x_ref, dxo_ref, g_ref, sc_ref = refs[npc:npc + 5]
        cin, (dx_ref, sums_ref), cout = (refs[npc + 5:npc + 5 + nc], refs[npc + 5 + nc:npc + 7 + nc],
                                         refs[npc + 7 + nc:npc + 7 + 2 * nc])
        comm_before, comm_after = _comm_hooks(comm, cin, cout, refs[npc + 7 + 2 * nc:], steps=s_len // ts)
        comm_before()

        @pl.when(pl.program_id(0) == 0)
        def _():
            sums_ref[...] = jnp.zeros_like(sums_ref)

        dh = jnp.zeros((ts, D_MODEL), F32)
        for p_ref, (col, width) in zip(p_refs, _PIECES):
            dh += _dot_nt(p_ref[...], w_ref[:, col:col + width])
        xv = x_ref[...]
        rstd = lax.rsqrt(jnp.mean(xv * xv, axis=-1, keepdims=True) + EPS)
        xn = xv * rstd
        sums_ref[0:1, :] += jnp.sum(dh, axis=0, keepdims=True)
        sums_ref[1:2, :] += jnp.sum(dh * (xn * g_ref[...]), axis=0, keepdims=True)
        dr = dh * (1.0 + sc_ref[...])
        sums_ref[2:3, :] += jnp.sum(dr * xn, axis=0, keepdims=True)
        dxn = dr * g_ref[...]
        dx_ref[...] = dxo_ref[...] + rstd * (dxn - xn * jnp.mean(dxn * xn, axis=-1, keepdims=True))
        comm_after()

    (g_pre, g_spec), (scale, sc_spec) = _rowvec(g_pre), _rowvec(scale)
    tile = pl.BlockSpec((ts, D_MODEL), lambda i: (i, 0))
    return pl.pallas_call(
        body, name="in_bwd_comm" if comm else "in_bwd", grid=(s_len // ts,),
        out_shape=[jax.ShapeDtypeStruct((s_len, D_MODEL), F32), jax.ShapeDtypeStruct((8, D_MODEL), F32)]
        + (_comm_out_shapes(*comm) if comm else []),
        in_specs=[pl.BlockSpec((ts, width), lambda i: (i, 0)) for _, width in _PIECES]
        + [pl.BlockSpec((D_MODEL, NP), lambda i: (0, 0)), tile, tile, g_spec, sc_spec] + [ANY] * nc,
        out_specs=[tile, pl.BlockSpec((8, D_MODEL), lambda i: (0, 0))] + [ANY] * nc,
        scratch_shapes=_comm_scratch(nc) if comm else [],
        compiler_params=_params(("arbitrary",), 48),
    )(*pieces, w_new, x, dxo, g_pre, scale, *(comm[1] if comm else []))


def _w_in_to_kernel(gathered, tr=128):
    def body(g_ref, o_ref):
        cols = jnp.concatenate([g_ref[k].astype(F32) for k in range(N_DEV)], axis=1)
        pad = jnp.zeros((tr, LANE - GLA_LOWRANK), F32)
        o_ref[...] = jnp.concatenate(
            [cols[:, 1024:1536], cols[:, 3088:3600], cols[:, 0:512], cols[:, 1552:2576], cols[:, 512:1024],
             cols[:, 2576:3088], cols[:, 1536:1552], pad], axis=1).astype(BF16)

    return pl.pallas_call(
        body, name="w_in_to_kernel", grid=(D_MODEL // tr,), out_shape=jax.ShapeDtypeStruct((D_MODEL, NP), BF16),
        in_specs=[pl.BlockSpec((N_DEV, tr, W_IN_SHARD), lambda i: (0, i, 0))],
        out_specs=pl.BlockSpec((tr, NP), lambda i: (i, 0)),
        compiler_params=_params(("arbitrary",)),
    )(gathered)


def _grad_w_in(h, pieces, ts=512, tr=128):
    s_len = h.shape[0]
    steps = s_len // ts

    def body(*refs):
        h_ref, p_refs = refs[0], refs[1:1 + len(_PIECES)]
        o_ref, acc = refs[1 + len(_PIECES):]

        @pl.when(pl.program_id(0) == 0)
        def _():
            acc[...] = jnp.zeros_like(acc)

        hv = h_ref[...]
        for p_ref, (col, width) in zip(p_refs, _PIECES):
            acc[:, col:col + width] += _dot_tn(hv, p_ref[...])

        @pl.when(pl.program_id(0) == steps - 1)
        def _():
            def rows_out(t, carry):
                rows = pl.ds(pl.multiple_of(t * tr, tr), tr)
                g = acc[rows, :]
                cols = jnp.concatenate(
                    [g[:, COL_QA:COL_QB], g[:, COL_VA:COL_VB], g[:, 0:512], g[:, COL_LR:COL_LR + GLA_LOWRANK],
                     g[:, COL_QB:COL_VA], g[:, COL_VB:COL_LR], g[:, 512:1024]], axis=1)
                for k in range(N_DEV):
                    o_ref[k, rows, :] = cols[:, W_IN_SHARD * k:W_IN_SHARD * (k + 1)].astype(BF16)
                return carry

            lax.fori_loop(0, D_MODEL // tr, rows_out, 0)

    return pl.pallas_call(
        body, name="grad_w_in", grid=(steps,),
        out_shape=jax.ShapeDtypeStruct((N_DEV, D_MODEL, W_IN_SHARD), BF16),
        in_specs=[pl.BlockSpec((ts, D_MODEL), lambda i: (i, 0))]
        + [pl.BlockSpec((ts, width), lambda i: (i, 0)) for _, width in _PIECES],
        out_specs=pl.BlockSpec((N_DEV, D_MODEL, W_IN_SHARD), lambda i: (0, 0, 0)),
        scratch_shapes=[pltpu.VMEM((D_MODEL, NP), F32)],
        compiler_params=_params(("arbitrary",), 56),
    )(h, *pieces)


def _matmul_tn(a, b, name, bn, ts=512):
    s_len, m = a.shape
    n = b.shape[1]
    steps = s_len // ts

    def body(a_ref, b_ref, o_ref, acc):
        @pl.when(pl.program_id(1) == 0)
        def _():
            acc[...] = jnp.zeros_like(acc)

        acc[...] += _dot_tn(a_ref[...], b_ref[...])

        @pl.when(pl.program_id(1) == steps - 1)
        def _():
            o_ref[...] = acc[...].astype(BF16)

    return pl.pallas_call(
        body, name=name, grid=(n // bn, steps),
        out_shape=jax.ShapeDtypeStruct((m, n), BF16),
        in_specs=[pl.BlockSpec((ts, m), lambda j, i: (i, 0)), pl.BlockSpec((ts, bn), lambda j, i: (i, j))],
        out_specs=pl.BlockSpec((m, bn), lambda j, i: (0, j)),
        scratch_shapes=[pltpu.VMEM((m, bn), F32)],
        compiler_params=_params(("arbitrary", "arbitrary"), 40),
    )(a, b)


def _adam_math(w, g, m, v):
    m = ADAM_B1 * m + (1.0 - ADAM_B1) * g
    v = ADAM_B2 * v + (1.0 - ADAM_B2) * (g * g)
    m_hat = m / (1.0 - ADAM_B1 ** ADAM_STEP)
    v_hat = v / (1.0 - ADAM_B2 ** ADAM_STEP)
    delta = -ADAM_LR * (m_hat / (jnp.sqrt(v_hat) + ADAM_EPS) + ADAM_WD * w)
    return delta, m, v


def _adamw(w, parts, m, v, name, tr):
    r, cdim = w.shape
    n_parts = parts.shape[0]

    def body(w_ref, p_ref, m_ref, v_ref, g_ref, d_ref, nm_ref, nv_ref):
        g = p_ref[0].astype(F32)
        for k in range(1, n_parts):
            g = g + p_ref[k].astype(F32)
        g_ref[...] = g
        d_ref[...], nm_ref[...], nv_ref[...] = _adam_math(w_ref[...], g, m_ref[...], v_ref[...])

    tile = pl.BlockSpec((tr, cdim), lambda i: (i, 0))
    shp = jax.ShapeDtypeStruct((r, cdim), F32)
    return pl.pallas_call(
        body, name=name, grid=(r // tr,), out_shape=(shp, shp, shp, shp),
        in_specs=[tile, pl.BlockSpec((n_parts, tr, cdim), lambda i: (0, i, 0)), tile, tile],
        out_specs=(tile, tile, tile, tile),
        compiler_params=_params(("arbitrary",), 40),
    )(w, parts, m, v)


def _adamw_layers(w, parts, m, v, name, tr):
    n_layers, r, cdim = w.shape

    def body(*refs):
        w_ref, p_refs, (m_ref, v_ref) = refs[0], refs[1:1 + n_layers], refs[1 + n_layers:3 + n_layers]
        g_ref, d_ref, nm_ref, nv_ref = refs[3 + n_layers:]
        for l, p_ref in enumerate(p_refs):
            @pl.when(pl.program_id(0) == l)
            def _(p_ref=p_ref):
                g = p_ref[0].astype(F32)
                for k in range(1, p_ref.shape[0]):
                    g = g + p_ref[k].astype(F32)
                g_ref[0] = g
                d_ref[0], nm_ref[0], nv_ref[0] = _adam_math(w_ref[0], g, m_ref[0], v_ref[0])

    tile = pl.BlockSpec((1, tr, cdim), lambda l, i: (l, i, 0))
    part = lambda own: pl.BlockSpec((parts[own].shape[0], tr, cdim), lambda l, i: (0, jnp.where(l == own, i, 0), 0))
    shp = jax.ShapeDtypeStruct(w.shape, F32)
    return pl.pallas_call(
        body, name=name, grid=(n_layers, r // tr), out_shape=(shp, shp, shp, shp),
        in_specs=[tile] + [part(l) for l in range(n_layers)] + [tile, tile],
        out_specs=(tile, tile, tile, tile),
        compiler_params=_params(("arbitrary", "arbitrary"), 40),
    )(w, *parts, m, v)


def _row(vec, width):
    vec = vec.reshape(1, -1)
    return jnp.pad(vec, ((0, 0), (0, width - vec.shape[1])))


def kernel(x, c, w_ada, b_ada, g_pre, w_in, w_gate_up, b_gate_up, g_gla, g_dil, w_out, g_post, loss_target, m_w_ada, m_b_ada, m_g_pre, m_w_in, m_w_gate_up, m_b_gate_up, m_g_gla, m_g_dil, m_w_out, m_g_post, v_w_ada, v_b_ada, v_g_pre, v_w_in, v_w_gate_up, v_b_gate_up, v_g_gla, v_g_dil, v_w_out, v_g_post):
    px, py, pc = _my_position()
    me = _linear(px, py, pc)
    xs = x[0]
    target = loss_target[0]
    s_len = xs.shape[0]
    assert s_len % (DIL_BLOCK * max(DIL_DILATIONS) * 2) == 0 and xs.shape[1] == D_MODEL

    c_all = _all_gather(jnp.pad(c, ((0, 7), (0, 0))), "gather_c").reshape(N_DEV, 8, D_MODEL)[:, 0]
    mod_part = _mod_fwd(c_all, w_ada)
    w_in_b, w_out_b = w_in.astype(BF16), w_out.astype(BF16)
    mod_all, wgu_all, w_in_all, w_out_all = _comm_call(
        "gather", [mod_part.reshape(DEPTH * N_DEV, ADA_SHARD), w_gate_up.reshape(DEPTH * GLA_LOWRANK, GU_SHARD),
                   w_in_b[0], w_out_b[0]], "gather_first")
    mod_all = mod_all.reshape(N_DEV, DEPTH, N_DEV, ADA_SHARD)
    mod_mine = lax.dynamic_index_in_dim(mod_all, me, axis=2, keepdims=False)
    mod = jnp.transpose(mod_mine, (1, 0, 2)).reshape(DEPTH, 3 * D_MODEL) + b_ada
    wgu_full = jnp.transpose(wgu_all.reshape(N_DEV, DEPTH, GLA_LOWRANK, GU_SHARD), (1, 2, 0, 3)).reshape(
        DEPTH, GLA_LOWRANK, GU_COLS)
    wgu_pad = jnp.pad(wgu_full, ((0, 0), (0, LANE - GLA_LOWRANK), (0, 0))).astype(BF16)

    def kernel_w_in(gathered):
        return _w_in_to_kernel(gathered.reshape(N_DEV, D_MODEL, W_IN_SHARD))

    cos, sin_signed = _rope_tables(s_len)
    g_heads = jnp.concatenate([g_gla, g_dil], axis=1)

    saved = []
    xl = xs
    for l in range(DEPTH):
        shift, scale, gate = ((mod, l, k) for k in range(3))
        w_new, w_out_l = kernel_w_in(w_in_all), w_out_all
        pf, pb, h = _prenorm_proj(xl, (g_pre, l, 0), scale, shift, w_new)
        o_a, states = _gla_fwd(pf, pb, wgu_pad, b_gate_up, l)
        if l + 1 < DEPTH:
            o_b, lse, w_in_all, w_out_all = _dil_fwd(pf, pb, cos, sin_signed,
                                                     comm=("gather", [w_in_b[l + 1], w_out_b[l + 1]]))
        else:
            o_b, lse = _dil_fwd(pf, pb, cos, sin_signed)
        x_next, y, u = _post_fwd(o_a, o_b, pf, (g_heads, l, 0), w_out_l, xl, gate, (g_post, l, 0))
        saved.append((xl, scale, gate, w_new, w_out_l, pf, pb, h, o_a, states, o_b, lse, y, u))
        xl = x_next

    dx, loss_part = _loss_grad(xl, target)

    small_rows = []
    gin_slots, gin_parts, gout_parts = None, [None] * DEPTH, [None] * DEPTH
    for l in reversed(range(DEPTH)):
        x_in, scale, gate, w_new, w_out_l, pf, pb, h, o_a, states, o_b, lse, y, u = saved[l]
        du, do, dz, sums_post = _post_bwd(dx, u, gate, (g_post, l, 0), w_out_l, o_a, o_b, pf, (g_heads, l, 0))
        gout_slots = _matmul_tn(y, du, "grad_w_out", 512)
        dq_a, dk_a, dv_a, dlr2, dwgu, dbgu = _gla_bwd(pf, pb, wgu_pad, b_gate_up, l, states, do)
        travelling = [gout_slots] + ([gin_slots] if gin_slots is not None else [])
        dq_b, dk_b, dv_b, *arrived = _dil_bwd(pf, pb, cos, sin_signed, do, o_b, lse, comm=("exchange", travelling))
        gout_parts[l] = arrived[0].reshape(N_DEV, OUT_SHARD, D_MODEL)
        if gin_slots is not None:
            gin_parts[l + 1] = arrived[1].reshape(N_DEV, D_MODEL, W_IN_SHARD)
        dlr = (dlr2[0] + dlr2[1]).astype(BF16)
        pieces = (dz, dq_a, dk_a, dq_b, dk_b, dv_a, dv_b, dlr)
        gin_slots = _grad_w_in(h, pieces).reshape(N_DEV * D_MODEL, W_IN_SHARD)
        if l == 0:
            dx, sums_in, arrived = _in_bwd(pieces, w_new, x_in, dx, (g_pre, l, 0), scale,
                                           comm=("exchange", [gin_slots]))
            gin_parts[0] = arrived.reshape(N_DEV, D_MODEL, W_IN_SHARD)
        else:
            dx, sums_in = _in_bwd(pieces, w_new, x_in, dx, (g_pre, l, 0), scale)
        dmod = jnp.concatenate([sums_in[0], sums_in[1], sums_post[0]])
        vecs = jnp.concatenate([sums_in[2], sums_post[1], sums_post[2], dbgu[0]])
        small_rows[0:0] = [_row(dmod, 4096), _row(vecs, 4096), _row(dwgu[:GLA_LOWRANK], 4096)]
    grad_x = dx[None]

    flat = lambda a, rows: a.reshape(rows, a.shape[-1])
    r_ada = DEPTH * D_MODEL
    g_w_in, d_w_in, nm_w_in, nv_w_in = _adamw_layers(w_in, gin_parts, m_w_in, v_w_in, "adamw_w_in", 256)
    g_w_out, d_w_out, nm_w_out, nv_w_out = _adamw_layers(w_out, gout_parts, m_w_out, v_w_out, "adamw_w_out", 128)

    small_rows += [_row(loss_part[0, 0:1], 4096), jnp.zeros((1, 4096), F32)]
    small = _all_gather(jnp.concatenate(small_rows, axis=0), "gather_small").reshape(N_DEV, 8, 4096)
    dmod_all = jnp.stack([small[:, 0, :3 * D_MODEL], small[:, 3, :3 * D_MODEL]])
    dmod_cols = lax.dynamic_slice_in_dim(dmod_all, me * ADA_SHARD, ADA_SHARD, axis=2)
    gwa = _w_ada_grad(c_all, dmod_cols).reshape(1, r_ada, ADA_SHARD)
    g_w_ada, d_w_ada, nm_w_ada, nv_w_ada = (
        t.reshape(w_ada.shape) for t in _adamw(flat(w_ada, r_ada), gwa, flat(m_w_ada, r_ada), flat(v_w_ada, r_ada),
                                               "adamw_w_ada", 256))

    def small_param(w, m, v, cols, row, name):
        n = w.shape[1]
        parts = jnp.stack([small[:, row, cols:cols + n], small[:, row + 3, cols:cols + n]], axis=1)
        return _adamw(w, parts, m, v, name, DEPTH)

    g_b_ada, d_b_ada, nm_b_ada, nv_b_ada = small_param(b_ada, m_b_ada, v_b_ada, 0, 0, "adamw_b_ada")
    g_g_pre, d_g_pre, nm_g_pre, nv_g_pre = small_param(g_pre, m_g_pre, v_g_pre, 0, 1, "adamw_g_pre")
    g_g_post, d_g_post, nm_g_post, nv_g_post = small_param(g_post, m_g_post, v_g_post, 1024, 1, "adamw_g_post")
    g_g_gla, d_g_gla, nm_g_gla, nv_g_gla = small_param(g_gla, m_g_gla, v_g_gla, 2048, 1, "adamw_g_gla")
    g_g_dil, d_g_dil, nm_g_dil, nv_g_dil = small_param(g_dil, m_g_dil, v_g_dil, 2560, 1, "adamw_g_dil")
    g_b_gu, d_b_gu, nm_b_gu, nv_b_gu = small_param(b_gate_up, m_b_gate_up, v_b_gate_up, 3072, 1, "adamw_b_gate_up")
    gu_parts = jnp.stack([small[:, 2], small[:, 5]], axis=1).reshape(N_DEV, DEPTH, GLA_LOWRANK, GU_COLS)
    gu_parts = lax.dynamic_slice_in_dim(gu_parts, me * GU_SHARD, GU_SHARD, axis=3).reshape(
        N_DEV, DEPTH * GLA_LOWRANK, GU_SHARD)
    r_gu = DEPTH * GLA_LOWRANK
    g_w_gu, d_w_gu, nm_w_gu, nv_w_gu = (
        t.reshape(w_gate_up.shape) for t in _adamw(flat(w_gate_up, r_gu), gu_parts, flat(m_w_gate_up, r_gu),
                                                   flat(v_w_gate_up, r_gu), "adamw_w_gate_up", r_gu))
    loss_parts = jnp.broadcast_to(small[:, 6, 0:1].reshape(N_DEV, 1, 1), (N_DEV, 8, LANE))
    loss = _sum_parts(loss_parts)[0, 0]

    return (loss, grad_x,
            g_w_ada, g_b_ada, g_g_pre, g_w_in, g_w_gu, g_b_gu, g_g_gla, g_g_dil, g_w_out, g_g_post,
            d_w_ada, d_b_ada, d_g_pre, d_w_in, d_w_gu, d_b_gu, d_g_gla, d_g_dil, d_w_out, d_g_post,
            nm_w_ada, nm_b_ada, nm_g_pre, nm_w_in, nm_w_gu, nm_b_gu, nm_g_gla, nm_g_dil, nm_w_out, nm_g_post,
            nv_w_ada, nv_b_ada, nv_g_pre, nv_w_in, nv_w_gu, nv_b_gu, nv_g_gla, nv_g_dil, nv_w_out, nv_g_post)


def _sum_parts(parts):
    n_parts = parts.shape[0]

    def body(p_ref, o_ref):
        acc = p_ref[0]
        for k in range(1, n_parts):
            acc = acc + p_ref[k]
        o_ref[...] = acc

    return pl.pallas_call(body, name="sum_loss", out_shape=jax.ShapeDtypeStruct(parts.shape[1:], F32))(parts)
```

```python
import functools
import math

import jax
import jax.numpy as jnp
from jax import lax
from jax.experimental import pallas as pl
from jax.experimental.pallas import tpu as pltpu

F32 = jnp.float32
BF16 = jnp.bfloat16

N_DEV = 8
D_MODEL = 1024
DEPTH = 2
GLA_HEADS = 4
GLA_DK = 64
GLA_DV = 128
GLA_CHUNK = 64
GLA_TAU = 16.0
GLA_LOWRANK = 16
DIL_HEADS = 4
DIL_HD = 128
DIL_BLOCK = 128
DIL_DILATIONS = (1, 4, 16)
ROPE_THETA = 10000.0
EPS = 1e-6
IN_COLS = 3600
W_IN_SHARD = IN_COLS // N_DEV
ADA_SHARD = 3 * D_MODEL // N_DEV
OUT_SHARD = D_MODEL // N_DEV
GU_COLS = GLA_HEADS * GLA_DK
GU_SHARD = GU_COLS // N_DEV

ADAM_LR = 0.001
ADAM_B1 = 0.9
ADAM_B2 = 0.999
ADAM_EPS = 1e-08
ADAM_WD = 0.01
ADAM_STEP = 10

NP = 3712
COL_Z, COL_QA, COL_KA, COL_QB, COL_KB, COL_VA, COL_VB, COL_LR = 0, 1024, 1280, 1536, 2048, 2560, 3072, 3584
NP_F32 = COL_VA
NP_BF16 = NP - NP_F32
LANE = 128
MASK_VALUE = -1e30

MESH = pl.DeviceIdType.MESH
ANY = pl.BlockSpec(memory_space=pl.ANY)


def _params(sem=None, vmem_mb=None):
    kw = {}
    if sem is not None:
        kw["dimension_semantics"] = sem
    if vmem_mb is not None:
        kw["vmem_limit_bytes"] = vmem_mb * 1024 * 1024
    return pltpu.CompilerParams(**kw)


def _dot(a, b):
    return jnp.dot(a, b, preferred_element_type=F32)


def _dot_nt(a, b):
    return lax.dot_general(a, b, (((1,), (1,)), ((), ())), preferred_element_type=F32)


def _dot_tn(a, b):
    return lax.dot_general(a, b, (((0,), (0,)), ((), ())), preferred_element_type=F32)


def _sigmoid(z):
    return 1.0 / (1.0 + jnp.exp(-z))


def _log_sigmoid(z):
    return jnp.minimum(z, 0.0) - jnp.log(1.0 + jnp.exp(-jnp.abs(z)))


def _rowvec(v, width=D_MODEL):
    arr, row, cb = v
    return arr.reshape(arr.shape[0], 1, arr.shape[1]), pl.BlockSpec((None, 1, width), lambda *_: (row, 0, cb))


def _my_position():
    return lax.axis_index("x"), lax.axis_index("y"), lax.axis_index("c")


def _linear(px, py, pc):
    return 4 * px + 2 * py + pc


def _gather_phase(phase, x_ref, out_ref, send_sem, recv_sem, local_sem):
    m = x_ref.shape[0]
    x, y, c = _my_position()
    me, sibling = (x, y, c), (x, y, 1 - c)
    chips = [(1 - x, y), (x, 1 - y), (1 - x, 1 - y)]

    def rows(px, py, pc):
        return out_ref.at[pl.ds(_linear(px, py, pc) * m, m), :]

    def copy(k, block, to, src=None):
        return pltpu.make_async_remote_copy(
            src_ref=rows(*block) if src is None else src, dst_ref=rows(*block),
            send_sem=send_sem(k), recv_sem=recv_sem(k), device_id=to, device_id_type=MESH)

    mine = pltpu.make_async_copy(x_ref, rows(*me), local_sem)
    first = [copy(0, me, sibling, src=x_ref)] + [copy(1 + j, me, (*chip, c), src=x_ref) for j, chip in enumerate(chips)]
    passed = [copy(4 + j, (*chip, c), sibling) for j, chip in enumerate(chips)]
    if phase == "start":
        mine.start()
        for cp in first:
            cp.start()
    elif phase == "forward":
        for j, chip in enumerate(chips):
            copy(1 + j, (*chip, c), me).wait_recv()
            passed[j].start()
    else:
        copy(0, sibling, me).wait_recv()
        for j, chip in enumerate(chips):
            copy(4 + j, (*chip, 1 - c), me).wait_recv()
        for cp in first + passed:
            cp.wait_send()
        mine.wait()


def _exchange_phase(phase, x_ref, out_ref, send_sem, recv_sem, local_sem):
    m = x_ref.shape[0] // N_DEV
    x, y, c = _my_position()
    me = _linear(x, y, c)

    def rows(ref, idx):
        return ref.at[pl.ds(idx * m, m), :]

    peers = [(1 - x if j & 4 else x, 1 - y if j & 2 else y, 1 - c if j & 1 else c) for j in range(1, N_DEV)]
    local = pltpu.make_async_copy(rows(x_ref, me), rows(out_ref, me), local_sem)
    sends = [pltpu.make_async_remote_copy(
        src_ref=rows(x_ref, _linear(*peer)), dst_ref=rows(out_ref, me),
        send_sem=send_sem(j), recv_sem=recv_sem(j), device_id=peer, device_id_type=MESH) for j, peer in enumerate(peers)]
    if phase == "start":
        local.start()
        for cp in sends:
            cp.start()
    else:
        for j, peer in enumerate(peers):
            pltpu.make_async_remote_copy(
                src_ref=rows(x_ref, _linear(*peer)), dst_ref=rows(out_ref, _linear(*peer)),
                send_sem=send_sem(j), recv_sem=recv_sem(j), device_id=peer, device_id_type=MESH).wait_recv()
        for cp in sends:
            cp.wait_send()
        local.wait()


_COMM_PHASES = {"gather": (_gather_phase, ("start", "forward", "finish")),
                "exchange": (_exchange_phase, ("start", "finish"))}


def _comm_scratch(n_arrays):
    return [pltpu.SemaphoreType.DMA((n_arrays, 7)), pltpu.SemaphoreType.DMA((n_arrays, 7)),
            pltpu.SemaphoreType.DMA((n_arrays,))]


def _comm_run(kind, phases, x_refs, out_refs, send_sems, recv_sems, local_sems):
    fn = _COMM_PHASES[kind][0]
    for phase in phases:
        for a, (x_ref, out_ref) in enumerate(zip(x_refs, out_refs)):
            fn(phase, x_ref, out_ref, lambda k, a=a: send_sems.at[a, k], lambda k, a=a: recv_sems.at[a, k],
               local_sems.at[a])


def _comm_out_shapes(kind, arrays):
    return [jax.ShapeDtypeStruct((N_DEV * a.shape[0], a.shape[1]) if kind == "gather" else a.shape, a.dtype)
            for a in arrays]


def _comm_call(kind, arrays, name):
    n = len(arrays)

    def body(*refs):
        _comm_run(kind, _COMM_PHASES[kind][1], refs[:n], refs[n:2 * n], *refs[2 * n:])

    return pl.pallas_call(body, name=name, out_shape=_comm_out_shapes(kind, arrays), in_specs=[ANY] * n,
                          out_specs=[ANY] * n, scratch_shapes=_comm_scratch(n))(*arrays)


def _all_gather(xs, name):
    return _comm_call("gather", [xs], name)[0]


def _all_to_all(xs, name):
    return _comm_call("exchange", [xs], name)[0]


def _mod_fwd(c_all, w_ada):
    def body(c_ref, w_ref, o_ref):
        cv = c_ref[...]
        sc = cv * _sigmoid(cv)
        o_ref[0] = _dot(sc.astype(BF16), w_ref[0].astype(BF16))

    return pl.pallas_call(
        body, name="mod_fwd", grid=(DEPTH,),
        out_shape=jax.ShapeDtypeStruct((DEPTH, N_DEV, ADA_SHARD), F32),
        in_specs=[pl.BlockSpec((N_DEV, D_MODEL), lambda l: (0, 0)),
                  pl.BlockSpec((1, D_MODEL, ADA_SHARD), lambda l: (l, 0, 0))],
        out_specs=pl.BlockSpec((1, N_DEV, ADA_SHARD), lambda l: (l, 0, 0)),
        compiler_params=_params(("arbitrary",)),
    )(c_all, w_ada)


def _w_ada_grad(c_all, dmod_cols):
    def body(c_ref, d_ref, o_ref):
        cv = c_ref[...]
        sc = cv * _sigmoid(cv)
        o_ref[0] = lax.dot_general(sc, d_ref[0], (((0,), (0,)), ((), ())), precision=lax.Precision.HIGHEST,
                                   preferred_element_type=F32)

    return pl.pallas_call(
        body, name="w_ada_grad", grid=(DEPTH,),
        out_shape=jax.ShapeDtypeStruct((DEPTH, D_MODEL, ADA_SHARD), F32),
        in_specs=[pl.BlockSpec((N_DEV, D_MODEL), lambda l: (0, 0)),
                  pl.BlockSpec((1, N_DEV, ADA_SHARD), lambda l: (l, 0, 0))],
        out_specs=pl.BlockSpec((1, D_MODEL, ADA_SHARD), lambda l: (l, 0, 0)),
        compiler_params=_params(("arbitrary",)),
    )(c_all, dmod_cols)


def _prenorm_proj(x, g_pre, scale, shift, w_new, ts=256):
    s_len = x.shape[0]

    def body(x_ref, g_ref, sc_ref, sh_ref, w_ref, pf_ref, pb_ref, h_ref):
        xv = x_ref[...]
        rstd = lax.rsqrt(jnp.mean(xv * xv, axis=-1, keepdims=True) + EPS)
        h = (xv * rstd * g_ref[...]) * (1.0 + sc_ref[...]) + sh_ref[...]
        hb = h.astype(BF16)
        h_ref[...] = hb
        for j in range(0, NP, 512):
            w = min(512, NP - j)
            acc = _dot(hb, w_ref[:, j:j + w])
            if j < NP_F32:
                pf_ref[:, j:j + w] = acc
            else:
                pb_ref[:, j - NP_F32:j - NP_F32 + w] = acc.astype(BF16)

    (g_pre, g_spec), (scale, sc_spec), (shift, sh_spec) = _rowvec(g_pre), _rowvec(scale), _rowvec(shift)
    return pl.pallas_call(
        body, name="prenorm_proj", grid=(s_len // ts,),
        out_shape=(jax.ShapeDtypeStruct((s_len, NP_F32), F32), jax.ShapeDtypeStruct((s_len, NP_BF16), BF16),
                   jax.ShapeDtypeStruct((s_len, D_MODEL), BF16)),
        in_specs=[pl.BlockSpec((ts, D_MODEL), lambda i: (i, 0)), g_spec, sc_spec, sh_spec,
                  pl.BlockSpec((D_MODEL, NP), lambda i: (0, 0))],
        out_specs=(pl.BlockSpec((ts, NP_F32), lambda i: (i, 0)), pl.BlockSpec((ts, NP_BF16), lambda i: (i, 0)),
                   pl.BlockSpec((ts, D_MODEL), lambda i: (i, 0))),
        compiler_params=_params(("arbitrary",), 48),
    )(x, g_pre, scale, shift, w_new)


GLA_GROUP = 4


def _gla_group_rows(t):
    return [pl.ds(pl.multiple_of((t * GLA_GROUP + j) * GLA_CHUNK, GLA_CHUNK), GLA_CHUNK) for j in range(GLA_GROUP)]


def _gla_chunks_common(q_ref, k_ref, lr_ref, wgu_ref, bgu_ref, rows_list):
    c = GLA_CHUNK
    ri = lax.broadcasted_iota(jnp.int32, (c, c), 0)
    ci = lax.broadcasted_iota(jnp.int32, (c, c), 1)
    tril = (ri >= ci).astype(F32)
    zs = [_dot(lr_ref[rows, :], wgu_ref[...]) + bgu_ref[...] for rows in rows_list]
    las = [_log_sigmoid(z) * (1.0 / GLA_TAU) for z in zs]
    bs = [jnp.dot(tril, la, precision=lax.Precision.HIGHEST, preferred_element_type=F32) for la in las]
    out = []
    for rows, z, b in zip(rows_list, zs, bs):
        q = q_ref[rows, :] * (GLA_DK ** -0.5)
        k = k_ref[rows, :]
        bl = b[c - 1:c, :]
        out.append(dict(z=z, b=b, bl=bl, qe=q * jnp.exp(b), ke=k * jnp.exp(-b), kend=k * jnp.exp(bl - b),
                        dec=jnp.exp(bl)))
    return out, ri, ci


def _head_lane_mask(hh):
    return (lax.broadcasted_iota(jnp.int32, (1, LANE), 1) // GLA_DK) == hh


def _state_block_mask():
    r = lax.broadcasted_iota(jnp.int32, (2 * GLA_DV, LANE), 0) // GLA_DV
    cc = lax.broadcasted_iota(jnp.int32, (2 * GLA_DV, LANE), 1) // GLA_DK
    return r == cc


def _gla_fwd(pf, pb, wgu, bgu, layer, comm=None):
    s_len = pf.shape[0]
    nc = s_len // GLA_CHUNK
    ncomm = len(comm[1]) if comm else 0

    def body(*refs):
        q_ref, k_ref, v_ref, lr_ref, wgu_ref, bgu_ref = refs[:6]
        cin, (o_ref, st_ref), cout = refs[6:6 + ncomm], refs[6 + ncomm:8 + ncomm], refs[8 + ncomm:8 + 2 * ncomm]
        qe_s, cs_s, dec_s = refs[8 + 2 * ncomm:11 + 2 * ncomm]
        comm_before, comm_after = _comm_hooks(comm, cin, cout, refs[11 + 2 * ncomm:], steps=2)
        comm_before()
        bd = _state_block_mask()

        def local(t, carry):
            rows_list = _gla_group_rows(t)
            cm, ri, ci = _gla_chunks_common(q_ref, k_ref, lr_ref, wgu_ref, bgu_ref, rows_list)
            vs = [v_ref[rows, :] for rows in rows_list]
            kebs = [c["ke"].astype(BF16) for c in cm]
            a = [[jnp.where(ri >= ci, _dot_nt(jnp.where(_head_lane_mask(hh), c["qe"], 0.0).astype(BF16), keb), 0.0)
                  .astype(BF16) for hh in range(2)] for c, keb in zip(cm, kebs)]
            oi = [[_dot(ah[hh], v[:, hh * GLA_DV:(hh + 1) * GLA_DV]) for hh in range(2)] for ah, v in zip(a, vs)]
            cs = [jnp.where(bd, _dot_tn(v, c["kend"].astype(BF16)), 0.0) for c, v in zip(cm, vs)]
            for j, (rows, c) in enumerate(zip(rows_list, cm)):
                n = t * GLA_GROUP + j
                o_ref[rows, :] = jnp.concatenate(oi[j], axis=1)
                qe_s[rows, :] = c["qe"].astype(BF16)
                cs_s[n] = cs[j]
                dec_s[n] = jnp.broadcast_to(c["dec"], (8, LANE))
            return carry

        lax.fori_loop(0, nc // GLA_GROUP, local, 0)

        def scan(n, st):
            st_ref[0, n] = st.astype(BF16)
            return dec_s[n][0:1, :] * st + cs_s[n]

        lax.fori_loop(0, nc, scan, jnp.zeros((2 * GLA_DV, LANE), F32))

        def inter(t, carry):
            rows_list = _gla_group_rows(t)
            add = [_dot_nt(qe_s[rows, :], st_ref[0, t * GLA_GROUP + j]) for j, rows in enumerate(rows_list)]
            for rows, av in zip(rows_list, add):
                o_ref[rows, :] = o_ref[rows, :] + av
            return carry

        lax.fori_loop(0, nc // GLA_GROUP, inter, 0)
        comm_after()

    return pl.pallas_call(
        body, name="gla_fwd_comm" if comm else "gla_fwd", grid=(2,),
        out_shape=[jax.ShapeDtypeStruct((s_len, GLA_HEADS * GLA_DV), F32),
                   jax.ShapeDtypeStruct((2, nc, 2 * GLA_DV, LANE), BF16)] + (_comm_out_shapes(*comm) if comm else []),
        in_specs=[pl.BlockSpec((s_len, LANE), lambda g: (0, COL_QA // LANE + g)),
                  pl.BlockSpec((s_len, LANE), lambda g: (0, COL_KA // LANE + g)),
                  pl.BlockSpec((s_len, 2 * GLA_DV), lambda g: (0, (COL_VA - NP_F32) // (2 * GLA_DV) + g)),
                  pl.BlockSpec((s_len, LANE), lambda g: (0, (COL_LR - NP_F32) // LANE)),
                  pl.BlockSpec((None, LANE, LANE), lambda g: (layer, 0, g)),
                  pl.BlockSpec((None, 1, LANE), lambda g: (layer, 0, g))] + [ANY] * ncomm,
        out_specs=[pl.BlockSpec((s_len, 2 * GLA_DV), lambda g: (0, g)),
                   pl.BlockSpec((1, nc, 2 * GLA_DV, LANE), lambda g: (g, 0, 0, 0))] + [ANY] * ncomm,
        scratch_shapes=[pltpu.VMEM((s_len, LANE), BF16), pltpu.VMEM((nc, 2 * GLA_DV, LANE), F32),
                        pltpu.VMEM((nc, 8, LANE), F32)] + (_comm_scratch(ncomm) if comm else []),
        compiler_params=_params(("arbitrary",), 56),
    )(pf, pf, pb, pb, wgu, bgu.reshape(bgu.shape[0], 1, GU_COLS), *(comm[1] if comm else []))


def _rope_tables(s_len):
    inv_freq = ROPE_THETA ** (-jnp.arange(0, DIL_HD, 2, dtype=F32) / DIL_HD)
    ang = jnp.arange(s_len, dtype=F32)[:, None] * inv_freq[None, :]
    cos, sin = jnp.cos(ang), jnp.sin(ang)
    return jnp.concatenate([cos, cos], axis=1), jnp.concatenate([-sin, sin], axis=1)


def _rope(xv, cos, sin_signed):
    return xv * cos + pltpu.roll(xv, DIL_HD // 2, 1) * sin_signed


DIL_GROUP = 4
DIL_GROUP_FWD = 8


def _dil_pair_block(i, half, d, nblk, group=DIL_GROUP):
    nb = nblk // d
    j = i + half * (nblk // group)
    if nb >= 2 * group:
        r, n = j % d, j // d
    else:
        r, n = j // nb, j % nb
    kb = jnp.maximum(n - 1, 0)
    qs = r + d * DIL_BLOCK * n
    ks = r + d * DIL_BLOCK * kb
    return qs, ks, jnp.minimum(n, 1)


def _dil_fill_bias(bias):
    qi = lax.broadcasted_iota(jnp.int32, (DIL_BLOCK, 2 * DIL_BLOCK), 0)
    kj = lax.broadcasted_iota(jnp.int32, (DIL_BLOCK, 2 * DIL_BLOCK), 1)
    for sel in range(2):
        dist = qi - kj + DIL_BLOCK * sel
        bias[sel] = jnp.where((dist >= 0) & (dist <= DIL_BLOCK), 0.0, MASK_VALUE)


def _strided(start, size, d):
    return pl.ds(start, size) if d == 1 else pl.ds(start, size, stride=d)


def _comm_hooks(comm, cin, cout, csem, steps=DIL_HEADS):
    def before():
        if comm:
            @pl.when(pl.program_id(0) == 0)
            def _():
                _comm_run(comm[0], ("start",), cin, cout, *csem)

            if comm[0] == "gather":
                @pl.when(pl.program_id(0) == steps - 1)
                def _():
                    _comm_run(comm[0], ("forward",), cin, cout, *csem)

    def after():
        if comm:
            @pl.when(pl.program_id(0) == steps - 1)
            def _():
                _comm_run(comm[0], ("finish",), cin, cout, *csem)

    return before, after


def _dil_fwd(pf, pb, cos, sin_signed, comm=None):
    s_len = pf.shape[0]
    nblk = s_len // DIL_BLOCK
    prep_rows = 256
    scale = DIL_HD ** -0.5
    nc = len(comm[1]) if comm else 0

    def body(*refs):
        q_ref, k_ref, v_ref, cos_ref, sin_ref = refs[:5]
        cin, (o_ref, lse_ref), cout = refs[5:5 + nc], refs[5 + nc:7 + nc], refs[7 + nc:7 + 2 * nc]
        qf, kf, vf, o0, o1, o2, l0, l1, l2, bias = refs[7 + 2 * nc:17 + 2 * nc]
        comm_before, comm_after = _comm_hooks(comm, cin, cout, refs[17 + 2 * nc:])
        comm_before()
        _dil_fill_bias(bias)

        def prep(t, carry):
            rows = pl.ds(pl.multiple_of(t * prep_rows, prep_rows), prep_rows)
            cs, sn = cos_ref[rows, :], sin_ref[rows, :]
            qf[rows, :] = _rope(q_ref[rows, :], cs, sn)
            kf[rows, :] = _rope(k_ref[rows, :], cs, sn)
            vf[rows, :] = v_ref[rows, :].astype(F32)
            return carry

        lax.fori_loop(0, s_len // prep_rows, prep, 0)
        ones = jnp.ones((2 * DIL_BLOCK, DIL_HD), BF16)

        for d, o_p, l_p in zip(DIL_DILATIONS, (o0, o1, o2), (l0, l1, l2)):
            if nblk // d == 2:
                units = DIL_GROUP_FWD // 2

                def whole(i, carry, d=d, o_p=o_p, l_p=l_p, units=units):
                    rows = [_strided(i + u * (d // units), 2 * DIL_BLOCK, d) for u in range(units)]
                    ld = [(qf[rw, :].astype(BF16), kf[rw, :].astype(BF16), vf[rw, :].astype(BF16)) for rw in rows]
                    both = bias[...].reshape(2 * DIL_BLOCK, 2 * DIL_BLOCK)
                    s = [_dot_nt(qb, kk) * scale + both for qb, kk, _ in ld]
                    m = [jnp.max(sv, axis=-1, keepdims=True) for sv in s]
                    p = [jnp.exp(sv - mv) for sv, mv in zip(s, m)]
                    den = [jnp.sum(pv, axis=-1, keepdims=True) for pv in p]
                    r = [_dot(pv.astype(BF16), vv) for pv, (_, _, vv) in zip(p, ld)]
                    for rv, dv, mv, rw in zip(r, den, m, rows):
                        o_p[rw, :] = rv / dv
                        l_p[rw, :] = jnp.broadcast_to(mv + jnp.log(dv), (2 * DIL_BLOCK, DIL_HD))
                    return carry

                lax.fori_loop(0, d // units, whole, 0)
                continue

            def pair(i, carry, d=d, o_p=o_p, l_p=l_p):
                idx = [_dil_pair_block(i, half, d, nblk, DIL_GROUP_FWD) for half in range(DIL_GROUP_FWD)]
                ld = [(qf[_strided(qs, DIL_BLOCK, d), :].astype(BF16),
                       kf[_strided(ks, 2 * DIL_BLOCK, d), :].astype(BF16),
                       vf[_strided(ks, 2 * DIL_BLOCK, d), :].astype(BF16)) for qs, ks, _ in idx]
                s = [_dot_nt(qb, kk) * scale + bias[sel] for (qb, kk, _), (_, _, sel) in zip(ld, idx)]
                m = [jnp.max(sv, axis=-1, keepdims=True) for sv in s]
                p = [jnp.exp(sv - mv) for sv, mv in zip(s, m)]
                den = [jnp.sum(pv, axis=-1, keepdims=True) for pv in p]
                r = [_dot(pv.astype(BF16), vv) for pv, (_, _, vv) in zip(p, ld)]
                for rv, dv, mv, (qs, _, _) in zip(r, den, m, idx):
                    o_p[_strided(qs, DIL_BLOCK, d), :] = rv / dv
                    l_p[_strided(qs, DIL_BLOCK, d), :] = jnp.broadcast_to(mv + jnp.log(dv), (DIL_BLOCK, DIL_HD))
                return carry

            lax.fori_loop(0, nblk // DIL_GROUP_FWD, pair, 0)

        def comb(t, carry):
            rows = pl.ds(pl.multiple_of(t * prep_rows, prep_rows), prep_rows)
            a0, a1, a2 = l0[rows, :], l1[rows, :], l2[rows, :]
            m = jnp.maximum(jnp.maximum(a0, a1), a2)
            e0, e1, e2 = jnp.exp(a0 - m), jnp.exp(a1 - m), jnp.exp(a2 - m)
            tot = e0 + e1 + e2
            o_ref[rows, :] = (e0 * o0[rows, :] + e1 * o1[rows, :] + e2 * o2[rows, :]) / tot
            lse_ref[rows, :] = m + jnp.log(tot)
            return carry

        lax.fori_loop(0, s_len // prep_rows, comb, 0)
        comm_after()

    head = lambda base: pl.BlockSpec((s_len, DIL_HD), lambda h: (0, base // DIL_HD + h))
    table = pl.BlockSpec((s_len, DIL_HD), lambda h: (0, 0))
    out = pl.BlockSpec((s_len, DIL_HD), lambda h: (0, h))
    shp = jax.ShapeDtypeStruct((s_len, DIL_HEADS * DIL_HD), F32)
    return pl.pallas_call(
        body, name="dil_fwd_comm" if comm else "dil_fwd", grid=(DIL_HEADS,),
        out_shape=[shp, shp] + (_comm_out_shapes(*comm) if comm else []),
        in_specs=[head(COL_QB), head(COL_KB), head(COL_VB - NP_F32), table, table] + [ANY] * nc,
        out_specs=[out, out] + [ANY] * nc,
        scratch_shapes=[pltpu.VMEM((s_len, DIL_HD), F32) for _ in range(9)]
        + [pltpu.VMEM((2, DIL_BLOCK, 2 * DIL_BLOCK), F32)] + (_comm_scratch(nc) if comm else []),
        compiler_params=_params(("arbitrary",), 56),
    )(pf, pf, pb, cos, sin_signed, *(comm[1] if comm else []))


def _silu_and_grad(z):
    sg = _sigmoid(z)
    return z * sg, sg * (1.0 + z * (1.0 - sg))


def _post_fwd(o_a, o_b, pf, g_heads, w_out, x, gate, g_post, ts=256):
    s_len = x.shape[0]
    half = GLA_HEADS * GLA_DV

    def body(oa_ref, ob_ref, z_ref, gh_ref, w_ref, x_ref, gate_ref, gp_ref, xo_ref, y_ref, u_ref):
        for src, base in ((oa_ref, 0), (ob_ref, half)):
            for hh in range(4):
                lo = hh * LANE
                og = src[:, lo:lo + LANE]
                on = og * lax.rsqrt(jnp.mean(og * og, axis=-1, keepdims=True) + EPS)
                zg = z_ref[:, base + lo:base + lo + LANE].astype(F32)
                y_ref[:, base + lo:base + lo + LANE] = (on * gh_ref[:, base + lo:base + lo + LANE]
                                                        * (zg * _sigmoid(zg))).astype(BF16)
        u = _dot(y_ref[...], w_ref[...])
        u_ref[...] = u.astype(BF16)
        rstd = lax.rsqrt(jnp.mean(u * u, axis=-1, keepdims=True) + EPS)
        xo_ref[...] = x_ref[...] + gate_ref[...] * (u * rstd * gp_ref[...])

    (g_heads, gh_spec), (gate, gate_spec), (g_post, gp_spec) = _rowvec(g_heads), _rowvec(gate), _rowvec(g_post)
    tile = pl.BlockSpec((ts, D_MODEL), lambda i: (i, 0))
    halft = pl.BlockSpec((ts, half), lambda i: (i, 0))
    return pl.pallas_call(
        body, name="post_fwd", grid=(s_len // ts,),
        out_shape=(jax.ShapeDtypeStruct((s_len, D_MODEL), F32), jax.ShapeDtypeStruct((s_len, D_MODEL), BF16),
                   jax.ShapeDtypeStruct((s_len, D_MODEL), BF16)),
        in_specs=[halft, halft, tile, gh_spec, pl.BlockSpec((D_MODEL, D_MODEL), lambda i: (0, 0)), tile, gate_spec,
                  gp_spec],
        out_specs=(tile, tile, tile),
        compiler_params=_params(("arbitrary",), 40),
    )(o_a, o_b, pf, g_heads, w_out, x, gate, g_post)


def _loss_grad(y, target, ts=512):
    s_len = y.shape[0]

    def body(y_ref, t_ref, dy_ref, loss_ref):
        @pl.when(pl.program_id(0) == 0)
        def _():
            loss_ref[...] = jnp.zeros_like(loss_ref)

        e = y_ref[...] - t_ref[...]
        dy_ref[...] = e * (1.0 / D_MODEL)
        loss_ref[...] += 0.5 * jnp.sum(jnp.mean(e * e, axis=-1, keepdims=True))

    tile = pl.BlockSpec((ts, D_MODEL), lambda i: (i, 0))
    return pl.pallas_call(
        body, name="loss_grad", grid=(s_len // ts,),
        out_shape=(jax.ShapeDtypeStruct((s_len, D_MODEL), F32), jax.ShapeDtypeStruct((8, LANE), F32)),
        in_specs=[tile, tile], out_specs=(tile, pl.BlockSpec((8, LANE), lambda i: (0, 0))),
        compiler_params=_params(("arbitrary",)),
    )(y, target)


def _post_bwd(dxo, u, gate, g_post, w_out, o_a, o_b, pf, g_heads, ts=256):
    s_len = dxo.shape[0]
    half = GLA_HEADS * GLA_DV

    def body(dx_ref, u_ref, gate_ref, gp_ref, w_ref, oa_ref, ob_ref, z_ref, gh_ref, du_ref, do_ref, dz_ref, sums_ref):
        @pl.when(pl.program_id(0) == 0)
        def _():
            sums_ref[...] = jnp.zeros_like(sums_ref)

        dx = dx_ref[...]
        u = u_ref[...].astype(F32)
        rstd = lax.rsqrt(jnp.mean(u * u, axis=-1, keepdims=True) + EPS)
        un = u * rstd
        sums_ref[0:1, :] += jnp.sum(dx * (un * gp_ref[...]), axis=0, keepdims=True)
        drn = dx * gate_ref[...]
        sums_ref[1:2, :] += jnp.sum(drn * un, axis=0, keepdims=True)
        dun = drn * gp_ref[...]
        du = rstd * (dun - un * jnp.mean(dun * un, axis=-1, keepdims=True))
        dub = du.astype(BF16)
        du_ref[...] = dub
        dy = _dot_nt(dub, w_ref[...])
        for src, base in ((oa_ref, 0), (ob_ref, half)):
            for hh in range(4):
                lo = base + hh * LANE
                og = src[:, hh * LANE:(hh + 1) * LANE]
                rs = lax.rsqrt(jnp.mean(og * og, axis=-1, keepdims=True) + EPS)
                on = og * rs
                zg = z_ref[:, lo:lo + LANE].astype(F32)
                sz, dsz = _silu_and_grad(zg)
                gg = gh_ref[:, lo:lo + LANE]
                dyg = dy[:, lo:lo + LANE]
                sums_ref[2:3, lo:lo + LANE] += jnp.sum(dyg * sz * on, axis=0, keepdims=True)
                dz_ref[:, lo:lo + LANE] = (dyg * on * gg * dsz).astype(BF16)
                don = dyg * gg * sz
                do_ref[:, lo:lo + LANE] = (rs * (don - on * jnp.mean(don * on, axis=-1, keepdims=True))).astype(BF16)

    (g_heads, gh_spec), (gate, gate_spec), (g_post, gp_spec) = _rowvec(g_heads), _rowvec(gate), _rowvec(g_post)
    tile = pl.BlockSpec((ts, D_MODEL), lambda i: (i, 0))
    halft = pl.BlockSpec((ts, half), lambda i: (i, 0))
    return pl.pallas_call(
        body, name="post_bwd", grid=(s_len // ts,),
        out_shape=(jax.ShapeDtypeStruct((s_len, D_MODEL), BF16), jax.ShapeDtypeStruct((s_len, D_MODEL), BF16),
                   jax.ShapeDtypeStruct((s_len, D_MODEL), BF16), jax.ShapeDtypeStruct((8, D_MODEL), F32)),
        in_specs=[tile, tile, gate_spec, gp_spec, pl.BlockSpec((D_MODEL, D_MODEL), lambda i: (0, 0)), halft, halft,
                  tile, gh_spec],
        out_specs=(tile, tile, tile, pl.BlockSpec((8, D_MODEL), lambda i: (0, 0))),
        compiler_params=_params(("arbitrary",), 40),
    )(dxo, u, gate, g_post, w_out, o_a, o_b, pf, g_heads)


def _gla_bwd(pf, pb, wgu, bgu, layer, states, do):
    s_len = pf.shape[0]
    nc = s_len // GLA_CHUNK
    c = GLA_CHUNK

    def body(q_ref, k_ref, v_ref, lr_ref, wgu_ref, bgu_ref, st_ref, do_ref,
             dq_ref, dk_ref, dv_ref, dlr_ref, dwgu_ref, dbgu_ref, ds_s, dec_s, dw_acc, db_acc):
        dw_acc[...] = jnp.zeros_like(dw_acc)
        db_acc[...] = jnp.zeros_like(db_acc)
        bd = _state_block_mask()
        last_row = lax.broadcasted_iota(jnp.int32, (c, LANE), 0) == c - 1

        def local(t, carry):
            rows_list = _gla_group_rows(t)
            cm, _, _ = _gla_chunks_common(q_ref, k_ref, lr_ref, wgu_ref, bgu_ref, rows_list)
            loc = [jnp.where(bd, _dot_tn(do_ref[rows, :], cc["qe"].astype(BF16)), 0.0)
                   for rows, cc in zip(rows_list, cm)]
            for j, cc in enumerate(cm):
                ds_s[t * GLA_GROUP + j] = loc[j]
                dec_s[t * GLA_GROUP + j] = jnp.broadcast_to(cc["dec"], (8, LANE))
            return carry

        lax.fori_loop(0, nc // GLA_GROUP, local, 0)

        def scan(t, dst):
            n = nc - 1 - t
            loc = ds_s[n]
            ds_s[n] = dst
            return dec_s[n][0:1, :] * dst + loc

        lax.fori_loop(0, nc, scan, jnp.zeros((2 * GLA_DV, LANE), F32))

        def rest(t, carry):
            rows_list = _gla_group_rows(t)
            cm, ri, ci = _gla_chunks_common(q_ref, k_ref, lr_ref, wgu_ref, bgu_ref, rows_list)
            ns = [t * GLA_GROUP + j for j in range(GLA_GROUP)]
            vs = [v_ref[rows, :] for rows in rows_list]
            dobs = [do_ref[rows, :] for rows in rows_list]
            stbs = [st_ref[0, n] for n in ns]
            dsts = [ds_s[n] for n in ns]
            dstbs = [d.astype(BF16) for d in dsts]
            qebs = [cc["qe"].astype(BF16) for cc in cm]
            kebs = [cc["ke"].astype(BF16) for cc in cm]
            kendbs = [cc["kend"].astype(BF16) for cc in cm]
            hms = [_head_lane_mask(hh) for hh in range(2)]
            qehs = [[jnp.where(hm, cc["qe"], 0.0).astype(BF16) for hm in hms] for cc in cm]
            kehs = [[jnp.where(hm, cc["ke"], 0.0).astype(BF16) for hm in hms] for cc in cm]
            heads = lambda x: [x[:, hh * GLA_DV:(hh + 1) * GLA_DV] for hh in range(2)]
            vhs, dohs = [heads(v) for v in vs], [heads(d) for d in dobs]

            dqe0 = [_dot(dob, stb) for dob, stb in zip(dobs, stbs)]
            dkend = [_dot(v, dstb) for v, dstb in zip(vs, dstbs)]
            dv0 = [_dot_nt(kb, dstb) for kb, dstb in zip(kendbs, dstbs)]
            a_t = [[jnp.where(ci >= ri, _dot_nt(kehs[j][hh], qebs[j]), 0.0).astype(BF16) for hh in range(2)]
                   for j in range(GLA_GROUP)]
            da = [[jnp.where(ri >= ci, _dot_nt(dohs[j][hh], vhs[j][hh]), 0.0).astype(BF16) for hh in range(2)]
                  for j in range(GLA_GROUP)]
            da_t = [[jnp.where(ci >= ri, _dot_nt(vhs[j][hh], dohs[j][hh]), 0.0).astype(BF16) for hh in range(2)]
                    for j in range(GLA_GROUP)]
            dv1 = [[_dot(a_t[j][hh], dohs[j][hh]) for hh in range(2)] for j in range(GLA_GROUP)]
            dqe1 = [[_dot(da[j][hh], kebs[j]) for hh in range(2)] for j in range(GLA_GROUP)]
            dke1 = [[_dot(da_t[j][hh], qehs[j][hh]) for hh in range(2)] for j in range(GLA_GROUP)]

            dbs, dzs = [], []
            for j, (rows, cc) in enumerate(zip(rows_list, cm)):
                qe, ke, kend, b, bl = cc["qe"], cc["ke"], cc["kend"], cc["b"], cc["bl"]
                dqe = dqe0[j] + jnp.where(hms[0], dqe1[j][0], 0.0) + jnp.where(hms[1], dqe1[j][1], 0.0)
                dke = jnp.where(hms[0], dke1[j][0], 0.0) + jnp.where(hms[1], dke1[j][1], 0.0)
                dv_ref[rows, :] = (dv0[j] + jnp.concatenate(dv1[j], axis=1)).astype(BF16)
                dq_ref[rows, :] = (dqe * jnp.exp(b) * (GLA_DK ** -0.5)).astype(BF16)
                dk_ref[rows, :] = (dke * jnp.exp(-b) + dkend[j] * jnp.exp(bl - b)).astype(BF16)
                ddec = jnp.sum(dsts[j] * stbs[j].astype(F32), axis=0, keepdims=True)
                dbl = jnp.sum(dkend[j] * kend, axis=0, keepdims=True) + ddec * cc["dec"]
                dbs.append(dqe * qe - dke * ke - dkend[j] * kend + jnp.where(last_row, dbl, 0.0))
            triu = (ci >= ri).astype(F32)
            dlas = [jnp.dot(triu, db, precision=lax.Precision.HIGHEST, preferred_element_type=F32) for db in dbs]
            dzs = [dla * (1.0 / GLA_TAU) * _sigmoid(-cc["z"]) for dla, cc in zip(dlas, cm)]
            dzbs = [dz.astype(BF16) for dz in dzs]
            dlrs = [_dot_nt(dzb, wgu_ref[...]) for dzb in dzbs]
            dws = [_dot_tn(lr_ref[rows, :], dzb) for rows, dzb in zip(rows_list, dzbs)]
            for rows, dlr in zip(rows_list, dlrs):
                dlr_ref[0, rows, :] = dlr
            dw_acc[...] += functools.reduce(lambda x, y: x + y, dws)
            db_acc[0:1, :] += jnp.sum(functools.reduce(lambda x, y: x + y, dzs), axis=0, keepdims=True)
            return carry

        lax.fori_loop(0, nc // GLA_GROUP, rest, 0)
        dwgu_ref[...] = dw_acc[...]
        dbgu_ref[...] = db_acc[...]

    pair = pl.BlockSpec((s_len, LANE), lambda g: (0, g))
    return pl.pallas_call(
        body, name="gla_bwd", grid=(2,),
        out_shape=(jax.ShapeDtypeStruct((s_len, GU_COLS), BF16), jax.ShapeDtypeStruct((s_len, GU_COLS), BF16),
                   jax.ShapeDtypeStruct((s_len, GLA_HEADS * GLA_DV), BF16),
                   jax.ShapeDtypeStruct((2, s_len, LANE), F32),
                   jax.ShapeDtypeStruct((LANE, GU_COLS), F32), jax.ShapeDtypeStruct((8, GU_COLS), F32)),
        in_specs=[pl.BlockSpec((s_len, LANE), lambda g: (0, COL_QA // LANE + g)),
                  pl.BlockSpec((s_len, LANE), lambda g: (0, COL_KA // LANE + g)),
                  pl.BlockSpec((s_len, 2 * GLA_DV), lambda g: (0, (COL_VA - NP_F32) // (2 * GLA_DV) + g)),
                  pl.BlockSpec((s_len, LANE), lambda g: (0, (COL_LR - NP_F32) // LANE)),
                  pl.BlockSpec((None, LANE, LANE), lambda g: (layer, 0, g)),
                  pl.BlockSpec((None, 1, LANE), lambda g: (layer, 0, g)),
                  pl.BlockSpec((1, nc, 2 * GLA_DV, LANE), lambda g: (g, 0, 0, 0)),
                  pl.BlockSpec((s_len, 2 * GLA_DV), lambda g: (0, g))],
        out_specs=(pair, pair, pl.BlockSpec((s_len, 2 * GLA_DV), lambda g: (0, g)),
                   pl.BlockSpec((1, s_len, LANE), lambda g: (g, 0, 0)),
                   pl.BlockSpec((LANE, LANE), lambda g: (0, g)), pl.BlockSpec((8, LANE), lambda g: (0, g))),
        scratch_shapes=[pltpu.VMEM((nc, 2 * GLA_DV, LANE), F32), pltpu.VMEM((nc, 8, LANE), F32),
                        pltpu.VMEM((LANE, LANE), F32), pltpu.VMEM((8, LANE), F32)],
        compiler_params=_params(("arbitrary",), 56),
    )(pf, pf, pb, pb, wgu, bgu.reshape(bgu.shape[0], 1, GU_COLS), states, do)


def _dil_bwd(pf, pb, cos, sin_signed, do, o_b, lse, comm=None):
    s_len = pf.shape[0]
    nblk = s_len // DIL_BLOCK
    prep_rows = 256
    scale = DIL_HD ** -0.5
    nc = len(comm[1]) if comm else 0

    def body(*refs):
        q_ref, k_ref, v_ref, cos_ref, sin_ref, do_ref, o_ref, lse_ref = refs[:8]
        cin, (dq_ref, dk_ref, dv_ref), cout = refs[8:8 + nc], refs[8 + nc:11 + nc], refs[11 + nc:11 + 2 * nc]
        qf, kf, vf, dof, dl, dqa, dka, dva, bias = refs[11 + 2 * nc:20 + 2 * nc]
        comm_before, comm_after = _comm_hooks(comm, cin, cout, refs[20 + 2 * nc:])
        comm_before()
        _dil_fill_bias(bias)

        def prep(t, carry):
            rows = pl.ds(pl.multiple_of(t * prep_rows, prep_rows), prep_rows)
            cs, sn = cos_ref[rows, :], sin_ref[rows, :]
            qf[rows, :] = _rope(q_ref[rows, :], cs, sn) * scale
            kf[rows, :] = _rope(k_ref[rows, :], cs, sn)
            vf[rows, :] = v_ref[rows, :].astype(F32)
            dov = do_ref[rows, :].astype(F32)
            dof[rows, :] = dov
            dl[rows, :] = jnp.broadcast_to(jnp.sum(dov * o_ref[rows, :], axis=-1, keepdims=True), (prep_rows, DIL_HD))
            zero = jnp.zeros((prep_rows, DIL_HD), F32)
            dqa[rows, :] = zero
            dka[rows, :] = zero
            dva[rows, :] = zero
            return carry

        lax.fori_loop(0, s_len // prep_rows, prep, 0)

        for d in DIL_DILATIONS:
            if nblk // d == 2:
                units = DIL_GROUP // 2

                def whole(i, carry, d=d, units=units):
                    rows = [_strided(i + u * (d // units), 2 * DIL_BLOCK, d) for u in range(units)]
                    ld = [(qf[rw, :].astype(BF16), kf[rw, :].astype(BF16), vf[rw, :].astype(BF16),
                           dof[rw, :].astype(BF16)) for rw in rows]
                    both = bias[...].reshape(2 * DIL_BLOCK, 2 * DIL_BLOCK)
                    s = [_dot_nt(qb, kk) + both for qb, kk, _, _ in ld]
                    dp = [_dot_nt(dob, vv) for _, _, vv, dob in ld]
                    p = [jnp.exp(sv - lse_ref[rw, :][:, 0:1]) for sv, rw in zip(s, rows)]
                    ds = [(pv * (dpv - dl[rw, :][:, 0:1])).astype(BF16) for pv, dpv, rw in zip(p, dp, rows)]
                    pb = [pv.astype(BF16) for pv in p]
                    gq = [_dot(dsv, kk) for dsv, (_, kk, _, _) in zip(ds, ld)]
                    gk = [_dot_tn(dsv, qb) for dsv, (qb, _, _, _) in zip(ds, ld)]
                    gv = [_dot_tn(pv, dob) for pv, (_, _, _, dob) in zip(pb, ld)]
                    for rw, a, b, c in zip(rows, gq, gk, gv):
                        dqa[rw, :] += a
                        dka[rw, :] += b
                        dva[rw, :] += c
                    return carry

                lax.fori_loop(0, d // units, whole, 0)
                continue

            def pair(i, carry, d=d):
                idx = [_dil_pair_block(i, half, d, nblk) for half in range(DIL_GROUP)]
                rows = [(_strided(qs, DIL_BLOCK, d), _strided(ks, 2 * DIL_BLOCK, d)) for qs, ks, _ in idx]
                ld = [(qf[qr, :].astype(BF16), kf[kr, :].astype(BF16), vf[kr, :].astype(BF16),
                       dof[qr, :].astype(BF16)) for qr, kr in rows]
                s = [_dot_nt(qb, kk) + bias[sel] for (qb, kk, _, _), (_, _, sel) in zip(ld, idx)]
                dp = [_dot_nt(dob, vv) for _, _, vv, dob in ld]
                p = [jnp.exp(sv - lse_ref[qr, :][:, 0:1]) for sv, (qr, _) in zip(s, rows)]
                ds = [(pv * (dpv - dl[qr, :][:, 0:1])).astype(BF16) for pv, dpv, (qr, _) in zip(p, dp, rows)]
                pb = [pv.astype(BF16) for pv in p]
                gq = [_dot(dsv, kk) for dsv, (_, kk, _, _) in zip(ds, ld)]
                gk = [_dot_tn(dsv, qb) for dsv, (qb, _, _, _) in zip(ds, ld)]
                gv = [_dot_tn(pv, dob) for pv, (_, _, _, dob) in zip(pb, ld)]
                for (qr, kr), a, b, c in zip(rows, gq, gk, gv):
                    dqa[qr, :] += a
                    dka[kr, :] += b
                    dva[kr, :] += c
                return carry

            lax.fori_loop(0, nblk // DIL_GROUP, pair, 0)

        def fin(t, carry):
            rows = pl.ds(pl.multiple_of(t * prep_rows, prep_rows), prep_rows)
            cs, sn = cos_ref[rows, :], sin_ref[rows, :]
            gq, gk = dqa[rows, :] * scale, dka[rows, :]
            dq_ref[rows, :] = (gq * cs - pltpu.roll(gq, DIL_HD // 2, 1) * sn).astype(BF16)
            dk_ref[rows, :] = (gk * cs - pltpu.roll(gk, DIL_HD // 2, 1) * sn).astype(BF16)
            dv_ref[rows, :] = dva[rows, :].astype(BF16)
            return carry

        lax.fori_loop(0, s_len // prep_rows, fin, 0)
        comm_after()

    head = lambda base: pl.BlockSpec((s_len, DIL_HD), lambda h: (0, base // DIL_HD + h))
    table = pl.BlockSpec((s_len, DIL_HD), lambda h: (0, 0))
    out = pl.BlockSpec((s_len, DIL_HD), lambda h: (0, h))
    shp = jax.ShapeDtypeStruct((s_len, DIL_HEADS * DIL_HD), BF16)
    return pl.pallas_call(
        body, name="dil_bwd_comm" if comm else "dil_bwd", grid=(DIL_HEADS,),
        out_shape=[shp, shp, shp] + (_comm_out_shapes(*comm) if comm else []),
        in_specs=[head(COL_QB), head(COL_KB), head(COL_VB - NP_F32), table, table,
                  pl.BlockSpec((s_len, DIL_HD), lambda h: (0, DIL_HEADS + h)), out, out] + [ANY] * nc,
        out_specs=[out, out, out] + [ANY] * nc,
        scratch_shapes=[pltpu.VMEM((s_len, DIL_HD), F32) for _ in range(8)]
        + [pltpu.VMEM((2, DIL_BLOCK, 2 * DIL_BLOCK), F32)] + (_comm_scratch(nc) if comm else []),
        compiler_params=_params(("arbitrary",), 56),
    )(pf, pf, pb, cos, sin_signed, do, o_b, lse, *(comm[1] if comm else []))


_PIECES = ((COL_Z, 1024), (COL_QA, 256), (COL_KA, 256), (COL_QB, 512), (COL_KB, 512), (COL_VA, 512), (COL_VB, 512),
           (COL_LR, 128))


def _in_bwd(pieces, w_new, x, dxo, g_pre, scale, comm=None, ts=256):
    s_len = x.shape[0]
    nc = len(comm[1]) if comm else 0
    npc = len(_PIECES)

    def body(*refs):
        p_refs = refs[:npc]
        w_ref, x_ref, dxo_ref, g_ref, sc_ref = refs[npc:npc + 5]
        cin, (dx_ref, sums_ref), cout = (refs[npc + 5:npc + 5 + nc], refs[npc + 5 + nc:npc + 7 + nc],
                                         refs[npc + 7 + nc:npc + 7 + 2 * nc])
        comm_before, comm_after = _comm_hooks(comm, cin, cout, refs[npc + 7 + 2 * nc:], steps=s_len // ts)
        comm_before()

        @pl.when(pl.program_id(0) == 0)
        def _():
            sums_ref[...] = jnp.zeros_like(sums_ref)

        dh = jnp.zeros((ts, D_MODEL), F32)
        for p_ref, (col, width) in zip(p_refs, _PIECES):
            dh += _dot_nt(p_ref[...], w_ref[:, col:col + width])
        xv = x_ref[...]
        rstd = lax.rsqrt(jnp.mean(xv * xv, axis=-1, keepdims=True) + EPS)
        xn = xv * rstd
        sums_ref[0:1, :] += jnp.sum(dh, axis=0, keepdims=True)
        sums_ref[1:2, :] += jnp.sum(dh * (xn * g_ref[...]), axis=0, keepdims=True)
        dr = dh * (1.0 + sc_ref[...])
        sums_ref[2:3, :] += jnp.sum(dr * xn, axis=0, keepdims=True)
        dxn = dr * g_ref[...]
        dx_ref[...] = dxo_ref[...] + rstd * (dxn - xn * jnp.mean(dxn * xn, axis=-1, keepdims=True))
        comm_after()

    (g_pre, g_spec), (scale, sc_spec) = _rowvec(g_pre), _rowvec(scale)
    tile = pl.BlockSpec((ts, D_MODEL), lambda i: (i, 0))
    return pl.pallas_call(
        body, name="in_bwd_comm" if comm else "in_bwd", grid=(s_len // ts,),
        out_shape=[jax.ShapeDtypeStruct((s_len, D_MODEL), F32), jax.ShapeDtypeStruct((8, D_MODEL), F32)]
        + (_comm_out_shapes(*comm) if comm else []),
        in_specs=[pl.BlockSpec((ts, width), lambda i: (i, 0)) for _, width in _PIECES]
        + [pl.BlockSpec((D_MODEL, NP), lambda i: (0, 0)), tile, tile, g_spec, sc_spec] + [ANY] * nc,
        out_specs=[tile, pl.BlockSpec((8, D_MODEL), lambda i: (0, 0))] + [ANY] * nc,
        scratch_shapes=_comm_scratch(nc) if comm else [],
        compiler_params=_params(("arbitrary",), 48),
    )(*pieces, w_new, x, dxo, g_pre, scale, *(comm[1] if comm else []))


def _w_in_to_kernel(gathered, tr=128):
    def body(g_ref, o_ref):
        cols = jnp.concatenate([g_ref[k].astype(F32) for k in range(N_DEV)], axis=1)
        pad = jnp.zeros((tr, LANE - GLA_LOWRANK), F32)
        o_ref[...] = jnp.concatenate(
            [cols[:, 1024:1536], cols[:, 3088:3600], cols[:, 0:512], cols[:, 1552:2576], cols[:, 512:1024],
             cols[:, 2576:3088], cols[:, 1536:1552], pad], axis=1).astype(BF16)

    return pl.pallas_call(
        body, name="w_in_to_kernel", grid=(D_MODEL // tr,), out_shape=jax.ShapeDtypeStruct((D_MODEL, NP), BF16),
        in_specs=[pl.BlockSpec((N_DEV, tr, W_IN_SHARD), lambda i: (0, i, 0))],
        out_specs=pl.BlockSpec((tr, NP), lambda i: (i, 0)),
        compiler_params=_params(("arbitrary",)),
    )(gathered)


def _grad_w_in(h, pieces, ts=512, tr=128):
    s_len = h.shape[0]
    steps = s_len // ts

    def body(*refs):
        h_ref, p_refs = refs[0], refs[1:1 + len(_PIECES)]
        o_ref, acc = refs[1 + len(_PIECES):]

        @pl.when(pl.program_id(0) == 0)
        def _():
            acc[...] = jnp.zeros_like(acc)

        hv = h_ref[...]
        for p_ref, (col, width) in zip(p_refs, _PIECES):
            acc[:, col:col + width] += _dot_tn(hv, p_ref[...])

        @pl.when(pl.program_id(0) == steps - 1)
        def _():
            def rows_out(t, carry):
                rows = pl.ds(pl.multiple_of(t * tr, tr), tr)
                g = acc[rows, :]
                cols = jnp.concatenate(
                    [g[:, COL_QA:COL_QB], g[:, COL_VA:COL_VB], g[:, 0:512], g[:, COL_LR:COL_LR + GLA_LOWRANK],
                     g[:, COL_QB:COL_VA], g[:, COL_VB:COL_LR], g[:, 512:1024]], axis=1)
                for k in range(N_DEV):
                    o_ref[k, rows, :] = cols[:, W_IN_SHARD * k:W_IN_SHARD * (k + 1)].astype(BF16)
                return carry

            lax.fori_loop(0, D_MODEL // tr, rows_out, 0)

    return pl.pallas_call(
        body, name="grad_w_in", grid=(steps,),
        out_shape=jax.ShapeDtypeStruct((N_DEV, D_MODEL, W_IN_SHARD), BF16),
        in_specs=[pl.BlockSpec((ts, D_MODEL), lambda i: (i, 0))]
        + [pl.BlockSpec((ts, width), lambda i: (i, 0)) for _, width in _PIECES],
        out_specs=pl.BlockSpec((N_DEV, D_MODEL, W_IN_SHARD), lambda i: (0, 0, 0)),
        scratch_shapes=[pltpu.VMEM((D_MODEL, NP), F32)],
        compiler_params=_params(("arbitrary",), 56),
    )(h, *pieces)


def _matmul_tn(a, b, name, bn, ts=512):
    s_len, m = a.shape
    n = b.shape[1]
    steps = s_len // ts

    def body(a_ref, b_ref, o_ref, acc):
        @pl.when(pl.program_id(1) == 0)
        def _():
            acc[...] = jnp.zeros_like(acc)

        acc[...] += _dot_tn(a_ref[...], b_ref[...])

        @pl.when(pl.program_id(1) == steps - 1)
        def _():
            o_ref[...] = acc[...].astype(BF16)

    return pl.pallas_call(
        body, name=name, grid=(n // bn, steps),
        out_shape=jax.ShapeDtypeStruct((m, n), BF16),
        in_specs=[pl.BlockSpec((ts, m), lambda j, i: (i, 0)), pl.BlockSpec((ts, bn), lambda j, i: (i, j))],
        out_specs=pl.BlockSpec((m, bn), lambda j, i: (0, j)),
        scratch_shapes=[pltpu.VMEM((m, bn), F32)],
        compiler_params=_params(("arbitrary", "arbitrary"), 40),
    )(a, b)


def _adam_math(w, g, m, v):
    m = ADAM_B1 * m + (1.0 - ADAM_B1) * g
    v = ADAM_B2 * v + (1.0 - ADAM_B2) * (g * g)
    m_hat = m / (1.0 - ADAM_B1 ** ADAM_STEP)
    v_hat = v / (1.0 - ADAM_B2 ** ADAM_STEP)
    delta = -ADAM_LR * (m_hat / (jnp.sqrt(v_hat) + ADAM_EPS) + ADAM_WD * w)
    return delta, m, v


def _adamw(w, parts, m, v, name, tr):
    r, cdim = w.shape
    n_parts = parts.shape[0]

    def body(w_ref, p_ref, m_ref, v_ref, g_ref, d_ref, nm_ref, nv_ref):
        g = p_ref[0].astype(F32)
        for k in range(1, n_parts):
            g = g + p_ref[k].astype(F32)
        g_ref[...] = g
        d_ref[...], nm_ref[...], nv_ref[...] = _adam_math(w_ref[...], g, m_ref[...], v_ref[...])

    tile = pl.BlockSpec((tr, cdim), lambda i: (i, 0))
    shp = jax.ShapeDtypeStruct((r, cdim), F32)
    return pl.pallas_call(
        body, name=name, grid=(r // tr,), out_shape=(shp, shp, shp, shp),
        in_specs=[tile, pl.BlockSpec((n_parts, tr, cdim), lambda i: (0, i, 0)), tile, tile],
        out_specs=(tile, tile, tile, tile),
        compiler_params=_params(("arbitrary",), 40),
    )(w, parts, m, v)


def _adamw_layers(w, parts, m, v, name, tr):
    n_layers, r, cdim = w.shape

    def body(*refs):
        w_ref, p_refs, (m_ref, v_ref) = refs[0], refs[1:1 + n_layers], refs[1 + n_layers:3 + n_layers]
        g_ref, d_ref, nm_ref, nv_ref = refs[3 + n_layers:]
        for l, p_ref in enumerate(p_refs):
            @pl.when(pl.program_id(0) == l)
            def _(p_ref=p_ref):
                g = p_ref[0].astype(F32)
                for k in range(1, p_ref.shape[0]):
                    g = g + p_ref[k].astype(F32)
                g_ref[0] = g
                d_ref[0], nm_ref[0], nv_ref[0] = _adam_math(w_ref[0], g, m_ref[0], v_ref[0])

    tile = pl.BlockSpec((1, tr, cdim), lambda l, i: (l, i, 0))
    part = lambda own: pl.BlockSpec((parts[own].shape[0], tr, cdim), lambda l, i: (0, jnp.where(l == own, i, 0), 0))
    shp = jax.ShapeDtypeStruct(w.shape, F32)
    return pl.pallas_call(
        body, name=name, grid=(n_layers, r // tr), out_shape=(shp, shp, shp, shp),
        in_specs=[tile] + [part(l) for l in range(n_layers)] + [tile, tile],
        out_specs=(tile, tile, tile, tile),
        compiler_params=_params(("arbitrary", "arbitrary"), 40),
    )(w, *parts, m, v)


def _row(vec, width):
    vec = vec.reshape(1, -1)
    return jnp.pad(vec, ((0, 0), (0, width - vec.shape[1])))


def kernel(x, c, w_ada, b_ada, g_pre, w_in, w_gate_up, b_gate_up, g_gla, g_dil, w_out, g_post, loss_target, m_w_ada, m_b_ada, m_g_pre, m_w_in, m_w_gate_up, m_b_gate_up, m_g_gla, m_g_dil, m_w_out, m_g_post, v_w_ada, v_b_ada, v_g_pre, v_w_in, v_w_gate_up, v_b_gate_up, v_g_gla, v_g_dil, v_w_out, v_g_post):
    px, py, pc = _my_position()
    me = _linear(px, py, pc)
    xs = x[0]
    target = loss_target[0]
    s_len = xs.shape[0]
    assert s_len % (DIL_BLOCK * max(DIL_DILATIONS) * 2) == 0 and xs.shape[1] == D_MODEL

    c_all = _all_gather(jnp.pad(c, ((0, 7), (0, 0))), "gather_c").reshape(N_DEV, 8, D_MODEL)[:, 0]
    mod_part = _mod_fwd(c_all, w_ada)
    w_in_b, w_out_b = w_in.astype(BF16), w_out.astype(BF16)
    mod_all, wgu_all, w_in_all = _comm_call(
        "gather", [mod_part.reshape(DEPTH * N_DEV, ADA_SHARD), w_gate_up.reshape(DEPTH * GLA_LOWRANK, GU_SHARD),
                   w_in_b[0]], "gather_first")
    w_out_all = None
    mod_all = mod_all.reshape(N_DEV, DEPTH, N_DEV, ADA_SHARD)
    mod_mine = lax.dynamic_index_in_dim(mod_all, me, axis=2, keepdims=False)
    mod = jnp.transpose(mod_mine, (1, 0, 2)).reshape(DEPTH, 3 * D_MODEL) + b_ada
    wgu_full = jnp.transpose(wgu_all.reshape(N_DEV, DEPTH, GLA_LOWRANK, GU_SHARD), (1, 2, 0, 3)).reshape(
        DEPTH, GLA_LOWRANK, GU_COLS)
    wgu_pad = jnp.pad(wgu_full, ((0, 0), (0, LANE - GLA_LOWRANK), (0, 0))).astype(BF16)

    def kernel_w_in(gathered):
        return _w_in_to_kernel(gathered.reshape(N_DEV, D_MODEL, W_IN_SHARD))

    cos, sin_signed = _rope_tables(s_len)
    g_heads = jnp.concatenate([g_gla, g_dil], axis=1)

    saved = []
    xl = xs
    for l in range(DEPTH):
        shift, scale, gate = ((mod, l, k) for k in range(3))
        w_new = kernel_w_in(w_in_all)
        pf, pb, h = _prenorm_proj(xl, (g_pre, l, 0), scale, shift, w_new)
        if w_out_all is None:
            o_a, states, w_out_all = _gla_fwd(pf, pb, wgu_pad, b_gate_up, l, comm=("gather", [w_out_b[l]]))
        else:
            o_a, states = _gla_fwd(pf, pb, wgu_pad, b_gate_up, l)
        w_out_l = w_out_all
        if l + 1 < DEPTH:
            o_b, lse, w_in_all, w_out_all = _dil_fwd(pf, pb, cos, sin_signed,
                                                     comm=("gather", [w_in_b[l + 1], w_out_b[l + 1]]))
        else:
            o_b, lse = _dil_fwd(pf, pb, cos, sin_signed)
        x_next, y, u = _post_fwd(o_a, o_b, pf, (g_heads, l, 0), w_out_l, xl, gate, (g_post, l, 0))
        saved.append((xl, scale, gate, w_new, w_out_l, pf, pb, h, o_a, states, o_b, lse, y, u))
        xl = x_next

    dx, loss_part = _loss_grad(xl, target)

    small_rows = []
    gin_slots, gin_parts, gout_parts = None, [None] * DEPTH, [None] * DEPTH
    for l in reversed(range(DEPTH)):
        x_in, scale, gate, w_new, w_out_l, pf, pb, h, o_a, states, o_b, lse, y, u = saved[l]
        du, do, dz, sums_post = _post_bwd(dx, u, gate, (g_post, l, 0), w_out_l, o_a, o_b, pf, (g_heads, l, 0))
        gout_slots = _matmul_tn(y, du, "grad_w_out", 512)
        dq_a, dk_a, dv_a, dlr2, dwgu, dbgu = _gla_bwd(pf, pb, wgu_pad, b_gate_up, l, states, do)
        travelling = [gout_slots] + ([gin_slots] if gin_slots is not None else [])
        dq_b, dk_b, dv_b, *arrived = _dil_bwd(pf, pb, cos, sin_signed, do, o_b, lse, comm=("exchange", travelling))
        gout_parts[l] = arrived[0].reshape(N_DEV, OUT_SHARD, D_MODEL)
        if gin_slots is not None:
            gin_parts[l + 1] = arrived[1].reshape(N_DEV, D_MODEL, W_IN_SHARD)
        dlr = (dlr2[0] + dlr2[1]).astype(BF16)
        pieces = (dz, dq_a, dk_a, dq_b, dk_b, dv_a, dv_b, dlr)
        gin_slots = _grad_w_in(h, pieces).reshape(N_DEV * D_MODEL, W_IN_SHARD)
        if l == 0:
            dx, sums_in, arrived = _in_bwd(pieces, w_new, x_in, dx, (g_pre, l, 0), scale,
                                           comm=("exchange", [gin_slots]))
            gin_parts[0] = arrived.reshape(N_DEV, D_MODEL, W_IN_SHARD)
        else:
            dx, sums_in = _in_bwd(pieces, w_new, x_in, dx, (g_pre, l, 0), scale)
        dmod = jnp.concatenate([sums_in[0], sums_in[1], sums_post[0]])
        vecs = jnp.concatenate([sums_in[2], sums_post[1], sums_post[2], dbgu[0]])
        small_rows[0:0] = [_row(dmod, 4096), _row(vecs, 4096), _row(dwgu[:GLA_LOWRANK], 4096)]
    grad_x = dx[None]

    flat = lambda a, rows: a.reshape(rows, a.shape[-1])
    r_ada = DEPTH * D_MODEL
    g_w_in, d_w_in, nm_w_in, nv_w_in = _adamw_layers(w_in, gin_parts, m_w_in, v_w_in, "adamw_w_in", 256)
    g_w_out, d_w_out, nm_w_out, nv_w_out = _adamw_layers(w_out, gout_parts, m_w_out, v_w_out, "adamw_w_out", 128)

    small_rows += [_row(loss_part[0, 0:1], 4096), jnp.zeros((1, 4096), F32)]
    small = _all_gather(jnp.concatenate(small_rows, axis=0), "gather_small").reshape(N_DEV, 8, 4096)
    dmod_all = jnp.stack([small[:, 0, :3 * D_MODEL], small[:, 3, :3 * D_MODEL]])
    dmod_cols = lax.dynamic_slice_in_dim(dmod_all, me * ADA_SHARD, ADA_SHARD, axis=2)
    gwa = _w_ada_grad(c_all, dmod_cols).reshape(1, r_ada, ADA_SHARD)
    g_w_ada, d_w_ada, nm_w_ada, nv_w_ada = (
        t.reshape(w_ada.shape) for t in _adamw(flat(w_ada, r_ada), gwa, flat(m_w_ada, r_ada), flat(v_w_ada, r_ada),
                                               "adamw_w_ada", 256))

    def small_param(w, m, v, cols, row, name):
        n = w.shape[1]
        parts = jnp.stack([small[:, row, cols:cols + n], small[:, row + 3, cols:cols + n]], axis=1)
        return _adamw(w, parts, m, v, name, DEPTH)

    g_b_ada, d_b_ada, nm_b_ada, nv_b_ada = small_param(b_ada, m_b_ada, v_b_ada, 0, 0, "adamw_b_ada")
    g_g_pre, d_g_pre, nm_g_pre, nv_g_pre = small_param(g_pre, m_g_pre, v_g_pre, 0, 1, "adamw_g_pre")
    g_g_post, d_g_post, nm_g_post, nv_g_post = small_param(g_post, m_g_post, v_g_post, 1024, 1, "adamw_g_post")
    g_g_gla, d_g_gla, nm_g_gla, nv_g_gla = small_param(g_gla, m_g_gla, v_g_gla, 2048, 1, "adamw_g_gla")
    g_g_dil, d_g_dil, nm_g_dil, nv_g_dil = small_param(g_dil, m_g_dil, v_g_dil, 2560, 1, "adamw_g_dil")
    g_b_gu, d_b_gu, nm_b_gu, nv_b_gu = small_param(b_gate_up, m_b_gate_up, v_b_gate_up, 3072, 1, "adamw_b_gate_up")
    gu_parts = jnp.stack([small[:, 2], small[:, 5]], axis=1).reshape(N_DEV, DEPTH, GLA_LOWRANK, GU_COLS)
    gu_parts = lax.dynamic_slice_in_dim(gu_parts, me * GU_SHARD, GU_SHARD, axis=3).reshape(
        N_DEV, DEPTH * GLA_LOWRANK, GU_SHARD)
    r_gu = DEPTH * GLA_LOWRANK
    g_w_gu, d_w_gu, nm_w_gu, nv_w_gu = (
        t.reshape(w_gate_up.shape) for t in _adamw(flat(w_gate_up, r_gu), gu_parts, flat(m_w_gate_up, r_gu),
                                                   flat(v_w_gate_up, r_gu), "adamw_w_gate_up", r_gu))
    loss_parts = jnp.broadcast_to(small[:, 6, 0:1].reshape(N_DEV, 1, 1), (N_DEV, 8, LANE))
    loss = _sum_parts(loss_parts)[0, 0]

    return (loss, grad_x,
            g_w_ada, g_b_ada, g_g_pre, g_w_in, g_w_gu, g_b_gu, g_g_gla, g_g_dil, g_w_out, g_g_post,
            d_w_ada, d_b_ada, d_g_pre, d_w_in, d_w_gu, d_b_gu, d_g_gla, d_g_dil, d_w_out, d_g_post,
            nm_w_ada, nm_b_ada, nm_g_pre, nm_w_in, nm_w_gu, nm_b_gu, nm_g_gla, nm_g_dil, nm_w_out, nm_g_post,
            nv_w_ada, nv_b_ada, nv_g_pre, nv_w_in, nv_w_gu, nv_b_gu, nv_g_gla, nv_g_dil, nv_w_out, nv_g_post)


def _sum_parts(parts):
    n_parts = parts.shape[0]

    def body(p_ref, o_ref):
        acc = p_ref[0]
        for k in range(1, n_parts):
            acc = acc + p_ref[k]
        o_ref[...] = acc

    return pl.pallas_call(body, name="sum_loss", out_shape=jax.ShapeDtypeStruct(parts.shape[1:], F32))(parts)
```

```python
import functools
import math

import jax
import jax.numpy as jnp
from jax import lax
from jax.experimental import pallas as pl
from jax.experimental.pallas import tpu as pltpu

F32 = jnp.float32
BF16 = jnp.bfloat16

N_DEV = 8
D_MODEL = 1024
DEPTH = 2
GLA_HEADS = 4
GLA_DK = 64
GLA_DV = 128
GLA_CHUNK = 64
GLA_TAU = 16.0
GLA_LOWRANK = 16
DIL_HEADS = 4
DIL_HD = 128
DIL_BLOCK = 128
DIL_DILATIONS = (1, 4, 16)
ROPE_THETA = 10000.0
EPS = 1e-6
IN_COLS = 3600
W_IN_SHARD = IN_COLS // N_DEV
ADA_SHARD = 3 * D_MODEL // N_DEV
OUT_SHARD = D_MODEL // N_DEV
GU_COLS = GLA_HEADS * GLA_DK
GU_SHARD = GU_COLS // N_DEV

ADAM_LR = 0.001
ADAM_B1 = 0.9
ADAM_B2 = 0.999
ADAM_EPS = 1e-08
ADAM_WD = 0.01
ADAM_STEP = 10

NP = 3712
COL_Z, COL_QA, COL_KA, COL_QB, COL_KB, COL_VA, COL_VB, COL_LR = 0, 1024, 1280, 1536, 2048, 2560, 3072, 3584
NP_F32 = COL_VA
NP_BF16 = NP - NP_F32
LANE = 128
MASK_VALUE = -1e30

MESH = pl.DeviceIdType.MESH
ANY = pl.BlockSpec(memory_space=pl.ANY)


def _params(sem=None, vmem_mb=None):
    kw = {}
    if sem is not None:
        kw["dimension_semantics"] = sem
    if vmem_mb is not None:
        kw["vmem_limit_bytes"] = vmem_mb * 1024 * 1024
    return pltpu.CompilerParams(**kw)


def _dot(a, b):
    return jnp.dot(a, b, preferred_element_type=F32)


def _dot_nt(a, b):
    return lax.dot_general(a, b, (((1,), (1,)), ((), ())), preferred_element_type=F32)


def _dot_tn(a, b):
    return lax.dot_general(a, b, (((0,), (0,)), ((), ())), preferred_element_type=F32)


def _sigmoid(z):
    return 1.0 / (1.0 + jnp.exp(-z))


def _log_sigmoid(z):
    return jnp.minimum(z, 0.0) - jnp.log(1.0 + jnp.exp(-jnp.abs(z)))


def _rowvec(v, width=D_MODEL):
    arr, row, cb = v
    return arr.reshape(arr.shape[0], 1, arr.shape[1]), pl.BlockSpec((None, 1, width), lambda *_: (row, 0, cb))


def _my_position():
    return lax.axis_index("x"), lax.axis_index("y"), lax.axis_index("c")


def _linear(px, py, pc):
    return 4 * px + 2 * py + pc


def _gather_phase(phase, x_ref, out_ref, send_sem, recv_sem, local_sem):
    m = x_ref.shape[0]
    x, y, c = _my_position()
    me, sibling = (x, y, c), (x, y, 1 - c)
    chips = [(1 - x, y), (x, 1 - y), (1 - x, 1 - y)]

    def rows(px, py, pc):
        return out_ref.at[pl.ds(_linear(px, py, pc) * m, m), :]

    def copy(k, block, to, src=None):
        return pltpu.make_async_remote_copy(
            src_ref=rows(*block) if src is None else src, dst_ref=rows(*block),
            send_sem=send_sem(k), recv_sem=recv_sem(k), device_id=to, device_id_type=MESH)

    mine = pltpu.make_async_copy(x_ref, rows(*me), local_sem)
    first = [copy(0, me, sibling, src=x_ref)] + [copy(1 + j, me, (*chip, c), src=x_ref) for j, chip in enumerate(chips)]
    passed = [copy(4 + j, (*chip, c), sibling) for j, chip in enumerate(chips)]
    if phase == "start":
        mine.start()
        for cp in first:
            cp.start()
    elif phase == "forward":
        for j, chip in enumerate(chips):
            copy(1 + j, (*chip, c), me).wait_recv()
            passed[j].start()
    else:
        copy(0, sibling, me).wait_recv()
        for j, chip in enumerate(chips):
            copy(4 + j, (*chip, 1 - c), me).wait_recv()
        for cp in first + passed:
            cp.wait_send()
        mine.wait()


def _exchange_phase(phase, x_ref, out_ref, send_sem, recv_sem, local_sem):
    m = x_ref.shape[0] // N_DEV
    x, y, c = _my_position()
    me = _linear(x, y, c)

    def rows(ref, idx):
        return ref.at[pl.ds(idx * m, m), :]

    peers = [(1 - x if j & 4 else x, 1 - y if j & 2 else y, 1 - c if j & 1 else c) for j in range(1, N_DEV)]
    local = pltpu.make_async_copy(rows(x_ref, me), rows(out_ref, me), local_sem)
    sends = [pltpu.make_async_remote_copy(
        src_ref=rows(x_ref, _linear(*peer)), dst_ref=rows(out_ref, me),
        send_sem=send_sem(j), recv_sem=recv_sem(j), device_id=peer, device_id_type=MESH) for j, peer in enumerate(peers)]
    if phase == "start":
        local.start()
        for cp in sends:
            cp.start()
    else:
        for j, peer in enumerate(peers):
            pltpu.make_async_remote_copy(
                src_ref=rows(x_ref, _linear(*peer)), dst_ref=rows(out_ref, _linear(*peer)),
                send_sem=send_sem(j), recv_sem=recv_sem(j), device_id=peer, device_id_type=MESH).wait_recv()
        for cp in sends:
            cp.wait_send()
        local.wait()


def _pairsum_exchange_phase(phase, x_ref, out_refs, send_sem, recv_sem, local_sem):
    out_ref, stage_ref, pair_ref = out_refs
    m, n = x_ref.shape[0] // N_DEV, x_ref.shape[1]
    x, y, c = _my_position()
    mine = 2 * x + y
    chips = [(qx, qy) for qx in range(2) for qy in range(2)]
    others = [(1 - x, y), (x, 1 - y), (1 - x, 1 - y)]

    def rows(ref, idx):
        return ref.at[pl.ds(idx * m, m), :]

    def remote(src, dst, k, to):
        return pltpu.make_async_remote_copy(src_ref=src, dst_ref=dst, send_sem=send_sem(k), recv_sem=recv_sem(k),
                                            device_id=to, device_id_type=MESH)

    to_sibling = [remote(rows(x_ref, _linear(qx, qy, 1 - c)), rows(stage_ref, q), q, (x, y, 1 - c))
                  for q, (qx, qy) in enumerate(chips)]
    to_chips = [remote(rows(pair_ref, 2 * qx + qy), rows(out_ref, mine), 4 + j, (qx, qy, c))
                for j, (qx, qy) in enumerate(others)]
    keep = pltpu.make_async_copy(rows(pair_ref, mine), rows(out_ref, mine), local_sem)
    if phase == "start":
        for cp in to_sibling:
            cp.start()
    elif phase == "reduce":
        for cp in to_sibling:
            cp.wait_recv()

        def through_vmem(a_buf, b_buf, sems):
            loads = [pltpu.make_async_copy(rows(x_ref, _linear(qx, qy, c)), a_buf.at[q], sems.at[q])
                     for q, (qx, qy) in enumerate(chips)]
            loads += [pltpu.make_async_copy(rows(stage_ref, q), b_buf.at[q], sems.at[4 + q]) for q in range(4)]
            for cp in loads:
                cp.start()
            for cp in loads:
                cp.wait()
            tr = 128

            def add(t, carry):
                q, r = t // (m // tr), t % (m // tr)
                tile = pl.ds(pl.multiple_of(r * tr, tr), tr)
                a_buf[q, tile, :] = (a_buf[q, tile, :].astype(F32) + b_buf[q, tile, :].astype(F32)).astype(x_ref.dtype)
                return carry

            lax.fori_loop(0, 4 * (m // tr), add, 0)
            stores = [pltpu.make_async_copy(a_buf.at[q], rows(pair_ref, q), sems.at[8 + q]) for q in range(4)]
            for cp in stores:
                cp.start()
            for cp in stores:
                cp.wait()

        pl.run_scoped(through_vmem, pltpu.VMEM((4, m, n), x_ref.dtype), pltpu.VMEM((4, m, n), x_ref.dtype),
                      pltpu.SemaphoreType.DMA((12,)))
    elif phase == "send":
        keep.start()
        for cp in to_chips:
            cp.start()
    else:
        for j, (qx, qy) in enumerate(others):
            remote(rows(pair_ref, mine), rows(out_ref, 2 * qx + qy), 4 + j, (qx, qy, c)).wait_recv()
        for cp in to_sibling + to_chips:
            cp.wait_send()
        keep.wait()


_COMM_PHASES = {"gather": (_gather_phase, ("start", "forward", "finish")),
                "exchange": (_exchange_phase, ("start", "finish")),
                "pairsum_exchange": (_pairsum_exchange_phase, ("start", "reduce", "send", "finish"))}


def _comm_scratch(n_arrays):
    return [pltpu.SemaphoreType.DMA((n_arrays, 7)), pltpu.SemaphoreType.DMA((n_arrays, 7)),
            pltpu.SemaphoreType.DMA((n_arrays,))]


def _comm_run(kind, phases, x_refs, out_refs, send_sems, recv_sems, local_sems):
    fn = _COMM_PHASES[kind][0]
    per = len(out_refs) // len(x_refs)
    for phase in phases:
        for a, x_ref in enumerate(x_refs):
            outs = out_refs[a] if per == 1 else tuple(out_refs[per * a:per * (a + 1)])
            fn(phase, x_ref, outs, lambda k, a=a: send_sems.at[a, k], lambda k, a=a: recv_sems.at[a, k],
               local_sems.at[a])


def _comm_out_shapes(kind, arrays):
    if kind == "pairsum_exchange":
        return [jax.ShapeDtypeStruct((a.shape[0] // 2, a.shape[1]), a.dtype) for a in arrays for _ in range(3)]
    return [jax.ShapeDtypeStruct((N_DEV * a.shape[0], a.shape[1]) if kind == "gather" else a.shape, a.dtype)
            for a in arrays]


def _comm_call(kind, arrays, name):
    n = len(arrays)
    shapes = _comm_out_shapes(kind, arrays)

    def body(*refs):
        _comm_run(kind, _COMM_PHASES[kind][1], refs[:n], refs[n:n + len(shapes)], *refs[n + len(shapes):])

    return pl.pallas_call(body, name=name, out_shape=shapes, in_specs=[ANY] * n, out_specs=[ANY] * len(shapes),
                          scratch_shapes=_comm_scratch(n))(*arrays)


def _all_gather(xs, name):
    return _comm_call("gather", [xs], name)[0]


def _all_to_all(xs, name):
    return _comm_call("exchange", [xs], name)[0]


def _mod_fwd(c_all, w_ada):
    def body(c_ref, w_ref, o_ref):
        cv = c_ref[...]
        sc = cv * _sigmoid(cv)
        o_ref[0] = _dot(sc.astype(BF16), w_ref[0].astype(BF16))

    return pl.pallas_call(
        body, name="mod_fwd", grid=(DEPTH,),
        out_shape=jax.ShapeDtypeStruct((DEPTH, N_DEV, ADA_SHARD), F32),
        in_specs=[pl.BlockSpec((N_DEV, D_MODEL), lambda l: (0, 0)),
                  pl.BlockSpec((1, D_MODEL, ADA_SHARD), lambda l: (l, 0, 0))],
        out_specs=pl.BlockSpec((1, N_DEV, ADA_SHARD), lambda l: (l, 0, 0)),
        compiler_params=_params(("arbitrary",)),
    )(c_all, w_ada)


def _w_ada_grad(c_all, dmod_cols):
    def body(c_ref, d_ref, o_ref):
        cv = c_ref[...]
        sc = cv * _sigmoid(cv)
        o_ref[0] = lax.dot_general(sc, d_ref[0], (((0,), (0,)), ((), ())), precision=lax.Precision.HIGHEST,
                                   preferred_element_type=F32)

    return pl.pallas_call(
        body, name="w_ada_grad", grid=(DEPTH,),
        out_shape=jax.ShapeDtypeStruct((DEPTH, D_MODEL, ADA_SHARD), F32),
        in_specs=[pl.BlockSpec((N_DEV, D_MODEL), lambda l: (0, 0)),
                  pl.BlockSpec((1, N_DEV, ADA_SHARD), lambda l: (l, 0, 0))],
        out_specs=pl.BlockSpec((1, D_MODEL, ADA_SHARD), lambda l: (l, 0, 0)),
        compiler_params=_params(("arbitrary",)),
    )(c_all, dmod_cols)


def _prenorm_proj(x, g_pre, scale, shift, w_new, ts=256):
    s_len = x.shape[0]

    def body(x_ref, g_ref, sc_ref, sh_ref, w_ref, pf_ref, pb_ref, h_ref):
        xv = x_ref[...]
        rstd = lax.rsqrt(jnp.mean(xv * xv, axis=-1, keepdims=True) + EPS)
        h = (xv * rstd * g_ref[...]) * (1.0 + sc_ref[...]) + sh_ref[...]
        hb = h.astype(BF16)
        h_ref[...] = hb
        for j in range(0, NP, 512):
            w = min(512, NP - j)
            acc = _dot(hb, w_ref[:, j:j + w])
            if j < NP_F32:
                pf_ref[:, j:j + w] = acc
            else:
                pb_ref[:, j - NP_F32:j - NP_F32 + w] = acc.astype(BF16)

    (g_pre, g_spec), (scale, sc_spec), (shift, sh_spec) = _rowvec(g_pre), _rowvec(scale), _rowvec(shift)
    return pl.pallas_call(
        body, name="prenorm_proj", grid=(s_len // ts,),
        out_shape=(jax.ShapeDtypeStruct((s_len, NP_F32), F32), jax.ShapeDtypeStruct((s_len, NP_BF16), BF16),
                   jax.ShapeDtypeStruct((s_len, D_MODEL), BF16)),
        in_specs=[pl.BlockSpec((ts, D_MODEL), lambda i: (i, 0)), g_spec, sc_spec, sh_spec,
                  pl.BlockSpec((D_MODEL, NP), lambda i: (0, 0))],
        out_specs=(pl.BlockSpec((ts, NP_F32), lambda i: (i, 0)), pl.BlockSpec((ts, NP_BF16), lambda i: (i, 0)),
                   pl.BlockSpec((ts, D_MODEL), lambda i: (i, 0))),
        compiler_params=_params(("arbitrary",), 48),
    )(x, g_pre, scale, shift, w_new)


GLA_GROUP = 4


def _gla_group_rows(t):
    return [pl.ds(pl.multiple_of((t * GLA_GROUP + j) * GLA_CHUNK, GLA_CHUNK), GLA_CHUNK) for j in range(GLA_GROUP)]


def _gla_chunks_common(q_ref, k_ref, lr_ref, wgu_ref, bgu_ref, rows_list):
    c = GLA_CHUNK
    ri = lax.broadcasted_iota(jnp.int32, (c, c), 0)
    ci = lax.broadcasted_iota(jnp.int32, (c, c), 1)
    tril = (ri >= ci).astype(F32)
    zs = [_dot(lr_ref[rows, :], wgu_ref[...]) + bgu_ref[...] for rows in rows_list]
    las = [_log_sigmoid(z) * (1.0 / GLA_TAU) for z in zs]
    bs = [jnp.dot(tril, la, precision=lax.Precision.HIGHEST, preferred_element_type=F32) for la in las]
    out = []
    for rows, z, b in zip(rows_list, zs, bs):
        q = q_ref[rows, :] * (GLA_DK ** -0.5)
        k = k_ref[rows, :]
        bl = b[c - 1:c, :]
        out.append(dict(z=z, b=b, bl=bl, qe=q * jnp.exp(b), ke=k * jnp.exp(-b), kend=k * jnp.exp(bl - b),
                        dec=jnp.exp(bl)))
    return out, ri, ci


def _head_lane_mask(hh):
    return (lax.broadcasted_iota(jnp.int32, (1, LANE), 1) // GLA_DK) == hh


def _state_block_mask():
    r = lax.broadcasted_iota(jnp.int32, (2 * GLA_DV, LANE), 0) // GLA_DV
    cc = lax.broadcasted_iota(jnp.int32, (2 * GLA_DV, LANE), 1) // GLA_DK
    return r == cc


def _gla_fwd(pf, pb, wgu, bgu, layer, comm=None):
    s_len = pf.shape[0]
    nc = s_len // GLA_CHUNK
    ncomm = len(comm[1]) if comm else 0

    def body(*refs):
        q_ref, k_ref, v_ref, lr_ref, wgu_ref, bgu_ref = refs[:6]
        cin, (o_ref, st_ref), cout = refs[6:6 + ncomm], refs[6 + ncomm:8 + ncomm], refs[8 + ncomm:8 + 2 * ncomm]
        qe_s, cs_s, dec_s = refs[8 + 2 * ncomm:11 + 2 * ncomm]
        comm_before, comm_after = _comm_hooks(comm, cin, cout, refs[11 + 2 * ncomm:], steps=2)
        comm_before()
        bd = _state_block_mask()

        def local(t, carry):
            rows_list = _gla_group_rows(t)
            cm, ri, ci = _gla_chunks_common(q_ref, k_ref, lr_ref, wgu_ref, bgu_ref, rows_list)
            vs = [v_ref[rows, :] for rows in rows_list]
            kebs = [c["ke"].astype(BF16) for c in cm]
            a = [[jnp.where(ri >= ci, _dot_nt(jnp.where(_head_lane_mask(hh), c["qe"], 0.0).astype(BF16), keb), 0.0)
                  .astype(BF16) for hh in range(2)] for c, keb in zip(cm, kebs)]
            oi = [[_dot(ah[hh], v[:, hh * GLA_DV:(hh + 1) * GLA_DV]) for hh in range(2)] for ah, v in zip(a, vs)]
            cs = [jnp.where(bd, _dot_tn(v, c["kend"].astype(BF16)), 0.0) for c, v in zip(cm, vs)]
            for j, (rows, c) in enumerate(zip(rows_list, cm)):
                n = t * GLA_GROUP + j
                o_ref[rows, :] = jnp.concatenate(oi[j], axis=1)
                qe_s[rows, :] = c["qe"].astype(BF16)
                cs_s[n] = cs[j]
                dec_s[n] = jnp.broadcast_to(c["dec"], (8, LANE))
            return carry

        lax.fori_loop(0, nc // GLA_GROUP, local, 0)

        def scan(n, st):
            st_ref[0, n] = st.astype(BF16)
            return dec_s[n][0:1, :] * st + cs_s[n]

        lax.fori_loop(0, nc, scan, jnp.zeros((2 * GLA_DV, LANE), F32))

        def inter(t, carry):
            rows_list = _gla_group_rows(t)
            add = [_dot_nt(qe_s[rows, :], st_ref[0, t * GLA_GROUP + j]) for j, rows in enumerate(rows_list)]
            for rows, av in zip(rows_list, add):
                o_ref[rows, :] = o_ref[rows, :] + av
            return carry

        lax.fori_loop(0, nc // GLA_GROUP, inter, 0)
        comm_after()

    return pl.pallas_call(
        body, name="gla_fwd_comm" if comm else "gla_fwd", grid=(2,),
        out_shape=[jax.ShapeDtypeStruct((s_len, GLA_HEADS * GLA_DV), F32),
                   jax.ShapeDtypeStruct((2, nc, 2 * GLA_DV, LANE), BF16)] + (_comm_out_shapes(*comm) if comm else []),
        in_specs=[pl.BlockSpec((s_len, LANE), lambda g: (0, COL_QA // LANE + g)),
                  pl.BlockSpec((s_len, LANE), lambda g: (0, COL_KA // LANE + g)),
                  pl.BlockSpec((s_len, 2 * GLA_DV), lambda g: (0, (COL_VA - NP_F32) // (2 * GLA_DV) + g)),
                  pl.BlockSpec((s_len, LANE), lambda g: (0, (COL_LR - NP_F32) // LANE)),
                  pl.BlockSpec((None, LANE, LANE), lambda g: (layer, 0, g)),
                  pl.BlockSpec((None, 1, LANE), lambda g: (layer, 0, g))] + [ANY] * ncomm,
        out_specs=[pl.BlockSpec((s_len, 2 * GLA_DV), lambda g: (0, g)),
                   pl.BlockSpec((1, nc, 2 * GLA_DV, LANE), lambda g: (g, 0, 0, 0))] + [ANY] * ncomm,
        scratch_shapes=[pltpu.VMEM((s_len, LANE), BF16), pltpu.VMEM((nc, 2 * GLA_DV, LANE), F32),
                        pltpu.VMEM((nc, 8, LANE), F32)] + (_comm_scratch(ncomm) if comm else []),
        compiler_params=_params(("arbitrary",), 56),
    )(pf, pf, pb, pb, wgu, bgu.reshape(bgu.shape[0], 1, GU_COLS), *(comm[1] if comm else []))


def _rope_tables(s_len):
    inv_freq = ROPE_THETA ** (-jnp.arange(0, DIL_HD, 2, dtype=F32) / DIL_HD)
    ang = jnp.arange(s_len, dtype=F32)[:, None] * inv_freq[None, :]
    cos, sin = jnp.cos(ang), jnp.sin(ang)
    return jnp.concatenate([cos, cos], axis=1), jnp.concatenate([-sin, sin], axis=1)


def _rope(xv, cos, sin_signed):
    return xv * cos + pltpu.roll(xv, DIL_HD // 2, 1) * sin_signed


DIL_GROUP = 4
DIL_GROUP_FWD = 8


def _dil_pair_block(i, half, d, nblk, group=DIL_GROUP):
    nb = nblk // d
    j = i + half * (nblk // group)
    if nb >= 2 * group:
        r, n = j % d, j // d
    else:
        r, n = j // nb, j % nb
    kb = jnp.maximum(n - 1, 0)
    qs = r + d * DIL_BLOCK * n
    ks = r + d * DIL_BLOCK * kb
    return qs, ks, jnp.minimum(n, 1)


def _dil_fill_bias(bias):
    qi = lax.broadcasted_iota(jnp.int32, (DIL_BLOCK, 2 * DIL_BLOCK), 0)
    kj = lax.broadcasted_iota(jnp.int32, (DIL_BLOCK, 2 * DIL_BLOCK), 1)
    for sel in range(2):
        dist = qi - kj + DIL_BLOCK * sel
        bias[sel] = jnp.where((dist >= 0) & (dist <= DIL_BLOCK), 0.0, MASK_VALUE)


def _strided(start, size, d):
    return pl.ds(start, size) if d == 1 else pl.ds(start, size, stride=d)


def _comm_hooks(comm, cin, cout, csem, steps=DIL_HEADS):
    def before():
        if comm:
            @pl.when(pl.program_id(0) == 0)
            def _():
                _comm_run(comm[0], ("start",), cin, cout, *csem)

            if comm[0] == "gather":
                @pl.when(pl.program_id(0) == steps - 1)
                def _():
                    _comm_run(comm[0], ("forward",), cin, cout, *csem)

            if comm[0] == "pairsum_exchange":
                @pl.when(pl.program_id(0) == min(3, steps - 1))
                def _():
                    _comm_run(comm[0], ("reduce", "send"), cin, cout, *csem)

    def after():
        if comm:
            @pl.when(pl.program_id(0) == steps - 1)
            def _():
                _comm_run(comm[0], ("finish",), cin, cout, *csem)

    return before, after


def _dil_fwd(pf, pb, cos, sin_signed, comm=None):
    s_len = pf.shape[0]
    nblk = s_len // DIL_BLOCK
    prep_rows = 256
    scale = DIL_HD ** -0.5
    nc = len(comm[1]) if comm else 0

    def body(*refs):
        q_ref, k_ref, v_ref, cos_ref, sin_ref = refs[:5]
        cin, (o_ref, lse_ref), cout = refs[5:5 + nc], refs[5 + nc:7 + nc], refs[7 + nc:7 + 2 * nc]
        qf, kf, vf, o0, o1, o2, l0, l1, l2, bias = refs[7 + 2 * nc:17 + 2 * nc]
        comm_before, comm_after = _comm_hooks(comm, cin, cout, refs[17 + 2 * nc:])
        comm_before()
        _dil_fill_bias(bias)

        def prep(t, carry):
            rows = pl.ds(pl.multiple_of(t * prep_rows, prep_rows), prep_rows)
            cs, sn = cos_ref[rows, :], sin_ref[rows, :]
            qf[rows, :] = _rope(q_ref[rows, :], cs, sn)
            kf[rows, :] = _rope(k_ref[rows, :], cs, sn)
            vf[rows, :] = v_ref[rows, :].astype(F32)
            return carry

        lax.fori_loop(0, s_len // prep_rows, prep, 0)
        ones = jnp.ones((2 * DIL_BLOCK, DIL_HD), BF16)

        for d, o_p, l_p in zip(DIL_DILATIONS, (o0, o1, o2), (l0, l1, l2)):
            if nblk // d == 2:
                units = DIL_GROUP_FWD // 2

                def whole(i, carry, d=d, o_p=o_p, l_p=l_p, units=units):
                    rows = [_strided(i + u * (d // units), 2 * DIL_BLOCK, d) for u in range(units)]
                    ld = [(qf[rw, :].astype(BF16), kf[rw, :].astype(BF16), vf[rw, :].astype(BF16)) for rw in rows]
                    both = bias[...].reshape(2 * DIL_BLOCK, 2 * DIL_BLOCK)
                    s = [_dot_nt(qb, kk) * scale + both for qb, kk, _ in ld]
                    m = [jnp.max(sv, axis=-1, keepdims=True) for sv in s]
                    p = [jnp.exp(sv - mv) for sv, mv in zip(s, m)]
                    den = [jnp.sum(pv, axis=-1, keepdims=True) for pv in p]
                    r = [_dot(pv.astype(BF16), vv) for pv, (_, _, vv) in zip(p, ld)]
                    for rv, dv, mv, rw in zip(r, den, m, rows):
                        o_p[rw, :] = rv / dv
                        l_p[rw, :] = jnp.broadcast_to(mv + jnp.log(dv), (2 * DIL_BLOCK, DIL_HD))
                    return carry

                lax.fori_loop(0, d // units, whole, 0)
                continue

            def pair(i, carry, d=d, o_p=o_p, l_p=l_p):
                idx = [_dil_pair_block(i, half, d, nblk, DIL_GROUP_FWD) for half in range(DIL_GROUP_FWD)]
                ld = [(qf[_strided(qs, DIL_BLOCK, d), :].astype(BF16),
                       kf[_strided(ks, 2 * DIL_BLOCK, d), :].astype(BF16),
                       vf[_strided(ks, 2 * DIL_BLOCK, d), :].astype(BF16)) for qs, ks, _ in idx]
                s = [_dot_nt(qb, kk) * scale + bias[sel] for (qb, kk, _), (_, _, sel) in zip(ld, idx)]
                m = [jnp.max(sv, axis=-1, keepdims=True) for sv in s]
                p = [jnp.exp(sv - mv) for sv, mv in zip(s, m)]
                den = [jnp.sum(pv, axis=-1, keepdims=True) for pv in p]
                r = [_dot(pv.astype(BF16), vv) for pv, (_, _, vv) in zip(p, ld)]
                for rv, dv, mv, (qs, _, _) in zip(r, den, m, idx):
                    o_p[_strided(qs, DIL_BLOCK, d), :] = rv / dv
                    l_p[_strided(qs, DIL_BLOCK, d), :] = jnp.broadcast_to(mv + jnp.log(dv), (DIL_BLOCK, DIL_HD))
                return carry

            lax.fori_loop(0, nblk // DIL_GROUP_FWD, pair, 0)

        def comb(t, carry):
            rows = pl.ds(pl.multiple_of(t * prep_rows, prep_rows), prep_rows)
            a0, a1, a2 = l0[rows, :], l1[rows, :], l2[rows, :]
            m = jnp.maximum(jnp.maximum(a0, a1), a2)
            e0, e1, e2 = jnp.exp(a0 - m), jnp.exp(a1 - m), jnp.exp(a2 - m)
            tot = e0 + e1 + e2
            o_ref[rows, :] = (e0 * o0[rows, :] + e1 * o1[rows, :] + e2 * o2[rows, :]) / tot
            lse_ref[rows, :] = m + jnp.log(tot)
            return carry

        lax.fori_loop(0, s_len // prep_rows, comb, 0)
        comm_after()

    head = lambda base: pl.BlockSpec((s_len, DIL_HD), lambda h: (0, base // DIL_HD + h))
    table = pl.BlockSpec((s_len, DIL_HD), lambda h: (0, 0))
    out = pl.BlockSpec((s_len, DIL_HD), lambda h: (0, h))
    shp = jax.ShapeDtypeStruct((s_len, DIL_HEADS * DIL_HD), F32)
    return pl.pallas_call(
        body, name="dil_fwd_comm" if comm else "dil_fwd", grid=(DIL_HEADS,),
        out_shape=[shp, shp] + (_comm_out_shapes(*comm) if comm else []),
        in_specs=[head(COL_QB), head(COL_KB), head(COL_VB - NP_F32), table, table] + [ANY] * nc,
        out_specs=[out, out] + [ANY] * nc,
        scratch_shapes=[pltpu.VMEM((s_len, DIL_HD), F32) for _ in range(9)]
        + [pltpu.VMEM((2, DIL_BLOCK, 2 * DIL_BLOCK), F32)] + (_comm_scratch(nc) if comm else []),
        compiler_params=_params(("arbitrary",), 56),
    )(pf, pf, pb, cos, sin_signed, *(comm[1] if comm else []))


def _silu_and_grad(z):
    sg = _sigmoid(z)
    return z * sg, sg * (1.0 + z * (1.0 - sg))


def _post_fwd(o_a, o_b, pf, g_heads, w_out, x, gate, g_post, ts=256):
    s_len = x.shape[0]
    half = GLA_HEADS * GLA_DV

    def body(oa_ref, ob_ref, z_ref, gh_ref, w_ref, x_ref, gate_ref, gp_ref, xo_ref, y_ref, u_ref):
        for src, base in ((oa_ref, 0), (ob_ref, half)):
            for hh in range(4):
                lo = hh * LANE
                og = src[:, lo:lo + LANE]
                on = og * lax.rsqrt(jnp.mean(og * og, axis=-1, keepdims=True) + EPS)
                zg = z_ref[:, base + lo:base + lo + LANE].astype(F32)
                y_ref[:, base + lo:base + lo + LANE] = (on * gh_ref[:, base + lo:base + lo + LANE]
                                                        * (zg * _sigmoid(zg))).astype(BF16)
        u = _dot(y_ref[...], w_ref[...])
        u_ref[...] = u.astype(BF16)
        rstd = lax.rsqrt(jnp.mean(u * u, axis=-1, keepdims=True) + EPS)
        xo_ref[...] = x_ref[...] + gate_ref[...] * (u * rstd * gp_ref[...])

    (g_heads, gh_spec), (gate, gate_spec), (g_post, gp_spec) = _rowvec(g_heads), _rowvec(gate), _rowvec(g_post)
    tile = pl.BlockSpec((ts, D_MODEL), lambda i: (i, 0))
    halft = pl.BlockSpec((ts, half), lambda i: (i, 0))
    return pl.pallas_call(
        body, name="post_fwd", grid=(s_len // ts,),
        out_shape=(jax.ShapeDtypeStruct((s_len, D_MODEL), F32), jax.ShapeDtypeStruct((s_len, D_MODEL), BF16),
                   jax.ShapeDtypeStruct((s_len, D_MODEL), BF16)),
        in_specs=[halft, halft, tile, gh_spec, pl.BlockSpec((D_MODEL, D_MODEL), lambda i: (0, 0)), tile, gate_spec,
                  gp_spec],
        out_specs=(tile, tile, tile),
        compiler_params=_params(("arbitrary",), 40),
    )(o_a, o_b, pf, g_heads, w_out, x, gate, g_post)


def _loss_grad(y, target, ts=512):
    s_len = y.shape[0]

    def body(y_ref, t_ref, dy_ref, loss_ref):
        @pl.when(pl.program_id(0) == 0)
        def _():
            loss_ref[...] = jnp.zeros_like(loss_ref)

        e = y_ref[...] - t_ref[...]
        dy_ref[...] = e * (1.0 / D_MODEL)
        loss_ref[...] += 0.5 * jnp.sum(jnp.mean(e * e, axis=-1, keepdims=True))

    tile = pl.BlockSpec((ts, D_MODEL), lambda i: (i, 0))
    return pl.pallas_call(
        body, name="loss_grad", grid=(s_len // ts,),
        out_shape=(jax.ShapeDtypeStruct((s_len, D_MODEL), F32), jax.ShapeDtypeStruct((8, LANE), F32)),
        in_specs=[tile, tile], out_specs=(tile, pl.BlockSpec((8, LANE), lambda i: (0, 0))),
        compiler_params=_params(("arbitrary",)),
    )(y, target)


def _post_bwd(dxo, u, gate, g_post, w_out, o_a, o_b, pf, g_heads, ts=256):
    s_len = dxo.shape[0]
    half = GLA_HEADS * GLA_DV

    def body(dx_ref, u_ref, gate_ref, gp_ref, w_ref, oa_ref, ob_ref, z_ref, gh_ref, du_ref, do_ref, dz_ref, sums_ref):
        @pl.when(pl.program_id(0) == 0)
        def _():
            sums_ref[...] = jnp.zeros_like(sums_ref)

        dx = dx_ref[...]
        u = u_ref[...].astype(F32)
        rstd = lax.rsqrt(jnp.mean(u * u, axis=-1, keepdims=True) + EPS)
        un = u * rstd
        sums_ref[0:1, :] += jnp.sum(dx * (un * gp_ref[...]), axis=0, keepdims=True)
        drn = dx * gate_ref[...]
        sums_ref[1:2, :] += jnp.sum(drn * un, axis=0, keepdims=True)
        dun = drn * gp_ref[...]
        du = rstd * (dun - un * jnp.mean(dun * un, axis=-1, keepdims=True))
        dub = du.astype(BF16)
        du_ref[...] = dub
        dy = _dot_nt(dub, w_ref[...])
        for src, base in ((oa_ref, 0), (ob_ref, half)):
            for hh in range(4):
                lo = base + hh * LANE
                og = src[:, hh * LANE:(hh + 1) * LANE]
                rs = lax.rsqrt(jnp.mean(og * og, axis=-1, keepdims=True) + EPS)
                on = og * rs
                zg = z_ref[:, lo:lo + LANE].astype(F32)
                sz, dsz = _silu_and_grad(zg)
                gg = gh_ref[:, lo:lo + LANE]
                dyg = dy[:, lo:lo + LANE]
                sums_ref[2:3, lo:lo + LANE] += jnp.sum(dyg * sz * on, axis=0, keepdims=True)
                dz_ref[:, lo:lo + LANE] = (dyg * on * gg * dsz).astype(BF16)
                don = dyg * gg * sz
                do_ref[:, lo:lo + LANE] = (rs * (don - on * jnp.mean(don * on, axis=-1, keepdims=True))).astype(BF16)

    (g_heads, gh_spec), (gate, gate_spec), (g_post, gp_spec) = _rowvec(g_heads), _rowvec(gate), _rowvec(g_post)
    tile = pl.BlockSpec((ts, D_MODEL), lambda i: (i, 0))
    halft = pl.BlockSpec((ts, half), lambda i: (i, 0))
    return pl.pallas_call(
        body, name="post_bwd", grid=(s_len // ts,),
        out_shape=(jax.ShapeDtypeStruct((s_len, D_MODEL), BF16), jax.ShapeDtypeStruct((s_len, D_MODEL), BF16),
                   jax.ShapeDtypeStruct((s_len, D_MODEL), BF16), jax.ShapeDtypeStruct((8, D_MODEL), F32)),
        in_specs=[tile, tile, gate_spec, gp_spec, pl.BlockSpec((D_MODEL, D_MODEL), lambda i: (0, 0)), halft, halft,
                  tile, gh_spec],
        out_specs=(tile, tile, tile, pl.BlockSpec((8, D_MODEL), lambda i: (0, 0))),
        compiler_params=_params(("arbitrary",), 40),
    )(dxo, u, gate, g_post, w_out, o_a, o_b, pf, g_heads)


def _gla_bwd(pf, pb, wgu, bgu, layer, states, do):
    s_len = pf.shape[0]
    nc = s_len // GLA_CHUNK
    c = GLA_CHUNK

    def body(q_ref, k_ref, v_ref, lr_ref, wgu_ref, bgu_ref, st_ref, do_ref,
             dq_ref, dk_ref, dv_ref, dlr_ref, dwgu_ref, dbgu_ref, ds_s, dec_s, dw_acc, db_acc):
        dw_acc[...] = jnp.zeros_like(dw_acc)
        db_acc[...] = jnp.zeros_like(db_acc)
        bd = _state_block_mask()
        last_row = lax.broadcasted_iota(jnp.int32, (c, LANE), 0) == c - 1

        def local(t, carry):
            rows_list = _gla_group_rows(t)
            cm, _, _ = _gla_chunks_common(q_ref, k_ref, lr_ref, wgu_ref, bgu_ref, rows_list)
            loc = [jnp.where(bd, _dot_tn(do_ref[rows, :], cc["qe"].astype(BF16)), 0.0)
                   for rows, cc in zip(rows_list, cm)]
            for j, cc in enumerate(cm):
                ds_s[t * GLA_GROUP + j] = loc[j]
                dec_s[t * GLA_GROUP + j] = jnp.broadcast_to(cc["dec"], (8, LANE))
            return carry

        lax.fori_loop(0, nc // GLA_GROUP, local, 0)

        def scan(t, dst):
            n = nc - 1 - t
            loc = ds_s[n]
            ds_s[n] = dst
            return dec_s[n][0:1, :] * dst + loc

        lax.fori_loop(0, nc, scan, jnp.zeros((2 * GLA_DV, LANE), F32))

        def rest(t, carry):
            rows_list = _gla_group_rows(t)
            cm, ri, ci = _gla_chunks_common(q_ref, k_ref, lr_ref, wgu_ref, bgu_ref, rows_list)
            ns = [t * GLA_GROUP + j for j in range(GLA_GROUP)]
            vs = [v_ref[rows, :] for rows in rows_list]
            dobs = [do_ref[rows, :] for rows in rows_list]
            stbs = [st_ref[0, n] for n in ns]
            dsts = [ds_s[n] for n in ns]
            dstbs = [d.astype(BF16) for d in dsts]
            qebs = [cc["qe"].astype(BF16) for cc in cm]
            kebs = [cc["ke"].astype(BF16) for cc in cm]
            kendbs = [cc["kend"].astype(BF16) for cc in cm]
            hms = [_head_lane_mask(hh) for hh in range(2)]
            qehs = [[jnp.where(hm, cc["qe"], 0.0).astype(BF16) for hm in hms] for cc in cm]
            kehs = [[jnp.where(hm, cc["ke"], 0.0).astype(BF16) for hm in hms] for cc in cm]
            heads = lambda x: [x[:, hh * GLA_DV:(hh + 1) * GLA_DV] for hh in range(2)]
            vhs, dohs = [heads(v) for v in vs], [heads(d) for d in dobs]

            dqe0 = [_dot(dob, stb) for dob, stb in zip(dobs, stbs)]
            dkend = [_dot(v, dstb) for v, dstb in zip(vs, dstbs)]
            dv0 = [_dot_nt(kb, dstb) for kb, dstb in zip(kendbs, dstbs)]
            a_t = [[jnp.where(ci >= ri, _dot_nt(kehs[j][hh], qebs[j]), 0.0).astype(BF16) for hh in range(2)]
                   for j in range(GLA_GROUP)]
            da = [[jnp.where(ri >= ci, _dot_nt(dohs[j][hh], vhs[j][hh]), 0.0).astype(BF16) for hh in range(2)]
                  for j in range(GLA_GROUP)]
            da_t = [[jnp.where(ci >= ri, _dot_nt(vhs[j][hh], dohs[j][hh]), 0.0).astype(BF16) for hh in range(2)]
                    for j in range(GLA_GROUP)]
            dv1 = [[_dot(a_t[j][hh], dohs[j][hh]) for hh in range(2)] for j in range(GLA_GROUP)]
            dqe1 = [[_dot(da[j][hh], kebs[j]) for hh in range(2)] for j in range(GLA_GROUP)]
            dke1 = [[_dot(da_t[j][hh], qehs[j][hh]) for hh in range(2)] for j in range(GLA_GROUP)]

            dbs, dzs = [], []
            for j, (rows, cc) in enumerate(zip(rows_list, cm)):
                qe, ke, kend, b, bl = cc["qe"], cc["ke"], cc["kend"], cc["b"], cc["bl"]
                dqe = dqe0[j] + jnp.where(hms[0], dqe1[j][0], 0.0) + jnp.where(hms[1], dqe1[j][1], 0.0)
                dke = jnp.where(hms[0], dke1[j][0], 0.0) + jnp.where(hms[1], dke1[j][1], 0.0)
                dv_ref[rows, :] = (dv0[j] + jnp.concatenate(dv1[j], axis=1)).astype(BF16)
                dq_ref[rows, :] = (dqe * jnp.exp(b) * (GLA_DK ** -0.5)).astype(BF16)
                dk_ref[rows, :] = (dke * jnp.exp(-b) + dkend[j] * jnp.exp(bl - b)).astype(BF16)
                ddec = jnp.sum(dsts[j] * stbs[j].astype(F32), axis=0, keepdims=True)
                dbl = jnp.sum(dkend[j] * kend, axis=0, keepdims=True) + ddec * cc["dec"]
                dbs.append(dqe * qe - dke * ke - dkend[j] * kend + jnp.where(last_row, dbl, 0.0))
            triu = (ci >= ri).astype(F32)
            dlas = [jnp.dot(triu, db, precision=lax.Precision.HIGHEST, preferred_element_type=F32) for db in dbs]
            dzs = [dla * (1.0 / GLA_TAU) * _sigmoid(-cc["z"]) for dla, cc in zip(dlas, cm)]
            dzbs = [dz.astype(BF16) for dz in dzs]
            dlrs = [_dot_nt(dzb, wgu_ref[...]) for dzb in dzbs]
            dws = [_dot_tn(lr_ref[rows, :], dzb) for rows, dzb in zip(rows_list, dzbs)]
            for rows, dlr in zip(rows_list, dlrs):
                dlr_ref[0, rows, :] = dlr
            dw_acc[...] += functools.reduce(lambda x, y: x + y, dws)
            db_acc[0:1, :] += jnp.sum(functools.reduce(lambda x, y: x + y, dzs), axis=0, keepdims=True)
            return carry

        lax.fori_loop(0, nc // GLA_GROUP, rest, 0)
        dwgu_ref[...] = dw_acc[...]
        dbgu_ref[...] = db_acc[...]

    pair = pl.BlockSpec((s_len, LANE), lambda g: (0, g))
    return pl.pallas_call(
        body, name="gla_bwd", grid=(2,),
        out_shape=(jax.ShapeDtypeStruct((s_len, GU_COLS), BF16), jax.ShapeDtypeStruct((s_len, GU_COLS), BF16),
                   jax.ShapeDtypeStruct((s_len, GLA_HEADS * GLA_DV), BF16),
                   jax.ShapeDtypeStruct((2, s_len, LANE), F32),
                   jax.ShapeDtypeStruct((LANE, GU_COLS), F32), jax.ShapeDtypeStruct((8, GU_COLS), F32)),
        in_specs=[pl.BlockSpec((s_len, LANE), lambda g: (0, COL_QA // LANE + g)),
                  pl.BlockSpec((s_len, LANE), lambda g: (0, COL_KA // LANE + g)),
                  pl.BlockSpec((s_len, 2 * GLA_DV), lambda g: (0, (COL_VA - NP_F32) // (2 * GLA_DV) + g)),
                  pl.BlockSpec((s_len, LANE), lambda g: (0, (COL_LR - NP_F32) // LANE)),
                  pl.BlockSpec((None, LANE, LANE), lambda g: (layer, 0, g)),
                  pl.BlockSpec((None, 1, LANE), lambda g: (layer, 0, g)),
                  pl.BlockSpec((1, nc, 2 * GLA_DV, LANE), lambda g: (g, 0, 0, 0)),
                  pl.BlockSpec((s_len, 2 * GLA_DV), lambda g: (0, g))],
        out_specs=(pair, pair, pl.BlockSpec((s_len, 2 * GLA_DV), lambda g: (0, g)),
                   pl.BlockSpec((1, s_len, LANE), lambda g: (g, 0, 0)),
                   pl.BlockSpec((LANE, LANE), lambda g: (0, g)), pl.BlockSpec((8, LANE), lambda g: (0, g))),
        scratch_shapes=[pltpu.VMEM((nc, 2 * GLA_DV, LANE), F32), pltpu.VMEM((nc, 8, LANE), F32),
                        pltpu.VMEM((LANE, LANE), F32), pltpu.VMEM((8, LANE), F32)],
        compiler_params=_params(("arbitrary",), 56),
    )(pf, pf, pb, pb, wgu, bgu.reshape(bgu.shape[0], 1, GU_COLS), states, do)


def _dil_bwd(pf, pb, cos, sin_signed, do, o_b, lse, comm=None):
    s_len = pf.shape[0]
    nblk = s_len // DIL_BLOCK
    prep_rows = 256
    scale = DIL_HD ** -0.5
    nc = len(comm[1]) if comm else 0

    def body(*refs):
        q_ref, k_ref, v_ref, cos_ref, sin_ref, do_ref, o_ref, lse_ref = refs[:8]
        cin, (dq_ref, dk_ref, dv_ref), cout = refs[8:8 + nc], refs[8 + nc:11 + nc], refs[11 + nc:11 + 2 * nc]
        qf, kf, vf, dof, dl, dqa, dka, dva, bias = refs[11 + 2 * nc:20 + 2 * nc]
        comm_before, comm_after = _comm_hooks(comm, cin, cout, refs[20 + 2 * nc:])
        comm_before()
        _dil_fill_bias(bias)

        def prep(t, carry):
            rows = pl.ds(pl.multiple_of(t * prep_rows, prep_rows), prep_rows)
            cs, sn = cos_ref[rows, :], sin_ref[rows, :]
            qf[rows, :] = _rope(q_ref[rows, :], cs, sn) * scale
            kf[rows, :] = _rope(k_ref[rows, :], cs, sn)
            vf[rows, :] = v_ref[rows, :].astype(F32)
            dov = do_ref[rows, :].astype(F32)
            dof[rows, :] = dov
            dl[rows, :] = jnp.broadcast_to(jnp.sum(dov * o_ref[rows, :], axis=-1, keepdims=True), (prep_rows, DIL_HD))
            zero = jnp.zeros((prep_rows, DIL_HD), F32)
            dqa[rows, :] = zero
            dka[rows, :] = zero
            dva[rows, :] = zero
            return carry

        lax.fori_loop(0, s_len // prep_rows, prep, 0)

        for d in DIL_DILATIONS:
            if nblk // d == 2:
                units = DIL_GROUP // 2

                def whole(i, carry, d=d, units=units):
                    rows = [_strided(i + u * (d // units), 2 * DIL_BLOCK, d) for u in range(units)]
                    ld = [(qf[rw, :].astype(BF16), kf[rw, :].astype(BF16), vf[rw, :].astype(BF16),
                           dof[rw, :].astype(BF16)) for rw in rows]
                    both = bias[...].reshape(2 * DIL_BLOCK, 2 * DIL_BLOCK)
                    s = [_dot_nt(qb, kk) + both for qb, kk, _, _ in ld]
                    dp = [_dot_nt(dob, vv) for _, _, vv, dob in ld]
                    p = [jnp.exp(sv - lse_ref[rw, :][:, 0:1]) for sv, rw in zip(s, rows)]
                    ds = [(pv * (dpv - dl[rw, :][:, 0:1])).astype(BF16) for pv, dpv, rw in zip(p, dp, rows)]
                    pb = [pv.astype(BF16) for pv in p]
                    gq = [_dot(dsv, kk) for dsv, (_, kk, _, _) in zip(ds, ld)]
                    gk = [_dot_tn(dsv, qb) for dsv, (qb, _, _, _) in zip(ds, ld)]
                    gv = [_dot_tn(pv, dob) for pv, (_, _, _, dob) in zip(pb, ld)]
                    for rw, a, b, c in zip(rows, gq, gk, gv):
                        dqa[rw, :] += a
                        dka[rw, :] += b
                        dva[rw, :] += c
                    return carry

                lax.fori_loop(0, d // units, whole, 0)
                continue

            def pair(i, carry, d=d):
                idx = [_dil_pair_block(i, half, d, nblk) for half in range(DIL_GROUP)]
                rows = [(_strided(qs, DIL_BLOCK, d), _strided(ks, 2 * DIL_BLOCK, d)) for qs, ks, _ in idx]
                ld = [(qf[qr, :].astype(BF16), kf[kr, :].astype(BF16), vf[kr, :].astype(BF16),
                       dof[qr, :].astype(BF16)) for qr, kr in rows]
                s = [_dot_nt(qb, kk) + bias[sel] for (qb, kk, _, _), (_, _, sel) in zip(ld, idx)]
                dp = [_dot_nt(dob, vv) for _, _, vv, dob in ld]
                p = [jnp.exp(sv - lse_ref[qr, :][:, 0:1]) for sv, (qr, _) in zip(s, rows)]
                ds = [(pv * (dpv - dl[qr, :][:, 0:1])).astype(BF16) for pv, dpv, (qr, _) in zip(p, dp, rows)]
                pb = [pv.astype(BF16) for pv in p]
                gq = [_dot(dsv, kk) for dsv, (_, kk, _, _) in zip(ds, ld)]
                gk = [_dot_tn(dsv, qb) for dsv, (qb, _, _, _) in zip(ds, ld)]
                gv = [_dot_tn(pv, dob) for pv, (_, _, _, dob) in zip(pb, ld)]
                for (qr, kr), a, b, c in zip(rows, gq, gk, gv):
                    dqa[qr, :] += a
                    dka[kr, :] += b
                    dva[kr, :] += c
                return carry

            lax.fori_loop(0, nblk // DIL_GROUP, pair, 0)

        def fin(t, carry):
            rows = pl.ds(pl.multiple_of(t * prep_rows, prep_rows), prep_rows)
            cs, sn = cos_ref[rows, :], sin_ref[rows, :]
            gq, gk = dqa[rows, :] * scale, dka[rows, :]
            dq_ref[rows, :] = (gq * cs - pltpu.roll(gq, DIL_HD // 2, 1) * sn).astype(BF16)
            dk_ref[rows, :] = (gk * cs - pltpu.roll(gk, DIL_HD // 2, 1) * sn).astype(BF16)
            dv_ref[rows, :] = dva[rows, :].astype(BF16)
            return carry

        lax.fori_loop(0, s_len // prep_rows, fin, 0)
        comm_after()

    head = lambda base: pl.BlockSpec((s_len, DIL_HD), lambda h: (0, base // DIL_HD + h))
    table = pl.BlockSpec((s_len, DIL_HD), lambda h: (0, 0))
    out = pl.BlockSpec((s_len, DIL_HD), lambda h: (0, h))
    shp = jax.ShapeDtypeStruct((s_len, DIL_HEADS * DIL_HD), BF16)
    return pl.pallas_call(
        body, name="dil_bwd_comm" if comm else "dil_bwd", grid=(DIL_HEADS,),
        out_shape=[shp, shp, shp] + (_comm_out_shapes(*comm) if comm else []),
        in_specs=[head(COL_QB), head(COL_KB), head(COL_VB - NP_F32), table, table,
                  pl.BlockSpec((s_len, DIL_HD), lambda h: (0, DIL_HEADS + h)), out, out] + [ANY] * nc,
        out_specs=[out, out, out] + [ANY] * nc,
        scratch_shapes=[pltpu.VMEM((s_len, DIL_HD), F32) for _ in range(8)]
        + [pltpu.VMEM((2, DIL_BLOCK, 2 * DIL_BLOCK), F32)] + (_comm_scratch(nc) if comm else []),
        compiler_params=_params(("arbitrary",), 56),
    )(pf, pf, pb, cos, sin_signed, do, o_b, lse, *(comm[1] if comm else []))


_PIECES = ((COL_Z, 1024), (COL_QA, 256), (COL_KA, 256), (COL_QB, 512), (COL_KB, 512), (COL_VA, 512), (COL_VB, 512),
           (COL_LR, 128))


def _in_bwd(pieces, w_new, x, dxo, g_pre, scale, comm=None, ts=256):
    s_len = x.shape[0]
    nc = len(comm[1]) if comm else 0
    nco = len(_comm_out_shapes(*comm)) if comm else 0
    npc = len(_PIECES)

    def body(*refs):
        p_refs = refs[:npc]
        w_ref, x_ref, dxo_ref, g_ref, sc_ref = refs[npc:npc + 5]
        cin, (dx_ref, sums_ref), cout = (refs[npc + 5:npc + 5 + nc], refs[npc + 5 + nc:npc + 7 + nc],
                                         refs[npc + 7 + nc:npc + 7 + nc + nco])
        comm_before, comm_after = _comm_hooks(comm, cin, cout, refs[npc + 7 + nc + nco:], steps=s_len // ts)
        comm_before()

        @pl.when(pl.program_id(0) == 0)
        def _():
            sums_ref[...] = jnp.zeros_like(sums_ref)

        dh = jnp.zeros((ts, D_MODEL), F32)
        for p_ref, (col, width) in zip(p_refs, _PIECES):
            dh += _dot_nt(p_ref[...], w_ref[:, col:col + width])
        xv = x_ref[...]
        rstd = lax.rsqrt(jnp.mean(xv * xv, axis=-1, keepdims=True) + EPS)
        xn = xv * rstd
        sums_ref[0:1, :] += jnp.sum(dh, axis=0, keepdims=True)
        sums_ref[1:2, :] += jnp.sum(dh * (xn * g_ref[...]), axis=0, keepdims=True)
        dr = dh * (1.0 + sc_ref[...])
        sums_ref[2:3, :] += jnp.sum(dr * xn, axis=0, keepdims=True)
        dxn = dr * g_ref[...]
        dx_ref[...] = dxo_ref[...] + rstd * (dxn - xn * jnp.mean(dxn * xn, axis=-1, keepdims=True))
        comm_after()

    (g_pre, g_spec), (scale, sc_spec) = _rowvec(g_pre), _rowvec(scale)
    tile = pl.BlockSpec((ts, D_MODEL), lambda i: (i, 0))
    return pl.pallas_call(
        body, name="in_bwd_comm" if comm else "in_bwd", grid=(s_len // ts,),
        out_shape=[jax.ShapeDtypeStruct((s_len, D_MODEL), F32), jax.ShapeDtypeStruct((8, D_MODEL), F32)]
        + (_comm_out_shapes(*comm) if comm else []),
        in_specs=[pl.BlockSpec((ts, width), lambda i: (i, 0)) for _, width in _PIECES]
        + [pl.BlockSpec((D_MODEL, NP), lambda i: (0, 0)), tile, tile, g_spec, sc_spec] + [ANY] * nc,
        out_specs=[tile, pl.BlockSpec((8, D_MODEL), lambda i: (0, 0))] + [ANY] * nco,
        scratch_shapes=_comm_scratch(nc) if comm else [],
        compiler_params=_params(("arbitrary",), 56),
    )(*pieces, w_new, x, dxo, g_pre, scale, *(comm[1] if comm else []))


def _w_in_to_kernel(gathered, tr=128):
    def body(g_ref, o_ref):
        cols = jnp.concatenate([g_ref[k].astype(F32) for k in range(N_DEV)], axis=1)
        pad = jnp.zeros((tr, LANE - GLA_LOWRANK), F32)
        o_ref[...] = jnp.concatenate(
            [cols[:, 1024:1536], cols[:, 3088:3600], cols[:, 0:512], cols[:, 1552:2576], cols[:, 512:1024],
             cols[:, 2576:3088], cols[:, 1536:1552], pad], axis=1).astype(BF16)

    return pl.pallas_call(
        body, name="w_in_to_kernel", grid=(D_MODEL // tr,), out_shape=jax.ShapeDtypeStruct((D_MODEL, NP), BF16),
        in_specs=[pl.BlockSpec((N_DEV, tr, W_IN_SHARD), lambda i: (0, i, 0))],
        out_specs=pl.BlockSpec((tr, NP), lambda i: (i, 0)),
        compiler_params=_params(("arbitrary",)),
    )(gathered)


def _grad_w_in(h, pieces, ts=512, tr=128):
    s_len = h.shape[0]
    steps = s_len // ts

    def body(*refs):
        h_ref, p_refs = refs[0], refs[1:1 + len(_PIECES)]
        o_ref, acc = refs[1 + len(_PIECES):]

        @pl.when(pl.program_id(0) == 0)
        def _():
            acc[...] = jnp.zeros_like(acc)

        hv = h_ref[...]
        for p_ref, (col, width) in zip(p_refs, _PIECES):
            acc[:, col:col + width] += _dot_tn(hv, p_ref[...])

        @pl.when(pl.program_id(0) == steps - 1)
        def _():
            def rows_out(t, carry):
                rows = pl.ds(pl.multiple_of(t * tr, tr), tr)
                g = acc[rows, :]
                cols = jnp.concatenate(
                    [g[:, COL_QA:COL_QB], g[:, COL_VA:COL_VB], g[:, 0:512], g[:, COL_LR:COL_LR + GLA_LOWRANK],
                     g[:, COL_QB:COL_VA], g[:, COL_VB:COL_LR], g[:, 512:1024]], axis=1)
                for k in range(N_DEV):
                    o_ref[k, rows, :] = cols[:, W_IN_SHARD * k:W_IN_SHARD * (k + 1)].astype(BF16)
                return carry

            lax.fori_loop(0, D_MODEL // tr, rows_out, 0)

    return pl.pallas_call(
        body, name="grad_w_in", grid=(steps,),
        out_shape=jax.ShapeDtypeStruct((N_DEV, D_MODEL, W_IN_SHARD), BF16),
        in_specs=[pl.BlockSpec((ts, D_MODEL), lambda i: (i, 0))]
        + [pl.BlockSpec((ts, width), lambda i: (i, 0)) for _, width in _PIECES],
        out_specs=pl.BlockSpec((N_DEV, D_MODEL, W_IN_SHARD), lambda i: (0, 0, 0)),
        scratch_shapes=[pltpu.VMEM((D_MODEL, NP), F32)],
        compiler_params=_params(("arbitrary",), 56),
    )(h, *pieces)


def _matmul_tn(a, b, name, bn, ts=512):
    s_len, m = a.shape
    n = b.shape[1]
    steps = s_len // ts

    def body(a_ref, b_ref, o_ref, acc):
        @pl.when(pl.program_id(1) == 0)
        def _():
            acc[...] = jnp.zeros_like(acc)

        acc[...] += _dot_tn(a_ref[...], b_ref[...])

        @pl.when(pl.program_id(1) == steps - 1)
        def _():
            o_ref[...] = acc[...].astype(BF16)

    return pl.pallas_call(
        body, name=name, grid=(n // bn, steps),
        out_shape=jax.ShapeDtypeStruct((m, n), BF16),
        in_specs=[pl.BlockSpec((ts, m), lambda j, i: (i, 0)), pl.BlockSpec((ts, bn), lambda j, i: (i, j))],
        out_specs=pl.BlockSpec((m, bn), lambda j, i: (0, j)),
        scratch_shapes=[pltpu.VMEM((m, bn), F32)],
        compiler_params=_params(("arbitrary", "arbitrary"), 40),
    )(a, b)


def _adam_math(w, g, m, v):
    m = ADAM_B1 * m + (1.0 - ADAM_B1) * g
    v = ADAM_B2 * v + (1.0 - ADAM_B2) * (g * g)
    m_hat = m / (1.0 - ADAM_B1 ** ADAM_STEP)
    v_hat = v / (1.0 - ADAM_B2 ** ADAM_STEP)
    delta = -ADAM_LR * (m_hat / (jnp.sqrt(v_hat) + ADAM_EPS) + ADAM_WD * w)
    return delta, m, v


def _adamw(w, parts, m, v, name, tr):
    r, cdim = w.shape
    n_parts = parts.shape[0]

    def body(w_ref, p_ref, m_ref, v_ref, g_ref, d_ref, nm_ref, nv_ref):
        g = p_ref[0].astype(F32)
        for k in range(1, n_parts):
            g = g + p_ref[k].astype(F32)
        g_ref[...] = g
        d_ref[...], nm_ref[...], nv_ref[...] = _adam_math(w_ref[...], g, m_ref[...], v_ref[...])

    tile = pl.BlockSpec((tr, cdim), lambda i: (i, 0))
    shp = jax.ShapeDtypeStruct((r, cdim), F32)
    return pl.pallas_call(
        body, name=name, grid=(r // tr,), out_shape=(shp, shp, shp, shp),
        in_specs=[tile, pl.BlockSpec((n_parts, tr, cdim), lambda i: (0, i, 0)), tile, tile],
        out_specs=(tile, tile, tile, tile),
        compiler_params=_params(("arbitrary",), 40),
    )(w, parts, m, v)


def _adamw_layers(w, parts, m, v, name, tr):
    n_layers, r, cdim = w.shape

    def body(*refs):
        w_ref, p_refs, (m_ref, v_ref) = refs[0], refs[1:1 + n_layers], refs[1 + n_layers:3 + n_layers]
        g_ref, d_ref, nm_ref, nv_ref = refs[3 + n_layers:]
        for l, p_ref in enumerate(p_refs):
            @pl.when(pl.program_id(0) == l)
            def _(p_ref=p_ref):
                g = p_ref[0].astype(F32)
                for k in range(1, p_ref.shape[0]):
                    g = g + p_ref[k].astype(F32)
                g_ref[0] = g
                d_ref[0], nm_ref[0], nv_ref[0] = _adam_math(w_ref[0], g, m_ref[0], v_ref[0])

    tile = pl.BlockSpec((1, tr, cdim), lambda l, i: (l, i, 0))
    part = lambda own: pl.BlockSpec((parts[own].shape[0], tr, cdim), lambda l, i: (0, jnp.where(l == own, i, 0), 0))
    shp = jax.ShapeDtypeStruct(w.shape, F32)
    return pl.pallas_call(
        body, name=name, grid=(n_layers, r // tr), out_shape=(shp, shp, shp, shp),
        in_specs=[tile] + [part(l) for l in range(n_layers)] + [tile, tile],
        out_specs=(tile, tile, tile, tile),
        compiler_params=_params(("arbitrary", "arbitrary"), 40),
    )(w, *parts, m, v)


def _row(vec, width):
    vec = vec.reshape(1, -1)
    return jnp.pad(vec, ((0, 0), (0, width - vec.shape[1])))


def kernel(x, c, w_ada, b_ada, g_pre, w_in, w_gate_up, b_gate_up, g_gla, g_dil, w_out, g_post, loss_target, m_w_ada, m_b_ada, m_g_pre, m_w_in, m_w_gate_up, m_b_gate_up, m_g_gla, m_g_dil, m_w_out, m_g_post, v_w_ada, v_b_ada, v_g_pre, v_w_in, v_w_gate_up, v_b_gate_up, v_g_gla, v_g_dil, v_w_out, v_g_post):
    px, py, pc = _my_position()
    me = _linear(px, py, pc)
    xs = x[0]
    target = loss_target[0]
    s_len = xs.shape[0]
    assert s_len % (DIL_BLOCK * max(DIL_DILATIONS) * 2) == 0 and xs.shape[1] == D_MODEL

    c_all = _all_gather(jnp.pad(c, ((0, 7), (0, 0))), "gather_c").reshape(N_DEV, 8, D_MODEL)[:, 0]
    mod_part = _mod_fwd(c_all, w_ada)
    w_in_b, w_out_b = w_in.astype(BF16), w_out.astype(BF16)
    mod_all, wgu_all, w_in_all = _comm_call(
        "gather", [mod_part.reshape(DEPTH * N_DEV, ADA_SHARD), w_gate_up.reshape(DEPTH * GLA_LOWRANK, GU_SHARD),
                   w_in_b[0]], "gather_first")
    mod_all = mod_all.reshape(N_DEV, DEPTH, N_DEV, ADA_SHARD)
    mod_mine = lax.dynamic_index_in_dim(mod_all, me, axis=2, keepdims=False)
    mod = jnp.transpose(mod_mine, (1, 0, 2)).reshape(DEPTH, 3 * D_MODEL) + b_ada
    wgu_full = jnp.transpose(wgu_all.reshape(N_DEV, DEPTH, GLA_LOWRANK, GU_SHARD), (1, 2, 0, 3)).reshape(
        DEPTH, GLA_LOWRANK, GU_COLS)
    wgu_pad = jnp.pad(wgu_full, ((0, 0), (0, LANE - GLA_LOWRANK), (0, 0))).astype(BF16)

    def kernel_w_in(gathered):
        return _w_in_to_kernel(gathered.reshape(N_DEV, D_MODEL, W_IN_SHARD))

    cos, sin_signed = _rope_tables(s_len)
    g_heads = jnp.concatenate([g_gla, g_dil], axis=1)

    saved = []
    xl = xs
    for l in range(DEPTH):
        shift, scale, gate = ((mod, l, k) for k in range(3))
        w_new = kernel_w_in(w_in_all)
        pf, pb, h = _prenorm_proj(xl, (g_pre, l, 0), scale, shift, w_new)
        o_a, states, w_out_l = _gla_fwd(pf, pb, wgu_pad, b_gate_up, l, comm=("gather", [w_out_b[l]]))
        if l + 1 < DEPTH:
            o_b, lse, w_in_all = _dil_fwd(pf, pb, cos, sin_signed, comm=("gather", [w_in_b[l + 1]]))
        else:
            o_b, lse = _dil_fwd(pf, pb, cos, sin_signed)
        x_next, y, u = _post_fwd(o_a, o_b, pf, (g_heads, l, 0), w_out_l, xl, gate, (g_post, l, 0))
        saved.append((xl, scale, gate, w_new, w_out_l, pf, pb, h, o_a, states, o_b, lse, y, u))
        xl = x_next

    dx, loss_part = _loss_grad(xl, target)

    small_rows = []
    gin_slots, gin_parts, gout_parts = None, [None] * DEPTH, [None] * DEPTH
    for l in reversed(range(DEPTH)):
        x_in, scale, gate, w_new, w_out_l, pf, pb, h, o_a, states, o_b, lse, y, u = saved[l]
        du, do, dz, sums_post = _post_bwd(dx, u, gate, (g_post, l, 0), w_out_l, o_a, o_b, pf, (g_heads, l, 0))
        gout_slots = _matmul_tn(y, du, "grad_w_out", 512)
        dq_a, dk_a, dv_a, dlr2, dwgu, dbgu = _gla_bwd(pf, pb, wgu_pad, b_gate_up, l, states, do)
        travelling = [gout_slots] + ([gin_slots] if gin_slots is not None else [])
        dq_b, dk_b, dv_b, *arrived = _dil_bwd(pf, pb, cos, sin_signed, do, o_b, lse, comm=("exchange", travelling))
        gout_parts[l] = arrived[0].reshape(N_DEV, OUT_SHARD, D_MODEL)
        if gin_slots is not None:
            gin_parts[l + 1] = arrived[1].reshape(N_DEV, D_MODEL, W_IN_SHARD)
        dlr = (dlr2[0] + dlr2[1]).astype(BF16)
        pieces = (dz, dq_a, dk_a, dq_b, dk_b, dv_a, dv_b, dlr)
        gin_slots = _grad_w_in(h, pieces).reshape(N_DEV * D_MODEL, W_IN_SHARD)
        if l == 0:
            dx, sums_in, arrived, _, _ = _in_bwd(pieces, w_new, x_in, dx, (g_pre, l, 0), scale,
                                                 comm=("pairsum_exchange", [gin_slots]))
            gin_parts[0] = arrived.reshape(N_DEV // 2, D_MODEL, W_IN_SHARD)
        else:
            dx, sums_in = _in_bwd(pieces, w_new, x_in, dx, (g_pre, l, 0), scale)
        dmod = jnp.concatenate([sums_in[0], sums_in[1], sums_post[0]])
        vecs = jnp.concatenate([sums_in[2], sums_post[1], sums_post[2], dbgu[0]])
        small_rows[0:0] = [_row(dmod, 4096), _row(vecs, 4096), _row(dwgu[:GLA_LOWRANK], 4096)]
    grad_x = dx[None]

    flat = lambda a, rows: a.reshape(rows, a.shape[-1])
    r_ada = DEPTH * D_MODEL
    g_w_in, d_w_in, nm_w_in, nv_w_in = _adamw_layers(w_in, gin_parts, m_w_in, v_w_in, "adamw_w_in", 256)
    g_w_out, d_w_out, nm_w_out, nv_w_out = _adamw_layers(w_out, gout_parts, m_w_out, v_w_out, "adamw_w_out", 128)

    small_rows += [_row(loss_part[0, 0:1], 4096), jnp.zeros((1, 4096), F32)]
    small = _all_gather(jnp.concatenate(small_rows, axis=0), "gather_small").reshape(N_DEV, 8, 4096)
    dmod_all = jnp.stack([small[:, 0, :3 * D_MODEL], small[:, 3, :3 * D_MODEL]])
    dmod_cols = lax.dynamic_slice_in_dim(dmod_all, me * ADA_SHARD, ADA_SHARD, axis=2)
    gwa = _w_ada_grad(c_all, dmod_cols).reshape(1, r_ada, ADA_SHARD)
    g_w_ada, d_w_ada, nm_w_ada, nv_w_ada = (
        t.reshape(w_ada.shape) for t in _adamw(flat(w_ada, r_ada), gwa, flat(m_w_ada, r_ada), flat(v_w_ada, r_ada),
                                               "adamw_w_ada", 256))

    def small_param(w, m, v, cols, row, name):
        n = w.shape[1]
        parts = jnp.stack([small[:, row, cols:cols + n], small[:, row + 3, cols:cols + n]], axis=1)
        return _adamw(w, parts, m, v, name, DEPTH)

    g_b_ada, d_b_ada, nm_b_ada, nv_b_ada = small_param(b_ada, m_b_ada, v_b_ada, 0, 0, "adamw_b_ada")
    g_g_pre, d_g_pre, nm_g_pre, nv_g_pre = small_param(g_pre, m_g_pre, v_g_pre, 0, 1, "adamw_g_pre")
    g_g_post, d_g_post, nm_g_post, nv_g_post = small_param(g_post, m_g_post, v_g_post, 1024, 1, "adamw_g_post")
    g_g_gla, d_g_gla, nm_g_gla, nv_g_gla = small_param(g_gla, m_g_gla, v_g_gla, 2048, 1, "adamw_g_gla")
    g_g_dil, d_g_dil, nm_g_dil, nv_g_dil = small_param(g_dil, m_g_dil, v_g_dil, 2560, 1, "adamw_g_dil")
    g_b_gu, d_b_gu, nm_b_gu, nv_b_gu = small_param(b_gate_up, m_b_gate_up, v_b_gate_up, 3072, 1, "adamw_b_gate_up")
    gu_parts = jnp.stack([small[:, 2], small[:, 5]], axis=1).reshape(N_DEV, DEPTH, GLA_LOWRANK, GU_COLS)
    gu_parts = lax.dynamic_slice_in_dim(gu_parts, me * GU_SHARD, GU_SHARD, axis=3).reshape(
        N_DEV, DEPTH * GLA_LOWRANK, GU_SHARD)
    r_gu = DEPTH * GLA_LOWRANK
    g_w_gu, d_w_gu, nm_w_gu, nv_w_gu = (
        t.reshape(w_gate_up.shape) for t in _adamw(flat(w_gate_up, r_gu), gu_parts, flat(m_w_gate_up, r_gu),
                                                   flat(v_w_gate_up, r_gu), "adamw_w_gate_up", r_gu))
    loss_parts = jnp.broadcast_to(small[:, 6, 0:1].reshape(N_DEV, 1, 1), (N_DEV, 8, LANE))
    loss = _sum_parts(loss_parts)[0, 0]

    return (loss, grad_x,
            g_w_ada, g_b_ada, g_g_pre, g_w_in, g_w_gu, g_b_gu, g_g_gla, g_g_dil, g_w_out, g_g_post,
            d_w_ada, d_b_ada, d_g_pre, d_w_in, d_w_gu, d_b_gu, d_g_gla, d_g_dil, d_w_out, d_g_post,
            nm_w_ada, nm_b_ada, nm_g_pre, nm_w_in, nm_w_gu, nm_b_gu, nm_g_gla, nm_g_dil, nm_w_out, nm_g_post,
            nv_w_ada, nv_b_ada, nv_g_pre, nv_w_in, nv_w_gu, nv_b_gu, nv_g_gla, nv_g_dil, nv_w_out, nv_g_post)


def _sum_parts(parts):
    n_parts = parts.shape[0]

    def body(p_ref, o_ref):
        acc = p_ref[0]
        for k in range(1, n_parts):
            acc = acc + p_ref[k]
        o_ref[...] = acc

    return pl.pallas_call(body, name="sum_loss", out_shape=jax.ShapeDtypeStruct(parts.shape[1:], F32))(parts)
```

```python
import functools
import math

import jax
import jax.numpy as jnp
from jax import lax
from jax.experimental import pallas as pl
from jax.experimental.pallas import tpu as pltpu

F32 = jnp.float32
BF16 = jnp.bfloat16

N_DEV = 8
D_MODEL = 1024
DEPTH = 2
GLA_HEADS = 4
GLA_DK = 64
GLA_DV = 128
GLA_CHUNK = 64
GLA_TAU = 16.0
GLA_LOWRANK = 16
DIL_HEADS = 4
DIL_HD = 128
DIL_BLOCK = 128
DIL_DILATIONS = (1, 4, 16)
ROPE_THETA = 10000.0
EPS = 1e-6
IN_COLS = 3600
W_IN_SHARD = IN_COLS // N_DEV
ADA_SHARD = 3 * D_MODEL // N_DEV
OUT_SHARD = D_MODEL // N_DEV
GU_COLS = GLA_HEADS * GLA_DK
GU_SHARD = GU_COLS // N_DEV

ADAM_LR = 0.001
ADAM_B1 = 0.9
ADAM_B2 = 0.999
ADAM_EPS = 1e-08
ADAM_WD = 0.01
ADAM_STEP = 10

NP = 3712
COL_Z, COL_QA, COL_KA, COL_QB, COL_KB, COL_VA, COL_VB, COL_LR = 0, 1024, 1280, 1536, 2048, 2560, 3072, 3584
NP_F32 = COL_VA
NP_BF16 = NP - NP_F32
LANE = 128
MASK_VALUE = -1e30

MESH = pl.DeviceIdType.MESH
ANY = pl.BlockSpec(memory_space=pl.ANY)


def _params(sem=None, vmem_mb=None):
    kw = {}
    if sem is not None:
        kw["dimension_semantics"] = sem
    if vmem_mb is not None:
        kw["vmem_limit_bytes"] = vmem_mb * 1024 * 1024
    return pltpu.CompilerParams(**kw)


def _dot(a, b):
    return jnp.dot(a, b, preferred_element_type=F32)


def _dot_nt(a, b):
    return lax.dot_general(a, b, (((1,), (1,)), ((), ())), preferred_element_type=F32)


def _dot_tn(a, b):
    return lax.dot_general(a, b, (((0,), (0,)), ((), ())), preferred_element_type=F32)


def _sigmoid(z):
    return 1.0 / (1.0 + jnp.exp(-z))


def _log_sigmoid(z):
    return jnp.minimum(z, 0.0) - jnp.log(1.0 + jnp.exp(-jnp.abs(z)))


def _rowvec(v, width=D_MODEL):
    arr, row, cb = v
    return arr.reshape(arr.shape[0], 1, arr.shape[1]), pl.BlockSpec((None, 1, width), lambda *_: (row, 0, cb))


def _my_position():
    return lax.axis_index("x"), lax.axis_index("y"), lax.axis_index("c")


def _linear(px, py, pc):
    return 4 * px + 2 * py + pc


def _gather_phase(phase, x_ref, out_ref, send_sem, recv_sem, local_sem):
    m = x_ref.shape[0]
    x, y, c = _my_position()
    me, sibling = (x, y, c), (x, y, 1 - c)
    chips = [(1 - x, y), (x, 1 - y), (1 - x, 1 - y)]

    def rows(px, py, pc):
        return out_ref.at[pl.ds(_linear(px, py, pc) * m, m), :]

    def copy(k, block, to, src=None):
        return pltpu.make_async_remote_copy(
            src_ref=rows(*block) if src is None else src, dst_ref=rows(*block),
            send_sem=send_sem(k), recv_sem=recv_sem(k), device_id=to, device_id_type=MESH)

    mine = pltpu.make_async_copy(x_ref, rows(*me), local_sem)
    first = [copy(0, me, sibling, src=x_ref)] + [copy(1 + j, me, (*chip, c), src=x_ref) for j, chip in enumerate(chips)]
    passed = [copy(4 + j, (*chip, c), sibling) for j, chip in enumerate(chips)]
    if phase == "start":
        mine.start()
        for cp in first:
            cp.start()
    elif phase == "forward":
        for j, chip in enumerate(chips):
            copy(1 + j, (*chip, c), me).wait_recv()
            passed[j].start()
    else:
        copy(0, sibling, me).wait_recv()
        for j, chip in enumerate(chips):
            copy(4 + j, (*chip, 1 - c), me).wait_recv()
        for cp in first + passed:
            cp.wait_send()
        mine.wait()


def _exchange_phase(phase, x_ref, out_ref, send_sem, recv_sem, local_sem):
    m = x_ref.shape[0] // N_DEV
    x, y, c = _my_position()
    me = _linear(x, y, c)

    def rows(ref, idx):
        return ref.at[pl.ds(idx * m, m), :]

    peers = [(1 - x if j & 4 else x, 1 - y if j & 2 else y, 1 - c if j & 1 else c) for j in range(1, N_DEV)]
    local = pltpu.make_async_copy(rows(x_ref, me), rows(out_ref, me), local_sem)
    sends = [pltpu.make_async_remote_copy(
        src_ref=rows(x_ref, _linear(*peer)), dst_ref=rows(out_ref, me),
        send_sem=send_sem(j), recv_sem=recv_sem(j), device_id=peer, device_id_type=MESH) for j, peer in enumerate(peers)]
    if phase == "start":
        local.start()
        for cp in sends:
            cp.start()
    else:
        for j, peer in enumerate(peers):
            pltpu.make_async_remote_copy(
                src_ref=rows(x_ref, _linear(*peer)), dst_ref=rows(out_ref, _linear(*peer)),
                send_sem=send_sem(j), recv_sem=recv_sem(j), device_id=peer, device_id_type=MESH).wait_recv()
        for cp in sends:
            cp.wait_send()
        local.wait()


def _pairsum_exchange_phase(phase, x_ref, out_refs, send_sem, recv_sem, local_sem):
    out_ref, stage_ref, pair_ref = out_refs
    m, n = x_ref.shape[0] // N_DEV, x_ref.shape[1]
    x, y, c = _my_position()
    mine = 2 * x + y
    chips = [(qx, qy) for qx in range(2) for qy in range(2)]
    others = [(1 - x, y), (x, 1 - y), (1 - x, 1 - y)]

    def rows(ref, idx):
        return ref.at[pl.ds(idx * m, m), :]

    def remote(src, dst, k, to):
        return pltpu.make_async_remote_copy(src_ref=src, dst_ref=dst, send_sem=send_sem(k), recv_sem=recv_sem(k),
                                            device_id=to, device_id_type=MESH)

    to_sibling = [remote(rows(x_ref, _linear(qx, qy, 1 - c)), rows(stage_ref, q), q, (x, y, 1 - c))
                  for q, (qx, qy) in enumerate(chips)]
    to_chips = [remote(rows(pair_ref, 2 * qx + qy), rows(out_ref, mine), 4 + j, (qx, qy, c))
                for j, (qx, qy) in enumerate(others)]
    keep = pltpu.make_async_copy(rows(pair_ref, mine), rows(out_ref, mine), local_sem)
    if phase == "start":
        for cp in to_sibling:
            cp.start()
    elif phase == "reduce":
        for cp in to_sibling:
            cp.wait_recv()

        def through_vmem(a_buf, b_buf, sems):
            loads = [pltpu.make_async_copy(rows(x_ref, _linear(qx, qy, c)), a_buf.at[q], sems.at[q])
                     for q, (qx, qy) in enumerate(chips)]
            loads += [pltpu.make_async_copy(rows(stage_ref, q), b_buf.at[q], sems.at[4 + q]) for q in range(4)]
            for cp in loads:
                cp.start()
            for cp in loads:
                cp.wait()
            tr = 128

            def add(t, carry):
                q, r = t // (m // tr), t % (m // tr)
                tile = pl.ds(pl.multiple_of(r * tr, tr), tr)
                a_buf[q, tile, :] = (a_buf[q, tile, :].astype(F32) + b_buf[q, tile, :].astype(F32)).astype(x_ref.dtype)
                return carry

            lax.fori_loop(0, 4 * (m // tr), add, 0)
            stores = [pltpu.make_async_copy(a_buf.at[q], rows(pair_ref, q), sems.at[8 + q]) for q in range(4)]
            for cp in stores:
                cp.start()
            for cp in stores:
                cp.wait()

        pl.run_scoped(through_vmem, pltpu.VMEM((4, m, n), x_ref.dtype), pltpu.VMEM((4, m, n), x_ref.dtype),
                      pltpu.SemaphoreType.DMA((12,)))
    elif phase == "send":
        keep.start()
        for cp in to_chips:
            cp.start()
    else:
        for j, (qx, qy) in enumerate(others):
            remote(rows(pair_ref, mine), rows(out_ref, 2 * qx + qy), 4 + j, (qx, qy, c)).wait_recv()
        for cp in to_sibling + to_chips:
            cp.wait_send()
        keep.wait()


_COMM_PHASES = {"gather": (_gather_phase, ("start", "forward", "finish")),
                "exchange": (_exchange_phase, ("start", "finish")),
                "pairsum_exchange": (_pairsum_exchange_phase, ("start", "reduce", "send", "finish"))}


def _comm_scratch(n_arrays):
    return [pltpu.SemaphoreType.DMA((n_arrays, 7)), pltpu.SemaphoreType.DMA((n_arrays, 7)),
            pltpu.SemaphoreType.DMA((n_arrays,))]


def _comm_run(kind, phases, x_refs, out_refs, send_sems, recv_sems, local_sems):
    fn = _COMM_PHASES[kind][0]
    per = len(out_refs) // len(x_refs)
    for phase in phases:
        for a, x_ref in enumerate(x_refs):
            outs = out_refs[a] if per == 1 else tuple(out_refs[per * a:per * (a + 1)])
            fn(phase, x_ref, outs, lambda k, a=a: send_sems.at[a, k], lambda k, a=a: recv_sems.at[a, k],
               local_sems.at[a])


def _comm_out_shapes(kind, arrays):
    if kind == "pairsum_exchange":
        return [jax.ShapeDtypeStruct((a.shape[0] // 2, a.shape[1]), a.dtype) for a in arrays for _ in range(3)]
    return [jax.ShapeDtypeStruct((N_DEV * a.shape[0], a.shape[1]) if kind == "gather" else a.shape, a.dtype)
            for a in arrays]


def _comm_call(kind, arrays, name):
    n = len(arrays)
    shapes = _comm_out_shapes(kind, arrays)

    def body(*refs):
        _comm_run(kind, _COMM_PHASES[kind][1], refs[:n], refs[n:n + len(shapes)], *refs[n + len(shapes):])

    return pl.pallas_call(body, name=name, out_shape=shapes, in_specs=[ANY] * n, out_specs=[ANY] * len(shapes),
                          scratch_shapes=_comm_scratch(n))(*arrays)


def _all_gather(xs, name):
    return _comm_call("gather", [xs], name)[0]


def _all_to_all(xs, name):
    return _comm_call("exchange", [xs], name)[0]


def _mod_fwd(c_all, w_ada):
    def body(c_ref, w_ref, o_ref):
        cv = c_ref[...]
        sc = cv * _sigmoid(cv)
        o_ref[0] = _dot(sc.astype(BF16), w_ref[0].astype(BF16))

    return pl.pallas_call(
        body, name="mod_fwd", grid=(DEPTH,),
        out_shape=jax.ShapeDtypeStruct((DEPTH, N_DEV, ADA_SHARD), F32),
        in_specs=[pl.BlockSpec((N_DEV, D_MODEL), lambda l: (0, 0)),
                  pl.BlockSpec((1, D_MODEL, ADA_SHARD), lambda l: (l, 0, 0))],
        out_specs=pl.BlockSpec((1, N_DEV, ADA_SHARD), lambda l: (l, 0, 0)),
        compiler_params=_params(("arbitrary",)),
    )(c_all, w_ada)


def _w_ada_grad(c_all, dmod_cols):
    def body(c_ref, d_ref, o_ref):
        cv = c_ref[...]
        sc = cv * _sigmoid(cv)
        o_ref[0] = lax.dot_general(sc, d_ref[0], (((0,), (0,)), ((), ())), precision=lax.Precision.HIGHEST,
                                   preferred_element_type=F32)

    return pl.pallas_call(
        body, name="w_ada_grad", grid=(DEPTH,),
        out_shape=jax.ShapeDtypeStruct((DEPTH, D_MODEL, ADA_SHARD), F32),
        in_specs=[pl.BlockSpec((N_DEV, D_MODEL), lambda l: (0, 0)),
                  pl.BlockSpec((1, N_DEV, ADA_SHARD), lambda l: (l, 0, 0))],
        out_specs=pl.BlockSpec((1, D_MODEL, ADA_SHARD), lambda l: (l, 0, 0)),
        compiler_params=_params(("arbitrary",)),
    )(c_all, dmod_cols)


def _prenorm_proj(x, g_pre, scale, shift, w_new, ts=256):
    s_len = x.shape[0]

    def body(x_ref, g_ref, sc_ref, sh_ref, w_ref, pf_ref, pb_ref, h_ref):
        xv = x_ref[...]
        rstd = lax.rsqrt(jnp.mean(xv * xv, axis=-1, keepdims=True) + EPS)
        h = (xv * rstd * g_ref[...]) * (1.0 + sc_ref[...]) + sh_ref[...]
        hb = h.astype(BF16)
        h_ref[...] = hb
        for j in range(0, NP, 512):
            w = min(512, NP - j)
            acc = _dot(hb, w_ref[:, j:j + w])
            if j < NP_F32:
                pf_ref[:, j:j + w] = acc
            else:
                pb_ref[:, j - NP_F32:j - NP_F32 + w] = acc.astype(BF16)

    (g_pre, g_spec), (scale, sc_spec), (shift, sh_spec) = _rowvec(g_pre), _rowvec(scale), _rowvec(shift)
    return pl.pallas_call(
        body, name="prenorm_proj", grid=(s_len // ts,),
        out_shape=(jax.ShapeDtypeStruct((s_len, NP_F32), F32), jax.ShapeDtypeStruct((s_len, NP_BF16), BF16),
                   jax.ShapeDtypeStruct((s_len, D_MODEL), BF16)),
        in_specs=[pl.BlockSpec((ts, D_MODEL), lambda i: (i, 0)), g_spec, sc_spec, sh_spec,
                  pl.BlockSpec((D_MODEL, NP), lambda i: (0, 0))],
        out_specs=(pl.BlockSpec((ts, NP_F32), lambda i: (i, 0)), pl.BlockSpec((ts, NP_BF16), lambda i: (i, 0)),
                   pl.BlockSpec((ts, D_MODEL), lambda i: (i, 0))),
        compiler_params=_params(("arbitrary",), 48),
    )(x, g_pre, scale, shift, w_new)


GLA_GROUP = 8


def _gla_group_rows(t):
    return [pl.ds(pl.multiple_of((t * GLA_GROUP + j) * GLA_CHUNK, GLA_CHUNK), GLA_CHUNK) for j in range(GLA_GROUP)]


def _gla_chunks_common(q_ref, k_ref, lr_ref, wgu_ref, bgu_ref, rows_list):
    c = GLA_CHUNK
    ri = lax.broadcasted_iota(jnp.int32, (c, c), 0)
    ci = lax.broadcasted_iota(jnp.int32, (c, c), 1)
    tril = (ri >= ci).astype(F32)
    zs = [_dot(lr_ref[rows, :], wgu_ref[...]) + bgu_ref[...] for rows in rows_list]
    las = [_log_sigmoid(z) * (1.0 / GLA_TAU) for z in zs]
    bs = [jnp.dot(tril, la, precision=lax.Precision.HIGHEST, preferred_element_type=F32) for la in las]
    out = []
    for rows, z, b in zip(rows_list, zs, bs):
        q = q_ref[rows, :] * (GLA_DK ** -0.5)
        k = k_ref[rows, :]
        bl = b[c - 1:c, :]
        out.append(dict(z=z, b=b, bl=bl, qe=q * jnp.exp(b), ke=k * jnp.exp(-b), kend=k * jnp.exp(bl - b),
                        dec=jnp.exp(bl)))
    return out, ri, ci


def _head_lane_mask(hh):
    return (lax.broadcasted_iota(jnp.int32, (1, LANE), 1) // GLA_DK) == hh


def _state_block_mask():
    r = lax.broadcasted_iota(jnp.int32, (2 * GLA_DV, LANE), 0) // GLA_DV
    cc = lax.broadcasted_iota(jnp.int32, (2 * GLA_DV, LANE), 1) // GLA_DK
    return r == cc


def _gla_fwd(pf, pb, wgu, bgu, layer, comm=None):
    s_len = pf.shape[0]
    nc = s_len // GLA_CHUNK
    ncomm = len(comm[1]) if comm else 0

    def body(*refs):
        q_ref, k_ref, v_ref, lr_ref, wgu_ref, bgu_ref = refs[:6]
        cin, (o_ref, st_ref), cout = refs[6:6 + ncomm], refs[6 + ncomm:8 + ncomm], refs[8 + ncomm:8 + 2 * ncomm]
        qe_s, cs_s, dec_s = refs[8 + 2 * ncomm:11 + 2 * ncomm]
        comm_before, comm_after = _comm_hooks(comm, cin, cout, refs[11 + 2 * ncomm:], steps=2)
        comm_before()
        bd = _state_block_mask()

        def local(t, carry):
            rows_list = _gla_group_rows(t)
            cm, ri, ci = _gla_chunks_common(q_ref, k_ref, lr_ref, wgu_ref, bgu_ref, rows_list)
            vs = [v_ref[rows, :] for rows in rows_list]
            kebs = [c["ke"].astype(BF16) for c in cm]
            a = [[jnp.where(ri >= ci, _dot_nt(jnp.where(_head_lane_mask(hh), c["qe"], 0.0).astype(BF16), keb), 0.0)
                  .astype(BF16) for hh in range(2)] for c, keb in zip(cm, kebs)]
            oi = [[_dot(ah[hh], v[:, hh * GLA_DV:(hh + 1) * GLA_DV]) for hh in range(2)] for ah, v in zip(a, vs)]
            cs = [jnp.where(bd, _dot_tn(v, c["kend"].astype(BF16)), 0.0) for c, v in zip(cm, vs)]
            for j, (rows, c) in enumerate(zip(rows_list, cm)):
                n = t * GLA_GROUP + j
                o_ref[rows, :] = jnp.concatenate(oi[j], axis=1)
                qe_s[rows, :] = c["qe"].astype(BF16)
                cs_s[n] = cs[j]
                dec_s[n] = jnp.broadcast_to(c["dec"], (8, LANE))
            return carry

        lax.fori_loop(0, nc // GLA_GROUP, local, 0)

        def scan(n, st):
            st_ref[0, n] = st.astype(BF16)
            return dec_s[n][0:1, :] * st + cs_s[n]

        lax.fori_loop(0, nc, scan, jnp.zeros((2 * GLA_DV, LANE), F32))

        def inter(t, carry):
            rows_list = _gla_group_rows(t)
            add = [_dot_nt(qe_s[rows, :], st_ref[0, t * GLA_GROUP + j]) for j, rows in enumerate(rows_list)]
            for rows, av in zip(rows_list, add):
                o_ref[rows, :] = o_ref[rows, :] + av
            return carry

        lax.fori_loop(0, nc // GLA_GROUP, inter, 0)
        comm_after()

    return pl.pallas_call(
        body, name="gla_fwd_comm" if comm else "gla_fwd", grid=(2,),
        out_shape=[jax.ShapeDtypeStruct((s_len, GLA_HEADS * GLA_DV), F32),
                   jax.ShapeDtypeStruct((2, nc, 2 * GLA_DV, LANE), BF16)] + (_comm_out_shapes(*comm) if comm else []),
        in_specs=[pl.BlockSpec((s_len, LANE), lambda g: (0, COL_QA // LANE + g)),
                  pl.BlockSpec((s_len, LANE), lambda g: (0, COL_KA // LANE + g)),
                  pl.BlockSpec((s_len, 2 * GLA_DV), lambda g: (0, (COL_VA - NP_F32) // (2 * GLA_DV) + g)),
                  pl.BlockSpec((s_len, LANE), lambda g: (0, (COL_LR - NP_F32) // LANE)),
                  pl.BlockSpec((None, LANE, LANE), lambda g: (layer, 0, g)),
                  pl.BlockSpec((None, 1, LANE), lambda g: (layer, 0, g))] + [ANY] * ncomm,
        out_specs=[pl.BlockSpec((s_len, 2 * GLA_DV), lambda g: (0, g)),
                   pl.BlockSpec((1, nc, 2 * GLA_DV, LANE), lambda g: (g, 0, 0, 0))] + [ANY] * ncomm,
        scratch_shapes=[pltpu.VMEM((s_len, LANE), BF16), pltpu.VMEM((nc, 2 * GLA_DV, LANE), F32),
                        pltpu.VMEM((nc, 8, LANE), F32)] + (_comm_scratch(ncomm) if comm else []),
        compiler_params=_params(("arbitrary",), 56),
    )(pf, pf, pb, pb, wgu, bgu.reshape(bgu.shape[0], 1, GU_COLS), *(comm[1] if comm else []))


def _rope_tables(s_len):
    inv_freq = ROPE_THETA ** (-jnp.arange(0, DIL_HD, 2, dtype=F32) / DIL_HD)
    ang = jnp.arange(s_len, dtype=F32)[:, None] * inv_freq[None, :]
    cos, sin = jnp.cos(ang), jnp.sin(ang)
    return jnp.concatenate([cos, cos], axis=1), jnp.concatenate([-sin, sin], axis=1)


def _rope(xv, cos, sin_signed):
    return xv * cos + pltpu.roll(xv, DIL_HD // 2, 1) * sin_signed


DIL_GROUP = 8
DIL_GROUP_FWD = 8


def _dil_pair_block(i, half, d, nblk, group=DIL_GROUP):
    nb = nblk // d
    j = i + half * (nblk // group)
    if nb >= 2 * group:
        r, n = j % d, j // d
    else:
        r, n = j // nb, j % nb
    kb = jnp.maximum(n - 1, 0)
    qs = r + d * DIL_BLOCK * n
    ks = r + d * DIL_BLOCK * kb
    return qs, ks, jnp.minimum(n, 1)


def _dil_fill_bias(bias):
    qi = lax.broadcasted_iota(jnp.int32, (DIL_BLOCK, 2 * DIL_BLOCK), 0)
    kj = lax.broadcasted_iota(jnp.int32, (DIL_BLOCK, 2 * DIL_BLOCK), 1)
    for sel in range(2):
        dist = qi - kj + DIL_BLOCK * sel
        bias[sel] = jnp.where((dist >= 0) & (dist <= DIL_BLOCK), 0.0, MASK_VALUE)


def _strided(start, size, d):
    return pl.ds(start, size) if d == 1 else pl.ds(start, size, stride=d)


def _comm_hooks(comm, cin, cout, csem, steps=DIL_HEADS):
    def before():
        if comm:
            @pl.when(pl.program_id(0) == 0)
            def _():
                _comm_run(comm[0], ("start",), cin, cout, *csem)

            if comm[0] == "gather":
                @pl.when(pl.program_id(0) == steps - 1)
                def _():
                    _comm_run(comm[0], ("forward",), cin, cout, *csem)

            if comm[0] == "pairsum_exchange":
                @pl.when(pl.program_id(0) == min(3, steps - 1))
                def _():
                    _comm_run(comm[0], ("reduce", "send"), cin, cout, *csem)

    def after():
        if comm:
            @pl.when(pl.program_id(0) == steps - 1)
            def _():
                _comm_run(comm[0], ("finish",), cin, cout, *csem)

    return before, after


def _dil_fwd(pf, pb, cos, sin_signed, comm=None):
    s_len = pf.shape[0]
    nblk = s_len // DIL_BLOCK
    prep_rows = 256
    scale = DIL_HD ** -0.5
    nc = len(comm[1]) if comm else 0

    def body(*refs):
        q_ref, k_ref, v_ref, cos_ref, sin_ref = refs[:5]
        cin, (o_ref, lse_ref), cout = refs[5:5 + nc], refs[5 + nc:7 + nc], refs[7 + nc:7 + 2 * nc]
        qf, kf, vf, o0, o1, o2, l0, l1, l2, bias = refs[7 + 2 * nc:17 + 2 * nc]
        comm_before, comm_after = _comm_hooks(comm, cin, cout, refs[17 + 2 * nc:])
        comm_before()
        _dil_fill_bias(bias)

        def prep(t, carry):
            rows = pl.ds(pl.multiple_of(t * prep_rows, prep_rows), prep_rows)
            cs, sn = cos_ref[rows, :], sin_ref[rows, :]
            qf[rows, :] = _rope(q_ref[rows, :], cs, sn)
            kf[rows, :] = _rope(k_ref[rows, :], cs, sn)
            vf[rows, :] = v_ref[rows, :].astype(F32)
            return carry

        lax.fori_loop(0, s_len // prep_rows, prep, 0)
        ones = jnp.ones((2 * DIL_BLOCK, DIL_HD), BF16)

        for d, o_p, l_p in zip(DIL_DILATIONS, (o0, o1, o2), (l0, l1, l2)):
            if nblk // d == 2:
                units = DIL_GROUP_FWD // 2

                def whole(i, carry, d=d, o_p=o_p, l_p=l_p, units=units):
                    rows = [_strided(i + u * (d // units), 2 * DIL_BLOCK, d) for u in range(units)]
                    ld = [(qf[rw, :].astype(BF16), kf[rw, :].astype(BF16), vf[rw, :].astype(BF16)) for rw in rows]
                    both = bias[...].reshape(2 * DIL_BLOCK, 2 * DIL_BLOCK)
                    s = [_dot_nt(qb, kk) * scale + both for qb, kk, _ in ld]
                    m = [jnp.max(sv, axis=-1, keepdims=True) for sv in s]
                    p = [jnp.exp(sv - mv) for sv, mv in zip(s, m)]
                    den = [jnp.sum(pv, axis=-1, keepdims=True) for pv in p]
                    r = [_dot(pv.astype(BF16), vv) for pv, (_, _, vv) in zip(p, ld)]
                    for rv, dv, mv, rw in zip(r, den, m, rows):
                        o_p[rw, :] = rv / dv
                        l_p[rw, :] = jnp.broadcast_to(mv + jnp.log(dv), (2 * DIL_BLOCK, DIL_HD))
                    return carry

                lax.fori_loop(0, d // units, whole, 0)
                continue

            def pair(i, carry, d=d, o_p=o_p, l_p=l_p):
                idx = [_dil_pair_block(i, half, d, nblk, DIL_GROUP_FWD) for half in range(DIL_GROUP_FWD)]
                ld = [(qf[_strided(qs, DIL_BLOCK, d), :].astype(BF16),
                       kf[_strided(ks, 2 * DIL_BLOCK, d), :].astype(BF16),
                       vf[_strided(ks, 2 * DIL_BLOCK, d), :].astype(BF16)) for qs, ks, _ in idx]
                s = [_dot_nt(qb, kk) * scale + bias[sel] for (qb, kk, _), (_, _, sel) in zip(ld, idx)]
                m = [jnp.max(sv, axis=-1, keepdims=True) for sv in s]
                p = [jnp.exp(sv - mv) for sv, mv in zip(s, m)]
                den = [jnp.sum(pv, axis=-1, keepdims=True) for pv in p]
                r = [_dot(pv.astype(BF16), vv) for pv, (_, _, vv) in zip(p, ld)]
                for rv, dv, mv, (qs, _, _) in zip(r, den, m, idx):
                    o_p[_strided(qs, DIL_BLOCK, d), :] = rv / dv
                    l_p[_strided(qs, DIL_BLOCK, d), :] = jnp.broadcast_to(mv + jnp.log(dv), (DIL_BLOCK, DIL_HD))
                return carry

            lax.fori_loop(0, nblk // DIL_GROUP_FWD, pair, 0)

        def comb(t, carry):
            rows = pl.ds(pl.multiple_of(t * prep_rows, prep_rows), prep_rows)
            a0, a1, a2 = l0[rows, :], l1[rows, :], l2[rows, :]
            m = jnp.maximum(jnp.maximum(a0, a1), a2)
            e0, e1, e2 = jnp.exp(a0 - m), jnp.exp(a1 - m), jnp.exp(a2 - m)
            tot = e0 + e1 + e2
            o_ref[rows, :] = (e0 * o0[rows, :] + e1 * o1[rows, :] + e2 * o2[rows, :]) / tot
            lse_ref[rows, :] = m + jnp.log(tot)
            return carry

        lax.fori_loop(0, s_len // prep_rows, comb, 0)
        comm_after()

    head = lambda base: pl.BlockSpec((s_len, DIL_HD), lambda h: (0, base // DIL_HD + h))
    table = pl.BlockSpec((s_len, DIL_HD), lambda h: (0, 0))
    out = pl.BlockSpec((s_len, DIL_HD), lambda h: (0, h))
    shp = jax.ShapeDtypeStruct((s_len, DIL_HEADS * DIL_HD), F32)
    return pl.pallas_call(
        body, name="dil_fwd_comm" if comm else "dil_fwd", grid=(DIL_HEADS,),
        out_shape=[shp, shp] + (_comm_out_shapes(*comm) if comm else []),
        in_specs=[head(COL_QB), head(COL_KB), head(COL_VB - NP_F32), table, table] + [ANY] * nc,
        out_specs=[out, out] + [ANY] * nc,
        scratch_shapes=[pltpu.VMEM((s_len, DIL_HD), F32) for _ in range(9)]
        + [pltpu.VMEM((2, DIL_BLOCK, 2 * DIL_BLOCK), F32)] + (_comm_scratch(nc) if comm else []),
        compiler_params=_params(("arbitrary",), 56),
    )(pf, pf, pb, cos, sin_signed, *(comm[1] if comm else []))


def _silu_and_grad(z):
    sg = _sigmoid(z)
    return z * sg, sg * (1.0 + z * (1.0 - sg))


def _post_fwd(o_a, o_b, pf, g_heads, w_out, x, gate, g_post, ts=256):
    s_len = x.shape[0]
    half = GLA_HEADS * GLA_DV

    def body(oa_ref, ob_ref, z_ref, gh_ref, w_ref, x_ref, gate_ref, gp_ref, xo_ref, y_ref, u_ref):
        for src, base in ((oa_ref, 0), (ob_ref, half)):
            for hh in range(4):
                lo = hh * LANE
                og = src[:, lo:lo + LANE]
                on = og * lax.rsqrt(jnp.mean(og * og, axis=-1, keepdims=True) + EPS)
                zg = z_ref[:, base + lo:base + lo + LANE].astype(F32)
                y_ref[:, base + lo:base + lo + LANE] = (on * gh_ref[:, base + lo:base + lo + LANE]
                                                        * (zg * _sigmoid(zg))).astype(BF16)
        u = _dot(y_ref[...], w_ref[...])
        u_ref[...] = u.astype(BF16)
        rstd = lax.rsqrt(jnp.mean(u * u, axis=-1, keepdims=True) + EPS)
        xo_ref[...] = x_ref[...] + gate_ref[...] * (u * rstd * gp_ref[...])

    (g_heads, gh_spec), (gate, gate_spec), (g_post, gp_spec) = _rowvec(g_heads), _rowvec(gate), _rowvec(g_post)
    tile = pl.BlockSpec((ts, D_MODEL), lambda i: (i, 0))
    halft = pl.BlockSpec((ts, half), lambda i: (i, 0))
    return pl.pallas_call(
        body, name="post_fwd", grid=(s_len // ts,),
        out_shape=(jax.ShapeDtypeStruct((s_len, D_MODEL), F32), jax.ShapeDtypeStruct((s_len, D_MODEL), BF16),
                   jax.ShapeDtypeStruct((s_len, D_MODEL), BF16)),
        in_specs=[halft, halft, tile, gh_spec, pl.BlockSpec((D_MODEL, D_MODEL), lambda i: (0, 0)), tile, gate_spec,
                  gp_spec],
        out_specs=(tile, tile, tile),
        compiler_params=_params(("arbitrary",), 40),
    )(o_a, o_b, pf, g_heads, w_out, x, gate, g_post)


def _loss_grad(y, target, ts=512):
    s_len = y.shape[0]

    def body(y_ref, t_ref, dy_ref, loss_ref):
        @pl.when(pl.program_id(0) == 0)
        def _():
            loss_ref[...] = jnp.zeros_like(loss_ref)

        e = y_ref[...] - t_ref[...]
        dy_ref[...] = e * (1.0 / D_MODEL)
        loss_ref[...] += 0.5 * jnp.sum(jnp.mean(e * e, axis=-1, keepdims=True))

    tile = pl.BlockSpec((ts, D_MODEL), lambda i: (i, 0))
    return pl.pallas_call(
        body, name="loss_grad", grid=(s_len // ts,),
        out_shape=(jax.ShapeDtypeStruct((s_len, D_MODEL), F32), jax.ShapeDtypeStruct((8, LANE), F32)),
        in_specs=[tile, tile], out_specs=(tile, pl.BlockSpec((8, LANE), lambda i: (0, 0))),
        compiler_params=_params(("arbitrary",)),
    )(y, target)


def _post_bwd(dxo, u, gate, g_post, w_out, o_a, o_b, pf, g_heads, ts=256):
    s_len = dxo.shape[0]
    half = GLA_HEADS * GLA_DV

    def body(dx_ref, u_ref, gate_ref, gp_ref, w_ref, oa_ref, ob_ref, z_ref, gh_ref, du_ref, do_ref, dz_ref, sums_ref):
        @pl.when(pl.program_id(0) == 0)
        def _():
            sums_ref[...] = jnp.zeros_like(sums_ref)

        dx = dx_ref[...]
        u = u_ref[...].astype(F32)
        rstd = lax.rsqrt(jnp.mean(u * u, axis=-1, keepdims=True) + EPS)
        un = u * rstd
        sums_ref[0:1, :] += jnp.sum(dx * (un * gp_ref[...]), axis=0, keepdims=True)
        drn = dx * gate_ref[...]
        sums_ref[1:2, :] += jnp.sum(drn * un, axis=0, keepdims=True)
        dun = drn * gp_ref[...]
        du = rstd * (dun - un * jnp.mean(dun * un, axis=-1, keepdims=True))
        dub = du.astype(BF16)
        du_ref[...] = dub
        dy = _dot_nt(dub, w_ref[...])
        for src, base in ((oa_ref, 0), (ob_ref, half)):
            for hh in range(4):
                lo = base + hh * LANE
                og = src[:, hh * LANE:(hh + 1) * LANE]
                rs = lax.rsqrt(jnp.mean(og * og, axis=-1, keepdims=True) + EPS)
                on = og * rs
                zg = z_ref[:, lo:lo + LANE].astype(F32)
                sz, dsz = _silu_and_grad(zg)
                gg = gh_ref[:, lo:lo + LANE]
                dyg = dy[:, lo:lo + LANE]
                sums_ref[2:3, lo:lo + LANE] += jnp.sum(dyg * sz * on, axis=0, keepdims=True)
                dz_ref[:, lo:lo + LANE] = (dyg * on * gg * dsz).astype(BF16)
                don = dyg * gg * sz
                do_ref[:, lo:lo + LANE] = (rs * (don - on * jnp.mean(don * on, axis=-1, keepdims=True))).astype(BF16)

    (g_heads, gh_spec), (gate, gate_spec), (g_post, gp_spec) = _rowvec(g_heads), _rowvec(gate), _rowvec(g_post)
    tile = pl.BlockSpec((ts, D_MODEL), lambda i: (i, 0))
    halft = pl.BlockSpec((ts, half), lambda i: (i, 0))
    return pl.pallas_call(
        body, name="post_bwd", grid=(s_len // ts,),
        out_shape=(jax.ShapeDtypeStruct((s_len, D_MODEL), BF16), jax.ShapeDtypeStruct((s_len, D_MODEL), BF16),
                   jax.ShapeDtypeStruct((s_len, D_MODEL), BF16), jax.ShapeDtypeStruct((8, D_MODEL), F32)),
        in_specs=[tile, tile, gate_spec, gp_spec, pl.BlockSpec((D_MODEL, D_MODEL), lambda i: (0, 0)), halft, halft,
                  tile, gh_spec],
        out_specs=(tile, tile, tile, pl.BlockSpec((8, D_MODEL), lambda i: (0, 0))),
        compiler_params=_params(("arbitrary",), 40),
    )(dxo, u, gate, g_post, w_out, o_a, o_b, pf, g_heads)


def _gla_bwd(pf, pb, wgu, bgu, layer, states, do):
    s_len = pf.shape[0]
    nc = s_len // GLA_CHUNK
    c = GLA_CHUNK

    def body(q_ref, k_ref, v_ref, lr_ref, wgu_ref, bgu_ref, st_ref, do_ref,
             dq_ref, dk_ref, dv_ref, dlr_ref, dwgu_ref, dbgu_ref, ds_s, dec_s, dw_acc, db_acc):
        dw_acc[...] = jnp.zeros_like(dw_acc)
        db_acc[...] = jnp.zeros_like(db_acc)
        bd = _state_block_mask()
        last_row = lax.broadcasted_iota(jnp.int32, (c, LANE), 0) == c - 1

        def local(t, carry):
            rows_list = _gla_group_rows(t)
            cm, _, _ = _gla_chunks_common(q_ref, k_ref, lr_ref, wgu_ref, bgu_ref, rows_list)
            loc = [jnp.where(bd, _dot_tn(do_ref[rows, :], cc["qe"].astype(BF16)), 0.0)
                   for rows, cc in zip(rows_list, cm)]
            for j, cc in enumerate(cm):
                ds_s[t * GLA_GROUP + j] = loc[j]
                dec_s[t * GLA_GROUP + j] = jnp.broadcast_to(cc["dec"], (8, LANE))
            return carry

        lax.fori_loop(0, nc // GLA_GROUP, local, 0)

        def scan(t, dst):
            n = nc - 1 - t
            loc = ds_s[n]
            ds_s[n] = dst
            return dec_s[n][0:1, :] * dst + loc

        lax.fori_loop(0, nc, scan, jnp.zeros((2 * GLA_DV, LANE), F32))

        def rest(t, carry):
            rows_list = _gla_group_rows(t)
            cm, ri, ci = _gla_chunks_common(q_ref, k_ref, lr_ref, wgu_ref, bgu_ref, rows_list)
            ns = [t * GLA_GROUP + j for j in range(GLA_GROUP)]
            vs = [v_ref[rows, :] for rows in rows_list]
            dobs = [do_ref[rows, :] for rows in rows_list]
            stbs = [st_ref[0, n] for n in ns]
            dsts = [ds_s[n] for n in ns]
            dstbs = [d.astype(BF16) for d in dsts]
            qebs = [cc["qe"].astype(BF16) for cc in cm]
            kebs = [cc["ke"].astype(BF16) for cc in cm]
            kendbs = [cc["kend"].astype(BF16) for cc in cm]
            hms = [_head_lane_mask(hh) for hh in range(2)]
            qehs = [[jnp.where(hm, cc["qe"], 0.0).astype(BF16) for hm in hms] for cc in cm]
            kehs = [[jnp.where(hm, cc["ke"], 0.0).astype(BF16) for hm in hms] for cc in cm]
            heads = lambda x: [x[:, hh * GLA_DV:(hh + 1) * GLA_DV] for hh in range(2)]
            vhs, dohs = [heads(v) for v in vs], [heads(d) for d in dobs]

            dqe0 = [_dot(dob, stb) for dob, stb in zip(dobs, stbs)]
            dkend = [_dot(v, dstb) for v, dstb in zip(vs, dstbs)]
            dv0 = [_dot_nt(kb, dstb) for kb, dstb in zip(kendbs, dstbs)]
            a_t = [[jnp.where(ci >= ri, _dot_nt(kehs[j][hh], qebs[j]), 0.0).astype(BF16) for hh in range(2)]
                   for j in range(GLA_GROUP)]
            da = [[jnp.where(ri >= ci, _dot_nt(dohs[j][hh], vhs[j][hh]), 0.0).astype(BF16) for hh in range(2)]
                  for j in range(GLA_GROUP)]
            da_t = [[jnp.where(ci >= ri, _dot_nt(vhs[j][hh], dohs[j][hh]), 0.0).astype(BF16) for hh in range(2)]
                    for j in range(GLA_GROUP)]
            dv1 = [[_dot(a_t[j][hh], dohs[j][hh]) for hh in range(2)] for j in range(GLA_GROUP)]
            dqe1 = [[_dot(da[j][hh], kebs[j]) for hh in range(2)] for j in range(GLA_GROUP)]
            dke1 = [[_dot(da_t[j][hh], qehs[j][hh]) for hh in range(2)] for j in range(GLA_GROUP)]

            dbs, dzs = [], []
            for j, (rows, cc) in enumerate(zip(rows_list, cm)):
                qe, ke, kend, b, bl = cc["qe"], cc["ke"], cc["kend"], cc["b"], cc["bl"]
                dqe = dqe0[j] + jnp.where(hms[0], dqe1[j][0], 0.0) + jnp.where(hms[1], dqe1[j][1], 0.0)
                dke = jnp.where(hms[0], dke1[j][0], 0.0) + jnp.where(hms[1], dke1[j][1], 0.0)
                dv_ref[rows, :] = (dv0[j] + jnp.concatenate(dv1[j], axis=1)).astype(BF16)
                dq_ref[rows, :] = (dqe * jnp.exp(b) * (GLA_DK ** -0.5)).astype(BF16)
                dk_ref[rows, :] = (dke * jnp.exp(-b) + dkend[j] * jnp.exp(bl - b)).astype(BF16)
                ddec = jnp.sum(dsts[j] * stbs[j].astype(F32), axis=0, keepdims=True)
                dbl = jnp.sum(dkend[j] * kend, axis=0, keepdims=True) + ddec * cc["dec"]
                dbs.append(dqe * qe - dke * ke - dkend[j] * kend + jnp.where(last_row, dbl, 0.0))
            triu = (ci >= ri).astype(F32)
            dlas = [jnp.dot(triu, db, precision=lax.Precision.HIGHEST, preferred_element_type=F32) for db in dbs]
            dzs = [dla * (1.0 / GLA_TAU) * _sigmoid(-cc["z"]) for dla, cc in zip(dlas, cm)]
            dzbs = [dz.astype(BF16) for dz in dzs]
            dlrs = [_dot_nt(dzb, wgu_ref[...]) for dzb in dzbs]
            dws = [_dot_tn(lr_ref[rows, :], dzb) for rows, dzb in zip(rows_list, dzbs)]
            for rows, dlr in zip(rows_list, dlrs):
                dlr_ref[0, rows, :] = dlr
            dw_acc[...] += functools.reduce(lambda x, y: x + y, dws)
            db_acc[0:1, :] += jnp.sum(functools.reduce(lambda x, y: x + y, dzs), axis=0, keepdims=True)
            return carry

        lax.fori_loop(0, nc // GLA_GROUP, rest, 0)
        dwgu_ref[...] = dw_acc[...]
        dbgu_ref[...] = db_acc[...]

    pair = pl.BlockSpec((s_len, LANE), lambda g: (0, g))
    return pl.pallas_call(
        body, name="gla_bwd", grid=(2,),
        out_shape=(jax.ShapeDtypeStruct((s_len, GU_COLS), BF16), jax.ShapeDtypeStruct((s_len, GU_COLS), BF16),
                   jax.ShapeDtypeStruct((s_len, GLA_HEADS * GLA_DV), BF16),
                   jax.ShapeDtypeStruct((2, s_len, LANE), F32),
                   jax.ShapeDtypeStruct((LANE, GU_COLS), F32), jax.ShapeDtypeStruct((8, GU_COLS), F32)),
        in_specs=[pl.BlockSpec((s_len, LANE), lambda g: (0, COL_QA // LANE + g)),
                  pl.BlockSpec((s_len, LANE), lambda g: (0, COL_KA // LANE + g)),
                  pl.BlockSpec((s_len, 2 * GLA_DV), lambda g: (0, (COL_VA - NP_F32) // (2 * GLA_DV) + g)),
                  pl.BlockSpec((s_len, LANE), lambda g: (0, (COL_LR - NP_F32) // LANE)),
                  pl.BlockSpec((None, LANE, LANE), lambda g: (layer, 0, g)),
                  pl.BlockSpec((None, 1, LANE), lambda g: (layer, 0, g)),
                  pl.BlockSpec((1, nc, 2 * GLA_DV, LANE), lambda g: (g, 0, 0, 0)),
                  pl.BlockSpec((s_len, 2 * GLA_DV), lambda g: (0, g))],
        out_specs=(pair, pair, pl.BlockSpec((s_len, 2 * GLA_DV), lambda g: (0, g)),
                   pl.BlockSpec((1, s_len, LANE), lambda g: (g, 0, 0)),
                   pl.BlockSpec((LANE, LANE), lambda g: (0, g)), pl.BlockSpec((8, LANE), lambda g: (0, g))),
        scratch_shapes=[pltpu.VMEM((nc, 2 * GLA_DV, LANE), F32), pltpu.VMEM((nc, 8, LANE), F32),
                        pltpu.VMEM((LANE, LANE), F32), pltpu.VMEM((8, LANE), F32)],
        compiler_params=_params(("arbitrary",), 56),
    )(pf, pf, pb, pb, wgu, bgu.reshape(bgu.shape[0], 1, GU_COLS), states, do)


def _dil_bwd(pf, pb, cos, sin_signed, do, o_b, lse, comm=None):
    s_len = pf.shape[0]
    nblk = s_len // DIL_BLOCK
    prep_rows = 256
    scale = DIL_HD ** -0.5
    nc = len(comm[1]) if comm else 0

    def body(*refs):
        q_ref, k_ref, v_ref, cos_ref, sin_ref, do_ref, o_ref, lse_ref = refs[:8]
        cin, (dq_ref, dk_ref, dv_ref), cout = refs[8:8 + nc], refs[8 + nc:11 + nc], refs[11 + nc:11 + 2 * nc]
        qf, kf, vf, dof, dl, dqa, dka, dva, bias = refs[11 + 2 * nc:20 + 2 * nc]
        comm_before, comm_after = _comm_hooks(comm, cin, cout, refs[20 + 2 * nc:])
        comm_before()
        _dil_fill_bias(bias)

        def prep(t, carry):
            rows = pl.ds(pl.multiple_of(t * prep_rows, prep_rows), prep_rows)
            cs, sn = cos_ref[rows, :], sin_ref[rows, :]
            qf[rows, :] = _rope(q_ref[rows, :], cs, sn) * scale
            kf[rows, :] = _rope(k_ref[rows, :], cs, sn)
            vf[rows, :] = v_ref[rows, :].astype(F32)
            dov = do_ref[rows, :].astype(F32)
            dof[rows, :] = dov
            dl[rows, :] = jnp.broadcast_to(jnp.sum(dov * o_ref[rows, :], axis=-1, keepdims=True), (prep_rows, DIL_HD))
            zero = jnp.zeros((prep_rows, DIL_HD), F32)
            dqa[rows, :] = zero
            dka[rows, :] = zero
            dva[rows, :] = zero
            return carry

        lax.fori_loop(0, s_len // prep_rows, prep, 0)

        for d in DIL_DILATIONS:
            if nblk // d == 2:
                units = DIL_GROUP // 2

                def whole(i, carry, d=d, units=units):
                    rows = [_strided(i + u * (d // units), 2 * DIL_BLOCK, d) for u in range(units)]
                    ld = [(qf[rw, :].astype(BF16), kf[rw, :].astype(BF16), vf[rw, :].astype(BF16),
                           dof[rw, :].astype(BF16)) for rw in rows]
                    both = bias[...].reshape(2 * DIL_BLOCK, 2 * DIL_BLOCK)
                    s = [_dot_nt(qb, kk) + both for qb, kk, _, _ in ld]
                    dp = [_dot_nt(dob, vv) for _, _, vv, dob in ld]
                    p = [jnp.exp(sv - lse_ref[rw, :][:, 0:1]) for sv, rw in zip(s, rows)]
                    ds = [(pv * (dpv - dl[rw, :][:, 0:1])).astype(BF16) for pv, dpv, rw in zip(p, dp, rows)]
                    pb = [pv.astype(BF16) for pv in p]
                    gq = [_dot(dsv, kk) for dsv, (_, kk, _, _) in zip(ds, ld)]
                    gk = [_dot_tn(dsv, qb) for dsv, (qb, _, _, _) in zip(ds, ld)]
                    gv = [_dot_tn(pv, dob) for pv, (_, _, _, dob) in zip(pb, ld)]
                    for rw, a, b, c in zip(rows, gq, gk, gv):
                        dqa[rw, :] += a
                        dka[rw, :] += b
                        dva[rw, :] += c
                    return carry

                lax.fori_loop(0, d // units, whole, 0)
                continue

            def pair(i, carry, d=d):
                idx = [_dil_pair_block(i, half, d, nblk) for half in range(DIL_GROUP)]
                rows = [(_strided(qs, DIL_BLOCK, d), _strided(ks, 2 * DIL_BLOCK, d)) for qs, ks, _ in idx]
                ld = [(qf[qr, :].astype(BF16), kf[kr, :].astype(BF16), vf[kr, :].astype(BF16),
                       dof[qr, :].astype(BF16)) for qr, kr in rows]
                s = [_dot_nt(qb, kk) + bias[sel] for (qb, kk, _, _), (_, _, sel) in zip(ld, idx)]
                dp = [_dot_nt(dob, vv) for _, _, vv, dob in ld]
                p = [jnp.exp(sv - lse_ref[qr, :][:, 0:1]) for sv, (qr, _) in zip(s, rows)]
                ds = [(pv * (dpv - dl[qr, :][:, 0:1])).astype(BF16) for pv, dpv, (qr, _) in zip(p, dp, rows)]
                pb = [pv.astype(BF16) for pv in p]
                gq = [_dot(dsv, kk) for dsv, (_, kk, _, _) in zip(ds, ld)]
                gk = [_dot_tn(dsv, qb) for dsv, (qb, _, _, _) in zip(ds, ld)]
                gv = [_dot_tn(pv, dob) for pv, (_, _, _, dob) in zip(pb, ld)]
                for (qr, kr), a, b, c in zip(rows, gq, gk, gv):
                    dqa[qr, :] += a
                    dka[kr, :] += b
                    dva[kr, :] += c
                return carry

            lax.fori_loop(0, nblk // DIL_GROUP, pair, 0)

        def fin(t, carry):
            rows = pl.ds(pl.multiple_of(t * prep_rows, prep_rows), prep_rows)
            cs, sn = cos_ref[rows, :], sin_ref[rows, :]
            gq, gk = dqa[rows, :] * scale, dka[rows, :]
            dq_ref[rows, :] = (gq * cs - pltpu.roll(gq, DIL_HD // 2, 1) * sn).astype(BF16)
            dk_ref[rows, :] = (gk * cs - pltpu.roll(gk, DIL_HD // 2, 1) * sn).astype(BF16)
            dv_ref[rows, :] = dva[rows, :].astype(BF16)
            return carry

        lax.fori_loop(0, s_len // prep_rows, fin, 0)
        comm_after()

    head = lambda base: pl.BlockSpec((s_len, DIL_HD), lambda h: (0, base // DIL_HD + h))
    table = pl.BlockSpec((s_len, DIL_HD), lambda h: (0, 0))
    out = pl.BlockSpec((s_len, DIL_HD), lambda h: (0, h))
    shp = jax.ShapeDtypeStruct((s_len, DIL_HEADS * DIL_HD), BF16)
    return pl.pallas_call(
        body, name="dil_bwd_comm" if comm else "dil_bwd", grid=(DIL_HEADS,),
        out_shape=[shp, shp, shp] + (_comm_out_shapes(*comm) if comm else []),
        in_specs=[head(COL_QB), head(COL_KB), head(COL_VB - NP_F32), table, table,
                  pl.BlockSpec((s_len, DIL_HD), lambda h: (0, DIL_HEADS + h)), out, out] + [ANY] * nc,
        out_specs=[out, out, out] + [ANY] * nc,
        scratch_shapes=[pltpu.VMEM((s_len, DIL_HD), F32) for _ in range(8)]
        + [pltpu.VMEM((2, DIL_BLOCK, 2 * DIL_BLOCK), F32)] + (_comm_scratch(nc) if comm else []),
        compiler_params=_params(("arbitrary",), 56),
    )(pf, pf, pb, cos, sin_signed, do, o_b, lse, *(comm[1] if comm else []))


_PIECES = ((COL_Z, 1024), (COL_QA, 256), (COL_KA, 256), (COL_QB, 512), (COL_KB, 512), (COL_VA, 512), (COL_VB, 512),
           (COL_LR, 128))


def _in_bwd(pieces, w_new, x, dxo, g_pre, scale, comm=None, ts=256):
    s_len = x.shape[0]
    nc = len(comm[1]) if comm else 0
    nco = len(_comm_out_shapes(*comm)) if comm else 0
    npc = len(_PIECES)

    def body(*refs):
        p_refs = refs[:npc]
        w_ref, x_ref, dxo_ref, g_ref, sc_ref = refs[npc:npc + 5]
        cin, (dx_ref, sums_ref), cout = (refs[npc + 5:npc + 5 + nc], refs[npc + 5 + nc:npc + 7 + nc],
                                         refs[npc + 7 + nc:npc + 7 + nc + nco])
        comm_before, comm_after = _comm_hooks(comm, cin, cout, refs[npc + 7 + nc + nco:], steps=s_len // ts)
        comm_before()

        @pl.when(pl.program_id(0) == 0)
        def _():
            sums_ref[...] = jnp.zeros_like(sums_ref)

        dh = jnp.zeros((ts, D_MODEL), F32)
        for p_ref, (col, width) in zip(p_refs, _PIECES):
            dh += _dot_nt(p_ref[...], w_ref[:, col:col + width])
        xv = x_ref[...]
        rstd = lax.rsqrt(jnp.mean(xv * xv, axis=-1, keepdims=True) + EPS)
        xn = xv * rstd
        sums_ref[0:1, :] += jnp.sum(dh, axis=0, keepdims=True)
        sums_ref[1:2, :] += jnp.sum(dh * (xn * g_ref[...]), axis=0, keepdims=True)
        dr = dh * (1.0 + sc_ref[...])
        sums_ref[2:3, :] += jnp.sum(dr * xn, axis=0, keepdims=True)
        dxn = dr * g_ref[...]
        dx_ref[...] = dxo_ref[...] + rstd * (dxn - xn * jnp.mean(dxn * xn, axis=-1, keepdims=True))
        comm_after()

    (g_pre, g_spec), (scale, sc_spec) = _rowvec(g_pre), _rowvec(scale)
    tile = pl.BlockSpec((ts, D_MODEL), lambda i: (i, 0))
    return pl.pallas_call(
        body, name="in_bwd_comm" if comm else "in_bwd", grid=(s_len // ts,),
        out_shape=[jax.ShapeDtypeStruct((s_len, D_MODEL), F32), jax.ShapeDtypeStruct((8, D_MODEL), F32)]
        + (_comm_out_shapes(*comm) if comm else []),
        in_specs=[pl.BlockSpec((ts, width), lambda i: (i, 0)) for _, width in _PIECES]
        + [pl.BlockSpec((D_MODEL, NP), lambda i: (0, 0)), tile, tile, g_spec, sc_spec] + [ANY] * nc,
        out_specs=[tile, pl.BlockSpec((8, D_MODEL), lambda i: (0, 0))] + [ANY] * nco,
        scratch_shapes=_comm_scratch(nc) if comm else [],
        compiler_params=_params(("arbitrary",), 56),
    )(*pieces, w_new, x, dxo, g_pre, scale, *(comm[1] if comm else []))


def _w_in_to_kernel(gathered, tr=128):
    def body(g_ref, o_ref):
        cols = jnp.concatenate([g_ref[k].astype(F32) for k in range(N_DEV)], axis=1)
        pad = jnp.zeros((tr, LANE - GLA_LOWRANK), F32)
        o_ref[...] = jnp.concatenate(
            [cols[:, 1024:1536], cols[:, 3088:3600], cols[:, 0:512], cols[:, 1552:2576], cols[:, 512:1024],
             cols[:, 2576:3088], cols[:, 1536:1552], pad], axis=1).astype(BF16)

    return pl.pallas_call(
        body, name="w_in_to_kernel", grid=(D_MODEL // tr,), out_shape=jax.ShapeDtypeStruct((D_MODEL, NP), BF16),
        in_specs=[pl.BlockSpec((N_DEV, tr, W_IN_SHARD), lambda i: (0, i, 0))],
        out_specs=pl.BlockSpec((tr, NP), lambda i: (i, 0)),
        compiler_params=_params(("arbitrary",)),
    )(gathered)


def _grad_w_in(h, pieces, ts=512, tr=128):
    s_len = h.shape[0]
    steps = s_len // ts

    def body(*refs):
        h_ref, p_refs = refs[0], refs[1:1 + len(_PIECES)]
        o_ref, acc = refs[1 + len(_PIECES):]

        @pl.when(pl.program_id(0) == 0)
        def _():
            acc[...] = jnp.zeros_like(acc)

        hv = h_ref[...]
        for p_ref, (col, width) in zip(p_refs, _PIECES):
            acc[:, col:col + width] += _dot_tn(hv, p_ref[...])

        @pl.when(pl.program_id(0) == steps - 1)
        def _():
            def rows_out(t, carry):
                rows = pl.ds(pl.multiple_of(t * tr, tr), tr)
                g = acc[rows, :]
                cols = jnp.concatenate(
                    [g[:, COL_QA:COL_QB], g[:, COL_VA:COL_VB], g[:, 0:512], g[:, COL_LR:COL_LR + GLA_LOWRANK],
                     g[:, COL_QB:COL_VA], g[:, COL_VB:COL_LR], g[:, 512:1024]], axis=1)
                for k in range(N_DEV):
                    o_ref[k, rows, :] = cols[:, W_IN_SHARD * k:W_IN_SHARD * (k + 1)].astype(BF16)
                return carry

            lax.fori_loop(0, D_MODEL // tr, rows_out, 0)

    return pl.pallas_call(
        body, name="grad_w_in", grid=(steps,),
        out_shape=jax.ShapeDtypeStruct((N_DEV, D_MODEL, W_IN_SHARD), BF16),
        in_specs=[pl.BlockSpec((ts, D_MODEL), lambda i: (i, 0))]
        + [pl.BlockSpec((ts, width), lambda i: (i, 0)) for _, width in _PIECES],
        out_specs=pl.BlockSpec((N_DEV, D_MODEL, W_IN_SHARD), lambda i: (0, 0, 0)),
        scratch_shapes=[pltpu.VMEM((D_MODEL, NP), F32)],
        compiler_params=_params(("arbitrary",), 56),
    )(h, *pieces)


def _matmul_tn(a, b, name, bn, ts=512):
    s_len, m = a.shape
    n = b.shape[1]
    steps = s_len // ts

    def body(a_ref, b_ref, o_ref, acc):
        @pl.when(pl.program_id(1) == 0)
        def _():
            acc[...] = jnp.zeros_like(acc)

        acc[...] += _dot_tn(a_ref[...], b_ref[...])

        @pl.when(pl.program_id(1) == steps - 1)
        def _():
            o_ref[...] = acc[...].astype(BF16)

    return pl.pallas_call(
        body, name=name, grid=(n // bn, steps),
        out_shape=jax.ShapeDtypeStruct((m, n), BF16),
        in_specs=[pl.BlockSpec((ts, m), lambda j, i: (i, 0)), pl.BlockSpec((ts, bn), lambda j, i: (i, j))],
        out_specs=pl.BlockSpec((m, bn), lambda j, i: (0, j)),
        scratch_shapes=[pltpu.VMEM((m, bn), F32)],
        compiler_params=_params(("arbitrary", "arbitrary"), 40),
    )(a, b)


def _adam_math(w, g, m, v):
    m = ADAM_B1 * m + (1.0 - ADAM_B1) * g
    v = ADAM_B2 * v + (1.0 - ADAM_B2) * (g * g)
    m_hat = m / (1.0 - ADAM_B1 ** ADAM_STEP)
    v_hat = v / (1.0 - ADAM_B2 ** ADAM_STEP)
    delta = -ADAM_LR * (m_hat / (jnp.sqrt(v_hat) + ADAM_EPS) + ADAM_WD * w)
    return delta, m, v


def _adamw(w, parts, m, v, name, tr):
    r, cdim = w.shape
    n_parts = parts.shape[0]

    def body(w_ref, p_ref, m_ref, v_ref, g_ref, d_ref, nm_ref, nv_ref):
        g = p_ref[0].astype(F32)
        for k in range(1, n_parts):
            g = g + p_ref[k].astype(F32)
        g_ref[...] = g
        d_ref[...], nm_ref[...], nv_ref[...] = _adam_math(w_ref[...], g, m_ref[...], v_ref[...])

    tile = pl.BlockSpec((tr, cdim), lambda i: (i, 0))
    shp = jax.ShapeDtypeStruct((r, cdim), F32)
    return pl.pallas_call(
        body, name=name, grid=(r // tr,), out_shape=(shp, shp, shp, shp),
        in_specs=[tile, pl.BlockSpec((n_parts, tr, cdim), lambda i: (0, i, 0)), tile, tile],
        out_specs=(tile, tile, tile, tile),
        compiler_params=_params(("arbitrary",), 40),
    )(w, parts, m, v)


def _adamw_layers(w, parts, m, v, name, tr):
    n_layers, r, cdim = w.shape

    def body(*refs):
        w_ref, p_refs, (m_ref, v_ref) = refs[0], refs[1:1 + n_layers], refs[1 + n_layers:3 + n_layers]
        g_ref, d_ref, nm_ref, nv_ref = refs[3 + n_layers:]
        for l, p_ref in enumerate(p_refs):
            @pl.when(pl.program_id(0) == l)
            def _(p_ref=p_ref):
                g = p_ref[0].astype(F32)
                for k in range(1, p_ref.shape[0]):
                    g = g + p_ref[k].astype(F32)
                g_ref[0] = g
                d_ref[0], nm_ref[0], nv_ref[0] = _adam_math(w_ref[0], g, m_ref[0], v_ref[0])

    tile = pl.BlockSpec((1, tr, cdim), lambda l, i: (l, i, 0))
    part = lambda own: pl.BlockSpec((parts[own].shape[0], tr, cdim), lambda l, i: (0, jnp.where(l == own, i, 0), 0))
    shp = jax.ShapeDtypeStruct(w.shape, F32)
    return pl.pallas_call(
        body, name=name, grid=(n_layers, r // tr), out_shape=(shp, shp, shp, shp),
        in_specs=[tile] + [part(l) for l in range(n_layers)] + [tile, tile],
        out_specs=(tile, tile, tile, tile),
        compiler_params=_params(("arbitrary", "arbitrary"), 40),
    )(w, *parts, m, v)


def _row(vec, width):
    vec = vec.reshape(1, -1)
    return jnp.pad(vec, ((0, 0), (0, width - vec.shape[1])))


def kernel(x, c, w_ada, b_ada, g_pre, w_in, w_gate_up, b_gate_up, g_gla, g_dil, w_out, g_post, loss_target, m_w_ada, m_b_ada, m_g_pre, m_w_in, m_w_gate_up, m_b_gate_up, m_g_gla, m_g_dil, m_w_out, m_g_post, v_w_ada, v_b_ada, v_g_pre, v_w_in, v_w_gate_up, v_b_gate_up, v_g_gla, v_g_dil, v_w_out, v_g_post):
    px, py, pc = _my_position()
    me = _linear(px, py, pc)
    xs = x[0]
    target = loss_target[0]
    s_len = xs.shape[0]
    assert s_len % (DIL_BLOCK * max(DIL_DILATIONS) * 2) == 0 and xs.shape[1] == D_MODEL

    c_all = _all_gather(jnp.pad(c, ((0, 7), (0, 0))), "gather_c").reshape(N_DEV, 8, D_MODEL)[:, 0]
    mod_part = _mod_fwd(c_all, w_ada)
    w_in_b, w_out_b = w_in.astype(BF16), w_out.astype(BF16)
    mod_all, wgu_all, w_in_all = _comm_call(
        "gather", [mod_part.reshape(DEPTH * N_DEV, ADA_SHARD), w_gate_up.reshape(DEPTH * GLA_LOWRANK, GU_SHARD),
                   w_in_b[0]], "gather_first")
    mod_all = mod_all.reshape(N_DEV, DEPTH, N_DEV, ADA_SHARD)
    mod_mine = lax.dynamic_index_in_dim(mod_all, me, axis=2, keepdims=False)
    mod = jnp.transpose(mod_mine, (1, 0, 2)).reshape(DEPTH, 3 * D_MODEL) + b_ada
    wgu_full = jnp.transpose(wgu_all.reshape(N_DEV, DEPTH, GLA_LOWRANK, GU_SHARD), (1, 2, 0, 3)).reshape(
        DEPTH, GLA_LOWRANK, GU_COLS)
    wgu_pad = jnp.pad(wgu_full, ((0, 0), (0, LANE - GLA_LOWRANK), (0, 0))).astype(BF16)

    def kernel_w_in(gathered):
        return _w_in_to_kernel(gathered.reshape(N_DEV, D_MODEL, W_IN_SHARD))

    cos, sin_signed = _rope_tables(s_len)
    g_heads = jnp.concatenate([g_gla, g_dil], axis=1)

    saved = []
    xl = xs
    for l in range(DEPTH):
        shift, scale, gate = ((mod, l, k) for k in range(3))
        w_new = kernel_w_in(w_in_all)
        pf, pb, h = _prenorm_proj(xl, (g_pre, l, 0), scale, shift, w_new)
        o_a, states, w_out_l = _gla_fwd(pf, pb, wgu_pad, b_gate_up, l, comm=("gather", [w_out_b[l]]))
        if l + 1 < DEPTH:
            o_b, lse, w_in_all = _dil_fwd(pf, pb, cos, sin_signed, comm=("gather", [w_in_b[l + 1]]))
        else:
            o_b, lse = _dil_fwd(pf, pb, cos, sin_signed)
        x_next, y, u = _post_fwd(o_a, o_b, pf, (g_heads, l, 0), w_out_l, xl, gate, (g_post, l, 0))
        saved.append((xl, scale, gate, w_new, w_out_l, pf, pb, h, o_a, states, o_b, lse, y, u))
        xl = x_next

    dx, loss_part = _loss_grad(xl, target)

    small_rows = []
    gin_slots, gin_parts, gout_parts = None, [None] * DEPTH, [None] * DEPTH
    for l in reversed(range(DEPTH)):
        x_in, scale, gate, w_new, w_out_l, pf, pb, h, o_a, states, o_b, lse, y, u = saved[l]
        du, do, dz, sums_post = _post_bwd(dx, u, gate, (g_post, l, 0), w_out_l, o_a, o_b, pf, (g_heads, l, 0))
        gout_slots = _matmul_tn(y, du, "grad_w_out", 512)
        dq_a, dk_a, dv_a, dlr2, dwgu, dbgu = _gla_bwd(pf, pb, wgu_pad, b_gate_up, l, states, do)
        travelling = [gout_slots] + ([gin_slots] if gin_slots is not None else [])
        dq_b, dk_b, dv_b, *arrived = _dil_bwd(pf, pb, cos, sin_signed, do, o_b, lse, comm=("exchange", travelling))
        gout_parts[l] = arrived[0].reshape(N_DEV, OUT_SHARD, D_MODEL)
        if gin_slots is not None:
            gin_parts[l + 1] = arrived[1].reshape(N_DEV, D_MODEL, W_IN_SHARD)
        dlr = (dlr2[0] + dlr2[1]).astype(BF16)
        pieces = (dz, dq_a, dk_a, dq_b, dk_b, dv_a, dv_b, dlr)
        gin_slots = _grad_w_in(h, pieces).reshape(N_DEV * D_MODEL, W_IN_SHARD)
        if l == 0:
            dx, sums_in, arrived, _, _ = _in_bwd(pieces, w_new, x_in, dx, (g_pre, l, 0), scale,
                                                 comm=("pairsum_exchange", [gin_slots]))
            gin_parts[0] = arrived.reshape(N_DEV // 2, D_MODEL, W_IN_SHARD)
        else:
            dx, sums_in = _in_bwd(pieces, w_new, x_in, dx, (g_pre, l, 0), scale)
        dmod = jnp.concatenate([sums_in[0], sums_in[1], sums_post[0]])
        vecs = jnp.concatenate([sums_in[2], sums_post[1], sums_post[2], dbgu[0]])
        small_rows[0:0] = [_row(dmod, 4096), _row(vecs, 4096), _row(dwgu[:GLA_LOWRANK], 4096)]
    grad_x = dx[None]

    flat = lambda a, rows: a.reshape(rows, a.shape[-1])
    r_ada = DEPTH * D_MODEL
    g_w_in, d_w_in, nm_w_in, nv_w_in = _adamw_layers(w_in, gin_parts, m_w_in, v_w_in, "adamw_w_in", 256)
    g_w_out, d_w_out, nm_w_out, nv_w_out = _adamw_layers(w_out, gout_parts, m_w_out, v_w_out, "adamw_w_out", 128)

    small_rows += [_row(loss_part[0, 0:1], 4096), jnp.zeros((1, 4096), F32)]
    small = _all_gather(jnp.concatenate(small_rows, axis=0), "gather_small").reshape(N_DEV, 8, 4096)
    dmod_all = jnp.stack([small[:, 0, :3 * D_MODEL], small[:, 3, :3 * D_MODEL]])
    dmod_cols = lax.dynamic_slice_in_dim(dmod_all, me * ADA_SHARD, ADA_SHARD, axis=2)
    gwa = _w_ada_grad(c_all, dmod_cols).reshape(1, r_ada, ADA_SHARD)
    g_w_ada, d_w_ada, nm_w_ada, nv_w_ada = (
        t.reshape(w_ada.shape) for t in _adamw(flat(w_ada, r_ada), gwa, flat(m_w_ada, r_ada), flat(v_w_ada, r_ada),
                                               "adamw_w_ada", 256))

    def small_param(w, m, v, cols, row, name):
        n = w.shape[1]
        parts = jnp.stack([small[:, row, cols:cols + n], small[:, row + 3, cols:cols + n]], axis=1)
        return _adamw(w, parts, m, v, name, DEPTH)

    g_b_ada, d_b_ada, nm_b_ada, nv_b_ada = small_param(b_ada, m_b_ada, v_b_ada, 0, 0, "adamw_b_ada")
    g_g_pre, d_g_pre, nm_g_pre, nv_g_pre = small_param(g_pre, m_g_pre, v_g_pre, 0, 1, "adamw_g_pre")
    g_g_post, d_g_post, nm_g_post, nv_g_post = small_param(g_post, m_g_post, v_g_post, 1024, 1, "adamw_g_post")
    g_g_gla, d_g_gla, nm_g_gla, nv_g_gla = small_param(g_gla, m_g_gla, v_g_gla, 2048, 1, "adamw_g_gla")
    g_g_dil, d_g_dil, nm_g_dil, nv_g_dil = small_param(g_dil, m_g_dil, v_g_dil, 2560, 1, "adamw_g_dil")
    g_b_gu, d_b_gu, nm_b_gu, nv_b_gu = small_param(b_gate_up, m_b_gate_up, v_b_gate_up, 3072, 1, "adamw_b_gate_up")
    gu_parts = jnp.stack([small[:, 2], small[:, 5]], axis=1).reshape(N_DEV, DEPTH, GLA_LOWRANK, GU_COLS)
    gu_parts = lax.dynamic_slice_in_dim(gu_parts, me * GU_SHARD, GU_SHARD, axis=3).reshape(
        N_DEV, DEPTH * GLA_LOWRANK, GU_SHARD)
    r_gu = DEPTH * GLA_LOWRANK
    g_w_gu, d_w_gu, nm_w_gu, nv_w_gu = (
        t.reshape(w_gate_up.shape) for t in _adamw(flat(w_gate_up, r_gu), gu_parts, flat(m_w_gate_up, r_gu),
                                                   flat(v_w_gate_up, r_gu), "adamw_w_gate_up", r_gu))
    loss_parts = jnp.broadcast_to(small[:, 6, 0:1].reshape(N_DEV, 1, 1), (N_DEV, 8, LANE))
    loss = _sum_parts(loss_parts)[0, 0]

    return (loss, grad_x,
            g_w_ada, g_b_ada, g_g_pre, g_w_in, g_w_gu, g_b_gu, g_g_gla, g_g_dil, g_w_out, g_g_post,
            d_w_ada, d_b_ada, d_g_pre, d_w_in, d_w_gu, d_b_gu, d_g_gla, d_g_dil, d_w_out, d_g_post,
            nm_w_ada, nm_b_ada, nm_g_pre, nm_w_in, nm_w_gu, nm_b_gu, nm_g_gla, nm_g_dil, nm_w_out, nm_g_post,
            nv_w_ada, nv_b_ada, nv_g_pre, nv_w_in, nv_w_gu, nv_b_gu, nv_g_gla, nv_g_dil, nv_w_out, nv_g_post)


def _sum_parts(parts):
    n_parts = parts.shape[0]

    def body(p_ref, o_ref):
        acc = p_ref[0]
        for k in range(1, n_parts):
            acc = acc + p_ref[k]
        o_ref[...] = acc

    return pl.pallas_call(body, name="sum_loss", out_shape=jax.ShapeDtypeStruct(parts.shape[1:], F32))(parts)
```

```python
import functools
import math

import jax
import jax.numpy as jnp
from jax import lax
from jax.experimental import pallas as pl
from jax.experimental.pallas import tpu as pltpu

F32 = jnp.float32
BF16 = jnp.bfloat16

N_DEV = 8
D_MODEL = 1024
DEPTH = 2
GLA_HEADS = 4
GLA_DK = 64
GLA_DV = 128
GLA_CHUNK = 64
GLA_TAU = 16.0
GLA_LOWRANK = 16
DIL_HEADS = 4
DIL_HD = 128
DIL_BLOCK = 128
DIL_DILATIONS = (1, 4, 16)
ROPE_THETA = 10000.0
EPS = 1e-6
IN_COLS = 3600
W_IN_SHARD = IN_COLS // N_DEV
ADA_SHARD = 3 * D_MODEL // N_DEV
OUT_SHARD = D_MODEL // N_DEV
GU_COLS = GLA_HEADS * GLA_DK
GU_SHARD = GU_COLS // N_DEV

ADAM_LR = 0.001
ADAM_B1 = 0.9
ADAM_B2 = 0.999
ADAM_EPS = 1e-08
ADAM_WD = 0.01
ADAM_STEP = 10

NP = 3712
COL_Z, COL_QA, COL_KA, COL_QB, COL_KB, COL_VA, COL_VB, COL_LR = 0, 1024, 1280, 1536, 2048, 2560, 3072, 3584
NP_F32 = COL_VA
NP_BF16 = NP - NP_F32
LANE = 128
MASK_VALUE = -1e30

MESH = pl.DeviceIdType.MESH
ANY = pl.BlockSpec(memory_space=pl.ANY)


def _params(sem=None, vmem_mb=None):
    kw = {}
    if sem is not None:
        kw["dimension_semantics"] = sem
    if vmem_mb is not None:
        kw["vmem_limit_bytes"] = vmem_mb * 1024 * 1024
    return pltpu.CompilerParams(**kw)


def _dot(a, b):
    return jnp.dot(a, b, preferred_element_type=F32)


def _dot_nt(a, b):
    return lax.dot_general(a, b, (((1,), (1,)), ((), ())), preferred_element_type=F32)


def _dot_tn(a, b):
    return lax.dot_general(a, b, (((0,), (0,)), ((), ())), preferred_element_type=F32)


def _sigmoid(z):
    return 1.0 / (1.0 + jnp.exp(-z))


def _log_sigmoid(z):
    return jnp.minimum(z, 0.0) - jnp.log(1.0 + jnp.exp(-jnp.abs(z)))


def _rowvec(v, width=D_MODEL):
    arr, row, cb = v
    return arr.reshape(arr.shape[0], 1, arr.shape[1]), pl.BlockSpec((None, 1, width), lambda *_: (row, 0, cb))


def _my_position():
    return lax.axis_index("x"), lax.axis_index("y"), lax.axis_index("c")


def _linear(px, py, pc):
    return 4 * px + 2 * py + pc


def _gather_phase(phase, x_ref, out_ref, send_sem, recv_sem, local_sem):
    m = x_ref.shape[0]
    x, y, c = _my_position()
    me, sibling = (x, y, c), (x, y, 1 - c)
    chips = [(1 - x, y), (x, 1 - y), (1 - x, 1 - y)]

    def rows(px, py, pc):
        return out_ref.at[pl.ds(_linear(px, py, pc) * m, m), :]

    def copy(k, block, to, src=None):
        return pltpu.make_async_remote_copy(
            src_ref=rows(*block) if src is None else src, dst_ref=rows(*block),
            send_sem=send_sem(k), recv_sem=recv_sem(k), device_id=to, device_id_type=MESH)

    mine = pltpu.make_async_copy(x_ref, rows(*me), local_sem)
    first = [copy(0, me, sibling, src=x_ref)] + [copy(1 + j, me, (*chip, c), src=x_ref) for j, chip in enumerate(chips)]
    passed = [copy(4 + j, (*chip, c), sibling) for j, chip in enumerate(chips)]
    if phase == "start":
        mine.start()
        for cp in first:
            cp.start()
    elif phase == "forward":
        for j, chip in enumerate(chips):
            copy(1 + j, (*chip, c), me).wait_recv()
            passed[j].start()
    else:
        copy(0, sibling, me).wait_recv()
        for j, chip in enumerate(chips):
            copy(4 + j, (*chip, 1 - c), me).wait_recv()
        for cp in first + passed:
            cp.wait_send()
        mine.wait()


def _exchange_phase(phase, x_ref, out_ref, send_sem, recv_sem, local_sem):
    m = x_ref.shape[0] // N_DEV
    x, y, c = _my_position()
    me = _linear(x, y, c)

    def rows(ref, idx):
        return ref.at[pl.ds(idx * m, m), :]

    peers = [(1 - x if j & 4 else x, 1 - y if j & 2 else y, 1 - c if j & 1 else c) for j in range(1, N_DEV)]
    local = pltpu.make_async_copy(rows(x_ref, me), rows(out_ref, me), local_sem)
    sends = [pltpu.make_async_remote_copy(
        src_ref=rows(x_ref, _linear(*peer)), dst_ref=rows(out_ref, me),
        send_sem=send_sem(j), recv_sem=recv_sem(j), device_id=peer, device_id_type=MESH) for j, peer in enumerate(peers)]
    if phase == "start":
        local.start()
        for cp in sends:
            cp.start()
    else:
        for j, peer in enumerate(peers):
            pltpu.make_async_remote_copy(
                src_ref=rows(x_ref, _linear(*peer)), dst_ref=rows(out_ref, _linear(*peer)),
                send_sem=send_sem(j), recv_sem=recv_sem(j), device_id=peer, device_id_type=MESH).wait_recv()
        for cp in sends:
            cp.wait_send()
        local.wait()


def _pairsum_exchange_phase(phase, x_ref, out_refs, send_sem, recv_sem, local_sem):
    out_ref, stage_ref, pair_ref = out_refs
    m, n = x_ref.shape[0] // N_DEV, x_ref.shape[1]
    x, y, c = _my_position()
    mine = 2 * x + y
    chips = [(qx, qy) for qx in range(2) for qy in range(2)]
    others = [(1 - x, y), (x, 1 - y), (1 - x, 1 - y)]

    def rows(ref, idx):
        return ref.at[pl.ds(idx * m, m), :]

    def remote(src, dst, k, to):
        return pltpu.make_async_remote_copy(src_ref=src, dst_ref=dst, send_sem=send_sem(k), recv_sem=recv_sem(k),
                                            device_id=to, device_id_type=MESH)

    to_sibling = [remote(rows(x_ref, _linear(qx, qy, 1 - c)), rows(stage_ref, q), q, (x, y, 1 - c))
                  for q, (qx, qy) in enumerate(chips)]
    to_chips = [remote(rows(pair_ref, 2 * qx + qy), rows(out_ref, mine), 4 + j, (qx, qy, c))
                for j, (qx, qy) in enumerate(others)]
    keep = pltpu.make_async_copy(rows(pair_ref, mine), rows(out_ref, mine), local_sem)
    if phase == "start":
        for cp in to_sibling:
            cp.start()
    elif phase == "reduce":
        for cp in to_sibling:
            cp.wait_recv()

        def through_vmem(a_buf, b_buf, sems):
            tr = 128
            loads = [(pltpu.make_async_copy(rows(x_ref, _linear(qx, qy, c)), a_buf.at[q % 2], sems.at[q % 2]),
                      pltpu.make_async_copy(rows(stage_ref, q), b_buf.at[q % 2], sems.at[2 + q % 2]))
                     for q, (qx, qy) in enumerate(chips)]
            stores = [pltpu.make_async_copy(a_buf.at[q % 2], rows(pair_ref, q), sems.at[4 + q % 2]) for q in range(4)]
            for q in range(4):
                if q >= 2:
                    stores[q - 2].wait()
                for cp in loads[q]:
                    cp.start()
                for cp in loads[q]:
                    cp.wait()

                def add(r, carry, q=q):
                    tile = pl.ds(pl.multiple_of(r * tr, tr), tr)
                    a_buf[q % 2, tile, :] = (a_buf[q % 2, tile, :].astype(F32)
                                             + b_buf[q % 2, tile, :].astype(F32)).astype(x_ref.dtype)
                    return carry

                lax.fori_loop(0, m // tr, add, 0)
                stores[q].start()
            stores[2].wait()
            stores[3].wait()

        pl.run_scoped(through_vmem, pltpu.VMEM((2, m, n), x_ref.dtype), pltpu.VMEM((2, m, n), x_ref.dtype),
                      pltpu.SemaphoreType.DMA((6,)))
    elif phase == "send":
        keep.start()
        for cp in to_chips:
            cp.start()
    else:
        for j, (qx, qy) in enumerate(others):
            remote(rows(pair_ref, mine), rows(out_ref, 2 * qx + qy), 4 + j, (qx, qy, c)).wait_recv()
        for cp in to_sibling + to_chips:
            cp.wait_send()
        keep.wait()


_COMM_PHASES = {"gather": (_gather_phase, ("start", "forward", "finish")),
                "exchange": (_exchange_phase, ("start", "finish")),
                "pairsum_exchange": (_pairsum_exchange_phase, ("start", "reduce", "send", "finish"))}


def _comm_scratch(n_arrays):
    return [pltpu.SemaphoreType.DMA((n_arrays, 7)), pltpu.SemaphoreType.DMA((n_arrays, 7)),
            pltpu.SemaphoreType.DMA((n_arrays,))]


def _comm_run(kind, phases, x_refs, out_refs, send_sems, recv_sems, local_sems):
    fn = _COMM_PHASES[kind][0]
    per = len(out_refs) // len(x_refs)
    for phase in phases:
        for a, x_ref in enumerate(x_refs):
            outs = out_refs[a] if per == 1 else tuple(out_refs[per * a:per * (a + 1)])
            fn(phase, x_ref, outs, lambda k, a=a: send_sems.at[a, k], lambda k, a=a: recv_sems.at[a, k],
               local_sems.at[a])


def _comm_out_shapes(kind, arrays):
    if kind == "pairsum_exchange":
        return [jax.ShapeDtypeStruct((a.shape[0] // 2, a.shape[1]), a.dtype) for a in arrays for _ in range(3)]
    return [jax.ShapeDtypeStruct((N_DEV * a.shape[0], a.shape[1]) if kind == "gather" else a.shape, a.dtype)
            for a in arrays]


def _comm_call(kind, arrays, name):
    n = len(arrays)
    shapes = _comm_out_shapes(kind, arrays)

    def body(*refs):
        _comm_run(kind, _COMM_PHASES[kind][1], refs[:n], refs[n:n + len(shapes)], *refs[n + len(shapes):])

    return pl.pallas_call(body, name=name, out_shape=shapes, in_specs=[ANY] * n, out_specs=[ANY] * len(shapes),
                          scratch_shapes=_comm_scratch(n))(*arrays)


def _all_gather(xs, name):
    return _comm_call("gather", [xs], name)[0]


def _all_to_all(xs, name):
    return _comm_call("exchange", [xs], name)[0]


def _mod_fwd(c_all, w_ada):
    def body(c_ref, w_ref, o_ref):
        cv = c_ref[...]
        sc = cv * _sigmoid(cv)
        o_ref[0] = _dot(sc.astype(BF16), w_ref[0].astype(BF16))

    return pl.pallas_call(
        body, name="mod_fwd", grid=(DEPTH,),
        out_shape=jax.ShapeDtypeStruct((DEPTH, N_DEV, ADA_SHARD), F32),
        in_specs=[pl.BlockSpec((N_DEV, D_MODEL), lambda l: (0, 0)),
                  pl.BlockSpec((1, D_MODEL, ADA_SHARD), lambda l: (l, 0, 0))],
        out_specs=pl.BlockSpec((1, N_DEV, ADA_SHARD), lambda l: (l, 0, 0)),
        compiler_params=_params(("arbitrary",)),
    )(c_all, w_ada)


def _w_ada_grad(c_all, dmod_cols):
    def body(c_ref, d_ref, o_ref):
        cv = c_ref[...]
        sc = cv * _sigmoid(cv)
        o_ref[0] = lax.dot_general(sc, d_ref[0], (((0,), (0,)), ((), ())), precision=lax.Precision.HIGHEST,
                                   preferred_element_type=F32)

    return pl.pallas_call(
        body, name="w_ada_grad", grid=(DEPTH,),
        out_shape=jax.ShapeDtypeStruct((DEPTH, D_MODEL, ADA_SHARD), F32),
        in_specs=[pl.BlockSpec((N_DEV, D_MODEL), lambda l: (0, 0)),
                  pl.BlockSpec((1, N_DEV, ADA_SHARD), lambda l: (l, 0, 0))],
        out_specs=pl.BlockSpec((1, D_MODEL, ADA_SHARD), lambda l: (l, 0, 0)),
        compiler_params=_params(("arbitrary",)),
    )(c_all, dmod_cols)


def _comm_plumbing(comm):
    if not comm:
        return 0, [], []
    return len(comm[1]), _comm_out_shapes(*comm), _comm_scratch(len(comm[1]))


def _split_refs(refs, n_in, n_out, n_scratch, comm):
    ci, shapes, _ = _comm_plumbing(comm)
    co = len(shapes)
    a, b, c = n_in + ci, n_in + ci + n_out, n_in + ci + n_out + co
    return refs[:n_in], refs[a:b], refs[c:c + n_scratch], refs[n_in:a], refs[b:c], refs[c + n_scratch:]


def _prenorm_proj(x, g_pre, scale, shift, w_new, comm=None, ts=256):
    s_len = x.shape[0]
    n_cin, c_shapes, c_scratch = _comm_plumbing(comm)

    def body(*refs):
        (x_ref, g_ref, sc_ref, sh_ref, w_ref), (pf_ref, pb_ref, h_ref), _, cin, cout, csem = _split_refs(
            refs, 5, 3, 0, comm)
        comm_before, comm_after = _comm_hooks(comm, cin, cout, csem, steps=s_len // ts)
        comm_before()
        xv = x_ref[...]
        rstd = lax.rsqrt(jnp.mean(xv * xv, axis=-1, keepdims=True) + EPS)
        h = (xv * rstd * g_ref[...]) * (1.0 + sc_ref[...]) + sh_ref[...]
        hb = h.astype(BF16)
        h_ref[...] = hb
        for j in range(0, NP, 512):
            w = min(512, NP - j)
            acc = _dot(hb, w_ref[:, j:j + w])
            if j < NP_F32:
                pf_ref[:, j:j + w] = acc
            else:
                pb_ref[:, j - NP_F32:j - NP_F32 + w] = acc.astype(BF16)
        comm_after()

    (g_pre, g_spec), (scale, sc_spec), (shift, sh_spec) = _rowvec(g_pre), _rowvec(scale), _rowvec(shift)
    return pl.pallas_call(
        body, name="prenorm_proj_comm" if comm else "prenorm_proj", grid=(s_len // ts,),
        out_shape=[jax.ShapeDtypeStruct((s_len, NP_F32), F32), jax.ShapeDtypeStruct((s_len, NP_BF16), BF16),
                   jax.ShapeDtypeStruct((s_len, D_MODEL), BF16)] + c_shapes,
        in_specs=[pl.BlockSpec((ts, D_MODEL), lambda i: (i, 0)), g_spec, sc_spec, sh_spec,
                  pl.BlockSpec((D_MODEL, NP), lambda i: (0, 0))] + [ANY] * n_cin,
        out_specs=[pl.BlockSpec((ts, NP_F32), lambda i: (i, 0)), pl.BlockSpec((ts, NP_BF16), lambda i: (i, 0)),
                   pl.BlockSpec((ts, D_MODEL), lambda i: (i, 0))] + [ANY] * len(c_shapes),
        scratch_shapes=c_scratch,
        compiler_params=_params(("arbitrary",), 48),
    )(x, g_pre, scale, shift, w_new, *(comm[1] if comm else []))


GLA_GROUP = 8


def _gla_group_rows(t):
    return [pl.ds(pl.multiple_of((t * GLA_GROUP + j) * GLA_CHUNK, GLA_CHUNK), GLA_CHUNK) for j in range(GLA_GROUP)]


def _gla_chunks_common(q_ref, k_ref, lr_ref, wgu_ref, bgu_ref, rows_list):
    c = GLA_CHUNK
    ri = lax.broadcasted_iota(jnp.int32, (c, c), 0)
    ci = lax.broadcasted_iota(jnp.int32, (c, c), 1)
    tril = (ri >= ci).astype(F32)
    zs = [_dot(lr_ref[rows, :], wgu_ref[...]) + bgu_ref[...] for rows in rows_list]
    las = [_log_sigmoid(z) * (1.0 / GLA_TAU) for z in zs]
    bs = [jnp.dot(tril, la, precision=lax.Precision.HIGHEST, preferred_element_type=F32) for la in las]
    out = []
    for rows, z, b in zip(rows_list, zs, bs):
        q = q_ref[rows, :] * (GLA_DK ** -0.5)
        k = k_ref[rows, :]
        bl = b[c - 1:c, :]
        out.append(dict(z=z, b=b, bl=bl, qe=q * jnp.exp(b), ke=k * jnp.exp(-b), kend=k * jnp.exp(bl - b),
                        dec=jnp.exp(bl)))
    return out, ri, ci


def _head_lane_mask(hh):
    return (lax.broadcasted_iota(jnp.int32, (1, LANE), 1) // GLA_DK) == hh


def _state_block_mask():
    r = lax.broadcasted_iota(jnp.int32, (2 * GLA_DV, LANE), 0) // GLA_DV
    cc = lax.broadcasted_iota(jnp.int32, (2 * GLA_DV, LANE), 1) // GLA_DK
    return r == cc


def _gla_fwd(pf, pb, wgu, bgu, layer, comm=None):
    s_len = pf.shape[0]
    nc = s_len // GLA_CHUNK
    ncomm = len(comm[1]) if comm else 0

    def body(*refs):
        q_ref, k_ref, v_ref, lr_ref, wgu_ref, bgu_ref = refs[:6]
        cin, (o_ref, st_ref), cout = refs[6:6 + ncomm], refs[6 + ncomm:8 + ncomm], refs[8 + ncomm:8 + 2 * ncomm]
        qe_s, cs_s, dec_s = refs[8 + 2 * ncomm:11 + 2 * ncomm]
        comm_before, comm_after = _comm_hooks(comm, cin, cout, refs[11 + 2 * ncomm:], steps=2)
        comm_before()
        bd = _state_block_mask()

        def local(t, carry):
            rows_list = _gla_group_rows(t)
            cm, ri, ci = _gla_chunks_common(q_ref, k_ref, lr_ref, wgu_ref, bgu_ref, rows_list)
            vs = [v_ref[rows, :] for rows in rows_list]
            kebs = [c["ke"].astype(BF16) for c in cm]
            a = [[jnp.where(ri >= ci, _dot_nt(jnp.where(_head_lane_mask(hh), c["qe"], 0.0).astype(BF16), keb), 0.0)
                  .astype(BF16) for hh in range(2)] for c, keb in zip(cm, kebs)]
            oi = [[_dot(ah[hh], v[:, hh * GLA_DV:(hh + 1) * GLA_DV]) for hh in range(2)] for ah, v in zip(a, vs)]
            cs = [jnp.where(bd, _dot_tn(v, c["kend"].astype(BF16)), 0.0) for c, v in zip(cm, vs)]
            for j, (rows, c) in enumerate(zip(rows_list, cm)):
                n = t * GLA_GROUP + j
                o_ref[rows, :] = jnp.concatenate(oi[j], axis=1)
                qe_s[rows, :] = c["qe"].astype(BF16)
                cs_s[n] = cs[j]
                dec_s[n] = jnp.broadcast_to(c["dec"], (8, LANE))
            return carry

        lax.fori_loop(0, nc // GLA_GROUP, local, 0)

        def scan(n, st):
            st_ref[0, n] = st.astype(BF16)
            return dec_s[n][0:1, :] * st + cs_s[n]

        lax.fori_loop(0, nc, scan, jnp.zeros((2 * GLA_DV, LANE), F32))

        def inter(t, carry):
            rows_list = _gla_group_rows(t)
            add = [_dot_nt(qe_s[rows, :], st_ref[0, t * GLA_GROUP + j]) for j, rows in enumerate(rows_list)]
            for rows, av in zip(rows_list, add):
                o_ref[rows, :] = o_ref[rows, :] + av
            return carry

        lax.fori_loop(0, nc // GLA_GROUP, inter, 0)
        comm_after()

    return pl.pallas_call(
        body, name="gla_fwd_comm" if comm else "gla_fwd", grid=(2,),
        out_shape=[jax.ShapeDtypeStruct((s_len, GLA_HEADS * GLA_DV), F32),
                   jax.ShapeDtypeStruct((2, nc, 2 * GLA_DV, LANE), BF16)] + (_comm_out_shapes(*comm) if comm else []),
        in_specs=[pl.BlockSpec((s_len, LANE), lambda g: (0, COL_QA // LANE + g)),
                  pl.BlockSpec((s_len, LANE), lambda g: (0, COL_KA // LANE + g)),
                  pl.BlockSpec((s_len, 2 * GLA_DV), lambda g: (0, (COL_VA - NP_F32) // (2 * GLA_DV) + g)),
                  pl.BlockSpec((s_len, LANE), lambda g: (0, (COL_LR - NP_F32) // LANE)),
                  pl.BlockSpec((None, LANE, LANE), lambda g: (layer, 0, g)),
                  pl.BlockSpec((None, 1, LANE), lambda g: (layer, 0, g))] + [ANY] * ncomm,
        out_specs=[pl.BlockSpec((s_len, 2 * GLA_DV), lambda g: (0, g)),
                   pl.BlockSpec((1, nc, 2 * GLA_DV, LANE), lambda g: (g, 0, 0, 0))] + [ANY] * ncomm,
        scratch_shapes=[pltpu.VMEM((s_len, LANE), BF16), pltpu.VMEM((nc, 2 * GLA_DV, LANE), F32),
                        pltpu.VMEM((nc, 8, LANE), F32)] + (_comm_scratch(ncomm) if comm else []),
        compiler_params=_params(("arbitrary",), 56),
    )(pf, pf, pb, pb, wgu, bgu.reshape(bgu.shape[0], 1, GU_COLS), *(comm[1] if comm else []))


def _rope_tables(s_len):
    inv_freq = ROPE_THETA ** (-jnp.arange(0, DIL_HD, 2, dtype=F32) / DIL_HD)
    ang = jnp.arange(s_len, dtype=F32)[:, None] * inv_freq[None, :]
    cos, sin = jnp.cos(ang), jnp.sin(ang)
    return jnp.concatenate([cos, cos], axis=1), jnp.concatenate([-sin, sin], axis=1)


def _rope(xv, cos, sin_signed):
    return xv * cos + pltpu.roll(xv, DIL_HD // 2, 1) * sin_signed


DIL_GROUP = 8
DIL_GROUP_FWD = 8


def _dil_pair_block(i, half, d, nblk, group=DIL_GROUP):
    nb = nblk // d
    j = i + half * (nblk // group)
    if nb >= 2 * group:
        r, n = j % d, j // d
    else:
        r, n = j // nb, j % nb
    kb = jnp.maximum(n - 1, 0)
    qs = r + d * DIL_BLOCK * n
    ks = r + d * DIL_BLOCK * kb
    return qs, ks, jnp.minimum(n, 1)


def _dil_fill_bias(bias):
    qi = lax.broadcasted_iota(jnp.int32, (DIL_BLOCK, 2 * DIL_BLOCK), 0)
    kj = lax.broadcasted_iota(jnp.int32, (DIL_BLOCK, 2 * DIL_BLOCK), 1)
    for sel in range(2):
        dist = qi - kj + DIL_BLOCK * sel
        bias[sel] = jnp.where((dist >= 0) & (dist <= DIL_BLOCK), 0.0, MASK_VALUE)


def _strided(start, size, d):
    return pl.ds(start, size) if d == 1 else pl.ds(start, size, stride=d)


def _comm_hooks(comm, cin, cout, csem, steps=DIL_HEADS):
    def before():
        if comm:
            @pl.when(pl.program_id(0) == 0)
            def _():
                _comm_run(comm[0], ("start",), cin, cout, *csem)

            if comm[0] == "gather":
                @pl.when(pl.program_id(0) == steps - 1)
                def _():
                    _comm_run(comm[0], ("forward",), cin, cout, *csem)

            if comm[0] == "pairsum_exchange":
                @pl.when(pl.program_id(0) == (1 if steps <= 4 else 3))
                def _():
                    _comm_run(comm[0], ("reduce", "send"), cin, cout, *csem)

    def after():
        if comm:
            @pl.when(pl.program_id(0) == steps - 1)
            def _():
                _comm_run(comm[0], ("finish",), cin, cout, *csem)

    return before, after


def _dil_fwd(pf, pb, cos, sin_signed, comm=None):
    s_len = pf.shape[0]
    nblk = s_len // DIL_BLOCK
    prep_rows = 256
    scale = DIL_HD ** -0.5
    nc = len(comm[1]) if comm else 0

    def body(*refs):
        q_ref, k_ref, v_ref, cos_ref, sin_ref = refs[:5]
        cin, (o_ref, lse_ref), cout = refs[5:5 + nc], refs[5 + nc:7 + nc], refs[7 + nc:7 + 2 * nc]
        qf, kf, vf, o0, o1, o2, l0, l1, l2, bias = refs[7 + 2 * nc:17 + 2 * nc]
        comm_before, comm_after = _comm_hooks(comm, cin, cout, refs[17 + 2 * nc:])
        comm_before()
        _dil_fill_bias(bias)

        def prep(t, carry):
            rows = pl.ds(pl.multiple_of(t * prep_rows, prep_rows), prep_rows)
            cs, sn = cos_ref[rows, :], sin_ref[rows, :]
            qf[rows, :] = _rope(q_ref[rows, :], cs, sn)
            kf[rows, :] = _rope(k_ref[rows, :], cs, sn)
            vf[rows, :] = v_ref[rows, :].astype(F32)
            return carry

        lax.fori_loop(0, s_len // prep_rows, prep, 0)
        ones = jnp.ones((2 * DIL_BLOCK, DIL_HD), BF16)

        for d, o_p, l_p in zip(DIL_DILATIONS, (o0, o1, o2), (l0, l1, l2)):
            if nblk // d == 2:
                units = DIL_GROUP_FWD // 2

                def whole(i, carry, d=d, o_p=o_p, l_p=l_p, units=units):
                    rows = [_strided(i + u * (d // units), 2 * DIL_BLOCK, d) for u in range(units)]
                    ld = [(qf[rw, :].astype(BF16), kf[rw, :].astype(BF16), vf[rw, :].astype(BF16)) for rw in rows]
                    both = bias[...].reshape(2 * DIL_BLOCK, 2 * DIL_BLOCK)
                    s = [_dot_nt(qb, kk) * scale + both for qb, kk, _ in ld]
                    m = [jnp.max(sv, axis=-1, keepdims=True) for sv in s]
                    p = [jnp.exp(sv - mv) for sv, mv in zip(s, m)]
                    den = [jnp.sum(pv, axis=-1, keepdims=True) for pv in p]
                    r = [_dot(pv.astype(BF16), vv) for pv, (_, _, vv) in zip(p, ld)]
                    for rv, dv, mv, rw in zip(r, den, m, rows):
                        o_p[rw, :] = rv / dv
                        l_p[rw, :] = jnp.broadcast_to(mv + jnp.log(dv), (2 * DIL_BLOCK, DIL_HD))
                    return carry

                lax.fori_loop(0, d // units, whole, 0)
                continue

            def pair(i, carry, d=d, o_p=o_p, l_p=l_p):
                idx = [_dil_pair_block(i, half, d, nblk, DIL_GROUP_FWD) for half in range(DIL_GROUP_FWD)]
                ld = [(qf[_strided(qs, DIL_BLOCK, d), :].astype(BF16),
                       kf[_strided(ks, 2 * DIL_BLOCK, d), :].astype(BF16),
                       vf[_strided(ks, 2 * DIL_BLOCK, d), :].astype(BF16)) for qs, ks, _ in idx]
                s = [_dot_nt(qb, kk) * scale + bias[sel] for (qb, kk, _), (_, _, sel) in zip(ld, idx)]
                m = [jnp.max(sv, axis=-1, keepdims=True) for sv in s]
                p = [jnp.exp(sv - mv) for sv, mv in zip(s, m)]
                den = [jnp.sum(pv, axis=-1, keepdims=True) for pv in p]
                r = [_dot(pv.astype(BF16), vv) for pv, (_, _, vv) in zip(p, ld)]
                for rv, dv, mv, (qs, _, _) in zip(r, den, m, idx):
                    o_p[_strided(qs, DIL_BLOCK, d), :] = rv / dv
                    l_p[_strided(qs, DIL_BLOCK, d), :] = jnp.broadcast_to(mv + jnp.log(dv), (DIL_BLOCK, DIL_HD))
                return carry

            lax.fori_loop(0, nblk // DIL_GROUP_FWD, pair, 0)

        def comb(t, carry):
            rows = pl.ds(pl.multiple_of(t * prep_rows, prep_rows), prep_rows)
            a0, a1, a2 = l0[rows, :], l1[rows, :], l2[rows, :]
            m = jnp.maximum(jnp.maximum(a0, a1), a2)
            e0, e1, e2 = jnp.exp(a0 - m), jnp.exp(a1 - m), jnp.exp(a2 - m)
            tot = e0 + e1 + e2
            o_ref[rows, :] = (e0 * o0[rows, :] + e1 * o1[rows, :] + e2 * o2[rows, :]) / tot
            lse_ref[rows, :] = m + jnp.log(tot)
            return carry

        lax.fori_loop(0, s_len // prep_rows, comb, 0)
        comm_after()

    head = lambda base: pl.BlockSpec((s_len, DIL_HD), lambda h: (0, base // DIL_HD + h))
    table = pl.BlockSpec((s_len, DIL_HD), lambda h: (0, 0))
    out = pl.BlockSpec((s_len, DIL_HD), lambda h: (0, h))
    shp = jax.ShapeDtypeStruct((s_len, DIL_HEADS * DIL_HD), F32)
    return pl.pallas_call(
        body, name="dil_fwd_comm" if comm else "dil_fwd", grid=(DIL_HEADS,),
        out_shape=[shp, shp] + (_comm_out_shapes(*comm) if comm else []),
        in_specs=[head(COL_QB), head(COL_KB), head(COL_VB - NP_F32), table, table] + [ANY] * nc,
        out_specs=[out, out] + [ANY] * nc,
        scratch_shapes=[pltpu.VMEM((s_len, DIL_HD), F32) for _ in range(9)]
        + [pltpu.VMEM((2, DIL_BLOCK, 2 * DIL_BLOCK), F32)] + (_comm_scratch(nc) if comm else []),
        compiler_params=_params(("arbitrary",), 56),
    )(pf, pf, pb, cos, sin_signed, *(comm[1] if comm else []))


def _silu_and_grad(z):
    sg = _sigmoid(z)
    return z * sg, sg * (1.0 + z * (1.0 - sg))


def _post_fwd(o_a, o_b, pf, g_heads, w_out, x, gate, g_post, ts=256):
    s_len = x.shape[0]
    half = GLA_HEADS * GLA_DV

    def body(oa_ref, ob_ref, z_ref, gh_ref, w_ref, x_ref, gate_ref, gp_ref, xo_ref, y_ref, u_ref):
        for src, base in ((oa_ref, 0), (ob_ref, half)):
            for hh in range(4):
                lo = hh * LANE
                og = src[:, lo:lo + LANE]
                on = og * lax.rsqrt(jnp.mean(og * og, axis=-1, keepdims=True) + EPS)
                zg = z_ref[:, base + lo:base + lo + LANE].astype(F32)
                y_ref[:, base + lo:base + lo + LANE] = (on * gh_ref[:, base + lo:base + lo + LANE]
                                                        * (zg * _sigmoid(zg))).astype(BF16)
        u = _dot(y_ref[...], w_ref[...])
        u_ref[...] = u.astype(BF16)
        rstd = lax.rsqrt(jnp.mean(u * u, axis=-1, keepdims=True) + EPS)
        xo_ref[...] = x_ref[...] + gate_ref[...] * (u * rstd * gp_ref[...])

    (g_heads, gh_spec), (gate, gate_spec), (g_post, gp_spec) = _rowvec(g_heads), _rowvec(gate), _rowvec(g_post)
    tile = pl.BlockSpec((ts, D_MODEL), lambda i: (i, 0))
    halft = pl.BlockSpec((ts, half), lambda i: (i, 0))
    return pl.pallas_call(
        body, name="post_fwd", grid=(s_len // ts,),
        out_shape=(jax.ShapeDtypeStruct((s_len, D_MODEL), F32), jax.ShapeDtypeStruct((s_len, D_MODEL), BF16),
                   jax.ShapeDtypeStruct((s_len, D_MODEL), BF16)),
        in_specs=[halft, halft, tile, gh_spec, pl.BlockSpec((D_MODEL, D_MODEL), lambda i: (0, 0)), tile, gate_spec,
                  gp_spec],
        out_specs=(tile, tile, tile),
        compiler_params=_params(("arbitrary",), 40),
    )(o_a, o_b, pf, g_heads, w_out, x, gate, g_post)


def _loss_grad(y, target, ts=512):
    s_len = y.shape[0]

    def body(y_ref, t_ref, dy_ref, loss_ref):
        @pl.when(pl.program_id(0) == 0)
        def _():
            loss_ref[...] = jnp.zeros_like(loss_ref)

        e = y_ref[...] - t_ref[...]
        dy_ref[...] = e * (1.0 / D_MODEL)
        loss_ref[...] += 0.5 * jnp.sum(jnp.mean(e * e, axis=-1, keepdims=True))

    tile = pl.BlockSpec((ts, D_MODEL), lambda i: (i, 0))
    return pl.pallas_call(
        body, name="loss_grad", grid=(s_len // ts,),
        out_shape=(jax.ShapeDtypeStruct((s_len, D_MODEL), F32), jax.ShapeDtypeStruct((8, LANE), F32)),
        in_specs=[tile, tile], out_specs=(tile, pl.BlockSpec((8, LANE), lambda i: (0, 0))),
        compiler_params=_params(("arbitrary",)),
    )(y, target)


def _post_bwd(dxo, u, gate, g_post, w_out, o_a, o_b, pf, g_heads, ts=256):
    s_len = dxo.shape[0]
    half = GLA_HEADS * GLA_DV

    def body(dx_ref, u_ref, gate_ref, gp_ref, w_ref, oa_ref, ob_ref, z_ref, gh_ref, du_ref, do_ref, dz_ref, sums_ref):
        @pl.when(pl.program_id(0) == 0)
        def _():
            sums_ref[...] = jnp.zeros_like(sums_ref)

        dx = dx_ref[...]
        u = u_ref[...].astype(F32)
        rstd = lax.rsqrt(jnp.mean(u * u, axis=-1, keepdims=True) + EPS)
        un = u * rstd
        sums_ref[0:1, :] += jnp.sum(dx * (un * gp_ref[...]), axis=0, keepdims=True)
        drn = dx * gate_ref[...]
        sums_ref[1:2, :] += jnp.sum(drn * un, axis=0, keepdims=True)
        dun = drn * gp_ref[...]
        du = rstd * (dun - un * jnp.mean(dun * un, axis=-1, keepdims=True))
        dub = du.astype(BF16)
        du_ref[...] = dub
        dy = _dot_nt(dub, w_ref[...])
        for src, base in ((oa_ref, 0), (ob_ref, half)):
            for hh in range(4):
                lo = base + hh * LANE
                og = src[:, hh * LANE:(hh + 1) * LANE]
                rs = lax.rsqrt(jnp.mean(og * og, axis=-1, keepdims=True) + EPS)
                on = og * rs
                zg = z_ref[:, lo:lo + LANE].astype(F32)
                sz, dsz = _silu_and_grad(zg)
                gg = gh_ref[:, lo:lo + LANE]
                dyg = dy[:, lo:lo + LANE]
                sums_ref[2:3, lo:lo + LANE] += jnp.sum(dyg * sz * on, axis=0, keepdims=True)
                dz_ref[:, lo:lo + LANE] = (dyg * on * gg * dsz).astype(BF16)
                don = dyg * gg * sz
                do_ref[:, lo:lo + LANE] = (rs * (don - on * jnp.mean(don * on, axis=-1, keepdims=True))).astype(BF16)

    (g_heads, gh_spec), (gate, gate_spec), (g_post, gp_spec) = _rowvec(g_heads), _rowvec(gate), _rowvec(g_post)
    tile = pl.BlockSpec((ts, D_MODEL), lambda i: (i, 0))
    halft = pl.BlockSpec((ts, half), lambda i: (i, 0))
    return pl.pallas_call(
        body, name="post_bwd", grid=(s_len // ts,),
        out_shape=(jax.ShapeDtypeStruct((s_len, D_MODEL), BF16), jax.ShapeDtypeStruct((s_len, D_MODEL), BF16),
                   jax.ShapeDtypeStruct((s_len, D_MODEL), BF16), jax.ShapeDtypeStruct((8, D_MODEL), F32)),
        in_specs=[tile, tile, gate_spec, gp_spec, pl.BlockSpec((D_MODEL, D_MODEL), lambda i: (0, 0)), halft, halft,
                  tile, gh_spec],
        out_specs=(tile, tile, tile, pl.BlockSpec((8, D_MODEL), lambda i: (0, 0))),
        compiler_params=_params(("arbitrary",), 40),
    )(dxo, u, gate, g_post, w_out, o_a, o_b, pf, g_heads)


def _gla_bwd(pf, pb, wgu, bgu, layer, states, do, comm=None):
    s_len = pf.shape[0]
    nc = s_len // GLA_CHUNK
    c = GLA_CHUNK
    n_cin, c_shapes, c_scratch = _comm_plumbing(comm)

    def body(*refs):
        ((q_ref, k_ref, v_ref, lr_ref, wgu_ref, bgu_ref, st_ref, do_ref),
         (dq_ref, dk_ref, dv_ref, dlr_ref, dwgu_ref, dbgu_ref), (ds_s, dec_s, dw_acc, db_acc),
         cin, cout, csem) = _split_refs(refs, 8, 6, 4, comm)
        comm_before, comm_after = _comm_hooks(comm, cin, cout, csem, steps=2)
        comm_before()
        dw_acc[...] = jnp.zeros_like(dw_acc)
        db_acc[...] = jnp.zeros_like(db_acc)
        bd = _state_block_mask()
        last_row = lax.broadcasted_iota(jnp.int32, (c, LANE), 0) == c - 1

        def local(t, carry):
            rows_list = _gla_group_rows(t)
            cm, _, _ = _gla_chunks_common(q_ref, k_ref, lr_ref, wgu_ref, bgu_ref, rows_list)
            loc = [jnp.where(bd, _dot_tn(do_ref[rows, :], cc["qe"].astype(BF16)), 0.0)
                   for rows, cc in zip(rows_list, cm)]
            for j, cc in enumerate(cm):
                ds_s[t * GLA_GROUP + j] = loc[j]
                dec_s[t * GLA_GROUP + j] = jnp.broadcast_to(cc["dec"], (8, LANE))
            return carry

        lax.fori_loop(0, nc // GLA_GROUP, local, 0)

        def scan(t, dst):
            n = nc - 1 - t
            loc = ds_s[n]
            ds_s[n] = dst
            return dec_s[n][0:1, :] * dst + loc

        lax.fori_loop(0, nc, scan, jnp.zeros((2 * GLA_DV, LANE), F32))

        def rest(t, carry):
            rows_list = _gla_group_rows(t)
            cm, ri, ci = _gla_chunks_common(q_ref, k_ref, lr_ref, wgu_ref, bgu_ref, rows_list)
            ns = [t * GLA_GROUP + j for j in range(GLA_GROUP)]
            vs = [v_ref[rows, :] for rows in rows_list]
            dobs = [do_ref[rows, :] for rows in rows_list]
            stbs = [st_ref[0, n] for n in ns]
            dsts = [ds_s[n] for n in ns]
            dstbs = [d.astype(BF16) for d in dsts]
            qebs = [cc["qe"].astype(BF16) for cc in cm]
            kebs = [cc["ke"].astype(BF16) for cc in cm]
            kendbs = [cc["kend"].astype(BF16) for cc in cm]
            hms = [_head_lane_mask(hh) for hh in range(2)]
            qehs = [[jnp.where(hm, cc["qe"], 0.0).astype(BF16) for hm in hms] for cc in cm]
            kehs = [[jnp.where(hm, cc["ke"], 0.0).astype(BF16) for hm in hms] for cc in cm]
            heads = lambda x: [x[:, hh * GLA_DV:(hh + 1) * GLA_DV] for hh in range(2)]
            vhs, dohs = [heads(v) for v in vs], [heads(d) for d in dobs]

            dqe0 = [_dot(dob, stb) for dob, stb in zip(dobs, stbs)]
            dkend = [_dot(v, dstb) for v, dstb in zip(vs, dstbs)]
            dv0 = [_dot_nt(kb, dstb) for kb, dstb in zip(kendbs, dstbs)]
            a_t = [[jnp.where(ci >= ri, _dot_nt(kehs[j][hh], qebs[j]), 0.0).astype(BF16) for hh in range(2)]
                   for j in range(GLA_GROUP)]
            da = [[jnp.where(ri >= ci, _dot_nt(dohs[j][hh], vhs[j][hh]), 0.0).astype(BF16) for hh in range(2)]
                  for j in range(GLA_GROUP)]
            da_t = [[jnp.where(ci >= ri, _dot_nt(vhs[j][hh], dohs[j][hh]), 0.0).astype(BF16) for hh in range(2)]
                    for j in range(GLA_GROUP)]
            dv1 = [[_dot(a_t[j][hh], dohs[j][hh]) for hh in range(2)] for j in range(GLA_GROUP)]
            dqe1 = [[_dot(da[j][hh], kebs[j]) for hh in range(2)] for j in range(GLA_GROUP)]
            dke1 = [[_dot(da_t[j][hh], qehs[j][hh]) for hh in range(2)] for j in range(GLA_GROUP)]

            dbs, dzs = [], []
            for j, (rows, cc) in enumerate(zip(rows_list, cm)):
                qe, ke, kend, b, bl = cc["qe"], cc["ke"], cc["kend"], cc["b"], cc["bl"]
                dqe = dqe0[j] + jnp.where(hms[0], dqe1[j][0], 0.0) + jnp.where(hms[1], dqe1[j][1], 0.0)
                dke = jnp.where(hms[0], dke1[j][0], 0.0) + jnp.where(hms[1], dke1[j][1], 0.0)
                dv_ref[rows, :] = (dv0[j] + jnp.concatenate(dv1[j], axis=1)).astype(BF16)
                dq_ref[rows, :] = (dqe * jnp.exp(b) * (GLA_DK ** -0.5)).astype(BF16)
                dk_ref[rows, :] = (dke * jnp.exp(-b) + dkend[j] * jnp.exp(bl - b)).astype(BF16)
                ddec = jnp.sum(dsts[j] * stbs[j].astype(F32), axis=0, keepdims=True)
                dbl = jnp.sum(dkend[j] * kend, axis=0, keepdims=True) + ddec * cc["dec"]
                dbs.append(dqe * qe - dke * ke - dkend[j] * kend + jnp.where(last_row, dbl, 0.0))
            triu = (ci >= ri).astype(F32)
            dlas = [jnp.dot(triu, db, precision=lax.Precision.HIGHEST, preferred_element_type=F32) for db in dbs]
            dzs = [dla * (1.0 / GLA_TAU) * _sigmoid(-cc["z"]) for dla, cc in zip(dlas, cm)]
            dzbs = [dz.astype(BF16) for dz in dzs]
            dlrs = [_dot_nt(dzb, wgu_ref[...]) for dzb in dzbs]
            dws = [_dot_tn(lr_ref[rows, :], dzb) for rows, dzb in zip(rows_list, dzbs)]
            for rows, dlr in zip(rows_list, dlrs):
                dlr_ref[0, rows, :] = dlr
            dw_acc[...] += functools.reduce(lambda x, y: x + y, dws)
            db_acc[0:1, :] += jnp.sum(functools.reduce(lambda x, y: x + y, dzs), axis=0, keepdims=True)
            return carry

        lax.fori_loop(0, nc // GLA_GROUP, rest, 0)
        dwgu_ref[...] = dw_acc[...]
        dbgu_ref[...] = db_acc[...]
        comm_after()

    pair = pl.BlockSpec((s_len, LANE), lambda g: (0, g))
    return pl.pallas_call(
        body, name="gla_bwd_comm" if comm else "gla_bwd", grid=(2,),
        out_shape=[jax.ShapeDtypeStruct((s_len, GU_COLS), BF16), jax.ShapeDtypeStruct((s_len, GU_COLS), BF16),
                   jax.ShapeDtypeStruct((s_len, GLA_HEADS * GLA_DV), BF16),
                   jax.ShapeDtypeStruct((2, s_len, LANE), F32),
                   jax.ShapeDtypeStruct((LANE, GU_COLS), F32), jax.ShapeDtypeStruct((8, GU_COLS), F32)] + c_shapes,
        in_specs=[pl.BlockSpec((s_len, LANE), lambda g: (0, COL_QA // LANE + g)),
                  pl.BlockSpec((s_len, LANE), lambda g: (0, COL_KA // LANE + g)),
                  pl.BlockSpec((s_len, 2 * GLA_DV), lambda g: (0, (COL_VA - NP_F32) // (2 * GLA_DV) + g)),
                  pl.BlockSpec((s_len, LANE), lambda g: (0, (COL_LR - NP_F32) // LANE)),
                  pl.BlockSpec((None, LANE, LANE), lambda g: (layer, 0, g)),
                  pl.BlockSpec((None, 1, LANE), lambda g: (layer, 0, g)),
                  pl.BlockSpec((1, nc, 2 * GLA_DV, LANE), lambda g: (g, 0, 0, 0)),
                  pl.BlockSpec((s_len, 2 * GLA_DV), lambda g: (0, g))] + [ANY] * n_cin,
        out_specs=[pair, pair, pl.BlockSpec((s_len, 2 * GLA_DV), lambda g: (0, g)),
                   pl.BlockSpec((1, s_len, LANE), lambda g: (g, 0, 0)),
                   pl.BlockSpec((LANE, LANE), lambda g: (0, g)), pl.BlockSpec((8, LANE), lambda g: (0, g))]
        + [ANY] * len(c_shapes),
        scratch_shapes=[pltpu.VMEM((nc, 2 * GLA_DV, LANE), F32), pltpu.VMEM((nc, 8, LANE), F32),
                        pltpu.VMEM((LANE, LANE), F32), pltpu.VMEM((8, LANE), F32)] + c_scratch,
        compiler_params=_params(("arbitrary",), 56),
    )(pf, pf, pb, pb, wgu, bgu.reshape(bgu.shape[0], 1, GU_COLS), states, do, *(comm[1] if comm else []))


def _dil_bwd(pf, pb, cos, sin_signed, do, o_b, lse, comm=None):
    s_len = pf.shape[0]
    nblk = s_len // DIL_BLOCK
    prep_rows = 256
    scale = DIL_HD ** -0.5
    nc = len(comm[1]) if comm else 0

    def body(*refs):
        ((q_ref, k_ref, v_ref, cos_ref, sin_ref, do_ref, o_ref, lse_ref), (dq_ref, dk_ref, dv_ref),
         (qf, kf, vf, dof, dl, dqa, dka, dva, bias), cin, cout, csem) = _split_refs(refs, 8, 3, 9, comm)
        comm_before, comm_after = _comm_hooks(comm, cin, cout, csem)
        comm_before()
        _dil_fill_bias(bias)

        def prep(t, carry):
            rows = pl.ds(pl.multiple_of(t * prep_rows, prep_rows), prep_rows)
            cs, sn = cos_ref[rows, :], sin_ref[rows, :]
            qf[rows, :] = _rope(q_ref[rows, :], cs, sn) * scale
            kf[rows, :] = _rope(k_ref[rows, :], cs, sn)
            vf[rows, :] = v_ref[rows, :].astype(F32)
            dov = do_ref[rows, :].astype(F32)
            dof[rows, :] = dov
            dl[rows, :] = jnp.broadcast_to(jnp.sum(dov * o_ref[rows, :], axis=-1, keepdims=True), (prep_rows, DIL_HD))
            zero = jnp.zeros((prep_rows, DIL_HD), F32)
            dqa[rows, :] = zero
            dka[rows, :] = zero
            dva[rows, :] = zero
            return carry

        lax.fori_loop(0, s_len // prep_rows, prep, 0)

        for d in DIL_DILATIONS:
            if nblk // d == 2:
                units = DIL_GROUP // 2

                def whole(i, carry, d=d, units=units):
                    rows = [_strided(i + u * (d // units), 2 * DIL_BLOCK, d) for u in range(units)]
                    ld = [(qf[rw, :].astype(BF16), kf[rw, :].astype(BF16), vf[rw, :].astype(BF16),
                           dof[rw, :].astype(BF16)) for rw in rows]
                    both = bias[...].reshape(2 * DIL_BLOCK, 2 * DIL_BLOCK)
                    s = [_dot_nt(qb, kk) + both for qb, kk, _, _ in ld]
                    dp = [_dot_nt(dob, vv) for _, _, vv, dob in ld]
                    p = [jnp.exp(sv - lse_ref[rw, :][:, 0:1]) for sv, rw in zip(s, rows)]
                    ds = [(pv * (dpv - dl[rw, :][:, 0:1])).astype(BF16) for pv, dpv, rw in zip(p, dp, rows)]
                    pb = [pv.astype(BF16) for pv in p]
                    gq = [_dot(dsv, kk) for dsv, (_, kk, _, _) in zip(ds, ld)]
                    gk = [_dot_tn(dsv, qb) for dsv, (qb, _, _, _) in zip(ds, ld)]
                    gv = [_dot_tn(pv, dob) for pv, (_, _, _, dob) in zip(pb, ld)]
                    for rw, a, b, c in zip(rows, gq, gk, gv):
                        dqa[rw, :] += a
                        dka[rw, :] += b
                        dva[rw, :] += c
                    return carry

                lax.fori_loop(0, d // units, whole, 0)
                continue

            def pair(i, carry, d=d):
                idx = [_dil_pair_block(i, half, d, nblk) for half in range(DIL_GROUP)]
                rows = [(_strided(qs, DIL_BLOCK, d), _strided(ks, 2 * DIL_BLOCK, d)) for qs, ks, _ in idx]
                ld = [(qf[qr, :].astype(BF16), kf[kr, :].astype(BF16), vf[kr, :].astype(BF16),
                       dof[qr, :].astype(BF16)) for qr, kr in rows]
                s = [_dot_nt(qb, kk) + bias[sel] for (qb, kk, _, _), (_, _, sel) in zip(ld, idx)]
                dp = [_dot_nt(dob, vv) for _, _, vv, dob in ld]
                p = [jnp.exp(sv - lse_ref[qr, :][:, 0:1]) for sv, (qr, _) in zip(s, rows)]
                ds = [(pv * (dpv - dl[qr, :][:, 0:1])).astype(BF16) for pv, dpv, (qr, _) in zip(p, dp, rows)]
                pb = [pv.astype(BF16) for pv in p]
                gq = [_dot(dsv, kk) for dsv, (_, kk, _, _) in zip(ds, ld)]
                gk = [_dot_tn(dsv, qb) for dsv, (qb, _, _, _) in zip(ds, ld)]
                gv = [_dot_tn(pv, dob) for pv, (_, _, _, dob) in zip(pb, ld)]
                for (qr, kr), a, b, c in zip(rows, gq, gk, gv):
                    dqa[qr, :] += a
                    dka[kr, :] += b
                    dva[kr, :] += c
                return carry

            lax.fori_loop(0, nblk // DIL_GROUP, pair, 0)

        def fin(t, carry):
            rows = pl.ds(pl.multiple_of(t * prep_rows, prep_rows), prep_rows)
            cs, sn = cos_ref[rows, :], sin_ref[rows, :]
            gq, gk = dqa[rows, :] * scale, dka[rows, :]
            dq_ref[rows, :] = (gq * cs - pltpu.roll(gq, DIL_HD // 2, 1) * sn).astype(BF16)
            dk_ref[rows, :] = (gk * cs - pltpu.roll(gk, DIL_HD // 2, 1) * sn).astype(BF16)
            dv_ref[rows, :] = dva[rows, :].astype(BF16)
            return carry

        lax.fori_loop(0, s_len // prep_rows, fin, 0)
        comm_after()

    head = lambda base: pl.BlockSpec((s_len, DIL_HD), lambda h: (0, base // DIL_HD + h))
    table = pl.BlockSpec((s_len, DIL_HD), lambda h: (0, 0))
    out = pl.BlockSpec((s_len, DIL_HD), lambda h: (0, h))
    shp = jax.ShapeDtypeStruct((s_len, DIL_HEADS * DIL_HD), BF16)
    return pl.pallas_call(
        body, name="dil_bwd_comm" if comm else "dil_bwd", grid=(DIL_HEADS,),
        out_shape=[shp, shp, shp] + (_comm_out_shapes(*comm) if comm else []),
        in_specs=[head(COL_QB), head(COL_KB), head(COL_VB - NP_F32), table, table,
                  pl.BlockSpec((s_len, DIL_HD), lambda h: (0, DIL_HEADS + h)), out, out] + [ANY] * nc,
        out_specs=[out, out, out] + [ANY] * len(_comm_plumbing(comm)[1]),
        scratch_shapes=[pltpu.VMEM((s_len, DIL_HD), F32) for _ in range(8)]
        + [pltpu.VMEM((2, DIL_BLOCK, 2 * DIL_BLOCK), F32)] + (_comm_scratch(nc) if comm else []),
        compiler_params=_params(("arbitrary",), 56),
    )(pf, pf, pb, cos, sin_signed, do, o_b, lse, *(comm[1] if comm else []))


_PIECES = ((COL_Z, 1024), (COL_QA, 256), (COL_KA, 256), (COL_QB, 512), (COL_KB, 512), (COL_VA, 512), (COL_VB, 512),
           (COL_LR, 128))


def _in_bwd(pieces, w_new, x, dxo, g_pre, scale, comm=None, ts=256):
    s_len = x.shape[0]
    nc = len(comm[1]) if comm else 0
    nco = len(_comm_out_shapes(*comm)) if comm else 0
    npc = len(_PIECES)

    def body(*refs):
        p_refs = refs[:npc]
        w_ref, x_ref, dxo_ref, g_ref, sc_ref = refs[npc:npc + 5]
        cin, (dx_ref, sums_ref), cout = (refs[npc + 5:npc + 5 + nc], refs[npc + 5 + nc:npc + 7 + nc],
                                         refs[npc + 7 + nc:npc + 7 + nc + nco])
        comm_before, comm_after = _comm_hooks(comm, cin, cout, refs[npc + 7 + nc + nco:], steps=s_len // ts)
        comm_before()

        @pl.when(pl.program_id(0) == 0)
        def _():
            sums_ref[...] = jnp.zeros_like(sums_ref)

        dh = jnp.zeros((ts, D_MODEL), F32)
        for p_ref, (col, width) in zip(p_refs, _PIECES):
            dh += _dot_nt(p_ref[...], w_ref[:, col:col + width])
        xv = x_ref[...]
        rstd = lax.rsqrt(jnp.mean(xv * xv, axis=-1, keepdims=True) + EPS)
        xn = xv * rstd
        sums_ref[0:1, :] += jnp.sum(dh, axis=0, keepdims=True)
        sums_ref[1:2, :] += jnp.sum(dh * (xn * g_ref[...]), axis=0, keepdims=True)
        dr = dh * (1.0 + sc_ref[...])
        sums_ref[2:3, :] += jnp.sum(dr * xn, axis=0, keepdims=True)
        dxn = dr * g_ref[...]
        dx_ref[...] = dxo_ref[...] + rstd * (dxn - xn * jnp.mean(dxn * xn, axis=-1, keepdims=True))
        comm_after()

    (g_pre, g_spec), (scale, sc_spec) = _rowvec(g_pre), _rowvec(scale)
    tile = pl.BlockSpec((ts, D_MODEL), lambda i: (i, 0))
    return pl.pallas_call(
        body, name="in_bwd_comm" if comm else "in_bwd", grid=(s_len // ts,),
        out_shape=[jax.ShapeDtypeStruct((s_len, D_MODEL), F32), jax.ShapeDtypeStruct((8, D_MODEL), F32)]
        + (_comm_out_shapes(*comm) if comm else []),
        in_specs=[pl.BlockSpec((ts, width), lambda i: (i, 0)) for _, width in _PIECES]
        + [pl.BlockSpec((D_MODEL, NP), lambda i: (0, 0)), tile, tile, g_spec, sc_spec] + [ANY] * nc,
        out_specs=[tile, pl.BlockSpec((8, D_MODEL), lambda i: (0, 0))] + [ANY] * nco,
        scratch_shapes=_comm_scratch(nc) if comm else [],
        compiler_params=_params(("arbitrary",), 56),
    )(*pieces, w_new, x, dxo, g_pre, scale, *(comm[1] if comm else []))


def _w_in_to_kernel(gathered, tr=128):
    def body(g_ref, o_ref):
        cols = jnp.concatenate([g_ref[k].astype(F32) for k in range(N_DEV)], axis=1)
        pad = jnp.zeros((tr, LANE - GLA_LOWRANK), F32)
        o_ref[...] = jnp.concatenate(
            [cols[:, 1024:1536], cols[:, 3088:3600], cols[:, 0:512], cols[:, 1552:2576], cols[:, 512:1024],
             cols[:, 2576:3088], cols[:, 1536:1552], pad], axis=1).astype(BF16)

    return pl.pallas_call(
        body, name="w_in_to_kernel", grid=(D_MODEL // tr,), out_shape=jax.ShapeDtypeStruct((D_MODEL, NP), BF16),
        in_specs=[pl.BlockSpec((N_DEV, tr, W_IN_SHARD), lambda i: (0, i, 0))],
        out_specs=pl.BlockSpec((tr, NP), lambda i: (i, 0)),
        compiler_params=_params(("arbitrary",)),
    )(gathered)


def _grad_w_in(h, pieces, ts=512, tr=128):
    s_len = h.shape[0]
    steps = s_len // ts

    def body(*refs):
        h_ref, p_refs = refs[0], refs[1:1 + len(_PIECES)]
        o_ref, acc = refs[1 + len(_PIECES):]

        @pl.when(pl.program_id(0) == 0)
        def _():
            acc[...] = jnp.zeros_like(acc)

        hv = h_ref[...]
        for p_ref, (col, width) in zip(p_refs, _PIECES):
            acc[:, col:col + width] += _dot_tn(hv, p_ref[...])

        @pl.when(pl.program_id(0) == steps - 1)
        def _():
            def rows_out(t, carry):
                rows = pl.ds(pl.multiple_of(t * tr, tr), tr)
                g = acc[rows, :]
                cols = jnp.concatenate(
                    [g[:, COL_QA:COL_QB], g[:, COL_VA:COL_VB], g[:, 0:512], g[:, COL_LR:COL_LR + GLA_LOWRANK],
                     g[:, COL_QB:COL_VA], g[:, COL_VB:COL_LR], g[:, 512:1024]], axis=1)
                for k in range(N_DEV):
                    o_ref[k, rows, :] = cols[:, W_IN_SHARD * k:W_IN_SHARD * (k + 1)].astype(BF16)
                return carry

            lax.fori_loop(0, D_MODEL // tr, rows_out, 0)

    return pl.pallas_call(
        body, name="grad_w_in", grid=(steps,),
        out_shape=jax.ShapeDtypeStruct((N_DEV, D_MODEL, W_IN_SHARD), BF16),
        in_specs=[pl.BlockSpec((ts, D_MODEL), lambda i: (i, 0))]
        + [pl.BlockSpec((ts, width), lambda i: (i, 0)) for _, width in _PIECES],
        out_specs=pl.BlockSpec((N_DEV, D_MODEL, W_IN_SHARD), lambda i: (0, 0, 0)),
        scratch_shapes=[pltpu.VMEM((D_MODEL, NP), F32)],
        compiler_params=_params(("arbitrary",), 56),
    )(h, *pieces)


def _matmul_tn(a, b, name, bn, ts=512):
    s_len, m = a.shape
    n = b.shape[1]
    steps = s_len // ts

    def body(a_ref, b_ref, o_ref, acc):
        @pl.when(pl.program_id(1) == 0)
        def _():
            acc[...] = jnp.zeros_like(acc)

        acc[...] += _dot_tn(a_ref[...], b_ref[...])

        @pl.when(pl.program_id(1) == steps - 1)
        def _():
            o_ref[...] = acc[...].astype(BF16)

    return pl.pallas_call(
        body, name=name, grid=(n // bn, steps),
        out_shape=jax.ShapeDtypeStruct((m, n), BF16),
        in_specs=[pl.BlockSpec((ts, m), lambda j, i: (i, 0)), pl.BlockSpec((ts, bn), lambda j, i: (i, j))],
        out_specs=pl.BlockSpec((m, bn), lambda j, i: (0, j)),
        scratch_shapes=[pltpu.VMEM((m, bn), F32)],
        compiler_params=_params(("arbitrary", "arbitrary"), 40),
    )(a, b)


def _adam_math(w, g, m, v):
    m = ADAM_B1 * m + (1.0 - ADAM_B1) * g
    v = ADAM_B2 * v + (1.0 - ADAM_B2) * (g * g)
    m_hat = m / (1.0 - ADAM_B1 ** ADAM_STEP)
    v_hat = v / (1.0 - ADAM_B2 ** ADAM_STEP)
    delta = -ADAM_LR * (m_hat / (jnp.sqrt(v_hat) + ADAM_EPS) + ADAM_WD * w)
    return delta, m, v


def _adamw(w, parts, m, v, name, tr):
    r, cdim = w.shape
    n_parts = parts.shape[0]

    def body(w_ref, p_ref, m_ref, v_ref, g_ref, d_ref, nm_ref, nv_ref):
        g = p_ref[0].astype(F32)
        for k in range(1, n_parts):
            g = g + p_ref[k].astype(F32)
        g_ref[...] = g
        d_ref[...], nm_ref[...], nv_ref[...] = _adam_math(w_ref[...], g, m_ref[...], v_ref[...])

    tile = pl.BlockSpec((tr, cdim), lambda i: (i, 0))
    shp = jax.ShapeDtypeStruct((r, cdim), F32)
    return pl.pallas_call(
        body, name=name, grid=(r // tr,), out_shape=(shp, shp, shp, shp),
        in_specs=[tile, pl.BlockSpec((n_parts, tr, cdim), lambda i: (0, i, 0)), tile, tile],
        out_specs=(tile, tile, tile, tile),
        compiler_params=_params(("arbitrary",), 40),
    )(w, parts, m, v)


def _adamw_layers(w, parts, m, v, name, tr):
    n_layers, r, cdim = w.shape

    def body(*refs):
        w_ref, p_refs, (m_ref, v_ref) = refs[0], refs[1:1 + n_layers], refs[1 + n_layers:3 + n_layers]
        g_ref, d_ref, nm_ref, nv_ref = refs[3 + n_layers:]
        for l, p_ref in enumerate(p_refs):
            @pl.when(pl.program_id(0) == l)
            def _(p_ref=p_ref):
                g = p_ref[0].astype(F32)
                for k in range(1, p_ref.shape[0]):
                    g = g + p_ref[k].astype(F32)
                g_ref[0] = g
                d_ref[0], nm_ref[0], nv_ref[0] = _adam_math(w_ref[0], g, m_ref[0], v_ref[0])

    tile = pl.BlockSpec((1, tr, cdim), lambda l, i: (l, i, 0))
    part = lambda own: pl.BlockSpec((parts[own].shape[0], tr, cdim), lambda l, i: (0, jnp.where(l == own, i, 0), 0))
    shp = jax.ShapeDtypeStruct(w.shape, F32)
    return pl.pallas_call(
        body, name=name, grid=(n_layers, r // tr), out_shape=(shp, shp, shp, shp),
        in_specs=[tile] + [part(l) for l in range(n_layers)] + [tile, tile],
        out_specs=(tile, tile, tile, tile),
        compiler_params=_params(("arbitrary", "arbitrary"), 40),
    )(w, *parts, m, v)


def _row(vec, width):
    vec = vec.reshape(1, -1)
    return jnp.pad(vec, ((0, 0), (0, width - vec.shape[1])))


def kernel(x, c, w_ada, b_ada, g_pre, w_in, w_gate_up, b_gate_up, g_gla, g_dil, w_out, g_post, loss_target, m_w_ada, m_b_ada, m_g_pre, m_w_in, m_w_gate_up, m_b_gate_up, m_g_gla, m_g_dil, m_w_out, m_g_post, v_w_ada, v_b_ada, v_g_pre, v_w_in, v_w_gate_up, v_b_gate_up, v_g_gla, v_g_dil, v_w_out, v_g_post):
    px, py, pc = _my_position()
    me = _linear(px, py, pc)
    xs = x[0]
    target = loss_target[0]
    s_len = xs.shape[0]
    assert s_len % (DIL_BLOCK * max(DIL_DILATIONS) * 2) == 0 and xs.shape[1] == D_MODEL

    c_all = _all_gather(jnp.pad(c, ((0, 7), (0, 0))), "gather_c").reshape(N_DEV, 8, D_MODEL)[:, 0]
    mod_part = _mod_fwd(c_all, w_ada)
    w_in_b, w_out_b = w_in.astype(BF16), w_out.astype(BF16)
    mod_all, wgu_all, w_in_all = _comm_call(
        "gather", [mod_part.reshape(DEPTH * N_DEV, ADA_SHARD), w_gate_up.reshape(DEPTH * GLA_LOWRANK, GU_SHARD),
                   w_in_b[0]], "gather_first")
    mod_all = mod_all.reshape(N_DEV, DEPTH, N_DEV, ADA_SHARD)
    mod_mine = lax.dynamic_index_in_dim(mod_all, me, axis=2, keepdims=False)
    mod = jnp.transpose(mod_mine, (1, 0, 2)).reshape(DEPTH, 3 * D_MODEL) + b_ada
    wgu_full = jnp.transpose(wgu_all.reshape(N_DEV, DEPTH, GLA_LOWRANK, GU_SHARD), (1, 2, 0, 3)).reshape(
        DEPTH, GLA_LOWRANK, GU_COLS)
    wgu_pad = jnp.pad(wgu_full, ((0, 0), (0, LANE - GLA_LOWRANK), (0, 0))).astype(BF16)

    def kernel_w_in(gathered):
        return _w_in_to_kernel(gathered.reshape(N_DEV, D_MODEL, W_IN_SHARD))

    cos, sin_signed = _rope_tables(s_len)
    g_heads = jnp.concatenate([g_gla, g_dil], axis=1)

    saved = []
    xl = xs
    for l in range(DEPTH):
        shift, scale, gate = ((mod, l, k) for k in range(3))
        w_new = kernel_w_in(w_in_all)
        pf, pb, h, w_out_l = _prenorm_proj(xl, (g_pre, l, 0), scale, shift, w_new, comm=("gather", [w_out_b[l]]))
        o_a, states = _gla_fwd(pf, pb, wgu_pad, b_gate_up, l)
        if l + 1 < DEPTH:
            o_b, lse, w_in_all = _dil_fwd(pf, pb, cos, sin_signed, comm=("gather", [w_in_b[l + 1]]))
        else:
            o_b, lse = _dil_fwd(pf, pb, cos, sin_signed)
        x_next, y, u = _post_fwd(o_a, o_b, pf, (g_heads, l, 0), w_out_l, xl, gate, (g_post, l, 0))
        saved.append((xl, scale, gate, w_new, w_out_l, pf, pb, h, o_a, states, o_b, lse, y, u))
        xl = x_next

    dx, loss_part = _loss_grad(xl, target)

    small_rows = []
    gin_slots, gin_parts, gout_parts = None, [None] * DEPTH, [None] * DEPTH
    for l in reversed(range(DEPTH)):
        x_in, scale, gate, w_new, w_out_l, pf, pb, h, o_a, states, o_b, lse, y, u = saved[l]
        du, do, dz, sums_post = _post_bwd(dx, u, gate, (g_post, l, 0), w_out_l, o_a, o_b, pf, (g_heads, l, 0))
        gout_slots = _matmul_tn(y, du, "grad_w_out", 512)
        dq_a, dk_a, dv_a, dlr2, dwgu, dbgu, arrived = _gla_bwd(pf, pb, wgu_pad, b_gate_up, l, states, do,
                                                               comm=("exchange", [gout_slots]))
        gout_parts[l] = arrived.reshape(N_DEV, OUT_SHARD, D_MODEL)
        if gin_slots is not None:
            dq_b, dk_b, dv_b, arrived, _, _ = _dil_bwd(pf, pb, cos, sin_signed, do, o_b, lse,
                                                       comm=("pairsum_exchange", [gin_slots]))
            gin_parts[l + 1] = arrived.reshape(N_DEV // 2, D_MODEL, W_IN_SHARD)
        else:
            dq_b, dk_b, dv_b = _dil_bwd(pf, pb, cos, sin_signed, do, o_b, lse)
        dlr = (dlr2[0] + dlr2[1]).astype(BF16)
        pieces = (dz, dq_a, dk_a, dq_b, dk_b, dv_a, dv_b, dlr)
        gin_slots = _grad_w_in(h, pieces).reshape(N_DEV * D_MODEL, W_IN_SHARD)
        if l == 0:
            dx, sums_in, arrived, _, _ = _in_bwd(pieces, w_new, x_in, dx, (g_pre, l, 0), scale,
                                                 comm=("pairsum_exchange", [gin_slots]))
            gin_parts[0] = arrived.reshape(N_DEV // 2, D_MODEL, W_IN_SHARD)
        else:
            dx, sums_in = _in_bwd(pieces, w_new, x_in, dx, (g_pre, l, 0), scale)
        dmod = jnp.concatenate([sums_in[0], sums_in[1], sums_post[0]])
        vecs = jnp.concatenate([sums_in[2], sums_post[1], sums_post[2], dbgu[0]])
        small_rows[0:0] = [_row(dmod, 4096), _row(vecs, 4096), _row(dwgu[:GLA_LOWRANK], 4096)]
    grad_x = dx[None]

    flat = lambda a, rows: a.reshape(rows, a.shape[-1])
    r_ada = DEPTH * D_MODEL
    g_w_in, d_w_in, nm_w_in, nv_w_in = _adamw_layers(w_in, gin_parts, m_w_in, v_w_in, "adamw_w_in", 256)
    g_w_out, d_w_out, nm_w_out, nv_w_out = _adamw_layers(w_out, gout_parts, m_w_out, v_w_out, "adamw_w_out", 128)

    small_rows += [_row(loss_part[0, 0:1], 4096), jnp.zeros((1, 4096), F32)]
    small = _all_gather(jnp.concatenate(small_rows, axis=0), "gather_small").reshape(N_DEV, 8, 4096)
    dmod_all = jnp.stack([small[:, 0, :3 * D_MODEL], small[:, 3, :3 * D_MODEL]])
    dmod_cols = lax.dynamic_slice_in_dim(dmod_all, me * ADA_SHARD, ADA_SHARD, axis=2)
    gwa = _w_ada_grad(c_all, dmod_cols).reshape(1, r_ada, ADA_SHARD)
    g_w_ada, d_w_ada, nm_w_ada, nv_w_ada = (
        t.reshape(w_ada.shape) for t in _adamw(flat(w_ada, r_ada), gwa, flat(m_w_ada, r_ada), flat(v_w_ada, r_ada),
                                               "adamw_w_ada", 256))

    def small_param(w, m, v, cols, row, name):
        n = w.shape[1]
        parts = jnp.stack([small[:, row, cols:cols + n], small[:, row + 3, cols:cols + n]], axis=1)
        return _adamw(w, parts, m, v, name, DEPTH)

    g_b_ada, d_b_ada, nm_b_ada, nv_b_ada = small_param(b_ada, m_b_ada, v_b_ada, 0, 0, "adamw_b_ada")
    g_g_pre, d_g_pre, nm_g_pre, nv_g_pre = small_param(g_pre, m_g_pre, v_g_pre, 0, 1, "adamw_g_pre")
    g_g_post, d_g_post, nm_g_post, nv_g_post = small_param(g_post, m_g_post, v_g_post, 1024, 1, "adamw_g_post")
    g_g_gla, d_g_gla, nm_g_gla, nv_g_gla = small_param(g_gla, m_g_gla, v_g_gla, 2048, 1, "adamw_g_gla")
    g_g_dil, d_g_dil, nm_g_dil, nv_g_dil = small_param(g_dil, m_g_dil, v_g_dil, 2560, 1, "adamw_g_dil")
    g_b_gu, d_b_gu, nm_b_gu, nv_b_gu = small_param(b_gate_up, m_b_gate_up, v_b_gate_up, 3072, 1, "adamw_b_gate_up")
    gu_parts = jnp.stack([small[:, 2], small[:, 5]], axis=1).reshape(N_DEV, DEPTH, GLA_LOWRANK, GU_COLS)
    gu_parts = lax.dynamic_slice_in_dim(gu_parts, me * GU_SHARD, GU_SHARD, axis=3).reshape(
        N_DEV, DEPTH * GLA_LOWRANK, GU_SHARD)
    r_gu = DEPTH * GLA_LOWRANK
    g_w_gu, d_w_gu, nm_w_gu, nv_w_gu = (
        t.reshape(w_gate_up.shape) for t in _adamw(flat(w_gate_up, r_gu), gu_parts, flat(m_w_gate_up, r_gu),
                                                   flat(v_w_gate_up, r_gu), "adamw_w_gate_up", r_gu))
    loss_parts = jnp.broadcast_to(small[:, 6, 0:1].reshape(N_DEV, 1, 1), (N_DEV, 8, LANE))
    loss = _sum_parts(loss_parts)[0, 0]

    return (loss, grad_x,
            g_w_ada, g_b_ada, g_g_pre, g_w_in, g_w_gu, g_b_gu, g_g_gla, g_g_dil, g_w_out, g_g_post,
            d_w_ada, d_b_ada, d_g_pre, d_w_in, d_w_gu, d_b_gu, d_g_gla, d_g_dil, d_w_out, d_g_post,
            nm_w_ada, nm_b_ada, nm_g_pre, nm_w_in, nm_w_gu, nm_b_gu, nm_g_gla, nm_g_dil, nm_w_out, nm_g_post,
            nv_w_ada, nv_b_ada, nv_g_pre, nv_w_in, nv_w_gu, nv_b_gu, nv_g_gla, nv_g_dil, nv_w_out, nv_g_post)


def _sum_parts(parts):
    n_parts = parts.shape[0]

    def body(p_ref, o_ref):
        acc = p_ref[0]
        for k in range(1, n_parts):
            acc = acc + p_ref[k]
        o_ref[...] = acc

    return pl.pallas_call(body, name="sum_loss", out_shape=jax.ShapeDtypeStruct(parts.shape[1:], F32))(parts)
```

```python
import functools
import math

import jax
import jax.numpy as jnp
from jax import lax
from jax.experimental import pallas as pl
from jax.experimental.pallas import tpu as pltpu

F32 = jnp.float32
BF16 = jnp.bfloat16

N_DEV = 8
D_MODEL = 1024
DEPTH = 2
GLA_HEADS = 4
GLA_DK = 64
GLA_DV = 128
GLA_CHUNK = 64
GLA_TAU = 16.0
GLA_LOWRANK = 16
DIL_HEADS = 4
DIL_HD = 128
DIL_BLOCK = 128
DIL_DILATIONS = (1, 4, 16)
ROPE_THETA = 10000.0
EPS = 1e-6
IN_COLS = 3600
W_IN_SHARD = IN_COLS // N_DEV
ADA_SHARD = 3 * D_MODEL // N_DEV
OUT_SHARD = D_MODEL // N_DEV
GU_COLS = GLA_HEADS * GLA_DK
GU_SHARD = GU_COLS // N_DEV

ADAM_LR = 0.001
ADAM_B1 = 0.9
ADAM_B2 = 0.999
ADAM_EPS = 1e-08
ADAM_WD = 0.01
ADAM_STEP = 10

NP = 3712
COL_Z, COL_QA, COL_KA, COL_QB, COL_KB, COL_VA, COL_VB, COL_LR = 0, 1024, 1280, 1536, 2048, 2560, 3072, 3584
NP_F32 = COL_VA
NP_BF16 = NP - NP_F32
LANE = 128
MASK_VALUE = -1e30

MESH = pl.DeviceIdType.MESH
ANY = pl.BlockSpec(memory_space=pl.ANY)


def _params(sem=None, vmem_mb=None):
    kw = {}
    if sem is not None:
        kw["dimension_semantics"] = sem
    if vmem_mb is not None:
        kw["vmem_limit_bytes"] = vmem_mb * 1024 * 1024
    return pltpu.CompilerParams(**kw)


def _dot(a, b):
    return jnp.dot(a, b, preferred_element_type=F32)


def _dot_nt(a, b):
    return lax.dot_general(a, b, (((1,), (1,)), ((), ())), preferred_element_type=F32)


def _dot_tn(a, b):
    return lax.dot_general(a, b, (((0,), (0,)), ((), ())), preferred_element_type=F32)


def _sigmoid(z):
    return 1.0 / (1.0 + jnp.exp(-z))


def _log_sigmoid(z):
    return jnp.minimum(z, 0.0) - jnp.log(1.0 + jnp.exp(-jnp.abs(z)))


def _rowvec(v, width=D_MODEL):
    arr, row, cb = v
    return arr.reshape(arr.shape[0], 1, arr.shape[1]), pl.BlockSpec((None, 1, width), lambda *_: (row, 0, cb))


def _my_position():
    return lax.axis_index("x"), lax.axis_index("y"), lax.axis_index("c")


def _linear(px, py, pc):
    return 4 * px + 2 * py + pc


def _gather_phase(phase, x_ref, out_ref, send_sem, recv_sem, local_sem):
    m = x_ref.shape[0]
    x, y, c = _my_position()
    me, sibling = (x, y, c), (x, y, 1 - c)
    chips = [(1 - x, y), (x, 1 - y), (1 - x, 1 - y)]

    def rows(px, py, pc):
        return out_ref.at[pl.ds(_linear(px, py, pc) * m, m), :]

    def copy(k, block, to, src=None):
        return pltpu.make_async_remote_copy(
            src_ref=rows(*block) if src is None else src, dst_ref=rows(*block),
            send_sem=send_sem(k), recv_sem=recv_sem(k), device_id=to, device_id_type=MESH)

    mine = pltpu.make_async_copy(x_ref, rows(*me), local_sem)
    first = [copy(0, me, sibling, src=x_ref)] + [copy(1 + j, me, (*chip, c), src=x_ref) for j, chip in enumerate(chips)]
    passed = [copy(4 + j, (*chip, c), sibling) for j, chip in enumerate(chips)]
    if phase == "start":
        mine.start()
        for cp in first:
            cp.start()
    elif phase == "forward":
        for j, chip in enumerate(chips):
            copy(1 + j, (*chip, c), me).wait_recv()
            passed[j].start()
    else:
        copy(0, sibling, me).wait_recv()
        for j, chip in enumerate(chips):
            copy(4 + j, (*chip, 1 - c), me).wait_recv()
        for cp in first + passed:
            cp.wait_send()
        mine.wait()


def _exchange_phase(phase, x_ref, out_ref, send_sem, recv_sem, local_sem):
    m = x_ref.shape[0] // N_DEV
    x, y, c = _my_position()
    me = _linear(x, y, c)

    def rows(ref, idx):
        return ref.at[pl.ds(idx * m, m), :]

    peers = [(1 - x if j & 4 else x, 1 - y if j & 2 else y, 1 - c if j & 1 else c) for j in range(1, N_DEV)]
    local = pltpu.make_async_copy(rows(x_ref, me), rows(out_ref, me), local_sem)
    sends = [pltpu.make_async_remote_copy(
        src_ref=rows(x_ref, _linear(*peer)), dst_ref=rows(out_ref, me),
        send_sem=send_sem(j), recv_sem=recv_sem(j), device_id=peer, device_id_type=MESH) for j, peer in enumerate(peers)]
    if phase == "start":
        local.start()
        for cp in sends:
            cp.start()
    else:
        for j, peer in enumerate(peers):
            pltpu.make_async_remote_copy(
                src_ref=rows(x_ref, _linear(*peer)), dst_ref=rows(out_ref, _linear(*peer)),
                send_sem=send_sem(j), recv_sem=recv_sem(j), device_id=peer, device_id_type=MESH).wait_recv()
        for cp in sends:
            cp.wait_send()
        local.wait()


def _pairsum_exchange_phase(phase, x_ref, out_refs, send_sem, recv_sem, local_sem):
    out_ref, stage_ref, pair_ref = out_refs
    m, n = x_ref.shape[0] // N_DEV, x_ref.shape[1]
    x, y, c = _my_position()
    mine = 2 * x + y
    chips = [(qx, qy) for qx in range(2) for qy in range(2)]
    others = [(1 - x, y), (x, 1 - y), (1 - x, 1 - y)]

    def rows(ref, idx):
        return ref.at[pl.ds(idx * m, m), :]

    def remote(src, dst, k, to):
        return pltpu.make_async_remote_copy(src_ref=src, dst_ref=dst, send_sem=send_sem(k), recv_sem=recv_sem(k),
                                            device_id=to, device_id_type=MESH)

    to_sibling = [remote(rows(x_ref, _linear(qx, qy, 1 - c)), rows(stage_ref, q), q, (x, y, 1 - c))
                  for q, (qx, qy) in enumerate(chips)]
    to_chips = [remote(rows(pair_ref, 2 * qx + qy), rows(out_ref, mine), 4 + j, (qx, qy, c))
                for j, (qx, qy) in enumerate(others)]
    keep = pltpu.make_async_copy(rows(pair_ref, mine), rows(out_ref, mine), local_sem)
    if phase == "start":
        for cp in to_sibling:
            cp.start()
    elif phase == "reduce":
        for cp in to_sibling:
            cp.wait_recv()

        def through_vmem(a_buf, b_buf, sems):
            tr = 128
            loads = [(pltpu.make_async_copy(rows(x_ref, _linear(qx, qy, c)), a_buf.at[q % 2], sems.at[q % 2]),
                      pltpu.make_async_copy(rows(stage_ref, q), b_buf.at[q % 2], sems.at[2 + q % 2]))
                     for q, (qx, qy) in enumerate(chips)]
            stores = [pltpu.make_async_copy(a_buf.at[q % 2], rows(pair_ref, q), sems.at[4 + q % 2]) for q in range(4)]
            for q in range(4):
                if q >= 2:
                    stores[q - 2].wait()
                for cp in loads[q]:
                    cp.start()
                for cp in loads[q]:
                    cp.wait()

                def add(r, carry, q=q):
                    tile = pl.ds(pl.multiple_of(r * tr, tr), tr)
                    a_buf[q % 2, tile, :] = (a_buf[q % 2, tile, :].astype(F32)
                                             + b_buf[q % 2, tile, :].astype(F32)).astype(x_ref.dtype)
                    return carry

                lax.fori_loop(0, m // tr, add, 0)
                stores[q].start()
            stores[2].wait()
            stores[3].wait()

        pl.run_scoped(through_vmem, pltpu.VMEM((2, m, n), x_ref.dtype), pltpu.VMEM((2, m, n), x_ref.dtype),
                      pltpu.SemaphoreType.DMA((6,)))
    elif phase == "send":
        keep.start()
        for cp in to_chips:
            cp.start()
    else:
        for j, (qx, qy) in enumerate(others):
            remote(rows(pair_ref, mine), rows(out_ref, 2 * qx + qy), 4 + j, (qx, qy, c)).wait_recv()
        for cp in to_sibling + to_chips:
            cp.wait_send()
        keep.wait()


_COMM_PHASES = {"gather": (_gather_phase, ("start", "forward", "finish")),
                "exchange": (_exchange_phase, ("start", "finish")),
                "pairsum_exchange": (_pairsum_exchange_phase, ("start", "reduce", "send", "finish"))}


def _comm_scratch(n_arrays):
    return [pltpu.SemaphoreType.DMA((n_arrays, 7)), pltpu.SemaphoreType.DMA((n_arrays, 7)),
            pltpu.SemaphoreType.DMA((n_arrays,))]


def _comm_run(kind, phases, x_refs, out_refs, send_sems, recv_sems, local_sems):
    fn = _COMM_PHASES[kind][0]
    per = len(out_refs) // len(x_refs)
    for phase in phases:
        for a, x_ref in enumerate(x_refs):
            outs = out_refs[a] if per == 1 else tuple(out_refs[per * a:per * (a + 1)])
            fn(phase, x_ref, outs, lambda k, a=a: send_sems.at[a, k], lambda k, a=a: recv_sems.at[a, k],
               local_sems.at[a])


def _comm_out_shapes(kind, arrays):
    if kind == "pairsum_exchange":
        return [jax.ShapeDtypeStruct((a.shape[0] // 2, a.shape[1]), a.dtype) for a in arrays for _ in range(3)]
    return [jax.ShapeDtypeStruct((N_DEV * a.shape[0], a.shape[1]) if kind == "gather" else a.shape, a.dtype)
            for a in arrays]


def _comm_call(kind, arrays, name):
    n = len(arrays)
    shapes = _comm_out_shapes(kind, arrays)

    def body(*refs):
        _comm_run(kind, _COMM_PHASES[kind][1], refs[:n], refs[n:n + len(shapes)], *refs[n + len(shapes):])

    return pl.pallas_call(body, name=name, out_shape=shapes, in_specs=[ANY] * n, out_specs=[ANY] * len(shapes),
                          scratch_shapes=_comm_scratch(n))(*arrays)


def _all_gather(xs, name):
    return _comm_call("gather", [xs], name)[0]


def _mod_fwd(c_all, w_ada):
    def body(c_ref, w_ref, o_ref):
        cv = c_ref[...]
        sc = cv * _sigmoid(cv)
        o_ref[0] = _dot(sc.astype(BF16), w_ref[0].astype(BF16))

    return pl.pallas_call(
        body, name="mod_fwd", grid=(DEPTH,),
        out_shape=jax.ShapeDtypeStruct((DEPTH, N_DEV, ADA_SHARD), F32),
        in_specs=[pl.BlockSpec((N_DEV, D_MODEL), lambda l: (0, 0)),
                  pl.BlockSpec((1, D_MODEL, ADA_SHARD), lambda l: (l, 0, 0))],
        out_specs=pl.BlockSpec((1, N_DEV, ADA_SHARD), lambda l: (l, 0, 0)),
        compiler_params=_params(("arbitrary",)),
    )(c_all, w_ada)


def _w_ada_grad(c_all, dmod_cols):
    def body(c_ref, d_ref, o_ref):
        cv = c_ref[...]
        sc = cv * _sigmoid(cv)
        o_ref[0] = lax.dot_general(sc, d_ref[0], (((0,), (0,)), ((), ())), precision=lax.Precision.HIGHEST,
                                   preferred_element_type=F32)

    return pl.pallas_call(
        body, name="w_ada_grad", grid=(DEPTH,),
        out_shape=jax.ShapeDtypeStruct((DEPTH, D_MODEL, ADA_SHARD), F32),
        in_specs=[pl.BlockSpec((N_DEV, D_MODEL), lambda l: (0, 0)),
                  pl.BlockSpec((1, N_DEV, ADA_SHARD), lambda l: (l, 0, 0))],
        out_specs=pl.BlockSpec((1, D_MODEL, ADA_SHARD), lambda l: (l, 0, 0)),
        compiler_params=_params(("arbitrary",)),
    )(c_all, dmod_cols)


def _comm_plumbing(comm):
    if not comm:
        return 0, [], []
    return len(comm[1]), _comm_out_shapes(*comm), _comm_scratch(len(comm[1]))


def _split_refs(refs, n_in, n_out, n_scratch, comm):
    ci, shapes, _ = _comm_plumbing(comm)
    co = len(shapes)
    a, b, c = n_in + ci, n_in + ci + n_out, n_in + ci + n_out + co
    return refs[:n_in], refs[a:b], refs[c:c + n_scratch], refs[n_in:a], refs[b:c], refs[c + n_scratch:]


def _prenorm_proj(x, g_pre, scale, shift, w_new, comm=None, ts=256):
    s_len = x.shape[0]
    n_cin, c_shapes, c_scratch = _comm_plumbing(comm)

    def body(*refs):
        (x_ref, g_ref, sc_ref, sh_ref, w_ref), (pf_ref, pb_ref, h_ref), _, cin, cout, csem = _split_refs(
            refs, 5, 3, 0, comm)
        comm_before, comm_after = _comm_hooks(comm, cin, cout, csem, steps=s_len // ts)
        comm_before()
        xv = x_ref[...]
        rstd = lax.rsqrt(jnp.mean(xv * xv, axis=-1, keepdims=True) + EPS)
        h = (xv * rstd * g_ref[...]) * (1.0 + sc_ref[...]) + sh_ref[...]
        hb = h.astype(BF16)
        h_ref[...] = hb
        for j in range(0, NP, 512):
            w = min(512, NP - j)
            acc = _dot(hb, w_ref[:, j:j + w])
            if j < NP_F32:
                pf_ref[:, j:j + w] = acc
            else:
                pb_ref[:, j - NP_F32:j - NP_F32 + w] = acc.astype(BF16)
        comm_after()

    (g_pre, g_spec), (scale, sc_spec), (shift, sh_spec) = _rowvec(g_pre), _rowvec(scale), _rowvec(shift)
    return pl.pallas_call(
        body, name="prenorm_proj_comm" if comm else "prenorm_proj", grid=(s_len // ts,),
        out_shape=[jax.ShapeDtypeStruct((s_len, NP_F32), F32), jax.ShapeDtypeStruct((s_len, NP_BF16), BF16),
                   jax.ShapeDtypeStruct((s_len, D_MODEL), BF16)] + c_shapes,
        in_specs=[pl.BlockSpec((ts, D_MODEL), lambda i: (i, 0)), g_spec, sc_spec, sh_spec,
                  pl.BlockSpec((D_MODEL, NP), lambda i: (0, 0))] + [ANY] * n_cin,
        out_specs=[pl.BlockSpec((ts, NP_F32), lambda i: (i, 0)), pl.BlockSpec((ts, NP_BF16), lambda i: (i, 0)),
                   pl.BlockSpec((ts, D_MODEL), lambda i: (i, 0))] + [ANY] * len(c_shapes),
        scratch_shapes=c_scratch,
        compiler_params=_params(("arbitrary",), 48),
    )(x, g_pre, scale, shift, w_new, *(comm[1] if comm else []))


GLA_GROUP = 8


def _gla_group_rows(t):
    return [pl.ds(pl.multiple_of((t * GLA_GROUP + j) * GLA_CHUNK, GLA_CHUNK), GLA_CHUNK) for j in range(GLA_GROUP)]


def _gla_chunks_common(q_ref, k_ref, lr_ref, wgu_ref, bgu_ref, rows_list):
    c = GLA_CHUNK
    ri = lax.broadcasted_iota(jnp.int32, (c, c), 0)
    ci = lax.broadcasted_iota(jnp.int32, (c, c), 1)
    tril = (ri >= ci).astype(F32)
    zs = [_dot(lr_ref[rows, :], wgu_ref[...]) + bgu_ref[...] for rows in rows_list]
    las = [_log_sigmoid(z) * (1.0 / GLA_TAU) for z in zs]
    bs = [jnp.dot(tril, la, precision=lax.Precision.HIGHEST, preferred_element_type=F32) for la in las]
    out = []
    for rows, z, b in zip(rows_list, zs, bs):
        q = q_ref[rows, :] * (GLA_DK ** -0.5)
        k = k_ref[rows, :]
        bl = b[c - 1:c, :]
        out.append(dict(z=z, b=b, bl=bl, qe=q * jnp.exp(b), ke=k * jnp.exp(-b), kend=k * jnp.exp(bl - b),
                        dec=jnp.exp(bl)))
    return out, ri, ci


def _head_lane_mask(hh):
    return (lax.broadcasted_iota(jnp.int32, (1, LANE), 1) // GLA_DK) == hh


def _state_block_mask():
    r = lax.broadcasted_iota(jnp.int32, (2 * GLA_DV, LANE), 0) // GLA_DV
    cc = lax.broadcasted_iota(jnp.int32, (2 * GLA_DV, LANE), 1) // GLA_DK
    return r == cc


def _gla_fwd(pf, pb, wgu, bgu, layer, comm=None):
    s_len = pf.shape[0]
    nc = s_len // GLA_CHUNK
    ncomm = len(comm[1]) if comm else 0

    def body(*refs):
        q_ref, k_ref, v_ref, lr_ref, wgu_ref, bgu_ref = refs[:6]
        cin, (o_ref, st_ref), cout = refs[6:6 + ncomm], refs[6 + ncomm:8 + ncomm], refs[8 + ncomm:8 + 2 * ncomm]
        qe_s, cs_s, dec_s = refs[8 + 2 * ncomm:11 + 2 * ncomm]
        comm_before, comm_after = _comm_hooks(comm, cin, cout, refs[11 + 2 * ncomm:], steps=2)
        comm_before()
        bd = _state_block_mask()

        def local(t, carry):
            rows_list = _gla_group_rows(t)
            cm, ri, ci = _gla_chunks_common(q_ref, k_ref, lr_ref, wgu_ref, bgu_ref, rows_list)
            vs = [v_ref[rows, :] for rows in rows_list]
            kebs = [c["ke"].astype(BF16) for c in cm]
            a = [[jnp.where(ri >= ci, _dot_nt(jnp.where(_head_lane_mask(hh), c["qe"], 0.0).astype(BF16), keb), 0.0)
                  .astype(BF16) for hh in range(2)] for c, keb in zip(cm, kebs)]
            oi = [[_dot(ah[hh], v[:, hh * GLA_DV:(hh + 1) * GLA_DV]) for hh in range(2)] for ah, v in zip(a, vs)]
            cs = [jnp.where(bd, _dot_tn(v, c["kend"].astype(BF16)), 0.0) for c, v in zip(cm, vs)]
            for j, (rows, c) in enumerate(zip(rows_list, cm)):
                n = t * GLA_GROUP + j
                o_ref[rows, :] = jnp.concatenate(oi[j], axis=1)
                qe_s[rows, :] = c["qe"].astype(BF16)
                cs_s[n] = cs[j]
                dec_s[n] = jnp.broadcast_to(c["dec"], (8, LANE))
            return carry

        lax.fori_loop(0, nc // GLA_GROUP, local, 0)

        def scan(n, st):
            st_ref[0, n] = st.astype(BF16)
            return dec_s[n][0:1, :] * st + cs_s[n]

        lax.fori_loop(0, nc, scan, jnp.zeros((2 * GLA_DV, LANE), F32))

        def inter(t, carry):
            rows_list = _gla_group_rows(t)
            add = [_dot_nt(qe_s[rows, :], st_ref[0, t * GLA_GROUP + j]) for j, rows in enumerate(rows_list)]
            for rows, av in zip(rows_list, add):
                o_ref[rows, :] = o_ref[rows, :] + av
            return carry

        lax.fori_loop(0, nc // GLA_GROUP, inter, 0)
        comm_after()

    return pl.pallas_call(
        body, name="gla_fwd_comm" if comm else "gla_fwd", grid=(2,),
        out_shape=[jax.ShapeDtypeStruct((s_len, GLA_HEADS * GLA_DV), F32),
                   jax.ShapeDtypeStruct((2, nc, 2 * GLA_DV, LANE), BF16)] + (_comm_out_shapes(*comm) if comm else []),
        in_specs=[pl.BlockSpec((s_len, LANE), lambda g: (0, COL_QA // LANE + g)),
                  pl.BlockSpec((s_len, LANE), lambda g: (0, COL_KA // LANE + g)),
                  pl.BlockSpec((s_len, 2 * GLA_DV), lambda g: (0, (COL_VA - NP_F32) // (2 * GLA_DV) + g)),
                  pl.BlockSpec((s_len, LANE), lambda g: (0, (COL_LR - NP_F32) // LANE)),
                  pl.BlockSpec((None, LANE, LANE), lambda g: (layer, 0, g)),
                  pl.BlockSpec((None, 1, LANE), lambda g: (layer, 0, g))] + [ANY] * ncomm,
        out_specs=[pl.BlockSpec((s_len, 2 * GLA_DV), lambda g: (0, g)),
                   pl.BlockSpec((1, nc, 2 * GLA_DV, LANE), lambda g: (g, 0, 0, 0))] + [ANY] * ncomm,
        scratch_shapes=[pltpu.VMEM((s_len, LANE), BF16), pltpu.VMEM((nc, 2 * GLA_DV, LANE), F32),
                        pltpu.VMEM((nc, 8, LANE), F32)] + (_comm_scratch(ncomm) if comm else []),
        compiler_params=_params(("arbitrary",), 56),
    )(pf, pf, pb, pb, wgu, bgu.reshape(bgu.shape[0], 1, GU_COLS), *(comm[1] if comm else []))


def _rope_tables(s_len):
    inv_freq = ROPE_THETA ** (-jnp.arange(0, DIL_HD, 2, dtype=F32) / DIL_HD)
    ang = jnp.arange(s_len, dtype=F32)[:, None] * inv_freq[None, :]
    cos, sin = jnp.cos(ang), jnp.sin(ang)
    return jnp.concatenate([cos, cos], axis=1), jnp.concatenate([-sin, sin], axis=1)


def _rope(xv, cos, sin_signed):
    return xv * cos + pltpu.roll(xv, DIL_HD // 2, 1) * sin_signed


DIL_GROUP = 8


def _dil_pair_block(i, half, d, nblk, group=DIL_GROUP):
    nb = nblk // d
    j = i + half * (nblk // group)
    if nb >= 2 * group:
        r, n = j % d, j // d
    else:
        r, n = j // nb, j % nb
    kb = jnp.maximum(n - 1, 0)
    qs = r + d * DIL_BLOCK * n
    ks = r + d * DIL_BLOCK * kb
    return qs, ks, jnp.minimum(n, 1)


def _dil_fill_bias(bias):
    qi = lax.broadcasted_iota(jnp.int32, (DIL_BLOCK, 2 * DIL_BLOCK), 0)
    kj = lax.broadcasted_iota(jnp.int32, (DIL_BLOCK, 2 * DIL_BLOCK), 1)
    for sel in range(2):
        dist = qi - kj + DIL_BLOCK * sel
        bias[sel] = jnp.where((dist >= 0) & (dist <= DIL_BLOCK), 0.0, MASK_VALUE)


def _strided(start, size, d):
    return pl.ds(start, size) if d == 1 else pl.ds(start, size, stride=d)


def _comm_hooks(comm, cin, cout, csem, steps=DIL_HEADS):
    def before():
        if comm:
            @pl.when(pl.program_id(0) == 0)
            def _():
                _comm_run(comm[0], ("start",), cin, cout, *csem)

            if comm[0] == "gather":
                @pl.when(pl.program_id(0) == steps - 1)
                def _():
                    _comm_run(comm[0], ("forward",), cin, cout, *csem)

            if comm[0] == "pairsum_exchange":
                @pl.when(pl.program_id(0) == (1 if steps <= 4 else 3))
                def _():
                    _comm_run(comm[0], ("reduce", "send"), cin, cout, *csem)

    def after():
        if comm:
            @pl.when(pl.program_id(0) == steps - 1)
            def _():
                _comm_run(comm[0], ("finish",), cin, cout, *csem)

    return before, after


def _dil_fwd(pf, pb, cos, sin_signed, comm=None):
    s_len = pf.shape[0]
    nblk = s_len // DIL_BLOCK
    prep_rows = 256
    scale = DIL_HD ** -0.5
    nc = len(comm[1]) if comm else 0

    def body(*refs):
        q_ref, k_ref, v_ref, cos_ref, sin_ref = refs[:5]
        cin, (o_ref, lse_ref), cout = refs[5:5 + nc], refs[5 + nc:7 + nc], refs[7 + nc:7 + 2 * nc]
        qf, kf, vf, o0, o1, o2, l0, l1, l2, bias = refs[7 + 2 * nc:17 + 2 * nc]
        comm_before, comm_after = _comm_hooks(comm, cin, cout, refs[17 + 2 * nc:])
        comm_before()
        _dil_fill_bias(bias)

        def prep(t, carry):
            rows = pl.ds(pl.multiple_of(t * prep_rows, prep_rows), prep_rows)
            cs, sn = cos_ref[rows, :], sin_ref[rows, :]
            qf[rows, :] = _rope(q_ref[rows, :], cs, sn)
            kf[rows, :] = _rope(k_ref[rows, :], cs, sn)
            vf[rows, :] = v_ref[rows, :].astype(F32)
            return carry

        lax.fori_loop(0, s_len // prep_rows, prep, 0)
        for d, o_p, l_p in zip(DIL_DILATIONS, (o0, o1, o2), (l0, l1, l2)):
            if nblk // d == 2:
                units = DIL_GROUP // 2

                def whole(i, carry, d=d, o_p=o_p, l_p=l_p, units=units):
                    rows = [_strided(i + u * (d // units), 2 * DIL_BLOCK, d) for u in range(units)]
                    ld = [(qf[rw, :].astype(BF16), kf[rw, :].astype(BF16), vf[rw, :].astype(BF16)) for rw in rows]
                    both = bias[...].reshape(2 * DIL_BLOCK, 2 * DIL_BLOCK)
                    s = [_dot_nt(qb, kk) * scale + both for qb, kk, _ in ld]
                    m = [jnp.max(sv, axis=-1, keepdims=True) for sv in s]
                    p = [jnp.exp(sv - mv) for sv, mv in zip(s, m)]
                    den = [jnp.sum(pv, axis=-1, keepdims=True) for pv in p]
                    r = [_dot(pv.astype(BF16), vv) for pv, (_, _, vv) in zip(p, ld)]
                    for rv, dv, mv, rw in zip(r, den, m, rows):
                        o_p[rw, :] = rv / dv
                        l_p[rw, :] = jnp.broadcast_to(mv + jnp.log(dv), (2 * DIL_BLOCK, DIL_HD))
                    return carry

                lax.fori_loop(0, d // units, whole, 0)
                continue

            def pair(i, carry, d=d, o_p=o_p, l_p=l_p):
                idx = [_dil_pair_block(i, half, d, nblk, DIL_GROUP) for half in range(DIL_GROUP)]
                ld = [(qf[_strided(qs, DIL_BLOCK, d), :].astype(BF16),
                       kf[_strided(ks, 2 * DIL_BLOCK, d), :].astype(BF16),
                       vf[_strided(ks, 2 * DIL_BLOCK, d), :].astype(BF16)) for qs, ks, _ in idx]
                s = [_dot_nt(qb, kk) * scale + bias[sel] for (qb, kk, _), (_, _, sel) in zip(ld, idx)]
                m = [jnp.max(sv, axis=-1, keepdims=True) for sv in s]
                p = [jnp.exp(sv - mv) for sv, mv in zip(s, m)]
                den = [jnp.sum(pv, axis=-1, keepdims=True) for pv in p]
                r = [_dot(pv.astype(BF16), vv) for pv, (_, _, vv) in zip(p, ld)]
                for rv, dv, mv, (qs, _, _) in zip(r, den, m, idx):
                    o_p[_strided(qs, DIL_BLOCK, d), :] = rv / dv
                    l_p[_strided(qs, DIL_BLOCK, d), :] = jnp.broadcast_to(mv + jnp.log(dv), (DIL_BLOCK, DIL_HD))
                return carry

            lax.fori_loop(0, nblk // DIL_GROUP, pair, 0)

        def comb(t, carry):
            rows = pl.ds(pl.multiple_of(t * prep_rows, prep_rows), prep_rows)
            a0, a1, a2 = l0[rows, :], l1[rows, :], l2[rows, :]
            m = jnp.maximum(jnp.maximum(a0, a1), a2)
            e0, e1, e2 = jnp.exp(a0 - m), jnp.exp(a1 - m), jnp.exp(a2 - m)
            tot = e0 + e1 + e2
            o_ref[rows, :] = (e0 * o0[rows, :] + e1 * o1[rows, :] + e2 * o2[rows, :]) / tot
            lse_ref[rows, :] = m + jnp.log(tot)
            return carry

        lax.fori_loop(0, s_len // prep_rows, comb, 0)
        comm_after()

    head = lambda base: pl.BlockSpec((s_len, DIL_HD), lambda h: (0, base // DIL_HD + h))
    table = pl.BlockSpec((s_len, DIL_HD), lambda h: (0, 0))
    out = pl.BlockSpec((s_len, DIL_HD), lambda h: (0, h))
    shp = jax.ShapeDtypeStruct((s_len, DIL_HEADS * DIL_HD), F32)
    return pl.pallas_call(
        body, name="dil_fwd_comm" if comm else "dil_fwd", grid=(DIL_HEADS,),
        out_shape=[shp, shp] + (_comm_out_shapes(*comm) if comm else []),
        in_specs=[head(COL_QB), head(COL_KB), head(COL_VB - NP_F32), table, table] + [ANY] * nc,
        out_specs=[out, out] + [ANY] * nc,
        scratch_shapes=[pltpu.VMEM((s_len, DIL_HD), F32) for _ in range(9)]
        + [pltpu.VMEM((2, DIL_BLOCK, 2 * DIL_BLOCK), F32)] + (_comm_scratch(nc) if comm else []),
        compiler_params=_params(("arbitrary",), 56),
    )(pf, pf, pb, cos, sin_signed, *(comm[1] if comm else []))


def _silu_and_grad(z):
    sg = _sigmoid(z)
    return z * sg, sg * (1.0 + z * (1.0 - sg))


def _post_fwd(o_a, o_b, pf, g_heads, w_out, x, gate, g_post, target=None, ts=256):
    s_len = x.shape[0]
    half = GLA_HEADS * GLA_DV
    last = target is not None

    def body(*refs):
        oa_ref, ob_ref, z_ref, gh_ref, w_ref, x_ref, gate_ref, gp_ref = refs[:8]
        xo_ref, y_ref, u_ref = refs[8 + last:11 + last]
        for src, base in ((oa_ref, 0), (ob_ref, half)):
            for hh in range(4):
                lo = hh * LANE
                og = src[:, lo:lo + LANE]
                on = og * lax.rsqrt(jnp.mean(og * og, axis=-1, keepdims=True) + EPS)
                zg = z_ref[:, base + lo:base + lo + LANE].astype(F32)
                y_ref[:, base + lo:base + lo + LANE] = (on * gh_ref[:, base + lo:base + lo + LANE]
                                                        * (zg * _sigmoid(zg))).astype(BF16)
        u = _dot(y_ref[...], w_ref[...])
        u_ref[...] = u.astype(BF16)
        rstd = lax.rsqrt(jnp.mean(u * u, axis=-1, keepdims=True) + EPS)
        x_out = x_ref[...] + gate_ref[...] * (u * rstd * gp_ref[...])
        if last:
            t_ref, loss_ref = refs[8], refs[12]

            @pl.when(pl.program_id(0) == 0)
            def _():
                loss_ref[...] = jnp.zeros_like(loss_ref)

            e = x_out - t_ref[...]
            xo_ref[...] = e * (1.0 / D_MODEL)
            loss_ref[...] += 0.5 * jnp.sum(jnp.mean(e * e, axis=-1, keepdims=True))
        else:
            xo_ref[...] = x_out

    (g_heads, gh_spec), (gate, gate_spec), (g_post, gp_spec) = _rowvec(g_heads), _rowvec(gate), _rowvec(g_post)
    tile = pl.BlockSpec((ts, D_MODEL), lambda i: (i, 0))
    halft = pl.BlockSpec((ts, half), lambda i: (i, 0))
    return pl.pallas_call(
        body, name="post_fwd_loss" if last else "post_fwd", grid=(s_len // ts,),
        out_shape=[jax.ShapeDtypeStruct((s_len, D_MODEL), F32), jax.ShapeDtypeStruct((s_len, D_MODEL), BF16),
                   jax.ShapeDtypeStruct((s_len, D_MODEL), BF16)]
        + ([jax.ShapeDtypeStruct((8, LANE), F32)] if last else []),
        in_specs=[halft, halft, tile, gh_spec, pl.BlockSpec((D_MODEL, D_MODEL), lambda i: (0, 0)), tile, gate_spec,
                  gp_spec] + ([tile] if last else []),
        out_specs=[tile, tile, tile] + ([pl.BlockSpec((8, LANE), lambda i: (0, 0))] if last else []),
        compiler_params=_params(("arbitrary",), 40),
    )(o_a, o_b, pf, g_heads, w_out, x, gate, g_post, *([target] if last else []))


def _post_bwd(dxo, u, gate, g_post, w_out, o_a, o_b, pf, g_heads, ts=256):
    s_len = dxo.shape[0]
    half = GLA_HEADS * GLA_DV

    def body(dx_ref, u_ref, gate_ref, gp_ref, w_ref, oa_ref, ob_ref, z_ref, gh_ref, du_ref, do_ref, dz_ref, sums_ref):
        @pl.when(pl.program_id(0) == 0)
        def _():
            sums_ref[...] = jnp.zeros_like(sums_ref)

        dx = dx_ref[...]
        u = u_ref[...].astype(F32)
        rstd = lax.rsqrt(jnp.mean(u * u, axis=-1, keepdims=True) + EPS)
        un = u * rstd
        sums_ref[0:1, :] += jnp.sum(dx * (un * gp_ref[...]), axis=0, keepdims=True)
        drn = dx * gate_ref[...]
        sums_ref[1:2, :] += jnp.sum(drn * un, axis=0, keepdims=True)
        dun = drn * gp_ref[...]
        du = rstd * (dun - un * jnp.mean(dun * un, axis=-1, keepdims=True))
        dub = du.astype(BF16)
        du_ref[...] = dub
        dy = _dot_nt(dub, w_ref[...])
        for src, base in ((oa_ref, 0), (ob_ref, half)):
            for hh in range(4):
                lo = base + hh * LANE
                og = src[:, hh * LANE:(hh + 1) * LANE]
                rs = lax.rsqrt(jnp.mean(og * og, axis=-1, keepdims=True) + EPS)
                on = og * rs
                zg = z_ref[:, lo:lo + LANE].astype(F32)
                sz, dsz = _silu_and_grad(zg)
                gg = gh_ref[:, lo:lo + LANE]
                dyg = dy[:, lo:lo + LANE]
                sums_ref[2:3, lo:lo + LANE] += jnp.sum(dyg * sz * on, axis=0, keepdims=True)
                dz_ref[:, lo:lo + LANE] = (dyg * on * gg * dsz).astype(BF16)
                don = dyg * gg * sz
                do_ref[:, lo:lo + LANE] = (rs * (don - on * jnp.mean(don * on, axis=-1, keepdims=True))).astype(BF16)

    (g_heads, gh_spec), (gate, gate_spec), (g_post, gp_spec) = _rowvec(g_heads), _rowvec(gate), _rowvec(g_post)
    tile = pl.BlockSpec((ts, D_MODEL), lambda i: (i, 0))
    halft = pl.BlockSpec((ts, half), lambda i: (i, 0))
    return pl.pallas_call(
        body, name="post_bwd", grid=(s_len // ts,),
        out_shape=(jax.ShapeDtypeStruct((s_len, D_MODEL), BF16), jax.ShapeDtypeStruct((s_len, D_MODEL), BF16),
                   jax.ShapeDtypeStruct((s_len, D_MODEL), BF16), jax.ShapeDtypeStruct((8, D_MODEL), F32)),
        in_specs=[tile, tile, gate_spec, gp_spec, pl.BlockSpec((D_MODEL, D_MODEL), lambda i: (0, 0)), halft, halft,
                  tile, gh_spec],
        out_specs=(tile, tile, tile, pl.BlockSpec((8, D_MODEL), lambda i: (0, 0))),
        compiler_params=_params(("arbitrary",), 40),
    )(dxo, u, gate, g_post, w_out, o_a, o_b, pf, g_heads)


def _gla_bwd(pf, pb, wgu, bgu, layer, states, do, comm=None):
    s_len = pf.shape[0]
    nc = s_len // GLA_CHUNK
    c = GLA_CHUNK
    n_cin, c_shapes, c_scratch = _comm_plumbing(comm)

    def body(*refs):
        ((q_ref, k_ref, v_ref, lr_ref, wgu_ref, bgu_ref, st_ref, do_ref),
         (dq_ref, dk_ref, dv_ref, dlr_ref, dwgu_ref, dbgu_ref), (ds_s, dec_s, dw_acc, db_acc),
         cin, cout, csem) = _split_refs(refs, 8, 6, 4, comm)
        comm_before, comm_after = _comm_hooks(comm, cin, cout, csem, steps=2)
        comm_before()
        dw_acc[...] = jnp.zeros_like(dw_acc)
        db_acc[...] = jnp.zeros_like(db_acc)
        bd = _state_block_mask()
        last_row = lax.broadcasted_iota(jnp.int32, (c, LANE), 0) == c - 1

        def local(t, carry):
            rows_list = _gla_group_rows(t)
            cm, _, _ = _gla_chunks_common(q_ref, k_ref, lr_ref, wgu_ref, bgu_ref, rows_list)
            loc = [jnp.where(bd, _dot_tn(do_ref[rows, :], cc["qe"].astype(BF16)), 0.0)
                   for rows, cc in zip(rows_list, cm)]
            for j, cc in enumerate(cm):
                ds_s[t * GLA_GROUP + j] = loc[j]
                dec_s[t * GLA_GROUP + j] = jnp.broadcast_to(cc["dec"], (8, LANE))
            return carry

        lax.fori_loop(0, nc // GLA_GROUP, local, 0)

        def scan(t, dst):
            n = nc - 1 - t
            loc = ds_s[n]
            ds_s[n] = dst
            return dec_s[n][0:1, :] * dst + loc

        lax.fori_loop(0, nc, scan, jnp.zeros((2 * GLA_DV, LANE), F32))

        def rest(t, carry):
            rows_list = _gla_group_rows(t)
            cm, ri, ci = _gla_chunks_common(q_ref, k_ref, lr_ref, wgu_ref, bgu_ref, rows_list)
            ns = [t * GLA_GROUP + j for j in range(GLA_GROUP)]
            vs = [v_ref[rows, :] for rows in rows_list]
            dobs = [do_ref[rows, :] for rows in rows_list]
            stbs = [st_ref[0, n] for n in ns]
            dsts = [ds_s[n] for n in ns]
            dstbs = [d.astype(BF16) for d in dsts]
            qebs = [cc["qe"].astype(BF16) for cc in cm]
            kebs = [cc["ke"].astype(BF16) for cc in cm]
            kendbs = [cc["kend"].astype(BF16) for cc in cm]
            hms = [_head_lane_mask(hh) for hh in range(2)]
            qehs = [[jnp.where(hm, cc["qe"], 0.0).astype(BF16) for hm in hms] for cc in cm]
            kehs = [[jnp.where(hm, cc["ke"], 0.0).astype(BF16) for hm in hms] for cc in cm]
            heads = lambda x: [x[:, hh * GLA_DV:(hh + 1) * GLA_DV] for hh in range(2)]
            vhs, dohs = [heads(v) for v in vs], [heads(d) for d in dobs]

            dqe0 = [_dot(dob, stb) for dob, stb in zip(dobs, stbs)]
            dkend = [_dot(v, dstb) for v, dstb in zip(vs, dstbs)]
            dv0 = [_dot_nt(kb, dstb) for kb, dstb in zip(kendbs, dstbs)]
            a_t = [[jnp.where(ci >= ri, _dot_nt(kehs[j][hh], qebs[j]), 0.0).astype(BF16) for hh in range(2)]
                   for j in range(GLA_GROUP)]
            da = [[jnp.where(ri >= ci, _dot_nt(dohs[j][hh], vhs[j][hh]), 0.0).astype(BF16) for hh in range(2)]
                  for j in range(GLA_GROUP)]
            da_t = [[jnp.where(ci >= ri, _dot_nt(vhs[j][hh], dohs[j][hh]), 0.0).astype(BF16) for hh in range(2)]
                    for j in range(GLA_GROUP)]
            dv1 = [[_dot(a_t[j][hh], dohs[j][hh]) for hh in range(2)] for j in range(GLA_GROUP)]
            dqe1 = [[_dot(da[j][hh], kebs[j]) for hh in range(2)] for j in range(GLA_GROUP)]
            dke1 = [[_dot(da_t[j][hh], qehs[j][hh]) for hh in range(2)] for j in range(GLA_GROUP)]

            dbs, dzs = [], []
            for j, (rows, cc) in enumerate(zip(rows_list, cm)):
                qe, ke, kend, b, bl = cc["qe"], cc["ke"], cc["kend"], cc["b"], cc["bl"]
                dqe = dqe0[j] + jnp.where(hms[0], dqe1[j][0], 0.0) + jnp.where(hms[1], dqe1[j][1], 0.0)
                dke = jnp.where(hms[0], dke1[j][0], 0.0) + jnp.where(hms[1], dke1[j][1], 0.0)
                dv_ref[rows, :] = (dv0[j] + jnp.concatenate(dv1[j], axis=1)).astype(BF16)
                dq_ref[rows, :] = (dqe * jnp.exp(b) * (GLA_DK ** -0.5)).astype(BF16)
                dk_ref[rows, :] = (dke * jnp.exp(-b) + dkend[j] * jnp.exp(bl - b)).astype(BF16)
                ddec = jnp.sum(dsts[j] * stbs[j].astype(F32), axis=0, keepdims=True)
                dbl = jnp.sum(dkend[j] * kend, axis=0, keepdims=True) + ddec * cc["dec"]
                dbs.append(dqe * qe - dke * ke - dkend[j] * kend + jnp.where(last_row, dbl, 0.0))
            triu = (ci >= ri).astype(F32)
            dlas = [jnp.dot(triu, db, precision=lax.Precision.HIGHEST, preferred_element_type=F32) for db in dbs]
            dzs = [dla * (1.0 / GLA_TAU) * _sigmoid(-cc["z"]) for dla, cc in zip(dlas, cm)]
            dzbs = [dz.astype(BF16) for dz in dzs]
            dlrs = [_dot_nt(dzb, wgu_ref[...]) for dzb in dzbs]
            dws = [_dot_tn(lr_ref[rows, :], dzb) for rows, dzb in zip(rows_list, dzbs)]
            for rows, dlr in zip(rows_list, dlrs):
                dlr_ref[0, rows, :] = dlr
            dw_acc[...] += functools.reduce(lambda x, y: x + y, dws)
            db_acc[0:1, :] += jnp.sum(functools.reduce(lambda x, y: x + y, dzs), axis=0, keepdims=True)
            return carry

        lax.fori_loop(0, nc // GLA_GROUP, rest, 0)
        dwgu_ref[...] = dw_acc[...]
        dbgu_ref[...] = db_acc[...]
        comm_after()

    pair = pl.BlockSpec((s_len, LANE), lambda g: (0, g))
    return pl.pallas_call(
        body, name="gla_bwd_comm" if comm else "gla_bwd", grid=(2,),
        out_shape=[jax.ShapeDtypeStruct((s_len, GU_COLS), BF16), jax.ShapeDtypeStruct((s_len, GU_COLS), BF16),
                   jax.ShapeDtypeStruct((s_len, GLA_HEADS * GLA_DV), BF16),
                   jax.ShapeDtypeStruct((2, s_len, LANE), F32),
                   jax.ShapeDtypeStruct((LANE, GU_COLS), F32), jax.ShapeDtypeStruct((8, GU_COLS), F32)] + c_shapes,
        in_specs=[pl.BlockSpec((s_len, LANE), lambda g: (0, COL_QA // LANE + g)),
                  pl.BlockSpec((s_len, LANE), lambda g: (0, COL_KA // LANE + g)),
                  pl.BlockSpec((s_len, 2 * GLA_DV), lambda g: (0, (COL_VA - NP_F32) // (2 * GLA_DV) + g)),
                  pl.BlockSpec((s_len, LANE), lambda g: (0, (COL_LR - NP_F32) // LANE)),
                  pl.BlockSpec((None, LANE, LANE), lambda g: (layer, 0, g)),
                  pl.BlockSpec((None, 1, LANE), lambda g: (layer, 0, g)),
                  pl.BlockSpec((1, nc, 2 * GLA_DV, LANE), lambda g: (g, 0, 0, 0)),
                  pl.BlockSpec((s_len, 2 * GLA_DV), lambda g: (0, g))] + [ANY] * n_cin,
        out_specs=[pair, pair, pl.BlockSpec((s_len, 2 * GLA_DV), lambda g: (0, g)),
                   pl.BlockSpec((1, s_len, LANE), lambda g: (g, 0, 0)),
                   pl.BlockSpec((LANE, LANE), lambda g: (0, g)), pl.BlockSpec((8, LANE), lambda g: (0, g))]
        + [ANY] * len(c_shapes),
        scratch_shapes=[pltpu.VMEM((nc, 2 * GLA_DV, LANE), F32), pltpu.VMEM((nc, 8, LANE), F32),
                        pltpu.VMEM((LANE, LANE), F32), pltpu.VMEM((8, LANE), F32)] + c_scratch,
        compiler_params=_params(("arbitrary",), 56),
    )(pf, pf, pb, pb, wgu, bgu.reshape(bgu.shape[0], 1, GU_COLS), states, do, *(comm[1] if comm else []))


def _dil_bwd(pf, pb, cos, sin_signed, do, o_b, lse, comm=None):
    s_len = pf.shape[0]
    nblk = s_len // DIL_BLOCK
    prep_rows = 256
    scale = DIL_HD ** -0.5
    nc = len(comm[1]) if comm else 0

    def body(*refs):
        ((q_ref, k_ref, v_ref, cos_ref, sin_ref, do_ref, o_ref, lse_ref), (dq_ref, dk_ref, dv_ref),
         (qf, kf, vf, dof, dl, dqa, dka, dva, bias), cin, cout, csem) = _split_refs(refs, 8, 3, 9, comm)
        comm_before, comm_after = _comm_hooks(comm, cin, cout, csem)
        comm_before()
        _dil_fill_bias(bias)

        def prep(t, carry):
            rows = pl.ds(pl.multiple_of(t * prep_rows, prep_rows), prep_rows)
            cs, sn = cos_ref[rows, :], sin_ref[rows, :]
            qf[rows, :] = _rope(q_ref[rows, :], cs, sn) * scale
            kf[rows, :] = _rope(k_ref[rows, :], cs, sn)
            vf[rows, :] = v_ref[rows, :].astype(F32)
            dov = do_ref[rows, :].astype(F32)
            dof[rows, :] = dov
            dl[rows, :] = jnp.broadcast_to(jnp.sum(dov * o_ref[rows, :], axis=-1, keepdims=True), (prep_rows, DIL_HD))
            zero = jnp.zeros((prep_rows, DIL_HD), F32)
            dqa[rows, :] = zero
            dka[rows, :] = zero
            dva[rows, :] = zero
            return carry

        lax.fori_loop(0, s_len // prep_rows, prep, 0)

        for d in DIL_DILATIONS:
            if nblk // d == 2:
                units = DIL_GROUP // 2

                def whole(i, carry, d=d, units=units):
                    rows = [_strided(i + u * (d // units), 2 * DIL_BLOCK, d) for u in range(units)]
                    ld = [(qf[rw, :].astype(BF16), kf[rw, :].astype(BF16), vf[rw, :].astype(BF16),
                           dof[rw, :].astype(BF16)) for rw in rows]
                    both = bias[...].reshape(2 * DIL_BLOCK, 2 * DIL_BLOCK)
                    s = [_dot_nt(qb, kk) + both for qb, kk, _, _ in ld]
                    dp = [_dot_nt(dob, vv) for _, _, vv, dob in ld]
                    p = [jnp.exp(sv - lse_ref[rw, :][:, 0:1]) for sv, rw in zip(s, rows)]
                    ds = [(pv * (dpv - dl[rw, :][:, 0:1])).astype(BF16) for pv, dpv, rw in zip(p, dp, rows)]
                    pb = [pv.astype(BF16) for pv in p]
                    gq = [_dot(dsv, kk) for dsv, (_, kk, _, _) in zip(ds, ld)]
                    gk = [_dot_tn(dsv, qb) for dsv, (qb, _, _, _) in zip(ds, ld)]
                    gv = [_dot_tn(pv, dob) for pv, (_, _, _, dob) in zip(pb, ld)]
                    for rw, a, b, c in zip(rows, gq, gk, gv):
                        dqa[rw, :] += a
                        dka[rw, :] += b
                        dva[rw, :] += c
                    return carry

                lax.fori_loop(0, d // units, whole, 0)
                continue

            def pair(i, carry, d=d):
                idx = [_dil_pair_block(i, half, d, nblk) for half in range(DIL_GROUP)]
                rows = [(_strided(qs, DIL_BLOCK, d), _strided(ks, 2 * DIL_BLOCK, d)) for qs, ks, _ in idx]
                ld = [(qf[qr, :].astype(BF16), kf[kr, :].astype(BF16), vf[kr, :].astype(BF16),
                       dof[qr, :].astype(BF16)) for qr, kr in rows]
                s = [_dot_nt(qb, kk) + bias[sel] for (qb, kk, _, _), (_, _, sel) in zip(ld, idx)]
                dp = [_dot_nt(dob, vv) for _, _, vv, dob in ld]
                p = [jnp.exp(sv - lse_ref[qr, :][:, 0:1]) for sv, (qr, _) in zip(s, rows)]
                ds = [(pv * (dpv - dl[qr, :][:, 0:1])).astype(BF16) for pv, dpv, (qr, _) in zip(p, dp, rows)]
                pb = [pv.astype(BF16) for pv in p]
                gq = [_dot(dsv, kk) for dsv, (_, kk, _, _) in zip(ds, ld)]
                gk = [_dot_tn(dsv, qb) for dsv, (qb, _, _, _) in zip(ds, ld)]
                gv = [_dot_tn(pv, dob) for pv, (_, _, _, dob) in zip(pb, ld)]
                for (qr, kr), a, b, c in zip(rows, gq, gk, gv):
                    dqa[qr, :] += a
                    dka[kr, :] += b
                    dva[kr, :] += c
                return carry

            lax.fori_loop(0, nblk // DIL_GROUP, pair, 0)

        def fin(t, carry):
            rows = pl.ds(pl.multiple_of(t * prep_rows, prep_rows), prep_rows)
            cs, sn = cos_ref[rows, :], sin_ref[rows, :]
            gq, gk = dqa[rows, :] * scale, dka[rows, :]
            dq_ref[rows, :] = (gq * cs - pltpu.roll(gq, DIL_HD // 2, 1) * sn).astype(BF16)
            dk_ref[rows, :] = (gk * cs - pltpu.roll(gk, DIL_HD // 2, 1) * sn).astype(BF16)
            dv_ref[rows, :] = dva[rows, :].astype(BF16)
            return carry

        lax.fori_loop(0, s_len // prep_rows, fin, 0)
        comm_after()

    head = lambda base: pl.BlockSpec((s_len, DIL_HD), lambda h: (0, base // DIL_HD + h))
    table = pl.BlockSpec((s_len, DIL_HD), lambda h: (0, 0))
    out = pl.BlockSpec((s_len, DIL_HD), lambda h: (0, h))
    shp = jax.ShapeDtypeStruct((s_len, DIL_HEADS * DIL_HD), BF16)
    return pl.pallas_call(
        body, name="dil_bwd_comm" if comm else "dil_bwd", grid=(DIL_HEADS,),
        out_shape=[shp, shp, shp] + (_comm_out_shapes(*comm) if comm else []),
        in_specs=[head(COL_QB), head(COL_KB), head(COL_VB - NP_F32), table, table,
                  pl.BlockSpec((s_len, DIL_HD), lambda h: (0, DIL_HEADS + h)), out, out] + [ANY] * nc,
        out_specs=[out, out, out] + [ANY] * len(_comm_plumbing(comm)[1]),
        scratch_shapes=[pltpu.VMEM((s_len, DIL_HD), F32) for _ in range(8)]
        + [pltpu.VMEM((2, DIL_BLOCK, 2 * DIL_BLOCK), F32)] + (_comm_scratch(nc) if comm else []),
        compiler_params=_params(("arbitrary",), 56),
    )(pf, pf, pb, cos, sin_signed, do, o_b, lse, *(comm[1] if comm else []))


_PIECES = ((COL_Z, 1024), (COL_QA, 256), (COL_KA, 256), (COL_QB, 512), (COL_KB, 512), (COL_VA, 512), (COL_VB, 512),
           (COL_LR, 128))


def _in_bwd(pieces, w_new, x, dxo, g_pre, scale, comm=None, ts=256):
    s_len = x.shape[0]
    nc = len(comm[1]) if comm else 0
    nco = len(_comm_out_shapes(*comm)) if comm else 0
    npc = len(_PIECES)

    def body(*refs):
        p_refs = refs[:npc]
        w_ref, x_ref, dxo_ref, g_ref, sc_ref = refs[npc:npc + 5]
        cin, (dx_ref, sums_ref), cout = (refs[npc + 5:npc + 5 + nc], refs[npc + 5 + nc:npc + 7 + nc],
                                         refs[npc + 7 + nc:npc + 7 + nc + nco])
        comm_before, comm_after = _comm_hooks(comm, cin, cout, refs[npc + 7 + nc + nco:], steps=s_len // ts)
        comm_before()

        @pl.when(pl.program_id(0) == 0)
        def _():
            sums_ref[...] = jnp.zeros_like(sums_ref)

        dh = jnp.zeros((ts, D_MODEL), F32)
        for p_ref, (col, width) in zip(p_refs, _PIECES):
            dh += _dot_nt(p_ref[...], w_ref[:, col:col + width])
        xv = x_ref[...]
        rstd = lax.rsqrt(jnp.mean(xv * xv, axis=-1, keepdims=True) + EPS)
        xn = xv * rstd
        sums_ref[0:1, :] += jnp.sum(dh, axis=0, keepdims=True)
        sums_ref[1:2, :] += jnp.sum(dh * (xn * g_ref[...]), axis=0, keepdims=True)
        dr = dh * (1.0 + sc_ref[...])
        sums_ref[2:3, :] += jnp.sum(dr * xn, axis=0, keepdims=True)
        dxn = dr * g_ref[...]
        dx_ref[...] = dxo_ref[...] + rstd * (dxn - xn * jnp.mean(dxn * xn, axis=-1, keepdims=True))
        comm_after()

    (g_pre, g_spec), (scale, sc_spec) = _rowvec(g_pre), _rowvec(scale)
    tile = pl.BlockSpec((ts, D_MODEL), lambda i: (i, 0))
    return pl.pallas_call(
        body, name="in_bwd_comm" if comm else "in_bwd", grid=(s_len // ts,),
        out_shape=[jax.ShapeDtypeStruct((s_len, D_MODEL), F32), jax.ShapeDtypeStruct((8, D_MODEL), F32)]
        + (_comm_out_shapes(*comm) if comm else []),
        in_specs=[pl.BlockSpec((ts, width), lambda i: (i, 0)) for _, width in _PIECES]
        + [pl.BlockSpec((D_MODEL, NP), lambda i: (0, 0)), tile, tile, g_spec, sc_spec] + [ANY] * nc,
        out_specs=[tile, pl.BlockSpec((8, D_MODEL), lambda i: (0, 0))] + [ANY] * nco,
        scratch_shapes=_comm_scratch(nc) if comm else [],
        compiler_params=_params(("arbitrary",), 56),
    )(*pieces, w_new, x, dxo, g_pre, scale, *(comm[1] if comm else []))


def _w_in_to_kernel(gathered, tr=128):
    def body(g_ref, o_ref):
        cols = jnp.concatenate([g_ref[k].astype(F32) for k in range(N_DEV)], axis=1)
        pad = jnp.zeros((tr, LANE - GLA_LOWRANK), F32)
        o_ref[...] = jnp.concatenate(
            [cols[:, 1024:1536], cols[:, 3088:3600], cols[:, 0:512], cols[:, 1552:2576], cols[:, 512:1024],
             cols[:, 2576:3088], cols[:, 1536:1552], pad], axis=1).astype(BF16)

    return pl.pallas_call(
        body, name="w_in_to_kernel", grid=(D_MODEL // tr,), out_shape=jax.ShapeDtypeStruct((D_MODEL, NP), BF16),
        in_specs=[pl.BlockSpec((N_DEV, tr, W_IN_SHARD), lambda i: (0, i, 0))],
        out_specs=pl.BlockSpec((tr, NP), lambda i: (i, 0)),
        compiler_params=_params(("arbitrary",)),
    )(gathered)


def _grad_w_in(h, pieces, ts=512, tr=128):
    s_len = h.shape[0]
    steps = s_len // ts

    def body(*refs):
        h_ref, p_refs = refs[0], refs[1:1 + len(_PIECES)]
        o_ref, acc = refs[1 + len(_PIECES):]

        @pl.when(pl.program_id(0) == 0)
        def _():
            acc[...] = jnp.zeros_like(acc)

        hv = h_ref[...]
        for p_ref, (col, width) in zip(p_refs, _PIECES):
            acc[:, col:col + width] += _dot_tn(hv, p_ref[...])

        @pl.when(pl.program_id(0) == steps - 1)
        def _():
            def rows_out(t, carry):
                rows = pl.ds(pl.multiple_of(t * tr, tr), tr)
                g = acc[rows, :]
                cols = jnp.concatenate(
                    [g[:, COL_QA:COL_QB], g[:, COL_VA:COL_VB], g[:, 0:512], g[:, COL_LR:COL_LR + GLA_LOWRANK],
                     g[:, COL_QB:COL_VA], g[:, COL_VB:COL_LR], g[:, 512:1024]], axis=1)
                for k in range(N_DEV):
                    o_ref[k, rows, :] = cols[:, W_IN_SHARD * k:W_IN_SHARD * (k + 1)].astype(BF16)
                return carry

            lax.fori_loop(0, D_MODEL // tr, rows_out, 0)

    return pl.pallas_call(
        body, name="grad_w_in", grid=(steps,),
        out_shape=jax.ShapeDtypeStruct((N_DEV, D_MODEL, W_IN_SHARD), BF16),
        in_specs=[pl.BlockSpec((ts, D_MODEL), lambda i: (i, 0))]
        + [pl.BlockSpec((ts, width), lambda i: (i, 0)) for _, width in _PIECES],
        out_specs=pl.BlockSpec((N_DEV, D_MODEL, W_IN_SHARD), lambda i: (0, 0, 0)),
        scratch_shapes=[pltpu.VMEM((D_MODEL, NP), F32)],
        compiler_params=_params(("arbitrary",), 56),
    )(h, *pieces)


def _matmul_tn(a, b, name, bn, ts=512):
    s_len, m = a.shape
    n = b.shape[1]
    steps = s_len // ts

    def body(a_ref, b_ref, o_ref, acc):
        @pl.when(pl.program_id(1) == 0)
        def _():
            acc[...] = jnp.zeros_like(acc)

        acc[...] += _dot_tn(a_ref[...], b_ref[...])

        @pl.when(pl.program_id(1) == steps - 1)
        def _():
            o_ref[...] = acc[...].astype(BF16)

    return pl.pallas_call(
        body, name=name, grid=(n // bn, steps),
        out_shape=jax.ShapeDtypeStruct((m, n), BF16),
        in_specs=[pl.BlockSpec((ts, m), lambda j, i: (i, 0)), pl.BlockSpec((ts, bn), lambda j, i: (i, j))],
        out_specs=pl.BlockSpec((m, bn), lambda j, i: (0, j)),
        scratch_shapes=[pltpu.VMEM((m, bn), F32)],
        compiler_params=_params(("arbitrary", "arbitrary"), 40),
    )(a, b)


def _adam_math(w, g, m, v):
    m = ADAM_B1 * m + (1.0 - ADAM_B1) * g
    v = ADAM_B2 * v + (1.0 - ADAM_B2) * (g * g)
    m_hat = m / (1.0 - ADAM_B1 ** ADAM_STEP)
    v_hat = v / (1.0 - ADAM_B2 ** ADAM_STEP)
    delta = -ADAM_LR * (m_hat / (jnp.sqrt(v_hat) + ADAM_EPS) + ADAM_WD * w)
    return delta, m, v


def _adamw(w, parts, m, v, name, tr):
    r, cdim = w.shape
    n_parts = parts.shape[0]

    def body(w_ref, p_ref, m_ref, v_ref, g_ref, d_ref, nm_ref, nv_ref):
        g = p_ref[0].astype(F32)
        for k in range(1, n_parts):
            g = g + p_ref[k].astype(F32)
        g_ref[...] = g
        d_ref[...], nm_ref[...], nv_ref[...] = _adam_math(w_ref[...], g, m_ref[...], v_ref[...])

    tile = pl.BlockSpec((tr, cdim), lambda i: (i, 0))
    shp = jax.ShapeDtypeStruct((r, cdim), F32)
    return pl.pallas_call(
        body, name=name, grid=(r // tr,), out_shape=(shp, shp, shp, shp),
        in_specs=[tile, pl.BlockSpec((n_parts, tr, cdim), lambda i: (0, i, 0)), tile, tile],
        out_specs=(tile, tile, tile, tile),
        compiler_params=_params(("arbitrary",), 40),
    )(w, parts, m, v)


def _adamw_layers(w, parts, m, v, name, tr):
    n_layers, r, cdim = w.shape

    def body(*refs):
        w_ref, p_refs, (m_ref, v_ref) = refs[0], refs[1:1 + n_layers], refs[1 + n_layers:3 + n_layers]
        g_ref, d_ref, nm_ref, nv_ref = refs[3 + n_layers:]
        for l, p_ref in enumerate(p_refs):
            @pl.when(pl.program_id(0) == l)
            def _(p_ref=p_ref):
                g = p_ref[0].astype(F32)
                for k in range(1, p_ref.shape[0]):
                    g = g + p_ref[k].astype(F32)
                g_ref[0] = g
                d_ref[0], nm_ref[0], nv_ref[0] = _adam_math(w_ref[0], g, m_ref[0], v_ref[0])

    tile = pl.BlockSpec((1, tr, cdim), lambda l, i: (l, i, 0))
    part = lambda own: pl.BlockSpec((parts[own].shape[0], tr, cdim), lambda l, i: (0, jnp.where(l == own, i, 0), 0))
    shp = jax.ShapeDtypeStruct(w.shape, F32)
    return pl.pallas_call(
        body, name=name, grid=(n_layers, r // tr), out_shape=(shp, shp, shp, shp),
        in_specs=[tile] + [part(l) for l in range(n_layers)] + [tile, tile],
        out_specs=(tile, tile, tile, tile),
        compiler_params=_params(("arbitrary", "arbitrary"), 40),
    )(w, *parts, m, v)


def _row(vec, width):
    vec = vec.reshape(1, -1)
    return jnp.pad(vec, ((0, 0), (0, width - vec.shape[1])))


def kernel(x, c, w_ada, b_ada, g_pre, w_in, w_gate_up, b_gate_up, g_gla, g_dil, w_out, g_post, loss_target, m_w_ada, m_b_ada, m_g_pre, m_w_in, m_w_gate_up, m_b_gate_up, m_g_gla, m_g_dil, m_w_out, m_g_post, v_w_ada, v_b_ada, v_g_pre, v_w_in, v_w_gate_up, v_b_gate_up, v_g_gla, v_g_dil, v_w_out, v_g_post):
    px, py, pc = _my_position()
    me = _linear(px, py, pc)
    xs = x[0]
    target = loss_target[0]
    s_len = xs.shape[0]
    assert s_len % (DIL_BLOCK * max(DIL_DILATIONS) * 2) == 0 and xs.shape[1] == D_MODEL

    c_all = _all_gather(jnp.pad(c, ((0, 7), (0, 0))), "gather_c").reshape(N_DEV, 8, D_MODEL)[:, 0]
    mod_part = _mod_fwd(c_all, w_ada)
    w_in_b, w_out_b = w_in.astype(BF16), w_out.astype(BF16)
    mod_all, wgu_all, w_in_all = _comm_call(
        "gather", [mod_part.reshape(DEPTH * N_DEV, ADA_SHARD), w_gate_up.reshape(DEPTH * GLA_LOWRANK, GU_SHARD),
                   w_in_b[0]], "gather_first")
    mod_all = mod_all.reshape(N_DEV, DEPTH, N_DEV, ADA_SHARD)
    mod_mine = lax.dynamic_index_in_dim(mod_all, me, axis=2, keepdims=False)
    mod = jnp.transpose(mod_mine, (1, 0, 2)).reshape(DEPTH, 3 * D_MODEL) + b_ada
    wgu_full = jnp.transpose(wgu_all.reshape(N_DEV, DEPTH, GLA_LOWRANK, GU_SHARD), (1, 2, 0, 3)).reshape(
        DEPTH, GLA_LOWRANK, GU_COLS)
    wgu_pad = jnp.pad(wgu_full, ((0, 0), (0, LANE - GLA_LOWRANK), (0, 0))).astype(BF16)

    def kernel_w_in(gathered):
        return _w_in_to_kernel(gathered.reshape(N_DEV, D_MODEL, W_IN_SHARD))

    cos, sin_signed = _rope_tables(s_len)
    g_heads = jnp.concatenate([g_gla, g_dil], axis=1)

    saved = []
    xl = xs
    for l in range(DEPTH):
        shift, scale, gate = ((mod, l, k) for k in range(3))
        w_new = kernel_w_in(w_in_all)
        pf, pb, h, w_out_l = _prenorm_proj(xl, (g_pre, l, 0), scale, shift, w_new, comm=("gather", [w_out_b[l]]))
        o_a, states = _gla_fwd(pf, pb, wgu_pad, b_gate_up, l)
        if l + 1 < DEPTH:
            o_b, lse, w_in_all = _dil_fwd(pf, pb, cos, sin_signed, comm=("gather", [w_in_b[l + 1]]))
        else:
            o_b, lse = _dil_fwd(pf, pb, cos, sin_signed)
        if l + 1 < DEPTH:
            x_next, y, u = _post_fwd(o_a, o_b, pf, (g_heads, l, 0), w_out_l, xl, gate, (g_post, l, 0))
        else:
            dx, y, u, loss_part = _post_fwd(o_a, o_b, pf, (g_heads, l, 0), w_out_l, xl, gate, (g_post, l, 0),
                                            target=target)
        saved.append((xl, scale, gate, w_new, w_out_l, pf, pb, h, o_a, states, o_b, lse, y, u))
        xl = x_next

    small_rows = []
    gin_slots, gin_parts, gout_parts = None, [None] * DEPTH, [None] * DEPTH
    for l in reversed(range(DEPTH)):
        x_in, scale, gate, w_new, w_out_l, pf, pb, h, o_a, states, o_b, lse, y, u = saved[l]
        du, do, dz, sums_post = _post_bwd(dx, u, gate, (g_post, l, 0), w_out_l, o_a, o_b, pf, (g_heads, l, 0))
        gout_slots = _matmul_tn(y, du, "grad_w_out", 512)
        dq_a, dk_a, dv_a, dlr2, dwgu, dbgu, arrived = _gla_bwd(pf, pb, wgu_pad, b_gate_up, l, states, do,
                                                               comm=("exchange", [gout_slots]))
        gout_parts[l] = arrived.reshape(N_DEV, OUT_SHARD, D_MODEL)
        if gin_slots is not None:
            dq_b, dk_b, dv_b, arrived, _, _ = _dil_bwd(pf, pb, cos, sin_signed, do, o_b, lse,
                                                       comm=("pairsum_exchange", [gin_slots]))
            gin_parts[l + 1] = arrived.reshape(N_DEV // 2, D_MODEL, W_IN_SHARD)
        else:
            dq_b, dk_b, dv_b = _dil_bwd(pf, pb, cos, sin_signed, do, o_b, lse)
        dlr = (dlr2[0] + dlr2[1]).astype(BF16)
        pieces = (dz, dq_a, dk_a, dq_b, dk_b, dv_a, dv_b, dlr)
        gin_slots = _grad_w_in(h, pieces).reshape(N_DEV * D_MODEL, W_IN_SHARD)
        if l == 0:
            dx, sums_in, arrived, _, _ = _in_bwd(pieces, w_new, x_in, dx, (g_pre, l, 0), scale,
                                                 comm=("pairsum_exchange", [gin_slots]))
            gin_parts[0] = arrived.reshape(N_DEV // 2, D_MODEL, W_IN_SHARD)
        else:
            dx, sums_in = _in_bwd(pieces, w_new, x_in, dx, (g_pre, l, 0), scale)
        dmod = jnp.concatenate([sums_in[0], sums_in[1], sums_post[0]])
        vecs = jnp.concatenate([sums_in[2], sums_post[1], sums_post[2], dbgu[0]])
        small_rows[0:0] = [_row(dmod, 4096), _row(vecs, 4096), _row(dwgu[:GLA_LOWRANK], 4096)]
    grad_x = dx[None]

    flat = lambda a, rows: a.reshape(rows, a.shape[-1])
    r_ada = DEPTH * D_MODEL
    g_w_in, d_w_in, nm_w_in, nv_w_in = _adamw_layers(w_in, gin_parts, m_w_in, v_w_in, "adamw_w_in", 256)
    g_w_out, d_w_out, nm_w_out, nv_w_out = _adamw_layers(w_out, gout_parts, m_w_out, v_w_out, "adamw_w_out", 128)

    small_rows += [_row(loss_part[0, 0:1], 4096), jnp.zeros((1, 4096), F32)]
    small = _all_gather(jnp.concatenate(small_rows, axis=0), "gather_small").reshape(N_DEV, 8, 4096)
    dmod_all = jnp.stack([small[:, 0, :3 * D_MODEL], small[:, 3, :3 * D_MODEL]])
    dmod_cols = lax.dynamic_slice_in_dim(dmod_all, me * ADA_SHARD, ADA_SHARD, axis=2)
    gwa = _w_ada_grad(c_all, dmod_cols).reshape(1, r_ada, ADA_SHARD)
    g_w_ada, d_w_ada, nm_w_ada, nv_w_ada = (
        t.reshape(w_ada.shape) for t in _adamw(flat(w_ada, r_ada), gwa, flat(m_w_ada, r_ada), flat(v_w_ada, r_ada),
                                               "adamw_w_ada", 256))

    def small_param(w, m, v, cols, row, name):
        n = w.shape[1]
        parts = jnp.stack([small[:, row, cols:cols + n], small[:, row + 3, cols:cols + n]], axis=1)
        return _adamw(w, parts, m, v, name, DEPTH)

    g_b_ada, d_b_ada, nm_b_ada, nv_b_ada = small_param(b_ada, m_b_ada, v_b_ada, 0, 0, "adamw_b_ada")
    g_g_pre, d_g_pre, nm_g_pre, nv_g_pre = small_param(g_pre, m_g_pre, v_g_pre, 0, 1, "adamw_g_pre")
    g_g_post, d_g_post, nm_g_post, nv_g_post = small_param(g_post, m_g_post, v_g_post, 1024, 1, "adamw_g_post")
    g_g_gla, d_g_gla, nm_g_gla, nv_g_gla = small_param(g_gla, m_g_gla, v_g_gla, 2048, 1, "adamw_g_gla")
    g_g_dil, d_g_dil, nm_g_dil, nv_g_dil = small_param(g_dil, m_g_dil, v_g_dil, 2560, 1, "adamw_g_dil")
    g_b_gu, d_b_gu, nm_b_gu, nv_b_gu = small_param(b_gate_up, m_b_gate_up, v_b_gate_up, 3072, 1, "adamw_b_gate_up")
    gu_parts = jnp.stack([small[:, 2], small[:, 5]], axis=1).reshape(N_DEV, DEPTH, GLA_LOWRANK, GU_COLS)
    gu_parts = lax.dynamic_slice_in_dim(gu_parts, me * GU_SHARD, GU_SHARD, axis=3).reshape(
        N_DEV, DEPTH * GLA_LOWRANK, GU_SHARD)
    r_gu = DEPTH * GLA_LOWRANK
    g_w_gu, d_w_gu, nm_w_gu, nv_w_gu = (
        t.reshape(w_gate_up.shape) for t in _adamw(flat(w_gate_up, r_gu), gu_parts, flat(m_w_gate_up, r_gu),
                                                   flat(v_w_gate_up, r_gu), "adamw_w_gate_up", r_gu))
    loss_parts = jnp.broadcast_to(small[:, 6, 0:1].reshape(N_DEV, 1, 1), (N_DEV, 8, LANE))
    loss = _sum_parts(loss_parts)[0, 0]

    return (loss, grad_x,
            g_w_ada, g_b_ada, g_g_pre, g_w_in, g_w_gu, g_b_gu, g_g_gla, g_g_dil, g_w_out, g_g_post,
            d_w_ada, d_b_ada, d_g_pre, d_w_in, d_w_gu, d_b_gu, d_g_gla, d_g_dil, d_w_out, d_g_post,
            nm_w_ada, nm_b_ada, nm_g_pre, nm_w_in, nm_w_gu, nm_b_gu, nm_g_gla, nm_g_dil, nm_w_out, nm_g_post,
            nv_w_ada, nv_b_ada, nv_g_pre, nv_w_in, nv_w_gu, nv_b_gu, nv_g_gla, nv_g_dil, nv_w_out, nv_g_post)


def _sum_parts(parts):
    n_parts = parts.shape[0]

    def body(p_ref, o_ref):
        acc = p_ref[0]
        for k in range(1, n_parts):
            acc = acc + p_ref[k]
        o_ref[...] = acc

    return pl.pallas_call(body, name="sum_loss", out_shape=jax.ShapeDtypeStruct(parts.shape[1:], F32))(parts)
```

```python
import functools
import math

import jax
import jax.numpy as jnp
from jax import lax
from jax.experimental import pallas as pl
from jax.experimental.pallas import tpu as pltpu

F32 = jnp.float32
BF16 = jnp.bfloat16

N_DEV = 8
D_MODEL = 1024
DEPTH = 2
GLA_HEADS = 4
GLA_DK = 64
GLA_DV = 128
GLA_CHUNK = 64
GLA_TAU = 16.0
GLA_LOWRANK = 16
DIL_HEADS = 4
DIL_HD = 128
DIL_BLOCK = 128
DIL_DILATIONS = (1, 4, 16)
ROPE_THETA = 10000.0
EPS = 1e-6
IN_COLS = 3600
W_IN_SHARD = IN_COLS // N_DEV
ADA_SHARD = 3 * D_MODEL // N_DEV
OUT_SHARD = D_MODEL // N_DEV
GU_COLS = GLA_HEADS * GLA_DK
GU_SHARD = GU_COLS // N_DEV

ADAM_LR = 0.001
ADAM_B1 = 0.9
ADAM_B2 = 0.999
ADAM_EPS = 1e-08
ADAM_WD = 0.01
ADAM_STEP = 10

NP = 3712
COL_Z, COL_QA, COL_KA, COL_QB, COL_KB, COL_VA, COL_VB, COL_LR = 0, 1024, 1280, 1536, 2048, 2560, 3072, 3584
NP_F32 = COL_VA
NP_BF16 = NP - NP_F32
LANE = 128
MASK_VALUE = -1e30

MESH = pl.DeviceIdType.MESH
ANY = pl.BlockSpec(memory_space=pl.ANY)


def _params(sem=None, vmem_mb=None):
    kw = {}
    if sem is not None:
        kw["dimension_semantics"] = sem
    if vmem_mb is not None:
        kw["vmem_limit_bytes"] = vmem_mb * 1024 * 1024
    return pltpu.CompilerParams(**kw)


def _dot(a, b):
    return jnp.dot(a, b, preferred_element_type=F32)


def _dot_nt(a, b):
    return lax.dot_general(a, b, (((1,), (1,)), ((), ())), preferred_element_type=F32)


def _dot_tn(a, b):
    return lax.dot_general(a, b, (((0,), (0,)), ((), ())), preferred_element_type=F32)


def _sigmoid(z):
    return 1.0 / (1.0 + jnp.exp(-z))


def _log_sigmoid(z):
    return jnp.minimum(z, 0.0) - jnp.log(1.0 + jnp.exp(-jnp.abs(z)))


def _rowvec(v, width=D_MODEL):
    arr, row, cb = v
    return arr.reshape(arr.shape[0], 1, arr.shape[1]), pl.BlockSpec((None, 1, width), lambda *_: (row, 0, cb))


def _my_position():
    return lax.axis_index("x"), lax.axis_index("y"), lax.axis_index("c")


def _linear(px, py, pc):
    return 4 * px + 2 * py + pc


def _gather_phase(phase, x_ref, out_ref, send_sem, recv_sem, local_sem):
    m = x_ref.shape[0]
    x, y, c = _my_position()
    me, sibling = (x, y, c), (x, y, 1 - c)
    chips = [(1 - x, y), (x, 1 - y), (1 - x, 1 - y)]

    def rows(px, py, pc):
        return out_ref.at[pl.ds(_linear(px, py, pc) * m, m), :]

    def copy(k, block, to, src=None):
        return pltpu.make_async_remote_copy(
            src_ref=rows(*block) if src is None else src, dst_ref=rows(*block),
            send_sem=send_sem(k), recv_sem=recv_sem(k), device_id=to, device_id_type=MESH)

    mine = pltpu.make_async_copy(x_ref, rows(*me), local_sem)
    first = [copy(0, me, sibling, src=x_ref)] + [copy(1 + j, me, (*chip, c), src=x_ref) for j, chip in enumerate(chips)]
    passed = [copy(4 + j, (*chip, c), sibling) for j, chip in enumerate(chips)]
    if phase == "start":
        mine.start()
        for cp in first:
            cp.start()
    elif phase == "forward":
        for j, chip in enumerate(chips):
            copy(1 + j, (*chip, c), me).wait_recv()
            passed[j].start()
    else:
        copy(0, sibling, me).wait_recv()
        for j, chip in enumerate(chips):
            copy(4 + j, (*chip, 1 - c), me).wait_recv()
        for cp in first + passed:
            cp.wait_send()
        mine.wait()


def _exchange_phase(phase, x_ref, out_ref, send_sem, recv_sem, local_sem):
    m = x_ref.shape[0] // N_DEV
    x, y, c = _my_position()
    me = _linear(x, y, c)

    def rows(ref, idx):
        return ref.at[pl.ds(idx * m, m), :]

    peers = [(1 - x if j & 4 else x, 1 - y if j & 2 else y, 1 - c if j & 1 else c) for j in range(1, N_DEV)]
    local = pltpu.make_async_copy(rows(x_ref, me), rows(out_ref, me), local_sem)
    sends = [pltpu.make_async_remote_copy(
        src_ref=rows(x_ref, _linear(*peer)), dst_ref=rows(out_ref, me),
        send_sem=send_sem(j), recv_sem=recv_sem(j), device_id=peer, device_id_type=MESH) for j, peer in enumerate(peers)]
    if phase == "start":
        local.start()
        for cp in sends:
            cp.start()
    else:
        for j, peer in enumerate(peers):
            pltpu.make_async_remote_copy(
                src_ref=rows(x_ref, _linear(*peer)), dst_ref=rows(out_ref, _linear(*peer)),
                send_sem=send_sem(j), recv_sem=recv_sem(j), device_id=peer, device_id_type=MESH).wait_recv()
        for cp in sends:
            cp.wait_send()
        local.wait()


def _pairsum_exchange_phase(phase, x_ref, out_refs, send_sem, recv_sem, local_sem):
    out_ref, stage_ref, pair_ref = out_refs
    m, n = x_ref.shape[0] // N_DEV, x_ref.shape[1]
    x, y, c = _my_position()
    mine = 2 * x + y
    chips = [(qx, qy) for qx in range(2) for qy in range(2)]
    others = [(1 - x, y), (x, 1 - y), (1 - x, 1 - y)]

    def rows(ref, idx):
        return ref.at[pl.ds(idx * m, m), :]

    def remote(src, dst, k, to):
        return pltpu.make_async_remote_copy(src_ref=src, dst_ref=dst, send_sem=send_sem(k), recv_sem=recv_sem(k),
                                            device_id=to, device_id_type=MESH)

    to_sibling = [remote(rows(x_ref, _linear(qx, qy, 1 - c)), rows(stage_ref, q), q, (x, y, 1 - c))
                  for q, (qx, qy) in enumerate(chips)]
    to_chips = [remote(rows(pair_ref, 2 * qx + qy), rows(out_ref, mine), 4 + j, (qx, qy, c))
                for j, (qx, qy) in enumerate(others)]
    keep = pltpu.make_async_copy(rows(pair_ref, mine), rows(out_ref, mine), local_sem)
    if phase == "start":
        for cp in to_sibling:
            cp.start()
    elif phase == "reduce":
        for cp in to_sibling:
            cp.wait_recv()

        def through_vmem(a_buf, b_buf, sems):
            tr = 128
            loads = [(pltpu.make_async_copy(rows(x_ref, _linear(qx, qy, c)), a_buf.at[q % 2], sems.at[q % 2]),
                      pltpu.make_async_copy(rows(stage_ref, q), b_buf.at[q % 2], sems.at[2 + q % 2]))
                     for q, (qx, qy) in enumerate(chips)]
            stores = [pltpu.make_async_copy(a_buf.at[q % 2], rows(pair_ref, q), sems.at[4 + q % 2]) for q in range(4)]
            for q in range(4):
                if q >= 2:
                    stores[q - 2].wait()
                for cp in loads[q]:
                    cp.start()
                for cp in loads[q]:
                    cp.wait()

                def add(r, carry, q=q):
                    tile = pl.ds(pl.multiple_of(r * tr, tr), tr)
                    a_buf[q % 2, tile, :] = (a_buf[q % 2, tile, :].astype(F32)
                                             + b_buf[q % 2, tile, :].astype(F32)).astype(x_ref.dtype)
                    return carry

                lax.fori_loop(0, m // tr, add, 0)
                stores[q].start()
            stores[2].wait()
            stores[3].wait()

        pl.run_scoped(through_vmem, pltpu.VMEM((2, m, n), x_ref.dtype), pltpu.VMEM((2, m, n), x_ref.dtype),
                      pltpu.SemaphoreType.DMA((6,)))
    elif phase == "send":
        keep.start()
        for cp in to_chips:
            cp.start()
    else:
        for j, (qx, qy) in enumerate(others):
            remote(rows(pair_ref, mine), rows(out_ref, 2 * qx + qy), 4 + j, (qx, qy, c)).wait_recv()
        for cp in to_sibling + to_chips:
            cp.wait_send()
        keep.wait()


_COMM_PHASES = {"gather": (_gather_phase, ("start", "forward", "finish")),
                "exchange": (_exchange_phase, ("start", "finish")),
                "pairsum_exchange": (_pairsum_exchange_phase, ("start", "reduce", "send", "finish"))}


def _comm_scratch(n_arrays):
    return [pltpu.SemaphoreType.DMA((n_arrays, 7)), pltpu.SemaphoreType.DMA((n_arrays, 7)),
            pltpu.SemaphoreType.DMA((n_arrays,))]


def _comm_run(kind, phases, x_refs, out_refs, send_sems, recv_sems, local_sems):
    fn = _COMM_PHASES[kind][0]
    per = len(out_refs) // len(x_refs)
    for phase in phases:
        for a, x_ref in enumerate(x_refs):
            outs = out_refs[a] if per == 1 else tuple(out_refs[per * a:per * (a + 1)])
            fn(phase, x_ref, outs, lambda k, a=a: send_sems.at[a, k], lambda k, a=a: recv_sems.at[a, k],
               local_sems.at[a])


def _comm_out_shapes(kind, arrays):
    if kind == "pairsum_exchange":
        return [jax.ShapeDtypeStruct((a.shape[0] // 2, a.shape[1]), a.dtype) for a in arrays for _ in range(3)]
    return [jax.ShapeDtypeStruct((N_DEV * a.shape[0], a.shape[1]) if kind == "gather" else a.shape, a.dtype)
            for a in arrays]


def _comm_call(kind, arrays, name):
    n = len(arrays)
    shapes = _comm_out_shapes(kind, arrays)

    def body(*refs):
        _comm_run(kind, _COMM_PHASES[kind][1], refs[:n], refs[n:n + len(shapes)], *refs[n + len(shapes):])

    return pl.pallas_call(body, name=name, out_shape=shapes, in_specs=[ANY] * n, out_specs=[ANY] * len(shapes),
                          scratch_shapes=_comm_scratch(n))(*arrays)


def _all_gather(xs, name):
    return _comm_call("gather", [xs], name)[0]


def _mod_fwd(c_all, w_ada):
    def body(c_ref, w_ref, o_ref):
        cv = c_ref[...]
        sc = cv * _sigmoid(cv)
        o_ref[0] = _dot(sc.astype(BF16), w_ref[0].astype(BF16))

    return pl.pallas_call(
        body, name="mod_fwd", grid=(DEPTH,),
        out_shape=jax.ShapeDtypeStruct((DEPTH, N_DEV, ADA_SHARD), F32),
        in_specs=[pl.BlockSpec((N_DEV, D_MODEL), lambda l: (0, 0)),
                  pl.BlockSpec((1, D_MODEL, ADA_SHARD), lambda l: (l, 0, 0))],
        out_specs=pl.BlockSpec((1, N_DEV, ADA_SHARD), lambda l: (l, 0, 0)),
        compiler_params=_params(("arbitrary",)),
    )(c_all, w_ada)


def _w_ada_grad(c_all, dmod_cols):
    def body(c_ref, d_ref, o_ref):
        cv = c_ref[...]
        sc = cv * _sigmoid(cv)
        o_ref[0] = lax.dot_general(sc, d_ref[0], (((0,), (0,)), ((), ())), precision=lax.Precision.HIGHEST,
                                   preferred_element_type=F32)

    return pl.pallas_call(
        body, name="w_ada_grad", grid=(DEPTH,),
        out_shape=jax.ShapeDtypeStruct((DEPTH, D_MODEL, ADA_SHARD), F32),
        in_specs=[pl.BlockSpec((N_DEV, D_MODEL), lambda l: (0, 0)),
                  pl.BlockSpec((1, N_DEV, ADA_SHARD), lambda l: (l, 0, 0))],
        out_specs=pl.BlockSpec((1, D_MODEL, ADA_SHARD), lambda l: (l, 0, 0)),
        compiler_params=_params(("arbitrary",)),
    )(c_all, dmod_cols)


def _comm_plumbing(comm):
    if not comm:
        return 0, [], []
    return len(comm[1]), _comm_out_shapes(*comm), _comm_scratch(len(comm[1]))


def _split_refs(refs, n_in, n_out, n_scratch, comm):
    ci, shapes, _ = _comm_plumbing(comm)
    co = len(shapes)
    a, b, c = n_in + ci, n_in + ci + n_out, n_in + ci + n_out + co
    return refs[:n_in], refs[a:b], refs[c:c + n_scratch], refs[n_in:a], refs[b:c], refs[c + n_scratch:]


def _prenorm_proj(x, g_pre, scale, shift, w_new, comm=None, ts=256):
    s_len = x.shape[0]
    n_cin, c_shapes, c_scratch = _comm_plumbing(comm)

    def body(*refs):
        (x_ref, g_ref, sc_ref, sh_ref, w_ref), (pf_ref, pb_ref, h_ref), _, cin, cout, csem = _split_refs(
            refs, 5, 3, 0, comm)
        comm_before, comm_after = _comm_hooks(comm, cin, cout, csem, steps=s_len // ts)
        comm_before()
        xv = x_ref[...]
        rstd = lax.rsqrt(jnp.mean(xv * xv, axis=-1, keepdims=True) + EPS)
        h = (xv * rstd * g_ref[...]) * (1.0 + sc_ref[...]) + sh_ref[...]
        hb = h.astype(BF16)
        h_ref[...] = hb
        for j in range(0, NP, 512):
            w = min(512, NP - j)
            acc = _dot(hb, w_ref[:, j:j + w])
            if j < NP_F32:
                pf_ref[:, j:j + w] = acc
            else:
                pb_ref[:, j - NP_F32:j - NP_F32 + w] = acc.astype(BF16)
        comm_after()

    (g_pre, g_spec), (scale, sc_spec), (shift, sh_spec) = _rowvec(g_pre), _rowvec(scale), _rowvec(shift)
    return pl.pallas_call(
        body, name="prenorm_proj_comm" if comm else "prenorm_proj", grid=(s_len // ts,),
        out_shape=[jax.ShapeDtypeStruct((s_len, NP_F32), F32), jax.ShapeDtypeStruct((s_len, NP_BF16), BF16),
                   jax.ShapeDtypeStruct((s_len, D_MODEL), BF16)] + c_shapes,
        in_specs=[pl.BlockSpec((ts, D_MODEL), lambda i: (i, 0)), g_spec, sc_spec, sh_spec,
                  pl.BlockSpec((D_MODEL, NP), lambda i: (0, 0))] + [ANY] * n_cin,
        out_specs=[pl.BlockSpec((ts, NP_F32), lambda i: (i, 0)), pl.BlockSpec((ts, NP_BF16), lambda i: (i, 0)),
                   pl.BlockSpec((ts, D_MODEL), lambda i: (i, 0))] + [ANY] * len(c_shapes),
        scratch_shapes=c_scratch,
        compiler_params=_params(("arbitrary",), 48),
    )(x, g_pre, scale, shift, w_new, *(comm[1] if comm else []))


GLA_GROUP = 8


def _gla_group_rows(t):
    return [pl.ds(pl.multiple_of((t * GLA_GROUP + j) * GLA_CHUNK, GLA_CHUNK), GLA_CHUNK) for j in range(GLA_GROUP)]


def _gla_chunks_common(q_ref, k_ref, lr_ref, wgu_ref, bgu_ref, rows_list):
    c = GLA_CHUNK
    ri = lax.broadcasted_iota(jnp.int32, (c, c), 0)
    ci = lax.broadcasted_iota(jnp.int32, (c, c), 1)
    tril = (ri >= ci).astype(F32)
    zs = [_dot(lr_ref[rows, :], wgu_ref[...]) + bgu_ref[...] for rows in rows_list]
    las = [_log_sigmoid(z) * (1.0 / GLA_TAU) for z in zs]
    bs = [jnp.dot(tril, la, precision=lax.Precision.HIGHEST, preferred_element_type=F32) for la in las]
    out = []
    for rows, z, b in zip(rows_list, zs, bs):
        q = q_ref[rows, :] * (GLA_DK ** -0.5)
        k = k_ref[rows, :]
        bl = b[c - 1:c, :]
        out.append(dict(z=z, b=b, bl=bl, qe=q * jnp.exp(b), ke=k * jnp.exp(-b), kend=k * jnp.exp(bl - b),
                        dec=jnp.exp(bl)))
    return out, ri, ci


def _head_lane_mask(hh):
    return (lax.broadcasted_iota(jnp.int32, (1, LANE), 1) // GLA_DK) == hh


def _state_block_mask():
    r = lax.broadcasted_iota(jnp.int32, (2 * GLA_DV, LANE), 0) // GLA_DV
    cc = lax.broadcasted_iota(jnp.int32, (2 * GLA_DV, LANE), 1) // GLA_DK
    return r == cc


def _gla_fwd(pf, pb, wgu, bgu, layer, comm=None):
    s_len = pf.shape[0]
    nc = s_len // GLA_CHUNK
    ncomm = len(comm[1]) if comm else 0

    def body(*refs):
        q_ref, k_ref, v_ref, lr_ref, wgu_ref, bgu_ref = refs[:6]
        cin, (o_ref, st_ref), cout = refs[6:6 + ncomm], refs[6 + ncomm:8 + ncomm], refs[8 + ncomm:8 + 2 * ncomm]
        qe_s, cs_s, dec_s = refs[8 + 2 * ncomm:11 + 2 * ncomm]
        comm_before, comm_after = _comm_hooks(comm, cin, cout, refs[11 + 2 * ncomm:], steps=2)
        comm_before()
        bd = _state_block_mask()

        def local(t, carry):
            rows_list = _gla_group_rows(t)
            cm, ri, ci = _gla_chunks_common(q_ref, k_ref, lr_ref, wgu_ref, bgu_ref, rows_list)
            vs = [v_ref[rows, :] for rows in rows_list]
            kebs = [c["ke"].astype(BF16) for c in cm]
            a = [[jnp.where(ri >= ci, _dot_nt(jnp.where(_head_lane_mask(hh), c["qe"], 0.0).astype(BF16), keb), 0.0)
                  .astype(BF16) for hh in range(2)] for c, keb in zip(cm, kebs)]
            oi = [[_dot(ah[hh], v[:, hh * GLA_DV:(hh + 1) * GLA_DV]) for hh in range(2)] for ah, v in zip(a, vs)]
            cs = [jnp.where(bd, _dot_tn(v, c["kend"].astype(BF16)), 0.0) for c, v in zip(cm, vs)]
            for j, (rows, c) in enumerate(zip(rows_list, cm)):
                n = t * GLA_GROUP + j
                o_ref[rows, :] = jnp.concatenate(oi[j], axis=1)
                qe_s[rows, :] = c["qe"].astype(BF16)
                cs_s[n] = cs[j]
                dec_s[n] = jnp.broadcast_to(c["dec"], (8, LANE))
            return carry

        lax.fori_loop(0, nc // GLA_GROUP, local, 0)

        def scan(n, st):
            st_ref[0, n] = st.astype(BF16)
            return dec_s[n][0:1, :] * st + cs_s[n]

        lax.fori_loop(0, nc, scan, jnp.zeros((2 * GLA_DV, LANE), F32))

        def inter(t, carry):
            rows_list = _gla_group_rows(t)
            add = [_dot_nt(qe_s[rows, :], st_ref[0, t * GLA_GROUP + j]) for j, rows in enumerate(rows_list)]
            for rows, av in zip(rows_list, add):
                o_ref[rows, :] = o_ref[rows, :] + av
            return carry

        lax.fori_loop(0, nc // GLA_GROUP, inter, 0)
        comm_after()

    return pl.pallas_call(
        body, name="gla_fwd_comm" if comm else "gla_fwd", grid=(2,),
        out_shape=[jax.ShapeDtypeStruct((s_len, GLA_HEADS * GLA_DV), F32),
                   jax.ShapeDtypeStruct((2, nc, 2 * GLA_DV, LANE), BF16)] + (_comm_out_shapes(*comm) if comm else []),
        in_specs=[pl.BlockSpec((s_len, LANE), lambda g: (0, COL_QA // LANE + g)),
                  pl.BlockSpec((s_len, LANE), lambda g: (0, COL_KA // LANE + g)),
                  pl.BlockSpec((s_len, 2 * GLA_DV), lambda g: (0, (COL_VA - NP_F32) // (2 * GLA_DV) + g)),
                  pl.BlockSpec((s_len, LANE), lambda g: (0, (COL_LR - NP_F32) // LANE)),
                  pl.BlockSpec((None, LANE, LANE), lambda g: (layer, 0, g)),
                  pl.BlockSpec((None, 1, LANE), lambda g: (layer, 0, g))] + [ANY] * ncomm,
        out_specs=[pl.BlockSpec((s_len, 2 * GLA_DV), lambda g: (0, g)),
                   pl.BlockSpec((1, nc, 2 * GLA_DV, LANE), lambda g: (g, 0, 0, 0))] + [ANY] * ncomm,
        scratch_shapes=[pltpu.VMEM((s_len, LANE), BF16), pltpu.VMEM((nc, 2 * GLA_DV, LANE), F32),
                        pltpu.VMEM((nc, 8, LANE), F32)] + (_comm_scratch(ncomm) if comm else []),
        compiler_params=_params(("arbitrary",), 56),
    )(pf, pf, pb, pb, wgu, bgu.reshape(bgu.shape[0], 1, GU_COLS), *(comm[1] if comm else []))


def _rope_tables(s_len):
    inv_freq = ROPE_THETA ** (-jnp.arange(0, DIL_HD, 2, dtype=F32) / DIL_HD)
    ang = jnp.arange(s_len, dtype=F32)[:, None] * inv_freq[None, :]
    cos, sin = jnp.cos(ang), jnp.sin(ang)
    return jnp.concatenate([cos, cos], axis=1), jnp.concatenate([-sin, sin], axis=1)


def _rope(xv, cos, sin_signed):
    return xv * cos + pltpu.roll(xv, DIL_HD // 2, 1) * sin_signed


DIL_GROUP = 8


def _dil_pair_block(i, half, d, nblk, group=DIL_GROUP):
    nb = nblk // d
    j = i + half * (nblk // group)
    if nb >= 2 * group:
        r, n = j % d, j // d
    else:
        r, n = j // nb, j % nb
    kb = jnp.maximum(n - 1, 0)
    qs = r + d * DIL_BLOCK * n
    ks = r + d * DIL_BLOCK * kb
    return qs, ks, jnp.minimum(n, 1)


def _dil_fill_bias(bias):
    qi = lax.broadcasted_iota(jnp.int32, (DIL_BLOCK, 2 * DIL_BLOCK), 0)
    kj = lax.broadcasted_iota(jnp.int32, (DIL_BLOCK, 2 * DIL_BLOCK), 1)
    for sel in range(2):
        dist = qi - kj + DIL_BLOCK * sel
        bias[sel] = jnp.where((dist >= 0) & (dist <= DIL_BLOCK), 0.0, MASK_VALUE)


def _strided(start, size, d):
    return pl.ds(start, size) if d == 1 else pl.ds(start, size, stride=d)


def _comm_hooks(comm, cin, cout, csem, steps=DIL_HEADS):
    def before():
        if comm:
            @pl.when(pl.program_id(0) == 0)
            def _():
                _comm_run(comm[0], ("start",), cin, cout, *csem)

            if comm[0] == "gather":
                @pl.when(pl.program_id(0) == steps - 1)
                def _():
                    _comm_run(comm[0], ("forward",), cin, cout, *csem)

            if comm[0] == "pairsum_exchange":
                @pl.when(pl.program_id(0) == (1 if steps <= 4 else 3))
                def _():
                    _comm_run(comm[0], ("reduce", "send"), cin, cout, *csem)

    def after():
        if comm:
            @pl.when(pl.program_id(0) == steps - 1)
            def _():
                _comm_run(comm[0], ("finish",), cin, cout, *csem)

    return before, after


def _dil_fwd(pf, pb, cos, sin_signed, comm=None):
    s_len = pf.shape[0]
    nblk = s_len // DIL_BLOCK
    prep_rows = 256
    scale = DIL_HD ** -0.5
    nc = len(comm[1]) if comm else 0

    def body(*refs):
        q_ref, k_ref, v_ref, cos_ref, sin_ref = refs[:5]
        cin, (o_ref, lse_ref), cout = refs[5:5 + nc], refs[5 + nc:7 + nc], refs[7 + nc:7 + 2 * nc]
        qf, kf, vf, o0, o1, o2, l0, l1, l2, bias = refs[7 + 2 * nc:17 + 2 * nc]
        comm_before, comm_after = _comm_hooks(comm, cin, cout, refs[17 + 2 * nc:])
        comm_before()
        _dil_fill_bias(bias)

        def prep(t, carry):
            rows = pl.ds(pl.multiple_of(t * prep_rows, prep_rows), prep_rows)
            cs, sn = cos_ref[rows, :], sin_ref[rows, :]
            qf[rows, :] = _rope(q_ref[rows, :], cs, sn)
            kf[rows, :] = _rope(k_ref[rows, :], cs, sn)
            vf[rows, :] = v_ref[rows, :].astype(F32)
            return carry

        lax.fori_loop(0, s_len // prep_rows, prep, 0)
        for d, o_p, l_p in zip(DIL_DILATIONS, (o0, o1, o2), (l0, l1, l2)):
            if nblk // d == 2:
                units = DIL_GROUP // 2

                def whole(i, carry, d=d, o_p=o_p, l_p=l_p, units=units):
                    rows = [_strided(i + u * (d // units), 2 * DIL_BLOCK, d) for u in range(units)]
                    ld = [(qf[rw, :].astype(BF16), kf[rw, :].astype(BF16), vf[rw, :].astype(BF16)) for rw in rows]
                    both = bias[...].reshape(2 * DIL_BLOCK, 2 * DIL_BLOCK)
                    s = [_dot_nt(qb, kk) * scale + both for qb, kk, _ in ld]
                    m = [jnp.max(sv, axis=-1, keepdims=True) for sv in s]
                    p = [jnp.exp(sv - mv) for sv, mv in zip(s, m)]
                    den = [jnp.sum(pv, axis=-1, keepdims=True) for pv in p]
                    r = [_dot(pv.astype(BF16), vv) for pv, (_, _, vv) in zip(p, ld)]
                    for rv, dv, mv, rw in zip(r, den, m, rows):
                        o_p[rw, :] = rv / dv
                        l_p[rw, :] = jnp.broadcast_to(mv + jnp.log(dv), (2 * DIL_BLOCK, DIL_HD))
                    return carry

                lax.fori_loop(0, d // units, whole, 0)
                continue

            def pair(i, carry, d=d, o_p=o_p, l_p=l_p):
                idx = [_dil_pair_block(i, half, d, nblk, DIL_GROUP) for half in range(DIL_GROUP)]
                ld = [(qf[_strided(qs, DIL_BLOCK, d), :].astype(BF16),
                       kf[_strided(ks, 2 * DIL_BLOCK, d), :].astype(BF16),
                       vf[_strided(ks, 2 * DIL_BLOCK, d), :].astype(BF16)) for qs, ks, _ in idx]
                s = [_dot_nt(qb, kk) * scale + bias[sel] for (qb, kk, _), (_, _, sel) in zip(ld, idx)]
                m = [jnp.max(sv, axis=-1, keepdims=True) for sv in s]
                p = [jnp.exp(sv - mv) for sv, mv in zip(s, m)]
                den = [jnp.sum(pv, axis=-1, keepdims=True) for pv in p]
                r = [_dot(pv.astype(BF16), vv) for pv, (_, _, vv) in zip(p, ld)]
                for rv, dv, mv, (qs, _, _) in zip(r, den, m, idx):
                    o_p[_strided(qs, DIL_BLOCK, d), :] = rv / dv
                    l_p[_strided(qs, DIL_BLOCK, d), :] = jnp.broadcast_to(mv + jnp.log(dv), (DIL_BLOCK, DIL_HD))
                return carry

            lax.fori_loop(0, nblk // DIL_GROUP, pair, 0)

        def comb(t, carry):
            rows = pl.ds(pl.multiple_of(t * prep_rows, prep_rows), prep_rows)
            a0, a1, a2 = l0[rows, :], l1[rows, :], l2[rows, :]
            m = jnp.maximum(jnp.maximum(a0, a1), a2)
            e0, e1, e2 = jnp.exp(a0 - m), jnp.exp(a1 - m), jnp.exp(a2 - m)
            tot = e0 + e1 + e2
            o_ref[rows, :] = (e0 * o0[rows, :] + e1 * o1[rows, :] + e2 * o2[rows, :]) / tot
            lse_ref[rows, :] = m + jnp.log(tot)
            return carry

        lax.fori_loop(0, s_len // prep_rows, comb, 0)
        comm_after()

    head = lambda base: pl.BlockSpec((s_len, DIL_HD), lambda h: (0, base // DIL_HD + h))
    table = pl.BlockSpec((s_len, DIL_HD), lambda h: (0, 0))
    out = pl.BlockSpec((s_len, DIL_HD), lambda h: (0, h))
    shp = jax.ShapeDtypeStruct((s_len, DIL_HEADS * DIL_HD), F32)
    return pl.pallas_call(
        body, name="dil_fwd_comm" if comm else "dil_fwd", grid=(DIL_HEADS,),
        out_shape=[shp, shp] + (_comm_out_shapes(*comm) if comm else []),
        in_specs=[head(COL_QB), head(COL_KB), head(COL_VB - NP_F32), table, table] + [ANY] * nc,
        out_specs=[out, out] + [ANY] * nc,
        scratch_shapes=[pltpu.VMEM((s_len, DIL_HD), F32) for _ in range(9)]
        + [pltpu.VMEM((2, DIL_BLOCK, 2 * DIL_BLOCK), F32)] + (_comm_scratch(nc) if comm else []),
        compiler_params=_params(("arbitrary",), 56),
    )(pf, pf, pb, cos, sin_signed, *(comm[1] if comm else []))


def _silu_and_grad(z):
    sg = _sigmoid(z)
    return z * sg, sg * (1.0 + z * (1.0 - sg))


def _post_fwd(o_a, o_b, pf, g_heads, w_out, x, gate, g_post, target=None, ts=256):
    s_len = x.shape[0]
    half = GLA_HEADS * GLA_DV
    last = target is not None

    def body(*refs):
        oa_ref, ob_ref, z_ref, gh_ref, w_ref, x_ref, gate_ref, gp_ref = refs[:8]
        xo_ref, y_ref, u_ref = refs[8 + last:11 + last]
        for src, base in ((oa_ref, 0), (ob_ref, half)):
            for hh in range(4):
                lo = hh * LANE
                og = src[:, lo:lo + LANE]
                on = og * lax.rsqrt(jnp.mean(og * og, axis=-1, keepdims=True) + EPS)
                zg = z_ref[:, base + lo:base + lo + LANE].astype(F32)
                y_ref[:, base + lo:base + lo + LANE] = (on * gh_ref[:, base + lo:base + lo + LANE]
                                                        * (zg * _sigmoid(zg))).astype(BF16)
        u = _dot(y_ref[...], w_ref[...])
        u_ref[...] = u.astype(BF16)
        rstd = lax.rsqrt(jnp.mean(u * u, axis=-1, keepdims=True) + EPS)
        x_out = x_ref[...] + gate_ref[...] * (u * rstd * gp_ref[...])
        if last:
            t_ref, loss_ref = refs[8], refs[12]

            @pl.when(pl.program_id(0) == 0)
            def _():
                loss_ref[...] = jnp.zeros_like(loss_ref)

            e = x_out - t_ref[...]
            xo_ref[...] = e * (1.0 / D_MODEL)
            loss_ref[...] += 0.5 * jnp.sum(jnp.mean(e * e, axis=-1, keepdims=True))
        else:
            xo_ref[...] = x_out

    (g_heads, gh_spec), (gate, gate_spec), (g_post, gp_spec) = _rowvec(g_heads), _rowvec(gate), _rowvec(g_post)
    tile = pl.BlockSpec((ts, D_MODEL), lambda i: (i, 0))
    halft = pl.BlockSpec((ts, half), lambda i: (i, 0))
    return pl.pallas_call(
        body, name="post_fwd_loss" if last else "post_fwd", grid=(s_len // ts,),
        out_shape=[jax.ShapeDtypeStruct((s_len, D_MODEL), F32), jax.ShapeDtypeStruct((s_len, D_MODEL), BF16),
                   jax.ShapeDtypeStruct((s_len, D_MODEL), BF16)]
        + ([jax.ShapeDtypeStruct((8, LANE), F32)] if last else []),
        in_specs=[halft, halft, tile, gh_spec, pl.BlockSpec((D_MODEL, D_MODEL), lambda i: (0, 0)), tile, gate_spec,
                  gp_spec] + ([tile] if last else []),
        out_specs=[tile, tile, tile] + ([pl.BlockSpec((8, LANE), lambda i: (0, 0))] if last else []),
        compiler_params=_params(("arbitrary",), 40),
    )(o_a, o_b, pf, g_heads, w_out, x, gate, g_post, *([target] if last else []))


def _post_bwd(dxo, u, gate, g_post, w_out, o_a, o_b, pf, g_heads, ts=256):
    s_len = dxo.shape[0]
    half = GLA_HEADS * GLA_DV

    def body(dx_ref, u_ref, gate_ref, gp_ref, w_ref, oa_ref, ob_ref, z_ref, gh_ref, du_ref, do_ref, dz_ref, sums_ref):
        @pl.when(pl.program_id(0) == 0)
        def _():
            sums_ref[...] = jnp.zeros_like(sums_ref)

        dx = dx_ref[...]
        u = u_ref[...].astype(F32)
        rstd = lax.rsqrt(jnp.mean(u * u, axis=-1, keepdims=True) + EPS)
        un = u * rstd
        sums_ref[0:1, :] += jnp.sum(dx * (un * gp_ref[...]), axis=0, keepdims=True)
        drn = dx * gate_ref[...]
        sums_ref[1:2, :] += jnp.sum(drn * un, axis=0, keepdims=True)
        dun = drn * gp_ref[...]
        du = rstd * (dun - un * jnp.mean(dun * un, axis=-1, keepdims=True))
        dub = du.astype(BF16)
        du_ref[...] = dub
        dy = _dot_nt(dub, w_ref[...])
        for src, base in ((oa_ref, 0), (ob_ref, half)):
            for hh in range(4):
                lo = base + hh * LANE
                og = src[:, hh * LANE:(hh + 1) * LANE]
                rs = lax.rsqrt(jnp.mean(og * og, axis=-1, keepdims=True) + EPS)
                on = og * rs
                zg = z_ref[:, lo:lo + LANE].astype(F32)
                sz, dsz = _silu_and_grad(zg)
                gg = gh_ref[:, lo:lo + LANE]
                dyg = dy[:, lo:lo + LANE]
                sums_ref[2:3, lo:lo + LANE] += jnp.sum(dyg * sz * on, axis=0, keepdims=True)
                dz_ref[:, lo:lo + LANE] = (dyg * on * gg * dsz).astype(BF16)
                don = dyg * gg * sz
                do_ref[:, lo:lo + LANE] = (rs * (don - on * jnp.mean(don * on, axis=-1, keepdims=True))).astype(BF16)

    (g_heads, gh_spec), (gate, gate_spec), (g_post, gp_spec) = _rowvec(g_heads), _rowvec(gate), _rowvec(g_post)
    tile = pl.BlockSpec((ts, D_MODEL), lambda i: (i, 0))
    halft = pl.BlockSpec((ts, half), lambda i: (i, 0))
    return pl.pallas_call(
        body, name="post_bwd", grid=(s_len // ts,),
        out_shape=(jax.ShapeDtypeStruct((s_len, D_MODEL), BF16), jax.ShapeDtypeStruct((s_len, D_MODEL), BF16),
                   jax.ShapeDtypeStruct((s_len, D_MODEL), BF16), jax.ShapeDtypeStruct((8, D_MODEL), F32)),
        in_specs=[tile, tile, gate_spec, gp_spec, pl.BlockSpec((D_MODEL, D_MODEL), lambda i: (0, 0)), halft, halft,
                  tile, gh_spec],
        out_specs=(tile, tile, tile, pl.BlockSpec((8, D_MODEL), lambda i: (0, 0))),
        compiler_params=_params(("arbitrary",), 40),
    )(dxo, u, gate, g_post, w_out, o_a, o_b, pf, g_heads)


def _gla_bwd(pf, pb, wgu, bgu, layer, states, do, comm=None):
    s_len = pf.shape[0]
    nc = s_len // GLA_CHUNK
    c = GLA_CHUNK
    n_cin, c_shapes, c_scratch = _comm_plumbing(comm)

    def body(*refs):
        ((q_ref, k_ref, v_ref, lr_ref, wgu_ref, bgu_ref, st_ref, do_ref),
         (dq_ref, dk_ref, dv_ref, dlr_ref, dwgu_ref, dbgu_ref), (ds_s, dec_s, dw_acc, db_acc),
         cin, cout, csem) = _split_refs(refs, 8, 6, 4, comm)
        comm_before, comm_after = _comm_hooks(comm, cin, cout, csem, steps=2)
        comm_before()
        dw_acc[...] = jnp.zeros_like(dw_acc)
        db_acc[...] = jnp.zeros_like(db_acc)
        bd = _state_block_mask()
        last_row = lax.broadcasted_iota(jnp.int32, (c, LANE), 0) == c - 1

        def local(t, carry):
            rows_list = _gla_group_rows(t)
            cm, _, _ = _gla_chunks_common(q_ref, k_ref, lr_ref, wgu_ref, bgu_ref, rows_list)
            loc = [jnp.where(bd, _dot_tn(do_ref[rows, :], cc["qe"].astype(BF16)), 0.0)
                   for rows, cc in zip(rows_list, cm)]
            for j, cc in enumerate(cm):
                ds_s[t * GLA_GROUP + j] = loc[j]
                dec_s[t * GLA_GROUP + j] = jnp.broadcast_to(cc["dec"], (8, LANE))
            return carry

        lax.fori_loop(0, nc // GLA_GROUP, local, 0)

        def scan(t, dst):
            n = nc - 1 - t
            loc = ds_s[n]
            ds_s[n] = dst
            return dec_s[n][0:1, :] * dst + loc

        lax.fori_loop(0, nc, scan, jnp.zeros((2 * GLA_DV, LANE), F32))

        def rest(t, carry):
            rows_list = _gla_group_rows(t)
            cm, ri, ci = _gla_chunks_common(q_ref, k_ref, lr_ref, wgu_ref, bgu_ref, rows_list)
            ns = [t * GLA_GROUP + j for j in range(GLA_GROUP)]
            vs = [v_ref[rows, :] for rows in rows_list]
            dobs = [do_ref[rows, :] for rows in rows_list]
            stbs = [st_ref[0, n] for n in ns]
            dsts = [ds_s[n] for n in ns]
            dstbs = [d.astype(BF16) for d in dsts]
            qebs = [cc["qe"].astype(BF16) for cc in cm]
            kebs = [cc["ke"].astype(BF16) for cc in cm]
            kendbs = [cc["kend"].astype(BF16) for cc in cm]
            hms = [_head_lane_mask(hh) for hh in range(2)]
            qehs = [[jnp.where(hm, cc["qe"], 0.0).astype(BF16) for hm in hms] for cc in cm]
            kehs = [[jnp.where(hm, cc["ke"], 0.0).astype(BF16) for hm in hms] for cc in cm]
            heads = lambda x: [x[:, hh * GLA_DV:(hh + 1) * GLA_DV] for hh in range(2)]
            vhs, dohs = [heads(v) for v in vs], [heads(d) for d in dobs]

            dqe0 = [_dot(dob, stb) for dob, stb in zip(dobs, stbs)]
            dkend = [_dot(v, dstb) for v, dstb in zip(vs, dstbs)]
            dv0 = [_dot_nt(kb, dstb) for kb, dstb in zip(kendbs, dstbs)]
            a_t = [[jnp.where(ci >= ri, _dot_nt(kehs[j][hh], qebs[j]), 0.0).astype(BF16) for hh in range(2)]
                   for j in range(GLA_GROUP)]
            da = [[jnp.where(ri >= ci, _dot_nt(dohs[j][hh], vhs[j][hh]), 0.0).astype(BF16) for hh in range(2)]
                  for j in range(GLA_GROUP)]
            da_t = [[jnp.where(ci >= ri, _dot_nt(vhs[j][hh], dohs[j][hh]), 0.0).astype(BF16) for hh in range(2)]
                    for j in range(GLA_GROUP)]
            dv1 = [[_dot(a_t[j][hh], dohs[j][hh]) for hh in range(2)] for j in range(GLA_GROUP)]
            dqe1 = [[_dot(da[j][hh], kebs[j]) for hh in range(2)] for j in range(GLA_GROUP)]
            dke1 = [[_dot(da_t[j][hh], qehs[j][hh]) for hh in range(2)] for j in range(GLA_GROUP)]

            dbs, dzs = [], []
            for j, (rows, cc) in enumerate(zip(rows_list, cm)):
                qe, ke, kend, b, bl = cc["qe"], cc["ke"], cc["kend"], cc["b"], cc["bl"]
                dqe = dqe0[j] + jnp.where(hms[0], dqe1[j][0], 0.0) + jnp.where(hms[1], dqe1[j][1], 0.0)
                dke = jnp.where(hms[0], dke1[j][0], 0.0) + jnp.where(hms[1], dke1[j][1], 0.0)
                dv_ref[rows, :] = (dv0[j] + jnp.concatenate(dv1[j], axis=1)).astype(BF16)
                dq_ref[rows, :] = (dqe * jnp.exp(b) * (GLA_DK ** -0.5)).astype(BF16)
                dk_ref[rows, :] = (dke * jnp.exp(-b) + dkend[j] * jnp.exp(bl - b)).astype(BF16)
                ddec = jnp.sum(dsts[j] * stbs[j].astype(F32), axis=0, keepdims=True)
                dbl = jnp.sum(dkend[j] * kend, axis=0, keepdims=True) + ddec * cc["dec"]
                dbs.append(dqe * qe - dke * ke - dkend[j] * kend + jnp.where(last_row, dbl, 0.0))
            triu = (ci >= ri).astype(F32)
            dlas = [jnp.dot(triu, db, precision=lax.Precision.HIGHEST, preferred_element_type=F32) for db in dbs]
            dzs = [dla * (1.0 / GLA_TAU) * _sigmoid(-cc["z"]) for dla, cc in zip(dlas, cm)]
            dzbs = [dz.astype(BF16) for dz in dzs]
            dlrs = [_dot_nt(dzb, wgu_ref[...]) for dzb in dzbs]
            dws = [_dot_tn(lr_ref[rows, :], dzb) for rows, dzb in zip(rows_list, dzbs)]
            for rows, dlr in zip(rows_list, dlrs):
                dlr_ref[0, rows, :] = dlr
            dw_acc[...] += functools.reduce(lambda x, y: x + y, dws)
            db_acc[0:1, :] += jnp.sum(functools.reduce(lambda x, y: x + y, dzs), axis=0, keepdims=True)
            return carry

        lax.fori_loop(0, nc // GLA_GROUP, rest, 0)
        dwgu_ref[...] = dw_acc[...]
        dbgu_ref[...] = db_acc[...]
        comm_after()

    pair = pl.BlockSpec((s_len, LANE), lambda g: (0, g))
    return pl.pallas_call(
        body, name="gla_bwd_comm" if comm else "gla_bwd", grid=(2,),
        out_shape=[jax.ShapeDtypeStruct((s_len, GU_COLS), BF16), jax.ShapeDtypeStruct((s_len, GU_COLS), BF16),
                   jax.ShapeDtypeStruct((s_len, GLA_HEADS * GLA_DV), BF16),
                   jax.ShapeDtypeStruct((2, s_len, LANE), F32),
                   jax.ShapeDtypeStruct((LANE, GU_COLS), F32), jax.ShapeDtypeStruct((8, GU_COLS), F32)] + c_shapes,
        in_specs=[pl.BlockSpec((s_len, LANE), lambda g: (0, COL_QA // LANE + g)),
                  pl.BlockSpec((s_len, LANE), lambda g: (0, COL_KA // LANE + g)),
                  pl.BlockSpec((s_len, 2 * GLA_DV), lambda g: (0, (COL_VA - NP_F32) // (2 * GLA_DV) + g)),
                  pl.BlockSpec((s_len, LANE), lambda g: (0, (COL_LR - NP_F32) // LANE)),
                  pl.BlockSpec((None, LANE, LANE), lambda g: (layer, 0, g)),
                  pl.BlockSpec((None, 1, LANE), lambda g: (layer, 0, g)),
                  pl.BlockSpec((1, nc, 2 * GLA_DV, LANE), lambda g: (g, 0, 0, 0)),
                  pl.BlockSpec((s_len, 2 * GLA_DV), lambda g: (0, g))] + [ANY] * n_cin,
        out_specs=[pair, pair, pl.BlockSpec((s_len, 2 * GLA_DV), lambda g: (0, g)),
                   pl.BlockSpec((1, s_len, LANE), lambda g: (g, 0, 0)),
                   pl.BlockSpec((LANE, LANE), lambda g: (0, g)), pl.BlockSpec((8, LANE), lambda g: (0, g))]
        + [ANY] * len(c_shapes),
        scratch_shapes=[pltpu.VMEM((nc, 2 * GLA_DV, LANE), F32), pltpu.VMEM((nc, 8, LANE), F32),
                        pltpu.VMEM((LANE, LANE), F32), pltpu.VMEM((8, LANE), F32)] + c_scratch,
        compiler_params=_params(("arbitrary",), 56),
    )(pf, pf, pb, pb, wgu, bgu.reshape(bgu.shape[0], 1, GU_COLS), states, do, *(comm[1] if comm else []))


def _dil_bwd(pf, pb, cos, sin_signed, do, o_b, lse, comm=None):
    s_len = pf.shape[0]
    nblk = s_len // DIL_BLOCK
    prep_rows = 256
    scale = DIL_HD ** -0.5
    nc = len(comm[1]) if comm else 0

    def body(*refs):
        ((q_ref, k_ref, v_ref, cos_ref, sin_ref, do_ref, o_ref, lse_ref), (dq_ref, dk_ref, dv_ref),
         (qf, kf, vf, dof, dl, dqa, dka, dva, bias), cin, cout, csem) = _split_refs(refs, 8, 3, 9, comm)
        comm_before, comm_after = _comm_hooks(comm, cin, cout, csem)
        comm_before()
        _dil_fill_bias(bias)

        def prep(t, carry):
            rows = pl.ds(pl.multiple_of(t * prep_rows, prep_rows), prep_rows)
            cs, sn = cos_ref[rows, :], sin_ref[rows, :]
            qf[rows, :] = _rope(q_ref[rows, :], cs, sn) * scale
            kf[rows, :] = _rope(k_ref[rows, :], cs, sn)
            vf[rows, :] = v_ref[rows, :].astype(F32)
            dov = do_ref[rows, :].astype(F32)
            dof[rows, :] = dov
            dl[rows, :] = jnp.broadcast_to(jnp.sum(dov * o_ref[rows, :], axis=-1, keepdims=True), (prep_rows, DIL_HD))
            zero = jnp.zeros((prep_rows, DIL_HD), F32)
            dqa[rows, :] = zero
            dka[rows, :] = zero
            dva[rows, :] = zero
            return carry

        lax.fori_loop(0, s_len // prep_rows, prep, 0)

        for d in DIL_DILATIONS:
            if nblk // d == 2:
                units = DIL_GROUP // 2

                def whole(i, carry, d=d, units=units):
                    rows = [_strided(i + u * (d // units), 2 * DIL_BLOCK, d) for u in range(units)]
                    ld = [(qf[rw, :].astype(BF16), kf[rw, :].astype(BF16), vf[rw, :].astype(BF16),
                           dof[rw, :].astype(BF16)) for rw in rows]
                    both = bias[...].reshape(2 * DIL_BLOCK, 2 * DIL_BLOCK)
                    s = [_dot_nt(qb, kk) + both for qb, kk, _, _ in ld]
                    dp = [_dot_nt(dob, vv) for _, _, vv, dob in ld]
                    p = [jnp.exp(sv - lse_ref[rw, :][:, 0:1]) for sv, rw in zip(s, rows)]
                    ds = [(pv * (dpv - dl[rw, :][:, 0:1])).astype(BF16) for pv, dpv, rw in zip(p, dp, rows)]
                    pb = [pv.astype(BF16) for pv in p]
                    gq = [_dot(dsv, kk) for dsv, (_, kk, _, _) in zip(ds, ld)]
                    gk = [_dot_tn(dsv, qb) for dsv, (qb, _, _, _) in zip(ds, ld)]
                    gv = [_dot_tn(pv, dob) for pv, (_, _, _, dob) in zip(pb, ld)]
                    for rw, a, b, c in zip(rows, gq, gk, gv):
                        dqa[rw, :] += a
                        dka[rw, :] += b
                        dva[rw, :] += c
                    return carry

                lax.fori_loop(0, d // units, whole, 0)
                continue

            def pair(i, carry, d=d):
                idx = [_dil_pair_block(i, half, d, nblk) for half in range(DIL_GROUP)]
                rows = [(_strided(qs, DIL_BLOCK, d), _strided(ks, 2 * DIL_BLOCK, d)) for qs, ks, _ in idx]
                ld = [(qf[qr, :].astype(BF16), kf[kr, :].astype(BF16), vf[kr, :].astype(BF16),
                       dof[qr, :].astype(BF16)) for qr, kr in rows]
                s = [_dot_nt(qb, kk) + bias[sel] for (qb, kk, _, _), (_, _, sel) in zip(ld, idx)]
                dp = [_dot_nt(dob, vv) for _, _, vv, dob in ld]
                p = [jnp.exp(sv - lse_ref[qr, :][:, 0:1]) for sv, (qr, _) in zip(s, rows)]
                ds = [(pv * (dpv - dl[qr, :][:, 0:1])).astype(BF16) for pv, dpv, (qr, _) in zip(p, dp, rows)]
                pb = [pv.astype(BF16) for pv in p]
                gq = [_dot(dsv, kk) for dsv, (_, kk, _, _) in zip(ds, ld)]
                gk = [_dot_tn(dsv, qb) for dsv, (qb, _, _, _) in zip(ds, ld)]
                gv = [_dot_tn(pv, dob) for pv, (_, _, _, dob) in zip(pb, ld)]
                for (qr, kr), a, b, c in zip(rows, gq, gk, gv):
                    dqa[qr, :] += a
                    dka[kr, :] += b
                    dva[kr, :] += c
                return carry

            lax.fori_loop(0, nblk // DIL_GROUP, pair, 0)

        def fin(t, carry):
            rows = pl.ds(pl.multiple_of(t * prep_rows, prep_rows), prep_rows)
            cs, sn = cos_ref[rows, :], sin_ref[rows, :]
            gq, gk = dqa[rows, :] * scale, dka[rows, :]
            dq_ref[rows, :] = (gq * cs - pltpu.roll(gq, DIL_HD // 2, 1) * sn).astype(BF16)
            dk_ref[rows, :] = (gk * cs - pltpu.roll(gk, DIL_HD // 2, 1) * sn).astype(BF16)
            dv_ref[rows, :] = dva[rows, :].astype(BF16)
            return carry

        lax.fori_loop(0, s_len // prep_rows, fin, 0)
        comm_after()

    head = lambda base: pl.BlockSpec((s_len, DIL_HD), lambda h: (0, base // DIL_HD + h))
    table = pl.BlockSpec((s_len, DIL_HD), lambda h: (0, 0))
    out = pl.BlockSpec((s_len, DIL_HD), lambda h: (0, h))
    shp = jax.ShapeDtypeStruct((s_len, DIL_HEADS * DIL_HD), BF16)
    return pl.pallas_call(
        body, name="dil_bwd_comm" if comm else "dil_bwd", grid=(DIL_HEADS,),
        out_shape=[shp, shp, shp] + (_comm_out_shapes(*comm) if comm else []),
        in_specs=[head(COL_QB), head(COL_KB), head(COL_VB - NP_F32), table, table,
                  pl.BlockSpec((s_len, DIL_HD), lambda h: (0, DIL_HEADS + h)), out, out] + [ANY] * nc,
        out_specs=[out, out, out] + [ANY] * len(_comm_plumbing(comm)[1]),
        scratch_shapes=[pltpu.VMEM((s_len, DIL_HD), F32) for _ in range(8)]
        + [pltpu.VMEM((2, DIL_BLOCK, 2 * DIL_BLOCK), F32)] + (_comm_scratch(nc) if comm else []),
        compiler_params=_params(("arbitrary",), 56),
    )(pf, pf, pb, cos, sin_signed, do, o_b, lse, *(comm[1] if comm else []))


_PIECES = ((COL_Z, 1024), (COL_QA, 256), (COL_KA, 256), (COL_QB, 512), (COL_KB, 512), (COL_VA, 512), (COL_VB, 512),
           (COL_LR, 128))


def _in_bwd(pieces, w_new, x, dxo, g_pre, scale, comm=None, ts=256):
    s_len = x.shape[0]
    nc = len(comm[1]) if comm else 0
    nco = len(_comm_out_shapes(*comm)) if comm else 0
    npc = len(_PIECES)

    def body(*refs):
        p_refs = refs[:npc]
        w_ref, x_ref, dxo_ref, g_ref, sc_ref = refs[npc:npc + 5]
        cin, (dx_ref, sums_ref), cout = (refs[npc + 5:npc + 5 + nc], refs[npc + 5 + nc:npc + 7 + nc],
                                         refs[npc + 7 + nc:npc + 7 + nc + nco])
        comm_before, comm_after = _comm_hooks(comm, cin, cout, refs[npc + 7 + nc + nco:], steps=s_len // ts)
        comm_before()

        @pl.when(pl.program_id(0) == 0)
        def _():
            sums_ref[...] = jnp.zeros_like(sums_ref)

        dh = jnp.zeros((ts, D_MODEL), F32)
        for p_ref, (col, width) in zip(p_refs, _PIECES):
            dh += _dot_nt(p_ref[...], w_ref[:, col:col + width])
        xv = x_ref[...]
        rstd = lax.rsqrt(jnp.mean(xv * xv, axis=-1, keepdims=True) + EPS)
        xn = xv * rstd
        sums_ref[0:1, :] += jnp.sum(dh, axis=0, keepdims=True)
        sums_ref[1:2, :] += jnp.sum(dh * (xn * g_ref[...]), axis=0, keepdims=True)
        dr = dh * (1.0 + sc_ref[...])
        sums_ref[2:3, :] += jnp.sum(dr * xn, axis=0, keepdims=True)
        dxn = dr * g_ref[...]
        dx_ref[...] = dxo_ref[...] + rstd * (dxn - xn * jnp.mean(dxn * xn, axis=-1, keepdims=True))
        comm_after()

    (g_pre, g_spec), (scale, sc_spec) = _rowvec(g_pre), _rowvec(scale)
    tile = pl.BlockSpec((ts, D_MODEL), lambda i: (i, 0))
    return pl.pallas_call(
        body, name="in_bwd_comm" if comm else "in_bwd", grid=(s_len // ts,),
        out_shape=[jax.ShapeDtypeStruct((s_len, D_MODEL), F32), jax.ShapeDtypeStruct((8, D_MODEL), F32)]
        + (_comm_out_shapes(*comm) if comm else []),
        in_specs=[pl.BlockSpec((ts, width), lambda i: (i, 0)) for _, width in _PIECES]
        + [pl.BlockSpec((D_MODEL, NP), lambda i: (0, 0)), tile, tile, g_spec, sc_spec] + [ANY] * nc,
        out_specs=[tile, pl.BlockSpec((8, D_MODEL), lambda i: (0, 0))] + [ANY] * nco,
        scratch_shapes=_comm_scratch(nc) if comm else [],
        compiler_params=_params(("arbitrary",), 56),
    )(*pieces, w_new, x, dxo, g_pre, scale, *(comm[1] if comm else []))


def _w_in_to_kernel(gathered, tr=128):
    def body(g_ref, o_ref):
        cols = jnp.concatenate([g_ref[k].astype(F32) for k in range(N_DEV)], axis=1)
        pad = jnp.zeros((tr, LANE - GLA_LOWRANK), F32)
        o_ref[...] = jnp.concatenate(
            [cols[:, 1024:1536], cols[:, 3088:3600], cols[:, 0:512], cols[:, 1552:2576], cols[:, 512:1024],
             cols[:, 2576:3088], cols[:, 1536:1552], pad], axis=1).astype(BF16)

    return pl.pallas_call(
        body, name="w_in_to_kernel", grid=(D_MODEL // tr,), out_shape=jax.ShapeDtypeStruct((D_MODEL, NP), BF16),
        in_specs=[pl.BlockSpec((N_DEV, tr, W_IN_SHARD), lambda i: (0, i, 0))],
        out_specs=pl.BlockSpec((tr, NP), lambda i: (i, 0)),
        compiler_params=_params(("arbitrary",)),
    )(gathered)


def _grad_w_in(h, pieces, ts=512, tr=128):
    s_len = h.shape[0]
    steps = s_len // ts

    def body(*refs):
        h_ref, p_refs = refs[0], refs[1:1 + len(_PIECES)]
        o_ref, acc = refs[1 + len(_PIECES):]

        @pl.when(pl.program_id(0) == 0)
        def _():
            acc[...] = jnp.zeros_like(acc)

        hv = h_ref[...]
        for p_ref, (col, width) in zip(p_refs, _PIECES):
            acc[:, col:col + width] += _dot_tn(hv, p_ref[...])

        @pl.when(pl.program_id(0) == steps - 1)
        def _():
            def rows_out(t, carry):
                rows = pl.ds(pl.multiple_of(t * tr, tr), tr)
                g = acc[rows, :]
                cols = jnp.concatenate(
                    [g[:, COL_QA:COL_QB], g[:, COL_VA:COL_VB], g[:, 0:512], g[:, COL_LR:COL_LR + GLA_LOWRANK],
                     g[:, COL_QB:COL_VA], g[:, COL_VB:COL_LR], g[:, 512:1024]], axis=1)
                for k in range(N_DEV):
                    o_ref[k, rows, :] = cols[:, W_IN_SHARD * k:W_IN_SHARD * (k + 1)].astype(BF16)
                return carry

            lax.fori_loop(0, D_MODEL // tr, rows_out, 0)

    return pl.pallas_call(
        body, name="grad_w_in", grid=(steps,),
        out_shape=jax.ShapeDtypeStruct((N_DEV, D_MODEL, W_IN_SHARD), BF16),
        in_specs=[pl.BlockSpec((ts, D_MODEL), lambda i: (i, 0))]
        + [pl.BlockSpec((ts, width), lambda i: (i, 0)) for _, width in _PIECES],
        out_specs=pl.BlockSpec((N_DEV, D_MODEL, W_IN_SHARD), lambda i: (0, 0, 0)),
        scratch_shapes=[pltpu.VMEM((D_MODEL, NP), F32)],
        compiler_params=_params(("arbitrary",), 56),
    )(h, *pieces)


def _matmul_tn(a, b, name, bn, ts=512):
    s_len, m = a.shape
    n = b.shape[1]
    steps = s_len // ts

    def body(a_ref, b_ref, o_ref, acc):
        @pl.when(pl.program_id(1) == 0)
        def _():
            acc[...] = jnp.zeros_like(acc)

        acc[...] += _dot_tn(a_ref[...], b_ref[...])

        @pl.when(pl.program_id(1) == steps - 1)
        def _():
            o_ref[...] = acc[...].astype(BF16)

    return pl.pallas_call(
        body, name=name, grid=(n // bn, steps),
        out_shape=jax.ShapeDtypeStruct((m, n), BF16),
        in_specs=[pl.BlockSpec((ts, m), lambda j, i: (i, 0)), pl.BlockSpec((ts, bn), lambda j, i: (i, j))],
        out_specs=pl.BlockSpec((m, bn), lambda j, i: (0, j)),
        scratch_shapes=[pltpu.VMEM((m, bn), F32)],
        compiler_params=_params(("arbitrary", "arbitrary"), 40),
    )(a, b)


def _adam_math(w, g, m, v):
    m = ADAM_B1 * m + (1.0 - ADAM_B1) * g
    v = ADAM_B2 * v + (1.0 - ADAM_B2) * (g * g)
    m_hat = m / (1.0 - ADAM_B1 ** ADAM_STEP)
    v_hat = v / (1.0 - ADAM_B2 ** ADAM_STEP)
    delta = -ADAM_LR * (m_hat / (jnp.sqrt(v_hat) + ADAM_EPS) + ADAM_WD * w)
    return delta, m, v


def _adamw(w, parts, m, v, name, tr):
    r, cdim = w.shape
    n_parts = parts.shape[0]

    def body(w_ref, p_ref, m_ref, v_ref, g_ref, d_ref, nm_ref, nv_ref):
        g = p_ref[0].astype(F32)
        for k in range(1, n_parts):
            g = g + p_ref[k].astype(F32)
        g_ref[...] = g
        d_ref[...], nm_ref[...], nv_ref[...] = _adam_math(w_ref[...], g, m_ref[...], v_ref[...])

    tile = pl.BlockSpec((tr, cdim), lambda i: (i, 0))
    shp = jax.ShapeDtypeStruct((r, cdim), F32)
    return pl.pallas_call(
        body, name=name, grid=(r // tr,), out_shape=(shp, shp, shp, shp),
        in_specs=[tile, pl.BlockSpec((n_parts, tr, cdim), lambda i: (0, i, 0)), tile, tile],
        out_specs=(tile, tile, tile, tile),
        compiler_params=_params(("arbitrary",), 40),
    )(w, parts, m, v)


def _adamw_layers(w, parts, m, v, name, tr):
    n_layers, r, cdim = w.shape

    def body(*refs):
        w_ref, p_refs, (m_ref, v_ref) = refs[0], refs[1:1 + n_layers], refs[1 + n_layers:3 + n_layers]
        g_ref, d_ref, nm_ref, nv_ref = refs[3 + n_layers:]
        for l, p_ref in enumerate(p_refs):
            @pl.when(pl.program_id(0) == l)
            def _(p_ref=p_ref):
                g = p_ref[0].astype(F32)
                for k in range(1, p_ref.shape[0]):
                    g = g + p_ref[k].astype(F32)
                g_ref[0] = g
                d_ref[0], nm_ref[0], nv_ref[0] = _adam_math(w_ref[0], g, m_ref[0], v_ref[0])

    tile = pl.BlockSpec((1, tr, cdim), lambda l, i: (l, i, 0))
    part = lambda own: pl.BlockSpec((parts[own].shape[0], tr, cdim), lambda l, i: (0, jnp.where(l == own, i, 0), 0))
    shp = jax.ShapeDtypeStruct(w.shape, F32)
    return pl.pallas_call(
        body, name=name, grid=(n_layers, r // tr), out_shape=(shp, shp, shp, shp),
        in_specs=[tile] + [part(l) for l in range(n_layers)] + [tile, tile],
        out_specs=(tile, tile, tile, tile),
        compiler_params=_params(("arbitrary", "arbitrary"), 40),
    )(w, *parts, m, v)


def _row(vec, width):
    vec = vec.reshape(1, -1)
    return jnp.pad(vec, ((0, 0), (0, width - vec.shape[1])))


def kernel(x, c, w_ada, b_ada, g_pre, w_in, w_gate_up, b_gate_up, g_gla, g_dil, w_out, g_post, loss_target, m_w_ada, m_b_ada, m_g_pre, m_w_in, m_w_gate_up, m_b_gate_up, m_g_gla, m_g_dil, m_w_out, m_g_post, v_w_ada, v_b_ada, v_g_pre, v_w_in, v_w_gate_up, v_b_gate_up, v_g_gla, v_g_dil, v_w_out, v_g_post):
    px, py, pc = _my_position()
    me = _linear(px, py, pc)
    xs = x[0]
    target = loss_target[0]
    s_len = xs.shape[0]
    assert s_len % (DIL_BLOCK * max(DIL_DILATIONS) * 2) == 0 and xs.shape[1] == D_MODEL

    c_all = _all_gather(jnp.pad(c, ((0, 7), (0, 0))), "gather_c").reshape(N_DEV, 8, D_MODEL)[:, 0]
    mod_part = _mod_fwd(c_all, w_ada)
    w_in_b, w_out_b = w_in.astype(BF16), w_out.astype(BF16)
    mod_all, wgu_all, w_in_all = _comm_call(
        "gather", [mod_part.reshape(DEPTH * N_DEV, ADA_SHARD), w_gate_up.reshape(DEPTH * GLA_LOWRANK, GU_SHARD),
                   w_in_b[0]], "gather_first")
    mod_all = mod_all.reshape(N_DEV, DEPTH, N_DEV, ADA_SHARD)
    mod_mine = lax.dynamic_index_in_dim(mod_all, me, axis=2, keepdims=False)
    mod = jnp.transpose(mod_mine, (1, 0, 2)).reshape(DEPTH, 3 * D_MODEL) + b_ada
    wgu_full = jnp.transpose(wgu_all.reshape(N_DEV, DEPTH, GLA_LOWRANK, GU_SHARD), (1, 2, 0, 3)).reshape(
        DEPTH, GLA_LOWRANK, GU_COLS)
    wgu_pad = jnp.pad(wgu_full, ((0, 0), (0, LANE - GLA_LOWRANK), (0, 0))).astype(BF16)

    def kernel_w_in(gathered):
        return _w_in_to_kernel(gathered.reshape(N_DEV, D_MODEL, W_IN_SHARD))

    cos, sin_signed = _rope_tables(s_len)
    g_heads = jnp.concatenate([g_gla, g_dil], axis=1)

    saved = []
    xl = xs
    for l in range(DEPTH):
        shift, scale, gate = ((mod, l, k) for k in range(3))
        w_new = kernel_w_in(w_in_all)
        pf, pb, h, w_out_l = _prenorm_proj(xl, (g_pre, l, 0), scale, shift, w_new, comm=("gather", [w_out_b[l]]))
        o_a, states = _gla_fwd(pf, pb, wgu_pad, b_gate_up, l)
        if l + 1 < DEPTH:
            o_b, lse, w_in_all = _dil_fwd(pf, pb, cos, sin_signed, comm=("gather", [w_in_b[l + 1]]))
        else:
            o_b, lse = _dil_fwd(pf, pb, cos, sin_signed)
        if l + 1 < DEPTH:
            x_next, y, u = _post_fwd(o_a, o_b, pf, (g_heads, l, 0), w_out_l, xl, gate, (g_post, l, 0))
        else:
            dx, y, u, loss_part = _post_fwd(o_a, o_b, pf, (g_heads, l, 0), w_out_l, xl, gate, (g_post, l, 0),
                                            target=target)
        saved.append((xl, scale, gate, w_new, w_out_l, pf, pb, h, o_a, states, o_b, lse, y, u))
        xl = x_next

    small_rows = []
    gin_slots, gin_parts, gout_parts = None, [None] * DEPTH, [None] * DEPTH
    for l in reversed(range(DEPTH)):
        x_in, scale, gate, w_new, w_out_l, pf, pb, h, o_a, states, o_b, lse, y, u = saved[l]
        du, do, dz, sums_post = _post_bwd(dx, u, gate, (g_post, l, 0), w_out_l, o_a, o_b, pf, (g_heads, l, 0))
        gout_slots = _matmul_tn(y, du, "grad_w_out", 512)
        dq_a, dk_a, dv_a, dlr2, dwgu, dbgu, arrived = _gla_bwd(pf, pb, wgu_pad, b_gate_up, l, states, do,
                                                               comm=("exchange", [gout_slots]))
        gout_parts[l] = arrived.reshape(N_DEV, OUT_SHARD, D_MODEL)
        if gin_slots is not None:
            dq_b, dk_b, dv_b, arrived, _, _ = _dil_bwd(pf, pb, cos, sin_signed, do, o_b, lse,
                                                       comm=("pairsum_exchange", [gin_slots]))
            gin_parts[l + 1] = arrived.reshape(N_DEV // 2, D_MODEL, W_IN_SHARD)
        else:
            dq_b, dk_b, dv_b = _dil_bwd(pf, pb, cos, sin_signed, do, o_b, lse)
        dlr = (dlr2[0] + dlr2[1]).astype(BF16)
        pieces = (dz, dq_a, dk_a, dq_b, dk_b, dv_a, dv_b, dlr)
        gin_slots = _grad_w_in(h, pieces).reshape(N_DEV * D_MODEL, W_IN_SHARD)
        if l == 0:
            dx, sums_in, arrived, _, _ = _in_bwd(pieces, w_new, x_in, dx, (g_pre, l, 0), scale,
                                                 comm=("pairsum_exchange", [gin_slots]))
            gin_parts[0] = arrived.reshape(N_DEV // 2, D_MODEL, W_IN_SHARD)
        else:
            dx, sums_in = _in_bwd(pieces, w_new, x_in, dx, (g_pre, l, 0), scale)
        dmod = jnp.concatenate([sums_in[0], sums_in[1], sums_post[0]])
        vecs = jnp.concatenate([sums_in[2], sums_post[1], sums_post[2], dbgu[0]])
        small_rows[0:0] = [_row(dmod, 4096), _row(vecs, 4096), _row(dwgu[:GLA_LOWRANK], 4096)]
    grad_x = dx[None]

    flat = lambda a, rows: a.reshape(rows, a.shape[-1])
    r_ada = DEPTH * D_MODEL
    g_w_in, d_w_in, nm_w_in, nv_w_in = _adamw_layers(w_in, gin_parts, m_w_in, v_w_in, "adamw_w_in", 256)
    g_w_out, d_w_out, nm_w_out, nv_w_out = _adamw_layers(w_out, gout_parts, m_w_out, v_w_out, "adamw_w_out", 128)

    small_rows += [_row(loss_part[0, 0:1], 4096), jnp.zeros((1, 4096), F32)]
    small = _all_gather(jnp.concatenate(small_rows, axis=0), "gather_small").reshape(N_DEV, 8, 4096)
    dmod_all = jnp.stack([small[:, 0, :3 * D_MODEL], small[:, 3, :3 * D_MODEL]])
    dmod_cols = lax.dynamic_slice_in_dim(dmod_all, me * ADA_SHARD, ADA_SHARD, axis=2)
    gwa = _w_ada_grad(c_all, dmod_cols).reshape(1, r_ada, ADA_SHARD)
    g_w_ada, d_w_ada, nm_w_ada, nv_w_ada = (
        t.reshape(w_ada.shape) for t in _adamw(flat(w_ada, r_ada), gwa, flat(m_w_ada, r_ada), flat(v_w_ada, r_ada),
                                               "adamw_w_ada", 256))

    where = ((0, 0), (1, 0), (1, 1024), (1, 2048), (1, 2560), (1, 3072))
    replicated = [(b_ada, m_b_ada, v_b_ada), (g_pre, m_g_pre, v_g_pre), (g_post, m_g_post, v_g_post),
                  (g_gla, m_g_gla, v_g_gla), (g_dil, m_g_dil, v_g_dil), (b_gate_up, m_b_gate_up, v_b_gate_up)]
    updated, loss = _adamw_replicated(small, replicated, where, loss_at=(6, 0))
    ((g_b_ada, d_b_ada, nm_b_ada, nv_b_ada), (g_g_pre, d_g_pre, nm_g_pre, nv_g_pre),
     (g_g_post, d_g_post, nm_g_post, nv_g_post), (g_g_gla, d_g_gla, nm_g_gla, nv_g_gla),
     (g_g_dil, d_g_dil, nm_g_dil, nv_g_dil), (g_b_gu, d_b_gu, nm_b_gu, nv_b_gu)) = updated
    gu_parts = jnp.stack([small[:, 2], small[:, 5]], axis=1).reshape(N_DEV, DEPTH, GLA_LOWRANK, GU_COLS)
    gu_parts = lax.dynamic_slice_in_dim(gu_parts, me * GU_SHARD, GU_SHARD, axis=3).reshape(
        N_DEV, DEPTH * GLA_LOWRANK, GU_SHARD)
    r_gu = DEPTH * GLA_LOWRANK
    g_w_gu, d_w_gu, nm_w_gu, nv_w_gu = (
        t.reshape(w_gate_up.shape) for t in _adamw(flat(w_gate_up, r_gu), gu_parts, flat(m_w_gate_up, r_gu),
                                                   flat(v_w_gate_up, r_gu), "adamw_w_gate_up", r_gu))
    return (loss, grad_x,
            g_w_ada, g_b_ada, g_g_pre, g_w_in, g_w_gu, g_b_gu, g_g_gla, g_g_dil, g_w_out, g_g_post,
            d_w_ada, d_b_ada, d_g_pre, d_w_in, d_w_gu, d_b_gu, d_g_gla, d_g_dil, d_w_out, d_g_post,
            nm_w_ada, nm_b_ada, nm_g_pre, nm_w_in, nm_w_gu, nm_b_gu, nm_g_gla, nm_g_dil, nm_w_out, nm_g_post,
            nv_w_ada, nv_b_ada, nv_g_pre, nv_w_in, nv_w_gu, nv_b_gu, nv_g_gla, nv_g_dil, nv_w_out, nv_g_post)


def _adamw_replicated(small, params, where, loss_at):
    n_parts = small.shape[0]

    def body(*refs):
        s_ref, p_refs, o_refs = refs[0], refs[1:1 + 3 * len(params)], refs[1 + 3 * len(params):]
        total = s_ref[0]
        for k in range(1, n_parts):
            total = total + s_ref[k]
        for i, (row, col) in enumerate(where):
            w_ref, m_ref, v_ref = p_refs[3 * i:3 * i + 3]
            n = w_ref.shape[1]
            g = jnp.concatenate([total[row + 3 * l:row + 3 * l + 1, col:col + n] for l in range(DEPTH)], axis=0)
            o_refs[4 * i][...] = g
            o_refs[4 * i + 1][...], o_refs[4 * i + 2][...], o_refs[4 * i + 3][...] = _adam_math(
                w_ref[...], g, m_ref[...], v_ref[...])
        o_refs[-1][...] = jnp.broadcast_to(total[loss_at[0]:loss_at[0] + 1, loss_at[1]:loss_at[1] + 1], (8, LANE))

    flat = [a for p in params for a in p]
    shapes = [jax.ShapeDtypeStruct(p[0].shape, F32) for p in params for _ in range(4)]
    outs = pl.pallas_call(body, name="adamw_replicated",
                          out_shape=shapes + [jax.ShapeDtypeStruct((8, LANE), F32)])(small, *flat)
    return [tuple(outs[4 * i:4 * i + 4]) for i in range(len(params))], outs[-1][0, 0]
```

```python
import functools
import math

import jax
import jax.numpy as jnp
from jax import lax
from jax.experimental import pallas as pl
from jax.experimental.pallas import tpu as pltpu

F32 = jnp.float32
BF16 = jnp.bfloat16

N_DEV = 8
D_MODEL = 1024
DEPTH = 2
GLA_HEADS = 4
GLA_DK = 64
GLA_DV = 128
GLA_CHUNK = 64
GLA_TAU = 16.0
GLA_LOWRANK = 16
DIL_HEADS = 4
DIL_HD = 128
DIL_BLOCK = 128
DIL_DILATIONS = (1, 4, 16)
ROPE_THETA = 10000.0
EPS = 1e-6
IN_COLS = 3600
W_IN_SHARD = IN_COLS // N_DEV
ADA_SHARD = 3 * D_MODEL // N_DEV
OUT_SHARD = D_MODEL // N_DEV
GU_COLS = GLA_HEADS * GLA_DK
GU_SHARD = GU_COLS // N_DEV

ADAM_LR = 0.001
ADAM_B1 = 0.9
ADAM_B2 = 0.999
ADAM_EPS = 1e-08
ADAM_WD = 0.01
ADAM_STEP = 10

NP = 3712
COL_Z, COL_QA, COL_KA, COL_QB, COL_KB, COL_VA, COL_VB, COL_LR = 0, 1024, 1280, 1536, 2048, 2560, 3072, 3584
NP_F32 = COL_VA
NP_BF16 = NP - NP_F32
LANE = 128
MASK_VALUE = -1e30

MESH = pl.DeviceIdType.MESH
ANY = pl.BlockSpec(memory_space=pl.ANY)


def _params(sem=None, vmem_mb=None):
    kw = {}
    if sem is not None:
        kw["dimension_semantics"] = sem
    if vmem_mb is not None:
        kw["vmem_limit_bytes"] = vmem_mb * 1024 * 1024
    return pltpu.CompilerParams(**kw)


def _dot(a, b):
    return jnp.dot(a, b, preferred_element_type=F32)


def _dot_nt(a, b):
    return lax.dot_general(a, b, (((1,), (1,)), ((), ())), preferred_element_type=F32)


def _dot_tn(a, b):
    return lax.dot_general(a, b, (((0,), (0,)), ((), ())), preferred_element_type=F32)


def _sigmoid(z):
    return 1.0 / (1.0 + jnp.exp(-z))


def _log_sigmoid(z):
    return jnp.minimum(z, 0.0) - jnp.log(1.0 + jnp.exp(-jnp.abs(z)))


def _rowvec(v, width=D_MODEL):
    arr, row, cb = v
    return arr.reshape(arr.shape[0], 1, arr.shape[1]), pl.BlockSpec((None, 1, width), lambda *_: (row, 0, cb))


def _my_position():
    return lax.axis_index("x"), lax.axis_index("y"), lax.axis_index("c")


def _linear(px, py, pc):
    return 4 * px + 2 * py + pc


def _gather_phase(phase, x_ref, out_ref, send_sem, recv_sem, local_sem):
    m = x_ref.shape[0]
    x, y, c = _my_position()
    me, sibling = (x, y, c), (x, y, 1 - c)
    chips = [(1 - x, y), (x, 1 - y), (1 - x, 1 - y)]

    def rows(px, py, pc):
        return out_ref.at[pl.ds(_linear(px, py, pc) * m, m), :]

    def copy(k, block, to, src=None):
        return pltpu.make_async_remote_copy(
            src_ref=rows(*block) if src is None else src, dst_ref=rows(*block),
            send_sem=send_sem(k), recv_sem=recv_sem(k), device_id=to, device_id_type=MESH)

    mine = pltpu.make_async_copy(x_ref, rows(*me), local_sem)
    first = [copy(0, me, sibling, src=x_ref)] + [copy(1 + j, me, (*chip, c), src=x_ref) for j, chip in enumerate(chips)]
    passed = [copy(4 + j, (*chip, c), sibling) for j, chip in enumerate(chips)]
    if phase == "start":
        mine.start()
        for cp in first:
            cp.start()
    elif phase == "forward":
        for j, chip in enumerate(chips):
            copy(1 + j, (*chip, c), me).wait_recv()
            passed[j].start()
    else:
        copy(0, sibling, me).wait_recv()
        for j, chip in enumerate(chips):
            copy(4 + j, (*chip, 1 - c), me).wait_recv()
        for cp in first + passed:
            cp.wait_send()
        mine.wait()


def _exchange_phase(phase, x_ref, out_ref, send_sem, recv_sem, local_sem):
    m = x_ref.shape[0] // N_DEV
    x, y, c = _my_position()
    me = _linear(x, y, c)

    def rows(ref, idx):
        return ref.at[pl.ds(idx * m, m), :]

    peers = [(1 - x if j & 4 else x, 1 - y if j & 2 else y, 1 - c if j & 1 else c) for j in range(1, N_DEV)]
    local = pltpu.make_async_copy(rows(x_ref, me), rows(out_ref, me), local_sem)
    sends = [pltpu.make_async_remote_copy(
        src_ref=rows(x_ref, _linear(*peer)), dst_ref=rows(out_ref, me),
        send_sem=send_sem(j), recv_sem=recv_sem(j), device_id=peer, device_id_type=MESH) for j, peer in enumerate(peers)]
    if phase == "start":
        local.start()
        for cp in sends:
            cp.start()
    else:
        for j, peer in enumerate(peers):
            pltpu.make_async_remote_copy(
                src_ref=rows(x_ref, _linear(*peer)), dst_ref=rows(out_ref, _linear(*peer)),
                send_sem=send_sem(j), recv_sem=recv_sem(j), device_id=peer, device_id_type=MESH).wait_recv()
        for cp in sends:
            cp.wait_send()
        local.wait()


def _pairsum_exchange_phase(phase, x_ref, out_refs, send_sem, recv_sem, local_sem):
    out_ref, stage_ref, pair_ref = out_refs
    m, n = x_ref.shape[0] // N_DEV, x_ref.shape[1]
    x, y, c = _my_position()
    mine = 2 * x + y
    chips = [(qx, qy) for qx in range(2) for qy in range(2)]
    others = [(1 - x, y), (x, 1 - y), (1 - x, 1 - y)]

    def rows(ref, idx):
        return ref.at[pl.ds(idx * m, m), :]

    def remote(src, dst, k, to):
        return pltpu.make_async_remote_copy(src_ref=src, dst_ref=dst, send_sem=send_sem(k), recv_sem=recv_sem(k),
                                            device_id=to, device_id_type=MESH)

    to_sibling = [remote(rows(x_ref, _linear(qx, qy, 1 - c)), rows(stage_ref, q), q, (x, y, 1 - c))
                  for q, (qx, qy) in enumerate(chips)]
    to_chips = [remote(rows(pair_ref, 2 * qx + qy), rows(out_ref, mine), 4 + j, (qx, qy, c))
                for j, (qx, qy) in enumerate(others)]
    keep = pltpu.make_async_copy(rows(pair_ref, mine), rows(out_ref, mine), local_sem)
    if phase == "start":
        for cp in to_sibling:
            cp.start()
    elif phase == "reduce":
        for cp in to_sibling:
            cp.wait_recv()

        def through_vmem(a_buf, b_buf, sems):
            tr = 128
            loads = [(pltpu.make_async_copy(rows(x_ref, _linear(qx, qy, c)), a_buf.at[q % 2], sems.at[q % 2]),
                      pltpu.make_async_copy(rows(stage_ref, q), b_buf.at[q % 2], sems.at[2 + q % 2]))
                     for q, (qx, qy) in enumerate(chips)]
            stores = [pltpu.make_async_copy(a_buf.at[q % 2], rows(pair_ref, q), sems.at[4 + q % 2]) for q in range(4)]
            for q in range(4):
                if q >= 2:
                    stores[q - 2].wait()
                for cp in loads[q]:
                    cp.start()
                for cp in loads[q]:
                    cp.wait()

                def add(r, carry, q=q):
                    tile = pl.ds(pl.multiple_of(r * tr, tr), tr)
                    a_buf[q % 2, tile, :] = (a_buf[q % 2, tile, :].astype(F32)
                                             + b_buf[q % 2, tile, :].astype(F32)).astype(x_ref.dtype)
                    return carry

                lax.fori_loop(0, m // tr, add, 0)
                stores[q].start()
            stores[2].wait()
            stores[3].wait()

        pl.run_scoped(through_vmem, pltpu.VMEM((2, m, n), x_ref.dtype), pltpu.VMEM((2, m, n), x_ref.dtype),
                      pltpu.SemaphoreType.DMA((6,)))
    elif phase == "send":
        keep.start()
        for cp in to_chips:
            cp.start()
    else:
        for j, (qx, qy) in enumerate(others):
            remote(rows(pair_ref, mine), rows(out_ref, 2 * qx + qy), 4 + j, (qx, qy, c)).wait_recv()
        for cp in to_sibling + to_chips:
            cp.wait_send()
        keep.wait()


_COMM_PHASES = {"gather": (_gather_phase, ("start", "forward", "finish")),
                "exchange": (_exchange_phase, ("start", "finish")),
                "pairsum_exchange": (_pairsum_exchange_phase, ("start", "reduce", "send", "finish"))}


def _comm_scratch(n_arrays):
    return [pltpu.SemaphoreType.DMA((n_arrays, 7)), pltpu.SemaphoreType.DMA((n_arrays, 7)),
            pltpu.SemaphoreType.DMA((n_arrays,))]


def _comm_run(kind, phases, x_refs, out_refs, send_sems, recv_sems, local_sems):
    fn = _COMM_PHASES[kind][0]
    per = len(out_refs) // len(x_refs)
    for phase in phases:
        for a, x_ref in enumerate(x_refs):
            outs = out_refs[a] if per == 1 else tuple(out_refs[per * a:per * (a + 1)])
            fn(phase, x_ref, outs, lambda k, a=a: send_sems.at[a, k], lambda k, a=a: recv_sems.at[a, k],
               local_sems.at[a])


def _comm_out_shapes(kind, arrays):
    if kind == "pairsum_exchange":
        return [jax.ShapeDtypeStruct((a.shape[0] // 2, a.shape[1]), a.dtype) for a in arrays for _ in range(3)]
    return [jax.ShapeDtypeStruct((N_DEV * a.shape[0], a.shape[1]) if kind == "gather" else a.shape, a.dtype)
            for a in arrays]


def _comm_call(kind, arrays, name):
    n = len(arrays)
    shapes = _comm_out_shapes(kind, arrays)

    def body(*refs):
        _comm_run(kind, _COMM_PHASES[kind][1], refs[:n], refs[n:n + len(shapes)], *refs[n + len(shapes):])

    return pl.pallas_call(body, name=name, out_shape=shapes, in_specs=[ANY] * n, out_specs=[ANY] * len(shapes),
                          scratch_shapes=_comm_scratch(n))(*arrays)


def _all_gather(xs, name):
    return _comm_call("gather", [xs], name)[0]


def _mod_fwd(c_all, w_ada):
    def body(c_ref, w_ref, o_ref):
        cv = c_ref[...]
        sc = cv * _sigmoid(cv)
        o_ref[0] = _dot(sc.astype(BF16), w_ref[0].astype(BF16))

    return pl.pallas_call(
        body, name="mod_fwd", grid=(DEPTH,),
        out_shape=jax.ShapeDtypeStruct((DEPTH, N_DEV, ADA_SHARD), F32),
        in_specs=[pl.BlockSpec((N_DEV, D_MODEL), lambda l: (0, 0)),
                  pl.BlockSpec((1, D_MODEL, ADA_SHARD), lambda l: (l, 0, 0))],
        out_specs=pl.BlockSpec((1, N_DEV, ADA_SHARD), lambda l: (l, 0, 0)),
        compiler_params=_params(("arbitrary",)),
    )(c_all, w_ada)


def _w_ada_grad(c_all, dmod_cols):
    def body(c_ref, d_ref, o_ref):
        cv = c_ref[...]
        sc = cv * _sigmoid(cv)
        o_ref[0] = lax.dot_general(sc, d_ref[0], (((0,), (0,)), ((), ())), precision=lax.Precision.HIGHEST,
                                   preferred_element_type=F32)

    return pl.pallas_call(
        body, name="w_ada_grad", grid=(DEPTH,),
        out_shape=jax.ShapeDtypeStruct((DEPTH, D_MODEL, ADA_SHARD), F32),
        in_specs=[pl.BlockSpec((N_DEV, D_MODEL), lambda l: (0, 0)),
                  pl.BlockSpec((1, N_DEV, ADA_SHARD), lambda l: (l, 0, 0))],
        out_specs=pl.BlockSpec((1, D_MODEL, ADA_SHARD), lambda l: (l, 0, 0)),
        compiler_params=_params(("arbitrary",)),
    )(c_all, dmod_cols)


def _comm_plumbing(comm):
    if not comm:
        return 0, [], []
    return len(comm[1]), _comm_out_shapes(*comm), _comm_scratch(len(comm[1]))


def _split_refs(refs, n_in, n_out, n_scratch, comm):
    ci, shapes, _ = _comm_plumbing(comm)
    co = len(shapes)
    a, b, c = n_in + ci, n_in + ci + n_out, n_in + ci + n_out + co
    return refs[:n_in], refs[a:b], refs[c:c + n_scratch], refs[n_in:a], refs[b:c], refs[c + n_scratch:]


def _prenorm_proj(x, g_pre, scale, shift, w_new, comm=None, ts=256):
    s_len = x.shape[0]
    n_cin, c_shapes, c_scratch = _comm_plumbing(comm)

    def body(*refs):
        (x_ref, g_ref, sc_ref, sh_ref, w_ref), (pf_ref, pb_ref, h_ref), _, cin, cout, csem = _split_refs(
            refs, 5, 3, 0, comm)
        comm_before, comm_after = _comm_hooks(comm, cin, cout, csem, steps=s_len // ts)
        comm_before()
        xv = x_ref[...]
        rstd = lax.rsqrt(jnp.mean(xv * xv, axis=-1, keepdims=True) + EPS)
        h = (xv * rstd * g_ref[...]) * (1.0 + sc_ref[...]) + sh_ref[...]
        hb = h.astype(BF16)
        h_ref[...] = hb
        for j in range(0, NP, 512):
            w = min(512, NP - j)
            acc = _dot(hb, w_ref[:, j:j + w])
            if j < NP_F32:
                pf_ref[:, j:j + w] = acc
            else:
                pb_ref[:, j - NP_F32:j - NP_F32 + w] = acc.astype(BF16)
        comm_after()

    (g_pre, g_spec), (scale, sc_spec), (shift, sh_spec) = _rowvec(g_pre), _rowvec(scale), _rowvec(shift)
    return pl.pallas_call(
        body, name="prenorm_proj_comm" if comm else "prenorm_proj", grid=(s_len // ts,),
        out_shape=[jax.ShapeDtypeStruct((s_len, NP_F32), F32), jax.ShapeDtypeStruct((s_len, NP_BF16), BF16),
                   jax.ShapeDtypeStruct((s_len, D_MODEL), BF16)] + c_shapes,
        in_specs=[pl.BlockSpec((ts, D_MODEL), lambda i: (i, 0)), g_spec, sc_spec, sh_spec,
                  pl.BlockSpec((D_MODEL, NP), lambda i: (0, 0))] + [ANY] * n_cin,
        out_specs=[pl.BlockSpec((ts, NP_F32), lambda i: (i, 0)), pl.BlockSpec((ts, NP_BF16), lambda i: (i, 0)),
                   pl.BlockSpec((ts, D_MODEL), lambda i: (i, 0))] + [ANY] * len(c_shapes),
        scratch_shapes=c_scratch,
        compiler_params=_params(("arbitrary",), 48),
    )(x, g_pre, scale, shift, w_new, *(comm[1] if comm else []))


GLA_GROUP = 8


def _gla_group_rows(t):
    return [pl.ds(pl.multiple_of((t * GLA_GROUP + j) * GLA_CHUNK, GLA_CHUNK), GLA_CHUNK) for j in range(GLA_GROUP)]


def _gla_chunks_common(q_ref, k_ref, lr_ref, wgu_ref, bgu_ref, rows_list):
    c = GLA_CHUNK
    ri = lax.broadcasted_iota(jnp.int32, (c, c), 0)
    ci = lax.broadcasted_iota(jnp.int32, (c, c), 1)
    tril = (ri >= ci).astype(F32)
    zs = [_dot(lr_ref[rows, :], wgu_ref[...]) + bgu_ref[...] for rows in rows_list]
    las = [_log_sigmoid(z) * (1.0 / GLA_TAU) for z in zs]
    bs = [jnp.dot(tril, la, precision=lax.Precision.HIGHEST, preferred_element_type=F32) for la in las]
    out = []
    for rows, z, b in zip(rows_list, zs, bs):
        q = q_ref[rows, :] * (GLA_DK ** -0.5)
        k = k_ref[rows, :]
        bl = b[c - 1:c, :]
        out.append(dict(z=z, b=b, bl=bl, qe=q * jnp.exp(b), ke=k * jnp.exp(-b), kend=k * jnp.exp(bl - b),
                        dec=jnp.exp(bl)))
    return out, ri, ci


def _head_lane_mask(hh):
    return (lax.broadcasted_iota(jnp.int32, (1, LANE), 1) // GLA_DK) == hh


def _state_block_mask():
    r = lax.broadcasted_iota(jnp.int32, (2 * GLA_DV, LANE), 0) // GLA_DV
    cc = lax.broadcasted_iota(jnp.int32, (2 * GLA_DV, LANE), 1) // GLA_DK
    return r == cc


def _gla_fwd(pf, pb, wgu, bgu, layer, comm=None):
    s_len = pf.shape[0]
    nc = s_len // GLA_CHUNK
    ncomm = len(comm[1]) if comm else 0

    def body(*refs):
        q_ref, k_ref, v_ref, lr_ref, wgu_ref, bgu_ref = refs[:6]
        cin, (o_ref, st_ref), cout = refs[6:6 + ncomm], refs[6 + ncomm:8 + ncomm], refs[8 + ncomm:8 + 2 * ncomm]
        qe_s, cs_s, dec_s = refs[8 + 2 * ncomm:11 + 2 * ncomm]
        comm_before, comm_after = _comm_hooks(comm, cin, cout, refs[11 + 2 * ncomm:], steps=2)
        comm_before()
        bd = _state_block_mask()

        def local(t, carry):
            rows_list = _gla_group_rows(t)
            cm, ri, ci = _gla_chunks_common(q_ref, k_ref, lr_ref, wgu_ref, bgu_ref, rows_list)
            vs = [v_ref[rows, :] for rows in rows_list]
            kebs = [c["ke"].astype(BF16) for c in cm]
            a = [[jnp.where(ri >= ci, _dot_nt(jnp.where(_head_lane_mask(hh), c["qe"], 0.0).astype(BF16), keb), 0.0)
                  .astype(BF16) for hh in range(2)] for c, keb in zip(cm, kebs)]
            oi = [[_dot(ah[hh], v[:, hh * GLA_DV:(hh + 1) * GLA_DV]) for hh in range(2)] for ah, v in zip(a, vs)]
            cs = [jnp.where(bd, _dot_tn(v, c["kend"].astype(BF16)), 0.0) for c, v in zip(cm, vs)]
            for j, (rows, c) in enumerate(zip(rows_list, cm)):
                n = t * GLA_GROUP + j
                o_ref[rows, :] = jnp.concatenate(oi[j], axis=1)
                qe_s[rows, :] = c["qe"].astype(BF16)
                cs_s[n] = cs[j]
                dec_s[n] = jnp.broadcast_to(c["dec"], (8, LANE))
            return carry

        lax.fori_loop(0, nc // GLA_GROUP, local, 0)

        def scan(n, st):
            st_ref[0, n] = st.astype(BF16)
            return dec_s[n][0:1, :] * st + cs_s[n]

        lax.fori_loop(0, nc, scan, jnp.zeros((2 * GLA_DV, LANE), F32))

        def inter(t, carry):
            rows_list = _gla_group_rows(t)
            add = [_dot_nt(qe_s[rows, :], st_ref[0, t * GLA_GROUP + j]) for j, rows in enumerate(rows_list)]
            for rows, av in zip(rows_list, add):
                o_ref[rows, :] = o_ref[rows, :] + av
            return carry

        lax.fori_loop(0, nc // GLA_GROUP, inter, 0)
        comm_after()

    return pl.pallas_call(
        body, name="gla_fwd_comm" if comm else "gla_fwd", grid=(2,),
        out_shape=[jax.ShapeDtypeStruct((s_len, GLA_HEADS * GLA_DV), F32),
                   jax.ShapeDtypeStruct((2, nc, 2 * GLA_DV, LANE), BF16)] + (_comm_out_shapes(*comm) if comm else []),
        in_specs=[pl.BlockSpec((s_len, LANE), lambda g: (0, COL_QA // LANE + g)),
                  pl.BlockSpec((s_len, LANE), lambda g: (0, COL_KA // LANE + g)),
                  pl.BlockSpec((s_len, 2 * GLA_DV), lambda g: (0, (COL_VA - NP_F32) // (2 * GLA_DV) + g)),
                  pl.BlockSpec((s_len, LANE), lambda g: (0, (COL_LR - NP_F32) // LANE)),
                  pl.BlockSpec((None, LANE, LANE), lambda g: (layer, 0, g)),
                  pl.BlockSpec((None, 1, LANE), lambda g: (layer, 0, g))] + [ANY] * ncomm,
        out_specs=[pl.BlockSpec((s_len, 2 * GLA_DV), lambda g: (0, g)),
                   pl.BlockSpec((1, nc, 2 * GLA_DV, LANE), lambda g: (g, 0, 0, 0))] + [ANY] * ncomm,
        scratch_shapes=[pltpu.VMEM((s_len, LANE), BF16), pltpu.VMEM((nc, 2 * GLA_DV, LANE), F32),
                        pltpu.VMEM((nc, 8, LANE), F32)] + (_comm_scratch(ncomm) if comm else []),
        compiler_params=_params(("arbitrary",), 56),
    )(pf, pf, pb, pb, wgu, bgu.reshape(bgu.shape[0], 1, GU_COLS), *(comm[1] if comm else []))


def _rope_tables(s_len):
    inv_freq = ROPE_THETA ** (-jnp.arange(0, DIL_HD, 2, dtype=F32) / DIL_HD)
    ang = jnp.arange(s_len, dtype=F32)[:, None] * inv_freq[None, :]
    cos, sin = jnp.cos(ang), jnp.sin(ang)
    return jnp.concatenate([cos, cos], axis=1), jnp.concatenate([-sin, sin], axis=1)


def _rope(xv, cos, sin_signed):
    return xv * cos + pltpu.roll(xv, DIL_HD // 2, 1) * sin_signed


DIL_GROUP = 8


def _dil_pair_block(i, half, d, nblk, group=DIL_GROUP):
    nb = nblk // d
    j = i + half * (nblk // group)
    if nb >= 2 * group:
        r, n = j % d, j // d
    else:
        r, n = j // nb, j % nb
    kb = jnp.maximum(n - 1, 0)
    qs = r + d * DIL_BLOCK * n
    ks = r + d * DIL_BLOCK * kb
    return qs, ks, jnp.minimum(n, 1)


def _dil_fill_bias(bias):
    qi = lax.broadcasted_iota(jnp.int32, (DIL_BLOCK, 2 * DIL_BLOCK), 0)
    kj = lax.broadcasted_iota(jnp.int32, (DIL_BLOCK, 2 * DIL_BLOCK), 1)
    for sel in range(2):
        dist = qi - kj + DIL_BLOCK * sel
        bias[sel] = jnp.where((dist >= 0) & (dist <= DIL_BLOCK), 0.0, MASK_VALUE)


def _strided(start, size, d):
    return pl.ds(start, size) if d == 1 else pl.ds(start, size, stride=d)


def _comm_hooks(comm, cin, cout, csem, steps=DIL_HEADS):
    def before():
        if comm:
            @pl.when(pl.program_id(0) == 0)
            def _():
                _comm_run(comm[0], ("start",), cin, cout, *csem)

            if comm[0] == "gather":
                @pl.when(pl.program_id(0) == steps - 1)
                def _():
                    _comm_run(comm[0], ("forward",), cin, cout, *csem)

            if comm[0] == "pairsum_exchange":
                @pl.when(pl.program_id(0) == (1 if steps <= 4 else 3))
                def _():
                    _comm_run(comm[0], ("reduce", "send"), cin, cout, *csem)

    def after():
        if comm:
            @pl.when(pl.program_id(0) == steps - 1)
            def _():
                _comm_run(comm[0], ("finish",), cin, cout, *csem)

    return before, after


def _dil_fwd(pf, pb, cos, sin_signed, comm=None):
    s_len = pf.shape[0]
    nblk = s_len // DIL_BLOCK
    prep_rows = 256
    scale = DIL_HD ** -0.5
    nc = len(comm[1]) if comm else 0

    def body(*refs):
        q_ref, k_ref, v_ref, cos_ref, sin_ref = refs[:5]
        cin, (o_ref, lse_ref), cout = refs[5:5 + nc], refs[5 + nc:7 + nc], refs[7 + nc:7 + 2 * nc]
        qf, kf, vf, o0, o1, o2, l0, l1, l2, bias = refs[7 + 2 * nc:17 + 2 * nc]
        comm_before, comm_after = _comm_hooks(comm, cin, cout, refs[17 + 2 * nc:])
        comm_before()
        _dil_fill_bias(bias)

        def prep(t, carry):
            rows = pl.ds(pl.multiple_of(t * prep_rows, prep_rows), prep_rows)
            cs, sn = cos_ref[rows, :], sin_ref[rows, :]
            qf[rows, :] = _rope(q_ref[rows, :], cs, sn)
            kf[rows, :] = _rope(k_ref[rows, :], cs, sn)
            vf[rows, :] = v_ref[rows, :].astype(F32)
            return carry

        lax.fori_loop(0, s_len // prep_rows, prep, 0)
        for d, o_p, l_p in zip(DIL_DILATIONS, (o0, o1, o2), (l0, l1, l2)):
            if nblk // d == 2:
                units = DIL_GROUP // 2

                def whole(i, carry, d=d, o_p=o_p, l_p=l_p, units=units):
                    rows = [_strided(i + u * (d // units), 2 * DIL_BLOCK, d) for u in range(units)]
                    ld = [(qf[rw, :].astype(BF16), kf[rw, :].astype(BF16), vf[rw, :].astype(BF16)) for rw in rows]
                    both = bias[...].reshape(2 * DIL_BLOCK, 2 * DIL_BLOCK)
                    s = [_dot_nt(qb, kk) * scale + both for qb, kk, _ in ld]
                    m = [jnp.max(sv, axis=-1, keepdims=True) for sv in s]
                    p = [jnp.exp(sv - mv) for sv, mv in zip(s, m)]
                    den = [jnp.sum(pv, axis=-1, keepdims=True) for pv in p]
                    r = [_dot(pv.astype(BF16), vv) for pv, (_, _, vv) in zip(p, ld)]
                    for rv, dv, mv, rw in zip(r, den, m, rows):
                        o_p[rw, :] = rv / dv
                        l_p[rw, :] = jnp.broadcast_to(mv + jnp.log(dv), (2 * DIL_BLOCK, DIL_HD))
                    return carry

                lax.fori_loop(0, d // units, whole, 0)
                continue

            def pair(i, carry, d=d, o_p=o_p, l_p=l_p):
                idx = [_dil_pair_block(i, half, d, nblk, DIL_GROUP) for half in range(DIL_GROUP)]
                ld = [(qf[_strided(qs, DIL_BLOCK, d), :].astype(BF16),
                       kf[_strided(ks, 2 * DIL_BLOCK, d), :].astype(BF16),
                       vf[_strided(ks, 2 * DIL_BLOCK, d), :].astype(BF16)) for qs, ks, _ in idx]
                s = [_dot_nt(qb, kk) * scale + bias[sel] for (qb, kk, _), (_, _, sel) in zip(ld, idx)]
                m = [jnp.max(sv, axis=-1, keepdims=True) for sv in s]
                p = [jnp.exp(sv - mv) for sv, mv in zip(s, m)]
                den = [jnp.sum(pv, axis=-1, keepdims=True) for pv in p]
                r = [_dot(pv.astype(BF16), vv) for pv, (_, _, vv) in zip(p, ld)]
                for rv, dv, mv, (qs, _, _) in zip(r, den, m, idx):
                    o_p[_strided(qs, DIL_BLOCK, d), :] = rv / dv
                    l_p[_strided(qs, DIL_BLOCK, d), :] = jnp.broadcast_to(mv + jnp.log(dv), (DIL_BLOCK, DIL_HD))
                return carry

            lax.fori_loop(0, nblk // DIL_GROUP, pair, 0)

        def comb(t, carry):
            rows = pl.ds(pl.multiple_of(t * prep_rows, prep_rows), prep_rows)
            a0, a1, a2 = l0[rows, :], l1[rows, :], l2[rows, :]
            m = jnp.maximum(jnp.maximum(a0, a1), a2)
            e0, e1, e2 = jnp.exp(a0 - m), jnp.exp(a1 - m), jnp.exp(a2 - m)
            tot = e0 + e1 + e2
            o_ref[rows, :] = (e0 * o0[rows, :] + e1 * o1[rows, :] + e2 * o2[rows, :]) / tot
            lse_ref[rows, :] = m + jnp.log(tot)
            return carry

        lax.fori_loop(0, s_len // prep_rows, comb, 0)
        comm_after()

    head = lambda base: pl.BlockSpec((s_len, DIL_HD), lambda h: (0, base // DIL_HD + h))
    table = pl.BlockSpec((s_len, DIL_HD), lambda h: (0, 0))
    out = pl.BlockSpec((s_len, DIL_HD), lambda h: (0, h))
    shp = jax.ShapeDtypeStruct((s_len, DIL_HEADS * DIL_HD), F32)
    return pl.pallas_call(
        body, name="dil_fwd_comm" if comm else "dil_fwd", grid=(DIL_HEADS,),
        out_shape=[shp, shp] + (_comm_out_shapes(*comm) if comm else []),
        in_specs=[head(COL_QB), head(COL_KB), head(COL_VB - NP_F32), table, table] + [ANY] * nc,
        out_specs=[out, out] + [ANY] * nc,
        scratch_shapes=[pltpu.VMEM((s_len, DIL_HD), F32) for _ in range(9)]
        + [pltpu.VMEM((2, DIL_BLOCK, 2 * DIL_BLOCK), F32)] + (_comm_scratch(nc) if comm else []),
        compiler_params=_params(("arbitrary",), 56),
    )(pf, pf, pb, cos, sin_signed, *(comm[1] if comm else []))


def _silu_and_grad(z):
    sg = _sigmoid(z)
    return z * sg, sg * (1.0 + z * (1.0 - sg))


def _post_fwd(o_a, o_b, pf, g_heads, w_out, x, gate, g_post, target=None, ts=256):
    s_len = x.shape[0]
    half = GLA_HEADS * GLA_DV
    last = target is not None

    def body(*refs):
        oa_ref, ob_ref, z_ref, gh_ref, w_ref, x_ref, gate_ref, gp_ref = refs[:8]
        xo_ref, y_ref, u_ref = refs[8 + last:11 + last]
        for src, base in ((oa_ref, 0), (ob_ref, half)):
            for hh in range(4):
                lo = hh * LANE
                og = src[:, lo:lo + LANE]
                on = og * lax.rsqrt(jnp.mean(og * og, axis=-1, keepdims=True) + EPS)
                zg = z_ref[:, base + lo:base + lo + LANE].astype(F32)
                y_ref[:, base + lo:base + lo + LANE] = (on * gh_ref[:, base + lo:base + lo + LANE]
                                                        * (zg * _sigmoid(zg))).astype(BF16)
        u = _dot(y_ref[...], w_ref[...])
        u_ref[...] = u.astype(BF16)
        rstd = lax.rsqrt(jnp.mean(u * u, axis=-1, keepdims=True) + EPS)
        x_out = x_ref[...] + gate_ref[...] * (u * rstd * gp_ref[...])
        if last:
            t_ref, loss_ref = refs[8], refs[12]

            @pl.when(pl.program_id(0) == 0)
            def _():
                loss_ref[...] = jnp.zeros_like(loss_ref)

            e = x_out - t_ref[...]
            xo_ref[...] = e * (1.0 / D_MODEL)
            loss_ref[...] += 0.5 * jnp.sum(jnp.mean(e * e, axis=-1, keepdims=True))
        else:
            xo_ref[...] = x_out

    (g_heads, gh_spec), (gate, gate_spec), (g_post, gp_spec) = _rowvec(g_heads), _rowvec(gate), _rowvec(g_post)
    tile = pl.BlockSpec((ts, D_MODEL), lambda i: (i, 0))
    halft = pl.BlockSpec((ts, half), lambda i: (i, 0))
    return pl.pallas_call(
        body, name="post_fwd_loss" if last else "post_fwd", grid=(s_len // ts,),
        out_shape=[jax.ShapeDtypeStruct((s_len, D_MODEL), F32), jax.ShapeDtypeStruct((s_len, D_MODEL), BF16),
                   jax.ShapeDtypeStruct((s_len, D_MODEL), BF16)]
        + ([jax.ShapeDtypeStruct((8, LANE), F32)] if last else []),
        in_specs=[halft, halft, tile, gh_spec, pl.BlockSpec((D_MODEL, D_MODEL), lambda i: (0, 0)), tile, gate_spec,
                  gp_spec] + ([tile] if last else []),
        out_specs=[tile, tile, tile] + ([pl.BlockSpec((8, LANE), lambda i: (0, 0))] if last else []),
        compiler_params=_params(("arbitrary",), 40),
    )(o_a, o_b, pf, g_heads, w_out, x, gate, g_post, *([target] if last else []))


def _post_bwd(dxo, u, gate, g_post, w_out, o_a, o_b, pf, g_heads, ts=256):
    s_len = dxo.shape[0]
    half = GLA_HEADS * GLA_DV

    def body(dx_ref, u_ref, gate_ref, gp_ref, w_ref, oa_ref, ob_ref, z_ref, gh_ref, du_ref, do_ref, dz_ref, sums_ref):
        @pl.when(pl.program_id(0) == 0)
        def _():
            sums_ref[...] = jnp.zeros_like(sums_ref)

        dx = dx_ref[...]
        u = u_ref[...].astype(F32)
        rstd = lax.rsqrt(jnp.mean(u * u, axis=-1, keepdims=True) + EPS)
        un = u * rstd
        sums_ref[0:1, :] += jnp.sum(dx * (un * gp_ref[...]), axis=0, keepdims=True)
        drn = dx * gate_ref[...]
        sums_ref[1:2, :] += jnp.sum(drn * un, axis=0, keepdims=True)
        dun = drn * gp_ref[...]
        du = rstd * (dun - un * jnp.mean(dun * un, axis=-1, keepdims=True))
        dub = du.astype(BF16)
        du_ref[...] = dub
        dy = _dot_nt(dub, w_ref[...])
        for src, base in ((oa_ref, 0), (ob_ref, half)):
            for hh in range(4):
                lo = base + hh * LANE
                og = src[:, hh * LANE:(hh + 1) * LANE]
                rs = lax.rsqrt(jnp.mean(og * og, axis=-1, keepdims=True) + EPS)
                on = og * rs
                zg = z_ref[:, lo:lo + LANE].astype(F32)
                sz, dsz = _silu_and_grad(zg)
                gg = gh_ref[:, lo:lo + LANE]
                dyg = dy[:, lo:lo + LANE]
                sums_ref[2:3, lo:lo + LANE] += jnp.sum(dyg * sz * on, axis=0, keepdims=True)
                dz_ref[:, lo:lo + LANE] = (dyg * on * gg * dsz).astype(BF16)
                don = dyg * gg * sz
                do_ref[:, lo:lo + LANE] = (rs * (don - on * jnp.mean(don * on, axis=-1, keepdims=True))).astype(BF16)

    (g_heads, gh_spec), (gate, gate_spec), (g_post, gp_spec) = _rowvec(g_heads), _rowvec(gate), _rowvec(g_post)
    tile = pl.BlockSpec((ts, D_MODEL), lambda i: (i, 0))
    halft = pl.BlockSpec((ts, half), lambda i: (i, 0))
    return pl.pallas_call(
        body, name="post_bwd", grid=(s_len // ts,),
        out_shape=(jax.ShapeDtypeStruct((s_len, D_MODEL), BF16), jax.ShapeDtypeStruct((s_len, D_MODEL), BF16),
                   jax.ShapeDtypeStruct((s_len, D_MODEL), BF16), jax.ShapeDtypeStruct((8, D_MODEL), F32)),
        in_specs=[tile, tile, gate_spec, gp_spec, pl.BlockSpec((D_MODEL, D_MODEL), lambda i: (0, 0)), halft, halft,
                  tile, gh_spec],
        out_specs=(tile, tile, tile, pl.BlockSpec((8, D_MODEL), lambda i: (0, 0))),
        compiler_params=_params(("arbitrary",), 40),
    )(dxo, u, gate, g_post, w_out, o_a, o_b, pf, g_heads)


def _gla_bwd(pf, pb, wgu, bgu, layer, states, do, comm=None):
    s_len = pf.shape[0]
    nc = s_len // GLA_CHUNK
    c = GLA_CHUNK
    n_cin, c_shapes, c_scratch = _comm_plumbing(comm)

    def body(*refs):
        ((q_ref, k_ref, v_ref, lr_ref, wgu_ref, bgu_ref, st_ref, do_ref),
         (dq_ref, dk_ref, dv_ref, dlr_ref, dwgu_ref, dbgu_ref), (ds_s, dec_s, dw_acc, db_acc),
         cin, cout, csem) = _split_refs(refs, 8, 6, 4, comm)
        comm_before, comm_after = _comm_hooks(comm, cin, cout, csem, steps=2)
        comm_before()
        dw_acc[...] = jnp.zeros_like(dw_acc)
        db_acc[...] = jnp.zeros_like(db_acc)
        bd = _state_block_mask()
        last_row = lax.broadcasted_iota(jnp.int32, (c, LANE), 0) == c - 1

        def local(t, carry):
            rows_list = _gla_group_rows(t)
            cm, _, _ = _gla_chunks_common(q_ref, k_ref, lr_ref, wgu_ref, bgu_ref, rows_list)
            loc = [jnp.where(bd, _dot_tn(do_ref[rows, :], cc["qe"].astype(BF16)), 0.0)
                   for rows, cc in zip(rows_list, cm)]
            for j, cc in enumerate(cm):
                ds_s[t * GLA_GROUP + j] = loc[j]
                dec_s[t * GLA_GROUP + j] = jnp.broadcast_to(cc["dec"], (8, LANE))
            return carry

        lax.fori_loop(0, nc // GLA_GROUP, local, 0)

        def scan(t, dst):
            n = nc - 1 - t
            loc = ds_s[n]
            ds_s[n] = dst
            return dec_s[n][0:1, :] * dst + loc

        lax.fori_loop(0, nc, scan, jnp.zeros((2 * GLA_DV, LANE), F32))

        def rest(t, carry):
            rows_list = _gla_group_rows(t)
            cm, ri, ci = _gla_chunks_common(q_ref, k_ref, lr_ref, wgu_ref, bgu_ref, rows_list)
            ns = [t * GLA_GROUP + j for j in range(GLA_GROUP)]
            vs = [v_ref[rows, :] for rows in rows_list]
            dobs = [do_ref[rows, :] for rows in rows_list]
            stbs = [st_ref[0, n] for n in ns]
            dsts = [ds_s[n] for n in ns]
            dstbs = [d.astype(BF16) for d in dsts]
            qebs = [cc["qe"].astype(BF16) for cc in cm]
            kebs = [cc["ke"].astype(BF16) for cc in cm]
            kendbs = [cc["kend"].astype(BF16) for cc in cm]
            hms = [_head_lane_mask(hh) for hh in range(2)]
            qehs = [[jnp.where(hm, cc["qe"], 0.0).astype(BF16) for hm in hms] for cc in cm]
            kehs = [[jnp.where(hm, cc["ke"], 0.0).astype(BF16) for hm in hms] for cc in cm]
            heads = lambda x: [x[:, hh * GLA_DV:(hh + 1) * GLA_DV] for hh in range(2)]
            vhs, dohs = [heads(v) for v in vs], [heads(d) for d in dobs]

            dqe0 = [_dot(dob, stb) for dob, stb in zip(dobs, stbs)]
            dkend = [_dot(v, dstb) for v, dstb in zip(vs, dstbs)]
            dv0 = [_dot_nt(kb, dstb) for kb, dstb in zip(kendbs, dstbs)]
            a_t = [[jnp.where(ci >= ri, _dot_nt(kehs[j][hh], qebs[j]), 0.0).astype(BF16) for hh in range(2)]
                   for j in range(GLA_GROUP)]
            da = [[jnp.where(ri >= ci, _dot_nt(dohs[j][hh], vhs[j][hh]), 0.0).astype(BF16) for hh in range(2)]
                  for j in range(GLA_GROUP)]
            da_t = [[jnp.where(ci >= ri, _dot_nt(vhs[j][hh], dohs[j][hh]), 0.0).astype(BF16) for hh in range(2)]
                    for j in range(GLA_GROUP)]
            dv1 = [[_dot(a_t[j][hh], dohs[j][hh]) for hh in range(2)] for j in range(GLA_GROUP)]
            dqe1 = [[_dot(da[j][hh], kebs[j]) for hh in range(2)] for j in range(GLA_GROUP)]
            dke1 = [[_dot(da_t[j][hh], qehs[j][hh]) for hh in range(2)] for j in range(GLA_GROUP)]

            dbs, dzs = [], []
            for j, (rows, cc) in enumerate(zip(rows_list, cm)):
                qe, ke, kend, b, bl = cc["qe"], cc["ke"], cc["kend"], cc["b"], cc["bl"]
                dqe = dqe0[j] + jnp.where(hms[0], dqe1[j][0], 0.0) + jnp.where(hms[1], dqe1[j][1], 0.0)
                dke = jnp.where(hms[0], dke1[j][0], 0.0) + jnp.where(hms[1], dke1[j][1], 0.0)
                dv_ref[rows, :] = (dv0[j] + jnp.concatenate(dv1[j], axis=1)).astype(BF16)
                dq_ref[rows, :] = (dqe * jnp.exp(b) * (GLA_DK ** -0.5)).astype(BF16)
                dk_ref[rows, :] = (dke * jnp.exp(-b) + dkend[j] * jnp.exp(bl - b)).astype(BF16)
                ddec = jnp.sum(dsts[j] * stbs[j].astype(F32), axis=0, keepdims=True)
                dbl = jnp.sum(dkend[j] * kend, axis=0, keepdims=True) + ddec * cc["dec"]
                dbs.append(dqe * qe - dke * ke - dkend[j] * kend + jnp.where(last_row, dbl, 0.0))
            triu = (ci >= ri).astype(F32)
            dlas = [jnp.dot(triu, db, precision=lax.Precision.HIGHEST, preferred_element_type=F32) for db in dbs]
            dzs = [dla * (1.0 / GLA_TAU) * _sigmoid(-cc["z"]) for dla, cc in zip(dlas, cm)]
            dzbs = [dz.astype(BF16) for dz in dzs]
            dlrs = [_dot_nt(dzb, wgu_ref[...]) for dzb in dzbs]
            dws = [_dot_tn(lr_ref[rows, :], dzb) for rows, dzb in zip(rows_list, dzbs)]
            for rows, dlr in zip(rows_list, dlrs):
                dlr_ref[0, rows, :] = dlr
            dw_acc[...] += functools.reduce(lambda x, y: x + y, dws)
            db_acc[0:1, :] += jnp.sum(functools.reduce(lambda x, y: x + y, dzs), axis=0, keepdims=True)
            return carry

        lax.fori_loop(0, nc // GLA_GROUP, rest, 0)
        dwgu_ref[...] = dw_acc[...]
        dbgu_ref[...] = db_acc[...]
        comm_after()

    pair = pl.BlockSpec((s_len, LANE), lambda g: (0, g))
    return pl.pallas_call(
        body, name="gla_bwd_comm" if comm else "gla_bwd", grid=(2,),
        out_shape=[jax.ShapeDtypeStruct((s_len, GU_COLS), BF16), jax.ShapeDtypeStruct((s_len, GU_COLS), BF16),
                   jax.ShapeDtypeStruct((s_len, GLA_HEADS * GLA_DV), BF16),
                   jax.ShapeDtypeStruct((2, s_len, LANE), F32),
                   jax.ShapeDtypeStruct((LANE, GU_COLS), F32), jax.ShapeDtypeStruct((8, GU_COLS), F32)] + c_shapes,
        in_specs=[pl.BlockSpec((s_len, LANE), lambda g: (0, COL_QA // LANE + g)),
                  pl.BlockSpec((s_len, LANE), lambda g: (0, COL_KA // LANE + g)),
                  pl.BlockSpec((s_len, 2 * GLA_DV), lambda g: (0, (COL_VA - NP_F32) // (2 * GLA_DV) + g)),
                  pl.BlockSpec((s_len, LANE), lambda g: (0, (COL_LR - NP_F32) // LANE)),
                  pl.BlockSpec((None, LANE, LANE), lambda g: (layer, 0, g)),
                  pl.BlockSpec((None, 1, LANE), lambda g: (layer, 0, g)),
                  pl.BlockSpec((1, nc, 2 * GLA_DV, LANE), lambda g: (g, 0, 0, 0)),
                  pl.BlockSpec((s_len, 2 * GLA_DV), lambda g: (0, g))] + [ANY] * n_cin,
        out_specs=[pair, pair, pl.BlockSpec((s_len, 2 * GLA_DV), lambda g: (0, g)),
                   pl.BlockSpec((1, s_len, LANE), lambda g: (g, 0, 0)),
                   pl.BlockSpec((LANE, LANE), lambda g: (0, g)), pl.BlockSpec((8, LANE), lambda g: (0, g))]
        + [ANY] * len(c_shapes),
        scratch_shapes=[pltpu.VMEM((nc, 2 * GLA_DV, LANE), F32), pltpu.VMEM((nc, 8, LANE), F32),
                        pltpu.VMEM((LANE, LANE), F32), pltpu.VMEM((8, LANE), F32)] + c_scratch,
        compiler_params=_params(("arbitrary",), 56),
    )(pf, pf, pb, pb, wgu, bgu.reshape(bgu.shape[0], 1, GU_COLS), states, do, *(comm[1] if comm else []))


def _dil_bwd(pf, pb, cos, sin_signed, do, o_b, lse, comm=None):
    s_len = pf.shape[0]
    nblk = s_len // DIL_BLOCK
    prep_rows = 256
    scale = DIL_HD ** -0.5
    nc = len(comm[1]) if comm else 0

    def body(*refs):
        ((q_ref, k_ref, v_ref, cos_ref, sin_ref, do_ref, o_ref, lse_ref), (dq_ref, dk_ref, dv_ref),
         (qf, kf, vf, dof, dl, dqa, dka, dva, bias), cin, cout, csem) = _split_refs(refs, 8, 3, 9, comm)
        comm_before, comm_after = _comm_hooks(comm, cin, cout, csem)
        comm_before()
        _dil_fill_bias(bias)

        def prep(t, carry):
            rows = pl.ds(pl.multiple_of(t * prep_rows, prep_rows), prep_rows)
            cs, sn = cos_ref[rows, :], sin_ref[rows, :]
            qf[rows, :] = _rope(q_ref[rows, :], cs, sn) * scale
            kf[rows, :] = _rope(k_ref[rows, :], cs, sn)
            vf[rows, :] = v_ref[rows, :].astype(F32)
            dov = do_ref[rows, :].astype(F32)
            dof[rows, :] = dov
            dl[rows, :] = jnp.broadcast_to(jnp.sum(dov * o_ref[rows, :], axis=-1, keepdims=True), (prep_rows, DIL_HD))
            zero = jnp.zeros((prep_rows, DIL_HD), F32)
            dqa[rows, :] = zero
            dka[rows, :] = zero
            dva[rows, :] = zero
            return carry

        lax.fori_loop(0, s_len // prep_rows, prep, 0)

        for d in DIL_DILATIONS:
            if nblk // d == 2:
                units = DIL_GROUP // 2

                def whole(i, carry, d=d, units=units):
                    rows = [_strided(i + u * (d // units), 2 * DIL_BLOCK, d) for u in range(units)]
                    ld = [(qf[rw, :].astype(BF16), kf[rw, :].astype(BF16), vf[rw, :].astype(BF16),
                           dof[rw, :].astype(BF16)) for rw in rows]
                    both = bias[...].reshape(2 * DIL_BLOCK, 2 * DIL_BLOCK)
                    s = [_dot_nt(qb, kk) + both for qb, kk, _, _ in ld]
                    dp = [_dot_nt(dob, vv) for _, _, vv, dob in ld]
                    p = [jnp.exp(sv - lse_ref[rw, :][:, 0:1]) for sv, rw in zip(s, rows)]
                    ds = [(pv * (dpv - dl[rw, :][:, 0:1])).astype(BF16) for pv, dpv, rw in zip(p, dp, rows)]
                    pb = [pv.astype(BF16) for pv in p]
                    gq = [_dot(dsv, kk) for dsv, (_, kk, _, _) in zip(ds, ld)]
                    gk = [_dot_tn(dsv, qb) for dsv, (qb, _, _, _) in zip(ds, ld)]
                    gv = [_dot_tn(pv, dob) for pv, (_, _, _, dob) in zip(pb, ld)]
                    for rw, a, b, c in zip(rows, gq, gk, gv):
                        dqa[rw, :] += a
                        dka[rw, :] += b
                        dva[rw, :] += c
                    return carry

                lax.fori_loop(0, d // units, whole, 0)
                continue

            def pair(i, carry, d=d):
                idx = [_dil_pair_block(i, half, d, nblk) for half in range(DIL_GROUP)]
                rows = [(_strided(qs, DIL_BLOCK, d), _strided(ks, 2 * DIL_BLOCK, d)) for qs, ks, _ in idx]
                ld = [(qf[qr, :].astype(BF16), kf[kr, :].astype(BF16), vf[kr, :].astype(BF16),
                       dof[qr, :].astype(BF16)) for qr, kr in rows]
                s = [_dot_nt(qb, kk) + bias[sel] for (qb, kk, _, _), (_, _, sel) in zip(ld, idx)]
                dp = [_dot_nt(dob, vv) for _, _, vv, dob in ld]
                p = [jnp.exp(sv - lse_ref[qr, :][:, 0:1]) for sv, (qr, _) in zip(s, rows)]
                ds = [(pv * (dpv - dl[qr, :][:, 0:1])).astype(BF16) for pv, dpv, (qr, _) in zip(p, dp, rows)]
                pb = [pv.astype(BF16) for pv in p]
                gq = [_dot(dsv, kk) for dsv, (_, kk, _, _) in zip(ds, ld)]
                gk = [_dot_tn(dsv, qb) for dsv, (qb, _, _, _) in zip(ds, ld)]
                gv = [_dot_tn(pv, dob) for pv, (_, _, _, dob) in zip(pb, ld)]
                for (qr, kr), a, b, c in zip(rows, gq, gk, gv):
                    dqa[qr, :] += a
                    dka[kr, :] += b
                    dva[kr, :] += c
                return carry

            lax.fori_loop(0, nblk // DIL_GROUP, pair, 0)

        def fin(t, carry):
            rows = pl.ds(pl.multiple_of(t * prep_rows, prep_rows), prep_rows)
            cs, sn = cos_ref[rows, :], sin_ref[rows, :]
            gq, gk = dqa[rows, :] * scale, dka[rows, :]
            dq_ref[rows, :] = (gq * cs - pltpu.roll(gq, DIL_HD // 2, 1) * sn).astype(BF16)
            dk_ref[rows, :] = (gk * cs - pltpu.roll(gk, DIL_HD // 2, 1) * sn).astype(BF16)
            dv_ref[rows, :] = dva[rows, :].astype(BF16)
            return carry

        lax.fori_loop(0, s_len // prep_rows, fin, 0)
        comm_after()

    head = lambda base: pl.BlockSpec((s_len, DIL_HD), lambda h: (0, base // DIL_HD + h))
    table = pl.BlockSpec((s_len, DIL_HD), lambda h: (0, 0))
    out = pl.BlockSpec((s_len, DIL_HD), lambda h: (0, h))
    shp = jax.ShapeDtypeStruct((s_len, DIL_HEADS * DIL_HD), BF16)
    return pl.pallas_call(
        body, name="dil_bwd_comm" if comm else "dil_bwd", grid=(DIL_HEADS,),
        out_shape=[shp, shp, shp] + (_comm_out_shapes(*comm) if comm else []),
        in_specs=[head(COL_QB), head(COL_KB), head(COL_VB - NP_F32), table, table,
                  pl.BlockSpec((s_len, DIL_HD), lambda h: (0, DIL_HEADS + h)), out, out] + [ANY] * nc,
        out_specs=[out, out, out] + [ANY] * len(_comm_plumbing(comm)[1]),
        scratch_shapes=[pltpu.VMEM((s_len, DIL_HD), F32) for _ in range(8)]
        + [pltpu.VMEM((2, DIL_BLOCK, 2 * DIL_BLOCK), F32)] + (_comm_scratch(nc) if comm else []),
        compiler_params=_params(("arbitrary",), 56),
    )(pf, pf, pb, cos, sin_signed, do, o_b, lse, *(comm[1] if comm else []))


_PIECES = ((COL_Z, 1024), (COL_QA, 256), (COL_KA, 256), (COL_QB, 512), (COL_KB, 512), (COL_VA, 512), (COL_VB, 512),
           (COL_LR, 128))


def _in_bwd(pieces, w_new, x, dxo, g_pre, scale, comm=None, ts=256):
    s_len = x.shape[0]
    nc = len(comm[1]) if comm else 0
    nco = len(_comm_out_shapes(*comm)) if comm else 0
    npc = len(_PIECES)

    def body(*refs):
        p_refs = refs[:npc]
        w_ref, x_ref, dxo_ref, g_ref, sc_ref = refs[npc:npc + 5]
        cin, (dx_ref, sums_ref), cout = (refs[npc + 5:npc + 5 + nc], refs[npc + 5 + nc:npc + 7 + nc],
                                         refs[npc + 7 + nc:npc + 7 + nc + nco])
        comm_before, comm_after = _comm_hooks(comm, cin, cout, refs[npc + 7 + nc + nco:], steps=s_len // ts)
        comm_before()

        @pl.when(pl.program_id(0) == 0)
        def _():
            sums_ref[...] = jnp.zeros_like(sums_ref)

        dh = jnp.zeros((ts, D_MODEL), F32)
        for p_ref, (col, width) in zip(p_refs, _PIECES):
            dh += _dot_nt(p_ref[...], w_ref[:, col:col + width])
        xv = x_ref[...]
        rstd = lax.rsqrt(jnp.mean(xv * xv, axis=-1, keepdims=True) + EPS)
        xn = xv * rstd
        sums_ref[0:1, :] += jnp.sum(dh, axis=0, keepdims=True)
        sums_ref[1:2, :] += jnp.sum(dh * (xn * g_ref[...]), axis=0, keepdims=True)
        dr = dh * (1.0 + sc_ref[...])
        sums_ref[2:3, :] += jnp.sum(dr * xn, axis=0, keepdims=True)
        dxn = dr * g_ref[...]
        dx_ref[...] = dxo_ref[...] + rstd * (dxn - xn * jnp.mean(dxn * xn, axis=-1, keepdims=True))
        comm_after()

    (g_pre, g_spec), (scale, sc_spec) = _rowvec(g_pre), _rowvec(scale)
    tile = pl.BlockSpec((ts, D_MODEL), lambda i: (i, 0))
    return pl.pallas_call(
        body, name="in_bwd_comm" if comm else "in_bwd", grid=(s_len // ts,),
        out_shape=[jax.ShapeDtypeStruct((s_len, D_MODEL), F32), jax.ShapeDtypeStruct((8, D_MODEL), F32)]
        + (_comm_out_shapes(*comm) if comm else []),
        in_specs=[pl.BlockSpec((ts, width), lambda i: (i, 0)) for _, width in _PIECES]
        + [pl.BlockSpec((D_MODEL, NP), lambda i: (0, 0)), tile, tile, g_spec, sc_spec] + [ANY] * nc,
        out_specs=[tile, pl.BlockSpec((8, D_MODEL), lambda i: (0, 0))] + [ANY] * nco,
        scratch_shapes=_comm_scratch(nc) if comm else [],
        compiler_params=_params(("arbitrary",), 56),
    )(*pieces, w_new, x, dxo, g_pre, scale, *(comm[1] if comm else []))


def _w_in_to_kernel(gathered, comm=None, tr=128):
    n_cin, c_shapes, c_scratch = _comm_plumbing(comm)

    def body(*refs):
        (g_ref,), (o_ref,), _, cin, cout, csem = _split_refs(refs, 1, 1, 0, comm)
        comm_before, comm_after = _comm_hooks(comm, cin, cout, csem, steps=D_MODEL // tr)
        comm_before()
        cols = jnp.concatenate([g_ref[k].astype(F32) for k in range(N_DEV)], axis=1)
        pad = jnp.zeros((tr, LANE - GLA_LOWRANK), F32)
        o_ref[...] = jnp.concatenate(
            [cols[:, 1024:1536], cols[:, 3088:3600], cols[:, 0:512], cols[:, 1552:2576], cols[:, 512:1024],
             cols[:, 2576:3088], cols[:, 1536:1552], pad], axis=1).astype(BF16)
        comm_after()

    return pl.pallas_call(
        body, name="w_in_to_kernel_comm" if comm else "w_in_to_kernel", grid=(D_MODEL // tr,),
        out_shape=[jax.ShapeDtypeStruct((D_MODEL, NP), BF16)] + c_shapes,
        in_specs=[pl.BlockSpec((N_DEV, tr, W_IN_SHARD), lambda i: (0, i, 0))] + [ANY] * n_cin,
        out_specs=[pl.BlockSpec((tr, NP), lambda i: (i, 0))] + [ANY] * len(c_shapes),
        scratch_shapes=c_scratch,
        compiler_params=_params(("arbitrary",)),
    )(gathered, *(comm[1] if comm else []))


def _grad_w_in(h, pieces, ts=512, tr=128):
    s_len = h.shape[0]
    steps = s_len // ts

    def body(*refs):
        h_ref, p_refs = refs[0], refs[1:1 + len(_PIECES)]
        o_ref, acc = refs[1 + len(_PIECES):]

        @pl.when(pl.program_id(0) == 0)
        def _():
            acc[...] = jnp.zeros_like(acc)

        hv = h_ref[...]
        for p_ref, (col, width) in zip(p_refs, _PIECES):
            acc[:, col:col + width] += _dot_tn(hv, p_ref[...])

        @pl.when(pl.program_id(0) == steps - 1)
        def _():
            def rows_out(t, carry):
                rows = pl.ds(pl.multiple_of(t * tr, tr), tr)
                g = acc[rows, :]
                cols = jnp.concatenate(
                    [g[:, COL_QA:COL_QB], g[:, COL_VA:COL_VB], g[:, 0:512], g[:, COL_LR:COL_LR + GLA_LOWRANK],
                     g[:, COL_QB:COL_VA], g[:, COL_VB:COL_LR], g[:, 512:1024]], axis=1)
                for k in range(N_DEV):
                    o_ref[k, rows, :] = cols[:, W_IN_SHARD * k:W_IN_SHARD * (k + 1)].astype(BF16)
                return carry

            lax.fori_loop(0, D_MODEL // tr, rows_out, 0)

    return pl.pallas_call(
        body, name="grad_w_in", grid=(steps,),
        out_shape=jax.ShapeDtypeStruct((N_DEV, D_MODEL, W_IN_SHARD), BF16),
        in_specs=[pl.BlockSpec((ts, D_MODEL), lambda i: (i, 0))]
        + [pl.BlockSpec((ts, width), lambda i: (i, 0)) for _, width in _PIECES],
        out_specs=pl.BlockSpec((N_DEV, D_MODEL, W_IN_SHARD), lambda i: (0, 0, 0)),
        scratch_shapes=[pltpu.VMEM((D_MODEL, NP), F32)],
        compiler_params=_params(("arbitrary",), 56),
    )(h, *pieces)


def _matmul_tn(a, b, name, bn, ts=512):
    s_len, m = a.shape
    n = b.shape[1]
    steps = s_len // ts

    def body(a_ref, b_ref, o_ref, acc):
        @pl.when(pl.program_id(1) == 0)
        def _():
            acc[...] = jnp.zeros_like(acc)

        acc[...] += _dot_tn(a_ref[...], b_ref[...])

        @pl.when(pl.program_id(1) == steps - 1)
        def _():
            o_ref[...] = acc[...].astype(BF16)

    return pl.pallas_call(
        body, name=name, grid=(n // bn, steps),
        out_shape=jax.ShapeDtypeStruct((m, n), BF16),
        in_specs=[pl.BlockSpec((ts, m), lambda j, i: (i, 0)), pl.BlockSpec((ts, bn), lambda j, i: (i, j))],
        out_specs=pl.BlockSpec((m, bn), lambda j, i: (0, j)),
        scratch_shapes=[pltpu.VMEM((m, bn), F32)],
        compiler_params=_params(("arbitrary", "arbitrary"), 40),
    )(a, b)


def _adam_math(w, g, m, v):
    m = ADAM_B1 * m + (1.0 - ADAM_B1) * g
    v = ADAM_B2 * v + (1.0 - ADAM_B2) * (g * g)
    m_hat = m / (1.0 - ADAM_B1 ** ADAM_STEP)
    v_hat = v / (1.0 - ADAM_B2 ** ADAM_STEP)
    delta = -ADAM_LR * (m_hat / (jnp.sqrt(v_hat) + ADAM_EPS) + ADAM_WD * w)
    return delta, m, v


def _adamw(w, parts, m, v, name, tr):
    r, cdim = w.shape
    n_parts = parts.shape[0]

    def body(w_ref, p_ref, m_ref, v_ref, g_ref, d_ref, nm_ref, nv_ref):
        g = p_ref[0].astype(F32)
        for k in range(1, n_parts):
            g = g + p_ref[k].astype(F32)
        g_ref[...] = g
        d_ref[...], nm_ref[...], nv_ref[...] = _adam_math(w_ref[...], g, m_ref[...], v_ref[...])

    tile = pl.BlockSpec((tr, cdim), lambda i: (i, 0))
    shp = jax.ShapeDtypeStruct((r, cdim), F32)
    return pl.pallas_call(
        body, name=name, grid=(r // tr,), out_shape=(shp, shp, shp, shp),
        in_specs=[tile, pl.BlockSpec((n_parts, tr, cdim), lambda i: (0, i, 0)), tile, tile],
        out_specs=(tile, tile, tile, tile),
        compiler_params=_params(("arbitrary",), 40),
    )(w, parts, m, v)


def _adamw_layers(w, parts, m, v, name, tr):
    n_layers, r, cdim = w.shape

    def body(*refs):
        w_ref, p_refs, (m_ref, v_ref) = refs[0], refs[1:1 + n_layers], refs[1 + n_layers:3 + n_layers]
        g_ref, d_ref, nm_ref, nv_ref = refs[3 + n_layers:]
        for l, p_ref in enumerate(p_refs):
            @pl.when(pl.program_id(0) == l)
            def _(p_ref=p_ref):
                g = p_ref[0].astype(F32)
                for k in range(1, p_ref.shape[0]):
                    g = g + p_ref[k].astype(F32)
                g_ref[0] = g
                d_ref[0], nm_ref[0], nv_ref[0] = _adam_math(w_ref[0], g, m_ref[0], v_ref[0])

    tile = pl.BlockSpec((1, tr, cdim), lambda l, i: (l, i, 0))
    part = lambda own: pl.BlockSpec((parts[own].shape[0], tr, cdim), lambda l, i: (0, jnp.where(l == own, i, 0), 0))
    shp = jax.ShapeDtypeStruct(w.shape, F32)
    return pl.pallas_call(
        body, name=name, grid=(n_layers, r // tr), out_shape=(shp, shp, shp, shp),
        in_specs=[tile] + [part(l) for l in range(n_layers)] + [tile, tile],
        out_specs=(tile, tile, tile, tile),
        compiler_params=_params(("arbitrary", "arbitrary"), 40),
    )(w, *parts, m, v)


def _row(vec, width):
    vec = vec.reshape(1, -1)
    return jnp.pad(vec, ((0, 0), (0, width - vec.shape[1])))


def kernel(x, c, w_ada, b_ada, g_pre, w_in, w_gate_up, b_gate_up, g_gla, g_dil, w_out, g_post, loss_target, m_w_ada, m_b_ada, m_g_pre, m_w_in, m_w_gate_up, m_b_gate_up, m_g_gla, m_g_dil, m_w_out, m_g_post, v_w_ada, v_b_ada, v_g_pre, v_w_in, v_w_gate_up, v_b_gate_up, v_g_gla, v_g_dil, v_w_out, v_g_post):
    px, py, pc = _my_position()
    me = _linear(px, py, pc)
    xs = x[0]
    target = loss_target[0]
    s_len = xs.shape[0]
    assert s_len % (DIL_BLOCK * max(DIL_DILATIONS) * 2) == 0 and xs.shape[1] == D_MODEL

    w_in_b, w_out_b = w_in.astype(BF16), w_out.astype(BF16)
    c_rows, wgu_all, w_in_all = _comm_call(
        "gather", [jnp.pad(c, ((0, 7), (0, 0))), w_gate_up.reshape(DEPTH * GLA_LOWRANK, GU_SHARD), w_in_b[0]],
        "gather_first")
    c_all = c_rows.reshape(N_DEV, 8, D_MODEL)[:, 0]
    mod_part = _mod_fwd(c_all, w_ada)
    w_new, mod_all = _w_in_to_kernel(w_in_all.reshape(N_DEV, D_MODEL, W_IN_SHARD),
                                     comm=("gather", [mod_part.reshape(DEPTH * N_DEV, ADA_SHARD)]))
    mod_all = mod_all.reshape(N_DEV, DEPTH, N_DEV, ADA_SHARD)
    mod_mine = lax.dynamic_index_in_dim(mod_all, me, axis=2, keepdims=False)
    mod = jnp.transpose(mod_mine, (1, 0, 2)).reshape(DEPTH, 3 * D_MODEL) + b_ada
    wgu_full = jnp.transpose(wgu_all.reshape(N_DEV, DEPTH, GLA_LOWRANK, GU_SHARD), (1, 2, 0, 3)).reshape(
        DEPTH, GLA_LOWRANK, GU_COLS)
    wgu_pad = jnp.pad(wgu_full, ((0, 0), (0, LANE - GLA_LOWRANK), (0, 0))).astype(BF16)

    cos, sin_signed = _rope_tables(s_len)
    g_heads = jnp.concatenate([g_gla, g_dil], axis=1)

    saved = []
    xl = xs
    for l in range(DEPTH):
        shift, scale, gate = ((mod, l, k) for k in range(3))
        if l > 0:
            w_new = _w_in_to_kernel(w_in_all.reshape(N_DEV, D_MODEL, W_IN_SHARD))[0]
        pf, pb, h, w_out_l = _prenorm_proj(xl, (g_pre, l, 0), scale, shift, w_new, comm=("gather", [w_out_b[l]]))
        o_a, states = _gla_fwd(pf, pb, wgu_pad, b_gate_up, l)
        if l + 1 < DEPTH:
            o_b, lse, w_in_all = _dil_fwd(pf, pb, cos, sin_signed, comm=("gather", [w_in_b[l + 1]]))
        else:
            o_b, lse = _dil_fwd(pf, pb, cos, sin_signed)
        if l + 1 < DEPTH:
            x_next, y, u = _post_fwd(o_a, o_b, pf, (g_heads, l, 0), w_out_l, xl, gate, (g_post, l, 0))
        else:
            dx, y, u, loss_part = _post_fwd(o_a, o_b, pf, (g_heads, l, 0), w_out_l, xl, gate, (g_post, l, 0),
                                            target=target)
        saved.append((xl, scale, gate, w_new, w_out_l, pf, pb, h, o_a, states, o_b, lse, y, u))
        xl = x_next

    small_rows = []
    gin_slots, gin_parts, gout_parts = None, [None] * DEPTH, [None] * DEPTH
    for l in reversed(range(DEPTH)):
        x_in, scale, gate, w_new, w_out_l, pf, pb, h, o_a, states, o_b, lse, y, u = saved[l]
        du, do, dz, sums_post = _post_bwd(dx, u, gate, (g_post, l, 0), w_out_l, o_a, o_b, pf, (g_heads, l, 0))
        gout_slots = _matmul_tn(y, du, "grad_w_out", 512)
        dq_a, dk_a, dv_a, dlr2, dwgu, dbgu, arrived = _gla_bwd(pf, pb, wgu_pad, b_gate_up, l, states, do,
                                                               comm=("exchange", [gout_slots]))
        gout_parts[l] = arrived.reshape(N_DEV, OUT_SHARD, D_MODEL)
        if gin_slots is not None:
            dq_b, dk_b, dv_b, arrived, _, _ = _dil_bwd(pf, pb, cos, sin_signed, do, o_b, lse,
                                                       comm=("pairsum_exchange", [gin_slots]))
            gin_parts[l + 1] = arrived.reshape(N_DEV // 2, D_MODEL, W_IN_SHARD)
        else:
            dq_b, dk_b, dv_b = _dil_bwd(pf, pb, cos, sin_signed, do, o_b, lse)
        dlr = (dlr2[0] + dlr2[1]).astype(BF16)
        pieces = (dz, dq_a, dk_a, dq_b, dk_b, dv_a, dv_b, dlr)
        gin_slots = _grad_w_in(h, pieces).reshape(N_DEV * D_MODEL, W_IN_SHARD)
        if l == 0:
            dx, sums_in, arrived, _, _ = _in_bwd(pieces, w_new, x_in, dx, (g_pre, l, 0), scale,
                                                 comm=("pairsum_exchange", [gin_slots]))
            gin_parts[0] = arrived.reshape(N_DEV // 2, D_MODEL, W_IN_SHARD)
        else:
            dx, sums_in = _in_bwd(pieces, w_new, x_in, dx, (g_pre, l, 0), scale)
        dmod = jnp.concatenate([sums_in[0], sums_in[1], sums_post[0]])
        vecs = jnp.concatenate([sums_in[2], sums_post[1], sums_post[2], dbgu[0]])
        small_rows[0:0] = [_row(dmod, 4096), _row(vecs, 4096), _row(dwgu[:GLA_LOWRANK], 4096)]
    grad_x = dx[None]

    flat = lambda a, rows: a.reshape(rows, a.shape[-1])
    r_ada = DEPTH * D_MODEL
    g_w_in, d_w_in, nm_w_in, nv_w_in = _adamw_layers(w_in, gin_parts, m_w_in, v_w_in, "adamw_w_in", 256)
    g_w_out, d_w_out, nm_w_out, nv_w_out = _adamw_layers(w_out, gout_parts, m_w_out, v_w_out, "adamw_w_out", 128)

    small_rows += [_row(loss_part[0, 0:1], 4096), jnp.zeros((1, 4096), F32)]
    small = _all_gather(jnp.concatenate(small_rows, axis=0), "gather_small").reshape(N_DEV, 8, 4096)
    dmod_all = jnp.stack([small[:, 0, :3 * D_MODEL], small[:, 3, :3 * D_MODEL]])
    dmod_cols = lax.dynamic_slice_in_dim(dmod_all, me * ADA_SHARD, ADA_SHARD, axis=2)
    gwa = _w_ada_grad(c_all, dmod_cols).reshape(1, r_ada, ADA_SHARD)
    g_w_ada, d_w_ada, nm_w_ada, nv_w_ada = (
        t.reshape(w_ada.shape) for t in _adamw(flat(w_ada, r_ada), gwa, flat(m_w_ada, r_ada), flat(v_w_ada, r_ada),
                                               "adamw_w_ada", 256))

    where = ((0, 0), (1, 0), (1, 1024), (1, 2048), (1, 2560), (1, 3072))
    replicated = [(b_ada, m_b_ada, v_b_ada), (g_pre, m_g_pre, v_g_pre), (g_post, m_g_post, v_g_post),
                  (g_gla, m_g_gla, v_g_gla), (g_dil, m_g_dil, v_g_dil), (b_gate_up, m_b_gate_up, v_b_gate_up)]
    updated, loss = _adamw_replicated(small, replicated, where, loss_at=(6, 0))
    ((g_b_ada, d_b_ada, nm_b_ada, nv_b_ada), (g_g_pre, d_g_pre, nm_g_pre, nv_g_pre),
     (g_g_post, d_g_post, nm_g_post, nv_g_post), (g_g_gla, d_g_gla, nm_g_gla, nv_g_gla),
     (g_g_dil, d_g_dil, nm_g_dil, nv_g_dil), (g_b_gu, d_b_gu, nm_b_gu, nv_b_gu)) = updated
    gu_parts = jnp.stack([small[:, 2], small[:, 5]], axis=1).reshape(N_DEV, DEPTH, GLA_LOWRANK, GU_COLS)
    gu_parts = lax.dynamic_slice_in_dim(gu_parts, me * GU_SHARD, GU_SHARD, axis=3).reshape(
        N_DEV, DEPTH * GLA_LOWRANK, GU_SHARD)
    r_gu = DEPTH * GLA_LOWRANK
    g_w_gu, d_w_gu, nm_w_gu, nv_w_gu = (
        t.reshape(w_gate_up.shape) for t in _adamw(flat(w_gate_up, r_gu), gu_parts, flat(m_w_gate_up, r_gu),
                                                   flat(v_w_gate_up, r_gu), "adamw_w_gate_up", r_gu))
    return (loss, grad_x,
            g_w_ada, g_b_ada, g_g_pre, g_w_in, g_w_gu, g_b_gu, g_g_gla, g_g_dil, g_w_out, g_g_post,
            d_w_ada, d_b_ada, d_g_pre, d_w_in, d_w_gu, d_b_gu, d_g_gla, d_g_dil, d_w_out, d_g_post,
            nm_w_ada, nm_b_ada, nm_g_pre, nm_w_in, nm_w_gu, nm_b_gu, nm_g_gla, nm_g_dil, nm_w_out, nm_g_post,
            nv_w_ada, nv_b_ada, nv_g_pre, nv_w_in, nv_w_gu, nv_b_gu, nv_g_gla, nv_g_dil, nv_w_out, nv_g_post)


def _adamw_replicated(small, params, where, loss_at):
    n_parts = small.shape[0]

    def body(*refs):
        s_ref, p_refs, o_refs = refs[0], refs[1:1 + 3 * len(params)], refs[1 + 3 * len(params):]
        total = s_ref[0]
        for k in range(1, n_parts):
            total = total + s_ref[k]
        for i, (row, col) in enumerate(where):
            w_ref, m_ref, v_ref = p_refs[3 * i:3 * i + 3]
            n = w_ref.shape[1]
            g = jnp.concatenate([total[row + 3 * l:row + 3 * l + 1, col:col + n] for l in range(DEPTH)], axis=0)
            o_refs[4 * i][...] = g
            o_refs[4 * i + 1][...], o_refs[4 * i + 2][...], o_refs[4 * i + 3][...] = _adam_math(
                w_ref[...], g, m_ref[...], v_ref[...])
        o_refs[-1][...] = jnp.broadcast_to(total[loss_at[0]:loss_at[0] + 1, loss_at[1]:loss_at[1] + 1], (8, LANE))

    flat = [a for p in params for a in p]
    shapes = [jax.ShapeDtypeStruct(p[0].shape, F32) for p in params for _ in range(4)]
    outs = pl.pallas_call(body, name="adamw_replicated",
                          out_shape=shapes + [jax.ShapeDtypeStruct((8, LANE), F32)])(small, *flat)
    return [tuple(outs[4 * i:4 * i + 4]) for i in range(len(params))], outs[-1][0, 0]
```

```python
import functools
import math

import jax
import jax.numpy as jnp
from jax import lax
from jax.experimental import pallas as pl
from jax.experimental.pallas import tpu as pltpu

F32 = jnp.float32
BF16 = jnp.bfloat16

N_DEV = 8
D_MODEL = 1024
DEPTH = 2
GLA_HEADS = 4
GLA_DK = 64
GLA_DV = 128
GLA_CHUNK = 64
GLA_TAU = 16.0
GLA_LOWRANK = 16
DIL_HEADS = 4
DIL_HD = 128
DIL_BLOCK = 128
DIL_DILATIONS = (1, 4, 16)
ROPE_THETA = 10000.0
EPS = 1e-6
IN_COLS = 3600
W_IN_SHARD = IN_COLS // N_DEV
ADA_SHARD = 3 * D_MODEL // N_DEV
OUT_SHARD = D_MODEL // N_DEV
GU_COLS = GLA_HEADS * GLA_DK
GU_SHARD = GU_COLS // N_DEV

ADAM_LR = 0.001
ADAM_B1 = 0.9
ADAM_B2 = 0.999
ADAM_EPS = 1e-08
ADAM_WD = 0.01
ADAM_STEP = 10

NP = 3712
COL_Z, COL_QA, COL_KA, COL_QB, COL_KB, COL_VA, COL_VB, COL_LR = 0, 1024, 1280, 1536, 2048, 2560, 3072, 3584
NP_F32 = COL_VA
NP_BF16 = NP - NP_F32
LANE = 128
MASK_VALUE = -1e30

MESH = pl.DeviceIdType.MESH
ANY = pl.BlockSpec(memory_space=pl.ANY)


def _params(sem=None, vmem_mb=None):
    kw = {}
    if sem is not None:
        kw["dimension_semantics"] = sem
    if vmem_mb is not None:
        kw["vmem_limit_bytes"] = vmem_mb * 1024 * 1024
    return pltpu.CompilerParams(**kw)


def _dot(a, b):
    return jnp.dot(a, b, preferred_element_type=F32)


def _dot_nt(a, b):
    return lax.dot_general(a, b, (((1,), (1,)), ((), ())), preferred_element_type=F32)


def _dot_tn(a, b):
    return lax.dot_general(a, b, (((0,), (0,)), ((), ())), preferred_element_type=F32)


def _sigmoid(z):
    return 1.0 / (1.0 + jnp.exp(-z))


def _log_sigmoid(z):
    return jnp.minimum(z, 0.0) - jnp.log(1.0 + jnp.exp(-jnp.abs(z)))


def _rowvec(v, width=D_MODEL):
    arr, row, cb = v
    return arr.reshape(arr.shape[0], 1, arr.shape[1]), pl.BlockSpec((None, 1, width), lambda *_: (row, 0, cb))


def _my_position():
    return lax.axis_index("x"), lax.axis_index("y"), lax.axis_index("c")


def _linear(px, py, pc):
    return 4 * px + 2 * py + pc


def _gather_phase(phase, x_ref, out_ref, send_sem, recv_sem, local_sem):
    m = x_ref.shape[0]
    x, y, c = _my_position()
    me, sibling = (x, y, c), (x, y, 1 - c)
    chips = [(1 - x, y), (x, 1 - y), (1 - x, 1 - y)]

    def rows(px, py, pc):
        return out_ref.at[pl.ds(_linear(px, py, pc) * m, m), :]

    def copy(k, block, to, src=None):
        return pltpu.make_async_remote_copy(
            src_ref=rows(*block) if src is None else src, dst_ref=rows(*block),
            send_sem=send_sem(k), recv_sem=recv_sem(k), device_id=to, device_id_type=MESH)

    mine = pltpu.make_async_copy(x_ref, rows(*me), local_sem)
    first = [copy(0, me, sibling, src=x_ref)] + [copy(1 + j, me, (*chip, c), src=x_ref) for j, chip in enumerate(chips)]
    passed = [copy(4 + j, (*chip, c), sibling) for j, chip in enumerate(chips)]
    if phase == "start":
        mine.start()
        for cp in first:
            cp.start()
    elif phase == "forward":
        for j, chip in enumerate(chips):
            copy(1 + j, (*chip, c), me).wait_recv()
            passed[j].start()
    else:
        copy(0, sibling, me).wait_recv()
        for j, chip in enumerate(chips):
            copy(4 + j, (*chip, 1 - c), me).wait_recv()
        for cp in first + passed:
            cp.wait_send()
        mine.wait()


def _exchange_phase(phase, x_ref, out_ref, send_sem, recv_sem, local_sem):
    m = x_ref.shape[0] // N_DEV
    x, y, c = _my_position()
    me = _linear(x, y, c)

    def rows(ref, idx):
        return ref.at[pl.ds(idx * m, m), :]

    peers = [(1 - x if j & 4 else x, 1 - y if j & 2 else y, 1 - c if j & 1 else c) for j in range(1, N_DEV)]
    local = pltpu.make_async_copy(rows(x_ref, me), rows(out_ref, me), local_sem)
    sends = [pltpu.make_async_remote_copy(
        src_ref=rows(x_ref, _linear(*peer)), dst_ref=rows(out_ref, me),
        send_sem=send_sem(j), recv_sem=recv_sem(j), device_id=peer, device_id_type=MESH) for j, peer in enumerate(peers)]
    if phase == "start":
        local.start()
        for cp in sends:
            cp.start()
    else:
        for j, peer in enumerate(peers):
            pltpu.make_async_remote_copy(
                src_ref=rows(x_ref, _linear(*peer)), dst_ref=rows(out_ref, _linear(*peer)),
                send_sem=send_sem(j), recv_sem=recv_sem(j), device_id=peer, device_id_type=MESH).wait_recv()
        for cp in sends:
            cp.wait_send()
        local.wait()


def _pairsum_exchange_phase(phase, x_ref, out_refs, send_sem, recv_sem, local_sem):
    out_ref, stage_ref, pair_ref = out_refs
    m, n = x_ref.shape[0] // N_DEV, x_ref.shape[1]
    x, y, c = _my_position()
    mine = 2 * x + y
    chips = [(qx, qy) for qx in range(2) for qy in range(2)]
    others = [(1 - x, y), (x, 1 - y), (1 - x, 1 - y)]

    def rows(ref, idx):
        return ref.at[pl.ds(idx * m, m), :]

    def remote(src, dst, k, to):
        return pltpu.make_async_remote_copy(src_ref=src, dst_ref=dst, send_sem=send_sem(k), recv_sem=recv_sem(k),
                                            device_id=to, device_id_type=MESH)

    to_sibling = [remote(rows(x_ref, _linear(qx, qy, 1 - c)), rows(stage_ref, q), q, (x, y, 1 - c))
                  for q, (qx, qy) in enumerate(chips)]
    to_chips = [remote(rows(pair_ref, 2 * qx + qy), rows(out_ref, mine), 4 + j, (qx, qy, c))
                for j, (qx, qy) in enumerate(others)]
    keep = pltpu.make_async_copy(rows(pair_ref, mine), rows(out_ref, mine), local_sem)
    if phase == "start":
        for cp in to_sibling:
            cp.start()
    elif phase == "reduce":
        for cp in to_sibling:
            cp.wait_recv()

        def through_vmem(a_buf, b_buf, sems):
            tr = 128
            loads = [(pltpu.make_async_copy(rows(x_ref, _linear(qx, qy, c)), a_buf.at[q % 2], sems.at[q % 2]),
                      pltpu.make_async_copy(rows(stage_ref, q), b_buf.at[q % 2], sems.at[2 + q % 2]))
                     for q, (qx, qy) in enumerate(chips)]
            stores = [pltpu.make_async_copy(a_buf.at[q % 2], rows(pair_ref, q), sems.at[4 + q % 2]) for q in range(4)]
            for q in range(4):
                if q >= 2:
                    stores[q - 2].wait()
                for cp in loads[q]:
                    cp.start()
                for cp in loads[q]:
                    cp.wait()

                def add(r, carry, q=q):
                    tile = pl.ds(pl.multiple_of(r * tr, tr), tr)
                    a_buf[q % 2, tile, :] = (a_buf[q % 2, tile, :].astype(F32)
                                             + b_buf[q % 2, tile, :].astype(F32)).astype(x_ref.dtype)
                    return carry

                lax.fori_loop(0, m // tr, add, 0)
                stores[q].start()
            stores[2].wait()
            stores[3].wait()

        pl.run_scoped(through_vmem, pltpu.VMEM((2, m, n), x_ref.dtype), pltpu.VMEM((2, m, n), x_ref.dtype),
                      pltpu.SemaphoreType.DMA((6,)))
    elif phase == "send":
        keep.start()
        for cp in to_chips:
            cp.start()
    else:
        for j, (qx, qy) in enumerate(others):
            remote(rows(pair_ref, mine), rows(out_ref, 2 * qx + qy), 4 + j, (qx, qy, c)).wait_recv()
        for cp in to_sibling + to_chips:
            cp.wait_send()
        keep.wait()


_COMM_PHASES = {"gather": (_gather_phase, ("start", "forward", "finish")),
                "exchange": (_exchange_phase, ("start", "finish")),
                "pairsum_exchange": (_pairsum_exchange_phase, ("start", "reduce", "send", "finish"))}


def _comm_scratch(n_arrays):
    return [pltpu.SemaphoreType.DMA((n_arrays, 7)), pltpu.SemaphoreType.DMA((n_arrays, 7)),
            pltpu.SemaphoreType.DMA((n_arrays,))]


def _comm_run(kind, phases, x_refs, out_refs, send_sems, recv_sems, local_sems):
    fn = _COMM_PHASES[kind][0]
    per = len(out_refs) // len(x_refs)
    for phase in phases:
        for a, x_ref in enumerate(x_refs):
            outs = out_refs[a] if per == 1 else tuple(out_refs[per * a:per * (a + 1)])
            fn(phase, x_ref, outs, lambda k, a=a: send_sems.at[a, k], lambda k, a=a: recv_sems.at[a, k],
               local_sems.at[a])


def _comm_out_shapes(kind, arrays):
    if kind == "pairsum_exchange":
        return [jax.ShapeDtypeStruct((a.shape[0] // 2, a.shape[1]), a.dtype) for a in arrays for _ in range(3)]
    return [jax.ShapeDtypeStruct((N_DEV * a.shape[0], a.shape[1]) if kind == "gather" else a.shape, a.dtype)
            for a in arrays]


def _comm_call(kind, arrays, name):
    n = len(arrays)
    shapes = _comm_out_shapes(kind, arrays)

    def body(*refs):
        _comm_run(kind, _COMM_PHASES[kind][1], refs[:n], refs[n:n + len(shapes)], *refs[n + len(shapes):])

    return pl.pallas_call(body, name=name, out_shape=shapes, in_specs=[ANY] * n, out_specs=[ANY] * len(shapes),
                          scratch_shapes=_comm_scratch(n))(*arrays)


def _all_gather(xs, name):
    return _comm_call("gather", [xs], name)[0]


def _mod_fwd(c_all, w_ada):
    def body(c_ref, w_ref, o_ref):
        cv = c_ref[...]
        sc = cv * _sigmoid(cv)
        o_ref[0] = _dot(sc.astype(BF16), w_ref[0].astype(BF16))

    return pl.pallas_call(
        body, name="mod_fwd", grid=(DEPTH,),
        out_shape=jax.ShapeDtypeStruct((DEPTH, N_DEV, ADA_SHARD), F32),
        in_specs=[pl.BlockSpec((N_DEV, D_MODEL), lambda l: (0, 0)),
                  pl.BlockSpec((1, D_MODEL, ADA_SHARD), lambda l: (l, 0, 0))],
        out_specs=pl.BlockSpec((1, N_DEV, ADA_SHARD), lambda l: (l, 0, 0)),
        compiler_params=_params(("arbitrary",)),
    )(c_all, w_ada)


def _w_ada_grad(c_all, dmod_cols):
    def body(c_ref, d_ref, o_ref):
        cv = c_ref[...]
        sc = cv * _sigmoid(cv)
        o_ref[0] = lax.dot_general(sc, d_ref[0], (((0,), (0,)), ((), ())), precision=lax.Precision.HIGHEST,
                                   preferred_element_type=F32)

    return pl.pallas_call(
        body, name="w_ada_grad", grid=(DEPTH,),
        out_shape=jax.ShapeDtypeStruct((DEPTH, D_MODEL, ADA_SHARD), F32),
        in_specs=[pl.BlockSpec((N_DEV, D_MODEL), lambda l: (0, 0)),
                  pl.BlockSpec((1, N_DEV, ADA_SHARD), lambda l: (l, 0, 0))],
        out_specs=pl.BlockSpec((1, D_MODEL, ADA_SHARD), lambda l: (l, 0, 0)),
        compiler_params=_params(("arbitrary",)),
    )(c_all, dmod_cols)


def _comm_plumbing(comm):
    if not comm:
        return 0, [], []
    return len(comm[1]), _comm_out_shapes(*comm), _comm_scratch(len(comm[1]))


def _split_refs(refs, n_in, n_out, n_scratch, comm):
    ci, shapes, _ = _comm_plumbing(comm)
    co = len(shapes)
    a, b, c = n_in + ci, n_in + ci + n_out, n_in + ci + n_out + co
    return refs[:n_in], refs[a:b], refs[c:c + n_scratch], refs[n_in:a], refs[b:c], refs[c + n_scratch:]


def _prenorm_proj(x, g_pre, scale, shift, w_new, comm=None, ts=512):
    s_len = x.shape[0]
    n_cin, c_shapes, c_scratch = _comm_plumbing(comm)

    def body(*refs):
        (x_ref, g_ref, sc_ref, sh_ref, w_ref), (pf_ref, pb_ref, h_ref), _, cin, cout, csem = _split_refs(
            refs, 5, 3, 0, comm)
        comm_before, comm_after = _comm_hooks(comm, cin, cout, csem, steps=s_len // ts)
        comm_before()
        xv = x_ref[...]
        rstd = lax.rsqrt(jnp.mean(xv * xv, axis=-1, keepdims=True) + EPS)
        h = (xv * rstd * g_ref[...]) * (1.0 + sc_ref[...]) + sh_ref[...]
        hb = h.astype(BF16)
        h_ref[...] = hb
        for j in range(0, NP, 512):
            w = min(512, NP - j)
            acc = _dot(hb, w_ref[:, j:j + w])
            if j < NP_F32:
                pf_ref[:, j:j + w] = acc
            else:
                pb_ref[:, j - NP_F32:j - NP_F32 + w] = acc.astype(BF16)
        comm_after()

    (g_pre, g_spec), (scale, sc_spec), (shift, sh_spec) = _rowvec(g_pre), _rowvec(scale), _rowvec(shift)
    return pl.pallas_call(
        body, name="prenorm_proj_comm" if comm else "prenorm_proj", grid=(s_len // ts,),
        out_shape=[jax.ShapeDtypeStruct((s_len, NP_F32), F32), jax.ShapeDtypeStruct((s_len, NP_BF16), BF16),
                   jax.ShapeDtypeStruct((s_len, D_MODEL), BF16)] + c_shapes,
        in_specs=[pl.BlockSpec((ts, D_MODEL), lambda i: (i, 0)), g_spec, sc_spec, sh_spec,
                  pl.BlockSpec((D_MODEL, NP), lambda i: (0, 0))] + [ANY] * n_cin,
        out_specs=[pl.BlockSpec((ts, NP_F32), lambda i: (i, 0)), pl.BlockSpec((ts, NP_BF16), lambda i: (i, 0)),
                   pl.BlockSpec((ts, D_MODEL), lambda i: (i, 0))] + [ANY] * len(c_shapes),
        scratch_shapes=c_scratch,
        compiler_params=_params(("arbitrary",), 48),
    )(x, g_pre, scale, shift, w_new, *(comm[1] if comm else []))


GLA_GROUP = 8


def _gla_group_rows(t):
    return [pl.ds(pl.multiple_of((t * GLA_GROUP + j) * GLA_CHUNK, GLA_CHUNK), GLA_CHUNK) for j in range(GLA_GROUP)]


def _gla_chunks_common(q_ref, k_ref, lr_ref, wgu_ref, bgu_ref, rows_list):
    c = GLA_CHUNK
    ri = lax.broadcasted_iota(jnp.int32, (c, c), 0)
    ci = lax.broadcasted_iota(jnp.int32, (c, c), 1)
    tril = (ri >= ci).astype(F32)
    zs = [_dot(lr_ref[rows, :], wgu_ref[...]) + bgu_ref[...] for rows in rows_list]
    las = [_log_sigmoid(z) * (1.0 / GLA_TAU) for z in zs]
    bs = [jnp.dot(tril, la, precision=lax.Precision.HIGHEST, preferred_element_type=F32) for la in las]
    out = []
    for rows, z, b in zip(rows_list, zs, bs):
        q = q_ref[rows, :] * (GLA_DK ** -0.5)
        k = k_ref[rows, :]
        bl = b[c - 1:c, :]
        out.append(dict(z=z, b=b, bl=bl, qe=q * jnp.exp(b), ke=k * jnp.exp(-b), kend=k * jnp.exp(bl - b),
                        dec=jnp.exp(bl)))
    return out, ri, ci


def _head_lane_mask(hh):
    return (lax.broadcasted_iota(jnp.int32, (1, LANE), 1) // GLA_DK) == hh


def _state_block_mask():
    r = lax.broadcasted_iota(jnp.int32, (2 * GLA_DV, LANE), 0) // GLA_DV
    cc = lax.broadcasted_iota(jnp.int32, (2 * GLA_DV, LANE), 1) // GLA_DK
    return r == cc


def _gla_fwd(pf, pb, wgu, bgu, layer, comm=None):
    s_len = pf.shape[0]
    nc = s_len // GLA_CHUNK
    ncomm = len(comm[1]) if comm else 0

    def body(*refs):
        q_ref, k_ref, v_ref, lr_ref, wgu_ref, bgu_ref = refs[:6]
        cin, (o_ref, st_ref), cout = refs[6:6 + ncomm], refs[6 + ncomm:8 + ncomm], refs[8 + ncomm:8 + 2 * ncomm]
        qe_s, cs_s, dec_s = refs[8 + 2 * ncomm:11 + 2 * ncomm]
        comm_before, comm_after = _comm_hooks(comm, cin, cout, refs[11 + 2 * ncomm:], steps=2)
        comm_before()
        bd = _state_block_mask()

        def local(t, carry):
            rows_list = _gla_group_rows(t)
            cm, ri, ci = _gla_chunks_common(q_ref, k_ref, lr_ref, wgu_ref, bgu_ref, rows_list)
            vs = [v_ref[rows, :] for rows in rows_list]
            kebs = [c["ke"].astype(BF16) for c in cm]
            a = [[jnp.where(ri >= ci, _dot_nt(jnp.where(_head_lane_mask(hh), c["qe"], 0.0).astype(BF16), keb), 0.0)
                  .astype(BF16) for hh in range(2)] for c, keb in zip(cm, kebs)]
            oi = [[_dot(ah[hh], v[:, hh * GLA_DV:(hh + 1) * GLA_DV]) for hh in range(2)] for ah, v in zip(a, vs)]
            cs = [jnp.where(bd, _dot_tn(v, c["kend"].astype(BF16)), 0.0) for c, v in zip(cm, vs)]
            for j, (rows, c) in enumerate(zip(rows_list, cm)):
                n = t * GLA_GROUP + j
                o_ref[rows, :] = jnp.concatenate(oi[j], axis=1)
                qe_s[rows, :] = c["qe"].astype(BF16)
                cs_s[n] = cs[j]
                dec_s[n] = jnp.broadcast_to(c["dec"], (8, LANE))
            return carry

        lax.fori_loop(0, nc // GLA_GROUP, local, 0)

        def scan(n, st):
            st_ref[0, n] = st.astype(BF16)
            return dec_s[n][0:1, :] * st + cs_s[n]

        lax.fori_loop(0, nc, scan, jnp.zeros((2 * GLA_DV, LANE), F32))

        def inter(t, carry):
            rows_list = _gla_group_rows(t)
            add = [_dot_nt(qe_s[rows, :], st_ref[0, t * GLA_GROUP + j]) for j, rows in enumerate(rows_list)]
            for rows, av in zip(rows_list, add):
                o_ref[rows, :] = o_ref[rows, :] + av
            return carry

        lax.fori_loop(0, nc // GLA_GROUP, inter, 0)
        comm_after()

    return pl.pallas_call(
        body, name="gla_fwd_comm" if comm else "gla_fwd", grid=(2,),
        out_shape=[jax.ShapeDtypeStruct((s_len, GLA_HEADS * GLA_DV), F32),
                   jax.ShapeDtypeStruct((2, nc, 2 * GLA_DV, LANE), BF16)] + (_comm_out_shapes(*comm) if comm else []),
        in_specs=[pl.BlockSpec((s_len, LANE), lambda g: (0, COL_QA // LANE + g)),
                  pl.BlockSpec((s_len, LANE), lambda g: (0, COL_KA // LANE + g)),
                  pl.BlockSpec((s_len, 2 * GLA_DV), lambda g: (0, (COL_VA - NP_F32) // (2 * GLA_DV) + g)),
                  pl.BlockSpec((s_len, LANE), lambda g: (0, (COL_LR - NP_F32) // LANE)),
                  pl.BlockSpec((None, LANE, LANE), lambda g: (layer, 0, g)),
                  pl.BlockSpec((None, 1, LANE), lambda g: (layer, 0, g))] + [ANY] * ncomm,
        out_specs=[pl.BlockSpec((s_len, 2 * GLA_DV), lambda g: (0, g)),
                   pl.BlockSpec((1, nc, 2 * GLA_DV, LANE), lambda g: (g, 0, 0, 0))] + [ANY] * ncomm,
        scratch_shapes=[pltpu.VMEM((s_len, LANE), BF16), pltpu.VMEM((nc, 2 * GLA_DV, LANE), F32),
                        pltpu.VMEM((nc, 8, LANE), F32)] + (_comm_scratch(ncomm) if comm else []),
        compiler_params=_params(("arbitrary",), 56),
    )(pf, pf, pb, pb, wgu, bgu.reshape(bgu.shape[0], 1, GU_COLS), *(comm[1] if comm else []))


def _rope_tables(s_len):
    inv_freq = ROPE_THETA ** (-jnp.arange(0, DIL_HD, 2, dtype=F32) / DIL_HD)
    ang = jnp.arange(s_len, dtype=F32)[:, None] * inv_freq[None, :]
    cos, sin = jnp.cos(ang), jnp.sin(ang)
    return jnp.concatenate([cos, cos], axis=1), jnp.concatenate([-sin, sin], axis=1)


def _rope(xv, cos, sin_signed):
    return xv * cos + pltpu.roll(xv, DIL_HD // 2, 1) * sin_signed


DIL_GROUP = 8


def _dil_pair_block(i, half, d, nblk, group=DIL_GROUP):
    nb = nblk // d
    j = i + half * (nblk // group)
    if nb >= 2 * group:
        r, n = j % d, j // d
    else:
        r, n = j // nb, j % nb
    kb = jnp.maximum(n - 1, 0)
    qs = r + d * DIL_BLOCK * n
    ks = r + d * DIL_BLOCK * kb
    return qs, ks, jnp.minimum(n, 1)


def _dil_fill_bias(bias):
    qi = lax.broadcasted_iota(jnp.int32, (DIL_BLOCK, 2 * DIL_BLOCK), 0)
    kj = lax.broadcasted_iota(jnp.int32, (DIL_BLOCK, 2 * DIL_BLOCK), 1)
    for sel in range(2):
        dist = qi - kj + DIL_BLOCK * sel
        bias[sel] = jnp.where((dist >= 0) & (dist <= DIL_BLOCK), 0.0, MASK_VALUE)


def _strided(start, size, d):
    return pl.ds(start, size) if d == 1 else pl.ds(start, size, stride=d)


def _comm_hooks(comm, cin, cout, csem, steps=DIL_HEADS):
    def before():
        if comm:
            @pl.when(pl.program_id(0) == 0)
            def _():
                _comm_run(comm[0], ("start",), cin, cout, *csem)

            if comm[0] == "gather":
                @pl.when(pl.program_id(0) == steps - 1)
                def _():
                    _comm_run(comm[0], ("forward",), cin, cout, *csem)

            if comm[0] == "pairsum_exchange":
                @pl.when(pl.program_id(0) == (1 if steps <= 4 else 3))
                def _():
                    _comm_run(comm[0], ("reduce", "send"), cin, cout, *csem)

    def after():
        if comm:
            @pl.when(pl.program_id(0) == steps - 1)
            def _():
                _comm_run(comm[0], ("finish",), cin, cout, *csem)

    return before, after


def _dil_fwd(pf, pb, cos, sin_signed, comm=None):
    s_len = pf.shape[0]
    nblk = s_len // DIL_BLOCK
    prep_rows = 256
    scale = DIL_HD ** -0.5
    nc = len(comm[1]) if comm else 0

    def body(*refs):
        q_ref, k_ref, v_ref, cos_ref, sin_ref = refs[:5]
        cin, (o_ref, lse_ref), cout = refs[5:5 + nc], refs[5 + nc:7 + nc], refs[7 + nc:7 + 2 * nc]
        qf, kf, vf, o0, o1, o2, l0, l1, l2, bias = refs[7 + 2 * nc:17 + 2 * nc]
        comm_before, comm_after = _comm_hooks(comm, cin, cout, refs[17 + 2 * nc:])
        comm_before()
        _dil_fill_bias(bias)

        def prep(t, carry):
            rows = pl.ds(pl.multiple_of(t * prep_rows, prep_rows), prep_rows)
            cs, sn = cos_ref[rows, :], sin_ref[rows, :]
            qf[rows, :] = _rope(q_ref[rows, :], cs, sn)
            kf[rows, :] = _rope(k_ref[rows, :], cs, sn)
            vf[rows, :] = v_ref[rows, :].astype(F32)
            return carry

        lax.fori_loop(0, s_len // prep_rows, prep, 0)
        for d, o_p, l_p in zip(DIL_DILATIONS, (o0, o1, o2), (l0, l1, l2)):
            if nblk // d == 2:
                units = DIL_GROUP // 2

                def whole(i, carry, d=d, o_p=o_p, l_p=l_p, units=units):
                    rows = [_strided(i + u * (d // units), 2 * DIL_BLOCK, d) for u in range(units)]
                    ld = [(qf[rw, :].astype(BF16), kf[rw, :].astype(BF16), vf[rw, :].astype(BF16)) for rw in rows]
                    both = bias[...].reshape(2 * DIL_BLOCK, 2 * DIL_BLOCK)
                    s = [_dot_nt(qb, kk) * scale + both for qb, kk, _ in ld]
                    m = [jnp.max(sv, axis=-1, keepdims=True) for sv in s]
                    p = [jnp.exp(sv - mv) for sv, mv in zip(s, m)]
                    den = [jnp.sum(pv, axis=-1, keepdims=True) for pv in p]
                    r = [_dot(pv.astype(BF16), vv) for pv, (_, _, vv) in zip(p, ld)]
                    for rv, dv, mv, rw in zip(r, den, m, rows):
                        o_p[rw, :] = rv / dv
                        l_p[rw, :] = jnp.broadcast_to(mv + jnp.log(dv), (2 * DIL_BLOCK, DIL_HD))
                    return carry

                lax.fori_loop(0, d // units, whole, 0)
                continue

            def pair(i, carry, d=d, o_p=o_p, l_p=l_p):
                idx = [_dil_pair_block(i, half, d, nblk, DIL_GROUP) for half in range(DIL_GROUP)]
                ld = [(qf[_strided(qs, DIL_BLOCK, d), :].astype(BF16),
                       kf[_strided(ks, 2 * DIL_BLOCK, d), :].astype(BF16),
                       vf[_strided(ks, 2 * DIL_BLOCK, d), :].astype(BF16)) for qs, ks, _ in idx]
                s = [_dot_nt(qb, kk) * scale + bias[sel] for (qb, kk, _), (_, _, sel) in zip(ld, idx)]
                m = [jnp.max(sv, axis=-1, keepdims=True) for sv in s]
                p = [jnp.exp(sv - mv) for sv, mv in zip(s, m)]
                den = [jnp.sum(pv, axis=-1, keepdims=True) for pv in p]
                r = [_dot(pv.astype(BF16), vv) for pv, (_, _, vv) in zip(p, ld)]
                for rv, dv, mv, (qs, _, _) in zip(r, den, m, idx):
                    o_p[_strided(qs, DIL_BLOCK, d), :] = rv / dv
                    l_p[_strided(qs, DIL_BLOCK, d), :] = jnp.broadcast_to(mv + jnp.log(dv), (DIL_BLOCK, DIL_HD))
                return carry

            lax.fori_loop(0, nblk // DIL_GROUP, pair, 0)

        def comb(t, carry):
            rows = pl.ds(pl.multiple_of(t * prep_rows, prep_rows), prep_rows)
            a0, a1, a2 = l0[rows, :], l1[rows, :], l2[rows, :]
            m = jnp.maximum(jnp.maximum(a0, a1), a2)
            e0, e1, e2 = jnp.exp(a0 - m), jnp.exp(a1 - m), jnp.exp(a2 - m)
            tot = e0 + e1 + e2
            o_ref[rows, :] = (e0 * o0[rows, :] + e1 * o1[rows, :] + e2 * o2[rows, :]) / tot
            lse_ref[rows, :] = m + jnp.log(tot)
            return carry

        lax.fori_loop(0, s_len // prep_rows, comb, 0)
        comm_after()

    head = lambda base: pl.BlockSpec((s_len, DIL_HD), lambda h: (0, base // DIL_HD + h))
    table = pl.BlockSpec((s_len, DIL_HD), lambda h: (0, 0))
    out = pl.BlockSpec((s_len, DIL_HD), lambda h: (0, h))
    shp = jax.ShapeDtypeStruct((s_len, DIL_HEADS * DIL_HD), F32)
    return pl.pallas_call(
        body, name="dil_fwd_comm" if comm else "dil_fwd", grid=(DIL_HEADS,),
        out_shape=[shp, shp] + (_comm_out_shapes(*comm) if comm else []),
        in_specs=[head(COL_QB), head(COL_KB), head(COL_VB - NP_F32), table, table] + [ANY] * nc,
        out_specs=[out, out] + [ANY] * nc,
        scratch_shapes=[pltpu.VMEM((s_len, DIL_HD), F32) for _ in range(9)]
        + [pltpu.VMEM((2, DIL_BLOCK, 2 * DIL_BLOCK), F32)] + (_comm_scratch(nc) if comm else []),
        compiler_params=_params(("arbitrary",), 56),
    )(pf, pf, pb, cos, sin_signed, *(comm[1] if comm else []))


def _silu_and_grad(z):
    sg = _sigmoid(z)
    return z * sg, sg * (1.0 + z * (1.0 - sg))


def _post_fwd(o_a, o_b, pf, g_heads, w_out, x, gate, g_post, target=None, ts=256):
    s_len = x.shape[0]
    half = GLA_HEADS * GLA_DV
    last = target is not None

    def body(*refs):
        oa_ref, ob_ref, z_ref, gh_ref, w_ref, x_ref, gate_ref, gp_ref = refs[:8]
        xo_ref, y_ref, u_ref = refs[8 + last:11 + last]
        for src, base in ((oa_ref, 0), (ob_ref, half)):
            for hh in range(4):
                lo = hh * LANE
                og = src[:, lo:lo + LANE]
                on = og * lax.rsqrt(jnp.mean(og * og, axis=-1, keepdims=True) + EPS)
                zg = z_ref[:, base + lo:base + lo + LANE].astype(F32)
                y_ref[:, base + lo:base + lo + LANE] = (on * gh_ref[:, base + lo:base + lo + LANE]
                                                        * (zg * _sigmoid(zg))).astype(BF16)
        u = _dot(y_ref[...], w_ref[...])
        u_ref[...] = u.astype(BF16)
        rstd = lax.rsqrt(jnp.mean(u * u, axis=-1, keepdims=True) + EPS)
        x_out = x_ref[...] + gate_ref[...] * (u * rstd * gp_ref[...])
        if last:
            t_ref, loss_ref = refs[8], refs[12]

            @pl.when(pl.program_id(0) == 0)
            def _():
                loss_ref[...] = jnp.zeros_like(loss_ref)

            e = x_out - t_ref[...]
            xo_ref[...] = e * (1.0 / D_MODEL)
            loss_ref[...] += 0.5 * jnp.sum(jnp.mean(e * e, axis=-1, keepdims=True))
        else:
            xo_ref[...] = x_out

    (g_heads, gh_spec), (gate, gate_spec), (g_post, gp_spec) = _rowvec(g_heads), _rowvec(gate), _rowvec(g_post)
    tile = pl.BlockSpec((ts, D_MODEL), lambda i: (i, 0))
    halft = pl.BlockSpec((ts, half), lambda i: (i, 0))
    return pl.pallas_call(
        body, name="post_fwd_loss" if last else "post_fwd", grid=(s_len // ts,),
        out_shape=[jax.ShapeDtypeStruct((s_len, D_MODEL), F32), jax.ShapeDtypeStruct((s_len, D_MODEL), BF16),
                   jax.ShapeDtypeStruct((s_len, D_MODEL), BF16)]
        + ([jax.ShapeDtypeStruct((8, LANE), F32)] if last else []),
        in_specs=[halft, halft, tile, gh_spec, pl.BlockSpec((D_MODEL, D_MODEL), lambda i: (0, 0)), tile, gate_spec,
                  gp_spec] + ([tile] if last else []),
        out_specs=[tile, tile, tile] + ([pl.BlockSpec((8, LANE), lambda i: (0, 0))] if last else []),
        compiler_params=_params(("arbitrary",), 40),
    )(o_a, o_b, pf, g_heads, w_out, x, gate, g_post, *([target] if last else []))


def _post_bwd(dxo, u, gate, g_post, w_out, o_a, o_b, pf, g_heads, ts=256):
    s_len = dxo.shape[0]
    half = GLA_HEADS * GLA_DV

    def body(dx_ref, u_ref, gate_ref, gp_ref, w_ref, oa_ref, ob_ref, z_ref, gh_ref, du_ref, do_ref, dz_ref, sums_ref):
        @pl.when(pl.program_id(0) == 0)
        def _():
            sums_ref[...] = jnp.zeros_like(sums_ref)

        dx = dx_ref[...]
        u = u_ref[...].astype(F32)
        rstd = lax.rsqrt(jnp.mean(u * u, axis=-1, keepdims=True) + EPS)
        un = u * rstd
        sums_ref[0:1, :] += jnp.sum(dx * (un * gp_ref[...]), axis=0, keepdims=True)
        drn = dx * gate_ref[...]
        sums_ref[1:2, :] += jnp.sum(drn * un, axis=0, keepdims=True)
        dun = drn * gp_ref[...]
        du = rstd * (dun - un * jnp.mean(dun * un, axis=-1, keepdims=True))
        dub = du.astype(BF16)
        du_ref[...] = dub
        dy = _dot_nt(dub, w_ref[...])
        for src, base in ((oa_ref, 0), (ob_ref, half)):
            for hh in range(4):
                lo = base + hh * LANE
                og = src[:, hh * LANE:(hh + 1) * LANE]
                rs = lax.rsqrt(jnp.mean(og * og, axis=-1, keepdims=True) + EPS)
                on = og * rs
                zg = z_ref[:, lo:lo + LANE].astype(F32)
                sz, dsz = _silu_and_grad(zg)
                gg = gh_ref[:, lo:lo + LANE]
                dyg = dy[:, lo:lo + LANE]
                sums_ref[2:3, lo:lo + LANE] += jnp.sum(dyg * sz * on, axis=0, keepdims=True)
                dz_ref[:, lo:lo + LANE] = (dyg * on * gg * dsz).astype(BF16)
                don = dyg * gg * sz
                do_ref[:, lo:lo + LANE] = (rs * (don - on * jnp.mean(don * on, axis=-1, keepdims=True))).astype(BF16)

    (g_heads, gh_spec), (gate, gate_spec), (g_post, gp_spec) = _rowvec(g_heads), _rowvec(gate), _rowvec(g_post)
    tile = pl.BlockSpec((ts, D_MODEL), lambda i: (i, 0))
    halft = pl.BlockSpec((ts, half), lambda i: (i, 0))
    return pl.pallas_call(
        body, name="post_bwd", grid=(s_len // ts,),
        out_shape=(jax.ShapeDtypeStruct((s_len, D_MODEL), BF16), jax.ShapeDtypeStruct((s_len, D_MODEL), BF16),
                   jax.ShapeDtypeStruct((s_len, D_MODEL), BF16), jax.ShapeDtypeStruct((8, D_MODEL), F32)),
        in_specs=[tile, tile, gate_spec, gp_spec, pl.BlockSpec((D_MODEL, D_MODEL), lambda i: (0, 0)), halft, halft,
                  tile, gh_spec],
        out_specs=(tile, tile, tile, pl.BlockSpec((8, D_MODEL), lambda i: (0, 0))),
        compiler_params=_params(("arbitrary",), 40),
    )(dxo, u, gate, g_post, w_out, o_a, o_b, pf, g_heads)


def _gla_bwd(pf, pb, wgu, bgu, layer, states, do, comm=None):
    s_len = pf.shape[0]
    nc = s_len // GLA_CHUNK
    c = GLA_CHUNK
    n_cin, c_shapes, c_scratch = _comm_plumbing(comm)

    def body(*refs):
        ((q_ref, k_ref, v_ref, lr_ref, wgu_ref, bgu_ref, st_ref, do_ref),
         (dq_ref, dk_ref, dv_ref, dlr_ref, dwgu_ref, dbgu_ref), (ds_s, dec_s, dw_acc, db_acc),
         cin, cout, csem) = _split_refs(refs, 8, 6, 4, comm)
        comm_before, comm_after = _comm_hooks(comm, cin, cout, csem, steps=2)
        comm_before()
        dw_acc[...] = jnp.zeros_like(dw_acc)
        db_acc[...] = jnp.zeros_like(db_acc)
        bd = _state_block_mask()
        last_row = lax.broadcasted_iota(jnp.int32, (c, LANE), 0) == c - 1

        def local(t, carry):
            rows_list = _gla_group_rows(t)
            cm, _, _ = _gla_chunks_common(q_ref, k_ref, lr_ref, wgu_ref, bgu_ref, rows_list)
            loc = [jnp.where(bd, _dot_tn(do_ref[rows, :], cc["qe"].astype(BF16)), 0.0)
                   for rows, cc in zip(rows_list, cm)]
            for j, cc in enumerate(cm):
                ds_s[t * GLA_GROUP + j] = loc[j]
                dec_s[t * GLA_GROUP + j] = jnp.broadcast_to(cc["dec"], (8, LANE))
            return carry

        lax.fori_loop(0, nc // GLA_GROUP, local, 0)

        def scan(t, dst):
            n = nc - 1 - t
            loc = ds_s[n]
            ds_s[n] = dst
            return dec_s[n][0:1, :] * dst + loc

        lax.fori_loop(0, nc, scan, jnp.zeros((2 * GLA_DV, LANE), F32))

        def rest(t, carry):
            rows_list = _gla_group_rows(t)
            cm, ri, ci = _gla_chunks_common(q_ref, k_ref, lr_ref, wgu_ref, bgu_ref, rows_list)
            ns = [t * GLA_GROUP + j for j in range(GLA_GROUP)]
            vs = [v_ref[rows, :] for rows in rows_list]
            dobs = [do_ref[rows, :] for rows in rows_list]
            stbs = [st_ref[0, n] for n in ns]
            dsts = [ds_s[n] for n in ns]
            dstbs = [d.astype(BF16) for d in dsts]
            qebs = [cc["qe"].astype(BF16) for cc in cm]
            kebs = [cc["ke"].astype(BF16) for cc in cm]
            kendbs = [cc["kend"].astype(BF16) for cc in cm]
            hms = [_head_lane_mask(hh) for hh in range(2)]
            qehs = [[jnp.where(hm, cc["qe"], 0.0).astype(BF16) for hm in hms] for cc in cm]
            kehs = [[jnp.where(hm, cc["ke"], 0.0).astype(BF16) for hm in hms] for cc in cm]
            heads = lambda x: [x[:, hh * GLA_DV:(hh + 1) * GLA_DV] for hh in range(2)]
            vhs, dohs = [heads(v) for v in vs], [heads(d) for d in dobs]

            dqe0 = [_dot(dob, stb) for dob, stb in zip(dobs, stbs)]
            dkend = [_dot(v, dstb) for v, dstb in zip(vs, dstbs)]
            dv0 = [_dot_nt(kb, dstb) for kb, dstb in zip(kendbs, dstbs)]
            a_t = [[jnp.where(ci >= ri, _dot_nt(kehs[j][hh], qebs[j]), 0.0).astype(BF16) for hh in range(2)]
                   for j in range(GLA_GROUP)]
            da = [[jnp.where(ri >= ci, _dot_nt(dohs[j][hh], vhs[j][hh]), 0.0).astype(BF16) for hh in range(2)]
                  for j in range(GLA_GROUP)]
            da_t = [[jnp.where(ci >= ri, _dot_nt(vhs[j][hh], dohs[j][hh]), 0.0).astype(BF16) for hh in range(2)]
                    for j in range(GLA_GROUP)]
            dv1 = [[_dot(a_t[j][hh], dohs[j][hh]) for hh in range(2)] for j in range(GLA_GROUP)]
            dqe1 = [[_dot(da[j][hh], kebs[j]) for hh in range(2)] for j in range(GLA_GROUP)]
            dke1 = [[_dot(da_t[j][hh], qehs[j][hh]) for hh in range(2)] for j in range(GLA_GROUP)]

            dbs, dzs = [], []
            for j, (rows, cc) in enumerate(zip(rows_list, cm)):
                qe, ke, kend, b, bl = cc["qe"], cc["ke"], cc["kend"], cc["b"], cc["bl"]
                dqe = dqe0[j] + jnp.where(hms[0], dqe1[j][0], 0.0) + jnp.where(hms[1], dqe1[j][1], 0.0)
                dke = jnp.where(hms[0], dke1[j][0], 0.0) + jnp.where(hms[1], dke1[j][1], 0.0)
                dv_ref[rows, :] = (dv0[j] + jnp.concatenate(dv1[j], axis=1)).astype(BF16)
                dq_ref[rows, :] = (dqe * jnp.exp(b) * (GLA_DK ** -0.5)).astype(BF16)
                dk_ref[rows, :] = (dke * jnp.exp(-b) + dkend[j] * jnp.exp(bl - b)).astype(BF16)
                ddec = jnp.sum(dsts[j] * stbs[j].astype(F32), axis=0, keepdims=True)
                dbl = jnp.sum(dkend[j] * kend, axis=0, keepdims=True) + ddec * cc["dec"]
                dbs.append(dqe * qe - dke * ke - dkend[j] * kend + jnp.where(last_row, dbl, 0.0))
            triu = (ci >= ri).astype(F32)
            dlas = [jnp.dot(triu, db, precision=lax.Precision.HIGHEST, preferred_element_type=F32) for db in dbs]
            dzs = [dla * (1.0 / GLA_TAU) * _sigmoid(-cc["z"]) for dla, cc in zip(dlas, cm)]
            dzbs = [dz.astype(BF16) for dz in dzs]
            dlrs = [_dot_nt(dzb, wgu_ref[...]) for dzb in dzbs]
            dws = [_dot_tn(lr_ref[rows, :], dzb) for rows, dzb in zip(rows_list, dzbs)]
            for rows, dlr in zip(rows_list, dlrs):
                dlr_ref[0, rows, :] = dlr
            dw_acc[...] += functools.reduce(lambda x, y: x + y, dws)
            db_acc[0:1, :] += jnp.sum(functools.reduce(lambda x, y: x + y, dzs), axis=0, keepdims=True)
            return carry

        lax.fori_loop(0, nc // GLA_GROUP, rest, 0)
        dwgu_ref[...] = dw_acc[...]
        dbgu_ref[...] = db_acc[...]
        comm_after()

    pair = pl.BlockSpec((s_len, LANE), lambda g: (0, g))
    return pl.pallas_call(
        body, name="gla_bwd_comm" if comm else "gla_bwd", grid=(2,),
        out_shape=[jax.ShapeDtypeStruct((s_len, GU_COLS), BF16), jax.ShapeDtypeStruct((s_len, GU_COLS), BF16),
                   jax.ShapeDtypeStruct((s_len, GLA_HEADS * GLA_DV), BF16),
                   jax.ShapeDtypeStruct((2, s_len, LANE), F32),
                   jax.ShapeDtypeStruct((LANE, GU_COLS), F32), jax.ShapeDtypeStruct((8, GU_COLS), F32)] + c_shapes,
        in_specs=[pl.BlockSpec((s_len, LANE), lambda g: (0, COL_QA // LANE + g)),
                  pl.BlockSpec((s_len, LANE), lambda g: (0, COL_KA // LANE + g)),
                  pl.BlockSpec((s_len, 2 * GLA_DV), lambda g: (0, (COL_VA - NP_F32) // (2 * GLA_DV) + g)),
                  pl.BlockSpec((s_len, LANE), lambda g: (0, (COL_LR - NP_F32) // LANE)),
                  pl.BlockSpec((None, LANE, LANE), lambda g: (layer, 0, g)),
                  pl.BlockSpec((None, 1, LANE), lambda g: (layer, 0, g)),
                  pl.BlockSpec((1, nc, 2 * GLA_DV, LANE), lambda g: (g, 0, 0, 0)),
                  pl.BlockSpec((s_len, 2 * GLA_DV), lambda g: (0, g))] + [ANY] * n_cin,
        out_specs=[pair, pair, pl.BlockSpec((s_len, 2 * GLA_DV), lambda g: (0, g)),
                   pl.BlockSpec((1, s_len, LANE), lambda g: (g, 0, 0)),
                   pl.BlockSpec((LANE, LANE), lambda g: (0, g)), pl.BlockSpec((8, LANE), lambda g: (0, g))]
        + [ANY] * len(c_shapes),
        scratch_shapes=[pltpu.VMEM((nc, 2 * GLA_DV, LANE), F32), pltpu.VMEM((nc, 8, LANE), F32),
                        pltpu.VMEM((LANE, LANE), F32), pltpu.VMEM((8, LANE), F32)] + c_scratch,
        compiler_params=_params(("arbitrary",), 56),
    )(pf, pf, pb, pb, wgu, bgu.reshape(bgu.shape[0], 1, GU_COLS), states, do, *(comm[1] if comm else []))


def _dil_bwd(pf, pb, cos, sin_signed, do, o_b, lse, comm=None):
    s_len = pf.shape[0]
    nblk = s_len // DIL_BLOCK
    prep_rows = 256
    scale = DIL_HD ** -0.5
    nc = len(comm[1]) if comm else 0

    def body(*refs):
        ((q_ref, k_ref, v_ref, cos_ref, sin_ref, do_ref, o_ref, lse_ref), (dq_ref, dk_ref, dv_ref),
         (qf, kf, vf, dof, dl, dqa, dka, dva, bias), cin, cout, csem) = _split_refs(refs, 8, 3, 9, comm)
        comm_before, comm_after = _comm_hooks(comm, cin, cout, csem)
        comm_before()
        _dil_fill_bias(bias)

        def prep(t, carry):
            rows = pl.ds(pl.multiple_of(t * prep_rows, prep_rows), prep_rows)
            cs, sn = cos_ref[rows, :], sin_ref[rows, :]
            qf[rows, :] = _rope(q_ref[rows, :], cs, sn) * scale
            kf[rows, :] = _rope(k_ref[rows, :], cs, sn)
            vf[rows, :] = v_ref[rows, :].astype(F32)
            dov = do_ref[rows, :].astype(F32)
            dof[rows, :] = dov
            dl[rows, :] = jnp.broadcast_to(jnp.sum(dov * o_ref[rows, :], axis=-1, keepdims=True), (prep_rows, DIL_HD))
            zero = jnp.zeros((prep_rows, DIL_HD), F32)
            dqa[rows, :] = zero
            dka[rows, :] = zero
            dva[rows, :] = zero
            return carry

        lax.fori_loop(0, s_len // prep_rows, prep, 0)

        for d in DIL_DILATIONS:
            if nblk // d == 2:
                units = DIL_GROUP // 2

                def whole(i, carry, d=d, units=units):
                    rows = [_strided(i + u * (d // units), 2 * DIL_BLOCK, d) for u in range(units)]
                    ld = [(qf[rw, :].astype(BF16), kf[rw, :].astype(BF16), vf[rw, :].astype(BF16),
                           dof[rw, :].astype(BF16)) for rw in rows]
                    both = bias[...].reshape(2 * DIL_BLOCK, 2 * DIL_BLOCK)
                    s = [_dot_nt(qb, kk) + both for qb, kk, _, _ in ld]
                    dp = [_dot_nt(dob, vv) for _, _, vv, dob in ld]
                    p = [jnp.exp(sv - lse_ref[rw, :][:, 0:1]) for sv, rw in zip(s, rows)]
                    ds = [(pv * (dpv - dl[rw, :][:, 0:1])).astype(BF16) for pv, dpv, rw in zip(p, dp, rows)]
                    pb = [pv.astype(BF16) for pv in p]
                    gq = [_dot(dsv, kk) for dsv, (_, kk, _, _) in zip(ds, ld)]
                    gk = [_dot_tn(dsv, qb) for dsv, (qb, _, _, _) in zip(ds, ld)]
                    gv = [_dot_tn(pv, dob) for pv, (_, _, _, dob) in zip(pb, ld)]
                    for rw, a, b, c in zip(rows, gq, gk, gv):
                        dqa[rw, :] += a
                        dka[rw, :] += b
                        dva[rw, :] += c
                    return carry

                lax.fori_loop(0, d // units, whole, 0)
                continue

            def pair(i, carry, d=d):
                idx = [_dil_pair_block(i, half, d, nblk) for half in range(DIL_GROUP)]
                rows = [(_strided(qs, DIL_BLOCK, d), _strided(ks, 2 * DIL_BLOCK, d)) for qs, ks, _ in idx]
                ld = [(qf[qr, :].astype(BF16), kf[kr, :].astype(BF16), vf[kr, :].astype(BF16),
                       dof[qr, :].astype(BF16)) for qr, kr in rows]
                s = [_dot_nt(qb, kk) + bias[sel] for (qb, kk, _, _), (_, _, sel) in zip(ld, idx)]
                dp = [_dot_nt(dob, vv) for _, _, vv, dob in ld]
                p = [jnp.exp(sv - lse_ref[qr, :][:, 0:1]) for sv, (qr, _) in zip(s, rows)]
                ds = [(pv * (dpv - dl[qr, :][:, 0:1])).astype(BF16) for pv, dpv, (qr, _) in zip(p, dp, rows)]
                pb = [pv.astype(BF16) for pv in p]
                gq = [_dot(dsv, kk) for dsv, (_, kk, _, _) in zip(ds, ld)]
                gk = [_dot_tn(dsv, qb) for dsv, (qb, _, _, _) in zip(ds, ld)]
                gv = [_dot_tn(pv, dob) for pv, (_, _, _, dob) in zip(pb, ld)]
                for (qr, kr), a, b, c in zip(rows, gq, gk, gv):
                    dqa[qr, :] += a
                    dka[kr, :] += b
                    dva[kr, :] += c
                return carry

            lax.fori_loop(0, nblk // DIL_GROUP, pair, 0)

        def fin(t, carry):
            rows = pl.ds(pl.multiple_of(t * prep_rows, prep_rows), prep_rows)
            cs, sn = cos_ref[rows, :], sin_ref[rows, :]
            gq, gk = dqa[rows, :] * scale, dka[rows, :]
            dq_ref[rows, :] = (gq * cs - pltpu.roll(gq, DIL_HD // 2, 1) * sn).astype(BF16)
            dk_ref[rows, :] = (gk * cs - pltpu.roll(gk, DIL_HD // 2, 1) * sn).astype(BF16)
            dv_ref[rows, :] = dva[rows, :].astype(BF16)
            return carry

        lax.fori_loop(0, s_len // prep_rows, fin, 0)
        comm_after()

    head = lambda base: pl.BlockSpec((s_len, DIL_HD), lambda h: (0, base // DIL_HD + h))
    table = pl.BlockSpec((s_len, DIL_HD), lambda h: (0, 0))
    out = pl.BlockSpec((s_len, DIL_HD), lambda h: (0, h))
    shp = jax.ShapeDtypeStruct((s_len, DIL_HEADS * DIL_HD), BF16)
    return pl.pallas_call(
        body, name="dil_bwd_comm" if comm else "dil_bwd", grid=(DIL_HEADS,),
        out_shape=[shp, shp, shp] + (_comm_out_shapes(*comm) if comm else []),
        in_specs=[head(COL_QB), head(COL_KB), head(COL_VB - NP_F32), table, table,
                  pl.BlockSpec((s_len, DIL_HD), lambda h: (0, DIL_HEADS + h)), out, out] + [ANY] * nc,
        out_specs=[out, out, out] + [ANY] * len(_comm_plumbing(comm)[1]),
        scratch_shapes=[pltpu.VMEM((s_len, DIL_HD), F32) for _ in range(8)]
        + [pltpu.VMEM((2, DIL_BLOCK, 2 * DIL_BLOCK), F32)] + (_comm_scratch(nc) if comm else []),
        compiler_params=_params(("arbitrary",), 56),
    )(pf, pf, pb, cos, sin_signed, do, o_b, lse, *(comm[1] if comm else []))


_PIECES = ((COL_Z, 1024), (COL_QA, 256), (COL_KA, 256), (COL_QB, 512), (COL_KB, 512), (COL_VA, 512), (COL_VB, 512),
           (COL_LR, 128))


def _in_bwd(pieces, w_new, x, dxo, g_pre, scale, comm=None, ts=512):
    s_len = x.shape[0]
    nc = len(comm[1]) if comm else 0
    nco = len(_comm_out_shapes(*comm)) if comm else 0
    npc = len(_PIECES)

    def body(*refs):
        p_refs = refs[:npc]
        w_ref, x_ref, dxo_ref, g_ref, sc_ref = refs[npc:npc + 5]
        cin, (dx_ref, sums_ref), cout = (refs[npc + 5:npc + 5 + nc], refs[npc + 5 + nc:npc + 7 + nc],
                                         refs[npc + 7 + nc:npc + 7 + nc + nco])
        comm_before, comm_after = _comm_hooks(comm, cin, cout, refs[npc + 7 + nc + nco:], steps=s_len // ts)
        comm_before()

        @pl.when(pl.program_id(0) == 0)
        def _():
            sums_ref[...] = jnp.zeros_like(sums_ref)

        dh = jnp.zeros((ts, D_MODEL), F32)
        for p_ref, (col, width) in zip(p_refs, _PIECES):
            dh += _dot_nt(p_ref[...], w_ref[:, col:col + width])
        xv = x_ref[...]
        rstd = lax.rsqrt(jnp.mean(xv * xv, axis=-1, keepdims=True) + EPS)
        xn = xv * rstd
        sums_ref[0:1, :] += jnp.sum(dh, axis=0, keepdims=True)
        sums_ref[1:2, :] += jnp.sum(dh * (xn * g_ref[...]), axis=0, keepdims=True)
        dr = dh * (1.0 + sc_ref[...])
        sums_ref[2:3, :] += jnp.sum(dr * xn, axis=0, keepdims=True)
        dxn = dr * g_ref[...]
        dx_ref[...] = dxo_ref[...] + rstd * (dxn - xn * jnp.mean(dxn * xn, axis=-1, keepdims=True))
        comm_after()

    (g_pre, g_spec), (scale, sc_spec) = _rowvec(g_pre), _rowvec(scale)
    tile = pl.BlockSpec((ts, D_MODEL), lambda i: (i, 0))
    return pl.pallas_call(
        body, name="in_bwd_comm" if comm else "in_bwd", grid=(s_len // ts,),
        out_shape=[jax.ShapeDtypeStruct((s_len, D_MODEL), F32), jax.ShapeDtypeStruct((8, D_MODEL), F32)]
        + (_comm_out_shapes(*comm) if comm else []),
        in_specs=[pl.BlockSpec((ts, width), lambda i: (i, 0)) for _, width in _PIECES]
        + [pl.BlockSpec((D_MODEL, NP), lambda i: (0, 0)), tile, tile, g_spec, sc_spec] + [ANY] * nc,
        out_specs=[tile, pl.BlockSpec((8, D_MODEL), lambda i: (0, 0))] + [ANY] * nco,
        scratch_shapes=_comm_scratch(nc) if comm else [],
        compiler_params=_params(("arbitrary",), 56),
    )(*pieces, w_new, x, dxo, g_pre, scale, *(comm[1] if comm else []))


def _w_in_to_kernel(gathered, comm=None, tr=128):
    n_cin, c_shapes, c_scratch = _comm_plumbing(comm)

    def body(*refs):
        (g_ref,), (o_ref,), _, cin, cout, csem = _split_refs(refs, 1, 1, 0, comm)
        comm_before, comm_after = _comm_hooks(comm, cin, cout, csem, steps=D_MODEL // tr)
        comm_before()
        cols = jnp.concatenate([g_ref[k].astype(F32) for k in range(N_DEV)], axis=1)
        pad = jnp.zeros((tr, LANE - GLA_LOWRANK), F32)
        o_ref[...] = jnp.concatenate(
            [cols[:, 1024:1536], cols[:, 3088:3600], cols[:, 0:512], cols[:, 1552:2576], cols[:, 512:1024],
             cols[:, 2576:3088], cols[:, 1536:1552], pad], axis=1).astype(BF16)
        comm_after()

    return pl.pallas_call(
        body, name="w_in_to_kernel_comm" if comm else "w_in_to_kernel", grid=(D_MODEL // tr,),
        out_shape=[jax.ShapeDtypeStruct((D_MODEL, NP), BF16)] + c_shapes,
        in_specs=[pl.BlockSpec((N_DEV, tr, W_IN_SHARD), lambda i: (0, i, 0))] + [ANY] * n_cin,
        out_specs=[pl.BlockSpec((tr, NP), lambda i: (i, 0))] + [ANY] * len(c_shapes),
        scratch_shapes=c_scratch,
        compiler_params=_params(("arbitrary",)),
    )(gathered, *(comm[1] if comm else []))


def _grad_w_in(h, pieces, ts=512, tr=128):
    s_len = h.shape[0]
    steps = s_len // ts

    def body(*refs):
        h_ref, p_refs = refs[0], refs[1:1 + len(_PIECES)]
        o_ref, acc = refs[1 + len(_PIECES):]

        @pl.when(pl.program_id(0) == 0)
        def _():
            acc[...] = jnp.zeros_like(acc)

        hv = h_ref[...]
        for p_ref, (col, width) in zip(p_refs, _PIECES):
            acc[:, col:col + width] += _dot_tn(hv, p_ref[...])

        @pl.when(pl.program_id(0) == steps - 1)
        def _():
            def rows_out(t, carry):
                rows = pl.ds(pl.multiple_of(t * tr, tr), tr)
                g = acc[rows, :]
                cols = jnp.concatenate(
                    [g[:, COL_QA:COL_QB], g[:, COL_VA:COL_VB], g[:, 0:512], g[:, COL_LR:COL_LR + GLA_LOWRANK],
                     g[:, COL_QB:COL_VA], g[:, COL_VB:COL_LR], g[:, 512:1024]], axis=1)
                for k in range(N_DEV):
                    o_ref[k, rows, :] = cols[:, W_IN_SHARD * k:W_IN_SHARD * (k + 1)].astype(BF16)
                return carry

            lax.fori_loop(0, D_MODEL // tr, rows_out, 0)

    return pl.pallas_call(
        body, name="grad_w_in", grid=(steps,),
        out_shape=jax.ShapeDtypeStruct((N_DEV, D_MODEL, W_IN_SHARD), BF16),
        in_specs=[pl.BlockSpec((ts, D_MODEL), lambda i: (i, 0))]
        + [pl.BlockSpec((ts, width), lambda i: (i, 0)) for _, width in _PIECES],
        out_specs=pl.BlockSpec((N_DEV, D_MODEL, W_IN_SHARD), lambda i: (0, 0, 0)),
        scratch_shapes=[pltpu.VMEM((D_MODEL, NP), F32)],
        compiler_params=_params(("arbitrary",), 56),
    )(h, *pieces)


def _matmul_tn(a, b, name, bn, ts=512):
    s_len, m = a.shape
    n = b.shape[1]
    steps = s_len // ts

    def body(a_ref, b_ref, o_ref, acc):
        @pl.when(pl.program_id(1) == 0)
        def _():
            acc[...] = jnp.zeros_like(acc)

        acc[...] += _dot_tn(a_ref[...], b_ref[...])

        @pl.when(pl.program_id(1) == steps - 1)
        def _():
            o_ref[...] = acc[...].astype(BF16)

    return pl.pallas_call(
        body, name=name, grid=(n // bn, steps),
        out_shape=jax.ShapeDtypeStruct((m, n), BF16),
        in_specs=[pl.BlockSpec((ts, m), lambda j, i: (i, 0)), pl.BlockSpec((ts, bn), lambda j, i: (i, j))],
        out_specs=pl.BlockSpec((m, bn), lambda j, i: (0, j)),
        scratch_shapes=[pltpu.VMEM((m, bn), F32)],
        compiler_params=_params(("arbitrary", "arbitrary"), 40),
    )(a, b)


def _adam_math(w, g, m, v):
    m = ADAM_B1 * m + (1.0 - ADAM_B1) * g
    v = ADAM_B2 * v + (1.0 - ADAM_B2) * (g * g)
    m_hat = m / (1.0 - ADAM_B1 ** ADAM_STEP)
    v_hat = v / (1.0 - ADAM_B2 ** ADAM_STEP)
    delta = -ADAM_LR * (m_hat / (jnp.sqrt(v_hat) + ADAM_EPS) + ADAM_WD * w)
    return delta, m, v


def _adamw(w, parts, m, v, name, tr):
    r, cdim = w.shape
    n_parts = parts.shape[0]

    def body(w_ref, p_ref, m_ref, v_ref, g_ref, d_ref, nm_ref, nv_ref):
        g = p_ref[0].astype(F32)
        for k in range(1, n_parts):
            g = g + p_ref[k].astype(F32)
        g_ref[...] = g
        d_ref[...], nm_ref[...], nv_ref[...] = _adam_math(w_ref[...], g, m_ref[...], v_ref[...])

    tile = pl.BlockSpec((tr, cdim), lambda i: (i, 0))
    shp = jax.ShapeDtypeStruct((r, cdim), F32)
    return pl.pallas_call(
        body, name=name, grid=(r // tr,), out_shape=(shp, shp, shp, shp),
        in_specs=[tile, pl.BlockSpec((n_parts, tr, cdim), lambda i: (0, i, 0)), tile, tile],
        out_specs=(tile, tile, tile, tile),
        compiler_params=_params(("arbitrary",), 40),
    )(w, parts, m, v)


def _adamw_layers(w, parts, m, v, name, tr):
    n_layers, r, cdim = w.shape

    def body(*refs):
        w_ref, p_refs, (m_ref, v_ref) = refs[0], refs[1:1 + n_layers], refs[1 + n_layers:3 + n_layers]
        g_ref, d_ref, nm_ref, nv_ref = refs[3 + n_layers:]
        for l, p_ref in enumerate(p_refs):
            @pl.when(pl.program_id(0) == l)
            def _(p_ref=p_ref):
                g = p_ref[0].astype(F32)
                for k in range(1, p_ref.shape[0]):
                    g = g + p_ref[k].astype(F32)
                g_ref[0] = g
                d_ref[0], nm_ref[0], nv_ref[0] = _adam_math(w_ref[0], g, m_ref[0], v_ref[0])

    tile = pl.BlockSpec((1, tr, cdim), lambda l, i: (l, i, 0))
    part = lambda own: pl.BlockSpec((parts[own].shape[0], tr, cdim), lambda l, i: (0, jnp.where(l == own, i, 0), 0))
    shp = jax.ShapeDtypeStruct(w.shape, F32)
    return pl.pallas_call(
        body, name=name, grid=(n_layers, r // tr), out_shape=(shp, shp, shp, shp),
        in_specs=[tile] + [part(l) for l in range(n_layers)] + [tile, tile],
        out_specs=(tile, tile, tile, tile),
        compiler_params=_params(("arbitrary", "arbitrary"), 40),
    )(w, *parts, m, v)


def _row(vec, width):
    vec = vec.reshape(1, -1)
    return jnp.pad(vec, ((0, 0), (0, width - vec.shape[1])))


def kernel(x, c, w_ada, b_ada, g_pre, w_in, w_gate_up, b_gate_up, g_gla, g_dil, w_out, g_post, loss_target, m_w_ada, m_b_ada, m_g_pre, m_w_in, m_w_gate_up, m_b_gate_up, m_g_gla, m_g_dil, m_w_out, m_g_post, v_w_ada, v_b_ada, v_g_pre, v_w_in, v_w_gate_up, v_b_gate_up, v_g_gla, v_g_dil, v_w_out, v_g_post):
    px, py, pc = _my_position()
    me = _linear(px, py, pc)
    xs = x[0]
    target = loss_target[0]
    s_len = xs.shape[0]
    assert s_len % (DIL_BLOCK * max(DIL_DILATIONS) * 2) == 0 and xs.shape[1] == D_MODEL

    w_in_b, w_out_b = w_in.astype(BF16), w_out.astype(BF16)
    c_rows, wgu_all, w_in_all = _comm_call(
        "gather", [jnp.pad(c, ((0, 7), (0, 0))), w_gate_up.reshape(DEPTH * GLA_LOWRANK, GU_SHARD), w_in_b[0]],
        "gather_first")
    c_all = c_rows.reshape(N_DEV, 8, D_MODEL)[:, 0]
    mod_part = _mod_fwd(c_all, w_ada)
    w_new, mod_all = _w_in_to_kernel(w_in_all.reshape(N_DEV, D_MODEL, W_IN_SHARD),
                                     comm=("gather", [mod_part.reshape(DEPTH * N_DEV, ADA_SHARD)]))
    mod_all = mod_all.reshape(N_DEV, DEPTH, N_DEV, ADA_SHARD)
    mod_mine = lax.dynamic_index_in_dim(mod_all, me, axis=2, keepdims=False)
    mod = jnp.transpose(mod_mine, (1, 0, 2)).reshape(DEPTH, 3 * D_MODEL) + b_ada
    wgu_full = jnp.transpose(wgu_all.reshape(N_DEV, DEPTH, GLA_LOWRANK, GU_SHARD), (1, 2, 0, 3)).reshape(
        DEPTH, GLA_LOWRANK, GU_COLS)
    wgu_pad = jnp.pad(wgu_full, ((0, 0), (0, LANE - GLA_LOWRANK), (0, 0))).astype(BF16)

    cos, sin_signed = _rope_tables(s_len)
    g_heads = jnp.concatenate([g_gla, g_dil], axis=1)

    saved = []
    xl = xs
    for l in range(DEPTH):
        shift, scale, gate = ((mod, l, k) for k in range(3))
        if l > 0:
            w_new = _w_in_to_kernel(w_in_all.reshape(N_DEV, D_MODEL, W_IN_SHARD))[0]
        pf, pb, h, w_out_l = _prenorm_proj(xl, (g_pre, l, 0), scale, shift, w_new, comm=("gather", [w_out_b[l]]))
        o_a, states = _gla_fwd(pf, pb, wgu_pad, b_gate_up, l)
        if l + 1 < DEPTH:
            o_b, lse, w_in_all = _dil_fwd(pf, pb, cos, sin_signed, comm=("gather", [w_in_b[l + 1]]))
        else:
            o_b, lse = _dil_fwd(pf, pb, cos, sin_signed)
        if l + 1 < DEPTH:
            x_next, y, u = _post_fwd(o_a, o_b, pf, (g_heads, l, 0), w_out_l, xl, gate, (g_post, l, 0))
        else:
            dx, y, u, loss_part = _post_fwd(o_a, o_b, pf, (g_heads, l, 0), w_out_l, xl, gate, (g_post, l, 0),
                                            target=target)
        saved.append((xl, scale, gate, w_new, w_out_l, pf, pb, h, o_a, states, o_b, lse, y, u))
        xl = x_next

    small_rows = []
    gin_slots, gin_parts, gout_parts = None, [None] * DEPTH, [None] * DEPTH
    for l in reversed(range(DEPTH)):
        x_in, scale, gate, w_new, w_out_l, pf, pb, h, o_a, states, o_b, lse, y, u = saved[l]
        du, do, dz, sums_post = _post_bwd(dx, u, gate, (g_post, l, 0), w_out_l, o_a, o_b, pf, (g_heads, l, 0))
        gout_slots = _matmul_tn(y, du, "grad_w_out", 512)
        dq_a, dk_a, dv_a, dlr2, dwgu, dbgu, arrived = _gla_bwd(pf, pb, wgu_pad, b_gate_up, l, states, do,
                                                               comm=("exchange", [gout_slots]))
        gout_parts[l] = arrived.reshape(N_DEV, OUT_SHARD, D_MODEL)
        if gin_slots is not None:
            dq_b, dk_b, dv_b, arrived, _, _ = _dil_bwd(pf, pb, cos, sin_signed, do, o_b, lse,
                                                       comm=("pairsum_exchange", [gin_slots]))
            gin_parts[l + 1] = arrived.reshape(N_DEV // 2, D_MODEL, W_IN_SHARD)
        else:
            dq_b, dk_b, dv_b = _dil_bwd(pf, pb, cos, sin_signed, do, o_b, lse)
        dlr = (dlr2[0] + dlr2[1]).astype(BF16)
        pieces = (dz, dq_a, dk_a, dq_b, dk_b, dv_a, dv_b, dlr)
        gin_slots = _grad_w_in(h, pieces).reshape(N_DEV * D_MODEL, W_IN_SHARD)
        if l == 0:
            dx, sums_in, arrived, _, _ = _in_bwd(pieces, w_new, x_in, dx, (g_pre, l, 0), scale,
                                                 comm=("pairsum_exchange", [gin_slots]))
            gin_parts[0] = arrived.reshape(N_DEV // 2, D_MODEL, W_IN_SHARD)
        else:
            dx, sums_in = _in_bwd(pieces, w_new, x_in, dx, (g_pre, l, 0), scale)
        dmod = jnp.concatenate([sums_in[0], sums_in[1], sums_post[0]])
        vecs = jnp.concatenate([sums_in[2], sums_post[1], sums_post[2], dbgu[0]])
        small_rows[0:0] = [_row(dmod, 4096), _row(vecs, 4096), _row(dwgu[:GLA_LOWRANK], 4096)]
    grad_x = dx[None]

    flat = lambda a, rows: a.reshape(rows, a.shape[-1])
    r_ada = DEPTH * D_MODEL
    g_w_in, d_w_in, nm_w_in, nv_w_in = _adamw_layers(w_in, gin_parts, m_w_in, v_w_in, "adamw_w_in", 256)
    g_w_out, d_w_out, nm_w_out, nv_w_out = _adamw_layers(w_out, gout_parts, m_w_out, v_w_out, "adamw_w_out", 128)

    small_rows += [_row(loss_part[0, 0:1], 4096), jnp.zeros((1, 4096), F32)]
    small = _all_gather(jnp.concatenate(small_rows, axis=0), "gather_small").reshape(N_DEV, 8, 4096)
    dmod_all = jnp.stack([small[:, 0, :3 * D_MODEL], small[:, 3, :3 * D_MODEL]])
    dmod_cols = lax.dynamic_slice_in_dim(dmod_all, me * ADA_SHARD, ADA_SHARD, axis=2)
    gwa = _w_ada_grad(c_all, dmod_cols).reshape(1, r_ada, ADA_SHARD)
    g_w_ada, d_w_ada, nm_w_ada, nv_w_ada = (
        t.reshape(w_ada.shape) for t in _adamw(flat(w_ada, r_ada), gwa, flat(m_w_ada, r_ada), flat(v_w_ada, r_ada),
                                               "adamw_w_ada", 256))

    where = ((0, 0), (1, 0), (1, 1024), (1, 2048), (1, 2560), (1, 3072))
    replicated = [(b_ada, m_b_ada, v_b_ada), (g_pre, m_g_pre, v_g_pre), (g_post, m_g_post, v_g_post),
                  (g_gla, m_g_gla, v_g_gla), (g_dil, m_g_dil, v_g_dil), (b_gate_up, m_b_gate_up, v_b_gate_up)]
    updated, loss = _adamw_replicated(small, replicated, where, loss_at=(6, 0))
    ((g_b_ada, d_b_ada, nm_b_ada, nv_b_ada), (g_g_pre, d_g_pre, nm_g_pre, nv_g_pre),
     (g_g_post, d_g_post, nm_g_post, nv_g_post), (g_g_gla, d_g_gla, nm_g_gla, nv_g_gla),
     (g_g_dil, d_g_dil, nm_g_dil, nv_g_dil), (g_b_gu, d_b_gu, nm_b_gu, nv_b_gu)) = updated
    gu_parts = jnp.stack([small[:, 2], small[:, 5]], axis=1).reshape(N_DEV, DEPTH, GLA_LOWRANK, GU_COLS)
    gu_parts = lax.dynamic_slice_in_dim(gu_parts, me * GU_SHARD, GU_SHARD, axis=3).reshape(
        N_DEV, DEPTH * GLA_LOWRANK, GU_SHARD)
    r_gu = DEPTH * GLA_LOWRANK
    g_w_gu, d_w_gu, nm_w_gu, nv_w_gu = (
        t.reshape(w_gate_up.shape) for t in _adamw(flat(w_gate_up, r_gu), gu_parts, flat(m_w_gate_up, r_gu),
                                                   flat(v_w_gate_up, r_gu), "adamw_w_gate_up", r_gu))
    return (loss, grad_x,
            g_w_ada, g_b_ada, g_g_pre, g_w_in, g_w_gu, g_b_gu, g_g_gla, g_g_dil, g_w_out, g_g_post,
            d_w_ada, d_b_ada, d_g_pre, d_w_in, d_w_gu, d_b_gu, d_g_gla, d_g_dil, d_w_out, d_g_post,
            nm_w_ada, nm_b_ada, nm_g_pre, nm_w_in, nm_w_gu, nm_b_gu, nm_g_gla, nm_g_dil, nm_w_out, nm_g_post,
            nv_w_ada, nv_b_ada, nv_g_pre, nv_w_in, nv_w_gu, nv_b_gu, nv_g_gla, nv_g_dil, nv_w_out, nv_g_post)


def _adamw_replicated(small, params, where, loss_at):
    n_parts = small.shape[0]

    def body(*refs):
        s_ref, p_refs, o_refs = refs[0], refs[1:1 + 3 * len(params)], refs[1 + 3 * len(params):]
        total = s_ref[0]
        for k in range(1, n_parts):
            total = total + s_ref[k]
        for i, (row, col) in enumerate(where):
            w_ref, m_ref, v_ref = p_refs[3 * i:3 * i + 3]
            n = w_ref.shape[1]
            g = jnp.concatenate([total[row + 3 * l:row + 3 * l + 1, col:col + n] for l in range(DEPTH)], axis=0)
            o_refs[4 * i][...] = g
            o_refs[4 * i + 1][...], o_refs[4 * i + 2][...], o_refs[4 * i + 3][...] = _adam_math(
                w_ref[...], g, m_ref[...], v_ref[...])
        o_refs[-1][...] = jnp.broadcast_to(total[loss_at[0]:loss_at[0] + 1, loss_at[1]:loss_at[1] + 1], (8, LANE))

    flat = [a for p in params for a in p]
    shapes = [jax.ShapeDtypeStruct(p[0].shape, F32) for p in params for _ in range(4)]
    outs = pl.pallas_call(body, name="adamw_replicated",
                          out_shape=shapes + [jax.ShapeDtypeStruct((8, LANE), F32)])(small, *flat)
    return [tuple(outs[4 * i:4 * i + 4]) for i in range(len(params))], outs[-1][0, 0]
```

```python
import functools
import math

import jax
import jax.numpy as jnp
from jax import lax
from jax.experimental import pallas as pl
from jax.experimental.pallas import tpu as pltpu

F32 = jnp.float32
BF16 = jnp.bfloat16

N_DEV = 8
D_MODEL = 1024
DEPTH = 2
GLA_HEADS = 4
GLA_DK = 64
GLA_DV = 128
GLA_CHUNK = 64
GLA_TAU = 16.0
GLA_LOWRANK = 16
DIL_HEADS = 4
DIL_HD = 128
DIL_BLOCK = 128
DIL_DILATIONS = (1, 4, 16)
ROPE_THETA = 10000.0
EPS = 1e-6
IN_COLS = 3600
W_IN_SHARD = IN_COLS // N_DEV
ADA_SHARD = 3 * D_MODEL // N_DEV
OUT_SHARD = D_MODEL // N_DEV
GU_COLS = GLA_HEADS * GLA_DK
GU_SHARD = GU_COLS // N_DEV

ADAM_LR = 0.001
ADAM_B1 = 0.9
ADAM_B2 = 0.999
ADAM_EPS = 1e-08
ADAM_WD = 0.01
ADAM_STEP = 10

NP = 3712
COL_Z, COL_QA, COL_KA, COL_QB, COL_KB, COL_VA, COL_VB, COL_LR = 0, 1024, 1280, 1536, 2048, 2560, 3072, 3584
NP_F32 = COL_VA
NP_BF16 = NP - NP_F32
LANE = 128
MASK_VALUE = -1e30

MESH = pl.DeviceIdType.MESH
ANY = pl.BlockSpec(memory_space=pl.ANY)


def _params(sem=None, vmem_mb=None):
    kw = {}
    if sem is not None:
        kw["dimension_semantics"] = sem
    if vmem_mb is not None:
        kw["vmem_limit_bytes"] = vmem_mb * 1024 * 1024
    return pltpu.CompilerParams(**kw)


def _dot(a, b):
    return jnp.dot(a, b, preferred_element_type=F32)


def _dot_nt(a, b):
    return lax.dot_general(a, b, (((1,), (1,)), ((), ())), preferred_element_type=F32)


def _dot_tn(a, b):
    return lax.dot_general(a, b, (((0,), (0,)), ((), ())), preferred_element_type=F32)


def _sigmoid(z):
    return 1.0 / (1.0 + jnp.exp(-z))


def _log_sigmoid(z):
    return jnp.minimum(z, 0.0) - jnp.log(1.0 + jnp.exp(-jnp.abs(z)))


def _rowvec(v, width=D_MODEL):
    arr, row, cb = v
    return arr.reshape(arr.shape[0], 1, arr.shape[1]), pl.BlockSpec((None, 1, width), lambda *_: (row, 0, cb))


def _my_position():
    return lax.axis_index("x"), lax.axis_index("y"), lax.axis_index("c")


def _linear(px, py, pc):
    return 4 * px + 2 * py + pc


def _gather_phase(phase, x_ref, out_ref, send_sem, recv_sem, local_sem):
    m = x_ref.shape[0]
    x, y, c = _my_position()
    me, sibling = (x, y, c), (x, y, 1 - c)
    chips = [(1 - x, y), (x, 1 - y), (1 - x, 1 - y)]

    def rows(px, py, pc):
        return out_ref.at[pl.ds(_linear(px, py, pc) * m, m), :]

    def copy(k, block, to, src=None):
        return pltpu.make_async_remote_copy(
            src_ref=rows(*block) if src is None else src, dst_ref=rows(*block),
            send_sem=send_sem(k), recv_sem=recv_sem(k), device_id=to, device_id_type=MESH)

    mine = pltpu.make_async_copy(x_ref, rows(*me), local_sem)
    first = [copy(0, me, sibling, src=x_ref)] + [copy(1 + j, me, (*chip, c), src=x_ref) for j, chip in enumerate(chips)]
    passed = [copy(4 + j, (*chip, c), sibling) for j, chip in enumerate(chips)]
    if phase == "start":
        mine.start()
        for cp in first:
            cp.start()
    elif phase == "forward":
        for j, chip in enumerate(chips):
            copy(1 + j, (*chip, c), me).wait_recv()
            passed[j].start()
    else:
        copy(0, sibling, me).wait_recv()
        for j, chip in enumerate(chips):
            copy(4 + j, (*chip, 1 - c), me).wait_recv()
        for cp in first + passed:
            cp.wait_send()
        mine.wait()


def _exchange_phase(phase, x_ref, out_ref, send_sem, recv_sem, local_sem):
    m = x_ref.shape[0] // N_DEV
    x, y, c = _my_position()
    me = _linear(x, y, c)

    def rows(ref, idx):
        return ref.at[pl.ds(idx * m, m), :]

    peers = [(1 - x if j & 4 else x, 1 - y if j & 2 else y, 1 - c if j & 1 else c) for j in range(1, N_DEV)]
    local = pltpu.make_async_copy(rows(x_ref, me), rows(out_ref, me), local_sem)
    sends = [pltpu.make_async_remote_copy(
        src_ref=rows(x_ref, _linear(*peer)), dst_ref=rows(out_ref, me),
        send_sem=send_sem(j), recv_sem=recv_sem(j), device_id=peer, device_id_type=MESH) for j, peer in enumerate(peers)]
    if phase == "start":
        local.start()
        for cp in sends:
            cp.start()
    else:
        for j, peer in enumerate(peers):
            pltpu.make_async_remote_copy(
                src_ref=rows(x_ref, _linear(*peer)), dst_ref=rows(out_ref, _linear(*peer)),
                send_sem=send_sem(j), recv_sem=recv_sem(j), device_id=peer, device_id_type=MESH).wait_recv()
        for cp in sends:
            cp.wait_send()
        local.wait()


def _pairsum_exchange_phase(phase, x_ref, out_refs, send_sem, recv_sem, local_sem):
    out_ref, stage_ref, pair_ref = out_refs
    m, n = x_ref.shape[0] // N_DEV, x_ref.shape[1]
    x, y, c = _my_position()
    mine = 2 * x + y
    chips = [(qx, qy) for qx in range(2) for qy in range(2)]
    others = [(1 - x, y), (x, 1 - y), (1 - x, 1 - y)]

    def rows(ref, idx):
        return ref.at[pl.ds(idx * m, m), :]

    def remote(src, dst, k, to):
        return pltpu.make_async_remote_copy(src_ref=src, dst_ref=dst, send_sem=send_sem(k), recv_sem=recv_sem(k),
                                            device_id=to, device_id_type=MESH)

    to_sibling = [remote(rows(x_ref, _linear(qx, qy, 1 - c)), rows(stage_ref, q), q, (x, y, 1 - c))
                  for q, (qx, qy) in enumerate(chips)]
    to_chips = [remote(rows(pair_ref, 2 * qx + qy), rows(out_ref, mine), 4 + j, (qx, qy, c))
                for j, (qx, qy) in enumerate(others)]
    keep = pltpu.make_async_copy(rows(pair_ref, mine), rows(out_ref, mine), local_sem)
    if phase == "start":
        for cp in to_sibling:
            cp.start()
    elif phase == "reduce":
        for cp in to_sibling:
            cp.wait_recv()

        def through_vmem(a_buf, b_buf, sems):
            tr = 128
            loads = [(pltpu.make_async_copy(rows(x_ref, _linear(qx, qy, c)), a_buf.at[q % 2], sems.at[q % 2]),
                      pltpu.make_async_copy(rows(stage_ref, q), b_buf.at[q % 2], sems.at[2 + q % 2]))
                     for q, (qx, qy) in enumerate(chips)]
            stores = [pltpu.make_async_copy(a_buf.at[q % 2], rows(pair_ref, q), sems.at[4 + q % 2]) for q in range(4)]
            for q in range(4):
                if q >= 2:
                    stores[q - 2].wait()
                for cp in loads[q]:
                    cp.start()
                for cp in loads[q]:
                    cp.wait()

                def add(r, carry, q=q):
                    tile = pl.ds(pl.multiple_of(r * tr, tr), tr)
                    a_buf[q % 2, tile, :] = (a_buf[q % 2, tile, :].astype(F32)
                                             + b_buf[q % 2, tile, :].astype(F32)).astype(x_ref.dtype)
                    return carry

                lax.fori_loop(0, m // tr, add, 0)
                stores[q].start()
            stores[2].wait()
            stores[3].wait()

        pl.run_scoped(through_vmem, pltpu.VMEM((2, m, n), x_ref.dtype), pltpu.VMEM((2, m, n), x_ref.dtype),
                      pltpu.SemaphoreType.DMA((6,)))
    elif phase == "send":
        keep.start()
        for cp in to_chips:
            cp.start()
    else:
        for j, (qx, qy) in enumerate(others):
            remote(rows(pair_ref, mine), rows(out_ref, 2 * qx + qy), 4 + j, (qx, qy, c)).wait_recv()
        for cp in to_sibling + to_chips:
            cp.wait_send()
        keep.wait()


_COMM_PHASES = {"gather": (_gather_phase, ("start", "forward", "finish")),
                "exchange": (_exchange_phase, ("start", "finish")),
                "pairsum_exchange": (_pairsum_exchange_phase, ("start", "reduce", "send", "finish"))}


def _comm_scratch(n_arrays):
    return [pltpu.SemaphoreType.DMA((n_arrays, 7)), pltpu.SemaphoreType.DMA((n_arrays, 7)),
            pltpu.SemaphoreType.DMA((n_arrays,))]


def _comm_run(kind, phases, x_refs, out_refs, send_sems, recv_sems, local_sems):
    fn = _COMM_PHASES[kind][0]
    per = len(out_refs) // len(x_refs)
    for phase in phases:
        for a, x_ref in enumerate(x_refs):
            outs = out_refs[a] if per == 1 else tuple(out_refs[per * a:per * (a + 1)])
            fn(phase, x_ref, outs, lambda k, a=a: send_sems.at[a, k], lambda k, a=a: recv_sems.at[a, k],
               local_sems.at[a])


def _comm_out_shapes(kind, arrays):
    if kind == "pairsum_exchange":
        return [jax.ShapeDtypeStruct((a.shape[0] // 2, a.shape[1]), a.dtype) for a in arrays for _ in range(3)]
    return [jax.ShapeDtypeStruct((N_DEV * a.shape[0], a.shape[1]) if kind == "gather" else a.shape, a.dtype)
            for a in arrays]


def _comm_call(kind, arrays, name):
    n = len(arrays)
    shapes = _comm_out_shapes(kind, arrays)

    def body(*refs):
        _comm_run(kind, _COMM_PHASES[kind][1], refs[:n], refs[n:n + len(shapes)], *refs[n + len(shapes):])

    return pl.pallas_call(body, name=name, out_shape=shapes, in_specs=[ANY] * n, out_specs=[ANY] * len(shapes),
                          scratch_shapes=_comm_scratch(n))(*arrays)


def _all_gather(xs, name):
    return _comm_call("gather", [xs], name)[0]


def _mod_fwd(c_all, w_ada):
    def body(c_ref, w_ref, o_ref):
        cv = c_ref[...]
        sc = cv * _sigmoid(cv)
        o_ref[0] = _dot(sc.astype(BF16), w_ref[0].astype(BF16))

    return pl.pallas_call(
        body, name="mod_fwd", grid=(DEPTH,),
        out_shape=jax.ShapeDtypeStruct((DEPTH, N_DEV, ADA_SHARD), F32),
        in_specs=[pl.BlockSpec((N_DEV, D_MODEL), lambda l: (0, 0)),
                  pl.BlockSpec((1, D_MODEL, ADA_SHARD), lambda l: (l, 0, 0))],
        out_specs=pl.BlockSpec((1, N_DEV, ADA_SHARD), lambda l: (l, 0, 0)),
        compiler_params=_params(("arbitrary",)),
    )(c_all, w_ada)


def _w_ada_grad(c_all, dmod_cols):
    def body(c_ref, d_ref, o_ref):
        cv = c_ref[...]
        sc = cv * _sigmoid(cv)
        o_ref[0] = lax.dot_general(sc, d_ref[0], (((0,), (0,)), ((), ())), precision=lax.Precision.HIGHEST,
                                   preferred_element_type=F32)

    return pl.pallas_call(
        body, name="w_ada_grad", grid=(DEPTH,),
        out_shape=jax.ShapeDtypeStruct((DEPTH, D_MODEL, ADA_SHARD), F32),
        in_specs=[pl.BlockSpec((N_DEV, D_MODEL), lambda l: (0, 0)),
                  pl.BlockSpec((1, N_DEV, ADA_SHARD), lambda l: (l, 0, 0))],
        out_specs=pl.BlockSpec((1, D_MODEL, ADA_SHARD), lambda l: (l, 0, 0)),
        compiler_params=_params(("arbitrary",)),
    )(c_all, dmod_cols)


def _comm_plumbing(comm):
    if not comm:
        return 0, [], []
    return len(comm[1]), _comm_out_shapes(*comm), _comm_scratch(len(comm[1]))


def _split_refs(refs, n_in, n_out, n_scratch, comm):
    ci, shapes, _ = _comm_plumbing(comm)
    co = len(shapes)
    a, b, c = n_in + ci, n_in + ci + n_out, n_in + ci + n_out + co
    return refs[:n_in], refs[a:b], refs[c:c + n_scratch], refs[n_in:a], refs[b:c], refs[c + n_scratch:]


def _prenorm_proj(x, g_pre, scale, shift, w_new, comm=None, ts=256):
    s_len = x.shape[0]
    n_cin, c_shapes, c_scratch = _comm_plumbing(comm)

    def body(*refs):
        (x_ref, g_ref, sc_ref, sh_ref, w_ref), (pf_ref, pb_ref, h_ref), _, cin, cout, csem = _split_refs(
            refs, 5, 3, 0, comm)
        comm_before, comm_after = _comm_hooks(comm, cin, cout, csem, steps=s_len // ts)
        comm_before()
        xv = x_ref[...]
        rstd = lax.rsqrt(jnp.mean(xv * xv, axis=-1, keepdims=True) + EPS)
        h = (xv * rstd * g_ref[...]) * (1.0 + sc_ref[...]) + sh_ref[...]
        hb = h.astype(BF16)
        h_ref[...] = hb
        for j in range(0, NP, 512):
            w = min(512, NP - j)
            acc = _dot(hb, w_ref[:, j:j + w])
            if j < NP_F32:
                pf_ref[:, j:j + w] = acc
            else:
                pb_ref[:, j - NP_F32:j - NP_F32 + w] = acc.astype(BF16)
        comm_after()

    (g_pre, g_spec), (scale, sc_spec), (shift, sh_spec) = _rowvec(g_pre), _rowvec(scale), _rowvec(shift)
    return pl.pallas_call(
        body, name="prenorm_proj_comm" if comm else "prenorm_proj", grid=(s_len // ts,),
        out_shape=[jax.ShapeDtypeStruct((s_len, NP_F32), F32), jax.ShapeDtypeStruct((s_len, NP_BF16), BF16),
                   jax.ShapeDtypeStruct((s_len, D_MODEL), BF16)] + c_shapes,
        in_specs=[pl.BlockSpec((ts, D_MODEL), lambda i: (i, 0)), g_spec, sc_spec, sh_spec,
                  pl.BlockSpec((D_MODEL, NP), lambda i: (0, 0))] + [ANY] * n_cin,
        out_specs=[pl.BlockSpec((ts, NP_F32), lambda i: (i, 0)), pl.BlockSpec((ts, NP_BF16), lambda i: (i, 0)),
                   pl.BlockSpec((ts, D_MODEL), lambda i: (i, 0))] + [ANY] * len(c_shapes),
        scratch_shapes=c_scratch,
        compiler_params=_params(("arbitrary",), 48),
    )(x, g_pre, scale, shift, w_new, *(comm[1] if comm else []))


GLA_GROUP = 16


def _gla_group_rows(t):
    return [pl.ds(pl.multiple_of((t * GLA_GROUP + j) * GLA_CHUNK, GLA_CHUNK), GLA_CHUNK) for j in range(GLA_GROUP)]


def _gla_chunks_common(q_ref, k_ref, lr_ref, wgu_ref, bgu_ref, rows_list):
    c = GLA_CHUNK
    ri = lax.broadcasted_iota(jnp.int32, (c, c), 0)
    ci = lax.broadcasted_iota(jnp.int32, (c, c), 1)
    tril = (ri >= ci).astype(F32)
    zs = [_dot(lr_ref[rows, :], wgu_ref[...]) + bgu_ref[...] for rows in rows_list]
    las = [_log_sigmoid(z) * (1.0 / GLA_TAU) for z in zs]
    bs = [jnp.dot(tril, la, precision=lax.Precision.HIGHEST, preferred_element_type=F32) for la in las]
    out = []
    for rows, z, b in zip(rows_list, zs, bs):
        q = q_ref[rows, :] * (GLA_DK ** -0.5)
        k = k_ref[rows, :]
        bl = b[c - 1:c, :]
        out.append(dict(z=z, b=b, bl=bl, qe=q * jnp.exp(b), ke=k * jnp.exp(-b), kend=k * jnp.exp(bl - b),
                        dec=jnp.exp(bl)))
    return out, ri, ci


def _head_lane_mask(hh):
    return (lax.broadcasted_iota(jnp.int32, (1, LANE), 1) // GLA_DK) == hh


def _state_block_mask():
    r = lax.broadcasted_iota(jnp.int32, (2 * GLA_DV, LANE), 0) // GLA_DV
    cc = lax.broadcasted_iota(jnp.int32, (2 * GLA_DV, LANE), 1) // GLA_DK
    return r == cc


def _gla_fwd(pf, pb, wgu, bgu, layer, comm=None):
    s_len = pf.shape[0]
    nc = s_len // GLA_CHUNK
    ncomm = len(comm[1]) if comm else 0

    def body(*refs):
        q_ref, k_ref, v_ref, lr_ref, wgu_ref, bgu_ref = refs[:6]
        cin, (o_ref, st_ref), cout = refs[6:6 + ncomm], refs[6 + ncomm:8 + ncomm], refs[8 + ncomm:8 + 2 * ncomm]
        qe_s, cs_s, dec_s = refs[8 + 2 * ncomm:11 + 2 * ncomm]
        comm_before, comm_after = _comm_hooks(comm, cin, cout, refs[11 + 2 * ncomm:], steps=2)
        comm_before()
        bd = _state_block_mask()

        def local(t, carry):
            rows_list = _gla_group_rows(t)
            cm, ri, ci = _gla_chunks_common(q_ref, k_ref, lr_ref, wgu_ref, bgu_ref, rows_list)
            vs = [v_ref[rows, :] for rows in rows_list]
            kebs = [c["ke"].astype(BF16) for c in cm]
            a = [[jnp.where(ri >= ci, _dot_nt(jnp.where(_head_lane_mask(hh), c["qe"], 0.0).astype(BF16), keb), 0.0)
                  .astype(BF16) for hh in range(2)] for c, keb in zip(cm, kebs)]
            oi = [[_dot(ah[hh], v[:, hh * GLA_DV:(hh + 1) * GLA_DV]) for hh in range(2)] for ah, v in zip(a, vs)]
            cs = [jnp.where(bd, _dot_tn(v, c["kend"].astype(BF16)), 0.0) for c, v in zip(cm, vs)]
            for j, (rows, c) in enumerate(zip(rows_list, cm)):
                n = t * GLA_GROUP + j
                o_ref[rows, :] = jnp.concatenate(oi[j], axis=1)
                qe_s[rows, :] = c["qe"].astype(BF16)
                cs_s[n] = cs[j]
                dec_s[n] = jnp.broadcast_to(c["dec"], (8, LANE))
            return carry

        lax.fori_loop(0, nc // GLA_GROUP, local, 0)

        def scan(n, st):
            st_ref[0, n] = st.astype(BF16)
            return dec_s[n][0:1, :] * st + cs_s[n]

        lax.fori_loop(0, nc, scan, jnp.zeros((2 * GLA_DV, LANE), F32))

        def inter(t, carry):
            rows_list = _gla_group_rows(t)
            add = [_dot_nt(qe_s[rows, :], st_ref[0, t * GLA_GROUP + j]) for j, rows in enumerate(rows_list)]
            for rows, av in zip(rows_list, add):
                o_ref[rows, :] = o_ref[rows, :] + av
            return carry

        lax.fori_loop(0, nc // GLA_GROUP, inter, 0)
        comm_after()

    return pl.pallas_call(
        body, name="gla_fwd_comm" if comm else "gla_fwd", grid=(2,),
        out_shape=[jax.ShapeDtypeStruct((s_len, GLA_HEADS * GLA_DV), F32),
                   jax.ShapeDtypeStruct((2, nc, 2 * GLA_DV, LANE), BF16)] + (_comm_out_shapes(*comm) if comm else []),
        in_specs=[pl.BlockSpec((s_len, LANE), lambda g: (0, COL_QA // LANE + g)),
                  pl.BlockSpec((s_len, LANE), lambda g: (0, COL_KA // LANE + g)),
                  pl.BlockSpec((s_len, 2 * GLA_DV), lambda g: (0, (COL_VA - NP_F32) // (2 * GLA_DV) + g)),
                  pl.BlockSpec((s_len, LANE), lambda g: (0, (COL_LR - NP_F32) // LANE)),
                  pl.BlockSpec((None, LANE, LANE), lambda g: (layer, 0, g)),
                  pl.BlockSpec((None, 1, LANE), lambda g: (layer, 0, g))] + [ANY] * ncomm,
        out_specs=[pl.BlockSpec((s_len, 2 * GLA_DV), lambda g: (0, g)),
                   pl.BlockSpec((1, nc, 2 * GLA_DV, LANE), lambda g: (g, 0, 0, 0))] + [ANY] * ncomm,
        scratch_shapes=[pltpu.VMEM((s_len, LANE), BF16), pltpu.VMEM((nc, 2 * GLA_DV, LANE), F32),
                        pltpu.VMEM((nc, 8, LANE), F32)] + (_comm_scratch(ncomm) if comm else []),
        compiler_params=_params(("arbitrary",), 56),
    )(pf, pf, pb, pb, wgu, bgu.reshape(bgu.shape[0], 1, GU_COLS), *(comm[1] if comm else []))


def _rope_tables(s_len):
    inv_freq = ROPE_THETA ** (-jnp.arange(0, DIL_HD, 2, dtype=F32) / DIL_HD)
    ang = jnp.arange(s_len, dtype=F32)[:, None] * inv_freq[None, :]
    cos, sin = jnp.cos(ang), jnp.sin(ang)
    return jnp.concatenate([cos, cos], axis=1), jnp.concatenate([-sin, sin], axis=1)


def _rope(xv, cos, sin_signed):
    return xv * cos + pltpu.roll(xv, DIL_HD // 2, 1) * sin_signed


DIL_GROUP = 8


def _dil_pair_block(i, half, d, nblk, group=DIL_GROUP):
    nb = nblk // d
    j = i + half * (nblk // group)
    if nb >= 2 * group:
        r, n = j % d, j // d
    else:
        r, n = j // nb, j % nb
    kb = jnp.maximum(n - 1, 0)
    qs = r + d * DIL_BLOCK * n
    ks = r + d * DIL_BLOCK * kb
    return qs, ks, jnp.minimum(n, 1)


def _dil_fill_bias(bias):
    qi = lax.broadcasted_iota(jnp.int32, (DIL_BLOCK, 2 * DIL_BLOCK), 0)
    kj = lax.broadcasted_iota(jnp.int32, (DIL_BLOCK, 2 * DIL_BLOCK), 1)
    for sel in range(2):
        dist = qi - kj + DIL_BLOCK * sel
        bias[sel] = jnp.where((dist >= 0) & (dist <= DIL_BLOCK), 0.0, MASK_VALUE)


def _strided(start, size, d):
    return pl.ds(start, size) if d == 1 else pl.ds(start, size, stride=d)


def _comm_hooks(comm, cin, cout, csem, steps=DIL_HEADS):
    def before():
        if comm:
            @pl.when(pl.program_id(0) == 0)
            def _():
                _comm_run(comm[0], ("start",), cin, cout, *csem)

            if comm[0] == "gather":
                @pl.when(pl.program_id(0) == steps - 1)
                def _():
                    _comm_run(comm[0], ("forward",), cin, cout, *csem)

            if comm[0] == "pairsum_exchange":
                @pl.when(pl.program_id(0) == (1 if steps <= 4 else 3))
                def _():
                    _comm_run(comm[0], ("reduce", "send"), cin, cout, *csem)

    def after():
        if comm:
            @pl.when(pl.program_id(0) == steps - 1)
            def _():
                _comm_run(comm[0], ("finish",), cin, cout, *csem)

    return before, after


def _dil_fwd(pf, pb, cos, sin_signed, comm=None):
    s_len = pf.shape[0]
    nblk = s_len // DIL_BLOCK
    prep_rows = 256
    scale = DIL_HD ** -0.5
    nc = len(comm[1]) if comm else 0

    def body(*refs):
        q_ref, k_ref, v_ref, cos_ref, sin_ref = refs[:5]
        cin, (o_ref, lse_ref), cout = refs[5:5 + nc], refs[5 + nc:7 + nc], refs[7 + nc:7 + 2 * nc]
        qf, kf, vf, o0, o1, o2, l0, l1, l2, bias = refs[7 + 2 * nc:17 + 2 * nc]
        comm_before, comm_after = _comm_hooks(comm, cin, cout, refs[17 + 2 * nc:])
        comm_before()
        _dil_fill_bias(bias)

        def prep(t, carry):
            rows = pl.ds(pl.multiple_of(t * prep_rows, prep_rows), prep_rows)
            cs, sn = cos_ref[rows, :], sin_ref[rows, :]
            qf[rows, :] = _rope(q_ref[rows, :], cs, sn)
            kf[rows, :] = _rope(k_ref[rows, :], cs, sn)
            vf[rows, :] = v_ref[rows, :].astype(F32)
            return carry

        lax.fori_loop(0, s_len // prep_rows, prep, 0)
        for d, o_p, l_p in zip(DIL_DILATIONS, (o0, o1, o2), (l0, l1, l2)):
            if nblk // d == 2:
                units = DIL_GROUP // 2

                def whole(i, carry, d=d, o_p=o_p, l_p=l_p, units=units):
                    rows = [_strided(i + u * (d // units), 2 * DIL_BLOCK, d) for u in range(units)]
                    ld = [(qf[rw, :].astype(BF16), kf[rw, :].astype(BF16), vf[rw, :].astype(BF16)) for rw in rows]
                    both = bias[...].reshape(2 * DIL_BLOCK, 2 * DIL_BLOCK)
                    s = [_dot_nt(qb, kk) * scale + both for qb, kk, _ in ld]
                    m = [jnp.max(sv, axis=-1, keepdims=True) for sv in s]
                    p = [jnp.exp(sv - mv) for sv, mv in zip(s, m)]
                    den = [jnp.sum(pv, axis=-1, keepdims=True) for pv in p]
                    r = [_dot(pv.astype(BF16), vv) for pv, (_, _, vv) in zip(p, ld)]
                    for rv, dv, mv, rw in zip(r, den, m, rows):
                        o_p[rw, :] = rv / dv
                        l_p[rw, :] = jnp.broadcast_to(mv + jnp.log(dv), (2 * DIL_BLOCK, DIL_HD))
                    return carry

                lax.fori_loop(0, d // units, whole, 0)
                continue

            def pair(i, carry, d=d, o_p=o_p, l_p=l_p):
                idx = [_dil_pair_block(i, half, d, nblk, DIL_GROUP) for half in range(DIL_GROUP)]
                ld = [(qf[_strided(qs, DIL_BLOCK, d), :].astype(BF16),
                       kf[_strided(ks, 2 * DIL_BLOCK, d), :].astype(BF16),
                       vf[_strided(ks, 2 * DIL_BLOCK, d), :].astype(BF16)) for qs, ks, _ in idx]
                s = [_dot_nt(qb, kk) * scale + bias[sel] for (qb, kk, _), (_, _, sel) in zip(ld, idx)]
                m = [jnp.max(sv, axis=-1, keepdims=True) for sv in s]
                p = [jnp.exp(sv - mv) for sv, mv in zip(s, m)]
                den = [jnp.sum(pv, axis=-1, keepdims=True) for pv in p]
                r = [_dot(pv.astype(BF16), vv) for pv, (_, _, vv) in zip(p, ld)]
                for rv, dv, mv, (qs, _, _) in zip(r, den, m, idx):
                    o_p[_strided(qs, DIL_BLOCK, d), :] = rv / dv
                    l_p[_strided(qs, DIL_BLOCK, d), :] = jnp.broadcast_to(mv + jnp.log(dv), (DIL_BLOCK, DIL_HD))
                return carry

            lax.fori_loop(0, nblk // DIL_GROUP, pair, 0)

        def comb(t, carry):
            rows = pl.ds(pl.multiple_of(t * prep_rows, prep_rows), prep_rows)
            a0, a1, a2 = l0[rows, :], l1[rows, :], l2[rows, :]
            m = jnp.maximum(jnp.maximum(a0, a1), a2)
            e0, e1, e2 = jnp.exp(a0 - m), jnp.exp(a1 - m), jnp.exp(a2 - m)
            tot = e0 + e1 + e2
            o_ref[rows, :] = (e0 * o0[rows, :] + e1 * o1[rows, :] + e2 * o2[rows, :]) / tot
            lse_ref[rows, :] = m + jnp.log(tot)
            return carry

        lax.fori_loop(0, s_len // prep_rows, comb, 0)
        comm_after()

    head = lambda base: pl.BlockSpec((s_len, DIL_HD), lambda h: (0, base // DIL_HD + h))
    table = pl.BlockSpec((s_len, DIL_HD), lambda h: (0, 0))
    out = pl.BlockSpec((s_len, DIL_HD), lambda h: (0, h))
    shp = jax.ShapeDtypeStruct((s_len, DIL_HEADS * DIL_HD), F32)
    return pl.pallas_call(
        body, name="dil_fwd_comm" if comm else "dil_fwd", grid=(DIL_HEADS,),
        out_shape=[shp, shp] + (_comm_out_shapes(*comm) if comm else []),
        in_specs=[head(COL_QB), head(COL_KB), head(COL_VB - NP_F32), table, table] + [ANY] * nc,
        out_specs=[out, out] + [ANY] * nc,
        scratch_shapes=[pltpu.VMEM((s_len, DIL_HD), F32) for _ in range(9)]
        + [pltpu.VMEM((2, DIL_BLOCK, 2 * DIL_BLOCK), F32)] + (_comm_scratch(nc) if comm else []),
        compiler_params=_params(("arbitrary",), 56),
    )(pf, pf, pb, cos, sin_signed, *(comm[1] if comm else []))


def _silu_and_grad(z):
    sg = _sigmoid(z)
    return z * sg, sg * (1.0 + z * (1.0 - sg))


def _post_fwd(o_a, o_b, pf, g_heads, w_out, x, gate, g_post, target=None, ts=256):
    s_len = x.shape[0]
    half = GLA_HEADS * GLA_DV
    last = target is not None

    def body(*refs):
        oa_ref, ob_ref, z_ref, gh_ref, w_ref, x_ref, gate_ref, gp_ref = refs[:8]
        xo_ref, y_ref, u_ref = refs[8 + last:11 + last]
        for src, base in ((oa_ref, 0), (ob_ref, half)):
            for hh in range(4):
                lo = hh * LANE
                og = src[:, lo:lo + LANE]
                on = og * lax.rsqrt(jnp.mean(og * og, axis=-1, keepdims=True) + EPS)
                zg = z_ref[:, base + lo:base + lo + LANE].astype(F32)
                y_ref[:, base + lo:base + lo + LANE] = (on * gh_ref[:, base + lo:base + lo + LANE]
                                                        * (zg * _sigmoid(zg))).astype(BF16)
        u = _dot(y_ref[...], w_ref[...])
        u_ref[...] = u.astype(BF16)
        rstd = lax.rsqrt(jnp.mean(u * u, axis=-1, keepdims=True) + EPS)
        x_out = x_ref[...] + gate_ref[...] * (u * rstd * gp_ref[...])
        if last:
            t_ref, loss_ref = refs[8], refs[12]

            @pl.when(pl.program_id(0) == 0)
            def _():
                loss_ref[...] = jnp.zeros_like(loss_ref)

            e = x_out - t_ref[...]
            xo_ref[...] = e * (1.0 / D_MODEL)
            loss_ref[...] += 0.5 * jnp.sum(jnp.mean(e * e, axis=-1, keepdims=True))
        else:
            xo_ref[...] = x_out

    (g_heads, gh_spec), (gate, gate_spec), (g_post, gp_spec) = _rowvec(g_heads), _rowvec(gate), _rowvec(g_post)
    tile = pl.BlockSpec((ts, D_MODEL), lambda i: (i, 0))
    halft = pl.BlockSpec((ts, half), lambda i: (i, 0))
    return pl.pallas_call(
        body, name="post_fwd_loss" if last else "post_fwd", grid=(s_len // ts,),
        out_shape=[jax.ShapeDtypeStruct((s_len, D_MODEL), F32), jax.ShapeDtypeStruct((s_len, D_MODEL), BF16),
                   jax.ShapeDtypeStruct((s_len, D_MODEL), BF16)]
        + ([jax.ShapeDtypeStruct((8, LANE), F32)] if last else []),
        in_specs=[halft, halft, tile, gh_spec, pl.BlockSpec((D_MODEL, D_MODEL), lambda i: (0, 0)), tile, gate_spec,
                  gp_spec] + ([tile] if last else []),
        out_specs=[tile, tile, tile] + ([pl.BlockSpec((8, LANE), lambda i: (0, 0))] if last else []),
        compiler_params=_params(("arbitrary",), 40),
    )(o_a, o_b, pf, g_heads, w_out, x, gate, g_post, *([target] if last else []))


def _post_bwd(dxo, u, gate, g_post, w_out, o_a, o_b, pf, g_heads, ts=256):
    s_len = dxo.shape[0]
    half = GLA_HEADS * GLA_DV

    def body(dx_ref, u_ref, gate_ref, gp_ref, w_ref, oa_ref, ob_ref, z_ref, gh_ref, du_ref, do_ref, dz_ref, sums_ref):
        @pl.when(pl.program_id(0) == 0)
        def _():
            sums_ref[...] = jnp.zeros_like(sums_ref)

        dx = dx_ref[...]
        u = u_ref[...].astype(F32)
        rstd = lax.rsqrt(jnp.mean(u * u, axis=-1, keepdims=True) + EPS)
        un = u * rstd
        sums_ref[0:1, :] += jnp.sum(dx * (un * gp_ref[...]), axis=0, keepdims=True)
        drn = dx * gate_ref[...]
        sums_ref[1:2, :] += jnp.sum(drn * un, axis=0, keepdims=True)
        dun = drn * gp_ref[...]
        du = rstd * (dun - un * jnp.mean(dun * un, axis=-1, keepdims=True))
        dub = du.astype(BF16)
        du_ref[...] = dub
        dy = _dot_nt(dub, w_ref[...])
        for src, base in ((oa_ref, 0), (ob_ref, half)):
            for hh in range(4):
                lo = base + hh * LANE
                og = src[:, hh * LANE:(hh + 1) * LANE]
                rs = lax.rsqrt(jnp.mean(og * og, axis=-1, keepdims=True) + EPS)
                on = og * rs
                zg = z_ref[:, lo:lo + LANE].astype(F32)
                sz, dsz = _silu_and_grad(zg)
                gg = gh_ref[:, lo:lo + LANE]
                dyg = dy[:, lo:lo + LANE]
                sums_ref[2:3, lo:lo + LANE] += jnp.sum(dyg * sz * on, axis=0, keepdims=True)
                dz_ref[:, lo:lo + LANE] = (dyg * on * gg * dsz).astype(BF16)
                don = dyg * gg * sz
                do_ref[:, lo:lo + LANE] = (rs * (don - on * jnp.mean(don * on, axis=-1, keepdims=True))).astype(BF16)

    (g_heads, gh_spec), (gate, gate_spec), (g_post, gp_spec) = _rowvec(g_heads), _rowvec(gate), _rowvec(g_post)
    tile = pl.BlockSpec((ts, D_MODEL), lambda i: (i, 0))
    halft = pl.BlockSpec((ts, half), lambda i: (i, 0))
    return pl.pallas_call(
        body, name="post_bwd", grid=(s_len // ts,),
        out_shape=(jax.ShapeDtypeStruct((s_len, D_MODEL), BF16), jax.ShapeDtypeStruct((s_len, D_MODEL), BF16),
                   jax.ShapeDtypeStruct((s_len, D_MODEL), BF16), jax.ShapeDtypeStruct((8, D_MODEL), F32)),
        in_specs=[tile, tile, gate_spec, gp_spec, pl.BlockSpec((D_MODEL, D_MODEL), lambda i: (0, 0)), halft, halft,
                  tile, gh_spec],
        out_specs=(tile, tile, tile, pl.BlockSpec((8, D_MODEL), lambda i: (0, 0))),
        compiler_params=_params(("arbitrary",), 40),
    )(dxo, u, gate, g_post, w_out, o_a, o_b, pf, g_heads)


def _gla_bwd(pf, pb, wgu, bgu, layer, states, do, comm=None):
    s_len = pf.shape[0]
    nc = s_len // GLA_CHUNK
    c = GLA_CHUNK
    n_cin, c_shapes, c_scratch = _comm_plumbing(comm)

    def body(*refs):
        ((q_ref, k_ref, v_ref, lr_ref, wgu_ref, bgu_ref, st_ref, do_ref),
         (dq_ref, dk_ref, dv_ref, dlr_ref, dwgu_ref, dbgu_ref), (ds_s, dec_s, dw_acc, db_acc),
         cin, cout, csem) = _split_refs(refs, 8, 6, 4, comm)
        comm_before, comm_after = _comm_hooks(comm, cin, cout, csem, steps=2)
        comm_before()
        dw_acc[...] = jnp.zeros_like(dw_acc)
        db_acc[...] = jnp.zeros_like(db_acc)
        bd = _state_block_mask()
        last_row = lax.broadcasted_iota(jnp.int32, (c, LANE), 0) == c - 1

        def local(t, carry):
            rows_list = _gla_group_rows(t)
            cm, _, _ = _gla_chunks_common(q_ref, k_ref, lr_ref, wgu_ref, bgu_ref, rows_list)
            loc = [jnp.where(bd, _dot_tn(do_ref[rows, :], cc["qe"].astype(BF16)), 0.0)
                   for rows, cc in zip(rows_list, cm)]
            for j, cc in enumerate(cm):
                ds_s[t * GLA_GROUP + j] = loc[j]
                dec_s[t * GLA_GROUP + j] = jnp.broadcast_to(cc["dec"], (8, LANE))
            return carry

        lax.fori_loop(0, nc // GLA_GROUP, local, 0)

        def scan(t, dst):
            n = nc - 1 - t
            loc = ds_s[n]
            ds_s[n] = dst
            return dec_s[n][0:1, :] * dst + loc

        lax.fori_loop(0, nc, scan, jnp.zeros((2 * GLA_DV, LANE), F32))

        def rest(t, carry):
            rows_list = _gla_group_rows(t)
            cm, ri, ci = _gla_chunks_common(q_ref, k_ref, lr_ref, wgu_ref, bgu_ref, rows_list)
            ns = [t * GLA_GROUP + j for j in range(GLA_GROUP)]
            vs = [v_ref[rows, :] for rows in rows_list]
            dobs = [do_ref[rows, :] for rows in rows_list]
            stbs = [st_ref[0, n] for n in ns]
            dsts = [ds_s[n] for n in ns]
            dstbs = [d.astype(BF16) for d in dsts]
            qebs = [cc["qe"].astype(BF16) for cc in cm]
            kebs = [cc["ke"].astype(BF16) for cc in cm]
            kendbs = [cc["kend"].astype(BF16) for cc in cm]
            hms = [_head_lane_mask(hh) for hh in range(2)]
            qehs = [[jnp.where(hm, cc["qe"], 0.0).astype(BF16) for hm in hms] for cc in cm]
            kehs = [[jnp.where(hm, cc["ke"], 0.0).astype(BF16) for hm in hms] for cc in cm]
            heads = lambda x: [x[:, hh * GLA_DV:(hh + 1) * GLA_DV] for hh in range(2)]
            vhs, dohs = [heads(v) for v in vs], [heads(d) for d in dobs]

            dqe0 = [_dot(dob, stb) for dob, stb in zip(dobs, stbs)]
            dkend = [_dot(v, dstb) for v, dstb in zip(vs, dstbs)]
            dv0 = [_dot_nt(kb, dstb) for kb, dstb in zip(kendbs, dstbs)]
            a_t = [[jnp.where(ci >= ri, _dot_nt(kehs[j][hh], qebs[j]), 0.0).astype(BF16) for hh in range(2)]
                   for j in range(GLA_GROUP)]
            da = [[jnp.where(ri >= ci, _dot_nt(dohs[j][hh], vhs[j][hh]), 0.0).astype(BF16) for hh in range(2)]
                  for j in range(GLA_GROUP)]
            da_t = [[jnp.where(ci >= ri, _dot_nt(vhs[j][hh], dohs[j][hh]), 0.0).astype(BF16) for hh in range(2)]
                    for j in range(GLA_GROUP)]
            dv1 = [[_dot(a_t[j][hh], dohs[j][hh]) for hh in range(2)] for j in range(GLA_GROUP)]
            dqe1 = [[_dot(da[j][hh], kebs[j]) for hh in range(2)] for j in range(GLA_GROUP)]
            dke1 = [[_dot(da_t[j][hh], qehs[j][hh]) for hh in range(2)] for j in range(GLA_GROUP)]

            dbs, dzs = [], []
            for j, (rows, cc) in enumerate(zip(rows_list, cm)):
                qe, ke, kend, b, bl = cc["qe"], cc["ke"], cc["kend"], cc["b"], cc["bl"]
                dqe = dqe0[j] + jnp.where(hms[0], dqe1[j][0], 0.0) + jnp.where(hms[1], dqe1[j][1], 0.0)
                dke = jnp.where(hms[0], dke1[j][0], 0.0) + jnp.where(hms[1], dke1[j][1], 0.0)
                dv_ref[rows, :] = (dv0[j] + jnp.concatenate(dv1[j], axis=1)).astype(BF16)
                dq_ref[rows, :] = (dqe * jnp.exp(b) * (GLA_DK ** -0.5)).astype(BF16)
                dk_ref[rows, :] = (dke * jnp.exp(-b) + dkend[j] * jnp.exp(bl - b)).astype(BF16)
                ddec = jnp.sum(dsts[j] * stbs[j].astype(F32), axis=0, keepdims=True)
                dbl = jnp.sum(dkend[j] * kend, axis=0, keepdims=True) + ddec * cc["dec"]
                dbs.append(dqe * qe - dke * ke - dkend[j] * kend + jnp.where(last_row, dbl, 0.0))
            triu = (ci >= ri).astype(F32)
            dlas = [jnp.dot(triu, db, precision=lax.Precision.HIGHEST, preferred_element_type=F32) for db in dbs]
            dzs = [dla * (1.0 / GLA_TAU) * _sigmoid(-cc["z"]) for dla, cc in zip(dlas, cm)]
            dzbs = [dz.astype(BF16) for dz in dzs]
            dlrs = [_dot_nt(dzb, wgu_ref[...]) for dzb in dzbs]
            dws = [_dot_tn(lr_ref[rows, :], dzb) for rows, dzb in zip(rows_list, dzbs)]
            for rows, dlr in zip(rows_list, dlrs):
                dlr_ref[0, rows, :] = dlr
            dw_acc[...] += functools.reduce(lambda x, y: x + y, dws)
            db_acc[0:1, :] += jnp.sum(functools.reduce(lambda x, y: x + y, dzs), axis=0, keepdims=True)
            return carry

        lax.fori_loop(0, nc // GLA_GROUP, rest, 0)
        dwgu_ref[...] = dw_acc[...]
        dbgu_ref[...] = db_acc[...]
        comm_after()

    pair = pl.BlockSpec((s_len, LANE), lambda g: (0, g))
    return pl.pallas_call(
        body, name="gla_bwd_comm" if comm else "gla_bwd", grid=(2,),
        out_shape=[jax.ShapeDtypeStruct((s_len, GU_COLS), BF16), jax.ShapeDtypeStruct((s_len, GU_COLS), BF16),
                   jax.ShapeDtypeStruct((s_len, GLA_HEADS * GLA_DV), BF16),
                   jax.ShapeDtypeStruct((2, s_len, LANE), F32),
                   jax.ShapeDtypeStruct((LANE, GU_COLS), F32), jax.ShapeDtypeStruct((8, GU_COLS), F32)] + c_shapes,
        in_specs=[pl.BlockSpec((s_len, LANE), lambda g: (0, COL_QA // LANE + g)),
                  pl.BlockSpec((s_len, LANE), lambda g: (0, COL_KA // LANE + g)),
                  pl.BlockSpec((s_len, 2 * GLA_DV), lambda g: (0, (COL_VA - NP_F32) // (2 * GLA_DV) + g)),
                  pl.BlockSpec((s_len, LANE), lambda g: (0, (COL_LR - NP_F32) // LANE)),
                  pl.BlockSpec((None, LANE, LANE), lambda g: (layer, 0, g)),
                  pl.BlockSpec((None, 1, LANE), lambda g: (layer, 0, g)),
                  pl.BlockSpec((1, nc, 2 * GLA_DV, LANE), lambda g: (g, 0, 0, 0)),
                  pl.BlockSpec((s_len, 2 * GLA_DV), lambda g: (0, g))] + [ANY] * n_cin,
        out_specs=[pair, pair, pl.BlockSpec((s_len, 2 * GLA_DV), lambda g: (0, g)),
                   pl.BlockSpec((1, s_len, LANE), lambda g: (g, 0, 0)),
                   pl.BlockSpec((LANE, LANE), lambda g: (0, g)), pl.BlockSpec((8, LANE), lambda g: (0, g))]
        + [ANY] * len(c_shapes),
        scratch_shapes=[pltpu.VMEM((nc, 2 * GLA_DV, LANE), F32), pltpu.VMEM((nc, 8, LANE), F32),
                        pltpu.VMEM((LANE, LANE), F32), pltpu.VMEM((8, LANE), F32)] + c_scratch,
        compiler_params=_params(("arbitrary",), 56),
    )(pf, pf, pb, pb, wgu, bgu.reshape(bgu.shape[0], 1, GU_COLS), states, do, *(comm[1] if comm else []))


def _dil_bwd(pf, pb, cos, sin_signed, do, o_b, lse, comm=None):
    s_len = pf.shape[0]
    nblk = s_len // DIL_BLOCK
    prep_rows = 256
    scale = DIL_HD ** -0.5
    nc = len(comm[1]) if comm else 0

    def body(*refs):
        ((q_ref, k_ref, v_ref, cos_ref, sin_ref, do_ref, o_ref, lse_ref), (dq_ref, dk_ref, dv_ref),
         (qf, kf, vf, dof, dl, dqa, dka, dva, bias), cin, cout, csem) = _split_refs(refs, 8, 3, 9, comm)
        comm_before, comm_after = _comm_hooks(comm, cin, cout, csem)
        comm_before()
        _dil_fill_bias(bias)

        def prep(t, carry):
            rows = pl.ds(pl.multiple_of(t * prep_rows, prep_rows), prep_rows)
            cs, sn = cos_ref[rows, :], sin_ref[rows, :]
            qf[rows, :] = _rope(q_ref[rows, :], cs, sn) * scale
            kf[rows, :] = _rope(k_ref[rows, :], cs, sn)
            vf[rows, :] = v_ref[rows, :].astype(F32)
            dov = do_ref[rows, :].astype(F32)
            dof[rows, :] = dov
            dl[rows, :] = jnp.broadcast_to(jnp.sum(dov * o_ref[rows, :], axis=-1, keepdims=True), (prep_rows, DIL_HD))
            zero = jnp.zeros((prep_rows, DIL_HD), F32)
            dqa[rows, :] = zero
            dka[rows, :] = zero
            dva[rows, :] = zero
            return carry

        lax.fori_loop(0, s_len // prep_rows, prep, 0)

        for d in DIL_DILATIONS:
            if nblk // d == 2:
                units = DIL_GROUP // 2

                def whole(i, carry, d=d, units=units):
                    rows = [_strided(i + u * (d // units), 2 * DIL_BLOCK, d) for u in range(units)]
                    ld = [(qf[rw, :].astype(BF16), kf[rw, :].astype(BF16), vf[rw, :].astype(BF16),
                           dof[rw, :].astype(BF16)) for rw in rows]
                    both = bias[...].reshape(2 * DIL_BLOCK, 2 * DIL_BLOCK)
                    s = [_dot_nt(qb, kk) + both for qb, kk, _, _ in ld]
                    dp = [_dot_nt(dob, vv) for _, _, vv, dob in ld]
                    p = [jnp.exp(sv - lse_ref[rw, :][:, 0:1]) for sv, rw in zip(s, rows)]
                    ds = [(pv * (dpv - dl[rw, :][:, 0:1])).astype(BF16) for pv, dpv, rw in zip(p, dp, rows)]
                    pb = [pv.astype(BF16) for pv in p]
                    gq = [_dot(dsv, kk) for dsv, (_, kk, _, _) in zip(ds, ld)]
                    gk = [_dot_tn(dsv, qb) for dsv, (qb, _, _, _) in zip(ds, ld)]
                    gv = [_dot_tn(pv, dob) for pv, (_, _, _, dob) in zip(pb, ld)]
                    for rw, a, b, c in zip(rows, gq, gk, gv):
                        dqa[rw, :] += a
                        dka[rw, :] += b
                        dva[rw, :] += c
                    return carry

                lax.fori_loop(0, d // units, whole, 0)
                continue

            def pair(i, carry, d=d):
                idx = [_dil_pair_block(i, half, d, nblk) for half in range(DIL_GROUP)]
                rows = [(_strided(qs, DIL_BLOCK, d), _strided(ks, 2 * DIL_BLOCK, d)) for qs, ks, _ in idx]
                ld = [(qf[qr, :].astype(BF16), kf[kr, :].astype(BF16), vf[kr, :].astype(BF16),
                       dof[qr, :].astype(BF16)) for qr, kr in rows]
                s = [_dot_nt(qb, kk) + bias[sel] for (qb, kk, _, _), (_, _, sel) in zip(ld, idx)]
                dp = [_dot_nt(dob, vv) for _, _, vv, dob in ld]
                p = [jnp.exp(sv - lse_ref[qr, :][:, 0:1]) for sv, (qr, _) in zip(s, rows)]
                ds = [(pv * (dpv - dl[qr, :][:, 0:1])).astype(BF16) for pv, dpv, (qr, _) in zip(p, dp, rows)]
                pb = [pv.astype(BF16) for pv in p]
                gq = [_dot(dsv, kk) for dsv, (_, kk, _, _) in zip(ds, ld)]
                gk = [_dot_tn(dsv, qb) for dsv, (qb, _, _, _) in zip(ds, ld)]
                gv = [_dot_tn(pv, dob) for pv, (_, _, _, dob) in zip(pb, ld)]
                for (qr, kr), a, b, c in zip(rows, gq, gk, gv):
                    dqa[qr, :] += a
                    dka[kr, :] += b
                    dva[kr, :] += c
                return carry

            lax.fori_loop(0, nblk // DIL_GROUP, pair, 0)

        def fin(t, carry):
            rows = pl.ds(pl.multiple_of(t * prep_rows, prep_rows), prep_rows)
            cs, sn = cos_ref[rows, :], sin_ref[rows, :]
            gq, gk = dqa[rows, :] * scale, dka[rows, :]
            dq_ref[rows, :] = (gq * cs - pltpu.roll(gq, DIL_HD // 2, 1) * sn).astype(BF16)
            dk_ref[rows, :] = (gk * cs - pltpu.roll(gk, DIL_HD // 2, 1) * sn).astype(BF16)
            dv_ref[rows, :] = dva[rows, :].astype(BF16)
            return carry

        lax.fori_loop(0, s_len // prep_rows, fin, 0)
        comm_after()

    head = lambda base: pl.BlockSpec((s_len, DIL_HD), lambda h: (0, base // DIL_HD + h))
    table = pl.BlockSpec((s_len, DIL_HD), lambda h: (0, 0))
    out = pl.BlockSpec((s_len, DIL_HD), lambda h: (0, h))
    shp = jax.ShapeDtypeStruct((s_len, DIL_HEADS * DIL_HD), BF16)
    return pl.pallas_call(
        body, name="dil_bwd_comm" if comm else "dil_bwd", grid=(DIL_HEADS,),
        out_shape=[shp, shp, shp] + (_comm_out_shapes(*comm) if comm else []),
        in_specs=[head(COL_QB), head(COL_KB), head(COL_VB - NP_F32), table, table,
                  pl.BlockSpec((s_len, DIL_HD), lambda h: (0, DIL_HEADS + h)), out, out] + [ANY] * nc,
        out_specs=[out, out, out] + [ANY] * len(_comm_plumbing(comm)[1]),
        scratch_shapes=[pltpu.VMEM((s_len, DIL_HD), F32) for _ in range(8)]
        + [pltpu.VMEM((2, DIL_BLOCK, 2 * DIL_BLOCK), F32)] + (_comm_scratch(nc) if comm else []),
        compiler_params=_params(("arbitrary",), 56),
    )(pf, pf, pb, cos, sin_signed, do, o_b, lse, *(comm[1] if comm else []))


_PIECES = ((COL_Z, 1024), (COL_QA, 256), (COL_KA, 256), (COL_QB, 512), (COL_KB, 512), (COL_VA, 512), (COL_VB, 512),
           (COL_LR, 128))


def _in_bwd(pieces, w_new, x, dxo, g_pre, scale, comm=None, ts=256):
    s_len = x.shape[0]
    nc = len(comm[1]) if comm else 0
    nco = len(_comm_out_shapes(*comm)) if comm else 0
    npc = len(_PIECES)

    def body(*refs):
        p_refs = refs[:npc]
        w_ref, x_ref, dxo_ref, g_ref, sc_ref = refs[npc:npc + 5]
        cin, (dx_ref, sums_ref), cout = (refs[npc + 5:npc + 5 + nc], refs[npc + 5 + nc:npc + 7 + nc],
                                         refs[npc + 7 + nc:npc + 7 + nc + nco])
        comm_before, comm_after = _comm_hooks(comm, cin, cout, refs[npc + 7 + nc + nco:], steps=s_len // ts)
        comm_before()

        @pl.when(pl.program_id(0) == 0)
        def _():
            sums_ref[...] = jnp.zeros_like(sums_ref)

        dh = jnp.zeros((ts, D_MODEL), F32)
        for p_ref, (col, width) in zip(p_refs, _PIECES):
            dh += _dot_nt(p_ref[...], w_ref[:, col:col + width])
        xv = x_ref[...]
        rstd = lax.rsqrt(jnp.mean(xv * xv, axis=-1, keepdims=True) + EPS)
        xn = xv * rstd
        sums_ref[0:1, :] += jnp.sum(dh, axis=0, keepdims=True)
        sums_ref[1:2, :] += jnp.sum(dh * (xn * g_ref[...]), axis=0, keepdims=True)
        dr = dh * (1.0 + sc_ref[...])
        sums_ref[2:3, :] += jnp.sum(dr * xn, axis=0, keepdims=True)
        dxn = dr * g_ref[...]
        dx_ref[...] = dxo_ref[...] + rstd * (dxn - xn * jnp.mean(dxn * xn, axis=-1, keepdims=True))
        comm_after()

    (g_pre, g_spec), (scale, sc_spec) = _rowvec(g_pre), _rowvec(scale)
    tile = pl.BlockSpec((ts, D_MODEL), lambda i: (i, 0))
    return pl.pallas_call(
        body, name="in_bwd_comm" if comm else "in_bwd", grid=(s_len // ts,),
        out_shape=[jax.ShapeDtypeStruct((s_len, D_MODEL), F32), jax.ShapeDtypeStruct((8, D_MODEL), F32)]
        + (_comm_out_shapes(*comm) if comm else []),
        in_specs=[pl.BlockSpec((ts, width), lambda i: (i, 0)) for _, width in _PIECES]
        + [pl.BlockSpec((D_MODEL, NP), lambda i: (0, 0)), tile, tile, g_spec, sc_spec] + [ANY] * nc,
        out_specs=[tile, pl.BlockSpec((8, D_MODEL), lambda i: (0, 0))] + [ANY] * nco,
        scratch_shapes=_comm_scratch(nc) if comm else [],
        compiler_params=_params(("arbitrary",), 56),
    )(*pieces, w_new, x, dxo, g_pre, scale, *(comm[1] if comm else []))


def _w_in_to_kernel(gathered, comm=None, tr=128):
    n_cin, c_shapes, c_scratch = _comm_plumbing(comm)

    def body(*refs):
        (g_ref,), (o_ref,), _, cin, cout, csem = _split_refs(refs, 1, 1, 0, comm)
        comm_before, comm_after = _comm_hooks(comm, cin, cout, csem, steps=D_MODEL // tr)
        comm_before()
        cols = jnp.concatenate([g_ref[k].astype(F32) for k in range(N_DEV)], axis=1)
        pad = jnp.zeros((tr, LANE - GLA_LOWRANK), F32)
        o_ref[...] = jnp.concatenate(
            [cols[:, 1024:1536], cols[:, 3088:3600], cols[:, 0:512], cols[:, 1552:2576], cols[:, 512:1024],
             cols[:, 2576:3088], cols[:, 1536:1552], pad], axis=1).astype(BF16)
        comm_after()

    return pl.pallas_call(
        body, name="w_in_to_kernel_comm" if comm else "w_in_to_kernel", grid=(D_MODEL // tr,),
        out_shape=[jax.ShapeDtypeStruct((D_MODEL, NP), BF16)] + c_shapes,
        in_specs=[pl.BlockSpec((N_DEV, tr, W_IN_SHARD), lambda i: (0, i, 0))] + [ANY] * n_cin,
        out_specs=[pl.BlockSpec((tr, NP), lambda i: (i, 0))] + [ANY] * len(c_shapes),
        scratch_shapes=c_scratch,
        compiler_params=_params(("arbitrary",)),
    )(gathered, *(comm[1] if comm else []))


def _grad_w_in(h, pieces, ts=512, tr=128):
    s_len = h.shape[0]
    steps = s_len // ts

    def body(*refs):
        h_ref, p_refs = refs[0], refs[1:1 + len(_PIECES)]
        o_ref, acc = refs[1 + len(_PIECES):]

        @pl.when(pl.program_id(0) == 0)
        def _():
            acc[...] = jnp.zeros_like(acc)

        hv = h_ref[...]
        for p_ref, (col, width) in zip(p_refs, _PIECES):
            acc[:, col:col + width] += _dot_tn(hv, p_ref[...])

        @pl.when(pl.program_id(0) == steps - 1)
        def _():
            def rows_out(t, carry):
                rows = pl.ds(pl.multiple_of(t * tr, tr), tr)
                g = acc[rows, :]
                cols = jnp.concatenate(
                    [g[:, COL_QA:COL_QB], g[:, COL_VA:COL_VB], g[:, 0:512], g[:, COL_LR:COL_LR + GLA_LOWRANK],
                     g[:, COL_QB:COL_VA], g[:, COL_VB:COL_LR], g[:, 512:1024]], axis=1)
                for k in range(N_DEV):
                    o_ref[k, rows, :] = cols[:, W_IN_SHARD * k:W_IN_SHARD * (k + 1)].astype(BF16)
                return carry

            lax.fori_loop(0, D_MODEL // tr, rows_out, 0)

    return pl.pallas_call(
        body, name="grad_w_in", grid=(steps,),
        out_shape=jax.ShapeDtypeStruct((N_DEV, D_MODEL, W_IN_SHARD), BF16),
        in_specs=[pl.BlockSpec((ts, D_MODEL), lambda i: (i, 0))]
        + [pl.BlockSpec((ts, width), lambda i: (i, 0)) for _, width in _PIECES],
        out_specs=pl.BlockSpec((N_DEV, D_MODEL, W_IN_SHARD), lambda i: (0, 0, 0)),
        scratch_shapes=[pltpu.VMEM((D_MODEL, NP), F32)],
        compiler_params=_params(("arbitrary",), 56),
    )(h, *pieces)


def _matmul_tn(a, b, name, bn, ts=512):
    s_len, m = a.shape
    n = b.shape[1]
    steps = s_len // ts

    def body(a_ref, b_ref, o_ref, acc):
        @pl.when(pl.program_id(1) == 0)
        def _():
            acc[...] = jnp.zeros_like(acc)

        acc[...] += _dot_tn(a_ref[...], b_ref[...])

        @pl.when(pl.program_id(1) == steps - 1)
        def _():
            o_ref[...] = acc[...].astype(BF16)

    return pl.pallas_call(
        body, name=name, grid=(n // bn, steps),
        out_shape=jax.ShapeDtypeStruct((m, n), BF16),
        in_specs=[pl.BlockSpec((ts, m), lambda j, i: (i, 0)), pl.BlockSpec((ts, bn), lambda j, i: (i, j))],
        out_specs=pl.BlockSpec((m, bn), lambda j, i: (0, j)),
        scratch_shapes=[pltpu.VMEM((m, bn), F32)],
        compiler_params=_params(("arbitrary", "arbitrary"), 40),
    )(a, b)


def _adam_math(w, g, m, v):
    m = ADAM_B1 * m + (1.0 - ADAM_B1) * g
    v = ADAM_B2 * v + (1.0 - ADAM_B2) * (g * g)
    m_hat = m / (1.0 - ADAM_B1 ** ADAM_STEP)
    v_hat = v / (1.0 - ADAM_B2 ** ADAM_STEP)
    delta = -ADAM_LR * (m_hat / (jnp.sqrt(v_hat) + ADAM_EPS) + ADAM_WD * w)
    return delta, m, v


def _adamw(w, parts, m, v, name, tr):
    r, cdim = w.shape
    n_parts = parts.shape[0]

    def body(w_ref, p_ref, m_ref, v_ref, g_ref, d_ref, nm_ref, nv_ref):
        g = p_ref[0].astype(F32)
        for k in range(1, n_parts):
            g = g + p_ref[k].astype(F32)
        g_ref[...] = g
        d_ref[...], nm_ref[...], nv_ref[...] = _adam_math(w_ref[...], g, m_ref[...], v_ref[...])

    tile = pl.BlockSpec((tr, cdim), lambda i: (i, 0))
    shp = jax.ShapeDtypeStruct((r, cdim), F32)
    return pl.pallas_call(
        body, name=name, grid=(r // tr,), out_shape=(shp, shp, shp, shp),
        in_specs=[tile, pl.BlockSpec((n_parts, tr, cdim), lambda i: (0, i, 0)), tile, tile],
        out_specs=(tile, tile, tile, tile),
        compiler_params=_params(("arbitrary",), 40),
    )(w, parts, m, v)


def _adamw_layers(w, parts, m, v, name, tr):
    n_layers, r, cdim = w.shape

    def body(*refs):
        w_ref, p_refs, (m_ref, v_ref) = refs[0], refs[1:1 + n_layers], refs[1 + n_layers:3 + n_layers]
        g_ref, d_ref, nm_ref, nv_ref = refs[3 + n_layers:]
        for l, p_ref in enumerate(p_refs):
            @pl.when(pl.program_id(0) == l)
            def _(p_ref=p_ref):
                g = p_ref[0].astype(F32)
                for k in range(1, p_ref.shape[0]):
                    g = g + p_ref[k].astype(F32)
                g_ref[0] = g
                d_ref[0], nm_ref[0], nv_ref[0] = _adam_math(w_ref[0], g, m_ref[0], v_ref[0])

    tile = pl.BlockSpec((1, tr, cdim), lambda l, i: (l, i, 0))
    part = lambda own: pl.BlockSpec((parts[own].shape[0], tr, cdim), lambda l, i: (0, jnp.where(l == own, i, 0), 0))
    shp = jax.ShapeDtypeStruct(w.shape, F32)
    return pl.pallas_call(
        body, name=name, grid=(n_layers, r // tr), out_shape=(shp, shp, shp, shp),
        in_specs=[tile] + [part(l) for l in range(n_layers)] + [tile, tile],
        out_specs=(tile, tile, tile, tile),
        compiler_params=_params(("arbitrary", "arbitrary"), 40),
    )(w, *parts, m, v)


def _row(vec, width):
    vec = vec.reshape(1, -1)
    return jnp.pad(vec, ((0, 0), (0, width - vec.shape[1])))


def kernel(x, c, w_ada, b_ada, g_pre, w_in, w_gate_up, b_gate_up, g_gla, g_dil, w_out, g_post, loss_target, m_w_ada, m_b_ada, m_g_pre, m_w_in, m_w_gate_up, m_b_gate_up, m_g_gla, m_g_dil, m_w_out, m_g_post, v_w_ada, v_b_ada, v_g_pre, v_w_in, v_w_gate_up, v_b_gate_up, v_g_gla, v_g_dil, v_w_out, v_g_post):
    px, py, pc = _my_position()
    me = _linear(px, py, pc)
    xs = x[0]
    target = loss_target[0]
    s_len = xs.shape[0]
    assert s_len % (DIL_BLOCK * max(DIL_DILATIONS) * 2) == 0 and xs.shape[1] == D_MODEL

    w_in_b, w_out_b = w_in.astype(BF16), w_out.astype(BF16)
    c_rows, wgu_all, w_in_all = _comm_call(
        "gather", [jnp.pad(c, ((0, 7), (0, 0))), w_gate_up.reshape(DEPTH * GLA_LOWRANK, GU_SHARD), w_in_b[0]],
        "gather_first")
    c_all = c_rows.reshape(N_DEV, 8, D_MODEL)[:, 0]
    mod_part = _mod_fwd(c_all, w_ada)
    w_new, mod_all = _w_in_to_kernel(w_in_all.reshape(N_DEV, D_MODEL, W_IN_SHARD),
                                     comm=("gather", [mod_part.reshape(DEPTH * N_DEV, ADA_SHARD)]))
    mod_all = mod_all.reshape(N_DEV, DEPTH, N_DEV, ADA_SHARD)
    mod_mine = lax.dynamic_index_in_dim(mod_all, me, axis=2, keepdims=False)
    mod = jnp.transpose(mod_mine, (1, 0, 2)).reshape(DEPTH, 3 * D_MODEL) + b_ada
    wgu_full = jnp.transpose(wgu_all.reshape(N_DEV, DEPTH, GLA_LOWRANK, GU_SHARD), (1, 2, 0, 3)).reshape(
        DEPTH, GLA_LOWRANK, GU_COLS)
    wgu_pad = jnp.pad(wgu_full, ((0, 0), (0, LANE - GLA_LOWRANK), (0, 0))).astype(BF16)

    cos, sin_signed = _rope_tables(s_len)
    g_heads = jnp.concatenate([g_gla, g_dil], axis=1)

    saved = []
    xl = xs
    for l in range(DEPTH):
        shift, scale, gate = ((mod, l, k) for k in range(3))
        if l > 0:
            w_new = _w_in_to_kernel(w_in_all.reshape(N_DEV, D_MODEL, W_IN_SHARD))[0]
        pf, pb, h, w_out_l = _prenorm_proj(xl, (g_pre, l, 0), scale, shift, w_new, comm=("gather", [w_out_b[l]]))
        o_a, states = _gla_fwd(pf, pb, wgu_pad, b_gate_up, l)
        if l + 1 < DEPTH:
            o_b, lse, w_in_all = _dil_fwd(pf, pb, cos, sin_signed, comm=("gather", [w_in_b[l + 1]]))
        else:
            o_b, lse = _dil_fwd(pf, pb, cos, sin_signed)
        if l + 1 < DEPTH:
            x_next, y, u = _post_fwd(o_a, o_b, pf, (g_heads, l, 0), w_out_l, xl, gate, (g_post, l, 0))
        else:
            dx, y, u, loss_part = _post_fwd(o_a, o_b, pf, (g_heads, l, 0), w_out_l, xl, gate, (g_post, l, 0),
                                            target=target)
        saved.append((xl, scale, gate, w_new, w_out_l, pf, pb, h, o_a, states, o_b, lse, y, u))
        xl = x_next

    small_rows = []
    gin_slots, gin_parts, gout_parts = None, [None] * DEPTH, [None] * DEPTH
    for l in reversed(range(DEPTH)):
        x_in, scale, gate, w_new, w_out_l, pf, pb, h, o_a, states, o_b, lse, y, u = saved[l]
        du, do, dz, sums_post = _post_bwd(dx, u, gate, (g_post, l, 0), w_out_l, o_a, o_b, pf, (g_heads, l, 0))
        gout_slots = _matmul_tn(y, du, "grad_w_out", 512)
        dq_a, dk_a, dv_a, dlr2, dwgu, dbgu, arrived = _gla_bwd(pf, pb, wgu_pad, b_gate_up, l, states, do,
                                                               comm=("exchange", [gout_slots]))
        gout_parts[l] = arrived.reshape(N_DEV, OUT_SHARD, D_MODEL)
        if gin_slots is not None:
            dq_b, dk_b, dv_b, arrived, _, _ = _dil_bwd(pf, pb, cos, sin_signed, do, o_b, lse,
                                                       comm=("pairsum_exchange", [gin_slots]))
            gin_parts[l + 1] = arrived.reshape(N_DEV // 2, D_MODEL, W_IN_SHARD)
        else:
            dq_b, dk_b, dv_b = _dil_bwd(pf, pb, cos, sin_signed, do, o_b, lse)
        dlr = (dlr2[0] + dlr2[1]).astype(BF16)
        pieces = (dz, dq_a, dk_a, dq_b, dk_b, dv_a, dv_b, dlr)
        gin_slots = _grad_w_in(h, pieces).reshape(N_DEV * D_MODEL, W_IN_SHARD)
        if l == 0:
            dx, sums_in, arrived, _, _ = _in_bwd(pieces, w_new, x_in, dx, (g_pre, l, 0), scale,
                                                 comm=("pairsum_exchange", [gin_slots]))
            gin_parts[0] = arrived.reshape(N_DEV // 2, D_MODEL, W_IN_SHARD)
        else:
            dx, sums_in = _in_bwd(pieces, w_new, x_in, dx, (g_pre, l, 0), scale)
        dmod = jnp.concatenate([sums_in[0], sums_in[1], sums_post[0]])
        vecs = jnp.concatenate([sums_in[2], sums_post[1], sums_post[2], dbgu[0]])
        small_rows[0:0] = [_row(dmod, 4096), _row(vecs, 4096), _row(dwgu[:GLA_LOWRANK], 4096)]
    grad_x = dx[None]

    flat = lambda a, rows: a.reshape(rows, a.shape[-1])
    r_ada = DEPTH * D_MODEL
    g_w_in, d_w_in, nm_w_in, nv_w_in = _adamw_layers(w_in, gin_parts, m_w_in, v_w_in, "adamw_w_in", 256)
    g_w_out, d_w_out, nm_w_out, nv_w_out = _adamw_layers(w_out, gout_parts, m_w_out, v_w_out, "adamw_w_out", 128)

    small_rows += [_row(loss_part[0, 0:1], 4096), jnp.zeros((1, 4096), F32)]
    small = _all_gather(jnp.concatenate(small_rows, axis=0), "gather_small").reshape(N_DEV, 8, 4096)
    dmod_all = jnp.stack([small[:, 0, :3 * D_MODEL], small[:, 3, :3 * D_MODEL]])
    dmod_cols = lax.dynamic_slice_in_dim(dmod_all, me * ADA_SHARD, ADA_SHARD, axis=2)
    gwa = _w_ada_grad(c_all, dmod_cols).reshape(1, r_ada, ADA_SHARD)
    g_w_ada, d_w_ada, nm_w_ada, nv_w_ada = (
        t.reshape(w_ada.shape) for t in _adamw(flat(w_ada, r_ada), gwa, flat(m_w_ada, r_ada), flat(v_w_ada, r_ada),
                                               "adamw_w_ada", 256))

    where = ((0, 0), (1, 0), (1, 1024), (1, 2048), (1, 2560), (1, 3072))
    replicated = [(b_ada, m_b_ada, v_b_ada), (g_pre, m_g_pre, v_g_pre), (g_post, m_g_post, v_g_post),
                  (g_gla, m_g_gla, v_g_gla), (g_dil, m_g_dil, v_g_dil), (b_gate_up, m_b_gate_up, v_b_gate_up)]
    updated, loss = _adamw_replicated(small, replicated, where, loss_at=(6, 0))
    ((g_b_ada, d_b_ada, nm_b_ada, nv_b_ada), (g_g_pre, d_g_pre, nm_g_pre, nv_g_pre),
     (g_g_post, d_g_post, nm_g_post, nv_g_post), (g_g_gla, d_g_gla, nm_g_gla, nv_g_gla),
     (g_g_dil, d_g_dil, nm_g_dil, nv_g_dil), (g_b_gu, d_b_gu, nm_b_gu, nv_b_gu)) = updated
    gu_parts = jnp.stack([small[:, 2], small[:, 5]], axis=1).reshape(N_DEV, DEPTH, GLA_LOWRANK, GU_COLS)
    gu_parts = lax.dynamic_slice_in_dim(gu_parts, me * GU_SHARD, GU_SHARD, axis=3).reshape(
        N_DEV, DEPTH * GLA_LOWRANK, GU_SHARD)
    r_gu = DEPTH * GLA_LOWRANK
    g_w_gu, d_w_gu, nm_w_gu, nv_w_gu = (
        t.reshape(w_gate_up.shape) for t in _adamw(flat(w_gate_up, r_gu), gu_parts, flat(m_w_gate_up, r_gu),
                                                   flat(v_w_gate_up, r_gu), "adamw_w_gate_up", r_gu))
    return (loss, grad_x,
            g_w_ada, g_b_ada, g_g_pre, g_w_in, g_w_gu, g_b_gu, g_g_gla, g_g_dil, g_w_out, g_g_post,
            d_w_ada, d_b_ada, d_g_pre, d_w_in, d_w_gu, d_b_gu, d_g_gla, d_g_dil, d_w_out, d_g_post,
            nm_w_ada, nm_b_ada, nm_g_pre, nm_w_in, nm_w_gu, nm_b_gu, nm_g_gla, nm_g_dil, nm_w_out, nm_g_post,
            nv_w_ada, nv_b_ada, nv_g_pre, nv_w_in, nv_w_gu, nv_b_gu, nv_g_gla, nv_g_dil, nv_w_out, nv_g_post)


def _adamw_replicated(small, params, where, loss_at):
    n_parts = small.shape[0]

    def body(*refs):
        s_ref, p_refs, o_refs = refs[0], refs[1:1 + 3 * len(params)], refs[1 + 3 * len(params):]
        total = s_ref[0]
        for k in range(1, n_parts):
            total = total + s_ref[k]
        for i, (row, col) in enumerate(where):
            w_ref, m_ref, v_ref = p_refs[3 * i:3 * i + 3]
            n = w_ref.shape[1]
            g = jnp.concatenate([total[row + 3 * l:row + 3 * l + 1, col:col + n] for l in range(DEPTH)], axis=0)
            o_refs[4 * i][...] = g
            o_refs[4 * i + 1][...], o_refs[4 * i + 2][...], o_refs[4 * i + 3][...] = _adam_math(
                w_ref[...], g, m_ref[...], v_ref[...])
        o_refs[-1][...] = jnp.broadcast_to(total[loss_at[0]:loss_at[0] + 1, loss_at[1]:loss_at[1] + 1], (8, LANE))

    flat = [a for p in params for a in p]
    shapes = [jax.ShapeDtypeStruct(p[0].shape, F32) for p in params for _ in range(4)]
    outs = pl.pallas_call(body, name="adamw_replicated",
                          out_shape=shapes + [jax.ShapeDtypeStruct((8, LANE), F32)])(small, *flat)
    return [tuple(outs[4 * i:4 * i + 4]) for i in range(len(params))], outs[-1][0, 0]
```

```python
import functools
import math

import jax
import jax.numpy as jnp
from jax import lax
from jax.experimental import pallas as pl
from jax.experimental.pallas import tpu as pltpu

F32 = jnp.float32
BF16 = jnp.bfloat16

N_DEV = 8
D_MODEL = 1024
DEPTH = 2
GLA_HEADS = 4
GLA_DK = 64
GLA_DV = 128
GLA_CHUNK = 64
GLA_TAU = 16.0
GLA_LOWRANK = 16
DIL_HEADS = 4
DIL_HD = 128
DIL_BLOCK = 128
DIL_DILATIONS = (1, 4, 16)
ROPE_THETA = 10000.0
EPS = 1e-6
IN_COLS = 3600
W_IN_SHARD = IN_COLS // N_DEV
ADA_SHARD = 3 * D_MODEL // N_DEV
OUT_SHARD = D_MODEL // N_DEV
GU_COLS = GLA_HEADS * GLA_DK
GU_SHARD = GU_COLS // N_DEV

ADAM_LR = 0.001
ADAM_B1 = 0.9
ADAM_B2 = 0.999
ADAM_EPS = 1e-08
ADAM_WD = 0.01
ADAM_STEP = 10

NP = 3712
COL_Z, COL_QA, COL_KA, COL_QB, COL_KB, COL_VA, COL_VB, COL_LR = 0, 1024, 1280, 1536, 2048, 2560, 3072, 3584
NP_F32 = COL_VA
NP_BF16 = NP - NP_F32
LANE = 128
MASK_VALUE = -1e30

MESH = pl.DeviceIdType.MESH
ANY = pl.BlockSpec(memory_space=pl.ANY)


def _params(sem=None, vmem_mb=None):
    kw = {}
    if sem is not None:
        kw["dimension_semantics"] = sem
    if vmem_mb is not None:
        kw["vmem_limit_bytes"] = vmem_mb * 1024 * 1024
    return pltpu.CompilerParams(**kw)


def _dot(a, b):
    return jnp.dot(a, b, preferred_element_type=F32)


def _dot_nt(a, b):
    return lax.dot_general(a, b, (((1,), (1,)), ((), ())), preferred_element_type=F32)


def _dot_tn(a, b):
    return lax.dot_general(a, b, (((0,), (0,)), ((), ())), preferred_element_type=F32)


def _sigmoid(z):
    return 1.0 / (1.0 + jnp.exp(-z))


def _log_sigmoid(z):
    return jnp.minimum(z, 0.0) - jnp.log(1.0 + jnp.exp(-jnp.abs(z)))


def _rowvec(v, width=D_MODEL):
    arr, row, cb = v
    return arr.reshape(arr.shape[0], 1, arr.shape[1]), pl.BlockSpec((None, 1, width), lambda *_: (row, 0, cb))


def _my_position():
    return lax.axis_index("x"), lax.axis_index("y"), lax.axis_index("c")


def _linear(px, py, pc):
    return 4 * px + 2 * py + pc


def _gather_phase(phase, x_ref, out_ref, send_sem, recv_sem, local_sem):
    m = x_ref.shape[0]
    x, y, c = _my_position()
    me, sibling = (x, y, c), (x, y, 1 - c)
    chips = [(1 - x, y), (x, 1 - y), (1 - x, 1 - y)]

    def rows(px, py, pc):
        return out_ref.at[pl.ds(_linear(px, py, pc) * m, m), :]

    def copy(k, block, to, src=None):
        return pltpu.make_async_remote_copy(
            src_ref=rows(*block) if src is None else src, dst_ref=rows(*block),
            send_sem=send_sem(k), recv_sem=recv_sem(k), device_id=to, device_id_type=MESH)

    mine = pltpu.make_async_copy(x_ref, rows(*me), local_sem)
    first = [copy(0, me, sibling, src=x_ref)] + [copy(1 + j, me, (*chip, c), src=x_ref) for j, chip in enumerate(chips)]
    passed = [copy(4 + j, (*chip, c), sibling) for j, chip in enumerate(chips)]
    if phase == "start":
        mine.start()
        for cp in first:
            cp.start()
    elif phase == "forward":
        for j, chip in enumerate(chips):
            copy(1 + j, (*chip, c), me).wait_recv()
            passed[j].start()
    else:
        copy(0, sibling, me).wait_recv()
        for j, chip in enumerate(chips):
            copy(4 + j, (*chip, 1 - c), me).wait_recv()
        for cp in first + passed:
            cp.wait_send()
        mine.wait()


def _exchange_phase(phase, x_ref, out_ref, send_sem, recv_sem, local_sem):
    m = x_ref.shape[0] // N_DEV
    x, y, c = _my_position()
    me = _linear(x, y, c)

    def rows(ref, idx):
        return ref.at[pl.ds(idx * m, m), :]

    peers = [(1 - x if j & 4 else x, 1 - y if j & 2 else y, 1 - c if j & 1 else c) for j in range(1, N_DEV)]
    local = pltpu.make_async_copy(rows(x_ref, me), rows(out_ref, me), local_sem)
    sends = [pltpu.make_async_remote_copy(
        src_ref=rows(x_ref, _linear(*peer)), dst_ref=rows(out_ref, me),
        send_sem=send_sem(j), recv_sem=recv_sem(j), device_id=peer, device_id_type=MESH) for j, peer in enumerate(peers)]
    if phase == "start":
        local.start()
        for cp in sends:
            cp.start()
    else:
        for j, peer in enumerate(peers):
            pltpu.make_async_remote_copy(
                src_ref=rows(x_ref, _linear(*peer)), dst_ref=rows(out_ref, _linear(*peer)),
                send_sem=send_sem(j), recv_sem=recv_sem(j), device_id=peer, device_id_type=MESH).wait_recv()
        for cp in sends:
            cp.wait_send()
        local.wait()


def _pairsum_exchange_phase(phase, x_ref, out_refs, send_sem, recv_sem, local_sem):
    out_ref, stage_ref, pair_ref = out_refs
    m, n = x_ref.shape[0] // N_DEV, x_ref.shape[1]
    x, y, c = _my_position()
    mine = 2 * x + y
    chips = [(qx, qy) for qx in range(2) for qy in range(2)]
    others = [(1 - x, y), (x, 1 - y), (1 - x, 1 - y)]

    def rows(ref, idx):
        return ref.at[pl.ds(idx * m, m), :]

    def remote(src, dst, k, to):
        return pltpu.make_async_remote_copy(src_ref=src, dst_ref=dst, send_sem=send_sem(k), recv_sem=recv_sem(k),
                                            device_id=to, device_id_type=MESH)

    to_sibling = [remote(rows(x_ref, _linear(qx, qy, 1 - c)), rows(stage_ref, q), q, (x, y, 1 - c))
                  for q, (qx, qy) in enumerate(chips)]
    to_chips = [remote(rows(pair_ref, 2 * qx + qy), rows(out_ref, mine), 4 + j, (qx, qy, c))
                for j, (qx, qy) in enumerate(others)]
    keep = pltpu.make_async_copy(rows(pair_ref, mine), rows(out_ref, mine), local_sem)
    if phase == "start":
        for cp in to_sibling:
            cp.start()
    elif phase == "reduce":
        for cp in to_sibling:
            cp.wait_recv()

        def through_vmem(a_buf, b_buf, sems):
            tr = 128
            loads = [(pltpu.make_async_copy(rows(x_ref, _linear(qx, qy, c)), a_buf.at[q % 2], sems.at[q % 2]),
                      pltpu.make_async_copy(rows(stage_ref, q), b_buf.at[q % 2], sems.at[2 + q % 2]))
                     for q, (qx, qy) in enumerate(chips)]
            stores = [pltpu.make_async_copy(a_buf.at[q % 2], rows(pair_ref, q), sems.at[4 + q % 2]) for q in range(4)]
            for cp in loads[0]:
                cp.start()
            for q in range(4):
                for cp in loads[q]:
                    cp.wait()
                if q + 1 < 4:
                    if q >= 1:
                        stores[q - 1].wait()
                    for cp in loads[q + 1]:
                        cp.start()

                def add(r, carry, q=q):
                    tile = pl.ds(pl.multiple_of(r * tr, tr), tr)
                    a_buf[q % 2, tile, :] = (a_buf[q % 2, tile, :].astype(F32)
                                             + b_buf[q % 2, tile, :].astype(F32)).astype(x_ref.dtype)
                    return carry

                lax.fori_loop(0, m // tr, add, 0)
                stores[q].start()
            stores[2].wait()
            stores[3].wait()

        pl.run_scoped(through_vmem, pltpu.VMEM((2, m, n), x_ref.dtype), pltpu.VMEM((2, m, n), x_ref.dtype),
                      pltpu.SemaphoreType.DMA((6,)))
    elif phase == "send":
        keep.start()
        for cp in to_chips:
            cp.start()
    else:
        for j, (qx, qy) in enumerate(others):
            remote(rows(pair_ref, mine), rows(out_ref, 2 * qx + qy), 4 + j, (qx, qy, c)).wait_recv()
        for cp in to_sibling + to_chips:
            cp.wait_send()
        keep.wait()


_COMM_PHASES = {"gather": (_gather_phase, ("start", "forward", "finish")),
                "exchange": (_exchange_phase, ("start", "finish")),
                "pairsum_exchange": (_pairsum_exchange_phase, ("start", "reduce", "send", "finish"))}


def _comm_scratch(n_arrays):
    return [pltpu.SemaphoreType.DMA((n_arrays, 7)), pltpu.SemaphoreType.DMA((n_arrays, 7)),
            pltpu.SemaphoreType.DMA((n_arrays,))]


def _comm_run(kind, phases, x_refs, out_refs, send_sems, recv_sems, local_sems):
    fn = _COMM_PHASES[kind][0]
    per = len(out_refs) // len(x_refs)
    for phase in phases:
        for a, x_ref in enumerate(x_refs):
            outs = out_refs[a] if per == 1 else tuple(out_refs[per * a:per * (a + 1)])
            fn(phase, x_ref, outs, lambda k, a=a: send_sems.at[a, k], lambda k, a=a: recv_sems.at[a, k],
               local_sems.at[a])


def _comm_out_shapes(kind, arrays):
    if kind == "pairsum_exchange":
        return [jax.ShapeDtypeStruct((a.shape[0] // 2, a.shape[1]), a.dtype) for a in arrays for _ in range(3)]
    return [jax.ShapeDtypeStruct((N_DEV * a.shape[0], a.shape[1]) if kind == "gather" else a.shape, a.dtype)
            for a in arrays]


def _comm_call(kind, arrays, name):
    n = len(arrays)
    shapes = _comm_out_shapes(kind, arrays)

    def body(*refs):
        _comm_run(kind, _COMM_PHASES[kind][1], refs[:n], refs[n:n + len(shapes)], *refs[n + len(shapes):])

    return pl.pallas_call(body, name=name, out_shape=shapes, in_specs=[ANY] * n, out_specs=[ANY] * len(shapes),
                          scratch_shapes=_comm_scratch(n))(*arrays)


def _all_gather(xs, name):
    return _comm_call("gather", [xs], name)[0]


def _mod_fwd(c_all, w_ada):
    def body(c_ref, w_ref, o_ref):
        cv = c_ref[...]
        sc = cv * _sigmoid(cv)
        o_ref[0] = _dot(sc.astype(BF16), w_ref[0].astype(BF16))

    return pl.pallas_call(
        body, name="mod_fwd", grid=(DEPTH,),
        out_shape=jax.ShapeDtypeStruct((DEPTH, N_DEV, ADA_SHARD), F32),
        in_specs=[pl.BlockSpec((N_DEV, D_MODEL), lambda l: (0, 0)),
                  pl.BlockSpec((1, D_MODEL, ADA_SHARD), lambda l: (l, 0, 0))],
        out_specs=pl.BlockSpec((1, N_DEV, ADA_SHARD), lambda l: (l, 0, 0)),
        compiler_params=_params(("arbitrary",)),
    )(c_all, w_ada)


def _w_ada_grad(c_all, dmod_cols):
    def body(c_ref, d_ref, o_ref):
        cv = c_ref[...]
        sc = cv * _sigmoid(cv)
        o_ref[0] = lax.dot_general(sc, d_ref[0], (((0,), (0,)), ((), ())), precision=lax.Precision.HIGHEST,
                                   preferred_element_type=F32)

    return pl.pallas_call(
        body, name="w_ada_grad", grid=(DEPTH,),
        out_shape=jax.ShapeDtypeStruct((DEPTH, D_MODEL, ADA_SHARD), F32),
        in_specs=[pl.BlockSpec((N_DEV, D_MODEL), lambda l: (0, 0)),
                  pl.BlockSpec((1, N_DEV, ADA_SHARD), lambda l: (l, 0, 0))],
        out_specs=pl.BlockSpec((1, D_MODEL, ADA_SHARD), lambda l: (l, 0, 0)),
        compiler_params=_params(("arbitrary",)),
    )(c_all, dmod_cols)


def _comm_plumbing(comm):
    if not comm:
        return 0, [], []
    return len(comm[1]), _comm_out_shapes(*comm), _comm_scratch(len(comm[1]))


def _split_refs(refs, n_in, n_out, n_scratch, comm):
    ci, shapes, _ = _comm_plumbing(comm)
    co = len(shapes)
    a, b, c = n_in + ci, n_in + ci + n_out, n_in + ci + n_out + co
    return refs[:n_in], refs[a:b], refs[c:c + n_scratch], refs[n_in:a], refs[b:c], refs[c + n_scratch:]


def _prenorm_proj(x, g_pre, scale, shift, w_new, comm=None, ts=256):
    s_len = x.shape[0]
    n_cin, c_shapes, c_scratch = _comm_plumbing(comm)

    def body(*refs):
        (x_ref, g_ref, sc_ref, sh_ref, w_ref), (pf_ref, pb_ref, h_ref), _, cin, cout, csem = _split_refs(
            refs, 5, 3, 0, comm)
        comm_before, comm_after = _comm_hooks(comm, cin, cout, csem, steps=s_len // ts)
        comm_before()
        xv = x_ref[...]
        rstd = lax.rsqrt(jnp.mean(xv * xv, axis=-1, keepdims=True) + EPS)
        h = (xv * rstd * g_ref[...]) * (1.0 + sc_ref[...]) + sh_ref[...]
        hb = h.astype(BF16)
        h_ref[...] = hb
        for j in range(0, NP, 512):
            w = min(512, NP - j)
            acc = _dot(hb, w_ref[:, j:j + w])
            if j < NP_F32:
                pf_ref[:, j:j + w] = acc
            else:
                pb_ref[:, j - NP_F32:j - NP_F32 + w] = acc.astype(BF16)
        comm_after()

    (g_pre, g_spec), (scale, sc_spec), (shift, sh_spec) = _rowvec(g_pre), _rowvec(scale), _rowvec(shift)
    return pl.pallas_call(
        body, name="prenorm_proj_comm" if comm else "prenorm_proj", grid=(s_len // ts,),
        out_shape=[jax.ShapeDtypeStruct((s_len, NP_F32), F32), jax.ShapeDtypeStruct((s_len, NP_BF16), BF16),
                   jax.ShapeDtypeStruct((s_len, D_MODEL), BF16)] + c_shapes,
        in_specs=[pl.BlockSpec((ts, D_MODEL), lambda i: (i, 0)), g_spec, sc_spec, sh_spec,
                  pl.BlockSpec((D_MODEL, NP), lambda i: (0, 0))] + [ANY] * n_cin,
        out_specs=[pl.BlockSpec((ts, NP_F32), lambda i: (i, 0)), pl.BlockSpec((ts, NP_BF16), lambda i: (i, 0)),
                   pl.BlockSpec((ts, D_MODEL), lambda i: (i, 0))] + [ANY] * len(c_shapes),
        scratch_shapes=c_scratch,
        compiler_params=_params(("arbitrary",), 48),
    )(x, g_pre, scale, shift, w_new, *(comm[1] if comm else []))


GLA_GROUP = 16


def _gla_group_rows(t):
    return [pl.ds(pl.multiple_of((t * GLA_GROUP + j) * GLA_CHUNK, GLA_CHUNK), GLA_CHUNK) for j in range(GLA_GROUP)]


def _gla_chunks_common(q_ref, k_ref, lr_ref, wgu_ref, bgu_ref, rows_list):
    c = GLA_CHUNK
    ri = lax.broadcasted_iota(jnp.int32, (c, c), 0)
    ci = lax.broadcasted_iota(jnp.int32, (c, c), 1)
    tril = (ri >= ci).astype(F32)
    zs = [_dot(lr_ref[rows, :], wgu_ref[...]) + bgu_ref[...] for rows in rows_list]
    las = [_log_sigmoid(z) * (1.0 / GLA_TAU) for z in zs]
    bs = [jnp.dot(tril, la, precision=lax.Precision.HIGHEST, preferred_element_type=F32) for la in las]
    out = []
    for rows, z, b in zip(rows_list, zs, bs):
        q = q_ref[rows, :] * (GLA_DK ** -0.5)
        k = k_ref[rows, :]
        bl = b[c - 1:c, :]
        out.append(dict(z=z, b=b, bl=bl, qe=q * jnp.exp(b), ke=k * jnp.exp(-b), kend=k * jnp.exp(bl - b),
                        dec=jnp.exp(bl)))
    return out, ri, ci


def _head_lane_mask(hh):
    return (lax.broadcasted_iota(jnp.int32, (1, LANE), 1) // GLA_DK) == hh


def _state_block_mask():
    r = lax.broadcasted_iota(jnp.int32, (2 * GLA_DV, LANE), 0) // GLA_DV
    cc = lax.broadcasted_iota(jnp.int32, (2 * GLA_DV, LANE), 1) // GLA_DK
    return r == cc


def _gla_fwd(pf, pb, wgu, bgu, layer, comm=None):
    s_len = pf.shape[0]
    nc = s_len // GLA_CHUNK
    ncomm = len(comm[1]) if comm else 0

    def body(*refs):
        q_ref, k_ref, v_ref, lr_ref, wgu_ref, bgu_ref = refs[:6]
        cin, (o_ref, st_ref), cout = refs[6:6 + ncomm], refs[6 + ncomm:8 + ncomm], refs[8 + ncomm:8 + 2 * ncomm]
        qe_s, cs_s, dec_s = refs[8 + 2 * ncomm:11 + 2 * ncomm]
        comm_before, comm_after = _comm_hooks(comm, cin, cout, refs[11 + 2 * ncomm:], steps=2)
        comm_before()
        bd = _state_block_mask()

        def local(t, carry):
            rows_list = _gla_group_rows(t)
            cm, ri, ci = _gla_chunks_common(q_ref, k_ref, lr_ref, wgu_ref, bgu_ref, rows_list)
            vs = [v_ref[rows, :] for rows in rows_list]
            kebs = [c["ke"].astype(BF16) for c in cm]
            a = [[jnp.where(ri >= ci, _dot_nt(jnp.where(_head_lane_mask(hh), c["qe"], 0.0).astype(BF16), keb), 0.0)
                  .astype(BF16) for hh in range(2)] for c, keb in zip(cm, kebs)]
            oi = [[_dot(ah[hh], v[:, hh * GLA_DV:(hh + 1) * GLA_DV]) for hh in range(2)] for ah, v in zip(a, vs)]
            cs = [jnp.where(bd, _dot_tn(v, c["kend"].astype(BF16)), 0.0) for c, v in zip(cm, vs)]
            for j, (rows, c) in enumerate(zip(rows_list, cm)):
                n = t * GLA_GROUP + j
                o_ref[rows, :] = jnp.concatenate(oi[j], axis=1)
                qe_s[rows, :] = c["qe"].astype(BF16)
                cs_s[n] = cs[j]
                dec_s[n] = jnp.broadcast_to(c["dec"], (8, LANE))
            return carry

        lax.fori_loop(0, nc // GLA_GROUP, local, 0)

        def scan(n, st):
            st_ref[0, n] = st.astype(BF16)
            return dec_s[n][0:1, :] * st + cs_s[n]

        lax.fori_loop(0, nc, scan, jnp.zeros((2 * GLA_DV, LANE), F32))

        def inter(t, carry):
            rows_list = _gla_group_rows(t)
            add = [_dot_nt(qe_s[rows, :], st_ref[0, t * GLA_GROUP + j]) for j, rows in enumerate(rows_list)]
            for rows, av in zip(rows_list, add):
                o_ref[rows, :] = o_ref[rows, :] + av
            return carry

        lax.fori_loop(0, nc // GLA_GROUP, inter, 0)
        comm_after()

    return pl.pallas_call(
        body, name="gla_fwd_comm" if comm else "gla_fwd", grid=(2,),
        out_shape=[jax.ShapeDtypeStruct((s_len, GLA_HEADS * GLA_DV), F32),
                   jax.ShapeDtypeStruct((2, nc, 2 * GLA_DV, LANE), BF16)] + (_comm_out_shapes(*comm) if comm else []),
        in_specs=[pl.BlockSpec((s_len, LANE), lambda g: (0, COL_QA // LANE + g)),
                  pl.BlockSpec((s_len, LANE), lambda g: (0, COL_KA // LANE + g)),
                  pl.BlockSpec((s_len, 2 * GLA_DV), lambda g: (0, (COL_VA - NP_F32) // (2 * GLA_DV) + g)),
                  pl.BlockSpec((s_len, LANE), lambda g: (0, (COL_LR - NP_F32) // LANE)),
                  pl.BlockSpec((None, LANE, LANE), lambda g: (layer, 0, g)),
                  pl.BlockSpec((None, 1, LANE), lambda g: (layer, 0, g))] + [ANY] * ncomm,
        out_specs=[pl.BlockSpec((s_len, 2 * GLA_DV), lambda g: (0, g)),
                   pl.BlockSpec((1, nc, 2 * GLA_DV, LANE), lambda g: (g, 0, 0, 0))] + [ANY] * ncomm,
        scratch_shapes=[pltpu.VMEM((s_len, LANE), BF16), pltpu.VMEM((nc, 2 * GLA_DV, LANE), F32),
                        pltpu.VMEM((nc, 8, LANE), F32)] + (_comm_scratch(ncomm) if comm else []),
        compiler_params=_params(("arbitrary",), 56),
    )(pf, pf, pb, pb, wgu, bgu.reshape(bgu.shape[0], 1, GU_COLS), *(comm[1] if comm else []))


def _rope_tables(s_len):
    inv_freq = ROPE_THETA ** (-jnp.arange(0, DIL_HD, 2, dtype=F32) / DIL_HD)
    ang = jnp.arange(s_len, dtype=F32)[:, None] * inv_freq[None, :]
    cos, sin = jnp.cos(ang), jnp.sin(ang)
    return jnp.concatenate([cos, cos], axis=1), jnp.concatenate([-sin, sin], axis=1)


def _rope(xv, cos, sin_signed):
    return xv * cos + pltpu.roll(xv, DIL_HD // 2, 1) * sin_signed


DIL_GROUP = 8


def _dil_pair_block(i, half, d, nblk, group=DIL_GROUP):
    nb = nblk // d
    j = i + half * (nblk // group)
    if nb >= 2 * group:
        r, n = j % d, j // d
    else:
        r, n = j // nb, j % nb
    kb = jnp.maximum(n - 1, 0)
    qs = r + d * DIL_BLOCK * n
    ks = r + d * DIL_BLOCK * kb
    return qs, ks, jnp.minimum(n, 1)


def _dil_fill_bias(bias):
    qi = lax.broadcasted_iota(jnp.int32, (DIL_BLOCK, 2 * DIL_BLOCK), 0)
    kj = lax.broadcasted_iota(jnp.int32, (DIL_BLOCK, 2 * DIL_BLOCK), 1)
    for sel in range(2):
        dist = qi - kj + DIL_BLOCK * sel
        bias[sel] = jnp.where((dist >= 0) & (dist <= DIL_BLOCK), 0.0, MASK_VALUE)


def _strided(start, size, d):
    return pl.ds(start, size) if d == 1 else pl.ds(start, size, stride=d)


def _comm_hooks(comm, cin, cout, csem, steps=DIL_HEADS):
    def before():
        if comm:
            @pl.when(pl.program_id(0) == 0)
            def _():
                _comm_run(comm[0], ("start",), cin, cout, *csem)

            if comm[0] == "gather":
                @pl.when(pl.program_id(0) == steps - 1)
                def _():
                    _comm_run(comm[0], ("forward",), cin, cout, *csem)

            if comm[0] == "pairsum_exchange":
                @pl.when(pl.program_id(0) == (1 if steps <= 4 else 2))
                def _():
                    _comm_run(comm[0], ("reduce", "send"), cin, cout, *csem)

    def after():
        if comm:
            @pl.when(pl.program_id(0) == steps - 1)
            def _():
                _comm_run(comm[0], ("finish",), cin, cout, *csem)

    return before, after


def _dil_fwd(pf, pb, cos, sin_signed, comm=None):
    s_len = pf.shape[0]
    nblk = s_len // DIL_BLOCK
    prep_rows = 256
    scale = DIL_HD ** -0.5
    nc = len(comm[1]) if comm else 0

    def body(*refs):
        q_ref, k_ref, v_ref, cos_ref, sin_ref = refs[:5]
        cin, (o_ref, lse_ref), cout = refs[5:5 + nc], refs[5 + nc:7 + nc], refs[7 + nc:7 + 2 * nc]
        qf, kf, vf, o0, o1, o2, l0, l1, l2, bias = refs[7 + 2 * nc:17 + 2 * nc]
        comm_before, comm_after = _comm_hooks(comm, cin, cout, refs[17 + 2 * nc:])
        comm_before()
        _dil_fill_bias(bias)

        def prep(t, carry):
            rows = pl.ds(pl.multiple_of(t * prep_rows, prep_rows), prep_rows)
            cs, sn = cos_ref[rows, :], sin_ref[rows, :]
            qf[rows, :] = _rope(q_ref[rows, :], cs, sn)
            kf[rows, :] = _rope(k_ref[rows, :], cs, sn)
            vf[rows, :] = v_ref[rows, :].astype(F32)
            return carry

        lax.fori_loop(0, s_len // prep_rows, prep, 0)
        for d, o_p, l_p in zip(DIL_DILATIONS, (o0, o1, o2), (l0, l1, l2)):
            if nblk // d == 2:
                units = DIL_GROUP // 2

                def whole(i, carry, d=d, o_p=o_p, l_p=l_p, units=units):
                    rows = [_strided(i + u * (d // units), 2 * DIL_BLOCK, d) for u in range(units)]
                    ld = [(qf[rw, :].astype(BF16), kf[rw, :].astype(BF16), vf[rw, :].astype(BF16)) for rw in rows]
                    both = bias[...].reshape(2 * DIL_BLOCK, 2 * DIL_BLOCK)
                    s = [_dot_nt(qb, kk) * scale + both for qb, kk, _ in ld]
                    m = [jnp.max(sv, axis=-1, keepdims=True) for sv in s]
                    p = [jnp.exp(sv - mv) for sv, mv in zip(s, m)]
                    den = [jnp.sum(pv, axis=-1, keepdims=True) for pv in p]
                    r = [_dot(pv.astype(BF16), vv) for pv, (_, _, vv) in zip(p, ld)]
                    for rv, dv, mv, rw in zip(r, den, m, rows):
                        o_p[rw, :] = rv / dv
                        l_p[rw, :] = jnp.broadcast_to(mv + jnp.log(dv), (2 * DIL_BLOCK, DIL_HD))
                    return carry

                lax.fori_loop(0, d // units, whole, 0)
                continue

            def pair(i, carry, d=d, o_p=o_p, l_p=l_p):
                idx = [_dil_pair_block(i, half, d, nblk, DIL_GROUP) for half in range(DIL_GROUP)]
                ld = [(qf[_strided(qs, DIL_BLOCK, d), :].astype(BF16),
                       kf[_strided(ks, 2 * DIL_BLOCK, d), :].astype(BF16),
                       vf[_strided(ks, 2 * DIL_BLOCK, d), :].astype(BF16)) for qs, ks, _ in idx]
                s = [_dot_nt(qb, kk) * scale + bias[sel] for (qb, kk, _), (_, _, sel) in zip(ld, idx)]
                m = [jnp.max(sv, axis=-1, keepdims=True) for sv in s]
                p = [jnp.exp(sv - mv) for sv, mv in zip(s, m)]
                den = [jnp.sum(pv, axis=-1, keepdims=True) for pv in p]
                r = [_dot(pv.astype(BF16), vv) for pv, (_, _, vv) in zip(p, ld)]
                for rv, dv, mv, (qs, _, _) in zip(r, den, m, idx):
                    o_p[_strided(qs, DIL_BLOCK, d), :] = rv / dv
                    l_p[_strided(qs, DIL_BLOCK, d), :] = jnp.broadcast_to(mv + jnp.log(dv), (DIL_BLOCK, DIL_HD))
                return carry

            lax.fori_loop(0, nblk // DIL_GROUP, pair, 0)

        def comb(t, carry):
            rows = pl.ds(pl.multiple_of(t * prep_rows, prep_rows), prep_rows)
            a0, a1, a2 = l0[rows, :], l1[rows, :], l2[rows, :]
            m = jnp.maximum(jnp.maximum(a0, a1), a2)
            e0, e1, e2 = jnp.exp(a0 - m), jnp.exp(a1 - m), jnp.exp(a2 - m)
            tot = e0 + e1 + e2
            o_ref[rows, :] = (e0 * o0[rows, :] + e1 * o1[rows, :] + e2 * o2[rows, :]) / tot
            lse_ref[rows, :] = m + jnp.log(tot)
            return carry

        lax.fori_loop(0, s_len // prep_rows, comb, 0)
        comm_after()

    head = lambda base: pl.BlockSpec((s_len, DIL_HD), lambda h: (0, base // DIL_HD + h))
    table = pl.BlockSpec((s_len, DIL_HD), lambda h: (0, 0))
    out = pl.BlockSpec((s_len, DIL_HD), lambda h: (0, h))
    shp = jax.ShapeDtypeStruct((s_len, DIL_HEADS * DIL_HD), F32)
    return pl.pallas_call(
        body, name="dil_fwd_comm" if comm else "dil_fwd", grid=(DIL_HEADS,),
        out_shape=[shp, shp] + (_comm_out_shapes(*comm) if comm else []),
        in_specs=[head(COL_QB), head(COL_KB), head(COL_VB - NP_F32), table, table] + [ANY] * nc,
        out_specs=[out, out] + [ANY] * nc,
        scratch_shapes=[pltpu.VMEM((s_len, DIL_HD), F32) for _ in range(9)]
        + [pltpu.VMEM((2, DIL_BLOCK, 2 * DIL_BLOCK), F32)] + (_comm_scratch(nc) if comm else []),
        compiler_params=_params(("arbitrary",), 56),
    )(pf, pf, pb, cos, sin_signed, *(comm[1] if comm else []))


def _silu_and_grad(z):
    sg = _sigmoid(z)
    return z * sg, sg * (1.0 + z * (1.0 - sg))


def _post_fwd(o_a, o_b, pf, g_heads, w_out, x, gate, g_post, target=None, ts=256):
    s_len = x.shape[0]
    half = GLA_HEADS * GLA_DV
    last = target is not None

    def body(*refs):
        oa_ref, ob_ref, z_ref, gh_ref, w_ref, x_ref, gate_ref, gp_ref = refs[:8]
        xo_ref, y_ref, u_ref = refs[8 + last:11 + last]
        for src, base in ((oa_ref, 0), (ob_ref, half)):
            for hh in range(4):
                lo = hh * LANE
                og = src[:, lo:lo + LANE]
                on = og * lax.rsqrt(jnp.mean(og * og, axis=-1, keepdims=True) + EPS)
                zg = z_ref[:, base + lo:base + lo + LANE].astype(F32)
                y_ref[:, base + lo:base + lo + LANE] = (on * gh_ref[:, base + lo:base + lo + LANE]
                                                        * (zg * _sigmoid(zg))).astype(BF16)
        u = _dot(y_ref[...], w_ref[...])
        u_ref[...] = u.astype(BF16)
        rstd = lax.rsqrt(jnp.mean(u * u, axis=-1, keepdims=True) + EPS)
        x_out = x_ref[...] + gate_ref[...] * (u * rstd * gp_ref[...])
        if last:
            t_ref, loss_ref = refs[8], refs[12]

            @pl.when(pl.program_id(0) == 0)
            def _():
                loss_ref[...] = jnp.zeros_like(loss_ref)

            e = x_out - t_ref[...]
            xo_ref[...] = e * (1.0 / D_MODEL)
            loss_ref[...] += 0.5 * jnp.sum(jnp.mean(e * e, axis=-1, keepdims=True))
        else:
            xo_ref[...] = x_out

    (g_heads, gh_spec), (gate, gate_spec), (g_post, gp_spec) = _rowvec(g_heads), _rowvec(gate), _rowvec(g_post)
    tile = pl.BlockSpec((ts, D_MODEL), lambda i: (i, 0))
    halft = pl.BlockSpec((ts, half), lambda i: (i, 0))
    return pl.pallas_call(
        body, name="post_fwd_loss" if last else "post_fwd", grid=(s_len // ts,),
        out_shape=[jax.ShapeDtypeStruct((s_len, D_MODEL), F32), jax.ShapeDtypeStruct((s_len, D_MODEL), BF16),
                   jax.ShapeDtypeStruct((s_len, D_MODEL), BF16)]
        + ([jax.ShapeDtypeStruct((8, LANE), F32)] if last else []),
        in_specs=[halft, halft, tile, gh_spec, pl.BlockSpec((D_MODEL, D_MODEL), lambda i: (0, 0)), tile, gate_spec,
                  gp_spec] + ([tile] if last else []),
        out_specs=[tile, tile, tile] + ([pl.BlockSpec((8, LANE), lambda i: (0, 0))] if last else []),
        compiler_params=_params(("arbitrary",), 40),
    )(o_a, o_b, pf, g_heads, w_out, x, gate, g_post, *([target] if last else []))


def _post_bwd(dxo, u, gate, g_post, w_out, o_a, o_b, pf, g_heads, ts=256):
    s_len = dxo.shape[0]
    half = GLA_HEADS * GLA_DV

    def body(dx_ref, u_ref, gate_ref, gp_ref, w_ref, oa_ref, ob_ref, z_ref, gh_ref, du_ref, do_ref, dz_ref, sums_ref):
        @pl.when(pl.program_id(0) == 0)
        def _():
            sums_ref[...] = jnp.zeros_like(sums_ref)

        dx = dx_ref[...]
        u = u_ref[...].astype(F32)
        rstd = lax.rsqrt(jnp.mean(u * u, axis=-1, keepdims=True) + EPS)
        un = u * rstd
        sums_ref[0:1, :] += jnp.sum(dx * (un * gp_ref[...]), axis=0, keepdims=True)
        drn = dx * gate_ref[...]
        sums_ref[1:2, :] += jnp.sum(drn * un, axis=0, keepdims=True)
        dun = drn * gp_ref[...]
        du = rstd * (dun - un * jnp.mean(dun * un, axis=-1, keepdims=True))
        dub = du.astype(BF16)
        du_ref[...] = dub
        dy = _dot_nt(dub, w_ref[...])
        for src, base in ((oa_ref, 0), (ob_ref, half)):
            for hh in range(4):
                lo = base + hh * LANE
                og = src[:, hh * LANE:(hh + 1) * LANE]
                rs = lax.rsqrt(jnp.mean(og * og, axis=-1, keepdims=True) + EPS)
                on = og * rs
                zg = z_ref[:, lo:lo + LANE].astype(F32)
                sz, dsz = _silu_and_grad(zg)
                gg = gh_ref[:, lo:lo + LANE]
                dyg = dy[:, lo:lo + LANE]
                sums_ref[2:3, lo:lo + LANE] += jnp.sum(dyg * sz * on, axis=0, keepdims=True)
                dz_ref[:, lo:lo + LANE] = (dyg * on * gg * dsz).astype(BF16)
                don = dyg * gg * sz
                do_ref[:, lo:lo + LANE] = (rs * (don - on * jnp.mean(don * on, axis=-1, keepdims=True))).astype(BF16)

    (g_heads, gh_spec), (gate, gate_spec), (g_post, gp_spec) = _rowvec(g_heads), _rowvec(gate), _rowvec(g_post)
    tile = pl.BlockSpec((ts, D_MODEL), lambda i: (i, 0))
    halft = pl.BlockSpec((ts, half), lambda i: (i, 0))
    return pl.pallas_call(
        body, name="post_bwd", grid=(s_len // ts,),
        out_shape=(jax.ShapeDtypeStruct((s_len, D_MODEL), BF16), jax.ShapeDtypeStruct((s_len, D_MODEL), BF16),
                   jax.ShapeDtypeStruct((s_len, D_MODEL), BF16), jax.ShapeDtypeStruct((8, D_MODEL), F32)),
        in_specs=[tile, tile, gate_spec, gp_spec, pl.BlockSpec((D_MODEL, D_MODEL), lambda i: (0, 0)), halft, halft,
                  tile, gh_spec],
        out_specs=(tile, tile, tile, pl.BlockSpec((8, D_MODEL), lambda i: (0, 0))),
        compiler_params=_params(("arbitrary",), 40),
    )(dxo, u, gate, g_post, w_out, o_a, o_b, pf, g_heads)


def _gla_bwd(pf, pb, wgu, bgu, layer, states, do, comm=None):
    s_len = pf.shape[0]
    nc = s_len // GLA_CHUNK
    c = GLA_CHUNK
    n_cin, c_shapes, c_scratch = _comm_plumbing(comm)

    def body(*refs):
        ((q_ref, k_ref, v_ref, lr_ref, wgu_ref, bgu_ref, st_ref, do_ref),
         (dq_ref, dk_ref, dv_ref, dlr_ref, dwgu_ref, dbgu_ref), (ds_s, dec_s, dw_acc, db_acc),
         cin, cout, csem) = _split_refs(refs, 8, 6, 4, comm)
        comm_before, comm_after = _comm_hooks(comm, cin, cout, csem, steps=2)
        comm_before()
        dw_acc[...] = jnp.zeros_like(dw_acc)
        db_acc[...] = jnp.zeros_like(db_acc)
        bd = _state_block_mask()
        last_row = lax.broadcasted_iota(jnp.int32, (c, LANE), 0) == c - 1

        def local(t, carry):
            rows_list = _gla_group_rows(t)
            cm, _, _ = _gla_chunks_common(q_ref, k_ref, lr_ref, wgu_ref, bgu_ref, rows_list)
            loc = [jnp.where(bd, _dot_tn(do_ref[rows, :], cc["qe"].astype(BF16)), 0.0)
                   for rows, cc in zip(rows_list, cm)]
            for j, cc in enumerate(cm):
                ds_s[t * GLA_GROUP + j] = loc[j]
                dec_s[t * GLA_GROUP + j] = jnp.broadcast_to(cc["dec"], (8, LANE))
            return carry

        lax.fori_loop(0, nc // GLA_GROUP, local, 0)

        def scan(t, dst):
            n = nc - 1 - t
            loc = ds_s[n]
            ds_s[n] = dst
            return dec_s[n][0:1, :] * dst + loc

        lax.fori_loop(0, nc, scan, jnp.zeros((2 * GLA_DV, LANE), F32))

        def rest(t, carry):
            rows_list = _gla_group_rows(t)
            cm, ri, ci = _gla_chunks_common(q_ref, k_ref, lr_ref, wgu_ref, bgu_ref, rows_list)
            ns = [t * GLA_GROUP + j for j in range(GLA_GROUP)]
            vs = [v_ref[rows, :] for rows in rows_list]
            dobs = [do_ref[rows, :] for rows in rows_list]
            stbs = [st_ref[0, n] for n in ns]
            dsts = [ds_s[n] for n in ns]
            dstbs = [d.astype(BF16) for d in dsts]
            qebs = [cc["qe"].astype(BF16) for cc in cm]
            kebs = [cc["ke"].astype(BF16) for cc in cm]
            kendbs = [cc["kend"].astype(BF16) for cc in cm]
            hms = [_head_lane_mask(hh) for hh in range(2)]
            qehs = [[jnp.where(hm, cc["qe"], 0.0).astype(BF16) for hm in hms] for cc in cm]
            kehs = [[jnp.where(hm, cc["ke"], 0.0).astype(BF16) for hm in hms] for cc in cm]
            heads = lambda x: [x[:, hh * GLA_DV:(hh + 1) * GLA_DV] for hh in range(2)]
            vhs, dohs = [heads(v) for v in vs], [heads(d) for d in dobs]

            dqe0 = [_dot(dob, stb) for dob, stb in zip(dobs, stbs)]
            dkend = [_dot(v, dstb) for v, dstb in zip(vs, dstbs)]
            dv0 = [_dot_nt(kb, dstb) for kb, dstb in zip(kendbs, dstbs)]
            a_t = [[jnp.where(ci >= ri, _dot_nt(kehs[j][hh], qebs[j]), 0.0).astype(BF16) for hh in range(2)]
                   for j in range(GLA_GROUP)]
            da = [[jnp.where(ri >= ci, _dot_nt(dohs[j][hh], vhs[j][hh]), 0.0).astype(BF16) for hh in range(2)]
                  for j in range(GLA_GROUP)]
            da_t = [[jnp.where(ci >= ri, _dot_nt(vhs[j][hh], dohs[j][hh]), 0.0).astype(BF16) for hh in range(2)]
                    for j in range(GLA_GROUP)]
            dv1 = [[_dot(a_t[j][hh], dohs[j][hh]) for hh in range(2)] for j in range(GLA_GROUP)]
            dqe1 = [[_dot(da[j][hh], kebs[j]) for hh in range(2)] for j in range(GLA_GROUP)]
            dke1 = [[_dot(da_t[j][hh], qehs[j][hh]) for hh in range(2)] for j in range(GLA_GROUP)]

            dbs, dzs = [], []
            for j, (rows, cc) in enumerate(zip(rows_list, cm)):
                qe, ke, kend, b, bl = cc["qe"], cc["ke"], cc["kend"], cc["b"], cc["bl"]
                dqe = dqe0[j] + jnp.where(hms[0], dqe1[j][0], 0.0) + jnp.where(hms[1], dqe1[j][1], 0.0)
                dke = jnp.where(hms[0], dke1[j][0], 0.0) + jnp.where(hms[1], dke1[j][1], 0.0)
                dv_ref[rows, :] = (dv0[j] + jnp.concatenate(dv1[j], axis=1)).astype(BF16)
                dq_ref[rows, :] = (dqe * jnp.exp(b) * (GLA_DK ** -0.5)).astype(BF16)
                dk_ref[rows, :] = (dke * jnp.exp(-b) + dkend[j] * jnp.exp(bl - b)).astype(BF16)
                ddec = jnp.sum(dsts[j] * stbs[j].astype(F32), axis=0, keepdims=True)
                dbl = jnp.sum(dkend[j] * kend, axis=0, keepdims=True) + ddec * cc["dec"]
                dbs.append(dqe * qe - dke * ke - dkend[j] * kend + jnp.where(last_row, dbl, 0.0))
            triu = (ci >= ri).astype(F32)
            dlas = [jnp.dot(triu, db, precision=lax.Precision.HIGHEST, preferred_element_type=F32) for db in dbs]
            dzs = [dla * (1.0 / GLA_TAU) * _sigmoid(-cc["z"]) for dla, cc in zip(dlas, cm)]
            dzbs = [dz.astype(BF16) for dz in dzs]
            dlrs = [_dot_nt(dzb, wgu_ref[...]) for dzb in dzbs]
            dws = [_dot_tn(lr_ref[rows, :], dzb) for rows, dzb in zip(rows_list, dzbs)]
            for rows, dlr in zip(rows_list, dlrs):
                dlr_ref[0, rows, :] = dlr
            dw_acc[...] += functools.reduce(lambda x, y: x + y, dws)
            db_acc[0:1, :] += jnp.sum(functools.reduce(lambda x, y: x + y, dzs), axis=0, keepdims=True)
            return carry

        lax.fori_loop(0, nc // GLA_GROUP, rest, 0)
        dwgu_ref[...] = dw_acc[...]
        dbgu_ref[...] = db_acc[...]
        comm_after()

    pair = pl.BlockSpec((s_len, LANE), lambda g: (0, g))
    return pl.pallas_call(
        body, name="gla_bwd_comm" if comm else "gla_bwd", grid=(2,),
        out_shape=[jax.ShapeDtypeStruct((s_len, GU_COLS), BF16), jax.ShapeDtypeStruct((s_len, GU_COLS), BF16),
                   jax.ShapeDtypeStruct((s_len, GLA_HEADS * GLA_DV), BF16),
                   jax.ShapeDtypeStruct((2, s_len, LANE), F32),
                   jax.ShapeDtypeStruct((LANE, GU_COLS), F32), jax.ShapeDtypeStruct((8, GU_COLS), F32)] + c_shapes,
        in_specs=[pl.BlockSpec((s_len, LANE), lambda g: (0, COL_QA // LANE + g)),
                  pl.BlockSpec((s_len, LANE), lambda g: (0, COL_KA // LANE + g)),
                  pl.BlockSpec((s_len, 2 * GLA_DV), lambda g: (0, (COL_VA - NP_F32) // (2 * GLA_DV) + g)),
                  pl.BlockSpec((s_len, LANE), lambda g: (0, (COL_LR - NP_F32) // LANE)),
                  pl.BlockSpec((None, LANE, LANE), lambda g: (layer, 0, g)),
                  pl.BlockSpec((None, 1, LANE), lambda g: (layer, 0, g)),
                  pl.BlockSpec((1, nc, 2 * GLA_DV, LANE), lambda g: (g, 0, 0, 0)),
                  pl.BlockSpec((s_len, 2 * GLA_DV), lambda g: (0, g))] + [ANY] * n_cin,
        out_specs=[pair, pair, pl.BlockSpec((s_len, 2 * GLA_DV), lambda g: (0, g)),
                   pl.BlockSpec((1, s_len, LANE), lambda g: (g, 0, 0)),
                   pl.BlockSpec((LANE, LANE), lambda g: (0, g)), pl.BlockSpec((8, LANE), lambda g: (0, g))]
        + [ANY] * len(c_shapes),
        scratch_shapes=[pltpu.VMEM((nc, 2 * GLA_DV, LANE), F32), pltpu.VMEM((nc, 8, LANE), F32),
                        pltpu.VMEM((LANE, LANE), F32), pltpu.VMEM((8, LANE), F32)] + c_scratch,
        compiler_params=_params(("arbitrary",), 56),
    )(pf, pf, pb, pb, wgu, bgu.reshape(bgu.shape[0], 1, GU_COLS), states, do, *(comm[1] if comm else []))


def _dil_bwd(pf, pb, cos, sin_signed, do, o_b, lse, comm=None):
    s_len = pf.shape[0]
    nblk = s_len // DIL_BLOCK
    prep_rows = 256
    scale = DIL_HD ** -0.5
    nc = len(comm[1]) if comm else 0

    def body(*refs):
        ((q_ref, k_ref, v_ref, cos_ref, sin_ref, do_ref, o_ref, lse_ref), (dq_ref, dk_ref, dv_ref),
         (qf, kf, vf, dof, dl, dqa, dka, dva, bias), cin, cout, csem) = _split_refs(refs, 8, 3, 9, comm)
        comm_before, comm_after = _comm_hooks(comm, cin, cout, csem)
        comm_before()
        _dil_fill_bias(bias)

        def prep(t, carry):
            rows = pl.ds(pl.multiple_of(t * prep_rows, prep_rows), prep_rows)
            cs, sn = cos_ref[rows, :], sin_ref[rows, :]
            qf[rows, :] = _rope(q_ref[rows, :], cs, sn) * scale
            kf[rows, :] = _rope(k_ref[rows, :], cs, sn)
            vf[rows, :] = v_ref[rows, :].astype(F32)
            dov = do_ref[rows, :].astype(F32)
            dof[rows, :] = dov
            dl[rows, :] = jnp.broadcast_to(jnp.sum(dov * o_ref[rows, :], axis=-1, keepdims=True), (prep_rows, DIL_HD))
            zero = jnp.zeros((prep_rows, DIL_HD), F32)
            dqa[rows, :] = zero
            dka[rows, :] = zero
            dva[rows, :] = zero
            return carry

        lax.fori_loop(0, s_len // prep_rows, prep, 0)

        for d in DIL_DILATIONS:
            if nblk // d == 2:
                units = DIL_GROUP // 2

                def whole(i, carry, d=d, units=units):
                    rows = [_strided(i + u * (d // units), 2 * DIL_BLOCK, d) for u in range(units)]
                    ld = [(qf[rw, :].astype(BF16), kf[rw, :].astype(BF16), vf[rw, :].astype(BF16),
                           dof[rw, :].astype(BF16)) for rw in rows]
                    both = bias[...].reshape(2 * DIL_BLOCK, 2 * DIL_BLOCK)
                    s = [_dot_nt(qb, kk) + both for qb, kk, _, _ in ld]
                    dp = [_dot_nt(dob, vv) for _, _, vv, dob in ld]
                    p = [jnp.exp(sv - lse_ref[rw, :][:, 0:1]) for sv, rw in zip(s, rows)]
                    ds = [(pv * (dpv - dl[rw, :][:, 0:1])).astype(BF16) for pv, dpv, rw in zip(p, dp, rows)]
                    pb = [pv.astype(BF16) for pv in p]
                    gq = [_dot(dsv, kk) for dsv, (_, kk, _, _) in zip(ds, ld)]
                    gk = [_dot_tn(dsv, qb) for dsv, (qb, _, _, _) in zip(ds, ld)]
                    gv = [_dot_tn(pv, dob) for pv, (_, _, _, dob) in zip(pb, ld)]
                    for rw, a, b, c in zip(rows, gq, gk, gv):
                        dqa[rw, :] += a
                        dka[rw, :] += b
                        dva[rw, :] += c
                    return carry

                lax.fori_loop(0, d // units, whole, 0)
                continue

            def pair(i, carry, d=d):
                idx = [_dil_pair_block(i, half, d, nblk) for half in range(DIL_GROUP)]
                rows = [(_strided(qs, DIL_BLOCK, d), _strided(ks, 2 * DIL_BLOCK, d)) for qs, ks, _ in idx]
                ld = [(qf[qr, :].astype(BF16), kf[kr, :].astype(BF16), vf[kr, :].astype(BF16),
                       dof[qr, :].astype(BF16)) for qr, kr in rows]
                s = [_dot_nt(qb, kk) + bias[sel] for (qb, kk, _, _), (_, _, sel) in zip(ld, idx)]
                dp = [_dot_nt(dob, vv) for _, _, vv, dob in ld]
                p = [jnp.exp(sv - lse_ref[qr, :][:, 0:1]) for sv, (qr, _) in zip(s, rows)]
                ds = [(pv * (dpv - dl[qr, :][:, 0:1])).astype(BF16) for pv, dpv, (qr, _) in zip(p, dp, rows)]
                pb = [pv.astype(BF16) for pv in p]
                gq = [_dot(dsv, kk) for dsv, (_, kk, _, _) in zip(ds, ld)]
                gk = [_dot_tn(dsv, qb) for dsv, (qb, _, _, _) in zip(ds, ld)]
                gv = [_dot_tn(pv, dob) for pv, (_, _, _, dob) in zip(pb, ld)]
                for (qr, kr), a, b, c in zip(rows, gq, gk, gv):
                    dqa[qr, :] += a
                    dka[kr, :] += b
                    dva[kr, :] += c
                return carry

            lax.fori_loop(0, nblk // DIL_GROUP, pair, 0)

        def fin(t, carry):
            rows = pl.ds(pl.multiple_of(t * prep_rows, prep_rows), prep_rows)
            cs, sn = cos_ref[rows, :], sin_ref[rows, :]
            gq, gk = dqa[rows, :] * scale, dka[rows, :]
            dq_ref[rows, :] = (gq * cs - pltpu.roll(gq, DIL_HD // 2, 1) * sn).astype(BF16)
            dk_ref[rows, :] = (gk * cs - pltpu.roll(gk, DIL_HD // 2, 1) * sn).astype(BF16)
            dv_ref[rows, :] = dva[rows, :].astype(BF16)
            return carry

        lax.fori_loop(0, s_len // prep_rows, fin, 0)
        comm_after()

    head = lambda base: pl.BlockSpec((s_len, DIL_HD), lambda h: (0, base // DIL_HD + h))
    table = pl.BlockSpec((s_len, DIL_HD), lambda h: (0, 0))
    out = pl.BlockSpec((s_len, DIL_HD), lambda h: (0, h))
    shp = jax.ShapeDtypeStruct((s_len, DIL_HEADS * DIL_HD), BF16)
    return pl.pallas_call(
        body, name="dil_bwd_comm" if comm else "dil_bwd", grid=(DIL_HEADS,),
        out_shape=[shp, shp, shp] + (_comm_out_shapes(*comm) if comm else []),
        in_specs=[head(COL_QB), head(COL_KB), head(COL_VB - NP_F32), table, table,
                  pl.BlockSpec((s_len, DIL_HD), lambda h: (0, DIL_HEADS + h)), out, out] + [ANY] * nc,
        out_specs=[out, out, out] + [ANY] * len(_comm_plumbing(comm)[1]),
        scratch_shapes=[pltpu.VMEM((s_len, DIL_HD), F32) for _ in range(8)]
        + [pltpu.VMEM((2, DIL_BLOCK, 2 * DIL_BLOCK), F32)] + (_comm_scratch(nc) if comm else []),
        compiler_params=_params(("arbitrary",), 56),
    )(pf, pf, pb, cos, sin_signed, do, o_b, lse, *(comm[1] if comm else []))


_PIECES = ((COL_Z, 1024), (COL_QA, 256), (COL_KA, 256), (COL_QB, 512), (COL_KB, 512), (COL_VA, 512), (COL_VB, 512),
           (COL_LR, 128))


def _in_bwd(pieces, w_new, x, dxo, g_pre, scale, comm=None, ts=256):
    s_len = x.shape[0]
    nc = len(comm[1]) if comm else 0
    nco = len(_comm_out_shapes(*comm)) if comm else 0
    npc = len(_PIECES)

    def body(*refs):
        p_refs = refs[:npc]
        w_ref, x_ref, dxo_ref, g_ref, sc_ref = refs[npc:npc + 5]
        cin, (dx_ref, sums_ref), cout = (refs[npc + 5:npc + 5 + nc], refs[npc + 5 + nc:npc + 7 + nc],
                                         refs[npc + 7 + nc:npc + 7 + nc + nco])
        comm_before, comm_after = _comm_hooks(comm, cin, cout, refs[npc + 7 + nc + nco:], steps=s_len // ts)
        comm_before()

        @pl.when(pl.program_id(0) == 0)
        def _():
            sums_ref[...] = jnp.zeros_like(sums_ref)

        dh = jnp.zeros((ts, D_MODEL), F32)
        for p_ref, (col, width) in zip(p_refs, _PIECES):
            dh += _dot_nt(p_ref[...], w_ref[:, col:col + width])
        xv = x_ref[...]
        rstd = lax.rsqrt(jnp.mean(xv * xv, axis=-1, keepdims=True) + EPS)
        xn = xv * rstd
        sums_ref[0:1, :] += jnp.sum(dh, axis=0, keepdims=True)
        sums_ref[1:2, :] += jnp.sum(dh * (xn * g_ref[...]), axis=0, keepdims=True)
        dr = dh * (1.0 + sc_ref[...])
        sums_ref[2:3, :] += jnp.sum(dr * xn, axis=0, keepdims=True)
        dxn = dr * g_ref[...]
        dx_ref[...] = dxo_ref[...] + rstd * (dxn - xn * jnp.mean(dxn * xn, axis=-1, keepdims=True))
        comm_after()

    (g_pre, g_spec), (scale, sc_spec) = _rowvec(g_pre), _rowvec(scale)
    tile = pl.BlockSpec((ts, D_MODEL), lambda i: (i, 0))
    return pl.pallas_call(
        body, name="in_bwd_comm" if comm else "in_bwd", grid=(s_len // ts,),
        out_shape=[jax.ShapeDtypeStruct((s_len, D_MODEL), F32), jax.ShapeDtypeStruct((8, D_MODEL), F32)]
        + (_comm_out_shapes(*comm) if comm else []),
        in_specs=[pl.BlockSpec((ts, width), lambda i: (i, 0)) for _, width in _PIECES]
        + [pl.BlockSpec((D_MODEL, NP), lambda i: (0, 0)), tile, tile, g_spec, sc_spec] + [ANY] * nc,
        out_specs=[tile, pl.BlockSpec((8, D_MODEL), lambda i: (0, 0))] + [ANY] * nco,
        scratch_shapes=_comm_scratch(nc) if comm else [],
        compiler_params=_params(("arbitrary",), 56),
    )(*pieces, w_new, x, dxo, g_pre, scale, *(comm[1] if comm else []))


def _w_in_to_kernel(gathered, comm=None, tr=128):
    n_cin, c_shapes, c_scratch = _comm_plumbing(comm)

    def body(*refs):
        (g_ref,), (o_ref,), _, cin, cout, csem = _split_refs(refs, 1, 1, 0, comm)
        comm_before, comm_after = _comm_hooks(comm, cin, cout, csem, steps=D_MODEL // tr)
        comm_before()
        cols = jnp.concatenate([g_ref[k].astype(F32) for k in range(N_DEV)], axis=1)
        pad = jnp.zeros((tr, LANE - GLA_LOWRANK), F32)
        o_ref[...] = jnp.concatenate(
            [cols[:, 1024:1536], cols[:, 3088:3600], cols[:, 0:512], cols[:, 1552:2576], cols[:, 512:1024],
             cols[:, 2576:3088], cols[:, 1536:1552], pad], axis=1).astype(BF16)
        comm_after()

    return pl.pallas_call(
        body, name="w_in_to_kernel_comm" if comm else "w_in_to_kernel", grid=(D_MODEL // tr,),
        out_shape=[jax.ShapeDtypeStruct((D_MODEL, NP), BF16)] + c_shapes,
        in_specs=[pl.BlockSpec((N_DEV, tr, W_IN_SHARD), lambda i: (0, i, 0))] + [ANY] * n_cin,
        out_specs=[pl.BlockSpec((tr, NP), lambda i: (i, 0))] + [ANY] * len(c_shapes),
        scratch_shapes=c_scratch,
        compiler_params=_params(("arbitrary",)),
    )(gathered, *(comm[1] if comm else []))


def _grad_w_in(h, pieces, ts=512, tr=128):
    s_len = h.shape[0]
    steps = s_len // ts

    def body(*refs):
        h_ref, p_refs = refs[0], refs[1:1 + len(_PIECES)]
        o_ref, acc = refs[1 + len(_PIECES):]

        @pl.when(pl.program_id(0) == 0)
        def _():
            acc[...] = jnp.zeros_like(acc)

        hv = h_ref[...]
        for p_ref, (col, width) in zip(p_refs, _PIECES):
            acc[:, col:col + width] += _dot_tn(hv, p_ref[...])

        @pl.when(pl.program_id(0) == steps - 1)
        def _():
            def rows_out(t, carry):
                rows = pl.ds(pl.multiple_of(t * tr, tr), tr)
                g = acc[rows, :]
                cols = jnp.concatenate(
                    [g[:, COL_QA:COL_QB], g[:, COL_VA:COL_VB], g[:, 0:512], g[:, COL_LR:COL_LR + GLA_LOWRANK],
                     g[:, COL_QB:COL_VA], g[:, COL_VB:COL_LR], g[:, 512:1024]], axis=1)
                for k in range(N_DEV):
                    o_ref[k, rows, :] = cols[:, W_IN_SHARD * k:W_IN_SHARD * (k + 1)].astype(BF16)
                return carry

            lax.fori_loop(0, D_MODEL // tr, rows_out, 0)

    return pl.pallas_call(
        body, name="grad_w_in", grid=(steps,),
        out_shape=jax.ShapeDtypeStruct((N_DEV, D_MODEL, W_IN_SHARD), BF16),
        in_specs=[pl.BlockSpec((ts, D_MODEL), lambda i: (i, 0))]
        + [pl.BlockSpec((ts, width), lambda i: (i, 0)) for _, width in _PIECES],
        out_specs=pl.BlockSpec((N_DEV, D_MODEL, W_IN_SHARD), lambda i: (0, 0, 0)),
        scratch_shapes=[pltpu.VMEM((D_MODEL, NP), F32)],
        compiler_params=_params(("arbitrary",), 56),
    )(h, *pieces)


def _matmul_tn(a, b, name, bn, ts=512):
    s_len, m = a.shape
    n = b.shape[1]
    steps = s_len // ts

    def body(a_ref, b_ref, o_ref, acc):
        @pl.when(pl.program_id(1) == 0)
        def _():
            acc[...] = jnp.zeros_like(acc)

        acc[...] += _dot_tn(a_ref[...], b_ref[...])

        @pl.when(pl.program_id(1) == steps - 1)
        def _():
            o_ref[...] = acc[...].astype(BF16)

    return pl.pallas_call(
        body, name=name, grid=(n // bn, steps),
        out_shape=jax.ShapeDtypeStruct((m, n), BF16),
        in_specs=[pl.BlockSpec((ts, m), lambda j, i: (i, 0)), pl.BlockSpec((ts, bn), lambda j, i: (i, j))],
        out_specs=pl.BlockSpec((m, bn), lambda j, i: (0, j)),
        scratch_shapes=[pltpu.VMEM((m, bn), F32)],
        compiler_params=_params(("arbitrary", "arbitrary"), 40),
    )(a, b)


def _adam_math(w, g, m, v):
    m = ADAM_B1 * m + (1.0 - ADAM_B1) * g
    v = ADAM_B2 * v + (1.0 - ADAM_B2) * (g * g)
    m_hat = m / (1.0 - ADAM_B1 ** ADAM_STEP)
    v_hat = v / (1.0 - ADAM_B2 ** ADAM_STEP)
    delta = -ADAM_LR * (m_hat / (jnp.sqrt(v_hat) + ADAM_EPS) + ADAM_WD * w)
    return delta, m, v


def _adamw(w, parts, m, v, name, tr):
    r, cdim = w.shape
    n_parts = parts.shape[0]

    def body(w_ref, p_ref, m_ref, v_ref, g_ref, d_ref, nm_ref, nv_ref):
        g = p_ref[0].astype(F32)
        for k in range(1, n_parts):
            g = g + p_ref[k].astype(F32)
        g_ref[...] = g
        d_ref[...], nm_ref[...], nv_ref[...] = _adam_math(w_ref[...], g, m_ref[...], v_ref[...])

    tile = pl.BlockSpec((tr, cdim), lambda i: (i, 0))
    shp = jax.ShapeDtypeStruct((r, cdim), F32)
    return pl.pallas_call(
        body, name=name, grid=(r // tr,), out_shape=(shp, shp, shp, shp),
        in_specs=[tile, pl.BlockSpec((n_parts, tr, cdim), lambda i: (0, i, 0)), tile, tile],
        out_specs=(tile, tile, tile, tile),
        compiler_params=_params(("arbitrary",), 40),
    )(w, parts, m, v)


def _adamw_layers(w, parts, m, v, name, tr):
    n_layers, r, cdim = w.shape

    def body(*refs):
        w_ref, p_refs, (m_ref, v_ref) = refs[0], refs[1:1 + n_layers], refs[1 + n_layers:3 + n_layers]
        g_ref, d_ref, nm_ref, nv_ref = refs[3 + n_layers:]
        for l, p_ref in enumerate(p_refs):
            @pl.when(pl.program_id(0) == l)
            def _(p_ref=p_ref):
                g = p_ref[0].astype(F32)
                for k in range(1, p_ref.shape[0]):
                    g = g + p_ref[k].astype(F32)
                g_ref[0] = g
                d_ref[0], nm_ref[0], nv_ref[0] = _adam_math(w_ref[0], g, m_ref[0], v_ref[0])

    tile = pl.BlockSpec((1, tr, cdim), lambda l, i: (l, i, 0))
    part = lambda own: pl.BlockSpec((parts[own].shape[0], tr, cdim), lambda l, i: (0, jnp.where(l == own, i, 0), 0))
    shp = jax.ShapeDtypeStruct(w.shape, F32)
    return pl.pallas_call(
        body, name=name, grid=(n_layers, r // tr), out_shape=(shp, shp, shp, shp),
        in_specs=[tile] + [part(l) for l in range(n_layers)] + [tile, tile],
        out_specs=(tile, tile, tile, tile),
        compiler_params=_params(("arbitrary", "arbitrary"), 40),
    )(w, *parts, m, v)


def _row(vec, width):
    vec = vec.reshape(1, -1)
    return jnp.pad(vec, ((0, 0), (0, width - vec.shape[1])))


def kernel(x, c, w_ada, b_ada, g_pre, w_in, w_gate_up, b_gate_up, g_gla, g_dil, w_out, g_post, loss_target, m_w_ada, m_b_ada, m_g_pre, m_w_in, m_w_gate_up, m_b_gate_up, m_g_gla, m_g_dil, m_w_out, m_g_post, v_w_ada, v_b_ada, v_g_pre, v_w_in, v_w_gate_up, v_b_gate_up, v_g_gla, v_g_dil, v_w_out, v_g_post):
    px, py, pc = _my_position()
    me = _linear(px, py, pc)
    xs = x[0]
    target = loss_target[0]
    s_len = xs.shape[0]
    assert s_len % (DIL_BLOCK * max(DIL_DILATIONS) * 2) == 0 and xs.shape[1] == D_MODEL

    w_in_b, w_out_b = w_in.astype(BF16), w_out.astype(BF16)
    c_rows, wgu_all, w_in_all = _comm_call(
        "gather", [jnp.pad(c, ((0, 7), (0, 0))), w_gate_up.reshape(DEPTH * GLA_LOWRANK, GU_SHARD), w_in_b[0]],
        "gather_first")
    c_all = c_rows.reshape(N_DEV, 8, D_MODEL)[:, 0]
    mod_part = _mod_fwd(c_all, w_ada)
    w_new, mod_all = _w_in_to_kernel(w_in_all.reshape(N_DEV, D_MODEL, W_IN_SHARD),
                                     comm=("gather", [mod_part.reshape(DEPTH * N_DEV, ADA_SHARD)]))
    mod_all = mod_all.reshape(N_DEV, DEPTH, N_DEV, ADA_SHARD)
    mod_mine = lax.dynamic_index_in_dim(mod_all, me, axis=2, keepdims=False)
    mod = jnp.transpose(mod_mine, (1, 0, 2)).reshape(DEPTH, 3 * D_MODEL) + b_ada
    wgu_full = jnp.transpose(wgu_all.reshape(N_DEV, DEPTH, GLA_LOWRANK, GU_SHARD), (1, 2, 0, 3)).reshape(
        DEPTH, GLA_LOWRANK, GU_COLS)
    wgu_pad = jnp.pad(wgu_full, ((0, 0), (0, LANE - GLA_LOWRANK), (0, 0))).astype(BF16)

    cos, sin_signed = _rope_tables(s_len)
    g_heads = jnp.concatenate([g_gla, g_dil], axis=1)

    saved = []
    xl = xs
    for l in range(DEPTH):
        shift, scale, gate = ((mod, l, k) for k in range(3))
        if l > 0:
            w_new = _w_in_to_kernel(w_in_all.reshape(N_DEV, D_MODEL, W_IN_SHARD))[0]
        pf, pb, h, w_out_l = _prenorm_proj(xl, (g_pre, l, 0), scale, shift, w_new, comm=("gather", [w_out_b[l]]))
        o_a, states = _gla_fwd(pf, pb, wgu_pad, b_gate_up, l)
        if l + 1 < DEPTH:
            o_b, lse, w_in_all = _dil_fwd(pf, pb, cos, sin_signed, comm=("gather", [w_in_b[l + 1]]))
        else:
            o_b, lse = _dil_fwd(pf, pb, cos, sin_signed)
        if l + 1 < DEPTH:
            x_next, y, u = _post_fwd(o_a, o_b, pf, (g_heads, l, 0), w_out_l, xl, gate, (g_post, l, 0))
        else:
            dx, y, u, loss_part = _post_fwd(o_a, o_b, pf, (g_heads, l, 0), w_out_l, xl, gate, (g_post, l, 0),
                                            target=target)
        saved.append((xl, scale, gate, w_new, w_out_l, pf, pb, h, o_a, states, o_b, lse, y, u))
        xl = x_next

    small_rows = []
    gin_slots, gin_parts, gout_parts = None, [None] * DEPTH, [None] * DEPTH
    for l in reversed(range(DEPTH)):
        x_in, scale, gate, w_new, w_out_l, pf, pb, h, o_a, states, o_b, lse, y, u = saved[l]
        du, do, dz, sums_post = _post_bwd(dx, u, gate, (g_post, l, 0), w_out_l, o_a, o_b, pf, (g_heads, l, 0))
        gout_slots = _matmul_tn(y, du, "grad_w_out", 512)
        dq_a, dk_a, dv_a, dlr2, dwgu, dbgu, arrived = _gla_bwd(pf, pb, wgu_pad, b_gate_up, l, states, do,
                                                               comm=("exchange", [gout_slots]))
        gout_parts[l] = arrived.reshape(N_DEV, OUT_SHARD, D_MODEL)
        if gin_slots is not None:
            dq_b, dk_b, dv_b, arrived, _, _ = _dil_bwd(pf, pb, cos, sin_signed, do, o_b, lse,
                                                       comm=("pairsum_exchange", [gin_slots]))
            gin_parts[l + 1] = arrived.reshape(N_DEV // 2, D_MODEL, W_IN_SHARD)
        else:
            dq_b, dk_b, dv_b = _dil_bwd(pf, pb, cos, sin_signed, do, o_b, lse)
        dlr = (dlr2[0] + dlr2[1]).astype(BF16)
        pieces = (dz, dq_a, dk_a, dq_b, dk_b, dv_a, dv_b, dlr)
        gin_slots = _grad_w_in(h, pieces).reshape(N_DEV * D_MODEL, W_IN_SHARD)
        if l == 0:
            dx, sums_in, arrived, _, _ = _in_bwd(pieces, w_new, x_in, dx, (g_pre, l, 0), scale,
                                                 comm=("pairsum_exchange", [gin_slots]))
            gin_parts[0] = arrived.reshape(N_DEV // 2, D_MODEL, W_IN_SHARD)
        else:
            dx, sums_in = _in_bwd(pieces, w_new, x_in, dx, (g_pre, l, 0), scale)
        dmod = jnp.concatenate([sums_in[0], sums_in[1], sums_post[0]])
        vecs = jnp.concatenate([sums_in[2], sums_post[1], sums_post[2], dbgu[0]])
        small_rows[0:0] = [_row(dmod, 4096), _row(vecs, 4096), _row(dwgu[:GLA_LOWRANK], 4096)]
    grad_x = dx[None]

    flat = lambda a, rows: a.reshape(rows, a.shape[-1])
    r_ada = DEPTH * D_MODEL
    g_w_in, d_w_in, nm_w_in, nv_w_in = _adamw_layers(w_in, gin_parts, m_w_in, v_w_in, "adamw_w_in", 256)
    g_w_out, d_w_out, nm_w_out, nv_w_out = _adamw_layers(w_out, gout_parts, m_w_out, v_w_out, "adamw_w_out", 128)

    small_rows += [_row(loss_part[0, 0:1], 4096), jnp.zeros((1, 4096), F32)]
    small = _all_gather(jnp.concatenate(small_rows, axis=0), "gather_small").reshape(N_DEV, 8, 4096)
    dmod_all = jnp.stack([small[:, 0, :3 * D_MODEL], small[:, 3, :3 * D_MODEL]])
    dmod_cols = lax.dynamic_slice_in_dim(dmod_all, me * ADA_SHARD, ADA_SHARD, axis=2)
    gwa = _w_ada_grad(c_all, dmod_cols).reshape(1, r_ada, ADA_SHARD)
    g_w_ada, d_w_ada, nm_w_ada, nv_w_ada = (
        t.reshape(w_ada.shape) for t in _adamw(flat(w_ada, r_ada), gwa, flat(m_w_ada, r_ada), flat(v_w_ada, r_ada),
                                               "adamw_w_ada", 256))

    where = ((0, 0), (1, 0), (1, 1024), (1, 2048), (1, 2560), (1, 3072))
    replicated = [(b_ada, m_b_ada, v_b_ada), (g_pre, m_g_pre, v_g_pre), (g_post, m_g_post, v_g_post),
                  (g_gla, m_g_gla, v_g_gla), (g_dil, m_g_dil, v_g_dil), (b_gate_up, m_b_gate_up, v_b_gate_up)]
    updated, loss = _adamw_replicated(small, replicated, where, loss_at=(6, 0))
    ((g_b_ada, d_b_ada, nm_b_ada, nv_b_ada), (g_g_pre, d_g_pre, nm_g_pre, nv_g_pre),
     (g_g_post, d_g_post, nm_g_post, nv_g_post), (g_g_gla, d_g_gla, nm_g_gla, nv_g_gla),
     (g_g_dil, d_g_dil, nm_g_dil, nv_g_dil), (g_b_gu, d_b_gu, nm_b_gu, nv_b_gu)) = updated
    gu_parts = jnp.stack([small[:, 2], small[:, 5]], axis=1).reshape(N_DEV, DEPTH, GLA_LOWRANK, GU_COLS)
    gu_parts = lax.dynamic_slice_in_dim(gu_parts, me * GU_SHARD, GU_SHARD, axis=3).reshape(
        N_DEV, DEPTH * GLA_LOWRANK, GU_SHARD)
    r_gu = DEPTH * GLA_LOWRANK
    g_w_gu, d_w_gu, nm_w_gu, nv_w_gu = (
        t.reshape(w_gate_up.shape) for t in _adamw(flat(w_gate_up, r_gu), gu_parts, flat(m_w_gate_up, r_gu),
                                                   flat(v_w_gate_up, r_gu), "adamw_w_gate_up", r_gu))
    return (loss, grad_x,
            g_w_ada, g_b_ada, g_g_pre, g_w_in, g_w_gu, g_b_gu, g_g_gla, g_g_dil, g_w_out, g_g_post,
            d_w_ada, d_b_ada, d_g_pre, d_w_in, d_w_gu, d_b_gu, d_g_gla, d_g_dil, d_w_out, d_g_post,
            nm_w_ada, nm_b_ada, nm_g_pre, nm_w_in, nm_w_gu, nm_b_gu, nm_g_gla, nm_g_dil, nm_w_out, nm_g_post,
            nv_w_ada, nv_b_ada, nv_g_pre, nv_w_in, nv_w_gu, nv_b_gu, nv_g_gla, nv_g_dil, nv_w_out, nv_g_post)


def _adamw_replicated(small, params, where, loss_at):
    n_parts = small.shape[0]

    def body(*refs):
        s_ref, p_refs, o_refs = refs[0], refs[1:1 + 3 * len(params)], refs[1 + 3 * len(params):]
        total = s_ref[0]
        for k in range(1, n_parts):
            total = total + s_ref[k]
        for i, (row, col) in enumerate(where):
            w_ref, m_ref, v_ref = p_refs[3 * i:3 * i + 3]
            n = w_ref.shape[1]
            g = jnp.concatenate([total[row + 3 * l:row + 3 * l + 1, col:col + n] for l in range(DEPTH)], axis=0)
            o_refs[4 * i][...] = g
            o_refs[4 * i + 1][...], o_refs[4 * i + 2][...], o_refs[4 * i + 3][...] = _adam_math(
                w_ref[...], g, m_ref[...], v_ref[...])
        o_refs[-1][...] = jnp.broadcast_to(total[loss_at[0]:loss_at[0] + 1, loss_at[1]:loss_at[1] + 1], (8, LANE))

    flat = [a for p in params for a in p]
    shapes = [jax.ShapeDtypeStruct(p[0].shape, F32) for p in params for _ in range(4)]
    outs = pl.pallas_call(body, name="adamw_replicated",
                          out_shape=shapes + [jax.ShapeDtypeStruct((8, LANE), F32)])(small, *flat)
    return [tuple(outs[4 * i:4 * i + 4]) for i in range(len(params))], outs[-1][0, 0]
```

```python
import functools
import math

import jax
import jax.numpy as jnp
from jax import lax
from jax.experimental import pallas as pl
from jax.experimental.pallas import tpu as pltpu

F32 = jnp.float32
BF16 = jnp.bfloat16

N_DEV = 8
D_MODEL = 1024
DEPTH = 2
GLA_HEADS = 4
GLA_DK = 64
GLA_DV = 128
GLA_CHUNK = 64
GLA_TAU = 16.0
GLA_LOWRANK = 16
DIL_HEADS = 4
DIL_HD = 128
DIL_BLOCK = 128
DIL_DILATIONS = (1, 4, 16)
ROPE_THETA = 10000.0
EPS = 1e-6
IN_COLS = 3600
W_IN_SHARD = IN_COLS // N_DEV
ADA_SHARD = 3 * D_MODEL // N_DEV
OUT_SHARD = D_MODEL // N_DEV
GU_COLS = GLA_HEADS * GLA_DK
GU_SHARD = GU_COLS // N_DEV

ADAM_LR = 0.001
ADAM_B1 = 0.9
ADAM_B2 = 0.999
ADAM_EPS = 1e-08
ADAM_WD = 0.01
ADAM_STEP = 10

NP = 3712
COL_Z, COL_QA, COL_KA, COL_QB, COL_KB, COL_VA, COL_VB, COL_LR = 0, 1024, 1280, 1536, 2048, 2560, 3072, 3584
NP_F32 = COL_VA
NP_BF16 = NP - NP_F32
LANE = 128
MASK_VALUE = -1e30

MESH = pl.DeviceIdType.MESH
ANY = pl.BlockSpec(memory_space=pl.ANY)


def _params(sem=None, vmem_mb=None):
    kw = {}
    if sem is not None:
        kw["dimension_semantics"] = sem
    if vmem_mb is not None:
        kw["vmem_limit_bytes"] = vmem_mb * 1024 * 1024
    return pltpu.CompilerParams(**kw)


def _dot(a, b):
    return jnp.dot(a, b, preferred_element_type=F32)


def _dot_nt(a, b):
    return lax.dot_general(a, b, (((1,), (1,)), ((), ())), preferred_element_type=F32)


def _dot_tn(a, b):
    return lax.dot_general(a, b, (((0,), (0,)), ((), ())), preferred_element_type=F32)


def _sigmoid(z):
    return 1.0 / (1.0 + jnp.exp(-z))


def _log_sigmoid(z):
    return jnp.minimum(z, 0.0) - jnp.log(1.0 + jnp.exp(-jnp.abs(z)))


def _rowvec(v, width=D_MODEL):
    arr, row, cb = v
    return arr.reshape(arr.shape[0], 1, arr.shape[1]), pl.BlockSpec((None, 1, width), lambda *_: (row, 0, cb))


def _my_position():
    return lax.axis_index("x"), lax.axis_index("y"), lax.axis_index("c")


def _linear(px, py, pc):
    return 4 * px + 2 * py + pc


def _gather_phase(phase, x_ref, out_ref, send_sem, recv_sem, local_sem):
    m = x_ref.shape[0]
    x, y, c = _my_position()
    me, sibling = (x, y, c), (x, y, 1 - c)
    chips = [(1 - x, y), (x, 1 - y), (1 - x, 1 - y)]

    def rows(px, py, pc):
        return out_ref.at[pl.ds(_linear(px, py, pc) * m, m), :]

    def copy(k, block, to, src=None):
        return pltpu.make_async_remote_copy(
            src_ref=rows(*block) if src is None else src, dst_ref=rows(*block),
            send_sem=send_sem(k), recv_sem=recv_sem(k), device_id=to, device_id_type=MESH)

    mine = pltpu.make_async_copy(x_ref, rows(*me), local_sem)
    first = [copy(0, me, sibling, src=x_ref)] + [copy(1 + j, me, (*chip, c), src=x_ref) for j, chip in enumerate(chips)]
    passed = [copy(4 + j, (*chip, c), sibling) for j, chip in enumerate(chips)]
    if phase == "start":
        mine.start()
        for cp in first:
            cp.start()
    elif phase == "forward":
        for j, chip in enumerate(chips):
            copy(1 + j, (*chip, c), me).wait_recv()
            passed[j].start()
    else:
        copy(0, sibling, me).wait_recv()
        for j, chip in enumerate(chips):
            copy(4 + j, (*chip, 1 - c), me).wait_recv()
        for cp in first + passed:
            cp.wait_send()
        mine.wait()


def _exchange_phase(phase, x_ref, out_ref, send_sem, recv_sem, local_sem):
    m = x_ref.shape[0] // N_DEV
    x, y, c = _my_position()
    me = _linear(x, y, c)

    def rows(ref, idx):
        return ref.at[pl.ds(idx * m, m), :]

    peers = [(1 - x if j & 4 else x, 1 - y if j & 2 else y, 1 - c if j & 1 else c) for j in range(1, N_DEV)]
    local = pltpu.make_async_copy(rows(x_ref, me), rows(out_ref, me), local_sem)
    sends = [pltpu.make_async_remote_copy(
        src_ref=rows(x_ref, _linear(*peer)), dst_ref=rows(out_ref, me),
        send_sem=send_sem(j), recv_sem=recv_sem(j), device_id=peer, device_id_type=MESH) for j, peer in enumerate(peers)]
    if phase == "start":
        local.start()
        for cp in sends:
            cp.start()
    else:
        for j, peer in enumerate(peers):
            pltpu.make_async_remote_copy(
                src_ref=rows(x_ref, _linear(*peer)), dst_ref=rows(out_ref, _linear(*peer)),
                send_sem=send_sem(j), recv_sem=recv_sem(j), device_id=peer, device_id_type=MESH).wait_recv()
        for cp in sends:
            cp.wait_send()
        local.wait()


def _pairsum_exchange_phase(phase, x_ref, out_refs, send_sem, recv_sem, local_sem):
    out_ref, stage_ref, pair_ref = out_refs
    m, n = x_ref.shape[0] // N_DEV, x_ref.shape[1]
    x, y, c = _my_position()
    mine = 2 * x + y
    chips = [(qx, qy) for qx in range(2) for qy in range(2)]
    others = [(1 - x, y), (x, 1 - y), (1 - x, 1 - y)]

    def rows(ref, idx):
        return ref.at[pl.ds(idx * m, m), :]

    def remote(src, dst, k, to):
        return pltpu.make_async_remote_copy(src_ref=src, dst_ref=dst, send_sem=send_sem(k), recv_sem=recv_sem(k),
                                            device_id=to, device_id_type=MESH)

    to_sibling = [remote(rows(x_ref, _linear(qx, qy, 1 - c)), rows(stage_ref, q), q, (x, y, 1 - c))
                  for q, (qx, qy) in enumerate(chips)]
    to_chips = [remote(rows(pair_ref, 2 * qx + qy), rows(out_ref, mine), 4 + j, (qx, qy, c))
                for j, (qx, qy) in enumerate(others)]
    keep = pltpu.make_async_copy(rows(pair_ref, mine), rows(out_ref, mine), local_sem)
    if phase == "start":
        for cp in to_sibling:
            cp.start()
    elif phase == "reduce":
        for cp in to_sibling:
            cp.wait_recv()

        def through_vmem(a_buf, b_buf, sems):
            tr = 128
            loads = [(pltpu.make_async_copy(rows(x_ref, _linear(qx, qy, c)), a_buf.at[q % 2], sems.at[q % 2]),
                      pltpu.make_async_copy(rows(stage_ref, q), b_buf.at[q % 2], sems.at[2 + q % 2]))
                     for q, (qx, qy) in enumerate(chips)]
            stores = [pltpu.make_async_copy(a_buf.at[q % 2], rows(pair_ref, q), sems.at[4 + q % 2]) for q in range(4)]
            for cp in loads[0]:
                cp.start()
            for q in range(4):
                for cp in loads[q]:
                    cp.wait()
                if q + 1 < 4:
                    if q >= 1:
                        stores[q - 1].wait()
                    for cp in loads[q + 1]:
                        cp.start()

                def add(r, carry, q=q):
                    tile = pl.ds(pl.multiple_of(r * tr, tr), tr)
                    a_buf[q % 2, tile, :] = (a_buf[q % 2, tile, :].astype(F32)
                                             + b_buf[q % 2, tile, :].astype(F32)).astype(x_ref.dtype)
                    return carry

                lax.fori_loop(0, m // tr, add, 0)
                stores[q].start()
            stores[2].wait()
            stores[3].wait()

        pl.run_scoped(through_vmem, pltpu.VMEM((2, m, n), x_ref.dtype), pltpu.VMEM((2, m, n), x_ref.dtype),
                      pltpu.SemaphoreType.DMA((6,)))
    elif phase == "send":
        keep.start()
        for cp in to_chips:
            cp.start()
    else:
        for j, (qx, qy) in enumerate(others):
            remote(rows(pair_ref, mine), rows(out_ref, 2 * qx + qy), 4 + j, (qx, qy, c)).wait_recv()
        for cp in to_sibling + to_chips:
            cp.wait_send()
        keep.wait()


_COMM_PHASES = {"gather": (_gather_phase, ("start", "forward", "finish")),
                "exchange": (_exchange_phase, ("start", "finish")),
                "pairsum_exchange": (_pairsum_exchange_phase, ("start", "reduce", "send", "finish"))}


def _comm_scratch(n_arrays):
    return [pltpu.SemaphoreType.DMA((n_arrays, 7)), pltpu.SemaphoreType.DMA((n_arrays, 7)),
            pltpu.SemaphoreType.DMA((n_arrays,))]


def _comm_run(kind, phases, x_refs, out_refs, send_sems, recv_sems, local_sems):
    fn = _COMM_PHASES[kind][0]
    per = len(out_refs) // len(x_refs)
    for phase in phases:
        for a, x_ref in enumerate(x_refs):
            outs = out_refs[a] if per == 1 else tuple(out_refs[per * a:per * (a + 1)])
            fn(phase, x_ref, outs, lambda k, a=a: send_sems.at[a, k], lambda k, a=a: recv_sems.at[a, k],
               local_sems.at[a])


def _comm_out_shapes(kind, arrays):
    if kind == "pairsum_exchange":
        return [jax.ShapeDtypeStruct((a.shape[0] // 2, a.shape[1]), a.dtype) for a in arrays for _ in range(3)]
    return [jax.ShapeDtypeStruct((N_DEV * a.shape[0], a.shape[1]) if kind == "gather" else a.shape, a.dtype)
            for a in arrays]


def _comm_call(kind, arrays, name):
    n = len(arrays)
    shapes = _comm_out_shapes(kind, arrays)

    def body(*refs):
        _comm_run(kind, _COMM_PHASES[kind][1], refs[:n], refs[n:n + len(shapes)], *refs[n + len(shapes):])

    return pl.pallas_call(body, name=name, out_shape=shapes, in_specs=[ANY] * n, out_specs=[ANY] * len(shapes),
                          scratch_shapes=_comm_scratch(n))(*arrays)


def _all_gather(xs, name):
    return _comm_call("gather", [xs], name)[0]


def _mod_fwd(c_all, w_ada):
    def body(c_ref, w_ref, o_ref):
        cv = c_ref[...]
        sc = cv * _sigmoid(cv)
        o_ref[0] = _dot(sc.astype(BF16), w_ref[0].astype(BF16))

    return pl.pallas_call(
        body, name="mod_fwd", grid=(DEPTH,),
        out_shape=jax.ShapeDtypeStruct((DEPTH, N_DEV, ADA_SHARD), F32),
        in_specs=[pl.BlockSpec((N_DEV, D_MODEL), lambda l: (0, 0)),
                  pl.BlockSpec((1, D_MODEL, ADA_SHARD), lambda l: (l, 0, 0))],
        out_specs=pl.BlockSpec((1, N_DEV, ADA_SHARD), lambda l: (l, 0, 0)),
        compiler_params=_params(("arbitrary",)),
    )(c_all, w_ada)


def _w_ada_grad(c_all, dmod_cols):
    def body(c_ref, d_ref, o_ref):
        cv = c_ref[...]
        sc = cv * _sigmoid(cv)
        o_ref[0] = lax.dot_general(sc, d_ref[0], (((0,), (0,)), ((), ())), precision=lax.Precision.HIGHEST,
                                   preferred_element_type=F32)

    return pl.pallas_call(
        body, name="w_ada_grad", grid=(DEPTH,),
        out_shape=jax.ShapeDtypeStruct((DEPTH, D_MODEL, ADA_SHARD), F32),
        in_specs=[pl.BlockSpec((N_DEV, D_MODEL), lambda l: (0, 0)),
                  pl.BlockSpec((1, N_DEV, ADA_SHARD), lambda l: (l, 0, 0))],
        out_specs=pl.BlockSpec((1, D_MODEL, ADA_SHARD), lambda l: (l, 0, 0)),
        compiler_params=_params(("arbitrary",)),
    )(c_all, dmod_cols)


def _comm_plumbing(comm):
    if not comm:
        return 0, [], []
    return len(comm[1]), _comm_out_shapes(*comm), _comm_scratch(len(comm[1]))


def _split_refs(refs, n_in, n_out, n_scratch, comm):
    ci, shapes, _ = _comm_plumbing(comm)
    co = len(shapes)
    a, b, c = n_in + ci, n_in + ci + n_out, n_in + ci + n_out + co
    return refs[:n_in], refs[a:b], refs[c:c + n_scratch], refs[n_in:a], refs[b:c], refs[c + n_scratch:]


def _prenorm_proj(x, g_pre, scale, shift, w_new, comm=None, ts=256):
    s_len = x.shape[0]
    n_cin, c_shapes, c_scratch = _comm_plumbing(comm)

    def body(*refs):
        (x_ref, g_ref, sc_ref, sh_ref, w_ref), (pf_ref, pb_ref, h_ref), _, cin, cout, csem = _split_refs(
            refs, 5, 3, 0, comm)
        comm_before, comm_after = _comm_hooks(comm, cin, cout, csem, steps=s_len // ts)
        comm_before()
        xv = x_ref[...]
        rstd = lax.rsqrt(jnp.mean(xv * xv, axis=-1, keepdims=True) + EPS)
        h = (xv * rstd * g_ref[...]) * (1.0 + sc_ref[...]) + sh_ref[...]
        hb = h.astype(BF16)
        h_ref[...] = hb
        for j in range(0, NP, 512):
            w = min(512, NP - j)
            acc = _dot(hb, w_ref[:, j:j + w])
            if j < NP_F32:
                pf_ref[:, j:j + w] = acc
            else:
                pb_ref[:, j - NP_F32:j - NP_F32 + w] = acc.astype(BF16)
        comm_after()

    (g_pre, g_spec), (scale, sc_spec), (shift, sh_spec) = _rowvec(g_pre), _rowvec(scale), _rowvec(shift)
    return pl.pallas_call(
        body, name="prenorm_proj_comm" if comm else "prenorm_proj", grid=(s_len // ts,),
        out_shape=[jax.ShapeDtypeStruct((s_len, NP_F32), F32), jax.ShapeDtypeStruct((s_len, NP_BF16), BF16),
                   jax.ShapeDtypeStruct((s_len, D_MODEL), BF16)] + c_shapes,
        in_specs=[pl.BlockSpec((ts, D_MODEL), lambda i: (i, 0)), g_spec, sc_spec, sh_spec,
                  pl.BlockSpec((D_MODEL, NP), lambda i: (0, 0))] + [ANY] * n_cin,
        out_specs=[pl.BlockSpec((ts, NP_F32), lambda i: (i, 0)), pl.BlockSpec((ts, NP_BF16), lambda i: (i, 0)),
                   pl.BlockSpec((ts, D_MODEL), lambda i: (i, 0))] + [ANY] * len(c_shapes),
        scratch_shapes=c_scratch,
        compiler_params=_params(("arbitrary",), 48),
    )(x, g_pre, scale, shift, w_new, *(comm[1] if comm else []))


GLA_GROUP = 16


def _gla_group_rows(t):
    return [pl.ds(pl.multiple_of((t * GLA_GROUP + j) * GLA_CHUNK, GLA_CHUNK), GLA_CHUNK) for j in range(GLA_GROUP)]


def _gla_chunks_common(q_ref, k_ref, lr_ref, wgu_ref, bgu_ref, rows_list):
    c = GLA_CHUNK
    ri = lax.broadcasted_iota(jnp.int32, (c, c), 0)
    ci = lax.broadcasted_iota(jnp.int32, (c, c), 1)
    tril = (ri >= ci).astype(F32)
    zs = [_dot(lr_ref[rows, :], wgu_ref[...]) + bgu_ref[...] for rows in rows_list]
    las = [_log_sigmoid(z) * (1.0 / GLA_TAU) for z in zs]
    bs = [jnp.dot(tril, la, precision=lax.Precision.HIGHEST, preferred_element_type=F32) for la in las]
    out = []
    for rows, z, b in zip(rows_list, zs, bs):
        q = q_ref[rows, :] * (GLA_DK ** -0.5)
        k = k_ref[rows, :]
        bl = b[c - 1:c, :]
        out.append(dict(z=z, b=b, bl=bl, qe=q * jnp.exp(b), ke=k * jnp.exp(-b), kend=k * jnp.exp(bl - b),
                        dec=jnp.exp(bl)))
    return out, ri, ci


def _head_lane_mask(hh):
    return (lax.broadcasted_iota(jnp.int32, (1, LANE), 1) // GLA_DK) == hh


def _state_block_mask():
    r = lax.broadcasted_iota(jnp.int32, (2 * GLA_DV, LANE), 0) // GLA_DV
    cc = lax.broadcasted_iota(jnp.int32, (2 * GLA_DV, LANE), 1) // GLA_DK
    return r == cc


def _gla_fwd(pf, pb, wgu, bgu, layer, comm=None):
    s_len = pf.shape[0]
    nc = s_len // GLA_CHUNK
    ncomm = len(comm[1]) if comm else 0

    def body(*refs):
        q_ref, k_ref, v_ref, lr_ref, wgu_ref, bgu_ref = refs[:6]
        cin, (o_ref, st_ref), cout = refs[6:6 + ncomm], refs[6 + ncomm:8 + ncomm], refs[8 + ncomm:8 + 2 * ncomm]
        qe_s, cs_s, dec_s = refs[8 + 2 * ncomm:11 + 2 * ncomm]
        comm_before, comm_after = _comm_hooks(comm, cin, cout, refs[11 + 2 * ncomm:], steps=2)
        comm_before()
        bd = _state_block_mask()

        def local(t, carry):
            rows_list = _gla_group_rows(t)
            cm, ri, ci = _gla_chunks_common(q_ref, k_ref, lr_ref, wgu_ref, bgu_ref, rows_list)
            vs = [v_ref[rows, :] for rows in rows_list]
            kebs = [c["ke"].astype(BF16) for c in cm]
            a = [[jnp.where(ri >= ci, _dot_nt(jnp.where(_head_lane_mask(hh), c["qe"], 0.0).astype(BF16), keb), 0.0)
                  .astype(BF16) for hh in range(2)] for c, keb in zip(cm, kebs)]
            oi = [[_dot(ah[hh], v[:, hh * GLA_DV:(hh + 1) * GLA_DV]) for hh in range(2)] for ah, v in zip(a, vs)]
            cs = [jnp.where(bd, _dot_tn(v, c["kend"].astype(BF16)), 0.0) for c, v in zip(cm, vs)]
            for j, (rows, c) in enumerate(zip(rows_list, cm)):
                n = t * GLA_GROUP + j
                o_ref[rows, :] = jnp.concatenate(oi[j], axis=1)
                qe_s[rows, :] = c["qe"].astype(BF16)
                cs_s[n] = cs[j]
                dec_s[n] = jnp.broadcast_to(c["dec"], (8, LANE))
            return carry

        lax.fori_loop(0, nc // GLA_GROUP, local, 0)

        def scan(n, st):
            st_ref[0, n] = st.astype(BF16)
            return dec_s[n][0:1, :] * st + cs_s[n]

        lax.fori_loop(0, nc, scan, jnp.zeros((2 * GLA_DV, LANE), F32))

        def inter(t, carry):
            rows_list = _gla_group_rows(t)
            add = [_dot_nt(qe_s[rows, :], st_ref[0, t * GLA_GROUP + j]) for j, rows in enumerate(rows_list)]
            for rows, av in zip(rows_list, add):
                o_ref[rows, :] = o_ref[rows, :] + av
            return carry

        lax.fori_loop(0, nc // GLA_GROUP, inter, 0)
        comm_after()

    return pl.pallas_call(
        body, name="gla_fwd_comm" if comm else "gla_fwd", grid=(2,),
        out_shape=[jax.ShapeDtypeStruct((s_len, GLA_HEADS * GLA_DV), F32),
                   jax.ShapeDtypeStruct((2, nc, 2 * GLA_DV, LANE), BF16)] + (_comm_out_shapes(*comm) if comm else []),
        in_specs=[pl.BlockSpec((s_len, LANE), lambda g: (0, COL_QA // LANE + g)),
                  pl.BlockSpec((s_len, LANE), lambda g: (0, COL_KA // LANE + g)),
                  pl.BlockSpec((s_len, 2 * GLA_DV), lambda g: (0, (COL_VA - NP_F32) // (2 * GLA_DV) + g)),
                  pl.BlockSpec((s_len, LANE), lambda g: (0, (COL_LR - NP_F32) // LANE)),
                  pl.BlockSpec((None, LANE, LANE), lambda g: (layer, 0, g)),
                  pl.BlockSpec((None, 1, LANE), lambda g: (layer, 0, g))] + [ANY] * ncomm,
        out_specs=[pl.BlockSpec((s_len, 2 * GLA_DV), lambda g: (0, g)),
                   pl.BlockSpec((1, nc, 2 * GLA_DV, LANE), lambda g: (g, 0, 0, 0))] + [ANY] * ncomm,
        scratch_shapes=[pltpu.VMEM((s_len, LANE), BF16), pltpu.VMEM((nc, 2 * GLA_DV, LANE), F32),
                        pltpu.VMEM((nc, 8, LANE), F32)] + (_comm_scratch(ncomm) if comm else []),
        compiler_params=_params(("arbitrary",), 56),
    )(pf, pf, pb, pb, wgu, bgu.reshape(bgu.shape[0], 1, GU_COLS), *(comm[1] if comm else []))


def _rope_tables(s_len):
    inv_freq = ROPE_THETA ** (-jnp.arange(0, DIL_HD, 2, dtype=F32) / DIL_HD)
    ang = jnp.arange(s_len, dtype=F32)[:, None] * inv_freq[None, :]
    cos, sin = jnp.cos(ang), jnp.sin(ang)
    return jnp.concatenate([cos, cos], axis=1), jnp.concatenate([-sin, sin], axis=1)


def _rope(xv, cos, sin_signed):
    return xv * cos + pltpu.roll(xv, DIL_HD // 2, 1) * sin_signed


DIL_GROUP = 8


def _dil_pair_block(i, half, d, nblk, group=DIL_GROUP):
    nb = nblk // d
    j = i + half * (nblk // group)
    if nb >= 2 * group:
        r, n = j % d, j // d
    else:
        r, n = j // nb, j % nb
    kb = jnp.maximum(n - 1, 0)
    qs = r + d * DIL_BLOCK * n
    ks = r + d * DIL_BLOCK * kb
    return qs, ks, jnp.minimum(n, 1)


def _dil_fill_bias(bias):
    qi = lax.broadcasted_iota(jnp.int32, (DIL_BLOCK, 2 * DIL_BLOCK), 0)
    kj = lax.broadcasted_iota(jnp.int32, (DIL_BLOCK, 2 * DIL_BLOCK), 1)
    for sel in range(2):
        dist = qi - kj + DIL_BLOCK * sel
        bias[sel] = jnp.where((dist >= 0) & (dist <= DIL_BLOCK), 0.0, MASK_VALUE)


def _strided(start, size, d):
    return pl.ds(start, size) if d == 1 else pl.ds(start, size, stride=d)


def _comm_hooks(comm, cin, cout, csem, steps=DIL_HEADS):
    def before():
        if comm:
            @pl.when(pl.program_id(0) == 0)
            def _():
                _comm_run(comm[0], ("start",), cin, cout, *csem)

            if comm[0] == "gather":
                @pl.when(pl.program_id(0) == steps - 1)
                def _():
                    _comm_run(comm[0], ("forward",), cin, cout, *csem)

            if comm[0] == "pairsum_exchange":
                @pl.when(pl.program_id(0) == (1 if steps <= 4 else 2))
                def _():
                    _comm_run(comm[0], ("reduce", "send"), cin, cout, *csem)

    def after():
        if comm:
            @pl.when(pl.program_id(0) == steps - 1)
            def _():
                _comm_run(comm[0], ("finish",), cin, cout, *csem)

    return before, after


def _dil_fwd(pf, pb, cos, sin_signed, comm=None):
    s_len = pf.shape[0]
    nblk = s_len // DIL_BLOCK
    prep_rows = 256
    scale = DIL_HD ** -0.5
    nc = len(comm[1]) if comm else 0

    def body(*refs):
        q_ref, k_ref, v_ref, cos_ref, sin_ref = refs[:5]
        cin, (o_ref, lse_ref), cout = refs[5:5 + nc], refs[5 + nc:7 + nc], refs[7 + nc:7 + 2 * nc]
        qf, kf, vf, o0, o1, o2, l0, l1, l2, bias = refs[7 + 2 * nc:17 + 2 * nc]
        comm_before, comm_after = _comm_hooks(comm, cin, cout, refs[17 + 2 * nc:])
        comm_before()
        _dil_fill_bias(bias)

        def prep(t, carry):
            rows = pl.ds(pl.multiple_of(t * prep_rows, prep_rows), prep_rows)
            cs, sn = cos_ref[rows, :], sin_ref[rows, :]
            qf[rows, :] = _rope(q_ref[rows, :], cs, sn)
            kf[rows, :] = _rope(k_ref[rows, :], cs, sn)
            vf[rows, :] = v_ref[rows, :].astype(F32)
            return carry

        lax.fori_loop(0, s_len // prep_rows, prep, 0)
        for d, o_p, l_p in zip(DIL_DILATIONS, (o0, o1, o2), (l0, l1, l2)):
            if nblk // d == 2:
                units = DIL_GROUP // 2

                def whole(i, carry, d=d, o_p=o_p, l_p=l_p, units=units):
                    rows = [_strided(i + u * (d // units), 2 * DIL_BLOCK, d) for u in range(units)]
                    ld = [(qf[rw, :].astype(BF16), kf[rw, :].astype(BF16), vf[rw, :].astype(BF16)) for rw in rows]
                    both = bias[...].reshape(2 * DIL_BLOCK, 2 * DIL_BLOCK)
                    s = [_dot_nt(qb, kk) * scale + both for qb, kk, _ in ld]
                    m = [jnp.max(sv, axis=-1, keepdims=True) for sv in s]
                    p = [jnp.exp(sv - mv) for sv, mv in zip(s, m)]
                    den = [jnp.sum(pv, axis=-1, keepdims=True) for pv in p]
                    r = [_dot(pv.astype(BF16), vv) for pv, (_, _, vv) in zip(p, ld)]
                    for rv, dv, mv, rw in zip(r, den, m, rows):
                        o_p[rw, :] = rv / dv
                        l_p[rw, :] = jnp.broadcast_to(mv + jnp.log(dv), (2 * DIL_BLOCK, DIL_HD))
                    return carry

                lax.fori_loop(0, d // units, whole, 0)
                continue

            def pair(i, carry, d=d, o_p=o_p, l_p=l_p):
                idx = [_dil_pair_block(i, half, d, nblk, DIL_GROUP) for half in range(DIL_GROUP)]
                ld = [(qf[_strided(qs, DIL_BLOCK, d), :].astype(BF16),
                       kf[_strided(ks, 2 * DIL_BLOCK, d), :].astype(BF16),
                       vf[_strided(ks, 2 * DIL_BLOCK, d), :].astype(BF16)) for qs, ks, _ in idx]
                s = [_dot_nt(qb, kk) * scale + bias[sel] for (qb, kk, _), (_, _, sel) in zip(ld, idx)]
                m = [jnp.max(sv, axis=-1, keepdims=True) for sv in s]
                p = [jnp.exp(sv - mv) for sv, mv in zip(s, m)]
                den = [jnp.sum(pv, axis=-1, keepdims=True) for pv in p]
                r = [_dot(pv.astype(BF16), vv) for pv, (_, _, vv) in zip(p, ld)]
                for rv, dv, mv, (qs, _, _) in zip(r, den, m, idx):
                    o_p[_strided(qs, DIL_BLOCK, d), :] = rv / dv
                    l_p[_strided(qs, DIL_BLOCK, d), :] = jnp.broadcast_to(mv + jnp.log(dv), (DIL_BLOCK, DIL_HD))
                return carry

            lax.fori_loop(0, nblk // DIL_GROUP, pair, 0)

        def comb(t, carry):
            rows = pl.ds(pl.multiple_of(t * prep_rows, prep_rows), prep_rows)
            a0, a1, a2 = l0[rows, :], l1[rows, :], l2[rows, :]
            m = jnp.maximum(jnp.maximum(a0, a1), a2)
            e0, e1, e2 = jnp.exp(a0 - m), jnp.exp(a1 - m), jnp.exp(a2 - m)
            tot = e0 + e1 + e2
            o_ref[rows, :] = (e0 * o0[rows, :] + e1 * o1[rows, :] + e2 * o2[rows, :]) / tot
            lse_ref[rows, :] = m + jnp.log(tot)
            return carry

        lax.fori_loop(0, s_len // prep_rows, comb, 0)
        comm_after()

    head = lambda base: pl.BlockSpec((s_len, DIL_HD), lambda h: (0, base // DIL_HD + h))
    table = pl.BlockSpec((s_len, DIL_HD), lambda h: (0, 0))
    out = pl.BlockSpec((s_len, DIL_HD), lambda h: (0, h))
    shp = jax.ShapeDtypeStruct((s_len, DIL_HEADS * DIL_HD), F32)
    return pl.pallas_call(
        body, name="dil_fwd_comm" if comm else "dil_fwd", grid=(DIL_HEADS,),
        out_shape=[shp, shp] + (_comm_out_shapes(*comm) if comm else []),
        in_specs=[head(COL_QB), head(COL_KB), head(COL_VB - NP_F32), table, table] + [ANY] * nc,
        out_specs=[out, out] + [ANY] * nc,
        scratch_shapes=[pltpu.VMEM((s_len, DIL_HD), F32) for _ in range(9)]
        + [pltpu.VMEM((2, DIL_BLOCK, 2 * DIL_BLOCK), F32)] + (_comm_scratch(nc) if comm else []),
        compiler_params=_params(("arbitrary",), 56),
    )(pf, pf, pb, cos, sin_signed, *(comm[1] if comm else []))


def _silu_and_grad(z):
    sg = _sigmoid(z)
    return z * sg, sg * (1.0 + z * (1.0 - sg))


def _post_fwd(o_a, o_b, pf, g_heads, w_out, x, gate, g_post, target=None, ts=256):
    s_len = x.shape[0]
    half = GLA_HEADS * GLA_DV
    last = target is not None

    def body(*refs):
        oa_ref, ob_ref, z_ref, gh_ref, w_ref, x_ref, gate_ref, gp_ref = refs[:8]
        xo_ref, u_ref = refs[8 + last:10 + last]
        y_ref = refs[-1]
        for src, base in ((oa_ref, 0), (ob_ref, half)):
            for hh in range(4):
                lo = hh * LANE
                og = src[:, lo:lo + LANE]
                on = og * lax.rsqrt(jnp.mean(og * og, axis=-1, keepdims=True) + EPS)
                zg = z_ref[:, base + lo:base + lo + LANE].astype(F32)
                y_ref[:, base + lo:base + lo + LANE] = (on * gh_ref[:, base + lo:base + lo + LANE]
                                                        * (zg * _sigmoid(zg))).astype(BF16)
        u = _dot(y_ref[...], w_ref[...])
        u_ref[...] = u.astype(BF16)
        rstd = lax.rsqrt(jnp.mean(u * u, axis=-1, keepdims=True) + EPS)
        x_out = x_ref[...] + gate_ref[...] * (u * rstd * gp_ref[...])
        if last:
            t_ref, loss_ref = refs[8], refs[11]

            @pl.when(pl.program_id(0) == 0)
            def _():
                loss_ref[...] = jnp.zeros_like(loss_ref)

            e = x_out - t_ref[...]
            xo_ref[...] = e * (1.0 / D_MODEL)
            loss_ref[...] += 0.5 * jnp.sum(jnp.mean(e * e, axis=-1, keepdims=True))
        else:
            xo_ref[...] = x_out

    (g_heads, gh_spec), (gate, gate_spec), (g_post, gp_spec) = _rowvec(g_heads), _rowvec(gate), _rowvec(g_post)
    tile = pl.BlockSpec((ts, D_MODEL), lambda i: (i, 0))
    halft = pl.BlockSpec((ts, half), lambda i: (i, 0))
    return pl.pallas_call(
        body, name="post_fwd_loss" if last else "post_fwd", grid=(s_len // ts,),
        out_shape=[jax.ShapeDtypeStruct((s_len, D_MODEL), F32), jax.ShapeDtypeStruct((s_len, D_MODEL), BF16)]
        + ([jax.ShapeDtypeStruct((8, LANE), F32)] if last else []),
        in_specs=[halft, halft, tile, gh_spec, pl.BlockSpec((D_MODEL, D_MODEL), lambda i: (0, 0)), tile, gate_spec,
                  gp_spec] + ([tile] if last else []),
        out_specs=[tile, tile] + ([pl.BlockSpec((8, LANE), lambda i: (0, 0))] if last else []),
        scratch_shapes=[pltpu.VMEM((ts, D_MODEL), BF16)],
        compiler_params=_params(("arbitrary",), 40),
    )(o_a, o_b, pf, g_heads, w_out, x, gate, g_post, *([target] if last else []))


def _post_bwd(dxo, u, gate, g_post, w_out, o_a, o_b, pf, g_heads, ts=512):
    s_len = dxo.shape[0]
    half = GLA_HEADS * GLA_DV
    steps = s_len // ts

    def body(dx_ref, u_ref, gate_ref, gp_ref, w_ref, oa_ref, ob_ref, z_ref, gh_ref, do_ref, dz_ref, sums_ref, gw_ref,
             y_s, acc):
        @pl.when(pl.program_id(0) == 0)
        def _():
            sums_ref[...] = jnp.zeros_like(sums_ref)
            acc[...] = jnp.zeros_like(acc)

        dx = dx_ref[...]
        u = u_ref[...].astype(F32)
        rstd = lax.rsqrt(jnp.mean(u * u, axis=-1, keepdims=True) + EPS)
        un = u * rstd
        sums_ref[0:1, :] += jnp.sum(dx * (un * gp_ref[...]), axis=0, keepdims=True)
        drn = dx * gate_ref[...]
        sums_ref[1:2, :] += jnp.sum(drn * un, axis=0, keepdims=True)
        dun = drn * gp_ref[...]
        du = rstd * (dun - un * jnp.mean(dun * un, axis=-1, keepdims=True))
        dub = du.astype(BF16)
        dy = _dot_nt(dub, w_ref[...])
        for src, base in ((oa_ref, 0), (ob_ref, half)):
            for hh in range(4):
                lo = base + hh * LANE
                og = src[:, hh * LANE:(hh + 1) * LANE]
                rs = lax.rsqrt(jnp.mean(og * og, axis=-1, keepdims=True) + EPS)
                on = og * rs
                zg = z_ref[:, lo:lo + LANE].astype(F32)
                sz, dsz = _silu_and_grad(zg)
                gg = gh_ref[:, lo:lo + LANE]
                dyg = dy[:, lo:lo + LANE]
                y_s[:, lo:lo + LANE] = (on * gg * sz).astype(BF16)
                sums_ref[2:3, lo:lo + LANE] += jnp.sum(dyg * sz * on, axis=0, keepdims=True)
                dz_ref[:, lo:lo + LANE] = (dyg * on * gg * dsz).astype(BF16)
                don = dyg * gg * sz
                do_ref[:, lo:lo + LANE] = (rs * (don - on * jnp.mean(don * on, axis=-1, keepdims=True))).astype(BF16)
        acc[...] += _dot_tn(y_s[...], dub)

        @pl.when(pl.program_id(0) == steps - 1)
        def _():
            gw_ref[...] = acc[...].astype(BF16)

    (g_heads, gh_spec), (gate, gate_spec), (g_post, gp_spec) = _rowvec(g_heads), _rowvec(gate), _rowvec(g_post)
    tile = pl.BlockSpec((ts, D_MODEL), lambda i: (i, 0))
    halft = pl.BlockSpec((ts, half), lambda i: (i, 0))
    whole = pl.BlockSpec((D_MODEL, D_MODEL), lambda i: (0, 0))
    return pl.pallas_call(
        body, name="post_bwd", grid=(steps,),
        out_shape=(jax.ShapeDtypeStruct((s_len, D_MODEL), BF16), jax.ShapeDtypeStruct((s_len, D_MODEL), BF16),
                   jax.ShapeDtypeStruct((8, D_MODEL), F32), jax.ShapeDtypeStruct((D_MODEL, D_MODEL), BF16)),
        in_specs=[tile, tile, gate_spec, gp_spec, whole, halft, halft, tile, gh_spec],
        out_specs=(tile, tile, pl.BlockSpec((8, D_MODEL), lambda i: (0, 0)), whole),
        scratch_shapes=[pltpu.VMEM((ts, D_MODEL), BF16), pltpu.VMEM((D_MODEL, D_MODEL), F32)],
        compiler_params=_params(("arbitrary",), 48),
    )(dxo, u, gate, g_post, w_out, o_a, o_b, pf, g_heads)


def _gla_bwd(pf, pb, wgu, bgu, layer, states, do, comm=None):
    s_len = pf.shape[0]
    nc = s_len // GLA_CHUNK
    c = GLA_CHUNK
    n_cin, c_shapes, c_scratch = _comm_plumbing(comm)

    def body(*refs):
        ((q_ref, k_ref, v_ref, lr_ref, wgu_ref, bgu_ref, st_ref, do_ref),
         (dq_ref, dk_ref, dv_ref, dlr_ref, dwgu_ref, dbgu_ref), (ds_s, dec_s, dw_acc, db_acc),
         cin, cout, csem) = _split_refs(refs, 8, 6, 4, comm)
        comm_before, comm_after = _comm_hooks(comm, cin, cout, csem, steps=2)
        comm_before()
        dw_acc[...] = jnp.zeros_like(dw_acc)
        db_acc[...] = jnp.zeros_like(db_acc)
        bd = _state_block_mask()
        last_row = lax.broadcasted_iota(jnp.int32, (c, LANE), 0) == c - 1

        def local(t, carry):
            rows_list = _gla_group_rows(t)
            cm, _, _ = _gla_chunks_common(q_ref, k_ref, lr_ref, wgu_ref, bgu_ref, rows_list)
            loc = [jnp.where(bd, _dot_tn(do_ref[rows, :], cc["qe"].astype(BF16)), 0.0)
                   for rows, cc in zip(rows_list, cm)]
            for j, cc in enumerate(cm):
                ds_s[t * GLA_GROUP + j] = loc[j]
                dec_s[t * GLA_GROUP + j] = jnp.broadcast_to(cc["dec"], (8, LANE))
            return carry

        lax.fori_loop(0, nc // GLA_GROUP, local, 0)

        def scan(t, dst):
            n = nc - 1 - t
            loc = ds_s[n]
            ds_s[n] = dst
            return dec_s[n][0:1, :] * dst + loc

        lax.fori_loop(0, nc, scan, jnp.zeros((2 * GLA_DV, LANE), F32))

        def rest(t, carry):
            rows_list = _gla_group_rows(t)
            cm, ri, ci = _gla_chunks_common(q_ref, k_ref, lr_ref, wgu_ref, bgu_ref, rows_list)
            ns = [t * GLA_GROUP + j for j in range(GLA_GROUP)]
            vs = [v_ref[rows, :] for rows in rows_list]
            dobs = [do_ref[rows, :] for rows in rows_list]
            stbs = [st_ref[0, n] for n in ns]
            dsts = [ds_s[n] for n in ns]
            dstbs = [d.astype(BF16) for d in dsts]
            qebs = [cc["qe"].astype(BF16) for cc in cm]
            kebs = [cc["ke"].astype(BF16) for cc in cm]
            kendbs = [cc["kend"].astype(BF16) for cc in cm]
            hms = [_head_lane_mask(hh) for hh in range(2)]
            qehs = [[jnp.where(hm, cc["qe"], 0.0).astype(BF16) for hm in hms] for cc in cm]
            kehs = [[jnp.where(hm, cc["ke"], 0.0).astype(BF16) for hm in hms] for cc in cm]
            heads = lambda x: [x[:, hh * GLA_DV:(hh + 1) * GLA_DV] for hh in range(2)]
            vhs, dohs = [heads(v) for v in vs], [heads(d) for d in dobs]

            dqe0 = [_dot(dob, stb) for dob, stb in zip(dobs, stbs)]
            dkend = [_dot(v, dstb) for v, dstb in zip(vs, dstbs)]
            dv0 = [_dot_nt(kb, dstb) for kb, dstb in zip(kendbs, dstbs)]
            a_t = [[jnp.where(ci >= ri, _dot_nt(kehs[j][hh], qebs[j]), 0.0).astype(BF16) for hh in range(2)]
                   for j in range(GLA_GROUP)]
            da = [[jnp.where(ri >= ci, _dot_nt(dohs[j][hh], vhs[j][hh]), 0.0).astype(BF16) for hh in range(2)]
                  for j in range(GLA_GROUP)]
            da_t = [[jnp.where(ci >= ri, _dot_nt(vhs[j][hh], dohs[j][hh]), 0.0).astype(BF16) for hh in range(2)]
                    for j in range(GLA_GROUP)]
            dv1 = [[_dot(a_t[j][hh], dohs[j][hh]) for hh in range(2)] for j in range(GLA_GROUP)]
            dqe1 = [[_dot(da[j][hh], kebs[j]) for hh in range(2)] for j in range(GLA_GROUP)]
            dke1 = [[_dot(da_t[j][hh], qehs[j][hh]) for hh in range(2)] for j in range(GLA_GROUP)]

            dbs, dzs = [], []
            for j, (rows, cc) in enumerate(zip(rows_list, cm)):
                qe, ke, kend, b, bl = cc["qe"], cc["ke"], cc["kend"], cc["b"], cc["bl"]
                dqe = dqe0[j] + jnp.where(hms[0], dqe1[j][0], 0.0) + jnp.where(hms[1], dqe1[j][1], 0.0)
                dke = jnp.where(hms[0], dke1[j][0], 0.0) + jnp.where(hms[1], dke1[j][1], 0.0)
                dv_ref[rows, :] = (dv0[j] + jnp.concatenate(dv1[j], axis=1)).astype(BF16)
                dq_ref[rows, :] = (dqe * jnp.exp(b) * (GLA_DK ** -0.5)).astype(BF16)
                dk_ref[rows, :] = (dke * jnp.exp(-b) + dkend[j] * jnp.exp(bl - b)).astype(BF16)
                ddec = jnp.sum(dsts[j] * stbs[j].astype(F32), axis=0, keepdims=True)
                dbl = jnp.sum(dkend[j] * kend, axis=0, keepdims=True) + ddec * cc["dec"]
                dbs.append(dqe * qe - dke * ke - dkend[j] * kend + jnp.where(last_row, dbl, 0.0))
            triu = (ci >= ri).astype(F32)
            dlas = [jnp.dot(triu, db, precision=lax.Precision.HIGHEST, preferred_element_type=F32) for db in dbs]
            dzs = [dla * (1.0 / GLA_TAU) * _sigmoid(-cc["z"]) for dla, cc in zip(dlas, cm)]
            dzbs = [dz.astype(BF16) for dz in dzs]
            dlrs = [_dot_nt(dzb, wgu_ref[...]) for dzb in dzbs]
            dws = [_dot_tn(lr_ref[rows, :], dzb) for rows, dzb in zip(rows_list, dzbs)]
            for rows, dlr in zip(rows_list, dlrs):
                dlr_ref[0, rows, :] = dlr
            dw_acc[...] += functools.reduce(lambda x, y: x + y, dws)
            db_acc[0:1, :] += jnp.sum(functools.reduce(lambda x, y: x + y, dzs), axis=0, keepdims=True)
            return carry

        lax.fori_loop(0, nc // GLA_GROUP, rest, 0)
        dwgu_ref[...] = dw_acc[...]
        dbgu_ref[...] = db_acc[...]
        comm_after()

    pair = pl.BlockSpec((s_len, LANE), lambda g: (0, g))
    return pl.pallas_call(
        body, name="gla_bwd_comm" if comm else "gla_bwd", grid=(2,),
        out_shape=[jax.ShapeDtypeStruct((s_len, GU_COLS), BF16), jax.ShapeDtypeStruct((s_len, GU_COLS), BF16),
                   jax.ShapeDtypeStruct((s_len, GLA_HEADS * GLA_DV), BF16),
                   jax.ShapeDtypeStruct((2, s_len, LANE), F32),
                   jax.ShapeDtypeStruct((LANE, GU_COLS), F32), jax.ShapeDtypeStruct((8, GU_COLS), F32)] + c_shapes,
        in_specs=[pl.BlockSpec((s_len, LANE), lambda g: (0, COL_QA // LANE + g)),
                  pl.BlockSpec((s_len, LANE), lambda g: (0, COL_KA // LANE + g)),
                  pl.BlockSpec((s_len, 2 * GLA_DV), lambda g: (0, (COL_VA - NP_F32) // (2 * GLA_DV) + g)),
                  pl.BlockSpec((s_len, LANE), lambda g: (0, (COL_LR - NP_F32) // LANE)),
                  pl.BlockSpec((None, LANE, LANE), lambda g: (layer, 0, g)),
                  pl.BlockSpec((None, 1, LANE), lambda g: (layer, 0, g)),
                  pl.BlockSpec((1, nc, 2 * GLA_DV, LANE), lambda g: (g, 0, 0, 0)),
                  pl.BlockSpec((s_len, 2 * GLA_DV), lambda g: (0, g))] + [ANY] * n_cin,
        out_specs=[pair, pair, pl.BlockSpec((s_len, 2 * GLA_DV), lambda g: (0, g)),
                   pl.BlockSpec((1, s_len, LANE), lambda g: (g, 0, 0)),
                   pl.BlockSpec((LANE, LANE), lambda g: (0, g)), pl.BlockSpec((8, LANE), lambda g: (0, g))]
        + [ANY] * len(c_shapes),
        scratch_shapes=[pltpu.VMEM((nc, 2 * GLA_DV, LANE), F32), pltpu.VMEM((nc, 8, LANE), F32),
                        pltpu.VMEM((LANE, LANE), F32), pltpu.VMEM((8, LANE), F32)] + c_scratch,
        compiler_params=_params(("arbitrary",), 56),
    )(pf, pf, pb, pb, wgu, bgu.reshape(bgu.shape[0], 1, GU_COLS), states, do, *(comm[1] if comm else []))


def _dil_bwd(pf, pb, cos, sin_signed, do, o_b, lse, comm=None):
    s_len = pf.shape[0]
    nblk = s_len // DIL_BLOCK
    prep_rows = 256
    scale = DIL_HD ** -0.5
    nc = len(comm[1]) if comm else 0

    def body(*refs):
        ((q_ref, k_ref, v_ref, cos_ref, sin_ref, do_ref, o_ref, lse_ref), (dq_ref, dk_ref, dv_ref),
         (qf, kf, vf, dof, dl, dqa, dka, dva, bias), cin, cout, csem) = _split_refs(refs, 8, 3, 9, comm)
        comm_before, comm_after = _comm_hooks(comm, cin, cout, csem)
        comm_before()
        _dil_fill_bias(bias)

        def prep(t, carry):
            rows = pl.ds(pl.multiple_of(t * prep_rows, prep_rows), prep_rows)
            cs, sn = cos_ref[rows, :], sin_ref[rows, :]
            qf[rows, :] = _rope(q_ref[rows, :], cs, sn) * scale
            kf[rows, :] = _rope(k_ref[rows, :], cs, sn)
            vf[rows, :] = v_ref[rows, :].astype(F32)
            dov = do_ref[rows, :].astype(F32)
            dof[rows, :] = dov
            dl[rows, :] = jnp.broadcast_to(jnp.sum(dov * o_ref[rows, :], axis=-1, keepdims=True), (prep_rows, DIL_HD))
            zero = jnp.zeros((prep_rows, DIL_HD), F32)
            dqa[rows, :] = zero
            dka[rows, :] = zero
            dva[rows, :] = zero
            return carry

        lax.fori_loop(0, s_len // prep_rows, prep, 0)

        for d in DIL_DILATIONS:
            if nblk // d == 2:
                units = DIL_GROUP // 2

                def whole(i, carry, d=d, units=units):
                    rows = [_strided(i + u * (d // units), 2 * DIL_BLOCK, d) for u in range(units)]
                    ld = [(qf[rw, :].astype(BF16), kf[rw, :].astype(BF16), vf[rw, :].astype(BF16),
                           dof[rw, :].astype(BF16)) for rw in rows]
                    both = bias[...].reshape(2 * DIL_BLOCK, 2 * DIL_BLOCK)
                    s = [_dot_nt(qb, kk) + both for qb, kk, _, _ in ld]
                    dp = [_dot_nt(dob, vv) for _, _, vv, dob in ld]
                    p = [jnp.exp(sv - lse_ref[rw, :][:, 0:1]) for sv, rw in zip(s, rows)]
                    ds = [(pv * (dpv - dl[rw, :][:, 0:1])).astype(BF16) for pv, dpv, rw in zip(p, dp, rows)]
                    pb = [pv.astype(BF16) for pv in p]
                    gq = [_dot(dsv, kk) for dsv, (_, kk, _, _) in zip(ds, ld)]
                    gk = [_dot_tn(dsv, qb) for dsv, (qb, _, _, _) in zip(ds, ld)]
                    gv = [_dot_tn(pv, dob) for pv, (_, _, _, dob) in zip(pb, ld)]
                    for rw, a, b, c in zip(rows, gq, gk, gv):
                        dqa[rw, :] += a
                        dka[rw, :] += b
                        dva[rw, :] += c
                    return carry

                lax.fori_loop(0, d // units, whole, 0)
                continue

            def pair(i, carry, d=d):
                idx = [_dil_pair_block(i, half, d, nblk) for half in range(DIL_GROUP)]
                rows = [(_strided(qs, DIL_BLOCK, d), _strided(ks, 2 * DIL_BLOCK, d)) for qs, ks, _ in idx]
                ld = [(qf[qr, :].astype(BF16), kf[kr, :].astype(BF16), vf[kr, :].astype(BF16),
                       dof[qr, :].astype(BF16)) for qr, kr in rows]
                s = [_dot_nt(qb, kk) + bias[sel] for (qb, kk, _, _), (_, _, sel) in zip(ld, idx)]
                dp = [_dot_nt(dob, vv) for _, _, vv, dob in ld]
                p = [jnp.exp(sv - lse_ref[qr, :][:, 0:1]) for sv, (qr, _) in zip(s, rows)]
                ds = [(pv * (dpv - dl[qr, :][:, 0:1])).astype(BF16) for pv, dpv, (qr, _) in zip(p, dp, rows)]
                pb = [pv.astype(BF16) for pv in p]
                gq = [_dot(dsv, kk) for dsv, (_, kk, _, _) in zip(ds, ld)]
                gk = [_dot_tn(dsv, qb) for dsv, (qb, _, _, _) in zip(ds, ld)]
                gv = [_dot_tn(pv, dob) for pv, (_, _, _, dob) in zip(pb, ld)]
                for (qr, kr), a, b, c in zip(rows, gq, gk, gv):
                    dqa[qr, :] += a
                    dka[kr, :] += b
                    dva[kr, :] += c
                return carry

            lax.fori_loop(0, nblk // DIL_GROUP, pair, 0)

        def fin(t, carry):
            rows = pl.ds(pl.multiple_of(t * prep_rows, prep_rows), prep_rows)
            cs, sn = cos_ref[rows, :], sin_ref[rows, :]
            gq, gk = dqa[rows, :] * scale, dka[rows, :]
            dq_ref[rows, :] = (gq * cs - pltpu.roll(gq, DIL_HD // 2, 1) * sn).astype(BF16)
            dk_ref[rows, :] = (gk * cs - pltpu.roll(gk, DIL_HD // 2, 1) * sn).astype(BF16)
            dv_ref[rows, :] = dva[rows, :].astype(BF16)
            return carry

        lax.fori_loop(0, s_len // prep_rows, fin, 0)
        comm_after()

    head = lambda base: pl.BlockSpec((s_len, DIL_HD), lambda h: (0, base // DIL_HD + h))
    table = pl.BlockSpec((s_len, DIL_HD), lambda h: (0, 0))
    out = pl.BlockSpec((s_len, DIL_HD), lambda h: (0, h))
    shp = jax.ShapeDtypeStruct((s_len, DIL_HEADS * DIL_HD), BF16)
    return pl.pallas_call(
        body, name="dil_bwd_comm" if comm else "dil_bwd", grid=(DIL_HEADS,),
        out_shape=[shp, shp, shp] + (_comm_out_shapes(*comm) if comm else []),
        in_specs=[head(COL_QB), head(COL_KB), head(COL_VB - NP_F32), table, table,
                  pl.BlockSpec((s_len, DIL_HD), lambda h: (0, DIL_HEADS + h)), out, out] + [ANY] * nc,
        out_specs=[out, out, out] + [ANY] * len(_comm_plumbing(comm)[1]),
        scratch_shapes=[pltpu.VMEM((s_len, DIL_HD), F32) for _ in range(8)]
        + [pltpu.VMEM((2, DIL_BLOCK, 2 * DIL_BLOCK), F32)] + (_comm_scratch(nc) if comm else []),
        compiler_params=_params(("arbitrary",), 56),
    )(pf, pf, pb, cos, sin_signed, do, o_b, lse, *(comm[1] if comm else []))


_PIECES = ((COL_Z, 1024), (COL_QA, 256), (COL_KA, 256), (COL_QB, 512), (COL_KB, 512), (COL_VA, 512), (COL_VB, 512),
           (COL_LR, 128))


def _in_bwd(pieces, w_new, x, dxo, g_pre, scale, comm=None, ts=256):
    s_len = x.shape[0]
    nc = len(comm[1]) if comm else 0
    nco = len(_comm_out_shapes(*comm)) if comm else 0
    npc = len(_PIECES)

    def body(*refs):
        p_refs = refs[:npc]
        w_ref, x_ref, dxo_ref, g_ref, sc_ref = refs[npc:npc + 5]
        cin, (dx_ref, sums_ref), cout = (refs[npc + 5:npc + 5 + nc], refs[npc + 5 + nc:npc + 7 + nc],
                                         refs[npc + 7 + nc:npc + 7 + nc + nco])
        comm_before, comm_after = _comm_hooks(comm, cin, cout, refs[npc + 7 + nc + nco:], steps=s_len // ts)
        comm_before()

        @pl.when(pl.program_id(0) == 0)
        def _():
            sums_ref[...] = jnp.zeros_like(sums_ref)

        dh = jnp.zeros((ts, D_MODEL), F32)
        for p_ref, (col, width) in zip(p_refs, _PIECES):
            dh += _dot_nt(p_ref[...], w_ref[:, col:col + width])
        xv = x_ref[...]
        rstd = lax.rsqrt(jnp.mean(xv * xv, axis=-1, keepdims=True) + EPS)
        xn = xv * rstd
        sums_ref[0:1, :] += jnp.sum(dh, axis=0, keepdims=True)
        sums_ref[1:2, :] += jnp.sum(dh * (xn * g_ref[...]), axis=0, keepdims=True)
        dr = dh * (1.0 + sc_ref[...])
        sums_ref[2:3, :] += jnp.sum(dr * xn, axis=0, keepdims=True)
        dxn = dr * g_ref[...]
        dx_ref[...] = dxo_ref[...] + rstd * (dxn - xn * jnp.mean(dxn * xn, axis=-1, keepdims=True))
        comm_after()

    (g_pre, g_spec), (scale, sc_spec) = _rowvec(g_pre), _rowvec(scale)
    tile = pl.BlockSpec((ts, D_MODEL), lambda i: (i, 0))
    return pl.pallas_call(
        body, name="in_bwd_comm" if comm else "in_bwd", grid=(s_len // ts,),
        out_shape=[jax.ShapeDtypeStruct((s_len, D_MODEL), F32), jax.ShapeDtypeStruct((8, D_MODEL), F32)]
        + (_comm_out_shapes(*comm) if comm else []),
        in_specs=[pl.BlockSpec((ts, width), lambda i: (i, 0)) for _, width in _PIECES]
        + [pl.BlockSpec((D_MODEL, NP), lambda i: (0, 0)), tile, tile, g_spec, sc_spec] + [ANY] * nc,
        out_specs=[tile, pl.BlockSpec((8, D_MODEL), lambda i: (0, 0))] + [ANY] * nco,
        scratch_shapes=_comm_scratch(nc) if comm else [],
        compiler_params=_params(("arbitrary",), 56),
    )(*pieces, w_new, x, dxo, g_pre, scale, *(comm[1] if comm else []))


def _w_in_to_kernel(gathered, comm=None, tr=128):
    n_cin, c_shapes, c_scratch = _comm_plumbing(comm)

    def body(*refs):
        (g_ref,), (o_ref,), _, cin, cout, csem = _split_refs(refs, 1, 1, 0, comm)
        comm_before, comm_after = _comm_hooks(comm, cin, cout, csem, steps=D_MODEL // tr)
        comm_before()
        cols = jnp.concatenate([g_ref[k].astype(F32) for k in range(N_DEV)], axis=1)
        pad = jnp.zeros((tr, LANE - GLA_LOWRANK), F32)
        o_ref[...] = jnp.concatenate(
            [cols[:, 1024:1536], cols[:, 3088:3600], cols[:, 0:512], cols[:, 1552:2576], cols[:, 512:1024],
             cols[:, 2576:3088], cols[:, 1536:1552], pad], axis=1).astype(BF16)
        comm_after()

    return pl.pallas_call(
        body, name="w_in_to_kernel_comm" if comm else "w_in_to_kernel", grid=(D_MODEL // tr,),
        out_shape=[jax.ShapeDtypeStruct((D_MODEL, NP), BF16)] + c_shapes,
        in_specs=[pl.BlockSpec((N_DEV, tr, W_IN_SHARD), lambda i: (0, i, 0))] + [ANY] * n_cin,
        out_specs=[pl.BlockSpec((tr, NP), lambda i: (i, 0))] + [ANY] * len(c_shapes),
        scratch_shapes=c_scratch,
        compiler_params=_params(("arbitrary",)),
    )(gathered, *(comm[1] if comm else []))


def _grad_w_in(h, pieces, ts=512, tr=128):
    s_len = h.shape[0]
    steps = s_len // ts

    def body(*refs):
        h_ref, p_refs = refs[0], refs[1:1 + len(_PIECES)]
        o_ref, acc = refs[1 + len(_PIECES):]

        @pl.when(pl.program_id(0) == 0)
        def _():
            acc[...] = jnp.zeros_like(acc)

        hv = h_ref[...]
        for p_ref, (col, width) in zip(p_refs, _PIECES):
            acc[:, col:col + width] += _dot_tn(hv, p_ref[...])

        @pl.when(pl.program_id(0) == steps - 1)
        def _():
            def rows_out(t, carry):
                rows = pl.ds(pl.multiple_of(t * tr, tr), tr)
                g = acc[rows, :]
                cols = jnp.concatenate(
                    [g[:, COL_QA:COL_QB], g[:, COL_VA:COL_VB], g[:, 0:512], g[:, COL_LR:COL_LR + GLA_LOWRANK],
                     g[:, COL_QB:COL_VA], g[:, COL_VB:COL_LR], g[:, 512:1024]], axis=1)
                for k in range(N_DEV):
                    o_ref[k, rows, :] = cols[:, W_IN_SHARD * k:W_IN_SHARD * (k + 1)].astype(BF16)
                return carry

            lax.fori_loop(0, D_MODEL // tr, rows_out, 0)

    return pl.pallas_call(
        body, name="grad_w_in", grid=(steps,),
        out_shape=jax.ShapeDtypeStruct((N_DEV, D_MODEL, W_IN_SHARD), BF16),
        in_specs=[pl.BlockSpec((ts, D_MODEL), lambda i: (i, 0))]
        + [pl.BlockSpec((ts, width), lambda i: (i, 0)) for _, width in _PIECES],
        out_specs=pl.BlockSpec((N_DEV, D_MODEL, W_IN_SHARD), lambda i: (0, 0, 0)),
        scratch_shapes=[pltpu.VMEM((D_MODEL, NP), F32)],
        compiler_params=_params(("arbitrary",), 56),
    )(h, *pieces)


def _adam_math(w, g, m, v):
    m = ADAM_B1 * m + (1.0 - ADAM_B1) * g
    v = ADAM_B2 * v + (1.0 - ADAM_B2) * (g * g)
    m_hat = m / (1.0 - ADAM_B1 ** ADAM_STEP)
    v_hat = v / (1.0 - ADAM_B2 ** ADAM_STEP)
    delta = -ADAM_LR * (m_hat / (jnp.sqrt(v_hat) + ADAM_EPS) + ADAM_WD * w)
    return delta, m, v


def _adamw(w, parts, m, v, name, tr):
    r, cdim = w.shape
    n_parts = parts.shape[0]

    def body(w_ref, p_ref, m_ref, v_ref, g_ref, d_ref, nm_ref, nv_ref):
        g = p_ref[0].astype(F32)
        for k in range(1, n_parts):
            g = g + p_ref[k].astype(F32)
        g_ref[...] = g
        d_ref[...], nm_ref[...], nv_ref[...] = _adam_math(w_ref[...], g, m_ref[...], v_ref[...])

    tile = pl.BlockSpec((tr, cdim), lambda i: (i, 0))
    shp = jax.ShapeDtypeStruct((r, cdim), F32)
    return pl.pallas_call(
        body, name=name, grid=(r // tr,), out_shape=(shp, shp, shp, shp),
        in_specs=[tile, pl.BlockSpec((n_parts, tr, cdim), lambda i: (0, i, 0)), tile, tile],
        out_specs=(tile, tile, tile, tile),
        compiler_params=_params(("arbitrary",), 40),
    )(w, parts, m, v)


def _adamw_layers(w, parts, m, v, name, tr):
    n_layers, r, cdim = w.shape

    def body(*refs):
        w_ref, p_refs, (m_ref, v_ref) = refs[0], refs[1:1 + n_layers], refs[1 + n_layers:3 + n_layers]
        g_ref, d_ref, nm_ref, nv_ref = refs[3 + n_layers:]
        for l, p_ref in enumerate(p_refs):
            @pl.when(pl.program_id(0) == l)
            def _(p_ref=p_ref):
                g = p_ref[0].astype(F32)
                for k in range(1, p_ref.shape[0]):
                    g = g + p_ref[k].astype(F32)
                g_ref[0] = g
                d_ref[0], nm_ref[0], nv_ref[0] = _adam_math(w_ref[0], g, m_ref[0], v_ref[0])

    tile = pl.BlockSpec((1, tr, cdim), lambda l, i: (l, i, 0))
    part = lambda own: pl.BlockSpec((parts[own].shape[0], tr, cdim), lambda l, i: (0, jnp.where(l == own, i, 0), 0))
    shp = jax.ShapeDtypeStruct(w.shape, F32)
    return pl.pallas_call(
        body, name=name, grid=(n_layers, r // tr), out_shape=(shp, shp, shp, shp),
        in_specs=[tile] + [part(l) for l in range(n_layers)] + [tile, tile],
        out_specs=(tile, tile, tile, tile),
        compiler_params=_params(("arbitrary", "arbitrary"), 40),
    )(w, *parts, m, v)


def _row(vec, width):
    vec = vec.reshape(1, -1)
    return jnp.pad(vec, ((0, 0), (0, width - vec.shape[1])))


def kernel(x, c, w_ada, b_ada, g_pre, w_in, w_gate_up, b_gate_up, g_gla, g_dil, w_out, g_post, loss_target, m_w_ada, m_b_ada, m_g_pre, m_w_in, m_w_gate_up, m_b_gate_up, m_g_gla, m_g_dil, m_w_out, m_g_post, v_w_ada, v_b_ada, v_g_pre, v_w_in, v_w_gate_up, v_b_gate_up, v_g_gla, v_g_dil, v_w_out, v_g_post):
    px, py, pc = _my_position()
    me = _linear(px, py, pc)
    xs = x[0]
    target = loss_target[0]
    s_len = xs.shape[0]
    assert s_len % (DIL_BLOCK * max(DIL_DILATIONS) * 2) == 0 and xs.shape[1] == D_MODEL

    w_in_b, w_out_b = w_in.astype(BF16), w_out.astype(BF16)
    c_rows, wgu_all, w_in_all = _comm_call(
        "gather", [jnp.pad(c, ((0, 7), (0, 0))), w_gate_up.reshape(DEPTH * GLA_LOWRANK, GU_SHARD), w_in_b[0]],
        "gather_first")
    c_all = c_rows.reshape(N_DEV, 8, D_MODEL)[:, 0]
    mod_part = _mod_fwd(c_all, w_ada)
    w_new, mod_all = _w_in_to_kernel(w_in_all.reshape(N_DEV, D_MODEL, W_IN_SHARD),
                                     comm=("gather", [mod_part.reshape(DEPTH * N_DEV, ADA_SHARD)]))
    mod_all = mod_all.reshape(N_DEV, DEPTH, N_DEV, ADA_SHARD)
    mod_mine = lax.dynamic_index_in_dim(mod_all, me, axis=2, keepdims=False)
    mod = jnp.transpose(mod_mine, (1, 0, 2)).reshape(DEPTH, 3 * D_MODEL) + b_ada
    wgu_full = jnp.transpose(wgu_all.reshape(N_DEV, DEPTH, GLA_LOWRANK, GU_SHARD), (1, 2, 0, 3)).reshape(
        DEPTH, GLA_LOWRANK, GU_COLS)
    wgu_pad = jnp.pad(wgu_full, ((0, 0), (0, LANE - GLA_LOWRANK), (0, 0))).astype(BF16)

    cos, sin_signed = _rope_tables(s_len)
    g_heads = jnp.concatenate([g_gla, g_dil], axis=1)

    saved = []
    xl = xs
    for l in range(DEPTH):
        shift, scale, gate = ((mod, l, k) for k in range(3))
        if l > 0:
            w_new = _w_in_to_kernel(w_in_all.reshape(N_DEV, D_MODEL, W_IN_SHARD))[0]
        pf, pb, h, w_out_l = _prenorm_proj(xl, (g_pre, l, 0), scale, shift, w_new, comm=("gather", [w_out_b[l]]))
        o_a, states = _gla_fwd(pf, pb, wgu_pad, b_gate_up, l)
        if l + 1 < DEPTH:
            o_b, lse, w_in_all = _dil_fwd(pf, pb, cos, sin_signed, comm=("gather", [w_in_b[l + 1]]))
        else:
            o_b, lse = _dil_fwd(pf, pb, cos, sin_signed)
        if l + 1 < DEPTH:
            x_next, u = _post_fwd(o_a, o_b, pf, (g_heads, l, 0), w_out_l, xl, gate, (g_post, l, 0))
        else:
            dx, u, loss_part = _post_fwd(o_a, o_b, pf, (g_heads, l, 0), w_out_l, xl, gate, (g_post, l, 0),
                                         target=target)
        saved.append((xl, scale, gate, w_new, w_out_l, pf, pb, h, o_a, states, o_b, lse, u))
        xl = x_next

    small_rows = []
    gin_slots, gin_parts, gout_parts = None, [None] * DEPTH, [None] * DEPTH
    for l in reversed(range(DEPTH)):
        x_in, scale, gate, w_new, w_out_l, pf, pb, h, o_a, states, o_b, lse, u = saved[l]
        do, dz, sums_post, gout_slots = _post_bwd(dx, u, gate, (g_post, l, 0), w_out_l, o_a, o_b, pf, (g_heads, l, 0))
        dq_a, dk_a, dv_a, dlr2, dwgu, dbgu, arrived = _gla_bwd(pf, pb, wgu_pad, b_gate_up, l, states, do,
                                                               comm=("exchange", [gout_slots]))
        gout_parts[l] = arrived.reshape(N_DEV, OUT_SHARD, D_MODEL)
        if gin_slots is not None:
            dq_b, dk_b, dv_b, arrived, _, _ = _dil_bwd(pf, pb, cos, sin_signed, do, o_b, lse,
                                                       comm=("pairsum_exchange", [gin_slots]))
            gin_parts[l + 1] = arrived.reshape(N_DEV // 2, D_MODEL, W_IN_SHARD)
        else:
            dq_b, dk_b, dv_b = _dil_bwd(pf, pb, cos, sin_signed, do, o_b, lse)
        dlr = (dlr2[0] + dlr2[1]).astype(BF16)
        pieces = (dz, dq_a, dk_a, dq_b, dk_b, dv_a, dv_b, dlr)
        gin_slots = _grad_w_in(h, pieces).reshape(N_DEV * D_MODEL, W_IN_SHARD)
        if l == 0:
            dx, sums_in, arrived, _, _ = _in_bwd(pieces, w_new, x_in, dx, (g_pre, l, 0), scale,
                                                 comm=("pairsum_exchange", [gin_slots]))
            gin_parts[0] = arrived.reshape(N_DEV // 2, D_MODEL, W_IN_SHARD)
        else:
            dx, sums_in = _in_bwd(pieces, w_new, x_in, dx, (g_pre, l, 0), scale)
        dmod = jnp.concatenate([sums_in[0], sums_in[1], sums_post[0]])
        vecs = jnp.concatenate([sums_in[2], sums_post[1], sums_post[2], dbgu[0]])
        small_rows[0:0] = [_row(dmod, 4096), _row(vecs, 4096), _row(dwgu[:GLA_LOWRANK], 4096)]
    grad_x = dx[None]

    flat = lambda a, rows: a.reshape(rows, a.shape[-1])
    r_ada = DEPTH * D_MODEL
    g_w_in, d_w_in, nm_w_in, nv_w_in = _adamw_layers(w_in, gin_parts, m_w_in, v_w_in, "adamw_w_in", 256)
    g_w_out, d_w_out, nm_w_out, nv_w_out = _adamw_layers(w_out, gout_parts, m_w_out, v_w_out, "adamw_w_out", 128)

    small_rows += [_row(loss_part[0, 0:1], 4096), jnp.zeros((1, 4096), F32)]
    small = _all_gather(jnp.concatenate(small_rows, axis=0), "gather_small").reshape(N_DEV, 8, 4096)
    dmod_all = jnp.stack([small[:, 0, :3 * D_MODEL], small[:, 3, :3 * D_MODEL]])
    dmod_cols = lax.dynamic_slice_in_dim(dmod_all, me * ADA_SHARD, ADA_SHARD, axis=2)
    gwa = _w_ada_grad(c_all, dmod_cols).reshape(1, r_ada, ADA_SHARD)
    g_w_ada, d_w_ada, nm_w_ada, nv_w_ada = (
        t.reshape(w_ada.shape) for t in _adamw(flat(w_ada, r_ada), gwa, flat(m_w_ada, r_ada), flat(v_w_ada, r_ada),
                                               "adamw_w_ada", 256))

    where = ((0, 0), (1, 0), (1, 1024), (1, 2048), (1, 2560), (1, 3072))
    replicated = [(b_ada, m_b_ada, v_b_ada), (g_pre, m_g_pre, v_g_pre), (g_post, m_g_post, v_g_post),
                  (g_gla, m_g_gla, v_g_gla), (g_dil, m_g_dil, v_g_dil), (b_gate_up, m_b_gate_up, v_b_gate_up)]
    updated, loss = _adamw_replicated(small, replicated, where, loss_at=(6, 0))
    ((g_b_ada, d_b_ada, nm_b_ada, nv_b_ada), (g_g_pre, d_g_pre, nm_g_pre, nv_g_pre),
     (g_g_post, d_g_post, nm_g_post, nv_g_post), (g_g_gla, d_g_gla, nm_g_gla, nv_g_gla),
     (g_g_dil, d_g_dil, nm_g_dil, nv_g_dil), (g_b_gu, d_b_gu, nm_b_gu, nv_b_gu)) = updated
    gu_parts = jnp.stack([small[:, 2], small[:, 5]], axis=1).reshape(N_DEV, DEPTH, GLA_LOWRANK, GU_COLS)
    gu_parts = lax.dynamic_slice_in_dim(gu_parts, me * GU_SHARD, GU_SHARD, axis=3).reshape(
        N_DEV, DEPTH * GLA_LOWRANK, GU_SHARD)
    r_gu = DEPTH * GLA_LOWRANK
    g_w_gu, d_w_gu, nm_w_gu, nv_w_gu = (
        t.reshape(w_gate_up.shape) for t in _adamw(flat(w_gate_up, r_gu), gu_parts, flat(m_w_gate_up, r_gu),
                                                   flat(v_w_gate_up, r_gu), "adamw_w_gate_up", r_gu))
    return (loss, grad_x,
            g_w_ada, g_b_ada, g_g_pre, g_w_in, g_w_gu, g_b_gu, g_g_gla, g_g_dil, g_w_out, g_g_post,
            d_w_ada, d_b_ada, d_g_pre, d_w_in, d_w_gu, d_b_gu, d_g_gla, d_g_dil, d_w_out, d_g_post,
            nm_w_ada, nm_b_ada, nm_g_pre, nm_w_in, nm_w_gu, nm_b_gu, nm_g_gla, nm_g_dil, nm_w_out, nm_g_post,
            nv_w_ada, nv_b_ada, nv_g_pre, nv_w_in, nv_w_gu, nv_b_gu, nv_g_gla, nv_g_dil, nv_w_out, nv_g_post)


def _adamw_replicated(small, params, where, loss_at):
    n_parts = small.shape[0]

    def body(*refs):
        s_ref, p_refs, o_refs = refs[0], refs[1:1 + 3 * len(params)], refs[1 + 3 * len(params):]
        total = s_ref[0]
        for k in range(1, n_parts):
            total = total + s_ref[k]
        for i, (row, col) in enumerate(where):
            w_ref, m_ref, v_ref = p_refs[3 * i:3 * i + 3]
            n = w_ref.shape[1]
            g = jnp.concatenate([total[row + 3 * l:row + 3 * l + 1, col:col + n] for l in range(DEPTH)], axis=0)
            o_refs[4 * i][...] = g
            o_refs[4 * i + 1][...], o_refs[4 * i + 2][...], o_refs[4 * i + 3][...] = _adam_math(
                w_ref[...], g, m_ref[...], v_ref[...])
        o_refs[-1][...] = jnp.broadcast_to(total[loss_at[0]:loss_at[0] + 1, loss_at[1]:loss_at[1] + 1], (8, LANE))

    flat = [a for p in params for a in p]
    shapes = [jax.ShapeDtypeStruct(p[0].shape, F32) for p in params for _ in range(4)]
    outs = pl.pallas_call(body, name="adamw_replicated",
                          out_shape=shapes + [jax.ShapeDtypeStruct((8, LANE), F32)])(small, *flat)
    return [tuple(outs[4 * i:4 * i + 4]) for i in range(len(params))], outs[-1][0, 0]
```

```python
import functools
import math

import jax
import jax.numpy as jnp
from jax import lax
from jax.experimental import pallas as pl
from jax.experimental.pallas import tpu as pltpu

F32 = jnp.float32
BF16 = jnp.bfloat16

N_DEV = 8
D_MODEL = 1024
DEPTH = 2
GLA_HEADS = 4
GLA_DK = 64
GLA_DV = 128
GLA_CHUNK = 64
GLA_TAU = 16.0
GLA_LOWRANK = 16
DIL_HEADS = 4
DIL_HD = 128
DIL_BLOCK = 128
DIL_DILATIONS = (1, 4, 16)
ROPE_THETA = 10000.0
EPS = 1e-6
IN_COLS = 3600
W_IN_SHARD = IN_COLS // N_DEV
ADA_SHARD = 3 * D_MODEL // N_DEV
OUT_SHARD = D_MODEL // N_DEV
GU_COLS = GLA_HEADS * GLA_DK
GU_SHARD = GU_COLS // N_DEV

ADAM_LR = 0.001
ADAM_B1 = 0.9
ADAM_B2 = 0.999
ADAM_EPS = 1e-08
ADAM_WD = 0.01
ADAM_STEP = 10

NP = 3712
COL_Z, COL_QA, COL_KA, COL_QB, COL_KB, COL_VA, COL_VB, COL_LR = 0, 1024, 1280, 1536, 2048, 2560, 3072, 3584
NP_F32 = COL_VA
NP_BF16 = NP - NP_F32
LANE = 128
MASK_VALUE = -1e30

MESH = pl.DeviceIdType.MESH
ANY = pl.BlockSpec(memory_space=pl.ANY)


def _params(sem=None, vmem_mb=None):
    kw = {}
    if sem is not None:
        kw["dimension_semantics"] = sem
    if vmem_mb is not None:
        kw["vmem_limit_bytes"] = vmem_mb * 1024 * 1024
    return pltpu.CompilerParams(**kw)


def _dot(a, b):
    return jnp.dot(a, b, preferred_element_type=F32)


def _dot_nt(a, b):
    return lax.dot_general(a, b, (((1,), (1,)), ((), ())), preferred_element_type=F32)


def _dot_tn(a, b):
    return lax.dot_general(a, b, (((0,), (0,)), ((), ())), preferred_element_type=F32)


def _sigmoid(z):
    return 1.0 / (1.0 + jnp.exp(-z))


def _log_sigmoid(z):
    return jnp.minimum(z, 0.0) - jnp.log(1.0 + jnp.exp(-jnp.abs(z)))


def _rowvec(v, width=D_MODEL):
    arr, row, cb = v
    return arr.reshape(arr.shape[0], 1, arr.shape[1]), pl.BlockSpec((None, 1, width), lambda *_: (row, 0, cb))


def _my_position():
    return lax.axis_index("x"), lax.axis_index("y"), lax.axis_index("c")


def _linear(px, py, pc):
    return 4 * px + 2 * py + pc


def _gather_phase(phase, x_ref, out_ref, send_sem, recv_sem, local_sem):
    m = x_ref.shape[0]
    x, y, c = _my_position()
    me, sibling = (x, y, c), (x, y, 1 - c)
    chips = [(1 - x, y), (x, 1 - y), (1 - x, 1 - y)]

    def rows(px, py, pc):
        return out_ref.at[pl.ds(_linear(px, py, pc) * m, m), :]

    def copy(k, block, to, src=None):
        return pltpu.make_async_remote_copy(
            src_ref=rows(*block) if src is None else src, dst_ref=rows(*block),
            send_sem=send_sem(k), recv_sem=recv_sem(k), device_id=to, device_id_type=MESH)

    mine = pltpu.make_async_copy(x_ref, rows(*me), local_sem)
    first = [copy(0, me, sibling, src=x_ref)] + [copy(1 + j, me, (*chip, c), src=x_ref) for j, chip in enumerate(chips)]
    passed = [copy(4 + j, (*chip, c), sibling) for j, chip in enumerate(chips)]
    if phase == "start":
        mine.start()
        for cp in first:
            cp.start()
    elif phase == "forward":
        for j, chip in enumerate(chips):
            copy(1 + j, (*chip, c), me).wait_recv()
            passed[j].start()
    else:
        copy(0, sibling, me).wait_recv()
        for j, chip in enumerate(chips):
            copy(4 + j, (*chip, 1 - c), me).wait_recv()
        for cp in first + passed:
            cp.wait_send()
        mine.wait()


def _exchange_phase(phase, x_ref, out_ref, send_sem, recv_sem, local_sem):
    m = x_ref.shape[0] // N_DEV
    x, y, c = _my_position()
    me = _linear(x, y, c)

    def rows(ref, idx):
        return ref.at[pl.ds(idx * m, m), :]

    peers = [(1 - x if j & 4 else x, 1 - y if j & 2 else y, 1 - c if j & 1 else c) for j in range(1, N_DEV)]
    local = pltpu.make_async_copy(rows(x_ref, me), rows(out_ref, me), local_sem)
    sends = [pltpu.make_async_remote_copy(
        src_ref=rows(x_ref, _linear(*peer)), dst_ref=rows(out_ref, me),
        send_sem=send_sem(j), recv_sem=recv_sem(j), device_id=peer, device_id_type=MESH) for j, peer in enumerate(peers)]
    if phase == "start":
        local.start()
        for cp in sends:
            cp.start()
    else:
        for j, peer in enumerate(peers):
            pltpu.make_async_remote_copy(
                src_ref=rows(x_ref, _linear(*peer)), dst_ref=rows(out_ref, _linear(*peer)),
                send_sem=send_sem(j), recv_sem=recv_sem(j), device_id=peer, device_id_type=MESH).wait_recv()
        for cp in sends:
            cp.wait_send()
        local.wait()


def _pairsum_exchange_phase(phase, x_ref, out_refs, send_sem, recv_sem, local_sem):
    out_ref, stage_ref, pair_ref = out_refs
    m, n = x_ref.shape[0] // N_DEV, x_ref.shape[1]
    x, y, c = _my_position()
    mine = 2 * x + y
    chips = [(qx, qy) for qx in range(2) for qy in range(2)]
    others = [(1 - x, y), (x, 1 - y), (1 - x, 1 - y)]

    def rows(ref, idx):
        return ref.at[pl.ds(idx * m, m), :]

    def remote(src, dst, k, to):
        return pltpu.make_async_remote_copy(src_ref=src, dst_ref=dst, send_sem=send_sem(k), recv_sem=recv_sem(k),
                                            device_id=to, device_id_type=MESH)

    to_sibling = [remote(rows(x_ref, _linear(qx, qy, 1 - c)), rows(stage_ref, q), q, (x, y, 1 - c))
                  for q, (qx, qy) in enumerate(chips)]
    to_chips = [remote(rows(pair_ref, 2 * qx + qy), rows(out_ref, mine), 4 + j, (qx, qy, c))
                for j, (qx, qy) in enumerate(others)]
    keep = pltpu.make_async_copy(rows(pair_ref, mine), rows(out_ref, mine), local_sem)
    if phase == "start":
        for cp in to_sibling:
            cp.start()
    elif phase == "reduce":
        for cp in to_sibling:
            cp.wait_recv()

        def through_vmem(a_buf, b_buf, sems):
            tr = 128
            loads = [(pltpu.make_async_copy(rows(x_ref, _linear(qx, qy, c)), a_buf.at[q % 2], sems.at[q % 2]),
                      pltpu.make_async_copy(rows(stage_ref, q), b_buf.at[q % 2], sems.at[2 + q % 2]))
                     for q, (qx, qy) in enumerate(chips)]
            stores = [pltpu.make_async_copy(a_buf.at[q % 2], rows(pair_ref, q), sems.at[4 + q % 2]) for q in range(4)]
            for cp in loads[0]:
                cp.start()
            for q in range(4):
                for cp in loads[q]:
                    cp.wait()
                if q + 1 < 4:
                    if q >= 1:
                        stores[q - 1].wait()
                    for cp in loads[q + 1]:
                        cp.start()

                def add(r, carry, q=q):
                    tile = pl.ds(pl.multiple_of(r * tr, tr), tr)
                    a_buf[q % 2, tile, :] = (a_buf[q % 2, tile, :].astype(F32)
                                             + b_buf[q % 2, tile, :].astype(F32)).astype(x_ref.dtype)
                    return carry

                lax.fori_loop(0, m // tr, add, 0)
                stores[q].start()
            stores[2].wait()
            stores[3].wait()

        pl.run_scoped(through_vmem, pltpu.VMEM((2, m, n), x_ref.dtype), pltpu.VMEM((2, m, n), x_ref.dtype),
                      pltpu.SemaphoreType.DMA((6,)))
    elif phase == "send":
        keep.start()
        for cp in to_chips:
            cp.start()
    else:
        for j, (qx, qy) in enumerate(others):
            remote(rows(pair_ref, mine), rows(out_ref, 2 * qx + qy), 4 + j, (qx, qy, c)).wait_recv()
        for cp in to_sibling + to_chips:
            cp.wait_send()
        keep.wait()


_COMM_PHASES = {"gather": (_gather_phase, ("start", "forward", "finish")),
                "exchange": (_exchange_phase, ("start", "finish")),
                "pairsum_exchange": (_pairsum_exchange_phase, ("start", "reduce", "send", "finish"))}


def _comm_scratch(n_arrays):
    return [pltpu.SemaphoreType.DMA((n_arrays, 7)), pltpu.SemaphoreType.DMA((n_arrays, 7)),
            pltpu.SemaphoreType.DMA((n_arrays,))]


def _comm_run(kind, phases, x_refs, out_refs, send_sems, recv_sems, local_sems):
    fn = _COMM_PHASES[kind][0]
    per = len(out_refs) // len(x_refs)
    for phase in phases:
        for a, x_ref in enumerate(x_refs):
            outs = out_refs[a] if per == 1 else tuple(out_refs[per * a:per * (a + 1)])
            fn(phase, x_ref, outs, lambda k, a=a: send_sems.at[a, k], lambda k, a=a: recv_sems.at[a, k],
               local_sems.at[a])


def _comm_out_shapes(kind, arrays):
    if kind == "pairsum_exchange":
        return [jax.ShapeDtypeStruct((a.shape[0] // 2, a.shape[1]), a.dtype) for a in arrays for _ in range(3)]
    return [jax.ShapeDtypeStruct((N_DEV * a.shape[0], a.shape[1]) if kind == "gather" else a.shape, a.dtype)
            for a in arrays]


def _comm_call(kind, arrays, name):
    n = len(arrays)
    shapes = _comm_out_shapes(kind, arrays)

    def body(*refs):
        _comm_run(kind, _COMM_PHASES[kind][1], refs[:n], refs[n:n + len(shapes)], *refs[n + len(shapes):])

    return pl.pallas_call(body, name=name, out_shape=shapes, in_specs=[ANY] * n, out_specs=[ANY] * len(shapes),
                          scratch_shapes=_comm_scratch(n))(*arrays)


def _all_gather(xs, name):
    return _comm_call("gather", [xs], name)[0]


def _mod_fwd(c_all, w_ada):
    def body(c_ref, w_ref, o_ref):
        cv = c_ref[...]
        sc = cv * _sigmoid(cv)
        o_ref[0] = _dot(sc.astype(BF16), w_ref[0].astype(BF16))

    return pl.pallas_call(
        body, name="mod_fwd", grid=(DEPTH,),
        out_shape=jax.ShapeDtypeStruct((DEPTH, N_DEV, ADA_SHARD), F32),
        in_specs=[pl.BlockSpec((N_DEV, D_MODEL), lambda l: (0, 0)),
                  pl.BlockSpec((1, D_MODEL, ADA_SHARD), lambda l: (l, 0, 0))],
        out_specs=pl.BlockSpec((1, N_DEV, ADA_SHARD), lambda l: (l, 0, 0)),
        compiler_params=_params(("arbitrary",)),
    )(c_all, w_ada)


def _w_ada_grad(c_all, dmod_cols):
    def body(c_ref, d_ref, o_ref):
        cv = c_ref[...]
        sc = cv * _sigmoid(cv)
        o_ref[0] = lax.dot_general(sc, d_ref[0], (((0,), (0,)), ((), ())), precision=lax.Precision.HIGHEST,
                                   preferred_element_type=F32)

    return pl.pallas_call(
        body, name="w_ada_grad", grid=(DEPTH,),
        out_shape=jax.ShapeDtypeStruct((DEPTH, D_MODEL, ADA_SHARD), F32),
        in_specs=[pl.BlockSpec((N_DEV, D_MODEL), lambda l: (0, 0)),
                  pl.BlockSpec((1, N_DEV, ADA_SHARD), lambda l: (l, 0, 0))],
        out_specs=pl.BlockSpec((1, D_MODEL, ADA_SHARD), lambda l: (l, 0, 0)),
        compiler_params=_params(("arbitrary",)),
    )(c_all, dmod_cols)


def _comm_plumbing(comm):
    if not comm:
        return 0, [], []
    return len(comm[1]), _comm_out_shapes(*comm), _comm_scratch(len(comm[1]))


def _split_refs(refs, n_in, n_out, n_scratch, comm):
    ci, shapes, _ = _comm_plumbing(comm)
    co = len(shapes)
    a, b, c = n_in + ci, n_in + ci + n_out, n_in + ci + n_out + co
    return refs[:n_in], refs[a:b], refs[c:c + n_scratch], refs[n_in:a], refs[b:c], refs[c + n_scratch:]


def _prenorm_proj(x, g_pre, scale, shift, w_new, cos, sin_signed, comm=None, ts=256):
    s_len = x.shape[0]
    n_cin, c_shapes, c_scratch = _comm_plumbing(comm)

    def body(*refs):
        (x_ref, g_ref, sc_ref, sh_ref, w_ref, cos_ref, sin_ref), (pf_ref, pb_ref, h_ref), _, cin, cout, csem = (
            _split_refs(refs, 7, 3, 0, comm))
        comm_before, comm_after = _comm_hooks(comm, cin, cout, csem, steps=s_len // ts)
        comm_before()
        xv = x_ref[...]
        rstd = lax.rsqrt(jnp.mean(xv * xv, axis=-1, keepdims=True) + EPS)
        h = (xv * rstd * g_ref[...]) * (1.0 + sc_ref[...]) + sh_ref[...]
        hb = h.astype(BF16)
        h_ref[...] = hb
        for j in range(0, NP, 512):
            w = min(512, NP - j)
            acc = _dot(hb, w_ref[:, j:j + w])
            if COL_QB <= j < COL_VA:
                for lo in range(0, w, DIL_HD):
                    pf_ref[:, j + lo:j + lo + DIL_HD] = _rope(acc[:, lo:lo + DIL_HD], cos_ref[...], sin_ref[...])
            elif j < NP_F32:
                pf_ref[:, j:j + w] = acc
            else:
                pb_ref[:, j - NP_F32:j - NP_F32 + w] = acc.astype(BF16)
        comm_after()

    (g_pre, g_spec), (scale, sc_spec), (shift, sh_spec) = _rowvec(g_pre), _rowvec(scale), _rowvec(shift)
    return pl.pallas_call(
        body, name="prenorm_proj_comm" if comm else "prenorm_proj", grid=(s_len // ts,),
        out_shape=[jax.ShapeDtypeStruct((s_len, NP_F32), F32), jax.ShapeDtypeStruct((s_len, NP_BF16), BF16),
                   jax.ShapeDtypeStruct((s_len, D_MODEL), BF16)] + c_shapes,
        in_specs=[pl.BlockSpec((ts, D_MODEL), lambda i: (i, 0)), g_spec, sc_spec, sh_spec,
                  pl.BlockSpec((D_MODEL, NP), lambda i: (0, 0)), pl.BlockSpec((ts, DIL_HD), lambda i: (i, 0)),
                  pl.BlockSpec((ts, DIL_HD), lambda i: (i, 0))] + [ANY] * n_cin,
        out_specs=[pl.BlockSpec((ts, NP_F32), lambda i: (i, 0)), pl.BlockSpec((ts, NP_BF16), lambda i: (i, 0)),
                   pl.BlockSpec((ts, D_MODEL), lambda i: (i, 0))] + [ANY] * len(c_shapes),
        scratch_shapes=c_scratch,
        compiler_params=_params(("arbitrary",), 48),
    )(x, g_pre, scale, shift, w_new, cos, sin_signed, *(comm[1] if comm else []))


GLA_GROUP = 16


def _gla_group_rows(t):
    return [pl.ds(pl.multiple_of((t * GLA_GROUP + j) * GLA_CHUNK, GLA_CHUNK), GLA_CHUNK) for j in range(GLA_GROUP)]


def _gla_chunks_common(q_ref, k_ref, lr_ref, wgu_ref, bgu_ref, rows_list):
    c = GLA_CHUNK
    ri = lax.broadcasted_iota(jnp.int32, (c, c), 0)
    ci = lax.broadcasted_iota(jnp.int32, (c, c), 1)
    tril = (ri >= ci).astype(F32)
    zs = [_dot(lr_ref[rows, :], wgu_ref[...]) + bgu_ref[...] for rows in rows_list]
    las = [_log_sigmoid(z) * (1.0 / GLA_TAU) for z in zs]
    bs = [jnp.dot(tril, la, precision=lax.Precision.HIGHEST, preferred_element_type=F32) for la in las]
    out = []
    for rows, z, b in zip(rows_list, zs, bs):
        q = q_ref[rows, :] * (GLA_DK ** -0.5)
        k = k_ref[rows, :]
        bl = b[c - 1:c, :]
        out.append(dict(z=z, b=b, bl=bl, qe=q * jnp.exp(b), ke=k * jnp.exp(-b), kend=k * jnp.exp(bl - b),
                        dec=jnp.exp(bl)))
    return out, ri, ci


def _head_lane_mask(hh):
    return (lax.broadcasted_iota(jnp.int32, (1, LANE), 1) // GLA_DK) == hh


def _state_block_mask():
    r = lax.broadcasted_iota(jnp.int32, (2 * GLA_DV, LANE), 0) // GLA_DV
    cc = lax.broadcasted_iota(jnp.int32, (2 * GLA_DV, LANE), 1) // GLA_DK
    return r == cc


def _gla_fwd(pf, pb, wgu, bgu, layer, comm=None):
    s_len = pf.shape[0]
    nc = s_len // GLA_CHUNK
    ncomm = len(comm[1]) if comm else 0

    def body(*refs):
        q_ref, k_ref, v_ref, lr_ref, wgu_ref, bgu_ref = refs[:6]
        cin, (o_ref, st_ref), cout = refs[6:6 + ncomm], refs[6 + ncomm:8 + ncomm], refs[8 + ncomm:8 + 2 * ncomm]
        qe_s, cs_s, dec_s = refs[8 + 2 * ncomm:11 + 2 * ncomm]
        comm_before, comm_after = _comm_hooks(comm, cin, cout, refs[11 + 2 * ncomm:], steps=2)
        comm_before()
        bd = _state_block_mask()

        def local(t, carry):
            rows_list = _gla_group_rows(t)
            cm, ri, ci = _gla_chunks_common(q_ref, k_ref, lr_ref, wgu_ref, bgu_ref, rows_list)
            vs = [v_ref[rows, :] for rows in rows_list]
            kebs = [c["ke"].astype(BF16) for c in cm]
            a = [[jnp.where(ri >= ci, _dot_nt(jnp.where(_head_lane_mask(hh), c["qe"], 0.0).astype(BF16), keb), 0.0)
                  .astype(BF16) for hh in range(2)] for c, keb in zip(cm, kebs)]
            oi = [[_dot(ah[hh], v[:, hh * GLA_DV:(hh + 1) * GLA_DV]) for hh in range(2)] for ah, v in zip(a, vs)]
            cs = [jnp.where(bd, _dot_tn(v, c["kend"].astype(BF16)), 0.0) for c, v in zip(cm, vs)]
            for j, (rows, c) in enumerate(zip(rows_list, cm)):
                n = t * GLA_GROUP + j
                o_ref[rows, :] = jnp.concatenate(oi[j], axis=1)
                qe_s[rows, :] = c["qe"].astype(BF16)
                cs_s[n] = cs[j]
                dec_s[n] = jnp.broadcast_to(c["dec"], (8, LANE))
            return carry

        lax.fori_loop(0, nc // GLA_GROUP, local, 0)

        def scan(n, st):
            st_ref[0, n] = st.astype(BF16)
            return dec_s[n][0:1, :] * st + cs_s[n]

        lax.fori_loop(0, nc, scan, jnp.zeros((2 * GLA_DV, LANE), F32))

        def inter(t, carry):
            rows_list = _gla_group_rows(t)
            add = [_dot_nt(qe_s[rows, :], st_ref[0, t * GLA_GROUP + j]) for j, rows in enumerate(rows_list)]
            for rows, av in zip(rows_list, add):
                o_ref[rows, :] = o_ref[rows, :] + av
            return carry

        lax.fori_loop(0, nc // GLA_GROUP, inter, 0)
        comm_after()

    return pl.pallas_call(
        body, name="gla_fwd_comm" if comm else "gla_fwd", grid=(2,),
        out_shape=[jax.ShapeDtypeStruct((s_len, GLA_HEADS * GLA_DV), F32),
                   jax.ShapeDtypeStruct((2, nc, 2 * GLA_DV, LANE), BF16)] + (_comm_out_shapes(*comm) if comm else []),
        in_specs=[pl.BlockSpec((s_len, LANE), lambda g: (0, COL_QA // LANE + g)),
                  pl.BlockSpec((s_len, LANE), lambda g: (0, COL_KA // LANE + g)),
                  pl.BlockSpec((s_len, 2 * GLA_DV), lambda g: (0, (COL_VA - NP_F32) // (2 * GLA_DV) + g)),
                  pl.BlockSpec((s_len, LANE), lambda g: (0, (COL_LR - NP_F32) // LANE)),
                  pl.BlockSpec((None, LANE, LANE), lambda g: (layer, 0, g)),
                  pl.BlockSpec((None, 1, LANE), lambda g: (layer, 0, g))] + [ANY] * ncomm,
        out_specs=[pl.BlockSpec((s_len, 2 * GLA_DV), lambda g: (0, g)),
                   pl.BlockSpec((1, nc, 2 * GLA_DV, LANE), lambda g: (g, 0, 0, 0))] + [ANY] * ncomm,
        scratch_shapes=[pltpu.VMEM((s_len, LANE), BF16), pltpu.VMEM((nc, 2 * GLA_DV, LANE), F32),
                        pltpu.VMEM((nc, 8, LANE), F32)] + (_comm_scratch(ncomm) if comm else []),
        compiler_params=_params(("arbitrary",), 56),
    )(pf, pf, pb, pb, wgu, bgu.reshape(bgu.shape[0], 1, GU_COLS), *(comm[1] if comm else []))


def _rope_tables(s_len):
    inv_freq = ROPE_THETA ** (-jnp.arange(0, DIL_HD, 2, dtype=F32) / DIL_HD)
    ang = jnp.arange(s_len, dtype=F32)[:, None] * inv_freq[None, :]
    cos, sin = jnp.cos(ang), jnp.sin(ang)
    return jnp.concatenate([cos, cos], axis=1), jnp.concatenate([-sin, sin], axis=1)


def _rope(xv, cos, sin_signed):
    return xv * cos + pltpu.roll(xv, DIL_HD // 2, 1) * sin_signed


DIL_GROUP = 8


def _dil_pair_block(i, half, d, nblk, group=DIL_GROUP):
    nb = nblk // d
    j = i + half * (nblk // group)
    if nb >= 2 * group:
        r, n = j % d, j // d
    else:
        r, n = j // nb, j % nb
    kb = jnp.maximum(n - 1, 0)
    qs = r + d * DIL_BLOCK * n
    ks = r + d * DIL_BLOCK * kb
    return qs, ks, jnp.minimum(n, 1)


def _dil_fill_bias(bias):
    qi = lax.broadcasted_iota(jnp.int32, (DIL_BLOCK, 2 * DIL_BLOCK), 0)
    kj = lax.broadcasted_iota(jnp.int32, (DIL_BLOCK, 2 * DIL_BLOCK), 1)
    for sel in range(2):
        dist = qi - kj + DIL_BLOCK * sel
        bias[sel] = jnp.where((dist >= 0) & (dist <= DIL_BLOCK), 0.0, MASK_VALUE)


def _strided(start, size, d):
    return pl.ds(start, size) if d == 1 else pl.ds(start, size, stride=d)


def _comm_hooks(comm, cin, cout, csem, steps=DIL_HEADS):
    def before():
        if comm:
            @pl.when(pl.program_id(0) == 0)
            def _():
                _comm_run(comm[0], ("start",), cin, cout, *csem)

            if comm[0] == "gather":
                @pl.when(pl.program_id(0) == steps - 1)
                def _():
                    _comm_run(comm[0], ("forward",), cin, cout, *csem)

            if comm[0] == "pairsum_exchange":
                @pl.when(pl.program_id(0) == (1 if steps <= 4 else 2))
                def _():
                    _comm_run(comm[0], ("reduce", "send"), cin, cout, *csem)

    def after():
        if comm:
            @pl.when(pl.program_id(0) == steps - 1)
            def _():
                _comm_run(comm[0], ("finish",), cin, cout, *csem)

    return before, after


def _dil_fwd(pf, pb, comm=None):
    s_len = pf.shape[0]
    nblk = s_len // DIL_BLOCK
    prep_rows = 256
    scale = DIL_HD ** -0.5
    nc = len(comm[1]) if comm else 0

    def body(*refs):
        ((qf, kf, v_ref), (o_ref, lse_ref), (vf, o0, o1, o2, l0, l1, l2, bias), cin, cout, csem) = _split_refs(
            refs, 3, 2, 8, comm)
        comm_before, comm_after = _comm_hooks(comm, cin, cout, csem)
        comm_before()
        _dil_fill_bias(bias)

        def prep(t, carry):
            rows = pl.ds(pl.multiple_of(t * prep_rows, prep_rows), prep_rows)
            vf[rows, :] = v_ref[rows, :].astype(F32)
            return carry

        lax.fori_loop(0, s_len // prep_rows, prep, 0)
        for d, o_p, l_p in zip(DIL_DILATIONS, (o0, o1, o2), (l0, l1, l2)):
            if nblk // d == 2:
                units = DIL_GROUP // 2

                def whole(i, carry, d=d, o_p=o_p, l_p=l_p, units=units):
                    rows = [_strided(i + u * (d // units), 2 * DIL_BLOCK, d) for u in range(units)]
                    ld = [(qf[rw, :].astype(BF16), kf[rw, :].astype(BF16), vf[rw, :].astype(BF16)) for rw in rows]
                    both = bias[...].reshape(2 * DIL_BLOCK, 2 * DIL_BLOCK)
                    s = [_dot_nt(qb, kk) * scale + both for qb, kk, _ in ld]
                    m = [jnp.max(sv, axis=-1, keepdims=True) for sv in s]
                    p = [jnp.exp(sv - mv) for sv, mv in zip(s, m)]
                    den = [jnp.sum(pv, axis=-1, keepdims=True) for pv in p]
                    r = [_dot(pv.astype(BF16), vv) for pv, (_, _, vv) in zip(p, ld)]
                    for rv, dv, mv, rw in zip(r, den, m, rows):
                        o_p[rw, :] = rv / dv
                        l_p[rw, :] = jnp.broadcast_to(mv + jnp.log(dv), (2 * DIL_BLOCK, DIL_HD))
                    return carry

                lax.fori_loop(0, d // units, whole, 0)
                continue

            def pair(i, carry, d=d, o_p=o_p, l_p=l_p):
                idx = [_dil_pair_block(i, half, d, nblk, DIL_GROUP) for half in range(DIL_GROUP)]
                ld = [(qf[_strided(qs, DIL_BLOCK, d), :].astype(BF16),
                       kf[_strided(ks, 2 * DIL_BLOCK, d), :].astype(BF16),
                       vf[_strided(ks, 2 * DIL_BLOCK, d), :].astype(BF16)) for qs, ks, _ in idx]
                s = [_dot_nt(qb, kk) * scale + bias[sel] for (qb, kk, _), (_, _, sel) in zip(ld, idx)]
                m = [jnp.max(sv, axis=-1, keepdims=True) for sv in s]
                p = [jnp.exp(sv - mv) for sv, mv in zip(s, m)]
                den = [jnp.sum(pv, axis=-1, keepdims=True) for pv in p]
                r = [_dot(pv.astype(BF16), vv) for pv, (_, _, vv) in zip(p, ld)]
                for rv, dv, mv, (qs, _, _) in zip(r, den, m, idx):
                    o_p[_strided(qs, DIL_BLOCK, d), :] = rv / dv
                    l_p[_strided(qs, DIL_BLOCK, d), :] = jnp.broadcast_to(mv + jnp.log(dv), (DIL_BLOCK, DIL_HD))
                return carry

            lax.fori_loop(0, nblk // DIL_GROUP, pair, 0)

        def comb(t, carry):
            rows = pl.ds(pl.multiple_of(t * prep_rows, prep_rows), prep_rows)
            a0, a1, a2 = l0[rows, :], l1[rows, :], l2[rows, :]
            m = jnp.maximum(jnp.maximum(a0, a1), a2)
            e0, e1, e2 = jnp.exp(a0 - m), jnp.exp(a1 - m), jnp.exp(a2 - m)
            tot = e0 + e1 + e2
            o_ref[rows, :] = (e0 * o0[rows, :] + e1 * o1[rows, :] + e2 * o2[rows, :]) / tot
            lse_ref[rows, :] = m + jnp.log(tot)
            return carry

        lax.fori_loop(0, s_len // prep_rows, comb, 0)
        comm_after()

    head = lambda base: pl.BlockSpec((s_len, DIL_HD), lambda h: (0, base // DIL_HD + h))
    out = pl.BlockSpec((s_len, DIL_HD), lambda h: (0, h))
    shp = jax.ShapeDtypeStruct((s_len, DIL_HEADS * DIL_HD), F32)
    return pl.pallas_call(
        body, name="dil_fwd_comm" if comm else "dil_fwd", grid=(DIL_HEADS,),
        out_shape=[shp, shp] + (_comm_out_shapes(*comm) if comm else []),
        in_specs=[head(COL_QB), head(COL_KB), head(COL_VB - NP_F32)] + [ANY] * nc,
        out_specs=[out, out] + [ANY] * nc,
        scratch_shapes=[pltpu.VMEM((s_len, DIL_HD), F32) for _ in range(7)]
        + [pltpu.VMEM((2, DIL_BLOCK, 2 * DIL_BLOCK), F32)] + (_comm_scratch(nc) if comm else []),
        compiler_params=_params(("arbitrary",), 56),
    )(pf, pf, pb, *(comm[1] if comm else []))


def _silu_and_grad(z):
    sg = _sigmoid(z)
    return z * sg, sg * (1.0 + z * (1.0 - sg))


def _post_fwd(o_a, o_b, pf, g_heads, w_out, x, gate, g_post, target=None, ts=256):
    s_len = x.shape[0]
    half = GLA_HEADS * GLA_DV
    last = target is not None

    def body(*refs):
        oa_ref, ob_ref, z_ref, gh_ref, w_ref, x_ref, gate_ref, gp_ref = refs[:8]
        xo_ref, u_ref = refs[8 + last:10 + last]
        y_ref = refs[-1]
        for src, base in ((oa_ref, 0), (ob_ref, half)):
            for hh in range(4):
                lo = hh * LANE
                og = src[:, lo:lo + LANE]
                on = og * lax.rsqrt(jnp.mean(og * og, axis=-1, keepdims=True) + EPS)
                zg = z_ref[:, base + lo:base + lo + LANE].astype(F32)
                y_ref[:, base + lo:base + lo + LANE] = (on * gh_ref[:, base + lo:base + lo + LANE]
                                                        * (zg * _sigmoid(zg))).astype(BF16)
        u = _dot(y_ref[...], w_ref[...])
        u_ref[...] = u.astype(BF16)
        rstd = lax.rsqrt(jnp.mean(u * u, axis=-1, keepdims=True) + EPS)
        x_out = x_ref[...] + gate_ref[...] * (u * rstd * gp_ref[...])
        if last:
            t_ref, loss_ref = refs[8], refs[11]

            @pl.when(pl.program_id(0) == 0)
            def _():
                loss_ref[...] = jnp.zeros_like(loss_ref)

            e = x_out - t_ref[...]
            xo_ref[...] = e * (1.0 / D_MODEL)
            loss_ref[...] += 0.5 * jnp.sum(jnp.mean(e * e, axis=-1, keepdims=True))
        else:
            xo_ref[...] = x_out

    (g_heads, gh_spec), (gate, gate_spec), (g_post, gp_spec) = _rowvec(g_heads), _rowvec(gate), _rowvec(g_post)
    tile = pl.BlockSpec((ts, D_MODEL), lambda i: (i, 0))
    halft = pl.BlockSpec((ts, half), lambda i: (i, 0))
    return pl.pallas_call(
        body, name="post_fwd_loss" if last else "post_fwd", grid=(s_len // ts,),
        out_shape=[jax.ShapeDtypeStruct((s_len, D_MODEL), F32), jax.ShapeDtypeStruct((s_len, D_MODEL), BF16)]
        + ([jax.ShapeDtypeStruct((8, LANE), F32)] if last else []),
        in_specs=[halft, halft, tile, gh_spec, pl.BlockSpec((D_MODEL, D_MODEL), lambda i: (0, 0)), tile, gate_spec,
                  gp_spec] + ([tile] if last else []),
        out_specs=[tile, tile] + ([pl.BlockSpec((8, LANE), lambda i: (0, 0))] if last else []),
        scratch_shapes=[pltpu.VMEM((ts, D_MODEL), BF16)],
        compiler_params=_params(("arbitrary",), 40),
    )(o_a, o_b, pf, g_heads, w_out, x, gate, g_post, *([target] if last else []))


def _post_bwd(dxo, u, gate, g_post, w_out, o_a, o_b, pf, g_heads, ts=512):
    s_len = dxo.shape[0]
    half = GLA_HEADS * GLA_DV
    steps = s_len // ts

    def body(dx_ref, u_ref, gate_ref, gp_ref, w_ref, oa_ref, ob_ref, z_ref, gh_ref, do_ref, dz_ref, sums_ref, gw_ref,
             y_s, acc):
        @pl.when(pl.program_id(0) == 0)
        def _():
            sums_ref[...] = jnp.zeros_like(sums_ref)
            acc[...] = jnp.zeros_like(acc)

        dx = dx_ref[...]
        u = u_ref[...].astype(F32)
        rstd = lax.rsqrt(jnp.mean(u * u, axis=-1, keepdims=True) + EPS)
        un = u * rstd
        sums_ref[0:1, :] += jnp.sum(dx * (un * gp_ref[...]), axis=0, keepdims=True)
        drn = dx * gate_ref[...]
        sums_ref[1:2, :] += jnp.sum(drn * un, axis=0, keepdims=True)
        dun = drn * gp_ref[...]
        du = rstd * (dun - un * jnp.mean(dun * un, axis=-1, keepdims=True))
        dub = du.astype(BF16)
        dy = _dot_nt(dub, w_ref[...])
        for src, base in ((oa_ref, 0), (ob_ref, half)):
            for hh in range(4):
                lo = base + hh * LANE
                og = src[:, hh * LANE:(hh + 1) * LANE]
                rs = lax.rsqrt(jnp.mean(og * og, axis=-1, keepdims=True) + EPS)
                on = og * rs
                zg = z_ref[:, lo:lo + LANE].astype(F32)
                sz, dsz = _silu_and_grad(zg)
                gg = gh_ref[:, lo:lo + LANE]
                dyg = dy[:, lo:lo + LANE]
                y_s[:, lo:lo + LANE] = (on * gg * sz).astype(BF16)
                sums_ref[2:3, lo:lo + LANE] += jnp.sum(dyg * sz * on, axis=0, keepdims=True)
                dz_ref[:, lo:lo + LANE] = (dyg * on * gg * dsz).astype(BF16)
                don = dyg * gg * sz
                do_ref[:, lo:lo + LANE] = (rs * (don - on * jnp.mean(don * on, axis=-1, keepdims=True))).astype(BF16)
        acc[...] += _dot_tn(y_s[...], dub)

        @pl.when(pl.program_id(0) == steps - 1)
        def _():
            gw_ref[...] = acc[...].astype(BF16)

    (g_heads, gh_spec), (gate, gate_spec), (g_post, gp_spec) = _rowvec(g_heads), _rowvec(gate), _rowvec(g_post)
    tile = pl.BlockSpec((ts, D_MODEL), lambda i: (i, 0))
    halft = pl.BlockSpec((ts, half), lambda i: (i, 0))
    whole = pl.BlockSpec((D_MODEL, D_MODEL), lambda i: (0, 0))
    return pl.pallas_call(
        body, name="post_bwd", grid=(steps,),
        out_shape=(jax.ShapeDtypeStruct((s_len, D_MODEL), BF16), jax.ShapeDtypeStruct((s_len, D_MODEL), BF16),
                   jax.ShapeDtypeStruct((8, D_MODEL), F32), jax.ShapeDtypeStruct((D_MODEL, D_MODEL), BF16)),
        in_specs=[tile, tile, gate_spec, gp_spec, whole, halft, halft, tile, gh_spec],
        out_specs=(tile, tile, pl.BlockSpec((8, D_MODEL), lambda i: (0, 0)), whole),
        scratch_shapes=[pltpu.VMEM((ts, D_MODEL), BF16), pltpu.VMEM((D_MODEL, D_MODEL), F32)],
        compiler_params=_params(("arbitrary",), 48),
    )(dxo, u, gate, g_post, w_out, o_a, o_b, pf, g_heads)


def _gla_bwd(pf, pb, wgu, bgu, layer, states, do, comm=None):
    s_len = pf.shape[0]
    nc = s_len // GLA_CHUNK
    c = GLA_CHUNK
    n_cin, c_shapes, c_scratch = _comm_plumbing(comm)

    def body(*refs):
        ((q_ref, k_ref, v_ref, lr_ref, wgu_ref, bgu_ref, st_ref, do_ref),
         (dq_ref, dk_ref, dv_ref, dlr_ref, dwgu_ref, dbgu_ref), (ds_s, dec_s, dw_acc, db_acc),
         cin, cout, csem) = _split_refs(refs, 8, 6, 4, comm)
        comm_before, comm_after = _comm_hooks(comm, cin, cout, csem, steps=2)
        comm_before()
        dw_acc[...] = jnp.zeros_like(dw_acc)
        db_acc[...] = jnp.zeros_like(db_acc)
        bd = _state_block_mask()
        last_row = lax.broadcasted_iota(jnp.int32, (c, LANE), 0) == c - 1

        def local(t, carry):
            rows_list = _gla_group_rows(t)
            cm, _, _ = _gla_chunks_common(q_ref, k_ref, lr_ref, wgu_ref, bgu_ref, rows_list)
            loc = [jnp.where(bd, _dot_tn(do_ref[rows, :], cc["qe"].astype(BF16)), 0.0)
                   for rows, cc in zip(rows_list, cm)]
            for j, cc in enumerate(cm):
                ds_s[t * GLA_GROUP + j] = loc[j]
                dec_s[t * GLA_GROUP + j] = jnp.broadcast_to(cc["dec"], (8, LANE))
            return carry

        lax.fori_loop(0, nc // GLA_GROUP, local, 0)

        def scan(t, dst):
            n = nc - 1 - t
            loc = ds_s[n]
            ds_s[n] = dst
            return dec_s[n][0:1, :] * dst + loc

        lax.fori_loop(0, nc, scan, jnp.zeros((2 * GLA_DV, LANE), F32))

        def rest(t, carry):
            rows_list = _gla_group_rows(t)
            cm, ri, ci = _gla_chunks_common(q_ref, k_ref, lr_ref, wgu_ref, bgu_ref, rows_list)
            ns = [t * GLA_GROUP + j for j in range(GLA_GROUP)]
            vs = [v_ref[rows, :] for rows in rows_list]
            dobs = [do_ref[rows, :] for rows in rows_list]
            stbs = [st_ref[0, n] for n in ns]
            dsts = [ds_s[n] for n in ns]
            dstbs = [d.astype(BF16) for d in dsts]
            qebs = [cc["qe"].astype(BF16) for cc in cm]
            kebs = [cc["ke"].astype(BF16) for cc in cm]
            kendbs = [cc["kend"].astype(BF16) for cc in cm]
            hms = [_head_lane_mask(hh) for hh in range(2)]
            qehs = [[jnp.where(hm, cc["qe"], 0.0).astype(BF16) for hm in hms] for cc in cm]
            kehs = [[jnp.where(hm, cc["ke"], 0.0).astype(BF16) for hm in hms] for cc in cm]
            heads = lambda x: [x[:, hh * GLA_DV:(hh + 1) * GLA_DV] for hh in range(2)]
            vhs, dohs = [heads(v) for v in vs], [heads(d) for d in dobs]

            dqe0 = [_dot(dob, stb) for dob, stb in zip(dobs, stbs)]
            dkend = [_dot(v, dstb) for v, dstb in zip(vs, dstbs)]
            dv0 = [_dot_nt(kb, dstb) for kb, dstb in zip(kendbs, dstbs)]
            a_t = [[jnp.where(ci >= ri, _dot_nt(kehs[j][hh], qebs[j]), 0.0).astype(BF16) for hh in range(2)]
                   for j in range(GLA_GROUP)]
            da = [[jnp.where(ri >= ci, _dot_nt(dohs[j][hh], vhs[j][hh]), 0.0).astype(BF16) for hh in range(2)]
                  for j in range(GLA_GROUP)]
            da_t = [[jnp.where(ci >= ri, _dot_nt(vhs[j][hh], dohs[j][hh]), 0.0).astype(BF16) for hh in range(2)]
                    for j in range(GLA_GROUP)]
            dv1 = [[_dot(a_t[j][hh], dohs[j][hh]) for hh in range(2)] for j in range(GLA_GROUP)]
            dqe1 = [[_dot(da[j][hh], kebs[j]) for hh in range(2)] for j in range(GLA_GROUP)]
            dke1 = [[_dot(da_t[j][hh], qehs[j][hh]) for hh in range(2)] for j in range(GLA_GROUP)]

            dbs, dzs = [], []
            for j, (rows, cc) in enumerate(zip(rows_list, cm)):
                qe, ke, kend, b, bl = cc["qe"], cc["ke"], cc["kend"], cc["b"], cc["bl"]
                dqe = dqe0[j] + jnp.where(hms[0], dqe1[j][0], 0.0) + jnp.where(hms[1], dqe1[j][1], 0.0)
                dke = jnp.where(hms[0], dke1[j][0], 0.0) + jnp.where(hms[1], dke1[j][1], 0.0)
                dv_ref[rows, :] = (dv0[j] + jnp.concatenate(dv1[j], axis=1)).astype(BF16)
                dq_ref[rows, :] = (dqe * jnp.exp(b) * (GLA_DK ** -0.5)).astype(BF16)
                dk_ref[rows, :] = (dke * jnp.exp(-b) + dkend[j] * jnp.exp(bl - b)).astype(BF16)
                ddec = jnp.sum(dsts[j] * stbs[j].astype(F32), axis=0, keepdims=True)
                dbl = jnp.sum(dkend[j] * kend, axis=0, keepdims=True) + ddec * cc["dec"]
                dbs.append(dqe * qe - dke * ke - dkend[j] * kend + jnp.where(last_row, dbl, 0.0))
            triu = (ci >= ri).astype(F32)
            dlas = [jnp.dot(triu, db, precision=lax.Precision.HIGHEST, preferred_element_type=F32) for db in dbs]
            dzs = [dla * (1.0 / GLA_TAU) * _sigmoid(-cc["z"]) for dla, cc in zip(dlas, cm)]
            dzbs = [dz.astype(BF16) for dz in dzs]
            dlrs = [_dot_nt(dzb, wgu_ref[...]) for dzb in dzbs]
            dws = [_dot_tn(lr_ref[rows, :], dzb) for rows, dzb in zip(rows_list, dzbs)]
            for rows, dlr in zip(rows_list, dlrs):
                dlr_ref[0, rows, :] = dlr
            dw_acc[...] += functools.reduce(lambda x, y: x + y, dws)
            db_acc[0:1, :] += jnp.sum(functools.reduce(lambda x, y: x + y, dzs), axis=0, keepdims=True)
            return carry

        lax.fori_loop(0, nc // GLA_GROUP, rest, 0)
        dwgu_ref[...] = dw_acc[...]
        dbgu_ref[...] = db_acc[...]
        comm_after()

    pair = pl.BlockSpec((s_len, LANE), lambda g: (0, g))
    return pl.pallas_call(
        body, name="gla_bwd_comm" if comm else "gla_bwd", grid=(2,),
        out_shape=[jax.ShapeDtypeStruct((s_len, GU_COLS), BF16), jax.ShapeDtypeStruct((s_len, GU_COLS), BF16),
                   jax.ShapeDtypeStruct((s_len, GLA_HEADS * GLA_DV), BF16),
                   jax.ShapeDtypeStruct((2, s_len, LANE), F32),
                   jax.ShapeDtypeStruct((LANE, GU_COLS), F32), jax.ShapeDtypeStruct((8, GU_COLS), F32)] + c_shapes,
        in_specs=[pl.BlockSpec((s_len, LANE), lambda g: (0, COL_QA // LANE + g)),
                  pl.BlockSpec((s_len, LANE), lambda g: (0, COL_KA // LANE + g)),
                  pl.BlockSpec((s_len, 2 * GLA_DV), lambda g: (0, (COL_VA - NP_F32) // (2 * GLA_DV) + g)),
                  pl.BlockSpec((s_len, LANE), lambda g: (0, (COL_LR - NP_F32) // LANE)),
                  pl.BlockSpec((None, LANE, LANE), lambda g: (layer, 0, g)),
                  pl.BlockSpec((None, 1, LANE), lambda g: (layer, 0, g)),
                  pl.BlockSpec((1, nc, 2 * GLA_DV, LANE), lambda g: (g, 0, 0, 0)),
                  pl.BlockSpec((s_len, 2 * GLA_DV), lambda g: (0, g))] + [ANY] * n_cin,
        out_specs=[pair, pair, pl.BlockSpec((s_len, 2 * GLA_DV), lambda g: (0, g)),
                   pl.BlockSpec((1, s_len, LANE), lambda g: (g, 0, 0)),
                   pl.BlockSpec((LANE, LANE), lambda g: (0, g)), pl.BlockSpec((8, LANE), lambda g: (0, g))]
        + [ANY] * len(c_shapes),
        scratch_shapes=[pltpu.VMEM((nc, 2 * GLA_DV, LANE), F32), pltpu.VMEM((nc, 8, LANE), F32),
                        pltpu.VMEM((LANE, LANE), F32), pltpu.VMEM((8, LANE), F32)] + c_scratch,
        compiler_params=_params(("arbitrary",), 56),
    )(pf, pf, pb, pb, wgu, bgu.reshape(bgu.shape[0], 1, GU_COLS), states, do, *(comm[1] if comm else []))


def _dil_bwd(pf, pb, cos, sin_signed, do, o_b, lse, comm=None):
    s_len = pf.shape[0]
    nblk = s_len // DIL_BLOCK
    prep_rows = 256
    scale = DIL_HD ** -0.5
    nc = len(comm[1]) if comm else 0

    def body(*refs):
        ((q_ref, kf, v_ref, cos_ref, sin_ref, do_ref, o_ref, lse_ref), (dq_ref, dk_ref, dv_ref),
         (qf, vf, dof, dl, dqa, dka, dva, bias), cin, cout, csem) = _split_refs(refs, 8, 3, 8, comm)
        comm_before, comm_after = _comm_hooks(comm, cin, cout, csem)
        comm_before()
        _dil_fill_bias(bias)

        def prep(t, carry):
            rows = pl.ds(pl.multiple_of(t * prep_rows, prep_rows), prep_rows)
            qf[rows, :] = q_ref[rows, :] * scale
            vf[rows, :] = v_ref[rows, :].astype(F32)
            dov = do_ref[rows, :].astype(F32)
            dof[rows, :] = dov
            dl[rows, :] = jnp.broadcast_to(jnp.sum(dov * o_ref[rows, :], axis=-1, keepdims=True), (prep_rows, DIL_HD))
            zero = jnp.zeros((prep_rows, DIL_HD), F32)
            dqa[rows, :] = zero
            dka[rows, :] = zero
            dva[rows, :] = zero
            return carry

        lax.fori_loop(0, s_len // prep_rows, prep, 0)

        for d in DIL_DILATIONS:
            if nblk // d == 2:
                units = DIL_GROUP // 2

                def whole(i, carry, d=d, units=units):
                    rows = [_strided(i + u * (d // units), 2 * DIL_BLOCK, d) for u in range(units)]
                    ld = [(qf[rw, :].astype(BF16), kf[rw, :].astype(BF16), vf[rw, :].astype(BF16),
                           dof[rw, :].astype(BF16)) for rw in rows]
                    both = bias[...].reshape(2 * DIL_BLOCK, 2 * DIL_BLOCK)
                    s = [_dot_nt(qb, kk) + both for qb, kk, _, _ in ld]
                    dp = [_dot_nt(dob, vv) for _, _, vv, dob in ld]
                    p = [jnp.exp(sv - lse_ref[rw, :][:, 0:1]) for sv, rw in zip(s, rows)]
                    ds = [(pv * (dpv - dl[rw, :][:, 0:1])).astype(BF16) for pv, dpv, rw in zip(p, dp, rows)]
                    pb = [pv.astype(BF16) for pv in p]
                    gq = [_dot(dsv, kk) for dsv, (_, kk, _, _) in zip(ds, ld)]
                    gk = [_dot_tn(dsv, qb) for dsv, (qb, _, _, _) in zip(ds, ld)]
                    gv = [_dot_tn(pv, dob) for pv, (_, _, _, dob) in zip(pb, ld)]
                    for rw, a, b, c in zip(rows, gq, gk, gv):
                        dqa[rw, :] += a
                        dka[rw, :] += b
                        dva[rw, :] += c
                    return carry

                lax.fori_loop(0, d // units, whole, 0)
                continue

            def pair(i, carry, d=d):
                idx = [_dil_pair_block(i, half, d, nblk) for half in range(DIL_GROUP)]
                rows = [(_strided(qs, DIL_BLOCK, d), _strided(ks, 2 * DIL_BLOCK, d)) for qs, ks, _ in idx]
                ld = [(qf[qr, :].astype(BF16), kf[kr, :].astype(BF16), vf[kr, :].astype(BF16),
                       dof[qr, :].astype(BF16)) for qr, kr in rows]
                s = [_dot_nt(qb, kk) + bias[sel] for (qb, kk, _, _), (_, _, sel) in zip(ld, idx)]
                dp = [_dot_nt(dob, vv) for _, _, vv, dob in ld]
                p = [jnp.exp(sv - lse_ref[qr, :][:, 0:1]) for sv, (qr, _) in zip(s, rows)]
                ds = [(pv * (dpv - dl[qr, :][:, 0:1])).astype(BF16) for pv, dpv, (qr, _) in zip(p, dp, rows)]
                pb = [pv.astype(BF16) for pv in p]
                gq = [_dot(dsv, kk) for dsv, (_, kk, _, _) in zip(ds, ld)]
                gk = [_dot_tn(dsv, qb) for dsv, (qb, _, _, _) in zip(ds, ld)]
                gv = [_dot_tn(pv, dob) for pv, (_, _, _, dob) in zip(pb, ld)]
                for (qr, kr), a, b, c in zip(rows, gq, gk, gv):
                    dqa[qr, :] += a
                    dka[kr, :] += b
                    dva[kr, :] += c
                return carry

            lax.fori_loop(0, nblk // DIL_GROUP, pair, 0)

        def fin(t, carry):
            rows = pl.ds(pl.multiple_of(t * prep_rows, prep_rows), prep_rows)
            cs, sn = cos_ref[rows, :], sin_ref[rows, :]
            gq, gk = dqa[rows, :] * scale, dka[rows, :]
            dq_ref[rows, :] = (gq * cs - pltpu.roll(gq, DIL_HD // 2, 1) * sn).astype(BF16)
            dk_ref[rows, :] = (gk * cs - pltpu.roll(gk, DIL_HD // 2, 1) * sn).astype(BF16)
            dv_ref[rows, :] = dva[rows, :].astype(BF16)
            return carry

        lax.fori_loop(0, s_len // prep_rows, fin, 0)
        comm_after()

    head = lambda base: pl.BlockSpec((s_len, DIL_HD), lambda h: (0, base // DIL_HD + h))
    table = pl.BlockSpec((s_len, DIL_HD), lambda h: (0, 0))
    out = pl.BlockSpec((s_len, DIL_HD), lambda h: (0, h))
    shp = jax.ShapeDtypeStruct((s_len, DIL_HEADS * DIL_HD), BF16)
    return pl.pallas_call(
        body, name="dil_bwd_comm" if comm else "dil_bwd", grid=(DIL_HEADS,),
        out_shape=[shp, shp, shp] + (_comm_out_shapes(*comm) if comm else []),
        in_specs=[head(COL_QB), head(COL_KB), head(COL_VB - NP_F32), table, table,
                  pl.BlockSpec((s_len, DIL_HD), lambda h: (0, DIL_HEADS + h)), out, out] + [ANY] * nc,
        out_specs=[out, out, out] + [ANY] * len(_comm_plumbing(comm)[1]),
        scratch_shapes=[pltpu.VMEM((s_len, DIL_HD), F32) for _ in range(7)]
        + [pltpu.VMEM((2, DIL_BLOCK, 2 * DIL_BLOCK), F32)] + (_comm_scratch(nc) if comm else []),
        compiler_params=_params(("arbitrary",), 56),
    )(pf, pf, pb, cos, sin_signed, do, o_b, lse, *(comm[1] if comm else []))


_PIECES = ((COL_Z, 1024), (COL_QA, 256), (COL_KA, 256), (COL_QB, 512), (COL_KB, 512), (COL_VA, 512), (COL_VB, 512),
           (COL_LR, 128))


def _in_bwd(pieces, w_new, x, dxo, g_pre, scale, comm=None, ts=256):
    s_len = x.shape[0]
    nc = len(comm[1]) if comm else 0
    nco = len(_comm_out_shapes(*comm)) if comm else 0
    npc = len(_PIECES)

    def body(*refs):
        p_refs = refs[:npc]
        w_ref, x_ref, dxo_ref, g_ref, sc_ref = refs[npc:npc + 5]
        cin, (dx_ref, sums_ref), cout = (refs[npc + 5:npc + 5 + nc], refs[npc + 5 + nc:npc + 7 + nc],
                                         refs[npc + 7 + nc:npc + 7 + nc + nco])
        comm_before, comm_after = _comm_hooks(comm, cin, cout, refs[npc + 7 + nc + nco:], steps=s_len // ts)
        comm_before()

        @pl.when(pl.program_id(0) == 0)
        def _():
            sums_ref[...] = jnp.zeros_like(sums_ref)

        dh = jnp.zeros((ts, D_MODEL), F32)
        for p_ref, (col, width) in zip(p_refs, _PIECES):
            dh += _dot_nt(p_ref[...], w_ref[:, col:col + width])
        xv = x_ref[...]
        rstd = lax.rsqrt(jnp.mean(xv * xv, axis=-1, keepdims=True) + EPS)
        xn = xv * rstd
        sums_ref[0:1, :] += jnp.sum(dh, axis=0, keepdims=True)
        sums_ref[1:2, :] += jnp.sum(dh * (xn * g_ref[...]), axis=0, keepdims=True)
        dr = dh * (1.0 + sc_ref[...])
        sums_ref[2:3, :] += jnp.sum(dr * xn, axis=0, keepdims=True)
        dxn = dr * g_ref[...]
        dx_ref[...] = dxo_ref[...] + rstd * (dxn - xn * jnp.mean(dxn * xn, axis=-1, keepdims=True))
        comm_after()

    (g_pre, g_spec), (scale, sc_spec) = _rowvec(g_pre), _rowvec(scale)
    tile = pl.BlockSpec((ts, D_MODEL), lambda i: (i, 0))
    return pl.pallas_call(
        body, name="in_bwd_comm" if comm else "in_bwd", grid=(s_len // ts,),
        out_shape=[jax.ShapeDtypeStruct((s_len, D_MODEL), F32), jax.ShapeDtypeStruct((8, D_MODEL), F32)]
        + (_comm_out_shapes(*comm) if comm else []),
        in_specs=[pl.BlockSpec((ts, width), lambda i: (i, 0)) for _, width in _PIECES]
        + [pl.BlockSpec((D_MODEL, NP), lambda i: (0, 0)), tile, tile, g_spec, sc_spec] + [ANY] * nc,
        out_specs=[tile, pl.BlockSpec((8, D_MODEL), lambda i: (0, 0))] + [ANY] * nco,
        scratch_shapes=_comm_scratch(nc) if comm else [],
        compiler_params=_params(("arbitrary",), 56),
    )(*pieces, w_new, x, dxo, g_pre, scale, *(comm[1] if comm else []))


def _w_in_to_kernel(gathered, comm=None, tr=128):
    n_cin, c_shapes, c_scratch = _comm_plumbing(comm)

    def body(*refs):
        (g_ref,), (o_ref,), _, cin, cout, csem = _split_refs(refs, 1, 1, 0, comm)
        comm_before, comm_after = _comm_hooks(comm, cin, cout, csem, steps=D_MODEL // tr)
        comm_before()
        cols = jnp.concatenate([g_ref[k].astype(F32) for k in range(N_DEV)], axis=1)
        pad = jnp.zeros((tr, LANE - GLA_LOWRANK), F32)
        o_ref[...] = jnp.concatenate(
            [cols[:, 1024:1536], cols[:, 3088:3600], cols[:, 0:512], cols[:, 1552:2576], cols[:, 512:1024],
             cols[:, 2576:3088], cols[:, 1536:1552], pad], axis=1).astype(BF16)
        comm_after()

    return pl.pallas_call(
        body, name="w_in_to_kernel_comm" if comm else "w_in_to_kernel", grid=(D_MODEL // tr,),
        out_shape=[jax.ShapeDtypeStruct((D_MODEL, NP), BF16)] + c_shapes,
        in_specs=[pl.BlockSpec((N_DEV, tr, W_IN_SHARD), lambda i: (0, i, 0))] + [ANY] * n_cin,
        out_specs=[pl.BlockSpec((tr, NP), lambda i: (i, 0))] + [ANY] * len(c_shapes),
        scratch_shapes=c_scratch,
        compiler_params=_params(("arbitrary",)),
    )(gathered, *(comm[1] if comm else []))


def _grad_w_in(h, pieces, ts=512, tr=128):
    s_len = h.shape[0]
    steps = s_len // ts

    def body(*refs):
        h_ref, p_refs = refs[0], refs[1:1 + len(_PIECES)]
        o_ref, acc = refs[1 + len(_PIECES):]

        @pl.when(pl.program_id(0) == 0)
        def _():
            acc[...] = jnp.zeros_like(acc)

        hv = h_ref[...]
        for p_ref, (col, width) in zip(p_refs, _PIECES):
            acc[:, col:col + width] += _dot_tn(hv, p_ref[...])

        @pl.when(pl.program_id(0) == steps - 1)
        def _():
            def rows_out(t, carry):
                rows = pl.ds(pl.multiple_of(t * tr, tr), tr)
                g = acc[rows, :]
                cols = jnp.concatenate(
                    [g[:, COL_QA:COL_QB], g[:, COL_VA:COL_VB], g[:, 0:512], g[:, COL_LR:COL_LR + GLA_LOWRANK],
                     g[:, COL_QB:COL_VA], g[:, COL_VB:COL_LR], g[:, 512:1024]], axis=1)
                for k in range(N_DEV):
                    o_ref[k, rows, :] = cols[:, W_IN_SHARD * k:W_IN_SHARD * (k + 1)].astype(BF16)
                return carry

            lax.fori_loop(0, D_MODEL // tr, rows_out, 0)

    return pl.pallas_call(
        body, name="grad_w_in", grid=(steps,),
        out_shape=jax.ShapeDtypeStruct((N_DEV, D_MODEL, W_IN_SHARD), BF16),
        in_specs=[pl.BlockSpec((ts, D_MODEL), lambda i: (i, 0))]
        + [pl.BlockSpec((ts, width), lambda i: (i, 0)) for _, width in _PIECES],
        out_specs=pl.BlockSpec((N_DEV, D_MODEL, W_IN_SHARD), lambda i: (0, 0, 0)),
        scratch_shapes=[pltpu.VMEM((D_MODEL, NP), F32)],
        compiler_params=_params(("arbitrary",), 56),
    )(h, *pieces)


def _adam_math(w, g, m, v):
    m = ADAM_B1 * m + (1.0 - ADAM_B1) * g
    v = ADAM_B2 * v + (1.0 - ADAM_B2) * (g * g)
    m_hat = m / (1.0 - ADAM_B1 ** ADAM_STEP)
    v_hat = v / (1.0 - ADAM_B2 ** ADAM_STEP)
    delta = -ADAM_LR * (m_hat / (jnp.sqrt(v_hat) + ADAM_EPS) + ADAM_WD * w)
    return delta, m, v


def _adamw(w, parts, m, v, name, tr):
    r, cdim = w.shape
    n_parts = parts.shape[0]

    def body(w_ref, p_ref, m_ref, v_ref, g_ref, d_ref, nm_ref, nv_ref):
        g = p_ref[0].astype(F32)
        for k in range(1, n_parts):
            g = g + p_ref[k].astype(F32)
        g_ref[...] = g
        d_ref[...], nm_ref[...], nv_ref[...] = _adam_math(w_ref[...], g, m_ref[...], v_ref[...])

    tile = pl.BlockSpec((tr, cdim), lambda i: (i, 0))
    shp = jax.ShapeDtypeStruct((r, cdim), F32)
    return pl.pallas_call(
        body, name=name, grid=(r // tr,), out_shape=(shp, shp, shp, shp),
        in_specs=[tile, pl.BlockSpec((n_parts, tr, cdim), lambda i: (0, i, 0)), tile, tile],
        out_specs=(tile, tile, tile, tile),
        compiler_params=_params(("arbitrary",), 40),
    )(w, parts, m, v)


def _adamw_layers(w, parts, m, v, name, tr):
    n_layers, r, cdim = w.shape

    def body(*refs):
        w_ref, p_refs, (m_ref, v_ref) = refs[0], refs[1:1 + n_layers], refs[1 + n_layers:3 + n_layers]
        g_ref, d_ref, nm_ref, nv_ref = refs[3 + n_layers:]
        for l, p_ref in enumerate(p_refs):
            @pl.when(pl.program_id(0) == l)
            def _(p_ref=p_ref):
                g = p_ref[0].astype(F32)
                for k in range(1, p_ref.shape[0]):
                    g = g + p_ref[k].astype(F32)
                g_ref[0] = g
                d_ref[0], nm_ref[0], nv_ref[0] = _adam_math(w_ref[0], g, m_ref[0], v_ref[0])

    tile = pl.BlockSpec((1, tr, cdim), lambda l, i: (l, i, 0))
    part = lambda own: pl.BlockSpec((parts[own].shape[0], tr, cdim), lambda l, i: (0, jnp.where(l == own, i, 0), 0))
    shp = jax.ShapeDtypeStruct(w.shape, F32)
    return pl.pallas_call(
        body, name=name, grid=(n_layers, r // tr), out_shape=(shp, shp, shp, shp),
        in_specs=[tile] + [part(l) for l in range(n_layers)] + [tile, tile],
        out_specs=(tile, tile, tile, tile),
        compiler_params=_params(("arbitrary", "arbitrary"), 40),
    )(w, *parts, m, v)


def _row(vec, width):
    vec = vec.reshape(1, -1)
    return jnp.pad(vec, ((0, 0), (0, width - vec.shape[1])))


def kernel(x, c, w_ada, b_ada, g_pre, w_in, w_gate_up, b_gate_up, g_gla, g_dil, w_out, g_post, loss_target, m_w_ada, m_b_ada, m_g_pre, m_w_in, m_w_gate_up, m_b_gate_up, m_g_gla, m_g_dil, m_w_out, m_g_post, v_w_ada, v_b_ada, v_g_pre, v_w_in, v_w_gate_up, v_b_gate_up, v_g_gla, v_g_dil, v_w_out, v_g_post):
    px, py, pc = _my_position()
    me = _linear(px, py, pc)
    xs = x[0]
    target = loss_target[0]
    s_len = xs.shape[0]
    assert s_len % (DIL_BLOCK * max(DIL_DILATIONS) * 2) == 0 and xs.shape[1] == D_MODEL

    w_in_b, w_out_b = w_in.astype(BF16), w_out.astype(BF16)
    c_rows, wgu_all, w_in_all = _comm_call(
        "gather", [jnp.pad(c, ((0, 7), (0, 0))), w_gate_up.reshape(DEPTH * GLA_LOWRANK, GU_SHARD), w_in_b[0]],
        "gather_first")
    c_all = c_rows.reshape(N_DEV, 8, D_MODEL)[:, 0]
    mod_part = _mod_fwd(c_all, w_ada)
    w_new, mod_all = _w_in_to_kernel(w_in_all.reshape(N_DEV, D_MODEL, W_IN_SHARD),
                                     comm=("gather", [mod_part.reshape(DEPTH * N_DEV, ADA_SHARD)]))
    mod_all = mod_all.reshape(N_DEV, DEPTH, N_DEV, ADA_SHARD)
    mod_mine = lax.dynamic_index_in_dim(mod_all, me, axis=2, keepdims=False)
    mod = jnp.transpose(mod_mine, (1, 0, 2)).reshape(DEPTH, 3 * D_MODEL) + b_ada
    wgu_full = jnp.transpose(wgu_all.reshape(N_DEV, DEPTH, GLA_LOWRANK, GU_SHARD), (1, 2, 0, 3)).reshape(
        DEPTH, GLA_LOWRANK, GU_COLS)
    wgu_pad = jnp.pad(wgu_full, ((0, 0), (0, LANE - GLA_LOWRANK), (0, 0))).astype(BF16)

    cos, sin_signed = _rope_tables(s_len)
    g_heads = jnp.concatenate([g_gla, g_dil], axis=1)

    saved = []
    xl = xs
    for l in range(DEPTH):
        shift, scale, gate = ((mod, l, k) for k in range(3))
        if l > 0:
            w_new = _w_in_to_kernel(w_in_all.reshape(N_DEV, D_MODEL, W_IN_SHARD))[0]
        pf, pb, h, w_out_l = _prenorm_proj(xl, (g_pre, l, 0), scale, shift, w_new, cos, sin_signed,
                                           comm=("gather", [w_out_b[l]]))
        o_a, states = _gla_fwd(pf, pb, wgu_pad, b_gate_up, l)
        if l + 1 < DEPTH:
            o_b, lse, w_in_all = _dil_fwd(pf, pb, comm=("gather", [w_in_b[l + 1]]))
        else:
            o_b, lse = _dil_fwd(pf, pb)
        if l + 1 < DEPTH:
            x_next, u = _post_fwd(o_a, o_b, pf, (g_heads, l, 0), w_out_l, xl, gate, (g_post, l, 0))
        else:
            dx, u, loss_part = _post_fwd(o_a, o_b, pf, (g_heads, l, 0), w_out_l, xl, gate, (g_post, l, 0),
                                         target=target)
        saved.append((xl, scale, gate, w_new, w_out_l, pf, pb, h, o_a, states, o_b, lse, u))
        xl = x_next

    small_rows = []
    gin_slots, gin_parts, gout_parts = None, [None] * DEPTH, [None] * DEPTH
    for l in reversed(range(DEPTH)):
        x_in, scale, gate, w_new, w_out_l, pf, pb, h, o_a, states, o_b, lse, u = saved[l]
        do, dz, sums_post, gout_slots = _post_bwd(dx, u, gate, (g_post, l, 0), w_out_l, o_a, o_b, pf, (g_heads, l, 0))
        dq_a, dk_a, dv_a, dlr2, dwgu, dbgu, arrived = _gla_bwd(pf, pb, wgu_pad, b_gate_up, l, states, do,
                                                               comm=("exchange", [gout_slots]))
        gout_parts[l] = arrived.reshape(N_DEV, OUT_SHARD, D_MODEL)
        if gin_slots is not None:
            dq_b, dk_b, dv_b, arrived, _, _ = _dil_bwd(pf, pb, cos, sin_signed, do, o_b, lse,
                                                       comm=("pairsum_exchange", [gin_slots]))
            gin_parts[l + 1] = arrived.reshape(N_DEV // 2, D_MODEL, W_IN_SHARD)
        else:
            dq_b, dk_b, dv_b = _dil_bwd(pf, pb, cos, sin_signed, do, o_b, lse)
        dlr = (dlr2[0] + dlr2[1]).astype(BF16)
        pieces = (dz, dq_a, dk_a, dq_b, dk_b, dv_a, dv_b, dlr)
        gin_slots = _grad_w_in(h, pieces).reshape(N_DEV * D_MODEL, W_IN_SHARD)
        if l == 0:
            dx, sums_in, arrived, _, _ = _in_bwd(pieces, w_new, x_in, dx, (g_pre, l, 0), scale,
                                                 comm=("pairsum_exchange", [gin_slots]))
            gin_parts[0] = arrived.reshape(N_DEV // 2, D_MODEL, W_IN_SHARD)
        else:
            dx, sums_in = _in_bwd(pieces, w_new, x_in, dx, (g_pre, l, 0), scale)
        dmod = jnp.concatenate([sums_in[0], sums_in[1], sums_post[0]])
        vecs = jnp.concatenate([sums_in[2], sums_post[1], sums_post[2], dbgu[0]])
        small_rows[0:0] = [_row(dmod, 4096), _row(vecs, 4096), _row(dwgu[:GLA_LOWRANK], 4096)]
    grad_x = dx[None]

    flat = lambda a, rows: a.reshape(rows, a.shape[-1])
    r_ada = DEPTH * D_MODEL
    g_w_in, d_w_in, nm_w_in, nv_w_in = _adamw_layers(w_in, gin_parts, m_w_in, v_w_in, "adamw_w_in", 256)
    g_w_out, d_w_out, nm_w_out, nv_w_out = _adamw_layers(w_out, gout_parts, m_w_out, v_w_out, "adamw_w_out", 128)

    small_rows += [_row(loss_part[0, 0:1], 4096), jnp.zeros((1, 4096), F32)]
    small = _all_gather(jnp.concatenate(small_rows, axis=0), "gather_small").reshape(N_DEV, 8, 4096)
    dmod_all = jnp.stack([small[:, 0, :3 * D_MODEL], small[:, 3, :3 * D_MODEL]])
    dmod_cols = lax.dynamic_slice_in_dim(dmod_all, me * ADA_SHARD, ADA_SHARD, axis=2)
    gwa = _w_ada_grad(c_all, dmod_cols).reshape(1, r_ada, ADA_SHARD)
    g_w_ada, d_w_ada, nm_w_ada, nv_w_ada = (
        t.reshape(w_ada.shape) for t in _adamw(flat(w_ada, r_ada), gwa, flat(m_w_ada, r_ada), flat(v_w_ada, r_ada),
                                               "adamw_w_ada", 256))

    where = ((0, 0), (1, 0), (1, 1024), (1, 2048), (1, 2560), (1, 3072))
    replicated = [(b_ada, m_b_ada, v_b_ada), (g_pre, m_g_pre, v_g_pre), (g_post, m_g_post, v_g_post),
                  (g_gla, m_g_gla, v_g_gla), (g_dil, m_g_dil, v_g_dil), (b_gate_up, m_b_gate_up, v_b_gate_up)]
    updated, loss = _adamw_replicated(small, replicated, where, loss_at=(6, 0))
    ((g_b_ada, d_b_ada, nm_b_ada, nv_b_ada), (g_g_pre, d_g_pre, nm_g_pre, nv_g_pre),
     (g_g_post, d_g_post, nm_g_post, nv_g_post), (g_g_gla, d_g_gla, nm_g_gla, nv_g_gla),
     (g_g_dil, d_g_dil, nm_g_dil, nv_g_dil), (g_b_gu, d_b_gu, nm_b_gu, nv_b_gu)) = updated
    gu_parts = jnp.stack([small[:, 2], small[:, 5]], axis=1).reshape(N_DEV, DEPTH, GLA_LOWRANK, GU_COLS)
    gu_parts = lax.dynamic_slice_in_dim(gu_parts, me * GU_SHARD, GU_SHARD, axis=3).reshape(
        N_DEV, DEPTH * GLA_LOWRANK, GU_SHARD)
    r_gu = DEPTH * GLA_LOWRANK
    g_w_gu, d_w_gu, nm_w_gu, nv_w_gu = (
        t.reshape(w_gate_up.shape) for t in _adamw(flat(w_gate_up, r_gu), gu_parts, flat(m_w_gate_up, r_gu),
                                                   flat(v_w_gate_up, r_gu), "adamw_w_gate_up", r_gu))
    return (loss, grad_x,
            g_w_ada, g_b_ada, g_g_pre, g_w_in, g_w_gu, g_b_gu, g_g_gla, g_g_dil, g_w_out, g_g_post,
            d_w_ada, d_b_ada, d_g_pre, d_w_in, d_w_gu, d_b_gu, d_g_gla, d_g_dil, d_w_out, d_g_post,
            nm_w_ada, nm_b_ada, nm_g_pre, nm_w_in, nm_w_gu, nm_b_gu, nm_g_gla, nm_g_dil, nm_w_out, nm_g_post,
            nv_w_ada, nv_b_ada, nv_g_pre, nv_w_in, nv_w_gu, nv_b_gu, nv_g_gla, nv_g_dil, nv_w_out, nv_g_post)


def _adamw_replicated(small, params, where, loss_at):
    n_parts = small.shape[0]

    def body(*refs):
        s_ref, p_refs, o_refs = refs[0], refs[1:1 + 3 * len(params)], refs[1 + 3 * len(params):]
        total = s_ref[0]
        for k in range(1, n_parts):
            total = total + s_ref[k]
        for i, (row, col) in enumerate(where):
            w_ref, m_ref, v_ref = p_refs[3 * i:3 * i + 3]
            n = w_ref.shape[1]
            g = jnp.concatenate([total[row + 3 * l:row + 3 * l + 1, col:col + n] for l in range(DEPTH)], axis=0)
            o_refs[4 * i][...] = g
            o_refs[4 * i + 1][...], o_refs[4 * i + 2][...], o_refs[4 * i + 3][...] = _adam_math(
                w_ref[...], g, m_ref[...], v_ref[...])
        o_refs[-1][...] = jnp.broadcast_to(total[loss_at[0]:loss_at[0] + 1, loss_at[1]:loss_at[1] + 1], (8, LANE))

    flat = [a for p in params for a in p]
    shapes = [jax.ShapeDtypeStruct(p[0].shape, F32) for p in params for _ in range(4)]
    outs = pl.pallas_call(body, name="adamw_replicated",
                          out_shape=shapes + [jax.ShapeDtypeStruct((8, LANE), F32)])(small, *flat)
    return [tuple(outs[4 * i:4 * i + 4]) for i in range(len(params))], outs[-1][0, 0]
```

```python
import functools
import math

import jax
import jax.numpy as jnp
from jax import lax
from jax.experimental import pallas as pl
from jax.experimental.pallas import tpu as pltpu

F32 = jnp.float32
BF16 = jnp.bfloat16

N_DEV = 8
D_MODEL = 1024
DEPTH = 2
GLA_HEADS = 4
GLA_DK = 64
GLA_DV = 128
GLA_CHUNK = 64
GLA_TAU = 16.0
GLA_LOWRANK = 16
DIL_HEADS = 4
DIL_HD = 128
DIL_BLOCK = 128
DIL_DILATIONS = (1, 4, 16)
ROPE_THETA = 10000.0
EPS = 1e-6
IN_COLS = 3600
W_IN_SHARD = IN_COLS // N_DEV
ADA_SHARD = 3 * D_MODEL // N_DEV
OUT_SHARD = D_MODEL // N_DEV
GU_COLS = GLA_HEADS * GLA_DK
GU_SHARD = GU_COLS // N_DEV

ADAM_LR = 0.001
ADAM_B1 = 0.9
ADAM_B2 = 0.999
ADAM_EPS = 1e-08
ADAM_WD = 0.01
ADAM_STEP = 10

NP = 3712
COL_Z, COL_QA, COL_KA, COL_QB, COL_KB, COL_VA, COL_VB, COL_LR = 0, 1024, 1280, 1536, 2048, 2560, 3072, 3584
NP_F32 = COL_VA
NP_BF16 = NP - NP_F32
LANE = 128
MASK_VALUE = -1e30

MESH = pl.DeviceIdType.MESH
ANY = pl.BlockSpec(memory_space=pl.ANY)


def _params(sem=None, vmem_mb=None):
    kw = {}
    if sem is not None:
        kw["dimension_semantics"] = sem
    if vmem_mb is not None:
        kw["vmem_limit_bytes"] = vmem_mb * 1024 * 1024
    return pltpu.CompilerParams(**kw)


def _dot(a, b):
    return jnp.dot(a, b, preferred_element_type=F32)


def _dot_nt(a, b):
    return lax.dot_general(a, b, (((1,), (1,)), ((), ())), preferred_element_type=F32)


def _dot_tn(a, b):
    return lax.dot_general(a, b, (((0,), (0,)), ((), ())), preferred_element_type=F32)


def _sigmoid(z):
    return 1.0 / (1.0 + jnp.exp(-z))


def _log_sigmoid(z):
    return jnp.minimum(z, 0.0) - jnp.log(1.0 + jnp.exp(-jnp.abs(z)))


def _rowvec(v, width=D_MODEL):
    arr, row, cb = v
    return arr.reshape(arr.shape[0], 1, arr.shape[1]), pl.BlockSpec((None, 1, width), lambda *_: (row, 0, cb))


def _my_position():
    return lax.axis_index("x"), lax.axis_index("y"), lax.axis_index("c")


def _linear(px, py, pc):
    return 4 * px + 2 * py + pc


def _gather_phase(phase, x_ref, out_ref, send_sem, recv_sem, local_sem):
    m = x_ref.shape[0]
    x, y, c = _my_position()
    me, sibling = (x, y, c), (x, y, 1 - c)
    chips = [(1 - x, y), (x, 1 - y), (1 - x, 1 - y)]

    def rows(px, py, pc):
        return out_ref.at[pl.ds(_linear(px, py, pc) * m, m), :]

    def copy(k, block, to, src=None):
        return pltpu.make_async_remote_copy(
            src_ref=rows(*block) if src is None else src, dst_ref=rows(*block),
            send_sem=send_sem(k), recv_sem=recv_sem(k), device_id=to, device_id_type=MESH)

    mine = pltpu.make_async_copy(x_ref, rows(*me), local_sem)
    first = [copy(0, me, sibling, src=x_ref)] + [copy(1 + j, me, (*chip, c), src=x_ref) for j, chip in enumerate(chips)]
    passed = [copy(4 + j, (*chip, c), sibling) for j, chip in enumerate(chips)]
    if phase == "start":
        mine.start()
        for cp in first:
            cp.start()
    elif phase == "forward":
        for j, chip in enumerate(chips):
            copy(1 + j, (*chip, c), me).wait_recv()
            passed[j].start()
    else:
        copy(0, sibling, me).wait_recv()
        for j, chip in enumerate(chips):
            copy(4 + j, (*chip, 1 - c), me).wait_recv()
        for cp in first + passed:
            cp.wait_send()
        mine.wait()


def _exchange_phase(phase, x_ref, out_ref, send_sem, recv_sem, local_sem):
    m = x_ref.shape[0] // N_DEV
    x, y, c = _my_position()
    me = _linear(x, y, c)

    def rows(ref, idx):
        return ref.at[pl.ds(idx * m, m), :]

    peers = [(1 - x if j & 4 else x, 1 - y if j & 2 else y, 1 - c if j & 1 else c) for j in range(1, N_DEV)]
    local = pltpu.make_async_copy(rows(x_ref, me), rows(out_ref, me), local_sem)
    sends = [pltpu.make_async_remote_copy(
        src_ref=rows(x_ref, _linear(*peer)), dst_ref=rows(out_ref, me),
        send_sem=send_sem(j), recv_sem=recv_sem(j), device_id=peer, device_id_type=MESH) for j, peer in enumerate(peers)]
    if phase == "start":
        local.start()
        for cp in sends:
            cp.start()
    else:
        for j, peer in enumerate(peers):
            pltpu.make_async_remote_copy(
                src_ref=rows(x_ref, _linear(*peer)), dst_ref=rows(out_ref, _linear(*peer)),
                send_sem=send_sem(j), recv_sem=recv_sem(j), device_id=peer, device_id_type=MESH).wait_recv()
        for cp in sends:
            cp.wait_send()
        local.wait()


def _pairsum_exchange_phase(phase, x_ref, out_refs, send_sem, recv_sem, local_sem):
    out_ref, stage_ref, pair_ref = out_refs
    m, n = x_ref.shape[0] // N_DEV, x_ref.shape[1]
    x, y, c = _my_position()
    mine = 2 * x + y
    chips = [(qx, qy) for qx in range(2) for qy in range(2)]
    others = [(1 - x, y), (x, 1 - y), (1 - x, 1 - y)]

    def rows(ref, idx):
        return ref.at[pl.ds(idx * m, m), :]

    def remote(src, dst, k, to):
        return pltpu.make_async_remote_copy(src_ref=src, dst_ref=dst, send_sem=send_sem(k), recv_sem=recv_sem(k),
                                            device_id=to, device_id_type=MESH)

    to_sibling = [remote(rows(x_ref, _linear(qx, qy, 1 - c)), rows(stage_ref, q), q, (x, y, 1 - c))
                  for q, (qx, qy) in enumerate(chips)]
    to_chips = [remote(rows(pair_ref, 2 * qx + qy), rows(out_ref, mine), 4 + j, (qx, qy, c))
                for j, (qx, qy) in enumerate(others)]
    keep = pltpu.make_async_copy(rows(pair_ref, mine), rows(out_ref, mine), local_sem)
    if phase == "start":
        for cp in to_sibling:
            cp.start()
    elif phase == "reduce":
        for cp in to_sibling:
            cp.wait_recv()

        def through_vmem(a_buf, b_buf, sems):
            tr = 128
            loads = [(pltpu.make_async_copy(rows(x_ref, _linear(qx, qy, c)), a_buf.at[q % 2], sems.at[q % 2]),
                      pltpu.make_async_copy(rows(stage_ref, q), b_buf.at[q % 2], sems.at[2 + q % 2]))
                     for q, (qx, qy) in enumerate(chips)]
            stores = [pltpu.make_async_copy(a_buf.at[q % 2], rows(pair_ref, q), sems.at[4 + q % 2]) for q in range(4)]
            for cp in loads[0]:
                cp.start()
            for q in range(4):
                for cp in loads[q]:
                    cp.wait()
                if q + 1 < 4:
                    if q >= 1:
                        stores[q - 1].wait()
                    for cp in loads[q + 1]:
                        cp.start()

                def add(r, carry, q=q):
                    tile = pl.ds(pl.multiple_of(r * tr, tr), tr)
                    a_buf[q % 2, tile, :] = (a_buf[q % 2, tile, :].astype(F32)
                                             + b_buf[q % 2, tile, :].astype(F32)).astype(x_ref.dtype)
                    return carry

                lax.fori_loop(0, m // tr, add, 0)
                stores[q].start()
            stores[2].wait()
            stores[3].wait()

        pl.run_scoped(through_vmem, pltpu.VMEM((2, m, n), x_ref.dtype), pltpu.VMEM((2, m, n), x_ref.dtype),
                      pltpu.SemaphoreType.DMA((6,)))
    elif phase == "send":
        keep.start()
        for cp in to_chips:
            cp.start()
    else:
        for j, (qx, qy) in enumerate(others):
            remote(rows(pair_ref, mine), rows(out_ref, 2 * qx + qy), 4 + j, (qx, qy, c)).wait_recv()
        for cp in to_sibling + to_chips:
            cp.wait_send()
        keep.wait()


_COMM_PHASES = {"gather": (_gather_phase, ("start", "forward", "finish")),
                "exchange": (_exchange_phase, ("start", "finish")),
                "pairsum_exchange": (_pairsum_exchange_phase, ("start", "reduce", "send", "finish"))}


def _comm_scratch(n_arrays):
    return [pltpu.SemaphoreType.DMA((n_arrays, 7)), pltpu.SemaphoreType.DMA((n_arrays, 7)),
            pltpu.SemaphoreType.DMA((n_arrays,))]


def _comm_run(kind, phases, x_refs, out_refs, send_sems, recv_sems, local_sems):
    fn = _COMM_PHASES[kind][0]
    per = len(out_refs) // len(x_refs)
    for phase in phases:
        for a, x_ref in enumerate(x_refs):
            outs = out_refs[a] if per == 1 else tuple(out_refs[per * a:per * (a + 1)])
            fn(phase, x_ref, outs, lambda k, a=a: send_sems.at[a, k], lambda k, a=a: recv_sems.at[a, k],
               local_sems.at[a])


def _comm_out_shapes(kind, arrays):
    if kind == "pairsum_exchange":
        return [jax.ShapeDtypeStruct((a.shape[0] // 2, a.shape[1]), a.dtype) for a in arrays for _ in range(3)]
    return [jax.ShapeDtypeStruct((N_DEV * a.shape[0], a.shape[1]) if kind == "gather" else a.shape, a.dtype)
            for a in arrays]


def _comm_call(kind, arrays, name):
    n = len(arrays)
    shapes = _comm_out_shapes(kind, arrays)

    def body(*refs):
        _comm_run(kind, _COMM_PHASES[kind][1], refs[:n], refs[n:n + len(shapes)], *refs[n + len(shapes):])

    return pl.pallas_call(body, name=name, out_shape=shapes, in_specs=[ANY] * n, out_specs=[ANY] * len(shapes),
                          scratch_shapes=_comm_scratch(n))(*arrays)


def _all_gather(xs, name):
    return _comm_call("gather", [xs], name)[0]


def _mod_fwd(c_all, w_ada):
    def body(c_ref, w_ref, o_ref):
        cv = c_ref[...]
        sc = cv * _sigmoid(cv)
        o_ref[0] = _dot(sc.astype(BF16), w_ref[0].astype(BF16))

    return pl.pallas_call(
        body, name="mod_fwd", grid=(DEPTH,),
        out_shape=jax.ShapeDtypeStruct((DEPTH, N_DEV, ADA_SHARD), F32),
        in_specs=[pl.BlockSpec((N_DEV, D_MODEL), lambda l: (0, 0)),
                  pl.BlockSpec((1, D_MODEL, ADA_SHARD), lambda l: (l, 0, 0))],
        out_specs=pl.BlockSpec((1, N_DEV, ADA_SHARD), lambda l: (l, 0, 0)),
        compiler_params=_params(("arbitrary",)),
    )(c_all, w_ada)


def _w_ada_grad(c_all, dmod_cols):
    def body(c_ref, d_ref, o_ref):
        cv = c_ref[...]
        sc = cv * _sigmoid(cv)
        o_ref[0] = lax.dot_general(sc, d_ref[0], (((0,), (0,)), ((), ())), precision=lax.Precision.HIGHEST,
                                   preferred_element_type=F32)

    return pl.pallas_call(
        body, name="w_ada_grad", grid=(DEPTH,),
        out_shape=jax.ShapeDtypeStruct((DEPTH, D_MODEL, ADA_SHARD), F32),
        in_specs=[pl.BlockSpec((N_DEV, D_MODEL), lambda l: (0, 0)),
                  pl.BlockSpec((1, N_DEV, ADA_SHARD), lambda l: (l, 0, 0))],
        out_specs=pl.BlockSpec((1, D_MODEL, ADA_SHARD), lambda l: (l, 0, 0)),
        compiler_params=_params(("arbitrary",)),
    )(c_all, dmod_cols)


def _comm_plumbing(comm):
    if not comm:
        return 0, [], []
    return len(comm[1]), _comm_out_shapes(*comm), _comm_scratch(len(comm[1]))


def _split_refs(refs, n_in, n_out, n_scratch, comm):
    ci, shapes, _ = _comm_plumbing(comm)
    co = len(shapes)
    a, b, c = n_in + ci, n_in + ci + n_out, n_in + ci + n_out + co
    return refs[:n_in], refs[a:b], refs[c:c + n_scratch], refs[n_in:a], refs[b:c], refs[c + n_scratch:]


def _prenorm_proj(x, g_pre, scale, shift, w_new, cos, sin_signed, comm=None, ts=256):
    s_len = x.shape[0]
    n_cin, c_shapes, c_scratch = _comm_plumbing(comm)

    def body(*refs):
        (x_ref, g_ref, sc_ref, sh_ref, w_ref, cos_ref, sin_ref), (pf_ref, pb_ref, h_ref), _, cin, cout, csem = (
            _split_refs(refs, 7, 3, 0, comm))
        comm_before, comm_after = _comm_hooks(comm, cin, cout, csem, steps=s_len // ts)
        comm_before()
        xv = x_ref[...]
        rstd = lax.rsqrt(jnp.mean(xv * xv, axis=-1, keepdims=True) + EPS)
        h = (xv * rstd * g_ref[...]) * (1.0 + sc_ref[...]) + sh_ref[...]
        hb = h.astype(BF16)
        h_ref[...] = hb
        for j in range(0, NP, 512):
            w = min(512, NP - j)
            acc = _dot(hb, w_ref[:, j:j + w])
            if COL_QB <= j < COL_VA:
                for lo in range(0, w, DIL_HD):
                    pf_ref[:, j + lo:j + lo + DIL_HD] = _rope(acc[:, lo:lo + DIL_HD], cos_ref[...], sin_ref[...])
            elif j < NP_F32:
                pf_ref[:, j:j + w] = acc
            else:
                pb_ref[:, j - NP_F32:j - NP_F32 + w] = acc.astype(BF16)
        comm_after()

    (g_pre, g_spec), (scale, sc_spec), (shift, sh_spec) = _rowvec(g_pre), _rowvec(scale), _rowvec(shift)
    return pl.pallas_call(
        body, name="prenorm_proj_comm" if comm else "prenorm_proj", grid=(s_len // ts,),
        out_shape=[jax.ShapeDtypeStruct((s_len, NP_F32), F32), jax.ShapeDtypeStruct((s_len, NP_BF16), BF16),
                   jax.ShapeDtypeStruct((s_len, D_MODEL), BF16)] + c_shapes,
        in_specs=[pl.BlockSpec((ts, D_MODEL), lambda i: (i, 0)), g_spec, sc_spec, sh_spec,
                  pl.BlockSpec((D_MODEL, NP), lambda i: (0, 0)), pl.BlockSpec((ts, DIL_HD), lambda i: (i, 0)),
                  pl.BlockSpec((ts, DIL_HD), lambda i: (i, 0))] + [ANY] * n_cin,
        out_specs=[pl.BlockSpec((ts, NP_F32), lambda i: (i, 0)), pl.BlockSpec((ts, NP_BF16), lambda i: (i, 0)),
                   pl.BlockSpec((ts, D_MODEL), lambda i: (i, 0))] + [ANY] * len(c_shapes),
        scratch_shapes=c_scratch,
        compiler_params=_params(("arbitrary",), 48),
    )(x, g_pre, scale, shift, w_new, cos, sin_signed, *(comm[1] if comm else []))


GLA_GROUP = 16


def _gla_group_rows(t):
    return [pl.ds(pl.multiple_of((t * GLA_GROUP + j) * GLA_CHUNK, GLA_CHUNK), GLA_CHUNK) for j in range(GLA_GROUP)]


def _gla_chunks_common(q_ref, k_ref, lr_ref, wgu_ref, bgu_ref, rows_list):
    c = GLA_CHUNK
    ri = lax.broadcasted_iota(jnp.int32, (c, c), 0)
    ci = lax.broadcasted_iota(jnp.int32, (c, c), 1)
    tril = (ri >= ci).astype(F32)
    zs = [_dot(lr_ref[rows, :], wgu_ref[...]) + bgu_ref[...] for rows in rows_list]
    las = [_log_sigmoid(z) * (1.0 / GLA_TAU) for z in zs]
    bs = [jnp.dot(tril, la, precision=lax.Precision.HIGHEST, preferred_element_type=F32) for la in las]
    out = []
    for rows, z, b in zip(rows_list, zs, bs):
        q = q_ref[rows, :] * (GLA_DK ** -0.5)
        k = k_ref[rows, :]
        bl = b[c - 1:c, :]
        out.append(dict(z=z, b=b, bl=bl, qe=q * jnp.exp(b), ke=k * jnp.exp(-b), kend=k * jnp.exp(bl - b),
                        dec=jnp.exp(bl)))
    return out, ri, ci


def _head_lane_mask(hh):
    return (lax.broadcasted_iota(jnp.int32, (1, LANE), 1) // GLA_DK) == hh


def _state_block_mask():
    r = lax.broadcasted_iota(jnp.int32, (2 * GLA_DV, LANE), 0) // GLA_DV
    cc = lax.broadcasted_iota(jnp.int32, (2 * GLA_DV, LANE), 1) // GLA_DK
    return r == cc


def _gla_fwd(pf, pb, wgu, bgu, layer, comm=None):
    s_len = pf.shape[0]
    nc = s_len // GLA_CHUNK
    ncomm = len(comm[1]) if comm else 0

    def body(*refs):
        q_ref, k_ref, v_ref, lr_ref, wgu_ref, bgu_ref = refs[:6]
        cin, (o_ref, st_ref), cout = refs[6:6 + ncomm], refs[6 + ncomm:8 + ncomm], refs[8 + ncomm:8 + 2 * ncomm]
        qe_s, cs_s, dec_s = refs[8 + 2 * ncomm:11 + 2 * ncomm]
        comm_before, comm_after = _comm_hooks(comm, cin, cout, refs[11 + 2 * ncomm:], steps=2)
        comm_before()
        bd = _state_block_mask()

        def local(t, carry):
            rows_list = _gla_group_rows(t)
            cm, ri, ci = _gla_chunks_common(q_ref, k_ref, lr_ref, wgu_ref, bgu_ref, rows_list)
            vs = [v_ref[rows, :] for rows in rows_list]
            kebs = [c["ke"].astype(BF16) for c in cm]
            a = [[jnp.where(ri >= ci, _dot_nt(jnp.where(_head_lane_mask(hh), c["qe"], 0.0).astype(BF16), keb), 0.0)
                  .astype(BF16) for hh in range(2)] for c, keb in zip(cm, kebs)]
            oi = [[_dot(ah[hh], v[:, hh * GLA_DV:(hh + 1) * GLA_DV]) for hh in range(2)] for ah, v in zip(a, vs)]
            cs = [jnp.where(bd, _dot_tn(v, c["kend"].astype(BF16)), 0.0) for c, v in zip(cm, vs)]
            for j, (rows, c) in enumerate(zip(rows_list, cm)):
                n = t * GLA_GROUP + j
                o_ref[rows, :] = jnp.concatenate(oi[j], axis=1)
                qe_s[rows, :] = c["qe"].astype(BF16)
                cs_s[n] = cs[j]
                dec_s[n] = jnp.broadcast_to(c["dec"], (8, LANE))
            return carry

        lax.fori_loop(0, nc // GLA_GROUP, local, 0)

        def scan(n, st):
            st_ref[0, n] = st.astype(BF16)
            return dec_s[n][0:1, :] * st + cs_s[n]

        lax.fori_loop(0, nc, scan, jnp.zeros((2 * GLA_DV, LANE), F32))

        def inter(t, carry):
            rows_list = _gla_group_rows(t)
            add = [_dot_nt(qe_s[rows, :], st_ref[0, t * GLA_GROUP + j]) for j, rows in enumerate(rows_list)]
            for rows, av in zip(rows_list, add):
                o_ref[rows, :] = o_ref[rows, :] + av
            return carry

        lax.fori_loop(0, nc // GLA_GROUP, inter, 0)
        comm_after()

    return pl.pallas_call(
        body, name="gla_fwd_comm" if comm else "gla_fwd", grid=(2,),
        out_shape=[jax.ShapeDtypeStruct((s_len, GLA_HEADS * GLA_DV), F32),
                   jax.ShapeDtypeStruct((2, nc, 2 * GLA_DV, LANE), BF16)] + (_comm_out_shapes(*comm) if comm else []),
        in_specs=[pl.BlockSpec((s_len, LANE), lambda g: (0, COL_QA // LANE + g)),
                  pl.BlockSpec((s_len, LANE), lambda g: (0, COL_KA // LANE + g)),
                  pl.BlockSpec((s_len, 2 * GLA_DV), lambda g: (0, (COL_VA - NP_F32) // (2 * GLA_DV) + g)),
                  pl.BlockSpec((s_len, LANE), lambda g: (0, (COL_LR - NP_F32) // LANE)),
                  pl.BlockSpec((None, LANE, LANE), lambda g: (layer, 0, g)),
                  pl.BlockSpec((None, 1, LANE), lambda g: (layer, 0, g))] + [ANY] * ncomm,
        out_specs=[pl.BlockSpec((s_len, 2 * GLA_DV), lambda g: (0, g)),
                   pl.BlockSpec((1, nc, 2 * GLA_DV, LANE), lambda g: (g, 0, 0, 0))] + [ANY] * ncomm,
        scratch_shapes=[pltpu.VMEM((s_len, LANE), BF16), pltpu.VMEM((nc, 2 * GLA_DV, LANE), F32),
                        pltpu.VMEM((nc, 8, LANE), F32)] + (_comm_scratch(ncomm) if comm else []),
        compiler_params=_params(("arbitrary",), 56),
    )(pf, pf, pb, pb, wgu, bgu.reshape(bgu.shape[0], 1, GU_COLS), *(comm[1] if comm else []))


def _rope_tables(s_len):
    inv_freq = ROPE_THETA ** (-jnp.arange(0, DIL_HD, 2, dtype=F32) / DIL_HD)
    ang = jnp.arange(s_len, dtype=F32)[:, None] * inv_freq[None, :]
    cos, sin = jnp.cos(ang), jnp.sin(ang)
    return jnp.concatenate([cos, cos], axis=1), jnp.concatenate([-sin, sin], axis=1)


def _rope(xv, cos, sin_signed):
    return xv * cos + pltpu.roll(xv, DIL_HD // 2, 1) * sin_signed


DIL_GROUP = 8


def _dil_pair_block(i, half, d, nblk, group=DIL_GROUP):
    nb = nblk // d
    j = i + half * (nblk // group)
    if nb >= 2 * group:
        r, n = j % d, j // d
    else:
        r, n = j // nb, j % nb
    kb = jnp.maximum(n - 1, 0)
    qs = r + d * DIL_BLOCK * n
    ks = r + d * DIL_BLOCK * kb
    return qs, ks, jnp.minimum(n, 1)


def _dil_fill_bias(bias):
    qi = lax.broadcasted_iota(jnp.int32, (DIL_BLOCK, 2 * DIL_BLOCK), 0)
    kj = lax.broadcasted_iota(jnp.int32, (DIL_BLOCK, 2 * DIL_BLOCK), 1)
    for sel in range(2):
        dist = qi - kj + DIL_BLOCK * sel
        bias[sel] = jnp.where((dist >= 0) & (dist <= DIL_BLOCK), 0.0, MASK_VALUE)


def _strided(start, size, d):
    return pl.ds(start, size) if d == 1 else pl.ds(start, size, stride=d)


def _comm_hooks(comm, cin, cout, csem, steps=DIL_HEADS):
    def before():
        if comm:
            @pl.when(pl.program_id(0) == 0)
            def _():
                _comm_run(comm[0], ("start",), cin, cout, *csem)

            if comm[0] == "gather":
                @pl.when(pl.program_id(0) == steps - 1)
                def _():
                    _comm_run(comm[0], ("forward",), cin, cout, *csem)

            if comm[0] == "pairsum_exchange":
                @pl.when(pl.program_id(0) == (1 if steps <= 4 else 2))
                def _():
                    _comm_run(comm[0], ("reduce", "send"), cin, cout, *csem)

    def after():
        if comm:
            @pl.when(pl.program_id(0) == steps - 1)
            def _():
                _comm_run(comm[0], ("finish",), cin, cout, *csem)

    return before, after


def _dil_fwd(pf, pb, comm=None):
    s_len = pf.shape[0]
    nblk = s_len // DIL_BLOCK
    prep_rows = 256
    scale = DIL_HD ** -0.5
    nc = len(comm[1]) if comm else 0

    def body(*refs):
        ((qf, kf, v_ref), (o_ref, lse_ref), (vf, o0, o1, o2, l0, l1, l2, bias), cin, cout, csem) = _split_refs(
            refs, 3, 2, 8, comm)
        comm_before, comm_after = _comm_hooks(comm, cin, cout, csem)
        comm_before()
        _dil_fill_bias(bias)

        def prep(t, carry):
            rows = pl.ds(pl.multiple_of(t * prep_rows, prep_rows), prep_rows)
            vf[rows, :] = v_ref[rows, :].astype(F32)
            return carry

        lax.fori_loop(0, s_len // prep_rows, prep, 0)
        for d, o_p, l_p in zip(DIL_DILATIONS, (o0, o1, o2), (l0, l1, l2)):
            if nblk // d == 2:
                units = DIL_GROUP // 2

                def whole(i, carry, d=d, o_p=o_p, l_p=l_p, units=units):
                    rows = [_strided(i + u * (d // units), 2 * DIL_BLOCK, d) for u in range(units)]
                    ld = [(qf[rw, :].astype(BF16), kf[rw, :].astype(BF16), vf[rw, :].astype(BF16)) for rw in rows]
                    both = bias[...].reshape(2 * DIL_BLOCK, 2 * DIL_BLOCK)
                    s = [_dot_nt(qb, kk) * scale + both for qb, kk, _ in ld]
                    m = [jnp.max(sv, axis=-1, keepdims=True) for sv in s]
                    p = [jnp.exp(sv - mv) for sv, mv in zip(s, m)]
                    den = [jnp.sum(pv, axis=-1, keepdims=True) for pv in p]
                    r = [_dot(pv.astype(BF16), vv) for pv, (_, _, vv) in zip(p, ld)]
                    for rv, dv, mv, rw in zip(r, den, m, rows):
                        o_p[rw, :] = rv / dv
                        l_p[rw, :] = jnp.broadcast_to(mv + jnp.log(dv), (2 * DIL_BLOCK, DIL_HD))
                    return carry

                lax.fori_loop(0, d // units, whole, 0)
                continue

            def pair(i, carry, d=d, o_p=o_p, l_p=l_p):
                idx = [_dil_pair_block(i, half, d, nblk, DIL_GROUP) for half in range(DIL_GROUP)]
                ld = [(qf[_strided(qs, DIL_BLOCK, d), :].astype(BF16),
                       kf[_strided(ks, 2 * DIL_BLOCK, d), :].astype(BF16),
                       vf[_strided(ks, 2 * DIL_BLOCK, d), :].astype(BF16)) for qs, ks, _ in idx]
                s = [_dot_nt(qb, kk) * scale + bias[sel] for (qb, kk, _), (_, _, sel) in zip(ld, idx)]
                m = [jnp.max(sv, axis=-1, keepdims=True) for sv in s]
                p = [jnp.exp(sv - mv) for sv, mv in zip(s, m)]
                den = [jnp.sum(pv, axis=-1, keepdims=True) for pv in p]
                r = [_dot(pv.astype(BF16), vv) for pv, (_, _, vv) in zip(p, ld)]
                for rv, dv, mv, (qs, _, _) in zip(r, den, m, idx):
                    o_p[_strided(qs, DIL_BLOCK, d), :] = rv / dv
                    l_p[_strided(qs, DIL_BLOCK, d), :] = jnp.broadcast_to(mv + jnp.log(dv), (DIL_BLOCK, DIL_HD))
                return carry

            lax.fori_loop(0, nblk // DIL_GROUP, pair, 0)

        def comb(t, carry):
            rows = pl.ds(pl.multiple_of(t * prep_rows, prep_rows), prep_rows)
            a0, a1, a2 = l0[rows, :], l1[rows, :], l2[rows, :]
            m = jnp.maximum(jnp.maximum(a0, a1), a2)
            e0, e1, e2 = jnp.exp(a0 - m), jnp.exp(a1 - m), jnp.exp(a2 - m)
            tot = e0 + e1 + e2
            o_ref[rows, :] = (e0 * o0[rows, :] + e1 * o1[rows, :] + e2 * o2[rows, :]) / tot
            lse_ref[rows, :] = m + jnp.log(tot)
            return carry

        lax.fori_loop(0, s_len // prep_rows, comb, 0)
        comm_after()

    head = lambda base: pl.BlockSpec((s_len, DIL_HD), lambda h: (0, base // DIL_HD + h))
    out = pl.BlockSpec((s_len, DIL_HD), lambda h: (0, h))
    shp = jax.ShapeDtypeStruct((s_len, DIL_HEADS * DIL_HD), F32)
    return pl.pallas_call(
        body, name="dil_fwd_comm" if comm else "dil_fwd", grid=(DIL_HEADS,),
        out_shape=[shp, shp] + (_comm_out_shapes(*comm) if comm else []),
        in_specs=[head(COL_QB), head(COL_KB), head(COL_VB - NP_F32)] + [ANY] * nc,
        out_specs=[out, out] + [ANY] * nc,
        scratch_shapes=[pltpu.VMEM((s_len, DIL_HD), F32) for _ in range(7)]
        + [pltpu.VMEM((2, DIL_BLOCK, 2 * DIL_BLOCK), F32)] + (_comm_scratch(nc) if comm else []),
        compiler_params=_params(("arbitrary",), 56),
    )(pf, pf, pb, *(comm[1] if comm else []))


def _silu_and_grad(z):
    sg = _sigmoid(z)
    return z * sg, sg * (1.0 + z * (1.0 - sg))


def _post_fwd(o_a, o_b, pf, g_heads, w_out, x, gate, g_post, target=None, ts=256):
    s_len = x.shape[0]
    half = GLA_HEADS * GLA_DV
    last = target is not None

    def body(*refs):
        oa_ref, ob_ref, z_ref, gh_ref, w_ref, x_ref, gate_ref, gp_ref = refs[:8]
        xo_ref, u_ref = refs[8 + last:10 + last]
        y_ref = refs[-1]
        for src, base in ((oa_ref, 0), (ob_ref, half)):
            for hh in range(4):
                lo = hh * LANE
                og = src[:, lo:lo + LANE]
                on = og * lax.rsqrt(jnp.mean(og * og, axis=-1, keepdims=True) + EPS)
                zg = z_ref[:, base + lo:base + lo + LANE].astype(F32)
                y_ref[:, base + lo:base + lo + LANE] = (on * gh_ref[:, base + lo:base + lo + LANE]
                                                        * (zg * _sigmoid(zg))).astype(BF16)
        u = _dot(y_ref[...], w_ref[...])
        u_ref[...] = u.astype(BF16)
        rstd = lax.rsqrt(jnp.mean(u * u, axis=-1, keepdims=True) + EPS)
        x_out = x_ref[...] + gate_ref[...] * (u * rstd * gp_ref[...])
        if last:
            t_ref, loss_ref = refs[8], refs[11]

            @pl.when(pl.program_id(0) == 0)
            def _():
                loss_ref[...] = jnp.zeros_like(loss_ref)

            e = x_out - t_ref[...]
            xo_ref[...] = e * (1.0 / D_MODEL)
            loss_ref[...] += 0.5 * jnp.sum(jnp.mean(e * e, axis=-1, keepdims=True))
        else:
            xo_ref[...] = x_out

    (g_heads, gh_spec), (gate, gate_spec), (g_post, gp_spec) = _rowvec(g_heads), _rowvec(gate), _rowvec(g_post)
    tile = pl.BlockSpec((ts, D_MODEL), lambda i: (i, 0))
    halft = pl.BlockSpec((ts, half), lambda i: (i, 0))
    return pl.pallas_call(
        body, name="post_fwd_loss" if last else "post_fwd", grid=(s_len // ts,),
        out_shape=[jax.ShapeDtypeStruct((s_len, D_MODEL), F32), jax.ShapeDtypeStruct((s_len, D_MODEL), BF16)]
        + ([jax.ShapeDtypeStruct((8, LANE), F32)] if last else []),
        in_specs=[halft, halft, tile, gh_spec, pl.BlockSpec((D_MODEL, D_MODEL), lambda i: (0, 0)), tile, gate_spec,
                  gp_spec] + ([tile] if last else []),
        out_specs=[tile, tile] + ([pl.BlockSpec((8, LANE), lambda i: (0, 0))] if last else []),
        scratch_shapes=[pltpu.VMEM((ts, D_MODEL), BF16)],
        compiler_params=_params(("arbitrary",), 40),
    )(o_a, o_b, pf, g_heads, w_out, x, gate, g_post, *([target] if last else []))


def _post_bwd(dxo, u, gate, g_post, w_out, o_a, o_b, pf, g_heads, ts=512):
    s_len = dxo.shape[0]
    half = GLA_HEADS * GLA_DV
    steps = s_len // ts

    def body(dx_ref, u_ref, gate_ref, gp_ref, w_ref, oa_ref, ob_ref, z_ref, gh_ref, do_ref, dz_ref, sums_ref, gw_ref,
             y_s, acc):
        @pl.when(pl.program_id(0) == 0)
        def _():
            sums_ref[...] = jnp.zeros_like(sums_ref)
            acc[...] = jnp.zeros_like(acc)

        dx = dx_ref[...]
        u = u_ref[...].astype(F32)
        rstd = lax.rsqrt(jnp.mean(u * u, axis=-1, keepdims=True) + EPS)
        un = u * rstd
        sums_ref[0:1, :] += jnp.sum(dx * (un * gp_ref[...]), axis=0, keepdims=True)
        drn = dx * gate_ref[...]
        sums_ref[1:2, :] += jnp.sum(drn * un, axis=0, keepdims=True)
        dun = drn * gp_ref[...]
        du = rstd * (dun - un * jnp.mean(dun * un, axis=-1, keepdims=True))
        dub = du.astype(BF16)
        dy = _dot_nt(dub, w_ref[...])
        for src, base in ((oa_ref, 0), (ob_ref, half)):
            for hh in range(4):
                lo = base + hh * LANE
                og = src[:, hh * LANE:(hh + 1) * LANE]
                rs = lax.rsqrt(jnp.mean(og * og, axis=-1, keepdims=True) + EPS)
                on = og * rs
                zg = z_ref[:, lo:lo + LANE].astype(F32)
                sz, dsz = _silu_and_grad(zg)
                gg = gh_ref[:, lo:lo + LANE]
                dyg = dy[:, lo:lo + LANE]
                y_s[:, lo:lo + LANE] = (on * gg * sz).astype(BF16)
                sums_ref[2:3, lo:lo + LANE] += jnp.sum(dyg * sz * on, axis=0, keepdims=True)
                dz_ref[:, lo:lo + LANE] = (dyg * on * gg * dsz).astype(BF16)
                don = dyg * gg * sz
                do_ref[:, lo:lo + LANE] = (rs * (don - on * jnp.mean(don * on, axis=-1, keepdims=True))).astype(BF16)
        acc[...] += _dot_tn(y_s[...], dub)

        @pl.when(pl.program_id(0) == steps - 1)
        def _():
            gw_ref[...] = acc[...].astype(BF16)

    (g_heads, gh_spec), (gate, gate_spec), (g_post, gp_spec) = _rowvec(g_heads), _rowvec(gate), _rowvec(g_post)
    tile = pl.BlockSpec((ts, D_MODEL), lambda i: (i, 0))
    halft = pl.BlockSpec((ts, half), lambda i: (i, 0))
    whole = pl.BlockSpec((D_MODEL, D_MODEL), lambda i: (0, 0))
    return pl.pallas_call(
        body, name="post_bwd", grid=(steps,),
        out_shape=(jax.ShapeDtypeStruct((s_len, D_MODEL), BF16), jax.ShapeDtypeStruct((s_len, D_MODEL), BF16),
                   jax.ShapeDtypeStruct((8, D_MODEL), F32), jax.ShapeDtypeStruct((D_MODEL, D_MODEL), BF16)),
        in_specs=[tile, tile, gate_spec, gp_spec, whole, halft, halft, tile, gh_spec],
        out_specs=(tile, tile, pl.BlockSpec((8, D_MODEL), lambda i: (0, 0)), whole),
        scratch_shapes=[pltpu.VMEM((ts, D_MODEL), BF16), pltpu.VMEM((D_MODEL, D_MODEL), F32)],
        compiler_params=_params(("arbitrary",), 48),
    )(dxo, u, gate, g_post, w_out, o_a, o_b, pf, g_heads)


def _gla_bwd(pf, pb, wgu, bgu, layer, states, do, comm=None):
    s_len = pf.shape[0]
    nc = s_len // GLA_CHUNK
    c = GLA_CHUNK
    n_cin, c_shapes, c_scratch = _comm_plumbing(comm)

    def body(*refs):
        ((q_ref, k_ref, v_ref, lr_ref, wgu_ref, bgu_ref, st_ref, do_ref),
         (dq_ref, dk_ref, dv_ref, dlr_ref, dwgu_ref, dbgu_ref), (ds_s, dec_s, dw_acc, db_acc),
         cin, cout, csem) = _split_refs(refs, 8, 6, 4, comm)
        comm_before, comm_after = _comm_hooks(comm, cin, cout, csem, steps=2)
        comm_before()
        dw_acc[...] = jnp.zeros_like(dw_acc)
        db_acc[...] = jnp.zeros_like(db_acc)
        bd = _state_block_mask()
        last_row = lax.broadcasted_iota(jnp.int32, (c, LANE), 0) == c - 1

        def local(t, carry):
            rows_list = _gla_group_rows(t)
            cm, _, _ = _gla_chunks_common(q_ref, k_ref, lr_ref, wgu_ref, bgu_ref, rows_list)
            loc = [jnp.where(bd, _dot_tn(do_ref[rows, :], cc["qe"].astype(BF16)), 0.0)
                   for rows, cc in zip(rows_list, cm)]
            for j, cc in enumerate(cm):
                ds_s[t * GLA_GROUP + j] = loc[j]
                dec_s[t * GLA_GROUP + j] = jnp.broadcast_to(cc["dec"], (8, LANE))
            return carry

        lax.fori_loop(0, nc // GLA_GROUP, local, 0)

        def scan(t, dst):
            n = nc - 1 - t
            loc = ds_s[n]
            ds_s[n] = dst
            return dec_s[n][0:1, :] * dst + loc

        lax.fori_loop(0, nc, scan, jnp.zeros((2 * GLA_DV, LANE), F32))

        def rest(t, carry):
            rows_list = _gla_group_rows(t)
            cm, ri, ci = _gla_chunks_common(q_ref, k_ref, lr_ref, wgu_ref, bgu_ref, rows_list)
            ns = [t * GLA_GROUP + j for j in range(GLA_GROUP)]
            vs = [v_ref[rows, :] for rows in rows_list]
            dobs = [do_ref[rows, :] for rows in rows_list]
            stbs = [st_ref[0, n] for n in ns]
            dsts = [ds_s[n] for n in ns]
            dstbs = [d.astype(BF16) for d in dsts]
            qebs = [cc["qe"].astype(BF16) for cc in cm]
            kebs = [cc["ke"].astype(BF16) for cc in cm]
            kendbs = [cc["kend"].astype(BF16) for cc in cm]
            hms = [_head_lane_mask(hh) for hh in range(2)]
            qehs = [[jnp.where(hm, cc["qe"], 0.0).astype(BF16) for hm in hms] for cc in cm]
            kehs = [[jnp.where(hm, cc["ke"], 0.0).astype(BF16) for hm in hms] for cc in cm]
            heads = lambda x: [x[:, hh * GLA_DV:(hh + 1) * GLA_DV] for hh in range(2)]
            vhs, dohs = [heads(v) for v in vs], [heads(d) for d in dobs]

            dqe0 = [_dot(dob, stb) for dob, stb in zip(dobs, stbs)]
            dkend = [_dot(v, dstb) for v, dstb in zip(vs, dstbs)]
            dv0 = [_dot_nt(kb, dstb) for kb, dstb in zip(kendbs, dstbs)]
            a_t = [[jnp.where(ci >= ri, _dot_nt(kehs[j][hh], qebs[j]), 0.0).astype(BF16) for hh in range(2)]
                   for j in range(GLA_GROUP)]
            da = [[jnp.where(ri >= ci, _dot_nt(dohs[j][hh], vhs[j][hh]), 0.0).astype(BF16) for hh in range(2)]
                  for j in range(GLA_GROUP)]
            da_t = [[jnp.where(ci >= ri, _dot_nt(vhs[j][hh], dohs[j][hh]), 0.0).astype(BF16) for hh in range(2)]
                    for j in range(GLA_GROUP)]
            dv1 = [[_dot(a_t[j][hh], dohs[j][hh]) for hh in range(2)] for j in range(GLA_GROUP)]
            dqe1 = [[_dot(da[j][hh], kebs[j]) for hh in range(2)] for j in range(GLA_GROUP)]
            dke1 = [[_dot(da_t[j][hh], qehs[j][hh]) for hh in range(2)] for j in range(GLA_GROUP)]

            dbs, dzs = [], []
            for j, (rows, cc) in enumerate(zip(rows_list, cm)):
                qe, ke, kend, b, bl = cc["qe"], cc["ke"], cc["kend"], cc["b"], cc["bl"]
                dqe = dqe0[j] + jnp.where(hms[0], dqe1[j][0], 0.0) + jnp.where(hms[1], dqe1[j][1], 0.0)
                dke = jnp.where(hms[0], dke1[j][0], 0.0) + jnp.where(hms[1], dke1[j][1], 0.0)
                dv_ref[rows, :] = (dv0[j] + jnp.concatenate(dv1[j], axis=1)).astype(BF16)
                dq_ref[rows, :] = (dqe * jnp.exp(b) * (GLA_DK ** -0.5)).astype(BF16)
                dk_ref[rows, :] = (dke * jnp.exp(-b) + dkend[j] * jnp.exp(bl - b)).astype(BF16)
                ddec = jnp.sum(dsts[j] * stbs[j].astype(F32), axis=0, keepdims=True)
                dbl = jnp.sum(dkend[j] * kend, axis=0, keepdims=True) + ddec * cc["dec"]
                dbs.append(dqe * qe - dke * ke - dkend[j] * kend + jnp.where(last_row, dbl, 0.0))
            triu = (ci >= ri).astype(F32)
            dlas = [jnp.dot(triu, db, precision=lax.Precision.HIGHEST, preferred_element_type=F32) for db in dbs]
            dzs = [dla * (1.0 / GLA_TAU) * _sigmoid(-cc["z"]) for dla, cc in zip(dlas, cm)]
            dzbs = [dz.astype(BF16) for dz in dzs]
            dlrs = [_dot_nt(dzb, wgu_ref[...]) for dzb in dzbs]
            dws = [_dot_tn(lr_ref[rows, :], dzb) for rows, dzb in zip(rows_list, dzbs)]
            for rows, dlr in zip(rows_list, dlrs):
                dlr_ref[0, rows, :] = dlr
            dw_acc[...] += functools.reduce(lambda x, y: x + y, dws)
            db_acc[0:1, :] += jnp.sum(functools.reduce(lambda x, y: x + y, dzs), axis=0, keepdims=True)
            return carry

        lax.fori_loop(0, nc // GLA_GROUP, rest, 0)
        dwgu_ref[...] = dw_acc[...]
        dbgu_ref[...] = db_acc[...]
        comm_after()

    pair = pl.BlockSpec((s_len, LANE), lambda g: (0, g))
    return pl.pallas_call(
        body, name="gla_bwd_comm" if comm else "gla_bwd", grid=(2,),
        out_shape=[jax.ShapeDtypeStruct((s_len, GU_COLS), BF16), jax.ShapeDtypeStruct((s_len, GU_COLS), BF16),
                   jax.ShapeDtypeStruct((s_len, GLA_HEADS * GLA_DV), BF16),
                   jax.ShapeDtypeStruct((2, s_len, LANE), F32),
                   jax.ShapeDtypeStruct((LANE, GU_COLS), F32), jax.ShapeDtypeStruct((8, GU_COLS), F32)] + c_shapes,
        in_specs=[pl.BlockSpec((s_len, LANE), lambda g: (0, COL_QA // LANE + g)),
                  pl.BlockSpec((s_len, LANE), lambda g: (0, COL_KA // LANE + g)),
                  pl.BlockSpec((s_len, 2 * GLA_DV), lambda g: (0, (COL_VA - NP_F32) // (2 * GLA_DV) + g)),
                  pl.BlockSpec((s_len, LANE), lambda g: (0, (COL_LR - NP_F32) // LANE)),
                  pl.BlockSpec((None, LANE, LANE), lambda g: (layer, 0, g)),
                  pl.BlockSpec((None, 1, LANE), lambda g: (layer, 0, g)),
                  pl.BlockSpec((1, nc, 2 * GLA_DV, LANE), lambda g: (g, 0, 0, 0)),
                  pl.BlockSpec((s_len, 2 * GLA_DV), lambda g: (0, g))] + [ANY] * n_cin,
        out_specs=[pair, pair, pl.BlockSpec((s_len, 2 * GLA_DV), lambda g: (0, g)),
                   pl.BlockSpec((1, s_len, LANE), lambda g: (g, 0, 0)),
                   pl.BlockSpec((LANE, LANE), lambda g: (0, g)), pl.BlockSpec((8, LANE), lambda g: (0, g))]
        + [ANY] * len(c_shapes),
        scratch_shapes=[pltpu.VMEM((nc, 2 * GLA_DV, LANE), F32), pltpu.VMEM((nc, 8, LANE), F32),
                        pltpu.VMEM((LANE, LANE), F32), pltpu.VMEM((8, LANE), F32)] + c_scratch,
        compiler_params=_params(("arbitrary",), 56),
    )(pf, pf, pb, pb, wgu, bgu.reshape(bgu.shape[0], 1, GU_COLS), states, do, *(comm[1] if comm else []))


def _dil_bwd(pf, pb, cos, sin_signed, do, o_b, lse, comm=None):
    s_len = pf.shape[0]
    nblk = s_len // DIL_BLOCK
    prep_rows = 256
    scale = DIL_HD ** -0.5
    nc = len(comm[1]) if comm else 0

    def body(*refs):
        ((q_ref, kf, v_ref, cos_ref, sin_ref, do_ref, o_ref, lse_ref), (dq_ref, dk_ref, dv_ref),
         (qf, vf, dof, dl, dqa, dka, dva, bias), cin, cout, csem) = _split_refs(refs, 8, 3, 8, comm)
        comm_before, comm_after = _comm_hooks(comm, cin, cout, csem)
        comm_before()
        _dil_fill_bias(bias)

        def prep(t, carry):
            rows = pl.ds(pl.multiple_of(t * prep_rows, prep_rows), prep_rows)
            qf[rows, :] = q_ref[rows, :] * scale
            vf[rows, :] = v_ref[rows, :].astype(F32)
            dov = do_ref[rows, :].astype(F32)
            dof[rows, :] = dov
            dl[rows, :] = jnp.broadcast_to(jnp.sum(dov * o_ref[rows, :], axis=-1, keepdims=True), (prep_rows, DIL_HD))
            zero = jnp.zeros((prep_rows, DIL_HD), F32)
            dqa[rows, :] = zero
            dka[rows, :] = zero
            dva[rows, :] = zero
            return carry

        lax.fori_loop(0, s_len // prep_rows, prep, 0)

        for d in DIL_DILATIONS:
            if nblk // d == 2:
                units = DIL_GROUP // 2

                def whole(i, carry, d=d, units=units):
                    rows = [_strided(i + u * (d // units), 2 * DIL_BLOCK, d) for u in range(units)]
                    ld = [(qf[rw, :].astype(BF16), kf[rw, :].astype(BF16), vf[rw, :].astype(BF16),
                           dof[rw, :].astype(BF16)) for rw in rows]
                    both = bias[...].reshape(2 * DIL_BLOCK, 2 * DIL_BLOCK)
                    s = [_dot_nt(qb, kk) + both for qb, kk, _, _ in ld]
                    dp = [_dot_nt(dob, vv) for _, _, vv, dob in ld]
                    p = [jnp.exp(sv - lse_ref[rw, :][:, 0:1]) for sv, rw in zip(s, rows)]
                    ds = [(pv * (dpv - dl[rw, :][:, 0:1])).astype(BF16) for pv, dpv, rw in zip(p, dp, rows)]
                    pb = [pv.astype(BF16) for pv in p]
                    gq = [_dot(dsv, kk) for dsv, (_, kk, _, _) in zip(ds, ld)]
                    gk = [_dot_tn(dsv, qb) for dsv, (qb, _, _, _) in zip(ds, ld)]
                    gv = [_dot_tn(pv, dob) for pv, (_, _, _, dob) in zip(pb, ld)]
                    for rw, a, b, c in zip(rows, gq, gk, gv):
                        dqa[rw, :] += a
                        dka[rw, :] += b
                        dva[rw, :] += c
                    return carry

                lax.fori_loop(0, d // units, whole, 0)
                continue

            def pair(i, carry, d=d):
                idx = [_dil_pair_block(i, half, d, nblk) for half in range(DIL_GROUP)]
                rows = [(_strided(qs, DIL_BLOCK, d), _strided(ks, 2 * DIL_BLOCK, d)) for qs, ks, _ in idx]
                ld = [(qf[qr, :].astype(BF16), kf[kr, :].astype(BF16), vf[kr, :].astype(BF16),
                       dof[qr, :].astype(BF16)) for qr, kr in rows]
                s = [_dot_nt(qb, kk) + bias[sel] for (qb, kk, _, _), (_, _, sel) in zip(ld, idx)]
                dp = [_dot_nt(dob, vv) for _, _, vv, dob in ld]
                p = [jnp.exp(sv - lse_ref[qr, :][:, 0:1]) for sv, (qr, _) in zip(s, rows)]
                ds = [(pv * (dpv - dl[qr, :][:, 0:1])).astype(BF16) for pv, dpv, (qr, _) in zip(p, dp, rows)]
                pb = [pv.astype(BF16) for pv in p]
                gq = [_dot(dsv, kk) for dsv, (_, kk, _, _) in zip(ds, ld)]
                gk = [_dot_tn(dsv, qb) for dsv, (qb, _, _, _) in zip(ds, ld)]
                gv = [_dot_tn(pv, dob) for pv, (_, _, _, dob) in zip(pb, ld)]
                for (qr, kr), a, b, c in zip(rows, gq, gk, gv):
                    dqa[qr, :] += a
                    dka[kr, :] += b
                    dva[kr, :] += c
                return carry

            lax.fori_loop(0, nblk // DIL_GROUP, pair, 0)

        def fin(t, carry):
            rows = pl.ds(pl.multiple_of(t * prep_rows, prep_rows), prep_rows)
            cs, sn = cos_ref[rows, :], sin_ref[rows, :]
            gq, gk = dqa[rows, :] * scale, dka[rows, :]
            dq_ref[rows, :] = (gq * cs - pltpu.roll(gq, DIL_HD // 2, 1) * sn).astype(BF16)
            dk_ref[rows, :] = (gk * cs - pltpu.roll(gk, DIL_HD // 2, 1) * sn).astype(BF16)
            dv_ref[rows, :] = dva[rows, :].astype(BF16)
            return carry

        lax.fori_loop(0, s_len // prep_rows, fin, 0)
        comm_after()

    head = lambda base: pl.BlockSpec((s_len, DIL_HD), lambda h: (0, base // DIL_HD + h))
    table = pl.BlockSpec((s_len, DIL_HD), lambda h: (0, 0))
    out = pl.BlockSpec((s_len, DIL_HD), lambda h: (0, h))
    shp = jax.ShapeDtypeStruct((s_len, DIL_HEADS * DIL_HD), BF16)
    return pl.pallas_call(
        body, name="dil_bwd_comm" if comm else "dil_bwd", grid=(DIL_HEADS,),
        out_shape=[shp, shp, shp] + (_comm_out_shapes(*comm) if comm else []),
        in_specs=[head(COL_QB), head(COL_KB), head(COL_VB - NP_F32), table, table,
                  pl.BlockSpec((s_len, DIL_HD), lambda h: (0, DIL_HEADS + h)), out, out] + [ANY] * nc,
        out_specs=[out, out, out] + [ANY] * len(_comm_plumbing(comm)[1]),
        scratch_shapes=[pltpu.VMEM((s_len, DIL_HD), F32) for _ in range(7)]
        + [pltpu.VMEM((2, DIL_BLOCK, 2 * DIL_BLOCK), F32)] + (_comm_scratch(nc) if comm else []),
        compiler_params=_params(("arbitrary",), 56),
    )(pf, pf, pb, cos, sin_signed, do, o_b, lse, *(comm[1] if comm else []))


_PIECES = ((COL_Z, 1024), (COL_QA, 256), (COL_KA, 256), (COL_QB, 512), (COL_KB, 512), (COL_VA, 512), (COL_VB, 512),
           (COL_LR, 128))


def _in_bwd(pieces, w_new, x, dxo, g_pre, scale, comm=None, ts=256):
    s_len = x.shape[0]
    nc = len(comm[1]) if comm else 0
    nco = len(_comm_out_shapes(*comm)) if comm else 0
    npc = len(_PIECES)

    def body(*refs):
        p_refs = refs[:npc]
        w_ref, x_ref, dxo_ref, g_ref, sc_ref = refs[npc:npc + 5]
        cin, (dx_ref, sums_ref), cout = (refs[npc + 5:npc + 5 + nc], refs[npc + 5 + nc:npc + 7 + nc],
                                         refs[npc + 7 + nc:npc + 7 + nc + nco])
        comm_before, comm_after = _comm_hooks(comm, cin, cout, refs[npc + 7 + nc + nco:], steps=s_len // ts)
        comm_before()

        @pl.when(pl.program_id(0) == 0)
        def _():
            sums_ref[...] = jnp.zeros_like(sums_ref)

        dh = jnp.zeros((ts, D_MODEL), F32)
        for p_ref, (col, width) in zip(p_refs, _PIECES):
            dh += _dot_nt(p_ref[...], w_ref[:, col:col + width])
        xv = x_ref[...]
        rstd = lax.rsqrt(jnp.mean(xv * xv, axis=-1, keepdims=True) + EPS)
        xn = xv * rstd
        sums_ref[0:1, :] += jnp.sum(dh, axis=0, keepdims=True)
        sums_ref[1:2, :] += jnp.sum(dh * (xn * g_ref[...]), axis=0, keepdims=True)
        dr = dh * (1.0 + sc_ref[...])
        sums_ref[2:3, :] += jnp.sum(dr * xn, axis=0, keepdims=True)
        dxn = dr * g_ref[...]
        dx_ref[...] = dxo_ref[...] + rstd * (dxn - xn * jnp.mean(dxn * xn, axis=-1, keepdims=True))
        comm_after()

    (g_pre, g_spec), (scale, sc_spec) = _rowvec(g_pre), _rowvec(scale)
    tile = pl.BlockSpec((ts, D_MODEL), lambda i: (i, 0))
    return pl.pallas_call(
        body, name="in_bwd_comm" if comm else "in_bwd", grid=(s_len // ts,),
        out_shape=[jax.ShapeDtypeStruct((s_len, D_MODEL), F32), jax.ShapeDtypeStruct((8, D_MODEL), F32)]
        + (_comm_out_shapes(*comm) if comm else []),
        in_specs=[pl.BlockSpec((ts, width), lambda i: (i, 0)) for _, width in _PIECES]
        + [pl.BlockSpec((D_MODEL, NP), lambda i: (0, 0)), tile, tile, g_spec, sc_spec] + [ANY] * nc,
        out_specs=[tile, pl.BlockSpec((8, D_MODEL), lambda i: (0, 0))] + [ANY] * nco,
        scratch_shapes=_comm_scratch(nc) if comm else [],
        compiler_params=_params(("arbitrary",), 56),
    )(*pieces, w_new, x, dxo, g_pre, scale, *(comm[1] if comm else []))


def _w_in_to_kernel(gathered, comm=None, tr=128):
    n_cin, c_shapes, c_scratch = _comm_plumbing(comm)
    n_parts = len(gathered)
    first = [sum(g.shape[1] for g in gathered[:p]) // tr for p in range(n_parts + 1)]

    def body(*refs):
        g_refs, (o_ref,), _, cin, cout, csem = _split_refs(refs, n_parts, 1, 0, comm)
        comm_before, comm_after = _comm_hooks(comm, cin, cout, csem, steps=D_MODEL // tr)
        comm_before()
        for p, g_ref in enumerate(g_refs):
            @pl.when((pl.program_id(0) >= first[p]) & (pl.program_id(0) < first[p + 1]))
            def _(g_ref=g_ref):
                cols = jnp.concatenate([g_ref[k].astype(F32) for k in range(N_DEV)], axis=1)
                pad = jnp.zeros((tr, LANE - GLA_LOWRANK), F32)
                o_ref[...] = jnp.concatenate(
                    [cols[:, 1024:1536], cols[:, 3088:3600], cols[:, 0:512], cols[:, 1552:2576], cols[:, 512:1024],
                     cols[:, 2576:3088], cols[:, 1536:1552], pad], axis=1).astype(BF16)
        comm_after()

    part = lambda p: pl.BlockSpec((N_DEV, tr, W_IN_SHARD),
                                  lambda i: (0, jnp.clip(i - first[p], 0, first[p + 1] - first[p] - 1), 0))
    return pl.pallas_call(
        body, name="w_in_to_kernel_comm" if comm else "w_in_to_kernel", grid=(D_MODEL // tr,),
        out_shape=[jax.ShapeDtypeStruct((D_MODEL, NP), BF16)] + c_shapes,
        in_specs=[part(p) for p in range(n_parts)] + [ANY] * n_cin,
        out_specs=[pl.BlockSpec((tr, NP), lambda i: (i, 0))] + [ANY] * len(c_shapes),
        scratch_shapes=c_scratch,
        compiler_params=_params(("arbitrary",)),
    )(*gathered, *(comm[1] if comm else []))


def _grad_w_in(h, pieces, ts=512, tr=128):
    s_len = h.shape[0]
    steps = s_len // ts

    def body(*refs):
        h_ref, p_refs = refs[0], refs[1:1 + len(_PIECES)]
        o_ref, acc = refs[1 + len(_PIECES):]

        @pl.when(pl.program_id(0) == 0)
        def _():
            acc[...] = jnp.zeros_like(acc)

        hv = h_ref[...]
        for p_ref, (col, width) in zip(p_refs, _PIECES):
            acc[:, col:col + width] += _dot_tn(hv, p_ref[...])

        @pl.when(pl.program_id(0) == steps - 1)
        def _():
            def rows_out(t, carry):
                rows = pl.ds(pl.multiple_of(t * tr, tr), tr)
                g = acc[rows, :]
                cols = jnp.concatenate(
                    [g[:, COL_QA:COL_QB], g[:, COL_VA:COL_VB], g[:, 0:512], g[:, COL_LR:COL_LR + GLA_LOWRANK],
                     g[:, COL_QB:COL_VA], g[:, COL_VB:COL_LR], g[:, 512:1024]], axis=1)
                for k in range(N_DEV):
                    o_ref[k, rows, :] = cols[:, W_IN_SHARD * k:W_IN_SHARD * (k + 1)].astype(BF16)
                return carry

            lax.fori_loop(0, D_MODEL // tr, rows_out, 0)

    return pl.pallas_call(
        body, name="grad_w_in", grid=(steps,),
        out_shape=jax.ShapeDtypeStruct((N_DEV, D_MODEL, W_IN_SHARD), BF16),
        in_specs=[pl.BlockSpec((ts, D_MODEL), lambda i: (i, 0))]
        + [pl.BlockSpec((ts, width), lambda i: (i, 0)) for _, width in _PIECES],
        out_specs=pl.BlockSpec((N_DEV, D_MODEL, W_IN_SHARD), lambda i: (0, 0, 0)),
        scratch_shapes=[pltpu.VMEM((D_MODEL, NP), F32)],
        compiler_params=_params(("arbitrary",), 56),
    )(h, *pieces)


def _adam_math(w, g, m, v):
    m = ADAM_B1 * m + (1.0 - ADAM_B1) * g
    v = ADAM_B2 * v + (1.0 - ADAM_B2) * (g * g)
    m_hat = m / (1.0 - ADAM_B1 ** ADAM_STEP)
    v_hat = v / (1.0 - ADAM_B2 ** ADAM_STEP)
    delta = -ADAM_LR * (m_hat / (jnp.sqrt(v_hat) + ADAM_EPS) + ADAM_WD * w)
    return delta, m, v


def _adamw(w, parts, m, v, name, tr):
    r, cdim = w.shape
    n_parts = parts.shape[0]

    def body(w_ref, p_ref, m_ref, v_ref, g_ref, d_ref, nm_ref, nv_ref):
        g = p_ref[0].astype(F32)
        for k in range(1, n_parts):
            g = g + p_ref[k].astype(F32)
        g_ref[...] = g
        d_ref[...], nm_ref[...], nv_ref[...] = _adam_math(w_ref[...], g, m_ref[...], v_ref[...])

    tile = pl.BlockSpec((tr, cdim), lambda i: (i, 0))
    shp = jax.ShapeDtypeStruct((r, cdim), F32)
    return pl.pallas_call(
        body, name=name, grid=(r // tr,), out_shape=(shp, shp, shp, shp),
        in_specs=[tile, pl.BlockSpec((n_parts, tr, cdim), lambda i: (0, i, 0)), tile, tile],
        out_specs=(tile, tile, tile, tile),
        compiler_params=_params(("arbitrary",), 40),
    )(w, parts, m, v)


def _adamw_layers(w, parts, m, v, name, tr):
    n_layers, r, cdim = w.shape

    def body(*refs):
        w_ref, p_refs, (m_ref, v_ref) = refs[0], refs[1:1 + n_layers], refs[1 + n_layers:3 + n_layers]
        g_ref, d_ref, nm_ref, nv_ref = refs[3 + n_layers:]
        for l, p_ref in enumerate(p_refs):
            @pl.when(pl.program_id(0) == l)
            def _(p_ref=p_ref):
                g = p_ref[0].astype(F32)
                for k in range(1, p_ref.shape[0]):
                    g = g + p_ref[k].astype(F32)
                g_ref[0] = g
                d_ref[0], nm_ref[0], nv_ref[0] = _adam_math(w_ref[0], g, m_ref[0], v_ref[0])

    tile = pl.BlockSpec((1, tr, cdim), lambda l, i: (l, i, 0))
    part = lambda own: pl.BlockSpec((parts[own].shape[0], tr, cdim), lambda l, i: (0, jnp.where(l == own, i, 0), 0))
    shp = jax.ShapeDtypeStruct(w.shape, F32)
    return pl.pallas_call(
        body, name=name, grid=(n_layers, r // tr), out_shape=(shp, shp, shp, shp),
        in_specs=[tile] + [part(l) for l in range(n_layers)] + [tile, tile],
        out_specs=(tile, tile, tile, tile),
        compiler_params=_params(("arbitrary", "arbitrary"), 40),
    )(w, *parts, m, v)


def _row(vec, width):
    vec = vec.reshape(1, -1)
    return jnp.pad(vec, ((0, 0), (0, width - vec.shape[1])))


def kernel(x, c, w_ada, b_ada, g_pre, w_in, w_gate_up, b_gate_up, g_gla, g_dil, w_out, g_post, loss_target, m_w_ada, m_b_ada, m_g_pre, m_w_in, m_w_gate_up, m_b_gate_up, m_g_gla, m_g_dil, m_w_out, m_g_post, v_w_ada, v_b_ada, v_g_pre, v_w_in, v_w_gate_up, v_b_gate_up, v_g_gla, v_g_dil, v_w_out, v_g_post):
    px, py, pc = _my_position()
    me = _linear(px, py, pc)
    xs = x[0]
    target = loss_target[0]
    s_len = xs.shape[0]
    assert s_len % (DIL_BLOCK * max(DIL_DILATIONS) * 2) == 0 and xs.shape[1] == D_MODEL

    w_in_b, w_out_b = w_in.astype(BF16), w_out.astype(BF16)
    c_rows, wgu_all, w_in_all = _comm_call(
        "gather", [jnp.pad(c, ((0, 7), (0, 0))), w_gate_up.reshape(DEPTH * GLA_LOWRANK, GU_SHARD), w_in_b[0]],
        "gather_first")
    c_all = c_rows.reshape(N_DEV, 8, D_MODEL)[:, 0]
    mod_part = _mod_fwd(c_all, w_ada)
    w_new, mod_all = _w_in_to_kernel([w_in_all.reshape(N_DEV, D_MODEL, W_IN_SHARD)],
                                     comm=("gather", [mod_part.reshape(DEPTH * N_DEV, ADA_SHARD)]))
    mod_all = mod_all.reshape(N_DEV, DEPTH, N_DEV, ADA_SHARD)
    mod_mine = lax.dynamic_index_in_dim(mod_all, me, axis=2, keepdims=False)
    mod = jnp.transpose(mod_mine, (1, 0, 2)).reshape(DEPTH, 3 * D_MODEL) + b_ada
    wgu_full = jnp.transpose(wgu_all.reshape(N_DEV, DEPTH, GLA_LOWRANK, GU_SHARD), (1, 2, 0, 3)).reshape(
        DEPTH, GLA_LOWRANK, GU_COLS)
    wgu_pad = jnp.pad(wgu_full, ((0, 0), (0, LANE - GLA_LOWRANK), (0, 0))).astype(BF16)

    cos, sin_signed = _rope_tables(s_len)
    g_heads = jnp.concatenate([g_gla, g_dil], axis=1)

    saved = []
    xl = xs
    for l in range(DEPTH):
        shift, scale, gate = ((mod, l, k) for k in range(3))
        if l > 0:
            w_new = _w_in_to_kernel([half.reshape(N_DEV, D_MODEL // 2, W_IN_SHARD) for half in w_in_halves])[0]
        if l + 1 < DEPTH:
            pf, pb, h, w_out_l, top = _prenorm_proj(xl, (g_pre, l, 0), scale, shift, w_new, cos, sin_signed,
                                                    comm=("gather", [w_out_b[l], w_in_b[l + 1, :D_MODEL // 2]]))
        else:
            pf, pb, h, w_out_l = _prenorm_proj(xl, (g_pre, l, 0), scale, shift, w_new, cos, sin_signed,
                                               comm=("gather", [w_out_b[l]]))
        o_a, states = _gla_fwd(pf, pb, wgu_pad, b_gate_up, l)
        if l + 1 < DEPTH:
            o_b, lse, bottom = _dil_fwd(pf, pb, comm=("gather", [w_in_b[l + 1, D_MODEL // 2:]]))
            w_in_halves = (top, bottom)
        else:
            o_b, lse = _dil_fwd(pf, pb)
        if l + 1 < DEPTH:
            x_next, u = _post_fwd(o_a, o_b, pf, (g_heads, l, 0), w_out_l, xl, gate, (g_post, l, 0))
        else:
            dx, u, loss_part = _post_fwd(o_a, o_b, pf, (g_heads, l, 0), w_out_l, xl, gate, (g_post, l, 0),
                                         target=target)
        saved.append((xl, scale, gate, w_new, w_out_l, pf, pb, h, o_a, states, o_b, lse, u))
        xl = x_next

    small_rows = []
    gin_slots, gin_parts, gout_parts = None, [None] * DEPTH, [None] * DEPTH
    for l in reversed(range(DEPTH)):
        x_in, scale, gate, w_new, w_out_l, pf, pb, h, o_a, states, o_b, lse, u = saved[l]
        do, dz, sums_post, gout_slots = _post_bwd(dx, u, gate, (g_post, l, 0), w_out_l, o_a, o_b, pf, (g_heads, l, 0))
        dq_a, dk_a, dv_a, dlr2, dwgu, dbgu, arrived = _gla_bwd(pf, pb, wgu_pad, b_gate_up, l, states, do,
                                                               comm=("exchange", [gout_slots]))
        gout_parts[l] = arrived.reshape(N_DEV, OUT_SHARD, D_MODEL)
        if gin_slots is not None:
            dq_b, dk_b, dv_b, arrived, _, _ = _dil_bwd(pf, pb, cos, sin_signed, do, o_b, lse,
                                                       comm=("pairsum_exchange", [gin_slots]))
            gin_parts[l + 1] = arrived.reshape(N_DEV // 2, D_MODEL, W_IN_SHARD)
        else:
            dq_b, dk_b, dv_b = _dil_bwd(pf, pb, cos, sin_signed, do, o_b, lse)
        dlr = (dlr2[0] + dlr2[1]).astype(BF16)
        pieces = (dz, dq_a, dk_a, dq_b, dk_b, dv_a, dv_b, dlr)
        gin_slots = _grad_w_in(h, pieces).reshape(N_DEV * D_MODEL, W_IN_SHARD)
        if l == 0:
            dx, sums_in, arrived, _, _ = _in_bwd(pieces, w_new, x_in, dx, (g_pre, l, 0), scale,
                                                 comm=("pairsum_exchange", [gin_slots]))
            gin_parts[0] = arrived.reshape(N_DEV // 2, D_MODEL, W_IN_SHARD)
        else:
            dx, sums_in = _in_bwd(pieces, w_new, x_in, dx, (g_pre, l, 0), scale)
        dmod = jnp.concatenate([sums_in[0], sums_in[1], sums_post[0]])
        vecs = jnp.concatenate([sums_in[2], sums_post[1], sums_post[2], dbgu[0]])
        small_rows[0:0] = [_row(dmod, 4096), _row(vecs, 4096), _row(dwgu[:GLA_LOWRANK], 4096)]
    grad_x = dx[None]

    flat = lambda a, rows: a.reshape(rows, a.shape[-1])
    r_ada = DEPTH * D_MODEL
    g_w_in, d_w_in, nm_w_in, nv_w_in = _adamw_layers(w_in, gin_parts, m_w_in, v_w_in, "adamw_w_in", 256)
    g_w_out, d_w_out, nm_w_out, nv_w_out = _adamw_layers(w_out, gout_parts, m_w_out, v_w_out, "adamw_w_out", 128)

    small_rows += [_row(loss_part[0, 0:1], 4096), jnp.zeros((1, 4096), F32)]
    small = _all_gather(jnp.concatenate(small_rows, axis=0), "gather_small").reshape(N_DEV, 8, 4096)
    dmod_all = jnp.stack([small[:, 0, :3 * D_MODEL], small[:, 3, :3 * D_MODEL]])
    dmod_cols = lax.dynamic_slice_in_dim(dmod_all, me * ADA_SHARD, ADA_SHARD, axis=2)
    gwa = _w_ada_grad(c_all, dmod_cols).reshape(1, r_ada, ADA_SHARD)
    g_w_ada, d_w_ada, nm_w_ada, nv_w_ada = (
        t.reshape(w_ada.shape) for t in _adamw(flat(w_ada, r_ada), gwa, flat(m_w_ada, r_ada), flat(v_w_ada, r_ada),
                                               "adamw_w_ada", 256))

    where = ((0, 0), (1, 0), (1, 1024), (1, 2048), (1, 2560), (1, 3072))
    replicated = [(b_ada, m_b_ada, v_b_ada), (g_pre, m_g_pre, v_g_pre), (g_post, m_g_post, v_g_post),
                  (g_gla, m_g_gla, v_g_gla), (g_dil, m_g_dil, v_g_dil), (b_gate_up, m_b_gate_up, v_b_gate_up)]
    updated, loss = _adamw_replicated(small, replicated, where, loss_at=(6, 0))
    ((g_b_ada, d_b_ada, nm_b_ada, nv_b_ada), (g_g_pre, d_g_pre, nm_g_pre, nv_g_pre),
     (g_g_post, d_g_post, nm_g_post, nv_g_post), (g_g_gla, d_g_gla, nm_g_gla, nv_g_gla),
     (g_g_dil, d_g_dil, nm_g_dil, nv_g_dil), (g_b_gu, d_b_gu, nm_b_gu, nv_b_gu)) = updated
    gu_parts = jnp.stack([small[:, 2], small[:, 5]], axis=1).reshape(N_DEV, DEPTH, GLA_LOWRANK, GU_COLS)
    gu_parts = lax.dynamic_slice_in_dim(gu_parts, me * GU_SHARD, GU_SHARD, axis=3).reshape(
        N_DEV, DEPTH * GLA_LOWRANK, GU_SHARD)
    r_gu = DEPTH * GLA_LOWRANK
    g_w_gu, d_w_gu, nm_w_gu, nv_w_gu = (
        t.reshape(w_gate_up.shape) for t in _adamw(flat(w_gate_up, r_gu), gu_parts, flat(m_w_gate_up, r_gu),
                                                   flat(v_w_gate_up, r_gu), "adamw_w_gate_up", r_gu))
    return (loss, grad_x,
            g_w_ada, g_b_ada, g_g_pre, g_w_in, g_w_gu, g_b_gu, g_g_gla, g_g_dil, g_w_out, g_g_post,
            d_w_ada, d_b_ada, d_g_pre, d_w_in, d_w_gu, d_b_gu, d_g_gla, d_g_dil, d_w_out, d_g_post,
            nm_w_ada, nm_b_ada, nm_g_pre, nm_w_in, nm_w_gu, nm_b_gu, nm_g_gla, nm_g_dil, nm_w_out, nm_g_post,
            nv_w_ada, nv_b_ada, nv_g_pre, nv_w_in, nv_w_gu, nv_b_gu, nv_g_gla, nv_g_dil, nv_w_out, nv_g_post)


def _adamw_replicated(small, params, where, loss_at):
    n_parts = small.shape[0]

    def body(*refs):
        s_ref, p_refs, o_refs = refs[0], refs[1:1 + 3 * len(params)], refs[1 + 3 * len(params):]
        total = s_ref[0]
        for k in range(1, n_parts):
            total = total + s_ref[k]
        for i, (row, col) in enumerate(where):
            w_ref, m_ref, v_ref = p_refs[3 * i:3 * i + 3]
            n = w_ref.shape[1]
            g = jnp.concatenate([total[row + 3 * l:row + 3 * l + 1, col:col + n] for l in range(DEPTH)], axis=0)
            o_refs[4 * i][...] = g
            o_refs[4 * i + 1][...], o_refs[4 * i + 2][...], o_refs[4 * i + 3][...] = _adam_math(
                w_ref[...], g, m_ref[...], v_ref[...])
        o_refs[-1][...] = jnp.broadcast_to(total[loss_at[0]:loss_at[0] + 1, loss_at[1]:loss_at[1] + 1], (8, LANE))

    flat = [a for p in params for a in p]
    shapes = [jax.ShapeDtypeStruct(p[0].shape, F32) for p in params for _ in range(4)]
    outs = pl.pallas_call(body, name="adamw_replicated",
                          out_shape=shapes + [jax.ShapeDtypeStruct((8, LANE), F32)])(small, *flat)
    return [tuple(outs[4 * i:4 * i + 4]) for i in range(len(params))], outs[-1][0, 0]
```

```python
import functools
import math

import jax
import jax.numpy as jnp
from jax import lax
from jax.experimental import pallas as pl
from jax.experimental.pallas import tpu as pltpu

F32 = jnp.float32
BF16 = jnp.bfloat16

N_DEV = 8
D_MODEL = 1024
DEPTH = 2
GLA_HEADS = 4
GLA_DK = 64
GLA_DV = 128
GLA_CHUNK = 64
GLA_TAU = 16.0
GLA_LOWRANK = 16
DIL_HEADS = 4
DIL_HD = 128
DIL_BLOCK = 128
DIL_DILATIONS = (1, 4, 16)
ROPE_THETA = 10000.0
EPS = 1e-6
IN_COLS = 3600
W_IN_SHARD = IN_COLS // N_DEV
ADA_SHARD = 3 * D_MODEL // N_DEV
OUT_SHARD = D_MODEL // N_DEV
GU_COLS = GLA_HEADS * GLA_DK
GU_SHARD = GU_COLS // N_DEV

ADAM_LR = 0.001
ADAM_B1 = 0.9
ADAM_B2 = 0.999
ADAM_EPS = 1e-08
ADAM_WD = 0.01
ADAM_STEP = 10

NP = 3712
COL_Z, COL_QA, COL_KA, COL_QB, COL_KB, COL_VA, COL_VB, COL_LR = 0, 1024, 1280, 1536, 2048, 2560, 3072, 3584
NP_F32 = COL_VA
NP_BF16 = NP - NP_F32
LANE = 128
MASK_VALUE = -1e30

MESH = pl.DeviceIdType.MESH
ANY = pl.BlockSpec(memory_space=pl.ANY)


def _params(sem=None, vmem_mb=None):
    kw = {}
    if sem is not None:
        kw["dimension_semantics"] = sem
    if vmem_mb is not None:
        kw["vmem_limit_bytes"] = vmem_mb * 1024 * 1024
    return pltpu.CompilerParams(**kw)


def _dot(a, b):
    return jnp.dot(a, b, preferred_element_type=F32)


def _dot_nt(a, b):
    return lax.dot_general(a, b, (((1,), (1,)), ((), ())), preferred_element_type=F32)


def _dot_tn(a, b):
    return lax.dot_general(a, b, (((0,), (0,)), ((), ())), preferred_element_type=F32)


def _sigmoid(z):
    return 1.0 / (1.0 + jnp.exp(-z))


def _log_sigmoid(z):
    return jnp.minimum(z, 0.0) - jnp.log(1.0 + jnp.exp(-jnp.abs(z)))


def _rowvec(v, width=D_MODEL):
    arr, row, cb = v
    return arr.reshape(arr.shape[0], 1, arr.shape[1]), pl.BlockSpec((None, 1, width), lambda *_: (row, 0, cb))


def _my_position():
    return lax.axis_index("x"), lax.axis_index("y"), lax.axis_index("c")


def _linear(px, py, pc):
    return 4 * px + 2 * py + pc


def _gather_phase(phase, x_ref, out_ref, send_sem, recv_sem, local_sem):
    m = x_ref.shape[0]
    x, y, c = _my_position()
    me, sibling = (x, y, c), (x, y, 1 - c)
    chips = [(1 - x, y), (x, 1 - y), (1 - x, 1 - y)]

    def rows(px, py, pc):
        return out_ref.at[pl.ds(_linear(px, py, pc) * m, m), :]

    def copy(k, block, to, src=None):
        return pltpu.make_async_remote_copy(
            src_ref=rows(*block) if src is None else src, dst_ref=rows(*block),
            send_sem=send_sem(k), recv_sem=recv_sem(k), device_id=to, device_id_type=MESH)

    mine = pltpu.make_async_copy(x_ref, rows(*me), local_sem)
    first = [copy(0, me, sibling, src=x_ref)] + [copy(1 + j, me, (*chip, c), src=x_ref) for j, chip in enumerate(chips)]
    passed = [copy(4 + j, (*chip, c), sibling) for j, chip in enumerate(chips)]
    if phase == "start":
        mine.start()
        for cp in first:
            cp.start()
    elif phase == "forward":
        for j, chip in enumerate(chips):
            copy(1 + j, (*chip, c), me).wait_recv()
            passed[j].start()
    else:
        copy(0, sibling, me).wait_recv()
        for j, chip in enumerate(chips):
            copy(4 + j, (*chip, 1 - c), me).wait_recv()
        for cp in first + passed:
            cp.wait_send()
        mine.wait()


def _exchange_phase(phase, x_ref, out_ref, send_sem, recv_sem, local_sem):
    m = x_ref.shape[0] // N_DEV
    x, y, c = _my_position()
    me = _linear(x, y, c)

    def rows(ref, idx):
        return ref.at[pl.ds(idx * m, m), :]

    peers = [(1 - x if j & 4 else x, 1 - y if j & 2 else y, 1 - c if j & 1 else c) for j in range(1, N_DEV)]
    local = pltpu.make_async_copy(rows(x_ref, me), rows(out_ref, me), local_sem)
    sends = [pltpu.make_async_remote_copy(
        src_ref=rows(x_ref, _linear(*peer)), dst_ref=rows(out_ref, me),
        send_sem=send_sem(j), recv_sem=recv_sem(j), device_id=peer, device_id_type=MESH) for j, peer in enumerate(peers)]
    if phase == "start":
        local.start()
        for cp in sends:
            cp.start()
    else:
        for j, peer in enumerate(peers):
            pltpu.make_async_remote_copy(
                src_ref=rows(x_ref, _linear(*peer)), dst_ref=rows(out_ref, _linear(*peer)),
                send_sem=send_sem(j), recv_sem=recv_sem(j), device_id=peer, device_id_type=MESH).wait_recv()
        for cp in sends:
            cp.wait_send()
        local.wait()


def _pairsum_exchange_phase(phase, x_ref, out_refs, send_sem, recv_sem, local_sem):
    out_ref, stage_ref, pair_ref = out_refs
    m, n = x_ref.shape[0] // N_DEV, x_ref.shape[1]
    x, y, c = _my_position()
    mine = 2 * x + y
    chips = [(qx, qy) for qx in range(2) for qy in range(2)]
    others = [(1 - x, y), (x, 1 - y), (1 - x, 1 - y)]

    def rows(ref, idx):
        return ref.at[pl.ds(idx * m, m), :]

    def remote(src, dst, k, to):
        return pltpu.make_async_remote_copy(src_ref=src, dst_ref=dst, send_sem=send_sem(k), recv_sem=recv_sem(k),
                                            device_id=to, device_id_type=MESH)

    to_sibling = [remote(rows(x_ref, _linear(qx, qy, 1 - c)), rows(stage_ref, q), q, (x, y, 1 - c))
                  for q, (qx, qy) in enumerate(chips)]
    to_chips = [remote(rows(pair_ref, 2 * qx + qy), rows(out_ref, mine), 4 + j, (qx, qy, c))
                for j, (qx, qy) in enumerate(others)]
    keep = pltpu.make_async_copy(rows(pair_ref, mine), rows(out_ref, mine), local_sem)
    if phase == "start":
        for cp in to_sibling:
            cp.start()
    elif phase == "reduce":
        for cp in to_sibling:
            cp.wait_recv()

        def through_vmem(a_buf, b_buf, sems):
            tr = 128
            loads = [(pltpu.make_async_copy(rows(x_ref, _linear(qx, qy, c)), a_buf.at[q % 2], sems.at[q % 2]),
                      pltpu.make_async_copy(rows(stage_ref, q), b_buf.at[q % 2], sems.at[2 + q % 2]))
                     for q, (qx, qy) in enumerate(chips)]
            stores = [pltpu.make_async_copy(a_buf.at[q % 2], rows(pair_ref, q), sems.at[4 + q % 2]) for q in range(4)]
            for cp in loads[0]:
                cp.start()
            for q in range(4):
                for cp in loads[q]:
                    cp.wait()
                if q + 1 < 4:
                    if q >= 1:
                        stores[q - 1].wait()
                    for cp in loads[q + 1]:
                        cp.start()

                def add(r, carry, q=q):
                    tile = pl.ds(pl.multiple_of(r * tr, tr), tr)
                    a_buf[q % 2, tile, :] = (a_buf[q % 2, tile, :].astype(F32)
                                             + b_buf[q % 2, tile, :].astype(F32)).astype(x_ref.dtype)
                    return carry

                lax.fori_loop(0, m // tr, add, 0)
                stores[q].start()
            stores[2].wait()
            stores[3].wait()

        pl.run_scoped(through_vmem, pltpu.VMEM((2, m, n), x_ref.dtype), pltpu.VMEM((2, m, n), x_ref.dtype),
                      pltpu.SemaphoreType.DMA((6,)))
    elif phase == "send":
        keep.start()
        for cp in to_chips:
            cp.start()
    else:
        for j, (qx, qy) in enumerate(others):
            remote(rows(pair_ref, mine), rows(out_ref, 2 * qx + qy), 4 + j, (qx, qy, c)).wait_recv()
        for cp in to_sibling + to_chips:
            cp.wait_send()
        keep.wait()


_COMM_PHASES = {"gather": (_gather_phase, ("start", "forward", "finish")),
                "exchange": (_exchange_phase, ("start", "finish")),
                "pairsum_exchange": (_pairsum_exchange_phase, ("start", "reduce", "send", "finish"))}


def _comm_scratch(n_arrays):
    return [pltpu.SemaphoreType.DMA((n_arrays, 7)), pltpu.SemaphoreType.DMA((n_arrays, 7)),
            pltpu.SemaphoreType.DMA((n_arrays,))]


def _comm_run(kind, phases, x_refs, out_refs, send_sems, recv_sems, local_sems):
    fn = _COMM_PHASES[kind][0]
    per = len(out_refs) // len(x_refs)
    for phase in phases:
        for a, x_ref in enumerate(x_refs):
            outs = out_refs[a] if per == 1 else tuple(out_refs[per * a:per * (a + 1)])
            fn(phase, x_ref, outs, lambda k, a=a: send_sems.at[a, k], lambda k, a=a: recv_sems.at[a, k],
               local_sems.at[a])


def _comm_out_shapes(kind, arrays):
    if kind == "pairsum_exchange":
        return [jax.ShapeDtypeStruct((a.shape[0] // 2, a.shape[1]), a.dtype) for a in arrays for _ in range(3)]
    return [jax.ShapeDtypeStruct((N_DEV * a.shape[0], a.shape[1]) if kind == "gather" else a.shape, a.dtype)
            for a in arrays]


def _comm_call(kind, arrays, name):
    n = len(arrays)
    shapes = _comm_out_shapes(kind, arrays)

    def body(*refs):
        _comm_run(kind, _COMM_PHASES[kind][1], refs[:n], refs[n:n + len(shapes)], *refs[n + len(shapes):])

    return pl.pallas_call(body, name=name, out_shape=shapes, in_specs=[ANY] * n, out_specs=[ANY] * len(shapes),
                          scratch_shapes=_comm_scratch(n))(*arrays)


def _all_gather(xs, name):
    return _comm_call("gather", [xs], name)[0]


def _mod_fwd(c_all, w_ada):
    def body(c_ref, w_ref, o_ref):
        cv = c_ref[...]
        sc = cv * _sigmoid(cv)
        o_ref[0] = _dot(sc.astype(BF16), w_ref[0].astype(BF16))

    return pl.pallas_call(
        body, name="mod_fwd", grid=(DEPTH,),
        out_shape=jax.ShapeDtypeStruct((DEPTH, N_DEV, ADA_SHARD), F32),
        in_specs=[pl.BlockSpec((N_DEV, D_MODEL), lambda l: (0, 0)),
                  pl.BlockSpec((1, D_MODEL, ADA_SHARD), lambda l: (l, 0, 0))],
        out_specs=pl.BlockSpec((1, N_DEV, ADA_SHARD), lambda l: (l, 0, 0)),
        compiler_params=_params(("arbitrary",)),
    )(c_all, w_ada)


def _w_ada_grad(c_all, dmod_cols):
    def body(c_ref, d_ref, o_ref):
        cv = c_ref[...]
        sc = cv * _sigmoid(cv)
        o_ref[0] = lax.dot_general(sc, d_ref[0], (((0,), (0,)), ((), ())), precision=lax.Precision.HIGHEST,
                                   preferred_element_type=F32)

    return pl.pallas_call(
        body, name="w_ada_grad", grid=(DEPTH,),
        out_shape=jax.ShapeDtypeStruct((DEPTH, D_MODEL, ADA_SHARD), F32),
        in_specs=[pl.BlockSpec((N_DEV, D_MODEL), lambda l: (0, 0)),
                  pl.BlockSpec((1, N_DEV, ADA_SHARD), lambda l: (l, 0, 0))],
        out_specs=pl.BlockSpec((1, D_MODEL, ADA_SHARD), lambda l: (l, 0, 0)),
        compiler_params=_params(("arbitrary",)),
    )(c_all, dmod_cols)


def _comm_plumbing(comm):
    if not comm:
        return 0, [], []
    return len(comm[1]), _comm_out_shapes(*comm), _comm_scratch(len(comm[1]))


def _split_refs(refs, n_in, n_out, n_scratch, comm):
    ci, shapes, _ = _comm_plumbing(comm)
    co = len(shapes)
    a, b, c = n_in + ci, n_in + ci + n_out, n_in + ci + n_out + co
    return refs[:n_in], refs[a:b], refs[c:c + n_scratch], refs[n_in:a], refs[b:c], refs[c + n_scratch:]


def _prenorm_proj(x, g_pre, scale, shift, w_new, cos, sin_signed, comm=None, ts=256):
    s_len = x.shape[0]
    n_cin, c_shapes, c_scratch = _comm_plumbing(comm)

    def body(*refs):
        (x_ref, g_ref, sc_ref, sh_ref, w_ref, cos_ref, sin_ref), (pf_ref, pb_ref, h_ref), _, cin, cout, csem = (
            _split_refs(refs, 7, 3, 0, comm))
        comm_before, comm_after = _comm_hooks(comm, cin, cout, csem, steps=s_len // ts)
        comm_before()
        xv = x_ref[...]
        rstd = lax.rsqrt(jnp.mean(xv * xv, axis=-1, keepdims=True) + EPS)
        h = (xv * rstd * g_ref[...]) * (1.0 + sc_ref[...]) + sh_ref[...]
        hb = h.astype(BF16)
        h_ref[...] = hb
        for j in range(0, NP, 512):
            w = min(512, NP - j)
            acc = _dot(hb, w_ref[:, j:j + w])
            if COL_QB <= j < COL_VA:
                for lo in range(0, w, DIL_HD):
                    pf_ref[:, j + lo:j + lo + DIL_HD] = _rope(acc[:, lo:lo + DIL_HD], cos_ref[...], sin_ref[...])
            elif j < NP_F32:
                pf_ref[:, j:j + w] = acc
            else:
                pb_ref[:, j - NP_F32:j - NP_F32 + w] = acc.astype(BF16)
        comm_after()

    (g_pre, g_spec), (scale, sc_spec), (shift, sh_spec) = _rowvec(g_pre), _rowvec(scale), _rowvec(shift)
    return pl.pallas_call(
        body, name="prenorm_proj_comm" if comm else "prenorm_proj", grid=(s_len // ts,),
        out_shape=[jax.ShapeDtypeStruct((s_len, NP_F32), F32), jax.ShapeDtypeStruct((s_len, NP_BF16), BF16),
                   jax.ShapeDtypeStruct((s_len, D_MODEL), BF16)] + c_shapes,
        in_specs=[pl.BlockSpec((ts, D_MODEL), lambda i: (i, 0)), g_spec, sc_spec, sh_spec,
                  pl.BlockSpec((D_MODEL, NP), lambda i: (0, 0)), pl.BlockSpec((ts, DIL_HD), lambda i: (i, 0)),
                  pl.BlockSpec((ts, DIL_HD), lambda i: (i, 0))] + [ANY] * n_cin,
        out_specs=[pl.BlockSpec((ts, NP_F32), lambda i: (i, 0)), pl.BlockSpec((ts, NP_BF16), lambda i: (i, 0)),
                   pl.BlockSpec((ts, D_MODEL), lambda i: (i, 0))] + [ANY] * len(c_shapes),
        scratch_shapes=c_scratch,
        compiler_params=_params(("arbitrary",), 48),
    )(x, g_pre, scale, shift, w_new, cos, sin_signed, *(comm[1] if comm else []))


GLA_GROUP = 16


def _gla_group_rows(t):
    return [pl.ds(pl.multiple_of((t * GLA_GROUP + j) * GLA_CHUNK, GLA_CHUNK), GLA_CHUNK) for j in range(GLA_GROUP)]


def _gla_chunks_common(q_ref, k_ref, lr_ref, wgu_ref, bgu_ref, rows_list):
    c = GLA_CHUNK
    ri = lax.broadcasted_iota(jnp.int32, (c, c), 0)
    ci = lax.broadcasted_iota(jnp.int32, (c, c), 1)
    tril = (ri >= ci).astype(F32)
    zs = [_dot(lr_ref[rows, :], wgu_ref[...]) + bgu_ref[...] for rows in rows_list]
    las = [_log_sigmoid(z) * (1.0 / GLA_TAU) for z in zs]
    bs = [jnp.dot(tril, la, precision=lax.Precision.HIGHEST, preferred_element_type=F32) for la in las]
    out = []
    for rows, z, b in zip(rows_list, zs, bs):
        q = q_ref[rows, :] * (GLA_DK ** -0.5)
        k = k_ref[rows, :]
        bl = b[c - 1:c, :]
        out.append(dict(z=z, b=b, bl=bl, qe=q * jnp.exp(b), ke=k * jnp.exp(-b), kend=k * jnp.exp(bl - b),
                        dec=jnp.exp(bl)))
    return out, ri, ci


def _head_lane_mask(hh):
    return (lax.broadcasted_iota(jnp.int32, (1, LANE), 1) // GLA_DK) == hh


def _state_block_mask():
    r = lax.broadcasted_iota(jnp.int32, (2 * GLA_DV, LANE), 0) // GLA_DV
    cc = lax.broadcasted_iota(jnp.int32, (2 * GLA_DV, LANE), 1) // GLA_DK
    return r == cc


def _gla_fwd(pf, pb, wgu, bgu, layer, comm=None):
    s_len = pf.shape[0]
    nc = s_len // GLA_CHUNK
    ncomm = len(comm[1]) if comm else 0

    def body(*refs):
        q_ref, k_ref, v_ref, lr_ref, wgu_ref, bgu_ref = refs[:6]
        cin, (o_ref, st_ref), cout = refs[6:6 + ncomm], refs[6 + ncomm:8 + ncomm], refs[8 + ncomm:8 + 2 * ncomm]
        qe_s, cs_s, dec_s = refs[8 + 2 * ncomm:11 + 2 * ncomm]
        comm_before, comm_after = _comm_hooks(comm, cin, cout, refs[11 + 2 * ncomm:], steps=2)
        comm_before()
        bd = _state_block_mask()

        def local(t, carry):
            rows_list = _gla_group_rows(t)
            cm, ri, ci = _gla_chunks_common(q_ref, k_ref, lr_ref, wgu_ref, bgu_ref, rows_list)
            vs = [v_ref[rows, :] for rows in rows_list]
            kebs = [c["ke"].astype(BF16) for c in cm]
            a = [[jnp.where(ri >= ci, _dot_nt(jnp.where(_head_lane_mask(hh), c["qe"], 0.0).astype(BF16), keb), 0.0)
                  .astype(BF16) for hh in range(2)] for c, keb in zip(cm, kebs)]
            oi = [[_dot(ah[hh], v[:, hh * GLA_DV:(hh + 1) * GLA_DV]) for hh in range(2)] for ah, v in zip(a, vs)]
            cs = [jnp.where(bd, _dot_tn(v, c["kend"].astype(BF16)), 0.0) for c, v in zip(cm, vs)]
            for j, (rows, c) in enumerate(zip(rows_list, cm)):
                n = t * GLA_GROUP + j
                o_ref[rows, :] = jnp.concatenate(oi[j], axis=1)
                qe_s[rows, :] = c["qe"].astype(BF16)
                cs_s[n] = cs[j]
                dec_s[n] = jnp.broadcast_to(c["dec"], (8, LANE))
            return carry

        lax.fori_loop(0, nc // GLA_GROUP, local, 0)

        def scan(n, st):
            st_ref[0, n] = st.astype(BF16)
            return dec_s[n][0:1, :] * st + cs_s[n]

        lax.fori_loop(0, nc, scan, jnp.zeros((2 * GLA_DV, LANE), F32))

        def inter(t, carry):
            rows_list = _gla_group_rows(t)
            add = [_dot_nt(qe_s[rows, :], st_ref[0, t * GLA_GROUP + j]) for j, rows in enumerate(rows_list)]
            for rows, av in zip(rows_list, add):
                o_ref[rows, :] = o_ref[rows, :] + av
            return carry

        lax.fori_loop(0, nc // GLA_GROUP, inter, 0)
        comm_after()

    return pl.pallas_call(
        body, name="gla_fwd_comm" if comm else "gla_fwd", grid=(2,),
        out_shape=[jax.ShapeDtypeStruct((s_len, GLA_HEADS * GLA_DV), F32),
                   jax.ShapeDtypeStruct((2, nc, 2 * GLA_DV, LANE), BF16)] + (_comm_out_shapes(*comm) if comm else []),
        in_specs=[pl.BlockSpec((s_len, LANE), lambda g: (0, COL_QA // LANE + g)),
                  pl.BlockSpec((s_len, LANE), lambda g: (0, COL_KA // LANE + g)),
                  pl.BlockSpec((s_len, 2 * GLA_DV), lambda g: (0, (COL_VA - NP_F32) // (2 * GLA_DV) + g)),
                  pl.BlockSpec((s_len, LANE), lambda g: (0, (COL_LR - NP_F32) // LANE)),
                  pl.BlockSpec((None, LANE, LANE), lambda g: (layer, 0, g)),
                  pl.BlockSpec((None, 1, LANE), lambda g: (layer, 0, g))] + [ANY] * ncomm,
        out_specs=[pl.BlockSpec((s_len, 2 * GLA_DV), lambda g: (0, g)),
                   pl.BlockSpec((1, nc, 2 * GLA_DV, LANE), lambda g: (g, 0, 0, 0))] + [ANY] * ncomm,
        scratch_shapes=[pltpu.VMEM((s_len, LANE), BF16), pltpu.VMEM((nc, 2 * GLA_DV, LANE), F32),
                        pltpu.VMEM((nc, 8, LANE), F32)] + (_comm_scratch(ncomm) if comm else []),
        compiler_params=_params(("arbitrary",), 56),
    )(pf, pf, pb, pb, wgu, bgu.reshape(bgu.shape[0], 1, GU_COLS), *(comm[1] if comm else []))


def _rope_tables(s_len):
    inv_freq = ROPE_THETA ** (-jnp.arange(0, DIL_HD, 2, dtype=F32) / DIL_HD)
    ang = jnp.arange(s_len, dtype=F32)[:, None] * inv_freq[None, :]
    cos, sin = jnp.cos(ang), jnp.sin(ang)
    return jnp.concatenate([cos, cos], axis=1), jnp.concatenate([-sin, sin], axis=1)


def _rope(xv, cos, sin_signed):
    return xv * cos + pltpu.roll(xv, DIL_HD // 2, 1) * sin_signed


DIL_GROUP = 8


def _dil_pair_block(i, half, d, nblk, group=DIL_GROUP):
    nb = nblk // d
    j = i + half * (nblk // group)
    if nb >= 2 * group:
        r, n = j % d, j // d
    else:
        r, n = j // nb, j % nb
    kb = jnp.maximum(n - 1, 0)
    qs = r + d * DIL_BLOCK * n
    ks = r + d * DIL_BLOCK * kb
    return qs, ks, jnp.minimum(n, 1)


def _dil_fill_bias(bias):
    qi = lax.broadcasted_iota(jnp.int32, (DIL_BLOCK, 2 * DIL_BLOCK), 0)
    kj = lax.broadcasted_iota(jnp.int32, (DIL_BLOCK, 2 * DIL_BLOCK), 1)
    for sel in range(2):
        dist = qi - kj + DIL_BLOCK * sel
        bias[sel] = jnp.where((dist >= 0) & (dist <= DIL_BLOCK), 0.0, MASK_VALUE)


def _strided(start, size, d):
    return pl.ds(start, size) if d == 1 else pl.ds(start, size, stride=d)


def _comm_hooks(comm, cin, cout, csem, steps=DIL_HEADS):
    def before():
        if comm:
            @pl.when(pl.program_id(0) == 0)
            def _():
                _comm_run(comm[0], ("start",), cin, cout, *csem)

            if comm[0] == "gather":
                @pl.when(pl.program_id(0) == steps - 1)
                def _():
                    _comm_run(comm[0], ("forward",), cin, cout, *csem)

            if comm[0] == "pairsum_exchange":
                @pl.when(pl.program_id(0) == (1 if steps <= 4 else 2))
                def _():
                    _comm_run(comm[0], ("reduce", "send"), cin, cout, *csem)

    def after():
        if comm:
            @pl.when(pl.program_id(0) == steps - 1)
            def _():
                _comm_run(comm[0], ("finish",), cin, cout, *csem)

    return before, after


def _dil_fwd(pf, pb, comm=None):
    s_len = pf.shape[0]
    nblk = s_len // DIL_BLOCK
    prep_rows = 256
    scale = DIL_HD ** -0.5
    nc = len(comm[1]) if comm else 0

    def body(*refs):
        ((qf, kf, v_ref), (o_ref, lse_ref), (vf, o0, o1, o2, l0, l1, l2, bias), cin, cout, csem) = _split_refs(
            refs, 3, 2, 8, comm)
        comm_before, comm_after = _comm_hooks(comm, cin, cout, csem)
        comm_before()
        _dil_fill_bias(bias)

        def prep(t, carry):
            rows = pl.ds(pl.multiple_of(t * prep_rows, prep_rows), prep_rows)
            vf[rows, :] = v_ref[rows, :].astype(F32)
            return carry

        lax.fori_loop(0, s_len // prep_rows, prep, 0)
        for d, o_p, l_p in zip(DIL_DILATIONS, (o0, o1, o2), (l0, l1, l2)):
            if nblk // d == 2:
                units = DIL_GROUP // 2

                def whole(i, carry, d=d, o_p=o_p, l_p=l_p, units=units):
                    rows = [_strided(i + u * (d // units), 2 * DIL_BLOCK, d) for u in range(units)]
                    ld = [(qf[rw, :].astype(BF16), kf[rw, :].astype(BF16), vf[rw, :].astype(BF16)) for rw in rows]
                    both = bias[...].reshape(2 * DIL_BLOCK, 2 * DIL_BLOCK)
                    s = [_dot_nt(qb, kk) * scale + both for qb, kk, _ in ld]
                    m = [jnp.max(sv, axis=-1, keepdims=True) for sv in s]
                    p = [jnp.exp(sv - mv) for sv, mv in zip(s, m)]
                    den = [jnp.sum(pv, axis=-1, keepdims=True) for pv in p]
                    r = [_dot(pv.astype(BF16), vv) for pv, (_, _, vv) in zip(p, ld)]
                    for rv, dv, mv, rw in zip(r, den, m, rows):
                        o_p[rw, :] = rv / dv
                        l_p[rw, :] = jnp.broadcast_to(mv + jnp.log(dv), (2 * DIL_BLOCK, DIL_HD))
                    return carry

                lax.fori_loop(0, d // units, whole, 0)
                continue

            def pair(i, carry, d=d, o_p=o_p, l_p=l_p):
                idx = [_dil_pair_block(i, half, d, nblk, DIL_GROUP) for half in range(DIL_GROUP)]
                ld = [(qf[_strided(qs, DIL_BLOCK, d), :].astype(BF16),
                       kf[_strided(ks, 2 * DIL_BLOCK, d), :].astype(BF16),
                       vf[_strided(ks, 2 * DIL_BLOCK, d), :].astype(BF16)) for qs, ks, _ in idx]
                s = [_dot_nt(qb, kk) * scale + bias[sel] for (qb, kk, _), (_, _, sel) in zip(ld, idx)]
                m = [jnp.max(sv, axis=-1, keepdims=True) for sv in s]
                p = [jnp.exp(sv - mv) for sv, mv in zip(s, m)]
                den = [jnp.sum(pv, axis=-1, keepdims=True) for pv in p]
                r = [_dot(pv.astype(BF16), vv) for pv, (_, _, vv) in zip(p, ld)]
                for rv, dv, mv, (qs, _, _) in zip(r, den, m, idx):
                    o_p[_strided(qs, DIL_BLOCK, d), :] = rv / dv
                    l_p[_strided(qs, DIL_BLOCK, d), :] = jnp.broadcast_to(mv + jnp.log(dv), (DIL_BLOCK, DIL_HD))
                return carry

            lax.fori_loop(0, nblk // DIL_GROUP, pair, 0)

        def comb(t, carry):
            rows = pl.ds(pl.multiple_of(t * prep_rows, prep_rows), prep_rows)
            a0, a1, a2 = l0[rows, :], l1[rows, :], l2[rows, :]
            m = jnp.maximum(jnp.maximum(a0, a1), a2)
            e0, e1, e2 = jnp.exp(a0 - m), jnp.exp(a1 - m), jnp.exp(a2 - m)
            tot = e0 + e1 + e2
            o_ref[rows, :] = (e0 * o0[rows, :] + e1 * o1[rows, :] + e2 * o2[rows, :]) / tot
            lse_ref[rows, :] = m + jnp.log(tot)
            return carry

        lax.fori_loop(0, s_len // prep_rows, comb, 0)
        comm_after()

    head = lambda base: pl.BlockSpec((s_len, DIL_HD), lambda h: (0, base // DIL_HD + h))
    out = pl.BlockSpec((s_len, DIL_HD), lambda h: (0, h))
    shp = jax.ShapeDtypeStruct((s_len, DIL_HEADS * DIL_HD), F32)
    return pl.pallas_call(
        body, name="dil_fwd_comm" if comm else "dil_fwd", grid=(DIL_HEADS,),
        out_shape=[shp, shp] + (_comm_out_shapes(*comm) if comm else []),
        in_specs=[head(COL_QB), head(COL_KB), head(COL_VB - NP_F32)] + [ANY] * nc,
        out_specs=[out, out] + [ANY] * nc,
        scratch_shapes=[pltpu.VMEM((s_len, DIL_HD), F32) for _ in range(7)]
        + [pltpu.VMEM((2, DIL_BLOCK, 2 * DIL_BLOCK), F32)] + (_comm_scratch(nc) if comm else []),
        compiler_params=_params(("arbitrary",), 56),
    )(pf, pf, pb, *(comm[1] if comm else []))


def _silu_and_grad(z):
    sg = _sigmoid(z)
    return z * sg, sg * (1.0 + z * (1.0 - sg))


def _post_fwd(o_a, o_b, pf, g_heads, w_out, x, gate, g_post, target=None, ts=256):
    s_len = x.shape[0]
    half = GLA_HEADS * GLA_DV
    last = target is not None

    def body(*refs):
        oa_ref, ob_ref, z_ref, gh_ref, w_ref, x_ref, gate_ref, gp_ref = refs[:8]
        xo_ref, u_ref = refs[8 + last:10 + last]
        y_ref = refs[-1]
        for src, base in ((oa_ref, 0), (ob_ref, half)):
            for hh in range(4):
                lo = hh * LANE
                og = src[:, lo:lo + LANE]
                on = og * lax.rsqrt(jnp.mean(og * og, axis=-1, keepdims=True) + EPS)
                zg = z_ref[:, base + lo:base + lo + LANE].astype(F32)
                y_ref[:, base + lo:base + lo + LANE] = (on * gh_ref[:, base + lo:base + lo + LANE]
                                                        * (zg * _sigmoid(zg))).astype(BF16)
        u = _dot(y_ref[...], w_ref[...])
        u_ref[...] = u.astype(BF16)
        rstd = lax.rsqrt(jnp.mean(u * u, axis=-1, keepdims=True) + EPS)
        x_out = x_ref[...] + gate_ref[...] * (u * rstd * gp_ref[...])
        if last:
            t_ref, loss_ref = refs[8], refs[11]

            @pl.when(pl.program_id(0) == 0)
            def _():
                loss_ref[...] = jnp.zeros_like(loss_ref)

            e = x_out - t_ref[...]
            xo_ref[...] = e * (1.0 / D_MODEL)
            loss_ref[...] += 0.5 * jnp.sum(jnp.mean(e * e, axis=-1, keepdims=True))
        else:
            xo_ref[...] = x_out

    (g_heads, gh_spec), (gate, gate_spec), (g_post, gp_spec) = _rowvec(g_heads), _rowvec(gate), _rowvec(g_post)
    tile = pl.BlockSpec((ts, D_MODEL), lambda i: (i, 0))
    halft = pl.BlockSpec((ts, half), lambda i: (i, 0))
    return pl.pallas_call(
        body, name="post_fwd_loss" if last else "post_fwd", grid=(s_len // ts,),
        out_shape=[jax.ShapeDtypeStruct((s_len, D_MODEL), F32), jax.ShapeDtypeStruct((s_len, D_MODEL), BF16)]
        + ([jax.ShapeDtypeStruct((8, LANE), F32)] if last else []),
        in_specs=[halft, halft, tile, gh_spec, pl.BlockSpec((D_MODEL, D_MODEL), lambda i: (0, 0)), tile, gate_spec,
                  gp_spec] + ([tile] if last else []),
        out_specs=[tile, tile] + ([pl.BlockSpec((8, LANE), lambda i: (0, 0))] if last else []),
        scratch_shapes=[pltpu.VMEM((ts, D_MODEL), BF16)],
        compiler_params=_params(("arbitrary",), 40),
    )(o_a, o_b, pf, g_heads, w_out, x, gate, g_post, *([target] if last else []))


def _post_bwd(dxo, u, gate, g_post, w_out, o_a, o_b, pf, g_heads, ts=512):
    s_len = dxo.shape[0]
    half = GLA_HEADS * GLA_DV
    steps = s_len // ts

    def body(dx_ref, u_ref, gate_ref, gp_ref, w_ref, oa_ref, ob_ref, z_ref, gh_ref, do_ref, dz_ref, sums_ref, gw_ref,
             y_s, acc):
        @pl.when(pl.program_id(0) == 0)
        def _():
            sums_ref[...] = jnp.zeros_like(sums_ref)
            acc[...] = jnp.zeros_like(acc)

        dx = dx_ref[...]
        u = u_ref[...].astype(F32)
        rstd = lax.rsqrt(jnp.mean(u * u, axis=-1, keepdims=True) + EPS)
        un = u * rstd
        sums_ref[0:1, :] += jnp.sum(dx * (un * gp_ref[...]), axis=0, keepdims=True)
        drn = dx * gate_ref[...]
        sums_ref[1:2, :] += jnp.sum(drn * un, axis=0, keepdims=True)
        dun = drn * gp_ref[...]
        du = rstd * (dun - un * jnp.mean(dun * un, axis=-1, keepdims=True))
        dub = du.astype(BF16)
        dy = _dot_nt(dub, w_ref[...])
        for src, base in ((oa_ref, 0), (ob_ref, half)):
            for hh in range(4):
                lo = base + hh * LANE
                og = src[:, hh * LANE:(hh + 1) * LANE]
                rs = lax.rsqrt(jnp.mean(og * og, axis=-1, keepdims=True) + EPS)
                on = og * rs
                zg = z_ref[:, lo:lo + LANE].astype(F32)
                sz, dsz = _silu_and_grad(zg)
                gg = gh_ref[:, lo:lo + LANE]
                dyg = dy[:, lo:lo + LANE]
                y_s[:, lo:lo + LANE] = (on * gg * sz).astype(BF16)
                sums_ref[2:3, lo:lo + LANE] += jnp.sum(dyg * sz * on, axis=0, keepdims=True)
                dz_ref[:, lo:lo + LANE] = (dyg * on * gg * dsz).astype(BF16)
                don = dyg * gg * sz
                do_ref[:, lo:lo + LANE] = (rs * (don - on * jnp.mean(don * on, axis=-1, keepdims=True))).astype(BF16)
        acc[...] += _dot_tn(y_s[...], dub)

        @pl.when(pl.program_id(0) == steps - 1)
        def _():
            gw_ref[...] = acc[...].astype(BF16)

    (g_heads, gh_spec), (gate, gate_spec), (g_post, gp_spec) = _rowvec(g_heads), _rowvec(gate), _rowvec(g_post)
    tile = pl.BlockSpec((ts, D_MODEL), lambda i: (i, 0))
    halft = pl.BlockSpec((ts, half), lambda i: (i, 0))
    whole = pl.BlockSpec((D_MODEL, D_MODEL), lambda i: (0, 0))
    return pl.pallas_call(
        body, name="post_bwd", grid=(steps,),
        out_shape=(jax.ShapeDtypeStruct((s_len, D_MODEL), BF16), jax.ShapeDtypeStruct((s_len, D_MODEL), BF16),
                   jax.ShapeDtypeStruct((8, D_MODEL), F32), jax.ShapeDtypeStruct((D_MODEL, D_MODEL), BF16)),
        in_specs=[tile, tile, gate_spec, gp_spec, whole, halft, halft, tile, gh_spec],
        out_specs=(tile, tile, pl.BlockSpec((8, D_MODEL), lambda i: (0, 0)), whole),
        scratch_shapes=[pltpu.VMEM((ts, D_MODEL), BF16), pltpu.VMEM((D_MODEL, D_MODEL), F32)],
        compiler_params=_params(("arbitrary",), 48),
    )(dxo, u, gate, g_post, w_out, o_a, o_b, pf, g_heads)


def _gla_bwd(pf, pb, wgu, bgu, layer, states, do, comm=None):
    s_len = pf.shape[0]
    nc = s_len // GLA_CHUNK
    c = GLA_CHUNK
    n_cin, c_shapes, c_scratch = _comm_plumbing(comm)

    def body(*refs):
        ((q_ref, k_ref, v_ref, lr_ref, wgu_ref, bgu_ref, st_ref, do_ref),
         (dq_ref, dk_ref, dv_ref, dlr_ref, dwgu_ref, dbgu_ref), (ds_s, dec_s, dw_acc, db_acc),
         cin, cout, csem) = _split_refs(refs, 8, 6, 4, comm)
        comm_before, comm_after = _comm_hooks(comm, cin, cout, csem, steps=2)
        comm_before()
        dw_acc[...] = jnp.zeros_like(dw_acc)
        db_acc[...] = jnp.zeros_like(db_acc)
        bd = _state_block_mask()
        last_row = lax.broadcasted_iota(jnp.int32, (c, LANE), 0) == c - 1

        def local(t, carry):
            rows_list = _gla_group_rows(t)
            cm, _, _ = _gla_chunks_common(q_ref, k_ref, lr_ref, wgu_ref, bgu_ref, rows_list)
            loc = [jnp.where(bd, _dot_tn(do_ref[rows, :], cc["qe"].astype(BF16)), 0.0)
                   for rows, cc in zip(rows_list, cm)]
            for j, cc in enumerate(cm):
                ds_s[t * GLA_GROUP + j] = loc[j]
                dec_s[t * GLA_GROUP + j] = jnp.broadcast_to(cc["dec"], (8, LANE))
            return carry

        lax.fori_loop(0, nc // GLA_GROUP, local, 0)

        def scan(t, dst):
            n = nc - 1 - t
            loc = ds_s[n]
            ds_s[n] = dst
            return dec_s[n][0:1, :] * dst + loc

        lax.fori_loop(0, nc, scan, jnp.zeros((2 * GLA_DV, LANE), F32))

        def rest(t, carry):
            rows_list = _gla_group_rows(t)
            cm, ri, ci = _gla_chunks_common(q_ref, k_ref, lr_ref, wgu_ref, bgu_ref, rows_list)
            ns = [t * GLA_GROUP + j for j in range(GLA_GROUP)]
            vs = [v_ref[rows, :] for rows in rows_list]
            dobs = [do_ref[rows, :] for rows in rows_list]
            stbs = [st_ref[0, n] for n in ns]
            dsts = [ds_s[n] for n in ns]
            dstbs = [d.astype(BF16) for d in dsts]
            qebs = [cc["qe"].astype(BF16) for cc in cm]
            kebs = [cc["ke"].astype(BF16) for cc in cm]
            kendbs = [cc["kend"].astype(BF16) for cc in cm]
            hms = [_head_lane_mask(hh) for hh in range(2)]
            qehs = [[jnp.where(hm, cc["qe"], 0.0).astype(BF16) for hm in hms] for cc in cm]
            kehs = [[jnp.where(hm, cc["ke"], 0.0).astype(BF16) for hm in hms] for cc in cm]
            heads = lambda x: [x[:, hh * GLA_DV:(hh + 1) * GLA_DV] for hh in range(2)]
            vhs, dohs = [heads(v) for v in vs], [heads(d) for d in dobs]

            dqe0 = [_dot(dob, stb) for dob, stb in zip(dobs, stbs)]
            dkend = [_dot(v, dstb) for v, dstb in zip(vs, dstbs)]
            dv0 = [_dot_nt(kb, dstb) for kb, dstb in zip(kendbs, dstbs)]
            a_t = [[jnp.where(ci >= ri, _dot_nt(kehs[j][hh], qebs[j]), 0.0).astype(BF16) for hh in range(2)]
                   for j in range(GLA_GROUP)]
            da = [[jnp.where(ri >= ci, _dot_nt(dohs[j][hh], vhs[j][hh]), 0.0).astype(BF16) for hh in range(2)]
                  for j in range(GLA_GROUP)]
            da_t = [[jnp.where(ci >= ri, _dot_nt(vhs[j][hh], dohs[j][hh]), 0.0).astype(BF16) for hh in range(2)]
                    for j in range(GLA_GROUP)]
            dv1 = [[_dot(a_t[j][hh], dohs[j][hh]) for hh in range(2)] for j in range(GLA_GROUP)]
            dqe1 = [[_dot(da[j][hh], kebs[j]) for hh in range(2)] for j in range(GLA_GROUP)]
            dke1 = [[_dot(da_t[j][hh], qehs[j][hh]) for hh in range(2)] for j in range(GLA_GROUP)]

            dbs, dzs = [], []
            for j, (rows, cc) in enumerate(zip(rows_list, cm)):
                qe, ke, kend, b, bl = cc["qe"], cc["ke"], cc["kend"], cc["b"], cc["bl"]
                dqe = dqe0[j] + jnp.where(hms[0], dqe1[j][0], 0.0) + jnp.where(hms[1], dqe1[j][1], 0.0)
                dke = jnp.where(hms[0], dke1[j][0], 0.0) + jnp.where(hms[1], dke1[j][1], 0.0)
                dv_ref[rows, :] = (dv0[j] + jnp.concatenate(dv1[j], axis=1)).astype(BF16)
                dq_ref[rows, :] = (dqe * jnp.exp(b) * (GLA_DK ** -0.5)).astype(BF16)
                dk_ref[rows, :] = (dke * jnp.exp(-b) + dkend[j] * jnp.exp(bl - b)).astype(BF16)
                ddec = jnp.sum(dsts[j] * stbs[j].astype(F32), axis=0, keepdims=True)
                dbl = jnp.sum(dkend[j] * kend, axis=0, keepdims=True) + ddec * cc["dec"]
                dbs.append(dqe * qe - dke * ke - dkend[j] * kend + jnp.where(last_row, dbl, 0.0))
            triu = (ci >= ri).astype(F32)
            dlas = [jnp.dot(triu, db, precision=lax.Precision.HIGHEST, preferred_element_type=F32) for db in dbs]
            dzs = [dla * (1.0 / GLA_TAU) * _sigmoid(-cc["z"]) for dla, cc in zip(dlas, cm)]
            dzbs = [dz.astype(BF16) for dz in dzs]
            dlrs = [_dot_nt(dzb, wgu_ref[...]) for dzb in dzbs]
            dws = [_dot_tn(lr_ref[rows, :], dzb) for rows, dzb in zip(rows_list, dzbs)]
            for rows, dlr in zip(rows_list, dlrs):
                dlr_ref[0, rows, :] = dlr
            dw_acc[...] += functools.reduce(lambda x, y: x + y, dws)
            db_acc[0:1, :] += jnp.sum(functools.reduce(lambda x, y: x + y, dzs), axis=0, keepdims=True)
            return carry

        lax.fori_loop(0, nc // GLA_GROUP, rest, 0)
        dwgu_ref[...] = dw_acc[...]
        dbgu_ref[...] = db_acc[...]
        comm_after()

    pair = pl.BlockSpec((s_len, LANE), lambda g: (0, g))
    return pl.pallas_call(
        body, name="gla_bwd_comm" if comm else "gla_bwd", grid=(2,),
        out_shape=[jax.ShapeDtypeStruct((s_len, GU_COLS), BF16), jax.ShapeDtypeStruct((s_len, GU_COLS), BF16),
                   jax.ShapeDtypeStruct((s_len, GLA_HEADS * GLA_DV), BF16),
                   jax.ShapeDtypeStruct((2, s_len, LANE), F32),
                   jax.ShapeDtypeStruct((LANE, GU_COLS), F32), jax.ShapeDtypeStruct((8, GU_COLS), F32)] + c_shapes,
        in_specs=[pl.BlockSpec((s_len, LANE), lambda g: (0, COL_QA // LANE + g)),
                  pl.BlockSpec((s_len, LANE), lambda g: (0, COL_KA // LANE + g)),
                  pl.BlockSpec((s_len, 2 * GLA_DV), lambda g: (0, (COL_VA - NP_F32) // (2 * GLA_DV) + g)),
                  pl.BlockSpec((s_len, LANE), lambda g: (0, (COL_LR - NP_F32) // LANE)),
                  pl.BlockSpec((None, LANE, LANE), lambda g: (layer, 0, g)),
                  pl.BlockSpec((None, 1, LANE), lambda g: (layer, 0, g)),
                  pl.BlockSpec((1, nc, 2 * GLA_DV, LANE), lambda g: (g, 0, 0, 0)),
                  pl.BlockSpec((s_len, 2 * GLA_DV), lambda g: (0, g))] + [ANY] * n_cin,
        out_specs=[pair, pair, pl.BlockSpec((s_len, 2 * GLA_DV), lambda g: (0, g)),
                   pl.BlockSpec((1, s_len, LANE), lambda g: (g, 0, 0)),
                   pl.BlockSpec((LANE, LANE), lambda g: (0, g)), pl.BlockSpec((8, LANE), lambda g: (0, g))]
        + [ANY] * len(c_shapes),
        scratch_shapes=[pltpu.VMEM((nc, 2 * GLA_DV, LANE), F32), pltpu.VMEM((nc, 8, LANE), F32),
                        pltpu.VMEM((LANE, LANE), F32), pltpu.VMEM((8, LANE), F32)] + c_scratch,
        compiler_params=_params(("arbitrary",), 56),
    )(pf, pf, pb, pb, wgu, bgu.reshape(bgu.shape[0], 1, GU_COLS), states, do, *(comm[1] if comm else []))


def _dil_bwd(pf, pb, cos, sin_signed, do, o_b, lse, comm=None):
    s_len = pf.shape[0]
    nblk = s_len // DIL_BLOCK
    prep_rows = 256
    scale = DIL_HD ** -0.5
    nc = len(comm[1]) if comm else 0

    def body(*refs):
        ((q_ref, kf, v_ref, cos_ref, sin_ref, do_ref, o_ref, lse_ref), (dq_ref, dk_ref, dv_ref),
         (qf, vf, dof, dl, dqa, dka, dva, bias), cin, cout, csem) = _split_refs(refs, 8, 3, 8, comm)
        comm_before, comm_after = _comm_hooks(comm, cin, cout, csem)
        comm_before()
        _dil_fill_bias(bias)

        def prep(t, carry):
            rows = pl.ds(pl.multiple_of(t * prep_rows, prep_rows), prep_rows)
            qf[rows, :] = q_ref[rows, :] * scale
            vf[rows, :] = v_ref[rows, :].astype(F32)
            dov = do_ref[rows, :].astype(F32)
            dof[rows, :] = dov
            dl[rows, :] = jnp.broadcast_to(jnp.sum(dov * o_ref[rows, :], axis=-1, keepdims=True), (prep_rows, DIL_HD))
            zero = jnp.zeros((prep_rows, DIL_HD), F32)
            dqa[rows, :] = zero
            dka[rows, :] = zero
            dva[rows, :] = zero
            return carry

        lax.fori_loop(0, s_len // prep_rows, prep, 0)

        for d in DIL_DILATIONS:
            if nblk // d == 2:
                units = DIL_GROUP // 2

                def whole(i, carry, d=d, units=units):
                    rows = [_strided(i + u * (d // units), 2 * DIL_BLOCK, d) for u in range(units)]
                    ld = [(qf[rw, :].astype(BF16), kf[rw, :].astype(BF16), vf[rw, :].astype(BF16),
                           dof[rw, :].astype(BF16)) for rw in rows]
                    both = bias[...].reshape(2 * DIL_BLOCK, 2 * DIL_BLOCK)
                    s = [_dot_nt(qb, kk) + both for qb, kk, _, _ in ld]
                    dp = [_dot_nt(dob, vv) for _, _, vv, dob in ld]
                    p = [jnp.exp(sv - lse_ref[rw, :][:, 0:1]) for sv, rw in zip(s, rows)]
                    ds = [(pv * (dpv - dl[rw, :][:, 0:1])).astype(BF16) for pv, dpv, rw in zip(p, dp, rows)]
                    pb = [pv.astype(BF16) for pv in p]
                    gq = [_dot(dsv, kk) for dsv, (_, kk, _, _) in zip(ds, ld)]
                    gk = [_dot_tn(dsv, qb) for dsv, (qb, _, _, _) in zip(ds, ld)]
                    gv = [_dot_tn(pv, dob) for pv, (_, _, _, dob) in zip(pb, ld)]
                    for rw, a, b, c in zip(rows, gq, gk, gv):
                        dqa[rw, :] += a
                        dka[rw, :] += b
                        dva[rw, :] += c
                    return carry

                lax.fori_loop(0, d // units, whole, 0)
                continue

            def pair(i, carry, d=d):
                idx = [_dil_pair_block(i, half, d, nblk) for half in range(DIL_GROUP)]
                rows = [(_strided(qs, DIL_BLOCK, d), _strided(ks, 2 * DIL_BLOCK, d)) for qs, ks, _ in idx]
                ld = [(qf[qr, :].astype(BF16), kf[kr, :].astype(BF16), vf[kr, :].astype(BF16),
                       dof[qr, :].astype(BF16)) for qr, kr in rows]
                s = [_dot_nt(qb, kk) + bias[sel] for (qb, kk, _, _), (_, _, sel) in zip(ld, idx)]
                dp = [_dot_nt(dob, vv) for _, _, vv, dob in ld]
                p = [jnp.exp(sv - lse_ref[qr, :][:, 0:1]) for sv, (qr, _) in zip(s, rows)]
                ds = [(pv * (dpv - dl[qr, :][:, 0:1])).astype(BF16) for pv, dpv, (qr, _) in zip(p, dp, rows)]
                pb = [pv.astype(BF16) for pv in p]
                gq = [_dot(dsv, kk) for dsv, (_, kk, _, _) in zip(ds, ld)]
                gk = [_dot_tn(dsv, qb) for dsv, (qb, _, _, _) in zip(ds, ld)]
                gv = [_dot_tn(pv, dob) for pv, (_, _, _, dob) in zip(pb, ld)]
                for (qr, kr), a, b, c in zip(rows, gq, gk, gv):
                    dqa[qr, :] += a
                    dka[kr, :] += b
                    dva[kr, :] += c
                return carry

            lax.fori_loop(0, nblk // DIL_GROUP, pair, 0)

        def fin(t, carry):
            rows = pl.ds(pl.multiple_of(t * prep_rows, prep_rows), prep_rows)
            cs, sn = cos_ref[rows, :], sin_ref[rows, :]
            gq, gk = dqa[rows, :] * scale, dka[rows, :]
            dq_ref[rows, :] = (gq * cs - pltpu.roll(gq, DIL_HD // 2, 1) * sn).astype(BF16)
            dk_ref[rows, :] = (gk * cs - pltpu.roll(gk, DIL_HD // 2, 1) * sn).astype(BF16)
            dv_ref[rows, :] = dva[rows, :].astype(BF16)
            return carry

        lax.fori_loop(0, s_len // prep_rows, fin, 0)
        comm_after()

    head = lambda base: pl.BlockSpec((s_len, DIL_HD), lambda h: (0, base // DIL_HD + h))
    table = pl.BlockSpec((s_len, DIL_HD), lambda h: (0, 0))
    out = pl.BlockSpec((s_len, DIL_HD), lambda h: (0, h))
    shp = jax.ShapeDtypeStruct((s_len, DIL_HEADS * DIL_HD), BF16)
    return pl.pallas_call(
        body, name="dil_bwd_comm" if comm else "dil_bwd", grid=(DIL_HEADS,),
        out_shape=[shp, shp, shp] + (_comm_out_shapes(*comm) if comm else []),
        in_specs=[head(COL_QB), head(COL_KB), head(COL_VB - NP_F32), table, table,
                  pl.BlockSpec((s_len, DIL_HD), lambda h: (0, DIL_HEADS + h)), out, out] + [ANY] * nc,
        out_specs=[out, out, out] + [ANY] * len(_comm_plumbing(comm)[1]),
        scratch_shapes=[pltpu.VMEM((s_len, DIL_HD), F32) for _ in range(7)]
        + [pltpu.VMEM((2, DIL_BLOCK, 2 * DIL_BLOCK), F32)] + (_comm_scratch(nc) if comm else []),
        compiler_params=_params(("arbitrary",), 56),
    )(pf, pf, pb, cos, sin_signed, do, o_b, lse, *(comm[1] if comm else []))


_PIECES = ((COL_Z, 1024), (COL_QA, 256), (COL_KA, 256), (COL_QB, 512), (COL_KB, 512), (COL_VA, 512), (COL_VB, 512),
           (COL_LR, 128))


def _in_bwd(pieces, w_new, x, dxo, g_pre, scale, comm=None, ts=256):
    s_len = x.shape[0]
    nc = len(comm[1]) if comm else 0
    nco = len(_comm_out_shapes(*comm)) if comm else 0
    npc = len(_PIECES)

    def body(*refs):
        p_refs = refs[:npc]
        w_ref, x_ref, dxo_ref, g_ref, sc_ref = refs[npc:npc + 5]
        cin, (dx_ref, sums_ref), cout = (refs[npc + 5:npc + 5 + nc], refs[npc + 5 + nc:npc + 7 + nc],
                                         refs[npc + 7 + nc:npc + 7 + nc + nco])
        comm_before, comm_after = _comm_hooks(comm, cin, cout, refs[npc + 7 + nc + nco:], steps=s_len // ts)
        comm_before()

        @pl.when(pl.program_id(0) == 0)
        def _():
            sums_ref[...] = jnp.zeros_like(sums_ref)

        dh = jnp.zeros((ts, D_MODEL), F32)
        for p_ref, (col, width) in zip(p_refs, _PIECES):
            dh += _dot_nt(p_ref[...], w_ref[:, col:col + width])
        xv = x_ref[...]
        rstd = lax.rsqrt(jnp.mean(xv * xv, axis=-1, keepdims=True) + EPS)
        xn = xv * rstd
        sums_ref[0:1, :] += jnp.sum(dh, axis=0, keepdims=True)
        sums_ref[1:2, :] += jnp.sum(dh * (xn * g_ref[...]), axis=0, keepdims=True)
        dr = dh * (1.0 + sc_ref[...])
        sums_ref[2:3, :] += jnp.sum(dr * xn, axis=0, keepdims=True)
        dxn = dr * g_ref[...]
        dx_ref[...] = dxo_ref[...] + rstd * (dxn - xn * jnp.mean(dxn * xn, axis=-1, keepdims=True))
        comm_after()

    (g_pre, g_spec), (scale, sc_spec) = _rowvec(g_pre), _rowvec(scale)
    tile = pl.BlockSpec((ts, D_MODEL), lambda i: (i, 0))
    return pl.pallas_call(
        body, name="in_bwd_comm" if comm else "in_bwd", grid=(s_len // ts,),
        out_shape=[jax.ShapeDtypeStruct((s_len, D_MODEL), F32), jax.ShapeDtypeStruct((8, D_MODEL), F32)]
        + (_comm_out_shapes(*comm) if comm else []),
        in_specs=[pl.BlockSpec((ts, width), lambda i: (i, 0)) for _, width in _PIECES]
        + [pl.BlockSpec((D_MODEL, NP), lambda i: (0, 0)), tile, tile, g_spec, sc_spec] + [ANY] * nc,
        out_specs=[tile, pl.BlockSpec((8, D_MODEL), lambda i: (0, 0))] + [ANY] * nco,
        scratch_shapes=_comm_scratch(nc) if comm else [],
        compiler_params=_params(("arbitrary",), 56),
    )(*pieces, w_new, x, dxo, g_pre, scale, *(comm[1] if comm else []))


def _w_in_to_kernel(gathered, comm=None, tr=128):
    n_cin, c_shapes, c_scratch = _comm_plumbing(comm)
    n_parts = len(gathered)
    first = [sum(g.shape[1] for g in gathered[:p]) // tr for p in range(n_parts + 1)]

    def body(*refs):
        g_refs, (o_ref,), _, cin, cout, csem = _split_refs(refs, n_parts, 1, 0, comm)
        comm_before, comm_after = _comm_hooks(comm, cin, cout, csem, steps=D_MODEL // tr)
        comm_before()
        for p, g_ref in enumerate(g_refs):
            @pl.when((pl.program_id(0) >= first[p]) & (pl.program_id(0) < first[p + 1]))
            def _(g_ref=g_ref):
                cols = jnp.concatenate([g_ref[k].astype(F32) for k in range(N_DEV)], axis=1)
                pad = jnp.zeros((tr, LANE - GLA_LOWRANK), F32)
                o_ref[...] = jnp.concatenate(
                    [cols[:, 1024:1536], cols[:, 3088:3600], cols[:, 0:512], cols[:, 1552:2576], cols[:, 512:1024],
                     cols[:, 2576:3088], cols[:, 1536:1552], pad], axis=1).astype(BF16)
        comm_after()

    part = lambda p: pl.BlockSpec((N_DEV, tr, W_IN_SHARD),
                                  lambda i: (0, jnp.clip(i - first[p], 0, first[p + 1] - first[p] - 1), 0))
    return pl.pallas_call(
        body, name="w_in_to_kernel_comm" if comm else "w_in_to_kernel", grid=(D_MODEL // tr,),
        out_shape=[jax.ShapeDtypeStruct((D_MODEL, NP), BF16)] + c_shapes,
        in_specs=[part(p) for p in range(n_parts)] + [ANY] * n_cin,
        out_specs=[pl.BlockSpec((tr, NP), lambda i: (i, 0))] + [ANY] * len(c_shapes),
        scratch_shapes=c_scratch,
        compiler_params=_params(("arbitrary",)),
    )(*gathered, *(comm[1] if comm else []))


def _grad_w_in(h, pieces, ts=512, tr=128):
    s_len = h.shape[0]
    steps = s_len // ts

    def body(*refs):
        h_ref, p_refs = refs[0], refs[1:1 + len(_PIECES)]
        o_ref, acc = refs[1 + len(_PIECES):]

        @pl.when(pl.program_id(0) == 0)
        def _():
            acc[...] = jnp.zeros_like(acc)

        hv = h_ref[...]
        for p_ref, (col, width) in zip(p_refs, _PIECES):
            acc[:, col:col + width] += _dot_tn(hv, p_ref[...])

        @pl.when(pl.program_id(0) == steps - 1)
        def _():
            def rows_out(t, carry):
                rows = pl.ds(pl.multiple_of(t * tr, tr), tr)
                g = acc[rows, :]
                cols = jnp.concatenate(
                    [g[:, COL_QA:COL_QB], g[:, COL_VA:COL_VB], g[:, 0:512], g[:, COL_LR:COL_LR + GLA_LOWRANK],
                     g[:, COL_QB:COL_VA], g[:, COL_VB:COL_LR], g[:, 512:1024]], axis=1)
                for k in range(N_DEV):
                    o_ref[k, rows, :] = cols[:, W_IN_SHARD * k:W_IN_SHARD * (k + 1)].astype(BF16)
                return carry

            lax.fori_loop(0, D_MODEL // tr, rows_out, 0)

    return pl.pallas_call(
        body, name="grad_w_in", grid=(steps,),
        out_shape=jax.ShapeDtypeStruct((N_DEV, D_MODEL, W_IN_SHARD), BF16),
        in_specs=[pl.BlockSpec((ts, D_MODEL), lambda i: (i, 0))]
        + [pl.BlockSpec((ts, width), lambda i: (i, 0)) for _, width in _PIECES],
        out_specs=pl.BlockSpec((N_DEV, D_MODEL, W_IN_SHARD), lambda i: (0, 0, 0)),
        scratch_shapes=[pltpu.VMEM((D_MODEL, NP), F32)],
        compiler_params=_params(("arbitrary",), 56),
    )(h, *pieces)


def _adam_math(w, g, m, v):
    m = ADAM_B1 * m + (1.0 - ADAM_B1) * g
    v = ADAM_B2 * v + (1.0 - ADAM_B2) * (g * g)
    m_hat = m / (1.0 - ADAM_B1 ** ADAM_STEP)
    v_hat = v / (1.0 - ADAM_B2 ** ADAM_STEP)
    delta = -ADAM_LR * (m_hat / (jnp.sqrt(v_hat) + ADAM_EPS) + ADAM_WD * w)
    return delta, m, v


def _adamw(w, parts, m, v, name, tr):
    r, cdim = w.shape
    n_parts = parts.shape[0]

    def body(w_ref, p_ref, m_ref, v_ref, g_ref, d_ref, nm_ref, nv_ref):
        g = p_ref[0].astype(F32)
        for k in range(1, n_parts):
            g = g + p_ref[k].astype(F32)
        g_ref[...] = g
        d_ref[...], nm_ref[...], nv_ref[...] = _adam_math(w_ref[...], g, m_ref[...], v_ref[...])

    tile = pl.BlockSpec((tr, cdim), lambda i: (i, 0))
    shp = jax.ShapeDtypeStruct((r, cdim), F32)
    return pl.pallas_call(
        body, name=name, grid=(r // tr,), out_shape=(shp, shp, shp, shp),
        in_specs=[tile, pl.BlockSpec((n_parts, tr, cdim), lambda i: (0, i, 0)), tile, tile],
        out_specs=(tile, tile, tile, tile),
        compiler_params=_params(("arbitrary",), 40),
    )(w, parts, m, v)


def _adamw_layers(w, parts, m, v, name, tr):
    n_layers, r, cdim = w.shape

    def body(*refs):
        w_ref, p_refs, (m_ref, v_ref) = refs[0], refs[1:1 + n_layers], refs[1 + n_layers:3 + n_layers]
        g_ref, d_ref, nm_ref, nv_ref = refs[3 + n_layers:]
        for l, p_ref in enumerate(p_refs):
            @pl.when(pl.program_id(0) == l)
            def _(p_ref=p_ref):
                g = p_ref[0].astype(F32)
                for k in range(1, p_ref.shape[0]):
                    g = g + p_ref[k].astype(F32)
                g_ref[0] = g
                d_ref[0], nm_ref[0], nv_ref[0] = _adam_math(w_ref[0], g, m_ref[0], v_ref[0])

    tile = pl.BlockSpec((1, tr, cdim), lambda l, i: (l, i, 0))
    part = lambda own: pl.BlockSpec((parts[own].shape[0], tr, cdim), lambda l, i: (0, jnp.where(l == own, i, 0), 0))
    shp = jax.ShapeDtypeStruct(w.shape, F32)
    return pl.pallas_call(
        body, name=name, grid=(n_layers, r // tr), out_shape=(shp, shp, shp, shp),
        in_specs=[tile] + [part(l) for l in range(n_layers)] + [tile, tile],
        out_specs=(tile, tile, tile, tile),
        compiler_params=_params(("arbitrary", "arbitrary"), 40),
    )(w, *parts, m, v)


def _row(vec, width):
    vec = vec.reshape(1, -1)
    return jnp.pad(vec, ((0, 0), (0, width - vec.shape[1])))


def kernel(x, c, w_ada, b_ada, g_pre, w_in, w_gate_up, b_gate_up, g_gla, g_dil, w_out, g_post, loss_target, m_w_ada, m_b_ada, m_g_pre, m_w_in, m_w_gate_up, m_b_gate_up, m_g_gla, m_g_dil, m_w_out, m_g_post, v_w_ada, v_b_ada, v_g_pre, v_w_in, v_w_gate_up, v_b_gate_up, v_g_gla, v_g_dil, v_w_out, v_g_post):
    px, py, pc = _my_position()
    me = _linear(px, py, pc)
    xs = x[0]
    target = loss_target[0]
    s_len = xs.shape[0]
    assert s_len % (DIL_BLOCK * max(DIL_DILATIONS) * 2) == 0 and xs.shape[1] == D_MODEL

    w_in_b, w_out_b = w_in.astype(BF16), w_out.astype(BF16)
    c_rows, wgu_all, w_in_all = _comm_call(
        "gather", [jnp.pad(c, ((0, 7), (0, 0))), w_gate_up.reshape(DEPTH * GLA_LOWRANK, GU_SHARD), w_in_b[0]],
        "gather_first")
    c_all = c_rows.reshape(N_DEV, 8, D_MODEL)[:, 0]
    mod_part = _mod_fwd(c_all, w_ada)
    w_new, mod_all = _w_in_to_kernel([w_in_all.reshape(N_DEV, D_MODEL, W_IN_SHARD)],
                                     comm=("gather", [mod_part.reshape(DEPTH * N_DEV, ADA_SHARD)]))
    mod_all = mod_all.reshape(N_DEV, DEPTH, N_DEV, ADA_SHARD)
    mod_mine = lax.dynamic_index_in_dim(mod_all, me, axis=2, keepdims=False)
    mod = jnp.transpose(mod_mine, (1, 0, 2)).reshape(DEPTH, 3 * D_MODEL) + b_ada
    wgu_full = jnp.transpose(wgu_all.reshape(N_DEV, DEPTH, GLA_LOWRANK, GU_SHARD), (1, 2, 0, 3)).reshape(
        DEPTH, GLA_LOWRANK, GU_COLS)
    wgu_pad = jnp.pad(wgu_full, ((0, 0), (0, LANE - GLA_LOWRANK), (0, 0))).astype(BF16)

    cos, sin_signed = _rope_tables(s_len)
    g_heads = jnp.concatenate([g_gla, g_dil], axis=1)

    saved = []
    xl = xs
    for l in range(DEPTH):
        shift, scale, gate = ((mod, l, k) for k in range(3))
        if l > 0:
            w_new = _w_in_to_kernel([half.reshape(N_DEV, D_MODEL // 2, W_IN_SHARD) for half in w_in_halves])[0]
        if l + 1 < DEPTH:
            own = [] if l > 0 else [w_out_b[0]]
            pf, pb, h, *arrived = _prenorm_proj(
                xl, (g_pre, l, 0), scale, shift, w_new, cos, sin_signed,
                comm=("gather", own + [w_out_b[l + 1], w_in_b[l + 1, :D_MODEL // 2]]))
            w_out_l = arrived[0] if l == 0 else w_out_next
            w_out_next, top = arrived[-2], arrived[-1]
        else:
            pf, pb, h = _prenorm_proj(xl, (g_pre, l, 0), scale, shift, w_new, cos, sin_signed)
            w_out_l = w_out_next
        o_a, states = _gla_fwd(pf, pb, wgu_pad, b_gate_up, l)
        if l + 1 < DEPTH:
            o_b, lse, bottom = _dil_fwd(pf, pb, comm=("gather", [w_in_b[l + 1, D_MODEL // 2:]]))
            w_in_halves = (top, bottom)
        else:
            o_b, lse = _dil_fwd(pf, pb)
        if l + 1 < DEPTH:
            x_next, u = _post_fwd(o_a, o_b, pf, (g_heads, l, 0), w_out_l, xl, gate, (g_post, l, 0))
        else:
            dx, u, loss_part = _post_fwd(o_a, o_b, pf, (g_heads, l, 0), w_out_l, xl, gate, (g_post, l, 0),
                                         target=target)
        saved.append((xl, scale, gate, w_new, w_out_l, pf, pb, h, o_a, states, o_b, lse, u))
        xl = x_next

    small_rows = []
    gin_slots, gin_parts, gout_parts = None, [None] * DEPTH, [None] * DEPTH
    for l in reversed(range(DEPTH)):
        x_in, scale, gate, w_new, w_out_l, pf, pb, h, o_a, states, o_b, lse, u = saved[l]
        do, dz, sums_post, gout_slots = _post_bwd(dx, u, gate, (g_post, l, 0), w_out_l, o_a, o_b, pf, (g_heads, l, 0))
        dq_a, dk_a, dv_a, dlr2, dwgu, dbgu, arrived = _gla_bwd(pf, pb, wgu_pad, b_gate_up, l, states, do,
                                                               comm=("exchange", [gout_slots]))
        gout_parts[l] = arrived.reshape(N_DEV, OUT_SHARD, D_MODEL)
        if gin_slots is not None:
            dq_b, dk_b, dv_b, arrived, _, _ = _dil_bwd(pf, pb, cos, sin_signed, do, o_b, lse,
                                                       comm=("pairsum_exchange", [gin_slots]))
            gin_parts[l + 1] = arrived.reshape(N_DEV // 2, D_MODEL, W_IN_SHARD)
        else:
            dq_b, dk_b, dv_b = _dil_bwd(pf, pb, cos, sin_signed, do, o_b, lse)
        dlr = (dlr2[0] + dlr2[1]).astype(BF16)
        pieces = (dz, dq_a, dk_a, dq_b, dk_b, dv_a, dv_b, dlr)
        gin_slots = _grad_w_in(h, pieces).reshape(N_DEV * D_MODEL, W_IN_SHARD)
        if l == 0:
            dx, sums_in, arrived, _, _ = _in_bwd(pieces, w_new, x_in, dx, (g_pre, l, 0), scale,
                                                 comm=("pairsum_exchange", [gin_slots]))
            gin_parts[0] = arrived.reshape(N_DEV // 2, D_MODEL, W_IN_SHARD)
        else:
            dx, sums_in = _in_bwd(pieces, w_new, x_in, dx, (g_pre, l, 0), scale)
        dmod = jnp.concatenate([sums_in[0], sums_in[1], sums_post[0]])
        vecs = jnp.concatenate([sums_in[2], sums_post[1], sums_post[2], dbgu[0]])
        small_rows[0:0] = [_row(dmod, 4096), _row(vecs, 4096), _row(dwgu[:GLA_LOWRANK], 4096)]
    grad_x = dx[None]

    flat = lambda a, rows: a.reshape(rows, a.shape[-1])
    r_ada = DEPTH * D_MODEL
    g_w_in, d_w_in, nm_w_in, nv_w_in = _adamw_layers(w_in, gin_parts, m_w_in, v_w_in, "adamw_w_in", 256)
    g_w_out, d_w_out, nm_w_out, nv_w_out = _adamw_layers(w_out, gout_parts, m_w_out, v_w_out, "adamw_w_out", 128)

    small_rows += [_row(loss_part[0, 0:1], 4096), jnp.zeros((1, 4096), F32)]
    small = _all_gather(jnp.concatenate(small_rows, axis=0), "gather_small").reshape(N_DEV, 8, 4096)
    dmod_all = jnp.stack([small[:, 0, :3 * D_MODEL], small[:, 3, :3 * D_MODEL]])
    dmod_cols = lax.dynamic_slice_in_dim(dmod_all, me * ADA_SHARD, ADA_SHARD, axis=2)
    gwa = _w_ada_grad(c_all, dmod_cols).reshape(1, r_ada, ADA_SHARD)
    g_w_ada, d_w_ada, nm_w_ada, nv_w_ada = (
        t.reshape(w_ada.shape) for t in _adamw(flat(w_ada, r_ada), gwa, flat(m_w_ada, r_ada), flat(v_w_ada, r_ada),
                                               "adamw_w_ada", 256))

    where = ((0, 0), (1, 0), (1, 1024), (1, 2048), (1, 2560), (1, 3072))
    replicated = [(b_ada, m_b_ada, v_b_ada), (g_pre, m_g_pre, v_g_pre), (g_post, m_g_post, v_g_post),
                  (g_gla, m_g_gla, v_g_gla), (g_dil, m_g_dil, v_g_dil), (b_gate_up, m_b_gate_up, v_b_gate_up)]
    updated, loss = _adamw_replicated(small, replicated, where, loss_at=(6, 0))
    ((g_b_ada, d_b_ada, nm_b_ada, nv_b_ada), (g_g_pre, d_g_pre, nm_g_pre, nv_g_pre),
     (g_g_post, d_g_post, nm_g_post, nv_g_post), (g_g_gla, d_g_gla, nm_g_gla, nv_g_gla),
     (g_g_dil, d_g_dil, nm_g_dil, nv_g_dil), (g_b_gu, d_b_gu, nm_b_gu, nv_b_gu)) = updated
    gu_parts = jnp.stack([small[:, 2], small[:, 5]], axis=1).reshape(N_DEV, DEPTH, GLA_LOWRANK, GU_COLS)
    gu_parts = lax.dynamic_slice_in_dim(gu_parts, me * GU_SHARD, GU_SHARD, axis=3).reshape(
        N_DEV, DEPTH * GLA_LOWRANK, GU_SHARD)
    r_gu = DEPTH * GLA_LOWRANK
    g_w_gu, d_w_gu, nm_w_gu, nv_w_gu = (
        t.reshape(w_gate_up.shape) for t in _adamw(flat(w_gate_up, r_gu), gu_parts, flat(m_w_gate_up, r_gu),
                                                   flat(v_w_gate_up, r_gu), "adamw_w_gate_up", r_gu))
    return (loss, grad_x,
            g_w_ada, g_b_ada, g_g_pre, g_w_in, g_w_gu, g_b_gu, g_g_gla, g_g_dil, g_w_out, g_g_post,
            d_w_ada, d_b_ada, d_g_pre, d_w_in, d_w_gu, d_b_gu, d_g_gla, d_g_dil, d_w_out, d_g_post,
            nm_w_ada, nm_b_ada, nm_g_pre, nm_w_in, nm_w_gu, nm_b_gu, nm_g_gla, nm_g_dil, nm_w_out, nm_g_post,
            nv_w_ada, nv_b_ada, nv_g_pre, nv_w_in, nv_w_gu, nv_b_gu, nv_g_gla, nv_g_dil, nv_w_out, nv_g_post)


def _adamw_replicated(small, params, where, loss_at):
    n_parts = small.shape[0]

    def body(*refs):
        s_ref, p_refs, o_refs = refs[0], refs[1:1 + 3 * len(params)], refs[1 + 3 * len(params):]
        total = s_ref[0]
        for k in range(1, n_parts):
            total = total + s_ref[k]
        for i, (row, col) in enumerate(where):
            w_ref, m_ref, v_ref = p_refs[3 * i:3 * i + 3]
            n = w_ref.shape[1]
            g = jnp.concatenate([total[row + 3 * l:row + 3 * l + 1, col:col + n] for l in range(DEPTH)], axis=0)
            o_refs[4 * i][...] = g
            o_refs[4 * i + 1][...], o_refs[4 * i + 2][...], o_refs[4 * i + 3][...] = _adam_math(
                w_ref[...], g, m_ref[...], v_ref[...])
        o_refs[-1][...] = jnp.broadcast_to(total[loss_at[0]:loss_at[0] + 1, loss_at[1]:loss_at[1] + 1], (8, LANE))

    flat = [a for p in params for a in p]
    shapes = [jax.ShapeDtypeStruct(p[0].shape, F32) for p in params for _ in range(4)]
    outs = pl.pallas_call(body, name="adamw_replicated",
                          out_shape=shapes + [jax.ShapeDtypeStruct((8, LANE), F32)])(small, *flat)
    return [tuple(outs[4 * i:4 * i + 4]) for i in range(len(params))], outs[-1][0, 0]
```

```python
import functools
import math

import jax
import jax.numpy as jnp
from jax import lax
from jax.experimental import pallas as pl
from jax.experimental.pallas import tpu as pltpu

F32 = jnp.float32
BF16 = jnp.bfloat16

N_DEV = 8
D_MODEL = 1024
DEPTH = 2
GLA_HEADS = 4
GLA_DK = 64
GLA_DV = 128
GLA_CHUNK = 64
GLA_TAU = 16.0
GLA_LOWRANK = 16
DIL_HEADS = 4
DIL_HD = 128
DIL_BLOCK = 128
DIL_DILATIONS = (1, 4, 16)
ROPE_THETA = 10000.0
EPS = 1e-6
IN_COLS = 3600
W_IN_SHARD = IN_COLS // N_DEV
ADA_SHARD = 3 * D_MODEL // N_DEV
OUT_SHARD = D_MODEL // N_DEV
GU_COLS = GLA_HEADS * GLA_DK
GU_SHARD = GU_COLS // N_DEV

ADAM_LR = 0.001
ADAM_B1 = 0.9
ADAM_B2 = 0.999
ADAM_EPS = 1e-08
ADAM_WD = 0.01
ADAM_STEP = 10

NP = 3712
COL_Z, COL_QA, COL_KA, COL_QB, COL_KB, COL_VA, COL_VB, COL_LR = 0, 1024, 1280, 1536, 2048, 2560, 3072, 3584
NP_F32 = COL_VA
NP_BF16 = NP - NP_F32
LANE = 128
MASK_VALUE = -1e30

MESH = pl.DeviceIdType.MESH
ANY = pl.BlockSpec(memory_space=pl.ANY)


def _params(sem=None, vmem_mb=None):
    kw = {}
    if sem is not None:
        kw["dimension_semantics"] = sem
    if vmem_mb is not None:
        kw["vmem_limit_bytes"] = vmem_mb * 1024 * 1024
    return pltpu.CompilerParams(**kw)


def _dot(a, b):
    return jnp.dot(a, b, preferred_element_type=F32)


def _dot_nt(a, b):
    return lax.dot_general(a, b, (((1,), (1,)), ((), ())), preferred_element_type=F32)


def _dot_tn(a, b):
    return lax.dot_general(a, b, (((0,), (0,)), ((), ())), preferred_element_type=F32)


def _sigmoid(z):
    return 1.0 / (1.0 + jnp.exp(-z))


def _log_sigmoid(z):
    return jnp.minimum(z, 0.0) - jnp.log(1.0 + jnp.exp(-jnp.abs(z)))


def _rowvec(v, width=D_MODEL):
    arr, row, cb = v
    return arr.reshape(arr.shape[0], 1, arr.shape[1]), pl.BlockSpec((None, 1, width), lambda *_: (row, 0, cb))


def _my_position():
    return lax.axis_index("x"), lax.axis_index("y"), lax.axis_index("c")


def _linear(px, py, pc):
    return 4 * px + 2 * py + pc


def _gather_phase(phase, x_ref, out_ref, send_sem, recv_sem, local_sem):
    m = x_ref.shape[0]
    x, y, c = _my_position()
    me, sibling = (x, y, c), (x, y, 1 - c)
    chips = [(1 - x, y), (x, 1 - y), (1 - x, 1 - y)]

    def rows(px, py, pc):
        return out_ref.at[pl.ds(_linear(px, py, pc) * m, m), :]

    def copy(k, block, to, src=None):
        return pltpu.make_async_remote_copy(
            src_ref=rows(*block) if src is None else src, dst_ref=rows(*block),
            send_sem=send_sem(k), recv_sem=recv_sem(k), device_id=to, device_id_type=MESH)

    mine = pltpu.make_async_copy(x_ref, rows(*me), local_sem)
    first = [copy(0, me, sibling, src=x_ref)] + [copy(1 + j, me, (*chip, c), src=x_ref) for j, chip in enumerate(chips)]
    passed = [copy(4 + j, (*chip, c), sibling) for j, chip in enumerate(chips)]
    if phase == "start":
        mine.start()
        for cp in first:
            cp.start()
    elif phase == "forward":
        for j, chip in enumerate(chips):
            copy(1 + j, (*chip, c), me).wait_recv()
            passed[j].start()
    else:
        copy(0, sibling, me).wait_recv()
        for j, chip in enumerate(chips):
            copy(4 + j, (*chip, 1 - c), me).wait_recv()
        for cp in first + passed:
            cp.wait_send()
        mine.wait()


def _exchange_phase(phase, x_ref, out_ref, send_sem, recv_sem, local_sem):
    m = x_ref.shape[0] // N_DEV
    x, y, c = _my_position()
    me = _linear(x, y, c)

    def rows(ref, idx):
        return ref.at[pl.ds(idx * m, m), :]

    peers = [(1 - x if j & 4 else x, 1 - y if j & 2 else y, 1 - c if j & 1 else c) for j in range(1, N_DEV)]
    local = pltpu.make_async_copy(rows(x_ref, me), rows(out_ref, me), local_sem)
    sends = [pltpu.make_async_remote_copy(
        src_ref=rows(x_ref, _linear(*peer)), dst_ref=rows(out_ref, me),
        send_sem=send_sem(j), recv_sem=recv_sem(j), device_id=peer, device_id_type=MESH) for j, peer in enumerate(peers)]
    if phase == "start":
        local.start()
        for cp in sends:
            cp.start()
    else:
        for j, peer in enumerate(peers):
            pltpu.make_async_remote_copy(
                src_ref=rows(x_ref, _linear(*peer)), dst_ref=rows(out_ref, _linear(*peer)),
                send_sem=send_sem(j), recv_sem=recv_sem(j), device_id=peer, device_id_type=MESH).wait_recv()
        for cp in sends:
            cp.wait_send()
        local.wait()


def _pairsum_exchange_phase(phase, x_ref, out_refs, send_sem, recv_sem, local_sem):
    out_ref, stage_ref, pair_ref = out_refs
    m, n = x_ref.shape[0] // N_DEV, x_ref.shape[1]
    x, y, c = _my_position()
    mine = 2 * x + y
    chips = [(qx, qy) for qx in range(2) for qy in range(2)]
    others = [(1 - x, y), (x, 1 - y), (1 - x, 1 - y)]

    def rows(ref, idx):
        return ref.at[pl.ds(idx * m, m), :]

    def remote(src, dst, k, to):
        return pltpu.make_async_remote_copy(src_ref=src, dst_ref=dst, send_sem=send_sem(k), recv_sem=recv_sem(k),
                                            device_id=to, device_id_type=MESH)

    to_sibling = [remote(rows(x_ref, _linear(qx, qy, 1 - c)), rows(stage_ref, q), q, (x, y, 1 - c))
                  for q, (qx, qy) in enumerate(chips)]
    to_chips = [remote(rows(pair_ref, 2 * qx + qy), rows(out_ref, mine), 4 + j, (qx, qy, c))
                for j, (qx, qy) in enumerate(others)]
    keep = pltpu.make_async_copy(rows(pair_ref, mine), rows(out_ref, mine), local_sem)
    if phase == "start":
        for cp in to_sibling:
            cp.start()
    elif phase == "reduce":
        for cp in to_sibling:
            cp.wait_recv()

        def through_vmem(a_buf, b_buf, sems):
            tr = 128
            loads = [(pltpu.make_async_copy(rows(x_ref, _linear(qx, qy, c)), a_buf.at[q % 2], sems.at[q % 2]),
                      pltpu.make_async_copy(rows(stage_ref, q), b_buf.at[q % 2], sems.at[2 + q % 2]))
                     for q, (qx, qy) in enumerate(chips)]
            stores = [pltpu.make_async_copy(a_buf.at[q % 2], rows(pair_ref, q), sems.at[4 + q % 2]) for q in range(4)]
            for cp in loads[0]:
                cp.start()
            for q in range(4):
                for cp in loads[q]:
                    cp.wait()
                if q + 1 < 4:
                    if q >= 1:
                        stores[q - 1].wait()
                    for cp in loads[q + 1]:
                        cp.start()

                def add(r, carry, q=q):
                    tile = pl.ds(pl.multiple_of(r * tr, tr), tr)
                    a_buf[q % 2, tile, :] = (a_buf[q % 2, tile, :].astype(F32)
                                             + b_buf[q % 2, tile, :].astype(F32)).astype(x_ref.dtype)
                    return carry

                lax.fori_loop(0, m // tr, add, 0)
                stores[q].start()
            stores[2].wait()
            stores[3].wait()

        pl.run_scoped(through_vmem, pltpu.VMEM((2, m, n), x_ref.dtype), pltpu.VMEM((2, m, n), x_ref.dtype),
                      pltpu.SemaphoreType.DMA((6,)))
    elif phase == "send":
        keep.start()
        for cp in to_chips:
            cp.start()
    else:
        for j, (qx, qy) in enumerate(others):
            remote(rows(pair_ref, mine), rows(out_ref, 2 * qx + qy), 4 + j, (qx, qy, c)).wait_recv()
        for cp in to_sibling + to_chips:
            cp.wait_send()
        keep.wait()


_COMM_PHASES = {"gather": (_gather_phase, ("start", "forward", "finish")),
                "exchange": (_exchange_phase, ("start", "finish")),
                "pairsum_exchange": (_pairsum_exchange_phase, ("start", "reduce", "send", "finish"))}


def _comm_scratch(n_arrays):
    return [pltpu.SemaphoreType.DMA((n_arrays, 7)), pltpu.SemaphoreType.DMA((n_arrays, 7)),
            pltpu.SemaphoreType.DMA((n_arrays,))]


def _comm_run(kind, phases, x_refs, out_refs, send_sems, recv_sems, local_sems):
    fn = _COMM_PHASES[kind][0]
    per = len(out_refs) // len(x_refs)
    for phase in phases:
        for a, x_ref in enumerate(x_refs):
            outs = out_refs[a] if per == 1 else tuple(out_refs[per * a:per * (a + 1)])
            fn(phase, x_ref, outs, lambda k, a=a: send_sems.at[a, k], lambda k, a=a: recv_sems.at[a, k],
               local_sems.at[a])


def _comm_out_shapes(kind, arrays):
    if kind == "pairsum_exchange":
        return [jax.ShapeDtypeStruct((a.shape[0] // 2, a.shape[1]), a.dtype) for a in arrays for _ in range(3)]
    return [jax.ShapeDtypeStruct((N_DEV * a.shape[0], a.shape[1]) if kind == "gather" else a.shape, a.dtype)
            for a in arrays]


def _comm_call(kind, arrays, name):
    n = len(arrays)
    shapes = _comm_out_shapes(kind, arrays)

    def body(*refs):
        _comm_run(kind, _COMM_PHASES[kind][1], refs[:n], refs[n:n + len(shapes)], *refs[n + len(shapes):])

    return pl.pallas_call(body, name=name, out_shape=shapes, in_specs=[ANY] * n, out_specs=[ANY] * len(shapes),
                          scratch_shapes=_comm_scratch(n))(*arrays)


def _all_gather(xs, name):
    return _comm_call("gather", [xs], name)[0]


def _mod_fwd(c_all, w_ada):
    def body(c_ref, w_ref, o_ref):
        cv = c_ref[...]
        sc = cv * _sigmoid(cv)
        o_ref[0] = _dot(sc.astype(BF16), w_ref[0].astype(BF16))

    return pl.pallas_call(
        body, name="mod_fwd", grid=(DEPTH,),
        out_shape=jax.ShapeDtypeStruct((DEPTH, N_DEV, ADA_SHARD), F32),
        in_specs=[pl.BlockSpec((N_DEV, D_MODEL), lambda l: (0, 0)),
                  pl.BlockSpec((1, D_MODEL, ADA_SHARD), lambda l: (l, 0, 0))],
        out_specs=pl.BlockSpec((1, N_DEV, ADA_SHARD), lambda l: (l, 0, 0)),
        compiler_params=_params(("arbitrary",)),
    )(c_all, w_ada)


def _w_ada_grad(c_all, dmod_cols):
    def body(c_ref, d_ref, o_ref):
        cv = c_ref[...]
        sc = cv * _sigmoid(cv)
        o_ref[0] = lax.dot_general(sc, d_ref[0], (((0,), (0,)), ((), ())), precision=lax.Precision.HIGHEST,
                                   preferred_element_type=F32)

    return pl.pallas_call(
        body, name="w_ada_grad", grid=(DEPTH,),
        out_shape=jax.ShapeDtypeStruct((DEPTH, D_MODEL, ADA_SHARD), F32),
        in_specs=[pl.BlockSpec((N_DEV, D_MODEL), lambda l: (0, 0)),
                  pl.BlockSpec((1, N_DEV, ADA_SHARD), lambda l: (l, 0, 0))],
        out_specs=pl.BlockSpec((1, D_MODEL, ADA_SHARD), lambda l: (l, 0, 0)),
        compiler_params=_params(("arbitrary",)),
    )(c_all, dmod_cols)


def _comm_plumbing(comm):
    if not comm:
        return 0, [], []
    return len(comm[1]), _comm_out_shapes(*comm), _comm_scratch(len(comm[1]))


def _split_refs(refs, n_in, n_out, n_scratch, comm):
    ci, shapes, _ = _comm_plumbing(comm)
    co = len(shapes)
    a, b, c = n_in + ci, n_in + ci + n_out, n_in + ci + n_out + co
    return refs[:n_in], refs[a:b], refs[c:c + n_scratch], refs[n_in:a], refs[b:c], refs[c + n_scratch:]


def _prenorm_proj(x, g_pre, scale, shift, w_new, cos, sin_signed, comm=None, ts=256):
    s_len = x.shape[0]
    n_cin, c_shapes, c_scratch = _comm_plumbing(comm)

    def body(*refs):
        (x_ref, g_ref, sc_ref, sh_ref, w_ref, cos_ref, sin_ref), (pf_ref, pb_ref, h_ref), _, cin, cout, csem = (
            _split_refs(refs, 7, 3, 0, comm))
        comm_before, comm_after = _comm_hooks(comm, cin, cout, csem, steps=s_len // ts)
        comm_before()
        xv = x_ref[...]
        rstd = lax.rsqrt(jnp.mean(xv * xv, axis=-1, keepdims=True) + EPS)
        h = (xv * rstd * g_ref[...]) * (1.0 + sc_ref[...]) + sh_ref[...]
        hb = h.astype(BF16)
        h_ref[...] = hb
        for j in range(0, NP, 512):
            w = min(512, NP - j)
            acc = _dot(hb, w_ref[:, j:j + w])
            if COL_QB <= j < COL_VA:
                for lo in range(0, w, DIL_HD):
                    pf_ref[:, j + lo:j + lo + DIL_HD] = _rope(acc[:, lo:lo + DIL_HD], cos_ref[...], sin_ref[...])
            elif j < NP_F32:
                pf_ref[:, j:j + w] = acc
            else:
                pb_ref[:, j - NP_F32:j - NP_F32 + w] = acc.astype(BF16)
        comm_after()

    (g_pre, g_spec), (scale, sc_spec), (shift, sh_spec) = _rowvec(g_pre), _rowvec(scale), _rowvec(shift)
    return pl.pallas_call(
        body, name="prenorm_proj_comm" if comm else "prenorm_proj", grid=(s_len // ts,),
        out_shape=[jax.ShapeDtypeStruct((s_len, NP_F32), F32), jax.ShapeDtypeStruct((s_len, NP_BF16), BF16),
                   jax.ShapeDtypeStruct((s_len, D_MODEL), BF16)] + c_shapes,
        in_specs=[pl.BlockSpec((ts, D_MODEL), lambda i: (i, 0)), g_spec, sc_spec, sh_spec,
                  pl.BlockSpec((D_MODEL, NP), lambda i: (0, 0)), pl.BlockSpec((ts, DIL_HD), lambda i: (i, 0)),
                  pl.BlockSpec((ts, DIL_HD), lambda i: (i, 0))] + [ANY] * n_cin,
        out_specs=[pl.BlockSpec((ts, NP_F32), lambda i: (i, 0)), pl.BlockSpec((ts, NP_BF16), lambda i: (i, 0)),
                   pl.BlockSpec((ts, D_MODEL), lambda i: (i, 0))] + [ANY] * len(c_shapes),
        scratch_shapes=c_scratch,
        compiler_params=_params(("arbitrary",), 48),
    )(x, g_pre, scale, shift, w_new, cos, sin_signed, *(comm[1] if comm else []))


GLA_GROUP = 16


def _gla_group_rows(t):
    return [pl.ds(pl.multiple_of((t * GLA_GROUP + j) * GLA_CHUNK, GLA_CHUNK), GLA_CHUNK) for j in range(GLA_GROUP)]


def _gla_chunks_common(q_ref, k_ref, lr_ref, wgu_ref, bgu_ref, rows_list):
    c = GLA_CHUNK
    ri = lax.broadcasted_iota(jnp.int32, (c, c), 0)
    ci = lax.broadcasted_iota(jnp.int32, (c, c), 1)
    tril = (ri >= ci).astype(F32)
    zs = [_dot(lr_ref[rows, :], wgu_ref[...]) + bgu_ref[...] for rows in rows_list]
    las = [_log_sigmoid(z) * (1.0 / GLA_TAU) for z in zs]
    bs = [jnp.dot(tril, la, precision=lax.Precision.HIGHEST, preferred_element_type=F32) for la in las]
    out = []
    for rows, z, b in zip(rows_list, zs, bs):
        q = q_ref[rows, :] * (GLA_DK ** -0.5)
        k = k_ref[rows, :]
        bl = b[c - 1:c, :]
        out.append(dict(z=z, b=b, bl=bl, qe=q * jnp.exp(b), ke=k * jnp.exp(-b), kend=k * jnp.exp(bl - b),
                        dec=jnp.exp(bl)))
    return out, ri, ci


def _head_lane_mask(hh):
    return (lax.broadcasted_iota(jnp.int32, (1, LANE), 1) // GLA_DK) == hh


def _state_block_mask():
    r = lax.broadcasted_iota(jnp.int32, (2 * GLA_DV, LANE), 0) // GLA_DV
    cc = lax.broadcasted_iota(jnp.int32, (2 * GLA_DV, LANE), 1) // GLA_DK
    return r == cc


def _gla_fwd(pf, pb, wgu, bgu, layer, comm=None):
    s_len = pf.shape[0]
    nc = s_len // GLA_CHUNK
    ncomm = len(comm[1]) if comm else 0

    def body(*refs):
        q_ref, k_ref, v_ref, lr_ref, wgu_ref, bgu_ref = refs[:6]
        cin, (o_ref, st_ref), cout = refs[6:6 + ncomm], refs[6 + ncomm:8 + ncomm], refs[8 + ncomm:8 + 2 * ncomm]
        qe_s, cs_s, dec_s = refs[8 + 2 * ncomm:11 + 2 * ncomm]
        comm_before, comm_after = _comm_hooks(comm, cin, cout, refs[11 + 2 * ncomm:], steps=2)
        comm_before()
        bd = _state_block_mask()

        def local(t, carry):
            rows_list = _gla_group_rows(t)
            cm, ri, ci = _gla_chunks_common(q_ref, k_ref, lr_ref, wgu_ref, bgu_ref, rows_list)
            vs = [v_ref[rows, :] for rows in rows_list]
            kebs = [c["ke"].astype(BF16) for c in cm]
            a = [[jnp.where(ri >= ci, _dot_nt(jnp.where(_head_lane_mask(hh), c["qe"], 0.0).astype(BF16), keb), 0.0)
                  .astype(BF16) for hh in range(2)] for c, keb in zip(cm, kebs)]
            oi = [[_dot(ah[hh], v[:, hh * GLA_DV:(hh + 1) * GLA_DV]) for hh in range(2)] for ah, v in zip(a, vs)]
            cs = [jnp.where(bd, _dot_tn(v, c["kend"].astype(BF16)), 0.0) for c, v in zip(cm, vs)]
            for j, (rows, c) in enumerate(zip(rows_list, cm)):
                n = t * GLA_GROUP + j
                o_ref[rows, :] = jnp.concatenate(oi[j], axis=1)
                qe_s[rows, :] = c["qe"].astype(BF16)
                cs_s[n] = cs[j]
                dec_s[n] = jnp.broadcast_to(c["dec"], (8, LANE))
            return carry

        lax.fori_loop(0, nc // GLA_GROUP, local, 0)

        def scan(n, st):
            st_ref[0, n] = st.astype(BF16)
            return dec_s[n][0:1, :] * st + cs_s[n]

        lax.fori_loop(0, nc, scan, jnp.zeros((2 * GLA_DV, LANE), F32))

        def inter(t, carry):
            rows_list = _gla_group_rows(t)
            add = [_dot_nt(qe_s[rows, :], st_ref[0, t * GLA_GROUP + j]) for j, rows in enumerate(rows_list)]
            for rows, av in zip(rows_list, add):
                o_ref[rows, :] = o_ref[rows, :] + av
            return carry

        lax.fori_loop(0, nc // GLA_GROUP, inter, 0)
        comm_after()

    return pl.pallas_call(
        body, name="gla_fwd_comm" if comm else "gla_fwd", grid=(2,),
        out_shape=[jax.ShapeDtypeStruct((s_len, GLA_HEADS * GLA_DV), F32),
                   jax.ShapeDtypeStruct((2, nc, 2 * GLA_DV, LANE), BF16)] + (_comm_out_shapes(*comm) if comm else []),
        in_specs=[pl.BlockSpec((s_len, LANE), lambda g: (0, COL_QA // LANE + g)),
                  pl.BlockSpec((s_len, LANE), lambda g: (0, COL_KA // LANE + g)),
                  pl.BlockSpec((s_len, 2 * GLA_DV), lambda g: (0, (COL_VA - NP_F32) // (2 * GLA_DV) + g)),
                  pl.BlockSpec((s_len, LANE), lambda g: (0, (COL_LR - NP_F32) // LANE)),
                  pl.BlockSpec((None, LANE, LANE), lambda g: (layer, 0, g)),
                  pl.BlockSpec((None, 1, LANE), lambda g: (layer, 0, g))] + [ANY] * ncomm,
        out_specs=[pl.BlockSpec((s_len, 2 * GLA_DV), lambda g: (0, g)),
                   pl.BlockSpec((1, nc, 2 * GLA_DV, LANE), lambda g: (g, 0, 0, 0))] + [ANY] * ncomm,
        scratch_shapes=[pltpu.VMEM((s_len, LANE), BF16), pltpu.VMEM((nc, 2 * GLA_DV, LANE), F32),
                        pltpu.VMEM((nc, 8, LANE), F32)] + (_comm_scratch(ncomm) if comm else []),
        compiler_params=_params(("arbitrary",), 56),
    )(pf, pf, pb, pb, wgu, bgu.reshape(bgu.shape[0], 1, GU_COLS), *(comm[1] if comm else []))


def _rope_tables(s_len):
    inv_freq = ROPE_THETA ** (-jnp.arange(0, DIL_HD, 2, dtype=F32) / DIL_HD)
    ang = jnp.arange(s_len, dtype=F32)[:, None] * inv_freq[None, :]
    cos, sin = jnp.cos(ang), jnp.sin(ang)
    return jnp.concatenate([cos, cos], axis=1), jnp.concatenate([-sin, sin], axis=1)


def _rope(xv, cos, sin_signed):
    return xv * cos + pltpu.roll(xv, DIL_HD // 2, 1) * sin_signed


DIL_GROUP = 8


def _dil_pair_block(i, half, d, nblk, group=DIL_GROUP):
    nb = nblk // d
    j = i + half * (nblk // group)
    if nb >= 2 * group:
        r, n = j % d, j // d
    else:
        r, n = j // nb, j % nb
    kb = jnp.maximum(n - 1, 0)
    qs = r + d * DIL_BLOCK * n
    ks = r + d * DIL_BLOCK * kb
    return qs, ks, jnp.minimum(n, 1)


def _dil_fill_bias(bias):
    qi = lax.broadcasted_iota(jnp.int32, (DIL_BLOCK, 2 * DIL_BLOCK), 0)
    kj = lax.broadcasted_iota(jnp.int32, (DIL_BLOCK, 2 * DIL_BLOCK), 1)
    for sel in range(2):
        dist = qi - kj + DIL_BLOCK * sel
        bias[sel] = jnp.where((dist >= 0) & (dist <= DIL_BLOCK), 0.0, MASK_VALUE)


def _strided(start, size, d):
    return pl.ds(start, size) if d == 1 else pl.ds(start, size, stride=d)


def _comm_hooks(comm, cin, cout, csem, steps=DIL_HEADS):
    def before():
        if comm:
            @pl.when(pl.program_id(0) == 0)
            def _():
                _comm_run(comm[0], ("start",), cin, cout, *csem)

            if comm[0] == "gather":
                @pl.when(pl.program_id(0) == (2 * steps) // 3)
                def _():
                    _comm_run(comm[0], ("forward",), cin, cout, *csem)

            if comm[0] == "pairsum_exchange":
                @pl.when(pl.program_id(0) == (1 if steps <= 4 else 2))
                def _():
                    _comm_run(comm[0], ("reduce", "send"), cin, cout, *csem)

    def after():
        if comm:
            @pl.when(pl.program_id(0) == steps - 1)
            def _():
                _comm_run(comm[0], ("finish",), cin, cout, *csem)

    return before, after


def _dil_fwd(pf, pb, comm=None):
    s_len = pf.shape[0]
    nblk = s_len // DIL_BLOCK
    prep_rows = 256
    scale = DIL_HD ** -0.5
    nc = len(comm[1]) if comm else 0

    def body(*refs):
        ((qf, kf, v_ref), (o_ref, lse_ref), (vf, o0, o1, o2, l0, l1, l2, bias), cin, cout, csem) = _split_refs(
            refs, 3, 2, 8, comm)
        comm_before, comm_after = _comm_hooks(comm, cin, cout, csem)
        comm_before()
        _dil_fill_bias(bias)

        def prep(t, carry):
            rows = pl.ds(pl.multiple_of(t * prep_rows, prep_rows), prep_rows)
            vf[rows, :] = v_ref[rows, :].astype(F32)
            return carry

        lax.fori_loop(0, s_len // prep_rows, prep, 0)
        for d, o_p, l_p in zip(DIL_DILATIONS, (o0, o1, o2), (l0, l1, l2)):
            if nblk // d == 2:
                units = DIL_GROUP // 2

                def whole(i, carry, d=d, o_p=o_p, l_p=l_p, units=units):
                    rows = [_strided(i + u * (d // units), 2 * DIL_BLOCK, d) for u in range(units)]
                    ld = [(qf[rw, :].astype(BF16), kf[rw, :].astype(BF16), vf[rw, :].astype(BF16)) for rw in rows]
                    both = bias[...].reshape(2 * DIL_BLOCK, 2 * DIL_BLOCK)
                    s = [_dot_nt(qb, kk) * scale + both for qb, kk, _ in ld]
                    m = [jnp.max(sv, axis=-1, keepdims=True) for sv in s]
                    p = [jnp.exp(sv - mv) for sv, mv in zip(s, m)]
                    den = [jnp.sum(pv, axis=-1, keepdims=True) for pv in p]
                    r = [_dot(pv.astype(BF16), vv) for pv, (_, _, vv) in zip(p, ld)]
                    for rv, dv, mv, rw in zip(r, den, m, rows):
                        o_p[rw, :] = rv / dv
                        l_p[rw, :] = jnp.broadcast_to(mv + jnp.log(dv), (2 * DIL_BLOCK, DIL_HD))
                    return carry

                lax.fori_loop(0, d // units, whole, 0)
                continue

            def pair(i, carry, d=d, o_p=o_p, l_p=l_p):
                idx = [_dil_pair_block(i, half, d, nblk, DIL_GROUP) for half in range(DIL_GROUP)]
                ld = [(qf[_strided(qs, DIL_BLOCK, d), :].astype(BF16),
                       kf[_strided(ks, 2 * DIL_BLOCK, d), :].astype(BF16),
                       vf[_strided(ks, 2 * DIL_BLOCK, d), :].astype(BF16)) for qs, ks, _ in idx]
                s = [_dot_nt(qb, kk) * scale + bias[sel] for (qb, kk, _), (_, _, sel) in zip(ld, idx)]
                m = [jnp.max(sv, axis=-1, keepdims=True) for sv in s]
                p = [jnp.exp(sv - mv) for sv, mv in zip(s, m)]
                den = [jnp.sum(pv, axis=-1, keepdims=True) for pv in p]
                r = [_dot(pv.astype(BF16), vv) for pv, (_, _, vv) in zip(p, ld)]
                for rv, dv, mv, (qs, _, _) in zip(r, den, m, idx):
                    o_p[_strided(qs, DIL_BLOCK, d), :] = rv / dv
                    l_p[_strided(qs, DIL_BLOCK, d), :] = jnp.broadcast_to(mv + jnp.log(dv), (DIL_BLOCK, DIL_HD))
                return carry

            lax.fori_loop(0, nblk // DIL_GROUP, pair, 0)

        def comb(t, carry):
            rows = pl.ds(pl.multiple_of(t * prep_rows, prep_rows), prep_rows)
            a0, a1, a2 = l0[rows, :], l1[rows, :], l2[rows, :]
            m = jnp.maximum(jnp.maximum(a0, a1), a2)
            e0, e1, e2 = jnp.exp(a0 - m), jnp.exp(a1 - m), jnp.exp(a2 - m)
            tot = e0 + e1 + e2
            o_ref[rows, :] = (e0 * o0[rows, :] + e1 * o1[rows, :] + e2 * o2[rows, :]) / tot
            lse_ref[rows, :] = m + jnp.log(tot)
            return carry

        lax.fori_loop(0, s_len // prep_rows, comb, 0)
        comm_after()

    head = lambda base: pl.BlockSpec((s_len, DIL_HD), lambda h: (0, base // DIL_HD + h))
    out = pl.BlockSpec((s_len, DIL_HD), lambda h: (0, h))
    shp = jax.ShapeDtypeStruct((s_len, DIL_HEADS * DIL_HD), F32)
    return pl.pallas_call(
        body, name="dil_fwd_comm" if comm else "dil_fwd", grid=(DIL_HEADS,),
        out_shape=[shp, shp] + (_comm_out_shapes(*comm) if comm else []),
        in_specs=[head(COL_QB), head(COL_KB), head(COL_VB - NP_F32)] + [ANY] * nc,
        out_specs=[out, out] + [ANY] * nc,
        scratch_shapes=[pltpu.VMEM((s_len, DIL_HD), F32) for _ in range(7)]
        + [pltpu.VMEM((2, DIL_BLOCK, 2 * DIL_BLOCK), F32)] + (_comm_scratch(nc) if comm else []),
        compiler_params=_params(("arbitrary",), 56),
    )(pf, pf, pb, *(comm[1] if comm else []))


def _silu_and_grad(z):
    sg = _sigmoid(z)
    return z * sg, sg * (1.0 + z * (1.0 - sg))


def _post_fwd(o_a, o_b, pf, g_heads, w_out, x, gate, g_post, target=None, ts=256):
    s_len = x.shape[0]
    half = GLA_HEADS * GLA_DV
    last = target is not None

    def body(*refs):
        oa_ref, ob_ref, z_ref, gh_ref, w_ref, x_ref, gate_ref, gp_ref = refs[:8]
        xo_ref, u_ref = refs[8 + last:10 + last]
        y_ref = refs[-1]
        for src, base in ((oa_ref, 0), (ob_ref, half)):
            for hh in range(4):
                lo = hh * LANE
                og = src[:, lo:lo + LANE]
                on = og * lax.rsqrt(jnp.mean(og * og, axis=-1, keepdims=True) + EPS)
                zg = z_ref[:, base + lo:base + lo + LANE].astype(F32)
                y_ref[:, base + lo:base + lo + LANE] = (on * gh_ref[:, base + lo:base + lo + LANE]
                                                        * (zg * _sigmoid(zg))).astype(BF16)
        u = _dot(y_ref[...], w_ref[...])
        u_ref[...] = u.astype(BF16)
        rstd = lax.rsqrt(jnp.mean(u * u, axis=-1, keepdims=True) + EPS)
        x_out = x_ref[...] + gate_ref[...] * (u * rstd * gp_ref[...])
        if last:
            t_ref, loss_ref = refs[8], refs[11]

            @pl.when(pl.program_id(0) == 0)
            def _():
                loss_ref[...] = jnp.zeros_like(loss_ref)

            e = x_out - t_ref[...]
            xo_ref[...] = e * (1.0 / D_MODEL)
            loss_ref[...] += 0.5 * jnp.sum(jnp.mean(e * e, axis=-1, keepdims=True))
        else:
            xo_ref[...] = x_out

    (g_heads, gh_spec), (gate, gate_spec), (g_post, gp_spec) = _rowvec(g_heads), _rowvec(gate), _rowvec(g_post)
    tile = pl.BlockSpec((ts, D_MODEL), lambda i: (i, 0))
    halft = pl.BlockSpec((ts, half), lambda i: (i, 0))
    return pl.pallas_call(
        body, name="post_fwd_loss" if last else "post_fwd", grid=(s_len // ts,),
        out_shape=[jax.ShapeDtypeStruct((s_len, D_MODEL), F32), jax.ShapeDtypeStruct((s_len, D_MODEL), BF16)]
        + ([jax.ShapeDtypeStruct((8, LANE), F32)] if last else []),
        in_specs=[halft, halft, tile, gh_spec, pl.BlockSpec((D_MODEL, D_MODEL), lambda i: (0, 0)), tile, gate_spec,
                  gp_spec] + ([tile] if last else []),
        out_specs=[tile, tile] + ([pl.BlockSpec((8, LANE), lambda i: (0, 0))] if last else []),
        scratch_shapes=[pltpu.VMEM((ts, D_MODEL), BF16)],
        compiler_params=_params(("arbitrary",), 40),
    )(o_a, o_b, pf, g_heads, w_out, x, gate, g_post, *([target] if last else []))


def _post_bwd(dxo, u, gate, g_post, w_out, o_a, o_b, pf, g_heads, ts=512):
    s_len = dxo.shape[0]
    half = GLA_HEADS * GLA_DV
    steps = s_len // ts

    def body(dx_ref, u_ref, gate_ref, gp_ref, w_ref, oa_ref, ob_ref, z_ref, gh_ref, do_ref, dz_ref, sums_ref, gw_ref,
             y_s, acc):
        @pl.when(pl.program_id(0) == 0)
        def _():
            sums_ref[...] = jnp.zeros_like(sums_ref)
            acc[...] = jnp.zeros_like(acc)

        dx = dx_ref[...]
        u = u_ref[...].astype(F32)
        rstd = lax.rsqrt(jnp.mean(u * u, axis=-1, keepdims=True) + EPS)
        un = u * rstd
        sums_ref[0:1, :] += jnp.sum(dx * (un * gp_ref[...]), axis=0, keepdims=True)
        drn = dx * gate_ref[...]
        sums_ref[1:2, :] += jnp.sum(drn * un, axis=0, keepdims=True)
        dun = drn * gp_ref[...]
        du = rstd * (dun - un * jnp.mean(dun * un, axis=-1, keepdims=True))
        dub = du.astype(BF16)
        dy = _dot_nt(dub, w_ref[...])
        for src, base in ((oa_ref, 0), (ob_ref, half)):
            for hh in range(4):
                lo = base + hh * LANE
                og = src[:, hh * LANE:(hh + 1) * LANE]
                rs = lax.rsqrt(jnp.mean(og * og, axis=-1, keepdims=True) + EPS)
                on = og * rs
                zg = z_ref[:, lo:lo + LANE].astype(F32)
                sz, dsz = _silu_and_grad(zg)
                gg = gh_ref[:, lo:lo + LANE]
                dyg = dy[:, lo:lo + LANE]
                y_s[:, lo:lo + LANE] = (on * gg * sz).astype(BF16)
                sums_ref[2:3, lo:lo + LANE] += jnp.sum(dyg * sz * on, axis=0, keepdims=True)
                dz_ref[:, lo:lo + LANE] = (dyg * on * gg * dsz).astype(BF16)
                don = dyg * gg * sz
                do_ref[:, lo:lo + LANE] = (rs * (don - on * jnp.mean(don * on, axis=-1, keepdims=True))).astype(BF16)
        acc[...] += _dot_tn(y_s[...], dub)

        @pl.when(pl.program_id(0) == steps - 1)
        def _():
            gw_ref[...] = acc[...].astype(BF16)

    (g_heads, gh_spec), (gate, gate_spec), (g_post, gp_spec) = _rowvec(g_heads), _rowvec(gate), _rowvec(g_post)
    tile = pl.BlockSpec((ts, D_MODEL), lambda i: (i, 0))
    halft = pl.BlockSpec((ts, half), lambda i: (i, 0))
    whole = pl.BlockSpec((D_MODEL, D_MODEL), lambda i: (0, 0))
    return pl.pallas_call(
        body, name="post_bwd", grid=(steps,),
        out_shape=(jax.ShapeDtypeStruct((s_len, D_MODEL), BF16), jax.ShapeDtypeStruct((s_len, D_MODEL), BF16),
                   jax.ShapeDtypeStruct((8, D_MODEL), F32), jax.ShapeDtypeStruct((D_MODEL, D_MODEL), BF16)),
        in_specs=[tile, tile, gate_spec, gp_spec, whole, halft, halft, tile, gh_spec],
        out_specs=(tile, tile, pl.BlockSpec((8, D_MODEL), lambda i: (0, 0)), whole),
        scratch_shapes=[pltpu.VMEM((ts, D_MODEL), BF16), pltpu.VMEM((D_MODEL, D_MODEL), F32)],
        compiler_params=_params(("arbitrary",), 48),
    )(dxo, u, gate, g_post, w_out, o_a, o_b, pf, g_heads)


def _gla_bwd(pf, pb, wgu, bgu, layer, states, do, comm=None):
    s_len = pf.shape[0]
    nc = s_len // GLA_CHUNK
    c = GLA_CHUNK
    n_cin, c_shapes, c_scratch = _comm_plumbing(comm)

    def body(*refs):
        ((q_ref, k_ref, v_ref, lr_ref, wgu_ref, bgu_ref, st_ref, do_ref),
         (dq_ref, dk_ref, dv_ref, dlr_ref, dwgu_ref, dbgu_ref), (ds_s, dec_s, dw_acc, db_acc),
         cin, cout, csem) = _split_refs(refs, 8, 6, 4, comm)
        comm_before, comm_after = _comm_hooks(comm, cin, cout, csem, steps=2)
        comm_before()
        dw_acc[...] = jnp.zeros_like(dw_acc)
        db_acc[...] = jnp.zeros_like(db_acc)
        bd = _state_block_mask()
        last_row = lax.broadcasted_iota(jnp.int32, (c, LANE), 0) == c - 1

        def local(t, carry):
            rows_list = _gla_group_rows(t)
            cm, _, _ = _gla_chunks_common(q_ref, k_ref, lr_ref, wgu_ref, bgu_ref, rows_list)
            loc = [jnp.where(bd, _dot_tn(do_ref[rows, :], cc["qe"].astype(BF16)), 0.0)
                   for rows, cc in zip(rows_list, cm)]
            for j, cc in enumerate(cm):
                ds_s[t * GLA_GROUP + j] = loc[j]
                dec_s[t * GLA_GROUP + j] = jnp.broadcast_to(cc["dec"], (8, LANE))
            return carry

        lax.fori_loop(0, nc // GLA_GROUP, local, 0)

        def scan(t, dst):
            n = nc - 1 - t
            loc = ds_s[n]
            ds_s[n] = dst
            return dec_s[n][0:1, :] * dst + loc

        lax.fori_loop(0, nc, scan, jnp.zeros((2 * GLA_DV, LANE), F32))

        def rest(t, carry):
            rows_list = _gla_group_rows(t)
            cm, ri, ci = _gla_chunks_common(q_ref, k_ref, lr_ref, wgu_ref, bgu_ref, rows_list)
            ns = [t * GLA_GROUP + j for j in range(GLA_GROUP)]
            vs = [v_ref[rows, :] for rows in rows_list]
            dobs = [do_ref[rows, :] for rows in rows_list]
            stbs = [st_ref[0, n] for n in ns]
            dsts = [ds_s[n] for n in ns]
            dstbs = [d.astype(BF16) for d in dsts]
            qebs = [cc["qe"].astype(BF16) for cc in cm]
            kebs = [cc["ke"].astype(BF16) for cc in cm]
            kendbs = [cc["kend"].astype(BF16) for cc in cm]
            hms = [_head_lane_mask(hh) for hh in range(2)]
            qehs = [[jnp.where(hm, cc["qe"], 0.0).astype(BF16) for hm in hms] for cc in cm]
            kehs = [[jnp.where(hm, cc["ke"], 0.0).astype(BF16) for hm in hms] for cc in cm]
            heads = lambda x: [x[:, hh * GLA_DV:(hh + 1) * GLA_DV] for hh in range(2)]
            vhs, dohs = [heads(v) for v in vs], [heads(d) for d in dobs]

            dqe0 = [_dot(dob, stb) for dob, stb in zip(dobs, stbs)]
            dkend = [_dot(v, dstb) for v, dstb in zip(vs, dstbs)]
            dv0 = [_dot_nt(kb, dstb) for kb, dstb in zip(kendbs, dstbs)]
            a_t = [[jnp.where(ci >= ri, _dot_nt(kehs[j][hh], qebs[j]), 0.0).astype(BF16) for hh in range(2)]
                   for j in range(GLA_GROUP)]
            da = [[jnp.where(ri >= ci, _dot_nt(dohs[j][hh], vhs[j][hh]), 0.0).astype(BF16) for hh in range(2)]
                  for j in range(GLA_GROUP)]
            da_t = [[jnp.where(ci >= ri, _dot_nt(vhs[j][hh], dohs[j][hh]), 0.0).astype(BF16) for hh in range(2)]
                    for j in range(GLA_GROUP)]
            dv1 = [[_dot(a_t[j][hh], dohs[j][hh]) for hh in range(2)] for j in range(GLA_GROUP)]
            dqe1 = [[_dot(da[j][hh], kebs[j]) for hh in range(2)] for j in range(GLA_GROUP)]
            dke1 = [[_dot(da_t[j][hh], qehs[j][hh]) for hh in range(2)] for j in range(GLA_GROUP)]

            dbs, dzs = [], []
            for j, (rows, cc) in enumerate(zip(rows_list, cm)):
                qe, ke, kend, b, bl = cc["qe"], cc["ke"], cc["kend"], cc["b"], cc["bl"]
                dqe = dqe0[j] + jnp.where(hms[0], dqe1[j][0], 0.0) + jnp.where(hms[1], dqe1[j][1], 0.0)
                dke = jnp.where(hms[0], dke1[j][0], 0.0) + jnp.where(hms[1], dke1[j][1], 0.0)
                dv_ref[rows, :] = (dv0[j] + jnp.concatenate(dv1[j], axis=1)).astype(BF16)
                dq_ref[rows, :] = (dqe * jnp.exp(b) * (GLA_DK ** -0.5)).astype(BF16)
                dk_ref[rows, :] = (dke * jnp.exp(-b) + dkend[j] * jnp.exp(bl - b)).astype(BF16)
                ddec = jnp.sum(dsts[j] * stbs[j].astype(F32), axis=0, keepdims=True)
                dbl = jnp.sum(dkend[j] * kend, axis=0, keepdims=True) + ddec * cc["dec"]
                dbs.append(dqe * qe - dke * ke - dkend[j] * kend + jnp.where(last_row, dbl, 0.0))
            triu = (ci >= ri).astype(F32)
            dlas = [jnp.dot(triu, db, precision=lax.Precision.HIGHEST, preferred_element_type=F32) for db in dbs]
            dzs = [dla * (1.0 / GLA_TAU) * _sigmoid(-cc["z"]) for dla, cc in zip(dlas, cm)]
            dzbs = [dz.astype(BF16) for dz in dzs]
            dlrs = [_dot_nt(dzb, wgu_ref[...]) for dzb in dzbs]
            dws = [_dot_tn(lr_ref[rows, :], dzb) for rows, dzb in zip(rows_list, dzbs)]
            for rows, dlr in zip(rows_list, dlrs):
                dlr_ref[0, rows, :] = dlr
            dw_acc[...] += functools.reduce(lambda x, y: x + y, dws)
            db_acc[0:1, :] += jnp.sum(functools.reduce(lambda x, y: x + y, dzs), axis=0, keepdims=True)
            return carry

        lax.fori_loop(0, nc // GLA_GROUP, rest, 0)
        dwgu_ref[...] = dw_acc[...]
        dbgu_ref[...] = db_acc[...]
        comm_after()

    pair = pl.BlockSpec((s_len, LANE), lambda g: (0, g))
    return pl.pallas_call(
        body, name="gla_bwd_comm" if comm else "gla_bwd", grid=(2,),
        out_shape=[jax.ShapeDtypeStruct((s_len, GU_COLS), BF16), jax.ShapeDtypeStruct((s_len, GU_COLS), BF16),
                   jax.ShapeDtypeStruct((s_len, GLA_HEADS * GLA_DV), BF16),
                   jax.ShapeDtypeStruct((2, s_len, LANE), F32),
                   jax.ShapeDtypeStruct((LANE, GU_COLS), F32), jax.ShapeDtypeStruct((8, GU_COLS), F32)] + c_shapes,
        in_specs=[pl.BlockSpec((s_len, LANE), lambda g: (0, COL_QA // LANE + g)),
                  pl.BlockSpec((s_len, LANE), lambda g: (0, COL_KA // LANE + g)),
                  pl.BlockSpec((s_len, 2 * GLA_DV), lambda g: (0, (COL_VA - NP_F32) // (2 * GLA_DV) + g)),
                  pl.BlockSpec((s_len, LANE), lambda g: (0, (COL_LR - NP_F32) // LANE)),
                  pl.BlockSpec((None, LANE, LANE), lambda g: (layer, 0, g)),
                  pl.BlockSpec((None, 1, LANE), lambda g: (layer, 0, g)),
                  pl.BlockSpec((1, nc, 2 * GLA_DV, LANE), lambda g: (g, 0, 0, 0)),
                  pl.BlockSpec((s_len, 2 * GLA_DV), lambda g: (0, g))] + [ANY] * n_cin,
        out_specs=[pair, pair, pl.BlockSpec((s_len, 2 * GLA_DV), lambda g: (0, g)),
                   pl.BlockSpec((1, s_len, LANE), lambda g: (g, 0, 0)),
                   pl.BlockSpec((LANE, LANE), lambda g: (0, g)), pl.BlockSpec((8, LANE), lambda g: (0, g))]
        + [ANY] * len(c_shapes),
        scratch_shapes=[pltpu.VMEM((nc, 2 * GLA_DV, LANE), F32), pltpu.VMEM((nc, 8, LANE), F32),
                        pltpu.VMEM((LANE, LANE), F32), pltpu.VMEM((8, LANE), F32)] + c_scratch,
        compiler_params=_params(("arbitrary",), 56),
    )(pf, pf, pb, pb, wgu, bgu.reshape(bgu.shape[0], 1, GU_COLS), states, do, *(comm[1] if comm else []))


def _dil_bwd(pf, pb, cos, sin_signed, do, o_b, lse, comm=None):
    s_len = pf.shape[0]
    nblk = s_len // DIL_BLOCK
    prep_rows = 256
    scale = DIL_HD ** -0.5
    nc = len(comm[1]) if comm else 0

    def body(*refs):
        ((q_ref, kf, v_ref, cos_ref, sin_ref, do_ref, o_ref, lse_ref), (dq_ref, dk_ref, dv_ref),
         (qf, vf, dof, dl, dqa, dka, dva, bias), cin, cout, csem) = _split_refs(refs, 8, 3, 8, comm)
        comm_before, comm_after = _comm_hooks(comm, cin, cout, csem)
        comm_before()
        _dil_fill_bias(bias)

        def prep(t, carry):
            rows = pl.ds(pl.multiple_of(t * prep_rows, prep_rows), prep_rows)
            qf[rows, :] = q_ref[rows, :] * scale
            vf[rows, :] = v_ref[rows, :].astype(F32)
            dov = do_ref[rows, :].astype(F32)
            dof[rows, :] = dov
            dl[rows, :] = jnp.broadcast_to(jnp.sum(dov * o_ref[rows, :], axis=-1, keepdims=True), (prep_rows, DIL_HD))
            zero = jnp.zeros((prep_rows, DIL_HD), F32)
            dqa[rows, :] = zero
            dka[rows, :] = zero
            dva[rows, :] = zero
            return carry

        lax.fori_loop(0, s_len // prep_rows, prep, 0)

        for d in DIL_DILATIONS:
            if nblk // d == 2:
                units = DIL_GROUP // 2

                def whole(i, carry, d=d, units=units):
                    rows = [_strided(i + u * (d // units), 2 * DIL_BLOCK, d) for u in range(units)]
                    ld = [(qf[rw, :].astype(BF16), kf[rw, :].astype(BF16), vf[rw, :].astype(BF16),
                           dof[rw, :].astype(BF16)) for rw in rows]
                    both = bias[...].reshape(2 * DIL_BLOCK, 2 * DIL_BLOCK)
                    s = [_dot_nt(qb, kk) + both for qb, kk, _, _ in ld]
                    dp = [_dot_nt(dob, vv) for _, _, vv, dob in ld]
                    p = [jnp.exp(sv - lse_ref[rw, :][:, 0:1]) for sv, rw in zip(s, rows)]
                    ds = [(pv * (dpv - dl[rw, :][:, 0:1])).astype(BF16) for pv, dpv, rw in zip(p, dp, rows)]
                    pb = [pv.astype(BF16) for pv in p]
                    gq = [_dot(dsv, kk) for dsv, (_, kk, _, _) in zip(ds, ld)]
                    gk = [_dot_tn(dsv, qb) for dsv, (qb, _, _, _) in zip(ds, ld)]
                    gv = [_dot_tn(pv, dob) for pv, (_, _, _, dob) in zip(pb, ld)]
                    for rw, a, b, c in zip(rows, gq, gk, gv):
                        dqa[rw, :] += a
                        dka[rw, :] += b
                        dva[rw, :] += c
                    return carry

                lax.fori_loop(0, d // units, whole, 0)
                continue

            def pair(i, carry, d=d):
                idx = [_dil_pair_block(i, half, d, nblk) for half in range(DIL_GROUP)]
                rows = [(_strided(qs, DIL_BLOCK, d), _strided(ks, 2 * DIL_BLOCK, d)) for qs, ks, _ in idx]
                ld = [(qf[qr, :].astype(BF16), kf[kr, :].astype(BF16), vf[kr, :].astype(BF16),
                       dof[qr, :].astype(BF16)) for qr, kr in rows]
                s = [_dot_nt(qb, kk) + bias[sel] for (qb, kk, _, _), (_, _, sel) in zip(ld, idx)]
                dp = [_dot_nt(dob, vv) for _, _, vv, dob in ld]
                p = [jnp.exp(sv - lse_ref[qr, :][:, 0:1]) for sv, (qr, _) in zip(s, rows)]
                ds = [(pv * (dpv - dl[qr, :][:, 0:1])).astype(BF16) for pv, dpv, (qr, _) in zip(p, dp, rows)]
                pb = [pv.astype(BF16) for pv in p]
                gq = [_dot(dsv, kk) for dsv, (_, kk, _, _) in zip(ds, ld)]
                gk = [_dot_tn(dsv, qb) for dsv, (qb, _, _, _) in zip(ds, ld)]
                gv = [_dot_tn(pv, dob) for pv, (_, _, _, dob) in zip(pb, ld)]
                for (qr, kr), a, b, c in zip(rows, gq, gk, gv):
                    dqa[qr, :] += a
                    dka[kr, :] += b
                    dva[kr, :] += c
                return carry

            lax.fori_loop(0, nblk // DIL_GROUP, pair, 0)

        def fin(t, carry):
            rows = pl.ds(pl.multiple_of(t * prep_rows, prep_rows), prep_rows)
            cs, sn = cos_ref[rows, :], sin_ref[rows, :]
            gq, gk = dqa[rows, :] * scale, dka[rows, :]
            dq_ref[rows, :] = (gq * cs - pltpu.roll(gq, DIL_HD // 2, 1) * sn).astype(BF16)
            dk_ref[rows, :] = (gk * cs - pltpu.roll(gk, DIL_HD // 2, 1) * sn).astype(BF16)
            dv_ref[rows, :] = dva[rows, :].astype(BF16)
            return carry

        lax.fori_loop(0, s_len // prep_rows, fin, 0)
        comm_after()

    head = lambda base: pl.BlockSpec((s_len, DIL_HD), lambda h: (0, base // DIL_HD + h))
    table = pl.BlockSpec((s_len, DIL_HD), lambda h: (0, 0))
    out = pl.BlockSpec((s_len, DIL_HD), lambda h: (0, h))
    shp = jax.ShapeDtypeStruct((s_len, DIL_HEADS * DIL_HD), BF16)
    return pl.pallas_call(
        body, name="dil_bwd_comm" if comm else "dil_bwd", grid=(DIL_HEADS,),
        out_shape=[shp, shp, shp] + (_comm_out_shapes(*comm) if comm else []),
        in_specs=[head(COL_QB), head(COL_KB), head(COL_VB - NP_F32), table, table,
                  pl.BlockSpec((s_len, DIL_HD), lambda h: (0, DIL_HEADS + h)), out, out] + [ANY] * nc,
        out_specs=[out, out, out] + [ANY] * len(_comm_plumbing(comm)[1]),
        scratch_shapes=[pltpu.VMEM((s_len, DIL_HD), F32) for _ in range(7)]
        + [pltpu.VMEM((2, DIL_BLOCK, 2 * DIL_BLOCK), F32)] + (_comm_scratch(nc) if comm else []),
        compiler_params=_params(("arbitrary",), 56),
    )(pf, pf, pb, cos, sin_signed, do, o_b, lse, *(comm[1] if comm else []))


_PIECES = ((COL_Z, 1024), (COL_QA, 256), (COL_KA, 256), (COL_QB, 512), (COL_KB, 512), (COL_VA, 512), (COL_VB, 512),
           (COL_LR, 128))


def _in_bwd(pieces, w_new, x, dxo, g_pre, scale, comm=None, ts=256):
    s_len = x.shape[0]
    nc = len(comm[1]) if comm else 0
    nco = len(_comm_out_shapes(*comm)) if comm else 0
    npc = len(_PIECES)

    def body(*refs):
        p_refs = refs[:npc]
        w_ref, x_ref, dxo_ref, g_ref, sc_ref = refs[npc:npc + 5]
        cin, (dx_ref, sums_ref), cout = (refs[npc + 5:npc + 5 + nc], refs[npc + 5 + nc:npc + 7 + nc],
                                         refs[npc + 7 + nc:npc + 7 + nc + nco])
        comm_before, comm_after = _comm_hooks(comm, cin, cout, refs[npc + 7 + nc + nco:], steps=s_len // ts)
        comm_before()

        @pl.when(pl.program_id(0) == 0)
        def _():
            sums_ref[...] = jnp.zeros_like(sums_ref)

        dh = jnp.zeros((ts, D_MODEL), F32)
        for p_ref, (col, width) in zip(p_refs, _PIECES):
            dh += _dot_nt(p_ref[...], w_ref[:, col:col + width])
        xv = x_ref[...]
        rstd = lax.rsqrt(jnp.mean(xv * xv, axis=-1, keepdims=True) + EPS)
        xn = xv * rstd
        sums_ref[0:1, :] += jnp.sum(dh, axis=0, keepdims=True)
        sums_ref[1:2, :] += jnp.sum(dh * (xn * g_ref[...]), axis=0, keepdims=True)
        dr = dh * (1.0 + sc_ref[...])
        sums_ref[2:3, :] += jnp.sum(dr * xn, axis=0, keepdims=True)
        dxn = dr * g_ref[...]
        dx_ref[...] = dxo_ref[...] + rstd * (dxn - xn * jnp.mean(dxn * xn, axis=-1, keepdims=True))
        comm_after()

    (g_pre, g_spec), (scale, sc_spec) = _rowvec(g_pre), _rowvec(scale)
    tile = pl.BlockSpec((ts, D_MODEL), lambda i: (i, 0))
    return pl.pallas_call(
        body, name="in_bwd_comm" if comm else "in_bwd", grid=(s_len // ts,),
        out_shape=[jax.ShapeDtypeStruct((s_len, D_MODEL), F32), jax.ShapeDtypeStruct((8, D_MODEL), F32)]
        + (_comm_out_shapes(*comm) if comm else []),
        in_specs=[pl.BlockSpec((ts, width), lambda i: (i, 0)) for _, width in _PIECES]
        + [pl.BlockSpec((D_MODEL, NP), lambda i: (0, 0)), tile, tile, g_spec, sc_spec] + [ANY] * nc,
        out_specs=[tile, pl.BlockSpec((8, D_MODEL), lambda i: (0, 0))] + [ANY] * nco,
        scratch_shapes=_comm_scratch(nc) if comm else [],
        compiler_params=_params(("arbitrary",), 56),
    )(*pieces, w_new, x, dxo, g_pre, scale, *(comm[1] if comm else []))


def _w_in_to_kernel(gathered, comm=None, tr=128):
    n_cin, c_shapes, c_scratch = _comm_plumbing(comm)
    n_parts = len(gathered)
    first = [sum(g.shape[1] for g in gathered[:p]) // tr for p in range(n_parts + 1)]

    def body(*refs):
        g_refs, (o_ref,), _, cin, cout, csem = _split_refs(refs, n_parts, 1, 0, comm)
        comm_before, comm_after = _comm_hooks(comm, cin, cout, csem, steps=D_MODEL // tr)
        comm_before()
        for p, g_ref in enumerate(g_refs):
            @pl.when((pl.program_id(0) >= first[p]) & (pl.program_id(0) < first[p + 1]))
            def _(g_ref=g_ref):
                cols = jnp.concatenate([g_ref[k].astype(F32) for k in range(N_DEV)], axis=1)
                pad = jnp.zeros((tr, LANE - GLA_LOWRANK), F32)
                o_ref[...] = jnp.concatenate(
                    [cols[:, 1024:1536], cols[:, 3088:3600], cols[:, 0:512], cols[:, 1552:2576], cols[:, 512:1024],
                     cols[:, 2576:3088], cols[:, 1536:1552], pad], axis=1).astype(BF16)
        comm_after()

    part = lambda p: pl.BlockSpec((N_DEV, tr, W_IN_SHARD),
                                  lambda i: (0, jnp.clip(i - first[p], 0, first[p + 1] - first[p] - 1), 0))
    return pl.pallas_call(
        body, name="w_in_to_kernel_comm" if comm else "w_in_to_kernel", grid=(D_MODEL // tr,),
        out_shape=[jax.ShapeDtypeStruct((D_MODEL, NP), BF16)] + c_shapes,
        in_specs=[part(p) for p in range(n_parts)] + [ANY] * n_cin,
        out_specs=[pl.BlockSpec((tr, NP), lambda i: (i, 0))] + [ANY] * len(c_shapes),
        scratch_shapes=c_scratch,
        compiler_params=_params(("arbitrary",)),
    )(*gathered, *(comm[1] if comm else []))


def _grad_w_in(h, pieces, ts=512, tr=128):
    s_len = h.shape[0]
    steps = s_len // ts

    def body(*refs):
        h_ref, p_refs = refs[0], refs[1:1 + len(_PIECES)]
        o_ref, acc = refs[1 + len(_PIECES):]

        @pl.when(pl.program_id(0) == 0)
        def _():
            acc[...] = jnp.zeros_like(acc)

        hv = h_ref[...]
        for p_ref, (col, width) in zip(p_refs, _PIECES):
            acc[:, col:col + width] += _dot_tn(hv, p_ref[...])

        @pl.when(pl.program_id(0) == steps - 1)
        def _():
            def rows_out(t, carry):
                rows = pl.ds(pl.multiple_of(t * tr, tr), tr)
                g = acc[rows, :]
                cols = jnp.concatenate(
                    [g[:, COL_QA:COL_QB], g[:, COL_VA:COL_VB], g[:, 0:512], g[:, COL_LR:COL_LR + GLA_LOWRANK],
                     g[:, COL_QB:COL_VA], g[:, COL_VB:COL_LR], g[:, 512:1024]], axis=1)
                for k in range(N_DEV):
                    o_ref[k, rows, :] = cols[:, W_IN_SHARD * k:W_IN_SHARD * (k + 1)].astype(BF16)
                return carry

            lax.fori_loop(0, D_MODEL // tr, rows_out, 0)

    return pl.pallas_call(
        body, name="grad_w_in", grid=(steps,),
        out_shape=jax.ShapeDtypeStruct((N_DEV, D_MODEL, W_IN_SHARD), BF16),
        in_specs=[pl.BlockSpec((ts, D_MODEL), lambda i: (i, 0))]
        + [pl.BlockSpec((ts, width), lambda i: (i, 0)) for _, width in _PIECES],
        out_specs=pl.BlockSpec((N_DEV, D_MODEL, W_IN_SHARD), lambda i: (0, 0, 0)),
        scratch_shapes=[pltpu.VMEM((D_MODEL, NP), F32)],
        compiler_params=_params(("arbitrary",), 56),
    )(h, *pieces)


def _adam_math(w, g, m, v):
    m = ADAM_B1 * m + (1.0 - ADAM_B1) * g
    v = ADAM_B2 * v + (1.0 - ADAM_B2) * (g * g)
    m_hat = m / (1.0 - ADAM_B1 ** ADAM_STEP)
    v_hat = v / (1.0 - ADAM_B2 ** ADAM_STEP)
    delta = -ADAM_LR * (m_hat / (jnp.sqrt(v_hat) + ADAM_EPS) + ADAM_WD * w)
    return delta, m, v


def _adamw(w, parts, m, v, name, tr):
    r, cdim = w.shape
    n_parts = parts.shape[0]

    def body(w_ref, p_ref, m_ref, v_ref, g_ref, d_ref, nm_ref, nv_ref):
        g = p_ref[0].astype(F32)
        for k in range(1, n_parts):
            g = g + p_ref[k].astype(F32)
        g_ref[...] = g
        d_ref[...], nm_ref[...], nv_ref[...] = _adam_math(w_ref[...], g, m_ref[...], v_ref[...])

    tile = pl.BlockSpec((tr, cdim), lambda i: (i, 0))
    shp = jax.ShapeDtypeStruct((r, cdim), F32)
    return pl.pallas_call(
        body, name=name, grid=(r // tr,), out_shape=(shp, shp, shp, shp),
        in_specs=[tile, pl.BlockSpec((n_parts, tr, cdim), lambda i: (0, i, 0)), tile, tile],
        out_specs=(tile, tile, tile, tile),
        compiler_params=_params(("arbitrary",), 40),
    )(w, parts, m, v)


def _adamw_layers(w, parts, m, v, name, tr):
    n_layers, r, cdim = w.shape

    def body(*refs):
        w_ref, p_refs, (m_ref, v_ref) = refs[0], refs[1:1 + n_layers], refs[1 + n_layers:3 + n_layers]
        g_ref, d_ref, nm_ref, nv_ref = refs[3 + n_layers:]
        for l, p_ref in enumerate(p_refs):
            @pl.when(pl.program_id(0) == l)
            def _(p_ref=p_ref):
                g = p_ref[0].astype(F32)
                for k in range(1, p_ref.shape[0]):
                    g = g + p_ref[k].astype(F32)
                g_ref[0] = g
                d_ref[0], nm_ref[0], nv_ref[0] = _adam_math(w_ref[0], g, m_ref[0], v_ref[0])

    tile = pl.BlockSpec((1, tr, cdim), lambda l, i: (l, i, 0))
    part = lambda own: pl.BlockSpec((parts[own].shape[0], tr, cdim), lambda l, i: (0, jnp.where(l == own, i, 0), 0))
    shp = jax.ShapeDtypeStruct(w.shape, F32)
    return pl.pallas_call(
        body, name=name, grid=(n_layers, r // tr), out_shape=(shp, shp, shp, shp),
        in_specs=[tile] + [part(l) for l in range(n_layers)] + [tile, tile],
        out_specs=(tile, tile, tile, tile),
        compiler_params=_params(("arbitrary", "arbitrary"), 40),
    )(w, *parts, m, v)


def _row(vec, width):
    vec = vec.reshape(1, -1)
    return jnp.pad(vec, ((0, 0), (0, width - vec.shape[1])))


def kernel(x, c, w_ada, b_ada, g_pre, w_in, w_gate_up, b_gate_up, g_gla, g_dil, w_out, g_post, loss_target, m_w_ada, m_b_ada, m_g_pre, m_w_in, m_w_gate_up, m_b_gate_up, m_g_gla, m_g_dil, m_w_out, m_g_post, v_w_ada, v_b_ada, v_g_pre, v_w_in, v_w_gate_up, v_b_gate_up, v_g_gla, v_g_dil, v_w_out, v_g_post):
    px, py, pc = _my_position()
    me = _linear(px, py, pc)
    xs = x[0]
    target = loss_target[0]
    s_len = xs.shape[0]
    assert s_len % (DIL_BLOCK * max(DIL_DILATIONS) * 2) == 0 and xs.shape[1] == D_MODEL

    w_in_b, w_out_b = w_in.astype(BF16), w_out.astype(BF16)
    c_rows, wgu_all, w_in_all = _comm_call(
        "gather", [jnp.pad(c, ((0, 7), (0, 0))), w_gate_up.reshape(DEPTH * GLA_LOWRANK, GU_SHARD), w_in_b[0]],
        "gather_first")
    c_all = c_rows.reshape(N_DEV, 8, D_MODEL)[:, 0]
    mod_part = _mod_fwd(c_all, w_ada)
    w_new, mod_all = _w_in_to_kernel([w_in_all.reshape(N_DEV, D_MODEL, W_IN_SHARD)],
                                     comm=("gather", [mod_part.reshape(DEPTH * N_DEV, ADA_SHARD)]))
    mod_all = mod_all.reshape(N_DEV, DEPTH, N_DEV, ADA_SHARD)
    mod_mine = lax.dynamic_index_in_dim(mod_all, me, axis=2, keepdims=False)
    mod = jnp.transpose(mod_mine, (1, 0, 2)).reshape(DEPTH, 3 * D_MODEL) + b_ada
    wgu_full = jnp.transpose(wgu_all.reshape(N_DEV, DEPTH, GLA_LOWRANK, GU_SHARD), (1, 2, 0, 3)).reshape(
        DEPTH, GLA_LOWRANK, GU_COLS)
    wgu_pad = jnp.pad(wgu_full, ((0, 0), (0, LANE - GLA_LOWRANK), (0, 0))).astype(BF16)

    cos, sin_signed = _rope_tables(s_len)
    g_heads = jnp.concatenate([g_gla, g_dil], axis=1)

    saved = []
    xl = xs
    for l in range(DEPTH):
        shift, scale, gate = ((mod, l, k) for k in range(3))
        if l > 0:
            w_new = _w_in_to_kernel([half.reshape(N_DEV, D_MODEL // 2, W_IN_SHARD) for half in w_in_halves])[0]
        if l + 1 < DEPTH:
            own = [] if l > 0 else [w_out_b[0]]
            pf, pb, h, *arrived = _prenorm_proj(
                xl, (g_pre, l, 0), scale, shift, w_new, cos, sin_signed,
                comm=("gather", own + [w_out_b[l + 1], w_in_b[l + 1, :D_MODEL // 2]]))
            w_out_l = arrived[0] if l == 0 else w_out_next
            w_out_next, top = arrived[-2], arrived[-1]
        else:
            pf, pb, h = _prenorm_proj(xl, (g_pre, l, 0), scale, shift, w_new, cos, sin_signed)
            w_out_l = w_out_next
        o_a, states = _gla_fwd(pf, pb, wgu_pad, b_gate_up, l)
        if l + 1 < DEPTH:
            o_b, lse, bottom = _dil_fwd(pf, pb, comm=("gather", [w_in_b[l + 1, D_MODEL // 2:]]))
            w_in_halves = (top, bottom)
        else:
            o_b, lse = _dil_fwd(pf, pb)
        if l + 1 < DEPTH:
            x_next, u = _post_fwd(o_a, o_b, pf, (g_heads, l, 0), w_out_l, xl, gate, (g_post, l, 0))
        else:
            dx, u, loss_part = _post_fwd(o_a, o_b, pf, (g_heads, l, 0), w_out_l, xl, gate, (g_post, l, 0),
                                         target=target)
        saved.append((xl, scale, gate, w_new, w_out_l, pf, pb, h, o_a, states, o_b, lse, u))
        xl = x_next

    small_rows = []
    gin_slots, gin_parts, gout_parts = None, [None] * DEPTH, [None] * DEPTH
    for l in reversed(range(DEPTH)):
        x_in, scale, gate, w_new, w_out_l, pf, pb, h, o_a, states, o_b, lse, u = saved[l]
        do, dz, sums_post, gout_slots = _post_bwd(dx, u, gate, (g_post, l, 0), w_out_l, o_a, o_b, pf, (g_heads, l, 0))
        dq_a, dk_a, dv_a, dlr2, dwgu, dbgu, arrived = _gla_bwd(pf, pb, wgu_pad, b_gate_up, l, states, do,
                                                               comm=("exchange", [gout_slots]))
        gout_parts[l] = arrived.reshape(N_DEV, OUT_SHARD, D_MODEL)
        if gin_slots is not None:
            dq_b, dk_b, dv_b, arrived, _, _ = _dil_bwd(pf, pb, cos, sin_signed, do, o_b, lse,
                                                       comm=("pairsum_exchange", [gin_slots]))
            gin_parts[l + 1] = arrived.reshape(N_DEV // 2, D_MODEL, W_IN_SHARD)
        else:
            dq_b, dk_b, dv_b = _dil_bwd(pf, pb, cos, sin_signed, do, o_b, lse)
        dlr = (dlr2[0] + dlr2[1]).astype(BF16)
        pieces = (dz, dq_a, dk_a, dq_b, dk_b, dv_a, dv_b, dlr)
        gin_slots = _grad_w_in(h, pieces).reshape(N_DEV * D_MODEL, W_IN_SHARD)
        if l == 0:
            dx, sums_in, arrived, _, _ = _in_bwd(pieces, w_new, x_in, dx, (g_pre, l, 0), scale,
                                                 comm=("pairsum_exchange", [gin_slots]))
            gin_parts[0] = arrived.reshape(N_DEV // 2, D_MODEL, W_IN_SHARD)
        else:
            dx, sums_in = _in_bwd(pieces, w_new, x_in, dx, (g_pre, l, 0), scale)
        dmod = jnp.concatenate([sums_in[0], sums_in[1], sums_post[0]])
        vecs = jnp.concatenate([sums_in[2], sums_post[1], sums_post[2], dbgu[0]])
        small_rows[0:0] = [_row(dmod, 4096), _row(vecs, 4096), _row(dwgu[:GLA_LOWRANK], 4096)]
    grad_x = dx[None]

    flat = lambda a, rows: a.reshape(rows, a.shape[-1])
    r_ada = DEPTH * D_MODEL
    g_w_in, d_w_in, nm_w_in, nv_w_in = _adamw_layers(w_in, gin_parts, m_w_in, v_w_in, "adamw_w_in", 256)
    g_w_out, d_w_out, nm_w_out, nv_w_out = _adamw_layers(w_out, gout_parts, m_w_out, v_w_out, "adamw_w_out", 128)

    small_rows += [_row(loss_part[0, 0:1], 4096), jnp.zeros((1, 4096), F32)]
    small = _all_gather(jnp.concatenate(small_rows, axis=0), "gather_small").reshape(N_DEV, 8, 4096)
    dmod_all = jnp.stack([small[:, 0, :3 * D_MODEL], small[:, 3, :3 * D_MODEL]])
    dmod_cols = lax.dynamic_slice_in_dim(dmod_all, me * ADA_SHARD, ADA_SHARD, axis=2)
    gwa = _w_ada_grad(c_all, dmod_cols).reshape(1, r_ada, ADA_SHARD)
    g_w_ada, d_w_ada, nm_w_ada, nv_w_ada = (
        t.reshape(w_ada.shape) for t in _adamw(flat(w_ada, r_ada), gwa, flat(m_w_ada, r_ada), flat(v_w_ada, r_ada),
                                               "adamw_w_ada", 256))

    where = ((0, 0), (1, 0), (1, 1024), (1, 2048), (1, 2560), (1, 3072))
    replicated = [(b_ada, m_b_ada, v_b_ada), (g_pre, m_g_pre, v_g_pre), (g_post, m_g_post, v_g_post),
                  (g_gla, m_g_gla, v_g_gla), (g_dil, m_g_dil, v_g_dil), (b_gate_up, m_b_gate_up, v_b_gate_up)]
    updated, loss = _adamw_replicated(small, replicated, where, loss_at=(6, 0))
    ((g_b_ada, d_b_ada, nm_b_ada, nv_b_ada), (g_g_pre, d_g_pre, nm_g_pre, nv_g_pre),
     (g_g_post, d_g_post, nm_g_post, nv_g_post), (g_g_gla, d_g_gla, nm_g_gla, nv_g_gla),
     (g_g_dil, d_g_dil, nm_g_dil, nv_g_dil), (g_b_gu, d_b_gu, nm_b_gu, nv_b_gu)) = updated
    gu_parts = jnp.stack([small[:, 2], small[:, 5]], axis=1).reshape(N_DEV, DEPTH, GLA_LOWRANK, GU_COLS)
    gu_parts = lax.dynamic_slice_in_dim(gu_parts, me * GU_SHARD, GU_SHARD, axis=3).reshape(
        N_DEV, DEPTH * GLA_LOWRANK, GU_SHARD)
    r_gu = DEPTH * GLA_LOWRANK
    g_w_gu, d_w_gu, nm_w_gu, nv_w_gu = (
        t.reshape(w_gate_up.shape) for t in _adamw(flat(w_gate_up, r_gu), gu_parts, flat(m_w_gate_up, r_gu),
                                                   flat(v_w_gate_up, r_gu), "adamw_w_gate_up", r_gu))
    return (loss, grad_x,
            g_w_ada, g_b_ada, g_g_pre, g_w_in, g_w_gu, g_b_gu, g_g_gla, g_g_dil, g_w_out, g_g_post,
            d_w_ada, d_b_ada, d_g_pre, d_w_in, d_w_gu, d_b_gu, d_g_gla, d_g_dil, d_w_out, d_g_post,
            nm_w_ada, nm_b_ada, nm_g_pre, nm_w_in, nm_w_gu, nm_b_gu, nm_g_gla, nm_g_dil, nm_w_out, nm_g_post,
            nv_w_ada, nv_b_ada, nv_g_pre, nv_w_in, nv_w_gu, nv_b_gu, nv_g_gla, nv_g_dil, nv_w_out, nv_g_post)


def _adamw_replicated(small, params, where, loss_at):
    n_parts = small.shape[0]

    def body(*refs):
        s_ref, p_refs, o_refs = refs[0], refs[1:1 + 3 * len(params)], refs[1 + 3 * len(params):]
        total = s_ref[0]
        for k in range(1, n_parts):
            total = total + s_ref[k]
        for i, (row, col) in enumerate(where):
            w_ref, m_ref, v_ref = p_refs[3 * i:3 * i + 3]
            n = w_ref.shape[1]
            g = jnp.concatenate([total[row + 3 * l:row + 3 * l + 1, col:col + n] for l in range(DEPTH)], axis=0)
            o_refs[4 * i][...] = g
            o_refs[4 * i + 1][...], o_refs[4 * i + 2][...], o_refs[4 * i + 3][...] = _adam_math(
                w_ref[...], g, m_ref[...], v_ref[...])
        o_refs[-1][...] = jnp.broadcast_to(total[loss_at[0]:loss_at[0] + 1, loss_at[1]:loss_at[1] + 1], (8, LANE))

    flat = [a for p in params for a in p]
    shapes = [jax.ShapeDtypeStruct(p[0].shape, F32) for p in params for _ in range(4)]
    outs = pl.pallas_call(body, name="adamw_replicated",
                          out_shape=shapes + [jax.ShapeDtypeStruct((8, LANE), F32)])(small, *flat)
    return [tuple(outs[4 * i:4 * i + 4]) for i in range(len(params))], outs[-1][0, 0]
```

```python
import functools
import math

import jax
import jax.numpy as jnp
from jax import lax
from jax.experimental import pallas as pl
from jax.experimental.pallas import tpu as pltpu

F32 = jnp.float32
BF16 = jnp.bfloat16

N_DEV = 8
D_MODEL = 1024
DEPTH = 2
GLA_HEADS = 4
GLA_DK = 64
GLA_DV = 128
GLA_CHUNK = 64
GLA_TAU = 16.0
GLA_LOWRANK = 16
DIL_HEADS = 4
DIL_HD = 128
DIL_BLOCK = 128
DIL_DILATIONS = (1, 4, 16)
ROPE_THETA = 10000.0
EPS = 1e-6
IN_COLS = 3600
W_IN_SHARD = IN_COLS // N_DEV
ADA_SHARD = 3 * D_MODEL // N_DEV
OUT_SHARD = D_MODEL // N_DEV
GU_COLS = GLA_HEADS * GLA_DK
GU_SHARD = GU_COLS // N_DEV

ADAM_LR = 0.001
ADAM_B1 = 0.9
ADAM_B2 = 0.999
ADAM_EPS = 1e-08
ADAM_WD = 0.01
ADAM_STEP = 10

NP = 3712
COL_Z, COL_QA, COL_KA, COL_QB, COL_KB, COL_VA, COL_VB, COL_LR = 0, 1024, 1280, 1536, 2048, 2560, 3072, 3584
NP_F32 = COL_VA
NP_BF16 = NP - NP_F32
LANE = 128
MASK_VALUE = -1e30

MESH = pl.DeviceIdType.MESH
ANY = pl.BlockSpec(memory_space=pl.ANY)


def _params(sem=None, vmem_mb=None):
    kw = {}
    if sem is not None:
        kw["dimension_semantics"] = sem
    if vmem_mb is not None:
        kw["vmem_limit_bytes"] = vmem_mb * 1024 * 1024
    return pltpu.CompilerParams(**kw)


def _dot(a, b):
    return jnp.dot(a, b, preferred_element_type=F32)


def _dot_nt(a, b):
    return lax.dot_general(a, b, (((1,), (1,)), ((), ())), preferred_element_type=F32)


def _dot_tn(a, b):
    return lax.dot_general(a, b, (((0,), (0,)), ((), ())), preferred_element_type=F32)


def _sigmoid(z):
    return 1.0 / (1.0 + jnp.exp(-z))


def _log_sigmoid(z):
    return jnp.minimum(z, 0.0) - jnp.log(1.0 + jnp.exp(-jnp.abs(z)))


def _rowvec(v, width=D_MODEL):
    arr, row, cb = v
    return arr.reshape(arr.shape[0], 1, arr.shape[1]), pl.BlockSpec((None, 1, width), lambda *_: (row, 0, cb))


def _my_position():
    return lax.axis_index("x"), lax.axis_index("y"), lax.axis_index("c")


def _linear(px, py, pc):
    return 4 * px + 2 * py + pc


def _gather_phase(phase, x_ref, out_ref, send_sem, recv_sem, local_sem):
    m = x_ref.shape[0]
    x, y, c = _my_position()
    me, sibling = (x, y, c), (x, y, 1 - c)
    chips = [(1 - x, y), (x, 1 - y), (1 - x, 1 - y)]

    def rows(px, py, pc):
        return out_ref.at[pl.ds(_linear(px, py, pc) * m, m), :]

    def copy(k, block, to, src=None):
        return pltpu.make_async_remote_copy(
            src_ref=rows(*block) if src is None else src, dst_ref=rows(*block),
            send_sem=send_sem(k), recv_sem=recv_sem(k), device_id=to, device_id_type=MESH)

    mine = pltpu.make_async_copy(x_ref, rows(*me), local_sem)
    first = [copy(0, me, sibling, src=x_ref)] + [copy(1 + j, me, (*chip, c), src=x_ref) for j, chip in enumerate(chips)]
    passed = [copy(4 + j, (*chip, c), sibling) for j, chip in enumerate(chips)]
    if phase == "start":
        mine.start()
        for cp in first:
            cp.start()
    elif phase == "forward":
        for j, chip in enumerate(chips):
            copy(1 + j, (*chip, c), me).wait_recv()
            passed[j].start()
    else:
        copy(0, sibling, me).wait_recv()
        for j, chip in enumerate(chips):
            copy(4 + j, (*chip, 1 - c), me).wait_recv()
        for cp in first + passed:
            cp.wait_send()
        mine.wait()


def _exchange_phase(phase, x_ref, out_ref, send_sem, recv_sem, local_sem):
    m = x_ref.shape[0] // N_DEV
    x, y, c = _my_position()
    me = _linear(x, y, c)

    def rows(ref, idx):
        return ref.at[pl.ds(idx * m, m), :]

    peers = [(1 - x if j & 4 else x, 1 - y if j & 2 else y, 1 - c if j & 1 else c) for j in range(1, N_DEV)]
    local = pltpu.make_async_copy(rows(x_ref, me), rows(out_ref, me), local_sem)
    sends = [pltpu.make_async_remote_copy(
        src_ref=rows(x_ref, _linear(*peer)), dst_ref=rows(out_ref, me),
        send_sem=send_sem(j), recv_sem=recv_sem(j), device_id=peer, device_id_type=MESH) for j, peer in enumerate(peers)]
    if phase == "start":
        local.start()
        for cp in sends:
            cp.start()
    else:
        for j, peer in enumerate(peers):
            pltpu.make_async_remote_copy(
                src_ref=rows(x_ref, _linear(*peer)), dst_ref=rows(out_ref, _linear(*peer)),
                send_sem=send_sem(j), recv_sem=recv_sem(j), device_id=peer, device_id_type=MESH).wait_recv()
        for cp in sends:
            cp.wait_send()
        local.wait()


def _pairsum_exchange_phase(phase, x_ref, out_refs, send_sem, recv_sem, local_sem):
    out_ref, stage_ref, pair_ref = out_refs
    m, n = x_ref.shape[0] // N_DEV, x_ref.shape[1]
    x, y, c = _my_position()
    mine = 2 * x + y
    chips = [(qx, qy) for qx in range(2) for qy in range(2)]
    others = [(1 - x, y), (x, 1 - y), (1 - x, 1 - y)]

    def rows(ref, idx):
        return ref.at[pl.ds(idx * m, m), :]

    def remote(src, dst, k, to):
        return pltpu.make_async_remote_copy(src_ref=src, dst_ref=dst, send_sem=send_sem(k), recv_sem=recv_sem(k),
                                            device_id=to, device_id_type=MESH)

    to_sibling = [remote(rows(x_ref, _linear(qx, qy, 1 - c)), rows(stage_ref, q), q, (x, y, 1 - c))
                  for q, (qx, qy) in enumerate(chips)]
    to_chips = [remote(rows(pair_ref, 2 * qx + qy), rows(out_ref, mine), 4 + j, (qx, qy, c))
                for j, (qx, qy) in enumerate(others)]
    keep = pltpu.make_async_copy(rows(pair_ref, mine), rows(out_ref, mine), local_sem)
    if phase == "start":
        for cp in to_sibling:
            cp.start()
    elif phase == "reduce":
        for cp in to_sibling:
            cp.wait_recv()

        def through_vmem(a_buf, b_buf, sems):
            tr = 128
            loads = [(pltpu.make_async_copy(rows(x_ref, _linear(qx, qy, c)), a_buf.at[q % 2], sems.at[q % 2]),
                      pltpu.make_async_copy(rows(stage_ref, q), b_buf.at[q % 2], sems.at[2 + q % 2]))
                     for q, (qx, qy) in enumerate(chips)]
            stores = [pltpu.make_async_copy(a_buf.at[q % 2], rows(pair_ref, q), sems.at[4 + q % 2]) for q in range(4)]
            for cp in loads[0]:
                cp.start()
            for q in range(4):
                for cp in loads[q]:
                    cp.wait()
                if q + 1 < 4:
                    if q >= 1:
                        stores[q - 1].wait()
                    for cp in loads[q + 1]:
                        cp.start()

                def add(r, carry, q=q):
                    tile = pl.ds(pl.multiple_of(r * tr, tr), tr)
                    a_buf[q % 2, tile, :] = (a_buf[q % 2, tile, :].astype(F32)
                                             + b_buf[q % 2, tile, :].astype(F32)).astype(x_ref.dtype)
                    return carry

                lax.fori_loop(0, m // tr, add, 0)
                stores[q].start()
            stores[2].wait()
            stores[3].wait()

        pl.run_scoped(through_vmem, pltpu.VMEM((2, m, n), x_ref.dtype), pltpu.VMEM((2, m, n), x_ref.dtype),
                      pltpu.SemaphoreType.DMA((6,)))
    elif phase == "send":
        keep.start()
        for cp in to_chips:
            cp.start()
    else:
        for j, (qx, qy) in enumerate(others):
            remote(rows(pair_ref, mine), rows(out_ref, 2 * qx + qy), 4 + j, (qx, qy, c)).wait_recv()
        for cp in to_sibling + to_chips:
            cp.wait_send()
        keep.wait()


_COMM_PHASES = {"gather": (_gather_phase, ("start", "forward", "finish")),
                "exchange": (_exchange_phase, ("start", "finish")),
                "pairsum_exchange": (_pairsum_exchange_phase, ("start", "reduce", "send", "finish"))}


def _comm_scratch(n_arrays):
    return [pltpu.SemaphoreType.DMA((n_arrays, 7)), pltpu.SemaphoreType.DMA((n_arrays, 7)),
            pltpu.SemaphoreType.DMA((n_arrays,))]


def _comm_run(kind, phases, x_refs, out_refs, send_sems, recv_sems, local_sems):
    fn = _COMM_PHASES[kind][0]
    per = len(out_refs) // len(x_refs)
    for phase in phases:
        for a, x_ref in enumerate(x_refs):
            outs = out_refs[a] if per == 1 else tuple(out_refs[per * a:per * (a + 1)])
            fn(phase, x_ref, outs, lambda k, a=a: send_sems.at[a, k], lambda k, a=a: recv_sems.at[a, k],
               local_sems.at[a])


def _comm_out_shapes(kind, arrays):
    if kind == "pairsum_exchange":
        return [jax.ShapeDtypeStruct((a.shape[0] // 2, a.shape[1]), a.dtype) for a in arrays for _ in range(3)]
    return [jax.ShapeDtypeStruct((N_DEV * a.shape[0], a.shape[1]) if kind == "gather" else a.shape, a.dtype)
            for a in arrays]


def _comm_call(kind, arrays, name):
    n = len(arrays)
    shapes = _comm_out_shapes(kind, arrays)

    def body(*refs):
        _comm_run(kind, _COMM_PHASES[kind][1], refs[:n], refs[n:n + len(shapes)], *refs[n + len(shapes):])

    return pl.pallas_call(body, name=name, out_shape=shapes, in_specs=[ANY] * n, out_specs=[ANY] * len(shapes),
                          scratch_shapes=_comm_scratch(n))(*arrays)


def _all_gather(xs, name):
    return _comm_call("gather", [xs], name)[0]


def _mod_fwd(c_all, w_ada):
    def body(c_ref, w_ref, o_ref):
        cv = c_ref[...]
        sc = cv * _sigmoid(cv)
        o_ref[0] = _dot(sc.astype(BF16), w_ref[0].astype(BF16))

    return pl.pallas_call(
        body, name="mod_fwd", grid=(DEPTH,),
        out_shape=jax.ShapeDtypeStruct((DEPTH, N_DEV, ADA_SHARD), F32),
        in_specs=[pl.BlockSpec((N_DEV, D_MODEL), lambda l: (0, 0)),
                  pl.BlockSpec((1, D_MODEL, ADA_SHARD), lambda l: (l, 0, 0))],
        out_specs=pl.BlockSpec((1, N_DEV, ADA_SHARD), lambda l: (l, 0, 0)),
        compiler_params=_params(("arbitrary",)),
    )(c_all, w_ada)


def _w_ada_grad(c_all, dmod_cols):
    def body(c_ref, d_ref, o_ref):
        cv = c_ref[...]
        sc = cv * _sigmoid(cv)
        o_ref[0] = lax.dot_general(sc, d_ref[0], (((0,), (0,)), ((), ())), precision=lax.Precision.HIGHEST,
                                   preferred_element_type=F32)

    return pl.pallas_call(
        body, name="w_ada_grad", grid=(DEPTH,),
        out_shape=jax.ShapeDtypeStruct((DEPTH, D_MODEL, ADA_SHARD), F32),
        in_specs=[pl.BlockSpec((N_DEV, D_MODEL), lambda l: (0, 0)),
                  pl.BlockSpec((1, N_DEV, ADA_SHARD), lambda l: (l, 0, 0))],
        out_specs=pl.BlockSpec((1, D_MODEL, ADA_SHARD), lambda l: (l, 0, 0)),
        compiler_params=_params(("arbitrary",)),
    )(c_all, dmod_cols)


def _comm_plumbing(comm):
    if not comm:
        return 0, [], []
    return len(comm[1]), _comm_out_shapes(*comm), _comm_scratch(len(comm[1]))


def _split_refs(refs, n_in, n_out, n_scratch, comm):
    ci, shapes, _ = _comm_plumbing(comm)
    co = len(shapes)
    a, b, c = n_in + ci, n_in + ci + n_out, n_in + ci + n_out + co
    return refs[:n_in], refs[a:b], refs[c:c + n_scratch], refs[n_in:a], refs[b:c], refs[c + n_scratch:]


def _prenorm_proj(x, g_pre, scale, shift, w_new, cos, sin_signed, comm=None, ts=256):
    s_len = x.shape[0]
    n_cin, c_shapes, c_scratch = _comm_plumbing(comm)

    def body(*refs):
        (x_ref, g_ref, sc_ref, sh_ref, w_ref, cos_ref, sin_ref), (pf_ref, pb_ref, h_ref), _, cin, cout, csem = (
            _split_refs(refs, 7, 3, 0, comm))
        comm_before, comm_after = _comm_hooks(comm, cin, cout, csem, steps=s_len // ts)
        comm_before()
        xv = x_ref[...]
        rstd = lax.rsqrt(jnp.mean(xv * xv, axis=-1, keepdims=True) + EPS)
        h = (xv * rstd * g_ref[...]) * (1.0 + sc_ref[...]) + sh_ref[...]
        hb = h.astype(BF16)
        h_ref[...] = hb
        for j in range(0, NP, 512):
            w = min(512, NP - j)
            acc = _dot(hb, w_ref[:, j:j + w])
            if COL_QB <= j < COL_VA:
                for lo in range(0, w, DIL_HD):
                    pf_ref[:, j + lo:j + lo + DIL_HD] = _rope(acc[:, lo:lo + DIL_HD], cos_ref[...], sin_ref[...])
            elif j < NP_F32:
                pf_ref[:, j:j + w] = acc
            else:
                pb_ref[:, j - NP_F32:j - NP_F32 + w] = acc.astype(BF16)
        comm_after()

    (g_pre, g_spec), (scale, sc_spec), (shift, sh_spec) = _rowvec(g_pre), _rowvec(scale), _rowvec(shift)
    return pl.pallas_call(
        body, name="prenorm_proj_comm" if comm else "prenorm_proj", grid=(s_len // ts,),
        out_shape=[jax.ShapeDtypeStruct((s_len, NP_F32), F32), jax.ShapeDtypeStruct((s_len, NP_BF16), BF16),
                   jax.ShapeDtypeStruct((s_len, D_MODEL), BF16)] + c_shapes,
        in_specs=[pl.BlockSpec((ts, D_MODEL), lambda i: (i, 0)), g_spec, sc_spec, sh_spec,
                  pl.BlockSpec((D_MODEL, NP), lambda i: (0, 0)), pl.BlockSpec((ts, DIL_HD), lambda i: (i, 0)),
                  pl.BlockSpec((ts, DIL_HD), lambda i: (i, 0))] + [ANY] * n_cin,
        out_specs=[pl.BlockSpec((ts, NP_F32), lambda i: (i, 0)), pl.BlockSpec((ts, NP_BF16), lambda i: (i, 0)),
                   pl.BlockSpec((ts, D_MODEL), lambda i: (i, 0))] + [ANY] * len(c_shapes),
        scratch_shapes=c_scratch,
        compiler_params=_params(("arbitrary",), 48),
    )(x, g_pre, scale, shift, w_new, cos, sin_signed, *(comm[1] if comm else []))


GLA_GROUP = 16


def _gla_group_rows(t):
    return [pl.ds(pl.multiple_of((t * GLA_GROUP + j) * GLA_CHUNK, GLA_CHUNK), GLA_CHUNK) for j in range(GLA_GROUP)]


def _gla_chunks_common(q_ref, k_ref, lr_ref, wgu_ref, bgu_ref, rows_list):
    c = GLA_CHUNK
    ri = lax.broadcasted_iota(jnp.int32, (c, c), 0)
    ci = lax.broadcasted_iota(jnp.int32, (c, c), 1)
    tril = (ri >= ci).astype(F32)
    zs = [_dot(lr_ref[rows, :], wgu_ref[...]) + bgu_ref[...] for rows in rows_list]
    las = [_log_sigmoid(z) * (1.0 / GLA_TAU) for z in zs]
    bs = [jnp.dot(tril, la, precision=lax.Precision.HIGHEST, preferred_element_type=F32) for la in las]
    out = []
    for rows, z, b in zip(rows_list, zs, bs):
        q = q_ref[rows, :] * (GLA_DK ** -0.5)
        k = k_ref[rows, :]
        bl = b[c - 1:c, :]
        out.append(dict(z=z, b=b, bl=bl, qe=q * jnp.exp(b), ke=k * jnp.exp(-b), kend=k * jnp.exp(bl - b),
                        dec=jnp.exp(bl)))
    return out, ri, ci


def _head_lane_mask(hh):
    return (lax.broadcasted_iota(jnp.int32, (1, LANE), 1) // GLA_DK) == hh


def _state_block_mask():
    r = lax.broadcasted_iota(jnp.int32, (2 * GLA_DV, LANE), 0) // GLA_DV
    cc = lax.broadcasted_iota(jnp.int32, (2 * GLA_DV, LANE), 1) // GLA_DK
    return r == cc


def _gla_fwd(pf, pb, wgu, bgu, layer, comm=None):
    s_len = pf.shape[0]
    nc = s_len // GLA_CHUNK
    ncomm = len(comm[1]) if comm else 0

    def body(*refs):
        q_ref, k_ref, v_ref, lr_ref, wgu_ref, bgu_ref = refs[:6]
        cin, (o_ref, st_ref), cout = refs[6:6 + ncomm], refs[6 + ncomm:8 + ncomm], refs[8 + ncomm:8 + 2 * ncomm]
        qe_s, cs_s, dec_s = refs[8 + 2 * ncomm:11 + 2 * ncomm]
        comm_before, comm_after = _comm_hooks(comm, cin, cout, refs[11 + 2 * ncomm:], steps=2)
        comm_before()
        bd = _state_block_mask()

        def local(t, carry):
            rows_list = _gla_group_rows(t)
            cm, ri, ci = _gla_chunks_common(q_ref, k_ref, lr_ref, wgu_ref, bgu_ref, rows_list)
            vs = [v_ref[rows, :] for rows in rows_list]
            kebs = [c["ke"].astype(BF16) for c in cm]
            a = [[jnp.where(ri >= ci, _dot_nt(jnp.where(_head_lane_mask(hh), c["qe"], 0.0).astype(BF16), keb), 0.0)
                  .astype(BF16) for hh in range(2)] for c, keb in zip(cm, kebs)]
            oi = [[_dot(ah[hh], v[:, hh * GLA_DV:(hh + 1) * GLA_DV]) for hh in range(2)] for ah, v in zip(a, vs)]
            cs = [jnp.where(bd, _dot_tn(v, c["kend"].astype(BF16)), 0.0) for c, v in zip(cm, vs)]
            for j, (rows, c) in enumerate(zip(rows_list, cm)):
                n = t * GLA_GROUP + j
                o_ref[rows, :] = jnp.concatenate(oi[j], axis=1)
                qe_s[rows, :] = c["qe"].astype(BF16)
                cs_s[n] = cs[j]
                dec_s[n] = jnp.broadcast_to(c["dec"], (8, LANE))
            return carry

        lax.fori_loop(0, nc // GLA_GROUP, local, 0)

        def scan(n, st):
            st_ref[0, n] = st.astype(BF16)
            return dec_s[n][0:1, :] * st + cs_s[n]

        lax.fori_loop(0, nc, scan, jnp.zeros((2 * GLA_DV, LANE), F32))

        def inter(t, carry):
            rows_list = _gla_group_rows(t)
            add = [_dot_nt(qe_s[rows, :], st_ref[0, t * GLA_GROUP + j]) for j, rows in enumerate(rows_list)]
            for rows, av in zip(rows_list, add):
                o_ref[rows, :] = o_ref[rows, :] + av
            return carry

        lax.fori_loop(0, nc // GLA_GROUP, inter, 0)
        comm_after()

    return pl.pallas_call(
        body, name="gla_fwd_comm" if comm else "gla_fwd", grid=(2,),
        out_shape=[jax.ShapeDtypeStruct((s_len, GLA_HEADS * GLA_DV), F32),
                   jax.ShapeDtypeStruct((2, nc, 2 * GLA_DV, LANE), BF16)] + (_comm_out_shapes(*comm) if comm else []),
        in_specs=[pl.BlockSpec((s_len, LANE), lambda g: (0, COL_QA // LANE + g)),
                  pl.BlockSpec((s_len, LANE), lambda g: (0, COL_KA // LANE + g)),
                  pl.BlockSpec((s_len, 2 * GLA_DV), lambda g: (0, (COL_VA - NP_F32) // (2 * GLA_DV) + g)),
                  pl.BlockSpec((s_len, LANE), lambda g: (0, (COL_LR - NP_F32) // LANE)),
                  pl.BlockSpec((None, LANE, LANE), lambda g: (layer, 0, g)),
                  pl.BlockSpec((None, 1, LANE), lambda g: (layer, 0, g))] + [ANY] * ncomm,
        out_specs=[pl.BlockSpec((s_len, 2 * GLA_DV), lambda g: (0, g)),
                   pl.BlockSpec((1, nc, 2 * GLA_DV, LANE), lambda g: (g, 0, 0, 0))] + [ANY] * ncomm,
        scratch_shapes=[pltpu.VMEM((s_len, LANE), BF16), pltpu.VMEM((nc, 2 * GLA_DV, LANE), F32),
                        pltpu.VMEM((nc, 8, LANE), F32)] + (_comm_scratch(ncomm) if comm else []),
        compiler_params=_params(("arbitrary",), 56),
    )(pf, pf, pb, pb, wgu, bgu.reshape(bgu.shape[0], 1, GU_COLS), *(comm[1] if comm else []))


def _rope_tables(s_len):
    inv_freq = ROPE_THETA ** (-jnp.arange(0, DIL_HD, 2, dtype=F32) / DIL_HD)
    ang = jnp.arange(s_len, dtype=F32)[:, None] * inv_freq[None, :]
    cos, sin = jnp.cos(ang), jnp.sin(ang)
    return jnp.concatenate([cos, cos], axis=1), jnp.concatenate([-sin, sin], axis=1)


def _rope(xv, cos, sin_signed):
    return xv * cos + pltpu.roll(xv, DIL_HD // 2, 1) * sin_signed


DIL_GROUP = 8


def _dil_pair_block(i, half, d, nblk, group=DIL_GROUP):
    nb = nblk // d
    j = i + half * (nblk // group)
    if nb >= 2 * group:
        r, n = j % d, j // d
    else:
        r, n = j // nb, j % nb
    kb = jnp.maximum(n - 1, 0)
    qs = r + d * DIL_BLOCK * n
    ks = r + d * DIL_BLOCK * kb
    return qs, ks, jnp.minimum(n, 1)


def _dil_fill_bias(bias):
    qi = lax.broadcasted_iota(jnp.int32, (DIL_BLOCK, 2 * DIL_BLOCK), 0)
    kj = lax.broadcasted_iota(jnp.int32, (DIL_BLOCK, 2 * DIL_BLOCK), 1)
    for sel in range(2):
        dist = qi - kj + DIL_BLOCK * sel
        bias[sel] = jnp.where((dist >= 0) & (dist <= DIL_BLOCK), 0.0, MASK_VALUE)


def _strided(start, size, d):
    return pl.ds(start, size) if d == 1 else pl.ds(start, size, stride=d)


def _comm_hooks(comm, cin, cout, csem, steps=DIL_HEADS):
    def before():
        if comm:
            @pl.when(pl.program_id(0) == 0)
            def _():
                _comm_run(comm[0], ("start",), cin, cout, *csem)

            if comm[0] == "gather":
                @pl.when(pl.program_id(0) == steps - 1)
                def _():
                    _comm_run(comm[0], ("forward",), cin, cout, *csem)

            if comm[0] == "pairsum_exchange":
                @pl.when(pl.program_id(0) == (1 if steps <= 4 else 2))
                def _():
                    _comm_run(comm[0], ("reduce", "send"), cin, cout, *csem)

    def after():
        if comm:
            @pl.when(pl.program_id(0) == steps - 1)
            def _():
                _comm_run(comm[0], ("finish",), cin, cout, *csem)

    return before, after


def _dil_fwd(pf, pb, comm=None):
    s_len = pf.shape[0]
    nblk = s_len // DIL_BLOCK
    prep_rows = 256
    scale = DIL_HD ** -0.5
    nc = len(comm[1]) if comm else 0

    def body(*refs):
        ((qf, kf, v_ref), (o_ref, lse_ref), (vf, o0, o1, o2, l0, l1, l2, bias), cin, cout, csem) = _split_refs(
            refs, 3, 2, 8, comm)
        comm_before, comm_after = _comm_hooks(comm, cin, cout, csem)
        comm_before()
        _dil_fill_bias(bias)

        def prep(t, carry):
            rows = pl.ds(pl.multiple_of(t * prep_rows, prep_rows), prep_rows)
            vf[rows, :] = v_ref[rows, :].astype(F32)
            return carry

        lax.fori_loop(0, s_len // prep_rows, prep, 0)
        for d, o_p, l_p in zip(DIL_DILATIONS, (o0, o1, o2), (l0, l1, l2)):
            if nblk // d == 2:
                units = DIL_GROUP // 2

                def whole(i, carry, d=d, o_p=o_p, l_p=l_p, units=units):
                    rows = [_strided(i + u * (d // units), 2 * DIL_BLOCK, d) for u in range(units)]
                    ld = [(qf[rw, :].astype(BF16), kf[rw, :].astype(BF16), vf[rw, :].astype(BF16)) for rw in rows]
                    both = bias[...].reshape(2 * DIL_BLOCK, 2 * DIL_BLOCK)
                    s = [_dot_nt(qb, kk) * scale + both for qb, kk, _ in ld]
                    m = [jnp.max(sv, axis=-1, keepdims=True) for sv in s]
                    p = [jnp.exp(sv - mv) for sv, mv in zip(s, m)]
                    den = [jnp.sum(pv, axis=-1, keepdims=True) for pv in p]
                    r = [_dot(pv.astype(BF16), vv) for pv, (_, _, vv) in zip(p, ld)]
                    for rv, dv, mv, rw in zip(r, den, m, rows):
                        o_p[rw, :] = rv / dv
                        l_p[rw, :] = jnp.broadcast_to(mv + jnp.log(dv), (2 * DIL_BLOCK, DIL_HD))
                    return carry

                lax.fori_loop(0, d // units, whole, 0)
                continue

            def pair(i, carry, d=d, o_p=o_p, l_p=l_p):
                idx = [_dil_pair_block(i, half, d, nblk, DIL_GROUP) for half in range(DIL_GROUP)]
                ld = [(qf[_strided(qs, DIL_BLOCK, d), :].astype(BF16),
                       kf[_strided(ks, 2 * DIL_BLOCK, d), :].astype(BF16),
                       vf[_strided(ks, 2 * DIL_BLOCK, d), :].astype(BF16)) for qs, ks, _ in idx]
                s = [_dot_nt(qb, kk) * scale + bias[sel] for (qb, kk, _), (_, _, sel) in zip(ld, idx)]
                m = [jnp.max(sv, axis=-1, keepdims=True) for sv in s]
                p = [jnp.exp(sv - mv) for sv, mv in zip(s, m)]
                den = [jnp.sum(pv, axis=-1, keepdims=True) for pv in p]
                r = [_dot(pv.astype(BF16), vv) for pv, (_, _, vv) in zip(p, ld)]
                for rv, dv, mv, (qs, _, _) in zip(r, den, m, idx):
                    o_p[_strided(qs, DIL_BLOCK, d), :] = rv / dv
                    l_p[_strided(qs, DIL_BLOCK, d), :] = jnp.broadcast_to(mv + jnp.log(dv), (DIL_BLOCK, DIL_HD))
                return carry

            lax.fori_loop(0, nblk // DIL_GROUP, pair, 0)

        def comb(t, carry):
            rows = pl.ds(pl.multiple_of(t * prep_rows, prep_rows), prep_rows)
            a0, a1, a2 = l0[rows, :], l1[rows, :], l2[rows, :]
            m = jnp.maximum(jnp.maximum(a0, a1), a2)
            e0, e1, e2 = jnp.exp(a0 - m), jnp.exp(a1 - m), jnp.exp(a2 - m)
            tot = e0 + e1 + e2
            o_ref[rows, :] = (e0 * o0[rows, :] + e1 * o1[rows, :] + e2 * o2[rows, :]) / tot
            lse_ref[rows, :] = m + jnp.log(tot)
            return carry

        lax.fori_loop(0, s_len // prep_rows, comb, 0)
        comm_after()

    head = lambda base: pl.BlockSpec((s_len, DIL_HD), lambda h: (0, base // DIL_HD + h))
    out = pl.BlockSpec((s_len, DIL_HD), lambda h: (0, h))
    shp = jax.ShapeDtypeStruct((s_len, DIL_HEADS * DIL_HD), F32)
    return pl.pallas_call(
        body, name="dil_fwd_comm" if comm else "dil_fwd", grid=(DIL_HEADS,),
        out_shape=[shp, shp] + (_comm_out_shapes(*comm) if comm else []),
        in_specs=[head(COL_QB), head(COL_KB), head(COL_VB - NP_F32)] + [ANY] * nc,
        out_specs=[out, out] + [ANY] * nc,
        scratch_shapes=[pltpu.VMEM((s_len, DIL_HD), F32) for _ in range(7)]
        + [pltpu.VMEM((2, DIL_BLOCK, 2 * DIL_BLOCK), F32)] + (_comm_scratch(nc) if comm else []),
        compiler_params=_params(("arbitrary",), 56),
    )(pf, pf, pb, *(comm[1] if comm else []))


def _silu_and_grad(z):
    sg = _sigmoid(z)
    return z * sg, sg * (1.0 + z * (1.0 - sg))


def _post_fwd(o_a, o_b, pf, g_heads, w_out, x, gate, g_post, target=None, ts=256):
    s_len = x.shape[0]
    half = GLA_HEADS * GLA_DV
    last = target is not None

    def body(*refs):
        oa_ref, ob_ref, z_ref, gh_ref, w_ref, x_ref, gate_ref, gp_ref = refs[:8]
        xo_ref, u_ref = refs[8 + last:10 + last]
        y_ref = refs[-1]
        for src, base in ((oa_ref, 0), (ob_ref, half)):
            for hh in range(4):
                lo = hh * LANE
                og = src[:, lo:lo + LANE]
                on = og * lax.rsqrt(jnp.mean(og * og, axis=-1, keepdims=True) + EPS)
                zg = z_ref[:, base + lo:base + lo + LANE].astype(F32)
                y_ref[:, base + lo:base + lo + LANE] = (on * gh_ref[:, base + lo:base + lo + LANE]
                                                        * (zg * _sigmoid(zg))).astype(BF16)
        u = _dot(y_ref[...], w_ref[...])
        u_ref[...] = u.astype(BF16)
        rstd = lax.rsqrt(jnp.mean(u * u, axis=-1, keepdims=True) + EPS)
        x_out = x_ref[...] + gate_ref[...] * (u * rstd * gp_ref[...])
        if last:
            t_ref, loss_ref = refs[8], refs[11]

            @pl.when(pl.program_id(0) == 0)
            def _():
                loss_ref[...] = jnp.zeros_like(loss_ref)

            e = x_out - t_ref[...]
            xo_ref[...] = e * (1.0 / D_MODEL)
            loss_ref[...] += 0.5 * jnp.sum(jnp.mean(e * e, axis=-1, keepdims=True))
        else:
            xo_ref[...] = x_out

    (g_heads, gh_spec), (gate, gate_spec), (g_post, gp_spec) = _rowvec(g_heads), _rowvec(gate), _rowvec(g_post)
    tile = pl.BlockSpec((ts, D_MODEL), lambda i: (i, 0))
    halft = pl.BlockSpec((ts, half), lambda i: (i, 0))
    return pl.pallas_call(
        body, name="post_fwd_loss" if last else "post_fwd", grid=(s_len // ts,),
        out_shape=[jax.ShapeDtypeStruct((s_len, D_MODEL), F32), jax.ShapeDtypeStruct((s_len, D_MODEL), BF16)]
        + ([jax.ShapeDtypeStruct((8, LANE), F32)] if last else []),
        in_specs=[halft, halft, tile, gh_spec, pl.BlockSpec((D_MODEL, D_MODEL), lambda i: (0, 0)), tile, gate_spec,
                  gp_spec] + ([tile] if last else []),
        out_specs=[tile, tile] + ([pl.BlockSpec((8, LANE), lambda i: (0, 0))] if last else []),
        scratch_shapes=[pltpu.VMEM((ts, D_MODEL), BF16)],
        compiler_params=_params(("arbitrary",), 40),
    )(o_a, o_b, pf, g_heads, w_out, x, gate, g_post, *([target] if last else []))


def _post_bwd(dxo, u, gate, g_post, w_out, o_a, o_b, pf, g_heads, ts=512):
    s_len = dxo.shape[0]
    half = GLA_HEADS * GLA_DV
    steps = s_len // ts

    def body(dx_ref, u_ref, gate_ref, gp_ref, w_ref, oa_ref, ob_ref, z_ref, gh_ref, do_ref, dz_ref, sums_ref, gw_ref,
             y_s, acc):
        @pl.when(pl.program_id(0) == 0)
        def _():
            sums_ref[...] = jnp.zeros_like(sums_ref)
            acc[...] = jnp.zeros_like(acc)

        dx = dx_ref[...]
        u = u_ref[...].astype(F32)
        rstd = lax.rsqrt(jnp.mean(u * u, axis=-1, keepdims=True) + EPS)
        un = u * rstd
        sums_ref[0:1, :] += jnp.sum(dx * (un * gp_ref[...]), axis=0, keepdims=True)
        drn = dx * gate_ref[...]
        sums_ref[1:2, :] += jnp.sum(drn * un, axis=0, keepdims=True)
        dun = drn * gp_ref[...]
        du = rstd * (dun - un * jnp.mean(dun * un, axis=-1, keepdims=True))
        dub = du.astype(BF16)
        dy = _dot_nt(dub, w_ref[...])
        for src, base in ((oa_ref, 0), (ob_ref, half)):
            for hh in range(4):
                lo = base + hh * LANE
                og = src[:, hh * LANE:(hh + 1) * LANE]
                rs = lax.rsqrt(jnp.mean(og * og, axis=-1, keepdims=True) + EPS)
                on = og * rs
                zg = z_ref[:, lo:lo + LANE].astype(F32)
                sz, dsz = _silu_and_grad(zg)
                gg = gh_ref[:, lo:lo + LANE]
                dyg = dy[:, lo:lo + LANE]
                y_s[:, lo:lo + LANE] = (on * gg * sz).astype(BF16)
                sums_ref[2:3, lo:lo + LANE] += jnp.sum(dyg * sz * on, axis=0, keepdims=True)
                dz_ref[:, lo:lo + LANE] = (dyg * on * gg * dsz).astype(BF16)
                don = dyg * gg * sz
                do_ref[:, lo:lo + LANE] = (rs * (don - on * jnp.mean(don * on, axis=-1, keepdims=True))).astype(BF16)
        acc[...] += _dot_tn(y_s[...], dub)

        @pl.when(pl.program_id(0) == steps - 1)
        def _():
            gw_ref[...] = acc[...].astype(BF16)

    (g_heads, gh_spec), (gate, gate_spec), (g_post, gp_spec) = _rowvec(g_heads), _rowvec(gate), _rowvec(g_post)
    tile = pl.BlockSpec((ts, D_MODEL), lambda i: (i, 0))
    halft = pl.BlockSpec((ts, half), lambda i: (i, 0))
    whole = pl.BlockSpec((D_MODEL, D_MODEL), lambda i: (0, 0))
    return pl.pallas_call(
        body, name="post_bwd", grid=(steps,),
        out_shape=(jax.ShapeDtypeStruct((s_len, D_MODEL), BF16), jax.ShapeDtypeStruct((s_len, D_MODEL), BF16),
                   jax.ShapeDtypeStruct((8, D_MODEL), F32), jax.ShapeDtypeStruct((D_MODEL, D_MODEL), BF16)),
        in_specs=[tile, tile, gate_spec, gp_spec, whole, halft, halft, tile, gh_spec],
        out_specs=(tile, tile, pl.BlockSpec((8, D_MODEL), lambda i: (0, 0)), whole),
        scratch_shapes=[pltpu.VMEM((ts, D_MODEL), BF16), pltpu.VMEM((D_MODEL, D_MODEL), F32)],
        compiler_params=_params(("arbitrary",), 48),
    )(dxo, u, gate, g_post, w_out, o_a, o_b, pf, g_heads)


def _gla_bwd(pf, pb, wgu, bgu, layer, states, do, comm=None):
    s_len = pf.shape[0]
    nc = s_len // GLA_CHUNK
    c = GLA_CHUNK
    n_cin, c_shapes, c_scratch = _comm_plumbing(comm)

    def body(*refs):
        ((q_ref, k_ref, v_ref, lr_ref, wgu_ref, bgu_ref, st_ref, do_ref),
         (dq_ref, dk_ref, dv_ref, dlr_ref, dwgu_ref, dbgu_ref), (ds_s, dec_s, dw_acc, db_acc),
         cin, cout, csem) = _split_refs(refs, 8, 6, 4, comm)
        comm_before, comm_after = _comm_hooks(comm, cin, cout, csem, steps=2)
        comm_before()
        dw_acc[...] = jnp.zeros_like(dw_acc)
        db_acc[...] = jnp.zeros_like(db_acc)
        bd = _state_block_mask()
        last_row = lax.broadcasted_iota(jnp.int32, (c, LANE), 0) == c - 1

        def local(t, carry):
            rows_list = _gla_group_rows(t)
            cm, _, _ = _gla_chunks_common(q_ref, k_ref, lr_ref, wgu_ref, bgu_ref, rows_list)
            loc = [jnp.where(bd, _dot_tn(do_ref[rows, :], cc["qe"].astype(BF16)), 0.0)
                   for rows, cc in zip(rows_list, cm)]
            for j, cc in enumerate(cm):
                ds_s[t * GLA_GROUP + j] = loc[j]
                dec_s[t * GLA_GROUP + j] = jnp.broadcast_to(cc["dec"], (8, LANE))
            return carry

        lax.fori_loop(0, nc // GLA_GROUP, local, 0)

        def scan(t, dst):
            n = nc - 1 - t
            loc = ds_s[n]
            ds_s[n] = dst
            return dec_s[n][0:1, :] * dst + loc

        lax.fori_loop(0, nc, scan, jnp.zeros((2 * GLA_DV, LANE), F32))

        def rest(t, carry):
            rows_list = _gla_group_rows(t)
            cm, ri, ci = _gla_chunks_common(q_ref, k_ref, lr_ref, wgu_ref, bgu_ref, rows_list)
            ns = [t * GLA_GROUP + j for j in range(GLA_GROUP)]
            vs = [v_ref[rows, :] for rows in rows_list]
            dobs = [do_ref[rows, :] for rows in rows_list]
            stbs = [st_ref[0, n] for n in ns]
            dsts = [ds_s[n] for n in ns]
            dstbs = [d.astype(BF16) for d in dsts]
            qebs = [cc["qe"].astype(BF16) for cc in cm]
            kebs = [cc["ke"].astype(BF16) for cc in cm]
            kendbs = [cc["kend"].astype(BF16) for cc in cm]
            hms = [_head_lane_mask(hh) for hh in range(2)]
            qehs = [[jnp.where(hm, cc["qe"], 0.0).astype(BF16) for hm in hms] for cc in cm]
            kehs = [[jnp.where(hm, cc["ke"], 0.0).astype(BF16) for hm in hms] for cc in cm]
            heads = lambda x: [x[:, hh * GLA_DV:(hh + 1) * GLA_DV] for hh in range(2)]
            vhs, dohs = [heads(v) for v in vs], [heads(d) for d in dobs]

            dqe0 = [_dot(dob, stb) for dob, stb in zip(dobs, stbs)]
            dkend = [_dot(v, dstb) for v, dstb in zip(vs, dstbs)]
            dv0 = [_dot_nt(kb, dstb) for kb, dstb in zip(kendbs, dstbs)]
            a_t = [[jnp.where(ci >= ri, _dot_nt(kehs[j][hh], qebs[j]), 0.0).astype(BF16) for hh in range(2)]
                   for j in range(GLA_GROUP)]
            da = [[jnp.where(ri >= ci, _dot_nt(dohs[j][hh], vhs[j][hh]), 0.0).astype(BF16) for hh in range(2)]
                  for j in range(GLA_GROUP)]
            da_t = [[jnp.where(ci >= ri, _dot_nt(vhs[j][hh], dohs[j][hh]), 0.0).astype(BF16) for hh in range(2)]
                    for j in range(GLA_GROUP)]
            dv1 = [[_dot(a_t[j][hh], dohs[j][hh]) for hh in range(2)] for j in range(GLA_GROUP)]
            dqe1 = [[_dot(da[j][hh], kebs[j]) for hh in range(2)] for j in range(GLA_GROUP)]
            dke1 = [[_dot(da_t[j][hh], qehs[j][hh]) for hh in range(2)] for j in range(GLA_GROUP)]

            dbs, dzs = [], []
            for j, (rows, cc) in enumerate(zip(rows_list, cm)):
                qe, ke, kend, b, bl = cc["qe"], cc["ke"], cc["kend"], cc["b"], cc["bl"]
                dqe = dqe0[j] + jnp.where(hms[0], dqe1[j][0], 0.0) + jnp.where(hms[1], dqe1[j][1], 0.0)
                dke = jnp.where(hms[0], dke1[j][0], 0.0) + jnp.where(hms[1], dke1[j][1], 0.0)
                dv_ref[rows, :] = (dv0[j] + jnp.concatenate(dv1[j], axis=1)).astype(BF16)
                dq_ref[rows, :] = (dqe * jnp.exp(b) * (GLA_DK ** -0.5)).astype(BF16)
                dk_ref[rows, :] = (dke * jnp.exp(-b) + dkend[j] * jnp.exp(bl - b)).astype(BF16)
                ddec = jnp.sum(dsts[j] * stbs[j].astype(F32), axis=0, keepdims=True)
                dbl = jnp.sum(dkend[j] * kend, axis=0, keepdims=True) + ddec * cc["dec"]
                dbs.append(dqe * qe - dke * ke - dkend[j] * kend + jnp.where(last_row, dbl, 0.0))
            triu = (ci >= ri).astype(F32)
            dlas = [jnp.dot(triu, db, precision=lax.Precision.HIGHEST, preferred_element_type=F32) for db in dbs]
            dzs = [dla * (1.0 / GLA_TAU) * _sigmoid(-cc["z"]) for dla, cc in zip(dlas, cm)]
            dzbs = [dz.astype(BF16) for dz in dzs]
            dlrs = [_dot_nt(dzb, wgu_ref[...]) for dzb in dzbs]
            dws = [_dot_tn(lr_ref[rows, :], dzb) for rows, dzb in zip(rows_list, dzbs)]
            for rows, dlr in zip(rows_list, dlrs):
                dlr_ref[0, rows, :] = dlr
            dw_acc[...] += functools.reduce(lambda x, y: x + y, dws)
            db_acc[0:1, :] += jnp.sum(functools.reduce(lambda x, y: x + y, dzs), axis=0, keepdims=True)
            return carry

        lax.fori_loop(0, nc // GLA_GROUP, rest, 0)
        dwgu_ref[...] = dw_acc[...]
        dbgu_ref[...] = db_acc[...]
        comm_after()

    pair = pl.BlockSpec((s_len, LANE), lambda g: (0, g))
    return pl.pallas_call(
        body, name="gla_bwd_comm" if comm else "gla_bwd", grid=(2,),
        out_shape=[jax.ShapeDtypeStruct((s_len, GU_COLS), BF16), jax.ShapeDtypeStruct((s_len, GU_COLS), BF16),
                   jax.ShapeDtypeStruct((s_len, GLA_HEADS * GLA_DV), BF16),
                   jax.ShapeDtypeStruct((2, s_len, LANE), F32),
                   jax.ShapeDtypeStruct((LANE, GU_COLS), F32), jax.ShapeDtypeStruct((8, GU_COLS), F32)] + c_shapes,
        in_specs=[pl.BlockSpec((s_len, LANE), lambda g: (0, COL_QA // LANE + g)),
                  pl.BlockSpec((s_len, LANE), lambda g: (0, COL_KA // LANE + g)),
                  pl.BlockSpec((s_len, 2 * GLA_DV), lambda g: (0, (COL_VA - NP_F32) // (2 * GLA_DV) + g)),
                  pl.BlockSpec((s_len, LANE), lambda g: (0, (COL_LR - NP_F32) // LANE)),
                  pl.BlockSpec((None, LANE, LANE), lambda g: (layer, 0, g)),
                  pl.BlockSpec((None, 1, LANE), lambda g: (layer, 0, g)),
                  pl.BlockSpec((1, nc, 2 * GLA_DV, LANE), lambda g: (g, 0, 0, 0)),
                  pl.BlockSpec((s_len, 2 * GLA_DV), lambda g: (0, g))] + [ANY] * n_cin,
        out_specs=[pair, pair, pl.BlockSpec((s_len, 2 * GLA_DV), lambda g: (0, g)),
                   pl.BlockSpec((1, s_len, LANE), lambda g: (g, 0, 0)),
                   pl.BlockSpec((LANE, LANE), lambda g: (0, g)), pl.BlockSpec((8, LANE), lambda g: (0, g))]
        + [ANY] * len(c_shapes),
        scratch_shapes=[pltpu.VMEM((nc, 2 * GLA_DV, LANE), F32), pltpu.VMEM((nc, 8, LANE), F32),
                        pltpu.VMEM((LANE, LANE), F32), pltpu.VMEM((8, LANE), F32)] + c_scratch,
        compiler_params=_params(("arbitrary",), 56),
    )(pf, pf, pb, pb, wgu, bgu.reshape(bgu.shape[0], 1, GU_COLS), states, do, *(comm[1] if comm else []))


def _dil_bwd(pf, pb, cos, sin_signed, do, o_b, lse, comm=None):
    s_len = pf.shape[0]
    nblk = s_len // DIL_BLOCK
    prep_rows = 256
    scale = DIL_HD ** -0.5
    nc = len(comm[1]) if comm else 0

    def body(*refs):
        ((q_ref, kf, v_ref, cos_ref, sin_ref, do_ref, o_ref, lse_ref), (dq_ref, dk_ref, dv_ref),
         (qf, vf, dof, dl, dqa, dka, dva, bias), cin, cout, csem) = _split_refs(refs, 8, 3, 8, comm)
        comm_before, comm_after = _comm_hooks(comm, cin, cout, csem)
        comm_before()
        _dil_fill_bias(bias)

        def prep(t, carry):
            rows = pl.ds(pl.multiple_of(t * prep_rows, prep_rows), prep_rows)
            qf[rows, :] = q_ref[rows, :] * scale
            vf[rows, :] = v_ref[rows, :].astype(F32)
            dov = do_ref[rows, :].astype(F32)
            dof[rows, :] = dov
            dl[rows, :] = jnp.broadcast_to(jnp.sum(dov * o_ref[rows, :], axis=-1, keepdims=True), (prep_rows, DIL_HD))
            zero = jnp.zeros((prep_rows, DIL_HD), F32)
            dqa[rows, :] = zero
            dka[rows, :] = zero
            dva[rows, :] = zero
            return carry

        lax.fori_loop(0, s_len // prep_rows, prep, 0)

        for d in DIL_DILATIONS:
            if nblk // d == 2:
                units = DIL_GROUP // 2

                def whole(i, carry, d=d, units=units):
                    rows = [_strided(i + u * (d // units), 2 * DIL_BLOCK, d) for u in range(units)]
                    ld = [(qf[rw, :].astype(BF16), kf[rw, :].astype(BF16), vf[rw, :].astype(BF16),
                           dof[rw, :].astype(BF16)) for rw in rows]
                    both = bias[...].reshape(2 * DIL_BLOCK, 2 * DIL_BLOCK)
                    s = [_dot_nt(qb, kk) + both for qb, kk, _, _ in ld]
                    dp = [_dot_nt(dob, vv) for _, _, vv, dob in ld]
                    p = [jnp.exp(sv - lse_ref[rw, :][:, 0:1]) for sv, rw in zip(s, rows)]
                    ds = [(pv * (dpv - dl[rw, :][:, 0:1])).astype(BF16) for pv, dpv, rw in zip(p, dp, rows)]
                    pb = [pv.astype(BF16) for pv in p]
                    gq = [_dot(dsv, kk) for dsv, (_, kk, _, _) in zip(ds, ld)]
                    gk = [_dot_tn(dsv, qb) for dsv, (qb, _, _, _) in zip(ds, ld)]
                    gv = [_dot_tn(pv, dob) for pv, (_, _, _, dob) in zip(pb, ld)]
                    for rw, a, b, c in zip(rows, gq, gk, gv):
                        dqa[rw, :] += a
                        dka[rw, :] += b
                        dva[rw, :] += c
                    return carry

                lax.fori_loop(0, d // units, whole, 0)
                continue

            def pair(i, carry, d=d):
                idx = [_dil_pair_block(i, half, d, nblk) for half in range(DIL_GROUP)]
                rows = [(_strided(qs, DIL_BLOCK, d), _strided(ks, 2 * DIL_BLOCK, d)) for qs, ks, _ in idx]
                ld = [(qf[qr, :].astype(BF16), kf[kr, :].astype(BF16), vf[kr, :].astype(BF16),
                       dof[qr, :].astype(BF16)) for qr, kr in rows]
                s = [_dot_nt(qb, kk) + bias[sel] for (qb, kk, _, _), (_, _, sel) in zip(ld, idx)]
                dp = [_dot_nt(dob, vv) for _, _, vv, dob in ld]
                p = [jnp.exp(sv - lse_ref[qr, :][:, 0:1]) for sv, (qr, _) in zip(s, rows)]
                ds = [(pv * (dpv - dl[qr, :][:, 0:1])).astype(BF16) for pv, dpv, (qr, _) in zip(p, dp, rows)]
                pb = [pv.astype(BF16) for pv in p]
                gq = [_dot(dsv, kk) for dsv, (_, kk, _, _) in zip(ds, ld)]
                gk = [_dot_tn(dsv, qb) for dsv, (qb, _, _, _) in zip(ds, ld)]
                gv = [_dot_tn(pv, dob) for pv, (_, _, _, dob) in zip(pb, ld)]
                for (qr, kr), a, b, c in zip(rows, gq, gk, gv):
                    dqa[qr, :] += a
                    dka[kr, :] += b
                    dva[kr, :] += c
                return carry

            lax.fori_loop(0, nblk // DIL_GROUP, pair, 0)

        def fin(t, carry):
            rows = pl.ds(pl.multiple_of(t * prep_rows, prep_rows), prep_rows)
            cs, sn = cos_ref[rows, :], sin_ref[rows, :]
            gq, gk = dqa[rows, :] * scale, dka[rows, :]
            dq_ref[rows, :] = (gq * cs - pltpu.roll(gq, DIL_HD // 2, 1) * sn).astype(BF16)
            dk_ref[rows, :] = (gk * cs - pltpu.roll(gk, DIL_HD // 2, 1) * sn).astype(BF16)
            dv_ref[rows, :] = dva[rows, :].astype(BF16)
            return carry

        lax.fori_loop(0, s_len // prep_rows, fin, 0)
        comm_after()

    head = lambda base: pl.BlockSpec((s_len, DIL_HD), lambda h: (0, base // DIL_HD + h))
    table = pl.BlockSpec((s_len, DIL_HD), lambda h: (0, 0))
    out = pl.BlockSpec((s_len, DIL_HD), lambda h: (0, h))
    shp = jax.ShapeDtypeStruct((s_len, DIL_HEADS * DIL_HD), BF16)
    return pl.pallas_call(
        body, name="dil_bwd_comm" if comm else "dil_bwd", grid=(DIL_HEADS,),
        out_shape=[shp, shp, shp] + (_comm_out_shapes(*comm) if comm else []),
        in_specs=[head(COL_QB), head(COL_KB), head(COL_VB - NP_F32), table, table,
                  pl.BlockSpec((s_len, DIL_HD), lambda h: (0, DIL_HEADS + h)), out, out] + [ANY] * nc,
        out_specs=[out, out, out] + [ANY] * len(_comm_plumbing(comm)[1]),
        scratch_shapes=[pltpu.VMEM((s_len, DIL_HD), F32) for _ in range(7)]
        + [pltpu.VMEM((2, DIL_BLOCK, 2 * DIL_BLOCK), F32)] + (_comm_scratch(nc) if comm else []),
        compiler_params=_params(("arbitrary",), 56),
    )(pf, pf, pb, cos, sin_signed, do, o_b, lse, *(comm[1] if comm else []))


_PIECES = ((COL_Z, 1024), (COL_QA, 256), (COL_KA, 256), (COL_QB, 512), (COL_KB, 512), (COL_VA, 512), (COL_VB, 512),
           (COL_LR, 128))


def _in_bwd(pieces, w_new, x, dxo, g_pre, scale, comm=None, ts=256):
    s_len = x.shape[0]
    nc = len(comm[1]) if comm else 0
    nco = len(_comm_out_shapes(*comm)) if comm else 0
    npc = len(_PIECES)

    def body(*refs):
        p_refs = refs[:npc]
        w_ref, x_ref, dxo_ref, g_ref, sc_ref = refs[npc:npc + 5]
        cin, (dx_ref, sums_ref), cout = (refs[npc + 5:npc + 5 + nc], refs[npc + 5 + nc:npc + 7 + nc],
                                         refs[npc + 7 + nc:npc + 7 + nc + nco])
        comm_before, comm_after = _comm_hooks(comm, cin, cout, refs[npc + 7 + nc + nco:], steps=s_len // ts)
        comm_before()

        @pl.when(pl.program_id(0) == 0)
        def _():
            sums_ref[...] = jnp.zeros_like(sums_ref)

        dh = jnp.zeros((ts, D_MODEL), F32)
        for p_ref, (col, width) in zip(p_refs, _PIECES):
            dh += _dot_nt(p_ref[...], w_ref[:, col:col + width])
        xv = x_ref[...]
        rstd = lax.rsqrt(jnp.mean(xv * xv, axis=-1, keepdims=True) + EPS)
        xn = xv * rstd
        sums_ref[0:1, :] += jnp.sum(dh, axis=0, keepdims=True)
        sums_ref[1:2, :] += jnp.sum(dh * (xn * g_ref[...]), axis=0, keepdims=True)
        dr = dh * (1.0 + sc_ref[...])
        sums_ref[2:3, :] += jnp.sum(dr * xn, axis=0, keepdims=True)
        dxn = dr * g_ref[...]
        dx_ref[...] = dxo_ref[...] + rstd * (dxn - xn * jnp.mean(dxn * xn, axis=-1, keepdims=True))
        comm_after()

    (g_pre, g_spec), (scale, sc_spec) = _rowvec(g_pre), _rowvec(scale)
    tile = pl.BlockSpec((ts, D_MODEL), lambda i: (i, 0))
    return pl.pallas_call(
        body, name="in_bwd_comm" if comm else "in_bwd", grid=(s_len // ts,),
        out_shape=[jax.ShapeDtypeStruct((s_len, D_MODEL), F32), jax.ShapeDtypeStruct((8, D_MODEL), F32)]
        + (_comm_out_shapes(*comm) if comm else []),
        in_specs=[pl.BlockSpec((ts, width), lambda i: (i, 0)) for _, width in _PIECES]
        + [pl.BlockSpec((D_MODEL, NP), lambda i: (0, 0)), tile, tile, g_spec, sc_spec] + [ANY] * nc,
        out_specs=[tile, pl.BlockSpec((8, D_MODEL), lambda i: (0, 0))] + [ANY] * nco,
        scratch_shapes=_comm_scratch(nc) if comm else [],
        compiler_params=_params(("arbitrary",), 56),
    )(*pieces, w_new, x, dxo, g_pre, scale, *(comm[1] if comm else []))


def _w_in_to_kernel(gathered, comm=None, tr=128):
    n_cin, c_shapes, c_scratch = _comm_plumbing(comm)
    n_parts = len(gathered)
    first = [sum(g.shape[1] for g in gathered[:p]) // tr for p in range(n_parts + 1)]

    def body(*refs):
        g_refs, (o_ref,), _, cin, cout, csem = _split_refs(refs, n_parts, 1, 0, comm)
        comm_before, comm_after = _comm_hooks(comm, cin, cout, csem, steps=D_MODEL // tr)
        comm_before()
        for p, g_ref in enumerate(g_refs):
            @pl.when((pl.program_id(0) >= first[p]) & (pl.program_id(0) < first[p + 1]))
            def _(g_ref=g_ref):
                cols = jnp.concatenate([g_ref[k].astype(F32) for k in range(N_DEV)], axis=1)
                pad = jnp.zeros((tr, LANE - GLA_LOWRANK), F32)
                o_ref[...] = jnp.concatenate(
                    [cols[:, 1024:1536], cols[:, 3088:3600], cols[:, 0:512], cols[:, 1552:2576], cols[:, 512:1024],
                     cols[:, 2576:3088], cols[:, 1536:1552], pad], axis=1).astype(BF16)
        comm_after()

    part = lambda p: pl.BlockSpec((N_DEV, tr, W_IN_SHARD),
                                  lambda i: (0, jnp.clip(i - first[p], 0, first[p + 1] - first[p] - 1), 0))
    return pl.pallas_call(
        body, name="w_in_to_kernel_comm" if comm else "w_in_to_kernel", grid=(D_MODEL // tr,),
        out_shape=[jax.ShapeDtypeStruct((D_MODEL, NP), BF16)] + c_shapes,
        in_specs=[part(p) for p in range(n_parts)] + [ANY] * n_cin,
        out_specs=[pl.BlockSpec((tr, NP), lambda i: (i, 0))] + [ANY] * len(c_shapes),
        scratch_shapes=c_scratch,
        compiler_params=_params(("arbitrary",)),
    )(*gathered, *(comm[1] if comm else []))


def _grad_w_in(h, pieces, ts=512, tr=128):
    s_len = h.shape[0]
    steps = s_len // ts

    def body(*refs):
        h_ref, p_refs = refs[0], refs[1:1 + len(_PIECES)]
        o_ref, acc = refs[1 + len(_PIECES):]

        @pl.when(pl.program_id(0) == 0)
        def _():
            acc[...] = jnp.zeros_like(acc)

        hv = h_ref[...]
        for p_ref, (col, width) in zip(p_refs, _PIECES):
            acc[:, col:col + width] += _dot_tn(hv, p_ref[...])

        @pl.when(pl.program_id(0) == steps - 1)
        def _():
            def rows_out(t, carry):
                rows = pl.ds(pl.multiple_of(t * tr, tr), tr)
                g = acc[rows, :]
                cols = jnp.concatenate(
                    [g[:, COL_QA:COL_QB], g[:, COL_VA:COL_VB], g[:, 0:512], g[:, COL_LR:COL_LR + GLA_LOWRANK],
                     g[:, COL_QB:COL_VA], g[:, COL_VB:COL_LR], g[:, 512:1024]], axis=1)
                for k in range(N_DEV):
                    o_ref[k, rows, :] = cols[:, W_IN_SHARD * k:W_IN_SHARD * (k + 1)].astype(BF16)
                return carry

            lax.fori_loop(0, D_MODEL // tr, rows_out, 0)

    return pl.pallas_call(
        body, name="grad_w_in", grid=(steps,),
        out_shape=jax.ShapeDtypeStruct((N_DEV, D_MODEL, W_IN_SHARD), BF16),
        in_specs=[pl.BlockSpec((ts, D_MODEL), lambda i: (i, 0))]
        + [pl.BlockSpec((ts, width), lambda i: (i, 0)) for _, width in _PIECES],
        out_specs=pl.BlockSpec((N_DEV, D_MODEL, W_IN_SHARD), lambda i: (0, 0, 0)),
        scratch_shapes=[pltpu.VMEM((D_MODEL, NP), F32)],
        compiler_params=_params(("arbitrary",), 56),
    )(h, *pieces)


def _adam_math(w, g, m, v):
    m = ADAM_B1 * m + (1.0 - ADAM_B1) * g
    v = ADAM_B2 * v + (1.0 - ADAM_B2) * (g * g)
    m_hat = m / (1.0 - ADAM_B1 ** ADAM_STEP)
    v_hat = v / (1.0 - ADAM_B2 ** ADAM_STEP)
    delta = -ADAM_LR * (m_hat / (jnp.sqrt(v_hat) + ADAM_EPS) + ADAM_WD * w)
    return delta, m, v


def _adamw(w, parts, m, v, name, tr):
    r, cdim = w.shape
    n_parts = parts.shape[0]

    def body(w_ref, p_ref, m_ref, v_ref, g_ref, d_ref, nm_ref, nv_ref):
        g = p_ref[0].astype(F32)
        for k in range(1, n_parts):
            g = g + p_ref[k].astype(F32)
        g_ref[...] = g
        d_ref[...], nm_ref[...], nv_ref[...] = _adam_math(w_ref[...], g, m_ref[...], v_ref[...])

    tile = pl.BlockSpec((tr, cdim), lambda i: (i, 0))
    shp = jax.ShapeDtypeStruct((r, cdim), F32)
    return pl.pallas_call(
        body, name=name, grid=(r // tr,), out_shape=(shp, shp, shp, shp),
        in_specs=[tile, pl.BlockSpec((n_parts, tr, cdim), lambda i: (0, i, 0)), tile, tile],
        out_specs=(tile, tile, tile, tile),
        compiler_params=_params(("arbitrary",), 40),
    )(w, parts, m, v)


def _adamw_layers(w, parts, m, v, name, tr):
    n_layers, r, cdim = w.shape

    def body(*refs):
        w_ref, p_refs, (m_ref, v_ref) = refs[0], refs[1:1 + n_layers], refs[1 + n_layers:3 + n_layers]
        g_ref, d_ref, nm_ref, nv_ref = refs[3 + n_layers:]
        for l, p_ref in enumerate(p_refs):
            @pl.when(pl.program_id(0) == l)
            def _(p_ref=p_ref):
                g = p_ref[0].astype(F32)
                for k in range(1, p_ref.shape[0]):
                    g = g + p_ref[k].astype(F32)
                g_ref[0] = g
                d_ref[0], nm_ref[0], nv_ref[0] = _adam_math(w_ref[0], g, m_ref[0], v_ref[0])

    tile = pl.BlockSpec((1, tr, cdim), lambda l, i: (l, i, 0))
    part = lambda own: pl.BlockSpec((parts[own].shape[0], tr, cdim), lambda l, i: (0, jnp.where(l == own, i, 0), 0))
    shp = jax.ShapeDtypeStruct(w.shape, F32)
    return pl.pallas_call(
        body, name=name, grid=(n_layers, r // tr), out_shape=(shp, shp, shp, shp),
        in_specs=[tile] + [part(l) for l in range(n_layers)] + [tile, tile],
        out_specs=(tile, tile, tile, tile),
        compiler_params=_params(("arbitrary", "arbitrary"), 40),
    )(w, *parts, m, v)


def _row(vec, width):
    vec = vec.reshape(1, -1)
    return jnp.pad(vec, ((0, 0), (0, width - vec.shape[1])))


def kernel(x, c, w_ada, b_ada, g_pre, w_in, w_gate_up, b_gate_up, g_gla, g_dil, w_out, g_post, loss_target, m_w_ada, m_b_ada, m_g_pre, m_w_in, m_w_gate_up, m_b_gate_up, m_g_gla, m_g_dil, m_w_out, m_g_post, v_w_ada, v_b_ada, v_g_pre, v_w_in, v_w_gate_up, v_b_gate_up, v_g_gla, v_g_dil, v_w_out, v_g_post):
    px, py, pc = _my_position()
    me = _linear(px, py, pc)
    xs = x[0]
    target = loss_target[0]
    s_len = xs.shape[0]
    assert s_len % (DIL_BLOCK * max(DIL_DILATIONS) * 2) == 0 and xs.shape[1] == D_MODEL

    w_in_b, w_out_b = w_in.astype(BF16), w_out.astype(BF16)
    c_rows, wgu_all, w_in_all = _comm_call(
        "gather", [jnp.pad(c, ((0, 7), (0, 0))), w_gate_up.reshape(DEPTH * GLA_LOWRANK, GU_SHARD), w_in_b[0]],
        "gather_first")
    c_all = c_rows.reshape(N_DEV, 8, D_MODEL)[:, 0]
    mod_part = _mod_fwd(c_all, w_ada)
    w_new, mod_all = _w_in_to_kernel([w_in_all.reshape(N_DEV, D_MODEL, W_IN_SHARD)],
                                     comm=("gather", [mod_part.reshape(DEPTH * N_DEV, ADA_SHARD)]))
    mod_all = mod_all.reshape(N_DEV, DEPTH, N_DEV, ADA_SHARD)
    mod_mine = lax.dynamic_index_in_dim(mod_all, me, axis=2, keepdims=False)
    mod = jnp.transpose(mod_mine, (1, 0, 2)).reshape(DEPTH, 3 * D_MODEL) + b_ada
    wgu_full = jnp.transpose(wgu_all.reshape(N_DEV, DEPTH, GLA_LOWRANK, GU_SHARD), (1, 2, 0, 3)).reshape(
        DEPTH, GLA_LOWRANK, GU_COLS)
    wgu_pad = jnp.pad(wgu_full, ((0, 0), (0, LANE - GLA_LOWRANK), (0, 0))).astype(BF16)

    cos, sin_signed = _rope_tables(s_len)
    g_heads = jnp.concatenate([g_gla, g_dil], axis=1)

    saved = []
    xl = xs
    for l in range(DEPTH):
        shift, scale, gate = ((mod, l, k) for k in range(3))
        if l > 0:
            w_new = _w_in_to_kernel([half.reshape(N_DEV, D_MODEL // 2, W_IN_SHARD) for half in w_in_halves])[0]
        if l + 1 < DEPTH:
            own = [] if l > 0 else [w_out_b[0]]
            pf, pb, h, *arrived = _prenorm_proj(xl, (g_pre, l, 0), scale, shift, w_new, cos, sin_signed,
                                                comm=("gather", own + [w_in_b[l + 1, :D_MODEL // 2]]))
            w_out_l = arrived[0] if l == 0 else w_out_next
            top = arrived[-1]
        else:
            pf, pb, h = _prenorm_proj(xl, (g_pre, l, 0), scale, shift, w_new, cos, sin_signed)
            w_out_l = w_out_next
        o_a, states = _gla_fwd(pf, pb, wgu_pad, b_gate_up, l)
        if l + 1 < DEPTH:
            o_b, lse, bottom, w_out_next = _dil_fwd(pf, pb, comm=("gather", [w_in_b[l + 1, D_MODEL // 2:],
                                                                             w_out_b[l + 1]]))
            w_in_halves = (top, bottom)
        else:
            o_b, lse = _dil_fwd(pf, pb)
        if l + 1 < DEPTH:
            x_next, u = _post_fwd(o_a, o_b, pf, (g_heads, l, 0), w_out_l, xl, gate, (g_post, l, 0))
        else:
            dx, u, loss_part = _post_fwd(o_a, o_b, pf, (g_heads, l, 0), w_out_l, xl, gate, (g_post, l, 0),
                                         target=target)
        saved.append((xl, scale, gate, w_new, w_out_l, pf, pb, h, o_a, states, o_b, lse, u))
        xl = x_next

    small_rows = []
    gin_slots, gin_parts, gout_parts = None, [None] * DEPTH, [None] * DEPTH
    for l in reversed(range(DEPTH)):
        x_in, scale, gate, w_new, w_out_l, pf, pb, h, o_a, states, o_b, lse, u = saved[l]
        do, dz, sums_post, gout_slots = _post_bwd(dx, u, gate, (g_post, l, 0), w_out_l, o_a, o_b, pf, (g_heads, l, 0))
        dq_a, dk_a, dv_a, dlr2, dwgu, dbgu, arrived = _gla_bwd(pf, pb, wgu_pad, b_gate_up, l, states, do,
                                                               comm=("exchange", [gout_slots]))
        gout_parts[l] = arrived.reshape(N_DEV, OUT_SHARD, D_MODEL)
        if gin_slots is not None:
            dq_b, dk_b, dv_b, arrived, _, _ = _dil_bwd(pf, pb, cos, sin_signed, do, o_b, lse,
                                                       comm=("pairsum_exchange", [gin_slots]))
            gin_parts[l + 1] = arrived.reshape(N_DEV // 2, D_MODEL, W_IN_SHARD)
        else:
            dq_b, dk_b, dv_b = _dil_bwd(pf, pb, cos, sin_signed, do, o_b, lse)
        dlr = (dlr2[0] + dlr2[1]).astype(BF16)
        pieces = (dz, dq_a, dk_a, dq_b, dk_b, dv_a, dv_b, dlr)
        gin_slots = _grad_w_in(h, pieces).reshape(N_DEV * D_MODEL, W_IN_SHARD)
        if l == 0:
            dx, sums_in, arrived, _, _ = _in_bwd(pieces, w_new, x_in, dx, (g_pre, l, 0), scale,
                                                 comm=("pairsum_exchange", [gin_slots]))
            gin_parts[0] = arrived.reshape(N_DEV // 2, D_MODEL, W_IN_SHARD)
        else:
            dx, sums_in = _in_bwd(pieces, w_new, x_in, dx, (g_pre, l, 0), scale)
        dmod = jnp.concatenate([sums_in[0], sums_in[1], sums_post[0]])
        vecs = jnp.concatenate([sums_in[2], sums_post[1], sums_post[2], dbgu[0]])
        small_rows[0:0] = [_row(dmod, 4096), _row(vecs, 4096), _row(dwgu[:GLA_LOWRANK], 4096)]
    grad_x = dx[None]

    flat = lambda a, rows: a.reshape(rows, a.shape[-1])
    r_ada = DEPTH * D_MODEL
    g_w_in, d_w_in, nm_w_in, nv_w_in = _adamw_layers(w_in, gin_parts, m_w_in, v_w_in, "adamw_w_in", 256)
    g_w_out, d_w_out, nm_w_out, nv_w_out = _adamw_layers(w_out, gout_parts, m_w_out, v_w_out, "adamw_w_out", 128)

    small_rows += [_row(loss_part[0, 0:1], 4096), jnp.zeros((1, 4096), F32)]
    small = _all_gather(jnp.concatenate(small_rows, axis=0), "gather_small").reshape(N_DEV, 8, 4096)
    dmod_all = jnp.stack([small[:, 0, :3 * D_MODEL], small[:, 3, :3 * D_MODEL]])
    dmod_cols = lax.dynamic_slice_in_dim(dmod_all, me * ADA_SHARD, ADA_SHARD, axis=2)
    gwa = _w_ada_grad(c_all, dmod_cols).reshape(1, r_ada, ADA_SHARD)
    g_w_ada, d_w_ada, nm_w_ada, nv_w_ada = (
        t.reshape(w_ada.shape) for t in _adamw(flat(w_ada, r_ada), gwa, flat(m_w_ada, r_ada), flat(v_w_ada, r_ada),
                                               "adamw_w_ada", 256))

    where = ((0, 0), (1, 0), (1, 1024), (1, 2048), (1, 2560), (1, 3072))
    replicated = [(b_ada, m_b_ada, v_b_ada), (g_pre, m_g_pre, v_g_pre), (g_post, m_g_post, v_g_post),
                  (g_gla, m_g_gla, v_g_gla), (g_dil, m_g_dil, v_g_dil), (b_gate_up, m_b_gate_up, v_b_gate_up)]
    updated, loss = _adamw_replicated(small, replicated, where, loss_at=(6, 0))
    ((g_b_ada, d_b_ada, nm_b_ada, nv_b_ada), (g_g_pre, d_g_pre, nm_g_pre, nv_g_pre),
     (g_g_post, d_g_post, nm_g_post, nv_g_post), (g_g_gla, d_g_gla, nm_g_gla, nv_g_gla),
     (g_g_dil, d_g_dil, nm_g_dil, nv_g_dil), (g_b_gu, d_b_gu, nm_b_gu, nv_b_gu)) = updated
    gu_parts = jnp.stack([small[:, 2], small[:, 5]], axis=1).reshape(N_DEV, DEPTH, GLA_LOWRANK, GU_COLS)
    gu_parts = lax.dynamic_slice_in_dim(gu_parts, me * GU_SHARD, GU_SHARD, axis=3).reshape(
        N_DEV, DEPTH * GLA_LOWRANK, GU_SHARD)
    r_gu = DEPTH * GLA_LOWRANK
    g_w_gu, d_w_gu, nm_w_gu, nv_w_gu = (
        t.reshape(w_gate_up.shape) for t in _adamw(flat(w_gate_up, r_gu), gu_parts, flat(m_w_gate_up, r_gu),
                                                   flat(v_w_gate_up, r_gu), "adamw_w_gate_up", r_gu))
    return (loss, grad_x,
            g_w_ada, g_b_ada, g_g_pre, g_w_in, g_w_gu, g_b_gu, g_g_gla, g_g_dil, g_w_out, g_g_post,
            d_w_ada, d_b_ada, d_g_pre, d_w_in, d_w_gu, d_b_gu, d_g_gla, d_g_dil, d_w_out, d_g_post,
            nm_w_ada, nm_b_ada, nm_g_pre, nm_w_in, nm_w_gu, nm_b_gu, nm_g_gla, nm_g_dil, nm_w_out, nm_g_post,
            nv_w_ada, nv_b_ada, nv_g_pre, nv_w_in, nv_w_gu, nv_b_gu, nv_g_gla, nv_g_dil, nv_w_out, nv_g_post)


def _adamw_replicated(small, params, where, loss_at):
    n_parts = small.shape[0]

    def body(*refs):
        s_ref, p_refs, o_refs = refs[0], refs[1:1 + 3 * len(params)], refs[1 + 3 * len(params):]
        total = s_ref[0]
        for k in range(1, n_parts):
            total = total + s_ref[k]
        for i, (row, col) in enumerate(where):
            w_ref, m_ref, v_ref = p_refs[3 * i:3 * i + 3]
            n = w_ref.shape[1]
            g = jnp.concatenate([total[row + 3 * l:row + 3 * l + 1, col:col + n] for l in range(DEPTH)], axis=0)
            o_refs[4 * i][...] = g
            o_refs[4 * i + 1][...], o_refs[4 * i + 2][...], o_refs[4 * i + 3][...] = _adam_math(
                w_ref[...], g, m_ref[...], v_ref[...])
        o_refs[-1][...] = jnp.broadcast_to(total[loss_at[0]:loss_at[0] + 1, loss_at[1]:loss_at[1] + 1], (8, LANE))

    flat = [a for p in params for a in p]
    shapes = [jax.ShapeDtypeStruct(p[0].shape, F32) for p in params for _ in range(4)]
    outs = pl.pallas_call(body, name="adamw_replicated",
                          out_shape=shapes + [jax.ShapeDtypeStruct((8, LANE), F32)])(small, *flat)
    return [tuple(outs[4 * i:4 * i + 4]) for i in range(len(params))], outs[-1][0, 0]
```

```python
import functools
import math

import jax
import jax.numpy as jnp
from jax import lax
from jax.experimental import pallas as pl
from jax.experimental.pallas import tpu as pltpu

F32 = jnp.float32
BF16 = jnp.bfloat16

N_DEV = 8
D_MODEL = 1024
DEPTH = 2
GLA_HEADS = 4
GLA_DK = 64
GLA_DV = 128
GLA_CHUNK = 64
GLA_TAU = 16.0
GLA_LOWRANK = 16
DIL_HEADS = 4
DIL_HD = 128
DIL_BLOCK = 128
DIL_DILATIONS = (1, 4, 16)
ROPE_THETA = 10000.0
EPS = 1e-6
IN_COLS = 3600
W_IN_SHARD = IN_COLS // N_DEV
ADA_SHARD = 3 * D_MODEL // N_DEV
OUT_SHARD = D_MODEL // N_DEV
GU_COLS = GLA_HEADS * GLA_DK
GU_SHARD = GU_COLS // N_DEV

ADAM_LR = 0.001
ADAM_B1 = 0.9
ADAM_B2 = 0.999
ADAM_EPS = 1e-08
ADAM_WD = 0.01
ADAM_STEP = 10

NP = 3712
COL_Z, COL_QA, COL_KA, COL_QB, COL_KB, COL_VA, COL_VB, COL_LR = 0, 1024, 1280, 1536, 2048, 2560, 3072, 3584
NP_F32 = COL_VA
NP_BF16 = NP - NP_F32
LANE = 128
MASK_VALUE = -1e30

MESH = pl.DeviceIdType.MESH
ANY = pl.BlockSpec(memory_space=pl.ANY)


def _params(sem=None, vmem_mb=None):
    kw = {}
    if sem is not None:
        kw["dimension_semantics"] = sem
    if vmem_mb is not None:
        kw["vmem_limit_bytes"] = vmem_mb * 1024 * 1024
    return pltpu.CompilerParams(**kw)


def _dot(a, b):
    return jnp.dot(a, b, preferred_element_type=F32)


def _dot_nt(a, b):
    return lax.dot_general(a, b, (((1,), (1,)), ((), ())), preferred_element_type=F32)


def _dot_tn(a, b):
    return lax.dot_general(a, b, (((0,), (0,)), ((), ())), preferred_element_type=F32)


def _sigmoid(z):
    return 1.0 / (1.0 + jnp.exp(-z))


def _log_sigmoid(z):
    return jnp.minimum(z, 0.0) - jnp.log(1.0 + jnp.exp(-jnp.abs(z)))


def _rowvec(v, width=D_MODEL):
    arr, row, cb = v
    return arr.reshape(arr.shape[0], 1, arr.shape[1]), pl.BlockSpec((None, 1, width), lambda *_: (row, 0, cb))


def _my_position():
    return lax.axis_index("x"), lax.axis_index("y"), lax.axis_index("c")


def _linear(px, py, pc):
    return 4 * px + 2 * py + pc


def _gather_phase(phase, x_ref, out_ref, send_sem, recv_sem, local_sem):
    m = x_ref.shape[0]
    x, y, c = _my_position()
    me, sibling = (x, y, c), (x, y, 1 - c)
    chips = [(1 - x, y), (x, 1 - y), (1 - x, 1 - y)]

    def rows(px, py, pc):
        return out_ref.at[pl.ds(_linear(px, py, pc) * m, m), :]

    def copy(k, block, to, src=None):
        return pltpu.make_async_remote_copy(
            src_ref=rows(*block) if src is None else src, dst_ref=rows(*block),
            send_sem=send_sem(k), recv_sem=recv_sem(k), device_id=to, device_id_type=MESH)

    mine = pltpu.make_async_copy(x_ref, rows(*me), local_sem)
    first = [copy(0, me, sibling, src=x_ref)] + [copy(1 + j, me, (*chip, c), src=x_ref) for j, chip in enumerate(chips)]
    passed = [copy(4 + j, (*chip, c), sibling) for j, chip in enumerate(chips)]
    if phase == "start":
        mine.start()
        for cp in first:
            cp.start()
    elif phase == "forward":
        for j, chip in enumerate(chips):
            copy(1 + j, (*chip, c), me).wait_recv()
            passed[j].start()
    else:
        copy(0, sibling, me).wait_recv()
        for j, chip in enumerate(chips):
            copy(4 + j, (*chip, 1 - c), me).wait_recv()
        for cp in first + passed:
            cp.wait_send()
        mine.wait()


def _exchange_phase(phase, x_ref, out_ref, send_sem, recv_sem, local_sem):
    m = x_ref.shape[0] // N_DEV
    x, y, c = _my_position()
    me = _linear(x, y, c)

    def rows(ref, idx):
        return ref.at[pl.ds(idx * m, m), :]

    peers = [(1 - x if j & 4 else x, 1 - y if j & 2 else y, 1 - c if j & 1 else c) for j in range(1, N_DEV)]
    local = pltpu.make_async_copy(rows(x_ref, me), rows(out_ref, me), local_sem)
    sends = [pltpu.make_async_remote_copy(
        src_ref=rows(x_ref, _linear(*peer)), dst_ref=rows(out_ref, me),
        send_sem=send_sem(j), recv_sem=recv_sem(j), device_id=peer, device_id_type=MESH) for j, peer in enumerate(peers)]
    if phase == "start":
        local.start()
        for cp in sends:
            cp.start()
    else:
        for j, peer in enumerate(peers):
            pltpu.make_async_remote_copy(
                src_ref=rows(x_ref, _linear(*peer)), dst_ref=rows(out_ref, _linear(*peer)),
                send_sem=send_sem(j), recv_sem=recv_sem(j), device_id=peer, device_id_type=MESH).wait_recv()
        for cp in sends:
            cp.wait_send()
        local.wait()


def _pairsum_exchange_phase(phase, x_ref, out_refs, send_sem, recv_sem, local_sem):
    out_ref, stage_ref, pair_ref = out_refs
    m, n = x_ref.shape[0] // N_DEV, x_ref.shape[1]
    x, y, c = _my_position()
    mine = 2 * x + y
    chips = [(qx, qy) for qx in range(2) for qy in range(2)]
    others = [(1 - x, y), (x, 1 - y), (1 - x, 1 - y)]

    def rows(ref, idx):
        return ref.at[pl.ds(idx * m, m), :]

    def remote(src, dst, k, to):
        return pltpu.make_async_remote_copy(src_ref=src, dst_ref=dst, send_sem=send_sem(k), recv_sem=recv_sem(k),
                                            device_id=to, device_id_type=MESH)

    to_sibling = [remote(rows(x_ref, _linear(qx, qy, 1 - c)), rows(stage_ref, q), q, (x, y, 1 - c))
                  for q, (qx, qy) in enumerate(chips)]
    to_chips = [remote(rows(pair_ref, 2 * qx + qy), rows(out_ref, mine), 4 + j, (qx, qy, c))
                for j, (qx, qy) in enumerate(others)]
    keep = pltpu.make_async_copy(rows(pair_ref, mine), rows(out_ref, mine), local_sem)
    if phase == "start":
        for cp in to_sibling:
            cp.start()
    elif phase == "reduce":
        for cp in to_sibling:
            cp.wait_recv()

        def through_vmem(a_buf, b_buf, sems):
            tr = 128
            loads = [(pltpu.make_async_copy(rows(x_ref, _linear(qx, qy, c)), a_buf.at[q % 2], sems.at[q % 2]),
                      pltpu.make_async_copy(rows(stage_ref, q), b_buf.at[q % 2], sems.at[2 + q % 2]))
                     for q, (qx, qy) in enumerate(chips)]
            stores = [pltpu.make_async_copy(a_buf.at[q % 2], rows(pair_ref, q), sems.at[4 + q % 2]) for q in range(4)]
            for cp in loads[0]:
                cp.start()
            for q in range(4):
                for cp in loads[q]:
                    cp.wait()
                if q + 1 < 4:
                    if q >= 1:
                        stores[q - 1].wait()
                    for cp in loads[q + 1]:
                        cp.start()

                def add(r, carry, q=q):
                    tile = pl.ds(pl.multiple_of(r * tr, tr), tr)
                    a_buf[q % 2, tile, :] = (a_buf[q % 2, tile, :].astype(F32)
                                             + b_buf[q % 2, tile, :].astype(F32)).astype(x_ref.dtype)
                    return carry

                lax.fori_loop(0, m // tr, add, 0)
                stores[q].start()
            stores[2].wait()
            stores[3].wait()

        pl.run_scoped(through_vmem, pltpu.VMEM((2, m, n), x_ref.dtype), pltpu.VMEM((2, m, n), x_ref.dtype),
                      pltpu.SemaphoreType.DMA((6,)))
    elif phase == "send":
        keep.start()
        for cp in to_chips:
            cp.start()
    else:
        for j, (qx, qy) in enumerate(others):
            remote(rows(pair_ref, mine), rows(out_ref, 2 * qx + qy), 4 + j, (qx, qy, c)).wait_recv()
        for cp in to_sibling + to_chips:
            cp.wait_send()
        keep.wait()


_COMM_PHASES = {"gather": (_gather_phase, ("start", "forward", "finish")),
                "exchange": (_exchange_phase, ("start", "finish")),
                "pairsum_exchange": (_pairsum_exchange_phase, ("start", "reduce", "send", "finish"))}


def _comm_scratch(n_arrays):
    return [pltpu.SemaphoreType.DMA((n_arrays, 7)), pltpu.SemaphoreType.DMA((n_arrays, 7)),
            pltpu.SemaphoreType.DMA((n_arrays,))]


def _comm_run(kind, phases, x_refs, out_refs, send_sems, recv_sems, local_sems):
    fn = _COMM_PHASES[kind][0]
    per = len(out_refs) // len(x_refs)
    for phase in phases:
        for a, x_ref in enumerate(x_refs):
            outs = out_refs[a] if per == 1 else tuple(out_refs[per * a:per * (a + 1)])
            fn(phase, x_ref, outs, lambda k, a=a: send_sems.at[a, k], lambda k, a=a: recv_sems.at[a, k],
               local_sems.at[a])


def _comm_out_shapes(kind, arrays):
    if kind == "pairsum_exchange":
        return [jax.ShapeDtypeStruct((a.shape[0] // 2, a.shape[1]), a.dtype) for a in arrays for _ in range(3)]
    return [jax.ShapeDtypeStruct((N_DEV * a.shape[0], a.shape[1]) if kind == "gather" else a.shape, a.dtype)
            for a in arrays]


def _comm_call(kind, arrays, name):
    n = len(arrays)
    shapes = _comm_out_shapes(kind, arrays)

    def body(*refs):
        _comm_run(kind, _COMM_PHASES[kind][1], refs[:n], refs[n:n + len(shapes)], *refs[n + len(shapes):])

    return pl.pallas_call(body, name=name, out_shape=shapes, in_specs=[ANY] * n, out_specs=[ANY] * len(shapes),
                          scratch_shapes=_comm_scratch(n))(*arrays)


def _all_gather(xs, name):
    return _comm_call("gather", [xs], name)[0]


def _mod_fwd(c_all, w_ada):
    def body(c_ref, w_ref, o_ref):
        cv = c_ref[...]
        sc = cv * _sigmoid(cv)
        o_ref[0] = _dot(sc.astype(BF16), w_ref[0].astype(BF16))

    return pl.pallas_call(
        body, name="mod_fwd", grid=(DEPTH,),
        out_shape=jax.ShapeDtypeStruct((DEPTH, N_DEV, ADA_SHARD), F32),
        in_specs=[pl.BlockSpec((N_DEV, D_MODEL), lambda l: (0, 0)),
                  pl.BlockSpec((1, D_MODEL, ADA_SHARD), lambda l: (l, 0, 0))],
        out_specs=pl.BlockSpec((1, N_DEV, ADA_SHARD), lambda l: (l, 0, 0)),
        compiler_params=_params(("arbitrary",)),
    )(c_all, w_ada)


def _w_ada_grad(c_all, dmod_cols):
    def body(c_ref, d_ref, o_ref):
        cv = c_ref[...]
        sc = cv * _sigmoid(cv)
        o_ref[0] = lax.dot_general(sc, d_ref[0], (((0,), (0,)), ((), ())), precision=lax.Precision.HIGHEST,
                                   preferred_element_type=F32)

    return pl.pallas_call(
        body, name="w_ada_grad", grid=(DEPTH,),
        out_shape=jax.ShapeDtypeStruct((DEPTH, D_MODEL, ADA_SHARD), F32),
        in_specs=[pl.BlockSpec((N_DEV, D_MODEL), lambda l: (0, 0)),
                  pl.BlockSpec((1, N_DEV, ADA_SHARD), lambda l: (l, 0, 0))],
        out_specs=pl.BlockSpec((1, D_MODEL, ADA_SHARD), lambda l: (l, 0, 0)),
        compiler_params=_params(("arbitrary",)),
    )(c_all, dmod_cols)


def _comm_plumbing(comm):
    if not comm:
        return 0, [], []
    return len(comm[1]), _comm_out_shapes(*comm), _comm_scratch(len(comm[1]))


def _split_refs(refs, n_in, n_out, n_scratch, comm):
    ci, shapes, _ = _comm_plumbing(comm)
    co = len(shapes)
    a, b, c = n_in + ci, n_in + ci + n_out, n_in + ci + n_out + co
    return refs[:n_in], refs[a:b], refs[c:c + n_scratch], refs[n_in:a], refs[b:c], refs[c + n_scratch:]


def _prenorm_proj(x, g_pre, scale, shift, w_new, cos, sin_signed, comm=None, ts=256):
    s_len = x.shape[0]
    n_cin, c_shapes, c_scratch = _comm_plumbing(comm)

    def body(*refs):
        (x_ref, g_ref, sc_ref, sh_ref, w_ref, cos_ref, sin_ref), (pf_ref, pb_ref, h_ref), _, cin, cout, csem = (
            _split_refs(refs, 7, 3, 0, comm))
        comm_before, comm_after = _comm_hooks(comm, cin, cout, csem, steps=s_len // ts)
        comm_before()
        xv = x_ref[...]
        rstd = lax.rsqrt(jnp.mean(xv * xv, axis=-1, keepdims=True) + EPS)
        h = (xv * rstd * g_ref[...]) * (1.0 + sc_ref[...]) + sh_ref[...]
        hb = h.astype(BF16)
        h_ref[...] = hb
        for j in range(0, NP, 512):
            w = min(512, NP - j)
            acc = _dot(hb, w_ref[:, j:j + w])
            if COL_QB <= j < COL_VA:
                for lo in range(0, w, DIL_HD):
                    pf_ref[:, j + lo:j + lo + DIL_HD] = _rope(acc[:, lo:lo + DIL_HD], cos_ref[...], sin_ref[...])
            elif j < NP_F32:
                pf_ref[:, j:j + w] = acc
            else:
                pb_ref[:, j - NP_F32:j - NP_F32 + w] = acc.astype(BF16)
        comm_after()

    (g_pre, g_spec), (scale, sc_spec), (shift, sh_spec) = _rowvec(g_pre), _rowvec(scale), _rowvec(shift)
    return pl.pallas_call(
        body, name="prenorm_proj_comm" if comm else "prenorm_proj", grid=(s_len // ts,),
        out_shape=[jax.ShapeDtypeStruct((s_len, NP_F32), F32), jax.ShapeDtypeStruct((s_len, NP_BF16), BF16),
                   jax.ShapeDtypeStruct((s_len, D_MODEL), BF16)] + c_shapes,
        in_specs=[pl.BlockSpec((ts, D_MODEL), lambda i: (i, 0)), g_spec, sc_spec, sh_spec,
                  pl.BlockSpec((D_MODEL, NP), lambda i: (0, 0)), pl.BlockSpec((ts, DIL_HD), lambda i: (i, 0)),
                  pl.BlockSpec((ts, DIL_HD), lambda i: (i, 0))] + [ANY] * n_cin,
        out_specs=[pl.BlockSpec((ts, NP_F32), lambda i: (i, 0)), pl.BlockSpec((ts, NP_BF16), lambda i: (i, 0)),
                   pl.BlockSpec((ts, D_MODEL), lambda i: (i, 0))] + [ANY] * len(c_shapes),
        scratch_shapes=c_scratch,
        compiler_params=_params(("arbitrary",), 48),
    )(x, g_pre, scale, shift, w_new, cos, sin_signed, *(comm[1] if comm else []))


GLA_GROUP = 16


def _gla_group_rows(t):
    return [pl.ds(pl.multiple_of((t * GLA_GROUP + j) * GLA_CHUNK, GLA_CHUNK), GLA_CHUNK) for j in range(GLA_GROUP)]


def _gla_chunks_common(q_ref, k_ref, lr_ref, wgu_ref, bgu_ref, rows_list):
    c = GLA_CHUNK
    ri = lax.broadcasted_iota(jnp.int32, (c, c), 0)
    ci = lax.broadcasted_iota(jnp.int32, (c, c), 1)
    tril = (ri >= ci).astype(F32)
    zs = [_dot(lr_ref[rows, :], wgu_ref[...]) + bgu_ref[...] for rows in rows_list]
    las = [_log_sigmoid(z) * (1.0 / GLA_TAU) for z in zs]
    bs = [jnp.dot(tril, la, precision=lax.Precision.HIGHEST, preferred_element_type=F32) for la in las]
    out = []
    for rows, z, b in zip(rows_list, zs, bs):
        q = q_ref[rows, :] * (GLA_DK ** -0.5)
        k = k_ref[rows, :]
        bl = b[c - 1:c, :]
        out.append(dict(z=z, b=b, bl=bl, qe=q * jnp.exp(b), ke=k * jnp.exp(-b), kend=k * jnp.exp(bl - b),
                        dec=jnp.exp(bl)))
    return out, ri, ci


def _head_lane_mask(hh):
    return (lax.broadcasted_iota(jnp.int32, (1, LANE), 1) // GLA_DK) == hh


def _state_block_mask():
    r = lax.broadcasted_iota(jnp.int32, (2 * GLA_DV, LANE), 0) // GLA_DV
    cc = lax.broadcasted_iota(jnp.int32, (2 * GLA_DV, LANE), 1) // GLA_DK
    return r == cc


def _gla_fwd(pf, pb, wgu, bgu, layer, comm=None):
    s_len = pf.shape[0]
    nc = s_len // GLA_CHUNK
    ncomm = len(comm[1]) if comm else 0

    def body(*refs):
        q_ref, k_ref, v_ref, lr_ref, wgu_ref, bgu_ref = refs[:6]
        cin, (o_ref, st_ref), cout = refs[6:6 + ncomm], refs[6 + ncomm:8 + ncomm], refs[8 + ncomm:8 + 2 * ncomm]
        qe_s, cs_s, dec_s = refs[8 + 2 * ncomm:11 + 2 * ncomm]
        comm_before, comm_after = _comm_hooks(comm, cin, cout, refs[11 + 2 * ncomm:], steps=2)
        comm_before()
        bd = _state_block_mask()

        def local(t, carry):
            rows_list = _gla_group_rows(t)
            cm, ri, ci = _gla_chunks_common(q_ref, k_ref, lr_ref, wgu_ref, bgu_ref, rows_list)
            vs = [v_ref[rows, :] for rows in rows_list]
            kebs = [c["ke"].astype(BF16) for c in cm]
            a = [[jnp.where(ri >= ci, _dot_nt(jnp.where(_head_lane_mask(hh), c["qe"], 0.0).astype(BF16), keb), 0.0)
                  .astype(BF16) for hh in range(2)] for c, keb in zip(cm, kebs)]
            oi = [[_dot(ah[hh], v[:, hh * GLA_DV:(hh + 1) * GLA_DV]) for hh in range(2)] for ah, v in zip(a, vs)]
            cs = [jnp.where(bd, _dot_tn(v, c["kend"].astype(BF16)), 0.0) for c, v in zip(cm, vs)]
            for j, (rows, c) in enumerate(zip(rows_list, cm)):
                n = t * GLA_GROUP + j
                o_ref[rows, :] = jnp.concatenate(oi[j], axis=1)
                qe_s[rows, :] = c["qe"].astype(BF16)
                cs_s[n] = cs[j]
                dec_s[n] = jnp.broadcast_to(c["dec"], (8, LANE))
            return carry

        lax.fori_loop(0, nc // GLA_GROUP, local, 0)

        def scan(n, st):
            st_ref[0, n] = st.astype(BF16)
            return dec_s[n][0:1, :] * st + cs_s[n]

        lax.fori_loop(0, nc, scan, jnp.zeros((2 * GLA_DV, LANE), F32))

        def inter(t, carry):
            rows_list = _gla_group_rows(t)
            add = [_dot_nt(qe_s[rows, :], st_ref[0, t * GLA_GROUP + j]) for j, rows in enumerate(rows_list)]
            for rows, av in zip(rows_list, add):
                o_ref[rows, :] = o_ref[rows, :] + av
            return carry

        lax.fori_loop(0, nc // GLA_GROUP, inter, 0)
        comm_after()

    return pl.pallas_call(
        body, name="gla_fwd_comm" if comm else "gla_fwd", grid=(2,),
        out_shape=[jax.ShapeDtypeStruct((s_len, GLA_HEADS * GLA_DV), F32),
                   jax.ShapeDtypeStruct((2, nc, 2 * GLA_DV, LANE), BF16)] + (_comm_out_shapes(*comm) if comm else []),
        in_specs=[pl.BlockSpec((s_len, LANE), lambda g: (0, COL_QA // LANE + g)),
                  pl.BlockSpec((s_len, LANE), lambda g: (0, COL_KA // LANE + g)),
                  pl.BlockSpec((s_len, 2 * GLA_DV), lambda g: (0, (COL_VA - NP_F32) // (2 * GLA_DV) + g)),
                  pl.BlockSpec((s_len, LANE), lambda g: (0, (COL_LR - NP_F32) // LANE)),
                  pl.BlockSpec((None, LANE, LANE), lambda g: (layer, 0, g)),
                  pl.BlockSpec((None, 1, LANE), lambda g: (layer, 0, g))] + [ANY] * ncomm,
        out_specs=[pl.BlockSpec((s_len, 2 * GLA_DV), lambda g: (0, g)),
                   pl.BlockSpec((1, nc, 2 * GLA_DV, LANE), lambda g: (g, 0, 0, 0))] + [ANY] * ncomm,
        scratch_shapes=[pltpu.VMEM((s_len, LANE), BF16), pltpu.VMEM((nc, 2 * GLA_DV, LANE), F32),
                        pltpu.VMEM((nc, 8, LANE), F32)] + (_comm_scratch(ncomm) if comm else []),
        compiler_params=_params(("arbitrary",), 56),
    )(pf, pf, pb, pb, wgu, bgu.reshape(bgu.shape[0], 1, GU_COLS), *(comm[1] if comm else []))


def _rope_tables(s_len):
    inv_freq = ROPE_THETA ** (-jnp.arange(0, DIL_HD, 2, dtype=F32) / DIL_HD)
    ang = jnp.arange(s_len, dtype=F32)[:, None] * inv_freq[None, :]
    cos, sin = jnp.cos(ang), jnp.sin(ang)
    return jnp.concatenate([cos, cos], axis=1), jnp.concatenate([-sin, sin], axis=1)


def _rope(xv, cos, sin_signed):
    return xv * cos + pltpu.roll(xv, DIL_HD // 2, 1) * sin_signed


DIL_GROUP = 8


def _dil_pair_block(i, half, d, nblk, group=DIL_GROUP):
    nb = nblk // d
    j = i + half * (nblk // group)
    if nb >= 2 * group:
        r, n = j % d, j // d
    else:
        r, n = j // nb, j % nb
    kb = jnp.maximum(n - 1, 0)
    qs = r + d * DIL_BLOCK * n
    ks = r + d * DIL_BLOCK * kb
    return qs, ks, jnp.minimum(n, 1)


def _dil_fill_bias(bias):
    qi = lax.broadcasted_iota(jnp.int32, (DIL_BLOCK, 2 * DIL_BLOCK), 0)
    kj = lax.broadcasted_iota(jnp.int32, (DIL_BLOCK, 2 * DIL_BLOCK), 1)
    for sel in range(2):
        dist = qi - kj + DIL_BLOCK * sel
        bias[sel] = jnp.where((dist >= 0) & (dist <= DIL_BLOCK), 0.0, MASK_VALUE)


def _strided(start, size, d):
    return pl.ds(start, size) if d == 1 else pl.ds(start, size, stride=d)


def _comm_hooks(comm, cin, cout, csem, steps=DIL_HEADS):
    def before():
        if comm:
            @pl.when(pl.program_id(0) == 0)
            def _():
                _comm_run(comm[0], ("start",), cin, cout, *csem)

            if comm[0] == "gather":
                @pl.when(pl.program_id(0) == steps - 1)
                def _():
                    _comm_run(comm[0], ("forward",), cin, cout, *csem)

            if comm[0] == "pairsum_exchange":
                @pl.when(pl.program_id(0) == (1 if steps <= 4 else 2))
                def _():
                    _comm_run(comm[0], ("reduce", "send"), cin, cout, *csem)

    def after():
        if comm:
            @pl.when(pl.program_id(0) == steps - 1)
            def _():
                _comm_run(comm[0], ("finish",), cin, cout, *csem)

    return before, after


def _dil_fwd(pf, pb, comm=None):
    s_len = pf.shape[0]
    nblk = s_len // DIL_BLOCK
    prep_rows = 256
    scale = DIL_HD ** -0.5
    nc = len(comm[1]) if comm else 0

    def body(*refs):
        ((qf, kf, v_ref), (o_ref, lse_ref), (vf, o0, o1, o2, l0, l1, l2, bias), cin, cout, csem) = _split_refs(
            refs, 3, 2, 8, comm)
        comm_before, comm_after = _comm_hooks(comm, cin, cout, csem)
        comm_before()
        _dil_fill_bias(bias)

        def prep(t, carry):
            rows = pl.ds(pl.multiple_of(t * prep_rows, prep_rows), prep_rows)
            vf[rows, :] = v_ref[rows, :].astype(F32)
            return carry

        lax.fori_loop(0, s_len // prep_rows, prep, 0)
        for d, o_p, l_p in zip(DIL_DILATIONS, (o0, o1, o2), (l0, l1, l2)):
            if nblk // d == 2:
                units = DIL_GROUP // 2

                def whole(i, carry, d=d, o_p=o_p, l_p=l_p, units=units):
                    rows = [_strided(i + u * (d // units), 2 * DIL_BLOCK, d) for u in range(units)]
                    ld = [(qf[rw, :].astype(BF16), kf[rw, :].astype(BF16), vf[rw, :].astype(BF16)) for rw in rows]
                    both = bias[...].reshape(2 * DIL_BLOCK, 2 * DIL_BLOCK)
                    s = [_dot_nt(qb, kk) * scale + both for qb, kk, _ in ld]
                    m = [jnp.max(sv, axis=-1, keepdims=True) for sv in s]
                    p = [jnp.exp(sv - mv) for sv, mv in zip(s, m)]
                    den = [jnp.sum(pv, axis=-1, keepdims=True) for pv in p]
                    r = [_dot(pv.astype(BF16), vv) for pv, (_, _, vv) in zip(p, ld)]
                    for rv, dv, mv, rw in zip(r, den, m, rows):
                        o_p[rw, :] = rv / dv
                        l_p[rw, :] = jnp.broadcast_to(mv + jnp.log(dv), (2 * DIL_BLOCK, DIL_HD))
                    return carry

                lax.fori_loop(0, d // units, whole, 0)
                continue

            def pair(i, carry, d=d, o_p=o_p, l_p=l_p):
                idx = [_dil_pair_block(i, half, d, nblk, DIL_GROUP) for half in range(DIL_GROUP)]
                ld = [(qf[_strided(qs, DIL_BLOCK, d), :].astype(BF16),
                       kf[_strided(ks, 2 * DIL_BLOCK, d), :].astype(BF16),
                       vf[_strided(ks, 2 * DIL_BLOCK, d), :].astype(BF16)) for qs, ks, _ in idx]
                s = [_dot_nt(qb, kk) * scale + bias[sel] for (qb, kk, _), (_, _, sel) in zip(ld, idx)]
                m = [jnp.max(sv, axis=-1, keepdims=True) for sv in s]
                p = [jnp.exp(sv - mv) for sv, mv in zip(s, m)]
                den = [jnp.sum(pv, axis=-1, keepdims=True) for pv in p]
                r = [_dot(pv.astype(BF16), vv) for pv, (_, _, vv) in zip(p, ld)]
                for rv, dv, mv, (qs, _, _) in zip(r, den, m, idx):
                    o_p[_strided(qs, DIL_BLOCK, d), :] = rv / dv
                    l_p[_strided(qs, DIL_BLOCK, d), :] = jnp.broadcast_to(mv + jnp.log(dv), (DIL_BLOCK, DIL_HD))
                return carry

            lax.fori_loop(0, nblk // DIL_GROUP, pair, 0)

        def comb(t, carry):
            rows = pl.ds(pl.multiple_of(t * prep_rows, prep_rows), prep_rows)
            a0, a1, a2 = l0[rows, :], l1[rows, :], l2[rows, :]
            m = jnp.maximum(jnp.maximum(a0, a1), a2)
            e0, e1, e2 = jnp.exp(a0 - m), jnp.exp(a1 - m), jnp.exp(a2 - m)
            tot = e0 + e1 + e2
            o_ref[rows, :] = (e0 * o0[rows, :] + e1 * o1[rows, :] + e2 * o2[rows, :]) / tot
            lse_ref[rows, :] = m + jnp.log(tot)
            return carry

        lax.fori_loop(0, s_len // prep_rows, comb, 0)
        comm_after()

    head = lambda base: pl.BlockSpec((s_len, DIL_HD), lambda h: (0, base // DIL_HD + h))
    out = pl.BlockSpec((s_len, DIL_HD), lambda h: (0, h))
    shp = jax.ShapeDtypeStruct((s_len, DIL_HEADS * DIL_HD), F32)
    return pl.pallas_call(
        body, name="dil_fwd_comm" if comm else "dil_fwd", grid=(DIL_HEADS,),
        out_shape=[shp, shp] + (_comm_out_shapes(*comm) if comm else []),
        in_specs=[head(COL_QB), head(COL_KB), head(COL_VB - NP_F32)] + [ANY] * nc,
        out_specs=[out, out] + [ANY] * nc,
        scratch_shapes=[pltpu.VMEM((s_len, DIL_HD), F32) for _ in range(7)]
        + [pltpu.VMEM((2, DIL_BLOCK, 2 * DIL_BLOCK), F32)] + (_comm_scratch(nc) if comm else []),
        compiler_params=_params(("arbitrary",), 56),
    )(pf, pf, pb, *(comm[1] if comm else []))


def _silu_and_grad(z):
    sg = _sigmoid(z)
    return z * sg, sg * (1.0 + z * (1.0 - sg))


def _post_fwd(o_a, o_b, pf, g_heads, w_out, x, gate, g_post, target=None, ts=512):
    s_len = x.shape[0]
    half = GLA_HEADS * GLA_DV
    last = target is not None

    def body(*refs):
        oa_ref, ob_ref, z_ref, gh_ref, w_ref, x_ref, gate_ref, gp_ref = refs[:8]
        xo_ref, u_ref = refs[8 + last:10 + last]
        y_ref = refs[-1]
        for src, base in ((oa_ref, 0), (ob_ref, half)):
            for hh in range(4):
                lo = hh * LANE
                og = src[:, lo:lo + LANE]
                on = og * lax.rsqrt(jnp.mean(og * og, axis=-1, keepdims=True) + EPS)
                zg = z_ref[:, base + lo:base + lo + LANE].astype(F32)
                y_ref[:, base + lo:base + lo + LANE] = (on * gh_ref[:, base + lo:base + lo + LANE]
                                                        * (zg * _sigmoid(zg))).astype(BF16)
        u = _dot(y_ref[...], w_ref[...])
        u_ref[...] = u.astype(BF16)
        rstd = lax.rsqrt(jnp.mean(u * u, axis=-1, keepdims=True) + EPS)
        x_out = x_ref[...] + gate_ref[...] * (u * rstd * gp_ref[...])
        if last:
            t_ref, loss_ref = refs[8], refs[11]

            @pl.when(pl.program_id(0) == 0)
            def _():
                loss_ref[...] = jnp.zeros_like(loss_ref)

            e = x_out - t_ref[...]
            xo_ref[...] = e * (1.0 / D_MODEL)
            loss_ref[...] += 0.5 * jnp.sum(jnp.mean(e * e, axis=-1, keepdims=True))
        else:
            xo_ref[...] = x_out

    (g_heads, gh_spec), (gate, gate_spec), (g_post, gp_spec) = _rowvec(g_heads), _rowvec(gate), _rowvec(g_post)
    tile = pl.BlockSpec((ts, D_MODEL), lambda i: (i, 0))
    halft = pl.BlockSpec((ts, half), lambda i: (i, 0))
    return pl.pallas_call(
        body, name="post_fwd_loss" if last else "post_fwd", grid=(s_len // ts,),
        out_shape=[jax.ShapeDtypeStruct((s_len, D_MODEL), F32), jax.ShapeDtypeStruct((s_len, D_MODEL), BF16)]
        + ([jax.ShapeDtypeStruct((8, LANE), F32)] if last else []),
        in_specs=[halft, halft, tile, gh_spec, pl.BlockSpec((D_MODEL, D_MODEL), lambda i: (0, 0)), tile, gate_spec,
                  gp_spec] + ([tile] if last else []),
        out_specs=[tile, tile] + ([pl.BlockSpec((8, LANE), lambda i: (0, 0))] if last else []),
        scratch_shapes=[pltpu.VMEM((ts, D_MODEL), BF16)],
        compiler_params=_params(("arbitrary",), 40),
    )(o_a, o_b, pf, g_heads, w_out, x, gate, g_post, *([target] if last else []))


def _post_bwd(dxo, u, gate, g_post, w_out, o_a, o_b, pf, g_heads, ts=512):
    s_len = dxo.shape[0]
    half = GLA_HEADS * GLA_DV
    steps = s_len // ts

    def body(dx_ref, u_ref, gate_ref, gp_ref, w_ref, oa_ref, ob_ref, z_ref, gh_ref, do_ref, dz_ref, sums_ref, gw_ref,
             y_s, acc):
        @pl.when(pl.program_id(0) == 0)
        def _():
            sums_ref[...] = jnp.zeros_like(sums_ref)
            acc[...] = jnp.zeros_like(acc)

        dx = dx_ref[...]
        u = u_ref[...].astype(F32)
        rstd = lax.rsqrt(jnp.mean(u * u, axis=-1, keepdims=True) + EPS)
        un = u * rstd
        sums_ref[0:1, :] += jnp.sum(dx * (un * gp_ref[...]), axis=0, keepdims=True)
        drn = dx * gate_ref[...]
        sums_ref[1:2, :] += jnp.sum(drn * un, axis=0, keepdims=True)
        dun = drn * gp_ref[...]
        du = rstd * (dun - un * jnp.mean(dun * un, axis=-1, keepdims=True))
        dub = du.astype(BF16)
        dy = _dot_nt(dub, w_ref[...])
        for src, base in ((oa_ref, 0), (ob_ref, half)):
            for hh in range(4):
                lo = base + hh * LANE
                og = src[:, hh * LANE:(hh + 1) * LANE]
                rs = lax.rsqrt(jnp.mean(og * og, axis=-1, keepdims=True) + EPS)
                on = og * rs
                zg = z_ref[:, lo:lo + LANE].astype(F32)
                sz, dsz = _silu_and_grad(zg)
                gg = gh_ref[:, lo:lo + LANE]
                dyg = dy[:, lo:lo + LANE]
                y_s[:, lo:lo + LANE] = (on * gg * sz).astype(BF16)
                sums_ref[2:3, lo:lo + LANE] += jnp.sum(dyg * sz * on, axis=0, keepdims=True)
                dz_ref[:, lo:lo + LANE] = (dyg * on * gg * dsz).astype(BF16)
                don = dyg * gg * sz
                do_ref[:, lo:lo + LANE] = (rs * (don - on * jnp.mean(don * on, axis=-1, keepdims=True))).astype(BF16)
        acc[...] += _dot_tn(y_s[...], dub)

        @pl.when(pl.program_id(0) == steps - 1)
        def _():
            gw_ref[...] = acc[...].astype(BF16)

    (g_heads, gh_spec), (gate, gate_spec), (g_post, gp_spec) = _rowvec(g_heads), _rowvec(gate), _rowvec(g_post)
    tile = pl.BlockSpec((ts, D_MODEL), lambda i: (i, 0))
    halft = pl.BlockSpec((ts, half), lambda i: (i, 0))
    whole = pl.BlockSpec((D_MODEL, D_MODEL), lambda i: (0, 0))
    return pl.pallas_call(
        body, name="post_bwd", grid=(steps,),
        out_shape=(jax.ShapeDtypeStruct((s_len, D_MODEL), BF16), jax.ShapeDtypeStruct((s_len, D_MODEL), BF16),
                   jax.ShapeDtypeStruct((8, D_MODEL), F32), jax.ShapeDtypeStruct((D_MODEL, D_MODEL), BF16)),
        in_specs=[tile, tile, gate_spec, gp_spec, whole, halft, halft, tile, gh_spec],
        out_specs=(tile, tile, pl.BlockSpec((8, D_MODEL), lambda i: (0, 0)), whole),
        scratch_shapes=[pltpu.VMEM((ts, D_MODEL), BF16), pltpu.VMEM((D_MODEL, D_MODEL), F32)],
        compiler_params=_params(("arbitrary",), 48),
    )(dxo, u, gate, g_post, w_out, o_a, o_b, pf, g_heads)


def _gla_bwd(pf, pb, wgu, bgu, layer, states, do, comm=None):
    s_len = pf.shape[0]
    nc = s_len // GLA_CHUNK
    c = GLA_CHUNK
    n_cin, c_shapes, c_scratch = _comm_plumbing(comm)

    def body(*refs):
        ((q_ref, k_ref, v_ref, lr_ref, wgu_ref, bgu_ref, st_ref, do_ref),
         (dq_ref, dk_ref, dv_ref, dlr_ref, dwgu_ref, dbgu_ref), (ds_s, dec_s, dw_acc, db_acc),
         cin, cout, csem) = _split_refs(refs, 8, 6, 4, comm)
        comm_before, comm_after = _comm_hooks(comm, cin, cout, csem, steps=2)
        comm_before()
        dw_acc[...] = jnp.zeros_like(dw_acc)
        db_acc[...] = jnp.zeros_like(db_acc)
        bd = _state_block_mask()
        last_row = lax.broadcasted_iota(jnp.int32, (c, LANE), 0) == c - 1

        def local(t, carry):
            rows_list = _gla_group_rows(t)
            cm, _, _ = _gla_chunks_common(q_ref, k_ref, lr_ref, wgu_ref, bgu_ref, rows_list)
            loc = [jnp.where(bd, _dot_tn(do_ref[rows, :], cc["qe"].astype(BF16)), 0.0)
                   for rows, cc in zip(rows_list, cm)]
            for j, cc in enumerate(cm):
                ds_s[t * GLA_GROUP + j] = loc[j]
                dec_s[t * GLA_GROUP + j] = jnp.broadcast_to(cc["dec"], (8, LANE))
            return carry

        lax.fori_loop(0, nc // GLA_GROUP, local, 0)

        def scan(t, dst):
            n = nc - 1 - t
            loc = ds_s[n]
            ds_s[n] = dst
            return dec_s[n][0:1, :] * dst + loc

        lax.fori_loop(0, nc, scan, jnp.zeros((2 * GLA_DV, LANE), F32))

        def rest(t, carry):
            rows_list = _gla_group_rows(t)
            cm, ri, ci = _gla_chunks_common(q_ref, k_ref, lr_ref, wgu_ref, bgu_ref, rows_list)
            ns = [t * GLA_GROUP + j for j in range(GLA_GROUP)]
            vs = [v_ref[rows, :] for rows in rows_list]
            dobs = [do_ref[rows, :] for rows in rows_list]
            stbs = [st_ref[0, n] for n in ns]
            dsts = [ds_s[n] for n in ns]
            dstbs = [d.astype(BF16) for d in dsts]
            qebs = [cc["qe"].astype(BF16) for cc in cm]
            kebs = [cc["ke"].astype(BF16) for cc in cm]
            kendbs = [cc["kend"].astype(BF16) for cc in cm]
            hms = [_head_lane_mask(hh) for hh in range(2)]
            qehs = [[jnp.where(hm, cc["qe"], 0.0).astype(BF16) for hm in hms] for cc in cm]
            kehs = [[jnp.where(hm, cc["ke"], 0.0).astype(BF16) for hm in hms] for cc in cm]
            heads = lambda x: [x[:, hh * GLA_DV:(hh + 1) * GLA_DV] for hh in range(2)]
            vhs, dohs = [heads(v) for v in vs], [heads(d) for d in dobs]

            dqe0 = [_dot(dob, stb) for dob, stb in zip(dobs, stbs)]
            dkend = [_dot(v, dstb) for v, dstb in zip(vs, dstbs)]
            dv0 = [_dot_nt(kb, dstb) for kb, dstb in zip(kendbs, dstbs)]
            a_t = [[jnp.where(ci >= ri, _dot_nt(kehs[j][hh], qebs[j]), 0.0).astype(BF16) for hh in range(2)]
                   for j in range(GLA_GROUP)]
            da = [[jnp.where(ri >= ci, _dot_nt(dohs[j][hh], vhs[j][hh]), 0.0).astype(BF16) for hh in range(2)]
                  for j in range(GLA_GROUP)]
            da_t = [[jnp.where(ci >= ri, _dot_nt(vhs[j][hh], dohs[j][hh]), 0.0).astype(BF16) for hh in range(2)]
                    for j in range(GLA_GROUP)]
            dv1 = [[_dot(a_t[j][hh], dohs[j][hh]) for hh in range(2)] for j in range(GLA_GROUP)]
            dqe1 = [[_dot(da[j][hh], kebs[j]) for hh in range(2)] for j in range(GLA_GROUP)]
            dke1 = [[_dot(da_t[j][hh], qehs[j][hh]) for hh in range(2)] for j in range(GLA_GROUP)]

            dbs, dzs = [], []
            for j, (rows, cc) in enumerate(zip(rows_list, cm)):
                qe, ke, kend, b, bl = cc["qe"], cc["ke"], cc["kend"], cc["b"], cc["bl"]
                dqe = dqe0[j] + jnp.where(hms[0], dqe1[j][0], 0.0) + jnp.where(hms[1], dqe1[j][1], 0.0)
                dke = jnp.where(hms[0], dke1[j][0], 0.0) + jnp.where(hms[1], dke1[j][1], 0.0)
                dv_ref[rows, :] = (dv0[j] + jnp.concatenate(dv1[j], axis=1)).astype(BF16)
                dq_ref[rows, :] = (dqe * jnp.exp(b) * (GLA_DK ** -0.5)).astype(BF16)
                dk_ref[rows, :] = (dke * jnp.exp(-b) + dkend[j] * jnp.exp(bl - b)).astype(BF16)
                ddec = jnp.sum(dsts[j] * stbs[j].astype(F32), axis=0, keepdims=True)
                dbl = jnp.sum(dkend[j] * kend, axis=0, keepdims=True) + ddec * cc["dec"]
                dbs.append(dqe * qe - dke * ke - dkend[j] * kend + jnp.where(last_row, dbl, 0.0))
            triu = (ci >= ri).astype(F32)
            dlas = [jnp.dot(triu, db, precision=lax.Precision.HIGHEST, preferred_element_type=F32) for db in dbs]
            dzs = [dla * (1.0 / GLA_TAU) * _sigmoid(-cc["z"]) for dla, cc in zip(dlas, cm)]
            dzbs = [dz.astype(BF16) for dz in dzs]
            dlrs = [_dot_nt(dzb, wgu_ref[...]) for dzb in dzbs]
            dws = [_dot_tn(lr_ref[rows, :], dzb) for rows, dzb in zip(rows_list, dzbs)]
            for rows, dlr in zip(rows_list, dlrs):
                dlr_ref[0, rows, :] = dlr
            dw_acc[...] += functools.reduce(lambda x, y: x + y, dws)
            db_acc[0:1, :] += jnp.sum(functools.reduce(lambda x, y: x + y, dzs), axis=0, keepdims=True)
            return carry

        lax.fori_loop(0, nc // GLA_GROUP, rest, 0)
        dwgu_ref[...] = dw_acc[...]
        dbgu_ref[...] = db_acc[...]
        comm_after()

    pair = pl.BlockSpec((s_len, LANE), lambda g: (0, g))
    return pl.pallas_call(
        body, name="gla_bwd_comm" if comm else "gla_bwd", grid=(2,),
        out_shape=[jax.ShapeDtypeStruct((s_len, GU_COLS), BF16), jax.ShapeDtypeStruct((s_len, GU_COLS), BF16),
                   jax.ShapeDtypeStruct((s_len, GLA_HEADS * GLA_DV), BF16),
                   jax.ShapeDtypeStruct((2, s_len, LANE), F32),
                   jax.ShapeDtypeStruct((LANE, GU_COLS), F32), jax.ShapeDtypeStruct((8, GU_COLS), F32)] + c_shapes,
        in_specs=[pl.BlockSpec((s_len, LANE), lambda g: (0, COL_QA // LANE + g)),
                  pl.BlockSpec((s_len, LANE), lambda g: (0, COL_KA // LANE + g)),
                  pl.BlockSpec((s_len, 2 * GLA_DV), lambda g: (0, (COL_VA - NP_F32) // (2 * GLA_DV) + g)),
                  pl.BlockSpec((s_len, LANE), lambda g: (0, (COL_LR - NP_F32) // LANE)),
                  pl.BlockSpec((None, LANE, LANE), lambda g: (layer, 0, g)),
                  pl.BlockSpec((None, 1, LANE), lambda g: (layer, 0, g)),
                  pl.BlockSpec((1, nc, 2 * GLA_DV, LANE), lambda g: (g, 0, 0, 0)),
                  pl.BlockSpec((s_len, 2 * GLA_DV), lambda g: (0, g))] + [ANY] * n_cin,
        out_specs=[pair, pair, pl.BlockSpec((s_len, 2 * GLA_DV), lambda g: (0, g)),
                   pl.BlockSpec((1, s_len, LANE), lambda g: (g, 0, 0)),
                   pl.BlockSpec((LANE, LANE), lambda g: (0, g)), pl.BlockSpec((8, LANE), lambda g: (0, g))]
        + [ANY] * len(c_shapes),
        scratch_shapes=[pltpu.VMEM((nc, 2 * GLA_DV, LANE), F32), pltpu.VMEM((nc, 8, LANE), F32),
                        pltpu.VMEM((LANE, LANE), F32), pltpu.VMEM((8, LANE), F32)] + c_scratch,
        compiler_params=_params(("arbitrary",), 56),
    )(pf, pf, pb, pb, wgu, bgu.reshape(bgu.shape[0], 1, GU_COLS), states, do, *(comm[1] if comm else []))


def _dil_bwd(pf, pb, cos, sin_signed, do, o_b, lse, comm=None):
    s_len = pf.shape[0]
    nblk = s_len // DIL_BLOCK
    prep_rows = 256
    scale = DIL_HD ** -0.5
    nc = len(comm[1]) if comm else 0

    def body(*refs):
        ((q_ref, kf, v_ref, cos_ref, sin_ref, do_ref, o_ref, lse_ref), (dq_ref, dk_ref, dv_ref),
         (qf, vf, dof, dl, dqa, dka, dva, bias), cin, cout, csem) = _split_refs(refs, 8, 3, 8, comm)
        comm_before, comm_after = _comm_hooks(comm, cin, cout, csem)
        comm_before()
        _dil_fill_bias(bias)

        def prep(t, carry):
            rows = pl.ds(pl.multiple_of(t * prep_rows, prep_rows), prep_rows)
            qf[rows, :] = q_ref[rows, :] * scale
            vf[rows, :] = v_ref[rows, :].astype(F32)
            dov = do_ref[rows, :].astype(F32)
            dof[rows, :] = dov
            dl[rows, :] = jnp.broadcast_to(jnp.sum(dov * o_ref[rows, :], axis=-1, keepdims=True), (prep_rows, DIL_HD))
            zero = jnp.zeros((prep_rows, DIL_HD), F32)
            dqa[rows, :] = zero
            dka[rows, :] = zero
            dva[rows, :] = zero
            return carry

        lax.fori_loop(0, s_len // prep_rows, prep, 0)

        for d in DIL_DILATIONS:
            if nblk // d == 2:
                units = DIL_GROUP // 2

                def whole(i, carry, d=d, units=units):
                    rows = [_strided(i + u * (d // units), 2 * DIL_BLOCK, d) for u in range(units)]
                    ld = [(qf[rw, :].astype(BF16), kf[rw, :].astype(BF16), vf[rw, :].astype(BF16),
                           dof[rw, :].astype(BF16)) for rw in rows]
                    both = bias[...].reshape(2 * DIL_BLOCK, 2 * DIL_BLOCK)
                    s = [_dot_nt(qb, kk) + both for qb, kk, _, _ in ld]
                    dp = [_dot_nt(dob, vv) for _, _, vv, dob in ld]
                    p = [jnp.exp(sv - lse_ref[rw, :][:, 0:1]) for sv, rw in zip(s, rows)]
                    ds = [(pv * (dpv - dl[rw, :][:, 0:1])).astype(BF16) for pv, dpv, rw in zip(p, dp, rows)]
                    pb = [pv.astype(BF16) for pv in p]
                    gq = [_dot(dsv, kk) for dsv, (_, kk, _, _) in zip(ds, ld)]
                    gk = [_dot_tn(dsv, qb) for dsv, (qb, _, _, _) in zip(ds, ld)]
                    gv = [_dot_tn(pv, dob) for pv, (_, _, _, dob) in zip(pb, ld)]
                    for rw, a, b, c in zip(rows, gq, gk, gv):
                        dqa[rw, :] += a
                        dka[rw, :] += b
                        dva[rw, :] += c
                    return carry

                lax.fori_loop(0, d // units, whole, 0)
                continue

            def pair(i, carry, d=d):
                idx = [_dil_pair_block(i, half, d, nblk) for half in range(DIL_GROUP)]
                rows = [(_strided(qs, DIL_BLOCK, d), _strided(ks, 2 * DIL_BLOCK, d)) for qs, ks, _ in idx]
                ld = [(qf[qr, :].astype(BF16), kf[kr, :].astype(BF16), vf[kr, :].astype(BF16),
                       dof[qr, :].astype(BF16)) for qr, kr in rows]
                s = [_dot_nt(qb, kk) + bias[sel] for (qb, kk, _, _), (_, _, sel) in zip(ld, idx)]
                dp = [_dot_nt(dob, vv) for _, _, vv, dob in ld]
                p = [jnp.exp(sv - lse_ref[qr, :][:, 0:1]) for sv, (qr, _) in zip(s, rows)]
                ds = [(pv * (dpv - dl[qr, :][:, 0:1])).astype(BF16) for pv, dpv, (qr, _) in zip(p, dp, rows)]
                pb = [pv.astype(BF16) for pv in p]
                gq = [_dot(dsv, kk) for dsv, (_, kk, _, _) in zip(ds, ld)]
                gk = [_dot_tn(dsv, qb) for dsv, (qb, _, _, _) in zip(ds, ld)]
                gv = [_dot_tn(pv, dob) for pv, (_, _, _, dob) in zip(pb, ld)]
                for (qr, kr), a, b, c in zip(rows, gq, gk, gv):
                    dqa[qr, :] += a
                    dka[kr, :] += b
                    dva[kr, :] += c
                return carry

            lax.fori_loop(0, nblk // DIL_GROUP, pair, 0)

        def fin(t, carry):
            rows = pl.ds(pl.multiple_of(t * prep_rows, prep_rows), prep_rows)
            cs, sn = cos_ref[rows, :], sin_ref[rows, :]
            gq, gk = dqa[rows, :] * scale, dka[rows, :]
            dq_ref[rows, :] = (gq * cs - pltpu.roll(gq, DIL_HD // 2, 1) * sn).astype(BF16)
            dk_ref[rows, :] = (gk * cs - pltpu.roll(gk, DIL_HD // 2, 1) * sn).astype(BF16)
            dv_ref[rows, :] = dva[rows, :].astype(BF16)
            return carry

        lax.fori_loop(0, s_len // prep_rows, fin, 0)
        comm_after()

    head = lambda base: pl.BlockSpec((s_len, DIL_HD), lambda h: (0, base // DIL_HD + h))
    table = pl.BlockSpec((s_len, DIL_HD), lambda h: (0, 0))
    out = pl.BlockSpec((s_len, DIL_HD), lambda h: (0, h))
    shp = jax.ShapeDtypeStruct((s_len, DIL_HEADS * DIL_HD), BF16)
    return pl.pallas_call(
        body, name="dil_bwd_comm" if comm else "dil_bwd", grid=(DIL_HEADS,),
        out_shape=[shp, shp, shp] + (_comm_out_shapes(*comm) if comm else []),
        in_specs=[head(COL_QB), head(COL_KB), head(COL_VB - NP_F32), table, table,
                  pl.BlockSpec((s_len, DIL_HD), lambda h: (0, DIL_HEADS + h)), out, out] + [ANY] * nc,
        out_specs=[out, out, out] + [ANY] * len(_comm_plumbing(comm)[1]),
        scratch_shapes=[pltpu.VMEM((s_len, DIL_HD), F32) for _ in range(7)]
        + [pltpu.VMEM((2, DIL_BLOCK, 2 * DIL_BLOCK), F32)] + (_comm_scratch(nc) if comm else []),
        compiler_params=_params(("arbitrary",), 56),
    )(pf, pf, pb, cos, sin_signed, do, o_b, lse, *(comm[1] if comm else []))


_PIECES = ((COL_Z, 1024), (COL_QA, 256), (COL_KA, 256), (COL_QB, 512), (COL_KB, 512), (COL_VA, 512), (COL_VB, 512),
           (COL_LR, 128))


def _in_bwd(pieces, w_new, x, dxo, g_pre, scale, comm=None, ts=256):
    s_len = x.shape[0]
    nc = len(comm[1]) if comm else 0
    nco = len(_comm_out_shapes(*comm)) if comm else 0
    npc = len(_PIECES)

    def body(*refs):
        p_refs = refs[:npc]
        w_ref, x_ref, dxo_ref, g_ref, sc_ref = refs[npc:npc + 5]
        cin, (dx_ref, sums_ref), cout = (refs[npc + 5:npc + 5 + nc], refs[npc + 5 + nc:npc + 7 + nc],
                                         refs[npc + 7 + nc:npc + 7 + nc + nco])
        comm_before, comm_after = _comm_hooks(comm, cin, cout, refs[npc + 7 + nc + nco:], steps=s_len // ts)
        comm_before()

        @pl.when(pl.program_id(0) == 0)
        def _():
            sums_ref[...] = jnp.zeros_like(sums_ref)

        dh = jnp.zeros((ts, D_MODEL), F32)
        for p_ref, (col, width) in zip(p_refs, _PIECES):
            dh += _dot_nt(p_ref[...], w_ref[:, col:col + width])
        xv = x_ref[...]
        rstd = lax.rsqrt(jnp.mean(xv * xv, axis=-1, keepdims=True) + EPS)
        xn = xv * rstd
        sums_ref[0:1, :] += jnp.sum(dh, axis=0, keepdims=True)
        sums_ref[1:2, :] += jnp.sum(dh * (xn * g_ref[...]), axis=0, keepdims=True)
        dr = dh * (1.0 + sc_ref[...])
        sums_ref[2:3, :] += jnp.sum(dr * xn, axis=0, keepdims=True)
        dxn = dr * g_ref[...]
        dx_ref[...] = dxo_ref[...] + rstd * (dxn - xn * jnp.mean(dxn * xn, axis=-1, keepdims=True))
        comm_after()

    (g_pre, g_spec), (scale, sc_spec) = _rowvec(g_pre), _rowvec(scale)
    tile = pl.BlockSpec((ts, D_MODEL), lambda i: (i, 0))
    return pl.pallas_call(
        body, name="in_bwd_comm" if comm else "in_bwd", grid=(s_len // ts,),
        out_shape=[jax.ShapeDtypeStruct((s_len, D_MODEL), F32), jax.ShapeDtypeStruct((8, D_MODEL), F32)]
        + (_comm_out_shapes(*comm) if comm else []),
        in_specs=[pl.BlockSpec((ts, width), lambda i: (i, 0)) for _, width in _PIECES]
        + [pl.BlockSpec((D_MODEL, NP), lambda i: (0, 0)), tile, tile, g_spec, sc_spec] + [ANY] * nc,
        out_specs=[tile, pl.BlockSpec((8, D_MODEL), lambda i: (0, 0))] + [ANY] * nco,
        scratch_shapes=_comm_scratch(nc) if comm else [],
        compiler_params=_params(("arbitrary",), 56),
    )(*pieces, w_new, x, dxo, g_pre, scale, *(comm[1] if comm else []))


def _w_in_to_kernel(gathered, comm=None, tr=128):
    n_cin, c_shapes, c_scratch = _comm_plumbing(comm)
    n_parts = len(gathered)
    first = [sum(g.shape[1] for g in gathered[:p]) // tr for p in range(n_parts + 1)]

    def body(*refs):
        g_refs, (o_ref,), _, cin, cout, csem = _split_refs(refs, n_parts, 1, 0, comm)
        comm_before, comm_after = _comm_hooks(comm, cin, cout, csem, steps=D_MODEL // tr)
        comm_before()
        for p, g_ref in enumerate(g_refs):
            @pl.when((pl.program_id(0) >= first[p]) & (pl.program_id(0) < first[p + 1]))
            def _(g_ref=g_ref):
                cols = jnp.concatenate([g_ref[k].astype(F32) for k in range(N_DEV)], axis=1)
                pad = jnp.zeros((tr, LANE - GLA_LOWRANK), F32)
                o_ref[...] = jnp.concatenate(
                    [cols[:, 1024:1536], cols[:, 3088:3600], cols[:, 0:512], cols[:, 1552:2576], cols[:, 512:1024],
                     cols[:, 2576:3088], cols[:, 1536:1552], pad], axis=1).astype(BF16)
        comm_after()

    part = lambda p: pl.BlockSpec((N_DEV, tr, W_IN_SHARD),
                                  lambda i: (0, jnp.clip(i - first[p], 0, first[p + 1] - first[p] - 1), 0))
    return pl.pallas_call(
        body, name="w_in_to_kernel_comm" if comm else "w_in_to_kernel", grid=(D_MODEL // tr,),
        out_shape=[jax.ShapeDtypeStruct((D_MODEL, NP), BF16)] + c_shapes,
        in_specs=[part(p) for p in range(n_parts)] + [ANY] * n_cin,
        out_specs=[pl.BlockSpec((tr, NP), lambda i: (i, 0))] + [ANY] * len(c_shapes),
        scratch_shapes=c_scratch,
        compiler_params=_params(("arbitrary",)),
    )(*gathered, *(comm[1] if comm else []))


def _grad_w_in(h, pieces, ts=512, tr=128):
    s_len = h.shape[0]
    steps = s_len // ts

    def body(*refs):
        h_ref, p_refs = refs[0], refs[1:1 + len(_PIECES)]
        o_ref, acc = refs[1 + len(_PIECES):]

        @pl.when(pl.program_id(0) == 0)
        def _():
            acc[...] = jnp.zeros_like(acc)

        hv = h_ref[...]
        for p_ref, (col, width) in zip(p_refs, _PIECES):
            acc[:, col:col + width] += _dot_tn(hv, p_ref[...])

        @pl.when(pl.program_id(0) == steps - 1)
        def _():
            def rows_out(t, carry):
                rows = pl.ds(pl.multiple_of(t * tr, tr), tr)
                g = acc[rows, :]
                cols = jnp.concatenate(
                    [g[:, COL_QA:COL_QB], g[:, COL_VA:COL_VB], g[:, 0:512], g[:, COL_LR:COL_LR + GLA_LOWRANK],
                     g[:, COL_QB:COL_VA], g[:, COL_VB:COL_LR], g[:, 512:1024]], axis=1)
                for k in range(N_DEV):
                    o_ref[k, rows, :] = cols[:, W_IN_SHARD * k:W_IN_SHARD * (k + 1)].astype(BF16)
                return carry

            lax.fori_loop(0, D_MODEL // tr, rows_out, 0)

    return pl.pallas_call(
        body, name="grad_w_in", grid=(steps,),
        out_shape=jax.ShapeDtypeStruct((N_DEV, D_MODEL, W_IN_SHARD), BF16),
        in_specs=[pl.BlockSpec((ts, D_MODEL), lambda i: (i, 0))]
        + [pl.BlockSpec((ts, width), lambda i: (i, 0)) for _, width in _PIECES],
        out_specs=pl.BlockSpec((N_DEV, D_MODEL, W_IN_SHARD), lambda i: (0, 0, 0)),
        scratch_shapes=[pltpu.VMEM((D_MODEL, NP), F32)],
        compiler_params=_params(("arbitrary",), 56),
    )(h, *pieces)


def _adam_math(w, g, m, v):
    m = ADAM_B1 * m + (1.0 - ADAM_B1) * g
    v = ADAM_B2 * v + (1.0 - ADAM_B2) * (g * g)
    m_hat = m / (1.0 - ADAM_B1 ** ADAM_STEP)
    v_hat = v / (1.0 - ADAM_B2 ** ADAM_STEP)
    delta = -ADAM_LR * (m_hat / (jnp.sqrt(v_hat) + ADAM_EPS) + ADAM_WD * w)
    return delta, m, v


def _adamw(w, parts, m, v, name, tr):
    r, cdim = w.shape
    n_parts = parts.shape[0]

    def body(w_ref, p_ref, m_ref, v_ref, g_ref, d_ref, nm_ref, nv_ref):
        g = p_ref[0].astype(F32)
        for k in range(1, n_parts):
            g = g + p_ref[k].astype(F32)
        g_ref[...] = g
        d_ref[...], nm_ref[...], nv_ref[...] = _adam_math(w_ref[...], g, m_ref[...], v_ref[...])

    tile = pl.BlockSpec((tr, cdim), lambda i: (i, 0))
    shp = jax.ShapeDtypeStruct((r, cdim), F32)
    return pl.pallas_call(
        body, name=name, grid=(r // tr,), out_shape=(shp, shp, shp, shp),
        in_specs=[tile, pl.BlockSpec((n_parts, tr, cdim), lambda i: (0, i, 0)), tile, tile],
        out_specs=(tile, tile, tile, tile),
        compiler_params=_params(("arbitrary",), 40),
    )(w, parts, m, v)


def _adamw_layers(w, parts, m, v, name, tr):
    n_layers, r, cdim = w.shape

    def body(*refs):
        w_ref, p_refs, (m_ref, v_ref) = refs[0], refs[1:1 + n_layers], refs[1 + n_layers:3 + n_layers]
        g_ref, d_ref, nm_ref, nv_ref = refs[3 + n_layers:]
        for l, p_ref in enumerate(p_refs):
            @pl.when(pl.program_id(0) == l)
            def _(p_ref=p_ref):
                g = p_ref[0].astype(F32)
                for k in range(1, p_ref.shape[0]):
                    g = g + p_ref[k].astype(F32)
                g_ref[0] = g
                d_ref[0], nm_ref[0], nv_ref[0] = _adam_math(w_ref[0], g, m_ref[0], v_ref[0])

    tile = pl.BlockSpec((1, tr, cdim), lambda l, i: (l, i, 0))
    part = lambda own: pl.BlockSpec((parts[own].shape[0], tr, cdim), lambda l, i: (0, jnp.where(l == own, i, 0), 0))
    shp = jax.ShapeDtypeStruct(w.shape, F32)
    return pl.pallas_call(
        body, name=name, grid=(n_layers, r // tr), out_shape=(shp, shp, shp, shp),
        in_specs=[tile] + [part(l) for l in range(n_layers)] + [tile, tile],
        out_specs=(tile, tile, tile, tile),
        compiler_params=_params(("arbitrary", "arbitrary"), 40),
    )(w, *parts, m, v)


def _row(vec, width):
    vec = vec.reshape(1, -1)
    return jnp.pad(vec, ((0, 0), (0, width - vec.shape[1])))


def kernel(x, c, w_ada, b_ada, g_pre, w_in, w_gate_up, b_gate_up, g_gla, g_dil, w_out, g_post, loss_target, m_w_ada, m_b_ada, m_g_pre, m_w_in, m_w_gate_up, m_b_gate_up, m_g_gla, m_g_dil, m_w_out, m_g_post, v_w_ada, v_b_ada, v_g_pre, v_w_in, v_w_gate_up, v_b_gate_up, v_g_gla, v_g_dil, v_w_out, v_g_post):
    px, py, pc = _my_position()
    me = _linear(px, py, pc)
    xs = x[0]
    target = loss_target[0]
    s_len = xs.shape[0]
    assert s_len % (DIL_BLOCK * max(DIL_DILATIONS) * 2) == 0 and xs.shape[1] == D_MODEL

    w_in_b, w_out_b = w_in.astype(BF16), w_out.astype(BF16)
    c_rows, wgu_all, w_in_all = _comm_call(
        "gather", [jnp.pad(c, ((0, 7), (0, 0))), w_gate_up.reshape(DEPTH * GLA_LOWRANK, GU_SHARD), w_in_b[0]],
        "gather_first")
    c_all = c_rows.reshape(N_DEV, 8, D_MODEL)[:, 0]
    mod_part = _mod_fwd(c_all, w_ada)
    w_new, mod_all = _w_in_to_kernel([w_in_all.reshape(N_DEV, D_MODEL, W_IN_SHARD)],
                                     comm=("gather", [mod_part.reshape(DEPTH * N_DEV, ADA_SHARD)]))
    mod_all = mod_all.reshape(N_DEV, DEPTH, N_DEV, ADA_SHARD)
    mod_mine = lax.dynamic_index_in_dim(mod_all, me, axis=2, keepdims=False)
    mod = jnp.transpose(mod_mine, (1, 0, 2)).reshape(DEPTH, 3 * D_MODEL) + b_ada
    wgu_full = jnp.transpose(wgu_all.reshape(N_DEV, DEPTH, GLA_LOWRANK, GU_SHARD), (1, 2, 0, 3)).reshape(
        DEPTH, GLA_LOWRANK, GU_COLS)
    wgu_pad = jnp.pad(wgu_full, ((0, 0), (0, LANE - GLA_LOWRANK), (0, 0))).astype(BF16)

    cos, sin_signed = _rope_tables(s_len)
    g_heads = jnp.concatenate([g_gla, g_dil], axis=1)

    saved = []
    xl = xs
    for l in range(DEPTH):
        shift, scale, gate = ((mod, l, k) for k in range(3))
        if l > 0:
            w_new = _w_in_to_kernel([half.reshape(N_DEV, D_MODEL // 2, W_IN_SHARD) for half in w_in_halves])[0]
        if l + 1 < DEPTH:
            own = [] if l > 0 else [w_out_b[0]]
            pf, pb, h, *arrived = _prenorm_proj(xl, (g_pre, l, 0), scale, shift, w_new, cos, sin_signed,
                                                comm=("gather", own + [w_in_b[l + 1, :D_MODEL // 2]]))
            w_out_l = arrived[0] if l == 0 else w_out_next
            top = arrived[-1]
        else:
            pf, pb, h = _prenorm_proj(xl, (g_pre, l, 0), scale, shift, w_new, cos, sin_signed)
            w_out_l = w_out_next
        o_a, states = _gla_fwd(pf, pb, wgu_pad, b_gate_up, l)
        if l + 1 < DEPTH:
            o_b, lse, bottom, w_out_next = _dil_fwd(pf, pb, comm=("gather", [w_in_b[l + 1, D_MODEL // 2:],
                                                                             w_out_b[l + 1]]))
            w_in_halves = (top, bottom)
        else:
            o_b, lse = _dil_fwd(pf, pb)
        if l + 1 < DEPTH:
            x_next, u = _post_fwd(o_a, o_b, pf, (g_heads, l, 0), w_out_l, xl, gate, (g_post, l, 0))
        else:
            dx, u, loss_part = _post_fwd(o_a, o_b, pf, (g_heads, l, 0), w_out_l, xl, gate, (g_post, l, 0),
                                         target=target)
        saved.append((xl, scale, gate, w_new, w_out_l, pf, pb, h, o_a, states, o_b, lse, u))
        xl = x_next

    small_rows = []
    gin_slots, gin_parts, gout_parts = None, [None] * DEPTH, [None] * DEPTH
    for l in reversed(range(DEPTH)):
        x_in, scale, gate, w_new, w_out_l, pf, pb, h, o_a, states, o_b, lse, u = saved[l]
        do, dz, sums_post, gout_slots = _post_bwd(dx, u, gate, (g_post, l, 0), w_out_l, o_a, o_b, pf, (g_heads, l, 0))
        dq_a, dk_a, dv_a, dlr2, dwgu, dbgu, arrived = _gla_bwd(pf, pb, wgu_pad, b_gate_up, l, states, do,
                                                               comm=("exchange", [gout_slots]))
        gout_parts[l] = arrived.reshape(N_DEV, OUT_SHARD, D_MODEL)
        if gin_slots is not None:
            dq_b, dk_b, dv_b, arrived, _, _ = _dil_bwd(pf, pb, cos, sin_signed, do, o_b, lse,
                                                       comm=("pairsum_exchange", [gin_slots]))
            gin_parts[l + 1] = arrived.reshape(N_DEV // 2, D_MODEL, W_IN_SHARD)
        else:
            dq_b, dk_b, dv_b = _dil_bwd(pf, pb, cos, sin_signed, do, o_b, lse)
        dlr = (dlr2[0] + dlr2[1]).astype(BF16)
        pieces = (dz, dq_a, dk_a, dq_b, dk_b, dv_a, dv_b, dlr)
        gin_slots = _grad_w_in(h, pieces).reshape(N_DEV * D_MODEL, W_IN_SHARD)
        if l == 0:
            dx, sums_in, arrived, _, _ = _in_bwd(pieces, w_new, x_in, dx, (g_pre, l, 0), scale,
                                                 comm=("pairsum_exchange", [gin_slots]))
            gin_parts[0] = arrived.reshape(N_DEV // 2, D_MODEL, W_IN_SHARD)
        else:
            dx, sums_in = _in_bwd(pieces, w_new, x_in, dx, (g_pre, l, 0), scale, ts=512)
        dmod = jnp.concatenate([sums_in[0], sums_in[1], sums_post[0]])
        vecs = jnp.concatenate([sums_in[2], sums_post[1], sums_post[2], dbgu[0]])
        small_rows[0:0] = [_row(dmod, 4096), _row(vecs, 4096), _row(dwgu[:GLA_LOWRANK], 4096)]
    grad_x = dx[None]

    flat = lambda a, rows: a.reshape(rows, a.shape[-1])
    r_ada = DEPTH * D_MODEL
    g_w_in, d_w_in, nm_w_in, nv_w_in = _adamw_layers(w_in, gin_parts, m_w_in, v_w_in, "adamw_w_in", 256)
    g_w_out, d_w_out, nm_w_out, nv_w_out = _adamw_layers(w_out, gout_parts, m_w_out, v_w_out, "adamw_w_out", 128)

    small_rows += [_row(loss_part[0, 0:1], 4096), jnp.zeros((1, 4096), F32)]
    small = _all_gather(jnp.concatenate(small_rows, axis=0), "gather_small").reshape(N_DEV, 8, 4096)
    dmod_all = jnp.stack([small[:, 0, :3 * D_MODEL], small[:, 3, :3 * D_MODEL]])
    dmod_cols = lax.dynamic_slice_in_dim(dmod_all, me * ADA_SHARD, ADA_SHARD, axis=2)
    gwa = _w_ada_grad(c_all, dmod_cols).reshape(1, r_ada, ADA_SHARD)
    g_w_ada, d_w_ada, nm_w_ada, nv_w_ada = (
        t.reshape(w_ada.shape) for t in _adamw(flat(w_ada, r_ada), gwa, flat(m_w_ada, r_ada), flat(v_w_ada, r_ada),
                                               "adamw_w_ada", 256))

    where = ((0, 0), (1, 0), (1, 1024), (1, 2048), (1, 2560), (1, 3072))
    replicated = [(b_ada, m_b_ada, v_b_ada), (g_pre, m_g_pre, v_g_pre), (g_post, m_g_post, v_g_post),
                  (g_gla, m_g_gla, v_g_gla), (g_dil, m_g_dil, v_g_dil), (b_gate_up, m_b_gate_up, v_b_gate_up)]
    updated, loss = _adamw_replicated(small, replicated, where, loss_at=(6, 0))
    ((g_b_ada, d_b_ada, nm_b_ada, nv_b_ada), (g_g_pre, d_g_pre, nm_g_pre, nv_g_pre),
     (g_g_post, d_g_post, nm_g_post, nv_g_post), (g_g_gla, d_g_gla, nm_g_gla, nv_g_gla),
     (g_g_dil, d_g_dil, nm_g_dil, nv_g_dil), (g_b_gu, d_b_gu, nm_b_gu, nv_b_gu)) = updated
    gu_parts = jnp.stack([small[:, 2], small[:, 5]], axis=1).reshape(N_DEV, DEPTH, GLA_LOWRANK, GU_COLS)
    gu_parts = lax.dynamic_slice_in_dim(gu_parts, me * GU_SHARD, GU_SHARD, axis=3).reshape(
        N_DEV, DEPTH * GLA_LOWRANK, GU_SHARD)
    r_gu = DEPTH * GLA_LOWRANK
    g_w_gu, d_w_gu, nm_w_gu, nv_w_gu = (
        t.reshape(w_gate_up.shape) for t in _adamw(flat(w_gate_up, r_gu), gu_parts, flat(m_w_gate_up, r_gu),
                                                   flat(v_w_gate_up, r_gu), "adamw_w_gate_up", r_gu))
    return (loss, grad_x,
            g_w_ada, g_b_ada, g_g_pre, g_w_in, g_w_gu, g_b_gu, g_g_gla, g_g_dil, g_w_out, g_g_post,
            d_w_ada, d_b_ada, d_g_pre, d_w_in, d_w_gu, d_b_gu, d_g_gla, d_g_dil, d_w_out, d_g_post,
            nm_w_ada, nm_b_ada, nm_g_pre, nm_w_in, nm_w_gu, nm_b_gu, nm_g_gla, nm_g_dil, nm_w_out, nm_g_post,
            nv_w_ada, nv_b_ada, nv_g_pre, nv_w_in, nv_w_gu, nv_b_gu, nv_g_gla, nv_g_dil, nv_w_out, nv_g_post)


def _adamw_replicated(small, params, where, loss_at):
    n_parts = small.shape[0]

    def body(*refs):
        s_ref, p_refs, o_refs = refs[0], refs[1:1 + 3 * len(params)], refs[1 + 3 * len(params):]
        total = s_ref[0]
        for k in range(1, n_parts):
            total = total + s_ref[k]
        for i, (row, col) in enumerate(where):
            w_ref, m_ref, v_ref = p_refs[3 * i:3 * i + 3]
            n = w_ref.shape[1]
            g = jnp.concatenate([total[row + 3 * l:row + 3 * l + 1, col:col + n] for l in range(DEPTH)], axis=0)
            o_refs[4 * i][...] = g
            o_refs[4 * i + 1][...], o_refs[4 * i + 2][...], o_refs[4 * i + 3][...] = _adam_math(
                w_ref[...], g, m_ref[...], v_ref[...])
        o_refs[-1][...] = jnp.broadcast_to(total[loss_at[0]:loss_at[0] + 1, loss_at[1]:loss_at[1] + 1], (8, LANE))

    flat = [a for p in params for a in p]
    shapes = [jax.ShapeDtypeStruct(p[0].shape, F32) for p in params for _ in range(4)]
    outs = pl.pallas_call(body, name="adamw_replicated",
                          out_shape=shapes + [jax.ShapeDtypeStruct((8, LANE), F32)])(small, *flat)
    return [tuple(outs[4 * i:4 * i + 4]) for i in range(len(params))], outs[-1][0, 0]
```

```python
import functools
import math

import jax
import jax.numpy as jnp
from jax import lax
from jax.experimental import pallas as pl
from jax.experimental.pallas import tpu as pltpu

F32 = jnp.float32
BF16 = jnp.bfloat16

N_DEV = 8
D_MODEL = 1024
DEPTH = 2
GLA_HEADS = 4
GLA_DK = 64
GLA_DV = 128
GLA_CHUNK = 64
GLA_TAU = 16.0
GLA_LOWRANK = 16
DIL_HEADS = 4
DIL_HD = 128
DIL_BLOCK = 128
DIL_DILATIONS = (1, 4, 16)
ROPE_THETA = 10000.0
EPS = 1e-6
IN_COLS = 3600
W_IN_SHARD = IN_COLS // N_DEV
ADA_SHARD = 3 * D_MODEL // N_DEV
OUT_SHARD = D_MODEL // N_DEV
GU_COLS = GLA_HEADS * GLA_DK
GU_SHARD = GU_COLS // N_DEV

ADAM_LR = 0.001
ADAM_B1 = 0.9
ADAM_B2 = 0.999
ADAM_EPS = 1e-08
ADAM_WD = 0.01
ADAM_STEP = 10

NP = 3712
COL_Z, COL_QA, COL_KA, COL_QB, COL_KB, COL_VA, COL_VB, COL_LR = 0, 1024, 1280, 1536, 2048, 2560, 3072, 3584
NP_F32 = COL_VA
NP_BF16 = NP - NP_F32
LANE = 128
MASK_VALUE = -1e30

MESH = pl.DeviceIdType.MESH
ANY = pl.BlockSpec(memory_space=pl.ANY)


def _params(sem=None, vmem_mb=None):
    kw = {}
    if sem is not None:
        kw["dimension_semantics"] = sem
    if vmem_mb is not None:
        kw["vmem_limit_bytes"] = vmem_mb * 1024 * 1024
    return pltpu.CompilerParams(**kw)


def _dot(a, b):
    return jnp.dot(a, b, preferred_element_type=F32)


def _dot_nt(a, b):
    return lax.dot_general(a, b, (((1,), (1,)), ((), ())), preferred_element_type=F32)


def _dot_tn(a, b):
    return lax.dot_general(a, b, (((0,), (0,)), ((), ())), preferred_element_type=F32)


def _sigmoid(z):
    return 1.0 / (1.0 + jnp.exp(-z))


def _log_sigmoid(z):
    return jnp.minimum(z, 0.0) - jnp.log(1.0 + jnp.exp(-jnp.abs(z)))


def _rowvec(v, width=D_MODEL):
    arr, row, cb = v
    return arr.reshape(arr.shape[0], 1, arr.shape[1]), pl.BlockSpec((None, 1, width), lambda *_: (row, 0, cb))


def _my_position():
    return lax.axis_index("x"), lax.axis_index("y"), lax.axis_index("c")


def _linear(px, py, pc):
    return 4 * px + 2 * py + pc


def _gather_phase(phase, x_ref, out_ref, send_sem, recv_sem, local_sem):
    m = x_ref.shape[0]
    x, y, c = _my_position()
    me, sibling = (x, y, c), (x, y, 1 - c)
    chips = [(1 - x, y), (x, 1 - y), (1 - x, 1 - y)]

    def rows(px, py, pc):
        return out_ref.at[pl.ds(_linear(px, py, pc) * m, m), :]

    def copy(k, block, to, src=None):
        return pltpu.make_async_remote_copy(
            src_ref=rows(*block) if src is None else src, dst_ref=rows(*block),
            send_sem=send_sem(k), recv_sem=recv_sem(k), device_id=to, device_id_type=MESH)

    mine = pltpu.make_async_copy(x_ref, rows(*me), local_sem)
    first = [copy(0, me, sibling, src=x_ref)] + [copy(1 + j, me, (*chip, c), src=x_ref) for j, chip in enumerate(chips)]
    passed = [copy(4 + j, (*chip, c), sibling) for j, chip in enumerate(chips)]
    if phase == "start":
        mine.start()
        for cp in first:
            cp.start()
    elif phase == "forward":
        for j, chip in enumerate(chips):
            copy(1 + j, (*chip, c), me).wait_recv()
            passed[j].start()
    else:
        copy(0, sibling, me).wait_recv()
        for j, chip in enumerate(chips):
            copy(4 + j, (*chip, 1 - c), me).wait_recv()
        for cp in first + passed:
            cp.wait_send()
        mine.wait()


def _exchange_phase(phase, x_ref, out_ref, send_sem, recv_sem, local_sem):
    m = x_ref.shape[0] // N_DEV
    x, y, c = _my_position()
    me = _linear(x, y, c)

    def rows(ref, idx):
        return ref.at[pl.ds(idx * m, m), :]

    peers = [(1 - x if j & 4 else x, 1 - y if j & 2 else y, 1 - c if j & 1 else c) for j in range(1, N_DEV)]
    local = pltpu.make_async_copy(rows(x_ref, me), rows(out_ref, me), local_sem)
    sends = [pltpu.make_async_remote_copy(
        src_ref=rows(x_ref, _linear(*peer)), dst_ref=rows(out_ref, me),
        send_sem=send_sem(j), recv_sem=recv_sem(j), device_id=peer, device_id_type=MESH) for j, peer in enumerate(peers)]
    if phase == "start":
        local.start()
        for cp in sends:
            cp.start()
    else:
        for j, peer in enumerate(peers):
            pltpu.make_async_remote_copy(
                src_ref=rows(x_ref, _linear(*peer)), dst_ref=rows(out_ref, _linear(*peer)),
                send_sem=send_sem(j), recv_sem=recv_sem(j), device_id=peer, device_id_type=MESH).wait_recv()
        for cp in sends:
            cp.wait_send()
        local.wait()


def _pairsum_exchange_phase(phase, x_ref, out_refs, send_sem, recv_sem, local_sem):
    out_ref, stage_ref, pair_ref = out_refs
    m, n = x_ref.shape[0] // N_DEV, x_ref.shape[1]
    x, y, c = _my_position()
    mine = 2 * x + y
    chips = [(qx, qy) for qx in range(2) for qy in range(2)]
    others = [(1 - x, y), (x, 1 - y), (1 - x, 1 - y)]

    def rows(ref, idx):
        return ref.at[pl.ds(idx * m, m), :]

    def remote(src, dst, k, to):
        return pltpu.make_async_remote_copy(src_ref=src, dst_ref=dst, send_sem=send_sem(k), recv_sem=recv_sem(k),
                                            device_id=to, device_id_type=MESH)

    to_sibling = [remote(rows(x_ref, _linear(qx, qy, 1 - c)), rows(stage_ref, q), q, (x, y, 1 - c))
                  for q, (qx, qy) in enumerate(chips)]
    to_chips = [remote(rows(pair_ref, 2 * qx + qy), rows(out_ref, mine), 4 + j, (qx, qy, c))
                for j, (qx, qy) in enumerate(others)]
    keep = pltpu.make_async_copy(rows(pair_ref, mine), rows(out_ref, mine), local_sem)
    if phase == "start":
        for cp in to_sibling:
            cp.start()
    elif phase == "reduce":
        for cp in to_sibling:
            cp.wait_recv()

        def through_vmem(a_buf, b_buf, sems):
            tr = 128
            loads = [(pltpu.make_async_copy(rows(x_ref, _linear(qx, qy, c)), a_buf.at[q % 2], sems.at[q % 2]),
                      pltpu.make_async_copy(rows(stage_ref, q), b_buf.at[q % 2], sems.at[2 + q % 2]))
                     for q, (qx, qy) in enumerate(chips)]
            stores = [pltpu.make_async_copy(a_buf.at[q % 2], rows(pair_ref, q), sems.at[4 + q % 2]) for q in range(4)]
            for cp in loads[0]:
                cp.start()
            for q in range(4):
                for cp in loads[q]:
                    cp.wait()
                if q + 1 < 4:
                    if q >= 1:
                        stores[q - 1].wait()
                    for cp in loads[q + 1]:
                        cp.start()

                def add(r, carry, q=q):
                    tile = pl.ds(pl.multiple_of(r * tr, tr), tr)
                    a_buf[q % 2, tile, :] = (a_buf[q % 2, tile, :].astype(F32)
                                             + b_buf[q % 2, tile, :].astype(F32)).astype(x_ref.dtype)
                    return carry

                lax.fori_loop(0, m // tr, add, 0)
                stores[q].start()
            stores[2].wait()
            stores[3].wait()

        pl.run_scoped(through_vmem, pltpu.VMEM((2, m, n), x_ref.dtype), pltpu.VMEM((2, m, n), x_ref.dtype),
                      pltpu.SemaphoreType.DMA((6,)))
    elif phase == "send":
        keep.start()
        for cp in to_chips:
            cp.start()
    else:
        for j, (qx, qy) in enumerate(others):
            remote(rows(pair_ref, mine), rows(out_ref, 2 * qx + qy), 4 + j, (qx, qy, c)).wait_recv()
        for cp in to_sibling + to_chips:
            cp.wait_send()
        keep.wait()


_COMM_PHASES = {"gather": (_gather_phase, ("start", "forward", "finish")),
                "exchange": (_exchange_phase, ("start", "finish")),
                "pairsum_exchange": (_pairsum_exchange_phase, ("start", "reduce", "send", "finish"))}


def _comm_scratch(n_arrays):
    return [pltpu.SemaphoreType.DMA((n_arrays, 7)), pltpu.SemaphoreType.DMA((n_arrays, 7)),
            pltpu.SemaphoreType.DMA((n_arrays,))]


def _comm_run(kind, phases, x_refs, out_refs, send_sems, recv_sems, local_sems):
    fn = _COMM_PHASES[kind][0]
    per = len(out_refs) // len(x_refs)
    for phase in phases:
        for a, x_ref in enumerate(x_refs):
            outs = out_refs[a] if per == 1 else tuple(out_refs[per * a:per * (a + 1)])
            fn(phase, x_ref, outs, lambda k, a=a: send_sems.at[a, k], lambda k, a=a: recv_sems.at[a, k],
               local_sems.at[a])


def _comm_out_shapes(kind, arrays):
    if kind == "pairsum_exchange":
        return [jax.ShapeDtypeStruct((a.shape[0] // 2, a.shape[1]), a.dtype) for a in arrays for _ in range(3)]
    return [jax.ShapeDtypeStruct((N_DEV * a.shape[0], a.shape[1]) if kind == "gather" else a.shape, a.dtype)
            for a in arrays]


def _comm_call(kind, arrays, name):
    n = len(arrays)
    shapes = _comm_out_shapes(kind, arrays)

    def body(*refs):
        _comm_run(kind, _COMM_PHASES[kind][1], refs[:n], refs[n:n + len(shapes)], *refs[n + len(shapes):])

    return pl.pallas_call(body, name=name, out_shape=shapes, in_specs=[ANY] * n, out_specs=[ANY] * len(shapes),
                          scratch_shapes=_comm_scratch(n))(*arrays)


def _all_gather(xs, name):
    return _comm_call("gather", [xs], name)[0]


def _mod_fwd(c_all, w_ada):
    def body(c_ref, w_ref, o_ref):
        cv = c_ref[...]
        sc = cv * _sigmoid(cv)
        o_ref[0] = _dot(sc.astype(BF16), w_ref[0].astype(BF16))

    return pl.pallas_call(
        body, name="mod_fwd", grid=(DEPTH,),
        out_shape=jax.ShapeDtypeStruct((DEPTH, N_DEV, ADA_SHARD), F32),
        in_specs=[pl.BlockSpec((N_DEV, D_MODEL), lambda l: (0, 0)),
                  pl.BlockSpec((1, D_MODEL, ADA_SHARD), lambda l: (l, 0, 0))],
        out_specs=pl.BlockSpec((1, N_DEV, ADA_SHARD), lambda l: (l, 0, 0)),
        compiler_params=_params(("arbitrary",)),
    )(c_all, w_ada)


def _w_ada_grad(c_all, dmod_cols):
    def body(c_ref, d_ref, o_ref):
        cv = c_ref[...]
        sc = cv * _sigmoid(cv)
        o_ref[0] = lax.dot_general(sc, d_ref[0], (((0,), (0,)), ((), ())), precision=lax.Precision.HIGHEST,
                                   preferred_element_type=F32)

    return pl.pallas_call(
        body, name="w_ada_grad", grid=(DEPTH,),
        out_shape=jax.ShapeDtypeStruct((DEPTH, D_MODEL, ADA_SHARD), F32),
        in_specs=[pl.BlockSpec((N_DEV, D_MODEL), lambda l: (0, 0)),
                  pl.BlockSpec((1, N_DEV, ADA_SHARD), lambda l: (l, 0, 0))],
        out_specs=pl.BlockSpec((1, D_MODEL, ADA_SHARD), lambda l: (l, 0, 0)),
        compiler_params=_params(("arbitrary",)),
    )(c_all, dmod_cols)


def _comm_plumbing(comm):
    if not comm:
        return 0, [], []
    return len(comm[1]), _comm_out_shapes(*comm), _comm_scratch(len(comm[1]))


def _split_refs(refs, n_in, n_out, n_scratch, comm):
    ci, shapes, _ = _comm_plumbing(comm)
    co = len(shapes)
    a, b, c = n_in + ci, n_in + ci + n_out, n_in + ci + n_out + co
    return refs[:n_in], refs[a:b], refs[c:c + n_scratch], refs[n_in:a], refs[b:c], refs[c + n_scratch:]


def _prenorm_proj(x, g_pre, scale, shift, w_new, cos, sin_signed, comm=None, ts=256):
    s_len = x.shape[0]
    n_cin, c_shapes, c_scratch = _comm_plumbing(comm)

    def body(*refs):
        (x_ref, g_ref, sc_ref, sh_ref, w_ref, cos_ref, sin_ref), (pf_ref, pb_ref, h_ref), _, cin, cout, csem = (
            _split_refs(refs, 7, 3, 0, comm))
        comm_before, comm_after = _comm_hooks(comm, cin, cout, csem, steps=s_len // ts)
        comm_before()
        xv = x_ref[...]
        rstd = lax.rsqrt(jnp.mean(xv * xv, axis=-1, keepdims=True) + EPS)
        h = (xv * rstd * g_ref[...]) * (1.0 + sc_ref[...]) + sh_ref[...]
        hb = h.astype(BF16)
        h_ref[...] = hb
        for j in range(0, NP, 512):
            w = min(512, NP - j)
            acc = _dot(hb, w_ref[:, j:j + w])
            if COL_QB <= j < COL_VA:
                for lo in range(0, w, DIL_HD):
                    pf_ref[:, j + lo:j + lo + DIL_HD] = _rope(acc[:, lo:lo + DIL_HD], cos_ref[...], sin_ref[...])
            elif j < NP_F32:
                pf_ref[:, j:j + w] = acc
            else:
                pb_ref[:, j - NP_F32:j - NP_F32 + w] = acc.astype(BF16)
        comm_after()

    (g_pre, g_spec), (scale, sc_spec), (shift, sh_spec) = _rowvec(g_pre), _rowvec(scale), _rowvec(shift)
    return pl.pallas_call(
        body, name="prenorm_proj_comm" if comm else "prenorm_proj", grid=(s_len // ts,),
        out_shape=[jax.ShapeDtypeStruct((s_len, NP_F32), F32), jax.ShapeDtypeStruct((s_len, NP_BF16), BF16),
                   jax.ShapeDtypeStruct((s_len, D_MODEL), BF16)] + c_shapes,
        in_specs=[pl.BlockSpec((ts, D_MODEL), lambda i: (i, 0)), g_spec, sc_spec, sh_spec,
                  pl.BlockSpec((D_MODEL, NP), lambda i: (0, 0)), pl.BlockSpec((ts, DIL_HD), lambda i: (i, 0)),
                  pl.BlockSpec((ts, DIL_HD), lambda i: (i, 0))] + [ANY] * n_cin,
        out_specs=[pl.BlockSpec((ts, NP_F32), lambda i: (i, 0)), pl.BlockSpec((ts, NP_BF16), lambda i: (i, 0)),
                   pl.BlockSpec((ts, D_MODEL), lambda i: (i, 0))] + [ANY] * len(c_shapes),
        scratch_shapes=c_scratch,
        compiler_params=_params(("arbitrary",), 48),
    )(x, g_pre, scale, shift, w_new, cos, sin_signed, *(comm[1] if comm else []))


GLA_GROUP = 16


def _gla_group_rows(t):
    return [pl.ds(pl.multiple_of((t * GLA_GROUP + j) * GLA_CHUNK, GLA_CHUNK), GLA_CHUNK) for j in range(GLA_GROUP)]


def _gla_chunks_common(q_ref, k_ref, lr_ref, wgu_ref, bgu_ref, rows_list):
    c = GLA_CHUNK
    ri = lax.broadcasted_iota(jnp.int32, (c, c), 0)
    ci = lax.broadcasted_iota(jnp.int32, (c, c), 1)
    tril = (ri >= ci).astype(F32)
    zs = [_dot(lr_ref[rows, :], wgu_ref[...]) + bgu_ref[...] for rows in rows_list]
    las = [_log_sigmoid(z) * (1.0 / GLA_TAU) for z in zs]
    bs = [jnp.dot(tril, la, precision=lax.Precision.HIGHEST, preferred_element_type=F32) for la in las]
    out = []
    for rows, z, b in zip(rows_list, zs, bs):
        q = q_ref[rows, :] * (GLA_DK ** -0.5)
        k = k_ref[rows, :]
        bl = b[c - 1:c, :]
        out.append(dict(z=z, b=b, bl=bl, qe=q * jnp.exp(b), ke=k * jnp.exp(-b), kend=k * jnp.exp(bl - b),
                        dec=jnp.exp(bl)))
    return out, ri, ci


def _head_lane_mask(hh):
    return (lax.broadcasted_iota(jnp.int32, (1, LANE), 1) // GLA_DK) == hh


def _state_block_mask():
    r = lax.broadcasted_iota(jnp.int32, (2 * GLA_DV, LANE), 0) // GLA_DV
    cc = lax.broadcasted_iota(jnp.int32, (2 * GLA_DV, LANE), 1) // GLA_DK
    return r == cc


def _gla_fwd(pf, pb, wgu, bgu, layer, comm=None):
    s_len = pf.shape[0]
    nc = s_len // GLA_CHUNK
    ncomm = len(comm[1]) if comm else 0

    def body(*refs):
        q_ref, k_ref, v_ref, lr_ref, wgu_ref, bgu_ref = refs[:6]
        cin, (o_ref, st_ref), cout = refs[6:6 + ncomm], refs[6 + ncomm:8 + ncomm], refs[8 + ncomm:8 + 2 * ncomm]
        qe_s, cs_s, dec_s = refs[8 + 2 * ncomm:11 + 2 * ncomm]
        comm_before, comm_after = _comm_hooks(comm, cin, cout, refs[11 + 2 * ncomm:], steps=2)
        comm_before()
        bd = _state_block_mask()

        def local(t, carry):
            rows_list = _gla_group_rows(t)
            cm, ri, ci = _gla_chunks_common(q_ref, k_ref, lr_ref, wgu_ref, bgu_ref, rows_list)
            vs = [v_ref[rows, :] for rows in rows_list]
            kebs = [c["ke"].astype(BF16) for c in cm]
            a = [[jnp.where(ri >= ci, _dot_nt(jnp.where(_head_lane_mask(hh), c["qe"], 0.0).astype(BF16), keb), 0.0)
                  .astype(BF16) for hh in range(2)] for c, keb in zip(cm, kebs)]
            oi = [[_dot(ah[hh], v[:, hh * GLA_DV:(hh + 1) * GLA_DV]) for hh in range(2)] for ah, v in zip(a, vs)]
            cs = [jnp.where(bd, _dot_tn(v, c["kend"].astype(BF16)), 0.0) for c, v in zip(cm, vs)]
            for j, (rows, c) in enumerate(zip(rows_list, cm)):
                n = t * GLA_GROUP + j
                o_ref[rows, :] = jnp.concatenate(oi[j], axis=1)
                qe_s[rows, :] = c["qe"].astype(BF16)
                cs_s[n] = cs[j]
                dec_s[n] = jnp.broadcast_to(c["dec"], (8, LANE))
            return carry

        lax.fori_loop(0, nc // GLA_GROUP, local, 0)

        def scan(n, st):
            st_ref[0, n] = st.astype(BF16)
            return dec_s[n][0:1, :] * st + cs_s[n]

        lax.fori_loop(0, nc, scan, jnp.zeros((2 * GLA_DV, LANE), F32))

        def inter(t, carry):
            rows_list = _gla_group_rows(t)
            add = [_dot_nt(qe_s[rows, :], st_ref[0, t * GLA_GROUP + j]) for j, rows in enumerate(rows_list)]
            for rows, av in zip(rows_list, add):
                o_ref[rows, :] = o_ref[rows, :] + av
            return carry

        lax.fori_loop(0, nc // GLA_GROUP, inter, 0)
        comm_after()

    return pl.pallas_call(
        body, name="gla_fwd_comm" if comm else "gla_fwd", grid=(2,),
        out_shape=[jax.ShapeDtypeStruct((s_len, GLA_HEADS * GLA_DV), F32),
                   jax.ShapeDtypeStruct((2, nc, 2 * GLA_DV, LANE), BF16)] + (_comm_out_shapes(*comm) if comm else []),
        in_specs=[pl.BlockSpec((s_len, LANE), lambda g: (0, COL_QA // LANE + g)),
                  pl.BlockSpec((s_len, LANE), lambda g: (0, COL_KA // LANE + g)),
                  pl.BlockSpec((s_len, 2 * GLA_DV), lambda g: (0, (COL_VA - NP_F32) // (2 * GLA_DV) + g)),
                  pl.BlockSpec((s_len, LANE), lambda g: (0, (COL_LR - NP_F32) // LANE)),
                  pl.BlockSpec((None, LANE, LANE), lambda g: (layer, 0, g)),
                  pl.BlockSpec((None, 1, LANE), lambda g: (layer, 0, g))] + [ANY] * ncomm,
        out_specs=[pl.BlockSpec((s_len, 2 * GLA_DV), lambda g: (0, g)),
                   pl.BlockSpec((1, nc, 2 * GLA_DV, LANE), lambda g: (g, 0, 0, 0))] + [ANY] * ncomm,
        scratch_shapes=[pltpu.VMEM((s_len, LANE), BF16), pltpu.VMEM((nc, 2 * GLA_DV, LANE), F32),
                        pltpu.VMEM((nc, 8, LANE), F32)] + (_comm_scratch(ncomm) if comm else []),
        compiler_params=_params(("arbitrary",), 56),
    )(pf, pf, pb, pb, wgu, bgu.reshape(bgu.shape[0], 1, GU_COLS), *(comm[1] if comm else []))


def _rope_tables(s_len):
    inv_freq = ROPE_THETA ** (-jnp.arange(0, DIL_HD, 2, dtype=F32) / DIL_HD)
    ang = jnp.arange(s_len, dtype=F32)[:, None] * inv_freq[None, :]
    cos, sin = jnp.cos(ang), jnp.sin(ang)
    return jnp.concatenate([cos, cos], axis=1), jnp.concatenate([-sin, sin], axis=1)


def _rope(xv, cos, sin_signed):
    return xv * cos + pltpu.roll(xv, DIL_HD // 2, 1) * sin_signed


DIL_GROUP = 8


def _dil_pair_block(i, half, d, nblk, group=DIL_GROUP):
    nb = nblk // d
    j = i + half * (nblk // group)
    if nb >= 2 * group:
        r, n = j % d, j // d
    else:
        r, n = j // nb, j % nb
    kb = jnp.maximum(n - 1, 0)
    qs = r + d * DIL_BLOCK * n
    ks = r + d * DIL_BLOCK * kb
    return qs, ks, jnp.minimum(n, 1)


def _dil_fill_bias(bias):
    qi = lax.broadcasted_iota(jnp.int32, (DIL_BLOCK, 2 * DIL_BLOCK), 0)
    kj = lax.broadcasted_iota(jnp.int32, (DIL_BLOCK, 2 * DIL_BLOCK), 1)
    for sel in range(2):
        dist = qi - kj + DIL_BLOCK * sel
        bias[sel] = jnp.where((dist >= 0) & (dist <= DIL_BLOCK), 0.0, MASK_VALUE)


def _strided(start, size, d):
    return pl.ds(start, size) if d == 1 else pl.ds(start, size, stride=d)


def _comm_hooks(comm, cin, cout, csem, steps=DIL_HEADS):
    def before():
        if comm:
            @pl.when(pl.program_id(0) == 0)
            def _():
                _comm_run(comm[0], ("start",), cin, cout, *csem)

            if comm[0] == "gather":
                @pl.when(pl.program_id(0) == steps - 1)
                def _():
                    _comm_run(comm[0], ("forward",), cin, cout, *csem)

            if comm[0] == "pairsum_exchange":
                @pl.when(pl.program_id(0) == (1 if steps <= 4 else 2))
                def _():
                    _comm_run(comm[0], ("reduce", "send"), cin, cout, *csem)

    def after():
        if comm:
            @pl.when(pl.program_id(0) == steps - 1)
            def _():
                _comm_run(comm[0], ("finish",), cin, cout, *csem)

    return before, after


def _dil_fwd(pf, pb, comm=None):
    s_len = pf.shape[0]
    nblk = s_len // DIL_BLOCK
    prep_rows = 256
    scale = DIL_HD ** -0.5
    nc = len(comm[1]) if comm else 0

    def body(*refs):
        ((qf, kf, v_ref), (o_ref, lse_ref), (vf, o0, o1, o2, l0, l1, l2, bias), cin, cout, csem) = _split_refs(
            refs, 3, 2, 8, comm)
        comm_before, comm_after = _comm_hooks(comm, cin, cout, csem)
        comm_before()
        _dil_fill_bias(bias)

        def prep(t, carry):
            rows = pl.ds(pl.multiple_of(t * prep_rows, prep_rows), prep_rows)
            vf[rows, :] = v_ref[rows, :].astype(F32)
            return carry

        lax.fori_loop(0, s_len // prep_rows, prep, 0)
        for d, o_p, l_p in zip(DIL_DILATIONS, (o0, o1, o2), (l0, l1, l2)):
            if nblk // d == 2:
                units = DIL_GROUP // 2

                def whole(i, carry, d=d, o_p=o_p, l_p=l_p, units=units):
                    rows = [_strided(i + u * (d // units), 2 * DIL_BLOCK, d) for u in range(units)]
                    ld = [(qf[rw, :].astype(BF16), kf[rw, :].astype(BF16), vf[rw, :].astype(BF16)) for rw in rows]
                    both = bias[...].reshape(2 * DIL_BLOCK, 2 * DIL_BLOCK)
                    s = [_dot_nt(qb, kk) * scale + both for qb, kk, _ in ld]
                    m = [jnp.max(sv, axis=-1, keepdims=True) for sv in s]
                    p = [jnp.exp(sv - mv) for sv, mv in zip(s, m)]
                    den = [jnp.sum(pv, axis=-1, keepdims=True) for pv in p]
                    r = [_dot(pv.astype(BF16), vv) for pv, (_, _, vv) in zip(p, ld)]
                    for rv, dv, mv, rw in zip(r, den, m, rows):
                        o_p[rw, :] = rv / dv
                        l_p[rw, :] = jnp.broadcast_to(mv + jnp.log(dv), (2 * DIL_BLOCK, DIL_HD))
                    return carry

                lax.fori_loop(0, d // units, whole, 0)
                continue

            def pair(i, carry, d=d, o_p=o_p, l_p=l_p):
                idx = [_dil_pair_block(i, half, d, nblk, DIL_GROUP) for half in range(DIL_GROUP)]
                ld = [(qf[_strided(qs, DIL_BLOCK, d), :].astype(BF16),
                       kf[_strided(ks, 2 * DIL_BLOCK, d), :].astype(BF16),
                       vf[_strided(ks, 2 * DIL_BLOCK, d), :].astype(BF16)) for qs, ks, _ in idx]
                s = [_dot_nt(qb, kk) * scale + bias[sel] for (qb, kk, _), (_, _, sel) in zip(ld, idx)]
                m = [jnp.max(sv, axis=-1, keepdims=True) for sv in s]
                p = [jnp.exp(sv - mv) for sv, mv in zip(s, m)]
                den = [jnp.sum(pv, axis=-1, keepdims=True) for pv in p]
                r = [_dot(pv.astype(BF16), vv) for pv, (_, _, vv) in zip(p, ld)]
                for rv, dv, mv, (qs, _, _) in zip(r, den, m, idx):
                    o_p[_strided(qs, DIL_BLOCK, d), :] = rv / dv
                    l_p[_strided(qs, DIL_BLOCK, d), :] = jnp.broadcast_to(mv + jnp.log(dv), (DIL_BLOCK, DIL_HD))
                return carry

            lax.fori_loop(0, nblk // DIL_GROUP, pair, 0)

        def comb(t, carry):
            rows = pl.ds(pl.multiple_of(t * prep_rows, prep_rows), prep_rows)
            a0, a1, a2 = l0[rows, :], l1[rows, :], l2[rows, :]
            m = jnp.maximum(jnp.maximum(a0, a1), a2)
            e0, e1, e2 = jnp.exp(a0 - m), jnp.exp(a1 - m), jnp.exp(a2 - m)
            tot = e0 + e1 + e2
            o_ref[rows, :] = (e0 * o0[rows, :] + e1 * o1[rows, :] + e2 * o2[rows, :]) / tot
            lse_ref[rows, :] = m + jnp.log(tot)
            return carry

        lax.fori_loop(0, s_len // prep_rows, comb, 0)
        comm_after()

    head = lambda base: pl.BlockSpec((s_len, DIL_HD), lambda h: (0, base // DIL_HD + h))
    out = pl.BlockSpec((s_len, DIL_HD), lambda h: (0, h))
    shp = jax.ShapeDtypeStruct((s_len, DIL_HEADS * DIL_HD), F32)
    return pl.pallas_call(
        body, name="dil_fwd_comm" if comm else "dil_fwd", grid=(DIL_HEADS,),
        out_shape=[shp, shp] + (_comm_out_shapes(*comm) if comm else []),
        in_specs=[head(COL_QB), head(COL_KB), head(COL_VB - NP_F32)] + [ANY] * nc,
        out_specs=[out, out] + [ANY] * nc,
        scratch_shapes=[pltpu.VMEM((s_len, DIL_HD), F32) for _ in range(7)]
        + [pltpu.VMEM((2, DIL_BLOCK, 2 * DIL_BLOCK), F32)] + (_comm_scratch(nc) if comm else []),
        compiler_params=_params(("arbitrary",), 56),
    )(pf, pf, pb, *(comm[1] if comm else []))


def _silu_and_grad(z):
    sg = _sigmoid(z)
    return z * sg, sg * (1.0 + z * (1.0 - sg))


def _post_fwd(o_a, o_b, pf, g_heads, w_out, x, gate, g_post, target=None, ts=512):
    s_len = x.shape[0]
    half = GLA_HEADS * GLA_DV
    last = target is not None

    def body(*refs):
        oa_ref, ob_ref, z_ref, gh_ref, w_ref, x_ref, gate_ref, gp_ref = refs[:8]
        xo_ref, u_ref = refs[8 + last:10 + last]
        y_ref = refs[-1]
        for src, base in ((oa_ref, 0), (ob_ref, half)):
            for hh in range(4):
                lo = hh * LANE
                og = src[:, lo:lo + LANE]
                on = og * lax.rsqrt(jnp.mean(og * og, axis=-1, keepdims=True) + EPS)
                zg = z_ref[:, base + lo:base + lo + LANE].astype(F32)
                y_ref[:, base + lo:base + lo + LANE] = (on * gh_ref[:, base + lo:base + lo + LANE]
                                                        * (zg * _sigmoid(zg))).astype(BF16)
        u = _dot(y_ref[...], w_ref[...])
        u_ref[...] = u.astype(BF16)
        rstd = lax.rsqrt(jnp.mean(u * u, axis=-1, keepdims=True) + EPS)
        x_out = x_ref[...] + gate_ref[...] * (u * rstd * gp_ref[...])
        if last:
            t_ref, loss_ref = refs[8], refs[11]

            @pl.when(pl.program_id(0) == 0)
            def _():
                loss_ref[...] = jnp.zeros_like(loss_ref)

            e = x_out - t_ref[...]
            xo_ref[...] = e * (1.0 / D_MODEL)
            loss_ref[...] += 0.5 * jnp.sum(jnp.mean(e * e, axis=-1, keepdims=True))
        else:
            xo_ref[...] = x_out

    (g_heads, gh_spec), (gate, gate_spec), (g_post, gp_spec) = _rowvec(g_heads), _rowvec(gate), _rowvec(g_post)
    tile = pl.BlockSpec((ts, D_MODEL), lambda i: (i, 0))
    halft = pl.BlockSpec((ts, half), lambda i: (i, 0))
    return pl.pallas_call(
        body, name="post_fwd_loss" if last else "post_fwd", grid=(s_len // ts,),
        out_shape=[jax.ShapeDtypeStruct((s_len, D_MODEL), F32), jax.ShapeDtypeStruct((s_len, D_MODEL), BF16)]
        + ([jax.ShapeDtypeStruct((8, LANE), F32)] if last else []),
        in_specs=[halft, halft, tile, gh_spec, pl.BlockSpec((D_MODEL, D_MODEL), lambda i: (0, 0)), tile, gate_spec,
                  gp_spec] + ([tile] if last else []),
        out_specs=[tile, tile] + ([pl.BlockSpec((8, LANE), lambda i: (0, 0))] if last else []),
        scratch_shapes=[pltpu.VMEM((ts, D_MODEL), BF16)],
        compiler_params=_params(("arbitrary",), 40),
    )(o_a, o_b, pf, g_heads, w_out, x, gate, g_post, *([target] if last else []))


def _post_bwd(dxo, u, gate, g_post, w_out, o_a, o_b, pf, g_heads, ts=512):
    s_len = dxo.shape[0]
    half = GLA_HEADS * GLA_DV
    steps = s_len // ts

    def body(dx_ref, u_ref, gate_ref, gp_ref, w_ref, oa_ref, ob_ref, z_ref, gh_ref, do_ref, dz_ref, sums_ref, gw_ref,
             y_s, acc):
        @pl.when(pl.program_id(0) == 0)
        def _():
            sums_ref[...] = jnp.zeros_like(sums_ref)
            acc[...] = jnp.zeros_like(acc)

        dx = dx_ref[...]
        u = u_ref[...].astype(F32)
        rstd = lax.rsqrt(jnp.mean(u * u, axis=-1, keepdims=True) + EPS)
        un = u * rstd
        sums_ref[0:1, :] += jnp.sum(dx * (un * gp_ref[...]), axis=0, keepdims=True)
        drn = dx * gate_ref[...]
        sums_ref[1:2, :] += jnp.sum(drn * un, axis=0, keepdims=True)
        dun = drn * gp_ref[...]
        du = rstd * (dun - un * jnp.mean(dun * un, axis=-1, keepdims=True))
        dub = du.astype(BF16)
        dy = _dot_nt(dub, w_ref[...])
        for src, base in ((oa_ref, 0), (ob_ref, half)):
            for hh in range(4):
                lo = base + hh * LANE
                og = src[:, hh * LANE:(hh + 1) * LANE]
                rs = lax.rsqrt(jnp.mean(og * og, axis=-1, keepdims=True) + EPS)
                on = og * rs
                zg = z_ref[:, lo:lo + LANE].astype(F32)
                sz, dsz = _silu_and_grad(zg)
                gg = gh_ref[:, lo:lo + LANE]
                dyg = dy[:, lo:lo + LANE]
                y_s[:, lo:lo + LANE] = (on * gg * sz).astype(BF16)
                sums_ref[2:3, lo:lo + LANE] += jnp.sum(dyg * sz * on, axis=0, keepdims=True)
                dz_ref[:, lo:lo + LANE] = (dyg * on * gg * dsz).astype(BF16)
                don = dyg * gg * sz
                do_ref[:, lo:lo + LANE] = (rs * (don - on * jnp.mean(don * on, axis=-1, keepdims=True))).astype(BF16)
        acc[...] += _dot_tn(y_s[...], dub)

        @pl.when(pl.program_id(0) == steps - 1)
        def _():
            gw_ref[...] = acc[...].astype(BF16)

    (g_heads, gh_spec), (gate, gate_spec), (g_post, gp_spec) = _rowvec(g_heads), _rowvec(gate), _rowvec(g_post)
    tile = pl.BlockSpec((ts, D_MODEL), lambda i: (i, 0))
    halft = pl.BlockSpec((ts, half), lambda i: (i, 0))
    whole = pl.BlockSpec((D_MODEL, D_MODEL), lambda i: (0, 0))
    return pl.pallas_call(
        body, name="post_bwd", grid=(steps,),
        out_shape=(jax.ShapeDtypeStruct((s_len, D_MODEL), BF16), jax.ShapeDtypeStruct((s_len, D_MODEL), BF16),
                   jax.ShapeDtypeStruct((8, D_MODEL), F32), jax.ShapeDtypeStruct((D_MODEL, D_MODEL), BF16)),
        in_specs=[tile, tile, gate_spec, gp_spec, whole, halft, halft, tile, gh_spec],
        out_specs=(tile, tile, pl.BlockSpec((8, D_MODEL), lambda i: (0, 0)), whole),
        scratch_shapes=[pltpu.VMEM((ts, D_MODEL), BF16), pltpu.VMEM((D_MODEL, D_MODEL), F32)],
        compiler_params=_params(("arbitrary",), 48),
    )(dxo, u, gate, g_post, w_out, o_a, o_b, pf, g_heads)


def _gla_bwd(pf, pb, wgu, bgu, layer, states, do, comm=None):
    s_len = pf.shape[0]
    nc = s_len // GLA_CHUNK
    c = GLA_CHUNK
    n_cin, c_shapes, c_scratch = _comm_plumbing(comm)

    def body(*refs):
        ((q_ref, k_ref, v_ref, lr_ref, wgu_ref, bgu_ref, st_ref, do_ref),
         (dq_ref, dk_ref, dv_ref, dlr_ref, dwgu_ref, dbgu_ref), (ds_s, dec_s, dw_acc, db_acc),
         cin, cout, csem) = _split_refs(refs, 8, 6, 4, comm)
        comm_before, comm_after = _comm_hooks(comm, cin, cout, csem, steps=2)
        comm_before()
        dw_acc[...] = jnp.zeros_like(dw_acc)
        db_acc[...] = jnp.zeros_like(db_acc)
        bd = _state_block_mask()
        last_row = lax.broadcasted_iota(jnp.int32, (c, LANE), 0) == c - 1

        def local(t, carry):
            rows_list = _gla_group_rows(t)
            cm, _, _ = _gla_chunks_common(q_ref, k_ref, lr_ref, wgu_ref, bgu_ref, rows_list)
            loc = [jnp.where(bd, _dot_tn(do_ref[rows, :], cc["qe"].astype(BF16)), 0.0)
                   for rows, cc in zip(rows_list, cm)]
            for j, cc in enumerate(cm):
                ds_s[t * GLA_GROUP + j] = loc[j]
                dec_s[t * GLA_GROUP + j] = jnp.broadcast_to(cc["dec"], (8, LANE))
            return carry

        lax.fori_loop(0, nc // GLA_GROUP, local, 0)

        def scan(t, dst):
            n = nc - 1 - t
            loc = ds_s[n]
            ds_s[n] = dst
            return dec_s[n][0:1, :] * dst + loc

        lax.fori_loop(0, nc, scan, jnp.zeros((2 * GLA_DV, LANE), F32))

        def rest(t, carry):
            rows_list = _gla_group_rows(t)
            cm, ri, ci = _gla_chunks_common(q_ref, k_ref, lr_ref, wgu_ref, bgu_ref, rows_list)
            ns = [t * GLA_GROUP + j for j in range(GLA_GROUP)]
            vs = [v_ref[rows, :] for rows in rows_list]
            dobs = [do_ref[rows, :] for rows in rows_list]
            stbs = [st_ref[0, n] for n in ns]
            dsts = [ds_s[n] for n in ns]
            dstbs = [d.astype(BF16) for d in dsts]
            qebs = [cc["qe"].astype(BF16) for cc in cm]
            kebs = [cc["ke"].astype(BF16) for cc in cm]
            kendbs = [cc["kend"].astype(BF16) for cc in cm]
            hms = [_head_lane_mask(hh) for hh in range(2)]
            qehs = [[jnp.where(hm, cc["qe"], 0.0).astype(BF16) for hm in hms] for cc in cm]
            kehs = [[jnp.where(hm, cc["ke"], 0.0).astype(BF16) for hm in hms] for cc in cm]
            heads = lambda x: [x[:, hh * GLA_DV:(hh + 1) * GLA_DV] for hh in range(2)]
            vhs, dohs = [heads(v) for v in vs], [heads(d) for d in dobs]

            dqe0 = [_dot(dob, stb) for dob, stb in zip(dobs, stbs)]
            dkend = [_dot(v, dstb) for v, dstb in zip(vs, dstbs)]
            dv0 = [_dot_nt(kb, dstb) for kb, dstb in zip(kendbs, dstbs)]
            a_t = [[jnp.where(ci >= ri, _dot_nt(kehs[j][hh], qebs[j]), 0.0).astype(BF16) for hh in range(2)]
                   for j in range(GLA_GROUP)]
            da = [[jnp.where(ri >= ci, _dot_nt(dohs[j][hh], vhs[j][hh]), 0.0).astype(BF16) for hh in range(2)]
                  for j in range(GLA_GROUP)]
            da_t = [[jnp.where(ci >= ri, _dot_nt(vhs[j][hh], dohs[j][hh]), 0.0).astype(BF16) for hh in range(2)]
                    for j in range(GLA_GROUP)]
            dv1 = [[_dot(a_t[j][hh], dohs[j][hh]) for hh in range(2)] for j in range(GLA_GROUP)]
            dqe1 = [[_dot(da[j][hh], kebs[j]) for hh in range(2)] for j in range(GLA_GROUP)]
            dke1 = [[_dot(da_t[j][hh], qehs[j][hh]) for hh in range(2)] for j in range(GLA_GROUP)]

            dbs, dzs = [], []
            for j, (rows, cc) in enumerate(zip(rows_list, cm)):
                qe, ke, kend, b, bl = cc["qe"], cc["ke"], cc["kend"], cc["b"], cc["bl"]
                dqe = dqe0[j] + jnp.where(hms[0], dqe1[j][0], 0.0) + jnp.where(hms[1], dqe1[j][1], 0.0)
                dke = jnp.where(hms[0], dke1[j][0], 0.0) + jnp.where(hms[1], dke1[j][1], 0.0)
                dv_ref[rows, :] = (dv0[j] + jnp.concatenate(dv1[j], axis=1)).astype(BF16)
                dq_ref[rows, :] = (dqe * jnp.exp(b) * (GLA_DK ** -0.5)).astype(BF16)
                dk_ref[rows, :] = (dke * jnp.exp(-b) + dkend[j] * jnp.exp(bl - b)).astype(BF16)
                ddec = jnp.sum(dsts[j] * stbs[j].astype(F32), axis=0, keepdims=True)
                dbl = jnp.sum(dkend[j] * kend, axis=0, keepdims=True) + ddec * cc["dec"]
                dbs.append(dqe * qe - dke * ke - dkend[j] * kend + jnp.where(last_row, dbl, 0.0))
            triu = (ci >= ri).astype(F32)
            dlas = [jnp.dot(triu, db, precision=lax.Precision.HIGHEST, preferred_element_type=F32) for db in dbs]
            dzs = [dla * (1.0 / GLA_TAU) * _sigmoid(-cc["z"]) for dla, cc in zip(dlas, cm)]
            dzbs = [dz.astype(BF16) for dz in dzs]
            dlrs = [_dot_nt(dzb, wgu_ref[...]) for dzb in dzbs]
            dws = [_dot_tn(lr_ref[rows, :], dzb) for rows, dzb in zip(rows_list, dzbs)]
            for rows, dlr in zip(rows_list, dlrs):
                dlr_ref[0, rows, :] = dlr
            dw_acc[...] += functools.reduce(lambda x, y: x + y, dws)
            db_acc[0:1, :] += jnp.sum(functools.reduce(lambda x, y: x + y, dzs), axis=0, keepdims=True)
            return carry

        lax.fori_loop(0, nc // GLA_GROUP, rest, 0)
        dwgu_ref[...] = dw_acc[...]
        dbgu_ref[...] = db_acc[...]
        comm_after()

    pair = pl.BlockSpec((s_len, LANE), lambda g: (0, g))
    return pl.pallas_call(
        body, name="gla_bwd_comm" if comm else "gla_bwd", grid=(2,),
        out_shape=[jax.ShapeDtypeStruct((s_len, GU_COLS), BF16), jax.ShapeDtypeStruct((s_len, GU_COLS), BF16),
                   jax.ShapeDtypeStruct((s_len, GLA_HEADS * GLA_DV), BF16),
                   jax.ShapeDtypeStruct((2, s_len, LANE), F32),
                   jax.ShapeDtypeStruct((LANE, GU_COLS), F32), jax.ShapeDtypeStruct((8, GU_COLS), F32)] + c_shapes,
        in_specs=[pl.BlockSpec((s_len, LANE), lambda g: (0, COL_QA // LANE + g)),
                  pl.BlockSpec((s_len, LANE), lambda g: (0, COL_KA // LANE + g)),
                  pl.BlockSpec((s_len, 2 * GLA_DV), lambda g: (0, (COL_VA - NP_F32) // (2 * GLA_DV) + g)),
                  pl.BlockSpec((s_len, LANE), lambda g: (0, (COL_LR - NP_F32) // LANE)),
                  pl.BlockSpec((None, LANE, LANE), lambda g: (layer, 0, g)),
                  pl.BlockSpec((None, 1, LANE), lambda g: (layer, 0, g)),
                  pl.BlockSpec((1, nc, 2 * GLA_DV, LANE), lambda g: (g, 0, 0, 0)),
                  pl.BlockSpec((s_len, 2 * GLA_DV), lambda g: (0, g))] + [ANY] * n_cin,
        out_specs=[pair, pair, pl.BlockSpec((s_len, 2 * GLA_DV), lambda g: (0, g)),
                   pl.BlockSpec((1, s_len, LANE), lambda g: (g, 0, 0)),
                   pl.BlockSpec((LANE, LANE), lambda g: (0, g)), pl.BlockSpec((8, LANE), lambda g: (0, g))]
        + [ANY] * len(c_shapes),
        scratch_shapes=[pltpu.VMEM((nc, 2 * GLA_DV, LANE), F32), pltpu.VMEM((nc, 8, LANE), F32),
                        pltpu.VMEM((LANE, LANE), F32), pltpu.VMEM((8, LANE), F32)] + c_scratch,
        compiler_params=_params(("arbitrary",), 56),
    )(pf, pf, pb, pb, wgu, bgu.reshape(bgu.shape[0], 1, GU_COLS), states, do, *(comm[1] if comm else []))


def _dil_bwd(pf, pb, do, o_b, lse, comm=None):
    s_len = pf.shape[0]
    nblk = s_len // DIL_BLOCK
    prep_rows = 256
    scale = DIL_HD ** -0.5
    nc = len(comm[1]) if comm else 0

    def body(*refs):
        ((q_ref, kf, v_ref, do_ref, o_ref, lse_ref), (dq_ref, dk_ref, dv_ref),
         (qf, vf, dof, dl, dqa, dka, dva, bias), cin, cout, csem) = _split_refs(refs, 6, 3, 8, comm)
        comm_before, comm_after = _comm_hooks(comm, cin, cout, csem)
        comm_before()
        _dil_fill_bias(bias)

        def prep(t, carry):
            rows = pl.ds(pl.multiple_of(t * prep_rows, prep_rows), prep_rows)
            qf[rows, :] = q_ref[rows, :] * scale
            vf[rows, :] = v_ref[rows, :].astype(F32)
            dov = do_ref[rows, :].astype(F32)
            dof[rows, :] = dov
            dl[rows, :] = jnp.broadcast_to(jnp.sum(dov * o_ref[rows, :], axis=-1, keepdims=True), (prep_rows, DIL_HD))
            zero = jnp.zeros((prep_rows, DIL_HD), F32)
            dqa[rows, :] = zero
            dka[rows, :] = zero
            dva[rows, :] = zero
            return carry

        lax.fori_loop(0, s_len // prep_rows, prep, 0)

        for d in DIL_DILATIONS:
            if nblk // d == 2:
                units = DIL_GROUP // 2

                def whole(i, carry, d=d, units=units):
                    rows = [_strided(i + u * (d // units), 2 * DIL_BLOCK, d) for u in range(units)]
                    ld = [(qf[rw, :].astype(BF16), kf[rw, :].astype(BF16), vf[rw, :].astype(BF16),
                           dof[rw, :].astype(BF16)) for rw in rows]
                    both = bias[...].reshape(2 * DIL_BLOCK, 2 * DIL_BLOCK)
                    s = [_dot_nt(qb, kk) + both for qb, kk, _, _ in ld]
                    dp = [_dot_nt(dob, vv) for _, _, vv, dob in ld]
                    p = [jnp.exp(sv - lse_ref[rw, :][:, 0:1]) for sv, rw in zip(s, rows)]
                    ds = [(pv * (dpv - dl[rw, :][:, 0:1])).astype(BF16) for pv, dpv, rw in zip(p, dp, rows)]
                    pb = [pv.astype(BF16) for pv in p]
                    gq = [_dot(dsv, kk) for dsv, (_, kk, _, _) in zip(ds, ld)]
                    gk = [_dot_tn(dsv, qb) for dsv, (qb, _, _, _) in zip(ds, ld)]
                    gv = [_dot_tn(pv, dob) for pv, (_, _, _, dob) in zip(pb, ld)]
                    for rw, a, b, c in zip(rows, gq, gk, gv):
                        dqa[rw, :] += a
                        dka[rw, :] += b
                        dva[rw, :] += c
                    return carry

                lax.fori_loop(0, d // units, whole, 0)
                continue

            def pair(i, carry, d=d):
                idx = [_dil_pair_block(i, half, d, nblk) for half in range(DIL_GROUP)]
                rows = [(_strided(qs, DIL_BLOCK, d), _strided(ks, 2 * DIL_BLOCK, d)) for qs, ks, _ in idx]
                ld = [(qf[qr, :].astype(BF16), kf[kr, :].astype(BF16), vf[kr, :].astype(BF16),
                       dof[qr, :].astype(BF16)) for qr, kr in rows]
                s = [_dot_nt(qb, kk) + bias[sel] for (qb, kk, _, _), (_, _, sel) in zip(ld, idx)]
                dp = [_dot_nt(dob, vv) for _, _, vv, dob in ld]
                p = [jnp.exp(sv - lse_ref[qr, :][:, 0:1]) for sv, (qr, _) in zip(s, rows)]
                ds = [(pv * (dpv - dl[qr, :][:, 0:1])).astype(BF16) for pv, dpv, (qr, _) in zip(p, dp, rows)]
                pb = [pv.astype(BF16) for pv in p]
                gq = [_dot(dsv, kk) for dsv, (_, kk, _, _) in zip(ds, ld)]
                gk = [_dot_tn(dsv, qb) for dsv, (qb, _, _, _) in zip(ds, ld)]
                gv = [_dot_tn(pv, dob) for pv, (_, _, _, dob) in zip(pb, ld)]
                for (qr, kr), a, b, c in zip(rows, gq, gk, gv):
                    dqa[qr, :] += a
                    dka[kr, :] += b
                    dva[kr, :] += c
                return carry

            lax.fori_loop(0, nblk // DIL_GROUP, pair, 0)

        def fin(t, carry):
            rows = pl.ds(pl.multiple_of(t * prep_rows, prep_rows), prep_rows)
            dq_ref[rows, :] = (dqa[rows, :] * scale).astype(BF16)
            dk_ref[rows, :] = dka[rows, :].astype(BF16)
            dv_ref[rows, :] = dva[rows, :].astype(BF16)
            return carry

        lax.fori_loop(0, s_len // prep_rows, fin, 0)
        comm_after()

    head = lambda base: pl.BlockSpec((s_len, DIL_HD), lambda h: (0, base // DIL_HD + h))
    out = pl.BlockSpec((s_len, DIL_HD), lambda h: (0, h))
    shp = jax.ShapeDtypeStruct((s_len, DIL_HEADS * DIL_HD), BF16)
    return pl.pallas_call(
        body, name="dil_bwd_comm" if comm else "dil_bwd", grid=(DIL_HEADS,),
        out_shape=[shp, shp, shp] + (_comm_out_shapes(*comm) if comm else []),
        in_specs=[head(COL_QB), head(COL_KB), head(COL_VB - NP_F32),
                  pl.BlockSpec((s_len, DIL_HD), lambda h: (0, DIL_HEADS + h)), out, out] + [ANY] * nc,
        out_specs=[out, out, out] + [ANY] * len(_comm_plumbing(comm)[1]),
        scratch_shapes=[pltpu.VMEM((s_len, DIL_HD), F32) for _ in range(7)]
        + [pltpu.VMEM((2, DIL_BLOCK, 2 * DIL_BLOCK), F32)] + (_comm_scratch(nc) if comm else []),
        compiler_params=_params(("arbitrary",), 56),
    )(pf, pf, pb, do, o_b, lse, *(comm[1] if comm else []))


_PIECES = ((COL_Z, 1024), (COL_QA, 256), (COL_KA, 256), (COL_QB, 512), (COL_KB, 512), (COL_VA, 512), (COL_VB, 512),
           (COL_LR, 128))


def _unrope_piece(p_ref, col, cos, sin_signed):
    if col not in (COL_QB, COL_KB):
        return p_ref[...]
    blocks = []
    for lo in range(0, p_ref.shape[1], DIL_HD):
        g = p_ref[:, lo:lo + DIL_HD].astype(F32)
        blocks.append((g * cos - pltpu.roll(g, DIL_HD // 2, 1) * sin_signed).astype(BF16))
    return jnp.concatenate(blocks, axis=1)


def _in_bwd(pieces, w_new, x, dxo, g_pre, scale, cos, sin_signed, comm=None, ts=256):
    s_len = x.shape[0]
    nc = len(comm[1]) if comm else 0
    nco = len(_comm_out_shapes(*comm)) if comm else 0
    npc = len(_PIECES)

    def body(*refs):
        ins, (dx_ref, sums_ref), _, cin, cout, csem = _split_refs(refs, npc + 7, 2, 0, comm)
        p_refs = ins[:npc]
        w_ref, x_ref, dxo_ref, g_ref, sc_ref, cos_ref, sin_ref = ins[npc:]
        comm_before, comm_after = _comm_hooks(comm, cin, cout, csem, steps=s_len // ts)
        comm_before()

        @pl.when(pl.program_id(0) == 0)
        def _():
            sums_ref[...] = jnp.zeros_like(sums_ref)

        dh = jnp.zeros((ts, D_MODEL), F32)
        for p_ref, (col, width) in zip(p_refs, _PIECES):
            dh += _dot_nt(_unrope_piece(p_ref, col, cos_ref[...], sin_ref[...]), w_ref[:, col:col + width])
        xv = x_ref[...]
        rstd = lax.rsqrt(jnp.mean(xv * xv, axis=-1, keepdims=True) + EPS)
        xn = xv * rstd
        sums_ref[0:1, :] += jnp.sum(dh, axis=0, keepdims=True)
        sums_ref[1:2, :] += jnp.sum(dh * (xn * g_ref[...]), axis=0, keepdims=True)
        dr = dh * (1.0 + sc_ref[...])
        sums_ref[2:3, :] += jnp.sum(dr * xn, axis=0, keepdims=True)
        dxn = dr * g_ref[...]
        dx_ref[...] = dxo_ref[...] + rstd * (dxn - xn * jnp.mean(dxn * xn, axis=-1, keepdims=True))
        comm_after()

    (g_pre, g_spec), (scale, sc_spec) = _rowvec(g_pre), _rowvec(scale)
    tile = pl.BlockSpec((ts, D_MODEL), lambda i: (i, 0))
    return pl.pallas_call(
        body, name="in_bwd_comm" if comm else "in_bwd", grid=(s_len // ts,),
        out_shape=[jax.ShapeDtypeStruct((s_len, D_MODEL), F32), jax.ShapeDtypeStruct((8, D_MODEL), F32)]
        + (_comm_out_shapes(*comm) if comm else []),
        in_specs=[pl.BlockSpec((ts, width), lambda i: (i, 0)) for _, width in _PIECES]
        + [pl.BlockSpec((D_MODEL, NP), lambda i: (0, 0)), tile, tile, g_spec, sc_spec,
           pl.BlockSpec((ts, DIL_HD), lambda i: (i, 0)), pl.BlockSpec((ts, DIL_HD), lambda i: (i, 0))] + [ANY] * nc,
        out_specs=[tile, pl.BlockSpec((8, D_MODEL), lambda i: (0, 0))] + [ANY] * nco,
        scratch_shapes=_comm_scratch(nc) if comm else [],
        compiler_params=_params(("arbitrary",), 56),
    )(*pieces, w_new, x, dxo, g_pre, scale, cos, sin_signed, *(comm[1] if comm else []))


def _w_in_to_kernel(gathered, comm=None, tr=128):
    n_cin, c_shapes, c_scratch = _comm_plumbing(comm)
    n_parts = len(gathered)
    first = [sum(g.shape[1] for g in gathered[:p]) // tr for p in range(n_parts + 1)]

    def body(*refs):
        g_refs, (o_ref,), _, cin, cout, csem = _split_refs(refs, n_parts, 1, 0, comm)
        comm_before, comm_after = _comm_hooks(comm, cin, cout, csem, steps=D_MODEL // tr)
        comm_before()
        for p, g_ref in enumerate(g_refs):
            @pl.when((pl.program_id(0) >= first[p]) & (pl.program_id(0) < first[p + 1]))
            def _(g_ref=g_ref):
                cols = jnp.concatenate([g_ref[k].astype(F32) for k in range(N_DEV)], axis=1)
                pad = jnp.zeros((tr, LANE - GLA_LOWRANK), F32)
                o_ref[...] = jnp.concatenate(
                    [cols[:, 1024:1536], cols[:, 3088:3600], cols[:, 0:512], cols[:, 1552:2576], cols[:, 512:1024],
                     cols[:, 2576:3088], cols[:, 1536:1552], pad], axis=1).astype(BF16)
        comm_after()

    part = lambda p: pl.BlockSpec((N_DEV, tr, W_IN_SHARD),
                                  lambda i: (0, jnp.clip(i - first[p], 0, first[p + 1] - first[p] - 1), 0))
    return pl.pallas_call(
        body, name="w_in_to_kernel_comm" if comm else "w_in_to_kernel", grid=(D_MODEL // tr,),
        out_shape=[jax.ShapeDtypeStruct((D_MODEL, NP), BF16)] + c_shapes,
        in_specs=[part(p) for p in range(n_parts)] + [ANY] * n_cin,
        out_specs=[pl.BlockSpec((tr, NP), lambda i: (i, 0))] + [ANY] * len(c_shapes),
        scratch_shapes=c_scratch,
        compiler_params=_params(("arbitrary",)),
    )(*gathered, *(comm[1] if comm else []))


def _grad_w_in(h, pieces, cos, sin_signed, ts=512, tr=128):
    s_len = h.shape[0]
    steps = s_len // ts

    def body(*refs):
        h_ref, p_refs = refs[0], refs[1:1 + len(_PIECES)]
        cos_ref, sin_ref, o_ref, acc = refs[1 + len(_PIECES):]

        @pl.when(pl.program_id(0) == 0)
        def _():
            acc[...] = jnp.zeros_like(acc)

        hv = h_ref[...]
        for p_ref, (col, width) in zip(p_refs, _PIECES):
            acc[:, col:col + width] += _dot_tn(hv, _unrope_piece(p_ref, col, cos_ref[...], sin_ref[...]))

        @pl.when(pl.program_id(0) == steps - 1)
        def _():
            def rows_out(t, carry):
                rows = pl.ds(pl.multiple_of(t * tr, tr), tr)
                g = acc[rows, :]
                cols = jnp.concatenate(
                    [g[:, COL_QA:COL_QB], g[:, COL_VA:COL_VB], g[:, 0:512], g[:, COL_LR:COL_LR + GLA_LOWRANK],
                     g[:, COL_QB:COL_VA], g[:, COL_VB:COL_LR], g[:, 512:1024]], axis=1)
                for k in range(N_DEV):
                    o_ref[k, rows, :] = cols[:, W_IN_SHARD * k:W_IN_SHARD * (k + 1)].astype(BF16)
                return carry

            lax.fori_loop(0, D_MODEL // tr, rows_out, 0)

    return pl.pallas_call(
        body, name="grad_w_in", grid=(steps,),
        out_shape=jax.ShapeDtypeStruct((N_DEV, D_MODEL, W_IN_SHARD), BF16),
        in_specs=[pl.BlockSpec((ts, D_MODEL), lambda i: (i, 0))]
        + [pl.BlockSpec((ts, width), lambda i: (i, 0)) for _, width in _PIECES]
        + [pl.BlockSpec((ts, DIL_HD), lambda i: (i, 0)), pl.BlockSpec((ts, DIL_HD), lambda i: (i, 0))],
        out_specs=pl.BlockSpec((N_DEV, D_MODEL, W_IN_SHARD), lambda i: (0, 0, 0)),
        scratch_shapes=[pltpu.VMEM((D_MODEL, NP), F32)],
        compiler_params=_params(("arbitrary",), 56),
    )(h, *pieces, cos, sin_signed)


def _adam_math(w, g, m, v):
    m = ADAM_B1 * m + (1.0 - ADAM_B1) * g
    v = ADAM_B2 * v + (1.0 - ADAM_B2) * (g * g)
    m_hat = m / (1.0 - ADAM_B1 ** ADAM_STEP)
    v_hat = v / (1.0 - ADAM_B2 ** ADAM_STEP)
    delta = -ADAM_LR * (m_hat / (jnp.sqrt(v_hat) + ADAM_EPS) + ADAM_WD * w)
    return delta, m, v


def _adamw(w, parts, m, v, name, tr):
    r, cdim = w.shape
    n_parts = parts.shape[0]

    def body(w_ref, p_ref, m_ref, v_ref, g_ref, d_ref, nm_ref, nv_ref):
        g = p_ref[0].astype(F32)
        for k in range(1, n_parts):
            g = g + p_ref[k].astype(F32)
        g_ref[...] = g
        d_ref[...], nm_ref[...], nv_ref[...] = _adam_math(w_ref[...], g, m_ref[...], v_ref[...])

    tile = pl.BlockSpec((tr, cdim), lambda i: (i, 0))
    shp = jax.ShapeDtypeStruct((r, cdim), F32)
    return pl.pallas_call(
        body, name=name, grid=(r // tr,), out_shape=(shp, shp, shp, shp),
        in_specs=[tile, pl.BlockSpec((n_parts, tr, cdim), lambda i: (0, i, 0)), tile, tile],
        out_specs=(tile, tile, tile, tile),
        compiler_params=_params(("arbitrary",), 40),
    )(w, parts, m, v)


def _adamw_layers(w, parts, m, v, name, tr):
    n_layers, r, cdim = w.shape

    def body(*refs):
        w_ref, p_refs, (m_ref, v_ref) = refs[0], refs[1:1 + n_layers], refs[1 + n_layers:3 + n_layers]
        g_ref, d_ref, nm_ref, nv_ref = refs[3 + n_layers:]
        for l, p_ref in enumerate(p_refs):
            @pl.when(pl.program_id(0) == l)
            def _(p_ref=p_ref):
                g = p_ref[0].astype(F32)
                for k in range(1, p_ref.shape[0]):
                    g = g + p_ref[k].astype(F32)
                g_ref[0] = g
                d_ref[0], nm_ref[0], nv_ref[0] = _adam_math(w_ref[0], g, m_ref[0], v_ref[0])

    tile = pl.BlockSpec((1, tr, cdim), lambda l, i: (l, i, 0))
    part = lambda own: pl.BlockSpec((parts[own].shape[0], tr, cdim), lambda l, i: (0, jnp.where(l == own, i, 0), 0))
    shp = jax.ShapeDtypeStruct(w.shape, F32)
    return pl.pallas_call(
        body, name=name, grid=(n_layers, r // tr), out_shape=(shp, shp, shp, shp),
        in_specs=[tile] + [part(l) for l in range(n_layers)] + [tile, tile],
        out_specs=(tile, tile, tile, tile),
        compiler_params=_params(("arbitrary", "arbitrary"), 40),
    )(w, *parts, m, v)


def _row(vec, width):
    vec = vec.reshape(1, -1)
    return jnp.pad(vec, ((0, 0), (0, width - vec.shape[1])))


def kernel(x, c, w_ada, b_ada, g_pre, w_in, w_gate_up, b_gate_up, g_gla, g_dil, w_out, g_post, loss_target, m_w_ada, m_b_ada, m_g_pre, m_w_in, m_w_gate_up, m_b_gate_up, m_g_gla, m_g_dil, m_w_out, m_g_post, v_w_ada, v_b_ada, v_g_pre, v_w_in, v_w_gate_up, v_b_gate_up, v_g_gla, v_g_dil, v_w_out, v_g_post):
    px, py, pc = _my_position()
    me = _linear(px, py, pc)
    xs = x[0]
    target = loss_target[0]
    s_len = xs.shape[0]
    assert s_len % (DIL_BLOCK * max(DIL_DILATIONS) * 2) == 0 and xs.shape[1] == D_MODEL

    w_in_b, w_out_b = w_in.astype(BF16), w_out.astype(BF16)
    c_rows, wgu_all, w_in_all = _comm_call(
        "gather", [jnp.pad(c, ((0, 7), (0, 0))), w_gate_up.reshape(DEPTH * GLA_LOWRANK, GU_SHARD), w_in_b[0]],
        "gather_first")
    c_all = c_rows.reshape(N_DEV, 8, D_MODEL)[:, 0]
    mod_part = _mod_fwd(c_all, w_ada)
    w_new, mod_all = _w_in_to_kernel([w_in_all.reshape(N_DEV, D_MODEL, W_IN_SHARD)],
                                     comm=("gather", [mod_part.reshape(DEPTH * N_DEV, ADA_SHARD)]))
    mod_all = mod_all.reshape(N_DEV, DEPTH, N_DEV, ADA_SHARD)
    mod_mine = lax.dynamic_index_in_dim(mod_all, me, axis=2, keepdims=False)
    mod = jnp.transpose(mod_mine, (1, 0, 2)).reshape(DEPTH, 3 * D_MODEL) + b_ada
    wgu_full = jnp.transpose(wgu_all.reshape(N_DEV, DEPTH, GLA_LOWRANK, GU_SHARD), (1, 2, 0, 3)).reshape(
        DEPTH, GLA_LOWRANK, GU_COLS)
    wgu_pad = jnp.pad(wgu_full, ((0, 0), (0, LANE - GLA_LOWRANK), (0, 0))).astype(BF16)

    cos, sin_signed = _rope_tables(s_len)
    g_heads = jnp.concatenate([g_gla, g_dil], axis=1)

    saved = []
    xl = xs
    for l in range(DEPTH):
        shift, scale, gate = ((mod, l, k) for k in range(3))
        if l > 0:
            w_new = _w_in_to_kernel([half.reshape(N_DEV, D_MODEL // 2, W_IN_SHARD) for half in w_in_halves])[0]
        if l + 1 < DEPTH:
            own = [] if l > 0 else [w_out_b[0]]
            pf, pb, h, *arrived = _prenorm_proj(xl, (g_pre, l, 0), scale, shift, w_new, cos, sin_signed,
                                                comm=("gather", own + [w_in_b[l + 1, :D_MODEL // 2]]))
            w_out_l = arrived[0] if l == 0 else w_out_next
            top = arrived[-1]
        else:
            pf, pb, h = _prenorm_proj(xl, (g_pre, l, 0), scale, shift, w_new, cos, sin_signed)
            w_out_l = w_out_next
        o_a, states = _gla_fwd(pf, pb, wgu_pad, b_gate_up, l)
        if l + 1 < DEPTH:
            o_b, lse, bottom, w_out_next = _dil_fwd(pf, pb, comm=("gather", [w_in_b[l + 1, D_MODEL // 2:],
                                                                             w_out_b[l + 1]]))
            w_in_halves = (top, bottom)
        else:
            o_b, lse = _dil_fwd(pf, pb)
        if l + 1 < DEPTH:
            x_next, u = _post_fwd(o_a, o_b, pf, (g_heads, l, 0), w_out_l, xl, gate, (g_post, l, 0))
        else:
            dx, u, loss_part = _post_fwd(o_a, o_b, pf, (g_heads, l, 0), w_out_l, xl, gate, (g_post, l, 0),
                                         target=target)
        saved.append((xl, scale, gate, w_new, w_out_l, pf, pb, h, o_a, states, o_b, lse, u))
        xl = x_next

    small_rows = []
    gin_slots, gin_parts, gout_parts = None, [None] * DEPTH, [None] * DEPTH
    for l in reversed(range(DEPTH)):
        x_in, scale, gate, w_new, w_out_l, pf, pb, h, o_a, states, o_b, lse, u = saved[l]
        do, dz, sums_post, gout_slots = _post_bwd(dx, u, gate, (g_post, l, 0), w_out_l, o_a, o_b, pf, (g_heads, l, 0))
        dq_a, dk_a, dv_a, dlr2, dwgu, dbgu, arrived = _gla_bwd(pf, pb, wgu_pad, b_gate_up, l, states, do,
                                                               comm=("exchange", [gout_slots]))
        gout_parts[l] = arrived.reshape(N_DEV, OUT_SHARD, D_MODEL)
        if gin_slots is not None:
            dq_b, dk_b, dv_b, arrived, _, _ = _dil_bwd(pf, pb, do, o_b, lse, comm=("pairsum_exchange", [gin_slots]))
            gin_parts[l + 1] = arrived.reshape(N_DEV // 2, D_MODEL, W_IN_SHARD)
        else:
            dq_b, dk_b, dv_b = _dil_bwd(pf, pb, do, o_b, lse)
        dlr = (dlr2[0] + dlr2[1]).astype(BF16)
        pieces = (dz, dq_a, dk_a, dq_b, dk_b, dv_a, dv_b, dlr)
        gin_slots = _grad_w_in(h, pieces, cos, sin_signed).reshape(N_DEV * D_MODEL, W_IN_SHARD)
        if l == 0:
            dx, sums_in, arrived, _, _ = _in_bwd(pieces, w_new, x_in, dx, (g_pre, l, 0), scale, cos, sin_signed,
                                                 comm=("pairsum_exchange", [gin_slots]))
            gin_parts[0] = arrived.reshape(N_DEV // 2, D_MODEL, W_IN_SHARD)
        else:
            dx, sums_in = _in_bwd(pieces, w_new, x_in, dx, (g_pre, l, 0), scale, cos, sin_signed, ts=512)
        dmod = jnp.concatenate([sums_in[0], sums_in[1], sums_post[0]])
        vecs = jnp.concatenate([sums_in[2], sums_post[1], sums_post[2], dbgu[0]])
        small_rows[0:0] = [_row(dmod, 4096), _row(vecs, 4096), _row(dwgu[:GLA_LOWRANK], 4096)]
    grad_x = dx[None]

    flat = lambda a, rows: a.reshape(rows, a.shape[-1])
    r_ada = DEPTH * D_MODEL
    g_w_in, d_w_in, nm_w_in, nv_w_in = _adamw_layers(w_in, gin_parts, m_w_in, v_w_in, "adamw_w_in", 256)
    g_w_out, d_w_out, nm_w_out, nv_w_out = _adamw_layers(w_out, gout_parts, m_w_out, v_w_out, "adamw_w_out", 128)

    small_rows += [_row(loss_part[0, 0:1], 4096), jnp.zeros((1, 4096), F32)]
    small = _all_gather(jnp.concatenate(small_rows, axis=0), "gather_small").reshape(N_DEV, 8, 4096)
    dmod_all = jnp.stack([small[:, 0, :3 * D_MODEL], small[:, 3, :3 * D_MODEL]])
    dmod_cols = lax.dynamic_slice_in_dim(dmod_all, me * ADA_SHARD, ADA_SHARD, axis=2)
    gwa = _w_ada_grad(c_all, dmod_cols).reshape(1, r_ada, ADA_SHARD)
    g_w_ada, d_w_ada, nm_w_ada, nv_w_ada = (
        t.reshape(w_ada.shape) for t in _adamw(flat(w_ada, r_ada), gwa, flat(m_w_ada, r_ada), flat(v_w_ada, r_ada),
                                               "adamw_w_ada", 256))

    where = ((0, 0), (1, 0), (1, 1024), (1, 2048), (1, 2560), (1, 3072))
    replicated = [(b_ada, m_b_ada, v_b_ada), (g_pre, m_g_pre, v_g_pre), (g_post, m_g_post, v_g_post),
                  (g_gla, m_g_gla, v_g_gla), (g_dil, m_g_dil, v_g_dil), (b_gate_up, m_b_gate_up, v_b_gate_up)]
    updated, loss = _adamw_replicated(small, replicated, where, loss_at=(6, 0))
    ((g_b_ada, d_b_ada, nm_b_ada, nv_b_ada), (g_g_pre, d_g_pre, nm_g_pre, nv_g_pre),
     (g_g_post, d_g_post, nm_g_post, nv_g_post), (g_g_gla, d_g_gla, nm_g_gla, nv_g_gla),
     (g_g_dil, d_g_dil, nm_g_dil, nv_g_dil), (g_b_gu, d_b_gu, nm_b_gu, nv_b_gu)) = updated
    gu_parts = jnp.stack([small[:, 2], small[:, 5]], axis=1).reshape(N_DEV, DEPTH, GLA_LOWRANK, GU_COLS)
    gu_parts = lax.dynamic_slice_in_dim(gu_parts, me * GU_SHARD, GU_SHARD, axis=3).reshape(
        N_DEV, DEPTH * GLA_LOWRANK, GU_SHARD)
    r_gu = DEPTH * GLA_LOWRANK
    g_w_gu, d_w_gu, nm_w_gu, nv_w_gu = (
        t.reshape(w_gate_up.shape) for t in _adamw(flat(w_gate_up, r_gu), gu_parts, flat(m_w_gate_up, r_gu),
                                                   flat(v_w_gate_up, r_gu), "adamw_w_gate_up", r_gu))
    return (loss, grad_x,
            g_w_ada, g_b_ada, g_g_pre, g_w_in, g_w_gu, g_b_gu, g_g_gla, g_g_dil, g_w_out, g_g_post,
            d_w_ada, d_b_ada, d_g_pre, d_w_in, d_w_gu, d_b_gu, d_g_gla, d_g_dil, d_w_out, d_g_post,
            nm_w_ada, nm_b_ada, nm_g_pre, nm_w_in, nm_w_gu, nm_b_gu, nm_g_gla, nm_g_dil, nm_w_out, nm_g_post,
            nv_w_ada, nv_b_ada, nv_g_pre, nv_w_in, nv_w_gu, nv_b_gu, nv_g_gla, nv_g_dil, nv_w_out, nv_g_post)


def _adamw_replicated(small, params, where, loss_at):
    n_parts = small.shape[0]

    def body(*refs):
        s_ref, p_refs, o_refs = refs[0], refs[1:1 + 3 * len(params)], refs[1 + 3 * len(params):]
        total = s_ref[0]
        for k in range(1, n_parts):
            total = total + s_ref[k]
        for i, (row, col) in enumerate(where):
            w_ref, m_ref, v_ref = p_refs[3 * i:3 * i + 3]
            n = w_ref.shape[1]
            g = jnp.concatenate([total[row + 3 * l:row + 3 * l + 1, col:col + n] for l in range(DEPTH)], axis=0)
            o_refs[4 * i][...] = g
            o_refs[4 * i + 1][...], o_refs[4 * i + 2][...], o_refs[4 * i + 3][...] = _adam_math(
                w_ref[...], g, m_ref[...], v_ref[...])
        o_refs[-1][...] = jnp.broadcast_to(total[loss_at[0]:loss_at[0] + 1, loss_at[1]:loss_at[1] + 1], (8, LANE))

    flat = [a for p in params for a in p]
    shapes = [jax.ShapeDtypeStruct(p[0].shape, F32) for p in params for _ in range(4)]
    outs = pl.pallas_call(body, name="adamw_replicated",
                          out_shape=shapes + [jax.ShapeDtypeStruct((8, LANE), F32)])(small, *flat)
    return [tuple(outs[4 * i:4 * i + 4]) for i in range(len(params))], outs[-1][0, 0]
```

```python
import functools
import math

import jax
import jax.numpy as jnp
from jax import lax
from jax.experimental import pallas as pl
from jax.experimental.pallas import tpu as pltpu

F32 = jnp.float32
BF16 = jnp.bfloat16

N_DEV = 8
D_MODEL = 1024
DEPTH = 2
GLA_HEADS = 4
GLA_DK = 64
GLA_DV = 128
GLA_CHUNK = 64
GLA_TAU = 16.0
GLA_LOWRANK = 16
DIL_HEADS = 4
DIL_HD = 128
DIL_BLOCK = 128
DIL_DILATIONS = (1, 4, 16)
ROPE_THETA = 10000.0
EPS = 1e-6
IN_COLS = 3600
W_IN_SHARD = IN_COLS // N_DEV
ADA_SHARD = 3 * D_MODEL // N_DEV
OUT_SHARD = D_MODEL // N_DEV
GU_COLS = GLA_HEADS * GLA_DK
GU_SHARD = GU_COLS // N_DEV

ADAM_LR = 0.001
ADAM_B1 = 0.9
ADAM_B2 = 0.999
ADAM_EPS = 1e-08
ADAM_WD = 0.01
ADAM_STEP = 10

NP = 3712
COL_Z, COL_QA, COL_KA, COL_QB, COL_KB, COL_VA, COL_VB, COL_LR = 0, 1024, 1280, 1536, 2048, 2560, 3072, 3584
NP_F32 = COL_VA
NP_BF16 = NP - NP_F32
LANE = 128
MASK_VALUE = -1e30

MESH = pl.DeviceIdType.MESH
ANY = pl.BlockSpec(memory_space=pl.ANY)


def _params(sem=None, vmem_mb=None):
    kw = {}
    if sem is not None:
        kw["dimension_semantics"] = sem
    if vmem_mb is not None:
        kw["vmem_limit_bytes"] = vmem_mb * 1024 * 1024
    return pltpu.CompilerParams(**kw)


def _dot(a, b):
    return jnp.dot(a, b, preferred_element_type=F32)


def _dot_nt(a, b):
    return lax.dot_general(a, b, (((1,), (1,)), ((), ())), preferred_element_type=F32)


def _dot_tn(a, b):
    return lax.dot_general(a, b, (((0,), (0,)), ((), ())), preferred_element_type=F32)


def _sigmoid(z):
    return 1.0 / (1.0 + jnp.exp(-z))


def _log_sigmoid(z):
    return jnp.minimum(z, 0.0) - jnp.log(1.0 + jnp.exp(-jnp.abs(z)))


def _rowvec(v, width=D_MODEL):
    arr, row, cb = v
    return arr.reshape(arr.shape[0], 1, arr.shape[1]), pl.BlockSpec((None, 1, width), lambda *_: (row, 0, cb))


def _my_position():
    return lax.axis_index("x"), lax.axis_index("y"), lax.axis_index("c")


def _linear(px, py, pc):
    return 4 * px + 2 * py + pc


def _gather_phase(phase, x_ref, out_ref, send_sem, recv_sem, local_sem):
    m = x_ref.shape[0]
    x, y, c = _my_position()
    me, sibling = (x, y, c), (x, y, 1 - c)
    chips = [(1 - x, y), (x, 1 - y), (1 - x, 1 - y)]

    def rows(px, py, pc):
        return out_ref.at[pl.ds(_linear(px, py, pc) * m, m), :]

    def copy(k, block, to, src=None):
        return pltpu.make_async_remote_copy(
            src_ref=rows(*block) if src is None else src, dst_ref=rows(*block),
            send_sem=send_sem(k), recv_sem=recv_sem(k), device_id=to, device_id_type=MESH)

    mine = pltpu.make_async_copy(x_ref, rows(*me), local_sem)
    first = [copy(0, me, sibling, src=x_ref)] + [copy(1 + j, me, (*chip, c), src=x_ref) for j, chip in enumerate(chips)]
    passed = [copy(4 + j, (*chip, c), sibling) for j, chip in enumerate(chips)]
    if phase == "start":
        mine.start()
        for cp in first:
            cp.start()
    elif phase == "forward":
        for j, chip in enumerate(chips):
            copy(1 + j, (*chip, c), me).wait_recv()
            passed[j].start()
    else:
        copy(0, sibling, me).wait_recv()
        for j, chip in enumerate(chips):
            copy(4 + j, (*chip, 1 - c), me).wait_recv()
        for cp in first + passed:
            cp.wait_send()
        mine.wait()


def _exchange_phase(phase, x_ref, out_ref, send_sem, recv_sem, local_sem):
    m = x_ref.shape[0] // N_DEV
    x, y, c = _my_position()
    me = _linear(x, y, c)

    def rows(ref, idx):
        return ref.at[pl.ds(idx * m, m), :]

    peers = [(1 - x if j & 4 else x, 1 - y if j & 2 else y, 1 - c if j & 1 else c) for j in range(1, N_DEV)]
    local = pltpu.make_async_copy(rows(x_ref, me), rows(out_ref, me), local_sem)
    sends = [pltpu.make_async_remote_copy(
        src_ref=rows(x_ref, _linear(*peer)), dst_ref=rows(out_ref, me),
        send_sem=send_sem(j), recv_sem=recv_sem(j), device_id=peer, device_id_type=MESH) for j, peer in enumerate(peers)]
    if phase == "start":
        local.start()
        for cp in sends:
            cp.start()
    else:
        for j, peer in enumerate(peers):
            pltpu.make_async_remote_copy(
                src_ref=rows(x_ref, _linear(*peer)), dst_ref=rows(out_ref, _linear(*peer)),
                send_sem=send_sem(j), recv_sem=recv_sem(j), device_id=peer, device_id_type=MESH).wait_recv()
        for cp in sends:
            cp.wait_send()
        local.wait()


def _pairsum_exchange_phase(phase, x_ref, out_refs, send_sem, recv_sem, local_sem):
    out_ref, stage_ref, pair_ref = out_refs
    m, n = x_ref.shape[0] // N_DEV, x_ref.shape[1]
    x, y, c = _my_position()
    mine = 2 * x + y
    chips = [(qx, qy) for qx in range(2) for qy in range(2)]
    others = [(1 - x, y), (x, 1 - y), (1 - x, 1 - y)]

    def rows(ref, idx):
        return ref.at[pl.ds(idx * m, m), :]

    def remote(src, dst, k, to):
        return pltpu.make_async_remote_copy(src_ref=src, dst_ref=dst, send_sem=send_sem(k), recv_sem=recv_sem(k),
                                            device_id=to, device_id_type=MESH)

    to_sibling = [remote(rows(x_ref, _linear(qx, qy, 1 - c)), rows(stage_ref, q), q, (x, y, 1 - c))
                  for q, (qx, qy) in enumerate(chips)]
    to_chips = [remote(rows(pair_ref, 2 * qx + qy), rows(out_ref, mine), 4 + j, (qx, qy, c))
                for j, (qx, qy) in enumerate(others)]
    keep = pltpu.make_async_copy(rows(pair_ref, mine), rows(out_ref, mine), local_sem)
    if phase == "start":
        for cp in to_sibling:
            cp.start()
    elif phase == "reduce":
        for cp in to_sibling:
            cp.wait_recv()

        def through_vmem(a_buf, b_buf, sems):
            tr = 128
            loads = [(pltpu.make_async_copy(rows(x_ref, _linear(qx, qy, c)), a_buf.at[q % 2], sems.at[q % 2]),
                      pltpu.make_async_copy(rows(stage_ref, q), b_buf.at[q % 2], sems.at[2 + q % 2]))
                     for q, (qx, qy) in enumerate(chips)]
            stores = [pltpu.make_async_copy(a_buf.at[q % 2], rows(pair_ref, q), sems.at[4 + q % 2]) for q in range(4)]
            for cp in loads[0]:
                cp.start()
            for q in range(4):
                for cp in loads[q]:
                    cp.wait()
                if q + 1 < 4:
                    if q >= 1:
                        stores[q - 1].wait()
                    for cp in loads[q + 1]:
                        cp.start()

                def add(r, carry, q=q):
                    tile = pl.ds(pl.multiple_of(r * tr, tr), tr)
                    a_buf[q % 2, tile, :] = (a_buf[q % 2, tile, :].astype(F32)
                                             + b_buf[q % 2, tile, :].astype(F32)).astype(x_ref.dtype)
                    return carry

                lax.fori_loop(0, m // tr, add, 0)
                stores[q].start()
            stores[2].wait()
            stores[3].wait()

        pl.run_scoped(through_vmem, pltpu.VMEM((2, m, n), x_ref.dtype), pltpu.VMEM((2, m, n), x_ref.dtype),
                      pltpu.SemaphoreType.DMA((6,)))
    elif phase == "send":
        keep.start()
        for cp in to_chips:
            cp.start()
    else:
        for j, (qx, qy) in enumerate(others):
            remote(rows(pair_ref, mine), rows(out_ref, 2 * qx + qy), 4 + j, (qx, qy, c)).wait_recv()
        for cp in to_sibling + to_chips:
            cp.wait_send()
        keep.wait()


_COMM_PHASES = {"gather": (_gather_phase, ("start", "forward", "finish")),
                "exchange": (_exchange_phase, ("start", "finish")),
                "pairsum_exchange": (_pairsum_exchange_phase, ("start", "reduce", "send", "finish"))}


def _comm_scratch(n_arrays):
    return [pltpu.SemaphoreType.DMA((n_arrays, 7)), pltpu.SemaphoreType.DMA((n_arrays, 7)),
            pltpu.SemaphoreType.DMA((n_arrays,))]


def _comm_run(kind, phases, x_refs, out_refs, send_sems, recv_sems, local_sems):
    fn = _COMM_PHASES[kind][0]
    per = len(out_refs) // len(x_refs)
    for phase in phases:
        for a, x_ref in enumerate(x_refs):
            outs = out_refs[a] if per == 1 else tuple(out_refs[per * a:per * (a + 1)])
            fn(phase, x_ref, outs, lambda k, a=a: send_sems.at[a, k], lambda k, a=a: recv_sems.at[a, k],
               local_sems.at[a])


def _comm_out_shapes(kind, arrays):
    if kind == "pairsum_exchange":
        return [jax.ShapeDtypeStruct((a.shape[0] // 2, a.shape[1]), a.dtype) for a in arrays for _ in range(3)]
    return [jax.ShapeDtypeStruct((N_DEV * a.shape[0], a.shape[1]) if kind == "gather" else a.shape, a.dtype)
            for a in arrays]


def _comm_call(kind, arrays, name):
    n = len(arrays)
    shapes = _comm_out_shapes(kind, arrays)

    def body(*refs):
        _comm_run(kind, _COMM_PHASES[kind][1], refs[:n], refs[n:n + len(shapes)], *refs[n + len(shapes):])

    return pl.pallas_call(body, name=name, out_shape=shapes, in_specs=[ANY] * n, out_specs=[ANY] * len(shapes),
                          scratch_shapes=_comm_scratch(n))(*arrays)


def _all_gather(xs, name):
    return _comm_call("gather", [xs], name)[0]


def _mod_fwd(c_all, w_ada):
    def body(c_ref, w_ref, o_ref):
        cv = c_ref[...]
        sc = cv * _sigmoid(cv)
        o_ref[0] = _dot(sc.astype(BF16), w_ref[0].astype(BF16))

    return pl.pallas_call(
        body, name="mod_fwd", grid=(DEPTH,),
        out_shape=jax.ShapeDtypeStruct((DEPTH, N_DEV, ADA_SHARD), F32),
        in_specs=[pl.BlockSpec((N_DEV, D_MODEL), lambda l: (0, 0)),
                  pl.BlockSpec((1, D_MODEL, ADA_SHARD), lambda l: (l, 0, 0))],
        out_specs=pl.BlockSpec((1, N_DEV, ADA_SHARD), lambda l: (l, 0, 0)),
        compiler_params=_params(("arbitrary",)),
    )(c_all, w_ada)


def _w_ada_grad(c_all, dmod_cols):
    def body(c_ref, d_ref, o_ref):
        cv = c_ref[...]
        sc = cv * _sigmoid(cv)
        o_ref[0] = lax.dot_general(sc, d_ref[0], (((0,), (0,)), ((), ())), precision=lax.Precision.HIGHEST,
                                   preferred_element_type=F32)

    return pl.pallas_call(
        body, name="w_ada_grad", grid=(DEPTH,),
        out_shape=jax.ShapeDtypeStruct((DEPTH, D_MODEL, ADA_SHARD), F32),
        in_specs=[pl.BlockSpec((N_DEV, D_MODEL), lambda l: (0, 0)),
                  pl.BlockSpec((1, N_DEV, ADA_SHARD), lambda l: (l, 0, 0))],
        out_specs=pl.BlockSpec((1, D_MODEL, ADA_SHARD), lambda l: (l, 0, 0)),
        compiler_params=_params(("arbitrary",)),
    )(c_all, dmod_cols)


def _comm_plumbing(comm):
    if not comm:
        return 0, [], []
    return len(comm[1]), _comm_out_shapes(*comm), _comm_scratch(len(comm[1]))


def _split_refs(refs, n_in, n_out, n_scratch, comm):
    ci, shapes, _ = _comm_plumbing(comm)
    co = len(shapes)
    a, b, c = n_in + ci, n_in + ci + n_out, n_in + ci + n_out + co
    return refs[:n_in], refs[a:b], refs[c:c + n_scratch], refs[n_in:a], refs[b:c], refs[c + n_scratch:]


def _prenorm_proj(x, g_pre, scale, shift, w_new, cos, sin_signed, comm=None, ts=256):
    s_len = x.shape[0]
    n_cin, c_shapes, c_scratch = _comm_plumbing(comm)

    def body(*refs):
        (x_ref, g_ref, sc_ref, sh_ref, w_ref, cos_ref, sin_ref), (pf_ref, pb_ref, h_ref), _, cin, cout, csem = (
            _split_refs(refs, 7, 3, 0, comm))
        comm_before, comm_after = _comm_hooks(comm, cin, cout, csem, steps=s_len // ts)
        comm_before()
        xv = x_ref[...]
        rstd = lax.rsqrt(jnp.mean(xv * xv, axis=-1, keepdims=True) + EPS)
        h = (xv * rstd * g_ref[...]) * (1.0 + sc_ref[...]) + sh_ref[...]
        hb = h.astype(BF16)
        h_ref[...] = hb
        for j in range(0, NP, 512):
            w = min(512, NP - j)
            acc = _dot(hb, w_ref[:, j:j + w])
            if COL_QB <= j < COL_VA:
                for lo in range(0, w, DIL_HD):
                    pf_ref[:, j + lo:j + lo + DIL_HD] = _rope(acc[:, lo:lo + DIL_HD], cos_ref[...], sin_ref[...])
            elif j < NP_F32:
                pf_ref[:, j:j + w] = acc
            else:
                pb_ref[:, j - NP_F32:j - NP_F32 + w] = acc.astype(BF16)
        comm_after()

    (g_pre, g_spec), (scale, sc_spec), (shift, sh_spec) = _rowvec(g_pre), _rowvec(scale), _rowvec(shift)
    return pl.pallas_call(
        body, name="prenorm_proj_comm" if comm else "prenorm_proj", grid=(s_len // ts,),
        out_shape=[jax.ShapeDtypeStruct((s_len, NP_F32), F32), jax.ShapeDtypeStruct((s_len, NP_BF16), BF16),
                   jax.ShapeDtypeStruct((s_len, D_MODEL), BF16)] + c_shapes,
        in_specs=[pl.BlockSpec((ts, D_MODEL), lambda i: (i, 0)), g_spec, sc_spec, sh_spec,
                  pl.BlockSpec((D_MODEL, NP), lambda i: (0, 0)), pl.BlockSpec((ts, DIL_HD), lambda i: (i, 0)),
                  pl.BlockSpec((ts, DIL_HD), lambda i: (i, 0))] + [ANY] * n_cin,
        out_specs=[pl.BlockSpec((ts, NP_F32), lambda i: (i, 0)), pl.BlockSpec((ts, NP_BF16), lambda i: (i, 0)),
                   pl.BlockSpec((ts, D_MODEL), lambda i: (i, 0))] + [ANY] * len(c_shapes),
        scratch_shapes=c_scratch,
        compiler_params=_params(("arbitrary",), 48),
    )(x, g_pre, scale, shift, w_new, cos, sin_signed, *(comm[1] if comm else []))


GLA_GROUP = 16


def _gla_group_rows(t):
    return [pl.ds(pl.multiple_of((t * GLA_GROUP + j) * GLA_CHUNK, GLA_CHUNK), GLA_CHUNK) for j in range(GLA_GROUP)]


def _gla_chunks_common(q_ref, k_ref, lr_ref, wgu_ref, bgu_ref, rows_list):
    c = GLA_CHUNK
    ri = lax.broadcasted_iota(jnp.int32, (c, c), 0)
    ci = lax.broadcasted_iota(jnp.int32, (c, c), 1)
    tril = (ri >= ci).astype(F32)
    zs = [_dot(lr_ref[rows, :], wgu_ref[...]) + bgu_ref[...] for rows in rows_list]
    las = [_log_sigmoid(z) * (1.0 / GLA_TAU) for z in zs]
    bs = [jnp.dot(tril, la, precision=lax.Precision.HIGHEST, preferred_element_type=F32) for la in las]
    out = []
    for rows, z, b in zip(rows_list, zs, bs):
        q = q_ref[rows, :] * (GLA_DK ** -0.5)
        k = k_ref[rows, :]
        bl = b[c - 1:c, :]
        out.append(dict(z=z, b=b, bl=bl, qe=q * jnp.exp(b), ke=k * jnp.exp(-b), kend=k * jnp.exp(bl - b),
                        dec=jnp.exp(bl)))
    return out, ri, ci


def _head_lane_mask(hh):
    return (lax.broadcasted_iota(jnp.int32, (1, LANE), 1) // GLA_DK) == hh


def _state_block_mask():
    r = lax.broadcasted_iota(jnp.int32, (2 * GLA_DV, LANE), 0) // GLA_DV
    cc = lax.broadcasted_iota(jnp.int32, (2 * GLA_DV, LANE), 1) // GLA_DK
    return r == cc


def _gla_fwd(pf, pb, wgu, bgu, layer, comm=None):
    s_len = pf.shape[0]
    nc = s_len // GLA_CHUNK
    ncomm = len(comm[1]) if comm else 0

    def body(*refs):
        q_ref, k_ref, v_ref, lr_ref, wgu_ref, bgu_ref = refs[:6]
        cin, (o_ref, st_ref), cout = refs[6:6 + ncomm], refs[6 + ncomm:8 + ncomm], refs[8 + ncomm:8 + 2 * ncomm]
        qe_s, cs_s, dec_s = refs[8 + 2 * ncomm:11 + 2 * ncomm]
        comm_before, comm_after = _comm_hooks(comm, cin, cout, refs[11 + 2 * ncomm:], steps=2)
        comm_before()
        bd = _state_block_mask()

        def local(t, carry):
            rows_list = _gla_group_rows(t)
            cm, ri, ci = _gla_chunks_common(q_ref, k_ref, lr_ref, wgu_ref, bgu_ref, rows_list)
            vs = [v_ref[rows, :] for rows in rows_list]
            kebs = [c["ke"].astype(BF16) for c in cm]
            a = [[jnp.where(ri >= ci, _dot_nt(jnp.where(_head_lane_mask(hh), c["qe"], 0.0).astype(BF16), keb), 0.0)
                  .astype(BF16) for hh in range(2)] for c, keb in zip(cm, kebs)]
            oi = [[_dot(ah[hh], v[:, hh * GLA_DV:(hh + 1) * GLA_DV]) for hh in range(2)] for ah, v in zip(a, vs)]
            cs = [jnp.where(bd, _dot_tn(v, c["kend"].astype(BF16)), 0.0) for c, v in zip(cm, vs)]
            for j, (rows, c) in enumerate(zip(rows_list, cm)):
                n = t * GLA_GROUP + j
                o_ref[rows, :] = jnp.concatenate(oi[j], axis=1)
                qe_s[rows, :] = c["qe"].astype(BF16)
                cs_s[n] = cs[j]
                dec_s[n] = jnp.broadcast_to(c["dec"], (8, LANE))
            return carry

        lax.fori_loop(0, nc // GLA_GROUP, local, 0)

        def scan(n, st):
            st_ref[0, n] = st.astype(BF16)
            return dec_s[n][0:1, :] * st + cs_s[n]

        lax.fori_loop(0, nc, scan, jnp.zeros((2 * GLA_DV, LANE), F32))

        def inter(t, carry):
            rows_list = _gla_group_rows(t)
            add = [_dot_nt(qe_s[rows, :], st_ref[0, t * GLA_GROUP + j]) for j, rows in enumerate(rows_list)]
            for rows, av in zip(rows_list, add):
                o_ref[rows, :] = o_ref[rows, :] + av
            return carry

        lax.fori_loop(0, nc // GLA_GROUP, inter, 0)
        comm_after()

    return pl.pallas_call(
        body, name="gla_fwd_comm" if comm else "gla_fwd", grid=(2,),
        out_shape=[jax.ShapeDtypeStruct((s_len, GLA_HEADS * GLA_DV), F32),
                   jax.ShapeDtypeStruct((2, nc, 2 * GLA_DV, LANE), BF16)] + (_comm_out_shapes(*comm) if comm else []),
        in_specs=[pl.BlockSpec((s_len, LANE), lambda g: (0, COL_QA // LANE + g)),
                  pl.BlockSpec((s_len, LANE), lambda g: (0, COL_KA // LANE + g)),
                  pl.BlockSpec((s_len, 2 * GLA_DV), lambda g: (0, (COL_VA - NP_F32) // (2 * GLA_DV) + g)),
                  pl.BlockSpec((s_len, LANE), lambda g: (0, (COL_LR - NP_F32) // LANE)),
                  pl.BlockSpec((None, LANE, LANE), lambda g: (layer, 0, g)),
                  pl.BlockSpec((None, 1, LANE), lambda g: (layer, 0, g))] + [ANY] * ncomm,
        out_specs=[pl.BlockSpec((s_len, 2 * GLA_DV), lambda g: (0, g)),
                   pl.BlockSpec((1, nc, 2 * GLA_DV, LANE), lambda g: (g, 0, 0, 0))] + [ANY] * ncomm,
        scratch_shapes=[pltpu.VMEM((s_len, LANE), BF16), pltpu.VMEM((nc, 2 * GLA_DV, LANE), F32),
                        pltpu.VMEM((nc, 8, LANE), F32)] + (_comm_scratch(ncomm) if comm else []),
        compiler_params=_params(("arbitrary",), 56),
    )(pf, pf, pb, pb, wgu, bgu.reshape(bgu.shape[0], 1, GU_COLS), *(comm[1] if comm else []))


def _rope_tables(s_len):
    inv_freq = ROPE_THETA ** (-jnp.arange(0, DIL_HD, 2, dtype=F32) / DIL_HD)
    ang = jnp.arange(s_len, dtype=F32)[:, None] * inv_freq[None, :]
    cos, sin = jnp.cos(ang), jnp.sin(ang)
    return jnp.concatenate([cos, cos], axis=1), jnp.concatenate([-sin, sin], axis=1)


def _rope(xv, cos, sin_signed):
    return xv * cos + pltpu.roll(xv, DIL_HD // 2, 1) * sin_signed


DIL_GROUP = 8


def _dil_pair_block(i, half, d, nblk, group=DIL_GROUP):
    nb = nblk // d
    j = i + half * (nblk // group)
    if nb >= 2 * group:
        r, n = j % d, j // d
    else:
        r, n = j // nb, j % nb
    kb = jnp.maximum(n - 1, 0)
    qs = r + d * DIL_BLOCK * n
    ks = r + d * DIL_BLOCK * kb
    return qs, ks, jnp.minimum(n, 1)


def _dil_fill_bias(bias):
    qi = lax.broadcasted_iota(jnp.int32, (DIL_BLOCK, 2 * DIL_BLOCK), 0)
    kj = lax.broadcasted_iota(jnp.int32, (DIL_BLOCK, 2 * DIL_BLOCK), 1)
    for sel in range(2):
        dist = qi - kj + DIL_BLOCK * sel
        bias[sel] = jnp.where((dist >= 0) & (dist <= DIL_BLOCK), 0.0, MASK_VALUE)


def _strided(start, size, d):
    return pl.ds(start, size) if d == 1 else pl.ds(start, size, stride=d)


def _comm_hooks(comm, cin, cout, csem, steps=DIL_HEADS):
    def before():
        if comm:
            @pl.when(pl.program_id(0) == 0)
            def _():
                _comm_run(comm[0], ("start",), cin, cout, *csem)

            if comm[0] == "gather":
                @pl.when(pl.program_id(0) == steps - 1)
                def _():
                    _comm_run(comm[0], ("forward",), cin, cout, *csem)

            if comm[0] == "pairsum_exchange":
                @pl.when(pl.program_id(0) == (1 if steps <= 4 else 2))
                def _():
                    _comm_run(comm[0], ("reduce", "send"), cin, cout, *csem)

    def after():
        if comm:
            @pl.when(pl.program_id(0) == steps - 1)
            def _():
                _comm_run(comm[0], ("finish",), cin, cout, *csem)

    return before, after


def _dil_fwd(pf, pb, comm=None):
    s_len = pf.shape[0]
    nblk = s_len // DIL_BLOCK
    prep_rows = 256
    scale = DIL_HD ** -0.5
    nc = len(comm[1]) if comm else 0

    def body(*refs):
        ((qf, kf, v_ref), (o_ref, lse_ref), (vf, o0, o1, o2, l0, l1, l2, bias), cin, cout, csem) = _split_refs(
            refs, 3, 2, 8, comm)
        comm_before, comm_after = _comm_hooks(comm, cin, cout, csem)
        comm_before()
        _dil_fill_bias(bias)

        def prep(t, carry):
            rows = pl.ds(pl.multiple_of(t * prep_rows, prep_rows), prep_rows)
            vf[rows, :] = v_ref[rows, :].astype(F32)
            return carry

        lax.fori_loop(0, s_len // prep_rows, prep, 0)
        for d, o_p, l_p in zip(DIL_DILATIONS, (o0, o1, o2), (l0, l1, l2)):
            if nblk // d == 2:
                units = DIL_GROUP // 2

                def whole(i, carry, d=d, o_p=o_p, l_p=l_p, units=units):
                    rows = [_strided(i + u * (d // units), 2 * DIL_BLOCK, d) for u in range(units)]
                    ld = [(qf[rw, :].astype(BF16), kf[rw, :].astype(BF16), vf[rw, :].astype(BF16)) for rw in rows]
                    both = bias[...].reshape(2 * DIL_BLOCK, 2 * DIL_BLOCK)
                    s = [_dot_nt(qb, kk) * scale + both for qb, kk, _ in ld]
                    m = [jnp.max(sv, axis=-1, keepdims=True) for sv in s]
                    p = [jnp.exp(sv - mv) for sv, mv in zip(s, m)]
                    den = [jnp.sum(pv, axis=-1, keepdims=True) for pv in p]
                    r = [_dot(pv.astype(BF16), vv) for pv, (_, _, vv) in zip(p, ld)]
                    for rv, dv, mv, rw in zip(r, den, m, rows):
                        o_p[rw, :] = rv / dv
                        l_p[rw, :] = jnp.broadcast_to(mv + jnp.log(dv), (2 * DIL_BLOCK, DIL_HD))
                    return carry

                lax.fori_loop(0, d // units, whole, 0)
                continue

            def pair(i, carry, d=d, o_p=o_p, l_p=l_p):
                idx = [_dil_pair_block(i, half, d, nblk, DIL_GROUP) for half in range(DIL_GROUP)]
                ld = [(qf[_strided(qs, DIL_BLOCK, d), :].astype(BF16),
                       kf[_strided(ks, 2 * DIL_BLOCK, d), :].astype(BF16),
                       vf[_strided(ks, 2 * DIL_BLOCK, d), :].astype(BF16)) for qs, ks, _ in idx]
                s = [_dot_nt(qb, kk) * scale + bias[sel] for (qb, kk, _), (_, _, sel) in zip(ld, idx)]
                m = [jnp.max(sv, axis=-1, keepdims=True) for sv in s]
                p = [jnp.exp(sv - mv) for sv, mv in zip(s, m)]
                den = [jnp.sum(pv, axis=-1, keepdims=True) for pv in p]
                r = [_dot(pv.astype(BF16), vv) for pv, (_, _, vv) in zip(p, ld)]
                for rv, dv, mv, (qs, _, _) in zip(r, den, m, idx):
                    o_p[_strided(qs, DIL_BLOCK, d), :] = rv / dv
                    l_p[_strided(qs, DIL_BLOCK, d), :] = jnp.broadcast_to(mv + jnp.log(dv), (DIL_BLOCK, DIL_HD))
                return carry

            lax.fori_loop(0, nblk // DIL_GROUP, pair, 0)

        def comb(t, carry):
            rows = pl.ds(pl.multiple_of(t * prep_rows, prep_rows), prep_rows)
            a0, a1, a2 = l0[rows, :], l1[rows, :], l2[rows, :]
            m = jnp.maximum(jnp.maximum(a0, a1), a2)
            e0, e1, e2 = jnp.exp(a0 - m), jnp.exp(a1 - m), jnp.exp(a2 - m)
            tot = e0 + e1 + e2
            o_ref[rows, :] = (e0 * o0[rows, :] + e1 * o1[rows, :] + e2 * o2[rows, :]) / tot
            lse_ref[rows, :] = m + jnp.log(tot)
            return carry

        lax.fori_loop(0, s_len // prep_rows, comb, 0)
        comm_after()

    head = lambda base: pl.BlockSpec((s_len, DIL_HD), lambda h: (0, base // DIL_HD + h))
    out = pl.BlockSpec((s_len, DIL_HD), lambda h: (0, h))
    shp = jax.ShapeDtypeStruct((s_len, DIL_HEADS * DIL_HD), F32)
    return pl.pallas_call(
        body, name="dil_fwd_comm" if comm else "dil_fwd", grid=(DIL_HEADS,),
        out_shape=[shp, shp] + (_comm_out_shapes(*comm) if comm else []),
        in_specs=[head(COL_QB), head(COL_KB), head(COL_VB - NP_F32)] + [ANY] * nc,
        out_specs=[out, out] + [ANY] * nc,
        scratch_shapes=[pltpu.VMEM((s_len, DIL_HD), F32) for _ in range(7)]
        + [pltpu.VMEM((2, DIL_BLOCK, 2 * DIL_BLOCK), F32)] + (_comm_scratch(nc) if comm else []),
        compiler_params=_params(("arbitrary",), 56),
    )(pf, pf, pb, *(comm[1] if comm else []))


def _silu_and_grad(z):
    sg = _sigmoid(z)
    return z * sg, sg * (1.0 + z * (1.0 - sg))


def _post_fwd(o_a, o_b, pf, g_heads, w_out, x, gate, g_post, target=None, ts=512):
    s_len = x.shape[0]
    half = GLA_HEADS * GLA_DV
    last = target is not None

    def body(*refs):
        oa_ref, ob_ref, z_ref, gh_ref, w_ref, x_ref, gate_ref, gp_ref = refs[:8]
        xo_ref, u_ref = refs[8 + last:10 + last]
        y_ref = refs[-1]
        for src, base in ((oa_ref, 0), (ob_ref, half)):
            for hh in range(4):
                lo = hh * LANE
                og = src[:, lo:lo + LANE]
                on = og * lax.rsqrt(jnp.mean(og * og, axis=-1, keepdims=True) + EPS)
                zg = z_ref[:, base + lo:base + lo + LANE].astype(F32)
                y_ref[:, base + lo:base + lo + LANE] = (on * gh_ref[:, base + lo:base + lo + LANE]
                                                        * (zg * _sigmoid(zg))).astype(BF16)
        u = _dot(y_ref[...], w_ref[...])
        u_ref[...] = u.astype(BF16)
        rstd = lax.rsqrt(jnp.mean(u * u, axis=-1, keepdims=True) + EPS)
        x_out = x_ref[...] + gate_ref[...] * (u * rstd * gp_ref[...])
        if last:
            t_ref, loss_ref = refs[8], refs[11]

            @pl.when(pl.program_id(0) == 0)
            def _():
                loss_ref[...] = jnp.zeros_like(loss_ref)

            e = x_out - t_ref[...]
            xo_ref[...] = e * (1.0 / D_MODEL)
            loss_ref[...] += 0.5 * jnp.sum(jnp.mean(e * e, axis=-1, keepdims=True))
        else:
            xo_ref[...] = x_out

    (g_heads, gh_spec), (gate, gate_spec), (g_post, gp_spec) = _rowvec(g_heads), _rowvec(gate), _rowvec(g_post)
    tile = pl.BlockSpec((ts, D_MODEL), lambda i: (i, 0))
    halft = pl.BlockSpec((ts, half), lambda i: (i, 0))
    return pl.pallas_call(
        body, name="post_fwd_loss" if last else "post_fwd", grid=(s_len // ts,),
        out_shape=[jax.ShapeDtypeStruct((s_len, D_MODEL), F32), jax.ShapeDtypeStruct((s_len, D_MODEL), BF16)]
        + ([jax.ShapeDtypeStruct((8, LANE), F32)] if last else []),
        in_specs=[halft, halft, tile, gh_spec, pl.BlockSpec((D_MODEL, D_MODEL), lambda i: (0, 0)), tile, gate_spec,
                  gp_spec] + ([tile] if last else []),
        out_specs=[tile, tile] + ([pl.BlockSpec((8, LANE), lambda i: (0, 0))] if last else []),
        scratch_shapes=[pltpu.VMEM((ts, D_MODEL), BF16)],
        compiler_params=_params(("arbitrary",), 40),
    )(o_a, o_b, pf, g_heads, w_out, x, gate, g_post, *([target] if last else []))


def _post_bwd(dxo, u, gate, g_post, w_out, o_a, o_b, pf, g_heads, ts=512):
    s_len = dxo.shape[0]
    half = GLA_HEADS * GLA_DV
    steps = s_len // ts

    def body(dx_ref, u_ref, gate_ref, gp_ref, w_ref, oa_ref, ob_ref, z_ref, gh_ref, do_ref, dz_ref, sums_ref, gw_ref,
             y_s, acc):
        @pl.when(pl.program_id(0) == 0)
        def _():
            sums_ref[...] = jnp.zeros_like(sums_ref)
            acc[...] = jnp.zeros_like(acc)

        dx = dx_ref[...]
        u = u_ref[...].astype(F32)
        rstd = lax.rsqrt(jnp.mean(u * u, axis=-1, keepdims=True) + EPS)
        un = u * rstd
        sums_ref[0:1, :] += jnp.sum(dx * (un * gp_ref[...]), axis=0, keepdims=True)
        drn = dx * gate_ref[...]
        sums_ref[1:2, :] += jnp.sum(drn * un, axis=0, keepdims=True)
        dun = drn * gp_ref[...]
        du = rstd * (dun - un * jnp.mean(dun * un, axis=-1, keepdims=True))
        dub = du.astype(BF16)
        dy = _dot_nt(dub, w_ref[...])
        for src, base in ((oa_ref, 0), (ob_ref, half)):
            for hh in range(4):
                lo = base + hh * LANE
                og = src[:, hh * LANE:(hh + 1) * LANE]
                rs = lax.rsqrt(jnp.mean(og * og, axis=-1, keepdims=True) + EPS)
                on = og * rs
                zg = z_ref[:, lo:lo + LANE].astype(F32)
                sz, dsz = _silu_and_grad(zg)
                gg = gh_ref[:, lo:lo + LANE]
                dyg = dy[:, lo:lo + LANE]
                y_s[:, lo:lo + LANE] = (on * gg * sz).astype(BF16)
                sums_ref[2:3, lo:lo + LANE] += jnp.sum(dyg * sz * on, axis=0, keepdims=True)
                dz_ref[:, lo:lo + LANE] = (dyg * on * gg * dsz).astype(BF16)
                don = dyg * gg * sz
                do_ref[:, lo:lo + LANE] = (rs * (don - on * jnp.mean(don * on, axis=-1, keepdims=True))).astype(BF16)
        acc[...] += _dot_tn(y_s[...], dub)

        @pl.when(pl.program_id(0) == steps - 1)
        def _():
            gw_ref[...] = acc[...].astype(BF16)

    (g_heads, gh_spec), (gate, gate_spec), (g_post, gp_spec) = _rowvec(g_heads), _rowvec(gate), _rowvec(g_post)
    tile = pl.BlockSpec((ts, D_MODEL), lambda i: (i, 0))
    halft = pl.BlockSpec((ts, half), lambda i: (i, 0))
    whole = pl.BlockSpec((D_MODEL, D_MODEL), lambda i: (0, 0))
    return pl.pallas_call(
        body, name="post_bwd", grid=(steps,),
        out_shape=(jax.ShapeDtypeStruct((s_len, D_MODEL), BF16), jax.ShapeDtypeStruct((s_len, D_MODEL), BF16),
                   jax.ShapeDtypeStruct((8, D_MODEL), F32), jax.ShapeDtypeStruct((D_MODEL, D_MODEL), BF16)),
        in_specs=[tile, tile, gate_spec, gp_spec, whole, halft, halft, tile, gh_spec],
        out_specs=(tile, tile, pl.BlockSpec((8, D_MODEL), lambda i: (0, 0)), whole),
        scratch_shapes=[pltpu.VMEM((ts, D_MODEL), BF16), pltpu.VMEM((D_MODEL, D_MODEL), F32)],
        compiler_params=_params(("arbitrary",), 48),
    )(dxo, u, gate, g_post, w_out, o_a, o_b, pf, g_heads)


def _gla_bwd(pf, pb, wgu, bgu, layer, states, do, comm=None):
    s_len = pf.shape[0]
    nc = s_len // GLA_CHUNK
    c = GLA_CHUNK
    n_cin, c_shapes, c_scratch = _comm_plumbing(comm)

    def body(*refs):
        ((q_ref, k_ref, v_ref, lr_ref, wgu_ref, bgu_ref, st_ref, do_ref),
         (dq_ref, dk_ref, dv_ref, dlr_ref, dwgu_ref, dbgu_ref), (ds_s, dec_s, dw_acc, db_acc),
         cin, cout, csem) = _split_refs(refs, 8, 6, 4, comm)
        comm_before, comm_after = _comm_hooks(comm, cin, cout, csem, steps=2)
        comm_before()
        dw_acc[...] = jnp.zeros_like(dw_acc)
        db_acc[...] = jnp.zeros_like(db_acc)
        bd = _state_block_mask()
        last_row = lax.broadcasted_iota(jnp.int32, (c, LANE), 0) == c - 1

        def local(t, carry):
            rows_list = _gla_group_rows(t)
            cm, _, _ = _gla_chunks_common(q_ref, k_ref, lr_ref, wgu_ref, bgu_ref, rows_list)
            loc = [jnp.where(bd, _dot_tn(do_ref[rows, :], cc["qe"].astype(BF16)), 0.0)
                   for rows, cc in zip(rows_list, cm)]
            for j, cc in enumerate(cm):
                ds_s[t * GLA_GROUP + j] = loc[j]
                dec_s[t * GLA_GROUP + j] = jnp.broadcast_to(cc["dec"], (8, LANE))
            return carry

        lax.fori_loop(0, nc // GLA_GROUP, local, 0)

        def scan(t, dst):
            n = nc - 1 - t
            loc = ds_s[n]
            ds_s[n] = dst
            return dec_s[n][0:1, :] * dst + loc

        lax.fori_loop(0, nc, scan, jnp.zeros((2 * GLA_DV, LANE), F32))

        def rest(t, carry):
            rows_list = _gla_group_rows(t)
            cm, ri, ci = _gla_chunks_common(q_ref, k_ref, lr_ref, wgu_ref, bgu_ref, rows_list)
            ns = [t * GLA_GROUP + j for j in range(GLA_GROUP)]
            vs = [v_ref[rows, :] for rows in rows_list]
            dobs = [do_ref[rows, :] for rows in rows_list]
            stbs = [st_ref[0, n] for n in ns]
            dsts = [ds_s[n] for n in ns]
            dstbs = [d.astype(BF16) for d in dsts]
            qebs = [cc["qe"].astype(BF16) for cc in cm]
            kebs = [cc["ke"].astype(BF16) for cc in cm]
            kendbs = [cc["kend"].astype(BF16) for cc in cm]
            hms = [_head_lane_mask(hh) for hh in range(2)]
            qehs = [[jnp.where(hm, cc["qe"], 0.0).astype(BF16) for hm in hms] for cc in cm]
            kehs = [[jnp.where(hm, cc["ke"], 0.0).astype(BF16) for hm in hms] for cc in cm]
            heads = lambda x: [x[:, hh * GLA_DV:(hh + 1) * GLA_DV] for hh in range(2)]
            vhs, dohs = [heads(v) for v in vs], [heads(d) for d in dobs]

            dqe0 = [_dot(dob, stb) for dob, stb in zip(dobs, stbs)]
            dkend = [_dot(v, dstb) for v, dstb in zip(vs, dstbs)]
            dv0 = [_dot_nt(kb, dstb) for kb, dstb in zip(kendbs, dstbs)]
            a_t = [[jnp.where(ci >= ri, _dot_nt(kehs[j][hh], qebs[j]), 0.0).astype(BF16) for hh in range(2)]
                   for j in range(GLA_GROUP)]
            da = [[jnp.where(ri >= ci, _dot_nt(dohs[j][hh], vhs[j][hh]), 0.0).astype(BF16) for hh in range(2)]
                  for j in range(GLA_GROUP)]
            da_t = [[jnp.where(ci >= ri, _dot_nt(vhs[j][hh], dohs[j][hh]), 0.0).astype(BF16) for hh in range(2)]
                    for j in range(GLA_GROUP)]
            dv1 = [[_dot(a_t[j][hh], dohs[j][hh]) for hh in range(2)] for j in range(GLA_GROUP)]
            dqe1 = [[_dot(da[j][hh], kebs[j]) for hh in range(2)] for j in range(GLA_GROUP)]
            dke1 = [[_dot(da_t[j][hh], qehs[j][hh]) for hh in range(2)] for j in range(GLA_GROUP)]

            dbs, dzs = [], []
            for j, (rows, cc) in enumerate(zip(rows_list, cm)):
                qe, ke, kend, b, bl = cc["qe"], cc["ke"], cc["kend"], cc["b"], cc["bl"]
                dqe = dqe0[j] + jnp.where(hms[0], dqe1[j][0], 0.0) + jnp.where(hms[1], dqe1[j][1], 0.0)
                dke = jnp.where(hms[0], dke1[j][0], 0.0) + jnp.where(hms[1], dke1[j][1], 0.0)
                dv_ref[rows, :] = (dv0[j] + jnp.concatenate(dv1[j], axis=1)).astype(BF16)
                dq_ref[rows, :] = (dqe * jnp.exp(b) * (GLA_DK ** -0.5)).astype(BF16)
                dk_ref[rows, :] = (dke * jnp.exp(-b) + dkend[j] * jnp.exp(bl - b)).astype(BF16)
                ddec = jnp.sum(dsts[j] * stbs[j].astype(F32), axis=0, keepdims=True)
                dbl = jnp.sum(dkend[j] * kend, axis=0, keepdims=True) + ddec * cc["dec"]
                dbs.append(dqe * qe - dke * ke - dkend[j] * kend + jnp.where(last_row, dbl, 0.0))
            triu = (ci >= ri).astype(F32)
            dlas = [jnp.dot(triu, db, precision=lax.Precision.HIGHEST, preferred_element_type=F32) for db in dbs]
            dzs = [dla * (1.0 / GLA_TAU) * _sigmoid(-cc["z"]) for dla, cc in zip(dlas, cm)]
            dzbs = [dz.astype(BF16) for dz in dzs]
            dlrs = [_dot_nt(dzb, wgu_ref[...]) for dzb in dzbs]
            dws = [_dot_tn(lr_ref[rows, :], dzb) for rows, dzb in zip(rows_list, dzbs)]
            for rows, dlr in zip(rows_list, dlrs):
                dlr_ref[0, rows, :] = dlr
            dw_acc[...] += functools.reduce(lambda x, y: x + y, dws)
            db_acc[0:1, :] += jnp.sum(functools.reduce(lambda x, y: x + y, dzs), axis=0, keepdims=True)
            return carry

        lax.fori_loop(0, nc // GLA_GROUP, rest, 0)
        dwgu_ref[...] = dw_acc[...]
        dbgu_ref[...] = db_acc[...]
        comm_after()

    pair = pl.BlockSpec((s_len, LANE), lambda g: (0, g))
    return pl.pallas_call(
        body, name="gla_bwd_comm" if comm else "gla_bwd", grid=(2,),
        out_shape=[jax.ShapeDtypeStruct((s_len, GU_COLS), BF16), jax.ShapeDtypeStruct((s_len, GU_COLS), BF16),
                   jax.ShapeDtypeStruct((s_len, GLA_HEADS * GLA_DV), BF16),
                   jax.ShapeDtypeStruct((2, s_len, LANE), F32),
                   jax.ShapeDtypeStruct((LANE, GU_COLS), F32), jax.ShapeDtypeStruct((8, GU_COLS), F32)] + c_shapes,
        in_specs=[pl.BlockSpec((s_len, LANE), lambda g: (0, COL_QA // LANE + g)),
                  pl.BlockSpec((s_len, LANE), lambda g: (0, COL_KA // LANE + g)),
                  pl.BlockSpec((s_len, 2 * GLA_DV), lambda g: (0, (COL_VA - NP_F32) // (2 * GLA_DV) + g)),
                  pl.BlockSpec((s_len, LANE), lambda g: (0, (COL_LR - NP_F32) // LANE)),
                  pl.BlockSpec((None, LANE, LANE), lambda g: (layer, 0, g)),
                  pl.BlockSpec((None, 1, LANE), lambda g: (layer, 0, g)),
                  pl.BlockSpec((1, nc, 2 * GLA_DV, LANE), lambda g: (g, 0, 0, 0)),
                  pl.BlockSpec((s_len, 2 * GLA_DV), lambda g: (0, g))] + [ANY] * n_cin,
        out_specs=[pair, pair, pl.BlockSpec((s_len, 2 * GLA_DV), lambda g: (0, g)),
                   pl.BlockSpec((1, s_len, LANE), lambda g: (g, 0, 0)),
                   pl.BlockSpec((LANE, LANE), lambda g: (0, g)), pl.BlockSpec((8, LANE), lambda g: (0, g))]
        + [ANY] * len(c_shapes),
        scratch_shapes=[pltpu.VMEM((nc, 2 * GLA_DV, LANE), F32), pltpu.VMEM((nc, 8, LANE), F32),
                        pltpu.VMEM((LANE, LANE), F32), pltpu.VMEM((8, LANE), F32)] + c_scratch,
        compiler_params=_params(("arbitrary",), 56),
    )(pf, pf, pb, pb, wgu, bgu.reshape(bgu.shape[0], 1, GU_COLS), states, do, *(comm[1] if comm else []))


def _dil_bwd(pf, pb, do, o_b, lse, comm=None):
    s_len = pf.shape[0]
    nblk = s_len // DIL_BLOCK
    prep_rows = 256
    scale = DIL_HD ** -0.5
    nc = len(comm[1]) if comm else 0

    def body(*refs):
        ((q_ref, kf, v_ref, do_ref, o_ref, lse_ref), (dq_ref, dk_ref, dv_ref),
         (qf, vf, dof, dl, dqa, dka, dva, bias), cin, cout, csem) = _split_refs(refs, 6, 3, 8, comm)
        comm_before, comm_after = _comm_hooks(comm, cin, cout, csem)
        comm_before()
        _dil_fill_bias(bias)

        def prep(t, carry):
            rows = pl.ds(pl.multiple_of(t * prep_rows, prep_rows), prep_rows)
            qf[rows, :] = q_ref[rows, :] * scale
            vf[rows, :] = v_ref[rows, :].astype(F32)
            dov = do_ref[rows, :].astype(F32)
            dof[rows, :] = dov
            dl[rows, :] = jnp.broadcast_to(jnp.sum(dov * o_ref[rows, :], axis=-1, keepdims=True), (prep_rows, DIL_HD))
            zero = jnp.zeros((prep_rows, DIL_HD), F32)
            dqa[rows, :] = zero
            dka[rows, :] = zero
            dva[rows, :] = zero
            return carry

        lax.fori_loop(0, s_len // prep_rows, prep, 0)

        for d in DIL_DILATIONS:
            if nblk // d == 2:
                units = DIL_GROUP // 2

                def whole(i, carry, d=d, units=units):
                    rows = [_strided(i + u * (d // units), 2 * DIL_BLOCK, d) for u in range(units)]
                    ld = [(qf[rw, :].astype(BF16), kf[rw, :].astype(BF16), vf[rw, :].astype(BF16),
                           dof[rw, :].astype(BF16)) for rw in rows]
                    both = bias[...].reshape(2 * DIL_BLOCK, 2 * DIL_BLOCK)
                    s = [_dot_nt(qb, kk) + both for qb, kk, _, _ in ld]
                    dp = [_dot_nt(dob, vv) for _, _, vv, dob in ld]
                    p = [jnp.exp(sv - lse_ref[rw, :][:, 0:1]) for sv, rw in zip(s, rows)]
                    ds = [(pv * (dpv - dl[rw, :][:, 0:1])).astype(BF16) for pv, dpv, rw in zip(p, dp, rows)]
                    pb = [pv.astype(BF16) for pv in p]
                    gq = [_dot(dsv, kk) for dsv, (_, kk, _, _) in zip(ds, ld)]
                    gk = [_dot_tn(dsv, qb) for dsv, (qb, _, _, _) in zip(ds, ld)]
                    gv = [_dot_tn(pv, dob) for pv, (_, _, _, dob) in zip(pb, ld)]
                    for rw, a, b, c in zip(rows, gq, gk, gv):
                        dqa[rw, :] += a
                        dka[rw, :] += b
                        dva[rw, :] += c
                    return carry

                lax.fori_loop(0, d // units, whole, 0)
                continue

            def pair(i, carry, d=d):
                idx = [_dil_pair_block(i, half, d, nblk) for half in range(DIL_GROUP)]
                rows = [(_strided(qs, DIL_BLOCK, d), _strided(ks, 2 * DIL_BLOCK, d)) for qs, ks, _ in idx]
                ld = [(qf[qr, :].astype(BF16), kf[kr, :].astype(BF16), vf[kr, :].astype(BF16),
                       dof[qr, :].astype(BF16)) for qr, kr in rows]
                s = [_dot_nt(qb, kk) + bias[sel] for (qb, kk, _, _), (_, _, sel) in zip(ld, idx)]
                dp = [_dot_nt(dob, vv) for _, _, vv, dob in ld]
                p = [jnp.exp(sv - lse_ref[qr, :][:, 0:1]) for sv, (qr, _) in zip(s, rows)]
                ds = [(pv * (dpv - dl[qr, :][:, 0:1])).astype(BF16) for pv, dpv, (qr, _) in zip(p, dp, rows)]
                pb = [pv.astype(BF16) for pv in p]
                gq = [_dot(dsv, kk) for dsv, (_, kk, _, _) in zip(ds, ld)]
                gk = [_dot_tn(dsv, qb) for dsv, (qb, _, _, _) in zip(ds, ld)]
                gv = [_dot_tn(pv, dob) for pv, (_, _, _, dob) in zip(pb, ld)]
                for (qr, kr), a, b, c in zip(rows, gq, gk, gv):
                    dqa[qr, :] += a
                    dka[kr, :] += b
                    dva[kr, :] += c
                return carry

            lax.fori_loop(0, nblk // DIL_GROUP, pair, 0)

        def fin(t, carry):
            rows = pl.ds(pl.multiple_of(t * prep_rows, prep_rows), prep_rows)
            dq_ref[rows, :] = (dqa[rows, :] * scale).astype(BF16)
            dk_ref[rows, :] = dka[rows, :].astype(BF16)
            dv_ref[rows, :] = dva[rows, :].astype(BF16)
            return carry

        lax.fori_loop(0, s_len // prep_rows, fin, 0)
        comm_after()

    head = lambda base: pl.BlockSpec((s_len, DIL_HD), lambda h: (0, base // DIL_HD + h))
    out = pl.BlockSpec((s_len, DIL_HD), lambda h: (0, h))
    shp = jax.ShapeDtypeStruct((s_len, DIL_HEADS * DIL_HD), BF16)
    return pl.pallas_call(
        body, name="dil_bwd_comm" if comm else "dil_bwd", grid=(DIL_HEADS,),
        out_shape=[shp, shp, shp] + (_comm_out_shapes(*comm) if comm else []),
        in_specs=[head(COL_QB), head(COL_KB), head(COL_VB - NP_F32),
                  pl.BlockSpec((s_len, DIL_HD), lambda h: (0, DIL_HEADS + h)), out, out] + [ANY] * nc,
        out_specs=[out, out, out] + [ANY] * len(_comm_plumbing(comm)[1]),
        scratch_shapes=[pltpu.VMEM((s_len, DIL_HD), F32) for _ in range(7)]
        + [pltpu.VMEM((2, DIL_BLOCK, 2 * DIL_BLOCK), F32)] + (_comm_scratch(nc) if comm else []),
        compiler_params=_params(("arbitrary",), 56),
    )(pf, pf, pb, do, o_b, lse, *(comm[1] if comm else []))


_PIECES = ((COL_Z, 1024), (COL_QA, 256), (COL_KA, 256), (COL_QB, 512), (COL_KB, 512), (COL_VA, 512), (COL_VB, 512),
           (COL_LR, 128))


def _unrope_piece(p_ref, col, cos, sin_signed):
    if col not in (COL_QB, COL_KB):
        return p_ref[...]
    blocks = []
    for lo in range(0, p_ref.shape[1], DIL_HD):
        g = p_ref[:, lo:lo + DIL_HD].astype(F32)
        blocks.append((g * cos - pltpu.roll(g, DIL_HD // 2, 1) * sin_signed).astype(BF16))
    return jnp.concatenate(blocks, axis=1)


def _in_bwd(pieces, w_new, x, dxo, g_pre, scale, cos, sin_signed, comm=None, ts=256):
    s_len = x.shape[0]
    nc = len(comm[1]) if comm else 0
    nco = len(_comm_out_shapes(*comm)) if comm else 0
    npc = len(_PIECES)

    def body(*refs):
        ins, (dx_ref, sums_ref), _, cin, cout, csem = _split_refs(refs, npc + 7, 2, 0, comm)
        p_refs = ins[:npc]
        w_ref, x_ref, dxo_ref, g_ref, sc_ref, cos_ref, sin_ref = ins[npc:]
        comm_before, comm_after = _comm_hooks(comm, cin, cout, csem, steps=s_len // ts)
        comm_before()

        @pl.when(pl.program_id(0) == 0)
        def _():
            sums_ref[...] = jnp.zeros_like(sums_ref)

        dh = jnp.zeros((ts, D_MODEL), F32)
        for p_ref, (col, width) in zip(p_refs, _PIECES):
            dh += _dot_nt(_unrope_piece(p_ref, col, cos_ref[...], sin_ref[...]), w_ref[:, col:col + width])
        xv = x_ref[...]
        rstd = lax.rsqrt(jnp.mean(xv * xv, axis=-1, keepdims=True) + EPS)
        xn = xv * rstd
        sums_ref[0:1, :] += jnp.sum(dh, axis=0, keepdims=True)
        sums_ref[1:2, :] += jnp.sum(dh * (xn * g_ref[...]), axis=0, keepdims=True)
        dr = dh * (1.0 + sc_ref[...])
        sums_ref[2:3, :] += jnp.sum(dr * xn, axis=0, keepdims=True)
        dxn = dr * g_ref[...]
        dx_ref[...] = dxo_ref[...] + rstd * (dxn - xn * jnp.mean(dxn * xn, axis=-1, keepdims=True))
        comm_after()

    (g_pre, g_spec), (scale, sc_spec) = _rowvec(g_pre), _rowvec(scale)
    tile = pl.BlockSpec((ts, D_MODEL), lambda i: (i, 0))
    return pl.pallas_call(
        body, name="in_bwd_comm" if comm else "in_bwd", grid=(s_len // ts,),
        out_shape=[jax.ShapeDtypeStruct((s_len, D_MODEL), F32), jax.ShapeDtypeStruct((8, D_MODEL), F32)]
        + (_comm_out_shapes(*comm) if comm else []),
        in_specs=[pl.BlockSpec((ts, width), lambda i: (i, 0)) for _, width in _PIECES]
        + [pl.BlockSpec((D_MODEL, NP), lambda i: (0, 0)), tile, tile, g_spec, sc_spec,
           pl.BlockSpec((ts, DIL_HD), lambda i: (i, 0)), pl.BlockSpec((ts, DIL_HD), lambda i: (i, 0))] + [ANY] * nc,
        out_specs=[tile, pl.BlockSpec((8, D_MODEL), lambda i: (0, 0))] + [ANY] * nco,
        scratch_shapes=_comm_scratch(nc) if comm else [],
        compiler_params=_params(("arbitrary",), 56),
    )(*pieces, w_new, x, dxo, g_pre, scale, cos, sin_signed, *(comm[1] if comm else []))


def _w_in_to_kernel(gathered, comm=None, tr=128):
    n_cin, c_shapes, c_scratch = _comm_plumbing(comm)
    n_parts = len(gathered)
    first = [sum(g.shape[1] for g in gathered[:p]) // tr for p in range(n_parts + 1)]

    def body(*refs):
        g_refs, (o_ref,), _, cin, cout, csem = _split_refs(refs, n_parts, 1, 0, comm)
        comm_before, comm_after = _comm_hooks(comm, cin, cout, csem, steps=D_MODEL // tr)
        comm_before()
        for p, g_ref in enumerate(g_refs):
            @pl.when((pl.program_id(0) >= first[p]) & (pl.program_id(0) < first[p + 1]))
            def _(g_ref=g_ref):
                cols = jnp.concatenate([g_ref[k].astype(F32) for k in range(N_DEV)], axis=1)
                pad = jnp.zeros((tr, LANE - GLA_LOWRANK), F32)
                o_ref[...] = jnp.concatenate(
                    [cols[:, 1024:1536], cols[:, 3088:3600], cols[:, 0:512], cols[:, 1552:2576], cols[:, 512:1024],
                     cols[:, 2576:3088], cols[:, 1536:1552], pad], axis=1).astype(BF16)
        comm_after()

    part = lambda p: pl.BlockSpec((N_DEV, tr, W_IN_SHARD),
                                  lambda i: (0, jnp.clip(i - first[p], 0, first[p + 1] - first[p] - 1), 0))
    return pl.pallas_call(
        body, name="w_in_to_kernel_comm" if comm else "w_in_to_kernel", grid=(D_MODEL // tr,),
        out_shape=[jax.ShapeDtypeStruct((D_MODEL, NP), BF16)] + c_shapes,
        in_specs=[part(p) for p in range(n_parts)] + [ANY] * n_cin,
        out_specs=[pl.BlockSpec((tr, NP), lambda i: (i, 0))] + [ANY] * len(c_shapes),
        scratch_shapes=c_scratch,
        compiler_params=_params(("arbitrary",)),
    )(*gathered, *(comm[1] if comm else []))


def _grad_w_in(h, pieces, cos, sin_signed, ts=1024, tr=128):
    s_len = h.shape[0]
    steps = s_len // ts

    def body(*refs):
        h_ref, p_refs = refs[0], refs[1:1 + len(_PIECES)]
        cos_ref, sin_ref, o_ref, acc = refs[1 + len(_PIECES):]

        @pl.when(pl.program_id(0) == 0)
        def _():
            acc[...] = jnp.zeros_like(acc)

        hv = h_ref[...]
        for p_ref, (col, width) in zip(p_refs, _PIECES):
            acc[:, col:col + width] += _dot_tn(hv, _unrope_piece(p_ref, col, cos_ref[...], sin_ref[...]))

        @pl.when(pl.program_id(0) == steps - 1)
        def _():
            def rows_out(t, carry):
                rows = pl.ds(pl.multiple_of(t * tr, tr), tr)
                g = acc[rows, :]
                cols = jnp.concatenate(
                    [g[:, COL_QA:COL_QB], g[:, COL_VA:COL_VB], g[:, 0:512], g[:, COL_LR:COL_LR + GLA_LOWRANK],
                     g[:, COL_QB:COL_VA], g[:, COL_VB:COL_LR], g[:, 512:1024]], axis=1)
                for k in range(N_DEV):
                    o_ref[k, rows, :] = cols[:, W_IN_SHARD * k:W_IN_SHARD * (k + 1)].astype(BF16)
                return carry

            lax.fori_loop(0, D_MODEL // tr, rows_out, 0)

    return pl.pallas_call(
        body, name="grad_w_in", grid=(steps,),
        out_shape=jax.ShapeDtypeStruct((N_DEV, D_MODEL, W_IN_SHARD), BF16),
        in_specs=[pl.BlockSpec((ts, D_MODEL), lambda i: (i, 0))]
        + [pl.BlockSpec((ts, width), lambda i: (i, 0)) for _, width in _PIECES]
        + [pl.BlockSpec((ts, DIL_HD), lambda i: (i, 0)), pl.BlockSpec((ts, DIL_HD), lambda i: (i, 0))],
        out_specs=pl.BlockSpec((N_DEV, D_MODEL, W_IN_SHARD), lambda i: (0, 0, 0)),
        scratch_shapes=[pltpu.VMEM((D_MODEL, NP), F32)],
        compiler_params=_params(("arbitrary",), 56),
    )(h, *pieces, cos, sin_signed)


def _adam_math(w, g, m, v):
    m = ADAM_B1 * m + (1.0 - ADAM_B1) * g
    v = ADAM_B2 * v + (1.0 - ADAM_B2) * (g * g)
    m_hat = m / (1.0 - ADAM_B1 ** ADAM_STEP)
    v_hat = v / (1.0 - ADAM_B2 ** ADAM_STEP)
    delta = -ADAM_LR * (m_hat / (jnp.sqrt(v_hat) + ADAM_EPS) + ADAM_WD * w)
    return delta, m, v


def _adamw(w, parts, m, v, name, tr):
    r, cdim = w.shape
    n_parts = parts.shape[0]

    def body(w_ref, p_ref, m_ref, v_ref, g_ref, d_ref, nm_ref, nv_ref):
        g = p_ref[0].astype(F32)
        for k in range(1, n_parts):
            g = g + p_ref[k].astype(F32)
        g_ref[...] = g
        d_ref[...], nm_ref[...], nv_ref[...] = _adam_math(w_ref[...], g, m_ref[...], v_ref[...])

    tile = pl.BlockSpec((tr, cdim), lambda i: (i, 0))
    shp = jax.ShapeDtypeStruct((r, cdim), F32)
    return pl.pallas_call(
        body, name=name, grid=(r // tr,), out_shape=(shp, shp, shp, shp),
        in_specs=[tile, pl.BlockSpec((n_parts, tr, cdim), lambda i: (0, i, 0)), tile, tile],
        out_specs=(tile, tile, tile, tile),
        compiler_params=_params(("arbitrary",), 40),
    )(w, parts, m, v)


def _adamw_layers(w, parts, m, v, name, tr):
    n_layers, r, cdim = w.shape

    def body(*refs):
        w_ref, p_refs, (m_ref, v_ref) = refs[0], refs[1:1 + n_layers], refs[1 + n_layers:3 + n_layers]
        g_ref, d_ref, nm_ref, nv_ref = refs[3 + n_layers:]
        for l, p_ref in enumerate(p_refs):
            @pl.when(pl.program_id(0) == l)
            def _(p_ref=p_ref):
                g = p_ref[0].astype(F32)
                for k in range(1, p_ref.shape[0]):
                    g = g + p_ref[k].astype(F32)
                g_ref[0] = g
                d_ref[0], nm_ref[0], nv_ref[0] = _adam_math(w_ref[0], g, m_ref[0], v_ref[0])

    tile = pl.BlockSpec((1, tr, cdim), lambda l, i: (l, i, 0))
    part = lambda own: pl.BlockSpec((parts[own].shape[0], tr, cdim), lambda l, i: (0, jnp.where(l == own, i, 0), 0))
    shp = jax.ShapeDtypeStruct(w.shape, F32)
    return pl.pallas_call(
        body, name=name, grid=(n_layers, r // tr), out_shape=(shp, shp, shp, shp),
        in_specs=[tile] + [part(l) for l in range(n_layers)] + [tile, tile],
        out_specs=(tile, tile, tile, tile),
        compiler_params=_params(("arbitrary", "arbitrary"), 40),
    )(w, *parts, m, v)


def _row(vec, width):
    vec = vec.reshape(1, -1)
    return jnp.pad(vec, ((0, 0), (0, width - vec.shape[1])))


def kernel(x, c, w_ada, b_ada, g_pre, w_in, w_gate_up, b_gate_up, g_gla, g_dil, w_out, g_post, loss_target, m_w_ada, m_b_ada, m_g_pre, m_w_in, m_w_gate_up, m_b_gate_up, m_g_gla, m_g_dil, m_w_out, m_g_post, v_w_ada, v_b_ada, v_g_pre, v_w_in, v_w_gate_up, v_b_gate_up, v_g_gla, v_g_dil, v_w_out, v_g_post):
    px, py, pc = _my_position()
    me = _linear(px, py, pc)
    xs = x[0]
    target = loss_target[0]
    s_len = xs.shape[0]
    assert s_len % (DIL_BLOCK * max(DIL_DILATIONS) * 2) == 0 and xs.shape[1] == D_MODEL

    w_in_b, w_out_b = w_in.astype(BF16), w_out.astype(BF16)
    c_rows, wgu_all, w_in_all = _comm_call(
        "gather", [jnp.pad(c, ((0, 7), (0, 0))), w_gate_up.reshape(DEPTH * GLA_LOWRANK, GU_SHARD), w_in_b[0]],
        "gather_first")
    c_all = c_rows.reshape(N_DEV, 8, D_MODEL)[:, 0]
    mod_part = _mod_fwd(c_all, w_ada)
    w_new, mod_all = _w_in_to_kernel([w_in_all.reshape(N_DEV, D_MODEL, W_IN_SHARD)],
                                     comm=("gather", [mod_part.reshape(DEPTH * N_DEV, ADA_SHARD)]))
    mod_all = mod_all.reshape(N_DEV, DEPTH, N_DEV, ADA_SHARD)
    mod_mine = lax.dynamic_index_in_dim(mod_all, me, axis=2, keepdims=False)
    mod = jnp.transpose(mod_mine, (1, 0, 2)).reshape(DEPTH, 3 * D_MODEL) + b_ada
    wgu_full = jnp.transpose(wgu_all.reshape(N_DEV, DEPTH, GLA_LOWRANK, GU_SHARD), (1, 2, 0, 3)).reshape(
        DEPTH, GLA_LOWRANK, GU_COLS)
    wgu_pad = jnp.pad(wgu_full, ((0, 0), (0, LANE - GLA_LOWRANK), (0, 0))).astype(BF16)

    cos, sin_signed = _rope_tables(s_len)
    g_heads = jnp.concatenate([g_gla, g_dil], axis=1)

    saved = []
    xl = xs
    for l in range(DEPTH):
        shift, scale, gate = ((mod, l, k) for k in range(3))
        if l > 0:
            w_new = _w_in_to_kernel([half.reshape(N_DEV, D_MODEL // 2, W_IN_SHARD) for half in w_in_halves])[0]
        if l + 1 < DEPTH:
            own = [] if l > 0 else [w_out_b[0]]
            pf, pb, h, *arrived = _prenorm_proj(xl, (g_pre, l, 0), scale, shift, w_new, cos, sin_signed,
                                                comm=("gather", own + [w_in_b[l + 1, :D_MODEL // 2]]))
            w_out_l = arrived[0] if l == 0 else w_out_next
            top = arrived[-1]
        else:
            pf, pb, h = _prenorm_proj(xl, (g_pre, l, 0), scale, shift, w_new, cos, sin_signed)
            w_out_l = w_out_next
        o_a, states = _gla_fwd(pf, pb, wgu_pad, b_gate_up, l)
        if l + 1 < DEPTH:
            o_b, lse, bottom, w_out_next = _dil_fwd(pf, pb, comm=("gather", [w_in_b[l + 1, D_MODEL // 2:],
                                                                             w_out_b[l + 1]]))
            w_in_halves = (top, bottom)
        else:
            o_b, lse = _dil_fwd(pf, pb)
        if l + 1 < DEPTH:
            x_next, u = _post_fwd(o_a, o_b, pf, (g_heads, l, 0), w_out_l, xl, gate, (g_post, l, 0))
        else:
            dx, u, loss_part = _post_fwd(o_a, o_b, pf, (g_heads, l, 0), w_out_l, xl, gate, (g_post, l, 0),
                                         target=target)
        saved.append((xl, scale, gate, w_new, w_out_l, pf, pb, h, o_a, states, o_b, lse, u))
        xl = x_next

    small_rows = []
    gin_slots, gin_parts, gout_parts = None, [None] * DEPTH, [None] * DEPTH
    for l in reversed(range(DEPTH)):
        x_in, scale, gate, w_new, w_out_l, pf, pb, h, o_a, states, o_b, lse, u = saved[l]
        do, dz, sums_post, gout_slots = _post_bwd(dx, u, gate, (g_post, l, 0), w_out_l, o_a, o_b, pf, (g_heads, l, 0))
        dq_a, dk_a, dv_a, dlr2, dwgu, dbgu, arrived = _gla_bwd(pf, pb, wgu_pad, b_gate_up, l, states, do,
                                                               comm=("exchange", [gout_slots]))
        gout_parts[l] = arrived.reshape(N_DEV, OUT_SHARD, D_MODEL)
        if gin_slots is not None:
            dq_b, dk_b, dv_b, arrived, _, _ = _dil_bwd(pf, pb, do, o_b, lse, comm=("pairsum_exchange", [gin_slots]))
            gin_parts[l + 1] = arrived.reshape(N_DEV // 2, D_MODEL, W_IN_SHARD)
        else:
            dq_b, dk_b, dv_b = _dil_bwd(pf, pb, do, o_b, lse)
        dlr = (dlr2[0] + dlr2[1]).astype(BF16)
        pieces = (dz, dq_a, dk_a, dq_b, dk_b, dv_a, dv_b, dlr)
        gin_slots = _grad_w_in(h, pieces, cos, sin_signed).reshape(N_DEV * D_MODEL, W_IN_SHARD)
        if l == 0:
            dx, sums_in, arrived, _, _ = _in_bwd(pieces, w_new, x_in, dx, (g_pre, l, 0), scale, cos, sin_signed,
                                                 comm=("pairsum_exchange", [gin_slots]))
            gin_parts[0] = arrived.reshape(N_DEV // 2, D_MODEL, W_IN_SHARD)
        else:
            dx, sums_in = _in_bwd(pieces, w_new, x_in, dx, (g_pre, l, 0), scale, cos, sin_signed, ts=512)
        dmod = jnp.concatenate([sums_in[0], sums_in[1], sums_post[0]])
        vecs = jnp.concatenate([sums_in[2], sums_post[1], sums_post[2], dbgu[0]])
        small_rows[0:0] = [_row(dmod, 4096), _row(vecs, 4096), _row(dwgu[:GLA_LOWRANK], 4096)]
    grad_x = dx[None]

    flat = lambda a, rows: a.reshape(rows, a.shape[-1])
    r_ada = DEPTH * D_MODEL
    g_w_in, d_w_in, nm_w_in, nv_w_in = _adamw_layers(w_in, gin_parts, m_w_in, v_w_in, "adamw_w_in", 256)
    g_w_out, d_w_out, nm_w_out, nv_w_out = _adamw_layers(w_out, gout_parts, m_w_out, v_w_out, "adamw_w_out", 128)

    small_rows += [_row(loss_part[0, 0:1], 4096), jnp.zeros((1, 4096), F32)]
    small = _all_gather(jnp.concatenate(small_rows, axis=0), "gather_small").reshape(N_DEV, 8, 4096)
    dmod_all = jnp.stack([small[:, 0, :3 * D_MODEL], small[:, 3, :3 * D_MODEL]])
    dmod_cols = lax.dynamic_slice_in_dim(dmod_all, me * ADA_SHARD, ADA_SHARD, axis=2)
    gwa = _w_ada_grad(c_all, dmod_cols).reshape(1, r_ada, ADA_SHARD)
    g_w_ada, d_w_ada, nm_w_ada, nv_w_ada = (
        t.reshape(w_ada.shape) for t in _adamw(flat(w_ada, r_ada), gwa, flat(m_w_ada, r_ada), flat(v_w_ada, r_ada),
                                               "adamw_w_ada", 256))

    where = ((0, 0), (1, 0), (1, 1024), (1, 2048), (1, 2560), (1, 3072))
    replicated = [(b_ada, m_b_ada, v_b_ada), (g_pre, m_g_pre, v_g_pre), (g_post, m_g_post, v_g_post),
                  (g_gla, m_g_gla, v_g_gla), (g_dil, m_g_dil, v_g_dil), (b_gate_up, m_b_gate_up, v_b_gate_up)]
    updated, loss = _adamw_replicated(small, replicated, where, loss_at=(6, 0))
    ((g_b_ada, d_b_ada, nm_b_ada, nv_b_ada), (g_g_pre, d_g_pre, nm_g_pre, nv_g_pre),
     (g_g_post, d_g_post, nm_g_post, nv_g_post), (g_g_gla, d_g_gla, nm_g_gla, nv_g_gla),
     (g_g_dil, d_g_dil, nm_g_dil, nv_g_dil), (g_b_gu, d_b_gu, nm_b_gu, nv_b_gu)) = updated
    gu_parts = jnp.stack([small[:, 2], small[:, 5]], axis=1).reshape(N_DEV, DEPTH, GLA_LOWRANK, GU_COLS)
    gu_parts = lax.dynamic_slice_in_dim(gu_parts, me * GU_SHARD, GU_SHARD, axis=3).reshape(
        N_DEV, DEPTH * GLA_LOWRANK, GU_SHARD)
    r_gu = DEPTH * GLA_LOWRANK
    g_w_gu, d_w_gu, nm_w_gu, nv_w_gu = (
        t.reshape(w_gate_up.shape) for t in _adamw(flat(w_gate_up, r_gu), gu_parts, flat(m_w_gate_up, r_gu),
                                                   flat(v_w_gate_up, r_gu), "adamw_w_gate_up", r_gu))
    return (loss, grad_x,
            g_w_ada, g_b_ada, g_g_pre, g_w_in, g_w_gu, g_b_gu, g_g_gla, g_g_dil, g_w_out, g_g_post,
            d_w_ada, d_b_ada, d_g_pre, d_w_in, d_w_gu, d_b_gu, d_g_gla, d_g_dil, d_w_out, d_g_post,
            nm_w_ada, nm_b_ada, nm_g_pre, nm_w_in, nm_w_gu, nm_b_gu, nm_g_gla, nm_g_dil, nm_w_out, nm_g_post,
            nv_w_ada, nv_b_ada, nv_g_pre, nv_w_in, nv_w_gu, nv_b_gu, nv_g_gla, nv_g_dil, nv_w_out, nv_g_post)


def _adamw_replicated(small, params, where, loss_at):
    n_parts = small.shape[0]

    def body(*refs):
        s_ref, p_refs, o_refs = refs[0], refs[1:1 + 3 * len(params)], refs[1 + 3 * len(params):]
        total = s_ref[0]
        for k in range(1, n_parts):
            total = total + s_ref[k]
        for i, (row, col) in enumerate(where):
            w_ref, m_ref, v_ref = p_refs[3 * i:3 * i + 3]
            n = w_ref.shape[1]
            g = jnp.concatenate([total[row + 3 * l:row + 3 * l + 1, col:col + n] for l in range(DEPTH)], axis=0)
            o_refs[4 * i][...] = g
            o_refs[4 * i + 1][...], o_refs[4 * i + 2][...], o_refs[4 * i + 3][...] = _adam_math(
                w_ref[...], g, m_ref[...], v_ref[...])
        o_refs[-1][...] = jnp.broadcast_to(total[loss_at[0]:loss_at[0] + 1, loss_at[1]:loss_at[1] + 1], (8, LANE))

    flat = [a for p in params for a in p]
    shapes = [jax.ShapeDtypeStruct(p[0].shape, F32) for p in params for _ in range(4)]
    outs = pl.pallas_call(body, name="adamw_replicated",
                          out_shape=shapes + [jax.ShapeDtypeStruct((8, LANE), F32)])(small, *flat)
    return [tuple(outs[4 * i:4 * i + 4]) for i in range(len(params))], outs[-1][0, 0]
```

```python
import functools
import math

import jax
import jax.numpy as jnp
from jax import lax
from jax.experimental import pallas as pl
from jax.experimental.pallas import tpu as pltpu

F32 = jnp.float32
BF16 = jnp.bfloat16

N_DEV = 8
D_MODEL = 1024
DEPTH = 2
GLA_HEADS = 4
GLA_DK = 64
GLA_DV = 128
GLA_CHUNK = 64
GLA_TAU = 16.0
GLA_LOWRANK = 16
DIL_HEADS = 4
DIL_HD = 128
DIL_BLOCK = 128
DIL_DILATIONS = (1, 4, 16)
ROPE_THETA = 10000.0
EPS = 1e-6
IN_COLS = 3600
W_IN_SHARD = IN_COLS // N_DEV
ADA_SHARD = 3 * D_MODEL // N_DEV
OUT_SHARD = D_MODEL // N_DEV
GU_COLS = GLA_HEADS * GLA_DK
GU_SHARD = GU_COLS // N_DEV

ADAM_LR = 0.001
ADAM_B1 = 0.9
ADAM_B2 = 0.999
ADAM_EPS = 1e-08
ADAM_WD = 0.01
ADAM_STEP = 10

NP = 3712
COL_Z, COL_QA, COL_KA, COL_QB, COL_KB, COL_VA, COL_VB, COL_LR = 0, 1024, 1280, 1536, 2048, 2560, 3072, 3584
NP_F32 = COL_VA
NP_BF16 = NP - NP_F32
LANE = 128
MASK_VALUE = -1e30

MESH = pl.DeviceIdType.MESH
ANY = pl.BlockSpec(memory_space=pl.ANY)


def _params(sem=None, vmem_mb=None):
    kw = {}
    if sem is not None:
        kw["dimension_semantics"] = sem
    if vmem_mb is not None:
        kw["vmem_limit_bytes"] = vmem_mb * 1024 * 1024
    return pltpu.CompilerParams(**kw)


def _dot(a, b):
    return jnp.dot(a, b, preferred_element_type=F32)


def _dot_nt(a, b):
    return lax.dot_general(a, b, (((1,), (1,)), ((), ())), preferred_element_type=F32)


def _dot_tn(a, b):
    return lax.dot_general(a, b, (((0,), (0,)), ((), ())), preferred_element_type=F32)


def _sigmoid(z):
    return 1.0 / (1.0 + jnp.exp(-z))


def _log_sigmoid(z):
    return jnp.minimum(z, 0.0) - jnp.log(1.0 + jnp.exp(-jnp.abs(z)))


def _rowvec(v, width=D_MODEL):
    arr, row, cb = v
    return arr.reshape(arr.shape[0], 1, arr.shape[1]), pl.BlockSpec((None, 1, width), lambda *_: (row, 0, cb))


def _my_position():
    return lax.axis_index("x"), lax.axis_index("y"), lax.axis_index("c")


def _linear(px, py, pc):
    return 4 * px + 2 * py + pc


def _gather_phase(phase, x_ref, out_ref, send_sem, recv_sem, local_sem):
    m = x_ref.shape[0]
    x, y, c = _my_position()
    me, sibling = (x, y, c), (x, y, 1 - c)
    chips = [(1 - x, y), (x, 1 - y), (1 - x, 1 - y)]

    def rows(px, py, pc):
        return out_ref.at[pl.ds(_linear(px, py, pc) * m, m), :]

    def copy(k, block, to, src=None):
        return pltpu.make_async_remote_copy(
            src_ref=rows(*block) if src is None else src, dst_ref=rows(*block),
            send_sem=send_sem(k), recv_sem=recv_sem(k), device_id=to, device_id_type=MESH)

    mine = pltpu.make_async_copy(x_ref, rows(*me), local_sem)
    first = [copy(0, me, sibling, src=x_ref)] + [copy(1 + j, me, (*chip, c), src=x_ref) for j, chip in enumerate(chips)]
    passed = [copy(4 + j, (*chip, c), sibling) for j, chip in enumerate(chips)]
    if phase == "start":
        mine.start()
        for cp in first:
            cp.start()
    elif phase == "forward":
        for j, chip in enumerate(chips):
            copy(1 + j, (*chip, c), me).wait_recv()
            passed[j].start()
    else:
        copy(0, sibling, me).wait_recv()
        for j, chip in enumerate(chips):
            copy(4 + j, (*chip, 1 - c), me).wait_recv()
        for cp in first + passed:
            cp.wait_send()
        mine.wait()


def _exchange_phase(phase, x_ref, out_ref, send_sem, recv_sem, local_sem):
    m = x_ref.shape[0] // N_DEV
    x, y, c = _my_position()
    me = _linear(x, y, c)

    def rows(ref, idx):
        return ref.at[pl.ds(idx * m, m), :]

    peers = [(1 - x if j & 4 else x, 1 - y if j & 2 else y, 1 - c if j & 1 else c) for j in range(1, N_DEV)]
    local = pltpu.make_async_copy(rows(x_ref, me), rows(out_ref, me), local_sem)
    sends = [pltpu.make_async_remote_copy(
        src_ref=rows(x_ref, _linear(*peer)), dst_ref=rows(out_ref, me),
        send_sem=send_sem(j), recv_sem=recv_sem(j), device_id=peer, device_id_type=MESH) for j, peer in enumerate(peers)]
    if phase == "start":
        local.start()
        for cp in sends:
            cp.start()
    else:
        for j, peer in enumerate(peers):
            pltpu.make_async_remote_copy(
                src_ref=rows(x_ref, _linear(*peer)), dst_ref=rows(out_ref, _linear(*peer)),
                send_sem=send_sem(j), recv_sem=recv_sem(j), device_id=peer, device_id_type=MESH).wait_recv()
        for cp in sends:
            cp.wait_send()
        local.wait()


def _pairsum_exchange_phase(phase, x_ref, out_refs, send_sem, recv_sem, local_sem):
    out_ref, stage_ref, pair_ref = out_refs
    m, n = x_ref.shape[0] // N_DEV, x_ref.shape[1]
    x, y, c = _my_position()
    mine = 2 * x + y
    chips = [(qx, qy) for qx in range(2) for qy in range(2)]
    others = [(1 - x, y), (x, 1 - y), (1 - x, 1 - y)]

    def rows(ref, idx):
        return ref.at[pl.ds(idx * m, m), :]

    def remote(src, dst, k, to):
        return pltpu.make_async_remote_copy(src_ref=src, dst_ref=dst, send_sem=send_sem(k), recv_sem=recv_sem(k),
                                            device_id=to, device_id_type=MESH)

    to_sibling = [remote(rows(x_ref, _linear(qx, qy, 1 - c)), rows(stage_ref, q), q, (x, y, 1 - c))
                  for q, (qx, qy) in enumerate(chips)]
    to_chips = [remote(rows(pair_ref, 2 * qx + qy), rows(out_ref, mine), 4 + j, (qx, qy, c))
                for j, (qx, qy) in enumerate(others)]
    keep = pltpu.make_async_copy(rows(pair_ref, mine), rows(out_ref, mine), local_sem)
    if phase == "start":
        for cp in to_sibling:
            cp.start()
    elif phase == "reduce":
        for cp in to_sibling:
            cp.wait_recv()

        def through_vmem(a_buf, b_buf, sems):
            tr = 128
            loads = [(pltpu.make_async_copy(rows(x_ref, _linear(qx, qy, c)), a_buf.at[q % 2], sems.at[q % 2]),
                      pltpu.make_async_copy(rows(stage_ref, q), b_buf.at[q % 2], sems.at[2 + q % 2]))
                     for q, (qx, qy) in enumerate(chips)]
            stores = [pltpu.make_async_copy(a_buf.at[q % 2], rows(pair_ref, q), sems.at[4 + q % 2]) for q in range(4)]
            for cp in loads[0]:
                cp.start()
            for q in range(4):
                for cp in loads[q]:
                    cp.wait()
                if q + 1 < 4:
                    if q >= 1:
                        stores[q - 1].wait()
                    for cp in loads[q + 1]:
                        cp.start()

                def add(r, carry, q=q):
                    tile = pl.ds(pl.multiple_of(r * tr, tr), tr)
                    a_buf[q % 2, tile, :] = (a_buf[q % 2, tile, :].astype(F32)
                                             + b_buf[q % 2, tile, :].astype(F32)).astype(x_ref.dtype)
                    return carry

                lax.fori_loop(0, m // tr, add, 0)
                stores[q].start()
            stores[2].wait()
            stores[3].wait()

        pl.run_scoped(through_vmem, pltpu.VMEM((2, m, n), x_ref.dtype), pltpu.VMEM((2, m, n), x_ref.dtype),
                      pltpu.SemaphoreType.DMA((6,)))
    elif phase == "send":
        keep.start()
        for cp in to_chips:
            cp.start()
    else:
        for j, (qx, qy) in enumerate(others):
            remote(rows(pair_ref, mine), rows(out_ref, 2 * qx + qy), 4 + j, (qx, qy, c)).wait_recv()
        for cp in to_sibling + to_chips:
            cp.wait_send()
        keep.wait()


_COMM_PHASES = {"gather": (_gather_phase, ("start", "forward", "finish")),
                "exchange": (_exchange_phase, ("start", "finish")),
                "pairsum_exchange": (_pairsum_exchange_phase, ("start", "reduce", "send", "finish"))}


def _comm_scratch(n_arrays):
    return [pltpu.SemaphoreType.DMA((n_arrays, 7)), pltpu.SemaphoreType.DMA((n_arrays, 7)),
            pltpu.SemaphoreType.DMA((n_arrays,))]


def _comm_run(kind, phases, x_refs, out_refs, send_sems, recv_sems, local_sems):
    fn = _COMM_PHASES[kind][0]
    per = len(out_refs) // len(x_refs)
    for phase in phases:
        for a, x_ref in enumerate(x_refs):
            outs = out_refs[a] if per == 1 else tuple(out_refs[per * a:per * (a + 1)])
            fn(phase, x_ref, outs, lambda k, a=a: send_sems.at[a, k], lambda k, a=a: recv_sems.at[a, k],
               local_sems.at[a])


def _comm_out_shapes(kind, arrays):
    if kind == "pairsum_exchange":
        return [jax.ShapeDtypeStruct((a.shape[0] // 2, a.shape[1]), a.dtype) for a in arrays for _ in range(3)]
    return [jax.ShapeDtypeStruct((N_DEV * a.shape[0], a.shape[1]) if kind == "gather" else a.shape, a.dtype)
            for a in arrays]


def _comm_call(kind, arrays, name):
    n = len(arrays)
    shapes = _comm_out_shapes(kind, arrays)

    def body(*refs):
        _comm_run(kind, _COMM_PHASES[kind][1], refs[:n], refs[n:n + len(shapes)], *refs[n + len(shapes):])

    return pl.pallas_call(body, name=name, out_shape=shapes, in_specs=[ANY] * n, out_specs=[ANY] * len(shapes),
                          scratch_shapes=_comm_scratch(n))(*arrays)


def _all_gather(xs, name):
    return _comm_call("gather", [xs], name)[0]


def _mod_fwd(c_all, w_ada):
    def body(c_ref, w_ref, o_ref):
        cv = c_ref[...]
        sc = cv * _sigmoid(cv)
        o_ref[0] = _dot(sc.astype(BF16), w_ref[0].astype(BF16))

    return pl.pallas_call(
        body, name="mod_fwd", grid=(DEPTH,),
        out_shape=jax.ShapeDtypeStruct((DEPTH, N_DEV, ADA_SHARD), F32),
        in_specs=[pl.BlockSpec((N_DEV, D_MODEL), lambda l: (0, 0)),
                  pl.BlockSpec((1, D_MODEL, ADA_SHARD), lambda l: (l, 0, 0))],
        out_specs=pl.BlockSpec((1, N_DEV, ADA_SHARD), lambda l: (l, 0, 0)),
        compiler_params=_params(("arbitrary",)),
    )(c_all, w_ada)


def _w_ada_grad(c_all, dmod_cols):
    def body(c_ref, d_ref, o_ref):
        cv = c_ref[...]
        sc = cv * _sigmoid(cv)
        o_ref[0] = lax.dot_general(sc, d_ref[0], (((0,), (0,)), ((), ())), precision=lax.Precision.HIGHEST,
                                   preferred_element_type=F32)

    return pl.pallas_call(
        body, name="w_ada_grad", grid=(DEPTH,),
        out_shape=jax.ShapeDtypeStruct((DEPTH, D_MODEL, ADA_SHARD), F32),
        in_specs=[pl.BlockSpec((N_DEV, D_MODEL), lambda l: (0, 0)),
                  pl.BlockSpec((1, N_DEV, ADA_SHARD), lambda l: (l, 0, 0))],
        out_specs=pl.BlockSpec((1, D_MODEL, ADA_SHARD), lambda l: (l, 0, 0)),
        compiler_params=_params(("arbitrary",)),
    )(c_all, dmod_cols)


def _comm_plumbing(comm):
    if not comm:
        return 0, [], []
    return len(comm[1]), _comm_out_shapes(*comm), _comm_scratch(len(comm[1]))


def _split_refs(refs, n_in, n_out, n_scratch, comm):
    ci, shapes, _ = _comm_plumbing(comm)
    co = len(shapes)
    a, b, c = n_in + ci, n_in + ci + n_out, n_in + ci + n_out + co
    return refs[:n_in], refs[a:b], refs[c:c + n_scratch], refs[n_in:a], refs[b:c], refs[c + n_scratch:]


def _prenorm_proj(x, g_pre, scale, shift, w_new, cos, sin_signed, comm=None, ts=256):
    s_len = x.shape[0]
    n_cin, c_shapes, c_scratch = _comm_plumbing(comm)

    def body(*refs):
        (x_ref, g_ref, sc_ref, sh_ref, w_ref, cos_ref, sin_ref), (pf_ref, pb_ref, h_ref), _, cin, cout, csem = (
            _split_refs(refs, 7, 3, 0, comm))
        comm_before, comm_after = _comm_hooks(comm, cin, cout, csem, steps=s_len // ts)
        comm_before()
        xv = x_ref[...]
        rstd = lax.rsqrt(jnp.mean(xv * xv, axis=-1, keepdims=True) + EPS)
        h = (xv * rstd * g_ref[...]) * (1.0 + sc_ref[...]) + sh_ref[...]
        hb = h.astype(BF16)
        h_ref[...] = hb
        for j in range(0, NP, 512):
            w = min(512, NP - j)
            acc = _dot(hb, w_ref[:, j:j + w])
            if COL_QB <= j < COL_VA:
                for lo in range(0, w, DIL_HD):
                    pf_ref[:, j + lo:j + lo + DIL_HD] = _rope(acc[:, lo:lo + DIL_HD], cos_ref[...], sin_ref[...])
            elif j < NP_F32:
                pf_ref[:, j:j + w] = acc
            else:
                pb_ref[:, j - NP_F32:j - NP_F32 + w] = acc.astype(BF16)
        comm_after()

    (g_pre, g_spec), (scale, sc_spec), (shift, sh_spec) = _rowvec(g_pre), _rowvec(scale), _rowvec(shift)
    return pl.pallas_call(
        body, name="prenorm_proj_comm" if comm else "prenorm_proj", grid=(s_len // ts,),
        out_shape=[jax.ShapeDtypeStruct((s_len, NP_F32), F32), jax.ShapeDtypeStruct((s_len, NP_BF16), BF16),
                   jax.ShapeDtypeStruct((s_len, D_MODEL), BF16)] + c_shapes,
        in_specs=[pl.BlockSpec((ts, D_MODEL), lambda i: (i, 0)), g_spec, sc_spec, sh_spec,
                  pl.BlockSpec((D_MODEL, NP), lambda i: (0, 0)), pl.BlockSpec((ts, DIL_HD), lambda i: (i, 0)),
                  pl.BlockSpec((ts, DIL_HD), lambda i: (i, 0))] + [ANY] * n_cin,
        out_specs=[pl.BlockSpec((ts, NP_F32), lambda i: (i, 0)), pl.BlockSpec((ts, NP_BF16), lambda i: (i, 0)),
                   pl.BlockSpec((ts, D_MODEL), lambda i: (i, 0))] + [ANY] * len(c_shapes),
        scratch_shapes=c_scratch,
        compiler_params=_params(("arbitrary",), 48),
    )(x, g_pre, scale, shift, w_new, cos, sin_signed, *(comm[1] if comm else []))


GLA_GROUP = 16


def _gla_group_rows(t):
    return [pl.ds(pl.multiple_of((t * GLA_GROUP + j) * GLA_CHUNK, GLA_CHUNK), GLA_CHUNK) for j in range(GLA_GROUP)]


def _gla_chunks_common(q_ref, k_ref, lr_ref, wgu_ref, bgu_ref, rows_list):
    c = GLA_CHUNK
    ri = lax.broadcasted_iota(jnp.int32, (c, c), 0)
    ci = lax.broadcasted_iota(jnp.int32, (c, c), 1)
    tril = (ri >= ci).astype(F32)
    zs = [_dot(lr_ref[rows, :], wgu_ref[...]) + bgu_ref[...] for rows in rows_list]
    las = [_log_sigmoid(z) * (1.0 / GLA_TAU) for z in zs]
    bs = [jnp.dot(tril, la, precision=lax.Precision.HIGHEST, preferred_element_type=F32) for la in las]
    out = []
    for rows, z, b in zip(rows_list, zs, bs):
        q = q_ref[rows, :] * (GLA_DK ** -0.5)
        k = k_ref[rows, :]
        bl = b[c - 1:c, :]
        out.append(dict(z=z, b=b, bl=bl, qe=q * jnp.exp(b), ke=k * jnp.exp(-b), kend=k * jnp.exp(bl - b),
                        dec=jnp.exp(bl)))
    return out, ri, ci


def _head_lane_mask(hh):
    return (lax.broadcasted_iota(jnp.int32, (1, LANE), 1) // GLA_DK) == hh


def _state_block_mask():
    r = lax.broadcasted_iota(jnp.int32, (2 * GLA_DV, LANE), 0) // GLA_DV
    cc = lax.broadcasted_iota(jnp.int32, (2 * GLA_DV, LANE), 1) // GLA_DK
    return r == cc


def _gla_fwd(pf, pb, wgu, bgu, layer, comm=None):
    s_len = pf.shape[0]
    nc = s_len // GLA_CHUNK
    ncomm = len(comm[1]) if comm else 0

    def body(*refs):
        q_ref, k_ref, v_ref, lr_ref, wgu_ref, bgu_ref = refs[:6]
        cin, (o_ref, st_ref), cout = refs[6:6 + ncomm], refs[6 + ncomm:8 + ncomm], refs[8 + ncomm:8 + 2 * ncomm]
        qe_s, cs_s, dec_s = refs[8 + 2 * ncomm:11 + 2 * ncomm]
        comm_before, comm_after = _comm_hooks(comm, cin, cout, refs[11 + 2 * ncomm:], steps=2)
        comm_before()
        bd = _state_block_mask()

        def local(t, carry):
            rows_list = _gla_group_rows(t)
            cm, ri, ci = _gla_chunks_common(q_ref, k_ref, lr_ref, wgu_ref, bgu_ref, rows_list)
            vs = [v_ref[rows, :] for rows in rows_list]
            kebs = [c["ke"].astype(BF16) for c in cm]
            a = [[jnp.where(ri >= ci, _dot_nt(jnp.where(_head_lane_mask(hh), c["qe"], 0.0).astype(BF16), keb), 0.0)
                  .astype(BF16) for hh in range(2)] for c, keb in zip(cm, kebs)]
            oi = [[_dot(ah[hh], v[:, hh * GLA_DV:(hh + 1) * GLA_DV]) for hh in range(2)] for ah, v in zip(a, vs)]
            cs = [jnp.where(bd, _dot_tn(v, c["kend"].astype(BF16)), 0.0) for c, v in zip(cm, vs)]
            for j, (rows, c) in enumerate(zip(rows_list, cm)):
                n = t * GLA_GROUP + j
                o_ref[rows, :] = jnp.concatenate(oi[j], axis=1)
                qe_s[rows, :] = c["qe"].astype(BF16)
                cs_s[n] = cs[j]
                dec_s[n] = jnp.broadcast_to(c["dec"], (8, LANE))
            return carry

        lax.fori_loop(0, nc // GLA_GROUP, local, 0)

        def scan(n, st):
            st_ref[0, n] = st.astype(BF16)
            return dec_s[n][0:1, :] * st + cs_s[n]

        lax.fori_loop(0, nc, scan, jnp.zeros((2 * GLA_DV, LANE), F32))

        def inter(t, carry):
            rows_list = _gla_group_rows(t)
            add = [_dot_nt(qe_s[rows, :], st_ref[0, t * GLA_GROUP + j]) for j, rows in enumerate(rows_list)]
            for rows, av in zip(rows_list, add):
                o_ref[rows, :] = o_ref[rows, :] + av
            return carry

        lax.fori_loop(0, nc // GLA_GROUP, inter, 0)
        comm_after()

    return pl.pallas_call(
        body, name="gla_fwd_comm" if comm else "gla_fwd", grid=(2,),
        out_shape=[jax.ShapeDtypeStruct((s_len, GLA_HEADS * GLA_DV), F32),
                   jax.ShapeDtypeStruct((2, nc, 2 * GLA_DV, LANE), BF16)] + (_comm_out_shapes(*comm) if comm else []),
        in_specs=[pl.BlockSpec((s_len, LANE), lambda g: (0, COL_QA // LANE + g)),
                  pl.BlockSpec((s_len, LANE), lambda g: (0, COL_KA // LANE + g)),
                  pl.BlockSpec((s_len, 2 * GLA_DV), lambda g: (0, (COL_VA - NP_F32) // (2 * GLA_DV) + g)),
                  pl.BlockSpec((s_len, LANE), lambda g: (0, (COL_LR - NP_F32) // LANE)),
                  pl.BlockSpec((None, LANE, LANE), lambda g: (layer, 0, g)),
                  pl.BlockSpec((None, 1, LANE), lambda g: (layer, 0, g))] + [ANY] * ncomm,
        out_specs=[pl.BlockSpec((s_len, 2 * GLA_DV), lambda g: (0, g)),
                   pl.BlockSpec((1, nc, 2 * GLA_DV, LANE), lambda g: (g, 0, 0, 0))] + [ANY] * ncomm,
        scratch_shapes=[pltpu.VMEM((s_len, LANE), BF16), pltpu.VMEM((nc, 2 * GLA_DV, LANE), F32),
                        pltpu.VMEM((nc, 8, LANE), F32)] + (_comm_scratch(ncomm) if comm else []),
        compiler_params=_params(("arbitrary",), 56),
    )(pf, pf, pb, pb, wgu, bgu.reshape(bgu.shape[0], 1, GU_COLS), *(comm[1] if comm else []))


def _rope_tables(s_len):
    inv_freq = ROPE_THETA ** (-jnp.arange(0, DIL_HD, 2, dtype=F32) / DIL_HD)
    ang = jnp.arange(s_len, dtype=F32)[:, None] * inv_freq[None, :]
    cos, sin = jnp.cos(ang), jnp.sin(ang)
    return jnp.concatenate([cos, cos], axis=1), jnp.concatenate([-sin, sin], axis=1)


def _rope(xv, cos, sin_signed):
    return xv * cos + pltpu.roll(xv, DIL_HD // 2, 1) * sin_signed


DIL_GROUP = 8


def _dil_pair_block(i, half, d, nblk, group=DIL_GROUP):
    nb = nblk // d
    j = i + half * (nblk // group)
    if nb >= 2 * group:
        r, n = j % d, j // d
    else:
        r, n = j // nb, j % nb
    kb = jnp.maximum(n - 1, 0)
    qs = r + d * DIL_BLOCK * n
    ks = r + d * DIL_BLOCK * kb
    return qs, ks, jnp.minimum(n, 1)


def _dil_fill_bias(bias):
    qi = lax.broadcasted_iota(jnp.int32, (DIL_BLOCK, 2 * DIL_BLOCK), 0)
    kj = lax.broadcasted_iota(jnp.int32, (DIL_BLOCK, 2 * DIL_BLOCK), 1)
    for sel in range(2):
        dist = qi - kj + DIL_BLOCK * sel
        bias[sel] = jnp.where((dist >= 0) & (dist <= DIL_BLOCK), 0.0, MASK_VALUE)


def _strided(start, size, d):
    return pl.ds(start, size) if d == 1 else pl.ds(start, size, stride=d)


def _comm_hooks(comm, cin, cout, csem, steps=DIL_HEADS):
    def before():
        if comm:
            @pl.when(pl.program_id(0) == 0)
            def _():
                _comm_run(comm[0], ("start",), cin, cout, *csem)

            if comm[0] == "gather":
                @pl.when(pl.program_id(0) == steps - 1)
                def _():
                    _comm_run(comm[0], ("forward",), cin, cout, *csem)

            if comm[0] == "pairsum_exchange":
                @pl.when(pl.program_id(0) == (1 if steps <= 4 else 2))
                def _():
                    _comm_run(comm[0], ("reduce", "send"), cin, cout, *csem)

    def after():
        if comm:
            @pl.when(pl.program_id(0) == steps - 1)
            def _():
                _comm_run(comm[0], ("finish",), cin, cout, *csem)

    return before, after


def _dil_fwd(pf, pb, comm=None):
    s_len = pf.shape[0]
    nblk = s_len // DIL_BLOCK
    prep_rows = 256
    scale = DIL_HD ** -0.5
    nc = len(comm[1]) if comm else 0

    def body(*refs):
        ((qf, kf, v_ref), (o_ref, lse_ref), (vf, o0, o1, o2, l0, l1, l2, bias), cin, cout, csem) = _split_refs(
            refs, 3, 2, 8, comm)
        comm_before, comm_after = _comm_hooks(comm, cin, cout, csem)
        comm_before()
        _dil_fill_bias(bias)

        def prep(t, carry):
            rows = pl.ds(pl.multiple_of(t * prep_rows, prep_rows), prep_rows)
            vf[rows, :] = v_ref[rows, :].astype(F32)
            return carry

        lax.fori_loop(0, s_len // prep_rows, prep, 0)
        for d, o_p, l_p in zip(DIL_DILATIONS, (o0, o1, o2), (l0, l1, l2)):
            if nblk // d == 2:
                units = DIL_GROUP // 2

                def whole(i, carry, d=d, o_p=o_p, l_p=l_p, units=units):
                    rows = [_strided(i + u * (d // units), 2 * DIL_BLOCK, d) for u in range(units)]
                    ld = [(qf[rw, :].astype(BF16), kf[rw, :].astype(BF16), vf[rw, :].astype(BF16)) for rw in rows]
                    both = bias[...].reshape(2 * DIL_BLOCK, 2 * DIL_BLOCK)
                    s = [_dot_nt(qb, kk) * scale + both for qb, kk, _ in ld]
                    m = [jnp.max(sv, axis=-1, keepdims=True) for sv in s]
                    p = [jnp.exp(sv - mv) for sv, mv in zip(s, m)]
                    den = [jnp.sum(pv, axis=-1, keepdims=True) for pv in p]
                    r = [_dot(pv.astype(BF16), vv) for pv, (_, _, vv) in zip(p, ld)]
                    for rv, dv, mv, rw in zip(r, den, m, rows):
                        o_p[rw, :] = rv / dv
                        l_p[rw, :] = jnp.broadcast_to(mv + jnp.log(dv), (2 * DIL_BLOCK, DIL_HD))
                    return carry

                lax.fori_loop(0, d // units, whole, 0)
                continue

            def pair(i, carry, d=d, o_p=o_p, l_p=l_p):
                idx = [_dil_pair_block(i, half, d, nblk, DIL_GROUP) for half in range(DIL_GROUP)]
                ld = [(qf[_strided(qs, DIL_BLOCK, d), :].astype(BF16),
                       kf[_strided(ks, 2 * DIL_BLOCK, d), :].astype(BF16),
                       vf[_strided(ks, 2 * DIL_BLOCK, d), :].astype(BF16)) for qs, ks, _ in idx]
                s = [_dot_nt(qb, kk) * scale + bias[sel] for (qb, kk, _), (_, _, sel) in zip(ld, idx)]
                m = [jnp.max(sv, axis=-1, keepdims=True) for sv in s]
                p = [jnp.exp(sv - mv) for sv, mv in zip(s, m)]
                den = [jnp.sum(pv, axis=-1, keepdims=True) for pv in p]
                r = [_dot(pv.astype(BF16), vv) for pv, (_, _, vv) in zip(p, ld)]
                for rv, dv, mv, (qs, _, _) in zip(r, den, m, idx):
                    o_p[_strided(qs, DIL_BLOCK, d), :] = rv / dv
                    l_p[_strided(qs, DIL_BLOCK, d), :] = jnp.broadcast_to(mv + jnp.log(dv), (DIL_BLOCK, DIL_HD))
                return carry

            lax.fori_loop(0, nblk // DIL_GROUP, pair, 0)

        def comb(t, carry):
            rows = pl.ds(pl.multiple_of(t * prep_rows, prep_rows), prep_rows)
            a0, a1, a2 = l0[rows, :], l1[rows, :], l2[rows, :]
            m = jnp.maximum(jnp.maximum(a0, a1), a2)
            e0, e1, e2 = jnp.exp(a0 - m), jnp.exp(a1 - m), jnp.exp(a2 - m)
            tot = e0 + e1 + e2
            o_ref[rows, :] = (e0 * o0[rows, :] + e1 * o1[rows, :] + e2 * o2[rows, :]) / tot
            lse_ref[rows, :] = m + jnp.log(tot)
            return carry

        lax.fori_loop(0, s_len // prep_rows, comb, 0)
        comm_after()

    head = lambda base: pl.BlockSpec((s_len, DIL_HD), lambda h: (0, base // DIL_HD + h))
    out = pl.BlockSpec((s_len, DIL_HD), lambda h: (0, h))
    shp = jax.ShapeDtypeStruct((s_len, DIL_HEADS * DIL_HD), F32)
    return pl.pallas_call(
        body, name="dil_fwd_comm" if comm else "dil_fwd", grid=(DIL_HEADS,),
        out_shape=[shp, shp] + (_comm_out_shapes(*comm) if comm else []),
        in_specs=[head(COL_QB), head(COL_KB), head(COL_VB - NP_F32)] + [ANY] * nc,
        out_specs=[out, out] + [ANY] * nc,
        scratch_shapes=[pltpu.VMEM((s_len, DIL_HD), F32) for _ in range(7)]
        + [pltpu.VMEM((2, DIL_BLOCK, 2 * DIL_BLOCK), F32)] + (_comm_scratch(nc) if comm else []),
        compiler_params=_params(("arbitrary",), 56),
    )(pf, pf, pb, *(comm[1] if comm else []))


def _silu_and_grad(z):
    sg = _sigmoid(z)
    return z * sg, sg * (1.0 + z * (1.0 - sg))


def _post_fwd(o_a, o_b, pf, g_heads, w_out, x, gate, g_post, target=None, ts=512):
    s_len = x.shape[0]
    half = GLA_HEADS * GLA_DV
    last = target is not None

    def body(*refs):
        oa_ref, ob_ref, z_ref, gh_ref, w_ref, x_ref, gate_ref, gp_ref = refs[:8]
        xo_ref, u_ref = refs[8 + last:10 + last]
        y_ref = refs[-1]
        for src, base in ((oa_ref, 0), (ob_ref, half)):
            for hh in range(4):
                lo = hh * LANE
                og = src[:, lo:lo + LANE]
                on = og * lax.rsqrt(jnp.mean(og * og, axis=-1, keepdims=True) + EPS)
                zg = z_ref[:, base + lo:base + lo + LANE].astype(F32)
                y_ref[:, base + lo:base + lo + LANE] = (on * gh_ref[:, base + lo:base + lo + LANE]
                                                        * (zg * _sigmoid(zg))).astype(BF16)
        u = _dot(y_ref[...], w_ref[...])
        u_ref[...] = u.astype(BF16)
        rstd = lax.rsqrt(jnp.mean(u * u, axis=-1, keepdims=True) + EPS)
        x_out = x_ref[...] + gate_ref[...] * (u * rstd * gp_ref[...])
        if last:
            t_ref, loss_ref = refs[8], refs[11]

            @pl.when(pl.program_id(0) == 0)
            def _():
                loss_ref[...] = jnp.zeros_like(loss_ref)

            e = x_out - t_ref[...]
            xo_ref[...] = e * (1.0 / D_MODEL)
            loss_ref[...] += 0.5 * jnp.sum(jnp.mean(e * e, axis=-1, keepdims=True))
        else:
            xo_ref[...] = x_out

    (g_heads, gh_spec), (gate, gate_spec), (g_post, gp_spec) = _rowvec(g_heads), _rowvec(gate), _rowvec(g_post)
    tile = pl.BlockSpec((ts, D_MODEL), lambda i: (i, 0))
    halft = pl.BlockSpec((ts, half), lambda i: (i, 0))
    return pl.pallas_call(
        body, name="post_fwd_loss" if last else "post_fwd", grid=(s_len // ts,),
        out_shape=[jax.ShapeDtypeStruct((s_len, D_MODEL), F32), jax.ShapeDtypeStruct((s_len, D_MODEL), BF16)]
        + ([jax.ShapeDtypeStruct((8, LANE), F32)] if last else []),
        in_specs=[halft, halft, tile, gh_spec, pl.BlockSpec((D_MODEL, D_MODEL), lambda i: (0, 0)), tile, gate_spec,
                  gp_spec] + ([tile] if last else []),
        out_specs=[tile, tile] + ([pl.BlockSpec((8, LANE), lambda i: (0, 0))] if last else []),
        scratch_shapes=[pltpu.VMEM((ts, D_MODEL), BF16)],
        compiler_params=_params(("arbitrary",), 40),
    )(o_a, o_b, pf, g_heads, w_out, x, gate, g_post, *([target] if last else []))


def _post_bwd(dxo, u, gate, g_post, w_out, o_a, o_b, pf, g_heads, ts=512):
    s_len = dxo.shape[0]
    half = GLA_HEADS * GLA_DV
    steps = s_len // ts

    def body(dx_ref, u_ref, gate_ref, gp_ref, w_ref, oa_ref, ob_ref, z_ref, gh_ref, do_ref, dz_ref, sums_ref, gw_ref,
             y_s, acc):
        @pl.when(pl.program_id(0) == 0)
        def _():
            sums_ref[...] = jnp.zeros_like(sums_ref)
            acc[...] = jnp.zeros_like(acc)

        dx = dx_ref[...]
        u = u_ref[...].astype(F32)
        rstd = lax.rsqrt(jnp.mean(u * u, axis=-1, keepdims=True) + EPS)
        un = u * rstd
        sums_ref[0:1, :] += jnp.sum(dx * (un * gp_ref[...]), axis=0, keepdims=True)
        drn = dx * gate_ref[...]
        sums_ref[1:2, :] += jnp.sum(drn * un, axis=0, keepdims=True)
        dun = drn * gp_ref[...]
        du = rstd * (dun - un * jnp.mean(dun * un, axis=-1, keepdims=True))
        dub = du.astype(BF16)
        dy = _dot_nt(dub, w_ref[...])
        for src, base in ((oa_ref, 0), (ob_ref, half)):
            for hh in range(4):
                lo = base + hh * LANE
                og = src[:, hh * LANE:(hh + 1) * LANE]
                rs = lax.rsqrt(jnp.mean(og * og, axis=-1, keepdims=True) + EPS)
                on = og * rs
                zg = z_ref[:, lo:lo + LANE].astype(F32)
                sz, dsz = _silu_and_grad(zg)
                gg = gh_ref[:, lo:lo + LANE]
                dyg = dy[:, lo:lo + LANE]
                y_s[:, lo:lo + LANE] = (on * gg * sz).astype(BF16)
                sums_ref[2:3, lo:lo + LANE] += jnp.sum(dyg * sz * on, axis=0, keepdims=True)
                dz_ref[:, lo:lo + LANE] = (dyg * on * gg * dsz).astype(BF16)
                don = dyg * gg * sz
                do_ref[:, lo:lo + LANE] = (rs * (don - on * jnp.mean(don * on, axis=-1, keepdims=True))).astype(BF16)
        acc[...] += _dot_tn(y_s[...], dub)

        @pl.when(pl.program_id(0) == steps - 1)
        def _():
            gw_ref[...] = acc[...].astype(BF16)

    (g_heads, gh_spec), (gate, gate_spec), (g_post, gp_spec) = _rowvec(g_heads), _rowvec(gate), _rowvec(g_post)
    tile = pl.BlockSpec((ts, D_MODEL), lambda i: (i, 0))
    halft = pl.BlockSpec((ts, half), lambda i: (i, 0))
    whole = pl.BlockSpec((D_MODEL, D_MODEL), lambda i: (0, 0))
    return pl.pallas_call(
        body, name="post_bwd", grid=(steps,),
        out_shape=(jax.ShapeDtypeStruct((s_len, D_MODEL), BF16), jax.ShapeDtypeStruct((s_len, D_MODEL), BF16),
                   jax.ShapeDtypeStruct((8, D_MODEL), F32), jax.ShapeDtypeStruct((D_MODEL, D_MODEL), BF16)),
        in_specs=[tile, tile, gate_spec, gp_spec, whole, halft, halft, tile, gh_spec],
        out_specs=(tile, tile, pl.BlockSpec((8, D_MODEL), lambda i: (0, 0)), whole),
        scratch_shapes=[pltpu.VMEM((ts, D_MODEL), BF16), pltpu.VMEM((D_MODEL, D_MODEL), F32)],
        compiler_params=_params(("arbitrary",), 48),
    )(dxo, u, gate, g_post, w_out, o_a, o_b, pf, g_heads)


def _gla_bwd(pf, pb, wgu, bgu, layer, states, do, comm=None):
    s_len = pf.shape[0]
    nc = s_len // GLA_CHUNK
    c = GLA_CHUNK
    n_cin, c_shapes, c_scratch = _comm_plumbing(comm)

    def body(*refs):
        ((q_ref, k_ref, v_ref, lr_ref, wgu_ref, bgu_ref, st_ref, do_ref),
         (dq_ref, dk_ref, dv_ref, dlr_ref, dwgu_ref, dbgu_ref), (ds_s, dec_s, dw_acc, db_acc),
         cin, cout, csem) = _split_refs(refs, 8, 6, 4, comm)
        comm_before, comm_after = _comm_hooks(comm, cin, cout, csem, steps=2)
        comm_before()
        dw_acc[...] = jnp.zeros_like(dw_acc)
        db_acc[...] = jnp.zeros_like(db_acc)
        bd = _state_block_mask()
        last_row = lax.broadcasted_iota(jnp.int32, (c, LANE), 0) == c - 1

        def local(t, carry):
            rows_list = _gla_group_rows(t)
            cm, _, _ = _gla_chunks_common(q_ref, k_ref, lr_ref, wgu_ref, bgu_ref, rows_list)
            loc = [jnp.where(bd, _dot_tn(do_ref[rows, :], cc["qe"].astype(BF16)), 0.0)
                   for rows, cc in zip(rows_list, cm)]
            for j, cc in enumerate(cm):
                ds_s[t * GLA_GROUP + j] = loc[j]
                dec_s[t * GLA_GROUP + j] = jnp.broadcast_to(cc["dec"], (8, LANE))
            return carry

        lax.fori_loop(0, nc // GLA_GROUP, local, 0)

        def scan(t, dst):
            n = nc - 1 - t
            loc = ds_s[n]
            ds_s[n] = dst
            return dec_s[n][0:1, :] * dst + loc

        lax.fori_loop(0, nc, scan, jnp.zeros((2 * GLA_DV, LANE), F32))

        def rest(t, carry):
            rows_list = _gla_group_rows(t)
            cm, ri, ci = _gla_chunks_common(q_ref, k_ref, lr_ref, wgu_ref, bgu_ref, rows_list)
            ns = [t * GLA_GROUP + j for j in range(GLA_GROUP)]
            vs = [v_ref[rows, :] for rows in rows_list]
            dobs = [do_ref[rows, :] for rows in rows_list]
            stbs = [st_ref[0, n] for n in ns]
            dsts = [ds_s[n] for n in ns]
            dstbs = [d.astype(BF16) for d in dsts]
            qebs = [cc["qe"].astype(BF16) for cc in cm]
            kebs = [cc["ke"].astype(BF16) for cc in cm]
            kendbs = [cc["kend"].astype(BF16) for cc in cm]
            hms = [_head_lane_mask(hh) for hh in range(2)]
            qehs = [[jnp.where(hm, cc["qe"], 0.0).astype(BF16) for hm in hms] for cc in cm]
            kehs = [[jnp.where(hm, cc["ke"], 0.0).astype(BF16) for hm in hms] for cc in cm]
            heads = lambda x: [x[:, hh * GLA_DV:(hh + 1) * GLA_DV] for hh in range(2)]
            vhs, dohs = [heads(v) for v in vs], [heads(d) for d in dobs]

            dqe0 = [_dot(dob, stb) for dob, stb in zip(dobs, stbs)]
            dkend = [_dot(v, dstb) for v, dstb in zip(vs, dstbs)]
            dv0 = [_dot_nt(kb, dstb) for kb, dstb in zip(kendbs, dstbs)]
            a_t = [[jnp.where(ci >= ri, _dot_nt(kehs[j][hh], qebs[j]), 0.0).astype(BF16) for hh in range(2)]
                   for j in range(GLA_GROUP)]
            da = [[jnp.where(ri >= ci, _dot_nt(dohs[j][hh], vhs[j][hh]), 0.0).astype(BF16) for hh in range(2)]
                  for j in range(GLA_GROUP)]
            da_t = [[jnp.where(ci >= ri, _dot_nt(vhs[j][hh], dohs[j][hh]), 0.0).astype(BF16) for hh in range(2)]
                    for j in range(GLA_GROUP)]
            dv1 = [[_dot(a_t[j][hh], dohs[j][hh]) for hh in range(2)] for j in range(GLA_GROUP)]
            dqe1 = [[_dot(da[j][hh], kebs[j]) for hh in range(2)] for j in range(GLA_GROUP)]
            dke1 = [[_dot(da_t[j][hh], qehs[j][hh]) for hh in range(2)] for j in range(GLA_GROUP)]

            dbs, dzs = [], []
            for j, (rows, cc) in enumerate(zip(rows_list, cm)):
                qe, ke, kend, b, bl = cc["qe"], cc["ke"], cc["kend"], cc["b"], cc["bl"]
                dqe = dqe0[j] + jnp.where(hms[0], dqe1[j][0], 0.0) + jnp.where(hms[1], dqe1[j][1], 0.0)
                dke = jnp.where(hms[0], dke1[j][0], 0.0) + jnp.where(hms[1], dke1[j][1], 0.0)
                dv_ref[rows, :] = (dv0[j] + jnp.concatenate(dv1[j], axis=1)).astype(BF16)
                dq_ref[rows, :] = (dqe * jnp.exp(b) * (GLA_DK ** -0.5)).astype(BF16)
                dk_ref[rows, :] = (dke * jnp.exp(-b) + dkend[j] * jnp.exp(bl - b)).astype(BF16)
                ddec = jnp.sum(dsts[j] * stbs[j].astype(F32), axis=0, keepdims=True)
                dbl = jnp.sum(dkend[j] * kend, axis=0, keepdims=True) + ddec * cc["dec"]
                dbs.append(dqe * qe - dke * ke - dkend[j] * kend + jnp.where(last_row, dbl, 0.0))
            triu = (ci >= ri).astype(F32)
            dlas = [jnp.dot(triu, db, precision=lax.Precision.HIGHEST, preferred_element_type=F32) for db in dbs]
            dzs = [dla * (1.0 / GLA_TAU) * _sigmoid(-cc["z"]) for dla, cc in zip(dlas, cm)]
            dzbs = [dz.astype(BF16) for dz in dzs]
            dlrs = [_dot_nt(dzb, wgu_ref[...]) for dzb in dzbs]
            dws = [_dot_tn(lr_ref[rows, :], dzb) for rows, dzb in zip(rows_list, dzbs)]
            for rows, dlr in zip(rows_list, dlrs):
                dlr_ref[0, rows, :] = dlr
            dw_acc[...] += functools.reduce(lambda x, y: x + y, dws)
            db_acc[0:1, :] += jnp.sum(functools.reduce(lambda x, y: x + y, dzs), axis=0, keepdims=True)
            return carry

        lax.fori_loop(0, nc // GLA_GROUP, rest, 0)
        dwgu_ref[...] = dw_acc[...]
        dbgu_ref[...] = db_acc[...]
        comm_after()

    pair = pl.BlockSpec((s_len, LANE), lambda g: (0, g))
    return pl.pallas_call(
        body, name="gla_bwd_comm" if comm else "gla_bwd", grid=(2,),
        out_shape=[jax.ShapeDtypeStruct((s_len, GU_COLS), BF16), jax.ShapeDtypeStruct((s_len, GU_COLS), BF16),
                   jax.ShapeDtypeStruct((s_len, GLA_HEADS * GLA_DV), BF16),
                   jax.ShapeDtypeStruct((2, s_len, LANE), F32),
                   jax.ShapeDtypeStruct((LANE, GU_COLS), F32), jax.ShapeDtypeStruct((8, GU_COLS), F32)] + c_shapes,
        in_specs=[pl.BlockSpec((s_len, LANE), lambda g: (0, COL_QA // LANE + g)),
                  pl.BlockSpec((s_len, LANE), lambda g: (0, COL_KA // LANE + g)),
                  pl.BlockSpec((s_len, 2 * GLA_DV), lambda g: (0, (COL_VA - NP_F32) // (2 * GLA_DV) + g)),
                  pl.BlockSpec((s_len, LANE), lambda g: (0, (COL_LR - NP_F32) // LANE)),
                  pl.BlockSpec((None, LANE, LANE), lambda g: (layer, 0, g)),
                  pl.BlockSpec((None, 1, LANE), lambda g: (layer, 0, g)),
                  pl.BlockSpec((1, nc, 2 * GLA_DV, LANE), lambda g: (g, 0, 0, 0)),
                  pl.BlockSpec((s_len, 2 * GLA_DV), lambda g: (0, g))] + [ANY] * n_cin,
        out_specs=[pair, pair, pl.BlockSpec((s_len, 2 * GLA_DV), lambda g: (0, g)),
                   pl.BlockSpec((1, s_len, LANE), lambda g: (g, 0, 0)),
                   pl.BlockSpec((LANE, LANE), lambda g: (0, g)), pl.BlockSpec((8, LANE), lambda g: (0, g))]
        + [ANY] * len(c_shapes),
        scratch_shapes=[pltpu.VMEM((nc, 2 * GLA_DV, LANE), F32), pltpu.VMEM((nc, 8, LANE), F32),
                        pltpu.VMEM((LANE, LANE), F32), pltpu.VMEM((8, LANE), F32)] + c_scratch,
        compiler_params=_params(("arbitrary",), 56),
    )(pf, pf, pb, pb, wgu, bgu.reshape(bgu.shape[0], 1, GU_COLS), states, do, *(comm[1] if comm else []))


def _dil_bwd(pf, pb, do, o_b, lse, comm=None):
    s_len = pf.shape[0]
    nblk = s_len // DIL_BLOCK
    prep_rows = 256
    scale = DIL_HD ** -0.5
    nc = len(comm[1]) if comm else 0

    def body(*refs):
        ((q_ref, kf, v_ref, do_ref, o_ref, lse_ref), (dq_ref, dk_ref, dv_ref),
         (qf, vf, dof, dl, dqa, dka, dva, bias), cin, cout, csem) = _split_refs(refs, 6, 3, 8, comm)
        comm_before, comm_after = _comm_hooks(comm, cin, cout, csem)
        comm_before()
        _dil_fill_bias(bias)

        def prep(t, carry):
            rows = pl.ds(pl.multiple_of(t * prep_rows, prep_rows), prep_rows)
            qf[rows, :] = q_ref[rows, :] * scale
            vf[rows, :] = v_ref[rows, :].astype(F32)
            dov = do_ref[rows, :].astype(F32)
            dof[rows, :] = dov
            dl[rows, :] = jnp.broadcast_to(jnp.sum(dov * o_ref[rows, :], axis=-1, keepdims=True), (prep_rows, DIL_HD))
            zero = jnp.zeros((prep_rows, DIL_HD), F32)
            dqa[rows, :] = zero
            dka[rows, :] = zero
            dva[rows, :] = zero
            return carry

        lax.fori_loop(0, s_len // prep_rows, prep, 0)

        for d in DIL_DILATIONS:
            if nblk // d == 2:
                units = DIL_GROUP // 2

                def whole(i, carry, d=d, units=units):
                    rows = [_strided(i + u * (d // units), 2 * DIL_BLOCK, d) for u in range(units)]
                    ld = [(qf[rw, :].astype(BF16), kf[rw, :].astype(BF16), vf[rw, :].astype(BF16),
                           dof[rw, :].astype(BF16)) for rw in rows]
                    both = bias[...].reshape(2 * DIL_BLOCK, 2 * DIL_BLOCK)
                    s = [_dot_nt(qb, kk) + both for qb, kk, _, _ in ld]
                    dp = [_dot_nt(dob, vv) for _, _, vv, dob in ld]
                    p = [jnp.exp(sv - lse_ref[rw, :][:, 0:1]) for sv, rw in zip(s, rows)]
                    ds = [(pv * (dpv - dl[rw, :][:, 0:1])).astype(BF16) for pv, dpv, rw in zip(p, dp, rows)]
                    pb = [pv.astype(BF16) for pv in p]
                    gq = [_dot(dsv, kk) for dsv, (_, kk, _, _) in zip(ds, ld)]
                    gk = [_dot_tn(dsv, qb) for dsv, (qb, _, _, _) in zip(ds, ld)]
                    gv = [_dot_tn(pv, dob) for pv, (_, _, _, dob) in zip(pb, ld)]
                    for rw, a, b, c in zip(rows, gq, gk, gv):
                        dqa[rw, :] += a
                        dka[rw, :] += b
                        dva[rw, :] += c
                    return carry

                lax.fori_loop(0, d // units, whole, 0)
                continue

            def pair(i, carry, d=d):
                idx = [_dil_pair_block(i, half, d, nblk) for half in range(DIL_GROUP)]
                rows = [(_strided(qs, DIL_BLOCK, d), _strided(ks, 2 * DIL_BLOCK, d)) for qs, ks, _ in idx]
                ld = [(qf[qr, :].astype(BF16), kf[kr, :].astype(BF16), vf[kr, :].astype(BF16),
                       dof[qr, :].astype(BF16)) for qr, kr in rows]
                s = [_dot_nt(qb, kk) + bias[sel] for (qb, kk, _, _), (_, _, sel) in zip(ld, idx)]
                dp = [_dot_nt(dob, vv) for _, _, vv, dob in ld]
                p = [jnp.exp(sv - lse_ref[qr, :][:, 0:1]) for sv, (qr, _) in zip(s, rows)]
                ds = [(pv * (dpv - dl[qr, :][:, 0:1])).astype(BF16) for pv, dpv, (qr, _) in zip(p, dp, rows)]
                pb = [pv.astype(BF16) for pv in p]
                gq = [_dot(dsv, kk) for dsv, (_, kk, _, _) in zip(ds, ld)]
                gk = [_dot_tn(dsv, qb) for dsv, (qb, _, _, _) in zip(ds, ld)]
                gv = [_dot_tn(pv, dob) for pv, (_, _, _, dob) in zip(pb, ld)]
                for (qr, kr), a, b, c in zip(rows, gq, gk, gv):
                    dqa[qr, :] += a
                    dka[kr, :] += b
                    dva[kr, :] += c
                return carry

            lax.fori_loop(0, nblk // DIL_GROUP, pair, 0)

        def fin(t, carry):
            rows = pl.ds(pl.multiple_of(t * prep_rows, prep_rows), prep_rows)
            dq_ref[rows, :] = (dqa[rows, :] * scale).astype(BF16)
            dk_ref[rows, :] = dka[rows, :].astype(BF16)
            dv_ref[rows, :] = dva[rows, :].astype(BF16)
            return carry

        lax.fori_loop(0, s_len // prep_rows, fin, 0)
        comm_after()

    head = lambda base: pl.BlockSpec((s_len, DIL_HD), lambda h: (0, base // DIL_HD + h))
    out = pl.BlockSpec((s_len, DIL_HD), lambda h: (0, h))
    shp = jax.ShapeDtypeStruct((s_len, DIL_HEADS * DIL_HD), BF16)
    return pl.pallas_call(
        body, name="dil_bwd_comm" if comm else "dil_bwd", grid=(DIL_HEADS,),
        out_shape=[shp, shp, shp] + (_comm_out_shapes(*comm) if comm else []),
        in_specs=[head(COL_QB), head(COL_KB), head(COL_VB - NP_F32),
                  pl.BlockSpec((s_len, DIL_HD), lambda h: (0, DIL_HEADS + h)), out, out] + [ANY] * nc,
        out_specs=[out, out, out] + [ANY] * len(_comm_plumbing(comm)[1]),
        scratch_shapes=[pltpu.VMEM((s_len, DIL_HD), F32) for _ in range(7)]
        + [pltpu.VMEM((2, DIL_BLOCK, 2 * DIL_BLOCK), F32)] + (_comm_scratch(nc) if comm else []),
        compiler_params=_params(("arbitrary",), 56),
    )(pf, pf, pb, do, o_b, lse, *(comm[1] if comm else []))


_PIECES = ((COL_Z, 1024), (COL_QA, 256), (COL_KA, 256), (COL_QB, 512), (COL_KB, 512), (COL_VA, 512), (COL_VB, 512),
           (COL_LR, 128))


def _unrope_piece(p_ref, col, cos, sin_signed):
    if col not in (COL_QB, COL_KB):
        return p_ref[...]
    blocks = []
    for lo in range(0, p_ref.shape[1], DIL_HD):
        g = p_ref[:, lo:lo + DIL_HD].astype(F32)
        blocks.append((g * cos - pltpu.roll(g, DIL_HD // 2, 1) * sin_signed).astype(BF16))
    return jnp.concatenate(blocks, axis=1)


def _in_bwd(pieces, w_new, x, dxo, g_pre, scale, cos, sin_signed, comm=None, ts=256):
    s_len = x.shape[0]
    nc = len(comm[1]) if comm else 0
    nco = len(_comm_out_shapes(*comm)) if comm else 0
    npc = len(_PIECES)

    def body(*refs):
        ins, (dx_ref, sums_ref), _, cin, cout, csem = _split_refs(refs, npc + 7, 2, 0, comm)
        p_refs = ins[:npc]
        w_ref, x_ref, dxo_ref, g_ref, sc_ref, cos_ref, sin_ref = ins[npc:]
        comm_before, comm_after = _comm_hooks(comm, cin, cout, csem, steps=s_len // ts)
        comm_before()

        @pl.when(pl.program_id(0) == 0)
        def _():
            sums_ref[...] = jnp.zeros_like(sums_ref)

        dh = jnp.zeros((ts, D_MODEL), F32)
        for p_ref, (col, width) in zip(p_refs, _PIECES):
            dh += _dot_nt(_unrope_piece(p_ref, col, cos_ref[...], sin_ref[...]), w_ref[:, col:col + width])
        xv = x_ref[...]
        rstd = lax.rsqrt(jnp.mean(xv * xv, axis=-1, keepdims=True) + EPS)
        xn = xv * rstd
        sums_ref[0:1, :] += jnp.sum(dh, axis=0, keepdims=True)
        sums_ref[1:2, :] += jnp.sum(dh * (xn * g_ref[...]), axis=0, keepdims=True)
        dr = dh * (1.0 + sc_ref[...])
        sums_ref[2:3, :] += jnp.sum(dr * xn, axis=0, keepdims=True)
        dxn = dr * g_ref[...]
        dx_ref[...] = dxo_ref[...] + rstd * (dxn - xn * jnp.mean(dxn * xn, axis=-1, keepdims=True))
        comm_after()

    (g_pre, g_spec), (scale, sc_spec) = _rowvec(g_pre), _rowvec(scale)
    tile = pl.BlockSpec((ts, D_MODEL), lambda i: (i, 0))
    return pl.pallas_call(
        body, name="in_bwd_comm" if comm else "in_bwd", grid=(s_len // ts,),
        out_shape=[jax.ShapeDtypeStruct((s_len, D_MODEL), F32), jax.ShapeDtypeStruct((8, D_MODEL), F32)]
        + (_comm_out_shapes(*comm) if comm else []),
        in_specs=[pl.BlockSpec((ts, width), lambda i: (i, 0)) for _, width in _PIECES]
        + [pl.BlockSpec((D_MODEL, NP), lambda i: (0, 0)), tile, tile, g_spec, sc_spec,
           pl.BlockSpec((ts, DIL_HD), lambda i: (i, 0)), pl.BlockSpec((ts, DIL_HD), lambda i: (i, 0))] + [ANY] * nc,
        out_specs=[tile, pl.BlockSpec((8, D_MODEL), lambda i: (0, 0))] + [ANY] * nco,
        scratch_shapes=_comm_scratch(nc) if comm else [],
        compiler_params=_params(("arbitrary",), 56),
    )(*pieces, w_new, x, dxo, g_pre, scale, cos, sin_signed, *(comm[1] if comm else []))


def _w_in_to_kernel(gathered, comm=None, tr=128):
    n_cin, c_shapes, c_scratch = _comm_plumbing(comm)
    n_parts = len(gathered)
    first = [sum(g.shape[1] for g in gathered[:p]) // tr for p in range(n_parts + 1)]

    def body(*refs):
        g_refs, (o_ref,), _, cin, cout, csem = _split_refs(refs, n_parts, 1, 0, comm)
        comm_before, comm_after = _comm_hooks(comm, cin, cout, csem, steps=D_MODEL // tr)
        comm_before()
        for p, g_ref in enumerate(g_refs):
            @pl.when((pl.program_id(0) >= first[p]) & (pl.program_id(0) < first[p + 1]))
            def _(g_ref=g_ref):
                cols = jnp.concatenate([g_ref[k].astype(F32) for k in range(N_DEV)], axis=1)
                pad = jnp.zeros((tr, LANE - GLA_LOWRANK), F32)
                o_ref[...] = jnp.concatenate(
                    [cols[:, 1024:1536], cols[:, 3088:3600], cols[:, 0:512], cols[:, 1552:2576], cols[:, 512:1024],
                     cols[:, 2576:3088], cols[:, 1536:1552], pad], axis=1).astype(BF16)
        comm_after()

    part = lambda p: pl.BlockSpec((N_DEV, tr, W_IN_SHARD),
                                  lambda i: (0, jnp.clip(i - first[p], 0, first[p + 1] - first[p] - 1), 0))
    return pl.pallas_call(
        body, name="w_in_to_kernel_comm" if comm else "w_in_to_kernel", grid=(D_MODEL // tr,),
        out_shape=[jax.ShapeDtypeStruct((D_MODEL, NP), BF16)] + c_shapes,
        in_specs=[part(p) for p in range(n_parts)] + [ANY] * n_cin,
        out_specs=[pl.BlockSpec((tr, NP), lambda i: (i, 0))] + [ANY] * len(c_shapes),
        scratch_shapes=c_scratch,
        compiler_params=_params(("arbitrary",)),
    )(*gathered, *(comm[1] if comm else []))


def _grad_w_in(h, pieces, cos, sin_signed, ts=1024, tr=128):
    s_len = h.shape[0]
    steps = s_len // ts

    def body(*refs):
        h_ref, p_refs = refs[0], refs[1:1 + len(_PIECES)]
        cos_ref, sin_ref, o_ref, acc = refs[1 + len(_PIECES):]

        @pl.when(pl.program_id(0) == 0)
        def _():
            acc[...] = jnp.zeros_like(acc)

        hv = h_ref[...]
        for p_ref, (col, width) in zip(p_refs, _PIECES):
            acc[:, col:col + width] += _dot_tn(hv, _unrope_piece(p_ref, col, cos_ref[...], sin_ref[...]))

        @pl.when(pl.program_id(0) == steps - 1)
        def _():
            def rows_out(t, carry):
                rows = pl.ds(pl.multiple_of(t * tr, tr), tr)
                g = acc[rows, :]
                cols = jnp.concatenate(
                    [g[:, COL_QA:COL_QB], g[:, COL_VA:COL_VB], g[:, 0:512], g[:, COL_LR:COL_LR + GLA_LOWRANK],
                     g[:, COL_QB:COL_VA], g[:, COL_VB:COL_LR], g[:, 512:1024]], axis=1)
                for k in range(N_DEV):
                    o_ref[k, rows, :] = cols[:, W_IN_SHARD * k:W_IN_SHARD * (k + 1)].astype(BF16)
                return carry

            lax.fori_loop(0, D_MODEL // tr, rows_out, 0)

    return pl.pallas_call(
        body, name="grad_w_in", grid=(steps,),
        out_shape=jax.ShapeDtypeStruct((N_DEV, D_MODEL, W_IN_SHARD), BF16),
        in_specs=[pl.BlockSpec((ts, D_MODEL), lambda i: (i, 0))]
        + [pl.BlockSpec((ts, width), lambda i: (i, 0)) for _, width in _PIECES]
        + [pl.BlockSpec((ts, DIL_HD), lambda i: (i, 0)), pl.BlockSpec((ts, DIL_HD), lambda i: (i, 0))],
        out_specs=pl.BlockSpec((N_DEV, D_MODEL, W_IN_SHARD), lambda i: (0, 0, 0)),
        scratch_shapes=[pltpu.VMEM((D_MODEL, NP), F32)],
        compiler_params=_params(("arbitrary",), 56),
    )(h, *pieces, cos, sin_signed)


def _adam_math(w, g, m, v):
    m = ADAM_B1 * m + (1.0 - ADAM_B1) * g
    v = ADAM_B2 * v + (1.0 - ADAM_B2) * (g * g)
    m_hat = m / (1.0 - ADAM_B1 ** ADAM_STEP)
    v_hat = v / (1.0 - ADAM_B2 ** ADAM_STEP)
    delta = -ADAM_LR * (m_hat / (jnp.sqrt(v_hat) + ADAM_EPS) + ADAM_WD * w)
    return delta, m, v


def _adamw(w, parts, m, v, name, tr):
    r, cdim = w.shape
    n_parts = parts.shape[0]

    def body(w_ref, p_ref, m_ref, v_ref, g_ref, d_ref, nm_ref, nv_ref):
        g = p_ref[0].astype(F32)
        for k in range(1, n_parts):
            g = g + p_ref[k].astype(F32)
        g_ref[...] = g
        d_ref[...], nm_ref[...], nv_ref[...] = _adam_math(w_ref[...], g, m_ref[...], v_ref[...])

    tile = pl.BlockSpec((tr, cdim), lambda i: (i, 0))
    shp = jax.ShapeDtypeStruct((r, cdim), F32)
    return pl.pallas_call(
        body, name=name, grid=(r // tr,), out_shape=(shp, shp, shp, shp),
        in_specs=[tile, pl.BlockSpec((n_parts, tr, cdim), lambda i: (0, i, 0)), tile, tile],
        out_specs=(tile, tile, tile, tile),
        compiler_params=_params(("arbitrary",), 40),
    )(w, parts, m, v)


def _adamw_layers(w, parts, m, v, name, tr):
    n_layers, r, cdim = w.shape

    def body(*refs):
        w_ref, p_refs, (m_ref, v_ref) = refs[0], refs[1:1 + n_layers], refs[1 + n_layers:3 + n_layers]
        g_ref, d_ref, nm_ref, nv_ref = refs[3 + n_layers:]
        for l, p_ref in enumerate(p_refs):
            @pl.when(pl.program_id(0) == l)
            def _(p_ref=p_ref):
                g = p_ref[0].astype(F32)
                for k in range(1, p_ref.shape[0]):
                    g = g + p_ref[k].astype(F32)
                g_ref[0] = g
                d_ref[0], nm_ref[0], nv_ref[0] = _adam_math(w_ref[0], g, m_ref[0], v_ref[0])

    tile = pl.BlockSpec((1, tr, cdim), lambda l, i: (l, i, 0))
    part = lambda own: pl.BlockSpec((parts[own].shape[0], tr, cdim), lambda l, i: (0, jnp.where(l == own, i, 0), 0))
    shp = jax.ShapeDtypeStruct(w.shape, F32)
    return pl.pallas_call(
        body, name=name, grid=(n_layers, r // tr), out_shape=(shp, shp, shp, shp),
        in_specs=[tile] + [part(l) for l in range(n_layers)] + [tile, tile],
        out_specs=(tile, tile, tile, tile),
        compiler_params=_params(("arbitrary", "arbitrary"), 40),
    )(w, *parts, m, v)


def _row(vec, width):
    vec = vec.reshape(1, -1)
    return jnp.pad(vec, ((0, 0), (0, width - vec.shape[1])))


def kernel(x, c, w_ada, b_ada, g_pre, w_in, w_gate_up, b_gate_up, g_gla, g_dil, w_out, g_post, loss_target, m_w_ada, m_b_ada, m_g_pre, m_w_in, m_w_gate_up, m_b_gate_up, m_g_gla, m_g_dil, m_w_out, m_g_post, v_w_ada, v_b_ada, v_g_pre, v_w_in, v_w_gate_up, v_b_gate_up, v_g_gla, v_g_dil, v_w_out, v_g_post):
    px, py, pc = _my_position()
    me = _linear(px, py, pc)
    xs = x[0]
    target = loss_target[0]
    s_len = xs.shape[0]
    assert s_len % (DIL_BLOCK * max(DIL_DILATIONS) * 2) == 0 and xs.shape[1] == D_MODEL

    w_in_b, w_out_b = w_in.astype(BF16), w_out.astype(BF16)
    c_rows, wgu_all, w_in_all = _comm_call(
        "gather", [jnp.pad(c, ((0, 7), (0, 0))), w_gate_up.reshape(DEPTH * GLA_LOWRANK, GU_SHARD), w_in_b[0]],
        "gather_first")
    c_all = c_rows.reshape(N_DEV, 8, D_MODEL)[:, 0]
    mod_part = _mod_fwd(c_all, w_ada)
    w_new, mod_all = _w_in_to_kernel([w_in_all.reshape(N_DEV, D_MODEL, W_IN_SHARD)],
                                     comm=("gather", [mod_part.reshape(DEPTH * N_DEV, ADA_SHARD)]))
    mod_all = mod_all.reshape(N_DEV, DEPTH, N_DEV, ADA_SHARD)
    mod_mine = lax.dynamic_index_in_dim(mod_all, me, axis=2, keepdims=False)
    mod = jnp.transpose(mod_mine, (1, 0, 2)).reshape(DEPTH, 3 * D_MODEL) + b_ada
    wgu_full = jnp.transpose(wgu_all.reshape(N_DEV, DEPTH, GLA_LOWRANK, GU_SHARD), (1, 2, 0, 3)).reshape(
        DEPTH, GLA_LOWRANK, GU_COLS)
    wgu_pad = jnp.pad(wgu_full, ((0, 0), (0, LANE - GLA_LOWRANK), (0, 0))).astype(BF16)

    cos, sin_signed = _rope_tables(s_len)
    g_heads = jnp.concatenate([g_gla, g_dil], axis=1)

    saved = []
    xl = xs
    for l in range(DEPTH):
        shift, scale, gate = ((mod, l, k) for k in range(3))
        if l > 0:
            w_new = _w_in_to_kernel([half.reshape(N_DEV, D_MODEL // 2, W_IN_SHARD) for half in w_in_halves])[0]
        if l + 1 < DEPTH:
            own = [] if l > 0 else [w_out_b[0]]
            pf, pb, h, *arrived = _prenorm_proj(xl, (g_pre, l, 0), scale, shift, w_new, cos, sin_signed,
                                                comm=("gather", own + [w_in_b[l + 1, :D_MODEL // 2]]))
            w_out_l = arrived[0] if l == 0 else w_out_next
            top = arrived[-1]
        else:
            pf, pb, h = _prenorm_proj(xl, (g_pre, l, 0), scale, shift, w_new, cos, sin_signed, ts=512)
            w_out_l = w_out_next
        o_a, states = _gla_fwd(pf, pb, wgu_pad, b_gate_up, l)
        if l + 1 < DEPTH:
            o_b, lse, bottom, w_out_next = _dil_fwd(pf, pb, comm=("gather", [w_in_b[l + 1, D_MODEL // 2:],
                                                                             w_out_b[l + 1]]))
            w_in_halves = (top, bottom)
        else:
            o_b, lse = _dil_fwd(pf, pb)
        if l + 1 < DEPTH:
            x_next, u = _post_fwd(o_a, o_b, pf, (g_heads, l, 0), w_out_l, xl, gate, (g_post, l, 0))
        else:
            dx, u, loss_part = _post_fwd(o_a, o_b, pf, (g_heads, l, 0), w_out_l, xl, gate, (g_post, l, 0),
                                         target=target)
        saved.append((xl, scale, gate, w_new, w_out_l, pf, pb, h, o_a, states, o_b, lse, u))
        xl = x_next

    small_rows = []
    gin_slots, gin_parts, gout_parts = None, [None] * DEPTH, [None] * DEPTH
    for l in reversed(range(DEPTH)):
        x_in, scale, gate, w_new, w_out_l, pf, pb, h, o_a, states, o_b, lse, u = saved[l]
        do, dz, sums_post, gout_slots = _post_bwd(dx, u, gate, (g_post, l, 0), w_out_l, o_a, o_b, pf, (g_heads, l, 0))
        dq_a, dk_a, dv_a, dlr2, dwgu, dbgu, arrived = _gla_bwd(pf, pb, wgu_pad, b_gate_up, l, states, do,
                                                               comm=("exchange", [gout_slots]))
        gout_parts[l] = arrived.reshape(N_DEV, OUT_SHARD, D_MODEL)
        if gin_slots is not None:
            dq_b, dk_b, dv_b, arrived, _, _ = _dil_bwd(pf, pb, do, o_b, lse, comm=("pairsum_exchange", [gin_slots]))
            gin_parts[l + 1] = arrived.reshape(N_DEV // 2, D_MODEL, W_IN_SHARD)
        else:
            dq_b, dk_b, dv_b = _dil_bwd(pf, pb, do, o_b, lse)
        dlr = (dlr2[0] + dlr2[1]).astype(BF16)
        pieces = (dz, dq_a, dk_a, dq_b, dk_b, dv_a, dv_b, dlr)
        gin_slots = _grad_w_in(h, pieces, cos, sin_signed).reshape(N_DEV * D_MODEL, W_IN_SHARD)
        if l == 0:
            dx, sums_in, arrived, _, _ = _in_bwd(pieces, w_new, x_in, dx, (g_pre, l, 0), scale, cos, sin_signed,
                                                 comm=("pairsum_exchange", [gin_slots]))
            gin_parts[0] = arrived.reshape(N_DEV // 2, D_MODEL, W_IN_SHARD)
        else:
            dx, sums_in = _in_bwd(pieces, w_new, x_in, dx, (g_pre, l, 0), scale, cos, sin_signed, ts=512)
        dmod = jnp.concatenate([sums_in[0], sums_in[1], sums_post[0]])
        vecs = jnp.concatenate([sums_in[2], sums_post[1], sums_post[2], dbgu[0]])
        small_rows[0:0] = [_row(dmod, 4096), _row(vecs, 4096), _row(dwgu[:GLA_LOWRANK], 4096)]
    grad_x = dx[None]

    flat = lambda a, rows: a.reshape(rows, a.shape[-1])
    r_ada = DEPTH * D_MODEL
    g_w_in, d_w_in, nm_w_in, nv_w_in = _adamw_layers(w_in, gin_parts, m_w_in, v_w_in, "adamw_w_in", 256)
    g_w_out, d_w_out, nm_w_out, nv_w_out = _adamw_layers(w_out, gout_parts, m_w_out, v_w_out, "adamw_w_out", 128)

    small_rows += [_row(loss_part[0, 0:1], 4096), jnp.zeros((1, 4096), F32)]
    small = _all_gather(jnp.concatenate(small_rows, axis=0), "gather_small").reshape(N_DEV, 8, 4096)
    dmod_all = jnp.stack([small[:, 0, :3 * D_MODEL], small[:, 3, :3 * D_MODEL]])
    dmod_cols = lax.dynamic_slice_in_dim(dmod_all, me * ADA_SHARD, ADA_SHARD, axis=2)
    gwa = _w_ada_grad(c_all, dmod_cols).reshape(1, r_ada, ADA_SHARD)
    g_w_ada, d_w_ada, nm_w_ada, nv_w_ada = (
        t.reshape(w_ada.shape) for t in _adamw(flat(w_ada, r_ada), gwa, flat(m_w_ada, r_ada), flat(v_w_ada, r_ada),
                                               "adamw_w_ada", 256))

    where = ((0, 0), (1, 0), (1, 1024), (1, 2048), (1, 2560), (1, 3072))
    replicated = [(b_ada, m_b_ada, v_b_ada), (g_pre, m_g_pre, v_g_pre), (g_post, m_g_post, v_g_post),
                  (g_gla, m_g_gla, v_g_gla), (g_dil, m_g_dil, v_g_dil), (b_gate_up, m_b_gate_up, v_b_gate_up)]
    updated, loss = _adamw_replicated(small, replicated, where, loss_at=(6, 0))
    ((g_b_ada, d_b_ada, nm_b_ada, nv_b_ada), (g_g_pre, d_g_pre, nm_g_pre, nv_g_pre),
     (g_g_post, d_g_post, nm_g_post, nv_g_post), (g_g_gla, d_g_gla, nm_g_gla, nv_g_gla),
     (g_g_dil, d_g_dil, nm_g_dil, nv_g_dil), (g_b_gu, d_b_gu, nm_b_gu, nv_b_gu)) = updated
    gu_parts = jnp.stack([small[:, 2], small[:, 5]], axis=1).reshape(N_DEV, DEPTH, GLA_LOWRANK, GU_COLS)
    gu_parts = lax.dynamic_slice_in_dim(gu_parts, me * GU_SHARD, GU_SHARD, axis=3).reshape(
        N_DEV, DEPTH * GLA_LOWRANK, GU_SHARD)
    r_gu = DEPTH * GLA_LOWRANK
    g_w_gu, d_w_gu, nm_w_gu, nv_w_gu = (
        t.reshape(w_gate_up.shape) for t in _adamw(flat(w_gate_up, r_gu), gu_parts, flat(m_w_gate_up, r_gu),
                                                   flat(v_w_gate_up, r_gu), "adamw_w_gate_up", r_gu))
    return (loss, grad_x,
            g_w_ada, g_b_ada, g_g_pre, g_w_in, g_w_gu, g_b_gu, g_g_gla, g_g_dil, g_w_out, g_g_post,
            d_w_ada, d_b_ada, d_g_pre, d_w_in, d_w_gu, d_b_gu, d_g_gla, d_g_dil, d_w_out, d_g_post,
            nm_w_ada, nm_b_ada, nm_g_pre, nm_w_in, nm_w_gu, nm_b_gu, nm_g_gla, nm_g_dil, nm_w_out, nm_g_post,
            nv_w_ada, nv_b_ada, nv_g_pre, nv_w_in, nv_w_gu, nv_b_gu, nv_g_gla, nv_g_dil, nv_w_out, nv_g_post)


def _adamw_replicated(small, params, where, loss_at):
    n_parts = small.shape[0]

    def body(*refs):
        s_ref, p_refs, o_refs = refs[0], refs[1:1 + 3 * len(params)], refs[1 + 3 * len(params):]
        total = s_ref[0]
        for k in range(1, n_parts):
            total = total + s_ref[k]
        for i, (row, col) in enumerate(where):
            w_ref, m_ref, v_ref = p_refs[3 * i:3 * i + 3]
            n = w_ref.shape[1]
            g = jnp.concatenate([total[row + 3 * l:row + 3 * l + 1, col:col + n] for l in range(DEPTH)], axis=0)
            o_refs[4 * i][...] = g
            o_refs[4 * i + 1][...], o_refs[4 * i + 2][...], o_refs[4 * i + 3][...] = _adam_math(
                w_ref[...], g, m_ref[...], v_ref[...])
        o_refs[-1][...] = jnp.broadcast_to(total[loss_at[0]:loss_at[0] + 1, loss_at[1]:loss_at[1] + 1], (8, LANE))

    flat = [a for p in params for a in p]
    shapes = [jax.ShapeDtypeStruct(p[0].shape, F32) for p in params for _ in range(4)]
    outs = pl.pallas_call(body, name="adamw_replicated",
                          out_shape=shapes + [jax.ShapeDtypeStruct((8, LANE), F32)])(small, *flat)
    return [tuple(outs[4 * i:4 * i + 4]) for i in range(len(params))], outs[-1][0, 0]
```
